```python
import math
import jax, jax.numpy as jnp
from jax import lax
import numpy as np

D_MODEL = 1024
BATCH = 16
SEQ = 2048
DEPTH = 1

MLA_HEADS = 4
Q_LORA_RANK = 256
KV_LORA_RANK = 256
QK_NOPE_DIM = 128
QK_ROPE_DIM = 64
QK_HEAD_DIM = QK_NOPE_DIM + QK_ROPE_DIM
V_HEAD_DIM = 128
MLA_WIDTH = MLA_HEADS * V_HEAD_DIM
ROPE_THETA = 10000.0
Q_BLOCK = 128
GDN_HEADS = 4
GDN_HEAD_DIM = 128
GDN_WIDTH = GDN_HEADS * GDN_HEAD_DIM
CONV_WIDTH = 4
CHUNK = 64
MIX_WIDTH = MLA_WIDTH + GDN_WIDTH
D_FF = 4 * D_MODEL
EPS = 1e-6
IN_SPLITS = (Q_LORA_RANK, KV_LORA_RANK, QK_ROPE_DIM,
             GDN_WIDTH, GDN_WIDTH, GDN_WIDTH, GDN_WIDTH, GDN_HEADS, GDN_HEADS)
D_IN = sum(IN_SPLITS)

kernel_name = "hymba_mla_gdn_sqrelu_layer"


def rms_norm(x, w):
    xf = x.astype(jnp.float32)
    y = xf * lax.rsqrt(jnp.mean(xf * xf, axis=-1, keepdims=True) + EPS)
    return (y * w.astype(jnp.float32)).astype(x.dtype)


def l2_norm(x):
    return x * lax.rsqrt(jnp.sum(x * x, axis=-1, keepdims=True) + EPS)


def split_cols(t, sizes):
    offs = np.cumsum(sizes)[:-1].tolist()
    return jnp.split(t, offs, axis=-1)


def rope_angles(positions):
    half = QK_ROPE_DIM // 2
    inv_freq = ROPE_THETA ** (-jnp.arange(half, dtype=jnp.float32) / half)
    ang = positions.astype(jnp.float32)[..., None] * inv_freq
    return jnp.cos(ang)[:, :, None, :], jnp.sin(ang)[:, :, None, :]


def apply_rope(t, cos, sin):
    tf = t.astype(jnp.float32)
    t1, t2 = jnp.split(tf, 2, axis=-1)
    return jnp.concatenate([t1 * cos - t2 * sin, t2 * cos + t1 * sin], axis=-1).astype(t.dtype)


def causal_attention(q, k, v):
    B, S, H, _ = q.shape
    n_blocks = S // Q_BLOCK
    scale = QK_HEAD_DIM ** -0.5
    qb = jnp.moveaxis(q.reshape(B, n_blocks, Q_BLOCK, H, QK_HEAD_DIM), 1, 0)
    key_pos = jnp.arange(S)

    def one_block(args):
        q_blk, blk = args
        s = jnp.einsum('bqhd,bkhd->bhqk', q_blk, k,
                       preferred_element_type=jnp.float32) * scale
        q_pos = blk * Q_BLOCK + jnp.arange(Q_BLOCK)
        s = jnp.where(key_pos[None, :] <= q_pos[:, None], s, -jnp.inf)
        p = jax.nn.softmax(s, axis=-1).astype(v.dtype)
        return jnp.einsum('bhqk,bkhd->bqhd', p, v)

    o = lax.map(one_block, (qb, jnp.arange(n_blocks)))
    return jnp.moveaxis(o, 0, 1).reshape(B, S, H, V_HEAD_DIM)


def mla_group(q_lat, kv_lat, k_pe, cos, sin, q_lat_norm_w, w_uq, kv_lat_norm_w, w_ukv,
              q_norm_w, k_norm_w, mla_out_norm_w):
    B, S, _ = q_lat.shape
    q = (rms_norm(q_lat, q_lat_norm_w) @ w_uq).reshape(B, S, MLA_HEADS, QK_HEAD_DIM)
    kv = (rms_norm(kv_lat, kv_lat_norm_w) @ w_ukv).reshape(B, S, MLA_HEADS, QK_NOPE_DIM + V_HEAD_DIM)
    k_nope, v = jnp.split(kv, [QK_NOPE_DIM], axis=-1)
    q_nope = rms_norm(q[..., :QK_NOPE_DIM], q_norm_w[:QK_NOPE_DIM])
    q_pe = apply_rope(rms_norm(q[..., QK_NOPE_DIM:], q_norm_w[QK_NOPE_DIM:]), cos, sin)
    k_nope = rms_norm(k_nope, k_norm_w[:QK_NOPE_DIM])
    k_pe = apply_rope(rms_norm(k_pe[:, :, None, :], k_norm_w[QK_NOPE_DIM:]), cos, sin)
    q = jnp.concatenate([q_nope, q_pe], axis=-1)
    k = jnp.concatenate([k_nope, jnp.broadcast_to(k_pe, (B, S, MLA_HEADS, QK_ROPE_DIM))], axis=-1)
    o = causal_attention(q, k, v)
    o = rms_norm(o, mla_out_norm_w)
    return o.reshape(B, S, MLA_WIDTH)


def causal_conv(x, w):
    S = x.shape[1]
    xp = jnp.pad(x, ((0, 0), (CONV_WIDTH - 1, 0), (0, 0)))
    return sum(w[i] * xp[:, i:i + S] for i in range(CONV_WIDTH))


def chunk_gated_delta(q, k, v, g, beta):
    B, H, S, D = q.shape
    N = S // CHUNK
    q, k, v = [t.reshape(B, H, N, CHUNK, D) for t in (q, k, v)]
    g = g.reshape(B, H, N, CHUNK)
    beta = beta.reshape(B, H, N, CHUNK)
    G = jnp.cumsum(g, axis=-1)
    idx = jnp.arange(CHUNK)
    causal = idx[:, None] >= idx[None, :]
    strict = idx[:, None] > idx[None, :]
    decay = jnp.exp(jnp.where(causal, G[..., :, None] - G[..., None, :], -jnp.inf))
    kk = jnp.einsum('bhncd,bhnjd->bhncj', k, k)
    L = jnp.where(strict, beta[..., :, None] * kk * decay, 0.0)
    A = L + jnp.eye(CHUNK, dtype=L.dtype)
    rhs = jnp.concatenate([v * beta[..., None], k * (beta * jnp.exp(G))[..., None]], axis=-1)
    sol = lax.linalg.triangular_solve(A, rhs, left_side=True, lower=True, unit_diagonal=True)
    u, w = jnp.split(sol, 2, axis=-1)
    attn_intra = jnp.einsum('bhncd,bhnjd->bhncj', q, k) * decay
    q_dec = q * jnp.exp(G)[..., None]
    k_dec = k * jnp.exp(G[..., -1:] - G)[..., None]
    chunk_decay = jnp.exp(G[..., -1])

    def step(state, xs):
        u_c, w_c, a_c, qd_c, kd_c, cd_c = xs
        v_new = u_c - jnp.einsum('bhcd,bhde->bhce', w_c, state)
        o_c = jnp.einsum('bhcd,bhde->bhce', qd_c, state) + jnp.einsum('bhcj,bhje->bhce', a_c, v_new)
        state = state * cd_c[..., None, None] + jnp.einsum('bhcd,bhce->bhde', kd_c, v_new)
        return state, o_c

    xs = tuple(jnp.moveaxis(t, 2, 0) for t in (u, w, attn_intra, q_dec, k_dec, chunk_decay))
    state0 = jnp.zeros((B, H, D, D), jnp.float32)
    _, o = lax.scan(step, state0, xs)
    return jnp.moveaxis(o, 0, 2).reshape(B, H, S, D)


def gdn_group(q, k, v, z, a, b, conv_w, a_log, dt_bias, gdn_norm_w):
    B, S, _ = q.shape
    qkv = jax.nn.silu(causal_conv(jnp.concatenate([q, k, v], axis=-1), conv_w))
    q, k, v = [t.reshape(B, S, GDN_HEADS, GDN_HEAD_DIM).transpose(0, 2, 1, 3).astype(jnp.float32)
               for t in jnp.split(qkv, 3, axis=-1)]
    q = l2_norm(q) * (GDN_HEAD_DIM ** -0.5)
    k = l2_norm(k)
    beta = jax.nn.sigmoid(b.astype(jnp.float32)).transpose(0, 2, 1)
    g = (-jnp.exp(a_log.astype(jnp.float32))
         * jax.nn.softplus(a.astype(jnp.float32) + dt_bias.astype(jnp.float32))).transpose(0, 2, 1)
    o = chunk_gated_delta(q, k, v, g, beta).transpose(0, 2, 1, 3).astype(z.dtype)
    zh = z.reshape(B, S, GDN_HEADS, GDN_HEAD_DIM)
    o = rms_norm(o, gdn_norm_w) * jax.nn.silu(zh)
    return o.reshape(B, S, GDN_WIDTH)


def _fwd_setup_inputs(seed: int = 0) -> dict:
    key = jax.random.key(seed)
    ks = jax.random.split(key, 20)
    L = DEPTH

    def normal(k, shape, fan_in):
        return jax.random.normal(k, shape, jnp.float32) * (fan_in ** -0.5)

    def gain(k, shape):
        return 1.0 + 0.02 * jax.random.normal(k, shape, jnp.float32)

    return {
        "x": jax.random.normal(ks[0], (BATCH, SEQ, D_MODEL), jnp.float32),
        "positions": jnp.broadcast_to(jnp.arange(SEQ, dtype=jnp.int32), (BATCH, SEQ)),
        "attn_norm_w": gain(ks[1], (L, D_MODEL)),
        "w_in": normal(ks[2], (L, D_MODEL, D_IN), D_MODEL),
        "q_lat_norm_w": gain(ks[3], (L, Q_LORA_RANK)),
        "w_uq": normal(ks[4], (L, Q_LORA_RANK, MLA_HEADS * QK_HEAD_DIM), Q_LORA_RANK),
        "kv_lat_norm_w": gain(ks[5], (L, KV_LORA_RANK)),
        "w_ukv": normal(ks[6], (L, KV_LORA_RANK, MLA_HEADS * (QK_NOPE_DIM + V_HEAD_DIM)), KV_LORA_RANK),
        "q_norm_w": gain(ks[7], (L, QK_HEAD_DIM)),
        "k_norm_w": gain(ks[8], (L, QK_HEAD_DIM)),
        "mla_out_norm_w": gain(ks[9], (L, MLA_HEADS, V_HEAD_DIM)),
        "conv_w": normal(ks[10], (L, CONV_WIDTH, 3 * GDN_WIDTH), CONV_WIDTH),
        "a_log": jnp.log(jax.random.uniform(ks[11], (L, GDN_HEADS), jnp.float32, 1.0, 16.0)),
        "dt_bias": 0.1 * jax.random.normal(ks[12], (L, GDN_HEADS), jnp.float32),
        "gdn_norm_w": gain(ks[13], (L, GDN_HEAD_DIM)),
        "w_out": normal(ks[14], (L, MIX_WIDTH, D_MODEL), MIX_WIDTH),
        "mlp_norm_w": gain(ks[15], (L, D_MODEL)),
        "w_up": normal(ks[16], (L, D_MODEL, D_FF), D_MODEL),
        "w_down": normal(ks[17], (L, D_FF, D_MODEL), D_FF),
    }


def _fwd_reference(x, positions, attn_norm_w, w_in, q_lat_norm_w, w_uq, kv_lat_norm_w, w_ukv,
              q_norm_w, k_norm_w, mla_out_norm_w, conv_w, a_log, dt_bias, gdn_norm_w,
              w_out, mlp_norm_w, w_up, w_down):
    cos, sin = rope_angles(positions)
    h = x
    for l in range(DEPTH):
        xn = rms_norm(h, attn_norm_w[l])
        proj = xn @ w_in[l]
        q_lat, kv_lat, k_pe, gq, gk, gv, gz, ga, gb = split_cols(proj, IN_SPLITS)
        mla_o = mla_group(q_lat, kv_lat, k_pe, cos, sin, q_lat_norm_w[l], w_uq[l],
                          kv_lat_norm_w[l], w_ukv[l], q_norm_w[l], k_norm_w[l], mla_out_norm_w[l])
        gdn_o = gdn_group(gq, gk, gv, gz, ga, gb, conv_w[l], a_log[l], dt_bias[l], gdn_norm_w[l])
        h = h + jnp.concatenate([mla_o, gdn_o], axis=-1) @ w_out[l]
        hn = rms_norm(h, mlp_norm_w[l])
        h = h + jnp.square(jax.nn.relu(hn @ w_up[l])) @ w_down[l]
    return h


import jax as _jax
import jax.numpy as _jnp

TWIN_FORMAT = 'train_step'
FWD_PARAMS = ['x', 'positions', 'attn_norm_w', 'w_in', 'q_lat_norm_w', 'w_uq', 'kv_lat_norm_w', 'w_ukv', 'q_norm_w', 'k_norm_w', 'mla_out_norm_w', 'conv_w', 'a_log', 'dt_bias', 'gdn_norm_w', 'w_out', 'mlp_norm_w', 'w_up', 'w_down']
TWIN_WEIGHTS = ['attn_norm_w', 'w_in', 'q_lat_norm_w', 'w_uq', 'kv_lat_norm_w', 'w_ukv', 'q_norm_w', 'k_norm_w', 'mla_out_norm_w', 'conv_w', 'a_log', 'dt_bias', 'gdn_norm_w', 'w_out', 'mlp_norm_w', 'w_up', 'w_down']
TWIN_DIFF_INPUT = 'x'
TWIN_INPUTS = ['x', 'positions', 'attn_norm_w', 'w_in', 'q_lat_norm_w', 'w_uq', 'kv_lat_norm_w', 'w_ukv', 'q_norm_w', 'k_norm_w', 'mla_out_norm_w', 'conv_w', 'a_log', 'dt_bias', 'gdn_norm_w', 'w_out', 'mlp_norm_w', 'w_up', 'w_down', 'loss_target', 'm_attn_norm_w', 'm_w_in', 'm_q_lat_norm_w', 'm_w_uq', 'm_kv_lat_norm_w', 'm_w_ukv', 'm_q_norm_w', 'm_k_norm_w', 'm_mla_out_norm_w', 'm_conv_w', 'm_a_log', 'm_dt_bias', 'm_gdn_norm_w', 'm_w_out', 'm_mlp_norm_w', 'm_w_up', 'm_w_down', 'v_attn_norm_w', 'v_w_in', 'v_q_lat_norm_w', 'v_w_uq', 'v_kv_lat_norm_w', 'v_w_ukv', 'v_q_norm_w', 'v_k_norm_w', 'v_mla_out_norm_w', 'v_conv_w', 'v_a_log', 'v_dt_bias', 'v_gdn_norm_w', 'v_w_out', 'v_mlp_norm_w', 'v_w_up', 'v_w_down']
TWIN_OUTPUTS = ['loss', 'grad_x', 'grad_attn_norm_w', 'grad_w_in', 'grad_q_lat_norm_w', 'grad_w_uq', 'grad_kv_lat_norm_w', 'grad_w_ukv', 'grad_q_norm_w', 'grad_k_norm_w', 'grad_mla_out_norm_w', 'grad_conv_w', 'grad_a_log', 'grad_dt_bias', 'grad_gdn_norm_w', 'grad_w_out', 'grad_mlp_norm_w', 'grad_w_up', 'grad_w_down', 'delta_attn_norm_w', 'delta_w_in', 'delta_q_lat_norm_w', 'delta_w_uq', 'delta_kv_lat_norm_w', 'delta_w_ukv', 'delta_q_norm_w', 'delta_k_norm_w', 'delta_mla_out_norm_w', 'delta_conv_w', 'delta_a_log', 'delta_dt_bias', 'delta_gdn_norm_w', 'delta_w_out', 'delta_mlp_norm_w', 'delta_w_up', 'delta_w_down', 'new_m_attn_norm_w', 'new_m_w_in', 'new_m_q_lat_norm_w', 'new_m_w_uq', 'new_m_kv_lat_norm_w', 'new_m_w_ukv', 'new_m_q_norm_w', 'new_m_k_norm_w', 'new_m_mla_out_norm_w', 'new_m_conv_w', 'new_m_a_log', 'new_m_dt_bias', 'new_m_gdn_norm_w', 'new_m_w_out', 'new_m_mlp_norm_w', 'new_m_w_up', 'new_m_w_down', 'new_v_attn_norm_w', 'new_v_w_in', 'new_v_q_lat_norm_w', 'new_v_w_uq', 'new_v_kv_lat_norm_w', 'new_v_w_ukv', 'new_v_q_norm_w', 'new_v_k_norm_w', 'new_v_mla_out_norm_w', 'new_v_conv_w', 'new_v_a_log', 'new_v_dt_bias', 'new_v_gdn_norm_w', 'new_v_w_out', 'new_v_mlp_norm_w', 'new_v_w_up', 'new_v_w_down']
TWIN_LEAF_KINDS = {'loss': 'loss', 'grad_x': 'grad_x', 'grad_attn_norm_w': 'grad_w', 'grad_w_in': 'grad_w', 'grad_q_lat_norm_w': 'grad_w', 'grad_w_uq': 'grad_w', 'grad_kv_lat_norm_w': 'grad_w', 'grad_w_ukv': 'grad_w', 'grad_q_norm_w': 'grad_w', 'grad_k_norm_w': 'grad_w', 'grad_mla_out_norm_w': 'grad_w', 'grad_conv_w': 'grad_w', 'grad_a_log': 'grad_w', 'grad_dt_bias': 'grad_w', 'grad_gdn_norm_w': 'grad_w', 'grad_w_out': 'grad_w', 'grad_mlp_norm_w': 'grad_w', 'grad_w_up': 'grad_w', 'grad_w_down': 'grad_w', 'delta_attn_norm_w': 'delta_w', 'delta_w_in': 'delta_w', 'delta_q_lat_norm_w': 'delta_w', 'delta_w_uq': 'delta_w', 'delta_kv_lat_norm_w': 'delta_w', 'delta_w_ukv': 'delta_w', 'delta_q_norm_w': 'delta_w', 'delta_k_norm_w': 'delta_w', 'delta_mla_out_norm_w': 'delta_w', 'delta_conv_w': 'delta_w', 'delta_a_log': 'delta_w', 'delta_dt_bias': 'delta_w', 'delta_gdn_norm_w': 'delta_w', 'delta_w_out': 'delta_w', 'delta_mlp_norm_w': 'delta_w', 'delta_w_up': 'delta_w', 'delta_w_down': 'delta_w', 'new_m_attn_norm_w': 'new_m', 'new_m_w_in': 'new_m', 'new_m_q_lat_norm_w': 'new_m', 'new_m_w_uq': 'new_m', 'new_m_kv_lat_norm_w': 'new_m', 'new_m_w_ukv': 'new_m', 'new_m_q_norm_w': 'new_m', 'new_m_k_norm_w': 'new_m', 'new_m_mla_out_norm_w': 'new_m', 'new_m_conv_w': 'new_m', 'new_m_a_log': 'new_m', 'new_m_dt_bias': 'new_m', 'new_m_gdn_norm_w': 'new_m', 'new_m_w_out': 'new_m', 'new_m_mlp_norm_w': 'new_m', 'new_m_w_up': 'new_m', 'new_m_w_down': 'new_m', 'new_v_attn_norm_w': 'new_v', 'new_v_w_in': 'new_v', 'new_v_q_lat_norm_w': 'new_v', 'new_v_w_uq': 'new_v', 'new_v_kv_lat_norm_w': 'new_v', 'new_v_w_ukv': 'new_v', 'new_v_q_norm_w': 'new_v', 'new_v_k_norm_w': 'new_v', 'new_v_mla_out_norm_w': 'new_v', 'new_v_conv_w': 'new_v', 'new_v_a_log': 'new_v', 'new_v_dt_bias': 'new_v', 'new_v_gdn_norm_w': 'new_v', 'new_v_w_out': 'new_v', 'new_v_mlp_norm_w': 'new_v', 'new_v_w_up': 'new_v', 'new_v_w_down': 'new_v'}


def _forward(args):
    return _fwd_reference(*[args[k] for k in FWD_PARAMS])


def _output_shape():
    out = _jax.eval_shape(lambda: _forward(_fwd_setup_inputs(0)))
    return out.shape, out.dtype

N_MICROBATCH = 1
ADAM_LR = 0.001
ADAM_B1 = 0.9
ADAM_B2 = 0.999
ADAM_EPS = 1e-08
ADAM_WD = 0.01
ADAM_STEP = 10
PER_EXAMPLE_BATCH_AXIS = {'x': 0, 'positions': 0, 'loss_target': 0}
SHARED_INPUTS = []
_WEIGHT_DTYPES = {'attn_norm_w': _jnp.float32, 'w_in': _jnp.float32, 'q_lat_norm_w': _jnp.float32, 'w_uq': _jnp.float32, 'kv_lat_norm_w': _jnp.float32, 'w_ukv': _jnp.float32, 'q_norm_w': _jnp.float32, 'k_norm_w': _jnp.float32, 'mla_out_norm_w': _jnp.float32, 'conv_w': _jnp.float32, 'a_log': _jnp.float32, 'dt_bias': _jnp.float32, 'gdn_norm_w': _jnp.float32, 'w_out': _jnp.float32, 'mlp_norm_w': _jnp.float32, 'w_up': _jnp.float32, 'w_down': _jnp.float32}
MOMENT_SCALE = {'attn_norm_w': 5.830807e+00, 'w_in': 1.714942e+00, 'q_lat_norm_w': 2.177298e+00, 'w_uq': 1.013411e+00, 'kv_lat_norm_w': 8.449655e+00, 'w_ukv': 3.263270e+00, 'q_norm_w': 2.974498e+00, 'k_norm_w': 2.773567e+00, 'mla_out_norm_w': 3.087142e+01, 'conv_w': 8.312191e-01, 'a_log': 1.160781e+01, 'dt_bias': 1.026601e+01, 'gdn_norm_w': 3.605992e+01, 'w_out': 3.491443e+00, 'mlp_norm_w': 9.504480e+01, 'w_up': 1.593017e+00, 'w_down': 8.119717e+00}


def _to_microbatches(a, axis):
    t = _jnp.moveaxis(a, axis, 0)
    t = t.reshape((N_MICROBATCH, t.shape[0] // N_MICROBATCH) + t.shape[1:])
    return _jnp.moveaxis(t, 1, axis + 1)


def setup_inputs(seed: int = 0) -> dict:
    inp = _fwd_setup_inputs(seed)
    key = _jax.random.fold_in(_jax.random.key(seed), 7919)
    shape, _ = _output_shape()
    out = dict(inp)
    out["loss_target"] = _jax.random.normal(_jax.random.fold_in(key, 0), shape, _jnp.float32)
    for i, name in enumerate(TWIN_WEIGHTS):
        w = inp[name].astype(_jnp.float32)
        if MOMENT_SCALE is None:
            s = _jnp.sqrt(_jnp.mean(_jnp.square(w)) + 1e-30)
        else:
            s = MOMENT_SCALE[name]
        km, kv = _jax.random.split(_jax.random.fold_in(key, i + 1))
        out[name] = w
        out["m_" + name] = s * _jax.random.normal(km, w.shape, _jnp.float32)
        out["v_" + name] = (s * s) * _jax.random.uniform(kv, w.shape, _jnp.float32, 0.5, 1.5)
    if N_MICROBATCH > 1:
        for name, axis in PER_EXAMPLE_BATCH_AXIS.items():
            out[name] = _to_microbatches(out[name], axis)
    return {'x': out['x'], 'positions': out['positions'], 'attn_norm_w': out['attn_norm_w'], 'w_in': out['w_in'], 'q_lat_norm_w': out['q_lat_norm_w'], 'w_uq': out['w_uq'], 'kv_lat_norm_w': out['kv_lat_norm_w'], 'w_ukv': out['w_ukv'], 'q_norm_w': out['q_norm_w'], 'k_norm_w': out['k_norm_w'], 'mla_out_norm_w': out['mla_out_norm_w'], 'conv_w': out['conv_w'], 'a_log': out['a_log'], 'dt_bias': out['dt_bias'], 'gdn_norm_w': out['gdn_norm_w'], 'w_out': out['w_out'], 'mlp_norm_w': out['mlp_norm_w'], 'w_up': out['w_up'], 'w_down': out['w_down'], 'loss_target': out['loss_target'], 'm_attn_norm_w': out['m_attn_norm_w'], 'm_w_in': out['m_w_in'], 'm_q_lat_norm_w': out['m_q_lat_norm_w'], 'm_w_uq': out['m_w_uq'], 'm_kv_lat_norm_w': out['m_kv_lat_norm_w'], 'm_w_ukv': out['m_w_ukv'], 'm_q_norm_w': out['m_q_norm_w'], 'm_k_norm_w': out['m_k_norm_w'], 'm_mla_out_norm_w': out['m_mla_out_norm_w'], 'm_conv_w': out['m_conv_w'], 'm_a_log': out['m_a_log'], 'm_dt_bias': out['m_dt_bias'], 'm_gdn_norm_w': out['m_gdn_norm_w'], 'm_w_out': out['m_w_out'], 'm_mlp_norm_w': out['m_mlp_norm_w'], 'm_w_up': out['m_w_up'], 'm_w_down': out['m_w_down'], 'v_attn_norm_w': out['v_attn_norm_w'], 'v_w_in': out['v_w_in'], 'v_q_lat_norm_w': out['v_q_lat_norm_w'], 'v_w_uq': out['v_w_uq'], 'v_kv_lat_norm_w': out['v_kv_lat_norm_w'], 'v_w_ukv': out['v_w_ukv'], 'v_q_norm_w': out['v_q_norm_w'], 'v_k_norm_w': out['v_k_norm_w'], 'v_mla_out_norm_w': out['v_mla_out_norm_w'], 'v_conv_w': out['v_conv_w'], 'v_a_log': out['v_a_log'], 'v_dt_bias': out['v_dt_bias'], 'v_gdn_norm_w': out['v_gdn_norm_w'], 'v_w_out': out['v_w_out'], 'v_mlp_norm_w': out['v_mlp_norm_w'], 'v_w_up': out['v_w_up'], 'v_w_down': out['v_w_down']}


def _loss(weights, diff, rest, loss_target):
    with _jax.named_scope("forward"):
        args = {**rest, TWIN_DIFF_INPUT: diff, **{k: w.astype(_WEIGHT_DTYPES[k]) for k, w in weights.items()}}
        y = _forward(args)
    with _jax.named_scope("loss_head"):
        err = _jnp.square(y.astype(_jnp.float32) - loss_target)
        return 0.5 * _jnp.sum(_jnp.mean(err, axis=-1)) if err.ndim else 0.5 * err


def _adamw(w, g, m, v):
    m = ADAM_B1 * m + (1.0 - ADAM_B1) * g
    v = ADAM_B2 * v + (1.0 - ADAM_B2) * _jnp.square(g)
    m_hat = m / (1.0 - ADAM_B1 ** ADAM_STEP)
    v_hat = v / (1.0 - ADAM_B2 ** ADAM_STEP)
    delta = -ADAM_LR * (m_hat / (_jnp.sqrt(v_hat) + ADAM_EPS) + ADAM_WD * w)
    return delta, m, v


def reference(x, positions, attn_norm_w, w_in, q_lat_norm_w, w_uq, kv_lat_norm_w, w_ukv, q_norm_w, k_norm_w, mla_out_norm_w, conv_w, a_log, dt_bias, gdn_norm_w, w_out, mlp_norm_w, w_up, w_down, loss_target, m_attn_norm_w, m_w_in, m_q_lat_norm_w, m_w_uq, m_kv_lat_norm_w, m_w_ukv, m_q_norm_w, m_k_norm_w, m_mla_out_norm_w, m_conv_w, m_a_log, m_dt_bias, m_gdn_norm_w, m_w_out, m_mlp_norm_w, m_w_up, m_w_down, v_attn_norm_w, v_w_in, v_q_lat_norm_w, v_w_uq, v_kv_lat_norm_w, v_w_ukv, v_q_norm_w, v_k_norm_w, v_mla_out_norm_w, v_conv_w, v_a_log, v_dt_bias, v_gdn_norm_w, v_w_out, v_mlp_norm_w, v_w_up, v_w_down):
    given = dict(x=x, positions=positions, attn_norm_w=attn_norm_w, w_in=w_in, q_lat_norm_w=q_lat_norm_w, w_uq=w_uq, kv_lat_norm_w=kv_lat_norm_w, w_ukv=w_ukv, q_norm_w=q_norm_w, k_norm_w=k_norm_w, mla_out_norm_w=mla_out_norm_w, conv_w=conv_w, a_log=a_log, dt_bias=dt_bias, gdn_norm_w=gdn_norm_w, w_out=w_out, mlp_norm_w=mlp_norm_w, w_up=w_up, w_down=w_down, loss_target=loss_target, m_attn_norm_w=m_attn_norm_w, m_w_in=m_w_in, m_q_lat_norm_w=m_q_lat_norm_w, m_w_uq=m_w_uq, m_kv_lat_norm_w=m_kv_lat_norm_w, m_w_ukv=m_w_ukv, m_q_norm_w=m_q_norm_w, m_k_norm_w=m_k_norm_w, m_mla_out_norm_w=m_mla_out_norm_w, m_conv_w=m_conv_w, m_a_log=m_a_log, m_dt_bias=m_dt_bias, m_gdn_norm_w=m_gdn_norm_w, m_w_out=m_w_out, m_mlp_norm_w=m_mlp_norm_w, m_w_up=m_w_up, m_w_down=m_w_down, v_attn_norm_w=v_attn_norm_w, v_w_in=v_w_in, v_q_lat_norm_w=v_q_lat_norm_w, v_w_uq=v_w_uq, v_kv_lat_norm_w=v_kv_lat_norm_w, v_w_ukv=v_w_ukv, v_q_norm_w=v_q_norm_w, v_k_norm_w=v_k_norm_w, v_mla_out_norm_w=v_mla_out_norm_w, v_conv_w=v_conv_w, v_a_log=v_a_log, v_dt_bias=v_dt_bias, v_gdn_norm_w=v_gdn_norm_w, v_w_out=v_w_out, v_mlp_norm_w=v_mlp_norm_w, v_w_up=v_w_up, v_w_down=v_w_down)
    weights = {n: given[n] for n in TWIN_WEIGHTS}
    shared = {n: given[n] for n in SHARED_INPUTS}
    per_example = {n: given[n] for n in ['x', 'positions']}
    grad_fn = _jax.value_and_grad(_loss, argnums=(0, 1))

    def one_microbatch(ex, loss_target):
        ex = dict(ex)
        diff = ex.pop(TWIN_DIFF_INPUT)
        return grad_fn(weights, diff, {**shared, **ex}, loss_target)

    if N_MICROBATCH == 1:
        loss, (grad_w, grad_x) = one_microbatch(per_example, given["loss_target"])
    else:
        def body(carry, xs):
            loss_sum, grad_sum = carry
            l_k, (gw_k, gx_k) = one_microbatch(xs[0], xs[1])
            with _jax.named_scope("update"):
                return (loss_sum + l_k, _jax.tree.map(_jnp.add, grad_sum, gw_k)), gx_k

        init = (_jnp.zeros((), _jnp.float32), _jax.tree.map(_jnp.zeros_like, weights))
        (loss, grad_w), grad_x = _jax.lax.scan(body, init, (per_example, given["loss_target"]))
    with _jax.named_scope("update"):
        delta_w, new_m, new_v = {}, {}, {}
        for n in TWIN_WEIGHTS:
            delta_w[n], new_m[n], new_v[n] = _adamw(weights[n], grad_w[n], given["m_" + n], given["v_" + n])
    return (loss, grad_x, *[grad_w[n] for n in TWIN_WEIGHTS], *[delta_w[n] for n in TWIN_WEIGHTS],
            *[new_m[n] for n in TWIN_WEIGHTS], *[new_v[n] for n in TWIN_WEIGHTS])
```

```python
import functools
import math

import jax
import jax.numpy as jnp
from jax import lax
from jax.experimental import pallas as pl
from jax.experimental.pallas import tpu as pltpu

F32 = jnp.float32
BF16 = jnp.bfloat16
HIGHEST = lax.Precision.HIGHEST
MESH = pl.DeviceIdType.MESH

EPS = 1e-6
HEADS = 4
HEAD_DIM = 128
ROPE_DIM = 64
ROPE_HALF = 32
QK_DIM = 192
QK_PAD = 256
LORA = 256
CHUNK = 64
CONV_TAPS = 4
ROPE_THETA = 10000.0
ATTN_SCALE = QK_DIM ** -0.5

LAT_W = 640
GQKV_W = 3 * HEADS * HEAD_DIM
GZ_W = HEADS * HEAD_DIM
GAB_W = 128
PROJ_SPLITS = ((0, LAT_W), (LAT_W, LAT_W + GQKV_W), (LAT_W + GQKV_W, LAT_W + GQKV_W + GZ_W),
               (LAT_W + GQKV_W + GZ_W, LAT_W + GQKV_W + GZ_W + GAB_W))
PROJ_W = PROJ_SPLITS[-1][1]

ADAM_LR = 0.001
ADAM_B1 = 0.9
ADAM_B2 = 0.999
ADAM_EPS = 1e-08
ADAM_WD = 0.01
ADAM_STEP = 10

TOKEN_TILE = 512
FF_TILE = 512
ATTN_TILE = 256
FLAT_TILE = 512
VMEM_LIMIT = 48 * 1024 * 1024

SHARDED = ("w_in", "w_uq", "w_ukv", "conv_w", "w_out", "w_up", "w_down")
ROW_SHARDED = ("w_out", "w_down")
SMALL = ("attn_norm_w", "q_lat_norm_w", "kv_lat_norm_w", "q_norm_w", "k_norm_w", "mla_out_norm_w", "a_log", "dt_bias",
         "gdn_norm_w", "mlp_norm_w")
WEIGHTS = ("attn_norm_w", "w_in", "q_lat_norm_w", "w_uq", "kv_lat_norm_w", "w_ukv", "q_norm_w", "k_norm_w", "mla_out_norm_w",
           "conv_w", "a_log", "dt_bias", "gdn_norm_w", "w_out", "mlp_norm_w", "w_up", "w_down")


def _sds(shape, dtype):
    return jax.ShapeDtypeStruct(shape, dtype)


def _params(semantics):
    return pltpu.CompilerParams(dimension_semantics=semantics, vmem_limit_bytes=VMEM_LIMIT)


def _block(n):
    for b in (512, 256, 128):
        if n % b == 0:
            return b
    return n


def _dg(a, b, ca, cb, prec):
    return lax.dot_general(a, b, (((ca,), (cb,)), ((), ())), precision=prec, preferred_element_type=F32)


def _matmul_family(dtype, prec):
    def nn_raw(a, b):
        return _dg(a.astype(dtype), b.astype(dtype), 1, 0, prec)

    def nt_raw(a, b):
        return _dg(a.astype(dtype), b.astype(dtype), 1, 1, prec)

    def tn_raw(a, b):
        return _dg(a.astype(dtype), b.astype(dtype), 0, 0, prec)

    @jax.custom_vjp
    def nn(a, b):
        return nn_raw(a, b)

    nn.defvjp(lambda a, b: (nn_raw(a, b), (a, b)), lambda r, g: (nt_raw(g, r[1]), tn_raw(r[0], g)))

    @jax.custom_vjp
    def nt(a, b):
        return nt_raw(a, b)

    nt.defvjp(lambda a, b: (nt_raw(a, b), (a, b)), lambda r, g: (nn_raw(g, r[1]), tn_raw(g, r[0])))

    @jax.custom_vjp
    def tn(a, b):
        return tn_raw(a, b)

    tn.defvjp(lambda a, b: (tn_raw(a, b), (a, b)), lambda r, g: (nt_raw(r[1], g), nn_raw(r[0], g)))
    return nn, nt, tn


_bf_nn, _bf_nt, _bf_tn = _matmul_family(BF16, None)
_hi_nn, _hi_nt, _hi_tn = _matmul_family(F32, HIGHEST)


@jax.custom_vjp
def _swap_halves(t):
    return pltpu.roll(t, 64, 1)


_swap_halves.defvjp(lambda t: (pltpu.roll(t, 64, 1), None), lambda _, g: (pltpu.roll(g, 64, 1),))


@functools.partial(jax.custom_vjp, nondiff_argnums=(2,))
def _shift_rows(x, keep, s):
    return pltpu.roll(x, s, 0) * keep


def _shift_rows_fwd(x, keep, s):
    return pltpu.roll(x, s, 0) * keep, keep


def _shift_rows_bwd(s, keep, g):
    return pltpu.roll(g * keep, keep.shape[0] - s, 0), jnp.zeros_like(keep)


_shift_rows.defvjp(_shift_rows_fwd, _shift_rows_bwd)


def _sigmoid(x):
    return 0.5 * jnp.tanh(0.5 * x) + 0.5


def _softplus(x):
    return jnp.maximum(x, 0.0) + jnp.log(1.0 + jnp.exp(jnp.minimum(x, -x)))


def _silu(x):
    return x * _sigmoid(x)


def _rms(x, w, n=None):
    n = x.shape[-1] if n is None else n
    r = lax.rsqrt(jnp.sum(x * x, axis=-1, keepdims=True) * (1.0 / n) + EPS)
    return x * r * w


def _rope(t, cos_f, sin_f):
    return t * cos_f + _swap_halves(t) * sin_f


def _rope_tables(pos_col, freq_row, sign_row):
    ang = pos_col.astype(F32) * freq_row
    return jnp.cos(ang), jnp.sin(ang) * sign_row


def _onehot_row(lane):
    return (lax.broadcasted_iota(jnp.int32, (1, 128), 1) == lane).astype(F32)


def _row_spec(tm, w):
    return pl.BlockSpec((tm, w), lambda i: (i, 0))


def _const_spec(shape):
    return pl.BlockSpec(shape, lambda *_: (0,) * len(shape))


def _in_proj_fwd(x2, w_an, w_in_p):
    T, D = x2.shape
    tm = min(TOKEN_TILE, T)

    def body(x_ref, wn_ref, w_ref, xn_ref, lat_ref, gqkv_ref, gz_ref, gab_ref):
        x = x_ref[...]
        r = lax.rsqrt(jnp.mean(x * x, axis=-1, keepdims=True) + EPS)
        xn = (x * r * wn_ref[...]).astype(BF16)
        xn_ref[...] = xn
        for ref, (a, b) in zip((lat_ref, gqkv_ref, gz_ref, gab_ref), PROJ_SPLITS):
            ref[...] = jnp.dot(xn, w_ref[:, a:b], preferred_element_type=F32)

    widths = [b - a for a, b in PROJ_SPLITS]
    return pl.pallas_call(
        body, grid=(T // tm,),
        in_specs=[_row_spec(tm, D), _const_spec((1, D)), _const_spec((D, PROJ_W))],
        out_specs=[_row_spec(tm, D)] + [_row_spec(tm, w) for w in widths],
        out_shape=[_sds((T, D), BF16)] + [_sds((T, w), F32) for w in widths],
        compiler_params=_params(("parallel",)), name="in_proj_fwd",
    )(x2, w_an, w_in_p)


def _in_proj_bwd(d_lat, d_gqkv, d_gz, d_gab, w_in_p, x2, w_an, dh):
    T, D = x2.shape
    tm = min(TOKEN_TILE, T)

    def body(dl_ref, dq_ref, dz_ref, da_ref, w_ref, x_ref, wn_ref, dh_ref, dx_ref, dp_ref, dwn_ref):
        @pl.when(pl.program_id(0) == 0)
        def _():
            dwn_ref[...] = jnp.zeros_like(dwn_ref)

        dxn = jnp.zeros((tm, D), F32)
        for ref, (a, b) in zip((dl_ref, dq_ref, dz_ref, da_ref), PROJ_SPLITS):
            piece = ref[...].astype(BF16)
            dp_ref[:, a:b] = piece
            dxn += _dg(piece, w_ref[:, a:b], 1, 1, None)
        _, pull = jax.vjp(_rms, x_ref[...], wn_ref[...])
        dx, dwn = pull(dxn)
        dx_ref[...] = dx + dh_ref[...]
        dwn_ref[...] += dwn

    widths = [b - a for a, b in PROJ_SPLITS]
    return pl.pallas_call(
        body, grid=(T // tm,),
        in_specs=[_row_spec(tm, w) for w in widths] + [_const_spec((D, PROJ_W)), _row_spec(tm, D), _const_spec((1, D)),
                                                       _row_spec(tm, D)],
        out_specs=[_row_spec(tm, D), _row_spec(tm, PROJ_W), _const_spec((1, D))],
        out_shape=[_sds((T, D), F32), _sds((T, PROJ_W), BF16), _sds((1, D), F32)],
        compiler_params=_params(("arbitrary",)), name="in_proj_bwd",
    )(d_lat, d_gqkv, d_gz, d_gab, w_in_p, x2, w_an, dh)


def _wgrad(a, b, name):
    T, k1 = a.shape
    k2 = b.shape[1]
    b1, b2, tt = _block(k1), _block(k2), min(TOKEN_TILE, T)

    def body(a_ref, b_ref, o_ref):
        @pl.when(pl.program_id(2) == 0)
        def _():
            o_ref[...] = jnp.zeros_like(o_ref)

        o_ref[...] += _dg(a_ref[...].astype(BF16), b_ref[...].astype(BF16), 0, 0, None)

    return pl.pallas_call(
        body, grid=(k1 // b1, k2 // b2, T // tt),
        in_specs=[pl.BlockSpec((tt, b1), lambda i, j, t: (t, i)), pl.BlockSpec((tt, b2), lambda i, j, t: (t, j))],
        out_specs=pl.BlockSpec((b1, b2), lambda i, j, t: (i, j)),
        out_shape=_sds((k1, k2), F32),
        compiler_params=_params(("parallel", "parallel", "arbitrary")), name=name,
    )(a, b)


def _mla_pre_fn(q_lat, kv_lat, kpe, ln_q, ln_kv, w_list, qn_n, qn_p, kn_n, kn_p, cos_f, sin_f):
    qn = _rms(q_lat, ln_q)
    kvn = _rms(kv_lat, ln_kv)
    kp = _rope(_rms(kpe, kn_p, ROPE_DIM), cos_f, sin_f)
    outs = []
    for h in range(HEADS):
        outs.append(_rms(_bf_nn(qn, w_list[h]), qn_n))
        outs.append(_rope(_rms(_bf_nn(qn, w_list[HEADS + h]), qn_p, ROPE_DIM), cos_f, sin_f))
        outs.append(_rms(_bf_nn(kvn, w_list[2 * HEADS + h]), kn_n))
        outs.append(_bf_nn(kvn, w_list[3 * HEADS + h]))
    return tuple(outs) + (kp,)


def _mla_pre_operands(lat_ref, pos_ref, ln_ref, w_ref, nw_ref, rope_ref):
    cos_f, sin_f = _rope_tables(pos_ref[...], rope_ref[0:1, :], rope_ref[1:2, :])
    diff = (lat_ref[:, 0:LORA], lat_ref[:, LORA:2 * LORA], lat_ref[:, 2 * LORA:LAT_W], ln_ref[0:1, :], ln_ref[1:2, :],
            [w_ref[i] for i in range(4 * HEADS)], nw_ref[0:1, :], nw_ref[1:2, :], nw_ref[2:3, :], nw_ref[3:4, :])
    return diff, cos_f, sin_f


def _mla_pre_fwd(lat, pos, ln_w, w_mla, nw, rope_rows):
    T = lat.shape[0]
    tm = min(TOKEN_TILE, T)

    def body(lat_ref, pos_ref, ln_ref, w_ref, nw_ref, rope_ref, q_ref, k_ref, v_ref):
        diff, cos_f, sin_f = _mla_pre_operands(lat_ref, pos_ref, ln_ref, w_ref, nw_ref, rope_ref)
        outs = _mla_pre_fn(*diff, cos_f, sin_f)
        kp = outs[-1].astype(BF16)
        for h in range(HEADS):
            q_n, q_p, k_n, v = outs[4 * h:4 * h + 4]
            q_ref[:, h * QK_PAD:h * QK_PAD + HEAD_DIM] = q_n.astype(BF16)
            q_ref[:, h * QK_PAD + HEAD_DIM:(h + 1) * QK_PAD] = q_p.astype(BF16)
            k_ref[:, h * QK_PAD:h * QK_PAD + HEAD_DIM] = k_n.astype(BF16)
            k_ref[:, h * QK_PAD + HEAD_DIM:(h + 1) * QK_PAD] = kp
            v_ref[:, h * HEAD_DIM:(h + 1) * HEAD_DIM] = v.astype(BF16)

    return pl.pallas_call(
        body, grid=(T // tm,),
        in_specs=[_row_spec(tm, LAT_W), _row_spec(tm, 1), _const_spec((2, LORA)), _const_spec((4 * HEADS, LORA, 128)),
                  _const_spec((8, 128)), _const_spec((8, 128))],
        out_specs=[_row_spec(tm, HEADS * QK_PAD), _row_spec(tm, HEADS * QK_PAD), _row_spec(tm, HEADS * HEAD_DIM)],
        out_shape=[_sds((T, HEADS * QK_PAD), BF16), _sds((T, HEADS * QK_PAD), BF16), _sds((T, HEADS * HEAD_DIM), BF16)],
        compiler_params=_params(("parallel",)), name="mla_pre_fwd",
    )(lat, pos, ln_w, w_mla, nw, rope_rows)


def _mla_pre_bwd(lat, pos, ln_w, w_mla, nw, rope_rows, dq, dk, dv):
    T = lat.shape[0]
    tm = min(TOKEN_TILE, T)

    def body(lat_ref, pos_ref, ln_ref, w_ref, nw_ref, rope_ref, dq_ref, dk_ref, dv_ref, dlat_ref, dln_ref, dw_ref, dnw_ref):
        @pl.when(pl.program_id(0) == 0)
        def _():
            dln_ref[...] = jnp.zeros_like(dln_ref)
            dw_ref[...] = jnp.zeros_like(dw_ref)
            dnw_ref[...] = jnp.zeros_like(dnw_ref)

        diff, cos_f, sin_f = _mla_pre_operands(lat_ref, pos_ref, ln_ref, w_ref, nw_ref, rope_ref)
        _, pull = jax.vjp(lambda *a: _mla_pre_fn(*a, cos_f, sin_f), *diff)
        cts = []
        d_kp = jnp.zeros((tm, 128), F32)
        for h in range(HEADS):
            cts.append(dq_ref[:, h * QK_PAD:h * QK_PAD + HEAD_DIM])
            cts.append(dq_ref[:, h * QK_PAD + HEAD_DIM:(h + 1) * QK_PAD])
            cts.append(dk_ref[:, h * QK_PAD:h * QK_PAD + HEAD_DIM])
            cts.append(dv_ref[:, h * HEAD_DIM:(h + 1) * HEAD_DIM])
            d_kp += dk_ref[:, h * QK_PAD + HEAD_DIM:(h + 1) * QK_PAD]
        d_ql, d_kvl, d_kpe, d_lnq, d_lnkv, d_w, d_qn_n, d_qn_p, d_kn_n, d_kn_p = pull(tuple(cts) + (d_kp,))
        dlat_ref[:, 0:LORA] = d_ql
        dlat_ref[:, LORA:2 * LORA] = d_kvl
        dlat_ref[:, 2 * LORA:LAT_W] = d_kpe
        dln_ref[0:1, :] += d_lnq
        dln_ref[1:2, :] += d_lnkv
        for i in range(4 * HEADS):
            dw_ref[i] += d_w[i]
        for i, d in enumerate((d_qn_n, d_qn_p, d_kn_n, d_kn_p)):
            dnw_ref[i:i + 1, :] += d

    return pl.pallas_call(
        body, grid=(T // tm,),
        in_specs=[_row_spec(tm, LAT_W), _row_spec(tm, 1), _const_spec((2, LORA)), _const_spec((4 * HEADS, LORA, 128)),
                  _const_spec((8, 128)), _const_spec((8, 128)),
                  _row_spec(tm, HEADS * QK_PAD), _row_spec(tm, HEADS * QK_PAD), _row_spec(tm, HEADS * HEAD_DIM)],
        out_specs=[_row_spec(tm, LAT_W), _const_spec((2, LORA)), _const_spec((4 * HEADS, LORA, 128)), _const_spec((8, 128))],
        out_shape=[_sds((T, LAT_W), F32), _sds((2, LORA), F32), _sds((4 * HEADS, LORA, 128), F32), _sds((8, 128), F32)],
        compiler_params=_params(("arbitrary",)), name="mla_pre_bwd",
    )(lat, pos, ln_w, w_mla, nw, rope_rows, dq, dk, dv)


def _causal_mask(i, j, tq, tk):
    row = i * tq + lax.broadcasted_iota(jnp.int32, (tq, tk), 0)
    col = j * tk + lax.broadcasted_iota(jnp.int32, (tq, tk), 1)
    return col <= row


def _attn_fwd(q, k, v):
    B, S, _ = q.shape
    t = min(ATTN_TILE, S)

    def body(q_ref, k_ref, v_ref, o_ref, lse_ref):
        i = pl.program_id(2)
        qb = q_ref[0]

        def step(j, carry):
            m, l, acc = carry
            rows = pl.ds(pl.multiple_of(j * t, t), t)
            s = _dg(qb, k_ref[0, rows, :], 1, 1, None) * ATTN_SCALE
            s = jnp.where(_causal_mask(i, j, t, t), s, -1e30)
            m_new = jnp.maximum(m, jnp.max(s, axis=-1, keepdims=True))
            p = jnp.exp(s - m_new)
            alpha = jnp.exp(m - m_new)
            l = alpha * l + jnp.sum(p, axis=-1, keepdims=True)
            acc = alpha * acc + jnp.dot(p.astype(BF16), v_ref[0, rows, :], preferred_element_type=F32)
            return m_new, l, acc

        init = (jnp.full((t, 1), -1e30, F32), jnp.zeros((t, 1), F32), jnp.zeros((t, HEAD_DIM), F32))
        m, l, acc = lax.fori_loop(0, i + 1, step, init)
        o_ref[0] = acc / l
        lse_ref[0, 0] = m + jnp.log(l)

    return pl.pallas_call(
        body, grid=(B, HEADS, S // t),
        in_specs=[pl.BlockSpec((1, t, QK_PAD), lambda b, h, i: (b, i, h)),
                  pl.BlockSpec((1, S, QK_PAD), lambda b, h, i: (b, 0, h)),
                  pl.BlockSpec((1, S, HEAD_DIM), lambda b, h, i: (b, 0, h))],
        out_specs=[pl.BlockSpec((1, t, HEAD_DIM), lambda b, h, i: (b, i, h)),
                   pl.BlockSpec((1, 1, t, 1), lambda b, h, i: (b, h, i, 0))],
        out_shape=[_sds((B, S, HEADS * HEAD_DIM), F32), _sds((B, HEADS, S, 1), F32)],
        compiler_params=_params(("parallel", "parallel", "parallel")), name="attn_fwd",
    )(q, k, v)


def _attn_bwd(q, k, v, o, lse, do):
    B, S, _ = q.shape
    t = min(ATTN_TILE, S)
    nq = S // t

    def body(q_ref, k_ref, v_ref, o_ref, lse_ref, do_ref, dq_ref, dk_ref, dv_ref, dsum_ref):
        j = pl.program_id(2)

        @pl.when(j == 0)
        def _():
            dq_ref[...] = jnp.zeros_like(dq_ref)
            dsum_ref[...] = jnp.sum(do_ref[0] * o_ref[0], axis=-1, keepdims=True)

        kb = k_ref[0]
        vb = v_ref[0]

        def step(i, carry):
            dk, dv = carry
            rows = pl.ds(pl.multiple_of(i * t, t), t)
            qb = q_ref[0, rows, :]
            dob = do_ref[0, rows, :].astype(BF16)
            s = _dg(qb, kb, 1, 1, None) * ATTN_SCALE
            p = jnp.where(_causal_mask(i, j, t, t), jnp.exp(s - lse_ref[0, 0, rows, :]), 0.0)
            pb = p.astype(BF16)
            dv = dv + _dg(pb, dob, 0, 0, None)
            dp = _dg(dob, vb, 1, 1, None)
            ds = (p * (dp - dsum_ref[rows, :]) * ATTN_SCALE).astype(BF16)
            dq_ref[0, rows, :] += jnp.dot(ds, kb, preferred_element_type=F32)
            dk = dk + _dg(ds, qb, 0, 0, None)
            return dk, dv

        dk, dv = lax.fori_loop(j, nq, step, (jnp.zeros((t, QK_PAD), F32), jnp.zeros((t, HEAD_DIM), F32)))
        dk_ref[0] = dk
        dv_ref[0] = dv

    return pl.pallas_call(
        body, grid=(B, HEADS, nq),
        in_specs=[pl.BlockSpec((1, S, QK_PAD), lambda b, h, j: (b, 0, h)),
                  pl.BlockSpec((1, t, QK_PAD), lambda b, h, j: (b, j, h)),
                  pl.BlockSpec((1, t, HEAD_DIM), lambda b, h, j: (b, j, h)),
                  pl.BlockSpec((1, S, HEAD_DIM), lambda b, h, j: (b, 0, h)),
                  pl.BlockSpec((1, 1, S, 1), lambda b, h, j: (b, h, 0, 0)),
                  pl.BlockSpec((1, S, HEAD_DIM), lambda b, h, j: (b, 0, h))],
        out_specs=[pl.BlockSpec((1, S, QK_PAD), lambda b, h, j: (b, 0, h)),
                   pl.BlockSpec((1, t, QK_PAD), lambda b, h, j: (b, j, h)),
                   pl.BlockSpec((1, t, HEAD_DIM), lambda b, h, j: (b, j, h))],
        out_shape=[_sds((B, S, HEADS * QK_PAD), F32), _sds((B, S, HEADS * QK_PAD), F32), _sds((B, S, HEADS * HEAD_DIM), F32)],
        scratch_shapes=[pltpu.VMEM((S, 1), F32)],
        compiler_params=_params(("parallel", "parallel", "arbitrary")), name="attn_bwd",
    )(q, k, v, o, lse, do)


def _gdn_pre_fn(xq, xk, xv, wq, wk, wv, keeps):
    def conv_silu(x, w):
        acc = x * w[3]
        for s in (1, 2, 3):
            acc = acc + _shift_rows(x, keeps[s - 1], s) * w[3 - s]
        return _silu(acc)

    def l2(x):
        return x * lax.rsqrt(jnp.sum(x * x, axis=-1, keepdims=True) + EPS)

    return l2(conv_silu(xq, wq)) * (HEAD_DIM ** -0.5), l2(conv_silu(xk, wk)), conv_silu(xv, wv)


def _gdn_pre_specs(S):
    x_specs = [pl.BlockSpec((1, S, HEAD_DIM), lambda h, b, g=g: (b, 0, g * HEADS + h)) for g in range(3)]
    w_specs = [pl.BlockSpec((CONV_TAPS, HEAD_DIM), lambda h, b, g=g: (0, g * HEADS + h)) for g in range(3)]
    out_spec = pl.BlockSpec((1, S, HEAD_DIM), lambda h, b: (b, 0, h))
    return x_specs, w_specs, out_spec


def _row_keeps(S):
    t = lax.broadcasted_iota(jnp.int32, (S, HEAD_DIM), 0)
    return [(t >= s).astype(F32) for s in (1, 2, 3)]


def _gdn_pre_fwd(gqkv, conv_w):
    B, S, _ = gqkv.shape
    x_specs, w_specs, out_spec = _gdn_pre_specs(S)

    def body(xq_ref, xk_ref, xv_ref, wq_ref, wk_ref, wv_ref, q_ref, k_ref, v_ref):
        taps = [[w[i:i + 1, :] for i in range(CONV_TAPS)] for w in (wq_ref, wk_ref, wv_ref)]
        q, k, v = _gdn_pre_fn(xq_ref[0], xk_ref[0], xv_ref[0], *taps, _row_keeps(S))
        q_ref[0], k_ref[0], v_ref[0] = q, k, v

    return pl.pallas_call(
        body, grid=(HEADS, B), in_specs=x_specs + w_specs, out_specs=[out_spec] * 3,
        out_shape=[_sds((B, S, HEADS * HEAD_DIM), F32)] * 3,
        compiler_params=_params(("parallel", "parallel")), name="gdn_pre_fwd",
    )(gqkv, gqkv, gqkv, conv_w, conv_w, conv_w)


def _gdn_pre_bwd(gqkv, conv_w, dq, dk, dv):
    B, S, _ = gqkv.shape
    x_specs, w_specs, out_spec = _gdn_pre_specs(S)
    dw_spec = pl.BlockSpec((CONV_TAPS, HEAD_DIM), lambda h, b: (0, h))

    def body(xq_ref, xk_ref, xv_ref, wq_ref, wk_ref, wv_ref, dq_ref, dk_ref, dv_ref,
             dxq_ref, dxk_ref, dxv_ref, dwq_ref, dwk_ref, dwv_ref):
        @pl.when(pl.program_id(1) == 0)
        def _():
            for r in (dwq_ref, dwk_ref, dwv_ref):
                r[...] = jnp.zeros_like(r)

        taps = [[w[i:i + 1, :] for i in range(CONV_TAPS)] for w in (wq_ref, wk_ref, wv_ref)]
        keeps = _row_keeps(S)
        _, pull = jax.vjp(lambda *a: _gdn_pre_fn(*a, keeps), xq_ref[0], xk_ref[0], xv_ref[0], *taps)
        dxq, dxk, dxv, dwq, dwk, dwv = pull((dq_ref[0], dk_ref[0], dv_ref[0]))
        dxq_ref[0], dxk_ref[0], dxv_ref[0] = dxq, dxk, dxv
        for ref, dw in ((dwq_ref, dwq), (dwk_ref, dwk), (dwv_ref, dwv)):
            for i in range(CONV_TAPS):
                ref[i:i + 1, :] += dw[i]

    hw = HEADS * HEAD_DIM
    return pl.pallas_call(
        body, grid=(HEADS, B), in_specs=x_specs + w_specs + [out_spec] * 3,
        out_specs=[out_spec] * 3 + [dw_spec] * 3,
        out_shape=[_sds((B, S, hw), F32)] * 3 + [_sds((CONV_TAPS, hw), F32)] * 3,
        compiler_params=_params(("parallel", "arbitrary")), name="gdn_pre_bwd",
    )(gqkv, gqkv, gqkv, conv_w, conv_w, conv_w, dq, dk, dv)


def _chunk_masks():
    i = lax.broadcasted_iota(jnp.int32, (CHUNK, CHUNK), 0)
    j = lax.broadcasted_iota(jnp.int32, (CHUNK, CHUNK), 1)
    return {"le": (j <= i).astype(F32), "gt_t": (j > i).astype(F32), "strict": (j < i).astype(F32)}


def _gdn_chunk_fn(h, masks):
    pick_a, pick_b = _onehot_row(h), _onehot_row(HEADS + h)
    lower, upper_t, strict = masks["le"], masks["gt_t"], masks["strict"]
    ones_row = jnp.ones((1, HEAD_DIM), F32)

    def f(q, k, v, gab, a_row, dt_row, state):
        ga = jnp.sum(gab * pick_a, axis=1, keepdims=True)
        gb = jnp.sum(gab * pick_b, axis=1, keepdims=True)
        a_log = jnp.sum(a_row * pick_a, axis=1, keepdims=True)
        dt_bias = jnp.sum(dt_row * pick_a, axis=1, keepdims=True)
        beta = _sigmoid(gb)
        g = -jnp.exp(a_log) * _softplus(ga + dt_bias)
        g_wide = g * ones_row
        cum = _hi_nn(lower, g_wide)
        rest = _hi_nn(upper_t, g_wide)
        total = jnp.sum(g_wide, axis=0, keepdims=True)
        diff = _hi_nn(lower, g * strict)
        decay = lower * jnp.exp(diff)
        e_cum = jnp.exp(cum)
        lmat = strict * (beta * _bf_nt(k, k) * decay)
        u = v * beta
        w = k * (beta * e_cum)
        u = u - _hi_nn(lmat, u)
        w = w - _hi_nn(lmat, w)
        power = lmat
        for _ in range(5):
            power = _hi_nn(power, power)
            u = u + _hi_nn(power, u)
            w = w + _hi_nn(power, w)
        attn = _bf_nt(q, k) * decay
        v_new = u - _bf_nn(w, state)
        o = _bf_nn(q * e_cum, state) + _bf_nn(attn, v_new)
        new_state = state * jnp.exp(total) + _bf_tn(k * jnp.exp(rest), v_new)
        return o, new_state

    return f


def _gdn_chunk_fwd(q, k, v, gab, scal):
    B, S, W = q.shape
    N = S // CHUNK

    def body(q_ref, k_ref, v_ref, gab_ref, sc_ref, o_ref, st_ref, state_ref):
        @pl.when(pl.program_id(1) == 0)
        def _():
            state_ref[...] = jnp.zeros_like(state_ref)

        masks = _chunk_masks()
        for h in range(HEADS):
            lanes = slice(h * HEAD_DIM, (h + 1) * HEAD_DIM)
            state = state_ref[h]
            st_ref[0, 0, h] = state
            o, new_state = _gdn_chunk_fn(h, masks)(q_ref[0, :, lanes], k_ref[0, :, lanes], v_ref[0, :, lanes], gab_ref[0],
                                                   sc_ref[0:1, :], sc_ref[1:2, :], state)
            o_ref[0, :, lanes] = o
            state_ref[h] = new_state

    seq = pl.BlockSpec((1, CHUNK, W), lambda b, n: (b, n, 0))
    return pl.pallas_call(
        body, grid=(B, N),
        in_specs=[seq, seq, seq, pl.BlockSpec((1, CHUNK, GAB_W), lambda b, n: (b, n, 0)), _const_spec((8, 128))],
        out_specs=[seq, pl.BlockSpec((1, 1, HEADS, HEAD_DIM, HEAD_DIM), lambda b, n: (b, n, 0, 0, 0))],
        out_shape=[_sds((B, S, W), F32), _sds((B, N, HEADS, HEAD_DIM, HEAD_DIM), F32)],
        scratch_shapes=[pltpu.VMEM((HEADS, HEAD_DIM, HEAD_DIM), F32)],
        compiler_params=_params(("parallel", "arbitrary")), name="gdn_chunk_fwd",
    )(q, k, v, gab, scal)


def _gdn_chunk_bwd(q, k, v, gab, scal, states, do):
    B, S, W = q.shape
    N = S // CHUNK

    def body(q_ref, k_ref, v_ref, gab_ref, sc_ref, st_ref, do_ref, dq_ref, dk_ref, dv_ref, dgab_ref, dsc_ref, dstate_ref):
        @pl.when(pl.program_id(1) == 0)
        def _():
            dstate_ref[...] = jnp.zeros_like(dstate_ref)

        @pl.when((pl.program_id(0) == 0) & (pl.program_id(1) == 0))
        def _():
            dsc_ref[...] = jnp.zeros_like(dsc_ref)

        masks = _chunk_masks()
        d_gab = jnp.zeros((CHUNK, GAB_W), F32)
        d_a = jnp.zeros((1, 128), F32)
        d_dt = jnp.zeros((1, 128), F32)
        for h in range(HEADS):
            lanes = slice(h * HEAD_DIM, (h + 1) * HEAD_DIM)
            _, pull = jax.vjp(_gdn_chunk_fn(h, masks), q_ref[0, :, lanes], k_ref[0, :, lanes], v_ref[0, :, lanes], gab_ref[0],
                              sc_ref[0:1, :], sc_ref[1:2, :], st_ref[0, 0, h])
            dq, dk, dv, dg, da, ddt, dstate = pull((do_ref[0, :, lanes], dstate_ref[h]))
            dq_ref[0, :, lanes] = dq
            dk_ref[0, :, lanes] = dk
            dv_ref[0, :, lanes] = dv
            dstate_ref[h] = dstate
            d_gab, d_a, d_dt = d_gab + dg, d_a + da, d_dt + ddt
        dgab_ref[0] = d_gab
        dsc_ref[0:1, :] += d_a
        dsc_ref[1:2, :] += d_dt

    seq = pl.BlockSpec((1, CHUNK, W), lambda b, n: (b, N - 1 - n, 0))
    gab_spec = pl.BlockSpec((1, CHUNK, GAB_W), lambda b, n: (b, N - 1 - n, 0))
    return pl.pallas_call(
        body, grid=(B, N),
        in_specs=[seq, seq, seq, gab_spec, _const_spec((8, 128)),
                  pl.BlockSpec((1, 1, HEADS, HEAD_DIM, HEAD_DIM), lambda b, n: (b, N - 1 - n, 0, 0, 0)), seq],
        out_specs=[seq, seq, seq, gab_spec, _const_spec((8, 128))],
        out_shape=[_sds((B, S, W), F32)] * 3 + [_sds((B, S, GAB_W), F32), _sds((8, 128), F32)],
        scratch_shapes=[pltpu.VMEM((HEADS, HEAD_DIM, HEAD_DIM), F32)],
        compiler_params=_params(("arbitrary", "arbitrary")), name="gdn_chunk_bwd",
    )(q, k, v, gab, scal, states, do)


def _mix_fn(ao, go, gz, w_mla, w_gdn):
    return tuple(_rms(ao[h], w_mla[h]) for h in range(HEADS)) + tuple(_rms(go[h], w_gdn) * _silu(gz[h]) for h in range(HEADS))


def _mix_operands(ao_ref, go_ref, gz_ref, nw_ref):
    blocks = lambda ref: [ref[:, h * HEAD_DIM:(h + 1) * HEAD_DIM] for h in range(HEADS)]
    return blocks(ao_ref), blocks(go_ref), blocks(gz_ref), [nw_ref[h:h + 1, :] for h in range(HEADS)], nw_ref[HEADS:HEADS + 1, :]


def _mix_fwd(ao, go, gz, nw, w_out, x2):
    T, D = x2.shape
    tm = min(TOKEN_TILE, T)
    MW = 2 * HEADS * HEAD_DIM

    def body(ao_ref, go_ref, gz_ref, nw_ref, w_ref, x_ref, mix_ref, h_ref):
        outs = _mix_fn(*_mix_operands(ao_ref, go_ref, gz_ref, nw_ref))
        for i, piece in enumerate(outs):
            mix_ref[:, i * HEAD_DIM:(i + 1) * HEAD_DIM] = piece.astype(BF16)
        h_ref[...] = x_ref[...] + jnp.dot(mix_ref[...], w_ref[...], preferred_element_type=F32)

    half = HEADS * HEAD_DIM
    return pl.pallas_call(
        body, grid=(T // tm,),
        in_specs=[_row_spec(tm, half), _row_spec(tm, half), _row_spec(tm, half), _const_spec((8, 128)), _const_spec((MW, D)),
                  _row_spec(tm, D)],
        out_specs=[_row_spec(tm, MW), _row_spec(tm, D)],
        out_shape=[_sds((T, MW), BF16), _sds((T, D), F32)],
        compiler_params=_params(("parallel",)), name="mix_fwd",
    )(ao, go, gz, nw, w_out, x2)


def _mix_bwd(ao, go, gz, nw, w_out, dh):
    T, D = dh.shape
    tm = min(TOKEN_TILE, T)
    MW = 2 * HEADS * HEAD_DIM
    half = HEADS * HEAD_DIM

    def body(ao_ref, go_ref, gz_ref, nw_ref, w_ref, dh_ref, dao_ref, dgo_ref, dgz_ref, dnw_ref):
        @pl.when(pl.program_id(0) == 0)
        def _():
            dnw_ref[...] = jnp.zeros_like(dnw_ref)

        d_mix = _dg(dh_ref[...].astype(BF16), w_ref[...], 1, 1, None)
        cts = tuple(d_mix[:, i * HEAD_DIM:(i + 1) * HEAD_DIM] for i in range(2 * HEADS))
        _, pull = jax.vjp(_mix_fn, *_mix_operands(ao_ref, go_ref, gz_ref, nw_ref))
        d_ao, d_go, d_gz, d_wm, d_wg = pull(cts)
        for h in range(HEADS):
            lanes = slice(h * HEAD_DIM, (h + 1) * HEAD_DIM)
            dao_ref[:, lanes] = d_ao[h]
            dgo_ref[:, lanes] = d_go[h]
            dgz_ref[:, lanes] = d_gz[h]
            dnw_ref[h:h + 1, :] += d_wm[h]
        dnw_ref[HEADS:HEADS + 1, :] += d_wg

    return pl.pallas_call(
        body, grid=(T // tm,),
        in_specs=[_row_spec(tm, half), _row_spec(tm, half), _row_spec(tm, half), _const_spec((8, 128)), _const_spec((MW, D)),
                  _row_spec(tm, D)],
        out_specs=[_row_spec(tm, half)] * 3 + [_const_spec((8, 128))],
        out_shape=[_sds((T, half), F32)] * 3 + [_sds((8, 128), F32)],
        compiler_params=_params(("arbitrary",)), name="mix_bwd",
    )(ao, go, gz, nw, w_out, dh)


def _mlp_fwd(h2, w_mn, w_up, w_down, target):
    T, D = h2.shape
    FF = w_up.shape[1]
    tm, tf = min(TOKEN_TILE, T), min(FF_TILE, FF)
    nf = FF // tf

    def body(h_ref, wn_ref, wu_ref, wd_ref, t_ref, hn_ref, dy_ref, sq_ref, acc_ref):
        j = pl.program_id(1)

        @pl.when(j == 0)
        def _():
            hn_ref[...] = _rms(h_ref[...], wn_ref[...]).astype(BF16)
            acc_ref[...] = jnp.zeros_like(acc_ref)

        up = jnp.dot(hn_ref[...], wu_ref[...], preferred_element_type=F32)
        act = jnp.square(jnp.maximum(up, 0.0)).astype(BF16)
        acc_ref[...] += jnp.dot(act, wd_ref[...], preferred_element_type=F32)

        @pl.when(j == nf - 1)
        def _():
            err = h_ref[...] + acc_ref[...] - t_ref[...]
            dy_ref[...] = err * (1.0 / D)
            sq_ref[...] = jnp.zeros_like(sq_ref) + jnp.sum(err * err)

    tok = lambda w: pl.BlockSpec((tm, w), lambda i, j: (i, 0))
    return pl.pallas_call(
        body, grid=(T // tm, nf),
        in_specs=[tok(D), _const_spec((1, D)), pl.BlockSpec((D, tf), lambda i, j: (0, j)), pl.BlockSpec((tf, D), lambda i, j: (j, 0)),
                  tok(D)],
        out_specs=[tok(D), tok(D), pl.BlockSpec((1, 8, 128), lambda i, j: (i, 0, 0))],
        out_shape=[_sds((T, D), BF16), _sds((T, D), F32), _sds((T // tm, 8, 128), F32)],
        scratch_shapes=[pltpu.VMEM((tm, D), F32)],
        compiler_params=_params(("parallel", "arbitrary")), name="mlp_fwd",
    )(h2, w_mn, w_up, w_down, target)


def _mlp_bwd(h2, w_mn, hn, w_up, w_down, dy):
    T, D = h2.shape
    FF = w_up.shape[1]
    tm, tf = min(TOKEN_TILE, T), min(FF_TILE, FF)
    nf = FF // tf

    def body(h_ref, wn_ref, hn_ref, wu_ref, wd_ref, dy_ref, dh_ref, act_ref, dup_ref, dwn_ref, acc_ref):
        i, j = pl.program_id(0), pl.program_id(1)

        @pl.when((i == 0) & (j == 0))
        def _():
            dwn_ref[...] = jnp.zeros_like(dwn_ref)

        @pl.when(j == 0)
        def _():
            acc_ref[...] = jnp.zeros_like(acc_ref)

        r = jnp.maximum(jnp.dot(hn_ref[...], wu_ref[...], preferred_element_type=F32), 0.0)
        act_ref[...] = (r * r).astype(BF16)
        d_act = _dg(dy_ref[...].astype(BF16), wd_ref[...], 1, 1, None)
        d_up = (d_act * (2.0 * r)).astype(BF16)
        dup_ref[...] = d_up
        acc_ref[...] += _dg(d_up, wu_ref[...], 1, 1, None)

        @pl.when(j == nf - 1)
        def _():
            _, pull = jax.vjp(_rms, h_ref[...], wn_ref[...])
            dh, dwn = pull(acc_ref[...])
            dh_ref[...] = dh + dy_ref[...]
            dwn_ref[...] += dwn

    tok = lambda w: pl.BlockSpec((tm, w), lambda i, j: (i, 0))
    ff = pl.BlockSpec((tm, tf), lambda i, j: (i, j))
    return pl.pallas_call(
        body, grid=(T // tm, nf),
        in_specs=[tok(D), _const_spec((1, D)), tok(D), pl.BlockSpec((D, tf), lambda i, j: (0, j)),
                  pl.BlockSpec((tf, D), lambda i, j: (j, 0)), tok(D)],
        out_specs=[tok(D), ff, ff, _const_spec((1, D))],
        out_shape=[_sds((T, D), F32), _sds((T, FF), BF16), _sds((T, FF), BF16), _sds((1, D), F32)],
        scratch_shapes=[pltpu.VMEM((tm, D), F32)],
        compiler_params=_params(("arbitrary", "arbitrary")), name="mlp_bwd",
    )(h2, w_mn, hn, w_up, w_down, dy)


def _rope_pad(a):
    z = jnp.zeros(a.shape[:-1] + (ROPE_HALF,), a.dtype)
    return jnp.concatenate([a[..., :ROPE_HALF], z, a[..., ROPE_HALF:], z], axis=-1)


def _rope_unpad(a):
    return jnp.concatenate([a[..., :ROPE_HALF], a[..., 2 * ROPE_HALF:3 * ROPE_HALF]], axis=-1)


_G0 = 2 * LORA + ROPE_DIM
_GZ0 = _G0 + GQKV_W
_GA0 = _GZ0 + GZ_W


def _widen_w_in(w):
    pad = jnp.zeros((w.shape[0], GAB_W - 2 * HEADS), w.dtype)
    return jnp.concatenate([w[:, :2 * LORA], _rope_pad(w[:, 2 * LORA:_G0]), w[:, _G0:_GA0], w[:, _GA0:], pad], axis=1)


def _narrow_w_in(w):
    return jnp.concatenate([w[:, :2 * LORA], _rope_unpad(w[:, 2 * LORA:LAT_W]), w[:, LAT_W:PROJ_SPLITS[2][1]],
                            w[:, PROJ_SPLITS[3][0]:PROJ_SPLITS[3][0] + 2 * HEADS]], axis=1)


def _stack_mla(w_uq, w_ukv):
    uq = w_uq.reshape(LORA, HEADS, QK_DIM)
    ukv = w_ukv.reshape(LORA, HEADS, 2 * HEAD_DIM)
    parts = [uq[:, :, :HEAD_DIM], _rope_pad(uq[:, :, HEAD_DIM:]), ukv[:, :, :HEAD_DIM], ukv[:, :, HEAD_DIM:]]
    return jnp.concatenate([p.transpose(1, 0, 2) for p in parts], axis=0)


def _unstack_mla(w):
    p = [w[i * HEADS:(i + 1) * HEADS].transpose(1, 0, 2) for i in range(4)]
    uq = jnp.concatenate([p[0], _rope_unpad(p[1])], axis=-1).reshape(LORA, HEADS * QK_DIM)
    ukv = jnp.concatenate([p[2], p[3]], axis=-1).reshape(LORA, HEADS * 2 * HEAD_DIM)
    return uq, ukv


def _rows8(rows):
    a = jnp.concatenate(rows, axis=0)
    return jnp.pad(a, ((0, 8 - a.shape[0]), (0, 0)))


def _qk_norm_rows(q_norm_w, k_norm_w):
    return _rows8([q_norm_w[:, :HEAD_DIM], _rope_pad(q_norm_w[:, HEAD_DIM:]), k_norm_w[:, :HEAD_DIM], _rope_pad(k_norm_w[:, HEAD_DIM:])])


def _rope_rows():
    inv_freq = ROPE_THETA ** (-jnp.arange(ROPE_HALF, dtype=F32) / ROPE_HALF)
    z = jnp.zeros((ROPE_HALF,), F32)
    freq = jnp.concatenate([inv_freq, z, inv_freq, z])
    sign = jnp.concatenate([-jnp.ones((ROPE_HALF,), F32), z, jnp.ones((ROPE_HALF,), F32), z])
    return _rows8([freq[None], sign[None]])


def _local_step(x, positions, w, target):
    B, S, D = x.shape
    T = B * S
    x2 = x.reshape(T, D)
    pos = positions.reshape(T, 1)
    w_in_p = _widen_w_in(w["w_in"][0]).astype(BF16)
    w_mla = _stack_mla(w["w_uq"][0], w["w_ukv"][0])
    ln_w = jnp.concatenate([w["q_lat_norm_w"], w["kv_lat_norm_w"]], axis=0)
    qk_nw = _qk_norm_rows(w["q_norm_w"], w["k_norm_w"])
    rope_rows = _rope_rows()
    conv_w = w["conv_w"][0]
    scal = _rows8([jnp.pad(w["a_log"], ((0, 0), (0, 128 - HEADS))), jnp.pad(w["dt_bias"], ((0, 0), (0, 128 - HEADS)))])
    mix_nw = _rows8([w["mla_out_norm_w"][0], w["gdn_norm_w"]])
    w_out = w["w_out"][0].astype(BF16)
    w_up = w["w_up"][0].astype(BF16)
    w_down = w["w_down"][0].astype(BF16)

    xn, lat, gqkv, gz, gab = _in_proj_fwd(x2, w["attn_norm_w"], w_in_p)
    q, k, v = _mla_pre_fwd(lat, pos, ln_w, w_mla, qk_nw, rope_rows)
    seq = lambda a: a.reshape(B, S, a.shape[-1])
    ao, lse = _attn_fwd(seq(q), seq(k), seq(v))
    gq, gk, gv = _gdn_pre_fwd(seq(gqkv), conv_w)
    go, states = _gdn_chunk_fwd(gq, gk, gv, seq(gab), scal)
    tok = lambda a: a.reshape(T, a.shape[-1])
    mix, h2 = _mix_fwd(tok(ao), tok(go), gz, mix_nw, w_out, x2)
    hn, dy, sq = _mlp_fwd(h2, w["mlp_norm_w"], w_up, w_down, target.reshape(T, D))
    loss = jnp.sum(sq[:, 0, 0]) * (0.5 / D)

    dh, act, d_up, d_mlp_nw = _mlp_bwd(h2, w["mlp_norm_w"], hn, w_up, w_down, dy)
    g_w_down = _wgrad(act, dy, "wgrad_down")
    g_w_up = _wgrad(hn, d_up, "wgrad_up")
    d_ao, d_go, d_gz, d_mix_nw = _mix_bwd(tok(ao), tok(go), gz, mix_nw, w_out, dh)
    g_w_out = _wgrad(mix, dh, "wgrad_out")
    d_gq, d_gk, d_gv, d_gab, d_scal = _gdn_chunk_bwd(gq, gk, gv, seq(gab), scal, states, seq(d_go))
    dxq, dxk, dxv, dcq, dck, dcv = _gdn_pre_bwd(seq(gqkv), conv_w, d_gq, d_gk, d_gv)
    d_gqkv = jnp.concatenate([dxq, dxk, dxv], axis=-1)
    g_conv = jnp.concatenate([dcq, dck, dcv], axis=-1)
    dq, dk, dv = _attn_bwd(seq(q), seq(k), seq(v), ao, lse, seq(d_ao))
    d_lat, d_ln, d_w_mla, d_qk_nw = _mla_pre_bwd(lat, pos, ln_w, w_mla, qk_nw, rope_rows, tok(dq), tok(dk), tok(dv))
    grad_x2, d_proj, d_attn_nw = _in_proj_bwd(d_lat, tok(d_gqkv), d_gz, tok(d_gab), w_in_p, x2, w["attn_norm_w"], dh)
    g_w_in = _narrow_w_in(_wgrad(xn, d_proj, "wgrad_in"))
    g_w_uq, g_w_ukv = _unstack_mla(d_w_mla)

    grads = {
        "attn_norm_w": d_attn_nw, "w_in": g_w_in[None], "q_lat_norm_w": d_ln[0:1], "w_uq": g_w_uq[None],
        "kv_lat_norm_w": d_ln[1:2], "w_ukv": g_w_ukv[None],
        "q_norm_w": jnp.concatenate([d_qk_nw[0:1], _rope_unpad(d_qk_nw[1:2])], axis=-1),
        "k_norm_w": jnp.concatenate([d_qk_nw[2:3], _rope_unpad(d_qk_nw[3:4])], axis=-1),
        "mla_out_norm_w": d_mix_nw[None, :HEADS], "conv_w": g_conv[None], "a_log": d_scal[0:1, :HEADS], "dt_bias": d_scal[1:2, :HEADS],
        "gdn_norm_w": d_mix_nw[HEADS:HEADS + 1], "w_out": g_w_out[None], "mlp_norm_w": d_mlp_nw, "w_up": g_w_up[None],
        "w_down": g_w_down[None],
    }
    return loss, grad_x2.reshape(B, S, D), grads


def _round_up(n, m):
    return -(-n // m) * m


def _pack_rows(arrays, rows):
    flat = jnp.concatenate([a.reshape(-1) for a in arrays])
    return jnp.pad(flat, (0, rows * 128 - flat.shape[0])).reshape(rows, 128)


def _unpack_rows(packed, shapes):
    flat, out, off = packed.reshape(-1), [], 0
    for s in shapes:
        n = math.prod(s)
        out.append(flat[off:off + n].reshape(s))
        off += n
    return out


def _shard_rows(shapes):
    return _round_up(-(-sum(math.prod(s) for s in shapes) // 128), 2 * FLAT_TILE)


def _full_from_shards(wall, names, shapes):
    flat, out, off = wall.reshape(4, -1), {}, 0
    for name, s in zip(names, shapes):
        n = math.prod(s)
        blk = flat[:, off:off + n].reshape((4,) + s[1:])
        off += n
        if name in ROW_SHARDED:
            out[name] = blk.reshape((1, 4 * s[1]) + s[2:])
        else:
            out[name] = blk.transpose(1, 0, 2).reshape(1, s[1], 4 * s[2])
    return out


def _shards_from_full(grads, names, shapes, rows):
    parts = []
    for name, s in zip(names, shapes):
        g = grads[name][0]
        if name in ROW_SHARDED:
            parts.append(g.reshape(4, -1))
        else:
            parts.append(g.reshape(s[1], 4, s[2]).transpose(1, 0, 2).reshape(4, -1))
    flat = jnp.concatenate(parts, axis=1)
    return jnp.pad(flat, ((0, 0), (0, rows * 128 - flat.shape[1]))).reshape(4, rows, 128)


def _pack_small(arrays):
    rows = [jnp.pad(a.reshape(-1), (0, _round_up(a.size, 128) - a.size)).reshape(-1, 128) for a in arrays]
    packed = jnp.concatenate(rows, axis=0)
    return jnp.pad(packed, ((0, _round_up(packed.shape[0], 8) - packed.shape[0]), (0, 0)))


def _unpack_small(packed, shapes):
    out, r = [], 0
    for s in shapes:
        n = math.prod(s)
        nr = _round_up(n, 128) // 128
        out.append(packed[r:r + nr].reshape(-1)[:n].reshape(s))
        r += nr
    return out


_ANY = pl.BlockSpec(memory_space=pl.ANY)
_OTHER_CHIPS = ((1, 0), (0, 1), (1, 1))


def _here():
    return lax.axis_index("x"), lax.axis_index("y"), lax.axis_index("c")


def _flip(v, bit):
    return 1 - v if bit else v


def _all_gather(shard):
    rows = shard.shape[0]
    half = rows // 2

    def body(src_ref, out_ref, send_sems, recv_sems, local_sem):
        x, y, c = _here()
        sibling = (x, y, 1 - c)
        mine = pl.ds(pl.multiple_of(c * half, 8), half)
        theirs = pl.ds(pl.multiple_of((1 - c) * half, 8), half)

        def remote(k, src, dst, to):
            return pltpu.make_async_remote_copy(src_ref=src, dst_ref=dst, send_sem=send_sems.at[k], recv_sem=recv_sems.at[k],
                                                device_id=to, device_id_type=MESH)

        own = pltpu.make_async_copy(src_ref, out_ref.at[2 * x + y], local_sem)
        own.start()
        chips = [(_flip(x, fx), _flip(y, fy)) for fx, fy in _OTHER_CHIPS]
        sends = [remote(j, src_ref.at[mine], out_ref.at[2 * x + y, mine], (cx, cy, c)) for j, (cx, cy) in enumerate(chips)]
        for cp in sends:
            cp.start()
        passed = []
        for j, (cx, cy) in enumerate(chips):
            landed = out_ref.at[2 * cx + cy, mine]
            remote(j, landed, landed, (cx, cy, c)).wait_recv()
            passed.append(remote(3 + j, landed, landed, sibling))
            passed[-1].start()
        for j, (cx, cy) in enumerate(chips):
            landed = out_ref.at[2 * cx + cy, theirs]
            remote(3 + j, landed, landed, sibling).wait_recv()
        for cp in sends + passed:
            cp.wait_send()
        own.wait()

    return pl.pallas_call(
        body, in_specs=[_ANY], out_specs=_ANY, out_shape=_sds((4, rows, 128), shard.dtype),
        scratch_shapes=[pltpu.SemaphoreType.DMA((6,)), pltpu.SemaphoreType.DMA((6,)), pltpu.SemaphoreType.DMA],
        name="all_gather_weights",
    )(shard)


def _scatter_partials(big, small):
    rows, srows = big.shape[1], small.shape[0]
    half = rows // 2

    def body(big_ref, small_ref, out_ref, sout_ref, send_sems, recv_sems, local_sems):
        x, y, c = _here()
        me = 4 * x + 2 * y + c
        own = pltpu.make_async_copy(big_ref.at[2 * x + y, pl.ds(pl.multiple_of(c * half, 8), half)], out_ref.at[me], local_sems.at[0])
        own_small = pltpu.make_async_copy(small_ref, sout_ref.at[me], local_sems.at[1])
        own.start()
        own_small.start()
        sends, peers = [], []
        for k in range(1, 8):
            px, py, pc = _flip(x, k & 4), _flip(y, k & 2), _flip(c, k & 1)
            peers.append(4 * px + 2 * py + pc)
            src = big_ref.at[2 * px + py, pl.ds(pl.multiple_of(pc * half, 8), half)]
            sends.append(pltpu.make_async_remote_copy(src_ref=src, dst_ref=out_ref.at[me], send_sem=send_sems.at[k - 1],
                                                      recv_sem=recv_sems.at[k - 1], device_id=(px, py, pc), device_id_type=MESH))
            sends.append(pltpu.make_async_remote_copy(src_ref=small_ref, dst_ref=sout_ref.at[me], send_sem=send_sems.at[6 + k],
                                                      recv_sem=recv_sems.at[6 + k], device_id=(px, py, pc), device_id_type=MESH))
        for cp in sends:
            cp.start()
        for k, peer in enumerate(peers):
            pltpu.make_async_remote_copy(src_ref=out_ref.at[peer], dst_ref=out_ref.at[peer], send_sem=send_sems.at[k],
                                         recv_sem=recv_sems.at[k], device_id=(x, y, c), device_id_type=MESH).wait_recv()
            pltpu.make_async_remote_copy(src_ref=sout_ref.at[peer], dst_ref=sout_ref.at[peer], send_sem=send_sems.at[7 + k],
                                         recv_sem=recv_sems.at[7 + k], device_id=(x, y, c), device_id_type=MESH).wait_recv()
        for cp in sends:
            cp.wait_send()
        own.wait()
        own_small.wait()

    return pl.pallas_call(
        body, in_specs=[_ANY, _ANY], out_specs=[_ANY, _ANY],
        out_shape=[_sds((8, half, 128), F32), _sds((8, srows, 128), F32)],
        scratch_shapes=[pltpu.SemaphoreType.DMA((14,)), pltpu.SemaphoreType.DMA((14,)), pltpu.SemaphoreType.DMA((2,))],
        name="scatter_partials",
    )(big, small)


def _exchange_halves(mine):
    half = mine.shape[0]

    def body(src_ref, out_ref, send_sem, recv_sem, local_sem):
        x, y, c = _here()
        own = pltpu.make_async_copy(src_ref, out_ref.at[c], local_sem)
        own.start()
        push = pltpu.make_async_remote_copy(src_ref=src_ref, dst_ref=out_ref.at[c], send_sem=send_sem, recv_sem=recv_sem,
                                            device_id=(x, y, 1 - c), device_id_type=MESH)
        push.start()
        pltpu.make_async_remote_copy(src_ref=src_ref, dst_ref=out_ref.at[1 - c], send_sem=send_sem, recv_sem=recv_sem,
                                     device_id=(x, y, 1 - c), device_id_type=MESH).wait_recv()
        push.wait_send()
        own.wait()

    return pl.pallas_call(
        body, in_specs=[_ANY], out_specs=_ANY, out_shape=_sds((2, half, 128), F32),
        scratch_shapes=[pltpu.SemaphoreType.DMA, pltpu.SemaphoreType.DMA, pltpu.SemaphoreType.DMA],
        name="exchange_halves",
    )(mine)


def _sum_slots(parts, name):
    _, rows, _ = parts.shape
    tr = min(FLAT_TILE, rows)

    def body(p_ref, o_ref):
        acc = p_ref[0]
        for d in range(1, 8):
            acc = acc + p_ref[d]
        o_ref[...] = acc

    return pl.pallas_call(
        body, grid=(rows // tr,), in_specs=[pl.BlockSpec((8, tr, 128), lambda i: (0, i, 0))],
        out_specs=pl.BlockSpec((tr, 128), lambda i: (i, 0)), out_shape=_sds((rows, 128), F32),
        compiler_params=_params(("parallel",)), name=name,
    )(parts)


def _adamw(w, g, m, v, name):
    rows = w.shape[0]
    tr = min(FLAT_TILE, rows)

    def body(w_ref, g_ref, m_ref, v_ref, d_ref, mo_ref, vo_ref):
        g = g_ref[...]
        m = ADAM_B1 * m_ref[...] + (1.0 - ADAM_B1) * g
        v = ADAM_B2 * v_ref[...] + (1.0 - ADAM_B2) * jnp.square(g)
        m_hat = m / (1.0 - ADAM_B1 ** ADAM_STEP)
        v_hat = v / (1.0 - ADAM_B2 ** ADAM_STEP)
        d_ref[...] = -ADAM_LR * (m_hat / (jnp.sqrt(v_hat) + ADAM_EPS) + ADAM_WD * w_ref[...])
        mo_ref[...] = m
        vo_ref[...] = v

    spec = pl.BlockSpec((tr, 128), lambda i: (i, 0))
    return pl.pallas_call(
        body, grid=(rows // tr,), in_specs=[spec] * 4, out_specs=[spec] * 3, out_shape=[_sds((rows, 128), F32)] * 3,
        compiler_params=_params(("parallel",)), name=name,
    )(w, g, m, v)


def kernel(x, positions, attn_norm_w, w_in, q_lat_norm_w, w_uq, kv_lat_norm_w, w_ukv, q_norm_w, k_norm_w, mla_out_norm_w, conv_w, a_log, dt_bias, gdn_norm_w, w_out, mlp_norm_w, w_up, w_down, loss_target, m_attn_norm_w, m_w_in, m_q_lat_norm_w, m_w_uq, m_kv_lat_norm_w, m_w_ukv, m_q_norm_w, m_k_norm_w, m_mla_out_norm_w, m_conv_w, m_a_log, m_dt_bias, m_gdn_norm_w, m_w_out, m_mlp_norm_w, m_w_up, m_w_down, v_attn_norm_w, v_w_in, v_q_lat_norm_w, v_w_uq, v_kv_lat_norm_w, v_w_ukv, v_q_norm_w, v_k_norm_w, v_mla_out_norm_w, v_conv_w, v_a_log, v_dt_bias, v_gdn_norm_w, v_w_out, v_mlp_norm_w, v_w_up, v_w_down):
    w = dict(zip(WEIGHTS, (attn_norm_w, w_in, q_lat_norm_w, w_uq, kv_lat_norm_w, w_ukv, q_norm_w, k_norm_w, mla_out_norm_w, conv_w,
                           a_log, dt_bias, gdn_norm_w, w_out, mlp_norm_w, w_up, w_down)))
    m = dict(zip(WEIGHTS, (m_attn_norm_w, m_w_in, m_q_lat_norm_w, m_w_uq, m_kv_lat_norm_w, m_w_ukv, m_q_norm_w, m_k_norm_w,
                           m_mla_out_norm_w, m_conv_w, m_a_log, m_dt_bias, m_gdn_norm_w, m_w_out, m_mlp_norm_w, m_w_up, m_w_down)))
    v = dict(zip(WEIGHTS, (v_attn_norm_w, v_w_in, v_q_lat_norm_w, v_w_uq, v_kv_lat_norm_w, v_w_ukv, v_q_norm_w, v_k_norm_w,
                           v_mla_out_norm_w, v_conv_w, v_a_log, v_dt_bias, v_gdn_norm_w, v_w_out, v_mlp_norm_w, v_w_up, v_w_down)))
    shard_shapes = [w[n].shape for n in SHARDED]
    small_shapes = [w[n].shape for n in SMALL]
    rows = _shard_rows(shard_shapes)

    w_packed = _pack_rows([w[n] for n in SHARDED], rows)
    full = _full_from_shards(_all_gather(w_packed), SHARDED, shard_shapes)
    loss, grad_x, partial = _local_step(x, positions, {**{n: w[n] for n in SMALL}, **full}, loss_target)

    parts, small_parts = _scatter_partials(_shards_from_full(partial, SHARDED, shard_shapes, rows),
                                           _pack_small([partial[n] for n in SMALL]))
    g_packed = _exchange_halves(_sum_slots(parts, "sum_shard_partials")).reshape(rows, 128)
    g_small = _sum_slots(small_parts, "sum_small_partials")

    d_packed, m_packed, v_packed = _adamw(w_packed, g_packed, _pack_rows([m[n] for n in SHARDED], rows),
                                          _pack_rows([v[n] for n in SHARDED], rows), "adamw_shard")
    d_small, m_small, v_small = _adamw(_pack_small([w[n] for n in SMALL]), g_small, _pack_small([m[n] for n in SMALL]),
                                       _pack_small([v[n] for n in SMALL]), "adamw_small")

    def by_name(packed, small_packed):
        out = dict(zip(SHARDED, _unpack_rows(packed, shard_shapes)))
        out.update(zip(SMALL, _unpack_small(small_packed, small_shapes)))
        return [out[n] for n in WEIGHTS]

    loss = lax.psum(loss, ("x", "y", "c"))
    return (loss, grad_x, *by_name(g_packed, g_small), *by_name(d_packed, d_small), *by_name(m_packed, m_small),
            *by_name(v_packed, v_small))
```

```python
import functools
import math

import jax
import jax.numpy as jnp
from jax import lax
from jax.experimental import pallas as pl
from jax.experimental.pallas import tpu as pltpu

F32 = jnp.float32
BF16 = jnp.bfloat16
MESH = pl.DeviceIdType.MESH

EPS = 1e-6
HEADS = 4
HEAD_DIM = 128
ROPE_DIM = 64
ROPE_HALF = 32
QK_DIM = 192
QK_PAD = 256
LORA = 256
CHUNK = 64
CONV_TAPS = 4
ROPE_THETA = 10000.0
ATTN_SCALE = QK_DIM ** -0.5

LAT_W = 640
GQKV_W = 3 * HEADS * HEAD_DIM
GZ_W = HEADS * HEAD_DIM
GAB_W = 128
PROJ_SPLITS = ((0, LAT_W), (LAT_W, LAT_W + GQKV_W), (LAT_W + GQKV_W, LAT_W + GQKV_W + GZ_W),
               (LAT_W + GQKV_W + GZ_W, LAT_W + GQKV_W + GZ_W + GAB_W))
PROJ_W = PROJ_SPLITS[-1][1]

ADAM_LR = 0.001
ADAM_B1 = 0.9
ADAM_B2 = 0.999
ADAM_EPS = 1e-08
ADAM_WD = 0.01
ADAM_STEP = 10

TOKEN_TILE = 512
FF_TILE = 512
ATTN_TILE = 256
VMEM_LIMIT = 48 * 1024 * 1024

SHARDED = ("w_in", "w_uq", "w_ukv", "conv_w", "w_out", "w_up", "w_down")
SMALL = ("attn_norm_w", "q_lat_norm_w", "kv_lat_norm_w", "q_norm_w", "k_norm_w", "mla_out_norm_w", "a_log", "dt_bias",
         "gdn_norm_w", "mlp_norm_w")
WEIGHTS = ("attn_norm_w", "w_in", "q_lat_norm_w", "w_uq", "kv_lat_norm_w", "w_ukv", "q_norm_w", "k_norm_w", "mla_out_norm_w",
           "conv_w", "a_log", "dt_bias", "gdn_norm_w", "w_out", "mlp_norm_w", "w_up", "w_down")


def _sds(shape, dtype):
    return jax.ShapeDtypeStruct(shape, dtype)


def _params(semantics):
    return pltpu.CompilerParams(dimension_semantics=semantics, vmem_limit_bytes=VMEM_LIMIT)


def _block(n):
    for b in (512, 256, 128):
        if n % b == 0:
            return b
    return n


def _dg(a, b, ca, cb, prec):
    return lax.dot_general(a, b, (((ca,), (cb,)), ((), ())), precision=prec, preferred_element_type=F32)


def _split_bf16(a):
    hi = a.astype(BF16)
    return hi, (a - hi.astype(F32)).astype(BF16)


def _dot_bf16(a, b, ca, cb):
    return _dg(a.astype(BF16), b.astype(BF16), ca, cb, None)


def _dot_bf16x3(a, b, ca, cb):
    a_hi, a_lo = _split_bf16(a)
    b_hi, b_lo = _split_bf16(b)
    return _dg(a_hi, b_hi, ca, cb, None) + (_dg(a_hi, b_lo, ca, cb, None) + _dg(a_lo, b_hi, ca, cb, None))


def _matmul_family(dot):
    def nn_raw(a, b):
        return dot(a, b, 1, 0)

    def nt_raw(a, b):
        return dot(a, b, 1, 1)

    def tn_raw(a, b):
        return dot(a, b, 0, 0)

    @jax.custom_vjp
    def nn(a, b):
        return nn_raw(a, b)

    nn.defvjp(lambda a, b: (nn_raw(a, b), (a, b)), lambda r, g: (nt_raw(g, r[1]), tn_raw(r[0], g)))

    @jax.custom_vjp
    def nt(a, b):
        return nt_raw(a, b)

    nt.defvjp(lambda a, b: (nt_raw(a, b), (a, b)), lambda r, g: (nn_raw(g, r[1]), tn_raw(g, r[0])))

    @jax.custom_vjp
    def tn(a, b):
        return tn_raw(a, b)

    tn.defvjp(lambda a, b: (tn_raw(a, b), (a, b)), lambda r, g: (nt_raw(r[1], g), nn_raw(r[0], g)))
    return nn, nt, tn


_bf_nn, _bf_nt, _bf_tn = _matmul_family(_dot_bf16)
_hi_nn, _hi_nt, _hi_tn = _matmul_family(_dot_bf16x3)


@jax.custom_vjp
def _swap_halves(t):
    return pltpu.roll(t, 64, 1)


_swap_halves.defvjp(lambda t: (pltpu.roll(t, 64, 1), None), lambda _, g: (pltpu.roll(g, 64, 1),))


@functools.partial(jax.custom_vjp, nondiff_argnums=(2,))
def _shift_rows(x, keep, s):
    return pltpu.roll(x, s, 0) * keep


def _shift_rows_fwd(x, keep, s):
    return pltpu.roll(x, s, 0) * keep, keep


def _shift_rows_bwd(s, keep, g):
    return pltpu.roll(g * keep, keep.shape[0] - s, 0), jnp.zeros_like(keep)


_shift_rows.defvjp(_shift_rows_fwd, _shift_rows_bwd)


def _sigmoid(x):
    return 0.5 * jnp.tanh(0.5 * x) + 0.5


def _softplus(x):
    return jnp.maximum(x, 0.0) + jnp.log(1.0 + jnp.exp(jnp.minimum(x, -x)))


def _silu(x):
    return x * _sigmoid(x)


def _rms(x, w, n=None):
    n = x.shape[-1] if n is None else n
    r = lax.rsqrt(jnp.sum(x * x, axis=-1, keepdims=True) * (1.0 / n) + EPS)
    return x * r * w


def _rope(t, cos_f, sin_f):
    return t * cos_f + _swap_halves(t) * sin_f


def _rope_tables(pos_col, freq_row, sign_row):
    ang = pos_col.astype(F32) * freq_row
    return jnp.cos(ang), jnp.sin(ang) * sign_row


def _onehot_row(lane):
    return (lax.broadcasted_iota(jnp.int32, (1, 128), 1) == lane).astype(F32)


def _row_spec(tm, w):
    return pl.BlockSpec((tm, w), lambda i: (i, 0))


def _const_spec(shape):
    return pl.BlockSpec(shape, lambda *_: (0,) * len(shape))


def _in_proj_fwd(x2, w_an, w_in_p):
    T, D = x2.shape
    tm = min(TOKEN_TILE, T)

    def body(x_ref, wn_ref, w_ref, xn_ref, lat_ref, gqkv_ref, gz_ref, gab_ref):
        x = x_ref[...]
        r = lax.rsqrt(jnp.mean(x * x, axis=-1, keepdims=True) + EPS)
        xn = (x * r * wn_ref[...]).astype(BF16)
        xn_ref[...] = xn
        for ref, (a, b) in zip((lat_ref, gqkv_ref, gz_ref, gab_ref), PROJ_SPLITS):
            ref[...] = jnp.dot(xn, w_ref[:, a:b], preferred_element_type=F32)

    widths = [b - a for a, b in PROJ_SPLITS]
    return pl.pallas_call(
        body, grid=(T // tm,),
        in_specs=[_row_spec(tm, D), _const_spec((1, D)), _const_spec((D, PROJ_W))],
        out_specs=[_row_spec(tm, D)] + [_row_spec(tm, w) for w in widths],
        out_shape=[_sds((T, D), BF16)] + [_sds((T, w), F32) for w in widths],
        compiler_params=_params(("parallel",)), name="in_proj_fwd",
    )(x2, w_an, w_in_p)


def _in_proj_bwd(d_lat, d_gqkv, d_gz, d_gab, w_in_p, x2, w_an, dh):
    T, D = x2.shape
    tm = min(TOKEN_TILE, T)

    def body(dl_ref, dq_ref, dz_ref, da_ref, w_ref, x_ref, wn_ref, dh_ref, dx_ref, dp_ref, dwn_ref):
        @pl.when(pl.program_id(0) == 0)
        def _():
            dwn_ref[...] = jnp.zeros_like(dwn_ref)

        dxn = jnp.zeros((tm, D), F32)
        for ref, (a, b) in zip((dl_ref, dq_ref, dz_ref, da_ref), PROJ_SPLITS):
            piece = ref[...].astype(BF16)
            dp_ref[:, a:b] = piece
            dxn += _dg(piece, w_ref[:, a:b], 1, 1, None)
        _, pull = jax.vjp(_rms, x_ref[...], wn_ref[...])
        dx, dwn = pull(dxn)
        dx_ref[...] = dx + dh_ref[...]
        dwn_ref[...] += dwn

    widths = [b - a for a, b in PROJ_SPLITS]
    return pl.pallas_call(
        body, grid=(T // tm,),
        in_specs=[_row_spec(tm, w) for w in widths] + [_const_spec((D, PROJ_W)), _row_spec(tm, D), _const_spec((1, D)),
                                                       _row_spec(tm, D)],
        out_specs=[_row_spec(tm, D), _row_spec(tm, PROJ_W), _const_spec((1, D))],
        out_shape=[_sds((T, D), F32), _sds((T, PROJ_W), BF16), _sds((1, D), F32)],
        compiler_params=_params(("arbitrary",)), name="in_proj_bwd",
    )(d_lat, d_gqkv, d_gz, d_gab, w_in_p, x2, w_an, dh)


def _wgrad(a, b, name, column_shards=1):
    T, k1 = a.shape
    k2 = b.shape[1]
    per_shard = k2 // column_shards
    b1, b2, tt = _block(k1), _block(per_shard), min(TOKEN_TILE, T)
    blocks_per_shard = per_shard // b2

    def body(a_ref, b_ref, o_ref):
        @pl.when(pl.program_id(2) == 0)
        def _():
            o_ref[...] = jnp.zeros_like(o_ref)

        o_ref[...] += _dg(a_ref[...].astype(BF16), b_ref[...].astype(BF16), 0, 0, None)

    if column_shards == 1:
        out_spec, out_shape = pl.BlockSpec((b1, b2), lambda i, j, t: (i, j)), _sds((k1, k2), F32)
    else:
        out_spec = pl.BlockSpec((None, b1, b2), lambda i, j, t: (j // blocks_per_shard, i, j % blocks_per_shard))
        out_shape = _sds((column_shards, k1, per_shard), F32)
    return pl.pallas_call(
        body, grid=(k1 // b1, k2 // b2, T // tt),
        in_specs=[pl.BlockSpec((tt, b1), lambda i, j, t: (t, i)), pl.BlockSpec((tt, b2), lambda i, j, t: (t, j))],
        out_specs=out_spec, out_shape=out_shape,
        compiler_params=_params(("parallel", "parallel", "arbitrary")), name=name,
    )(a, b)


def _mla_pre_fn(q_lat, kv_lat, kpe, ln_q, ln_kv, w_list, qn_n, qn_p, kn_n, kn_p, cos_f, sin_f):
    qn = _rms(q_lat, ln_q)
    kvn = _rms(kv_lat, ln_kv)
    kp = _rope(_rms(kpe, kn_p, ROPE_DIM), cos_f, sin_f)
    outs = []
    for h in range(HEADS):
        outs.append(_rms(_bf_nn(qn, w_list[h]), qn_n))
        outs.append(_rope(_rms(_bf_nn(qn, w_list[HEADS + h]), qn_p, ROPE_DIM), cos_f, sin_f))
        outs.append(_rms(_bf_nn(kvn, w_list[2 * HEADS + h]), kn_n))
        outs.append(_bf_nn(kvn, w_list[3 * HEADS + h]))
    return tuple(outs) + (kp,)


def _mla_pre_operands(lat_ref, pos_ref, ln_ref, w_ref, nw_ref, rope_ref):
    cos_f, sin_f = _rope_tables(pos_ref[...], rope_ref[0:1, :], rope_ref[1:2, :])
    diff = (lat_ref[:, 0:LORA], lat_ref[:, LORA:2 * LORA], lat_ref[:, 2 * LORA:LAT_W], ln_ref[0:1, :], ln_ref[1:2, :],
            [w_ref[i].astype(F32) for i in range(4 * HEADS)], nw_ref[0:1, :], nw_ref[1:2, :], nw_ref[2:3, :], nw_ref[3:4, :])
    return diff, cos_f, sin_f


def _mla_pre_fwd(lat, pos, ln_w, w_mla, nw, rope_rows):
    T = lat.shape[0]
    tm = min(TOKEN_TILE, T)

    def body(lat_ref, pos_ref, ln_ref, w_ref, nw_ref, rope_ref, q_ref, k_ref, v_ref):
        diff, cos_f, sin_f = _mla_pre_operands(lat_ref, pos_ref, ln_ref, w_ref, nw_ref, rope_ref)
        outs = _mla_pre_fn(*diff, cos_f, sin_f)
        kp = outs[-1].astype(BF16)
        for h in range(HEADS):
            q_n, q_p, k_n, v = outs[4 * h:4 * h + 4]
            q_ref[:, h * QK_PAD:h * QK_PAD + HEAD_DIM] = q_n.astype(BF16)
            q_ref[:, h * QK_PAD + HEAD_DIM:(h + 1) * QK_PAD] = q_p.astype(BF16)
            k_ref[:, h * QK_PAD:h * QK_PAD + HEAD_DIM] = k_n.astype(BF16)
            k_ref[:, h * QK_PAD + HEAD_DIM:(h + 1) * QK_PAD] = kp
            v_ref[:, h * HEAD_DIM:(h + 1) * HEAD_DIM] = v.astype(BF16)

    return pl.pallas_call(
        body, grid=(T // tm,),
        in_specs=[_row_spec(tm, LAT_W), _row_spec(tm, 1), _const_spec((2, LORA)), _const_spec((4 * HEADS, LORA, 128)),
                  _const_spec((8, 128)), _const_spec((8, 128))],
        out_specs=[_row_spec(tm, HEADS * QK_PAD), _row_spec(tm, HEADS * QK_PAD), _row_spec(tm, HEADS * HEAD_DIM)],
        out_shape=[_sds((T, HEADS * QK_PAD), BF16), _sds((T, HEADS * QK_PAD), BF16), _sds((T, HEADS * HEAD_DIM), BF16)],
        compiler_params=_params(("parallel",)), name="mla_pre_fwd",
    )(lat, pos, ln_w, w_mla, nw, rope_rows)


def _mla_pre_bwd(lat, pos, ln_w, w_mla, nw, rope_rows, dq, dk, dv):
    T = lat.shape[0]
    tm = min(TOKEN_TILE, T)

    def body(lat_ref, pos_ref, ln_ref, w_ref, nw_ref, rope_ref, dq_ref, dk_ref, dv_ref, dlat_ref, dln_ref, dw_ref, dnw_ref):
        @pl.when(pl.program_id(0) == 0)
        def _():
            dln_ref[...] = jnp.zeros_like(dln_ref)
            dw_ref[...] = jnp.zeros_like(dw_ref)
            dnw_ref[...] = jnp.zeros_like(dnw_ref)

        diff, cos_f, sin_f = _mla_pre_operands(lat_ref, pos_ref, ln_ref, w_ref, nw_ref, rope_ref)
        _, pull = jax.vjp(lambda *a: _mla_pre_fn(*a, cos_f, sin_f), *diff)
        cts = []
        d_kp = jnp.zeros((tm, 128), F32)
        for h in range(HEADS):
            cts.append(dq_ref[:, h * QK_PAD:h * QK_PAD + HEAD_DIM])
            cts.append(dq_ref[:, h * QK_PAD + HEAD_DIM:(h + 1) * QK_PAD])
            cts.append(dk_ref[:, h * QK_PAD:h * QK_PAD + HEAD_DIM])
            cts.append(dv_ref[:, h * HEAD_DIM:(h + 1) * HEAD_DIM])
            d_kp += dk_ref[:, h * QK_PAD + HEAD_DIM:(h + 1) * QK_PAD]
        d_ql, d_kvl, d_kpe, d_lnq, d_lnkv, d_w, d_qn_n, d_qn_p, d_kn_n, d_kn_p = pull(tuple(cts) + (d_kp,))
        dlat_ref[:, 0:LORA] = d_ql
        dlat_ref[:, LORA:2 * LORA] = d_kvl
        dlat_ref[:, 2 * LORA:LAT_W] = d_kpe
        dln_ref[0:1, :] += d_lnq
        dln_ref[1:2, :] += d_lnkv
        for i in range(4 * HEADS):
            dw_ref[i] += d_w[i]
        for i, d in enumerate((d_qn_n, d_qn_p, d_kn_n, d_kn_p)):
            dnw_ref[i:i + 1, :] += d

    return pl.pallas_call(
        body, grid=(T // tm,),
        in_specs=[_row_spec(tm, LAT_W), _row_spec(tm, 1), _const_spec((2, LORA)), _const_spec((4 * HEADS, LORA, 128)),
                  _const_spec((8, 128)), _const_spec((8, 128)),
                  _row_spec(tm, HEADS * QK_PAD), _row_spec(tm, HEADS * QK_PAD), _row_spec(tm, HEADS * HEAD_DIM)],
        out_specs=[_row_spec(tm, LAT_W), _const_spec((2, LORA)), _const_spec((4 * HEADS, LORA, 128)), _const_spec((8, 128))],
        out_shape=[_sds((T, LAT_W), F32), _sds((2, LORA), F32), _sds((4 * HEADS, LORA, 128), F32), _sds((8, 128), F32)],
        compiler_params=_params(("arbitrary",)), name="mla_pre_bwd",
    )(lat, pos, ln_w, w_mla, nw, rope_rows, dq, dk, dv)


def _causal_mask(i, j, tq, tk):
    row = i * tq + lax.broadcasted_iota(jnp.int32, (tq, tk), 0)
    col = j * tk + lax.broadcasted_iota(jnp.int32, (tq, tk), 1)
    return col <= row


def _attn_fwd(q, k, v):
    B, S, _ = q.shape
    t = min(ATTN_TILE, S)

    def body(q_ref, k_ref, v_ref, o_ref, lse_ref):
        i = pl.program_id(2)
        qb = q_ref[0]

        def step(j, carry):
            m, l, acc = carry
            rows = pl.ds(pl.multiple_of(j * t, t), t)
            s = _dg(qb, k_ref[0, rows, :], 1, 1, None) * ATTN_SCALE
            s = jnp.where(_causal_mask(i, j, t, t), s, -1e30)
            m_new = jnp.maximum(m, jnp.max(s, axis=-1, keepdims=True))
            p = jnp.exp(s - m_new)
            alpha = jnp.exp(m - m_new)
            l = alpha * l + jnp.sum(p, axis=-1, keepdims=True)
            acc = alpha * acc + jnp.dot(p.astype(BF16), v_ref[0, rows, :], preferred_element_type=F32)
            return m_new, l, acc

        init = (jnp.full((t, 1), -1e30, F32), jnp.zeros((t, 1), F32), jnp.zeros((t, HEAD_DIM), F32))
        m, l, acc = lax.fori_loop(0, i + 1, step, init)
        o_ref[0] = acc / l
        lse_ref[0, 0] = m + jnp.log(l)

    return pl.pallas_call(
        body, grid=(B, HEADS, S // t),
        in_specs=[pl.BlockSpec((1, t, QK_PAD), lambda b, h, i: (b, i, h)),
                  pl.BlockSpec((1, S, QK_PAD), lambda b, h, i: (b, 0, h)),
                  pl.BlockSpec((1, S, HEAD_DIM), lambda b, h, i: (b, 0, h))],
        out_specs=[pl.BlockSpec((1, t, HEAD_DIM), lambda b, h, i: (b, i, h)),
                   pl.BlockSpec((1, 1, t, 1), lambda b, h, i: (b, h, i, 0))],
        out_shape=[_sds((B, S, HEADS * HEAD_DIM), F32), _sds((B, HEADS, S, 1), F32)],
        compiler_params=_params(("parallel", "parallel", "parallel")), name="attn_fwd",
    )(q, k, v)


def _attn_bwd(q, k, v, o, lse, do):
    B, S, _ = q.shape
    t = min(ATTN_TILE, S)
    nq = S // t

    def body(q_ref, k_ref, v_ref, o_ref, lse_ref, do_ref, dq_ref, dk_ref, dv_ref, dsum_ref):
        j = pl.program_id(2)

        @pl.when(j == 0)
        def _():
            dq_ref[...] = jnp.zeros_like(dq_ref)
            dsum_ref[...] = jnp.sum(do_ref[0] * o_ref[0], axis=-1, keepdims=True)

        kb = k_ref[0]
        vb = v_ref[0]

        def step(i, carry):
            dk, dv = carry
            rows = pl.ds(pl.multiple_of(i * t, t), t)
            qb = q_ref[0, rows, :]
            dob = do_ref[0, rows, :].astype(BF16)
            s = _dg(qb, kb, 1, 1, None) * ATTN_SCALE
            p = jnp.where(_causal_mask(i, j, t, t), jnp.exp(s - lse_ref[0, 0, rows, :]), 0.0)
            pb = p.astype(BF16)
            dv = dv + _dg(pb, dob, 0, 0, None)
            dp = _dg(dob, vb, 1, 1, None)
            ds = (p * (dp - dsum_ref[rows, :]) * ATTN_SCALE).astype(BF16)
            dq_ref[0, rows, :] += jnp.dot(ds, kb, preferred_element_type=F32)
            dk = dk + _dg(ds, qb, 0, 0, None)
            return dk, dv

        dk, dv = lax.fori_loop(j, nq, step, (jnp.zeros((t, QK_PAD), F32), jnp.zeros((t, HEAD_DIM), F32)))
        dk_ref[0] = dk
        dv_ref[0] = dv

    return pl.pallas_call(
        body, grid=(B, HEADS, nq),
        in_specs=[pl.BlockSpec((1, S, QK_PAD), lambda b, h, j: (b, 0, h)),
                  pl.BlockSpec((1, t, QK_PAD), lambda b, h, j: (b, j, h)),
                  pl.BlockSpec((1, t, HEAD_DIM), lambda b, h, j: (b, j, h)),
                  pl.BlockSpec((1, S, HEAD_DIM), lambda b, h, j: (b, 0, h)),
                  pl.BlockSpec((1, 1, S, 1), lambda b, h, j: (b, h, 0, 0)),
                  pl.BlockSpec((1, S, HEAD_DIM), lambda b, h, j: (b, 0, h))],
        out_specs=[pl.BlockSpec((1, S, QK_PAD), lambda b, h, j: (b, 0, h)),
                   pl.BlockSpec((1, t, QK_PAD), lambda b, h, j: (b, j, h)),
                   pl.BlockSpec((1, t, HEAD_DIM), lambda b, h, j: (b, j, h))],
        out_shape=[_sds((B, S, HEADS * QK_PAD), F32), _sds((B, S, HEADS * QK_PAD), F32), _sds((B, S, HEADS * HEAD_DIM), F32)],
        scratch_shapes=[pltpu.VMEM((S, 1), F32)],
        compiler_params=_params(("parallel", "parallel", "arbitrary")), name="attn_bwd",
    )(q, k, v, o, lse, do)


def _gdn_pre_fn(xq, xk, xv, wq, wk, wv, keeps):
    def conv_silu(x, w):
        acc = x * w[3]
        for s in (1, 2, 3):
            acc = acc + _shift_rows(x, keeps[s - 1], s) * w[3 - s]
        return _silu(acc)

    def l2(x):
        return x * lax.rsqrt(jnp.sum(x * x, axis=-1, keepdims=True) + EPS)

    return l2(conv_silu(xq, wq)) * (HEAD_DIM ** -0.5), l2(conv_silu(xk, wk)), conv_silu(xv, wv)


def _gdn_pre_specs(S):
    x_specs = [pl.BlockSpec((1, S, HEAD_DIM), lambda h, b, g=g: (b, 0, g * HEADS + h)) for g in range(3)]
    w_specs = [pl.BlockSpec((CONV_TAPS, HEAD_DIM), lambda h, b, g=g: (0, g * HEADS + h)) for g in range(3)]
    out_spec = pl.BlockSpec((1, S, HEAD_DIM), lambda h, b: (b, 0, h))
    return x_specs, w_specs, out_spec


def _row_keeps(S):
    t = lax.broadcasted_iota(jnp.int32, (S, HEAD_DIM), 0)
    return [(t >= s).astype(F32) for s in (1, 2, 3)]


def _gdn_pre_fwd(gqkv, conv_w):
    B, S, _ = gqkv.shape
    x_specs, w_specs, out_spec = _gdn_pre_specs(S)

    def body(xq_ref, xk_ref, xv_ref, wq_ref, wk_ref, wv_ref, q_ref, k_ref, v_ref):
        taps = [[w[i:i + 1, :] for i in range(CONV_TAPS)] for w in (wq_ref, wk_ref, wv_ref)]
        q, k, v = _gdn_pre_fn(xq_ref[0], xk_ref[0], xv_ref[0], *taps, _row_keeps(S))
        q_ref[0], k_ref[0], v_ref[0] = q, k, v

    return pl.pallas_call(
        body, grid=(HEADS, B), in_specs=x_specs + w_specs, out_specs=[out_spec] * 3,
        out_shape=[_sds((B, S, HEADS * HEAD_DIM), F32)] * 3,
        compiler_params=_params(("parallel", "parallel")), name="gdn_pre_fwd",
    )(gqkv, gqkv, gqkv, conv_w, conv_w, conv_w)


def _gdn_pre_bwd(gqkv, conv_w, dq, dk, dv):
    B, S, _ = gqkv.shape
    x_specs, w_specs, out_spec = _gdn_pre_specs(S)
    dw_spec = pl.BlockSpec((CONV_TAPS, HEAD_DIM), lambda h, b: (0, h))

    def body(xq_ref, xk_ref, xv_ref, wq_ref, wk_ref, wv_ref, dq_ref, dk_ref, dv_ref,
             dxq_ref, dxk_ref, dxv_ref, dwq_ref, dwk_ref, dwv_ref):
        @pl.when(pl.program_id(1) == 0)
        def _():
            for r in (dwq_ref, dwk_ref, dwv_ref):
                r[...] = jnp.zeros_like(r)

        taps = [[w[i:i + 1, :] for i in range(CONV_TAPS)] for w in (wq_ref, wk_ref, wv_ref)]
        keeps = _row_keeps(S)
        _, pull = jax.vjp(lambda *a: _gdn_pre_fn(*a, keeps), xq_ref[0], xk_ref[0], xv_ref[0], *taps)
        dxq, dxk, dxv, dwq, dwk, dwv = pull((dq_ref[0], dk_ref[0], dv_ref[0]))
        dxq_ref[0], dxk_ref[0], dxv_ref[0] = dxq, dxk, dxv
        for ref, dw in ((dwq_ref, dwq), (dwk_ref, dwk), (dwv_ref, dwv)):
            for i in range(CONV_TAPS):
                ref[i:i + 1, :] += dw[i]

    hw = HEADS * HEAD_DIM
    return pl.pallas_call(
        body, grid=(HEADS, B), in_specs=x_specs + w_specs + [out_spec] * 3,
        out_specs=[out_spec] * 3 + [dw_spec] * 3,
        out_shape=[_sds((B, S, hw), F32)] * 3 + [_sds((CONV_TAPS, hw), F32)] * 3,
        compiler_params=_params(("parallel", "arbitrary")), name="gdn_pre_bwd",
    )(gqkv, gqkv, gqkv, conv_w, conv_w, conv_w, dq, dk, dv)


def _chunk_masks():
    i = lax.broadcasted_iota(jnp.int32, (CHUNK, CHUNK), 0)
    j = lax.broadcasted_iota(jnp.int32, (CHUNK, CHUNK), 1)
    return {"le": (j <= i).astype(F32), "gt_t": (j > i).astype(F32), "strict": (j < i).astype(F32)}


def _gdn_chunk_fn(h, masks):
    pick_a, pick_b = _onehot_row(h), _onehot_row(HEADS + h)
    lower, upper_t, strict = masks["le"], masks["gt_t"], masks["strict"]
    ones_row = jnp.ones((1, HEAD_DIM), F32)

    def f(q, k, v, gab, a_row, dt_row, state):
        ga = jnp.sum(gab * pick_a, axis=1, keepdims=True)
        gb = jnp.sum(gab * pick_b, axis=1, keepdims=True)
        a_log = jnp.sum(a_row * pick_a, axis=1, keepdims=True)
        dt_bias = jnp.sum(dt_row * pick_a, axis=1, keepdims=True)
        beta = _sigmoid(gb)
        g = -jnp.exp(a_log) * _softplus(ga + dt_bias)
        g_wide = g * ones_row
        cum = _hi_nn(lower, g_wide)
        rest = _hi_nn(upper_t, g_wide)
        total = jnp.sum(g_wide, axis=0, keepdims=True)
        diff = _hi_nn(lower, g * strict)
        decay = lower * jnp.exp(diff)
        e_cum = jnp.exp(cum)
        lmat = strict * (beta * _bf_nt(k, k) * decay)
        u = v * beta
        w = k * (beta * e_cum)
        u = u - _hi_nn(lmat, u)
        w = w - _hi_nn(lmat, w)
        power = lmat
        for _ in range(5):
            power = _hi_nn(power, power)
            u = u + _hi_nn(power, u)
            w = w + _hi_nn(power, w)
        attn = _bf_nt(q, k) * decay
        v_new = u - _bf_nn(w, state)
        o = _bf_nn(q * e_cum, state) + _bf_nn(attn, v_new)
        new_state = state * jnp.exp(total) + _bf_tn(k * jnp.exp(rest), v_new)
        return o, new_state

    return f


def _gdn_chunk_fwd(q, k, v, gab, scal, shards):
    B, S, W = q.shape
    N = S // CHUNK
    ns = len(shards)

    def body(*refs):
        q_ref, k_ref, v_ref, gab_ref, sc_ref = refs[:5]
        src_refs = refs[5:5 + ns]
        o_ref, st_ref = refs[5 + ns:7 + ns]
        dst_refs = refs[7 + ns:7 + 2 * ns]
        state_ref, send_sems, recv_sems, local_sems = refs[7 + 2 * ns:]
        b, n = pl.program_id(0), pl.program_id(1)

        @pl.when((b == 0) & (n == 0))
        def _():
            for start in _gather_copies(src_refs, dst_refs, send_sems, recv_sems, local_sems)[0]:
                start()

        @pl.when(n == 0)
        def _():
            state_ref[...] = jnp.zeros_like(state_ref)

        masks = _chunk_masks()
        for h in range(HEADS):
            lanes = slice(h * HEAD_DIM, (h + 1) * HEAD_DIM)
            state = state_ref[h]
            st_ref[0, 0, h] = state
            o, new_state = _gdn_chunk_fn(h, masks)(q_ref[0, :, lanes], k_ref[0, :, lanes], v_ref[0, :, lanes], gab_ref[0],
                                                   sc_ref[0:1, :], sc_ref[1:2, :], state)
            o_ref[0, :, lanes] = o
            state_ref[h] = new_state

        @pl.when((b == B - 1) & (n == N - 1))
        def _():
            for wait in _gather_copies(src_refs, dst_refs, send_sems, recv_sems, local_sems)[1]:
                wait()

    seq = pl.BlockSpec((1, CHUNK, W), lambda b, n: (b, n, 0))
    return pl.pallas_call(
        body, grid=(B, N),
        in_specs=[seq, seq, seq, pl.BlockSpec((1, CHUNK, GAB_W), lambda b, n: (b, n, 0)), _const_spec((8, 128))] + [_ANY] * ns,
        out_specs=[seq, pl.BlockSpec((1, 1, HEADS, HEAD_DIM, HEAD_DIM), lambda b, n: (b, n, 0, 0, 0))] + [_ANY] * ns,
        out_shape=[_sds((B, S, W), F32), _sds((B, N, HEADS, HEAD_DIM, HEAD_DIM), F32)] + [_sds((4,) + s.shape, s.dtype) for s in shards],
        scratch_shapes=[pltpu.VMEM((HEADS, HEAD_DIM, HEAD_DIM), F32), pltpu.SemaphoreType.DMA((3 * ns,)),
                        pltpu.SemaphoreType.DMA((3 * ns,)), pltpu.SemaphoreType.DMA((ns,))],
        compiler_params=_params(("arbitrary", "arbitrary")), name="gdn_chunk_fwd",
    )(q, k, v, gab, scal, *shards)


def _gdn_chunk_bwd(q, k, v, gab, scal, states, do, partials):
    B, S, W = q.shape
    N = S // CHUNK
    ns = len(partials)

    def body(*refs):
        q_ref, k_ref, v_ref, gab_ref, sc_ref, st_ref, do_ref = refs[:7]
        src_refs = refs[7:7 + ns]
        dq_ref, dk_ref, dv_ref, dgab_ref, dsc_ref = refs[7 + ns:12 + ns]
        dst_refs = refs[12 + ns:12 + 2 * ns]
        dstate_ref, send_sems, recv_sems, local_sems = refs[12 + 2 * ns:]
        b, n = pl.program_id(0), pl.program_id(1)

        @pl.when((b == 0) & (n == 0))
        def _():
            for start in _scatter_copies(src_refs, dst_refs, send_sems, recv_sems, local_sems)[0]:
                start()

        @pl.when(n == 0)
        def _():
            dstate_ref[...] = jnp.zeros_like(dstate_ref)

        @pl.when((b == 0) & (n == 0))
        def _():
            dsc_ref[...] = jnp.zeros_like(dsc_ref)

        masks = _chunk_masks()
        d_gab = jnp.zeros((CHUNK, GAB_W), F32)
        d_a = jnp.zeros((1, 128), F32)
        d_dt = jnp.zeros((1, 128), F32)
        for h in range(HEADS):
            lanes = slice(h * HEAD_DIM, (h + 1) * HEAD_DIM)
            _, pull = jax.vjp(_gdn_chunk_fn(h, masks), q_ref[0, :, lanes], k_ref[0, :, lanes], v_ref[0, :, lanes], gab_ref[0],
                              sc_ref[0:1, :], sc_ref[1:2, :], st_ref[0, 0, h])
            dq, dk, dv, dg, da, ddt, dstate = pull((do_ref[0, :, lanes], dstate_ref[h]))
            dq_ref[0, :, lanes] = dq
            dk_ref[0, :, lanes] = dk
            dv_ref[0, :, lanes] = dv
            dstate_ref[h] = dstate
            d_gab, d_a, d_dt = d_gab + dg, d_a + da, d_dt + ddt
        dgab_ref[0] = d_gab
        dsc_ref[0:1, :] += d_a
        dsc_ref[1:2, :] += d_dt

        @pl.when((b == B - 1) & (n == N - 1))
        def _():
            for wait in _scatter_copies(src_refs, dst_refs, send_sems, recv_sems, local_sems)[1]:
                wait()

    seq = pl.BlockSpec((1, CHUNK, W), lambda b, n: (b, N - 1 - n, 0))
    gab_spec = pl.BlockSpec((1, CHUNK, GAB_W), lambda b, n: (b, N - 1 - n, 0))
    return pl.pallas_call(
        body, grid=(B, N),
        in_specs=[seq, seq, seq, gab_spec, _const_spec((8, 128)),
                  pl.BlockSpec((1, 1, HEADS, HEAD_DIM, HEAD_DIM), lambda b, n: (b, N - 1 - n, 0, 0, 0)), seq] + [_ANY] * ns,
        out_specs=[seq, seq, seq, gab_spec, _const_spec((8, 128))] + [_ANY] * ns,
        out_shape=[_sds((B, S, W), F32)] * 3 + [_sds((B, S, GAB_W), F32), _sds((8, 128), F32)] + [_scattered_shape(p) for p in partials],
        scratch_shapes=[pltpu.VMEM((HEADS, HEAD_DIM, HEAD_DIM), F32), pltpu.SemaphoreType.DMA((7 * ns,)),
                        pltpu.SemaphoreType.DMA((7 * ns,)), pltpu.SemaphoreType.DMA((ns,))],
        compiler_params=_params(("arbitrary", "arbitrary")), name="gdn_chunk_bwd",
    )(q, k, v, gab, scal, states, do, *partials)


def _mix_fn(ao, go, gz, w_mla, w_gdn):
    return tuple(_rms(ao[h], w_mla[h]) for h in range(HEADS)) + tuple(_rms(go[h], w_gdn) * _silu(gz[h]) for h in range(HEADS))


def _mix_operands(ao_ref, go_ref, gz_ref, nw_ref):
    blocks = lambda ref: [ref[:, h * HEAD_DIM:(h + 1) * HEAD_DIM] for h in range(HEADS)]
    return blocks(ao_ref), blocks(go_ref), blocks(gz_ref), [nw_ref[h:h + 1, :] for h in range(HEADS)], nw_ref[HEADS:HEADS + 1, :]


def _mix_fwd(ao, go, gz, nw, w_out, x2):
    T, D = x2.shape
    tm = min(TOKEN_TILE, T)
    MW = 2 * HEADS * HEAD_DIM

    def body(ao_ref, go_ref, gz_ref, nw_ref, w_ref, x_ref, mix_ref, h_ref):
        outs = _mix_fn(*_mix_operands(ao_ref, go_ref, gz_ref, nw_ref))
        for i, piece in enumerate(outs):
            mix_ref[:, i * HEAD_DIM:(i + 1) * HEAD_DIM] = piece.astype(BF16)
        h_ref[...] = x_ref[...] + jnp.dot(mix_ref[...], w_ref[...], preferred_element_type=F32)

    half = HEADS * HEAD_DIM
    return pl.pallas_call(
        body, grid=(T // tm,),
        in_specs=[_row_spec(tm, half), _row_spec(tm, half), _row_spec(tm, half), _const_spec((8, 128)), _const_spec((MW, D)),
                  _row_spec(tm, D)],
        out_specs=[_row_spec(tm, MW), _row_spec(tm, D)],
        out_shape=[_sds((T, MW), BF16), _sds((T, D), F32)],
        compiler_params=_params(("parallel",)), name="mix_fwd",
    )(ao, go, gz, nw, w_out, x2)


def _mix_bwd(ao, go, gz, nw, w_out, dh):
    T, D = dh.shape
    tm = min(TOKEN_TILE, T)
    MW = 2 * HEADS * HEAD_DIM
    half = HEADS * HEAD_DIM

    def body(ao_ref, go_ref, gz_ref, nw_ref, w_ref, dh_ref, dao_ref, dgo_ref, dgz_ref, dnw_ref):
        @pl.when(pl.program_id(0) == 0)
        def _():
            dnw_ref[...] = jnp.zeros_like(dnw_ref)

        d_mix = _dg(dh_ref[...].astype(BF16), w_ref[...], 1, 1, None)
        cts = tuple(d_mix[:, i * HEAD_DIM:(i + 1) * HEAD_DIM] for i in range(2 * HEADS))
        _, pull = jax.vjp(_mix_fn, *_mix_operands(ao_ref, go_ref, gz_ref, nw_ref))
        d_ao, d_go, d_gz, d_wm, d_wg = pull(cts)
        for h in range(HEADS):
            lanes = slice(h * HEAD_DIM, (h + 1) * HEAD_DIM)
            dao_ref[:, lanes] = d_ao[h]
            dgo_ref[:, lanes] = d_go[h]
            dgz_ref[:, lanes] = d_gz[h]
            dnw_ref[h:h + 1, :] += d_wm[h]
        dnw_ref[HEADS:HEADS + 1, :] += d_wg

    return pl.pallas_call(
        body, grid=(T // tm,),
        in_specs=[_row_spec(tm, half), _row_spec(tm, half), _row_spec(tm, half), _const_spec((8, 128)), _const_spec((MW, D)),
                  _row_spec(tm, D)],
        out_specs=[_row_spec(tm, half)] * 3 + [_const_spec((8, 128))],
        out_shape=[_sds((T, half), F32)] * 3 + [_sds((8, 128), F32)],
        compiler_params=_params(("arbitrary",)), name="mix_bwd",
    )(ao, go, gz, nw, w_out, dh)


def _up_spec(w_up, tf):
    per_shard = w_up.shape[2] // tf
    return pl.BlockSpec((None, w_up.shape[1], tf), lambda i, j: (j // per_shard, 0, j % per_shard))


def _mlp_fwd(h2, w_mn, w_up, w_down, target):
    T, D = h2.shape
    FF = w_down.shape[0]
    tm, tf = min(TOKEN_TILE, T), min(FF_TILE, w_up.shape[2])
    nf = FF // tf

    def body(h_ref, wn_ref, wu_ref, wd_ref, t_ref, hn_ref, dy_ref, sq_ref, acc_ref):
        j = pl.program_id(1)

        @pl.when(j == 0)
        def _():
            hn_ref[...] = _rms(h_ref[...], wn_ref[...]).astype(BF16)
            acc_ref[...] = jnp.zeros_like(acc_ref)

        up = jnp.dot(hn_ref[...], wu_ref[...], preferred_element_type=F32)
        act = jnp.square(jnp.maximum(up, 0.0)).astype(BF16)
        acc_ref[...] += jnp.dot(act, wd_ref[...], preferred_element_type=F32)

        @pl.when(j == nf - 1)
        def _():
            err = h_ref[...] + acc_ref[...] - t_ref[...]
            dy_ref[...] = err * (1.0 / D)
            sq_ref[...] = jnp.zeros_like(sq_ref) + jnp.sum(err * err)

    tok = lambda w: pl.BlockSpec((tm, w), lambda i, j: (i, 0))
    return pl.pallas_call(
        body, grid=(T // tm, nf),
        in_specs=[tok(D), _const_spec((1, D)), _up_spec(w_up, tf), pl.BlockSpec((tf, D), lambda i, j: (j, 0)), tok(D)],
        out_specs=[tok(D), tok(D), pl.BlockSpec((1, 8, 128), lambda i, j: (i, 0, 0))],
        out_shape=[_sds((T, D), BF16), _sds((T, D), F32), _sds((T // tm, 8, 128), F32)],
        scratch_shapes=[pltpu.VMEM((tm, D), F32)],
        compiler_params=_params(("parallel", "arbitrary")), name="mlp_fwd",
    )(h2, w_mn, w_up, w_down, target)


def _mlp_bwd(h2, w_mn, hn, w_up, w_down, dy):
    T, D = h2.shape
    FF = w_down.shape[0]
    tm, tf = min(TOKEN_TILE, T), min(FF_TILE, w_up.shape[2])
    nf = FF // tf

    def body(h_ref, wn_ref, hn_ref, wu_ref, wd_ref, dy_ref, dh_ref, act_ref, dup_ref, dwn_ref, acc_ref):
        i, j = pl.program_id(0), pl.program_id(1)

        @pl.when((i == 0) & (j == 0))
        def _():
            dwn_ref[...] = jnp.zeros_like(dwn_ref)

        @pl.when(j == 0)
        def _():
            acc_ref[...] = jnp.zeros_like(acc_ref)

        r = jnp.maximum(jnp.dot(hn_ref[...], wu_ref[...], preferred_element_type=F32), 0.0)
        act_ref[...] = (r * r).astype(BF16)
        d_act = _dg(dy_ref[...].astype(BF16), wd_ref[...], 1, 1, None)
        d_up = (d_act * (2.0 * r)).astype(BF16)
        dup_ref[...] = d_up
        acc_ref[...] += _dg(d_up, wu_ref[...], 1, 1, None)

        @pl.when(j == nf - 1)
        def _():
            _, pull = jax.vjp(_rms, h_ref[...], wn_ref[...])
            dh, dwn = pull(acc_ref[...])
            dh_ref[...] = dh + dy_ref[...]
            dwn_ref[...] += dwn

    tok = lambda w: pl.BlockSpec((tm, w), lambda i, j: (i, 0))
    ff = pl.BlockSpec((tm, tf), lambda i, j: (i, j))
    return pl.pallas_call(
        body, grid=(T // tm, nf),
        in_specs=[tok(D), _const_spec((1, D)), tok(D), _up_spec(w_up, tf), pl.BlockSpec((tf, D), lambda i, j: (j, 0)), tok(D)],
        out_specs=[tok(D), ff, ff, _const_spec((1, D))],
        out_shape=[_sds((T, D), F32), _sds((T, FF), BF16), _sds((T, FF), BF16), _sds((1, D), F32)],
        scratch_shapes=[pltpu.VMEM((tm, D), F32)],
        compiler_params=_params(("arbitrary", "arbitrary")), name="mlp_bwd",
    )(h2, w_mn, hn, w_up, w_down, dy)


def _rope_pad(a):
    z = jnp.zeros(a.shape[:-1] + (ROPE_HALF,), a.dtype)
    return jnp.concatenate([a[..., :ROPE_HALF], z, a[..., ROPE_HALF:], z], axis=-1)


def _rope_unpad(a):
    return jnp.concatenate([a[..., :ROPE_HALF], a[..., 2 * ROPE_HALF:3 * ROPE_HALF]], axis=-1)


_G0 = 2 * LORA + ROPE_DIM
_GZ0 = _G0 + GQKV_W
_GA0 = _GZ0 + GZ_W


def _widen_w_in(w):
    pad = jnp.zeros((w.shape[0], GAB_W - 2 * HEADS), w.dtype)
    return jnp.concatenate([w[:, :2 * LORA], _rope_pad(w[:, 2 * LORA:_G0]), w[:, _G0:_GA0], w[:, _GA0:], pad], axis=1)


def _narrow_w_in(w):
    return jnp.concatenate([w[:, :2 * LORA], _rope_unpad(w[:, 2 * LORA:LAT_W]), w[:, LAT_W:PROJ_SPLITS[2][1]],
                            w[:, PROJ_SPLITS[3][0]:PROJ_SPLITS[3][0] + 2 * HEADS]], axis=1)


def _stack_mla(w_uq, w_ukv):
    uq = w_uq.reshape(LORA, HEADS, QK_DIM)
    ukv = w_ukv.reshape(LORA, HEADS, 2 * HEAD_DIM)
    parts = [uq[:, :, :HEAD_DIM], _rope_pad(uq[:, :, HEAD_DIM:]), ukv[:, :, :HEAD_DIM], ukv[:, :, HEAD_DIM:]]
    return jnp.concatenate([p.transpose(1, 0, 2) for p in parts], axis=0)


def _unstack_mla(w):
    p = [w[i * HEADS:(i + 1) * HEADS].transpose(1, 0, 2) for i in range(4)]
    uq = jnp.concatenate([p[0], _rope_unpad(p[1])], axis=-1).reshape(LORA, HEADS * QK_DIM)
    ukv = jnp.concatenate([p[2], p[3]], axis=-1).reshape(LORA, HEADS * 2 * HEAD_DIM)
    return uq, ukv


def _rows8(rows):
    a = jnp.concatenate(rows, axis=0)
    return jnp.pad(a, ((0, 8 - a.shape[0]), (0, 0)))


def _qk_norm_rows(q_norm_w, k_norm_w):
    return _rows8([q_norm_w[:, :HEAD_DIM], _rope_pad(q_norm_w[:, HEAD_DIM:]), k_norm_w[:, :HEAD_DIM], _rope_pad(k_norm_w[:, HEAD_DIM:])])


def _rope_rows():
    inv_freq = ROPE_THETA ** (-jnp.arange(ROPE_HALF, dtype=F32) / ROPE_HALF)
    z = jnp.zeros((ROPE_HALF,), F32)
    freq = jnp.concatenate([inv_freq, z, inv_freq, z])
    sign = jnp.concatenate([-jnp.ones((ROPE_HALF,), F32), z, jnp.ones((ROPE_HALF,), F32), z])
    return _rows8([freq[None], sign[None]])


def _round_up(n, m):
    return -(-n // m) * m


def _column_shards(a):
    return a.reshape(a.shape[0], 4, a.shape[1] // 4).transpose(1, 0, 2)


def _from_column_shards(a):
    return a.transpose(1, 0, 2).reshape(a.shape[1], 4 * a.shape[2])


def _pack_small(arrays):
    rows = [jnp.pad(a.reshape(-1), (0, _round_up(a.size, 128) - a.size)).reshape(-1, 128) for a in arrays]
    packed = jnp.concatenate(rows, axis=0)
    return jnp.pad(packed, ((0, _round_up(packed.shape[0], 8) - packed.shape[0]), (0, 0)))


def _unpack_small(packed, shapes):
    out, r = [], 0
    for s in shapes:
        n = math.prod(s)
        nr = _round_up(n, 128) // 128
        out.append(packed[r:r + nr].reshape(-1)[:n].reshape(s))
        r += nr
    return out


_ANY = pl.BlockSpec(memory_space=pl.ANY)
_OTHER_CHIPS = ((1, 0), (0, 1), (1, 1))


def _here():
    return lax.axis_index("x"), lax.axis_index("y"), lax.axis_index("c")


def _flip(v, bit):
    return 1 - v if bit else v


def _remote(src, dst, send_sems, recv_sems, k, to):
    return pltpu.make_async_remote_copy(src_ref=src, dst_ref=dst, send_sem=send_sems.at[k], recv_sem=recv_sems.at[k],
                                        device_id=to, device_id_type=MESH)


def _gather_copies(srcs, dsts, send_sems, recv_sems, local_sems):
    x, y, c = _here()
    slot = 2 * x + y
    starts, waits = [], []
    for i, (src, dst) in enumerate(zip(srcs, dsts)):
        own = pltpu.make_async_copy(src, dst.at[slot], local_sems.at[i])
        starts.append(own.start)
        waits.append(own.wait)
        for j, (fx, fy) in enumerate(_OTHER_CHIPS):
            cx, cy = _flip(x, fx), _flip(y, fy)
            push = _remote(src, dst.at[slot], send_sems, recv_sems, 3 * i + j, (cx, cy, c))
            landed = dst.at[2 * cx + cy]
            starts.append(push.start)
            waits += [_remote(landed, landed, send_sems, recv_sems, 3 * i + j, (cx, cy, c)).wait_recv, push.wait_send]
    return starts, waits


def _gather_scratch(n):
    return [pltpu.SemaphoreType.DMA((3 * n,)), pltpu.SemaphoreType.DMA((3 * n,)), pltpu.SemaphoreType.DMA((n,))]


def _all_gather(shards, name):
    ns = len(shards)

    def body(*refs):
        starts, waits = _gather_copies(refs[:ns], refs[ns:2 * ns], *refs[2 * ns:])
        for call in starts + waits:
            call()

    return pl.pallas_call(
        body, in_specs=[_ANY] * ns, out_specs=[_ANY] * ns, out_shape=[_sds((4,) + s.shape, s.dtype) for s in shards],
        scratch_shapes=_gather_scratch(ns), name=name,
    )(*shards)


def _scattered_shape(p):
    return _sds((8, p.shape[1] // 2) + p.shape[2:], p.dtype)


def _scatter_copies(srcs, dsts, send_sems, recv_sems, local_sems, whole=0):
    x, y, c = _here()
    me = 4 * x + 2 * y + c
    starts, waits = [], []
    for i, (src, dst) in enumerate(zip(srcs, dsts)):
        def piece(px, py, pc, src=src, entire=i >= len(srcs) - whole):
            if entire:
                return src
            half = src.shape[1] // 2
            return src.at[2 * px + py, pl.ds(pl.multiple_of(pc * half, 8), half)]

        own = pltpu.make_async_copy(piece(x, y, c), dst.at[me], local_sems.at[i])
        starts.append(own.start)
        waits.append(own.wait)
        for k in range(1, 8):
            px, py, pc = _flip(x, k & 4), _flip(y, k & 2), _flip(c, k & 1)
            push = _remote(piece(px, py, pc), dst.at[me], send_sems, recv_sems, 7 * i + k - 1, (px, py, pc))
            landed = dst.at[4 * px + 2 * py + pc]
            starts.append(push.start)
            waits += [_remote(landed, landed, send_sems, recv_sems, 7 * i + k - 1, (px, py, pc)).wait_recv, push.wait_send]
    return starts, waits


def _scatter_scratch(n):
    return [pltpu.SemaphoreType.DMA((7 * n,)), pltpu.SemaphoreType.DMA((7 * n,)), pltpu.SemaphoreType.DMA((n,))]


def _scatter(partials, wholes, name):
    ns = len(partials) + len(wholes)

    def body(*refs):
        starts, waits = _scatter_copies(refs[:ns], refs[ns:2 * ns], *refs[2 * ns:], whole=len(wholes))
        for call in starts + waits:
            call()

    return pl.pallas_call(
        body, in_specs=[_ANY] * ns, out_specs=[_ANY] * ns,
        out_shape=[_scattered_shape(p) for p in partials] + [_sds((8,) + s.shape, s.dtype) for s in wholes],
        scratch_shapes=_scatter_scratch(ns), name=name,
    )(*partials, *wholes)


def _exchange_halves(halves):
    ns = len(halves)

    def body(*refs):
        srcs, dsts = refs[:ns], refs[ns:2 * ns]
        send_sems, recv_sems, local_sems = refs[2 * ns:]
        x, y, c = _here()
        sibling = (x, y, 1 - c)
        starts, waits = [], []
        for i, (src, dst) in enumerate(zip(srcs, dsts)):
            own = pltpu.make_async_copy(src, dst.at[c], local_sems.at[i])
            push = _remote(src, dst.at[c], send_sems, recv_sems, i, sibling)
            landed = dst.at[1 - c]
            starts += [own.start, push.start]
            waits += [_remote(landed, landed, send_sems, recv_sems, i, sibling).wait_recv, push.wait_send, own.wait]
        for call in starts + waits:
            call()

    return pl.pallas_call(
        body, in_specs=[_ANY] * ns, out_specs=[_ANY] * ns, out_shape=[_sds((2,) + h.shape, h.dtype) for h in halves],
        scratch_shapes=[pltpu.SemaphoreType.DMA((ns,)), pltpu.SemaphoreType.DMA((ns,)), pltpu.SemaphoreType.DMA((ns,))],
        name="exchange_halves",
    )(*halves)


def _row_tile(rows, row_bytes, budget):
    tr = rows
    while tr * row_bytes > budget and tr % 16 == 0:
        tr //= 2
    return tr


def _sum_slots(parts, name):
    _, rows, cols = parts.shape
    tr = _row_tile(rows, 8 * cols * 4, 2 * 1024 * 1024)

    def body(p_ref, o_ref):
        acc = p_ref[0]
        for d in range(1, 8):
            acc = acc + p_ref[d]
        o_ref[...] = acc

    return pl.pallas_call(
        body, grid=(rows // tr,), in_specs=[pl.BlockSpec((8, tr, cols), lambda i: (0, i, 0))],
        out_specs=pl.BlockSpec((tr, cols), lambda i: (i, 0)), out_shape=_sds((rows, cols), F32),
        compiler_params=_params(("parallel",)), name=name,
    )(parts)


def _adamw(w, g, m, v, name):
    rows, cols = w.shape
    tr = _row_tile(rows, 7 * cols * 4, 4 * 1024 * 1024)

    def body(w_ref, g_ref, m_ref, v_ref, d_ref, mo_ref, vo_ref):
        g = g_ref[...]
        m = ADAM_B1 * m_ref[...] + (1.0 - ADAM_B1) * g
        v = ADAM_B2 * v_ref[...] + (1.0 - ADAM_B2) * jnp.square(g)
        m_hat = m / (1.0 - ADAM_B1 ** ADAM_STEP)
        v_hat = v / (1.0 - ADAM_B2 ** ADAM_STEP)
        d_ref[...] = -ADAM_LR * (m_hat / (jnp.sqrt(v_hat) + ADAM_EPS) + ADAM_WD * w_ref[...])
        mo_ref[...] = m
        vo_ref[...] = v

    spec = pl.BlockSpec((tr, cols), lambda i: (i, 0))
    return pl.pallas_call(
        body, grid=(rows // tr,), in_specs=[spec] * 4, out_specs=[spec] * 3, out_shape=[_sds((rows, cols), F32)] * 3,
        compiler_params=_params(("parallel",)), name=name,
    )(w, g, m, v)


def kernel(x, positions, attn_norm_w, w_in, q_lat_norm_w, w_uq, kv_lat_norm_w, w_ukv, q_norm_w, k_norm_w, mla_out_norm_w, conv_w, a_log, dt_bias, gdn_norm_w, w_out, mlp_norm_w, w_up, w_down, loss_target, m_attn_norm_w, m_w_in, m_q_lat_norm_w, m_w_uq, m_kv_lat_norm_w, m_w_ukv, m_q_norm_w, m_k_norm_w, m_mla_out_norm_w, m_conv_w, m_a_log, m_dt_bias, m_gdn_norm_w, m_w_out, m_mlp_norm_w, m_w_up, m_w_down, v_attn_norm_w, v_w_in, v_q_lat_norm_w, v_w_uq, v_kv_lat_norm_w, v_w_ukv, v_q_norm_w, v_k_norm_w, v_mla_out_norm_w, v_conv_w, v_a_log, v_dt_bias, v_gdn_norm_w, v_w_out, v_mlp_norm_w, v_w_up, v_w_down):
    w = dict(zip(WEIGHTS, (attn_norm_w, w_in, q_lat_norm_w, w_uq, kv_lat_norm_w, w_ukv, q_norm_w, k_norm_w, mla_out_norm_w, conv_w,
                           a_log, dt_bias, gdn_norm_w, w_out, mlp_norm_w, w_up, w_down)))
    m = dict(zip(WEIGHTS, (m_attn_norm_w, m_w_in, m_q_lat_norm_w, m_w_uq, m_kv_lat_norm_w, m_w_ukv, m_q_norm_w, m_k_norm_w,
                           m_mla_out_norm_w, m_conv_w, m_a_log, m_dt_bias, m_gdn_norm_w, m_w_out, m_mlp_norm_w, m_w_up, m_w_down)))
    v = dict(zip(WEIGHTS, (v_attn_norm_w, v_w_in, v_q_lat_norm_w, v_w_uq, v_kv_lat_norm_w, v_w_ukv, v_q_norm_w, v_k_norm_w,
                           v_mla_out_norm_w, v_conv_w, v_a_log, v_dt_bias, v_gdn_norm_w, v_w_out, v_mlp_norm_w, v_w_up, v_w_down)))
    B, S, D = x.shape
    T = B * S
    x2, pos, target = x.reshape(T, D), positions.reshape(T, 1), loss_target.reshape(T, D)
    seq = lambda a: a.reshape(B, S, a.shape[-1])
    tok = lambda a: a.reshape(T, a.shape[-1])
    local = {n: w[n][0] for n in SHARDED}

    g_in, g_uq, g_ukv, g_conv = _all_gather([local["w_in"].astype(BF16), local["w_uq"].astype(BF16), local["w_ukv"].astype(BF16),
                                             local["conv_w"]], "gather_first_weights")
    w_in_p = _widen_w_in(_from_column_shards(g_in))
    w_mla = _stack_mla(_from_column_shards(g_uq), _from_column_shards(g_ukv))
    conv_full = _from_column_shards(g_conv)
    ln_w = jnp.concatenate([q_lat_norm_w, kv_lat_norm_w], axis=0)
    qk_nw = _qk_norm_rows(q_norm_w, k_norm_w)
    rope_rows = _rope_rows()
    scal = _rows8([jnp.pad(a_log, ((0, 0), (0, 128 - HEADS))), jnp.pad(dt_bias, ((0, 0), (0, 128 - HEADS)))])
    mix_nw = _rows8([mla_out_norm_w[0], gdn_norm_w])

    xn, lat, gqkv, gz, gab = _in_proj_fwd(x2, attn_norm_w, w_in_p)
    q, k, v_att = _mla_pre_fwd(lat, pos, ln_w, w_mla, qk_nw, rope_rows)
    ao, lse = _attn_fwd(seq(q), seq(k), seq(v_att))
    gq, gk, gv = _gdn_pre_fwd(seq(gqkv), conv_full)
    go, states, g_out, w_up_b, g_down = _gdn_chunk_fwd(
        gq, gk, gv, seq(gab), scal, [local["w_out"].astype(BF16), local["w_up"].astype(BF16), local["w_down"].astype(BF16)])
    w_out_b = g_out.reshape(-1, D)
    w_down_b = g_down.reshape(-1, D)
    mix, h2 = _mix_fwd(tok(ao), tok(go), gz, mix_nw, w_out_b, x2)
    hn, dy, sq = _mlp_fwd(h2, mlp_norm_w, w_up_b, w_down_b, target)
    loss = lax.psum(jnp.sum(sq[:, 0, 0]) * (0.5 / D), ("x", "y", "c"))

    dh, act, d_up, d_mlp_nw = _mlp_bwd(h2, mlp_norm_w, hn, w_up_b, w_down_b, dy)
    p_down = _wgrad(act, dy, "wgrad_down").reshape(4, -1, D)
    p_up = _wgrad(hn, d_up, "wgrad_up", column_shards=4)
    d_ao, d_go, d_gz, d_mix_nw = _mix_bwd(tok(ao), tok(go), gz, mix_nw, w_out_b, dh)
    p_out = _wgrad(mix, dh, "wgrad_out").reshape(4, -1, D)
    d_gq, d_gk, d_gv, d_gab, d_scal, s_up, s_down, s_out = _gdn_chunk_bwd(gq, gk, gv, seq(gab), scal, states, seq(d_go),
                                                                          [p_up, p_down, p_out])
    dxq, dxk, dxv, dcq, dck, dcv = _gdn_pre_bwd(seq(gqkv), conv_full, d_gq, d_gk, d_gv)
    dq, dk, dv = _attn_bwd(seq(q), seq(k), seq(v_att), ao, lse, seq(d_ao))
    d_lat, d_ln, d_w_mla, d_qk_nw = _mla_pre_bwd(lat, pos, ln_w, w_mla, qk_nw, rope_rows, tok(dq), tok(dk), tok(dv))
    d_gqkv = jnp.concatenate([dxq, dxk, dxv], axis=-1)
    grad_x2, d_proj, d_attn_nw = _in_proj_bwd(d_lat, tok(d_gqkv), d_gz, tok(d_gab), w_in_p, x2, attn_norm_w, dh)
    p_in = _column_shards(_narrow_w_in(_wgrad(xn, d_proj, "wgrad_in")))
    p_uq, p_ukv = (_column_shards(a) for a in _unstack_mla(d_w_mla))
    small_partial = {
        "attn_norm_w": d_attn_nw, "q_lat_norm_w": d_ln[0:1], "kv_lat_norm_w": d_ln[1:2],
        "q_norm_w": jnp.concatenate([d_qk_nw[0:1], _rope_unpad(d_qk_nw[1:2])], axis=-1),
        "k_norm_w": jnp.concatenate([d_qk_nw[2:3], _rope_unpad(d_qk_nw[3:4])], axis=-1),
        "mla_out_norm_w": d_mix_nw[None, :HEADS], "a_log": d_scal[0:1, :HEADS], "dt_bias": d_scal[1:2, :HEADS],
        "gdn_norm_w": d_mix_nw[HEADS:HEADS + 1], "mlp_norm_w": d_mlp_nw,
    }
    conv_partial = jnp.concatenate([dcq, dck, dcv], axis=-1)
    s_in, s_uq, s_ukv, s_small = _scatter([p_in, p_uq, p_ukv], [_pack_small([small_partial[n] for n in SMALL] + [conv_partial])],
                                          "scatter_last_partials")

    names = ("w_in", "w_uq", "w_ukv", "w_up", "w_down", "w_out")
    halves = [_sum_slots(s, "sum_" + n) for n, s in zip(names, (s_in, s_uq, s_ukv, s_up, s_down, s_out))]
    grad = {n: g.reshape(local[n].shape) for n, g in zip(names, _exchange_halves(halves))}
    small_shapes = [w[n].shape for n in SMALL]
    *g_small, g_conv_all = _unpack_small(_sum_slots(s_small, "sum_small"), small_shapes + [conv_partial.shape])
    grad.update(zip(SMALL, g_small))
    conv_cols = local["conv_w"].shape[1]
    grad["conv_w"] = lax.dynamic_slice_in_dim(g_conv_all, (2 * lax.axis_index("x") + lax.axis_index("y")) * conv_cols, conv_cols, axis=1)

    delta, new_m, new_v = {}, {}, {}
    for n in names:
        delta[n], new_m[n], new_v[n] = _adamw(local[n], grad[n], m[n][0], v[n][0], "adamw_" + n)
    packed_names = SMALL + ("conv_w",)
    packed_shapes = small_shapes + [local["conv_w"].shape]
    take = lambda d: _pack_small([d[n][0] if n == "conv_w" and d[n].ndim == 3 else d[n] for n in packed_names])
    outs = _adamw(take(w), take(grad), take(m), take(v), "adamw_small")
    for d, packed in zip((delta, new_m, new_v), outs):
        d.update(zip(packed_names, _unpack_small(packed, packed_shapes)))

    def in_order(d):
        return [d[n].reshape(w[n].shape) for n in WEIGHTS]

    return (loss, grad_x2.reshape(B, S, D), *in_order(grad), *in_order(delta), *in_order(new_m), *in_order(new_v))
```

```python
import functools
import math

import jax
import jax.numpy as jnp
from jax import lax
from jax.experimental import pallas as pl
from jax.experimental.pallas import tpu as pltpu

F32 = jnp.float32
BF16 = jnp.bfloat16
MESH = pl.DeviceIdType.MESH

EPS = 1e-6
HEADS = 4
HEAD_DIM = 128
ROPE_DIM = 64
ROPE_HALF = 32
QK_DIM = 192
QK_PAD = 256
LORA = 256
CHUNK = 64
CONV_TAPS = 4
ROPE_THETA = 10000.0
ATTN_SCALE = QK_DIM ** -0.5

LAT_W = 640
GQKV_W = 3 * HEADS * HEAD_DIM
GZ_W = HEADS * HEAD_DIM
GAB_W = 128
PROJ_SPLITS = ((0, LAT_W), (LAT_W, LAT_W + GQKV_W), (LAT_W + GQKV_W, LAT_W + GQKV_W + GZ_W),
               (LAT_W + GQKV_W + GZ_W, LAT_W + GQKV_W + GZ_W + GAB_W))
PROJ_W = PROJ_SPLITS[-1][1]

ADAM_LR = 0.001
ADAM_B1 = 0.9
ADAM_B2 = 0.999
ADAM_EPS = 1e-08
ADAM_WD = 0.01
ADAM_STEP = 10

TOKEN_TILE = 512
FF_TILE = 512
ATTN_TILE = 512
WGRAD_OUT_BYTES = 8 * 1024 * 1024
VMEM_LIMIT = 48 * 1024 * 1024

SHARDED = ("w_in", "w_uq", "w_ukv", "conv_w", "w_out", "w_up", "w_down")
SMALL = ("attn_norm_w", "q_lat_norm_w", "kv_lat_norm_w", "q_norm_w", "k_norm_w", "mla_out_norm_w", "a_log", "dt_bias",
         "gdn_norm_w", "mlp_norm_w")
WEIGHTS = ("attn_norm_w", "w_in", "q_lat_norm_w", "w_uq", "kv_lat_norm_w", "w_ukv", "q_norm_w", "k_norm_w", "mla_out_norm_w",
           "conv_w", "a_log", "dt_bias", "gdn_norm_w", "w_out", "mlp_norm_w", "w_up", "w_down")


def _sds(shape, dtype):
    return jax.ShapeDtypeStruct(shape, dtype)


def _params(semantics):
    return pltpu.CompilerParams(dimension_semantics=semantics, vmem_limit_bytes=VMEM_LIMIT)


def _block(n):
    for b in (512, 256, 128):
        if n % b == 0:
            return b
    return n


def _dg(a, b, ca, cb, prec):
    return lax.dot_general(a, b, (((ca,), (cb,)), ((), ())), precision=prec, preferred_element_type=F32)


def _split_bf16(a):
    hi = a.astype(BF16)
    return hi, (a - hi.astype(F32)).astype(BF16)


def _dot_bf16(a, b, ca, cb):
    return _dg(a.astype(BF16), b.astype(BF16), ca, cb, None)


def _dot_bf16x3(a, b, ca, cb):
    a_hi, a_lo = _split_bf16(a)
    b_hi, b_lo = _split_bf16(b)
    return _dg(a_hi, b_hi, ca, cb, None) + (_dg(a_hi, b_lo, ca, cb, None) + _dg(a_lo, b_hi, ca, cb, None))


def _matmul_family(dot):
    def nn_raw(a, b):
        return dot(a, b, 1, 0)

    def nt_raw(a, b):
        return dot(a, b, 1, 1)

    def tn_raw(a, b):
        return dot(a, b, 0, 0)

    @jax.custom_vjp
    def nn(a, b):
        return nn_raw(a, b)

    nn.defvjp(lambda a, b: (nn_raw(a, b), (a, b)), lambda r, g: (nt_raw(g, r[1]), tn_raw(r[0], g)))

    @jax.custom_vjp
    def nt(a, b):
        return nt_raw(a, b)

    nt.defvjp(lambda a, b: (nt_raw(a, b), (a, b)), lambda r, g: (nn_raw(g, r[1]), tn_raw(g, r[0])))

    @jax.custom_vjp
    def tn(a, b):
        return tn_raw(a, b)

    tn.defvjp(lambda a, b: (tn_raw(a, b), (a, b)), lambda r, g: (nt_raw(r[1], g), nn_raw(r[0], g)))
    return nn, nt, tn


_bf_nn, _bf_nt, _bf_tn = _matmul_family(_dot_bf16)
_hi_nn, _hi_nt, _hi_tn = _matmul_family(_dot_bf16x3)


@jax.custom_vjp
def _swap_halves(t):
    return pltpu.roll(t, 64, 1)


_swap_halves.defvjp(lambda t: (pltpu.roll(t, 64, 1), None), lambda _, g: (pltpu.roll(g, 64, 1),))


@functools.partial(jax.custom_vjp, nondiff_argnums=(2,))
def _shift_rows(x, keep, s):
    return pltpu.roll(x, s, 0) * keep


def _shift_rows_fwd(x, keep, s):
    return pltpu.roll(x, s, 0) * keep, keep


def _shift_rows_bwd(s, keep, g):
    return pltpu.roll(g * keep, keep.shape[0] - s, 0), jnp.zeros_like(keep)


_shift_rows.defvjp(_shift_rows_fwd, _shift_rows_bwd)


def _sigmoid(x):
    return 0.5 * jnp.tanh(0.5 * x) + 0.5


def _softplus(x):
    return jnp.maximum(x, 0.0) + jnp.log(1.0 + jnp.exp(jnp.minimum(x, -x)))


def _silu(x):
    return x * _sigmoid(x)


def _rms(x, w, n=None):
    n = x.shape[-1] if n is None else n
    r = lax.rsqrt(jnp.sum(x * x, axis=-1, keepdims=True) * (1.0 / n) + EPS)
    return x * r * w


def _rope(t, cos_f, sin_f):
    return t * cos_f + _swap_halves(t) * sin_f


def _rope_tables(pos_col, freq_row, sign_row):
    ang = pos_col.astype(F32) * freq_row
    return jnp.cos(ang), jnp.sin(ang) * sign_row


def _onehot_row(lane):
    return (lax.broadcasted_iota(jnp.int32, (1, 128), 1) == lane).astype(F32)


def _row_spec(tm, w):
    return pl.BlockSpec((tm, w), lambda i: (i, 0))


def _const_spec(shape):
    return pl.BlockSpec(shape, lambda *_: (0,) * len(shape))


def _in_proj_fwd(x2, w_an, w_in_p):
    T, D = x2.shape
    tm = min(TOKEN_TILE, T)

    def body(x_ref, wn_ref, w_ref, xn_ref, lat_ref, gqkv_ref, gz_ref, gab_ref):
        x = x_ref[...]
        r = lax.rsqrt(jnp.mean(x * x, axis=-1, keepdims=True) + EPS)
        xn = (x * r * wn_ref[...]).astype(BF16)
        xn_ref[...] = xn
        for ref, (a, b) in zip((lat_ref, gqkv_ref, gz_ref, gab_ref), PROJ_SPLITS):
            ref[...] = jnp.dot(xn, w_ref[:, a:b], preferred_element_type=F32)

    widths = [b - a for a, b in PROJ_SPLITS]
    return pl.pallas_call(
        body, grid=(T // tm,),
        in_specs=[_row_spec(tm, D), _const_spec((1, D)), _const_spec((D, PROJ_W))],
        out_specs=[_row_spec(tm, D)] + [_row_spec(tm, w) for w in widths],
        out_shape=[_sds((T, D), BF16)] + [_sds((T, w), F32) for w in widths],
        compiler_params=_params(("parallel",)), name="in_proj_fwd",
    )(x2, w_an, w_in_p)


def _in_proj_bwd(d_lat, d_gqkv, d_gz, d_gab, w_in_p, x2, w_an, dh):
    T, D = x2.shape
    tm = min(TOKEN_TILE, T)

    def body(dl_ref, dq_ref, dz_ref, da_ref, w_ref, x_ref, wn_ref, dh_ref, dx_ref, dp_ref, dwn_ref):
        @pl.when(pl.program_id(0) == 0)
        def _():
            dwn_ref[...] = jnp.zeros_like(dwn_ref)

        dxn = jnp.zeros((tm, D), F32)
        for ref, (a, b) in zip((dl_ref, dq_ref, dz_ref, da_ref), PROJ_SPLITS):
            piece = ref[...].astype(BF16)
            dp_ref[:, a:b] = piece
            dxn += _dg(piece, w_ref[:, a:b], 1, 1, None)
        _, pull = jax.vjp(_rms, x_ref[...], wn_ref[...])
        dx, dwn = pull(dxn)
        dx_ref[...] = dx + dh_ref[...]
        dwn_ref[...] += dwn

    widths = [b - a for a, b in PROJ_SPLITS]
    return pl.pallas_call(
        body, grid=(T // tm,),
        in_specs=[_row_spec(tm, w) for w in widths] + [_const_spec((D, PROJ_W)), _row_spec(tm, D), _const_spec((1, D)),
                                                       _row_spec(tm, D)],
        out_specs=[_row_spec(tm, D), _row_spec(tm, PROJ_W), _const_spec((1, D))],
        out_shape=[_sds((T, D), F32), _sds((T, PROJ_W), BF16), _sds((1, D), F32)],
        compiler_params=_params(("arbitrary",)), name="in_proj_bwd",
    )(d_lat, d_gqkv, d_gz, d_gab, w_in_p, x2, w_an, dh)


def _wgrad(a, b, name, column_shards=1):
    T, k1 = a.shape
    k2 = b.shape[1]
    per_shard = k2 // column_shards
    tt = min(TOKEN_TILE, T)
    b1 = k1
    while b1 * k2 * 4 > WGRAD_OUT_BYTES and b1 % 256 == 0:
        b1 //= 2
    step = _block(per_shard)

    def body(a_ref, b_ref, o_ref):
        @pl.when(pl.program_id(1) == 0)
        def _():
            o_ref[...] = jnp.zeros_like(o_ref)

        a_t = a_ref[...].astype(BF16).T
        for c0 in range(0, k2, step):
            part = jnp.dot(a_t, b_ref[:, c0:c0 + step].astype(BF16), preferred_element_type=F32)
            if column_shards == 1:
                o_ref[:, c0:c0 + step] += part
            else:
                o_ref[c0 // per_shard, :, c0 % per_shard:c0 % per_shard + step] += part

    if column_shards == 1:
        out_spec, out_shape = pl.BlockSpec((b1, k2), lambda i, t: (i, 0)), _sds((k1, k2), F32)
    else:
        out_spec = pl.BlockSpec((column_shards, b1, per_shard), lambda i, t: (0, i, 0))
        out_shape = _sds((column_shards, k1, per_shard), F32)
    return pl.pallas_call(
        body, grid=(k1 // b1, T // tt),
        in_specs=[pl.BlockSpec((tt, b1), lambda i, t: (t, i)), pl.BlockSpec((tt, k2), lambda i, t: (t, 0))],
        out_specs=out_spec, out_shape=out_shape,
        compiler_params=_params(("parallel", "arbitrary")), name=name,
    )(a, b)


def _mla_pre_fn(q_lat, kv_lat, kpe, ln_q, ln_kv, w_list, qn_n, qn_p, kn_n, kn_p, cos_f, sin_f):
    qn = _rms(q_lat, ln_q)
    kvn = _rms(kv_lat, ln_kv)
    kp = _rope(_rms(kpe, kn_p, ROPE_DIM), cos_f, sin_f)
    outs = []
    for h in range(HEADS):
        outs.append(_rms(_bf_nn(qn, w_list[h]), qn_n))
        outs.append(_rope(_rms(_bf_nn(qn, w_list[HEADS + h]), qn_p, ROPE_DIM), cos_f, sin_f))
        outs.append(_rms(_bf_nn(kvn, w_list[2 * HEADS + h]), kn_n))
        outs.append(_bf_nn(kvn, w_list[3 * HEADS + h]))
    return tuple(outs) + (kp,)


def _mla_pre_operands(lat_ref, pos_ref, ln_ref, w_ref, nw_ref, rope_ref):
    cos_f, sin_f = _rope_tables(pos_ref[...], rope_ref[0:1, :], rope_ref[1:2, :])
    diff = (lat_ref[:, 0:LORA], lat_ref[:, LORA:2 * LORA], lat_ref[:, 2 * LORA:LAT_W], ln_ref[0:1, :], ln_ref[1:2, :],
            [w_ref[i].astype(F32) for i in range(4 * HEADS)], nw_ref[0:1, :], nw_ref[1:2, :], nw_ref[2:3, :], nw_ref[3:4, :])
    return diff, cos_f, sin_f


def _mla_pre_fwd(lat, pos, ln_w, w_mla, nw, rope_rows):
    T = lat.shape[0]
    tm = min(TOKEN_TILE, T)

    def body(lat_ref, pos_ref, ln_ref, w_ref, nw_ref, rope_ref, q_ref, k_ref, v_ref):
        diff, cos_f, sin_f = _mla_pre_operands(lat_ref, pos_ref, ln_ref, w_ref, nw_ref, rope_ref)
        outs = _mla_pre_fn(*diff, cos_f, sin_f)
        kp = outs[-1].astype(BF16)
        for h in range(HEADS):
            q_n, q_p, k_n, v = outs[4 * h:4 * h + 4]
            q_ref[:, h * QK_PAD:h * QK_PAD + HEAD_DIM] = q_n.astype(BF16)
            q_ref[:, h * QK_PAD + HEAD_DIM:(h + 1) * QK_PAD] = q_p.astype(BF16)
            k_ref[:, h * QK_PAD:h * QK_PAD + HEAD_DIM] = k_n.astype(BF16)
            k_ref[:, h * QK_PAD + HEAD_DIM:(h + 1) * QK_PAD] = kp
            v_ref[:, h * HEAD_DIM:(h + 1) * HEAD_DIM] = v.astype(BF16)

    return pl.pallas_call(
        body, grid=(T // tm,),
        in_specs=[_row_spec(tm, LAT_W), _row_spec(tm, 1), _const_spec((2, LORA)), _const_spec((4 * HEADS, LORA, 128)),
                  _const_spec((8, 128)), _const_spec((8, 128))],
        out_specs=[_row_spec(tm, HEADS * QK_PAD), _row_spec(tm, HEADS * QK_PAD), _row_spec(tm, HEADS * HEAD_DIM)],
        out_shape=[_sds((T, HEADS * QK_PAD), BF16), _sds((T, HEADS * QK_PAD), BF16), _sds((T, HEADS * HEAD_DIM), BF16)],
        compiler_params=_params(("parallel",)), name="mla_pre_fwd",
    )(lat, pos, ln_w, w_mla, nw, rope_rows)


def _mla_pre_bwd(lat, pos, ln_w, w_mla, nw, rope_rows, dq, dk, dv):
    T = lat.shape[0]
    tm = min(TOKEN_TILE, T)

    def body(lat_ref, pos_ref, ln_ref, w_ref, nw_ref, rope_ref, dq_ref, dk_ref, dv_ref, dlat_ref, dln_ref, dw_ref, dnw_ref):
        @pl.when(pl.program_id(0) == 0)
        def _():
            dln_ref[...] = jnp.zeros_like(dln_ref)
            dw_ref[...] = jnp.zeros_like(dw_ref)
            dnw_ref[...] = jnp.zeros_like(dnw_ref)

        diff, cos_f, sin_f = _mla_pre_operands(lat_ref, pos_ref, ln_ref, w_ref, nw_ref, rope_ref)
        _, pull = jax.vjp(lambda *a: _mla_pre_fn(*a, cos_f, sin_f), *diff)
        cts = []
        d_kp = jnp.zeros((tm, 128), F32)
        for h in range(HEADS):
            cts.append(dq_ref[:, h * QK_PAD:h * QK_PAD + HEAD_DIM])
            cts.append(dq_ref[:, h * QK_PAD + HEAD_DIM:(h + 1) * QK_PAD])
            cts.append(dk_ref[:, h * QK_PAD:h * QK_PAD + HEAD_DIM])
            cts.append(dv_ref[:, h * HEAD_DIM:(h + 1) * HEAD_DIM])
            d_kp += dk_ref[:, h * QK_PAD + HEAD_DIM:(h + 1) * QK_PAD]
        d_ql, d_kvl, d_kpe, d_lnq, d_lnkv, d_w, d_qn_n, d_qn_p, d_kn_n, d_kn_p = pull(tuple(cts) + (d_kp,))
        dlat_ref[:, 0:LORA] = d_ql
        dlat_ref[:, LORA:2 * LORA] = d_kvl
        dlat_ref[:, 2 * LORA:LAT_W] = d_kpe
        dln_ref[0:1, :] += d_lnq
        dln_ref[1:2, :] += d_lnkv
        for i in range(4 * HEADS):
            dw_ref[i] += d_w[i]
        for i, d in enumerate((d_qn_n, d_qn_p, d_kn_n, d_kn_p)):
            dnw_ref[i:i + 1, :] += d

    return pl.pallas_call(
        body, grid=(T // tm,),
        in_specs=[_row_spec(tm, LAT_W), _row_spec(tm, 1), _const_spec((2, LORA)), _const_spec((4 * HEADS, LORA, 128)),
                  _const_spec((8, 128)), _const_spec((8, 128)),
                  _row_spec(tm, HEADS * QK_PAD), _row_spec(tm, HEADS * QK_PAD), _row_spec(tm, HEADS * HEAD_DIM)],
        out_specs=[_row_spec(tm, LAT_W), _const_spec((2, LORA)), _const_spec((4 * HEADS, LORA, 128)), _const_spec((8, 128))],
        out_shape=[_sds((T, LAT_W), F32), _sds((2, LORA), F32), _sds((4 * HEADS, LORA, 128), F32), _sds((8, 128), F32)],
        compiler_params=_params(("arbitrary",)), name="mla_pre_bwd",
    )(lat, pos, ln_w, w_mla, nw, rope_rows, dq, dk, dv)


def _causal_mask(i, j, tq, tk):
    row = i * tq + lax.broadcasted_iota(jnp.int32, (tq, tk), 0)
    col = j * tk + lax.broadcasted_iota(jnp.int32, (tq, tk), 1)
    return col <= row


def _attn_fwd(q, k, v):
    B, S, _ = q.shape
    t = min(ATTN_TILE, S)

    def body(q_ref, k_ref, v_ref, o_ref, lse_ref):
        i = pl.program_id(2)
        qb = q_ref[0]

        def step(j, carry, diagonal):
            m, l, acc = carry
            rows = pl.ds(pl.multiple_of(j * t, t), t)
            s = _dg(qb, k_ref[0, rows, :], 1, 1, None) * ATTN_SCALE
            if diagonal:
                s = jnp.where(_causal_mask(0, 0, t, t), s, -1e30)
            m_new = jnp.maximum(m, jnp.max(s, axis=-1, keepdims=True))
            p = jnp.exp(s - m_new)
            alpha = jnp.exp(m - m_new)
            l = alpha * l + jnp.sum(p, axis=-1, keepdims=True)
            acc = alpha * acc + jnp.dot(p.astype(BF16), v_ref[0, rows, :], preferred_element_type=F32)
            return m_new, l, acc

        init = (jnp.full((t, 1), -1e30, F32), jnp.zeros((t, 1), F32), jnp.zeros((t, HEAD_DIM), F32))
        below = lax.fori_loop(0, i, lambda j, carry: step(j, carry, False), init)
        m, l, acc = step(i, below, True)
        o_ref[0] = acc / l
        lse_ref[0, 0] = m + jnp.log(l)

    return pl.pallas_call(
        body, grid=(B, HEADS, S // t),
        in_specs=[pl.BlockSpec((1, t, QK_PAD), lambda b, h, i: (b, i, h)),
                  pl.BlockSpec((1, S, QK_PAD), lambda b, h, i: (b, 0, h)),
                  pl.BlockSpec((1, S, HEAD_DIM), lambda b, h, i: (b, 0, h))],
        out_specs=[pl.BlockSpec((1, t, HEAD_DIM), lambda b, h, i: (b, i, h)),
                   pl.BlockSpec((1, 1, t, 1), lambda b, h, i: (b, h, i, 0))],
        out_shape=[_sds((B, S, HEADS * HEAD_DIM), F32), _sds((B, HEADS, S, 1), F32)],
        compiler_params=_params(("parallel", "parallel", "parallel")), name="attn_fwd",
    )(q, k, v)


def _attn_bwd(q, k, v, o, lse, do, halves):
    B, S, _ = q.shape
    t = min(ATTN_TILE, S)
    nq = S // t
    ns = len(halves)

    def body(*refs):
        q_ref, k_ref, v_ref, o_ref, lse_ref, do_ref = refs[:6]
        src_refs = refs[6:6 + ns]
        dq_ref, dk_ref, dv_ref = refs[6 + ns:9 + ns]
        dst_refs = refs[9 + ns:9 + 2 * ns]
        dsum_ref, send_sems, recv_sems, local_sems = refs[9 + 2 * ns:]
        b, h, j = pl.program_id(0), pl.program_id(1), pl.program_id(2)

        @pl.when((b == 0) & (h == 0) & (j == 0))
        def _():
            for start in _swap_copies(src_refs, dst_refs, send_sems, recv_sems, local_sems)[0]:
                start()

        @pl.when(j == 0)
        def _():
            dq_ref[...] = jnp.zeros_like(dq_ref)
            dsum_ref[...] = jnp.sum(do_ref[0] * o_ref[0], axis=-1, keepdims=True)

        kb = k_ref[0]
        vb = v_ref[0]

        def step(i, carry, diagonal):
            dk, dv = carry
            rows = pl.ds(pl.multiple_of(i * t, t), t)
            qb = q_ref[0, rows, :]
            dob = do_ref[0, rows, :].astype(BF16)
            s = _dg(qb, kb, 1, 1, None) * ATTN_SCALE
            p = jnp.exp(s - lse_ref[0, 0, rows, :])
            if diagonal:
                p = jnp.where(_causal_mask(0, 0, t, t), p, 0.0)
            pb = p.astype(BF16)
            dv = dv + _dg(pb, dob, 0, 0, None)
            dp = _dg(dob, vb, 1, 1, None)
            ds = (p * (dp - dsum_ref[rows, :]) * ATTN_SCALE).astype(BF16)
            dq_ref[0, rows, :] += jnp.dot(ds, kb, preferred_element_type=F32)
            dk = dk + _dg(ds, qb, 0, 0, None)
            return dk, dv

        on_diagonal = step(j, (jnp.zeros((t, QK_PAD), F32), jnp.zeros((t, HEAD_DIM), F32)), True)
        dk, dv = lax.fori_loop(j + 1, nq, lambda i, carry: step(i, carry, False), on_diagonal)
        dk_ref[0] = dk
        dv_ref[0] = dv

        @pl.when((b == B - 1) & (h == HEADS - 1) & (j == nq - 1))
        def _():
            for wait in _swap_copies(src_refs, dst_refs, send_sems, recv_sems, local_sems)[1]:
                wait()

    return pl.pallas_call(
        body, grid=(B, HEADS, nq),
        in_specs=[pl.BlockSpec((1, S, QK_PAD), lambda b, h, j: (b, 0, h)),
                  pl.BlockSpec((1, t, QK_PAD), lambda b, h, j: (b, j, h)),
                  pl.BlockSpec((1, t, HEAD_DIM), lambda b, h, j: (b, j, h)),
                  pl.BlockSpec((1, S, HEAD_DIM), lambda b, h, j: (b, 0, h)),
                  pl.BlockSpec((1, 1, S, 1), lambda b, h, j: (b, h, 0, 0)),
                  pl.BlockSpec((1, S, HEAD_DIM), lambda b, h, j: (b, 0, h))] + [_ANY] * ns,
        out_specs=[pl.BlockSpec((1, S, QK_PAD), lambda b, h, j: (b, 0, h)),
                   pl.BlockSpec((1, t, QK_PAD), lambda b, h, j: (b, j, h)),
                   pl.BlockSpec((1, t, HEAD_DIM), lambda b, h, j: (b, j, h))] + [_ANY] * ns,
        out_shape=[_sds((B, S, HEADS * QK_PAD), F32), _sds((B, S, HEADS * QK_PAD), F32), _sds((B, S, HEADS * HEAD_DIM), F32)]
                  + [_sds((2,) + s.shape, s.dtype) for s in halves],
        scratch_shapes=[pltpu.VMEM((S, 1), F32)] + _swap_scratch(ns),
        compiler_params=_params(("arbitrary", "arbitrary", "arbitrary")), name="attn_bwd",
    )(q, k, v, o, lse, do, *halves)


def _gdn_pre_fn(xq, xk, xv, wq, wk, wv, keeps):
    def conv_silu(x, w):
        acc = x * w[3]
        for s in (1, 2, 3):
            acc = acc + _shift_rows(x, keeps[s - 1], s) * w[3 - s]
        return _silu(acc)

    def l2(x):
        return x * lax.rsqrt(jnp.sum(x * x, axis=-1, keepdims=True) + EPS)

    return l2(conv_silu(xq, wq)) * (HEAD_DIM ** -0.5), l2(conv_silu(xk, wk)), conv_silu(xv, wv)


def _gdn_pre_specs(S):
    x_specs = [pl.BlockSpec((1, S, HEAD_DIM), lambda h, b, g=g: (b, 0, g * HEADS + h)) for g in range(3)]
    w_specs = [pl.BlockSpec((CONV_TAPS, HEAD_DIM), lambda h, b, g=g: (0, g * HEADS + h)) for g in range(3)]
    out_spec = pl.BlockSpec((1, S, HEAD_DIM), lambda h, b: (b, 0, h))
    return x_specs, w_specs, out_spec


def _row_keeps(S):
    t = lax.broadcasted_iota(jnp.int32, (S, HEAD_DIM), 0)
    return [(t >= s).astype(F32) for s in (1, 2, 3)]


def _gdn_pre_fwd(gqkv, conv_w):
    B, S, _ = gqkv.shape
    x_specs, w_specs, out_spec = _gdn_pre_specs(S)

    def body(xq_ref, xk_ref, xv_ref, wq_ref, wk_ref, wv_ref, q_ref, k_ref, v_ref):
        taps = [[w[i:i + 1, :] for i in range(CONV_TAPS)] for w in (wq_ref, wk_ref, wv_ref)]
        q, k, v = _gdn_pre_fn(xq_ref[0], xk_ref[0], xv_ref[0], *taps, _row_keeps(S))
        q_ref[0], k_ref[0], v_ref[0] = q, k, v

    return pl.pallas_call(
        body, grid=(HEADS, B), in_specs=x_specs + w_specs, out_specs=[out_spec] * 3,
        out_shape=[_sds((B, S, HEADS * HEAD_DIM), F32)] * 3,
        compiler_params=_params(("parallel", "parallel")), name="gdn_pre_fwd",
    )(gqkv, gqkv, gqkv, conv_w, conv_w, conv_w)


def _gdn_pre_bwd(gqkv, conv_w, dq, dk, dv):
    B, S, _ = gqkv.shape
    x_specs, w_specs, out_spec = _gdn_pre_specs(S)
    dw_spec = pl.BlockSpec((CONV_TAPS, HEAD_DIM), lambda h, b: (0, h))

    def body(xq_ref, xk_ref, xv_ref, wq_ref, wk_ref, wv_ref, dq_ref, dk_ref, dv_ref,
             dxq_ref, dxk_ref, dxv_ref, dwq_ref, dwk_ref, dwv_ref):
        @pl.when(pl.program_id(1) == 0)
        def _():
            for r in (dwq_ref, dwk_ref, dwv_ref):
                r[...] = jnp.zeros_like(r)

        taps = [[w[i:i + 1, :] for i in range(CONV_TAPS)] for w in (wq_ref, wk_ref, wv_ref)]
        keeps = _row_keeps(S)
        _, pull = jax.vjp(lambda *a: _gdn_pre_fn(*a, keeps), xq_ref[0], xk_ref[0], xv_ref[0], *taps)
        dxq, dxk, dxv, dwq, dwk, dwv = pull((dq_ref[0], dk_ref[0], dv_ref[0]))
        dxq_ref[0], dxk_ref[0], dxv_ref[0] = dxq, dxk, dxv
        for ref, dw in ((dwq_ref, dwq), (dwk_ref, dwk), (dwv_ref, dwv)):
            for i in range(CONV_TAPS):
                ref[i:i + 1, :] += dw[i]

    hw = HEADS * HEAD_DIM
    return pl.pallas_call(
        body, grid=(HEADS, B), in_specs=x_specs + w_specs + [out_spec] * 3,
        out_specs=[out_spec] * 3 + [dw_spec] * 3,
        out_shape=[_sds((B, S, hw), F32)] * 3 + [_sds((CONV_TAPS, hw), F32)] * 3,
        compiler_params=_params(("parallel", "arbitrary")), name="gdn_pre_bwd",
    )(gqkv, gqkv, gqkv, conv_w, conv_w, conv_w, dq, dk, dv)


def _chunk_masks():
    i = lax.broadcasted_iota(jnp.int32, (CHUNK, CHUNK), 0)
    j = lax.broadcasted_iota(jnp.int32, (CHUNK, CHUNK), 1)
    return {"le": (j <= i).astype(F32), "gt_t": (j > i).astype(F32), "strict": (j < i).astype(F32)}


def _gdn_chunk_fn(h, masks):
    pick_a, pick_b = _onehot_row(h), _onehot_row(HEADS + h)
    lower, upper_t, strict = masks["le"], masks["gt_t"], masks["strict"]
    ones_row = jnp.ones((1, HEAD_DIM), F32)

    def f(q, k, v, gab, a_row, dt_row, state):
        ga = jnp.sum(gab * pick_a, axis=1, keepdims=True)
        gb = jnp.sum(gab * pick_b, axis=1, keepdims=True)
        a_log = jnp.sum(a_row * pick_a, axis=1, keepdims=True)
        dt_bias = jnp.sum(dt_row * pick_a, axis=1, keepdims=True)
        beta = _sigmoid(gb)
        g = -jnp.exp(a_log) * _softplus(ga + dt_bias)
        g_wide = g * ones_row
        cum = _hi_nn(lower, g_wide)
        rest = _hi_nn(upper_t, g_wide)
        total = jnp.sum(g_wide, axis=0, keepdims=True)
        diff = _hi_nn(lower, g * strict)
        decay = lower * jnp.exp(diff)
        e_cum = jnp.exp(cum)
        lmat = strict * (beta * _bf_nt(k, k) * decay)
        u = v * beta
        w = k * (beta * e_cum)
        u = u - _hi_nn(lmat, u)
        w = w - _hi_nn(lmat, w)
        power = lmat
        for _ in range(5):
            power = _hi_nn(power, power)
            u = u + _hi_nn(power, u)
            w = w + _hi_nn(power, w)
        attn = _bf_nt(q, k) * decay
        v_new = u - _bf_nn(w, state)
        o = _bf_nn(q * e_cum, state) + _bf_nn(attn, v_new)
        new_state = state * jnp.exp(total) + _bf_tn(k * jnp.exp(rest), v_new)
        return o, new_state

    return f


def _gdn_chunk_fwd(q, k, v, gab, scal, shards):
    B, S, W = q.shape
    N = S // CHUNK
    ns = len(shards)

    def body(*refs):
        q_ref, k_ref, v_ref, gab_ref, sc_ref = refs[:5]
        src_refs = refs[5:5 + ns]
        o_ref, st_ref = refs[5 + ns:7 + ns]
        dst_refs = refs[7 + ns:7 + 2 * ns]
        state_ref, send_sems, recv_sems, local_sems = refs[7 + 2 * ns:]
        n = pl.program_id(0)

        @pl.when(n == 0)
        def _():
            for start in _gather_copies(src_refs, dst_refs, send_sems, recv_sems, local_sems)[0]:
                start()
            state_ref[...] = jnp.zeros_like(state_ref)

        masks = _chunk_masks()
        for b in range(B):
            for h in range(HEADS):
                lanes = slice(h * HEAD_DIM, (h + 1) * HEAD_DIM)
                state = state_ref[b * HEADS + h]
                st_ref[b, 0, h] = state
                o, new_state = _gdn_chunk_fn(h, masks)(q_ref[b, :, lanes], k_ref[b, :, lanes], v_ref[b, :, lanes], gab_ref[b],
                                                       sc_ref[0:1, :], sc_ref[1:2, :], state)
                o_ref[b, :, lanes] = o
                state_ref[b * HEADS + h] = new_state

        @pl.when(n == N - 1)
        def _():
            for wait in _gather_copies(src_refs, dst_refs, send_sems, recv_sems, local_sems)[1]:
                wait()

    seq = pl.BlockSpec((B, CHUNK, W), lambda n: (0, n, 0))
    return pl.pallas_call(
        body, grid=(N,),
        in_specs=[seq, seq, seq, pl.BlockSpec((B, CHUNK, GAB_W), lambda n: (0, n, 0)), _const_spec((8, 128))] + [_ANY] * ns,
        out_specs=[seq, pl.BlockSpec((B, 1, HEADS, HEAD_DIM, HEAD_DIM), lambda n: (0, n, 0, 0, 0))] + [_ANY] * ns,
        out_shape=[_sds((B, S, W), F32), _sds((B, N, HEADS, HEAD_DIM, HEAD_DIM), F32)] + [_sds((4,) + s.shape, s.dtype) for s in shards],
        scratch_shapes=[pltpu.VMEM((B * HEADS, HEAD_DIM, HEAD_DIM), F32)] + _gather_scratch(ns),
        compiler_params=_params(("arbitrary",)), name="gdn_chunk_fwd",
    )(q, k, v, gab, scal, *shards)


def _gdn_chunk_bwd(q, k, v, gab, scal, states, do, partials):
    B, S, W = q.shape
    N = S // CHUNK
    ns = len(partials)

    def body(*refs):
        q_ref, k_ref, v_ref, gab_ref, sc_ref, st_ref, do_ref = refs[:7]
        src_refs = refs[7:7 + ns]
        dq_ref, dk_ref, dv_ref, dgab_ref, dsc_ref = refs[7 + ns:12 + ns]
        dst_refs = refs[12 + ns:12 + 2 * ns]
        dstate_ref, send_sems, recv_sems, local_sems = refs[12 + 2 * ns:]
        n = pl.program_id(0)

        @pl.when(n == 0)
        def _():
            for start in _scatter_copies(src_refs, dst_refs, send_sems, recv_sems, local_sems)[0]:
                start()
            dstate_ref[...] = jnp.zeros_like(dstate_ref)
            dsc_ref[...] = jnp.zeros_like(dsc_ref)

        masks = _chunk_masks()
        d_a = jnp.zeros((1, 128), F32)
        d_dt = jnp.zeros((1, 128), F32)
        for b in range(B):
            d_gab = jnp.zeros((CHUNK, GAB_W), F32)
            for h in range(HEADS):
                lanes = slice(h * HEAD_DIM, (h + 1) * HEAD_DIM)
                _, pull = jax.vjp(_gdn_chunk_fn(h, masks), q_ref[b, :, lanes], k_ref[b, :, lanes], v_ref[b, :, lanes], gab_ref[b],
                                  sc_ref[0:1, :], sc_ref[1:2, :], st_ref[b, 0, h])
                dq, dk, dv, dg, da, ddt, dstate = pull((do_ref[b, :, lanes], dstate_ref[b * HEADS + h]))
                dq_ref[b, :, lanes] = dq
                dk_ref[b, :, lanes] = dk
                dv_ref[b, :, lanes] = dv
                dstate_ref[b * HEADS + h] = dstate
                d_gab, d_a, d_dt = d_gab + dg, d_a + da, d_dt + ddt
            dgab_ref[b] = d_gab
        dsc_ref[0:1, :] += d_a
        dsc_ref[1:2, :] += d_dt

        @pl.when(n == N - 1)
        def _():
            for wait in _scatter_copies(src_refs, dst_refs, send_sems, recv_sems, local_sems)[1]:
                wait()

    seq = pl.BlockSpec((B, CHUNK, W), lambda n: (0, N - 1 - n, 0))
    gab_spec = pl.BlockSpec((B, CHUNK, GAB_W), lambda n: (0, N - 1 - n, 0))
    return pl.pallas_call(
        body, grid=(N,),
        in_specs=[seq, seq, seq, gab_spec, _const_spec((8, 128)),
                  pl.BlockSpec((B, 1, HEADS, HEAD_DIM, HEAD_DIM), lambda n: (0, N - 1 - n, 0, 0, 0)), seq] + [_ANY] * ns,
        out_specs=[seq, seq, seq, gab_spec, _const_spec((8, 128))] + [_ANY] * ns,
        out_shape=[_sds((B, S, W), F32)] * 3 + [_sds((B, S, GAB_W), F32), _sds((8, 128), F32)] + [_scattered_shape(p) for p in partials],
        scratch_shapes=[pltpu.VMEM((B * HEADS, HEAD_DIM, HEAD_DIM), F32)] + _scatter_scratch(ns),
        compiler_params=_params(("arbitrary",)), name="gdn_chunk_bwd",
    )(q, k, v, gab, scal, states, do, *partials)


def _mix_fn(ao, go, gz, w_mla, w_gdn):
    return tuple(_rms(ao[h], w_mla[h]) for h in range(HEADS)) + tuple(_rms(go[h], w_gdn) * _silu(gz[h]) for h in range(HEADS))


def _mix_operands(ao_ref, go_ref, gz_ref, nw_ref):
    blocks = lambda ref: [ref[:, h * HEAD_DIM:(h + 1) * HEAD_DIM] for h in range(HEADS)]
    return blocks(ao_ref), blocks(go_ref), blocks(gz_ref), [nw_ref[h:h + 1, :] for h in range(HEADS)], nw_ref[HEADS:HEADS + 1, :]


def _mix_fwd(ao, go, gz, nw, w_out, x2):
    T, D = x2.shape
    tm = min(TOKEN_TILE, T)
    MW = 2 * HEADS * HEAD_DIM

    def body(ao_ref, go_ref, gz_ref, nw_ref, w_ref, x_ref, mix_ref, h_ref):
        outs = _mix_fn(*_mix_operands(ao_ref, go_ref, gz_ref, nw_ref))
        for i, piece in enumerate(outs):
            mix_ref[:, i * HEAD_DIM:(i + 1) * HEAD_DIM] = piece.astype(BF16)
        h_ref[...] = x_ref[...] + jnp.dot(mix_ref[...], w_ref[...], preferred_element_type=F32)

    half = HEADS * HEAD_DIM
    return pl.pallas_call(
        body, grid=(T // tm,),
        in_specs=[_row_spec(tm, half), _row_spec(tm, half), _row_spec(tm, half), _const_spec((8, 128)), _const_spec((MW, D)),
                  _row_spec(tm, D)],
        out_specs=[_row_spec(tm, MW), _row_spec(tm, D)],
        out_shape=[_sds((T, MW), BF16), _sds((T, D), F32)],
        compiler_params=_params(("parallel",)), name="mix_fwd",
    )(ao, go, gz, nw, w_out, x2)


def _mix_bwd(ao, go, gz, nw, w_out, dh):
    T, D = dh.shape
    tm = min(TOKEN_TILE, T)
    MW = 2 * HEADS * HEAD_DIM
    half = HEADS * HEAD_DIM

    def body(ao_ref, go_ref, gz_ref, nw_ref, w_ref, dh_ref, dao_ref, dgo_ref, dgz_ref, dnw_ref):
        @pl.when(pl.program_id(0) == 0)
        def _():
            dnw_ref[...] = jnp.zeros_like(dnw_ref)

        d_mix = _dg(dh_ref[...].astype(BF16), w_ref[...], 1, 1, None)
        cts = tuple(d_mix[:, i * HEAD_DIM:(i + 1) * HEAD_DIM] for i in range(2 * HEADS))
        _, pull = jax.vjp(_mix_fn, *_mix_operands(ao_ref, go_ref, gz_ref, nw_ref))
        d_ao, d_go, d_gz, d_wm, d_wg = pull(cts)
        for h in range(HEADS):
            lanes = slice(h * HEAD_DIM, (h + 1) * HEAD_DIM)
            dao_ref[:, lanes] = d_ao[h]
            dgo_ref[:, lanes] = d_go[h]
            dgz_ref[:, lanes] = d_gz[h]
            dnw_ref[h:h + 1, :] += d_wm[h]
        dnw_ref[HEADS:HEADS + 1, :] += d_wg

    return pl.pallas_call(
        body, grid=(T // tm,),
        in_specs=[_row_spec(tm, half), _row_spec(tm, half), _row_spec(tm, half), _const_spec((8, 128)), _const_spec((MW, D)),
                  _row_spec(tm, D)],
        out_specs=[_row_spec(tm, half)] * 3 + [_const_spec((8, 128))],
        out_shape=[_sds((T, half), F32)] * 3 + [_sds((8, 128), F32)],
        compiler_params=_params(("arbitrary",)), name="mix_bwd",
    )(ao, go, gz, nw, w_out, dh)


def _up_spec(w_up, tf):
    per_shard = w_up.shape[2] // tf
    return pl.BlockSpec((None, w_up.shape[1], tf), lambda i, j: (j // per_shard, 0, j % per_shard))


def _mlp_fwd(h2, w_mn, w_up, w_down, target):
    T, D = h2.shape
    FF = w_down.shape[0]
    tm, tf = min(TOKEN_TILE, T), min(FF_TILE, w_up.shape[2])
    nf = FF // tf

    def body(h_ref, wn_ref, wu_ref, wd_ref, t_ref, hn_ref, dy_ref, sq_ref, acc_ref):
        j = pl.program_id(1)

        @pl.when(j == 0)
        def _():
            hn_ref[...] = _rms(h_ref[...], wn_ref[...]).astype(BF16)
            acc_ref[...] = jnp.zeros_like(acc_ref)

        up = jnp.dot(hn_ref[...], wu_ref[...], preferred_element_type=F32)
        act = jnp.square(jnp.maximum(up, 0.0)).astype(BF16)
        acc_ref[...] += jnp.dot(act, wd_ref[...], preferred_element_type=F32)

        @pl.when(j == nf - 1)
        def _():
            err = h_ref[...] + acc_ref[...] - t_ref[...]
            dy_ref[...] = err * (1.0 / D)
            sq_ref[...] = jnp.zeros_like(sq_ref) + jnp.sum(err * err)

    tok = lambda w: pl.BlockSpec((tm, w), lambda i, j: (i, 0))
    return pl.pallas_call(
        body, grid=(T // tm, nf),
        in_specs=[tok(D), _const_spec((1, D)), _up_spec(w_up, tf), pl.BlockSpec((tf, D), lambda i, j: (j, 0)), tok(D)],
        out_specs=[tok(D), tok(D), pl.BlockSpec((1, 8, 128), lambda i, j: (i, 0, 0))],
        out_shape=[_sds((T, D), BF16), _sds((T, D), F32), _sds((T // tm, 8, 128), F32)],
        scratch_shapes=[pltpu.VMEM((tm, D), F32)],
        compiler_params=_params(("parallel", "arbitrary")), name="mlp_fwd",
    )(h2, w_mn, w_up, w_down, target)


def _mlp_bwd(h2, w_mn, hn, w_up, w_down, dy):
    T, D = h2.shape
    FF = w_down.shape[0]
    tm, tf = min(TOKEN_TILE, T), min(FF_TILE, w_up.shape[2])
    nf = FF // tf

    def body(h_ref, wn_ref, hn_ref, wu_ref, wd_ref, dy_ref, dh_ref, act_ref, dup_ref, dwn_ref, acc_ref):
        i, j = pl.program_id(0), pl.program_id(1)

        @pl.when((i == 0) & (j == 0))
        def _():
            dwn_ref[...] = jnp.zeros_like(dwn_ref)

        @pl.when(j == 0)
        def _():
            acc_ref[...] = jnp.zeros_like(acc_ref)

        r = jnp.maximum(jnp.dot(hn_ref[...], wu_ref[...], preferred_element_type=F32), 0.0)
        act_ref[...] = (r * r).astype(BF16)
        d_act = _dg(dy_ref[...].astype(BF16), wd_ref[...], 1, 1, None)
        d_up = (d_act * (2.0 * r)).astype(BF16)
        dup_ref[...] = d_up
        acc_ref[...] += _dg(d_up, wu_ref[...], 1, 1, None)

        @pl.when(j == nf - 1)
        def _():
            _, pull = jax.vjp(_rms, h_ref[...], wn_ref[...])
            dh, dwn = pull(acc_ref[...])
            dh_ref[...] = dh + dy_ref[...]
            dwn_ref[...] += dwn

    tok = lambda w: pl.BlockSpec((tm, w), lambda i, j: (i, 0))
    ff = pl.BlockSpec((tm, tf), lambda i, j: (i, j))
    return pl.pallas_call(
        body, grid=(T // tm, nf),
        in_specs=[tok(D), _const_spec((1, D)), tok(D), _up_spec(w_up, tf), pl.BlockSpec((tf, D), lambda i, j: (j, 0)), tok(D)],
        out_specs=[tok(D), ff, ff, _const_spec((1, D))],
        out_shape=[_sds((T, D), F32), _sds((T, FF), BF16), _sds((T, FF), BF16), _sds((1, D), F32)],
        scratch_shapes=[pltpu.VMEM((tm, D), F32)],
        compiler_params=_params(("arbitrary", "arbitrary")), name="mlp_bwd",
    )(h2, w_mn, hn, w_up, w_down, dy)


def _rope_pad(a):
    z = jnp.zeros(a.shape[:-1] + (ROPE_HALF,), a.dtype)
    return jnp.concatenate([a[..., :ROPE_HALF], z, a[..., ROPE_HALF:], z], axis=-1)


def _rope_unpad(a):
    return jnp.concatenate([a[..., :ROPE_HALF], a[..., 2 * ROPE_HALF:3 * ROPE_HALF]], axis=-1)


_G0 = 2 * LORA + ROPE_DIM
_GZ0 = _G0 + GQKV_W
_GA0 = _GZ0 + GZ_W


def _widen_w_in(w):
    pad = jnp.zeros((w.shape[0], GAB_W - 2 * HEADS), w.dtype)
    return jnp.concatenate([w[:, :2 * LORA], _rope_pad(w[:, 2 * LORA:_G0]), w[:, _G0:_GA0], w[:, _GA0:], pad], axis=1)


def _narrow_w_in(w):
    return jnp.concatenate([w[:, :2 * LORA], _rope_unpad(w[:, 2 * LORA:LAT_W]), w[:, LAT_W:PROJ_SPLITS[2][1]],
                            w[:, PROJ_SPLITS[3][0]:PROJ_SPLITS[3][0] + 2 * HEADS]], axis=1)


def _stack_mla(w_uq, w_ukv):
    uq = w_uq.reshape(LORA, HEADS, QK_DIM)
    ukv = w_ukv.reshape(LORA, HEADS, 2 * HEAD_DIM)
    parts = [uq[:, :, :HEAD_DIM], _rope_pad(uq[:, :, HEAD_DIM:]), ukv[:, :, :HEAD_DIM], ukv[:, :, HEAD_DIM:]]
    return jnp.concatenate([p.transpose(1, 0, 2) for p in parts], axis=0)


def _unstack_mla(w):
    p = [w[i * HEADS:(i + 1) * HEADS].transpose(1, 0, 2) for i in range(4)]
    uq = jnp.concatenate([p[0], _rope_unpad(p[1])], axis=-1).reshape(LORA, HEADS * QK_DIM)
    ukv = jnp.concatenate([p[2], p[3]], axis=-1).reshape(LORA, HEADS * 2 * HEAD_DIM)
    return uq, ukv


def _rows8(rows):
    a = jnp.concatenate(rows, axis=0)
    return jnp.pad(a, ((0, 8 - a.shape[0]), (0, 0)))


def _qk_norm_rows(q_norm_w, k_norm_w):
    return _rows8([q_norm_w[:, :HEAD_DIM], _rope_pad(q_norm_w[:, HEAD_DIM:]), k_norm_w[:, :HEAD_DIM], _rope_pad(k_norm_w[:, HEAD_DIM:])])


def _rope_rows():
    inv_freq = ROPE_THETA ** (-jnp.arange(ROPE_HALF, dtype=F32) / ROPE_HALF)
    z = jnp.zeros((ROPE_HALF,), F32)
    freq = jnp.concatenate([inv_freq, z, inv_freq, z])
    sign = jnp.concatenate([-jnp.ones((ROPE_HALF,), F32), z, jnp.ones((ROPE_HALF,), F32), z])
    return _rows8([freq[None], sign[None]])


def _round_up(n, m):
    return -(-n // m) * m


def _column_shards(a):
    return a.reshape(a.shape[0], 4, a.shape[1] // 4).transpose(1, 0, 2)


def _from_column_shards(a):
    return a.transpose(1, 0, 2).reshape(a.shape[1], 4 * a.shape[2])


def _pack_small(arrays):
    rows = [jnp.pad(a.reshape(-1), (0, _round_up(a.size, 128) - a.size)).reshape(-1, 128) for a in arrays]
    packed = jnp.concatenate(rows, axis=0)
    return jnp.pad(packed, ((0, _round_up(packed.shape[0], 8) - packed.shape[0]), (0, 0)))


def _unpack_small(packed, shapes):
    out, r = [], 0
    for s in shapes:
        n = math.prod(s)
        nr = _round_up(n, 128) // 128
        out.append(packed[r:r + nr].reshape(-1)[:n].reshape(s))
        r += nr
    return out


_ANY = pl.BlockSpec(memory_space=pl.ANY)
_OTHER_CHIPS = ((1, 0), (0, 1), (1, 1))


def _here():
    return lax.axis_index("x"), lax.axis_index("y"), lax.axis_index("c")


def _flip(v, bit):
    return 1 - v if bit else v


def _remote(src, dst, send_sems, recv_sems, k, to):
    return pltpu.make_async_remote_copy(src_ref=src, dst_ref=dst, send_sem=send_sems.at[k], recv_sem=recv_sems.at[k],
                                        device_id=to, device_id_type=MESH)


def _gather_copies(srcs, dsts, send_sems, recv_sems, local_sems):
    x, y, c = _here()
    slot = 2 * x + y
    starts, waits = [], []
    for i, (src, dst) in enumerate(zip(srcs, dsts)):
        own = pltpu.make_async_copy(src, dst.at[slot], local_sems.at[i])
        starts.append(own.start)
        waits.append(own.wait)
        for j, (fx, fy) in enumerate(_OTHER_CHIPS):
            cx, cy = _flip(x, fx), _flip(y, fy)
            push = _remote(src, dst.at[slot], send_sems, recv_sems, 3 * i + j, (cx, cy, c))
            landed = dst.at[2 * cx + cy]
            starts.append(push.start)
            waits += [_remote(landed, landed, send_sems, recv_sems, 3 * i + j, (cx, cy, c)).wait_recv, push.wait_send]
    return starts, waits


def _gather_scratch(n):
    return [pltpu.SemaphoreType.DMA((3 * n,)), pltpu.SemaphoreType.DMA((3 * n,)), pltpu.SemaphoreType.DMA((n,))]


def _all_gather(shards, name):
    ns = len(shards)

    def body(*refs):
        starts, waits = _gather_copies(refs[:ns], refs[ns:2 * ns], *refs[2 * ns:])
        for call in starts + waits:
            call()

    return pl.pallas_call(
        body, in_specs=[_ANY] * ns, out_specs=[_ANY] * ns, out_shape=[_sds((4,) + s.shape, s.dtype) for s in shards],
        scratch_shapes=_gather_scratch(ns), name=name,
    )(*shards)


def _scattered_shape(p):
    return _sds((8, p.shape[1] // 2) + p.shape[2:], p.dtype)


def _scatter_copies(srcs, dsts, send_sems, recv_sems, local_sems, whole=0):
    x, y, c = _here()
    me = 4 * x + 2 * y + c
    starts, waits = [], []
    for i, (src, dst) in enumerate(zip(srcs, dsts)):
        def piece(px, py, pc, src=src, entire=i >= len(srcs) - whole):
            if entire:
                return src
            half = src.shape[1] // 2
            return src.at[2 * px + py, pl.ds(pl.multiple_of(pc * half, 8), half)]

        own = pltpu.make_async_copy(piece(x, y, c), dst.at[me], local_sems.at[i])
        starts.append(own.start)
        waits.append(own.wait)
        for k in range(1, 8):
            px, py, pc = _flip(x, k & 4), _flip(y, k & 2), _flip(c, k & 1)
            push = _remote(piece(px, py, pc), dst.at[me], send_sems, recv_sems, 7 * i + k - 1, (px, py, pc))
            landed = dst.at[4 * px + 2 * py + pc]
            starts.append(push.start)
            waits += [_remote(landed, landed, send_sems, recv_sems, 7 * i + k - 1, (px, py, pc)).wait_recv, push.wait_send]
    return starts, waits


def _scatter_scratch(n):
    return [pltpu.SemaphoreType.DMA((7 * n,)), pltpu.SemaphoreType.DMA((7 * n,)), pltpu.SemaphoreType.DMA((n,))]


def _scatter(partials, wholes, name):
    ns = len(partials) + len(wholes)

    def body(*refs):
        starts, waits = _scatter_copies(refs[:ns], refs[ns:2 * ns], *refs[2 * ns:], whole=len(wholes))
        for call in starts + waits:
            call()

    return pl.pallas_call(
        body, in_specs=[_ANY] * ns, out_specs=[_ANY] * ns,
        out_shape=[_scattered_shape(p) for p in partials] + [_sds((8,) + s.shape, s.dtype) for s in wholes],
        scratch_shapes=_scatter_scratch(ns), name=name,
    )(*partials, *wholes)


def _swap_copies(srcs, dsts, send_sems, recv_sems, local_sems):
    x, y, c = _here()
    sibling = (x, y, 1 - c)
    starts, waits = [], []
    for i, (src, dst) in enumerate(zip(srcs, dsts)):
        own = pltpu.make_async_copy(src, dst.at[c], local_sems.at[i])
        push = _remote(src, dst.at[c], send_sems, recv_sems, i, sibling)
        landed = dst.at[1 - c]
        starts += [own.start, push.start]
        waits += [_remote(landed, landed, send_sems, recv_sems, i, sibling).wait_recv, push.wait_send, own.wait]
    return starts, waits


def _swap_scratch(n):
    return [pltpu.SemaphoreType.DMA((n,)), pltpu.SemaphoreType.DMA((n,)), pltpu.SemaphoreType.DMA((n,))]


def _exchange_halves(halves):
    ns = len(halves)

    def body(*refs):
        starts, waits = _swap_copies(refs[:ns], refs[ns:2 * ns], *refs[2 * ns:])
        for call in starts + waits:
            call()

    return pl.pallas_call(
        body, in_specs=[_ANY] * ns, out_specs=[_ANY] * ns, out_shape=[_sds((2,) + h.shape, h.dtype) for h in halves],
        scratch_shapes=_swap_scratch(ns), name="exchange_halves",
    )(*halves)


def _row_tile(rows, row_bytes, budget):
    tr = rows
    while tr * row_bytes > budget and tr % 16 == 0:
        tr //= 2
    return tr


def _sum_slots(parts, name):
    _, rows, cols = parts.shape
    tr = _row_tile(rows, 8 * cols * 4, 2 * 1024 * 1024)

    def body(p_ref, o_ref):
        acc = p_ref[0]
        for d in range(1, 8):
            acc = acc + p_ref[d]
        o_ref[...] = acc

    return pl.pallas_call(
        body, grid=(rows // tr,), in_specs=[pl.BlockSpec((8, tr, cols), lambda i: (0, i, 0))],
        out_specs=pl.BlockSpec((tr, cols), lambda i: (i, 0)), out_shape=_sds((rows, cols), F32),
        compiler_params=_params(("parallel",)), name=name,
    )(parts)


def _adamw(w, g, m, v, name):
    rows, cols = w.shape
    tr = _row_tile(rows, 7 * cols * 4, 4 * 1024 * 1024)

    def body(w_ref, g_ref, m_ref, v_ref, d_ref, mo_ref, vo_ref):
        g = g_ref[...]
        m = ADAM_B1 * m_ref[...] + (1.0 - ADAM_B1) * g
        v = ADAM_B2 * v_ref[...] + (1.0 - ADAM_B2) * jnp.square(g)
        m_hat = m / (1.0 - ADAM_B1 ** ADAM_STEP)
        v_hat = v / (1.0 - ADAM_B2 ** ADAM_STEP)
        d_ref[...] = -ADAM_LR * (m_hat / (jnp.sqrt(v_hat) + ADAM_EPS) + ADAM_WD * w_ref[...])
        mo_ref[...] = m
        vo_ref[...] = v

    spec = pl.BlockSpec((tr, cols), lambda i: (i, 0))
    return pl.pallas_call(
        body, grid=(rows // tr,), in_specs=[spec] * 4, out_specs=[spec] * 3, out_shape=[_sds((rows, cols), F32)] * 3,
        compiler_params=_params(("parallel",)), name=name,
    )(w, g, m, v)


def kernel(x, positions, attn_norm_w, w_in, q_lat_norm_w, w_uq, kv_lat_norm_w, w_ukv, q_norm_w, k_norm_w, mla_out_norm_w, conv_w, a_log, dt_bias, gdn_norm_w, w_out, mlp_norm_w, w_up, w_down, loss_target, m_attn_norm_w, m_w_in, m_q_lat_norm_w, m_w_uq, m_kv_lat_norm_w, m_w_ukv, m_q_norm_w, m_k_norm_w, m_mla_out_norm_w, m_conv_w, m_a_log, m_dt_bias, m_gdn_norm_w, m_w_out, m_mlp_norm_w, m_w_up, m_w_down, v_attn_norm_w, v_w_in, v_q_lat_norm_w, v_w_uq, v_kv_lat_norm_w, v_w_ukv, v_q_norm_w, v_k_norm_w, v_mla_out_norm_w, v_conv_w, v_a_log, v_dt_bias, v_gdn_norm_w, v_w_out, v_mlp_norm_w, v_w_up, v_w_down):
    w = dict(zip(WEIGHTS, (attn_norm_w, w_in, q_lat_norm_w, w_uq, kv_lat_norm_w, w_ukv, q_norm_w, k_norm_w, mla_out_norm_w, conv_w,
                           a_log, dt_bias, gdn_norm_w, w_out, mlp_norm_w, w_up, w_down)))
    m = dict(zip(WEIGHTS, (m_attn_norm_w, m_w_in, m_q_lat_norm_w, m_w_uq, m_kv_lat_norm_w, m_w_ukv, m_q_norm_w, m_k_norm_w,
                           m_mla_out_norm_w, m_conv_w, m_a_log, m_dt_bias, m_gdn_norm_w, m_w_out, m_mlp_norm_w, m_w_up, m_w_down)))
    v = dict(zip(WEIGHTS, (v_attn_norm_w, v_w_in, v_q_lat_norm_w, v_w_uq, v_kv_lat_norm_w, v_w_ukv, v_q_norm_w, v_k_norm_w,
                           v_mla_out_norm_w, v_conv_w, v_a_log, v_dt_bias, v_gdn_norm_w, v_w_out, v_mlp_norm_w, v_w_up, v_w_down)))
    B, S, D = x.shape
    T = B * S
    x2, pos, target = x.reshape(T, D), positions.reshape(T, 1), loss_target.reshape(T, D)
    seq = lambda a: a.reshape(B, S, a.shape[-1])
    tok = lambda a: a.reshape(T, a.shape[-1])
    local = {n: w[n][0] for n in SHARDED}

    g_in, g_uq, g_ukv, g_conv = _all_gather([local["w_in"].astype(BF16), local["w_uq"].astype(BF16), local["w_ukv"].astype(BF16),
                                             local["conv_w"]], "gather_first_weights")
    w_in_p = _widen_w_in(_from_column_shards(g_in))
    w_mla = _stack_mla(_from_column_shards(g_uq), _from_column_shards(g_ukv))
    conv_full = _from_column_shards(g_conv)
    ln_w = jnp.concatenate([q_lat_norm_w, kv_lat_norm_w], axis=0)
    qk_nw = _qk_norm_rows(q_norm_w, k_norm_w)
    rope_rows = _rope_rows()
    scal = _rows8([jnp.pad(a_log, ((0, 0), (0, 128 - HEADS))), jnp.pad(dt_bias, ((0, 0), (0, 128 - HEADS)))])
    mix_nw = _rows8([mla_out_norm_w[0], gdn_norm_w])

    xn, lat, gqkv, gz, gab = _in_proj_fwd(x2, attn_norm_w, w_in_p)
    q, k, v_att = _mla_pre_fwd(lat, pos, ln_w, w_mla, qk_nw, rope_rows)
    ao, lse = _attn_fwd(seq(q), seq(k), seq(v_att))
    gq, gk, gv = _gdn_pre_fwd(seq(gqkv), conv_full)
    go, states, g_out, w_up_b, g_down = _gdn_chunk_fwd(
        gq, gk, gv, seq(gab), scal, [local["w_out"].astype(BF16), local["w_up"].astype(BF16), local["w_down"].astype(BF16)])
    w_out_b = g_out.reshape(-1, D)
    w_down_b = g_down.reshape(-1, D)
    mix, h2 = _mix_fwd(tok(ao), tok(go), gz, mix_nw, w_out_b, x2)
    hn, dy, sq = _mlp_fwd(h2, mlp_norm_w, w_up_b, w_down_b, target)
    loss = lax.psum(jnp.sum(sq[:, 0, 0]) * (0.5 / D), ("x", "y", "c"))

    dh, act, d_up, d_mlp_nw = _mlp_bwd(h2, mlp_norm_w, hn, w_up_b, w_down_b, dy)
    p_down = _wgrad(act, dy, "wgrad_down").reshape(4, -1, D)
    p_up = _wgrad(hn, d_up, "wgrad_up", column_shards=4)
    d_ao, d_go, d_gz, d_mix_nw = _mix_bwd(tok(ao), tok(go), gz, mix_nw, w_out_b, dh)
    p_out = _wgrad(mix, dh, "wgrad_out").reshape(4, -1, D)
    d_gq, d_gk, d_gv, d_gab, d_scal, s_up, s_down, s_out = _gdn_chunk_bwd(gq, gk, gv, seq(gab), scal, states, seq(d_go),
                                                                          [p_up, p_down, p_out])
    dxq, dxk, dxv, dcq, dck, dcv = _gdn_pre_bwd(seq(gqkv), conv_full, d_gq, d_gk, d_gv)
    early = ("w_up", "w_down", "w_out")
    early_halves = [_sum_slots(s, "sum_" + n) for n, s in zip(early, (s_up, s_down, s_out))]
    dq, dk, dv, *early_grads = _attn_bwd(seq(q), seq(k), seq(v_att), ao, lse, seq(d_ao), early_halves)
    d_lat, d_ln, d_w_mla, d_qk_nw = _mla_pre_bwd(lat, pos, ln_w, w_mla, qk_nw, rope_rows, tok(dq), tok(dk), tok(dv))
    d_gqkv = jnp.concatenate([dxq, dxk, dxv], axis=-1)
    grad_x2, d_proj, d_attn_nw = _in_proj_bwd(d_lat, tok(d_gqkv), d_gz, tok(d_gab), w_in_p, x2, attn_norm_w, dh)
    p_in = _column_shards(_narrow_w_in(_wgrad(xn, d_proj, "wgrad_in")))
    p_uq, p_ukv = (_column_shards(a) for a in _unstack_mla(d_w_mla))
    small_partial = {
        "attn_norm_w": d_attn_nw, "q_lat_norm_w": d_ln[0:1], "kv_lat_norm_w": d_ln[1:2],
        "q_norm_w": jnp.concatenate([d_qk_nw[0:1], _rope_unpad(d_qk_nw[1:2])], axis=-1),
        "k_norm_w": jnp.concatenate([d_qk_nw[2:3], _rope_unpad(d_qk_nw[3:4])], axis=-1),
        "mla_out_norm_w": d_mix_nw[None, :HEADS], "a_log": d_scal[0:1, :HEADS], "dt_bias": d_scal[1:2, :HEADS],
        "gdn_norm_w": d_mix_nw[HEADS:HEADS + 1], "mlp_norm_w": d_mlp_nw,
    }
    conv_partial = jnp.concatenate([dcq, dck, dcv], axis=-1)
    s_in, s_uq, s_ukv, s_small = _scatter([p_in, p_uq, p_ukv], [_pack_small([small_partial[n] for n in SMALL] + [conv_partial])],
                                          "scatter_last_partials")

    late = ("w_in", "w_uq", "w_ukv")
    late_grads = _exchange_halves([_sum_slots(s, "sum_" + n) for n, s in zip(late, (s_in, s_uq, s_ukv))])
    names = early + late
    grad = {n: g.reshape(local[n].shape) for n, g in zip(names, list(early_grads) + list(late_grads))}
    small_shapes = [w[n].shape for n in SMALL]
    *g_small, g_conv_all = _unpack_small(_sum_slots(s_small, "sum_small"), small_shapes + [conv_partial.shape])
    grad.update(zip(SMALL, g_small))
    conv_cols = local["conv_w"].shape[1]
    grad["conv_w"] = lax.dynamic_slice_in_dim(g_conv_all, (2 * lax.axis_index("x") + lax.axis_index("y")) * conv_cols, conv_cols, axis=1)

    delta, new_m, new_v = {}, {}, {}
    for n in names:
        delta[n], new_m[n], new_v[n] = _adamw(local[n], grad[n], m[n][0], v[n][0], "adamw_" + n)
    packed_names = SMALL + ("conv_w",)
    packed_shapes = small_shapes + [local["conv_w"].shape]
    take = lambda d: _pack_small([d[n][0] if n == "conv_w" and d[n].ndim == 3 else d[n] for n in packed_names])
    outs = _adamw(take(w), take(grad), take(m), take(v), "adamw_small")
    for d, packed in zip((delta, new_m, new_v), outs):
        d.update(zip(packed_names, _unpack_small(packed, packed_shapes)))

    def in_order(d):
        return [d[n].reshape(w[n].shape) for n in WEIGHTS]

    return (loss, grad_x2.reshape(B, S, D), *in_order(grad), *in_order(delta), *in_order(new_m), *in_order(new_v))
```

```python
import functools
import math

import jax
import jax.numpy as jnp
from jax import lax
from jax.experimental import pallas as pl
from jax.experimental.pallas import tpu as pltpu

F32 = jnp.float32
BF16 = jnp.bfloat16
MESH = pl.DeviceIdType.MESH

EPS = 1e-6
HEADS = 4
HEAD_DIM = 128
ROPE_DIM = 64
ROPE_HALF = 32
QK_DIM = 192
QK_PAD = 256
LORA = 256
CHUNK = 64
CONV_TAPS = 4
ROPE_THETA = 10000.0
ATTN_SCALE = QK_DIM ** -0.5

LAT_W = 640
GQKV_W = 3 * HEADS * HEAD_DIM
GZ_W = HEADS * HEAD_DIM
GAB_W = 128
PROJ_SPLITS = ((0, LAT_W), (LAT_W, LAT_W + GQKV_W), (LAT_W + GQKV_W, LAT_W + GQKV_W + GZ_W),
               (LAT_W + GQKV_W + GZ_W, LAT_W + GQKV_W + GZ_W + GAB_W))
PROJ_W = PROJ_SPLITS[-1][1]

ADAM_LR = 0.001
ADAM_B1 = 0.9
ADAM_B2 = 0.999
ADAM_EPS = 1e-08
ADAM_WD = 0.01
ADAM_STEP = 10

TOKEN_TILE = 512
FF_TILE = 512
ATTN_TILE = 512
WGRAD_OUT_BYTES = 8 * 1024 * 1024
VMEM_LIMIT = 48 * 1024 * 1024

SHARDED = ("w_in", "w_uq", "w_ukv", "conv_w", "w_out", "w_up", "w_down")
SMALL = ("attn_norm_w", "q_lat_norm_w", "kv_lat_norm_w", "q_norm_w", "k_norm_w", "mla_out_norm_w", "a_log", "dt_bias",
         "gdn_norm_w", "mlp_norm_w")
WEIGHTS = ("attn_norm_w", "w_in", "q_lat_norm_w", "w_uq", "kv_lat_norm_w", "w_ukv", "q_norm_w", "k_norm_w", "mla_out_norm_w",
           "conv_w", "a_log", "dt_bias", "gdn_norm_w", "w_out", "mlp_norm_w", "w_up", "w_down")


def _sds(shape, dtype):
    return jax.ShapeDtypeStruct(shape, dtype)


def _params(semantics):
    return pltpu.CompilerParams(dimension_semantics=semantics, vmem_limit_bytes=VMEM_LIMIT)


def _block(n):
    for b in (512, 256, 128):
        if n % b == 0:
            return b
    return n


def _dg(a, b, ca, cb, prec):
    return lax.dot_general(a, b, (((ca,), (cb,)), ((), ())), precision=prec, preferred_element_type=F32)


def _split_bf16(a):
    hi = a.astype(BF16)
    return hi, (a - hi.astype(F32)).astype(BF16)


def _dot_bf16(a, b, ca, cb):
    return _dg(a.astype(BF16), b.astype(BF16), ca, cb, None)


def _dot_bf16x3(a, b, ca, cb):
    a_hi, a_lo = _split_bf16(a)
    b_hi, b_lo = _split_bf16(b)
    return _dg(jnp.concatenate([a_hi, a_hi, a_lo], axis=ca), jnp.concatenate([b_hi, b_lo, b_hi], axis=cb), ca, cb, None)


def _matmul_family(dot):
    def nn_raw(a, b):
        return dot(a, b, 1, 0)

    def nt_raw(a, b):
        return dot(a, b, 1, 1)

    def tn_raw(a, b):
        return dot(a, b, 0, 0)

    @jax.custom_vjp
    def nn(a, b):
        return nn_raw(a, b)

    nn.defvjp(lambda a, b: (nn_raw(a, b), (a, b)), lambda r, g: (nt_raw(g, r[1]), tn_raw(r[0], g)))

    @jax.custom_vjp
    def nt(a, b):
        return nt_raw(a, b)

    nt.defvjp(lambda a, b: (nt_raw(a, b), (a, b)), lambda r, g: (nn_raw(g, r[1]), tn_raw(g, r[0])))

    @jax.custom_vjp
    def tn(a, b):
        return tn_raw(a, b)

    tn.defvjp(lambda a, b: (tn_raw(a, b), (a, b)), lambda r, g: (nt_raw(r[1], g), nn_raw(r[0], g)))
    return nn, nt, tn


_bf_nn, _bf_nt, _bf_tn = _matmul_family(_dot_bf16)
_hi_nn, _hi_nt, _hi_tn = _matmul_family(_dot_bf16x3)


@jax.custom_vjp
def _lane_halves(x):
    n = x.shape[1] // 2
    return x[:, :n], x[:, n:]


_lane_halves.defvjp(lambda x: (_lane_halves(x), None), lambda _, g: (jnp.concatenate(g, axis=1),))


@jax.custom_vjp
def _row_halves(x):
    n = x.shape[0] // 2
    return x[:n], x[n:]


_row_halves.defvjp(lambda x: (_row_halves(x), None), lambda _, g: (jnp.concatenate(g, axis=0),))


@jax.custom_vjp
def _swap_halves(t):
    return pltpu.roll(t, 64, 1)


_swap_halves.defvjp(lambda t: (pltpu.roll(t, 64, 1), None), lambda _, g: (pltpu.roll(g, 64, 1),))


@functools.partial(jax.custom_vjp, nondiff_argnums=(2,))
def _shift_rows(x, keep, s):
    return pltpu.roll(x, s, 0) * keep


def _shift_rows_fwd(x, keep, s):
    return pltpu.roll(x, s, 0) * keep, keep


def _shift_rows_bwd(s, keep, g):
    return pltpu.roll(g * keep, keep.shape[0] - s, 0), jnp.zeros_like(keep)


_shift_rows.defvjp(_shift_rows_fwd, _shift_rows_bwd)


def _sigmoid(x):
    return 0.5 * jnp.tanh(0.5 * x) + 0.5


def _softplus(x):
    return jnp.maximum(x, 0.0) + jnp.log(1.0 + jnp.exp(jnp.minimum(x, -x)))


def _silu(x):
    return x * _sigmoid(x)


def _rms(x, w, n=None):
    n = x.shape[-1] if n is None else n
    r = lax.rsqrt(jnp.sum(x * x, axis=-1, keepdims=True) * (1.0 / n) + EPS)
    return x * r * w


def _rope(t, cos_f, sin_f):
    return t * cos_f + _swap_halves(t) * sin_f


def _rope_tables(pos_col, freq_row, sign_row):
    ang = pos_col.astype(F32) * freq_row
    return jnp.cos(ang), jnp.sin(ang) * sign_row


def _onehot_row(lane):
    return (lax.broadcasted_iota(jnp.int32, (1, 128), 1) == lane).astype(F32)


def _row_spec(tm, w):
    return pl.BlockSpec((tm, w), lambda i: (i, 0))


def _const_spec(shape):
    return pl.BlockSpec(shape, lambda *_: (0,) * len(shape))


def _in_proj_fwd(x2, w_an, w_in_p):
    T, D = x2.shape
    tm = min(TOKEN_TILE, T)

    def body(x_ref, wn_ref, w_ref, xn_ref, lat_ref, gqkv_ref, gz_ref, gab_ref):
        x = x_ref[...]
        r = lax.rsqrt(jnp.mean(x * x, axis=-1, keepdims=True) + EPS)
        xn = (x * r * wn_ref[...]).astype(BF16)
        xn_ref[...] = xn
        for ref, (a, b) in zip((lat_ref, gqkv_ref, gz_ref, gab_ref), PROJ_SPLITS):
            ref[...] = jnp.dot(xn, w_ref[:, a:b], preferred_element_type=F32)

    widths = [b - a for a, b in PROJ_SPLITS]
    return pl.pallas_call(
        body, grid=(T // tm,),
        in_specs=[_row_spec(tm, D), _const_spec((1, D)), _const_spec((D, PROJ_W))],
        out_specs=[_row_spec(tm, D)] + [_row_spec(tm, w) for w in widths],
        out_shape=[_sds((T, D), BF16)] + [_sds((T, w), F32) for w in widths],
        compiler_params=_params(("parallel",)), name="in_proj_fwd",
    )(x2, w_an, w_in_p)


def _in_proj_bwd(d_lat, d_gqkv, d_gz, d_gab, w_in_p, x2, w_an, dh):
    T, D = x2.shape
    tm = min(TOKEN_TILE, T)

    def body(dl_ref, dq_ref, dz_ref, da_ref, w_ref, x_ref, wn_ref, dh_ref, dx_ref, dp_ref, dwn_ref):
        @pl.when(pl.program_id(0) == 0)
        def _():
            dwn_ref[...] = jnp.zeros_like(dwn_ref)

        dxn = jnp.zeros((tm, D), F32)
        for ref, (a, b) in zip((dl_ref, dq_ref, dz_ref, da_ref), PROJ_SPLITS):
            piece = ref[...].astype(BF16)
            dp_ref[:, a:b] = piece
            dxn += _dg(piece, w_ref[:, a:b], 1, 1, None)
        _, pull = jax.vjp(_rms, x_ref[...], wn_ref[...])
        dx, dwn = pull(dxn)
        dx_ref[...] = dx + dh_ref[...]
        dwn_ref[...] += dwn

    widths = [b - a for a, b in PROJ_SPLITS]
    return pl.pallas_call(
        body, grid=(T // tm,),
        in_specs=[_row_spec(tm, w) for w in widths] + [_const_spec((D, PROJ_W)), _row_spec(tm, D), _const_spec((1, D)),
                                                       _row_spec(tm, D)],
        out_specs=[_row_spec(tm, D), _row_spec(tm, PROJ_W), _const_spec((1, D))],
        out_shape=[_sds((T, D), F32), _sds((T, PROJ_W), BF16), _sds((1, D), F32)],
        compiler_params=_params(("arbitrary",)), name="in_proj_bwd",
    )(d_lat, d_gqkv, d_gz, d_gab, w_in_p, x2, w_an, dh)


def _wgrad(a, b, name, column_shards=1):
    T, k1 = a.shape
    k2 = b.shape[1]
    per_shard = k2 // column_shards
    tt = min(TOKEN_TILE, T)
    b1 = k1
    while b1 * k2 * 4 > WGRAD_OUT_BYTES and b1 % 256 == 0:
        b1 //= 2
    step = _block(per_shard)

    def body(a_ref, b_ref, o_ref):
        @pl.when(pl.program_id(1) == 0)
        def _():
            o_ref[...] = jnp.zeros_like(o_ref)

        a_t = a_ref[...].astype(BF16).T
        for c0 in range(0, k2, step):
            part = jnp.dot(a_t, b_ref[:, c0:c0 + step].astype(BF16), preferred_element_type=F32)
            if column_shards == 1:
                o_ref[:, c0:c0 + step] += part
            else:
                o_ref[c0 // per_shard, :, c0 % per_shard:c0 % per_shard + step] += part

    if column_shards == 1:
        out_spec, out_shape = pl.BlockSpec((b1, k2), lambda i, t: (i, 0)), _sds((k1, k2), F32)
    else:
        out_spec = pl.BlockSpec((column_shards, b1, per_shard), lambda i, t: (0, i, 0))
        out_shape = _sds((column_shards, k1, per_shard), F32)
    return pl.pallas_call(
        body, grid=(k1 // b1, T // tt),
        in_specs=[pl.BlockSpec((tt, b1), lambda i, t: (t, i)), pl.BlockSpec((tt, k2), lambda i, t: (t, 0))],
        out_specs=out_spec, out_shape=out_shape,
        compiler_params=_params(("parallel", "arbitrary")), name=name,
    )(a, b)


def _mla_pre_fn(q_lat, kv_lat, kpe, ln_q, ln_kv, w_list, qn_n, qn_p, kn_n, kn_p, cos_f, sin_f):
    qn = _rms(q_lat, ln_q)
    kvn = _rms(kv_lat, ln_kv)
    kp = _rope(_rms(kpe, kn_p, ROPE_DIM), cos_f, sin_f)
    outs = []
    for h in range(HEADS):
        outs.append(_rms(_bf_nn(qn, w_list[h]), qn_n))
        outs.append(_rope(_rms(_bf_nn(qn, w_list[HEADS + h]), qn_p, ROPE_DIM), cos_f, sin_f))
        outs.append(_rms(_bf_nn(kvn, w_list[2 * HEADS + h]), kn_n))
        outs.append(_bf_nn(kvn, w_list[3 * HEADS + h]))
    return tuple(outs) + (kp,)


def _mla_pre_operands(lat_ref, pos_ref, ln_ref, w_ref, nw_ref, rope_ref):
    cos_f, sin_f = _rope_tables(pos_ref[...], rope_ref[0:1, :], rope_ref[1:2, :])
    diff = (lat_ref[:, 0:LORA], lat_ref[:, LORA:2 * LORA], lat_ref[:, 2 * LORA:LAT_W], ln_ref[0:1, :], ln_ref[1:2, :],
            [w_ref[i].astype(F32) for i in range(4 * HEADS)], nw_ref[0:1, :], nw_ref[1:2, :], nw_ref[2:3, :], nw_ref[3:4, :])
    return diff, cos_f, sin_f


def _mla_pre_fwd(lat, pos, ln_w, w_mla, nw, rope_rows):
    T = lat.shape[0]
    tm = min(TOKEN_TILE, T)

    def body(lat_ref, pos_ref, ln_ref, w_ref, nw_ref, rope_ref, q_ref, k_ref, v_ref):
        diff, cos_f, sin_f = _mla_pre_operands(lat_ref, pos_ref, ln_ref, w_ref, nw_ref, rope_ref)
        outs = _mla_pre_fn(*diff, cos_f, sin_f)
        kp = outs[-1].astype(BF16)
        for h in range(HEADS):
            q_n, q_p, k_n, v = outs[4 * h:4 * h + 4]
            q_ref[:, h * QK_PAD:h * QK_PAD + HEAD_DIM] = q_n.astype(BF16)
            q_ref[:, h * QK_PAD + HEAD_DIM:(h + 1) * QK_PAD] = q_p.astype(BF16)
            k_ref[:, h * QK_PAD:h * QK_PAD + HEAD_DIM] = k_n.astype(BF16)
            k_ref[:, h * QK_PAD + HEAD_DIM:(h + 1) * QK_PAD] = kp
            v_ref[:, h * HEAD_DIM:(h + 1) * HEAD_DIM] = v.astype(BF16)

    return pl.pallas_call(
        body, grid=(T // tm,),
        in_specs=[_row_spec(tm, LAT_W), _row_spec(tm, 1), _const_spec((2, LORA)), _const_spec((4 * HEADS, LORA, 128)),
                  _const_spec((8, 128)), _const_spec((8, 128))],
        out_specs=[_row_spec(tm, HEADS * QK_PAD), _row_spec(tm, HEADS * QK_PAD), _row_spec(tm, HEADS * HEAD_DIM)],
        out_shape=[_sds((T, HEADS * QK_PAD), BF16), _sds((T, HEADS * QK_PAD), BF16), _sds((T, HEADS * HEAD_DIM), BF16)],
        compiler_params=_params(("parallel",)), name="mla_pre_fwd",
    )(lat, pos, ln_w, w_mla, nw, rope_rows)


def _mla_pre_bwd(lat, pos, ln_w, w_mla, nw, rope_rows, dq, dk, dv):
    T = lat.shape[0]
    tm = min(TOKEN_TILE, T)

    def body(lat_ref, pos_ref, ln_ref, w_ref, nw_ref, rope_ref, dq_ref, dk_ref, dv_ref, dlat_ref, dln_ref, dw_ref, dnw_ref):
        @pl.when(pl.program_id(0) == 0)
        def _():
            dln_ref[...] = jnp.zeros_like(dln_ref)
            dw_ref[...] = jnp.zeros_like(dw_ref)
            dnw_ref[...] = jnp.zeros_like(dnw_ref)

        diff, cos_f, sin_f = _mla_pre_operands(lat_ref, pos_ref, ln_ref, w_ref, nw_ref, rope_ref)
        _, pull = jax.vjp(lambda *a: _mla_pre_fn(*a, cos_f, sin_f), *diff)
        cts = []
        d_kp = jnp.zeros((tm, 128), F32)
        for h in range(HEADS):
            cts.append(dq_ref[:, h * QK_PAD:h * QK_PAD + HEAD_DIM])
            cts.append(dq_ref[:, h * QK_PAD + HEAD_DIM:(h + 1) * QK_PAD])
            cts.append(dk_ref[:, h * QK_PAD:h * QK_PAD + HEAD_DIM])
            cts.append(dv_ref[:, h * HEAD_DIM:(h + 1) * HEAD_DIM])
            d_kp += dk_ref[:, h * QK_PAD + HEAD_DIM:(h + 1) * QK_PAD]
        d_ql, d_kvl, d_kpe, d_lnq, d_lnkv, d_w, d_qn_n, d_qn_p, d_kn_n, d_kn_p = pull(tuple(cts) + (d_kp,))
        dlat_ref[:, 0:LORA] = d_ql
        dlat_ref[:, LORA:2 * LORA] = d_kvl
        dlat_ref[:, 2 * LORA:LAT_W] = d_kpe
        dln_ref[0:1, :] += d_lnq
        dln_ref[1:2, :] += d_lnkv
        for i in range(4 * HEADS):
            dw_ref[i] += d_w[i]
        for i, d in enumerate((d_qn_n, d_qn_p, d_kn_n, d_kn_p)):
            dnw_ref[i:i + 1, :] += d

    return pl.pallas_call(
        body, grid=(T // tm,),
        in_specs=[_row_spec(tm, LAT_W), _row_spec(tm, 1), _const_spec((2, LORA)), _const_spec((4 * HEADS, LORA, 128)),
                  _const_spec((8, 128)), _const_spec((8, 128)),
                  _row_spec(tm, HEADS * QK_PAD), _row_spec(tm, HEADS * QK_PAD), _row_spec(tm, HEADS * HEAD_DIM)],
        out_specs=[_row_spec(tm, LAT_W), _const_spec((2, LORA)), _const_spec((4 * HEADS, LORA, 128)), _const_spec((8, 128))],
        out_shape=[_sds((T, LAT_W), F32), _sds((2, LORA), F32), _sds((4 * HEADS, LORA, 128), F32), _sds((8, 128), F32)],
        compiler_params=_params(("arbitrary",)), name="mla_pre_bwd",
    )(lat, pos, ln_w, w_mla, nw, rope_rows, dq, dk, dv)


def _causal_mask(i, j, tq, tk):
    row = i * tq + lax.broadcasted_iota(jnp.int32, (tq, tk), 0)
    col = j * tk + lax.broadcasted_iota(jnp.int32, (tq, tk), 1)
    return col <= row


def _attn_fwd(q, k, v):
    B, S, _ = q.shape
    t = min(ATTN_TILE, S)

    def body(q_ref, k_ref, v_ref, o_ref, lse_ref):
        i = pl.program_id(2)
        qb = q_ref[0]

        def step(j, carry, diagonal):
            m, l, acc = carry
            rows = pl.ds(pl.multiple_of(j * t, t), t)
            s = _dg(qb, k_ref[0, rows, :], 1, 1, None) * ATTN_SCALE
            if diagonal:
                s = jnp.where(_causal_mask(0, 0, t, t), s, -1e30)
            m_new = jnp.maximum(m, jnp.max(s, axis=-1, keepdims=True))
            p = jnp.exp(s - m_new)
            alpha = jnp.exp(m - m_new)
            l = alpha * l + jnp.sum(p, axis=-1, keepdims=True)
            acc = alpha * acc + jnp.dot(p.astype(BF16), v_ref[0, rows, :], preferred_element_type=F32)
            return m_new, l, acc

        init = (jnp.full((t, 1), -1e30, F32), jnp.zeros((t, 1), F32), jnp.zeros((t, HEAD_DIM), F32))
        below = lax.fori_loop(0, i, lambda j, carry: step(j, carry, False), init)
        m, l, acc = step(i, below, True)
        o_ref[0] = acc / l
        lse_ref[0, 0] = m + jnp.log(l)

    return pl.pallas_call(
        body, grid=(B, HEADS, S // t),
        in_specs=[pl.BlockSpec((1, t, QK_PAD), lambda b, h, i: (b, i, h)),
                  pl.BlockSpec((1, S, QK_PAD), lambda b, h, i: (b, 0, h)),
                  pl.BlockSpec((1, S, HEAD_DIM), lambda b, h, i: (b, 0, h))],
        out_specs=[pl.BlockSpec((1, t, HEAD_DIM), lambda b, h, i: (b, i, h)),
                   pl.BlockSpec((1, 1, t, 1), lambda b, h, i: (b, h, i, 0))],
        out_shape=[_sds((B, S, HEADS * HEAD_DIM), F32), _sds((B, HEADS, S, 1), F32)],
        compiler_params=_params(("parallel", "parallel", "parallel")), name="attn_fwd",
    )(q, k, v)


def _attn_bwd(q, k, v, o, lse, do, halves):
    B, S, _ = q.shape
    t = min(ATTN_TILE, S)
    nq = S // t
    ns = len(halves)

    def body(*refs):
        q_ref, k_ref, v_ref, o_ref, lse_ref, do_ref = refs[:6]
        src_refs = refs[6:6 + ns]
        dq_ref, dk_ref, dv_ref = refs[6 + ns:9 + ns]
        dst_refs = refs[9 + ns:9 + 2 * ns]
        dsum_ref, send_sems, recv_sems, local_sems = refs[9 + 2 * ns:]
        b, h, j = pl.program_id(0), pl.program_id(1), pl.program_id(2)

        @pl.when((b == 0) & (h == 0) & (j == 0))
        def _():
            for start in _swap_copies(src_refs, dst_refs, send_sems, recv_sems, local_sems)[0]:
                start()

        @pl.when(j == 0)
        def _():
            dq_ref[...] = jnp.zeros_like(dq_ref)
            dsum_ref[...] = jnp.sum(do_ref[0] * o_ref[0], axis=-1, keepdims=True)

        kb = k_ref[0]
        vb = v_ref[0]

        def step(i, carry, diagonal):
            dk, dv = carry
            rows = pl.ds(pl.multiple_of(i * t, t), t)
            qb = q_ref[0, rows, :]
            dob = do_ref[0, rows, :].astype(BF16)
            s = _dg(qb, kb, 1, 1, None) * ATTN_SCALE
            p = jnp.exp(s - lse_ref[0, 0, rows, :])
            if diagonal:
                p = jnp.where(_causal_mask(0, 0, t, t), p, 0.0)
            pb = p.astype(BF16)
            dv = dv + _dg(pb, dob, 0, 0, None)
            dp = _dg(dob, vb, 1, 1, None)
            ds = (p * (dp - dsum_ref[rows, :]) * ATTN_SCALE).astype(BF16)
            dq_ref[0, rows, :] += jnp.dot(ds, kb, preferred_element_type=F32)
            dk = dk + _dg(ds, qb, 0, 0, None)
            return dk, dv

        on_diagonal = step(j, (jnp.zeros((t, QK_PAD), F32), jnp.zeros((t, HEAD_DIM), F32)), True)
        dk, dv = lax.fori_loop(j + 1, nq, lambda i, carry: step(i, carry, False), on_diagonal)
        dk_ref[0] = dk
        dv_ref[0] = dv

        @pl.when((b == B - 1) & (h == HEADS - 1) & (j == nq - 1))
        def _():
            for wait in _swap_copies(src_refs, dst_refs, send_sems, recv_sems, local_sems)[1]:
                wait()

    return pl.pallas_call(
        body, grid=(B, HEADS, nq),
        in_specs=[pl.BlockSpec((1, S, QK_PAD), lambda b, h, j: (b, 0, h)),
                  pl.BlockSpec((1, t, QK_PAD), lambda b, h, j: (b, j, h)),
                  pl.BlockSpec((1, t, HEAD_DIM), lambda b, h, j: (b, j, h)),
                  pl.BlockSpec((1, S, HEAD_DIM), lambda b, h, j: (b, 0, h)),
                  pl.BlockSpec((1, 1, S, 1), lambda b, h, j: (b, h, 0, 0)),
                  pl.BlockSpec((1, S, HEAD_DIM), lambda b, h, j: (b, 0, h))] + [_ANY] * ns,
        out_specs=[pl.BlockSpec((1, S, QK_PAD), lambda b, h, j: (b, 0, h)),
                   pl.BlockSpec((1, t, QK_PAD), lambda b, h, j: (b, j, h)),
                   pl.BlockSpec((1, t, HEAD_DIM), lambda b, h, j: (b, j, h))] + [_ANY] * ns,
        out_shape=[_sds((B, S, HEADS * QK_PAD), F32), _sds((B, S, HEADS * QK_PAD), F32), _sds((B, S, HEADS * HEAD_DIM), F32)]
                  + [_sds((2,) + s.shape, s.dtype) for s in halves],
        scratch_shapes=[pltpu.VMEM((S, 1), F32)] + _swap_scratch(ns),
        compiler_params=_params(("arbitrary", "arbitrary", "arbitrary")), name="attn_bwd",
    )(q, k, v, o, lse, do, *halves)


def _gdn_pre_fn(xq, xk, xv, wq, wk, wv, keeps):
    def conv_silu(x, w):
        acc = x * w[3]
        for s in (1, 2, 3):
            acc = acc + _shift_rows(x, keeps[s - 1], s) * w[3 - s]
        return _silu(acc)

    def l2(x):
        return x * lax.rsqrt(jnp.sum(x * x, axis=-1, keepdims=True) + EPS)

    return l2(conv_silu(xq, wq)) * (HEAD_DIM ** -0.5), l2(conv_silu(xk, wk)), conv_silu(xv, wv)


def _gdn_pre_specs(S):
    x_specs = [pl.BlockSpec((1, S, HEAD_DIM), lambda h, b, g=g: (b, 0, g * HEADS + h)) for g in range(3)]
    w_specs = [pl.BlockSpec((CONV_TAPS, HEAD_DIM), lambda h, b, g=g: (0, g * HEADS + h)) for g in range(3)]
    out_spec = pl.BlockSpec((1, S, HEAD_DIM), lambda h, b: (b, 0, h))
    return x_specs, w_specs, out_spec


def _row_keeps(S):
    t = lax.broadcasted_iota(jnp.int32, (S, HEAD_DIM), 0)
    return [(t >= s).astype(F32) for s in (1, 2, 3)]


def _gdn_pre_fwd(gqkv, conv_w):
    B, S, _ = gqkv.shape
    x_specs, w_specs, out_spec = _gdn_pre_specs(S)

    def body(xq_ref, xk_ref, xv_ref, wq_ref, wk_ref, wv_ref, q_ref, k_ref, v_ref):
        taps = [[w[i:i + 1, :] for i in range(CONV_TAPS)] for w in (wq_ref, wk_ref, wv_ref)]
        q, k, v = _gdn_pre_fn(xq_ref[0], xk_ref[0], xv_ref[0], *taps, _row_keeps(S))
        q_ref[0], k_ref[0], v_ref[0] = q, k, v

    return pl.pallas_call(
        body, grid=(HEADS, B), in_specs=x_specs + w_specs, out_specs=[out_spec] * 3,
        out_shape=[_sds((B, S, HEADS * HEAD_DIM), F32)] * 3,
        compiler_params=_params(("parallel", "parallel")), name="gdn_pre_fwd",
    )(gqkv, gqkv, gqkv, conv_w, conv_w, conv_w)


def _gdn_pre_bwd(gqkv, conv_w, dq, dk, dv):
    B, S, _ = gqkv.shape
    x_specs, w_specs, out_spec = _gdn_pre_specs(S)
    dw_spec = pl.BlockSpec((CONV_TAPS, HEAD_DIM), lambda h, b: (0, h))

    def body(xq_ref, xk_ref, xv_ref, wq_ref, wk_ref, wv_ref, dq_ref, dk_ref, dv_ref,
             dxq_ref, dxk_ref, dxv_ref, dwq_ref, dwk_ref, dwv_ref):
        @pl.when(pl.program_id(1) == 0)
        def _():
            for r in (dwq_ref, dwk_ref, dwv_ref):
                r[...] = jnp.zeros_like(r)

        taps = [[w[i:i + 1, :] for i in range(CONV_TAPS)] for w in (wq_ref, wk_ref, wv_ref)]
        keeps = _row_keeps(S)
        _, pull = jax.vjp(lambda *a: _gdn_pre_fn(*a, keeps), xq_ref[0], xk_ref[0], xv_ref[0], *taps)
        dxq, dxk, dxv, dwq, dwk, dwv = pull((dq_ref[0], dk_ref[0], dv_ref[0]))
        dxq_ref[0], dxk_ref[0], dxv_ref[0] = dxq, dxk, dxv
        for ref, dw in ((dwq_ref, dwq), (dwk_ref, dwk), (dwv_ref, dwv)):
            for i in range(CONV_TAPS):
                ref[i:i + 1, :] += dw[i]

    hw = HEADS * HEAD_DIM
    return pl.pallas_call(
        body, grid=(HEADS, B), in_specs=x_specs + w_specs + [out_spec] * 3,
        out_specs=[out_spec] * 3 + [dw_spec] * 3,
        out_shape=[_sds((B, S, hw), F32)] * 3 + [_sds((CONV_TAPS, hw), F32)] * 3,
        compiler_params=_params(("parallel", "arbitrary")), name="gdn_pre_bwd",
    )(gqkv, gqkv, gqkv, conv_w, conv_w, conv_w, dq, dk, dv)


def _chunk_masks():
    i = lax.broadcasted_iota(jnp.int32, (CHUNK, CHUNK), 0)
    j = lax.broadcasted_iota(jnp.int32, (CHUNK, CHUNK), 1)
    lower, after = (j <= i).astype(F32), (j > i).astype(F32)
    return {"le": lower, "le_gt": jnp.concatenate([lower, after], axis=0), "strict": (j < i).astype(F32)}


def _gdn_chunk_fn(h, masks):
    pick_a, pick_b = _onehot_row(h), _onehot_row(HEADS + h)
    lower, lower_after, strict = masks["le"], masks["le_gt"], masks["strict"]
    ones_row = jnp.ones((1, HEAD_DIM), F32)

    def f(q, k, v, gab, a_row, dt_row, state):
        ga = jnp.sum(gab * pick_a, axis=1, keepdims=True)
        gb = jnp.sum(gab * pick_b, axis=1, keepdims=True)
        a_log = jnp.sum(a_row * pick_a, axis=1, keepdims=True)
        dt_bias = jnp.sum(dt_row * pick_a, axis=1, keepdims=True)
        beta = _sigmoid(gb)
        g = -jnp.exp(a_log) * _softplus(ga + dt_bias)
        g_wide = g * ones_row
        cum, rest = _row_halves(_hi_nn(lower_after, g_wide))
        total = jnp.sum(g_wide, axis=0, keepdims=True)
        diff = _hi_nn(lower, g * strict)
        decay = lower * jnp.exp(diff)
        e_cum = jnp.exp(cum)
        lmat = strict * (beta * _bf_nt(k, k) * decay)
        rhs = jnp.concatenate([v * beta, k * (beta * e_cum)], axis=1)
        rhs = rhs - _hi_nn(lmat, rhs)
        power = lmat
        for _ in range(5):
            power = _hi_nn(power, power)
            rhs = rhs + _hi_nn(power, rhs)
        u, w = _lane_halves(rhs)
        attn = _bf_nt(q, k) * decay
        v_new = u - _bf_nn(w, state)
        o = _bf_nn(q * e_cum, state) + _bf_nn(attn, v_new)
        new_state = state * jnp.exp(total) + _bf_tn(k * jnp.exp(rest), v_new)
        return o, new_state

    return f


def _gdn_chunk_fwd(q, k, v, gab, scal, shards):
    B, S, W = q.shape
    N = S // CHUNK
    ns = len(shards)

    def body(*refs):
        q_ref, k_ref, v_ref, gab_ref, sc_ref = refs[:5]
        src_refs = refs[5:5 + ns]
        o_ref, st_ref = refs[5 + ns:7 + ns]
        dst_refs = refs[7 + ns:7 + 2 * ns]
        state_ref, send_sems, recv_sems, local_sems = refs[7 + 2 * ns:]
        n = pl.program_id(0)

        @pl.when(n == 0)
        def _():
            for start in _gather_copies(src_refs, dst_refs, send_sems, recv_sems, local_sems)[0]:
                start()
            state_ref[...] = jnp.zeros_like(state_ref)

        masks = _chunk_masks()
        for b in range(B):
            for h in range(HEADS):
                lanes = slice(h * HEAD_DIM, (h + 1) * HEAD_DIM)
                state = state_ref[b * HEADS + h]
                st_ref[b, 0, h] = state
                o, new_state = _gdn_chunk_fn(h, masks)(q_ref[b, :, lanes], k_ref[b, :, lanes], v_ref[b, :, lanes], gab_ref[b],
                                                       sc_ref[0:1, :], sc_ref[1:2, :], state)
                o_ref[b, :, lanes] = o
                state_ref[b * HEADS + h] = new_state

        @pl.when(n == N - 1)
        def _():
            for wait in _gather_copies(src_refs, dst_refs, send_sems, recv_sems, local_sems)[1]:
                wait()

    seq = pl.BlockSpec((B, CHUNK, W), lambda n: (0, n, 0))
    return pl.pallas_call(
        body, grid=(N,),
        in_specs=[seq, seq, seq, pl.BlockSpec((B, CHUNK, GAB_W), lambda n: (0, n, 0)), _const_spec((8, 128))] + [_ANY] * ns,
        out_specs=[seq, pl.BlockSpec((B, 1, HEADS, HEAD_DIM, HEAD_DIM), lambda n: (0, n, 0, 0, 0))] + [_ANY] * ns,
        out_shape=[_sds((B, S, W), F32), _sds((B, N, HEADS, HEAD_DIM, HEAD_DIM), F32)] + [_sds((4,) + s.shape, s.dtype) for s in shards],
        scratch_shapes=[pltpu.VMEM((B * HEADS, HEAD_DIM, HEAD_DIM), F32)] + _gather_scratch(ns),
        compiler_params=_params(("arbitrary",)), name="gdn_chunk_fwd",
    )(q, k, v, gab, scal, *shards)


def _gdn_chunk_bwd(q, k, v, gab, scal, states, do, partials):
    B, S, W = q.shape
    N = S // CHUNK
    ns = len(partials)

    def body(*refs):
        q_ref, k_ref, v_ref, gab_ref, sc_ref, st_ref, do_ref = refs[:7]
        src_refs = refs[7:7 + ns]
        dq_ref, dk_ref, dv_ref, dgab_ref, dsc_ref = refs[7 + ns:12 + ns]
        dst_refs = refs[12 + ns:12 + 2 * ns]
        dstate_ref, send_sems, recv_sems, local_sems = refs[12 + 2 * ns:]
        n = pl.program_id(0)

        @pl.when(n == 0)
        def _():
            for start in _scatter_copies(src_refs, dst_refs, send_sems, recv_sems, local_sems)[0]:
                start()
            dstate_ref[...] = jnp.zeros_like(dstate_ref)
            dsc_ref[...] = jnp.zeros_like(dsc_ref)

        masks = _chunk_masks()
        d_a = jnp.zeros((1, 128), F32)
        d_dt = jnp.zeros((1, 128), F32)
        for b in range(B):
            d_gab = jnp.zeros((CHUNK, GAB_W), F32)
            for h in range(HEADS):
                lanes = slice(h * HEAD_DIM, (h + 1) * HEAD_DIM)
                _, pull = jax.vjp(_gdn_chunk_fn(h, masks), q_ref[b, :, lanes], k_ref[b, :, lanes], v_ref[b, :, lanes], gab_ref[b],
                                  sc_ref[0:1, :], sc_ref[1:2, :], st_ref[b, 0, h])
                dq, dk, dv, dg, da, ddt, dstate = pull((do_ref[b, :, lanes], dstate_ref[b * HEADS + h]))
                dq_ref[b, :, lanes] = dq
                dk_ref[b, :, lanes] = dk
                dv_ref[b, :, lanes] = dv
                dstate_ref[b * HEADS + h] = dstate
                d_gab, d_a, d_dt = d_gab + dg, d_a + da, d_dt + ddt
            dgab_ref[b] = d_gab
        dsc_ref[0:1, :] += d_a
        dsc_ref[1:2, :] += d_dt

        @pl.when(n == N - 1)
        def _():
            for wait in _scatter_copies(src_refs, dst_refs, send_sems, recv_sems, local_sems)[1]:
                wait()

    seq = pl.BlockSpec((B, CHUNK, W), lambda n: (0, N - 1 - n, 0))
    gab_spec = pl.BlockSpec((B, CHUNK, GAB_W), lambda n: (0, N - 1 - n, 0))
    return pl.pallas_call(
        body, grid=(N,),
        in_specs=[seq, seq, seq, gab_spec, _const_spec((8, 128)),
                  pl.BlockSpec((B, 1, HEADS, HEAD_DIM, HEAD_DIM), lambda n: (0, N - 1 - n, 0, 0, 0)), seq] + [_ANY] * ns,
        out_specs=[seq, seq, seq, gab_spec, _const_spec((8, 128))] + [_ANY] * ns,
        out_shape=[_sds((B, S, W), F32)] * 3 + [_sds((B, S, GAB_W), F32), _sds((8, 128), F32)] + [_scattered_shape(p) for p in partials],
        scratch_shapes=[pltpu.VMEM((B * HEADS, HEAD_DIM, HEAD_DIM), F32)] + _scatter_scratch(ns),
        compiler_params=_params(("arbitrary",)), name="gdn_chunk_bwd",
    )(q, k, v, gab, scal, states, do, *partials)


def _mix_fn(ao, go, gz, w_mla, w_gdn):
    return tuple(_rms(ao[h], w_mla[h]) for h in range(HEADS)) + tuple(_rms(go[h], w_gdn) * _silu(gz[h]) for h in range(HEADS))


def _mix_operands(ao_ref, go_ref, gz_ref, nw_ref):
    blocks = lambda ref: [ref[:, h * HEAD_DIM:(h + 1) * HEAD_DIM] for h in range(HEADS)]
    return blocks(ao_ref), blocks(go_ref), blocks(gz_ref), [nw_ref[h:h + 1, :] for h in range(HEADS)], nw_ref[HEADS:HEADS + 1, :]


def _mix_fwd(ao, go, gz, nw, w_out, x2):
    T, D = x2.shape
    tm = min(TOKEN_TILE, T)
    MW = 2 * HEADS * HEAD_DIM

    def body(ao_ref, go_ref, gz_ref, nw_ref, w_ref, x_ref, mix_ref, h_ref):
        outs = _mix_fn(*_mix_operands(ao_ref, go_ref, gz_ref, nw_ref))
        for i, piece in enumerate(outs):
            mix_ref[:, i * HEAD_DIM:(i + 1) * HEAD_DIM] = piece.astype(BF16)
        h_ref[...] = x_ref[...] + jnp.dot(mix_ref[...], w_ref[...], preferred_element_type=F32)

    half = HEADS * HEAD_DIM
    return pl.pallas_call(
        body, grid=(T // tm,),
        in_specs=[_row_spec(tm, half), _row_spec(tm, half), _row_spec(tm, half), _const_spec((8, 128)), _const_spec((MW, D)),
                  _row_spec(tm, D)],
        out_specs=[_row_spec(tm, MW), _row_spec(tm, D)],
        out_shape=[_sds((T, MW), BF16), _sds((T, D), F32)],
        compiler_params=_params(("parallel",)), name="mix_fwd",
    )(ao, go, gz, nw, w_out, x2)


def _mix_bwd(ao, go, gz, nw, w_out, dh):
    T, D = dh.shape
    tm = min(TOKEN_TILE, T)
    MW = 2 * HEADS * HEAD_DIM
    half = HEADS * HEAD_DIM

    def body(ao_ref, go_ref, gz_ref, nw_ref, w_ref, dh_ref, dao_ref, dgo_ref, dgz_ref, dnw_ref):
        @pl.when(pl.program_id(0) == 0)
        def _():
            dnw_ref[...] = jnp.zeros_like(dnw_ref)

        d_mix = _dg(dh_ref[...].astype(BF16), w_ref[...], 1, 1, None)
        cts = tuple(d_mix[:, i * HEAD_DIM:(i + 1) * HEAD_DIM] for i in range(2 * HEADS))
        _, pull = jax.vjp(_mix_fn, *_mix_operands(ao_ref, go_ref, gz_ref, nw_ref))
        d_ao, d_go, d_gz, d_wm, d_wg = pull(cts)
        for h in range(HEADS):
            lanes = slice(h * HEAD_DIM, (h + 1) * HEAD_DIM)
            dao_ref[:, lanes] = d_ao[h]
            dgo_ref[:, lanes] = d_go[h]
            dgz_ref[:, lanes] = d_gz[h]
            dnw_ref[h:h + 1, :] += d_wm[h]
        dnw_ref[HEADS:HEADS + 1, :] += d_wg

    return pl.pallas_call(
        body, grid=(T // tm,),
        in_specs=[_row_spec(tm, half), _row_spec(tm, half), _row_spec(tm, half), _const_spec((8, 128)), _const_spec((MW, D)),
                  _row_spec(tm, D)],
        out_specs=[_row_spec(tm, half)] * 3 + [_const_spec((8, 128))],
        out_shape=[_sds((T, half), F32)] * 3 + [_sds((8, 128), F32)],
        compiler_params=_params(("arbitrary",)), name="mix_bwd",
    )(ao, go, gz, nw, w_out, dh)


def _up_spec(w_up, tf):
    per_shard = w_up.shape[2] // tf
    return pl.BlockSpec((None, w_up.shape[1], tf), lambda i, j: (j // per_shard, 0, j % per_shard))


def _mlp_fwd(h2, w_mn, w_up, w_down, target):
    T, D = h2.shape
    FF = w_down.shape[0]
    tm, tf = min(TOKEN_TILE, T), min(FF_TILE, w_up.shape[2])
    nf = FF // tf

    def body(h_ref, wn_ref, wu_ref, wd_ref, t_ref, hn_ref, dy_ref, sq_ref, acc_ref):
        j = pl.program_id(1)

        @pl.when(j == 0)
        def _():
            hn_ref[...] = _rms(h_ref[...], wn_ref[...]).astype(BF16)
            acc_ref[...] = jnp.zeros_like(acc_ref)

        up = jnp.dot(hn_ref[...], wu_ref[...], preferred_element_type=F32)
        act = jnp.square(jnp.maximum(up, 0.0)).astype(BF16)
        acc_ref[...] += jnp.dot(act, wd_ref[...], preferred_element_type=F32)

        @pl.when(j == nf - 1)
        def _():
            err = h_ref[...] + acc_ref[...] - t_ref[...]
            dy_ref[...] = err * (1.0 / D)
            sq_ref[...] = jnp.zeros_like(sq_ref) + jnp.sum(err * err)

    tok = lambda w: pl.BlockSpec((tm, w), lambda i, j: (i, 0))
    return pl.pallas_call(
        body, grid=(T // tm, nf),
        in_specs=[tok(D), _const_spec((1, D)), _up_spec(w_up, tf), pl.BlockSpec((tf, D), lambda i, j: (j, 0)), tok(D)],
        out_specs=[tok(D), tok(D), pl.BlockSpec((1, 8, 128), lambda i, j: (i, 0, 0))],
        out_shape=[_sds((T, D), BF16), _sds((T, D), F32), _sds((T // tm, 8, 128), F32)],
        scratch_shapes=[pltpu.VMEM((tm, D), F32)],
        compiler_params=_params(("parallel", "arbitrary")), name="mlp_fwd",
    )(h2, w_mn, w_up, w_down, target)


def _mlp_bwd(h2, w_mn, hn, w_up, w_down, dy):
    T, D = h2.shape
    FF = w_down.shape[0]
    tm, tf = min(TOKEN_TILE, T), min(FF_TILE, w_up.shape[2])
    nf = FF // tf

    def body(h_ref, wn_ref, hn_ref, wu_ref, wd_ref, dy_ref, dh_ref, act_ref, dup_ref, dwn_ref, acc_ref):
        i, j = pl.program_id(0), pl.program_id(1)

        @pl.when((i == 0) & (j == 0))
        def _():
            dwn_ref[...] = jnp.zeros_like(dwn_ref)

        @pl.when(j == 0)
        def _():
            acc_ref[...] = jnp.zeros_like(acc_ref)

        r = jnp.maximum(jnp.dot(hn_ref[...], wu_ref[...], preferred_element_type=F32), 0.0)
        act_ref[...] = (r * r).astype(BF16)
        d_act = _dg(dy_ref[...].astype(BF16), wd_ref[...], 1, 1, None)
        d_up = (d_act * (2.0 * r)).astype(BF16)
        dup_ref[...] = d_up
        acc_ref[...] += _dg(d_up, wu_ref[...], 1, 1, None)

        @pl.when(j == nf - 1)
        def _():
            _, pull = jax.vjp(_rms, h_ref[...], wn_ref[...])
            dh, dwn = pull(acc_ref[...])
            dh_ref[...] = dh + dy_ref[...]
            dwn_ref[...] += dwn

    tok = lambda w: pl.BlockSpec((tm, w), lambda i, j: (i, 0))
    ff = pl.BlockSpec((tm, tf), lambda i, j: (i, j))
    return pl.pallas_call(
        body, grid=(T // tm, nf),
        in_specs=[tok(D), _const_spec((1, D)), tok(D), _up_spec(w_up, tf), pl.BlockSpec((tf, D), lambda i, j: (j, 0)), tok(D)],
        out_specs=[tok(D), ff, ff, _const_spec((1, D))],
        out_shape=[_sds((T, D), F32), _sds((T, FF), BF16), _sds((T, FF), BF16), _sds((1, D), F32)],
        scratch_shapes=[pltpu.VMEM((tm, D), F32)],
        compiler_params=_params(("arbitrary", "arbitrary")), name="mlp_bwd",
    )(h2, w_mn, hn, w_up, w_down, dy)


def _rope_pad(a):
    z = jnp.zeros(a.shape[:-1] + (ROPE_HALF,), a.dtype)
    return jnp.concatenate([a[..., :ROPE_HALF], z, a[..., ROPE_HALF:], z], axis=-1)


def _rope_unpad(a):
    return jnp.concatenate([a[..., :ROPE_HALF], a[..., 2 * ROPE_HALF:3 * ROPE_HALF]], axis=-1)


_G0 = 2 * LORA + ROPE_DIM
_GZ0 = _G0 + GQKV_W
_GA0 = _GZ0 + GZ_W


def _widen_w_in(w):
    pad = jnp.zeros((w.shape[0], GAB_W - 2 * HEADS), w.dtype)
    return jnp.concatenate([w[:, :2 * LORA], _rope_pad(w[:, 2 * LORA:_G0]), w[:, _G0:_GA0], w[:, _GA0:], pad], axis=1)


def _narrow_w_in(w):
    return jnp.concatenate([w[:, :2 * LORA], _rope_unpad(w[:, 2 * LORA:LAT_W]), w[:, LAT_W:PROJ_SPLITS[2][1]],
                            w[:, PROJ_SPLITS[3][0]:PROJ_SPLITS[3][0] + 2 * HEADS]], axis=1)


def _stack_mla(w_uq, w_ukv):
    uq = w_uq.reshape(LORA, HEADS, QK_DIM)
    ukv = w_ukv.reshape(LORA, HEADS, 2 * HEAD_DIM)
    parts = [uq[:, :, :HEAD_DIM], _rope_pad(uq[:, :, HEAD_DIM:]), ukv[:, :, :HEAD_DIM], ukv[:, :, HEAD_DIM:]]
    return jnp.concatenate([p.transpose(1, 0, 2) for p in parts], axis=0)


def _unstack_mla(w):
    p = [w[i * HEADS:(i + 1) * HEADS].transpose(1, 0, 2) for i in range(4)]
    uq = jnp.concatenate([p[0], _rope_unpad(p[1])], axis=-1).reshape(LORA, HEADS * QK_DIM)
    ukv = jnp.concatenate([p[2], p[3]], axis=-1).reshape(LORA, HEADS * 2 * HEAD_DIM)
    return uq, ukv


def _rows8(rows):
    a = jnp.concatenate(rows, axis=0)
    return jnp.pad(a, ((0, 8 - a.shape[0]), (0, 0)))


def _qk_norm_rows(q_norm_w, k_norm_w):
    return _rows8([q_norm_w[:, :HEAD_DIM], _rope_pad(q_norm_w[:, HEAD_DIM:]), k_norm_w[:, :HEAD_DIM], _rope_pad(k_norm_w[:, HEAD_DIM:])])


def _rope_rows():
    inv_freq = ROPE_THETA ** (-jnp.arange(ROPE_HALF, dtype=F32) / ROPE_HALF)
    z = jnp.zeros((ROPE_HALF,), F32)
    freq = jnp.concatenate([inv_freq, z, inv_freq, z])
    sign = jnp.concatenate([-jnp.ones((ROPE_HALF,), F32), z, jnp.ones((ROPE_HALF,), F32), z])
    return _rows8([freq[None], sign[None]])


def _round_up(n, m):
    return -(-n // m) * m


def _column_shards(a):
    return a.reshape(a.shape[0], 4, a.shape[1] // 4).transpose(1, 0, 2)


def _from_column_shards(a):
    return a.transpose(1, 0, 2).reshape(a.shape[1], 4 * a.shape[2])


def _pack_small(arrays):
    rows = [jnp.pad(a.reshape(-1), (0, _round_up(a.size, 128) - a.size)).reshape(-1, 128) for a in arrays]
    packed = jnp.concatenate(rows, axis=0)
    return jnp.pad(packed, ((0, _round_up(packed.shape[0], 8) - packed.shape[0]), (0, 0)))


def _unpack_small(packed, shapes):
    out, r = [], 0
    for s in shapes:
        n = math.prod(s)
        nr = _round_up(n, 128) // 128
        out.append(packed[r:r + nr].reshape(-1)[:n].reshape(s))
        r += nr
    return out


_ANY = pl.BlockSpec(memory_space=pl.ANY)
_OTHER_CHIPS = ((1, 0), (0, 1), (1, 1))


def _here():
    return lax.axis_index("x"), lax.axis_index("y"), lax.axis_index("c")


def _flip(v, bit):
    return 1 - v if bit else v


def _remote(src, dst, send_sems, recv_sems, k, to):
    return pltpu.make_async_remote_copy(src_ref=src, dst_ref=dst, send_sem=send_sems.at[k], recv_sem=recv_sems.at[k],
                                        device_id=to, device_id_type=MESH)


def _gather_copies(srcs, dsts, send_sems, recv_sems, local_sems):
    x, y, c = _here()
    slot = 2 * x + y
    starts, waits = [], []
    for i, (src, dst) in enumerate(zip(srcs, dsts)):
        own = pltpu.make_async_copy(src, dst.at[slot], local_sems.at[i])
        starts.append(own.start)
        waits.append(own.wait)
        for j, (fx, fy) in enumerate(_OTHER_CHIPS):
            cx, cy = _flip(x, fx), _flip(y, fy)
            push = _remote(src, dst.at[slot], send_sems, recv_sems, 3 * i + j, (cx, cy, c))
            landed = dst.at[2 * cx + cy]
            starts.append(push.start)
            waits += [_remote(landed, landed, send_sems, recv_sems, 3 * i + j, (cx, cy, c)).wait_recv, push.wait_send]
    return starts, waits


def _gather_scratch(n):
    return [pltpu.SemaphoreType.DMA((3 * n,)), pltpu.SemaphoreType.DMA((3 * n,)), pltpu.SemaphoreType.DMA((n,))]


def _all_gather(shards, name):
    ns = len(shards)

    def body(*refs):
        starts, waits = _gather_copies(refs[:ns], refs[ns:2 * ns], *refs[2 * ns:])
        for call in starts + waits:
            call()

    return pl.pallas_call(
        body, in_specs=[_ANY] * ns, out_specs=[_ANY] * ns, out_shape=[_sds((4,) + s.shape, s.dtype) for s in shards],
        scratch_shapes=_gather_scratch(ns), name=name,
    )(*shards)


def _scattered_shape(p):
    return _sds((8, p.shape[1] // 2) + p.shape[2:], p.dtype)


def _scatter_copies(srcs, dsts, send_sems, recv_sems, local_sems, whole=0):
    x, y, c = _here()
    me = 4 * x + 2 * y + c
    starts, waits = [], []
    for i, (src, dst) in enumerate(zip(srcs, dsts)):
        def piece(px, py, pc, src=src, entire=i >= len(srcs) - whole):
            if entire:
                return src
            half = src.shape[1] // 2
            return src.at[2 * px + py, pl.ds(pl.multiple_of(pc * half, 8), half)]

        own = pltpu.make_async_copy(piece(x, y, c), dst.at[me], local_sems.at[i])
        starts.append(own.start)
        waits.append(own.wait)
        for k in range(1, 8):
            px, py, pc = _flip(x, k & 4), _flip(y, k & 2), _flip(c, k & 1)
            push = _remote(piece(px, py, pc), dst.at[me], send_sems, recv_sems, 7 * i + k - 1, (px, py, pc))
            landed = dst.at[4 * px + 2 * py + pc]
            starts.append(push.start)
            waits += [_remote(landed, landed, send_sems, recv_sems, 7 * i + k - 1, (px, py, pc)).wait_recv, push.wait_send]
    return starts, waits


def _scatter_scratch(n):
    return [pltpu.SemaphoreType.DMA((7 * n,)), pltpu.SemaphoreType.DMA((7 * n,)), pltpu.SemaphoreType.DMA((n,))]


def _scatter(partials, wholes, name):
    ns = len(partials) + len(wholes)

    def body(*refs):
        starts, waits = _scatter_copies(refs[:ns], refs[ns:2 * ns], *refs[2 * ns:], whole=len(wholes))
        for call in starts + waits:
            call()

    return pl.pallas_call(
        body, in_specs=[_ANY] * ns, out_specs=[_ANY] * ns,
        out_shape=[_scattered_shape(p) for p in partials] + [_sds((8,) + s.shape, s.dtype) for s in wholes],
        scratch_shapes=_scatter_scratch(ns), name=name,
    )(*partials, *wholes)


def _swap_copies(srcs, dsts, send_sems, recv_sems, local_sems):
    x, y, c = _here()
    sibling = (x, y, 1 - c)
    starts, waits = [], []
    for i, (src, dst) in enumerate(zip(srcs, dsts)):
        own = pltpu.make_async_copy(src, dst.at[c], local_sems.at[i])
        push = _remote(src, dst.at[c], send_sems, recv_sems, i, sibling)
        landed = dst.at[1 - c]
        starts += [own.start, push.start]
        waits += [_remote(landed, landed, send_sems, recv_sems, i, sibling).wait_recv, push.wait_send, own.wait]
    return starts, waits


def _swap_scratch(n):
    return [pltpu.SemaphoreType.DMA((n,)), pltpu.SemaphoreType.DMA((n,)), pltpu.SemaphoreType.DMA((n,))]


def _exchange_halves(halves):
    ns = len(halves)

    def body(*refs):
        starts, waits = _swap_copies(refs[:ns], refs[ns:2 * ns], *refs[2 * ns:])
        for call in starts + waits:
            call()

    return pl.pallas_call(
        body, in_specs=[_ANY] * ns, out_specs=[_ANY] * ns, out_shape=[_sds((2,) + h.shape, h.dtype) for h in halves],
        scratch_shapes=_swap_scratch(ns), name="exchange_halves",
    )(*halves)


def _row_tile(rows, row_bytes, budget):
    tr = rows
    while tr * row_bytes > budget and tr % 16 == 0:
        tr //= 2
    return tr


def _sum_slots(parts, name):
    _, rows, cols = parts.shape
    tr = _row_tile(rows, 8 * cols * 4, 2 * 1024 * 1024)

    def body(p_ref, o_ref):
        acc = p_ref[0]
        for d in range(1, 8):
            acc = acc + p_ref[d]
        o_ref[...] = acc

    return pl.pallas_call(
        body, grid=(rows // tr,), in_specs=[pl.BlockSpec((8, tr, cols), lambda i: (0, i, 0))],
        out_specs=pl.BlockSpec((tr, cols), lambda i: (i, 0)), out_shape=_sds((rows, cols), F32),
        compiler_params=_params(("parallel",)), name=name,
    )(parts)


def _adamw(w, g, m, v, name):
    rows, cols = w.shape
    tr = _row_tile(rows, 7 * cols * 4, 4 * 1024 * 1024)

    def body(w_ref, g_ref, m_ref, v_ref, d_ref, mo_ref, vo_ref):
        g = g_ref[...]
        m = ADAM_B1 * m_ref[...] + (1.0 - ADAM_B1) * g
        v = ADAM_B2 * v_ref[...] + (1.0 - ADAM_B2) * jnp.square(g)
        m_hat = m / (1.0 - ADAM_B1 ** ADAM_STEP)
        v_hat = v / (1.0 - ADAM_B2 ** ADAM_STEP)
        d_ref[...] = -ADAM_LR * (m_hat / (jnp.sqrt(v_hat) + ADAM_EPS) + ADAM_WD * w_ref[...])
        mo_ref[...] = m
        vo_ref[...] = v

    spec = pl.BlockSpec((tr, cols), lambda i: (i, 0))
    return pl.pallas_call(
        body, grid=(rows // tr,), in_specs=[spec] * 4, out_specs=[spec] * 3, out_shape=[_sds((rows, cols), F32)] * 3,
        compiler_params=_params(("parallel",)), name=name,
    )(w, g, m, v)


def kernel(x, positions, attn_norm_w, w_in, q_lat_norm_w, w_uq, kv_lat_norm_w, w_ukv, q_norm_w, k_norm_w, mla_out_norm_w, conv_w, a_log, dt_bias, gdn_norm_w, w_out, mlp_norm_w, w_up, w_down, loss_target, m_attn_norm_w, m_w_in, m_q_lat_norm_w, m_w_uq, m_kv_lat_norm_w, m_w_ukv, m_q_norm_w, m_k_norm_w, m_mla_out_norm_w, m_conv_w, m_a_log, m_dt_bias, m_gdn_norm_w, m_w_out, m_mlp_norm_w, m_w_up, m_w_down, v_attn_norm_w, v_w_in, v_q_lat_norm_w, v_w_uq, v_kv_lat_norm_w, v_w_ukv, v_q_norm_w, v_k_norm_w, v_mla_out_norm_w, v_conv_w, v_a_log, v_dt_bias, v_gdn_norm_w, v_w_out, v_mlp_norm_w, v_w_up, v_w_down):
    w = dict(zip(WEIGHTS, (attn_norm_w, w_in, q_lat_norm_w, w_uq, kv_lat_norm_w, w_ukv, q_norm_w, k_norm_w, mla_out_norm_w, conv_w,
                           a_log, dt_bias, gdn_norm_w, w_out, mlp_norm_w, w_up, w_down)))
    m = dict(zip(WEIGHTS, (m_attn_norm_w, m_w_in, m_q_lat_norm_w, m_w_uq, m_kv_lat_norm_w, m_w_ukv, m_q_norm_w, m_k_norm_w,
                           m_mla_out_norm_w, m_conv_w, m_a_log, m_dt_bias, m_gdn_norm_w, m_w_out, m_mlp_norm_w, m_w_up, m_w_down)))
    v = dict(zip(WEIGHTS, (v_attn_norm_w, v_w_in, v_q_lat_norm_w, v_w_uq, v_kv_lat_norm_w, v_w_ukv, v_q_norm_w, v_k_norm_w,
                           v_mla_out_norm_w, v_conv_w, v_a_log, v_dt_bias, v_gdn_norm_w, v_w_out, v_mlp_norm_w, v_w_up, v_w_down)))
    B, S, D = x.shape
    T = B * S
    x2, pos, target = x.reshape(T, D), positions.reshape(T, 1), loss_target.reshape(T, D)
    seq = lambda a: a.reshape(B, S, a.shape[-1])
    tok = lambda a: a.reshape(T, a.shape[-1])
    local = {n: w[n][0] for n in SHARDED}

    g_in, g_uq, g_ukv, g_conv = _all_gather([local["w_in"].astype(BF16), local["w_uq"].astype(BF16), local["w_ukv"].astype(BF16),
                                             local["conv_w"]], "gather_first_weights")
    w_in_p = _widen_w_in(_from_column_shards(g_in))
    w_mla = _stack_mla(_from_column_shards(g_uq), _from_column_shards(g_ukv))
    conv_full = _from_column_shards(g_conv)
    ln_w = jnp.concatenate([q_lat_norm_w, kv_lat_norm_w], axis=0)
    qk_nw = _qk_norm_rows(q_norm_w, k_norm_w)
    rope_rows = _rope_rows()
    scal = _rows8([jnp.pad(a_log, ((0, 0), (0, 128 - HEADS))), jnp.pad(dt_bias, ((0, 0), (0, 128 - HEADS)))])
    mix_nw = _rows8([mla_out_norm_w[0], gdn_norm_w])

    xn, lat, gqkv, gz, gab = _in_proj_fwd(x2, attn_norm_w, w_in_p)
    q, k, v_att = _mla_pre_fwd(lat, pos, ln_w, w_mla, qk_nw, rope_rows)
    ao, lse = _attn_fwd(seq(q), seq(k), seq(v_att))
    gq, gk, gv = _gdn_pre_fwd(seq(gqkv), conv_full)
    go, states, g_out, w_up_b, g_down = _gdn_chunk_fwd(
        gq, gk, gv, seq(gab), scal, [local["w_out"].astype(BF16), local["w_up"].astype(BF16), local["w_down"].astype(BF16)])
    w_out_b = g_out.reshape(-1, D)
    w_down_b = g_down.reshape(-1, D)
    mix, h2 = _mix_fwd(tok(ao), tok(go), gz, mix_nw, w_out_b, x2)
    hn, dy, sq = _mlp_fwd(h2, mlp_norm_w, w_up_b, w_down_b, target)
    loss = lax.psum(jnp.sum(sq[:, 0, 0]) * (0.5 / D), ("x", "y", "c"))

    dh, act, d_up, d_mlp_nw = _mlp_bwd(h2, mlp_norm_w, hn, w_up_b, w_down_b, dy)
    p_down = _wgrad(act, dy, "wgrad_down").reshape(4, -1, D)
    p_up = _wgrad(hn, d_up, "wgrad_up", column_shards=4)
    d_ao, d_go, d_gz, d_mix_nw = _mix_bwd(tok(ao), tok(go), gz, mix_nw, w_out_b, dh)
    p_out = _wgrad(mix, dh, "wgrad_out").reshape(4, -1, D)
    d_gq, d_gk, d_gv, d_gab, d_scal, s_up, s_down, s_out = _gdn_chunk_bwd(gq, gk, gv, seq(gab), scal, states, seq(d_go),
                                                                          [p_up, p_down, p_out])
    dxq, dxk, dxv, dcq, dck, dcv = _gdn_pre_bwd(seq(gqkv), conv_full, d_gq, d_gk, d_gv)
    early = ("w_up", "w_down", "w_out")
    early_halves = [_sum_slots(s, "sum_" + n) for n, s in zip(early, (s_up, s_down, s_out))]
    dq, dk, dv, *early_grads = _attn_bwd(seq(q), seq(k), seq(v_att), ao, lse, seq(d_ao), early_halves)
    d_lat, d_ln, d_w_mla, d_qk_nw = _mla_pre_bwd(lat, pos, ln_w, w_mla, qk_nw, rope_rows, tok(dq), tok(dk), tok(dv))
    d_gqkv = jnp.concatenate([dxq, dxk, dxv], axis=-1)
    grad_x2, d_proj, d_attn_nw = _in_proj_bwd(d_lat, tok(d_gqkv), d_gz, tok(d_gab), w_in_p, x2, attn_norm_w, dh)
    p_in = _column_shards(_narrow_w_in(_wgrad(xn, d_proj, "wgrad_in")))
    p_uq, p_ukv = (_column_shards(a) for a in _unstack_mla(d_w_mla))
    small_partial = {
        "attn_norm_w": d_attn_nw, "q_lat_norm_w": d_ln[0:1], "kv_lat_norm_w": d_ln[1:2],
        "q_norm_w": jnp.concatenate([d_qk_nw[0:1], _rope_unpad(d_qk_nw[1:2])], axis=-1),
        "k_norm_w": jnp.concatenate([d_qk_nw[2:3], _rope_unpad(d_qk_nw[3:4])], axis=-1),
        "mla_out_norm_w": d_mix_nw[None, :HEADS], "a_log": d_scal[0:1, :HEADS], "dt_bias": d_scal[1:2, :HEADS],
        "gdn_norm_w": d_mix_nw[HEADS:HEADS + 1], "mlp_norm_w": d_mlp_nw,
    }
    conv_partial = jnp.concatenate([dcq, dck, dcv], axis=-1)
    s_in, s_uq, s_ukv, s_small = _scatter([p_in, p_uq, p_ukv], [_pack_small([small_partial[n] for n in SMALL] + [conv_partial])],
                                          "scatter_last_partials")

    late = ("w_in", "w_uq", "w_ukv")
    late_grads = _exchange_halves([_sum_slots(s, "sum_" + n) for n, s in zip(late, (s_in, s_uq, s_ukv))])
    names = early + late
    grad = {n: g.reshape(local[n].shape) for n, g in zip(names, list(early_grads) + list(late_grads))}
    small_shapes = [w[n].shape for n in SMALL]
    *g_small, g_conv_all = _unpack_small(_sum_slots(s_small, "sum_small"), small_shapes + [conv_partial.shape])
    grad.update(zip(SMALL, g_small))
    conv_cols = local["conv_w"].shape[1]
    grad["conv_w"] = lax.dynamic_slice_in_dim(g_conv_all, (2 * lax.axis_index("x") + lax.axis_index("y")) * conv_cols, conv_cols, axis=1)

    delta, new_m, new_v = {}, {}, {}
    for n in names:
        delta[n], new_m[n], new_v[n] = _adamw(local[n], grad[n], m[n][0], v[n][0], "adamw_" + n)
    packed_names = SMALL + ("conv_w",)
    packed_shapes = small_shapes + [local["conv_w"].shape]
    take = lambda d: _pack_small([d[n][0] if n == "conv_w" and d[n].ndim == 3 else d[n] for n in packed_names])
    outs = _adamw(take(w), take(grad), take(m), take(v), "adamw_small")
    for d, packed in zip((delta, new_m, new_v), outs):
        d.update(zip(packed_names, _unpack_small(packed, packed_shapes)))

    def in_order(d):
        return [d[n].reshape(w[n].shape) for n in WEIGHTS]

    return (loss, grad_x2.reshape(B, S, D), *in_order(grad), *in_order(delta), *in_order(new_m), *in_order(new_v))
```

```python
import functools
import math

import jax
import jax.numpy as jnp
from jax import lax
from jax.experimental import pallas as pl
from jax.experimental.pallas import tpu as pltpu

F32 = jnp.float32
BF16 = jnp.bfloat16
MESH = pl.DeviceIdType.MESH

EPS = 1e-6
HEADS = 4
HEAD_DIM = 128
ROPE_DIM = 64
ROPE_HALF = 32
QK_DIM = 192
QK_PAD = 256
LORA = 256
CHUNK = 64
CONV_TAPS = 4
ROPE_THETA = 10000.0
ATTN_SCALE = QK_DIM ** -0.5

LAT_W = 640
GQKV_W = 3 * HEADS * HEAD_DIM
GZ_W = HEADS * HEAD_DIM
GAB_W = 128
PROJ_SPLITS = ((0, LAT_W), (LAT_W, LAT_W + GQKV_W), (LAT_W + GQKV_W, LAT_W + GQKV_W + GZ_W),
               (LAT_W + GQKV_W + GZ_W, LAT_W + GQKV_W + GZ_W + GAB_W))
PROJ_W = PROJ_SPLITS[-1][1]

ADAM_LR = 0.001
ADAM_B1 = 0.9
ADAM_B2 = 0.999
ADAM_EPS = 1e-08
ADAM_WD = 0.01
ADAM_STEP = 10

TOKEN_TILE = 512
FF_TILE = 512
ATTN_TILE = 512
WGRAD_OUT_BYTES = 8 * 1024 * 1024
VMEM_LIMIT = 48 * 1024 * 1024

SHARDED = ("w_in", "w_uq", "w_ukv", "conv_w", "w_out", "w_up", "w_down")
SMALL = ("attn_norm_w", "q_lat_norm_w", "kv_lat_norm_w", "q_norm_w", "k_norm_w", "mla_out_norm_w", "a_log", "dt_bias",
         "gdn_norm_w", "mlp_norm_w")
WEIGHTS = ("attn_norm_w", "w_in", "q_lat_norm_w", "w_uq", "kv_lat_norm_w", "w_ukv", "q_norm_w", "k_norm_w", "mla_out_norm_w",
           "conv_w", "a_log", "dt_bias", "gdn_norm_w", "w_out", "mlp_norm_w", "w_up", "w_down")


def _sds(shape, dtype):
    return jax.ShapeDtypeStruct(shape, dtype)


def _params(semantics):
    return pltpu.CompilerParams(dimension_semantics=semantics, vmem_limit_bytes=VMEM_LIMIT)


def _block(n):
    for b in (512, 256, 128):
        if n % b == 0:
            return b
    return n


def _dg(a, b, ca, cb, prec):
    lead = a.ndim - 2
    batch = (tuple(range(lead)),) * 2
    return lax.dot_general(a, b, (((ca + lead,), (cb + lead,)), batch), precision=prec, preferred_element_type=F32)


def _split_bf16(a):
    hi = a.astype(BF16)
    return hi, (a - hi.astype(F32)).astype(BF16)


def _dot_bf16(a, b, ca, cb):
    return _dg(a.astype(BF16), b.astype(BF16), ca, cb, None)


def _dot_bf16x3(a, b, ca, cb):
    a_hi, a_lo = _split_bf16(a)
    b_hi, b_lo = _split_bf16(b)
    lead = a.ndim - 2
    return _dg(jnp.concatenate([a_hi, a_hi, a_lo], axis=ca + lead), jnp.concatenate([b_hi, b_lo, b_hi], axis=cb + lead), ca, cb, None)


def _matmul_family(dot):
    def nn_raw(a, b):
        return dot(a, b, 1, 0)

    def nt_raw(a, b):
        return dot(a, b, 1, 1)

    def tn_raw(a, b):
        return dot(a, b, 0, 0)

    @jax.custom_vjp
    def nn(a, b):
        return nn_raw(a, b)

    nn.defvjp(lambda a, b: (nn_raw(a, b), (a, b)), lambda r, g: (nt_raw(g, r[1]), tn_raw(r[0], g)))

    @jax.custom_vjp
    def nt(a, b):
        return nt_raw(a, b)

    nt.defvjp(lambda a, b: (nt_raw(a, b), (a, b)), lambda r, g: (nn_raw(g, r[1]), tn_raw(g, r[0])))

    @jax.custom_vjp
    def tn(a, b):
        return tn_raw(a, b)

    tn.defvjp(lambda a, b: (tn_raw(a, b), (a, b)), lambda r, g: (nt_raw(r[1], g), nn_raw(r[0], g)))
    return nn, nt, tn


_bf_nn, _bf_nt, _bf_tn = _matmul_family(_dot_bf16)
_hi_nn, _hi_nt, _hi_tn = _matmul_family(_dot_bf16x3)


@jax.custom_vjp
def _lane_halves(x):
    n = x.shape[-1] // 2
    return x[..., :n], x[..., n:]


_lane_halves.defvjp(lambda x: (_lane_halves(x), None), lambda _, g: (jnp.concatenate(g, axis=-1),))


@jax.custom_vjp
def _row_halves(x):
    n = x.shape[-2] // 2
    return x[..., :n, :], x[..., n:, :]


_row_halves.defvjp(lambda x: (_row_halves(x), None), lambda _, g: (jnp.concatenate(g, axis=-2),))


@jax.custom_vjp
def _swap_halves(t):
    return pltpu.roll(t, 64, 1)


_swap_halves.defvjp(lambda t: (pltpu.roll(t, 64, 1), None), lambda _, g: (pltpu.roll(g, 64, 1),))


@functools.partial(jax.custom_vjp, nondiff_argnums=(2,))
def _shift_rows(x, keep, s):
    return pltpu.roll(x, s, 0) * keep


def _shift_rows_fwd(x, keep, s):
    return pltpu.roll(x, s, 0) * keep, keep


def _shift_rows_bwd(s, keep, g):
    return pltpu.roll(g * keep, keep.shape[0] - s, 0), jnp.zeros_like(keep)


_shift_rows.defvjp(_shift_rows_fwd, _shift_rows_bwd)


def _sigmoid(x):
    return 0.5 * jnp.tanh(0.5 * x) + 0.5


def _softplus(x):
    return jnp.maximum(x, 0.0) + jnp.log(1.0 + jnp.exp(jnp.minimum(x, -x)))


def _silu(x):
    return x * _sigmoid(x)


def _rms(x, w, n=None):
    n = x.shape[-1] if n is None else n
    r = lax.rsqrt(jnp.sum(x * x, axis=-1, keepdims=True) * (1.0 / n) + EPS)
    return x * r * w


def _rope(t, cos_f, sin_f):
    return t * cos_f + _swap_halves(t) * sin_f


def _rope_tables(pos_col, freq_row, sign_row):
    ang = pos_col.astype(F32) * freq_row
    return jnp.cos(ang), jnp.sin(ang) * sign_row


def _onehot_row(lane):
    return (lax.broadcasted_iota(jnp.int32, (1, 128), 1) == lane).astype(F32)


def _row_spec(tm, w):
    return pl.BlockSpec((tm, w), lambda i: (i, 0))


def _const_spec(shape):
    return pl.BlockSpec(shape, lambda *_: (0,) * len(shape))


def _in_proj_fwd(x2, w_an, w_in_p):
    T, D = x2.shape
    tm = min(TOKEN_TILE, T)

    def body(x_ref, wn_ref, w_ref, xn_ref, lat_ref, gqkv_ref, gz_ref, gab_ref):
        x = x_ref[...]
        r = lax.rsqrt(jnp.mean(x * x, axis=-1, keepdims=True) + EPS)
        xn = (x * r * wn_ref[...]).astype(BF16)
        xn_ref[...] = xn
        for ref, (a, b) in zip((lat_ref, gqkv_ref, gz_ref, gab_ref), PROJ_SPLITS):
            ref[...] = jnp.dot(xn, w_ref[:, a:b], preferred_element_type=F32)

    widths = [b - a for a, b in PROJ_SPLITS]
    return pl.pallas_call(
        body, grid=(T // tm,),
        in_specs=[_row_spec(tm, D), _const_spec((1, D)), _const_spec((D, PROJ_W))],
        out_specs=[_row_spec(tm, D)] + [_row_spec(tm, w) for w in widths],
        out_shape=[_sds((T, D), BF16)] + [_sds((T, w), F32) for w in widths],
        compiler_params=_params(("parallel",)), name="in_proj_fwd",
    )(x2, w_an, w_in_p)


def _in_proj_bwd(d_lat, d_gqkv, d_gz, d_gab, w_in_p, x2, w_an, dh):
    T, D = x2.shape
    tm = min(TOKEN_TILE, T)

    def body(dl_ref, dq_ref, dz_ref, da_ref, w_ref, x_ref, wn_ref, dh_ref, dx_ref, dp_ref, dwn_ref):
        @pl.when(pl.program_id(0) == 0)
        def _():
            dwn_ref[...] = jnp.zeros_like(dwn_ref)

        dxn = jnp.zeros((tm, D), F32)
        for ref, (a, b) in zip((dl_ref, dq_ref, dz_ref, da_ref), PROJ_SPLITS):
            piece = ref[...].astype(BF16)
            dp_ref[:, a:b] = piece
            dxn += _dg(piece, w_ref[:, a:b], 1, 1, None)
        _, pull = jax.vjp(_rms, x_ref[...], wn_ref[...])
        dx, dwn = pull(dxn)
        dx_ref[...] = dx + dh_ref[...]
        dwn_ref[...] += dwn

    widths = [b - a for a, b in PROJ_SPLITS]
    return pl.pallas_call(
        body, grid=(T // tm,),
        in_specs=[_row_spec(tm, w) for w in widths] + [_const_spec((D, PROJ_W)), _row_spec(tm, D), _const_spec((1, D)),
                                                       _row_spec(tm, D)],
        out_specs=[_row_spec(tm, D), _row_spec(tm, PROJ_W), _const_spec((1, D))],
        out_shape=[_sds((T, D), F32), _sds((T, PROJ_W), BF16), _sds((1, D), F32)],
        compiler_params=_params(("arbitrary",)), name="in_proj_bwd",
    )(d_lat, d_gqkv, d_gz, d_gab, w_in_p, x2, w_an, dh)


def _wgrad(a, b, name, column_shards=1):
    T, k1 = a.shape
    k2 = b.shape[1]
    per_shard = k2 // column_shards
    tt = min(TOKEN_TILE, T)
    b1 = k1
    while b1 * k2 * 4 > WGRAD_OUT_BYTES and b1 % 256 == 0:
        b1 //= 2
    step = _block(per_shard)

    def body(a_ref, b_ref, o_ref):
        @pl.when(pl.program_id(1) == 0)
        def _():
            o_ref[...] = jnp.zeros_like(o_ref)

        a_t = a_ref[...].astype(BF16).T
        for c0 in range(0, k2, step):
            part = jnp.dot(a_t, b_ref[:, c0:c0 + step].astype(BF16), preferred_element_type=F32)
            if column_shards == 1:
                o_ref[:, c0:c0 + step] += part
            else:
                o_ref[c0 // per_shard, :, c0 % per_shard:c0 % per_shard + step] += part

    if column_shards == 1:
        out_spec, out_shape = pl.BlockSpec((b1, k2), lambda i, t: (i, 0)), _sds((k1, k2), F32)
    else:
        out_spec = pl.BlockSpec((column_shards, b1, per_shard), lambda i, t: (0, i, 0))
        out_shape = _sds((column_shards, k1, per_shard), F32)
    return pl.pallas_call(
        body, grid=(k1 // b1, T // tt),
        in_specs=[pl.BlockSpec((tt, b1), lambda i, t: (t, i)), pl.BlockSpec((tt, k2), lambda i, t: (t, 0))],
        out_specs=out_spec, out_shape=out_shape,
        compiler_params=_params(("parallel", "arbitrary")), name=name,
    )(a, b)


def _mla_pre_fn(q_lat, kv_lat, kpe, ln_q, ln_kv, w_list, qn_n, qn_p, kn_n, kn_p, cos_f, sin_f):
    qn = _rms(q_lat, ln_q)
    kvn = _rms(kv_lat, ln_kv)
    kp = _rope(_rms(kpe, kn_p, ROPE_DIM), cos_f, sin_f)
    outs = []
    for h in range(HEADS):
        outs.append(_rms(_bf_nn(qn, w_list[h]), qn_n))
        outs.append(_rope(_rms(_bf_nn(qn, w_list[HEADS + h]), qn_p, ROPE_DIM), cos_f, sin_f))
        outs.append(_rms(_bf_nn(kvn, w_list[2 * HEADS + h]), kn_n))
        outs.append(_bf_nn(kvn, w_list[3 * HEADS + h]))
    return tuple(outs) + (kp,)


def _mla_pre_operands(lat_ref, pos_ref, ln_ref, w_ref, nw_ref, rope_ref):
    cos_f, sin_f = _rope_tables(pos_ref[...], rope_ref[0:1, :], rope_ref[1:2, :])
    diff = (lat_ref[:, 0:LORA], lat_ref[:, LORA:2 * LORA], lat_ref[:, 2 * LORA:LAT_W], ln_ref[0:1, :], ln_ref[1:2, :],
            [w_ref[i].astype(F32) for i in range(4 * HEADS)], nw_ref[0:1, :], nw_ref[1:2, :], nw_ref[2:3, :], nw_ref[3:4, :])
    return diff, cos_f, sin_f


def _mla_pre_fwd(lat, pos, ln_w, w_mla, nw, rope_rows):
    T = lat.shape[0]
    tm = min(TOKEN_TILE, T)

    def body(lat_ref, pos_ref, ln_ref, w_ref, nw_ref, rope_ref, q_ref, k_ref, v_ref):
        diff, cos_f, sin_f = _mla_pre_operands(lat_ref, pos_ref, ln_ref, w_ref, nw_ref, rope_ref)
        outs = _mla_pre_fn(*diff, cos_f, sin_f)
        kp = outs[-1].astype(BF16)
        for h in range(HEADS):
            q_n, q_p, k_n, v = outs[4 * h:4 * h + 4]
            q_ref[:, h * QK_PAD:h * QK_PAD + HEAD_DIM] = q_n.astype(BF16)
            q_ref[:, h * QK_PAD + HEAD_DIM:(h + 1) * QK_PAD] = q_p.astype(BF16)
            k_ref[:, h * QK_PAD:h * QK_PAD + HEAD_DIM] = k_n.astype(BF16)
            k_ref[:, h * QK_PAD + HEAD_DIM:(h + 1) * QK_PAD] = kp
            v_ref[:, h * HEAD_DIM:(h + 1) * HEAD_DIM] = v.astype(BF16)

    return pl.pallas_call(
        body, grid=(T // tm,),
        in_specs=[_row_spec(tm, LAT_W), _row_spec(tm, 1), _const_spec((2, LORA)), _const_spec((4 * HEADS, LORA, 128)),
                  _const_spec((8, 128)), _const_spec((8, 128))],
        out_specs=[_row_spec(tm, HEADS * QK_PAD), _row_spec(tm, HEADS * QK_PAD), _row_spec(tm, HEADS * HEAD_DIM)],
        out_shape=[_sds((T, HEADS * QK_PAD), BF16), _sds((T, HEADS * QK_PAD), BF16), _sds((T, HEADS * HEAD_DIM), BF16)],
        compiler_params=_params(("parallel",)), name="mla_pre_fwd",
    )(lat, pos, ln_w, w_mla, nw, rope_rows)


def _mla_pre_bwd(lat, pos, ln_w, w_mla, nw, rope_rows, dq, dk, dv):
    T = lat.shape[0]
    tm = min(TOKEN_TILE, T)

    def body(lat_ref, pos_ref, ln_ref, w_ref, nw_ref, rope_ref, dq_ref, dk_ref, dv_ref, dlat_ref, dln_ref, dw_ref, dnw_ref):
        @pl.when(pl.program_id(0) == 0)
        def _():
            dln_ref[...] = jnp.zeros_like(dln_ref)
            dw_ref[...] = jnp.zeros_like(dw_ref)
            dnw_ref[...] = jnp.zeros_like(dnw_ref)

        diff, cos_f, sin_f = _mla_pre_operands(lat_ref, pos_ref, ln_ref, w_ref, nw_ref, rope_ref)
        _, pull = jax.vjp(lambda *a: _mla_pre_fn(*a, cos_f, sin_f), *diff)
        cts = []
        d_kp = jnp.zeros((tm, 128), F32)
        for h in range(HEADS):
            cts.append(dq_ref[:, h * QK_PAD:h * QK_PAD + HEAD_DIM])
            cts.append(dq_ref[:, h * QK_PAD + HEAD_DIM:(h + 1) * QK_PAD])
            cts.append(dk_ref[:, h * QK_PAD:h * QK_PAD + HEAD_DIM])
            cts.append(dv_ref[:, h * HEAD_DIM:(h + 1) * HEAD_DIM])
            d_kp += dk_ref[:, h * QK_PAD + HEAD_DIM:(h + 1) * QK_PAD]
        d_ql, d_kvl, d_kpe, d_lnq, d_lnkv, d_w, d_qn_n, d_qn_p, d_kn_n, d_kn_p = pull(tuple(cts) + (d_kp,))
        dlat_ref[:, 0:LORA] = d_ql
        dlat_ref[:, LORA:2 * LORA] = d_kvl
        dlat_ref[:, 2 * LORA:LAT_W] = d_kpe
        dln_ref[0:1, :] += d_lnq
        dln_ref[1:2, :] += d_lnkv
        for i in range(4 * HEADS):
            dw_ref[i] += d_w[i]
        for i, d in enumerate((d_qn_n, d_qn_p, d_kn_n, d_kn_p)):
            dnw_ref[i:i + 1, :] += d

    return pl.pallas_call(
        body, grid=(T // tm,),
        in_specs=[_row_spec(tm, LAT_W), _row_spec(tm, 1), _const_spec((2, LORA)), _const_spec((4 * HEADS, LORA, 128)),
                  _const_spec((8, 128)), _const_spec((8, 128)),
                  _row_spec(tm, HEADS * QK_PAD), _row_spec(tm, HEADS * QK_PAD), _row_spec(tm, HEADS * HEAD_DIM)],
        out_specs=[_row_spec(tm, LAT_W), _const_spec((2, LORA)), _const_spec((4 * HEADS, LORA, 128)), _const_spec((8, 128))],
        out_shape=[_sds((T, LAT_W), F32), _sds((2, LORA), F32), _sds((4 * HEADS, LORA, 128), F32), _sds((8, 128), F32)],
        compiler_params=_params(("arbitrary",)), name="mla_pre_bwd",
    )(lat, pos, ln_w, w_mla, nw, rope_rows, dq, dk, dv)


def _causal_mask(i, j, tq, tk):
    row = i * tq + lax.broadcasted_iota(jnp.int32, (tq, tk), 0)
    col = j * tk + lax.broadcasted_iota(jnp.int32, (tq, tk), 1)
    return col <= row


def _attn_fwd(q, k, v):
    B, S, _ = q.shape
    t = min(ATTN_TILE, S)

    def body(q_ref, k_ref, v_ref, o_ref, lse_ref):
        i = pl.program_id(2)
        qb = q_ref[0]

        def step(j, carry, diagonal):
            m, l, acc = carry
            rows = pl.ds(pl.multiple_of(j * t, t), t)
            s = _dg(qb, k_ref[0, rows, :], 1, 1, None) * ATTN_SCALE
            if diagonal:
                s = jnp.where(_causal_mask(0, 0, t, t), s, -1e30)
            m_new = jnp.maximum(m, jnp.max(s, axis=-1, keepdims=True))
            p = jnp.exp(s - m_new)
            alpha = jnp.exp(m - m_new)
            l = alpha * l + jnp.sum(p, axis=-1, keepdims=True)
            acc = alpha * acc + jnp.dot(p.astype(BF16), v_ref[0, rows, :], preferred_element_type=F32)
            return m_new, l, acc

        init = (jnp.full((t, 1), -1e30, F32), jnp.zeros((t, 1), F32), jnp.zeros((t, HEAD_DIM), F32))
        below = lax.fori_loop(0, i, lambda j, carry: step(j, carry, False), init)
        m, l, acc = step(i, below, True)
        o_ref[0] = acc / l
        lse_ref[0, 0] = m + jnp.log(l)

    return pl.pallas_call(
        body, grid=(B, HEADS, S // t),
        in_specs=[pl.BlockSpec((1, t, QK_PAD), lambda b, h, i: (b, i, h)),
                  pl.BlockSpec((1, S, QK_PAD), lambda b, h, i: (b, 0, h)),
                  pl.BlockSpec((1, S, HEAD_DIM), lambda b, h, i: (b, 0, h))],
        out_specs=[pl.BlockSpec((1, t, HEAD_DIM), lambda b, h, i: (b, i, h)),
                   pl.BlockSpec((1, 1, t, 1), lambda b, h, i: (b, h, i, 0))],
        out_shape=[_sds((B, S, HEADS * HEAD_DIM), F32), _sds((B, HEADS, S, 1), F32)],
        compiler_params=_params(("parallel", "parallel", "parallel")), name="attn_fwd",
    )(q, k, v)


def _attn_bwd(q, k, v, o, lse, do, halves):
    B, S, _ = q.shape
    t = min(ATTN_TILE, S)
    nq = S // t
    ns = len(halves)

    def body(*refs):
        q_ref, k_ref, v_ref, o_ref, lse_ref, do_ref = refs[:6]
        src_refs = refs[6:6 + ns]
        dq_ref, dk_ref, dv_ref = refs[6 + ns:9 + ns]
        dst_refs = refs[9 + ns:9 + 2 * ns]
        dsum_ref, send_sems, recv_sems, local_sems = refs[9 + 2 * ns:]
        b, h, j = pl.program_id(0), pl.program_id(1), pl.program_id(2)

        @pl.when((b == 0) & (h == 0) & (j == 0))
        def _():
            for start in _swap_copies(src_refs, dst_refs, send_sems, recv_sems, local_sems)[0]:
                start()

        @pl.when(j == 0)
        def _():
            dq_ref[...] = jnp.zeros_like(dq_ref)
            dsum_ref[...] = jnp.sum(do_ref[0] * o_ref[0], axis=-1, keepdims=True)

        kb = k_ref[0]
        vb = v_ref[0]

        def step(i, carry, diagonal):
            dk, dv = carry
            rows = pl.ds(pl.multiple_of(i * t, t), t)
            qb = q_ref[0, rows, :]
            dob = do_ref[0, rows, :].astype(BF16)
            s = _dg(qb, kb, 1, 1, None) * ATTN_SCALE
            p = jnp.exp(s - lse_ref[0, 0, rows, :])
            if diagonal:
                p = jnp.where(_causal_mask(0, 0, t, t), p, 0.0)
            pb = p.astype(BF16)
            dv = dv + _dg(pb, dob, 0, 0, None)
            dp = _dg(dob, vb, 1, 1, None)
            ds = (p * (dp - dsum_ref[rows, :]) * ATTN_SCALE).astype(BF16)
            dq_ref[0, rows, :] += jnp.dot(ds, kb, preferred_element_type=F32)
            dk = dk + _dg(ds, qb, 0, 0, None)
            return dk, dv

        on_diagonal = step(j, (jnp.zeros((t, QK_PAD), F32), jnp.zeros((t, HEAD_DIM), F32)), True)
        dk, dv = lax.fori_loop(j + 1, nq, lambda i, carry: step(i, carry, False), on_diagonal)
        dk_ref[0] = dk
        dv_ref[0] = dv

        @pl.when((b == B - 1) & (h == HEADS - 1) & (j == nq - 1))
        def _():
            for wait in _swap_copies(src_refs, dst_refs, send_sems, recv_sems, local_sems)[1]:
                wait()

    return pl.pallas_call(
        body, grid=(B, HEADS, nq),
        in_specs=[pl.BlockSpec((1, S, QK_PAD), lambda b, h, j: (b, 0, h)),
                  pl.BlockSpec((1, t, QK_PAD), lambda b, h, j: (b, j, h)),
                  pl.BlockSpec((1, t, HEAD_DIM), lambda b, h, j: (b, j, h)),
                  pl.BlockSpec((1, S, HEAD_DIM), lambda b, h, j: (b, 0, h)),
                  pl.BlockSpec((1, 1, S, 1), lambda b, h, j: (b, h, 0, 0)),
                  pl.BlockSpec((1, S, HEAD_DIM), lambda b, h, j: (b, 0, h))] + [_ANY] * ns,
        out_specs=[pl.BlockSpec((1, S, QK_PAD), lambda b, h, j: (b, 0, h)),
                   pl.BlockSpec((1, t, QK_PAD), lambda b, h, j: (b, j, h)),
                   pl.BlockSpec((1, t, HEAD_DIM), lambda b, h, j: (b, j, h))] + [_ANY] * ns,
        out_shape=[_sds((B, S, HEADS * QK_PAD), F32), _sds((B, S, HEADS * QK_PAD), F32), _sds((B, S, HEADS * HEAD_DIM), F32)]
                  + [_sds((2,) + s.shape, s.dtype) for s in halves],
        scratch_shapes=[pltpu.VMEM((S, 1), F32)] + _swap_scratch(ns),
        compiler_params=_params(("arbitrary", "arbitrary", "arbitrary")), name="attn_bwd",
    )(q, k, v, o, lse, do, *halves)


def _gdn_pre_fn(xq, xk, xv, wq, wk, wv, keeps):
    def conv_silu(x, w):
        acc = x * w[3]
        for s in (1, 2, 3):
            acc = acc + _shift_rows(x, keeps[s - 1], s) * w[3 - s]
        return _silu(acc)

    def l2(x):
        return x * lax.rsqrt(jnp.sum(x * x, axis=-1, keepdims=True) + EPS)

    return l2(conv_silu(xq, wq)) * (HEAD_DIM ** -0.5), l2(conv_silu(xk, wk)), conv_silu(xv, wv)


def _gdn_pre_specs(S):
    x_specs = [pl.BlockSpec((1, S, HEAD_DIM), lambda h, b, g=g: (b, 0, g * HEADS + h)) for g in range(3)]
    w_specs = [pl.BlockSpec((CONV_TAPS, HEAD_DIM), lambda h, b, g=g: (0, g * HEADS + h)) for g in range(3)]
    out_spec = pl.BlockSpec((1, S, HEAD_DIM), lambda h, b: (b, 0, h))
    return x_specs, w_specs, out_spec


def _row_keeps(S):
    t = lax.broadcasted_iota(jnp.int32, (S, HEAD_DIM), 0)
    return [(t >= s).astype(F32) for s in (1, 2, 3)]


def _gdn_pre_fwd(gqkv, conv_w):
    B, S, _ = gqkv.shape
    x_specs, w_specs, out_spec = _gdn_pre_specs(S)

    def body(xq_ref, xk_ref, xv_ref, wq_ref, wk_ref, wv_ref, q_ref, k_ref, v_ref):
        taps = [[w[i:i + 1, :] for i in range(CONV_TAPS)] for w in (wq_ref, wk_ref, wv_ref)]
        q, k, v = _gdn_pre_fn(xq_ref[0], xk_ref[0], xv_ref[0], *taps, _row_keeps(S))
        q_ref[0], k_ref[0], v_ref[0] = q, k, v

    return pl.pallas_call(
        body, grid=(HEADS, B), in_specs=x_specs + w_specs, out_specs=[out_spec] * 3,
        out_shape=[_sds((B, S, HEADS * HEAD_DIM), F32)] * 3,
        compiler_params=_params(("parallel", "parallel")), name="gdn_pre_fwd",
    )(gqkv, gqkv, gqkv, conv_w, conv_w, conv_w)


def _gdn_pre_bwd(gqkv, conv_w, dq, dk, dv):
    B, S, _ = gqkv.shape
    x_specs, w_specs, out_spec = _gdn_pre_specs(S)
    dw_spec = pl.BlockSpec((CONV_TAPS, HEAD_DIM), lambda h, b: (0, h))

    def body(xq_ref, xk_ref, xv_ref, wq_ref, wk_ref, wv_ref, dq_ref, dk_ref, dv_ref,
             dxq_ref, dxk_ref, dxv_ref, dwq_ref, dwk_ref, dwv_ref):
        @pl.when(pl.program_id(1) == 0)
        def _():
            for r in (dwq_ref, dwk_ref, dwv_ref):
                r[...] = jnp.zeros_like(r)

        taps = [[w[i:i + 1, :] for i in range(CONV_TAPS)] for w in (wq_ref, wk_ref, wv_ref)]
        keeps = _row_keeps(S)
        _, pull = jax.vjp(lambda *a: _gdn_pre_fn(*a, keeps), xq_ref[0], xk_ref[0], xv_ref[0], *taps)
        dxq, dxk, dxv, dwq, dwk, dwv = pull((dq_ref[0], dk_ref[0], dv_ref[0]))
        dxq_ref[0], dxk_ref[0], dxv_ref[0] = dxq, dxk, dxv
        for ref, dw in ((dwq_ref, dwq), (dwk_ref, dwk), (dwv_ref, dwv)):
            for i in range(CONV_TAPS):
                ref[i:i + 1, :] += dw[i]

    hw = HEADS * HEAD_DIM
    return pl.pallas_call(
        body, grid=(HEADS, B), in_specs=x_specs + w_specs + [out_spec] * 3,
        out_specs=[out_spec] * 3 + [dw_spec] * 3,
        out_shape=[_sds((B, S, hw), F32)] * 3 + [_sds((CONV_TAPS, hw), F32)] * 3,
        compiler_params=_params(("parallel", "arbitrary")), name="gdn_pre_bwd",
    )(gqkv, gqkv, gqkv, conv_w, conv_w, conv_w, dq, dk, dv)


def _chunk_masks():
    i = lax.broadcasted_iota(jnp.int32, (CHUNK, CHUNK), 0)
    j = lax.broadcasted_iota(jnp.int32, (CHUNK, CHUNK), 1)
    lower, after = (j <= i).astype(F32), (j > i).astype(F32)
    return {"le": lower, "le_gt": jnp.concatenate([lower, after], axis=0), "strict": (j < i).astype(F32)}


def _gdn_chunk_fn(groups, masks):
    lane = lax.broadcasted_iota(jnp.int32, (groups, 1, 128), 2)
    head = lax.broadcasted_iota(jnp.int32, (groups, 1, 128), 0) % HEADS
    pick_a, pick_b = (lane == head).astype(F32), (lane == head + HEADS).astype(F32)
    lower, lower_after, strict = (jnp.broadcast_to(masks[n], (groups,) + masks[n].shape) for n in ("le", "le_gt", "strict"))
    ones_row = jnp.ones((1, 1, HEAD_DIM), F32)

    def f(q, k, v, gab, a_row, dt_row, state):
        ga = jnp.sum(gab * pick_a, axis=2, keepdims=True)
        gb = jnp.sum(gab * pick_b, axis=2, keepdims=True)
        a_log = jnp.sum(a_row * pick_a, axis=2, keepdims=True)
        dt_bias = jnp.sum(dt_row * pick_a, axis=2, keepdims=True)
        beta = _sigmoid(gb)
        g = -jnp.exp(a_log) * _softplus(ga + dt_bias)
        g_wide = g * ones_row
        cum, rest = _row_halves(_hi_nn(lower_after, g_wide))
        total = jnp.sum(g_wide, axis=1, keepdims=True)
        diff = _hi_nn(lower, g * strict)
        decay = lower * jnp.exp(diff)
        e_cum = jnp.exp(cum)
        lmat = strict * (beta * _bf_nt(k, k) * decay)
        rhs = jnp.concatenate([v * beta, k * (beta * e_cum)], axis=2)
        rhs = rhs - _hi_nn(lmat, rhs)
        power = lmat
        for _ in range(5):
            power = _hi_nn(power, power)
            rhs = rhs + _hi_nn(power, rhs)
        u, w = _lane_halves(rhs)
        attn = _bf_nt(q, k) * decay
        v_new = u - _bf_nn(w, state)
        o = _bf_nn(q * e_cum, state) + _bf_nn(attn, v_new)
        new_state = state * jnp.exp(total) + _bf_tn(k * jnp.exp(rest), v_new)
        return o, new_state

    return f


def _gdn_chunk_fwd(q, k, v, gab, scal, shards):
    B, S, W = q.shape
    N = S // CHUNK
    ns = len(shards)

    def body(*refs):
        q_ref, k_ref, v_ref, gab_ref, sc_ref = refs[:5]
        src_refs = refs[5:5 + ns]
        o_ref, st_ref = refs[5 + ns:7 + ns]
        dst_refs = refs[7 + ns:7 + 2 * ns]
        state_ref, send_sems, recv_sems, local_sems = refs[7 + 2 * ns:]
        n = pl.program_id(0)

        @pl.when(n == 0)
        def _():
            for start in _gather_copies(src_refs, dst_refs, send_sems, recv_sems, local_sems)[0]:
                start()
            state_ref[...] = jnp.zeros_like(state_ref)

        groups = [(b, h) for b in range(B) for h in range(HEADS)]
        gather = lambda ref: jnp.stack([ref[b, :, h * HEAD_DIM:(h + 1) * HEAD_DIM] for b, h in groups])
        state = state_ref[...]
        for i, (b, h) in enumerate(groups):
            st_ref[b, 0, h] = state[i]
        o, new_state = _gdn_chunk_fn(len(groups), _chunk_masks())(
            gather(q_ref), gather(k_ref), gather(v_ref), jnp.stack([gab_ref[b] for b, _ in groups]), sc_ref[0:1, :], sc_ref[1:2, :], state)
        for i, (b, h) in enumerate(groups):
            o_ref[b, :, h * HEAD_DIM:(h + 1) * HEAD_DIM] = o[i]
        state_ref[...] = new_state

        @pl.when(n == N - 1)
        def _():
            for wait in _gather_copies(src_refs, dst_refs, send_sems, recv_sems, local_sems)[1]:
                wait()

    seq = pl.BlockSpec((B, CHUNK, W), lambda n: (0, n, 0))
    return pl.pallas_call(
        body, grid=(N,),
        in_specs=[seq, seq, seq, pl.BlockSpec((B, CHUNK, GAB_W), lambda n: (0, n, 0)), _const_spec((8, 128))] + [_ANY] * ns,
        out_specs=[seq, pl.BlockSpec((B, 1, HEADS, HEAD_DIM, HEAD_DIM), lambda n: (0, n, 0, 0, 0))] + [_ANY] * ns,
        out_shape=[_sds((B, S, W), F32), _sds((B, N, HEADS, HEAD_DIM, HEAD_DIM), F32)] + [_sds((4,) + s.shape, s.dtype) for s in shards],
        scratch_shapes=[pltpu.VMEM((B * HEADS, HEAD_DIM, HEAD_DIM), F32)] + _gather_scratch(ns),
        compiler_params=_params(("arbitrary",)), name="gdn_chunk_fwd",
    )(q, k, v, gab, scal, *shards)


def _gdn_chunk_bwd(q, k, v, gab, scal, states, do, partials):
    B, S, W = q.shape
    N = S // CHUNK
    ns = len(partials)

    def body(*refs):
        q_ref, k_ref, v_ref, gab_ref, sc_ref, st_ref, do_ref = refs[:7]
        src_refs = refs[7:7 + ns]
        dq_ref, dk_ref, dv_ref, dgab_ref, dsc_ref = refs[7 + ns:12 + ns]
        dst_refs = refs[12 + ns:12 + 2 * ns]
        dstate_ref, send_sems, recv_sems, local_sems = refs[12 + 2 * ns:]
        n = pl.program_id(0)

        @pl.when(n == 0)
        def _():
            for start in _scatter_copies(src_refs, dst_refs, send_sems, recv_sems, local_sems)[0]:
                start()
            dstate_ref[...] = jnp.zeros_like(dstate_ref)
            dsc_ref[...] = jnp.zeros_like(dsc_ref)

        groups = [(b, h) for b in range(B) for h in range(HEADS)]
        gather = lambda ref: jnp.stack([ref[b, :, h * HEAD_DIM:(h + 1) * HEAD_DIM] for b, h in groups])
        _, pull = jax.vjp(_gdn_chunk_fn(len(groups), _chunk_masks()), gather(q_ref), gather(k_ref), gather(v_ref),
                          jnp.stack([gab_ref[b] for b, _ in groups]), sc_ref[0:1, :], sc_ref[1:2, :],
                          jnp.stack([st_ref[b, 0, h] for b, h in groups]))
        dq, dk, dv, dg, d_a, d_dt, dstate = pull((gather(do_ref), dstate_ref[...]))
        for i, (b, h) in enumerate(groups):
            lanes = slice(h * HEAD_DIM, (h + 1) * HEAD_DIM)
            dq_ref[b, :, lanes] = dq[i]
            dk_ref[b, :, lanes] = dk[i]
            dv_ref[b, :, lanes] = dv[i]
        for b in range(B):
            dgab_ref[b] = sum(dg[b * HEADS + h] for h in range(HEADS))
        dstate_ref[...] = dstate
        dsc_ref[0:1, :] += d_a
        dsc_ref[1:2, :] += d_dt

        @pl.when(n == N - 1)
        def _():
            for wait in _scatter_copies(src_refs, dst_refs, send_sems, recv_sems, local_sems)[1]:
                wait()

    seq = pl.BlockSpec((B, CHUNK, W), lambda n: (0, N - 1 - n, 0))
    gab_spec = pl.BlockSpec((B, CHUNK, GAB_W), lambda n: (0, N - 1 - n, 0))
    return pl.pallas_call(
        body, grid=(N,),
        in_specs=[seq, seq, seq, gab_spec, _const_spec((8, 128)),
                  pl.BlockSpec((B, 1, HEADS, HEAD_DIM, HEAD_DIM), lambda n: (0, N - 1 - n, 0, 0, 0)), seq] + [_ANY] * ns,
        out_specs=[seq, seq, seq, gab_spec, _const_spec((8, 128))] + [_ANY] * ns,
        out_shape=[_sds((B, S, W), F32)] * 3 + [_sds((B, S, GAB_W), F32), _sds((8, 128), F32)] + [_scattered_shape(p) for p in partials],
        scratch_shapes=[pltpu.VMEM((B * HEADS, HEAD_DIM, HEAD_DIM), F32)] + _scatter_scratch(ns),
        compiler_params=_params(("arbitrary",)), name="gdn_chunk_bwd",
    )(q, k, v, gab, scal, states, do, *partials)


def _mix_fn(ao, go, gz, w_mla, w_gdn):
    return tuple(_rms(ao[h], w_mla[h]) for h in range(HEADS)) + tuple(_rms(go[h], w_gdn) * _silu(gz[h]) for h in range(HEADS))


def _mix_operands(ao_ref, go_ref, gz_ref, nw_ref):
    blocks = lambda ref: [ref[:, h * HEAD_DIM:(h + 1) * HEAD_DIM] for h in range(HEADS)]
    return blocks(ao_ref), blocks(go_ref), blocks(gz_ref), [nw_ref[h:h + 1, :] for h in range(HEADS)], nw_ref[HEADS:HEADS + 1, :]


def _mix_fwd(ao, go, gz, nw, w_out, x2):
    T, D = x2.shape
    tm = min(TOKEN_TILE, T)
    MW = 2 * HEADS * HEAD_DIM

    def body(ao_ref, go_ref, gz_ref, nw_ref, w_ref, x_ref, mix_ref, h_ref):
        outs = _mix_fn(*_mix_operands(ao_ref, go_ref, gz_ref, nw_ref))
        for i, piece in enumerate(outs):
            mix_ref[:, i * HEAD_DIM:(i + 1) * HEAD_DIM] = piece.astype(BF16)
        h_ref[...] = x_ref[...] + jnp.dot(mix_ref[...], w_ref[...], preferred_element_type=F32)

    half = HEADS * HEAD_DIM
    return pl.pallas_call(
        body, grid=(T // tm,),
        in_specs=[_row_spec(tm, half), _row_spec(tm, half), _row_spec(tm, half), _const_spec((8, 128)), _const_spec((MW, D)),
                  _row_spec(tm, D)],
        out_specs=[_row_spec(tm, MW), _row_spec(tm, D)],
        out_shape=[_sds((T, MW), BF16), _sds((T, D), F32)],
        compiler_params=_params(("parallel",)), name="mix_fwd",
    )(ao, go, gz, nw, w_out, x2)


def _mix_bwd(ao, go, gz, nw, w_out, dh):
    T, D = dh.shape
    tm = min(TOKEN_TILE, T)
    MW = 2 * HEADS * HEAD_DIM
    half = HEADS * HEAD_DIM

    def body(ao_ref, go_ref, gz_ref, nw_ref, w_ref, dh_ref, dao_ref, dgo_ref, dgz_ref, dnw_ref):
        @pl.when(pl.program_id(0) == 0)
        def _():
            dnw_ref[...] = jnp.zeros_like(dnw_ref)

        d_mix = _dg(dh_ref[...].astype(BF16), w_ref[...], 1, 1, None)
        cts = tuple(d_mix[:, i * HEAD_DIM:(i + 1) * HEAD_DIM] for i in range(2 * HEADS))
        _, pull = jax.vjp(_mix_fn, *_mix_operands(ao_ref, go_ref, gz_ref, nw_ref))
        d_ao, d_go, d_gz, d_wm, d_wg = pull(cts)
        for h in range(HEADS):
            lanes = slice(h * HEAD_DIM, (h + 1) * HEAD_DIM)
            dao_ref[:, lanes] = d_ao[h]
            dgo_ref[:, lanes] = d_go[h]
            dgz_ref[:, lanes] = d_gz[h]
            dnw_ref[h:h + 1, :] += d_wm[h]
        dnw_ref[HEADS:HEADS + 1, :] += d_wg

    return pl.pallas_call(
        body, grid=(T // tm,),
        in_specs=[_row_spec(tm, half), _row_spec(tm, half), _row_spec(tm, half), _const_spec((8, 128)), _const_spec((MW, D)),
                  _row_spec(tm, D)],
        out_specs=[_row_spec(tm, half)] * 3 + [_const_spec((8, 128))],
        out_shape=[_sds((T, half), F32)] * 3 + [_sds((8, 128), F32)],
        compiler_params=_params(("arbitrary",)), name="mix_bwd",
    )(ao, go, gz, nw, w_out, dh)


def _up_spec(w_up, tf):
    per_shard = w_up.shape[2] // tf
    return pl.BlockSpec((None, w_up.shape[1], tf), lambda i, j: (j // per_shard, 0, j % per_shard))


def _mlp_fwd(h2, w_mn, w_up, w_down, target):
    T, D = h2.shape
    FF = w_down.shape[0]
    tm, tf = min(TOKEN_TILE, T), min(FF_TILE, w_up.shape[2])
    nf = FF // tf

    def body(h_ref, wn_ref, wu_ref, wd_ref, t_ref, hn_ref, dy_ref, sq_ref, acc_ref):
        j = pl.program_id(1)

        @pl.when(j == 0)
        def _():
            hn_ref[...] = _rms(h_ref[...], wn_ref[...]).astype(BF16)
            acc_ref[...] = jnp.zeros_like(acc_ref)

        up = jnp.dot(hn_ref[...], wu_ref[...], preferred_element_type=F32)
        act = jnp.square(jnp.maximum(up, 0.0)).astype(BF16)
        acc_ref[...] += jnp.dot(act, wd_ref[...], preferred_element_type=F32)

        @pl.when(j == nf - 1)
        def _():
            err = h_ref[...] + acc_ref[...] - t_ref[...]
            dy_ref[...] = err * (1.0 / D)
            sq_ref[...] = jnp.zeros_like(sq_ref) + jnp.sum(err * err)

    tok = lambda w: pl.BlockSpec((tm, w), lambda i, j: (i, 0))
    return pl.pallas_call(
        body, grid=(T // tm, nf),
        in_specs=[tok(D), _const_spec((1, D)), _up_spec(w_up, tf), pl.BlockSpec((tf, D), lambda i, j: (j, 0)), tok(D)],
        out_specs=[tok(D), tok(D), pl.BlockSpec((1, 8, 128), lambda i, j: (i, 0, 0))],
        out_shape=[_sds((T, D), BF16), _sds((T, D), F32), _sds((T // tm, 8, 128), F32)],
        scratch_shapes=[pltpu.VMEM((tm, D), F32)],
        compiler_params=_params(("parallel", "arbitrary")), name="mlp_fwd",
    )(h2, w_mn, w_up, w_down, target)


def _mlp_bwd(h2, w_mn, hn, w_up, w_down, dy):
    T, D = h2.shape
    FF = w_down.shape[0]
    tm, tf = min(TOKEN_TILE, T), min(FF_TILE, w_up.shape[2])
    nf = FF // tf

    def body(h_ref, wn_ref, hn_ref, wu_ref, wd_ref, dy_ref, dh_ref, act_ref, dup_ref, dwn_ref, acc_ref):
        i, j = pl.program_id(0), pl.program_id(1)

        @pl.when((i == 0) & (j == 0))
        def _():
            dwn_ref[...] = jnp.zeros_like(dwn_ref)

        @pl.when(j == 0)
        def _():
            acc_ref[...] = jnp.zeros_like(acc_ref)

        r = jnp.maximum(jnp.dot(hn_ref[...], wu_ref[...], preferred_element_type=F32), 0.0)
        act_ref[...] = (r * r).astype(BF16)
        d_act = _dg(dy_ref[...].astype(BF16), wd_ref[...], 1, 1, None)
        d_up = (d_act * (2.0 * r)).astype(BF16)
        dup_ref[...] = d_up
        acc_ref[...] += _dg(d_up, wu_ref[...], 1, 1, None)

        @pl.when(j == nf - 1)
        def _():
            _, pull = jax.vjp(_rms, h_ref[...], wn_ref[...])
            dh, dwn = pull(acc_ref[...])
            dh_ref[...] = dh + dy_ref[...]
            dwn_ref[...] += dwn

    tok = lambda w: pl.BlockSpec((tm, w), lambda i, j: (i, 0))
    ff = pl.BlockSpec((tm, tf), lambda i, j: (i, j))
    return pl.pallas_call(
        body, grid=(T // tm, nf),
        in_specs=[tok(D), _const_spec((1, D)), tok(D), _up_spec(w_up, tf), pl.BlockSpec((tf, D), lambda i, j: (j, 0)), tok(D)],
        out_specs=[tok(D), ff, ff, _const_spec((1, D))],
        out_shape=[_sds((T, D), F32), _sds((T, FF), BF16), _sds((T, FF), BF16), _sds((1, D), F32)],
        scratch_shapes=[pltpu.VMEM((tm, D), F32)],
        compiler_params=_params(("arbitrary", "arbitrary")), name="mlp_bwd",
    )(h2, w_mn, hn, w_up, w_down, dy)


def _rope_pad(a):
    z = jnp.zeros(a.shape[:-1] + (ROPE_HALF,), a.dtype)
    return jnp.concatenate([a[..., :ROPE_HALF], z, a[..., ROPE_HALF:], z], axis=-1)


def _rope_unpad(a):
    return jnp.concatenate([a[..., :ROPE_HALF], a[..., 2 * ROPE_HALF:3 * ROPE_HALF]], axis=-1)


_G0 = 2 * LORA + ROPE_DIM
_GZ0 = _G0 + GQKV_W
_GA0 = _GZ0 + GZ_W


def _widen_w_in(w):
    pad = jnp.zeros((w.shape[0], GAB_W - 2 * HEADS), w.dtype)
    return jnp.concatenate([w[:, :2 * LORA], _rope_pad(w[:, 2 * LORA:_G0]), w[:, _G0:_GA0], w[:, _GA0:], pad], axis=1)


def _narrow_w_in(w):
    return jnp.concatenate([w[:, :2 * LORA], _rope_unpad(w[:, 2 * LORA:LAT_W]), w[:, LAT_W:PROJ_SPLITS[2][1]],
                            w[:, PROJ_SPLITS[3][0]:PROJ_SPLITS[3][0] + 2 * HEADS]], axis=1)


def _stack_mla(w_uq, w_ukv):
    uq = w_uq.reshape(LORA, HEADS, QK_DIM)
    ukv = w_ukv.reshape(LORA, HEADS, 2 * HEAD_DIM)
    parts = [uq[:, :, :HEAD_DIM], _rope_pad(uq[:, :, HEAD_DIM:]), ukv[:, :, :HEAD_DIM], ukv[:, :, HEAD_DIM:]]
    return jnp.concatenate([p.transpose(1, 0, 2) for p in parts], axis=0)


def _unstack_mla(w):
    p = [w[i * HEADS:(i + 1) * HEADS].transpose(1, 0, 2) for i in range(4)]
    uq = jnp.concatenate([p[0], _rope_unpad(p[1])], axis=-1).reshape(LORA, HEADS * QK_DIM)
    ukv = jnp.concatenate([p[2], p[3]], axis=-1).reshape(LORA, HEADS * 2 * HEAD_DIM)
    return uq, ukv


def _rows8(rows):
    a = jnp.concatenate(rows, axis=0)
    return jnp.pad(a, ((0, 8 - a.shape[0]), (0, 0)))


def _qk_norm_rows(q_norm_w, k_norm_w):
    return _rows8([q_norm_w[:, :HEAD_DIM], _rope_pad(q_norm_w[:, HEAD_DIM:]), k_norm_w[:, :HEAD_DIM], _rope_pad(k_norm_w[:, HEAD_DIM:])])


def _rope_rows():
    inv_freq = ROPE_THETA ** (-jnp.arange(ROPE_HALF, dtype=F32) / ROPE_HALF)
    z = jnp.zeros((ROPE_HALF,), F32)
    freq = jnp.concatenate([inv_freq, z, inv_freq, z])
    sign = jnp.concatenate([-jnp.ones((ROPE_HALF,), F32), z, jnp.ones((ROPE_HALF,), F32), z])
    return _rows8([freq[None], sign[None]])


def _round_up(n, m):
    return -(-n // m) * m


def _column_shards(a):
    return a.reshape(a.shape[0], 4, a.shape[1] // 4).transpose(1, 0, 2)


def _from_column_shards(a):
    return a.transpose(1, 0, 2).reshape(a.shape[1], 4 * a.shape[2])


def _pack_small(arrays):
    rows = [jnp.pad(a.reshape(-1), (0, _round_up(a.size, 128) - a.size)).reshape(-1, 128) for a in arrays]
    packed = jnp.concatenate(rows, axis=0)
    return jnp.pad(packed, ((0, _round_up(packed.shape[0], 8) - packed.shape[0]), (0, 0)))


def _unpack_small(packed, shapes):
    out, r = [], 0
    for s in shapes:
        n = math.prod(s)
        nr = _round_up(n, 128) // 128
        out.append(packed[r:r + nr].reshape(-1)[:n].reshape(s))
        r += nr
    return out


_ANY = pl.BlockSpec(memory_space=pl.ANY)
_OTHER_CHIPS = ((1, 0), (0, 1), (1, 1))


def _here():
    return lax.axis_index("x"), lax.axis_index("y"), lax.axis_index("c")


def _flip(v, bit):
    return 1 - v if bit else v


def _remote(src, dst, send_sems, recv_sems, k, to):
    return pltpu.make_async_remote_copy(src_ref=src, dst_ref=dst, send_sem=send_sems.at[k], recv_sem=recv_sems.at[k],
                                        device_id=to, device_id_type=MESH)


def _gather_copies(srcs, dsts, send_sems, recv_sems, local_sems):
    x, y, c = _here()
    slot = 2 * x + y
    starts, waits = [], []
    for i, (src, dst) in enumerate(zip(srcs, dsts)):
        own = pltpu.make_async_copy(src, dst.at[slot], local_sems.at[i])
        starts.append(own.start)
        waits.append(own.wait)
        for j, (fx, fy) in enumerate(_OTHER_CHIPS):
            cx, cy = _flip(x, fx), _flip(y, fy)
            push = _remote(src, dst.at[slot], send_sems, recv_sems, 3 * i + j, (cx, cy, c))
            landed = dst.at[2 * cx + cy]
            starts.append(push.start)
            waits += [_remote(landed, landed, send_sems, recv_sems, 3 * i + j, (cx, cy, c)).wait_recv, push.wait_send]
    return starts, waits


def _gather_scratch(n):
    return [pltpu.SemaphoreType.DMA((3 * n,)), pltpu.SemaphoreType.DMA((3 * n,)), pltpu.SemaphoreType.DMA((n,))]


def _all_gather(shards, name):
    ns = len(shards)

    def body(*refs):
        starts, waits = _gather_copies(refs[:ns], refs[ns:2 * ns], *refs[2 * ns:])
        for call in starts + waits:
            call()

    return pl.pallas_call(
        body, in_specs=[_ANY] * ns, out_specs=[_ANY] * ns, out_shape=[_sds((4,) + s.shape, s.dtype) for s in shards],
        scratch_shapes=_gather_scratch(ns), name=name,
    )(*shards)


def _scattered_shape(p):
    return _sds((8, p.shape[1] // 2) + p.shape[2:], p.dtype)


def _scatter_copies(srcs, dsts, send_sems, recv_sems, local_sems, whole=0):
    x, y, c = _here()
    me = 4 * x + 2 * y + c
    starts, waits = [], []
    for i, (src, dst) in enumerate(zip(srcs, dsts)):
        def piece(px, py, pc, src=src, entire=i >= len(srcs) - whole):
            if entire:
                return src
            half = src.shape[1] // 2
            return src.at[2 * px + py, pl.ds(pl.multiple_of(pc * half, 8), half)]

        own = pltpu.make_async_copy(piece(x, y, c), dst.at[me], local_sems.at[i])
        starts.append(own.start)
        waits.append(own.wait)
        for k in range(1, 8):
            px, py, pc = _flip(x, k & 4), _flip(y, k & 2), _flip(c, k & 1)
            push = _remote(piece(px, py, pc), dst.at[me], send_sems, recv_sems, 7 * i + k - 1, (px, py, pc))
            landed = dst.at[4 * px + 2 * py + pc]
            starts.append(push.start)
            waits += [_remote(landed, landed, send_sems, recv_sems, 7 * i + k - 1, (px, py, pc)).wait_recv, push.wait_send]
    return starts, waits


def _scatter_scratch(n):
    return [pltpu.SemaphoreType.DMA((7 * n,)), pltpu.SemaphoreType.DMA((7 * n,)), pltpu.SemaphoreType.DMA((n,))]


def _scatter(partials, wholes, name):
    ns = len(partials) + len(wholes)

    def body(*refs):
        starts, waits = _scatter_copies(refs[:ns], refs[ns:2 * ns], *refs[2 * ns:], whole=len(wholes))
        for call in starts + waits:
            call()

    return pl.pallas_call(
        body, in_specs=[_ANY] * ns, out_specs=[_ANY] * ns,
        out_shape=[_scattered_shape(p) for p in partials] + [_sds((8,) + s.shape, s.dtype) for s in wholes],
        scratch_shapes=_scatter_scratch(ns), name=name,
    )(*partials, *wholes)


def _swap_copies(srcs, dsts, send_sems, recv_sems, local_sems):
    x, y, c = _here()
    sibling = (x, y, 1 - c)
    starts, waits = [], []
    for i, (src, dst) in enumerate(zip(srcs, dsts)):
        own = pltpu.make_async_copy(src, dst.at[c], local_sems.at[i])
        push = _remote(src, dst.at[c], send_sems, recv_sems, i, sibling)
        landed = dst.at[1 - c]
        starts += [own.start, push.start]
        waits += [_remote(landed, landed, send_sems, recv_sems, i, sibling).wait_recv, push.wait_send, own.wait]
    return starts, waits


def _swap_scratch(n):
    return [pltpu.SemaphoreType.DMA((n,)), pltpu.SemaphoreType.DMA((n,)), pltpu.SemaphoreType.DMA((n,))]


def _exchange_halves(halves):
    ns = len(halves)

    def body(*refs):
        starts, waits = _swap_copies(refs[:ns], refs[ns:2 * ns], *refs[2 * ns:])
        for call in starts + waits:
            call()

    return pl.pallas_call(
        body, in_specs=[_ANY] * ns, out_specs=[_ANY] * ns, out_shape=[_sds((2,) + h.shape, h.dtype) for h in halves],
        scratch_shapes=_swap_scratch(ns), name="exchange_halves",
    )(*halves)


def _row_tile(rows, row_bytes, budget):
    tr = rows
    while tr * row_bytes > budget and tr % 16 == 0:
        tr //= 2
    return tr


def _sum_slots(parts, name):
    _, rows, cols = parts.shape
    tr = _row_tile(rows, 8 * cols * 4, 2 * 1024 * 1024)

    def body(p_ref, o_ref):
        acc = p_ref[0]
        for d in range(1, 8):
            acc = acc + p_ref[d]
        o_ref[...] = acc

    return pl.pallas_call(
        body, grid=(rows // tr,), in_specs=[pl.BlockSpec((8, tr, cols), lambda i: (0, i, 0))],
        out_specs=pl.BlockSpec((tr, cols), lambda i: (i, 0)), out_shape=_sds((rows, cols), F32),
        compiler_params=_params(("parallel",)), name=name,
    )(parts)


def _adamw(w, g, m, v, name):
    rows, cols = w.shape
    tr = _row_tile(rows, 7 * cols * 4, 4 * 1024 * 1024)

    def body(w_ref, g_ref, m_ref, v_ref, d_ref, mo_ref, vo_ref):
        g = g_ref[...]
        m = ADAM_B1 * m_ref[...] + (1.0 - ADAM_B1) * g
        v = ADAM_B2 * v_ref[...] + (1.0 - ADAM_B2) * jnp.square(g)
        m_hat = m / (1.0 - ADAM_B1 ** ADAM_STEP)
        v_hat = v / (1.0 - ADAM_B2 ** ADAM_STEP)
        d_ref[...] = -ADAM_LR * (m_hat / (jnp.sqrt(v_hat) + ADAM_EPS) + ADAM_WD * w_ref[...])
        mo_ref[...] = m
        vo_ref[...] = v

    spec = pl.BlockSpec((tr, cols), lambda i: (i, 0))
    return pl.pallas_call(
        body, grid=(rows // tr,), in_specs=[spec] * 4, out_specs=[spec] * 3, out_shape=[_sds((rows, cols), F32)] * 3,
        compiler_params=_params(("parallel",)), name=name,
    )(w, g, m, v)


def kernel(x, positions, attn_norm_w, w_in, q_lat_norm_w, w_uq, kv_lat_norm_w, w_ukv, q_norm_w, k_norm_w, mla_out_norm_w, conv_w, a_log, dt_bias, gdn_norm_w, w_out, mlp_norm_w, w_up, w_down, loss_target, m_attn_norm_w, m_w_in, m_q_lat_norm_w, m_w_uq, m_kv_lat_norm_w, m_w_ukv, m_q_norm_w, m_k_norm_w, m_mla_out_norm_w, m_conv_w, m_a_log, m_dt_bias, m_gdn_norm_w, m_w_out, m_mlp_norm_w, m_w_up, m_w_down, v_attn_norm_w, v_w_in, v_q_lat_norm_w, v_w_uq, v_kv_lat_norm_w, v_w_ukv, v_q_norm_w, v_k_norm_w, v_mla_out_norm_w, v_conv_w, v_a_log, v_dt_bias, v_gdn_norm_w, v_w_out, v_mlp_norm_w, v_w_up, v_w_down):
    w = dict(zip(WEIGHTS, (attn_norm_w, w_in, q_lat_norm_w, w_uq, kv_lat_norm_w, w_ukv, q_norm_w, k_norm_w, mla_out_norm_w, conv_w,
                           a_log, dt_bias, gdn_norm_w, w_out, mlp_norm_w, w_up, w_down)))
    m = dict(zip(WEIGHTS, (m_attn_norm_w, m_w_in, m_q_lat_norm_w, m_w_uq, m_kv_lat_norm_w, m_w_ukv, m_q_norm_w, m_k_norm_w,
                           m_mla_out_norm_w, m_conv_w, m_a_log, m_dt_bias, m_gdn_norm_w, m_w_out, m_mlp_norm_w, m_w_up, m_w_down)))
    v = dict(zip(WEIGHTS, (v_attn_norm_w, v_w_in, v_q_lat_norm_w, v_w_uq, v_kv_lat_norm_w, v_w_ukv, v_q_norm_w, v_k_norm_w,
                           v_mla_out_norm_w, v_conv_w, v_a_log, v_dt_bias, v_gdn_norm_w, v_w_out, v_mlp_norm_w, v_w_up, v_w_down)))
    B, S, D = x.shape
    T = B * S
    x2, pos, target = x.reshape(T, D), positions.reshape(T, 1), loss_target.reshape(T, D)
    seq = lambda a: a.reshape(B, S, a.shape[-1])
    tok = lambda a: a.reshape(T, a.shape[-1])
    local = {n: w[n][0] for n in SHARDED}

    g_in, g_uq, g_ukv, g_conv = _all_gather([local["w_in"].astype(BF16), local["w_uq"].astype(BF16), local["w_ukv"].astype(BF16),
                                             local["conv_w"]], "gather_first_weights")
    w_in_p = _widen_w_in(_from_column_shards(g_in))
    w_mla = _stack_mla(_from_column_shards(g_uq), _from_column_shards(g_ukv))
    conv_full = _from_column_shards(g_conv)
    ln_w = jnp.concatenate([q_lat_norm_w, kv_lat_norm_w], axis=0)
    qk_nw = _qk_norm_rows(q_norm_w, k_norm_w)
    rope_rows = _rope_rows()
    scal = _rows8([jnp.pad(a_log, ((0, 0), (0, 128 - HEADS))), jnp.pad(dt_bias, ((0, 0), (0, 128 - HEADS)))])
    mix_nw = _rows8([mla_out_norm_w[0], gdn_norm_w])

    xn, lat, gqkv, gz, gab = _in_proj_fwd(x2, attn_norm_w, w_in_p)
    q, k, v_att = _mla_pre_fwd(lat, pos, ln_w, w_mla, qk_nw, rope_rows)
    ao, lse = _attn_fwd(seq(q), seq(k), seq(v_att))
    gq, gk, gv = _gdn_pre_fwd(seq(gqkv), conv_full)
    go, states, g_out, w_up_b, g_down = _gdn_chunk_fwd(
        gq, gk, gv, seq(gab), scal, [local["w_out"].astype(BF16), local["w_up"].astype(BF16), local["w_down"].astype(BF16)])
    w_out_b = g_out.reshape(-1, D)
    w_down_b = g_down.reshape(-1, D)
    mix, h2 = _mix_fwd(tok(ao), tok(go), gz, mix_nw, w_out_b, x2)
    hn, dy, sq = _mlp_fwd(h2, mlp_norm_w, w_up_b, w_down_b, target)
    loss = lax.psum(jnp.sum(sq[:, 0, 0]) * (0.5 / D), ("x", "y", "c"))

    dh, act, d_up, d_mlp_nw = _mlp_bwd(h2, mlp_norm_w, hn, w_up_b, w_down_b, dy)
    p_down = _wgrad(act, dy, "wgrad_down").reshape(4, -1, D)
    p_up = _wgrad(hn, d_up, "wgrad_up", column_shards=4)
    d_ao, d_go, d_gz, d_mix_nw = _mix_bwd(tok(ao), tok(go), gz, mix_nw, w_out_b, dh)
    p_out = _wgrad(mix, dh, "wgrad_out").reshape(4, -1, D)
    d_gq, d_gk, d_gv, d_gab, d_scal, s_up, s_down, s_out = _gdn_chunk_bwd(gq, gk, gv, seq(gab), scal, states, seq(d_go),
                                                                          [p_up, p_down, p_out])
    dxq, dxk, dxv, dcq, dck, dcv = _gdn_pre_bwd(seq(gqkv), conv_full, d_gq, d_gk, d_gv)
    early = ("w_up", "w_down", "w_out")
    early_halves = [_sum_slots(s, "sum_" + n) for n, s in zip(early, (s_up, s_down, s_out))]
    dq, dk, dv, *early_grads = _attn_bwd(seq(q), seq(k), seq(v_att), ao, lse, seq(d_ao), early_halves)
    d_lat, d_ln, d_w_mla, d_qk_nw = _mla_pre_bwd(lat, pos, ln_w, w_mla, qk_nw, rope_rows, tok(dq), tok(dk), tok(dv))
    d_gqkv = jnp.concatenate([dxq, dxk, dxv], axis=-1)
    grad_x2, d_proj, d_attn_nw = _in_proj_bwd(d_lat, tok(d_gqkv), d_gz, tok(d_gab), w_in_p, x2, attn_norm_w, dh)
    p_in = _column_shards(_narrow_w_in(_wgrad(xn, d_proj, "wgrad_in")))
    p_uq, p_ukv = (_column_shards(a) for a in _unstack_mla(d_w_mla))
    small_partial = {
        "attn_norm_w": d_attn_nw, "q_lat_norm_w": d_ln[0:1], "kv_lat_norm_w": d_ln[1:2],
        "q_norm_w": jnp.concatenate([d_qk_nw[0:1], _rope_unpad(d_qk_nw[1:2])], axis=-1),
        "k_norm_w": jnp.concatenate([d_qk_nw[2:3], _rope_unpad(d_qk_nw[3:4])], axis=-1),
        "mla_out_norm_w": d_mix_nw[None, :HEADS], "a_log": d_scal[0:1, :HEADS], "dt_bias": d_scal[1:2, :HEADS],
        "gdn_norm_w": d_mix_nw[HEADS:HEADS + 1], "mlp_norm_w": d_mlp_nw,
    }
    conv_partial = jnp.concatenate([dcq, dck, dcv], axis=-1)
    s_in, s_uq, s_ukv, s_small = _scatter([p_in, p_uq, p_ukv], [_pack_small([small_partial[n] for n in SMALL] + [conv_partial])],
                                          "scatter_last_partials")

    late = ("w_in", "w_uq", "w_ukv")
    late_grads = _exchange_halves([_sum_slots(s, "sum_" + n) for n, s in zip(late, (s_in, s_uq, s_ukv))])
    names = early + late
    grad = {n: g.reshape(local[n].shape) for n, g in zip(names, list(early_grads) + list(late_grads))}
    small_shapes = [w[n].shape for n in SMALL]
    *g_small, g_conv_all = _unpack_small(_sum_slots(s_small, "sum_small"), small_shapes + [conv_partial.shape])
    grad.update(zip(SMALL, g_small))
    conv_cols = local["conv_w"].shape[1]
    grad["conv_w"] = lax.dynamic_slice_in_dim(g_conv_all, (2 * lax.axis_index("x") + lax.axis_index("y")) * conv_cols, conv_cols, axis=1)

    delta, new_m, new_v = {}, {}, {}
    for n in names:
        delta[n], new_m[n], new_v[n] = _adamw(local[n], grad[n], m[n][0], v[n][0], "adamw_" + n)
    packed_names = SMALL + ("conv_w",)
    packed_shapes = small_shapes + [local["conv_w"].shape]
    take = lambda d: _pack_small([d[n][0] if n == "conv_w" and d[n].ndim == 3 else d[n] for n in packed_names])
    outs = _adamw(take(w), take(grad), take(m), take(v), "adamw_small")
    for d, packed in zip((delta, new_m, new_v), outs):
        d.update(zip(packed_names, _unpack_small(packed, packed_shapes)))

    def in_order(d):
        return [d[n].reshape(w[n].shape) for n in WEIGHTS]

    return (loss, grad_x2.reshape(B, S, D), *in_order(grad), *in_order(delta), *in_order(new_m), *in_order(new_v))
```

```python
import functools
import math

import jax
import jax.numpy as jnp
from jax import lax
from jax.experimental import pallas as pl
from jax.experimental.pallas import tpu as pltpu

F32 = jnp.float32
BF16 = jnp.bfloat16
MESH = pl.DeviceIdType.MESH

EPS = 1e-6
HEADS = 4
HEAD_DIM = 128
ROPE_DIM = 64
ROPE_HALF = 32
QK_DIM = 192
QK_PAD = 256
LORA = 256
CHUNK = 64
CONV_TAPS = 4
ROPE_THETA = 10000.0
ATTN_SCALE = QK_DIM ** -0.5

LAT_W = 640
GQKV_W = 3 * HEADS * HEAD_DIM
GZ_W = HEADS * HEAD_DIM
GAB_W = 128
PROJ_SPLITS = ((0, LAT_W), (LAT_W, LAT_W + GQKV_W), (LAT_W + GQKV_W, LAT_W + GQKV_W + GZ_W),
               (LAT_W + GQKV_W + GZ_W, LAT_W + GQKV_W + GZ_W + GAB_W))
PROJ_W = PROJ_SPLITS[-1][1]

ADAM_LR = 0.001
ADAM_B1 = 0.9
ADAM_B2 = 0.999
ADAM_EPS = 1e-08
ADAM_WD = 0.01
ADAM_STEP = 10

TOKEN_TILE = 512
FF_TILE = 512
ATTN_TILE = 512
WGRAD_OUT_BYTES = 8 * 1024 * 1024
VMEM_LIMIT = 48 * 1024 * 1024

SHARDED = ("w_in", "w_uq", "w_ukv", "conv_w", "w_out", "w_up", "w_down")
SMALL = ("attn_norm_w", "q_lat_norm_w", "kv_lat_norm_w", "q_norm_w", "k_norm_w", "mla_out_norm_w", "a_log", "dt_bias",
         "gdn_norm_w", "mlp_norm_w")
WEIGHTS = ("attn_norm_w", "w_in", "q_lat_norm_w", "w_uq", "kv_lat_norm_w", "w_ukv", "q_norm_w", "k_norm_w", "mla_out_norm_w",
           "conv_w", "a_log", "dt_bias", "gdn_norm_w", "w_out", "mlp_norm_w", "w_up", "w_down")


def _sds(shape, dtype):
    return jax.ShapeDtypeStruct(shape, dtype)


def _params(semantics):
    return pltpu.CompilerParams(dimension_semantics=semantics, vmem_limit_bytes=VMEM_LIMIT)


def _block(n):
    for b in (512, 256, 128):
        if n % b == 0:
            return b
    return n


def _dg(a, b, ca, cb, prec):
    lead = a.ndim - 2
    batch = (tuple(range(lead)),) * 2
    return lax.dot_general(a, b, (((ca + lead,), (cb + lead,)), batch), precision=prec, preferred_element_type=F32)


def _split_bf16(a):
    hi = a.astype(BF16)
    return hi, (a - hi.astype(F32)).astype(BF16)


def _dot_bf16(a, b, ca, cb):
    return _dg(a.astype(BF16), b.astype(BF16), ca, cb, None)


def _dot_bf16x3(a, b, ca, cb):
    a_hi, a_lo = _split_bf16(a)
    b_hi, b_lo = _split_bf16(b)
    lead = a.ndim - 2
    return _dg(jnp.concatenate([a_hi, a_hi, a_lo], axis=ca + lead), jnp.concatenate([b_hi, b_lo, b_hi], axis=cb + lead), ca, cb, None)


def _matmul_family(dot):
    def nn_raw(a, b):
        return dot(a, b, 1, 0)

    def nt_raw(a, b):
        return dot(a, b, 1, 1)

    def tn_raw(a, b):
        return dot(a, b, 0, 0)

    @jax.custom_vjp
    def nn(a, b):
        return nn_raw(a, b)

    nn.defvjp(lambda a, b: (nn_raw(a, b), (a, b)), lambda r, g: (nt_raw(g, r[1]), tn_raw(r[0], g)))

    @jax.custom_vjp
    def nt(a, b):
        return nt_raw(a, b)

    nt.defvjp(lambda a, b: (nt_raw(a, b), (a, b)), lambda r, g: (nn_raw(g, r[1]), tn_raw(g, r[0])))

    @jax.custom_vjp
    def tn(a, b):
        return tn_raw(a, b)

    tn.defvjp(lambda a, b: (tn_raw(a, b), (a, b)), lambda r, g: (nt_raw(r[1], g), nn_raw(r[0], g)))
    return nn, nt, tn


_bf_nn, _bf_nt, _bf_tn = _matmul_family(_dot_bf16)
_hi_nn, _hi_nt, _hi_tn = _matmul_family(_dot_bf16x3)


@jax.custom_vjp
def _lane_halves(x):
    n = x.shape[-1] // 2
    return x[..., :n], x[..., n:]


_lane_halves.defvjp(lambda x: (_lane_halves(x), None), lambda _, g: (jnp.concatenate(g, axis=-1),))


@jax.custom_vjp
def _row_halves(x):
    n = x.shape[-2] // 2
    return x[..., :n, :], x[..., n:, :]


_row_halves.defvjp(lambda x: (_row_halves(x), None), lambda _, g: (jnp.concatenate(g, axis=-2),))


@jax.custom_vjp
def _swap_halves(t):
    return pltpu.roll(t, 64, 1)


_swap_halves.defvjp(lambda t: (pltpu.roll(t, 64, 1), None), lambda _, g: (pltpu.roll(g, 64, 1),))


@functools.partial(jax.custom_vjp, nondiff_argnums=(2,))
def _shift_rows(x, keep, s):
    return pltpu.roll(x, s, 0) * keep


def _shift_rows_fwd(x, keep, s):
    return pltpu.roll(x, s, 0) * keep, keep


def _shift_rows_bwd(s, keep, g):
    return pltpu.roll(g * keep, keep.shape[0] - s, 0), jnp.zeros_like(keep)


_shift_rows.defvjp(_shift_rows_fwd, _shift_rows_bwd)


def _sigmoid(x):
    return 0.5 * jnp.tanh(0.5 * x) + 0.5


def _softplus(x):
    return jnp.maximum(x, 0.0) + jnp.log(1.0 + jnp.exp(jnp.minimum(x, -x)))


def _silu(x):
    return x * _sigmoid(x)


def _rms(x, w, n=None):
    n = x.shape[-1] if n is None else n
    r = lax.rsqrt(jnp.sum(x * x, axis=-1, keepdims=True) * (1.0 / n) + EPS)
    return x * r * w


def _rope(t, cos_f, sin_f):
    return t * cos_f + _swap_halves(t) * sin_f


def _rope_tables(pos_col, freq_row, sign_row):
    ang = pos_col.astype(F32) * freq_row
    return jnp.cos(ang), jnp.sin(ang) * sign_row


def _onehot_row(lane):
    return (lax.broadcasted_iota(jnp.int32, (1, 128), 1) == lane).astype(F32)


def _row_spec(tm, w):
    return pl.BlockSpec((tm, w), lambda i: (i, 0))


def _const_spec(shape):
    return pl.BlockSpec(shape, lambda *_: (0,) * len(shape))


def _in_proj_fwd(x2, w_an, w_in_p):
    T, D = x2.shape
    tm = min(TOKEN_TILE, T)

    def body(x_ref, wn_ref, w_ref, xn_ref, lat_ref, gqkv_ref, gz_ref, gab_ref):
        x = x_ref[...]
        r = lax.rsqrt(jnp.mean(x * x, axis=-1, keepdims=True) + EPS)
        xn = (x * r * wn_ref[...]).astype(BF16)
        xn_ref[...] = xn
        for ref, (a, b) in zip((lat_ref, gqkv_ref, gz_ref, gab_ref), PROJ_SPLITS):
            ref[...] = jnp.dot(xn, w_ref[:, a:b], preferred_element_type=F32)

    widths = [b - a for a, b in PROJ_SPLITS]
    return pl.pallas_call(
        body, grid=(T // tm,),
        in_specs=[_row_spec(tm, D), _const_spec((1, D)), _const_spec((D, PROJ_W))],
        out_specs=[_row_spec(tm, D)] + [_row_spec(tm, w) for w in widths],
        out_shape=[_sds((T, D), BF16)] + [_sds((T, w), F32) for w in widths],
        compiler_params=_params(("parallel",)), name="in_proj_fwd",
    )(x2, w_an, w_in_p)


def _in_proj_bwd(d_lat, d_gqkv, d_gz, d_gab, w_in_p, x2, w_an, dh):
    T, D = x2.shape
    tm = min(TOKEN_TILE, T)

    def body(dl_ref, dq_ref, dz_ref, da_ref, w_ref, x_ref, wn_ref, dh_ref, dx_ref, dp_ref, dwn_ref):
        @pl.when(pl.program_id(0) == 0)
        def _():
            dwn_ref[...] = jnp.zeros_like(dwn_ref)

        dxn = jnp.zeros((tm, D), F32)
        for ref, (a, b) in zip((dl_ref, dq_ref, dz_ref, da_ref), PROJ_SPLITS):
            piece = ref[...].astype(BF16)
            dp_ref[:, a:b] = piece
            dxn += _dg(piece, w_ref[:, a:b], 1, 1, None)
        _, pull = jax.vjp(_rms, x_ref[...], wn_ref[...])
        dx, dwn = pull(dxn)
        dx_ref[...] = dx + dh_ref[...]
        dwn_ref[...] += dwn

    widths = [b - a for a, b in PROJ_SPLITS]
    return pl.pallas_call(
        body, grid=(T // tm,),
        in_specs=[_row_spec(tm, w) for w in widths] + [_const_spec((D, PROJ_W)), _row_spec(tm, D), _const_spec((1, D)),
                                                       _row_spec(tm, D)],
        out_specs=[_row_spec(tm, D), _row_spec(tm, PROJ_W), _const_spec((1, D))],
        out_shape=[_sds((T, D), F32), _sds((T, PROJ_W), BF16), _sds((1, D), F32)],
        compiler_params=_params(("arbitrary",)), name="in_proj_bwd",
    )(d_lat, d_gqkv, d_gz, d_gab, w_in_p, x2, w_an, dh)


def _wgrad(a, b, name, column_shards=1, out_dtype=BF16):
    T, k1 = a.shape
    k2 = b.shape[1]
    per_shard = k2 // column_shards
    tt = min(TOKEN_TILE, T)
    b1 = k1
    while b1 * k2 * 4 > WGRAD_OUT_BYTES and b1 % 256 == 0:
        b1 //= 2
    step = _block(per_shard)

    def body(a_ref, b_ref, o_ref, acc_ref):
        t = pl.program_id(1)

        @pl.when(t == 0)
        def _():
            acc_ref[...] = jnp.zeros_like(acc_ref)

        a_t = a_ref[...].astype(BF16).T
        for c0 in range(0, k2, step):
            part = jnp.dot(a_t, b_ref[:, c0:c0 + step].astype(BF16), preferred_element_type=F32)
            if column_shards == 1:
                acc_ref[:, c0:c0 + step] += part
            else:
                acc_ref[c0 // per_shard, :, c0 % per_shard:c0 % per_shard + step] += part

        @pl.when(t == T // tt - 1)
        def _():
            o_ref[...] = acc_ref[...].astype(o_ref.dtype)

    if column_shards == 1:
        block, out_spec, out_shape = (b1, k2), pl.BlockSpec((b1, k2), lambda i, t: (i, 0)), _sds((k1, k2), out_dtype)
    else:
        block = (column_shards, b1, per_shard)
        out_spec, out_shape = pl.BlockSpec(block, lambda i, t: (0, i, 0)), _sds((column_shards, k1, per_shard), out_dtype)
    return pl.pallas_call(
        body, grid=(k1 // b1, T // tt),
        in_specs=[pl.BlockSpec((tt, b1), lambda i, t: (t, i)), pl.BlockSpec((tt, k2), lambda i, t: (t, 0))],
        out_specs=out_spec, out_shape=out_shape, scratch_shapes=[pltpu.VMEM(block, F32)],
        compiler_params=_params(("parallel", "arbitrary")), name=name,
    )(a, b)


def _mla_pre_fn(q_lat, kv_lat, kpe, ln_q, ln_kv, w_list, qn_n, qn_p, kn_n, kn_p, cos_f, sin_f):
    qn = _rms(q_lat, ln_q)
    kvn = _rms(kv_lat, ln_kv)
    kp = _rope(_rms(kpe, kn_p, ROPE_DIM), cos_f, sin_f)
    outs = []
    for h in range(HEADS):
        outs.append(_rms(_bf_nn(qn, w_list[h]), qn_n))
        outs.append(_rope(_rms(_bf_nn(qn, w_list[HEADS + h]), qn_p, ROPE_DIM), cos_f, sin_f))
        outs.append(_rms(_bf_nn(kvn, w_list[2 * HEADS + h]), kn_n))
        outs.append(_bf_nn(kvn, w_list[3 * HEADS + h]))
    return tuple(outs) + (kp,)


def _mla_pre_operands(lat_ref, pos_ref, ln_ref, w_ref, nw_ref, rope_ref):
    cos_f, sin_f = _rope_tables(pos_ref[...], rope_ref[0:1, :], rope_ref[1:2, :])
    diff = (lat_ref[:, 0:LORA], lat_ref[:, LORA:2 * LORA], lat_ref[:, 2 * LORA:LAT_W], ln_ref[0:1, :], ln_ref[1:2, :],
            [w_ref[i].astype(F32) for i in range(4 * HEADS)], nw_ref[0:1, :], nw_ref[1:2, :], nw_ref[2:3, :], nw_ref[3:4, :])
    return diff, cos_f, sin_f


def _mla_pre_fwd(lat, pos, ln_w, w_mla, nw, rope_rows):
    T = lat.shape[0]
    tm = min(TOKEN_TILE, T)

    def body(lat_ref, pos_ref, ln_ref, w_ref, nw_ref, rope_ref, q_ref, k_ref, v_ref):
        diff, cos_f, sin_f = _mla_pre_operands(lat_ref, pos_ref, ln_ref, w_ref, nw_ref, rope_ref)
        outs = _mla_pre_fn(*diff, cos_f, sin_f)
        kp = outs[-1].astype(BF16)
        for h in range(HEADS):
            q_n, q_p, k_n, v = outs[4 * h:4 * h + 4]
            q_ref[:, h * QK_PAD:h * QK_PAD + HEAD_DIM] = q_n.astype(BF16)
            q_ref[:, h * QK_PAD + HEAD_DIM:(h + 1) * QK_PAD] = q_p.astype(BF16)
            k_ref[:, h * QK_PAD:h * QK_PAD + HEAD_DIM] = k_n.astype(BF16)
            k_ref[:, h * QK_PAD + HEAD_DIM:(h + 1) * QK_PAD] = kp
            v_ref[:, h * HEAD_DIM:(h + 1) * HEAD_DIM] = v.astype(BF16)

    return pl.pallas_call(
        body, grid=(T // tm,),
        in_specs=[_row_spec(tm, LAT_W), _row_spec(tm, 1), _const_spec((2, LORA)), _const_spec((4 * HEADS, LORA, 128)),
                  _const_spec((8, 128)), _const_spec((8, 128))],
        out_specs=[_row_spec(tm, HEADS * QK_PAD), _row_spec(tm, HEADS * QK_PAD), _row_spec(tm, HEADS * HEAD_DIM)],
        out_shape=[_sds((T, HEADS * QK_PAD), BF16), _sds((T, HEADS * QK_PAD), BF16), _sds((T, HEADS * HEAD_DIM), BF16)],
        compiler_params=_params(("parallel",)), name="mla_pre_fwd",
    )(lat, pos, ln_w, w_mla, nw, rope_rows)


def _mla_pre_bwd(lat, pos, ln_w, w_mla, nw, rope_rows, dq, dk, dv):
    T = lat.shape[0]
    tm = min(TOKEN_TILE, T)

    def body(lat_ref, pos_ref, ln_ref, w_ref, nw_ref, rope_ref, dq_ref, dk_ref, dv_ref, dlat_ref, dln_ref, dw_ref, dnw_ref):
        @pl.when(pl.program_id(0) == 0)
        def _():
            dln_ref[...] = jnp.zeros_like(dln_ref)
            dw_ref[...] = jnp.zeros_like(dw_ref)
            dnw_ref[...] = jnp.zeros_like(dnw_ref)

        diff, cos_f, sin_f = _mla_pre_operands(lat_ref, pos_ref, ln_ref, w_ref, nw_ref, rope_ref)
        _, pull = jax.vjp(lambda *a: _mla_pre_fn(*a, cos_f, sin_f), *diff)
        cts = []
        d_kp = jnp.zeros((tm, 128), F32)
        for h in range(HEADS):
            cts.append(dq_ref[:, h * QK_PAD:h * QK_PAD + HEAD_DIM])
            cts.append(dq_ref[:, h * QK_PAD + HEAD_DIM:(h + 1) * QK_PAD])
            cts.append(dk_ref[:, h * QK_PAD:h * QK_PAD + HEAD_DIM])
            cts.append(dv_ref[:, h * HEAD_DIM:(h + 1) * HEAD_DIM])
            d_kp += dk_ref[:, h * QK_PAD + HEAD_DIM:(h + 1) * QK_PAD]
        d_ql, d_kvl, d_kpe, d_lnq, d_lnkv, d_w, d_qn_n, d_qn_p, d_kn_n, d_kn_p = pull(tuple(cts) + (d_kp,))
        dlat_ref[:, 0:LORA] = d_ql
        dlat_ref[:, LORA:2 * LORA] = d_kvl
        dlat_ref[:, 2 * LORA:LAT_W] = d_kpe
        dln_ref[0:1, :] += d_lnq
        dln_ref[1:2, :] += d_lnkv
        for i in range(4 * HEADS):
            dw_ref[i] += d_w[i]
        for i, d in enumerate((d_qn_n, d_qn_p, d_kn_n, d_kn_p)):
            dnw_ref[i:i + 1, :] += d

    return pl.pallas_call(
        body, grid=(T // tm,),
        in_specs=[_row_spec(tm, LAT_W), _row_spec(tm, 1), _const_spec((2, LORA)), _const_spec((4 * HEADS, LORA, 128)),
                  _const_spec((8, 128)), _const_spec((8, 128)),
                  _row_spec(tm, HEADS * QK_PAD), _row_spec(tm, HEADS * QK_PAD), _row_spec(tm, HEADS * HEAD_DIM)],
        out_specs=[_row_spec(tm, LAT_W), _const_spec((2, LORA)), _const_spec((4 * HEADS, LORA, 128)), _const_spec((8, 128))],
        out_shape=[_sds((T, LAT_W), F32), _sds((2, LORA), F32), _sds((4 * HEADS, LORA, 128), F32), _sds((8, 128), F32)],
        compiler_params=_params(("arbitrary",)), name="mla_pre_bwd",
    )(lat, pos, ln_w, w_mla, nw, rope_rows, dq, dk, dv)


def _causal_mask(i, j, tq, tk):
    row = i * tq + lax.broadcasted_iota(jnp.int32, (tq, tk), 0)
    col = j * tk + lax.broadcasted_iota(jnp.int32, (tq, tk), 1)
    return col <= row


def _attn_fwd(q, k, v, shards):
    B, S, _ = q.shape
    t = min(ATTN_TILE, S)
    nq = S // t
    ns = len(shards)

    def body(*refs):
        q_ref, k_ref, v_ref = refs[:3]
        src_refs = refs[3:3 + ns]
        o_ref, lse_ref = refs[3 + ns:5 + ns]
        dst_refs = refs[5 + ns:5 + 2 * ns]
        sems = refs[5 + 2 * ns:]
        b, h, i = pl.program_id(0), pl.program_id(1), pl.program_id(2)
        qb = q_ref[0]

        @pl.when((b == 0) & (h == 0) & (i == 0))
        def _():
            for start in _gather_copies(src_refs, dst_refs, *sems)[0]:
                start()

        def step(j, carry, diagonal):
            m, l, acc = carry
            rows = pl.ds(pl.multiple_of(j * t, t), t)
            s = _dg(qb, k_ref[0, rows, :], 1, 1, None) * ATTN_SCALE
            if diagonal:
                s = jnp.where(_causal_mask(0, 0, t, t), s, -1e30)
            m_new = jnp.maximum(m, jnp.max(s, axis=-1, keepdims=True))
            p = jnp.exp(s - m_new)
            alpha = jnp.exp(m - m_new)
            l = alpha * l + jnp.sum(p, axis=-1, keepdims=True)
            acc = alpha * acc + jnp.dot(p.astype(BF16), v_ref[0, rows, :], preferred_element_type=F32)
            return m_new, l, acc

        init = (jnp.full((t, 1), -1e30, F32), jnp.zeros((t, 1), F32), jnp.zeros((t, HEAD_DIM), F32))
        below = lax.fori_loop(0, i, lambda j, carry: step(j, carry, False), init)
        m, l, acc = step(i, below, True)
        o_ref[0] = acc / l
        lse_ref[0, 0] = m + jnp.log(l)

        @pl.when((b == B - 1) & (h == HEADS - 1) & (i == nq - 1))
        def _():
            for wait in _gather_copies(src_refs, dst_refs, *sems)[1]:
                wait()

    return pl.pallas_call(
        body, grid=(B, HEADS, nq),
        in_specs=[pl.BlockSpec((1, t, QK_PAD), lambda b, h, i: (b, i, h)),
                  pl.BlockSpec((1, S, QK_PAD), lambda b, h, i: (b, 0, h)),
                  pl.BlockSpec((1, S, HEAD_DIM), lambda b, h, i: (b, 0, h))] + [_ANY] * ns,
        out_specs=[pl.BlockSpec((1, t, HEAD_DIM), lambda b, h, i: (b, i, h)),
                   pl.BlockSpec((1, 1, t, 1), lambda b, h, i: (b, h, i, 0))] + [_ANY] * ns,
        out_shape=[_sds((B, S, HEADS * HEAD_DIM), F32), _sds((B, HEADS, S, 1), F32)] + [_sds((4,) + s.shape, s.dtype) for s in shards],
        scratch_shapes=_gather_scratch(ns),
        compiler_params=_params(("arbitrary", "arbitrary", "arbitrary")), name="attn_fwd",
    )(q, k, v, *shards)


def _attn_bwd(q, k, v, o, lse, do, halves):
    B, S, _ = q.shape
    t = min(ATTN_TILE, S)
    nq = S // t
    ns = len(halves)

    def body(*refs):
        q_ref, k_ref, v_ref, o_ref, lse_ref, do_ref = refs[:6]
        src_refs = refs[6:6 + ns]
        dq_ref, dk_ref, dv_ref = refs[6 + ns:9 + ns]
        dst_refs = refs[9 + ns:9 + 2 * ns]
        dsum_ref, send_sems, recv_sems, local_sems = refs[9 + 2 * ns:]
        b, h, j = pl.program_id(0), pl.program_id(1), pl.program_id(2)

        @pl.when((b == 0) & (h == 0) & (j == 0))
        def _():
            for start in _swap_copies(src_refs, dst_refs, send_sems, recv_sems, local_sems)[0]:
                start()

        @pl.when(j == 0)
        def _():
            dq_ref[...] = jnp.zeros_like(dq_ref)
            dsum_ref[...] = jnp.sum(do_ref[0] * o_ref[0], axis=-1, keepdims=True)

        kb = k_ref[0]
        vb = v_ref[0]

        def step(i, carry, diagonal):
            dk, dv = carry
            rows = pl.ds(pl.multiple_of(i * t, t), t)
            qb = q_ref[0, rows, :]
            dob = do_ref[0, rows, :].astype(BF16)
            s = _dg(qb, kb, 1, 1, None) * ATTN_SCALE
            p = jnp.exp(s - lse_ref[0, 0, rows, :])
            if diagonal:
                p = jnp.where(_causal_mask(0, 0, t, t), p, 0.0)
            pb = p.astype(BF16)
            dv = dv + _dg(pb, dob, 0, 0, None)
            dp = _dg(dob, vb, 1, 1, None)
            ds = (p * (dp - dsum_ref[rows, :]) * ATTN_SCALE).astype(BF16)
            dq_ref[0, rows, :] += jnp.dot(ds, kb, preferred_element_type=F32)
            dk = dk + _dg(ds, qb, 0, 0, None)
            return dk, dv

        on_diagonal = step(j, (jnp.zeros((t, QK_PAD), F32), jnp.zeros((t, HEAD_DIM), F32)), True)
        dk, dv = lax.fori_loop(j + 1, nq, lambda i, carry: step(i, carry, False), on_diagonal)
        dk_ref[0] = dk
        dv_ref[0] = dv

        @pl.when((b == B - 1) & (h == HEADS - 1) & (j == nq - 1))
        def _():
            for wait in _swap_copies(src_refs, dst_refs, send_sems, recv_sems, local_sems)[1]:
                wait()

    return pl.pallas_call(
        body, grid=(B, HEADS, nq),
        in_specs=[pl.BlockSpec((1, S, QK_PAD), lambda b, h, j: (b, 0, h)),
                  pl.BlockSpec((1, t, QK_PAD), lambda b, h, j: (b, j, h)),
                  pl.BlockSpec((1, t, HEAD_DIM), lambda b, h, j: (b, j, h)),
                  pl.BlockSpec((1, S, HEAD_DIM), lambda b, h, j: (b, 0, h)),
                  pl.BlockSpec((1, 1, S, 1), lambda b, h, j: (b, h, 0, 0)),
                  pl.BlockSpec((1, S, HEAD_DIM), lambda b, h, j: (b, 0, h))] + [_ANY] * ns,
        out_specs=[pl.BlockSpec((1, S, QK_PAD), lambda b, h, j: (b, 0, h)),
                   pl.BlockSpec((1, t, QK_PAD), lambda b, h, j: (b, j, h)),
                   pl.BlockSpec((1, t, HEAD_DIM), lambda b, h, j: (b, j, h))] + [_ANY] * ns,
        out_shape=[_sds((B, S, HEADS * QK_PAD), F32), _sds((B, S, HEADS * QK_PAD), F32), _sds((B, S, HEADS * HEAD_DIM), F32)]
                  + [_sds((2,) + s.shape, s.dtype) for s in halves],
        scratch_shapes=[pltpu.VMEM((S, 1), F32)] + _swap_scratch(ns),
        compiler_params=_params(("arbitrary", "arbitrary", "arbitrary")), name="attn_bwd",
    )(q, k, v, o, lse, do, *halves)


def _gdn_pre_fn(xq, xk, xv, wq, wk, wv, keeps):
    def conv_silu(x, w):
        acc = x * w[3]
        for s in (1, 2, 3):
            acc = acc + _shift_rows(x, keeps[s - 1], s) * w[3 - s]
        return _silu(acc)

    def l2(x):
        return x * lax.rsqrt(jnp.sum(x * x, axis=-1, keepdims=True) + EPS)

    return l2(conv_silu(xq, wq)) * (HEAD_DIM ** -0.5), l2(conv_silu(xk, wk)), conv_silu(xv, wv)


def _gdn_pre_specs(S):
    x_specs = [pl.BlockSpec((1, S, HEAD_DIM), lambda h, b, g=g: (b, 0, g * HEADS + h)) for g in range(3)]
    w_specs = [pl.BlockSpec((CONV_TAPS, HEAD_DIM), lambda h, b, g=g: (0, g * HEADS + h)) for g in range(3)]
    out_spec = pl.BlockSpec((1, S, HEAD_DIM), lambda h, b: (b, 0, h))
    return x_specs, w_specs, out_spec


def _row_keeps(S):
    t = lax.broadcasted_iota(jnp.int32, (S, HEAD_DIM), 0)
    return [(t >= s).astype(F32) for s in (1, 2, 3)]


def _gdn_pre_fwd(gqkv, conv_w):
    B, S, _ = gqkv.shape
    x_specs, w_specs, out_spec = _gdn_pre_specs(S)

    def body(xq_ref, xk_ref, xv_ref, wq_ref, wk_ref, wv_ref, q_ref, k_ref, v_ref):
        taps = [[w[i:i + 1, :] for i in range(CONV_TAPS)] for w in (wq_ref, wk_ref, wv_ref)]
        q, k, v = _gdn_pre_fn(xq_ref[0], xk_ref[0], xv_ref[0], *taps, _row_keeps(S))
        q_ref[0], k_ref[0], v_ref[0] = q, k, v

    return pl.pallas_call(
        body, grid=(HEADS, B), in_specs=x_specs + w_specs, out_specs=[out_spec] * 3,
        out_shape=[_sds((B, S, HEADS * HEAD_DIM), F32)] * 3,
        compiler_params=_params(("parallel", "parallel")), name="gdn_pre_fwd",
    )(gqkv, gqkv, gqkv, conv_w, conv_w, conv_w)


def _gdn_pre_bwd(gqkv, conv_w, dq, dk, dv):
    B, S, _ = gqkv.shape
    x_specs, w_specs, out_spec = _gdn_pre_specs(S)
    dw_spec = pl.BlockSpec((CONV_TAPS, HEAD_DIM), lambda h, b: (0, h))

    def body(xq_ref, xk_ref, xv_ref, wq_ref, wk_ref, wv_ref, dq_ref, dk_ref, dv_ref,
             dxq_ref, dxk_ref, dxv_ref, dwq_ref, dwk_ref, dwv_ref):
        @pl.when(pl.program_id(1) == 0)
        def _():
            for r in (dwq_ref, dwk_ref, dwv_ref):
                r[...] = jnp.zeros_like(r)

        taps = [[w[i:i + 1, :] for i in range(CONV_TAPS)] for w in (wq_ref, wk_ref, wv_ref)]
        keeps = _row_keeps(S)
        _, pull = jax.vjp(lambda *a: _gdn_pre_fn(*a, keeps), xq_ref[0], xk_ref[0], xv_ref[0], *taps)
        dxq, dxk, dxv, dwq, dwk, dwv = pull((dq_ref[0], dk_ref[0], dv_ref[0]))
        dxq_ref[0], dxk_ref[0], dxv_ref[0] = dxq, dxk, dxv
        for ref, dw in ((dwq_ref, dwq), (dwk_ref, dwk), (dwv_ref, dwv)):
            for i in range(CONV_TAPS):
                ref[i:i + 1, :] += dw[i]

    hw = HEADS * HEAD_DIM
    return pl.pallas_call(
        body, grid=(HEADS, B), in_specs=x_specs + w_specs + [out_spec] * 3,
        out_specs=[out_spec] * 3 + [dw_spec] * 3,
        out_shape=[_sds((B, S, hw), F32)] * 3 + [_sds((CONV_TAPS, hw), F32)] * 3,
        compiler_params=_params(("parallel", "arbitrary")), name="gdn_pre_bwd",
    )(gqkv, gqkv, gqkv, conv_w, conv_w, conv_w, dq, dk, dv)


def _chunk_masks():
    i = lax.broadcasted_iota(jnp.int32, (CHUNK, CHUNK), 0)
    j = lax.broadcasted_iota(jnp.int32, (CHUNK, CHUNK), 1)
    lower, after = (j <= i).astype(F32), (j > i).astype(F32)
    return {"le": lower, "le_gt": jnp.concatenate([lower, after], axis=0), "strict": (j < i).astype(F32)}


def _gdn_chunk_fn(groups, masks):
    lane = lax.broadcasted_iota(jnp.int32, (groups, 1, 128), 2)
    head = lax.broadcasted_iota(jnp.int32, (groups, 1, 128), 0) % HEADS
    pick_a, pick_b = (lane == head).astype(F32), (lane == head + HEADS).astype(F32)
    lower, lower_after, strict = (jnp.broadcast_to(masks[n], (groups,) + masks[n].shape) for n in ("le", "le_gt", "strict"))
    ones_row = jnp.ones((1, 1, HEAD_DIM), F32)

    def f(q, k, v, gab, a_row, dt_row, state):
        ga = jnp.sum(gab * pick_a, axis=2, keepdims=True)
        gb = jnp.sum(gab * pick_b, axis=2, keepdims=True)
        a_log = jnp.sum(a_row * pick_a, axis=2, keepdims=True)
        dt_bias = jnp.sum(dt_row * pick_a, axis=2, keepdims=True)
        beta = _sigmoid(gb)
        g = -jnp.exp(a_log) * _softplus(ga + dt_bias)
        g_wide = g * ones_row
        cum, rest = _row_halves(_hi_nn(lower_after, g_wide))
        total = jnp.sum(g_wide, axis=1, keepdims=True)
        diff = _hi_nn(lower, g * strict)
        decay = lower * jnp.exp(diff)
        e_cum = jnp.exp(cum)
        lmat = strict * (beta * _bf_nt(k, k) * decay)
        rhs = jnp.concatenate([v * beta, k * (beta * e_cum)], axis=2)
        rhs = rhs - _hi_nn(lmat, rhs)
        power = lmat
        for _ in range(5):
            power = _hi_nn(power, power)
            rhs = rhs + _hi_nn(power, rhs)
        u, w = _lane_halves(rhs)
        attn = _bf_nt(q, k) * decay
        v_new = u - _bf_nn(w, state)
        o = _bf_nn(q * e_cum, state) + _bf_nn(attn, v_new)
        new_state = state * jnp.exp(total) + _bf_tn(k * jnp.exp(rest), v_new)
        return o, new_state

    return f


def _gdn_chunk_fwd(q, k, v, gab, scal, shards):
    B, S, W = q.shape
    N = S // CHUNK
    ns = len(shards)

    def body(*refs):
        q_ref, k_ref, v_ref, gab_ref, sc_ref = refs[:5]
        src_refs = refs[5:5 + ns]
        o_ref, st_ref = refs[5 + ns:7 + ns]
        dst_refs = refs[7 + ns:7 + 2 * ns]
        state_ref, send_sems, recv_sems, local_sems = refs[7 + 2 * ns:]
        n = pl.program_id(0)

        @pl.when(n == 0)
        def _():
            for start in _gather_copies(src_refs, dst_refs, send_sems, recv_sems, local_sems)[0]:
                start()
            state_ref[...] = jnp.zeros_like(state_ref)

        groups = [(b, h) for b in range(B) for h in range(HEADS)]
        gather = lambda ref: jnp.stack([ref[b, :, h * HEAD_DIM:(h + 1) * HEAD_DIM] for b, h in groups])
        state = state_ref[...]
        for i, (b, h) in enumerate(groups):
            st_ref[b, 0, h] = state[i]
        o, new_state = _gdn_chunk_fn(len(groups), _chunk_masks())(
            gather(q_ref), gather(k_ref), gather(v_ref), jnp.stack([gab_ref[b] for b, _ in groups]), sc_ref[0:1, :], sc_ref[1:2, :], state)
        for i, (b, h) in enumerate(groups):
            o_ref[b, :, h * HEAD_DIM:(h + 1) * HEAD_DIM] = o[i]
        state_ref[...] = new_state

        @pl.when(n == N - 1)
        def _():
            for wait in _gather_copies(src_refs, dst_refs, send_sems, recv_sems, local_sems)[1]:
                wait()

    seq = pl.BlockSpec((B, CHUNK, W), lambda n: (0, n, 0))
    return pl.pallas_call(
        body, grid=(N,),
        in_specs=[seq, seq, seq, pl.BlockSpec((B, CHUNK, GAB_W), lambda n: (0, n, 0)), _const_spec((8, 128))] + [_ANY] * ns,
        out_specs=[seq, pl.BlockSpec((B, 1, HEADS, HEAD_DIM, HEAD_DIM), lambda n: (0, n, 0, 0, 0))] + [_ANY] * ns,
        out_shape=[_sds((B, S, W), F32), _sds((B, N, HEADS, HEAD_DIM, HEAD_DIM), F32)] + [_sds((4,) + s.shape, s.dtype) for s in shards],
        scratch_shapes=[pltpu.VMEM((B * HEADS, HEAD_DIM, HEAD_DIM), F32)] + _gather_scratch(ns),
        compiler_params=_params(("arbitrary",)), name="gdn_chunk_fwd",
    )(q, k, v, gab, scal, *shards)


def _gdn_chunk_bwd(q, k, v, gab, scal, states, do, partials):
    B, S, W = q.shape
    N = S // CHUNK
    ns = len(partials)

    def body(*refs):
        q_ref, k_ref, v_ref, gab_ref, sc_ref, st_ref, do_ref = refs[:7]
        src_refs = refs[7:7 + ns]
        dq_ref, dk_ref, dv_ref, dgab_ref, dsc_ref = refs[7 + ns:12 + ns]
        dst_refs = refs[12 + ns:12 + 2 * ns]
        dstate_ref, send_sems, recv_sems, local_sems = refs[12 + 2 * ns:]
        n = pl.program_id(0)

        @pl.when(n == 0)
        def _():
            for start in _scatter_copies(src_refs, dst_refs, send_sems, recv_sems, local_sems)[0]:
                start()
            dstate_ref[...] = jnp.zeros_like(dstate_ref)
            dsc_ref[...] = jnp.zeros_like(dsc_ref)

        groups = [(b, h) for b in range(B) for h in range(HEADS)]
        gather = lambda ref: jnp.stack([ref[b, :, h * HEAD_DIM:(h + 1) * HEAD_DIM] for b, h in groups])
        _, pull = jax.vjp(_gdn_chunk_fn(len(groups), _chunk_masks()), gather(q_ref), gather(k_ref), gather(v_ref),
                          jnp.stack([gab_ref[b] for b, _ in groups]), sc_ref[0:1, :], sc_ref[1:2, :],
                          jnp.stack([st_ref[b, 0, h] for b, h in groups]))
        dq, dk, dv, dg, d_a, d_dt, dstate = pull((gather(do_ref), dstate_ref[...]))
        for i, (b, h) in enumerate(groups):
            lanes = slice(h * HEAD_DIM, (h + 1) * HEAD_DIM)
            dq_ref[b, :, lanes] = dq[i]
            dk_ref[b, :, lanes] = dk[i]
            dv_ref[b, :, lanes] = dv[i]
        for b in range(B):
            dgab_ref[b] = sum(dg[b * HEADS + h] for h in range(HEADS))
        dstate_ref[...] = dstate
        dsc_ref[0:1, :] += d_a
        dsc_ref[1:2, :] += d_dt

        @pl.when(n == N - 1)
        def _():
            for wait in _scatter_copies(src_refs, dst_refs, send_sems, recv_sems, local_sems)[1]:
                wait()

    seq = pl.BlockSpec((B, CHUNK, W), lambda n: (0, N - 1 - n, 0))
    gab_spec = pl.BlockSpec((B, CHUNK, GAB_W), lambda n: (0, N - 1 - n, 0))
    return pl.pallas_call(
        body, grid=(N,),
        in_specs=[seq, seq, seq, gab_spec, _const_spec((8, 128)),
                  pl.BlockSpec((B, 1, HEADS, HEAD_DIM, HEAD_DIM), lambda n: (0, N - 1 - n, 0, 0, 0)), seq] + [_ANY] * ns,
        out_specs=[seq, seq, seq, gab_spec, _const_spec((8, 128))] + [_ANY] * ns,
        out_shape=[_sds((B, S, W), F32)] * 3 + [_sds((B, S, GAB_W), F32), _sds((8, 128), F32)] + [_scattered_shape(p) for p in partials],
        scratch_shapes=[pltpu.VMEM((B * HEADS, HEAD_DIM, HEAD_DIM), F32)] + _scatter_scratch(ns),
        compiler_params=_params(("arbitrary",)), name="gdn_chunk_bwd",
    )(q, k, v, gab, scal, states, do, *partials)


def _mix_fn(ao, go, gz, w_mla, w_gdn):
    return tuple(_rms(ao[h], w_mla[h]) for h in range(HEADS)) + tuple(_rms(go[h], w_gdn) * _silu(gz[h]) for h in range(HEADS))


def _mix_operands(ao_ref, go_ref, gz_ref, nw_ref):
    blocks = lambda ref: [ref[:, h * HEAD_DIM:(h + 1) * HEAD_DIM] for h in range(HEADS)]
    return blocks(ao_ref), blocks(go_ref), blocks(gz_ref), [nw_ref[h:h + 1, :] for h in range(HEADS)], nw_ref[HEADS:HEADS + 1, :]


def _mix_fwd(ao, go, gz, nw, w_out, x2):
    T, D = x2.shape
    tm = min(TOKEN_TILE, T)
    MW = 2 * HEADS * HEAD_DIM

    def body(ao_ref, go_ref, gz_ref, nw_ref, w_ref, x_ref, mix_ref, h_ref):
        outs = _mix_fn(*_mix_operands(ao_ref, go_ref, gz_ref, nw_ref))
        for i, piece in enumerate(outs):
            mix_ref[:, i * HEAD_DIM:(i + 1) * HEAD_DIM] = piece.astype(BF16)
        h_ref[...] = x_ref[...] + jnp.dot(mix_ref[...], w_ref[...], preferred_element_type=F32)

    half = HEADS * HEAD_DIM
    return pl.pallas_call(
        body, grid=(T // tm,),
        in_specs=[_row_spec(tm, half), _row_spec(tm, half), _row_spec(tm, half), _const_spec((8, 128)), _const_spec((MW, D)),
                  _row_spec(tm, D)],
        out_specs=[_row_spec(tm, MW), _row_spec(tm, D)],
        out_shape=[_sds((T, MW), BF16), _sds((T, D), F32)],
        compiler_params=_params(("parallel",)), name="mix_fwd",
    )(ao, go, gz, nw, w_out, x2)


def _mix_bwd(ao, go, gz, nw, w_out, dh):
    T, D = dh.shape
    tm = min(TOKEN_TILE, T)
    MW = 2 * HEADS * HEAD_DIM
    half = HEADS * HEAD_DIM

    def body(ao_ref, go_ref, gz_ref, nw_ref, w_ref, dh_ref, dao_ref, dgo_ref, dgz_ref, dnw_ref):
        @pl.when(pl.program_id(0) == 0)
        def _():
            dnw_ref[...] = jnp.zeros_like(dnw_ref)

        d_mix = _dg(dh_ref[...].astype(BF16), w_ref[...], 1, 1, None)
        cts = tuple(d_mix[:, i * HEAD_DIM:(i + 1) * HEAD_DIM] for i in range(2 * HEADS))
        _, pull = jax.vjp(_mix_fn, *_mix_operands(ao_ref, go_ref, gz_ref, nw_ref))
        d_ao, d_go, d_gz, d_wm, d_wg = pull(cts)
        for h in range(HEADS):
            lanes = slice(h * HEAD_DIM, (h + 1) * HEAD_DIM)
            dao_ref[:, lanes] = d_ao[h]
            dgo_ref[:, lanes] = d_go[h]
            dgz_ref[:, lanes] = d_gz[h]
            dnw_ref[h:h + 1, :] += d_wm[h]
        dnw_ref[HEADS:HEADS + 1, :] += d_wg

    return pl.pallas_call(
        body, grid=(T // tm,),
        in_specs=[_row_spec(tm, half), _row_spec(tm, half), _row_spec(tm, half), _const_spec((8, 128)), _const_spec((MW, D)),
                  _row_spec(tm, D)],
        out_specs=[_row_spec(tm, half)] * 3 + [_const_spec((8, 128))],
        out_shape=[_sds((T, half), F32)] * 3 + [_sds((8, 128), F32)],
        compiler_params=_params(("arbitrary",)), name="mix_bwd",
    )(ao, go, gz, nw, w_out, dh)


def _up_spec(w_up, tf):
    per_shard = w_up.shape[2] // tf
    return pl.BlockSpec((None, w_up.shape[1], tf), lambda i, j: (j // per_shard, 0, j % per_shard))


def _mlp_fwd(h2, w_mn, w_up, w_down, target):
    T, D = h2.shape
    FF = w_down.shape[0]
    tm, tf = min(TOKEN_TILE, T), min(FF_TILE, w_up.shape[2])
    nf = FF // tf

    def body(h_ref, wn_ref, wu_ref, wd_ref, t_ref, hn_ref, dy_ref, sq_ref, acc_ref):
        j = pl.program_id(1)

        @pl.when(j == 0)
        def _():
            hn_ref[...] = _rms(h_ref[...], wn_ref[...]).astype(BF16)
            acc_ref[...] = jnp.zeros_like(acc_ref)

        up = jnp.dot(hn_ref[...], wu_ref[...], preferred_element_type=F32)
        act = jnp.square(jnp.maximum(up, 0.0)).astype(BF16)
        acc_ref[...] += jnp.dot(act, wd_ref[...], preferred_element_type=F32)

        @pl.when(j == nf - 1)
        def _():
            err = h_ref[...] + acc_ref[...] - t_ref[...]
            dy_ref[...] = err * (1.0 / D)
            sq_ref[...] = jnp.zeros_like(sq_ref) + jnp.sum(err * err)

    tok = lambda w: pl.BlockSpec((tm, w), lambda i, j: (i, 0))
    return pl.pallas_call(
        body, grid=(T // tm, nf),
        in_specs=[tok(D), _const_spec((1, D)), _up_spec(w_up, tf), pl.BlockSpec((tf, D), lambda i, j: (j, 0)), tok(D)],
        out_specs=[tok(D), tok(D), pl.BlockSpec((1, 8, 128), lambda i, j: (i, 0, 0))],
        out_shape=[_sds((T, D), BF16), _sds((T, D), F32), _sds((T // tm, 8, 128), F32)],
        scratch_shapes=[pltpu.VMEM((tm, D), F32)],
        compiler_params=_params(("parallel", "arbitrary")), name="mlp_fwd",
    )(h2, w_mn, w_up, w_down, target)


def _mlp_bwd(h2, w_mn, hn, w_up, w_down, dy):
    T, D = h2.shape
    FF = w_down.shape[0]
    tm, tf = min(TOKEN_TILE, T), min(FF_TILE, w_up.shape[2])
    nf = FF // tf

    def body(h_ref, wn_ref, hn_ref, wu_ref, wd_ref, dy_ref, dh_ref, act_ref, dup_ref, dwn_ref, acc_ref):
        i, j = pl.program_id(0), pl.program_id(1)

        @pl.when((i == 0) & (j == 0))
        def _():
            dwn_ref[...] = jnp.zeros_like(dwn_ref)

        @pl.when(j == 0)
        def _():
            acc_ref[...] = jnp.zeros_like(acc_ref)

        r = jnp.maximum(jnp.dot(hn_ref[...], wu_ref[...], preferred_element_type=F32), 0.0)
        act_ref[...] = (r * r).astype(BF16)
        d_act = _dg(dy_ref[...].astype(BF16), wd_ref[...], 1, 1, None)
        d_up = (d_act * (2.0 * r)).astype(BF16)
        dup_ref[...] = d_up
        acc_ref[...] += _dg(d_up, wu_ref[...], 1, 1, None)

        @pl.when(j == nf - 1)
        def _():
            _, pull = jax.vjp(_rms, h_ref[...], wn_ref[...])
            dh, dwn = pull(acc_ref[...])
            dh_ref[...] = dh + dy_ref[...]
            dwn_ref[...] += dwn

    tok = lambda w: pl.BlockSpec((tm, w), lambda i, j: (i, 0))
    ff = pl.BlockSpec((tm, tf), lambda i, j: (i, j))
    return pl.pallas_call(
        body, grid=(T // tm, nf),
        in_specs=[tok(D), _const_spec((1, D)), tok(D), _up_spec(w_up, tf), pl.BlockSpec((tf, D), lambda i, j: (j, 0)), tok(D)],
        out_specs=[tok(D), ff, ff, _const_spec((1, D))],
        out_shape=[_sds((T, D), F32), _sds((T, FF), BF16), _sds((T, FF), BF16), _sds((1, D), F32)],
        scratch_shapes=[pltpu.VMEM((tm, D), F32)],
        compiler_params=_params(("arbitrary", "arbitrary")), name="mlp_bwd",
    )(h2, w_mn, hn, w_up, w_down, dy)


def _rope_pad(a):
    z = jnp.zeros(a.shape[:-1] + (ROPE_HALF,), a.dtype)
    return jnp.concatenate([a[..., :ROPE_HALF], z, a[..., ROPE_HALF:], z], axis=-1)


def _rope_unpad(a):
    return jnp.concatenate([a[..., :ROPE_HALF], a[..., 2 * ROPE_HALF:3 * ROPE_HALF]], axis=-1)


_G0 = 2 * LORA + ROPE_DIM
_GZ0 = _G0 + GQKV_W
_GA0 = _GZ0 + GZ_W


def _widen_w_in(w):
    pad = jnp.zeros((w.shape[0], GAB_W - 2 * HEADS), w.dtype)
    return jnp.concatenate([w[:, :2 * LORA], _rope_pad(w[:, 2 * LORA:_G0]), w[:, _G0:_GA0], w[:, _GA0:], pad], axis=1)


def _narrow_w_in(w):
    return jnp.concatenate([w[:, :2 * LORA], _rope_unpad(w[:, 2 * LORA:LAT_W]), w[:, LAT_W:PROJ_SPLITS[2][1]],
                            w[:, PROJ_SPLITS[3][0]:PROJ_SPLITS[3][0] + 2 * HEADS]], axis=1)


def _stack_mla(w_uq, w_ukv):
    uq = w_uq.reshape(LORA, HEADS, QK_DIM)
    ukv = w_ukv.reshape(LORA, HEADS, 2 * HEAD_DIM)
    parts = [uq[:, :, :HEAD_DIM], _rope_pad(uq[:, :, HEAD_DIM:]), ukv[:, :, :HEAD_DIM], ukv[:, :, HEAD_DIM:]]
    return jnp.concatenate([p.transpose(1, 0, 2) for p in parts], axis=0)


def _unstack_mla(w):
    p = [w[i * HEADS:(i + 1) * HEADS].transpose(1, 0, 2) for i in range(4)]
    uq = jnp.concatenate([p[0], _rope_unpad(p[1])], axis=-1).reshape(LORA, HEADS * QK_DIM)
    ukv = jnp.concatenate([p[2], p[3]], axis=-1).reshape(LORA, HEADS * 2 * HEAD_DIM)
    return uq, ukv


def _rows8(rows):
    a = jnp.concatenate(rows, axis=0)
    return jnp.pad(a, ((0, 8 - a.shape[0]), (0, 0)))


def _qk_norm_rows(q_norm_w, k_norm_w):
    return _rows8([q_norm_w[:, :HEAD_DIM], _rope_pad(q_norm_w[:, HEAD_DIM:]), k_norm_w[:, :HEAD_DIM], _rope_pad(k_norm_w[:, HEAD_DIM:])])


def _rope_rows():
    inv_freq = ROPE_THETA ** (-jnp.arange(ROPE_HALF, dtype=F32) / ROPE_HALF)
    z = jnp.zeros((ROPE_HALF,), F32)
    freq = jnp.concatenate([inv_freq, z, inv_freq, z])
    sign = jnp.concatenate([-jnp.ones((ROPE_HALF,), F32), z, jnp.ones((ROPE_HALF,), F32), z])
    return _rows8([freq[None], sign[None]])


def _round_up(n, m):
    return -(-n // m) * m


def _column_shards(a):
    return a.reshape(a.shape[0], 4, a.shape[1] // 4).transpose(1, 0, 2)


def _from_column_shards(a):
    return a.transpose(1, 0, 2).reshape(a.shape[1], 4 * a.shape[2])


def _pack_small(arrays):
    rows = [jnp.pad(a.reshape(-1), (0, _round_up(a.size, 128) - a.size)).reshape(-1, 128) for a in arrays]
    packed = jnp.concatenate(rows, axis=0)
    return jnp.pad(packed, ((0, _round_up(packed.shape[0], 8) - packed.shape[0]), (0, 0)))


def _unpack_small(packed, shapes):
    out, r = [], 0
    for s in shapes:
        n = math.prod(s)
        nr = _round_up(n, 128) // 128
        out.append(packed[r:r + nr].reshape(-1)[:n].reshape(s))
        r += nr
    return out


_ANY = pl.BlockSpec(memory_space=pl.ANY)
_OTHER_CHIPS = ((1, 0), (0, 1), (1, 1))


def _here():
    return lax.axis_index("x"), lax.axis_index("y"), lax.axis_index("c")


def _flip(v, bit):
    return 1 - v if bit else v


def _remote(src, dst, send_sems, recv_sems, k, to):
    return pltpu.make_async_remote_copy(src_ref=src, dst_ref=dst, send_sem=send_sems.at[k], recv_sem=recv_sems.at[k],
                                        device_id=to, device_id_type=MESH)


def _gather_copies(srcs, dsts, send_sems, recv_sems, local_sems):
    x, y, c = _here()
    slot = 2 * x + y
    starts, waits = [], []
    for i, (src, dst) in enumerate(zip(srcs, dsts)):
        own = pltpu.make_async_copy(src, dst.at[slot], local_sems.at[i])
        starts.append(own.start)
        waits.append(own.wait)
        for j, (fx, fy) in enumerate(_OTHER_CHIPS):
            cx, cy = _flip(x, fx), _flip(y, fy)
            push = _remote(src, dst.at[slot], send_sems, recv_sems, 3 * i + j, (cx, cy, c))
            landed = dst.at[2 * cx + cy]
            starts.append(push.start)
            waits += [_remote(landed, landed, send_sems, recv_sems, 3 * i + j, (cx, cy, c)).wait_recv, push.wait_send]
    return starts, waits


def _gather_scratch(n):
    return [pltpu.SemaphoreType.DMA((3 * n,)), pltpu.SemaphoreType.DMA((3 * n,)), pltpu.SemaphoreType.DMA((n,))]


def _all_gather(shards, name):
    ns = len(shards)

    def body(*refs):
        starts, waits = _gather_copies(refs[:ns], refs[ns:2 * ns], *refs[2 * ns:])
        for call in starts + waits:
            call()

    return pl.pallas_call(
        body, in_specs=[_ANY] * ns, out_specs=[_ANY] * ns, out_shape=[_sds((4,) + s.shape, s.dtype) for s in shards],
        scratch_shapes=_gather_scratch(ns), name=name,
    )(*shards)


def _scattered_shape(p):
    return _sds((8, p.shape[1] // 2) + p.shape[2:], p.dtype)


def _scatter_copies(srcs, dsts, send_sems, recv_sems, local_sems, whole=0):
    x, y, c = _here()
    me = 4 * x + 2 * y + c
    starts, waits = [], []
    for i, (src, dst) in enumerate(zip(srcs, dsts)):
        def piece(px, py, pc, src=src, entire=i >= len(srcs) - whole):
            if entire:
                return src
            half = src.shape[1] // 2
            return src.at[2 * px + py, pl.ds(pl.multiple_of(pc * half, 16), half)]

        own = pltpu.make_async_copy(piece(x, y, c), dst.at[me], local_sems.at[i])
        starts.append(own.start)
        waits.append(own.wait)
        for k in range(1, 8):
            px, py, pc = _flip(x, k & 4), _flip(y, k & 2), _flip(c, k & 1)
            push = _remote(piece(px, py, pc), dst.at[me], send_sems, recv_sems, 7 * i + k - 1, (px, py, pc))
            landed = dst.at[4 * px + 2 * py + pc]
            starts.append(push.start)
            waits += [_remote(landed, landed, send_sems, recv_sems, 7 * i + k - 1, (px, py, pc)).wait_recv, push.wait_send]
    return starts, waits


def _scatter_scratch(n):
    return [pltpu.SemaphoreType.DMA((7 * n,)), pltpu.SemaphoreType.DMA((7 * n,)), pltpu.SemaphoreType.DMA((n,))]


def _scatter(partials, wholes, name):
    ns = len(partials) + len(wholes)

    def body(*refs):
        starts, waits = _scatter_copies(refs[:ns], refs[ns:2 * ns], *refs[2 * ns:], whole=len(wholes))
        for call in starts + waits:
            call()

    return pl.pallas_call(
        body, in_specs=[_ANY] * ns, out_specs=[_ANY] * ns,
        out_shape=[_scattered_shape(p) for p in partials] + [_sds((8,) + s.shape, s.dtype) for s in wholes],
        scratch_shapes=_scatter_scratch(ns), name=name,
    )(*partials, *wholes)


def _swap_copies(srcs, dsts, send_sems, recv_sems, local_sems):
    x, y, c = _here()
    sibling = (x, y, 1 - c)
    starts, waits = [], []
    for i, (src, dst) in enumerate(zip(srcs, dsts)):
        own = pltpu.make_async_copy(src, dst.at[c], local_sems.at[i])
        push = _remote(src, dst.at[c], send_sems, recv_sems, i, sibling)
        landed = dst.at[1 - c]
        starts += [own.start, push.start]
        waits += [_remote(landed, landed, send_sems, recv_sems, i, sibling).wait_recv, push.wait_send, own.wait]
    return starts, waits


def _swap_scratch(n):
    return [pltpu.SemaphoreType.DMA((n,)), pltpu.SemaphoreType.DMA((n,)), pltpu.SemaphoreType.DMA((n,))]


def _exchange_halves(halves):
    ns = len(halves)

    def body(*refs):
        starts, waits = _swap_copies(refs[:ns], refs[ns:2 * ns], *refs[2 * ns:])
        for call in starts + waits:
            call()

    return pl.pallas_call(
        body, in_specs=[_ANY] * ns, out_specs=[_ANY] * ns, out_shape=[_sds((2,) + h.shape, h.dtype) for h in halves],
        scratch_shapes=_swap_scratch(ns), name="exchange_halves",
    )(*halves)


def _row_tile(rows, row_bytes, budget):
    tr = rows
    while tr * row_bytes > budget and tr % 16 == 0:
        tr //= 2
    return tr


def _sum_slots(parts, name):
    _, rows, cols = parts.shape
    tr = _row_tile(rows, 8 * cols * 4, 2 * 1024 * 1024)

    def body(p_ref, o_ref):
        acc = p_ref[0].astype(F32)
        for d in range(1, 8):
            acc = acc + p_ref[d].astype(F32)
        o_ref[...] = acc

    return pl.pallas_call(
        body, grid=(rows // tr,), in_specs=[pl.BlockSpec((8, tr, cols), lambda i: (0, i, 0))],
        out_specs=pl.BlockSpec((tr, cols), lambda i: (i, 0)), out_shape=_sds((rows, cols), F32),
        compiler_params=_params(("parallel",)), name=name,
    )(parts)


def _adamw(w, g, m, v, name):
    rows, cols = w.shape
    tr = _row_tile(rows, 7 * cols * 4, 4 * 1024 * 1024)

    def body(w_ref, g_ref, m_ref, v_ref, d_ref, mo_ref, vo_ref):
        g = g_ref[...]
        m = ADAM_B1 * m_ref[...] + (1.0 - ADAM_B1) * g
        v = ADAM_B2 * v_ref[...] + (1.0 - ADAM_B2) * jnp.square(g)
        m_hat = m / (1.0 - ADAM_B1 ** ADAM_STEP)
        v_hat = v / (1.0 - ADAM_B2 ** ADAM_STEP)
        d_ref[...] = -ADAM_LR * (m_hat / (jnp.sqrt(v_hat) + ADAM_EPS) + ADAM_WD * w_ref[...])
        mo_ref[...] = m
        vo_ref[...] = v

    spec = pl.BlockSpec((tr, cols), lambda i: (i, 0))
    return pl.pallas_call(
        body, grid=(rows // tr,), in_specs=[spec] * 4, out_specs=[spec] * 3, out_shape=[_sds((rows, cols), F32)] * 3,
        compiler_params=_params(("parallel",)), name=name,
    )(w, g, m, v)


def kernel(x, positions, attn_norm_w, w_in, q_lat_norm_w, w_uq, kv_lat_norm_w, w_ukv, q_norm_w, k_norm_w, mla_out_norm_w, conv_w, a_log, dt_bias, gdn_norm_w, w_out, mlp_norm_w, w_up, w_down, loss_target, m_attn_norm_w, m_w_in, m_q_lat_norm_w, m_w_uq, m_kv_lat_norm_w, m_w_ukv, m_q_norm_w, m_k_norm_w, m_mla_out_norm_w, m_conv_w, m_a_log, m_dt_bias, m_gdn_norm_w, m_w_out, m_mlp_norm_w, m_w_up, m_w_down, v_attn_norm_w, v_w_in, v_q_lat_norm_w, v_w_uq, v_kv_lat_norm_w, v_w_ukv, v_q_norm_w, v_k_norm_w, v_mla_out_norm_w, v_conv_w, v_a_log, v_dt_bias, v_gdn_norm_w, v_w_out, v_mlp_norm_w, v_w_up, v_w_down):
    w = dict(zip(WEIGHTS, (attn_norm_w, w_in, q_lat_norm_w, w_uq, kv_lat_norm_w, w_ukv, q_norm_w, k_norm_w, mla_out_norm_w, conv_w,
                           a_log, dt_bias, gdn_norm_w, w_out, mlp_norm_w, w_up, w_down)))
    m = dict(zip(WEIGHTS, (m_attn_norm_w, m_w_in, m_q_lat_norm_w, m_w_uq, m_kv_lat_norm_w, m_w_ukv, m_q_norm_w, m_k_norm_w,
                           m_mla_out_norm_w, m_conv_w, m_a_log, m_dt_bias, m_gdn_norm_w, m_w_out, m_mlp_norm_w, m_w_up, m_w_down)))
    v = dict(zip(WEIGHTS, (v_attn_norm_w, v_w_in, v_q_lat_norm_w, v_w_uq, v_kv_lat_norm_w, v_w_ukv, v_q_norm_w, v_k_norm_w,
                           v_mla_out_norm_w, v_conv_w, v_a_log, v_dt_bias, v_gdn_norm_w, v_w_out, v_mlp_norm_w, v_w_up, v_w_down)))
    B, S, D = x.shape
    T = B * S
    x2, pos, target = x.reshape(T, D), positions.reshape(T, 1), loss_target.reshape(T, D)
    seq = lambda a: a.reshape(B, S, a.shape[-1])
    tok = lambda a: a.reshape(T, a.shape[-1])
    local = {n: w[n][0] for n in SHARDED}

    g_in, g_uq, g_ukv, g_conv = _all_gather([local["w_in"].astype(BF16), local["w_uq"].astype(BF16), local["w_ukv"].astype(BF16),
                                             local["conv_w"]], "gather_first_weights")
    w_in_p = _widen_w_in(_from_column_shards(g_in))
    w_mla = _stack_mla(_from_column_shards(g_uq), _from_column_shards(g_ukv))
    conv_full = _from_column_shards(g_conv)
    ln_w = jnp.concatenate([q_lat_norm_w, kv_lat_norm_w], axis=0)
    qk_nw = _qk_norm_rows(q_norm_w, k_norm_w)
    rope_rows = _rope_rows()
    scal = _rows8([jnp.pad(a_log, ((0, 0), (0, 128 - HEADS))), jnp.pad(dt_bias, ((0, 0), (0, 128 - HEADS)))])
    mix_nw = _rows8([mla_out_norm_w[0], gdn_norm_w])

    xn, lat, gqkv, gz, gab = _in_proj_fwd(x2, attn_norm_w, w_in_p)
    q, k, v_att = _mla_pre_fwd(lat, pos, ln_w, w_mla, qk_nw, rope_rows)
    ao, lse, g_down = _attn_fwd(seq(q), seq(k), seq(v_att), [local["w_down"].astype(BF16)])
    gq, gk, gv = _gdn_pre_fwd(seq(gqkv), conv_full)
    go, states, g_out, w_up_b = _gdn_chunk_fwd(gq, gk, gv, seq(gab), scal, [local["w_out"].astype(BF16), local["w_up"].astype(BF16)])
    w_out_b = g_out.reshape(-1, D)
    w_down_b = g_down.reshape(-1, D)
    mix, h2 = _mix_fwd(tok(ao), tok(go), gz, mix_nw, w_out_b, x2)
    hn, dy, sq = _mlp_fwd(h2, mlp_norm_w, w_up_b, w_down_b, target)
    loss = lax.psum(jnp.sum(sq[:, 0, 0]) * (0.5 / D), ("x", "y", "c"))

    dh, act, d_up, d_mlp_nw = _mlp_bwd(h2, mlp_norm_w, hn, w_up_b, w_down_b, dy)
    p_down = _wgrad(act, dy, "wgrad_down").reshape(4, -1, D)
    p_up = _wgrad(hn, d_up, "wgrad_up", column_shards=4)
    d_ao, d_go, d_gz, d_mix_nw = _mix_bwd(tok(ao), tok(go), gz, mix_nw, w_out_b, dh)
    p_out = _wgrad(mix, dh, "wgrad_out").reshape(4, -1, D)
    d_gq, d_gk, d_gv, d_gab, d_scal, s_up, s_down, s_out = _gdn_chunk_bwd(gq, gk, gv, seq(gab), scal, states, seq(d_go),
                                                                          [p_up, p_down, p_out])
    dxq, dxk, dxv, dcq, dck, dcv = _gdn_pre_bwd(seq(gqkv), conv_full, d_gq, d_gk, d_gv)
    early = ("w_up", "w_down", "w_out")
    early_halves = [_sum_slots(s, "sum_" + n) for n, s in zip(early, (s_up, s_down, s_out))]
    dq, dk, dv, *early_grads = _attn_bwd(seq(q), seq(k), seq(v_att), ao, lse, seq(d_ao), early_halves)
    d_lat, d_ln, d_w_mla, d_qk_nw = _mla_pre_bwd(lat, pos, ln_w, w_mla, qk_nw, rope_rows, tok(dq), tok(dk), tok(dv))
    d_gqkv = jnp.concatenate([dxq, dxk, dxv], axis=-1)
    grad_x2, d_proj, d_attn_nw = _in_proj_bwd(d_lat, tok(d_gqkv), d_gz, tok(d_gab), w_in_p, x2, attn_norm_w, dh)
    p_in = _column_shards(_narrow_w_in(_wgrad(xn, d_proj, "wgrad_in")))
    p_uq, p_ukv = (_column_shards(a).astype(BF16) for a in _unstack_mla(d_w_mla))
    small_partial = {
        "attn_norm_w": d_attn_nw, "q_lat_norm_w": d_ln[0:1], "kv_lat_norm_w": d_ln[1:2],
        "q_norm_w": jnp.concatenate([d_qk_nw[0:1], _rope_unpad(d_qk_nw[1:2])], axis=-1),
        "k_norm_w": jnp.concatenate([d_qk_nw[2:3], _rope_unpad(d_qk_nw[3:4])], axis=-1),
        "mla_out_norm_w": d_mix_nw[None, :HEADS], "a_log": d_scal[0:1, :HEADS], "dt_bias": d_scal[1:2, :HEADS],
        "gdn_norm_w": d_mix_nw[HEADS:HEADS + 1], "mlp_norm_w": d_mlp_nw,
    }
    conv_partial = jnp.concatenate([dcq, dck, dcv], axis=-1)
    s_in, s_uq, s_ukv, s_small = _scatter([p_in, p_uq, p_ukv], [_pack_small([small_partial[n] for n in SMALL] + [conv_partial])],
                                          "scatter_last_partials")

    late = ("w_in", "w_uq", "w_ukv")
    late_grads = _exchange_halves([_sum_slots(s, "sum_" + n) for n, s in zip(late, (s_in, s_uq, s_ukv))])
    names = early + late
    grad = {n: g.reshape(local[n].shape) for n, g in zip(names, list(early_grads) + list(late_grads))}
    small_shapes = [w[n].shape for n in SMALL]
    *g_small, g_conv_all = _unpack_small(_sum_slots(s_small, "sum_small"), small_shapes + [conv_partial.shape])
    grad.update(zip(SMALL, g_small))
    conv_cols = local["conv_w"].shape[1]
    grad["conv_w"] = lax.dynamic_slice_in_dim(g_conv_all, (2 * lax.axis_index("x") + lax.axis_index("y")) * conv_cols, conv_cols, axis=1)

    delta, new_m, new_v = {}, {}, {}
    for n in names:
        delta[n], new_m[n], new_v[n] = _adamw(local[n], grad[n], m[n][0], v[n][0], "adamw_" + n)
    packed_names = SMALL + ("conv_w",)
    packed_shapes = small_shapes + [local["conv_w"].shape]
    take = lambda d: _pack_small([d[n][0] if n == "conv_w" and d[n].ndim == 3 else d[n] for n in packed_names])
    outs = _adamw(take(w), take(grad), take(m), take(v), "adamw_small")
    for d, packed in zip((delta, new_m, new_v), outs):
        d.update(zip(packed_names, _unpack_small(packed, packed_shapes)))

    def in_order(d):
        return [d[n].reshape(w[n].shape) for n in WEIGHTS]

    return (loss, grad_x2.reshape(B, S, D), *in_order(grad), *in_order(delta), *in_order(new_m), *in_order(new_v))
```

```python
import functools
import math

import jax
import jax.numpy as jnp
from jax import lax
from jax.experimental import pallas as pl
from jax.experimental.pallas import tpu as pltpu

F32 = jnp.float32
BF16 = jnp.bfloat16
MESH = pl.DeviceIdType.MESH

EPS = 1e-6
HEADS = 4
HEAD_DIM = 128
ROPE_DIM = 64
ROPE_HALF = 32
QK_DIM = 192
QK_PAD = 256
LORA = 256
CHUNK = 64
CONV_TAPS = 4
ROPE_THETA = 10000.0
ATTN_SCALE = QK_DIM ** -0.5

LAT_W = 640
GQKV_W = 3 * HEADS * HEAD_DIM
GZ_W = HEADS * HEAD_DIM
GAB_W = 128
PROJ_SPLITS = ((0, LAT_W), (LAT_W, LAT_W + GQKV_W), (LAT_W + GQKV_W, LAT_W + GQKV_W + GZ_W),
               (LAT_W + GQKV_W + GZ_W, LAT_W + GQKV_W + GZ_W + GAB_W))
PROJ_W = PROJ_SPLITS[-1][1]

ADAM_LR = 0.001
ADAM_B1 = 0.9
ADAM_B2 = 0.999
ADAM_EPS = 1e-08
ADAM_WD = 0.01
ADAM_STEP = 10

TOKEN_TILE = 512
MLP_TOKEN_TILE = 1024
FF_TILE = 512
ATTN_TILE = 512
ATTN_HEADS_PER_STEP = 2
WGRAD_OUT_BYTES = 8 * 1024 * 1024
VMEM_LIMIT = 48 * 1024 * 1024

SHARDED = ("w_in", "w_uq", "w_ukv", "conv_w", "w_out", "w_up", "w_down")
SMALL = ("attn_norm_w", "q_lat_norm_w", "kv_lat_norm_w", "q_norm_w", "k_norm_w", "mla_out_norm_w", "a_log", "dt_bias",
         "gdn_norm_w", "mlp_norm_w")
WEIGHTS = ("attn_norm_w", "w_in", "q_lat_norm_w", "w_uq", "kv_lat_norm_w", "w_ukv", "q_norm_w", "k_norm_w", "mla_out_norm_w",
           "conv_w", "a_log", "dt_bias", "gdn_norm_w", "w_out", "mlp_norm_w", "w_up", "w_down")


def _sds(shape, dtype):
    return jax.ShapeDtypeStruct(shape, dtype)


def _params(semantics):
    return pltpu.CompilerParams(dimension_semantics=semantics, vmem_limit_bytes=VMEM_LIMIT)


def _block(n):
    for b in (512, 256, 128):
        if n % b == 0:
            return b
    return n


def _dg(a, b, ca, cb, prec):
    lead = a.ndim - 2
    batch = (tuple(range(lead)),) * 2
    return lax.dot_general(a, b, (((ca + lead,), (cb + lead,)), batch), precision=prec, preferred_element_type=F32)


def _split_bf16(a):
    hi = a.astype(BF16)
    return hi, (a - hi.astype(F32)).astype(BF16)


def _dot_bf16(a, b, ca, cb):
    return _dg(a.astype(BF16), b.astype(BF16), ca, cb, None)


def _dot_bf16x3(a, b, ca, cb):
    a_hi, a_lo = _split_bf16(a)
    b_hi, b_lo = _split_bf16(b)
    lead = a.ndim - 2
    return _dg(jnp.concatenate([a_hi, a_hi, a_lo], axis=ca + lead), jnp.concatenate([b_hi, b_lo, b_hi], axis=cb + lead), ca, cb, None)


def _matmul_family(dot):
    def nn_raw(a, b):
        return dot(a, b, 1, 0)

    def nt_raw(a, b):
        return dot(a, b, 1, 1)

    def tn_raw(a, b):
        return dot(a, b, 0, 0)

    @jax.custom_vjp
    def nn(a, b):
        return nn_raw(a, b)

    nn.defvjp(lambda a, b: (nn_raw(a, b), (a, b)), lambda r, g: (nt_raw(g, r[1]), tn_raw(r[0], g)))

    @jax.custom_vjp
    def nt(a, b):
        return nt_raw(a, b)

    nt.defvjp(lambda a, b: (nt_raw(a, b), (a, b)), lambda r, g: (nn_raw(g, r[1]), tn_raw(g, r[0])))

    @jax.custom_vjp
    def tn(a, b):
        return tn_raw(a, b)

    tn.defvjp(lambda a, b: (tn_raw(a, b), (a, b)), lambda r, g: (nt_raw(r[1], g), nn_raw(r[0], g)))
    return nn, nt, tn


_bf_nn, _bf_nt, _bf_tn = _matmul_family(_dot_bf16)
_hi_nn, _hi_nt, _hi_tn = _matmul_family(_dot_bf16x3)


@jax.custom_vjp
def _lane_halves(x):
    n = x.shape[-1] // 2
    return x[..., :n], x[..., n:]


_lane_halves.defvjp(lambda x: (_lane_halves(x), None), lambda _, g: (jnp.concatenate(g, axis=-1),))


@jax.custom_vjp
def _row_halves(x):
    n = x.shape[-2] // 2
    return x[..., :n, :], x[..., n:, :]


_row_halves.defvjp(lambda x: (_row_halves(x), None), lambda _, g: (jnp.concatenate(g, axis=-2),))


@jax.custom_vjp
def _swap_halves(t):
    return pltpu.roll(t, 64, 1)


_swap_halves.defvjp(lambda t: (pltpu.roll(t, 64, 1), None), lambda _, g: (pltpu.roll(g, 64, 1),))


@functools.partial(jax.custom_vjp, nondiff_argnums=(2,))
def _shift_rows(x, keep, s):
    return pltpu.roll(x, s, 0) * keep


def _shift_rows_fwd(x, keep, s):
    return pltpu.roll(x, s, 0) * keep, keep


def _shift_rows_bwd(s, keep, g):
    return pltpu.roll(g * keep, keep.shape[0] - s, 0), jnp.zeros_like(keep)


_shift_rows.defvjp(_shift_rows_fwd, _shift_rows_bwd)


def _sigmoid(x):
    return 0.5 * jnp.tanh(0.5 * x) + 0.5


def _softplus(x):
    return jnp.maximum(x, 0.0) + jnp.log(1.0 + jnp.exp(jnp.minimum(x, -x)))


def _silu(x):
    return x * _sigmoid(x)


def _rms(x, w, n=None):
    n = x.shape[-1] if n is None else n
    r = lax.rsqrt(jnp.sum(x * x, axis=-1, keepdims=True) * (1.0 / n) + EPS)
    return x * r * w


def _rope(t, cos_f, sin_f):
    return t * cos_f + _swap_halves(t) * sin_f


def _rope_tables(pos_col, freq_row, sign_row):
    ang = pos_col.astype(F32) * freq_row
    return jnp.cos(ang), jnp.sin(ang) * sign_row


def _onehot_row(lane):
    return (lax.broadcasted_iota(jnp.int32, (1, 128), 1) == lane).astype(F32)


def _row_spec(tm, w):
    return pl.BlockSpec((tm, w), lambda i: (i, 0))


def _const_spec(shape):
    return pl.BlockSpec(shape, lambda *_: (0,) * len(shape))


def _in_proj_fwd(x2, w_an, w_in_p):
    T, D = x2.shape
    tm = min(TOKEN_TILE, T)

    def body(x_ref, wn_ref, w_ref, xn_ref, lat_ref, gqkv_ref, gz_ref, gab_ref):
        x = x_ref[...]
        r = lax.rsqrt(jnp.mean(x * x, axis=-1, keepdims=True) + EPS)
        xn = (x * r * wn_ref[...]).astype(BF16)
        xn_ref[...] = xn
        for ref, (a, b) in zip((lat_ref, gqkv_ref, gz_ref, gab_ref), PROJ_SPLITS):
            ref[...] = jnp.dot(xn, w_ref[:, a:b], preferred_element_type=F32)

    widths = [b - a for a, b in PROJ_SPLITS]
    return pl.pallas_call(
        body, grid=(T // tm,),
        in_specs=[_row_spec(tm, D), _const_spec((1, D)), _const_spec((D, PROJ_W))],
        out_specs=[_row_spec(tm, D)] + [_row_spec(tm, w) for w in widths],
        out_shape=[_sds((T, D), BF16)] + [_sds((T, w), F32) for w in widths],
        compiler_params=_params(("parallel",)), name="in_proj_fwd",
    )(x2, w_an, w_in_p)


def _in_proj_bwd(pieces, w_in_p, x2, w_an, dh):
    T, D = x2.shape
    tm = min(TOKEN_TILE, T)
    widths = [p.shape[1] for p in pieces]
    starts = [sum(widths[:i]) for i in range(len(widths))]
    assert sum(widths) == PROJ_W

    def body(*refs):
        piece_refs = refs[:len(pieces)]
        w_ref, x_ref, wn_ref, dh_ref, dx_ref, dp_ref, dwn_ref = refs[len(pieces):]

        @pl.when(pl.program_id(0) == 0)
        def _():
            dwn_ref[...] = jnp.zeros_like(dwn_ref)

        dxn = jnp.zeros((tm, D), F32)
        for ref, a, width in zip(piece_refs, starts, widths):
            piece = ref[...].astype(BF16)
            dp_ref[:, a:a + width] = piece
            dxn += _dg(piece, w_ref[:, a:a + width], 1, 1, None)
        _, pull = jax.vjp(_rms, x_ref[...], wn_ref[...])
        dx, dwn = pull(dxn)
        dx_ref[...] = dx + dh_ref[...]
        dwn_ref[...] += dwn

    return pl.pallas_call(
        body, grid=(T // tm,),
        in_specs=[_row_spec(tm, w) for w in widths] + [_const_spec((D, PROJ_W)), _row_spec(tm, D), _const_spec((1, D)),
                                                       _row_spec(tm, D)],
        out_specs=[_row_spec(tm, D), _row_spec(tm, PROJ_W), _const_spec((1, D))],
        out_shape=[_sds((T, D), F32), _sds((T, PROJ_W), BF16), _sds((1, D), F32)],
        compiler_params=_params(("arbitrary",)), name="in_proj_bwd",
    )(*pieces, w_in_p, x2, w_an, dh)


def _wgrad(a, b, name, column_shards=1, out_dtype=BF16):
    T, k1 = a.shape
    k2 = b.shape[1]
    per_shard = k2 // column_shards
    tt = min(TOKEN_TILE, T)
    b1 = k1
    while b1 * k2 * 4 > WGRAD_OUT_BYTES and b1 % 256 == 0:
        b1 //= 2
    step = _block(per_shard)

    def body(a_ref, b_ref, o_ref, acc_ref):
        t = pl.program_id(1)

        @pl.when(t == 0)
        def _():
            acc_ref[...] = jnp.zeros_like(acc_ref)

        a_t = a_ref[...].astype(BF16).T
        for c0 in range(0, k2, step):
            part = jnp.dot(a_t, b_ref[:, c0:c0 + step].astype(BF16), preferred_element_type=F32)
            if column_shards == 1:
                acc_ref[:, c0:c0 + step] += part
            else:
                acc_ref[c0 // per_shard, :, c0 % per_shard:c0 % per_shard + step] += part

        @pl.when(t == T // tt - 1)
        def _():
            o_ref[...] = acc_ref[...].astype(o_ref.dtype)

    if column_shards == 1:
        block, out_spec, out_shape = (b1, k2), pl.BlockSpec((b1, k2), lambda i, t: (i, 0)), _sds((k1, k2), out_dtype)
    else:
        block = (column_shards, b1, per_shard)
        out_spec, out_shape = pl.BlockSpec(block, lambda i, t: (0, i, 0)), _sds((column_shards, k1, per_shard), out_dtype)
    return pl.pallas_call(
        body, grid=(k1 // b1, T // tt),
        in_specs=[pl.BlockSpec((tt, b1), lambda i, t: (t, i)), pl.BlockSpec((tt, k2), lambda i, t: (t, 0))],
        out_specs=out_spec, out_shape=out_shape, scratch_shapes=[pltpu.VMEM(block, F32)],
        compiler_params=_params(("parallel", "arbitrary")), name=name,
    )(a, b)


def _mla_pre_fn(q_lat, kv_lat, kpe, ln_q, ln_kv, w_list, qn_n, qn_p, kn_n, kn_p, cos_f, sin_f):
    qn = _rms(q_lat, ln_q)
    kvn = _rms(kv_lat, ln_kv)
    kp = _rope(_rms(kpe, kn_p, ROPE_DIM), cos_f, sin_f)
    outs = []
    for h in range(HEADS):
        outs.append(_rms(_bf_nn(qn, w_list[h]), qn_n))
        outs.append(_rope(_rms(_bf_nn(qn, w_list[HEADS + h]), qn_p, ROPE_DIM), cos_f, sin_f))
        outs.append(_rms(_bf_nn(kvn, w_list[2 * HEADS + h]), kn_n))
        outs.append(_bf_nn(kvn, w_list[3 * HEADS + h]))
    return tuple(outs) + (kp,)


def _mla_pre_operands(lat_ref, pos_ref, ln_ref, w_ref, nw_ref, rope_ref):
    cos_f, sin_f = _rope_tables(pos_ref[...], rope_ref[0:1, :], rope_ref[1:2, :])
    diff = (lat_ref[:, 0:LORA], lat_ref[:, LORA:2 * LORA], lat_ref[:, 2 * LORA:LAT_W], ln_ref[0:1, :], ln_ref[1:2, :],
            [w_ref[i].astype(F32) for i in range(4 * HEADS)], nw_ref[0:1, :], nw_ref[1:2, :], nw_ref[2:3, :], nw_ref[3:4, :])
    return diff, cos_f, sin_f


def _mla_pre_fwd(lat, pos, ln_w, w_mla, nw, rope_rows):
    T = lat.shape[0]
    tm = min(TOKEN_TILE, T)

    def body(lat_ref, pos_ref, ln_ref, w_ref, nw_ref, rope_ref, q_ref, k_ref, v_ref):
        diff, cos_f, sin_f = _mla_pre_operands(lat_ref, pos_ref, ln_ref, w_ref, nw_ref, rope_ref)
        outs = _mla_pre_fn(*diff, cos_f, sin_f)
        kp = outs[-1].astype(BF16)
        for h in range(HEADS):
            q_n, q_p, k_n, v = outs[4 * h:4 * h + 4]
            q_ref[:, h * QK_PAD:h * QK_PAD + HEAD_DIM] = q_n.astype(BF16)
            q_ref[:, h * QK_PAD + HEAD_DIM:(h + 1) * QK_PAD] = q_p.astype(BF16)
            k_ref[:, h * QK_PAD:h * QK_PAD + HEAD_DIM] = k_n.astype(BF16)
            k_ref[:, h * QK_PAD + HEAD_DIM:(h + 1) * QK_PAD] = kp
            v_ref[:, h * HEAD_DIM:(h + 1) * HEAD_DIM] = v.astype(BF16)

    return pl.pallas_call(
        body, grid=(T // tm,),
        in_specs=[_row_spec(tm, LAT_W), _row_spec(tm, 1), _const_spec((2, LORA)), _const_spec((4 * HEADS, LORA, 128)),
                  _const_spec((8, 128)), _const_spec((8, 128))],
        out_specs=[_row_spec(tm, HEADS * QK_PAD), _row_spec(tm, HEADS * QK_PAD), _row_spec(tm, HEADS * HEAD_DIM)],
        out_shape=[_sds((T, HEADS * QK_PAD), BF16), _sds((T, HEADS * QK_PAD), BF16), _sds((T, HEADS * HEAD_DIM), BF16)],
        compiler_params=_params(("parallel",)), name="mla_pre_fwd",
    )(lat, pos, ln_w, w_mla, nw, rope_rows)


def _mla_pre_bwd(lat, pos, ln_w, w_mla, nw, rope_rows, dq, dk, dv):
    T = lat.shape[0]
    tm = min(TOKEN_TILE, T)

    def body(lat_ref, pos_ref, ln_ref, w_ref, nw_ref, rope_ref, dq_ref, dk_ref, dv_ref, dlat_ref, dln_ref, dw_ref, dnw_ref):
        @pl.when(pl.program_id(0) == 0)
        def _():
            dln_ref[...] = jnp.zeros_like(dln_ref)
            dw_ref[...] = jnp.zeros_like(dw_ref)
            dnw_ref[...] = jnp.zeros_like(dnw_ref)

        diff, cos_f, sin_f = _mla_pre_operands(lat_ref, pos_ref, ln_ref, w_ref, nw_ref, rope_ref)
        _, pull = jax.vjp(lambda *a: _mla_pre_fn(*a, cos_f, sin_f), *diff)
        cts = []
        d_kp = jnp.zeros((tm, 128), F32)
        for h in range(HEADS):
            cts.append(dq_ref[:, h * QK_PAD:h * QK_PAD + HEAD_DIM])
            cts.append(dq_ref[:, h * QK_PAD + HEAD_DIM:(h + 1) * QK_PAD])
            cts.append(dk_ref[:, h * QK_PAD:h * QK_PAD + HEAD_DIM])
            cts.append(dv_ref[:, h * HEAD_DIM:(h + 1) * HEAD_DIM])
            d_kp += dk_ref[:, h * QK_PAD + HEAD_DIM:(h + 1) * QK_PAD]
        d_ql, d_kvl, d_kpe, d_lnq, d_lnkv, d_w, d_qn_n, d_qn_p, d_kn_n, d_kn_p = pull(tuple(cts) + (d_kp,))
        dlat_ref[:, 0:LORA] = d_ql
        dlat_ref[:, LORA:2 * LORA] = d_kvl
        dlat_ref[:, 2 * LORA:LAT_W] = d_kpe
        dln_ref[0:1, :] += d_lnq
        dln_ref[1:2, :] += d_lnkv
        for i in range(4 * HEADS):
            dw_ref[i] += d_w[i]
        for i, d in enumerate((d_qn_n, d_qn_p, d_kn_n, d_kn_p)):
            dnw_ref[i:i + 1, :] += d

    return pl.pallas_call(
        body, grid=(T // tm,),
        in_specs=[_row_spec(tm, LAT_W), _row_spec(tm, 1), _const_spec((2, LORA)), _const_spec((4 * HEADS, LORA, 128)),
                  _const_spec((8, 128)), _const_spec((8, 128)),
                  _row_spec(tm, HEADS * QK_PAD), _row_spec(tm, HEADS * QK_PAD), _row_spec(tm, HEADS * HEAD_DIM)],
        out_specs=[_row_spec(tm, LAT_W), _const_spec((2, LORA)), _const_spec((4 * HEADS, LORA, 128)), _const_spec((8, 128))],
        out_shape=[_sds((T, LAT_W), F32), _sds((2, LORA), F32), _sds((4 * HEADS, LORA, 128), F32), _sds((8, 128), F32)],
        compiler_params=_params(("arbitrary",)), name="mla_pre_bwd",
    )(lat, pos, ln_w, w_mla, nw, rope_rows, dq, dk, dv)


def _causal_mask(i, j, tq, tk):
    row = i * tq + lax.broadcasted_iota(jnp.int32, (tq, tk), 0)
    col = j * tk + lax.broadcasted_iota(jnp.int32, (tq, tk), 1)
    return col <= row


def _attn_fwd(q, k, v, shards):
    B, S, _ = q.shape
    t = min(ATTN_TILE, S)
    nq = S // t
    ns = len(shards)

    hp = ATTN_HEADS_PER_STEP
    qk = lambda h: slice(h * QK_PAD, (h + 1) * QK_PAD)
    vd = lambda h: slice(h * HEAD_DIM, (h + 1) * HEAD_DIM)

    def body(*refs):
        q_ref, k_ref, v_ref = refs[:3]
        src_refs = refs[3:3 + ns]
        o_ref, lse_ref = refs[3 + ns:5 + ns]
        dst_refs = refs[5 + ns:5 + 2 * ns]
        sems = refs[5 + 2 * ns:]
        b, g, i = pl.program_id(0), pl.program_id(1), pl.program_id(2)
        qb = [q_ref[0, :, qk(h)] for h in range(hp)]

        @pl.when((b == 0) & (g == 0) & (i == 0))
        def _():
            for start in _gather_copies(src_refs, dst_refs, *sems)[0]:
                start()

        def step(j, carry, diagonal):
            rows = pl.ds(pl.multiple_of(j * t, t), t)
            s = [_dg(qb[h], k_ref[0, rows, qk(h)], 1, 1, None) * ATTN_SCALE for h in range(hp)]
            if diagonal:
                keep = _causal_mask(0, 0, t, t)
                s = [jnp.where(keep, x, -1e30) for x in s]
            m_new = [jnp.maximum(carry[h][0], jnp.max(s[h], axis=-1, keepdims=True)) for h in range(hp)]
            p = [jnp.exp(s[h] - m_new[h]) for h in range(hp)]
            alpha = [jnp.exp(carry[h][0] - m_new[h]) for h in range(hp)]
            l = [alpha[h] * carry[h][1] + jnp.sum(p[h], axis=-1, keepdims=True) for h in range(hp)]
            pv = [jnp.dot(p[h].astype(BF16), v_ref[0, rows, vd(h)], preferred_element_type=F32) for h in range(hp)]
            return tuple((m_new[h], l[h], alpha[h] * carry[h][2] + pv[h]) for h in range(hp))

        init = tuple((jnp.full((t, 1), -1e30, F32), jnp.zeros((t, 1), F32), jnp.zeros((t, HEAD_DIM), F32)) for _ in range(hp))
        below = lax.fori_loop(0, i, lambda j, carry: step(j, carry, False), init)
        for h, (m, l, acc) in enumerate(step(i, below, True)):
            o_ref[0, :, vd(h)] = acc / l
            lse_ref[0, h] = m + jnp.log(l)

        @pl.when((b == B - 1) & (g == HEADS // hp - 1) & (i == nq - 1))
        def _():
            for wait in _gather_copies(src_refs, dst_refs, *sems)[1]:
                wait()

    return pl.pallas_call(
        body, grid=(B, HEADS // hp, nq),
        in_specs=[pl.BlockSpec((1, t, hp * QK_PAD), lambda b, g, i: (b, i, g)),
                  pl.BlockSpec((1, S, hp * QK_PAD), lambda b, g, i: (b, 0, g)),
                  pl.BlockSpec((1, S, hp * HEAD_DIM), lambda b, g, i: (b, 0, g))] + [_ANY] * ns,
        out_specs=[pl.BlockSpec((1, t, hp * HEAD_DIM), lambda b, g, i: (b, i, g)),
                   pl.BlockSpec((1, hp, t, 1), lambda b, g, i: (b, g, i, 0))] + [_ANY] * ns,
        out_shape=[_sds((B, S, HEADS * HEAD_DIM), F32), _sds((B, HEADS, S, 1), F32)] + [_sds((4,) + s.shape, s.dtype) for s in shards],
        scratch_shapes=_gather_scratch(ns),
        compiler_params=_params(("arbitrary", "arbitrary", "arbitrary")), name="attn_fwd",
    )(q, k, v, *shards)


def _attn_bwd(q, k, v, o, lse, do, halves):
    B, S, _ = q.shape
    t = min(ATTN_TILE, S)
    nq = S // t
    ns = len(halves)

    hp = ATTN_HEADS_PER_STEP
    qk = lambda h: slice(h * QK_PAD, (h + 1) * QK_PAD)
    vd = lambda h: slice(h * HEAD_DIM, (h + 1) * HEAD_DIM)
    heads = range(hp)

    def body(*refs):
        q_ref, k_ref, v_ref, o_ref, lse_ref, do_ref = refs[:6]
        src_refs = refs[6:6 + ns]
        dq_ref, dk_ref, dv_ref = refs[6 + ns:9 + ns]
        dst_refs = refs[9 + ns:9 + 2 * ns]
        dsum_ref, send_sems, recv_sems, local_sems = refs[9 + 2 * ns:]
        b, g, j = pl.program_id(0), pl.program_id(1), pl.program_id(2)

        @pl.when((b == 0) & (g == 0) & (j == 0))
        def _():
            for start in _swap_copies(src_refs, dst_refs, send_sems, recv_sems, local_sems)[0]:
                start()

        @pl.when(j == 0)
        def _():
            dq_ref[...] = jnp.zeros_like(dq_ref)
            for h in heads:
                dsum_ref[h] = jnp.sum(do_ref[0, :, vd(h)] * o_ref[0, :, vd(h)], axis=-1, keepdims=True)

        kb = [k_ref[0, :, qk(h)] for h in heads]
        vb = [v_ref[0, :, vd(h)] for h in heads]

        def step(i, carry, diagonal):
            rows = pl.ds(pl.multiple_of(i * t, t), t)
            qb = [q_ref[0, rows, qk(h)] for h in heads]
            dob = [do_ref[0, rows, vd(h)].astype(BF16) for h in heads]
            s = [_dg(qb[h], kb[h], 1, 1, None) * ATTN_SCALE for h in heads]
            p = [jnp.exp(s[h] - lse_ref[0, h, rows, :]) for h in heads]
            if diagonal:
                keep = _causal_mask(0, 0, t, t)
                p = [jnp.where(keep, x, 0.0) for x in p]
            dp = [_dg(dob[h], vb[h], 1, 1, None) for h in heads]
            dv = [carry[h][1] + _dg(p[h].astype(BF16), dob[h], 0, 0, None) for h in heads]
            ds = [(p[h] * (dp[h] - dsum_ref[h, rows, :]) * ATTN_SCALE).astype(BF16) for h in heads]
            for h in heads:
                dq_ref[0, rows, qk(h)] += jnp.dot(ds[h], kb[h], preferred_element_type=F32)
            return tuple((carry[h][0] + _dg(ds[h], qb[h], 0, 0, None), dv[h]) for h in heads)

        zeros = tuple((jnp.zeros((t, QK_PAD), F32), jnp.zeros((t, HEAD_DIM), F32)) for _ in heads)
        on_diagonal = step(j, zeros, True)
        done = lax.fori_loop(j + 1, nq, lambda i, carry: step(i, carry, False), on_diagonal)
        for h, (dk, dv) in enumerate(done):
            dk_ref[0, :, qk(h)] = dk
            dv_ref[0, :, vd(h)] = dv

        @pl.when((b == B - 1) & (g == HEADS // hp - 1) & (j == nq - 1))
        def _():
            for wait in _swap_copies(src_refs, dst_refs, send_sems, recv_sems, local_sems)[1]:
                wait()

    return pl.pallas_call(
        body, grid=(B, HEADS // hp, nq),
        in_specs=[pl.BlockSpec((1, S, hp * QK_PAD), lambda b, g, j: (b, 0, g)),
                  pl.BlockSpec((1, t, hp * QK_PAD), lambda b, g, j: (b, j, g)),
                  pl.BlockSpec((1, t, hp * HEAD_DIM), lambda b, g, j: (b, j, g)),
                  pl.BlockSpec((1, S, hp * HEAD_DIM), lambda b, g, j: (b, 0, g)),
                  pl.BlockSpec((1, hp, S, 1), lambda b, g, j: (b, g, 0, 0)),
                  pl.BlockSpec((1, S, hp * HEAD_DIM), lambda b, g, j: (b, 0, g))] + [_ANY] * ns,
        out_specs=[pl.BlockSpec((1, S, hp * QK_PAD), lambda b, g, j: (b, 0, g)),
                   pl.BlockSpec((1, t, hp * QK_PAD), lambda b, g, j: (b, j, g)),
                   pl.BlockSpec((1, t, hp * HEAD_DIM), lambda b, g, j: (b, j, g))] + [_ANY] * ns,
        out_shape=[_sds((B, S, HEADS * QK_PAD), F32), _sds((B, S, HEADS * QK_PAD), F32), _sds((B, S, HEADS * HEAD_DIM), F32)]
                  + [_sds((2,) + s.shape, s.dtype) for s in halves],
        scratch_shapes=[pltpu.VMEM((hp, S, 1), F32)] + _swap_scratch(ns),
        compiler_params=_params(("arbitrary", "arbitrary", "arbitrary")), name="attn_bwd",
    )(q, k, v, o, lse, do, *halves)


def _gdn_pre_fn(xq, xk, xv, wq, wk, wv, keeps):
    def conv_silu(x, w):
        acc = x * w[3]
        for s in (1, 2, 3):
            acc = acc + _shift_rows(x, keeps[s - 1], s) * w[3 - s]
        return _silu(acc)

    def l2(x):
        return x * lax.rsqrt(jnp.sum(x * x, axis=-1, keepdims=True) + EPS)

    return l2(conv_silu(xq, wq)) * (HEAD_DIM ** -0.5), l2(conv_silu(xk, wk)), conv_silu(xv, wv)


def _gdn_pre_specs(S):
    x_specs = [pl.BlockSpec((1, S, HEAD_DIM), lambda h, b, g=g: (b, 0, g * HEADS + h)) for g in range(3)]
    w_specs = [pl.BlockSpec((CONV_TAPS, HEAD_DIM), lambda h, b, g=g: (0, g * HEADS + h)) for g in range(3)]
    out_spec = pl.BlockSpec((1, S, HEAD_DIM), lambda h, b: (b, 0, h))
    return x_specs, w_specs, out_spec


def _row_keeps(S):
    t = lax.broadcasted_iota(jnp.int32, (S, HEAD_DIM), 0)
    return [(t >= s).astype(F32) for s in (1, 2, 3)]


def _gdn_pre_fwd(gqkv, conv_w):
    B, S, _ = gqkv.shape
    x_specs, w_specs, out_spec = _gdn_pre_specs(S)

    def body(xq_ref, xk_ref, xv_ref, wq_ref, wk_ref, wv_ref, q_ref, k_ref, v_ref):
        taps = [[w[i:i + 1, :] for i in range(CONV_TAPS)] for w in (wq_ref, wk_ref, wv_ref)]
        q, k, v = _gdn_pre_fn(xq_ref[0], xk_ref[0], xv_ref[0], *taps, _row_keeps(S))
        q_ref[0], k_ref[0], v_ref[0] = q, k, v

    return pl.pallas_call(
        body, grid=(HEADS, B), in_specs=x_specs + w_specs, out_specs=[out_spec] * 3,
        out_shape=[_sds((B, S, HEADS * HEAD_DIM), F32)] * 3,
        compiler_params=_params(("parallel", "parallel")), name="gdn_pre_fwd",
    )(gqkv, gqkv, gqkv, conv_w, conv_w, conv_w)


def _gdn_pre_bwd(gqkv, conv_w, dq, dk, dv):
    B, S, _ = gqkv.shape
    x_specs, w_specs, out_spec = _gdn_pre_specs(S)
    dw_spec = pl.BlockSpec((CONV_TAPS, HEAD_DIM), lambda h, b: (0, h))

    def body(xq_ref, xk_ref, xv_ref, wq_ref, wk_ref, wv_ref, dq_ref, dk_ref, dv_ref,
             dxq_ref, dxk_ref, dxv_ref, dwq_ref, dwk_ref, dwv_ref):
        @pl.when(pl.program_id(1) == 0)
        def _():
            for r in (dwq_ref, dwk_ref, dwv_ref):
                r[...] = jnp.zeros_like(r)

        taps = [[w[i:i + 1, :] for i in range(CONV_TAPS)] for w in (wq_ref, wk_ref, wv_ref)]
        keeps = _row_keeps(S)
        _, pull = jax.vjp(lambda *a: _gdn_pre_fn(*a, keeps), xq_ref[0], xk_ref[0], xv_ref[0], *taps)
        dxq, dxk, dxv, dwq, dwk, dwv = pull((dq_ref[0], dk_ref[0], dv_ref[0]))
        dxq_ref[0], dxk_ref[0], dxv_ref[0] = dxq, dxk, dxv
        for ref, dw in ((dwq_ref, dwq), (dwk_ref, dwk), (dwv_ref, dwv)):
            for i in range(CONV_TAPS):
                ref[i:i + 1, :] += dw[i]

    hw = HEADS * HEAD_DIM
    return pl.pallas_call(
        body, grid=(HEADS, B), in_specs=x_specs + w_specs + [out_spec] * 3,
        out_specs=[out_spec] * 3 + [dw_spec] * 3,
        out_shape=[_sds((B, S, hw), F32)] * 3 + [_sds((CONV_TAPS, hw), F32)] * 3,
        compiler_params=_params(("parallel", "arbitrary")), name="gdn_pre_bwd",
    )(gqkv, gqkv, gqkv, conv_w, conv_w, conv_w, dq, dk, dv)


def _chunk_masks():
    i = lax.broadcasted_iota(jnp.int32, (CHUNK, CHUNK), 0)
    j = lax.broadcasted_iota(jnp.int32, (CHUNK, CHUNK), 1)
    lower, after = (j <= i).astype(F32), (j > i).astype(F32)
    return {"le": lower, "le_gt": jnp.concatenate([lower, after], axis=0), "strict": (j < i).astype(F32)}


def _gdn_chunk_fn(groups, masks):
    lane = lax.broadcasted_iota(jnp.int32, (groups, 1, 128), 2)
    head = lax.broadcasted_iota(jnp.int32, (groups, 1, 128), 0) % HEADS
    pick_a, pick_b = (lane == head).astype(F32), (lane == head + HEADS).astype(F32)
    lower, lower_after, strict = (jnp.broadcast_to(masks[n], (groups,) + masks[n].shape) for n in ("le", "le_gt", "strict"))
    ones_row = jnp.ones((1, 1, HEAD_DIM), F32)

    def f(q, k, v, gab, a_row, dt_row, state):
        ga = jnp.sum(gab * pick_a, axis=2, keepdims=True)
        gb = jnp.sum(gab * pick_b, axis=2, keepdims=True)
        a_log = jnp.sum(a_row * pick_a, axis=2, keepdims=True)
        dt_bias = jnp.sum(dt_row * pick_a, axis=2, keepdims=True)
        beta = _sigmoid(gb)
        g = -jnp.exp(a_log) * _softplus(ga + dt_bias)
        g_wide = g * ones_row
        cum, rest = _row_halves(_hi_nn(lower_after, g_wide))
        total = jnp.sum(g_wide, axis=1, keepdims=True)
        diff = _hi_nn(lower, g * strict)
        decay = lower * jnp.exp(diff)
        e_cum = jnp.exp(cum)
        lmat = strict * (beta * _bf_nt(k, k) * decay)
        rhs = jnp.concatenate([v * beta, k * (beta * e_cum)], axis=2)
        rhs = rhs - _hi_nn(lmat, rhs)
        power = lmat
        for _ in range(5):
            power = _hi_nn(power, power)
            rhs = rhs + _hi_nn(power, rhs)
        u, w = _lane_halves(rhs)
        attn = _bf_nt(q, k) * decay
        v_new = u - _bf_nn(w, state)
        o = _bf_nn(q * e_cum, state) + _bf_nn(attn, v_new)
        new_state = state * jnp.exp(total) + _bf_tn(k * jnp.exp(rest), v_new)
        return o, new_state

    return f


def _gdn_chunk_fwd(q, k, v, gab, scal, shards):
    B, S, W = q.shape
    N = S // CHUNK
    ns = len(shards)

    def body(*refs):
        q_ref, k_ref, v_ref, gab_ref, sc_ref = refs[:5]
        src_refs = refs[5:5 + ns]
        o_ref, st_ref = refs[5 + ns:7 + ns]
        dst_refs = refs[7 + ns:7 + 2 * ns]
        state_ref, send_sems, recv_sems, local_sems = refs[7 + 2 * ns:]
        n = pl.program_id(0)

        @pl.when(n == 0)
        def _():
            for start in _gather_copies(src_refs, dst_refs, send_sems, recv_sems, local_sems)[0]:
                start()
            state_ref[...] = jnp.zeros_like(state_ref)

        groups = [(b, h) for b in range(B) for h in range(HEADS)]
        gather = lambda ref: jnp.stack([ref[b, :, h * HEAD_DIM:(h + 1) * HEAD_DIM] for b, h in groups])
        state = state_ref[...]
        for i, (b, h) in enumerate(groups):
            st_ref[b, 0, h] = state[i]
        o, new_state = _gdn_chunk_fn(len(groups), _chunk_masks())(
            gather(q_ref), gather(k_ref), gather(v_ref), jnp.stack([gab_ref[b] for b, _ in groups]), sc_ref[0:1, :], sc_ref[1:2, :], state)
        for i, (b, h) in enumerate(groups):
            o_ref[b, :, h * HEAD_DIM:(h + 1) * HEAD_DIM] = o[i]
        state_ref[...] = new_state

        @pl.when(n == N - 1)
        def _():
            for wait in _gather_copies(src_refs, dst_refs, send_sems, recv_sems, local_sems)[1]:
                wait()

    seq = pl.BlockSpec((B, CHUNK, W), lambda n: (0, n, 0))
    return pl.pallas_call(
        body, grid=(N,),
        in_specs=[seq, seq, seq, pl.BlockSpec((B, CHUNK, GAB_W), lambda n: (0, n, 0)), _const_spec((8, 128))] + [_ANY] * ns,
        out_specs=[seq, pl.BlockSpec((B, 1, HEADS, HEAD_DIM, HEAD_DIM), lambda n: (0, n, 0, 0, 0))] + [_ANY] * ns,
        out_shape=[_sds((B, S, W), F32), _sds((B, N, HEADS, HEAD_DIM, HEAD_DIM), F32)] + [_sds((4,) + s.shape, s.dtype) for s in shards],
        scratch_shapes=[pltpu.VMEM((B * HEADS, HEAD_DIM, HEAD_DIM), F32)] + _gather_scratch(ns),
        compiler_params=_params(("arbitrary",)), name="gdn_chunk_fwd",
    )(q, k, v, gab, scal, *shards)


def _gdn_chunk_bwd(q, k, v, gab, scal, states, do, partials):
    B, S, W = q.shape
    N = S // CHUNK
    ns = len(partials)

    def body(*refs):
        q_ref, k_ref, v_ref, gab_ref, sc_ref, st_ref, do_ref = refs[:7]
        src_refs = refs[7:7 + ns]
        dq_ref, dk_ref, dv_ref, dgab_ref, dsc_ref = refs[7 + ns:12 + ns]
        dst_refs = refs[12 + ns:12 + 2 * ns]
        dstate_ref, send_sems, recv_sems, local_sems = refs[12 + 2 * ns:]
        n = pl.program_id(0)

        @pl.when(n == 0)
        def _():
            for start in _scatter_copies(src_refs, dst_refs, send_sems, recv_sems, local_sems)[0]:
                start()
            dstate_ref[...] = jnp.zeros_like(dstate_ref)
            dsc_ref[...] = jnp.zeros_like(dsc_ref)

        groups = [(b, h) for b in range(B) for h in range(HEADS)]
        gather = lambda ref: jnp.stack([ref[b, :, h * HEAD_DIM:(h + 1) * HEAD_DIM] for b, h in groups])
        _, pull = jax.vjp(_gdn_chunk_fn(len(groups), _chunk_masks()), gather(q_ref), gather(k_ref), gather(v_ref),
                          jnp.stack([gab_ref[b] for b, _ in groups]), sc_ref[0:1, :], sc_ref[1:2, :],
                          jnp.stack([st_ref[b, 0, h] for b, h in groups]))
        dq, dk, dv, dg, d_a, d_dt, dstate = pull((gather(do_ref), dstate_ref[...]))
        for i, (b, h) in enumerate(groups):
            lanes = slice(h * HEAD_DIM, (h + 1) * HEAD_DIM)
            dq_ref[b, :, lanes] = dq[i]
            dk_ref[b, :, lanes] = dk[i]
            dv_ref[b, :, lanes] = dv[i]
        for b in range(B):
            dgab_ref[b] = sum(dg[b * HEADS + h] for h in range(HEADS))
        dstate_ref[...] = dstate
        dsc_ref[0:1, :] += d_a
        dsc_ref[1:2, :] += d_dt

        @pl.when(n == N - 1)
        def _():
            for wait in _scatter_copies(src_refs, dst_refs, send_sems, recv_sems, local_sems)[1]:
                wait()

    seq = pl.BlockSpec((B, CHUNK, W), lambda n: (0, N - 1 - n, 0))
    gab_spec = pl.BlockSpec((B, CHUNK, GAB_W), lambda n: (0, N - 1 - n, 0))
    return pl.pallas_call(
        body, grid=(N,),
        in_specs=[seq, seq, seq, gab_spec, _const_spec((8, 128)),
                  pl.BlockSpec((B, 1, HEADS, HEAD_DIM, HEAD_DIM), lambda n: (0, N - 1 - n, 0, 0, 0)), seq] + [_ANY] * ns,
        out_specs=[seq, seq, seq, gab_spec, _const_spec((8, 128))] + [_ANY] * ns,
        out_shape=[_sds((B, S, W), F32)] * 3 + [_sds((B, S, GAB_W), F32), _sds((8, 128), F32)] + [_scattered_shape(p) for p in partials],
        scratch_shapes=[pltpu.VMEM((B * HEADS, HEAD_DIM, HEAD_DIM), F32)] + _scatter_scratch(ns),
        compiler_params=_params(("arbitrary",)), name="gdn_chunk_bwd",
    )(q, k, v, gab, scal, states, do, *partials)


def _mix_fn(ao, go, gz, w_mla, w_gdn):
    return tuple(_rms(ao[h], w_mla[h]) for h in range(HEADS)) + tuple(_rms(go[h], w_gdn) * _silu(gz[h]) for h in range(HEADS))


def _mix_operands(ao_ref, go_ref, gz_ref, nw_ref):
    blocks = lambda ref: [ref[:, h * HEAD_DIM:(h + 1) * HEAD_DIM] for h in range(HEADS)]
    return blocks(ao_ref), blocks(go_ref), blocks(gz_ref), [nw_ref[h:h + 1, :] for h in range(HEADS)], nw_ref[HEADS:HEADS + 1, :]


def _mix_fwd(ao, go, gz, nw, w_out, x2):
    T, D = x2.shape
    tm = min(TOKEN_TILE, T)
    MW = 2 * HEADS * HEAD_DIM

    def body(ao_ref, go_ref, gz_ref, nw_ref, w_ref, x_ref, mix_ref, h_ref):
        outs = _mix_fn(*_mix_operands(ao_ref, go_ref, gz_ref, nw_ref))
        for i, piece in enumerate(outs):
            mix_ref[:, i * HEAD_DIM:(i + 1) * HEAD_DIM] = piece.astype(BF16)
        h_ref[...] = x_ref[...] + jnp.dot(mix_ref[...], w_ref[...], preferred_element_type=F32)

    half = HEADS * HEAD_DIM
    return pl.pallas_call(
        body, grid=(T // tm,),
        in_specs=[_row_spec(tm, half), _row_spec(tm, half), _row_spec(tm, half), _const_spec((8, 128)), _const_spec((MW, D)),
                  _row_spec(tm, D)],
        out_specs=[_row_spec(tm, MW), _row_spec(tm, D)],
        out_shape=[_sds((T, MW), BF16), _sds((T, D), F32)],
        compiler_params=_params(("parallel",)), name="mix_fwd",
    )(ao, go, gz, nw, w_out, x2)


def _mix_bwd(ao, go, gz, nw, w_out, dh):
    T, D = dh.shape
    tm = min(TOKEN_TILE, T)
    MW = 2 * HEADS * HEAD_DIM
    half = HEADS * HEAD_DIM

    def body(ao_ref, go_ref, gz_ref, nw_ref, w_ref, dh_ref, dao_ref, dgo_ref, dgz_ref, dnw_ref):
        @pl.when(pl.program_id(0) == 0)
        def _():
            dnw_ref[...] = jnp.zeros_like(dnw_ref)

        d_mix = _dg(dh_ref[...].astype(BF16), w_ref[...], 1, 1, None)
        cts = tuple(d_mix[:, i * HEAD_DIM:(i + 1) * HEAD_DIM] for i in range(2 * HEADS))
        _, pull = jax.vjp(_mix_fn, *_mix_operands(ao_ref, go_ref, gz_ref, nw_ref))
        d_ao, d_go, d_gz, d_wm, d_wg = pull(cts)
        for h in range(HEADS):
            lanes = slice(h * HEAD_DIM, (h + 1) * HEAD_DIM)
            dao_ref[:, lanes] = d_ao[h]
            dgo_ref[:, lanes] = d_go[h]
            dgz_ref[:, lanes] = d_gz[h]
            dnw_ref[h:h + 1, :] += d_wm[h]
        dnw_ref[HEADS:HEADS + 1, :] += d_wg

    return pl.pallas_call(
        body, grid=(T // tm,),
        in_specs=[_row_spec(tm, half), _row_spec(tm, half), _row_spec(tm, half), _const_spec((8, 128)), _const_spec((MW, D)),
                  _row_spec(tm, D)],
        out_specs=[_row_spec(tm, half)] * 3 + [_const_spec((8, 128))],
        out_shape=[_sds((T, half), F32)] * 3 + [_sds((8, 128), F32)],
        compiler_params=_params(("arbitrary",)), name="mix_bwd",
    )(ao, go, gz, nw, w_out, dh)


def _up_spec(w_up, tf):
    per_shard = w_up.shape[2] // tf
    return pl.BlockSpec((None, w_up.shape[1], tf), lambda i, j: (j // per_shard, 0, j % per_shard))


def _mlp_fwd(h2, w_mn, w_up, w_down, target):
    T, D = h2.shape
    FF = w_down.shape[0]
    tm, tf = min(MLP_TOKEN_TILE, T), min(FF_TILE, w_up.shape[2])
    nf = FF // tf

    def body(h_ref, wn_ref, wu_ref, wd_ref, t_ref, hn_ref, dy_ref, sq_ref, acc_ref):
        j = pl.program_id(1)

        @pl.when(j == 0)
        def _():
            hn_ref[...] = _rms(h_ref[...], wn_ref[...]).astype(BF16)
            acc_ref[...] = jnp.zeros_like(acc_ref)

        up = jnp.dot(hn_ref[...], wu_ref[...], preferred_element_type=F32)
        act = jnp.square(jnp.maximum(up, 0.0)).astype(BF16)
        acc_ref[...] += jnp.dot(act, wd_ref[...], preferred_element_type=F32)

        @pl.when(j == nf - 1)
        def _():
            err = h_ref[...] + acc_ref[...] - t_ref[...]
            dy_ref[...] = err * (1.0 / D)
            sq_ref[...] = jnp.zeros_like(sq_ref) + jnp.sum(err * err)

    tok = lambda w: pl.BlockSpec((tm, w), lambda i, j: (i, 0))
    return pl.pallas_call(
        body, grid=(T // tm, nf),
        in_specs=[tok(D), _const_spec((1, D)), _up_spec(w_up, tf), pl.BlockSpec((tf, D), lambda i, j: (j, 0)), tok(D)],
        out_specs=[tok(D), tok(D), pl.BlockSpec((1, 8, 128), lambda i, j: (i, 0, 0))],
        out_shape=[_sds((T, D), BF16), _sds((T, D), F32), _sds((T // tm, 8, 128), F32)],
        scratch_shapes=[pltpu.VMEM((tm, D), F32)],
        compiler_params=_params(("parallel", "arbitrary")), name="mlp_fwd",
    )(h2, w_mn, w_up, w_down, target)


def _mlp_bwd(h2, w_mn, hn, w_up, w_down, dy):
    T, D = h2.shape
    FF = w_down.shape[0]
    tm, tf = min(MLP_TOKEN_TILE, T), min(FF_TILE, w_up.shape[2])
    nf = FF // tf

    def body(h_ref, wn_ref, hn_ref, wu_ref, wd_ref, dy_ref, dh_ref, act_ref, dup_ref, dwn_ref, acc_ref):
        i, j = pl.program_id(0), pl.program_id(1)

        @pl.when((i == 0) & (j == 0))
        def _():
            dwn_ref[...] = jnp.zeros_like(dwn_ref)

        @pl.when(j == 0)
        def _():
            acc_ref[...] = jnp.zeros_like(acc_ref)

        r = jnp.maximum(jnp.dot(hn_ref[...], wu_ref[...], preferred_element_type=F32), 0.0)
        act_ref[...] = (r * r).astype(BF16)
        d_act = _dg(dy_ref[...].astype(BF16), wd_ref[...], 1, 1, None)
        d_up = (d_act * (2.0 * r)).astype(BF16)
        dup_ref[...] = d_up
        acc_ref[...] += _dg(d_up, wu_ref[...], 1, 1, None)

        @pl.when(j == nf - 1)
        def _():
            _, pull = jax.vjp(_rms, h_ref[...], wn_ref[...])
            dh, dwn = pull(acc_ref[...])
            dh_ref[...] = dh + dy_ref[...]
            dwn_ref[...] += dwn

    tok = lambda w: pl.BlockSpec((tm, w), lambda i, j: (i, 0))
    ff = pl.BlockSpec((tm, tf), lambda i, j: (i, j))
    return pl.pallas_call(
        body, grid=(T // tm, nf),
        in_specs=[tok(D), _const_spec((1, D)), tok(D), _up_spec(w_up, tf), pl.BlockSpec((tf, D), lambda i, j: (j, 0)), tok(D)],
        out_specs=[tok(D), ff, ff, _const_spec((1, D))],
        out_shape=[_sds((T, D), F32), _sds((T, FF), BF16), _sds((T, FF), BF16), _sds((1, D), F32)],
        scratch_shapes=[pltpu.VMEM((tm, D), F32)],
        compiler_params=_params(("arbitrary", "arbitrary")), name="mlp_bwd",
    )(h2, w_mn, hn, w_up, w_down, dy)


def _rope_pad(a):
    z = jnp.zeros(a.shape[:-1] + (ROPE_HALF,), a.dtype)
    return jnp.concatenate([a[..., :ROPE_HALF], z, a[..., ROPE_HALF:], z], axis=-1)


def _rope_unpad(a):
    return jnp.concatenate([a[..., :ROPE_HALF], a[..., 2 * ROPE_HALF:3 * ROPE_HALF]], axis=-1)


_G0 = 2 * LORA + ROPE_DIM
_GZ0 = _G0 + GQKV_W
_GA0 = _GZ0 + GZ_W


def _widen_w_in(w):
    pad = jnp.zeros((w.shape[0], GAB_W - 2 * HEADS), w.dtype)
    return jnp.concatenate([w[:, :2 * LORA], _rope_pad(w[:, 2 * LORA:_G0]), w[:, _G0:_GA0], w[:, _GA0:], pad], axis=1)


def _narrow_w_in(w):
    return jnp.concatenate([w[:, :2 * LORA], _rope_unpad(w[:, 2 * LORA:LAT_W]), w[:, LAT_W:PROJ_SPLITS[2][1]],
                            w[:, PROJ_SPLITS[3][0]:PROJ_SPLITS[3][0] + 2 * HEADS]], axis=1)


def _stack_mla(w_uq, w_ukv):
    uq = w_uq.reshape(LORA, HEADS, QK_DIM)
    ukv = w_ukv.reshape(LORA, HEADS, 2 * HEAD_DIM)
    parts = [uq[:, :, :HEAD_DIM], _rope_pad(uq[:, :, HEAD_DIM:]), ukv[:, :, :HEAD_DIM], ukv[:, :, HEAD_DIM:]]
    return jnp.concatenate([p.transpose(1, 0, 2) for p in parts], axis=0)


def _unstack_mla(w):
    p = [w[i * HEADS:(i + 1) * HEADS].transpose(1, 0, 2) for i in range(4)]
    uq = jnp.concatenate([p[0], _rope_unpad(p[1])], axis=-1).reshape(LORA, HEADS * QK_DIM)
    ukv = jnp.concatenate([p[2], p[3]], axis=-1).reshape(LORA, HEADS * 2 * HEAD_DIM)
    return uq, ukv


def _rows8(rows):
    a = jnp.concatenate(rows, axis=0)
    return jnp.pad(a, ((0, 8 - a.shape[0]), (0, 0)))


def _qk_norm_rows(q_norm_w, k_norm_w):
    return _rows8([q_norm_w[:, :HEAD_DIM], _rope_pad(q_norm_w[:, HEAD_DIM:]), k_norm_w[:, :HEAD_DIM], _rope_pad(k_norm_w[:, HEAD_DIM:])])


def _rope_rows():
    inv_freq = ROPE_THETA ** (-jnp.arange(ROPE_HALF, dtype=F32) / ROPE_HALF)
    z = jnp.zeros((ROPE_HALF,), F32)
    freq = jnp.concatenate([inv_freq, z, inv_freq, z])
    sign = jnp.concatenate([-jnp.ones((ROPE_HALF,), F32), z, jnp.ones((ROPE_HALF,), F32), z])
    return _rows8([freq[None], sign[None]])


def _round_up(n, m):
    return -(-n // m) * m


def _column_shards(a):
    return a.reshape(a.shape[0], 4, a.shape[1] // 4).transpose(1, 0, 2)


def _from_column_shards(a):
    return a.transpose(1, 0, 2).reshape(a.shape[1], 4 * a.shape[2])


def _pack_small(arrays):
    rows = [jnp.pad(a.reshape(-1), (0, _round_up(a.size, 128) - a.size)).reshape(-1, 128) for a in arrays]
    packed = jnp.concatenate(rows, axis=0)
    return jnp.pad(packed, ((0, _round_up(packed.shape[0], 8) - packed.shape[0]), (0, 0)))


def _unpack_small(packed, shapes):
    out, r = [], 0
    for s in shapes:
        n = math.prod(s)
        nr = _round_up(n, 128) // 128
        out.append(packed[r:r + nr].reshape(-1)[:n].reshape(s))
        r += nr
    return out


_ANY = pl.BlockSpec(memory_space=pl.ANY)
_OTHER_CHIPS = ((1, 0), (0, 1), (1, 1))


def _here():
    return lax.axis_index("x"), lax.axis_index("y"), lax.axis_index("c")


def _flip(v, bit):
    return 1 - v if bit else v


def _remote(src, dst, send_sems, recv_sems, k, to):
    return pltpu.make_async_remote_copy(src_ref=src, dst_ref=dst, send_sem=send_sems.at[k], recv_sem=recv_sems.at[k],
                                        device_id=to, device_id_type=MESH)


def _gather_copies(srcs, dsts, send_sems, recv_sems, local_sems):
    x, y, c = _here()
    slot = 2 * x + y
    starts, waits = [], []
    for i, (src, dst) in enumerate(zip(srcs, dsts)):
        own = pltpu.make_async_copy(src, dst.at[slot], local_sems.at[i])
        starts.append(own.start)
        waits.append(own.wait)
        for j, (fx, fy) in enumerate(_OTHER_CHIPS):
            cx, cy = _flip(x, fx), _flip(y, fy)
            push = _remote(src, dst.at[slot], send_sems, recv_sems, 3 * i + j, (cx, cy, c))
            landed = dst.at[2 * cx + cy]
            starts.append(push.start)
            waits += [_remote(landed, landed, send_sems, recv_sems, 3 * i + j, (cx, cy, c)).wait_recv, push.wait_send]
    return starts, waits


def _gather_scratch(n):
    return [pltpu.SemaphoreType.DMA((3 * n,)), pltpu.SemaphoreType.DMA((3 * n,)), pltpu.SemaphoreType.DMA((n,))]


def _all_gather(shards, name):
    ns = len(shards)

    def body(*refs):
        starts, waits = _gather_copies(refs[:ns], refs[ns:2 * ns], *refs[2 * ns:])
        for call in starts + waits:
            call()

    return pl.pallas_call(
        body, in_specs=[_ANY] * ns, out_specs=[_ANY] * ns, out_shape=[_sds((4,) + s.shape, s.dtype) for s in shards],
        scratch_shapes=_gather_scratch(ns), name=name,
    )(*shards)


def _scattered_shape(p):
    return _sds((8, p.shape[1] // 2) + p.shape[2:], p.dtype)


def _scatter_copies(srcs, dsts, send_sems, recv_sems, local_sems, whole=0):
    x, y, c = _here()
    me = 4 * x + 2 * y + c
    starts, waits = [], []
    for i, (src, dst) in enumerate(zip(srcs, dsts)):
        def piece(px, py, pc, src=src, entire=i >= len(srcs) - whole):
            if entire:
                return src
            half = src.shape[1] // 2
            return src.at[2 * px + py, pl.ds(pl.multiple_of(pc * half, 16), half)]

        own = pltpu.make_async_copy(piece(x, y, c), dst.at[me], local_sems.at[i])
        starts.append(own.start)
        waits.append(own.wait)
        for k in range(1, 8):
            px, py, pc = _flip(x, k & 4), _flip(y, k & 2), _flip(c, k & 1)
            push = _remote(piece(px, py, pc), dst.at[me], send_sems, recv_sems, 7 * i + k - 1, (px, py, pc))
            landed = dst.at[4 * px + 2 * py + pc]
            starts.append(push.start)
            waits += [_remote(landed, landed, send_sems, recv_sems, 7 * i + k - 1, (px, py, pc)).wait_recv, push.wait_send]
    return starts, waits


def _scatter_scratch(n):
    return [pltpu.SemaphoreType.DMA((7 * n,)), pltpu.SemaphoreType.DMA((7 * n,)), pltpu.SemaphoreType.DMA((n,))]


def _scatter(partials, wholes, name):
    ns = len(partials) + len(wholes)

    def body(*refs):
        starts, waits = _scatter_copies(refs[:ns], refs[ns:2 * ns], *refs[2 * ns:], whole=len(wholes))
        for call in starts + waits:
            call()

    return pl.pallas_call(
        body, in_specs=[_ANY] * ns, out_specs=[_ANY] * ns,
        out_shape=[_scattered_shape(p) for p in partials] + [_sds((8,) + s.shape, s.dtype) for s in wholes],
        scratch_shapes=_scatter_scratch(ns), name=name,
    )(*partials, *wholes)


def _swap_copies(srcs, dsts, send_sems, recv_sems, local_sems):
    x, y, c = _here()
    sibling = (x, y, 1 - c)
    starts, waits = [], []
    for i, (src, dst) in enumerate(zip(srcs, dsts)):
        own = pltpu.make_async_copy(src, dst.at[c], local_sems.at[i])
        push = _remote(src, dst.at[c], send_sems, recv_sems, i, sibling)
        landed = dst.at[1 - c]
        starts += [own.start, push.start]
        waits += [_remote(landed, landed, send_sems, recv_sems, i, sibling).wait_recv, push.wait_send, own.wait]
    return starts, waits


def _swap_scratch(n):
    return [pltpu.SemaphoreType.DMA((n,)), pltpu.SemaphoreType.DMA((n,)), pltpu.SemaphoreType.DMA((n,))]


def _exchange_halves(halves):
    ns = len(halves)

    def body(*refs):
        starts, waits = _swap_copies(refs[:ns], refs[ns:2 * ns], *refs[2 * ns:])
        for call in starts + waits:
            call()

    return pl.pallas_call(
        body, in_specs=[_ANY] * ns, out_specs=[_ANY] * ns, out_shape=[_sds((2,) + h.shape, h.dtype) for h in halves],
        scratch_shapes=_swap_scratch(ns), name="exchange_halves",
    )(*halves)


def _row_tile(rows, row_bytes, budget):
    tr = rows
    while tr * row_bytes > budget and tr % 16 == 0:
        tr //= 2
    return tr


def _sum_slots(parts, name):
    _, rows, cols = parts.shape
    tr = _row_tile(rows, 8 * cols * 4, 2 * 1024 * 1024)

    def body(p_ref, o_ref):
        acc = p_ref[0].astype(F32)
        for d in range(1, 8):
            acc = acc + p_ref[d].astype(F32)
        o_ref[...] = acc

    return pl.pallas_call(
        body, grid=(rows // tr,), in_specs=[pl.BlockSpec((8, tr, cols), lambda i: (0, i, 0))],
        out_specs=pl.BlockSpec((tr, cols), lambda i: (i, 0)), out_shape=_sds((rows, cols), F32),
        compiler_params=_params(("parallel",)), name=name,
    )(parts)


def _adamw(w, g, m, v, name):
    rows, cols = w.shape
    tr = _row_tile(rows, 7 * cols * 4, 4 * 1024 * 1024)

    def body(w_ref, g_ref, m_ref, v_ref, d_ref, mo_ref, vo_ref):
        g = g_ref[...]
        m = ADAM_B1 * m_ref[...] + (1.0 - ADAM_B1) * g
        v = ADAM_B2 * v_ref[...] + (1.0 - ADAM_B2) * jnp.square(g)
        m_hat = m / (1.0 - ADAM_B1 ** ADAM_STEP)
        v_hat = v / (1.0 - ADAM_B2 ** ADAM_STEP)
        d_ref[...] = -ADAM_LR * (m_hat / (jnp.sqrt(v_hat) + ADAM_EPS) + ADAM_WD * w_ref[...])
        mo_ref[...] = m
        vo_ref[...] = v

    spec = pl.BlockSpec((tr, cols), lambda i: (i, 0))
    return pl.pallas_call(
        body, grid=(rows // tr,), in_specs=[spec] * 4, out_specs=[spec] * 3, out_shape=[_sds((rows, cols), F32)] * 3,
        compiler_params=_params(("parallel",)), name=name,
    )(w, g, m, v)


def kernel(x, positions, attn_norm_w, w_in, q_lat_norm_w, w_uq, kv_lat_norm_w, w_ukv, q_norm_w, k_norm_w, mla_out_norm_w, conv_w, a_log, dt_bias, gdn_norm_w, w_out, mlp_norm_w, w_up, w_down, loss_target, m_attn_norm_w, m_w_in, m_q_lat_norm_w, m_w_uq, m_kv_lat_norm_w, m_w_ukv, m_q_norm_w, m_k_norm_w, m_mla_out_norm_w, m_conv_w, m_a_log, m_dt_bias, m_gdn_norm_w, m_w_out, m_mlp_norm_w, m_w_up, m_w_down, v_attn_norm_w, v_w_in, v_q_lat_norm_w, v_w_uq, v_kv_lat_norm_w, v_w_ukv, v_q_norm_w, v_k_norm_w, v_mla_out_norm_w, v_conv_w, v_a_log, v_dt_bias, v_gdn_norm_w, v_w_out, v_mlp_norm_w, v_w_up, v_w_down):
    w = dict(zip(WEIGHTS, (attn_norm_w, w_in, q_lat_norm_w, w_uq, kv_lat_norm_w, w_ukv, q_norm_w, k_norm_w, mla_out_norm_w, conv_w,
                           a_log, dt_bias, gdn_norm_w, w_out, mlp_norm_w, w_up, w_down)))
    m = dict(zip(WEIGHTS, (m_attn_norm_w, m_w_in, m_q_lat_norm_w, m_w_uq, m_kv_lat_norm_w, m_w_ukv, m_q_norm_w, m_k_norm_w,
                           m_mla_out_norm_w, m_conv_w, m_a_log, m_dt_bias, m_gdn_norm_w, m_w_out, m_mlp_norm_w, m_w_up, m_w_down)))
    v = dict(zip(WEIGHTS, (v_attn_norm_w, v_w_in, v_q_lat_norm_w, v_w_uq, v_kv_lat_norm_w, v_w_ukv, v_q_norm_w, v_k_norm_w,
                           v_mla_out_norm_w, v_conv_w, v_a_log, v_dt_bias, v_gdn_norm_w, v_w_out, v_mlp_norm_w, v_w_up, v_w_down)))
    B, S, D = x.shape
    T = B * S
    x2, pos, target = x.reshape(T, D), positions.reshape(T, 1), loss_target.reshape(T, D)
    seq = lambda a: a.reshape(B, S, a.shape[-1])
    tok = lambda a: a.reshape(T, a.shape[-1])
    local = {n: w[n][0] for n in SHARDED}

    g_in, g_uq, g_ukv, g_conv = _all_gather([local["w_in"].astype(BF16), local["w_uq"].astype(BF16), local["w_ukv"].astype(BF16),
                                             local["conv_w"]], "gather_first_weights")
    w_in_p = _widen_w_in(_from_column_shards(g_in))
    w_mla = _stack_mla(_from_column_shards(g_uq), _from_column_shards(g_ukv))
    conv_full = _from_column_shards(g_conv)
    ln_w = jnp.concatenate([q_lat_norm_w, kv_lat_norm_w], axis=0)
    qk_nw = _qk_norm_rows(q_norm_w, k_norm_w)
    rope_rows = _rope_rows()
    scal = _rows8([jnp.pad(a_log, ((0, 0), (0, 128 - HEADS))), jnp.pad(dt_bias, ((0, 0), (0, 128 - HEADS)))])
    mix_nw = _rows8([mla_out_norm_w[0], gdn_norm_w])

    xn, lat, gqkv, gz, gab = _in_proj_fwd(x2, attn_norm_w, w_in_p)
    q, k, v_att = _mla_pre_fwd(lat, pos, ln_w, w_mla, qk_nw, rope_rows)
    ao, lse, g_down = _attn_fwd(seq(q), seq(k), seq(v_att), [local["w_down"].astype(BF16)])
    gq, gk, gv = _gdn_pre_fwd(seq(gqkv), conv_full)
    go, states, g_out, w_up_b = _gdn_chunk_fwd(gq, gk, gv, seq(gab), scal, [local["w_out"].astype(BF16), local["w_up"].astype(BF16)])
    w_out_b = g_out.reshape(-1, D)
    w_down_b = g_down.reshape(-1, D)
    mix, h2 = _mix_fwd(tok(ao), tok(go), gz, mix_nw, w_out_b, x2)
    hn, dy, sq = _mlp_fwd(h2, mlp_norm_w, w_up_b, w_down_b, target)
    loss = lax.psum(jnp.sum(sq[:, 0, 0]) * (0.5 / D), ("x", "y", "c"))

    dh, act, d_up, d_mlp_nw = _mlp_bwd(h2, mlp_norm_w, hn, w_up_b, w_down_b, dy)
    p_down = _wgrad(act, dy, "wgrad_down").reshape(4, -1, D)
    p_up = _wgrad(hn, d_up, "wgrad_up", column_shards=4)
    d_ao, d_go, d_gz, d_mix_nw = _mix_bwd(tok(ao), tok(go), gz, mix_nw, w_out_b, dh)
    p_out = _wgrad(mix, dh, "wgrad_out").reshape(4, -1, D)
    d_gq, d_gk, d_gv, d_gab, d_scal, s_up, s_down, s_out = _gdn_chunk_bwd(gq, gk, gv, seq(gab), scal, states, seq(d_go),
                                                                          [p_up, p_down, p_out])
    dxq, dxk, dxv, dcq, dck, dcv = _gdn_pre_bwd(seq(gqkv), conv_full, d_gq, d_gk, d_gv)
    early = ("w_up", "w_down", "w_out")
    early_halves = [_sum_slots(s, "sum_" + n) for n, s in zip(early, (s_up, s_down, s_out))]
    dq, dk, dv, *early_grads = _attn_bwd(seq(q), seq(k), seq(v_att), ao, lse, seq(d_ao), early_halves)
    d_lat, d_ln, d_w_mla, d_qk_nw = _mla_pre_bwd(lat, pos, ln_w, w_mla, qk_nw, rope_rows, tok(dq), tok(dk), tok(dv))
    grad_x2, d_proj, d_attn_nw = _in_proj_bwd([d_lat, tok(dxq), tok(dxk), tok(dxv), d_gz, tok(d_gab)], w_in_p, x2, attn_norm_w, dh)
    p_in = _column_shards(_narrow_w_in(_wgrad(xn, d_proj, "wgrad_in")))
    p_uq, p_ukv = (_column_shards(a).astype(BF16) for a in _unstack_mla(d_w_mla))
    small_partial = {
        "attn_norm_w": d_attn_nw, "q_lat_norm_w": d_ln[0:1], "kv_lat_norm_w": d_ln[1:2],
        "q_norm_w": jnp.concatenate([d_qk_nw[0:1], _rope_unpad(d_qk_nw[1:2])], axis=-1),
        "k_norm_w": jnp.concatenate([d_qk_nw[2:3], _rope_unpad(d_qk_nw[3:4])], axis=-1),
        "mla_out_norm_w": d_mix_nw[None, :HEADS], "a_log": d_scal[0:1, :HEADS], "dt_bias": d_scal[1:2, :HEADS],
        "gdn_norm_w": d_mix_nw[HEADS:HEADS + 1], "mlp_norm_w": d_mlp_nw,
    }
    conv_partial = jnp.concatenate([dcq, dck, dcv], axis=-1)
    s_in, s_uq, s_ukv, s_small = _scatter([p_in, p_uq, p_ukv], [_pack_small([small_partial[n] for n in SMALL] + [conv_partial])],
                                          "scatter_last_partials")

    late = ("w_in", "w_uq", "w_ukv")
    late_grads = _exchange_halves([_sum_slots(s, "sum_" + n) for n, s in zip(late, (s_in, s_uq, s_ukv))])
    names = early + late
    grad = {n: g.reshape(local[n].shape) for n, g in zip(names, list(early_grads) + list(late_grads))}
    small_shapes = [w[n].shape for n in SMALL]
    *g_small, g_conv_all = _unpack_small(_sum_slots(s_small, "sum_small"), small_shapes + [conv_partial.shape])
    grad.update(zip(SMALL, g_small))
    conv_cols = local["conv_w"].shape[1]
    grad["conv_w"] = lax.dynamic_slice_in_dim(g_conv_all, (2 * lax.axis_index("x") + lax.axis_index("y")) * conv_cols, conv_cols, axis=1)

    delta, new_m, new_v = {}, {}, {}
    for n in names:
        delta[n], new_m[n], new_v[n] = _adamw(local[n], grad[n], m[n][0], v[n][0], "adamw_" + n)
    packed_names = SMALL + ("conv_w",)
    packed_shapes = small_shapes + [local["conv_w"].shape]
    take = lambda d: _pack_small([d[n][0] if n == "conv_w" and d[n].ndim == 3 else d[n] for n in packed_names])
    outs = _adamw(take(w), take(grad), take(m), take(v), "adamw_small")
    for d, packed in zip((delta, new_m, new_v), outs):
        d.update(zip(packed_names, _unpack_small(packed, packed_shapes)))

    def in_order(d):
        return [d[n].reshape(w[n].shape) for n in WEIGHTS]

    return (loss, grad_x2.reshape(B, S, D), *in_order(grad), *in_order(delta), *in_order(new_m), *in_order(new_v))
```

```python
import functools
import math

import jax
import jax.numpy as jnp
from jax import lax
from jax.experimental import pallas as pl
from jax.experimental.pallas import tpu as pltpu

F32 = jnp.float32
BF16 = jnp.bfloat16
MESH = pl.DeviceIdType.MESH

EPS = 1e-6
HEADS = 4
HEAD_DIM = 128
ROPE_DIM = 64
ROPE_HALF = 32
QK_DIM = 192
QK_PAD = 256
LORA = 256
CHUNK = 64
CONV_TAPS = 4
ROPE_THETA = 10000.0
ATTN_SCALE = QK_DIM ** -0.5

LAT_W = 640
GQKV_W = 3 * HEADS * HEAD_DIM
GZ_W = HEADS * HEAD_DIM
GAB_W = 128
PROJ_SPLITS = ((0, LAT_W), (LAT_W, LAT_W + GQKV_W), (LAT_W + GQKV_W, LAT_W + GQKV_W + GZ_W),
               (LAT_W + GQKV_W + GZ_W, LAT_W + GQKV_W + GZ_W + GAB_W))
PROJ_W = PROJ_SPLITS[-1][1]

ADAM_LR = 0.001
ADAM_B1 = 0.9
ADAM_B2 = 0.999
ADAM_EPS = 1e-08
ADAM_WD = 0.01
ADAM_STEP = 10

TOKEN_TILE = 512
MLP_TOKEN_TILE = 1024
FF_TILE = 512
ATTN_TILE = 512
ATTN_HEADS_PER_STEP = 2
WGRAD_OUT_BYTES = 8 * 1024 * 1024
VMEM_LIMIT = 48 * 1024 * 1024

SHARDED = ("w_in", "w_uq", "w_ukv", "conv_w", "w_out", "w_up", "w_down")
SMALL = ("attn_norm_w", "q_lat_norm_w", "kv_lat_norm_w", "q_norm_w", "k_norm_w", "mla_out_norm_w", "a_log", "dt_bias",
         "gdn_norm_w", "mlp_norm_w")
WEIGHTS = ("attn_norm_w", "w_in", "q_lat_norm_w", "w_uq", "kv_lat_norm_w", "w_ukv", "q_norm_w", "k_norm_w", "mla_out_norm_w",
           "conv_w", "a_log", "dt_bias", "gdn_norm_w", "w_out", "mlp_norm_w", "w_up", "w_down")


def _sds(shape, dtype):
    return jax.ShapeDtypeStruct(shape, dtype)


def _params(semantics):
    return pltpu.CompilerParams(dimension_semantics=semantics, vmem_limit_bytes=VMEM_LIMIT)


def _block(n):
    for b in (512, 256, 128):
        if n % b == 0:
            return b
    return n


def _dg(a, b, ca, cb, prec):
    lead = a.ndim - 2
    batch = (tuple(range(lead)),) * 2
    return lax.dot_general(a, b, (((ca + lead,), (cb + lead,)), batch), precision=prec, preferred_element_type=F32)


def _split_bf16(a):
    hi = a.astype(BF16)
    return hi, (a - hi.astype(F32)).astype(BF16)


def _dot_bf16(a, b, ca, cb):
    return _dg(a.astype(BF16), b.astype(BF16), ca, cb, None)


def _dot_bf16x3(a, b, ca, cb):
    a_hi, a_lo = _split_bf16(a)
    b_hi, b_lo = _split_bf16(b)
    lead = a.ndim - 2
    return _dg(jnp.concatenate([a_hi, a_hi, a_lo], axis=ca + lead), jnp.concatenate([b_hi, b_lo, b_hi], axis=cb + lead), ca, cb, None)


def _matmul_family(dot):
    def nn_raw(a, b):
        return dot(a, b, 1, 0)

    def nt_raw(a, b):
        return dot(a, b, 1, 1)

    def tn_raw(a, b):
        return dot(a, b, 0, 0)

    @jax.custom_vjp
    def nn(a, b):
        return nn_raw(a, b)

    nn.defvjp(lambda a, b: (nn_raw(a, b), (a, b)), lambda r, g: (nt_raw(g, r[1]), tn_raw(r[0], g)))

    @jax.custom_vjp
    def nt(a, b):
        return nt_raw(a, b)

    nt.defvjp(lambda a, b: (nt_raw(a, b), (a, b)), lambda r, g: (nn_raw(g, r[1]), tn_raw(g, r[0])))

    @jax.custom_vjp
    def tn(a, b):
        return tn_raw(a, b)

    tn.defvjp(lambda a, b: (tn_raw(a, b), (a, b)), lambda r, g: (nt_raw(r[1], g), nn_raw(r[0], g)))
    return nn, nt, tn


_bf_nn, _bf_nt, _bf_tn = _matmul_family(_dot_bf16)
_hi_nn, _hi_nt, _hi_tn = _matmul_family(_dot_bf16x3)


@jax.custom_vjp
def _lane_halves(x):
    n = x.shape[-1] // 2
    return x[..., :n], x[..., n:]


_lane_halves.defvjp(lambda x: (_lane_halves(x), None), lambda _, g: (jnp.concatenate(g, axis=-1),))


@jax.custom_vjp
def _row_halves(x):
    n = x.shape[-2] // 2
    return x[..., :n, :], x[..., n:, :]


_row_halves.defvjp(lambda x: (_row_halves(x), None), lambda _, g: (jnp.concatenate(g, axis=-2),))


@jax.custom_vjp
def _swap_halves(t):
    return pltpu.roll(t, 64, 1)


_swap_halves.defvjp(lambda t: (pltpu.roll(t, 64, 1), None), lambda _, g: (pltpu.roll(g, 64, 1),))


@functools.partial(jax.custom_vjp, nondiff_argnums=(2,))
def _shift_rows(x, keep, s):
    return pltpu.roll(x, s, 0) * keep


def _shift_rows_fwd(x, keep, s):
    return pltpu.roll(x, s, 0) * keep, keep


def _shift_rows_bwd(s, keep, g):
    return pltpu.roll(g * keep, keep.shape[0] - s, 0), jnp.zeros_like(keep)


_shift_rows.defvjp(_shift_rows_fwd, _shift_rows_bwd)


def _sigmoid(x):
    return 0.5 * jnp.tanh(0.5 * x) + 0.5


def _softplus(x):
    return jnp.maximum(x, 0.0) + jnp.log(1.0 + jnp.exp(jnp.minimum(x, -x)))


def _silu(x):
    return x * _sigmoid(x)


def _rms(x, w, n=None):
    n = x.shape[-1] if n is None else n
    r = lax.rsqrt(jnp.sum(x * x, axis=-1, keepdims=True) * (1.0 / n) + EPS)
    return x * r * w


def _rope(t, cos_f, sin_f):
    return t * cos_f + _swap_halves(t) * sin_f


def _rope_tables(pos_col, freq_row, sign_row):
    ang = pos_col.astype(F32) * freq_row
    return jnp.cos(ang), jnp.sin(ang) * sign_row


def _onehot_row(lane):
    return (lax.broadcasted_iota(jnp.int32, (1, 128), 1) == lane).astype(F32)


def _row_spec(tm, w):
    return pl.BlockSpec((tm, w), lambda i: (i, 0))


def _const_spec(shape):
    return pl.BlockSpec(shape, lambda *_: (0,) * len(shape))


def _in_proj_fwd(x2, w_an, w_in_p):
    T, D = x2.shape
    tm = min(TOKEN_TILE, T)

    def body(x_ref, wn_ref, w_ref, xn_ref, lat_ref, gqkv_ref, gz_ref, gab_ref):
        x = x_ref[...]
        r = lax.rsqrt(jnp.mean(x * x, axis=-1, keepdims=True) + EPS)
        xn = (x * r * wn_ref[...]).astype(BF16)
        xn_ref[...] = xn
        for ref, (a, b) in zip((lat_ref, gqkv_ref, gz_ref, gab_ref), PROJ_SPLITS):
            ref[...] = _dg(xn, w_ref[a:b, :], 1, 1, None)

    widths = [b - a for a, b in PROJ_SPLITS]
    return pl.pallas_call(
        body, grid=(T // tm,),
        in_specs=[_row_spec(tm, D), _const_spec((1, D)), _const_spec((PROJ_W, D))],
        out_specs=[_row_spec(tm, D)] + [_row_spec(tm, w) for w in widths],
        out_shape=[_sds((T, D), BF16)] + [_sds((T, w), F32) for w in widths],
        compiler_params=_params(("parallel",)), name="in_proj_fwd",
    )(x2, w_an, w_in_p)


def _in_proj_bwd(pieces, w_in_p, x2, w_an, dh):
    T, D = x2.shape
    tm = min(TOKEN_TILE, T)
    widths = [p.shape[1] for p in pieces]
    starts = [sum(widths[:i]) for i in range(len(widths))]
    assert sum(widths) == PROJ_W

    def body(*refs):
        piece_refs = refs[:len(pieces)]
        w_ref, x_ref, wn_ref, dh_ref, dx_ref, dp_ref, dwn_ref = refs[len(pieces):]

        @pl.when(pl.program_id(0) == 0)
        def _():
            dwn_ref[...] = jnp.zeros_like(dwn_ref)

        dxn = jnp.zeros((tm, D), F32)
        for ref, a, width in zip(piece_refs, starts, widths):
            piece = ref[...].astype(BF16)
            dp_ref[:, a:a + width] = piece
            dxn += _dg(piece, w_ref[a:a + width, :], 1, 0, None)
        _, pull = jax.vjp(_rms, x_ref[...], wn_ref[...])
        dx, dwn = pull(dxn)
        dx_ref[...] = dx + dh_ref[...]
        dwn_ref[...] += dwn

    return pl.pallas_call(
        body, grid=(T // tm,),
        in_specs=[_row_spec(tm, w) for w in widths] + [_const_spec((PROJ_W, D)), _row_spec(tm, D), _const_spec((1, D)),
                                                       _row_spec(tm, D)],
        out_specs=[_row_spec(tm, D), _row_spec(tm, PROJ_W), _const_spec((1, D))],
        out_shape=[_sds((T, D), F32), _sds((T, PROJ_W), BF16), _sds((1, D), F32)],
        compiler_params=_params(("arbitrary",)), name="in_proj_bwd",
    )(*pieces, w_in_p, x2, w_an, dh)


def _wgrad(a, b, name, column_shards=1, out_dtype=BF16):
    T, k1 = a.shape
    k2 = b.shape[1]
    per_shard = k2 // column_shards
    tt = min(TOKEN_TILE, T)
    b1 = k1
    while b1 * k2 * 4 > WGRAD_OUT_BYTES and b1 % 256 == 0:
        b1 //= 2
    step = _block(per_shard)

    def body(a_ref, b_ref, o_ref, acc_ref):
        t = pl.program_id(1)

        @pl.when(t == 0)
        def _():
            acc_ref[...] = jnp.zeros_like(acc_ref)

        a_t = a_ref[...].astype(BF16).T
        for c0 in range(0, k2, step):
            part = jnp.dot(a_t, b_ref[:, c0:c0 + step].astype(BF16), preferred_element_type=F32)
            if column_shards == 1:
                acc_ref[:, c0:c0 + step] += part
            else:
                acc_ref[c0 // per_shard, :, c0 % per_shard:c0 % per_shard + step] += part

        @pl.when(t == T // tt - 1)
        def _():
            o_ref[...] = acc_ref[...].astype(o_ref.dtype)

    if column_shards == 1:
        block, out_spec, out_shape = (b1, k2), pl.BlockSpec((b1, k2), lambda i, t: (i, 0)), _sds((k1, k2), out_dtype)
    else:
        block = (column_shards, b1, per_shard)
        out_spec, out_shape = pl.BlockSpec(block, lambda i, t: (0, i, 0)), _sds((column_shards, k1, per_shard), out_dtype)
    return pl.pallas_call(
        body, grid=(k1 // b1, T // tt),
        in_specs=[pl.BlockSpec((tt, b1), lambda i, t: (t, i)), pl.BlockSpec((tt, k2), lambda i, t: (t, 0))],
        out_specs=out_spec, out_shape=out_shape, scratch_shapes=[pltpu.VMEM(block, F32)],
        compiler_params=_params(("parallel", "arbitrary")), name=name,
    )(a, b)


def _mla_pre_fn(q_lat, kv_lat, kpe, ln_q, ln_kv, w_list, qn_n, qn_p, kn_n, kn_p, cos_f, sin_f):
    qn = _rms(q_lat, ln_q)
    kvn = _rms(kv_lat, ln_kv)
    kp = _rope(_rms(kpe, kn_p, ROPE_DIM), cos_f, sin_f)
    outs = []
    for h in range(HEADS):
        outs.append(_rms(_bf_nn(qn, w_list[h]), qn_n))
        outs.append(_rope(_rms(_bf_nn(qn, w_list[HEADS + h]), qn_p, ROPE_DIM), cos_f, sin_f))
        outs.append(_rms(_bf_nn(kvn, w_list[2 * HEADS + h]), kn_n))
        outs.append(_bf_nn(kvn, w_list[3 * HEADS + h]))
    return tuple(outs) + (kp,)


def _mla_pre_operands(lat_ref, pos_ref, ln_ref, w_ref, nw_ref, rope_ref):
    cos_f, sin_f = _rope_tables(pos_ref[...], rope_ref[0:1, :], rope_ref[1:2, :])
    diff = (lat_ref[:, 0:LORA], lat_ref[:, LORA:2 * LORA], lat_ref[:, 2 * LORA:LAT_W], ln_ref[0:1, :], ln_ref[1:2, :],
            [w_ref[i].astype(F32) for i in range(4 * HEADS)], nw_ref[0:1, :], nw_ref[1:2, :], nw_ref[2:3, :], nw_ref[3:4, :])
    return diff, cos_f, sin_f


def _mla_pre_fwd(lat, pos, ln_w, w_mla, nw, rope_rows):
    T = lat.shape[0]
    tm = min(TOKEN_TILE, T)

    def body(lat_ref, pos_ref, ln_ref, w_ref, nw_ref, rope_ref, q_ref, k_ref, v_ref):
        diff, cos_f, sin_f = _mla_pre_operands(lat_ref, pos_ref, ln_ref, w_ref, nw_ref, rope_ref)
        outs = _mla_pre_fn(*diff, cos_f, sin_f)
        kp = outs[-1].astype(BF16)
        for h in range(HEADS):
            q_n, q_p, k_n, v = outs[4 * h:4 * h + 4]
            q_ref[:, h * QK_PAD:h * QK_PAD + HEAD_DIM] = q_n.astype(BF16)
            q_ref[:, h * QK_PAD + HEAD_DIM:(h + 1) * QK_PAD] = q_p.astype(BF16)
            k_ref[:, h * QK_PAD:h * QK_PAD + HEAD_DIM] = k_n.astype(BF16)
            k_ref[:, h * QK_PAD + HEAD_DIM:(h + 1) * QK_PAD] = kp
            v_ref[:, h * HEAD_DIM:(h + 1) * HEAD_DIM] = v.astype(BF16)

    return pl.pallas_call(
        body, grid=(T // tm,),
        in_specs=[_row_spec(tm, LAT_W), _row_spec(tm, 1), _const_spec((2, LORA)), _const_spec((4 * HEADS, LORA, 128)),
                  _const_spec((8, 128)), _const_spec((8, 128))],
        out_specs=[_row_spec(tm, HEADS * QK_PAD), _row_spec(tm, HEADS * QK_PAD), _row_spec(tm, HEADS * HEAD_DIM)],
        out_shape=[_sds((T, HEADS * QK_PAD), BF16), _sds((T, HEADS * QK_PAD), BF16), _sds((T, HEADS * HEAD_DIM), BF16)],
        compiler_params=_params(("parallel",)), name="mla_pre_fwd",
    )(lat, pos, ln_w, w_mla, nw, rope_rows)


def _mla_pre_bwd(lat, pos, ln_w, w_mla, nw, rope_rows, dq, dk, dv):
    T = lat.shape[0]
    tm = min(TOKEN_TILE, T)

    def body(lat_ref, pos_ref, ln_ref, w_ref, nw_ref, rope_ref, dq_ref, dk_ref, dv_ref, dlat_ref, dln_ref, dw_ref, dnw_ref):
        @pl.when(pl.program_id(0) == 0)
        def _():
            dln_ref[...] = jnp.zeros_like(dln_ref)
            dw_ref[...] = jnp.zeros_like(dw_ref)
            dnw_ref[...] = jnp.zeros_like(dnw_ref)

        diff, cos_f, sin_f = _mla_pre_operands(lat_ref, pos_ref, ln_ref, w_ref, nw_ref, rope_ref)
        _, pull = jax.vjp(lambda *a: _mla_pre_fn(*a, cos_f, sin_f), *diff)
        cts = []
        d_kp = jnp.zeros((tm, 128), F32)
        for h in range(HEADS):
            cts.append(dq_ref[:, h * QK_PAD:h * QK_PAD + HEAD_DIM])
            cts.append(dq_ref[:, h * QK_PAD + HEAD_DIM:(h + 1) * QK_PAD])
            cts.append(dk_ref[:, h * QK_PAD:h * QK_PAD + HEAD_DIM])
            cts.append(dv_ref[:, h * HEAD_DIM:(h + 1) * HEAD_DIM])
            d_kp += dk_ref[:, h * QK_PAD + HEAD_DIM:(h + 1) * QK_PAD]
        d_ql, d_kvl, d_kpe, d_lnq, d_lnkv, d_w, d_qn_n, d_qn_p, d_kn_n, d_kn_p = pull(tuple(cts) + (d_kp,))
        dlat_ref[:, 0:LORA] = d_ql
        dlat_ref[:, LORA:2 * LORA] = d_kvl
        dlat_ref[:, 2 * LORA:LAT_W] = d_kpe
        dln_ref[0:1, :] += d_lnq
        dln_ref[1:2, :] += d_lnkv
        for i in range(4 * HEADS):
            dw_ref[i] += d_w[i]
        for i, d in enumerate((d_qn_n, d_qn_p, d_kn_n, d_kn_p)):
            dnw_ref[i:i + 1, :] += d

    return pl.pallas_call(
        body, grid=(T // tm,),
        in_specs=[_row_spec(tm, LAT_W), _row_spec(tm, 1), _const_spec((2, LORA)), _const_spec((4 * HEADS, LORA, 128)),
                  _const_spec((8, 128)), _const_spec((8, 128)),
                  _row_spec(tm, HEADS * QK_PAD), _row_spec(tm, HEADS * QK_PAD), _row_spec(tm, HEADS * HEAD_DIM)],
        out_specs=[_row_spec(tm, LAT_W), _const_spec((2, LORA)), _const_spec((4 * HEADS, LORA, 128)), _const_spec((8, 128))],
        out_shape=[_sds((T, LAT_W), F32), _sds((2, LORA), F32), _sds((4 * HEADS, LORA, 128), F32), _sds((8, 128), F32)],
        compiler_params=_params(("arbitrary",)), name="mla_pre_bwd",
    )(lat, pos, ln_w, w_mla, nw, rope_rows, dq, dk, dv)


def _causal_mask(i, j, tq, tk):
    row = i * tq + lax.broadcasted_iota(jnp.int32, (tq, tk), 0)
    col = j * tk + lax.broadcasted_iota(jnp.int32, (tq, tk), 1)
    return col <= row


def _attn_fwd(q, k, v, shards):
    B, S, _ = q.shape
    t = min(ATTN_TILE, S)
    nq = S // t
    ns = len(shards)

    hp = ATTN_HEADS_PER_STEP
    qk = lambda h: slice(h * QK_PAD, (h + 1) * QK_PAD)
    vd = lambda h: slice(h * HEAD_DIM, (h + 1) * HEAD_DIM)

    def body(*refs):
        q_ref, k_ref, v_ref = refs[:3]
        src_refs = refs[3:3 + ns]
        o_ref, lse_ref = refs[3 + ns:5 + ns]
        dst_refs = refs[5 + ns:5 + 2 * ns]
        sems = refs[5 + 2 * ns:]
        b, g, i = pl.program_id(0), pl.program_id(1), pl.program_id(2)
        qb = [q_ref[0, :, qk(h)] for h in range(hp)]

        @pl.when((b == 0) & (g == 0) & (i == 0))
        def _():
            for start in _gather_copies(src_refs, dst_refs, *sems)[0]:
                start()

        def step(j, carry, diagonal):
            rows = pl.ds(pl.multiple_of(j * t, t), t)
            s = [_dg(qb[h], k_ref[0, rows, qk(h)], 1, 1, None) * ATTN_SCALE for h in range(hp)]
            if diagonal:
                keep = _causal_mask(0, 0, t, t)
                s = [jnp.where(keep, x, -1e30) for x in s]
            m_new = [jnp.maximum(carry[h][0], jnp.max(s[h], axis=-1, keepdims=True)) for h in range(hp)]
            p = [jnp.exp(s[h] - m_new[h]) for h in range(hp)]
            alpha = [jnp.exp(carry[h][0] - m_new[h]) for h in range(hp)]
            l = [alpha[h] * carry[h][1] + jnp.sum(p[h], axis=-1, keepdims=True) for h in range(hp)]
            pv = [jnp.dot(p[h].astype(BF16), v_ref[0, rows, vd(h)], preferred_element_type=F32) for h in range(hp)]
            return tuple((m_new[h], l[h], alpha[h] * carry[h][2] + pv[h]) for h in range(hp))

        init = tuple((jnp.full((t, 1), -1e30, F32), jnp.zeros((t, 1), F32), jnp.zeros((t, HEAD_DIM), F32)) for _ in range(hp))
        below = lax.fori_loop(0, i, lambda j, carry: step(j, carry, False), init)
        for h, (m, l, acc) in enumerate(step(i, below, True)):
            o_ref[0, :, vd(h)] = acc / l
            lse_ref[0, h] = m + jnp.log(l)

        @pl.when((b == B - 1) & (g == HEADS // hp - 1) & (i == nq - 1))
        def _():
            for wait in _gather_copies(src_refs, dst_refs, *sems)[1]:
                wait()

    return pl.pallas_call(
        body, grid=(B, HEADS // hp, nq),
        in_specs=[pl.BlockSpec((1, t, hp * QK_PAD), lambda b, g, i: (b, i, g)),
                  pl.BlockSpec((1, S, hp * QK_PAD), lambda b, g, i: (b, 0, g)),
                  pl.BlockSpec((1, S, hp * HEAD_DIM), lambda b, g, i: (b, 0, g))] + [_ANY] * ns,
        out_specs=[pl.BlockSpec((1, t, hp * HEAD_DIM), lambda b, g, i: (b, i, g)),
                   pl.BlockSpec((1, hp, t, 1), lambda b, g, i: (b, g, i, 0))] + [_ANY] * ns,
        out_shape=[_sds((B, S, HEADS * HEAD_DIM), F32), _sds((B, HEADS, S, 1), F32)] + [_sds((4,) + s.shape, s.dtype) for s in shards],
        scratch_shapes=_gather_scratch(ns),
        compiler_params=_params(("arbitrary", "arbitrary", "arbitrary")), name="attn_fwd",
    )(q, k, v, *shards)


def _attn_bwd(q, k, v, o, lse, do, halves):
    B, S, _ = q.shape
    t = min(ATTN_TILE, S)
    nq = S // t
    ns = len(halves)

    hp = ATTN_HEADS_PER_STEP
    qk = lambda h: slice(h * QK_PAD, (h + 1) * QK_PAD)
    vd = lambda h: slice(h * HEAD_DIM, (h + 1) * HEAD_DIM)
    heads = range(hp)

    def body(*refs):
        q_ref, k_ref, v_ref, o_ref, lse_ref, do_ref = refs[:6]
        src_refs = refs[6:6 + ns]
        dq_ref, dk_ref, dv_ref = refs[6 + ns:9 + ns]
        dst_refs = refs[9 + ns:9 + 2 * ns]
        dsum_ref, send_sems, recv_sems, local_sems = refs[9 + 2 * ns:]
        b, g, j = pl.program_id(0), pl.program_id(1), pl.program_id(2)

        @pl.when((b == 0) & (g == 0) & (j == 0))
        def _():
            for start in _swap_copies(src_refs, dst_refs, send_sems, recv_sems, local_sems)[0]:
                start()

        @pl.when(j == 0)
        def _():
            dq_ref[...] = jnp.zeros_like(dq_ref)
            for h in heads:
                dsum_ref[h] = jnp.sum(do_ref[0, :, vd(h)] * o_ref[0, :, vd(h)], axis=-1, keepdims=True)

        kb = [k_ref[0, :, qk(h)] for h in heads]
        vb = [v_ref[0, :, vd(h)] for h in heads]

        def step(i, carry, diagonal):
            rows = pl.ds(pl.multiple_of(i * t, t), t)
            qb = [q_ref[0, rows, qk(h)] for h in heads]
            dob = [do_ref[0, rows, vd(h)].astype(BF16) for h in heads]
            s = [_dg(qb[h], kb[h], 1, 1, None) * ATTN_SCALE for h in heads]
            p = [jnp.exp(s[h] - lse_ref[0, h, rows, :]) for h in heads]
            if diagonal:
                keep = _causal_mask(0, 0, t, t)
                p = [jnp.where(keep, x, 0.0) for x in p]
            dp = [_dg(dob[h], vb[h], 1, 1, None) for h in heads]
            dv = [carry[h][1] + _dg(p[h].astype(BF16), dob[h], 0, 0, None) for h in heads]
            ds = [(p[h] * (dp[h] - dsum_ref[h, rows, :]) * ATTN_SCALE).astype(BF16) for h in heads]
            for h in heads:
                dq_ref[0, rows, qk(h)] += jnp.dot(ds[h], kb[h], preferred_element_type=F32)
            return tuple((carry[h][0] + _dg(ds[h], qb[h], 0, 0, None), dv[h]) for h in heads)

        zeros = tuple((jnp.zeros((t, QK_PAD), F32), jnp.zeros((t, HEAD_DIM), F32)) for _ in heads)
        on_diagonal = step(j, zeros, True)
        done = lax.fori_loop(j + 1, nq, lambda i, carry: step(i, carry, False), on_diagonal)
        for h, (dk, dv) in enumerate(done):
            dk_ref[0, :, qk(h)] = dk
            dv_ref[0, :, vd(h)] = dv

        @pl.when((b == B - 1) & (g == HEADS // hp - 1) & (j == nq - 1))
        def _():
            for wait in _swap_copies(src_refs, dst_refs, send_sems, recv_sems, local_sems)[1]:
                wait()

    return pl.pallas_call(
        body, grid=(B, HEADS // hp, nq),
        in_specs=[pl.BlockSpec((1, S, hp * QK_PAD), lambda b, g, j: (b, 0, g)),
                  pl.BlockSpec((1, t, hp * QK_PAD), lambda b, g, j: (b, j, g)),
                  pl.BlockSpec((1, t, hp * HEAD_DIM), lambda b, g, j: (b, j, g)),
                  pl.BlockSpec((1, S, hp * HEAD_DIM), lambda b, g, j: (b, 0, g)),
                  pl.BlockSpec((1, hp, S, 1), lambda b, g, j: (b, g, 0, 0)),
                  pl.BlockSpec((1, S, hp * HEAD_DIM), lambda b, g, j: (b, 0, g))] + [_ANY] * ns,
        out_specs=[pl.BlockSpec((1, S, hp * QK_PAD), lambda b, g, j: (b, 0, g)),
                   pl.BlockSpec((1, t, hp * QK_PAD), lambda b, g, j: (b, j, g)),
                   pl.BlockSpec((1, t, hp * HEAD_DIM), lambda b, g, j: (b, j, g))] + [_ANY] * ns,
        out_shape=[_sds((B, S, HEADS * QK_PAD), F32), _sds((B, S, HEADS * QK_PAD), F32), _sds((B, S, HEADS * HEAD_DIM), F32)]
                  + [_swapped_shape(s) for s in halves],
        scratch_shapes=[pltpu.VMEM((hp, S, 1), F32)] + _swap_scratch(ns),
        compiler_params=_params(("arbitrary", "arbitrary", "arbitrary")), name="attn_bwd",
    )(q, k, v, o, lse, do, *halves)


def _gdn_pre_fn(xq, xk, xv, wq, wk, wv, keeps):
    def conv_silu(x, w):
        acc = x * w[3]
        for s in (1, 2, 3):
            acc = acc + _shift_rows(x, keeps[s - 1], s) * w[3 - s]
        return _silu(acc)

    def l2(x):
        return x * lax.rsqrt(jnp.sum(x * x, axis=-1, keepdims=True) + EPS)

    return l2(conv_silu(xq, wq)) * (HEAD_DIM ** -0.5), l2(conv_silu(xk, wk)), conv_silu(xv, wv)


def _gdn_pre_specs(S):
    x_specs = [pl.BlockSpec((1, S, HEAD_DIM), lambda h, b, g=g: (b, 0, g * HEADS + h)) for g in range(3)]
    w_specs = [pl.BlockSpec((CONV_TAPS, HEAD_DIM), lambda h, b, g=g: (0, g * HEADS + h)) for g in range(3)]
    out_spec = pl.BlockSpec((1, S, HEAD_DIM), lambda h, b: (b, 0, h))
    return x_specs, w_specs, out_spec


def _row_keeps(S):
    t = lax.broadcasted_iota(jnp.int32, (S, HEAD_DIM), 0)
    return [(t >= s).astype(F32) for s in (1, 2, 3)]


def _gdn_pre_fwd(gqkv, conv_w):
    B, S, _ = gqkv.shape
    x_specs, w_specs, out_spec = _gdn_pre_specs(S)

    def body(xq_ref, xk_ref, xv_ref, wq_ref, wk_ref, wv_ref, q_ref, k_ref, v_ref):
        taps = [[w[i:i + 1, :] for i in range(CONV_TAPS)] for w in (wq_ref, wk_ref, wv_ref)]
        q, k, v = _gdn_pre_fn(xq_ref[0], xk_ref[0], xv_ref[0], *taps, _row_keeps(S))
        q_ref[0], k_ref[0], v_ref[0] = q, k, v

    return pl.pallas_call(
        body, grid=(HEADS, B), in_specs=x_specs + w_specs, out_specs=[out_spec] * 3,
        out_shape=[_sds((B, S, HEADS * HEAD_DIM), F32)] * 3,
        compiler_params=_params(("parallel", "parallel")), name="gdn_pre_fwd",
    )(gqkv, gqkv, gqkv, conv_w, conv_w, conv_w)


def _gdn_pre_bwd(gqkv, conv_w, dq, dk, dv):
    B, S, _ = gqkv.shape
    x_specs, w_specs, out_spec = _gdn_pre_specs(S)
    dw_spec = pl.BlockSpec((CONV_TAPS, HEAD_DIM), lambda h, b: (0, h))

    def body(xq_ref, xk_ref, xv_ref, wq_ref, wk_ref, wv_ref, dq_ref, dk_ref, dv_ref,
             dxq_ref, dxk_ref, dxv_ref, dwq_ref, dwk_ref, dwv_ref):
        @pl.when(pl.program_id(1) == 0)
        def _():
            for r in (dwq_ref, dwk_ref, dwv_ref):
                r[...] = jnp.zeros_like(r)

        taps = [[w[i:i + 1, :] for i in range(CONV_TAPS)] for w in (wq_ref, wk_ref, wv_ref)]
        keeps = _row_keeps(S)
        _, pull = jax.vjp(lambda *a: _gdn_pre_fn(*a, keeps), xq_ref[0], xk_ref[0], xv_ref[0], *taps)
        dxq, dxk, dxv, dwq, dwk, dwv = pull((dq_ref[0], dk_ref[0], dv_ref[0]))
        dxq_ref[0], dxk_ref[0], dxv_ref[0] = dxq, dxk, dxv
        for ref, dw in ((dwq_ref, dwq), (dwk_ref, dwk), (dwv_ref, dwv)):
            for i in range(CONV_TAPS):
                ref[i:i + 1, :] += dw[i]

    hw = HEADS * HEAD_DIM
    return pl.pallas_call(
        body, grid=(HEADS, B), in_specs=x_specs + w_specs + [out_spec] * 3,
        out_specs=[out_spec] * 3 + [dw_spec] * 3,
        out_shape=[_sds((B, S, hw), F32)] * 3 + [_sds((CONV_TAPS, hw), F32)] * 3,
        compiler_params=_params(("parallel", "arbitrary")), name="gdn_pre_bwd",
    )(gqkv, gqkv, gqkv, conv_w, conv_w, conv_w, dq, dk, dv)


def _chunk_masks():
    i = lax.broadcasted_iota(jnp.int32, (CHUNK, CHUNK), 0)
    j = lax.broadcasted_iota(jnp.int32, (CHUNK, CHUNK), 1)
    lower, after = (j <= i).astype(F32), (j > i).astype(F32)
    return {"le": lower, "le_gt": jnp.concatenate([lower, after], axis=0), "strict": (j < i).astype(F32)}


def _gdn_chunk_fn(groups, masks):
    lane = lax.broadcasted_iota(jnp.int32, (groups, 1, 128), 2)
    head = lax.broadcasted_iota(jnp.int32, (groups, 1, 128), 0) % HEADS
    pick_a, pick_b = (lane == head).astype(F32), (lane == head + HEADS).astype(F32)
    lower, lower_after, strict = (jnp.broadcast_to(masks[n], (groups,) + masks[n].shape) for n in ("le", "le_gt", "strict"))
    ones_row = jnp.ones((1, 1, HEAD_DIM), F32)

    def f(q, k, v, gab, a_row, dt_row, state):
        ga = jnp.sum(gab * pick_a, axis=2, keepdims=True)
        gb = jnp.sum(gab * pick_b, axis=2, keepdims=True)
        a_log = jnp.sum(a_row * pick_a, axis=2, keepdims=True)
        dt_bias = jnp.sum(dt_row * pick_a, axis=2, keepdims=True)
        beta = _sigmoid(gb)
        g = -jnp.exp(a_log) * _softplus(ga + dt_bias)
        g_wide = g * ones_row
        cum, rest = _row_halves(_hi_nn(lower_after, g_wide))
        total = jnp.sum(g_wide, axis=1, keepdims=True)
        diff = _hi_nn(lower, g * strict)
        decay = lower * jnp.exp(diff)
        e_cum = jnp.exp(cum)
        lmat = strict * (beta * _bf_nt(k, k) * decay)
        rhs = jnp.concatenate([v * beta, k * (beta * e_cum)], axis=2)
        rhs = rhs - _hi_nn(lmat, rhs)
        power = lmat
        for _ in range(5):
            power = _hi_nn(power, power)
            rhs = rhs + _hi_nn(power, rhs)
        u, w = _lane_halves(rhs)
        attn = _bf_nt(q, k) * decay
        v_new = u - _bf_nn(w, state)
        o = _bf_nn(q * e_cum, state) + _bf_nn(attn, v_new)
        new_state = state * jnp.exp(total) + _bf_tn(k * jnp.exp(rest), v_new)
        return o, new_state

    return f


def _gdn_chunk_fwd(q, k, v, gab, scal, shards):
    B, S, W = q.shape
    N = S // CHUNK
    ns = len(shards)

    def body(*refs):
        q_ref, k_ref, v_ref, gab_ref, sc_ref = refs[:5]
        src_refs = refs[5:5 + ns]
        o_ref, st_ref = refs[5 + ns:7 + ns]
        dst_refs = refs[7 + ns:7 + 2 * ns]
        state_ref, send_sems, recv_sems, local_sems = refs[7 + 2 * ns:]
        n = pl.program_id(0)

        @pl.when(n == 0)
        def _():
            for start in _gather_copies(src_refs, dst_refs, send_sems, recv_sems, local_sems)[0]:
                start()
            state_ref[...] = jnp.zeros_like(state_ref)

        groups = [(b, h) for b in range(B) for h in range(HEADS)]
        gather = lambda ref: jnp.stack([ref[b, :, h * HEAD_DIM:(h + 1) * HEAD_DIM] for b, h in groups])
        state = state_ref[...]
        for i, (b, h) in enumerate(groups):
            st_ref[b, 0, h] = state[i]
        o, new_state = _gdn_chunk_fn(len(groups), _chunk_masks())(
            gather(q_ref), gather(k_ref), gather(v_ref), jnp.stack([gab_ref[b] for b, _ in groups]), sc_ref[0:1, :], sc_ref[1:2, :], state)
        for i, (b, h) in enumerate(groups):
            o_ref[b, :, h * HEAD_DIM:(h + 1) * HEAD_DIM] = o[i]
        state_ref[...] = new_state

        @pl.when(n == N - 1)
        def _():
            for wait in _gather_copies(src_refs, dst_refs, send_sems, recv_sems, local_sems)[1]:
                wait()

    seq = pl.BlockSpec((B, CHUNK, W), lambda n: (0, n, 0))
    return pl.pallas_call(
        body, grid=(N,),
        in_specs=[seq, seq, seq, pl.BlockSpec((B, CHUNK, GAB_W), lambda n: (0, n, 0)), _const_spec((8, 128))] + [_ANY] * ns,
        out_specs=[seq, pl.BlockSpec((B, 1, HEADS, HEAD_DIM, HEAD_DIM), lambda n: (0, n, 0, 0, 0))] + [_ANY] * ns,
        out_shape=[_sds((B, S, W), F32), _sds((B, N, HEADS, HEAD_DIM, HEAD_DIM), F32)] + [_sds((4,) + s.shape, s.dtype) for s in shards],
        scratch_shapes=[pltpu.VMEM((B * HEADS, HEAD_DIM, HEAD_DIM), F32)] + _gather_scratch(ns),
        compiler_params=_params(("arbitrary",)), name="gdn_chunk_fwd",
    )(q, k, v, gab, scal, *shards)


def _gdn_chunk_bwd(q, k, v, gab, scal, states, do, partials):
    B, S, W = q.shape
    N = S // CHUNK
    ns = len(partials)

    def body(*refs):
        q_ref, k_ref, v_ref, gab_ref, sc_ref, st_ref, do_ref = refs[:7]
        src_refs = refs[7:7 + ns]
        dq_ref, dk_ref, dv_ref, dgab_ref, dsc_ref = refs[7 + ns:12 + ns]
        dst_refs = refs[12 + ns:12 + 2 * ns]
        dstate_ref, send_sems, recv_sems, local_sems = refs[12 + 2 * ns:]
        n = pl.program_id(0)

        @pl.when(n == 0)
        def _():
            for start in _scatter_copies(src_refs, dst_refs, send_sems, recv_sems, local_sems)[0]:
                start()
            dstate_ref[...] = jnp.zeros_like(dstate_ref)
            dsc_ref[...] = jnp.zeros_like(dsc_ref)

        groups = [(b, h) for b in range(B) for h in range(HEADS)]
        gather = lambda ref: jnp.stack([ref[b, :, h * HEAD_DIM:(h + 1) * HEAD_DIM] for b, h in groups])
        _, pull = jax.vjp(_gdn_chunk_fn(len(groups), _chunk_masks()), gather(q_ref), gather(k_ref), gather(v_ref),
                          jnp.stack([gab_ref[b] for b, _ in groups]), sc_ref[0:1, :], sc_ref[1:2, :],
                          jnp.stack([st_ref[b, 0, h] for b, h in groups]))
        dq, dk, dv, dg, d_a, d_dt, dstate = pull((gather(do_ref), dstate_ref[...]))
        for i, (b, h) in enumerate(groups):
            lanes = slice(h * HEAD_DIM, (h + 1) * HEAD_DIM)
            dq_ref[b, :, lanes] = dq[i]
            dk_ref[b, :, lanes] = dk[i]
            dv_ref[b, :, lanes] = dv[i]
        for b in range(B):
            dgab_ref[b] = sum(dg[b * HEADS + h] for h in range(HEADS))
        dstate_ref[...] = dstate
        dsc_ref[0:1, :] += d_a
        dsc_ref[1:2, :] += d_dt

        @pl.when(n == N - 1)
        def _():
            for wait in _scatter_copies(src_refs, dst_refs, send_sems, recv_sems, local_sems)[1]:
                wait()

    seq = pl.BlockSpec((B, CHUNK, W), lambda n: (0, N - 1 - n, 0))
    gab_spec = pl.BlockSpec((B, CHUNK, GAB_W), lambda n: (0, N - 1 - n, 0))
    return pl.pallas_call(
        body, grid=(N,),
        in_specs=[seq, seq, seq, gab_spec, _const_spec((8, 128)),
                  pl.BlockSpec((B, 1, HEADS, HEAD_DIM, HEAD_DIM), lambda n: (0, N - 1 - n, 0, 0, 0)), seq] + [_ANY] * ns,
        out_specs=[seq, seq, seq, gab_spec, _const_spec((8, 128))] + [_ANY] * ns,
        out_shape=[_sds((B, S, W), F32)] * 3 + [_sds((B, S, GAB_W), F32), _sds((8, 128), F32)] + [_scattered_shape(p) for p in partials],
        scratch_shapes=[pltpu.VMEM((B * HEADS, HEAD_DIM, HEAD_DIM), F32)] + _scatter_scratch(ns),
        compiler_params=_params(("arbitrary",)), name="gdn_chunk_bwd",
    )(q, k, v, gab, scal, states, do, *partials)


def _mix_fn(ao, go, gz, w_mla, w_gdn):
    return tuple(_rms(ao[h], w_mla[h]) for h in range(HEADS)) + tuple(_rms(go[h], w_gdn) * _silu(gz[h]) for h in range(HEADS))


def _mix_operands(ao_ref, go_ref, gz_ref, nw_ref):
    blocks = lambda ref: [ref[:, h * HEAD_DIM:(h + 1) * HEAD_DIM] for h in range(HEADS)]
    return blocks(ao_ref), blocks(go_ref), blocks(gz_ref), [nw_ref[h:h + 1, :] for h in range(HEADS)], nw_ref[HEADS:HEADS + 1, :]


def _mix_fwd(ao, go, gz, nw, w_out, x2):
    T, D = x2.shape
    tm = min(TOKEN_TILE, T)
    MW = 2 * HEADS * HEAD_DIM

    def body(ao_ref, go_ref, gz_ref, nw_ref, w_ref, x_ref, mix_ref, h_ref):
        outs = _mix_fn(*_mix_operands(ao_ref, go_ref, gz_ref, nw_ref))
        for i, piece in enumerate(outs):
            mix_ref[:, i * HEAD_DIM:(i + 1) * HEAD_DIM] = piece.astype(BF16)
        h_ref[...] = x_ref[...] + jnp.dot(mix_ref[...], w_ref[...], preferred_element_type=F32)

    half = HEADS * HEAD_DIM
    return pl.pallas_call(
        body, grid=(T // tm,),
        in_specs=[_row_spec(tm, half), _row_spec(tm, half), _row_spec(tm, half), _const_spec((8, 128)), _const_spec((MW, D)),
                  _row_spec(tm, D)],
        out_specs=[_row_spec(tm, MW), _row_spec(tm, D)],
        out_shape=[_sds((T, MW), BF16), _sds((T, D), F32)],
        compiler_params=_params(("parallel",)), name="mix_fwd",
    )(ao, go, gz, nw, w_out, x2)


def _mix_bwd(ao, go, gz, nw, w_out, dh):
    T, D = dh.shape
    tm = min(TOKEN_TILE, T)
    MW = 2 * HEADS * HEAD_DIM
    half = HEADS * HEAD_DIM

    def body(ao_ref, go_ref, gz_ref, nw_ref, w_ref, dh_ref, dao_ref, dgo_ref, dgz_ref, dnw_ref):
        @pl.when(pl.program_id(0) == 0)
        def _():
            dnw_ref[...] = jnp.zeros_like(dnw_ref)

        d_mix = _dg(dh_ref[...].astype(BF16), w_ref[...], 1, 1, None)
        cts = tuple(d_mix[:, i * HEAD_DIM:(i + 1) * HEAD_DIM] for i in range(2 * HEADS))
        _, pull = jax.vjp(_mix_fn, *_mix_operands(ao_ref, go_ref, gz_ref, nw_ref))
        d_ao, d_go, d_gz, d_wm, d_wg = pull(cts)
        for h in range(HEADS):
            lanes = slice(h * HEAD_DIM, (h + 1) * HEAD_DIM)
            dao_ref[:, lanes] = d_ao[h]
            dgo_ref[:, lanes] = d_go[h]
            dgz_ref[:, lanes] = d_gz[h]
            dnw_ref[h:h + 1, :] += d_wm[h]
        dnw_ref[HEADS:HEADS + 1, :] += d_wg

    return pl.pallas_call(
        body, grid=(T // tm,),
        in_specs=[_row_spec(tm, half), _row_spec(tm, half), _row_spec(tm, half), _const_spec((8, 128)), _const_spec((MW, D)),
                  _row_spec(tm, D)],
        out_specs=[_row_spec(tm, half)] * 3 + [_const_spec((8, 128))],
        out_shape=[_sds((T, half), F32)] * 3 + [_sds((8, 128), F32)],
        compiler_params=_params(("arbitrary",)), name="mix_bwd",
    )(ao, go, gz, nw, w_out, dh)


def _up_spec(w_up, tf):
    per_shard = w_up.shape[2] // tf
    return pl.BlockSpec((None, w_up.shape[1], tf), lambda i, j: (j // per_shard, 0, j % per_shard))


def _mlp_fwd(h2, w_mn, w_up, w_down, target):
    T, D = h2.shape
    FF = w_down.shape[0]
    tm, tf = min(MLP_TOKEN_TILE, T), min(FF_TILE, w_up.shape[2])
    nf = FF // tf

    def body(h_ref, wn_ref, wu_ref, wd_ref, t_ref, hn_ref, dy_ref, sq_ref, acc_ref):
        j = pl.program_id(1)

        @pl.when(j == 0)
        def _():
            hn_ref[...] = _rms(h_ref[...], wn_ref[...]).astype(BF16)
            acc_ref[...] = jnp.zeros_like(acc_ref)

        up = jnp.dot(hn_ref[...], wu_ref[...], preferred_element_type=F32)
        act = jnp.square(jnp.maximum(up, 0.0)).astype(BF16)
        acc_ref[...] += jnp.dot(act, wd_ref[...], preferred_element_type=F32)

        @pl.when(j == nf - 1)
        def _():
            err = h_ref[...] + acc_ref[...] - t_ref[...]
            dy_ref[...] = err * (1.0 / D)
            sq_ref[...] = jnp.zeros_like(sq_ref) + jnp.sum(err * err)

    tok = lambda w: pl.BlockSpec((tm, w), lambda i, j: (i, 0))
    return pl.pallas_call(
        body, grid=(T // tm, nf),
        in_specs=[tok(D), _const_spec((1, D)), _up_spec(w_up, tf), pl.BlockSpec((tf, D), lambda i, j: (j, 0)), tok(D)],
        out_specs=[tok(D), tok(D), pl.BlockSpec((1, 8, 128), lambda i, j: (i, 0, 0))],
        out_shape=[_sds((T, D), BF16), _sds((T, D), F32), _sds((T // tm, 8, 128), F32)],
        scratch_shapes=[pltpu.VMEM((tm, D), F32)],
        compiler_params=_params(("parallel", "arbitrary")), name="mlp_fwd",
    )(h2, w_mn, w_up, w_down, target)


def _mlp_bwd(h2, w_mn, hn, w_up, w_down, dy):
    T, D = h2.shape
    FF = w_down.shape[0]
    tm, tf = min(MLP_TOKEN_TILE, T), min(FF_TILE, w_up.shape[2])
    nf = FF // tf

    def body(h_ref, wn_ref, hn_ref, wu_ref, wd_ref, dy_ref, dh_ref, act_ref, dup_ref, dwn_ref, acc_ref):
        i, j = pl.program_id(0), pl.program_id(1)

        @pl.when((i == 0) & (j == 0))
        def _():
            dwn_ref[...] = jnp.zeros_like(dwn_ref)

        @pl.when(j == 0)
        def _():
            acc_ref[...] = jnp.zeros_like(acc_ref)

        r = jnp.maximum(jnp.dot(hn_ref[...], wu_ref[...], preferred_element_type=F32), 0.0)
        act_ref[...] = (r * r).astype(BF16)
        d_act = _dg(dy_ref[...].astype(BF16), wd_ref[...], 1, 1, None)
        d_up = (d_act * (2.0 * r)).astype(BF16)
        dup_ref[...] = d_up
        acc_ref[...] += _dg(d_up, wu_ref[...], 1, 1, None)

        @pl.when(j == nf - 1)
        def _():
            _, pull = jax.vjp(_rms, h_ref[...], wn_ref[...])
            dh, dwn = pull(acc_ref[...])
            dh_ref[...] = dh + dy_ref[...]
            dwn_ref[...] += dwn

    tok = lambda w: pl.BlockSpec((tm, w), lambda i, j: (i, 0))
    ff = pl.BlockSpec((tm, tf), lambda i, j: (i, j))
    return pl.pallas_call(
        body, grid=(T // tm, nf),
        in_specs=[tok(D), _const_spec((1, D)), tok(D), _up_spec(w_up, tf), pl.BlockSpec((tf, D), lambda i, j: (j, 0)), tok(D)],
        out_specs=[tok(D), ff, ff, _const_spec((1, D))],
        out_shape=[_sds((T, D), F32), _sds((T, FF), BF16), _sds((T, FF), BF16), _sds((1, D), F32)],
        scratch_shapes=[pltpu.VMEM((tm, D), F32)],
        compiler_params=_params(("arbitrary", "arbitrary")), name="mlp_bwd",
    )(h2, w_mn, hn, w_up, w_down, dy)


def _rope_pad(a):
    z = jnp.zeros(a.shape[:-1] + (ROPE_HALF,), a.dtype)
    return jnp.concatenate([a[..., :ROPE_HALF], z, a[..., ROPE_HALF:], z], axis=-1)


def _rope_unpad(a):
    return jnp.concatenate([a[..., :ROPE_HALF], a[..., 2 * ROPE_HALF:3 * ROPE_HALF]], axis=-1)


_G0 = 2 * LORA + ROPE_DIM
W_IN_COLS = _G0 + GQKV_W + GZ_W + 2 * HEADS


def _widen_w_in_t(w_t):
    z = jnp.zeros((ROPE_HALF, w_t.shape[1]), w_t.dtype)
    pad = jnp.zeros((GAB_W - 2 * HEADS, w_t.shape[1]), w_t.dtype)
    return jnp.concatenate([w_t[:2 * LORA + ROPE_HALF], z, w_t[2 * LORA + ROPE_HALF:_G0], z, w_t[_G0:], pad], axis=0)


def _narrow_w_in_t(w_t):
    return jnp.concatenate([w_t[:2 * LORA + ROPE_HALF], w_t[2 * LORA + 2 * ROPE_HALF:2 * LORA + 3 * ROPE_HALF],
                            w_t[LAT_W:LAT_W + W_IN_COLS - _G0]], axis=0)


def _stack_mla(w_uq, w_ukv):
    uq = w_uq.reshape(LORA, HEADS, QK_DIM)
    ukv = w_ukv.reshape(LORA, HEADS, 2 * HEAD_DIM)
    parts = [uq[:, :, :HEAD_DIM], _rope_pad(uq[:, :, HEAD_DIM:]), ukv[:, :, :HEAD_DIM], ukv[:, :, HEAD_DIM:]]
    return jnp.concatenate([p.transpose(1, 0, 2) for p in parts], axis=0)


def _unstack_mla(w):
    p = [w[i * HEADS:(i + 1) * HEADS].transpose(1, 0, 2) for i in range(4)]
    uq = jnp.concatenate([p[0], _rope_unpad(p[1])], axis=-1).reshape(LORA, HEADS * QK_DIM)
    ukv = jnp.concatenate([p[2], p[3]], axis=-1).reshape(LORA, HEADS * 2 * HEAD_DIM)
    return uq, ukv


def _rows8(rows):
    a = jnp.concatenate(rows, axis=0)
    return jnp.pad(a, ((0, 8 - a.shape[0]), (0, 0)))


def _qk_norm_rows(q_norm_w, k_norm_w):
    return _rows8([q_norm_w[:, :HEAD_DIM], _rope_pad(q_norm_w[:, HEAD_DIM:]), k_norm_w[:, :HEAD_DIM], _rope_pad(k_norm_w[:, HEAD_DIM:])])


def _rope_rows():
    inv_freq = ROPE_THETA ** (-jnp.arange(ROPE_HALF, dtype=F32) / ROPE_HALF)
    z = jnp.zeros((ROPE_HALF,), F32)
    freq = jnp.concatenate([inv_freq, z, inv_freq, z])
    sign = jnp.concatenate([-jnp.ones((ROPE_HALF,), F32), z, jnp.ones((ROPE_HALF,), F32), z])
    return _rows8([freq[None], sign[None]])


def _round_up(n, m):
    return -(-n // m) * m


def _column_shards(a):
    return a.reshape(a.shape[0], 4, a.shape[1] // 4).transpose(1, 0, 2)


def _from_column_shards(a):
    return a.transpose(1, 0, 2).reshape(a.shape[1], 4 * a.shape[2])


def _pack_small(arrays):
    rows = [jnp.pad(a.reshape(-1), (0, _round_up(a.size, 128) - a.size)).reshape(-1, 128) for a in arrays]
    packed = jnp.concatenate(rows, axis=0)
    return jnp.pad(packed, ((0, _round_up(packed.shape[0], 8) - packed.shape[0]), (0, 0)))


def _unpack_small(packed, shapes):
    out, r = [], 0
    for s in shapes:
        n = math.prod(s)
        nr = _round_up(n, 128) // 128
        out.append(packed[r:r + nr].reshape(-1)[:n].reshape(s))
        r += nr
    return out


_ANY = pl.BlockSpec(memory_space=pl.ANY)
_OTHER_CHIPS = ((1, 0), (0, 1), (1, 1))


def _here():
    return lax.axis_index("x"), lax.axis_index("y"), lax.axis_index("c")


def _flip(v, bit):
    return 1 - v if bit else v


def _remote(src, dst, send_sems, recv_sems, k, to):
    return pltpu.make_async_remote_copy(src_ref=src, dst_ref=dst, send_sem=send_sems.at[k], recv_sem=recv_sems.at[k],
                                        device_id=to, device_id_type=MESH)


def _gather_copies(srcs, dsts, send_sems, recv_sems, local_sems):
    x, y, c = _here()
    slot = 2 * x + y
    starts, waits = [], []
    for i, (src, dst) in enumerate(zip(srcs, dsts)):
        own = pltpu.make_async_copy(src, dst.at[slot], local_sems.at[i])
        starts.append(own.start)
        waits.append(own.wait)
        for j, (fx, fy) in enumerate(_OTHER_CHIPS):
            cx, cy = _flip(x, fx), _flip(y, fy)
            push = _remote(src, dst.at[slot], send_sems, recv_sems, 3 * i + j, (cx, cy, c))
            landed = dst.at[2 * cx + cy]
            starts.append(push.start)
            waits += [_remote(landed, landed, send_sems, recv_sems, 3 * i + j, (cx, cy, c)).wait_recv, push.wait_send]
    return starts, waits


def _gather_scratch(n):
    return [pltpu.SemaphoreType.DMA((3 * n,)), pltpu.SemaphoreType.DMA((3 * n,)), pltpu.SemaphoreType.DMA((n,))]


def _all_gather(shards, name):
    ns = len(shards)

    def body(*refs):
        starts, waits = _gather_copies(refs[:ns], refs[ns:2 * ns], *refs[2 * ns:])
        for call in starts + waits:
            call()

    return pl.pallas_call(
        body, in_specs=[_ANY] * ns, out_specs=[_ANY] * ns, out_shape=[_sds((4,) + s.shape, s.dtype) for s in shards],
        scratch_shapes=_gather_scratch(ns), name=name,
    )(*shards)


def _by_lanes(shape):
    return (shape[-2] // 2) % 16 != 0


def _scattered_shape(p):
    r, c = p.shape[1:]
    return _sds((8, r, c // 2) if _by_lanes(p.shape) else (8, r // 2, c), p.dtype)


def _scatter_copies(srcs, dsts, send_sems, recv_sems, local_sems, whole=0):
    x, y, c = _here()
    me = 4 * x + 2 * y + c
    starts, waits = [], []
    for i, (src, dst) in enumerate(zip(srcs, dsts)):
        def piece(px, py, pc, src=src, entire=i >= len(srcs) - whole):
            if entire:
                return src
            if _by_lanes(src.shape):
                half = src.shape[2] // 2
                return src.at[2 * px + py, :, pl.ds(pl.multiple_of(pc * half, 128), half)]
            half = src.shape[1] // 2
            return src.at[2 * px + py, pl.ds(pl.multiple_of(pc * half, 16), half)]

        own = pltpu.make_async_copy(piece(x, y, c), dst.at[me], local_sems.at[i])
        starts.append(own.start)
        waits.append(own.wait)
        for k in range(1, 8):
            px, py, pc = _flip(x, k & 4), _flip(y, k & 2), _flip(c, k & 1)
            push = _remote(piece(px, py, pc), dst.at[me], send_sems, recv_sems, 7 * i + k - 1, (px, py, pc))
            landed = dst.at[4 * px + 2 * py + pc]
            starts.append(push.start)
            waits += [_remote(landed, landed, send_sems, recv_sems, 7 * i + k - 1, (px, py, pc)).wait_recv, push.wait_send]
    return starts, waits


def _scatter_scratch(n):
    return [pltpu.SemaphoreType.DMA((7 * n,)), pltpu.SemaphoreType.DMA((7 * n,)), pltpu.SemaphoreType.DMA((n,))]


def _scatter(partials, wholes, name):
    ns = len(partials) + len(wholes)

    def body(*refs):
        starts, waits = _scatter_copies(refs[:ns], refs[ns:2 * ns], *refs[2 * ns:], whole=len(wholes))
        for call in starts + waits:
            call()

    return pl.pallas_call(
        body, in_specs=[_ANY] * ns, out_specs=[_ANY] * ns,
        out_shape=[_scattered_shape(p) for p in partials] + [_sds((8,) + s.shape, s.dtype) for s in wholes],
        scratch_shapes=_scatter_scratch(ns), name=name,
    )(*partials, *wholes)


def _swapped_shape(half):
    r, c = half.shape
    return _sds((r, 2 * c) if _by_lanes((r, 2 * c)) else (2, r, c), half.dtype)


def _swap_copies(srcs, dsts, send_sems, recv_sems, local_sems):
    x, y, c = _here()
    sibling = (x, y, 1 - c)
    starts, waits = [], []
    for i, (src, dst) in enumerate(zip(srcs, dsts)):
        if len(dst.shape) == 2:
            lanes = src.shape[1]
            mine, other = (dst.at[:, pl.ds(pl.multiple_of(k * lanes, 128), lanes)] for k in (c, 1 - c))
        else:
            mine, other = dst.at[c], dst.at[1 - c]
        own = pltpu.make_async_copy(src, mine, local_sems.at[i])
        push = _remote(src, mine, send_sems, recv_sems, i, sibling)
        starts += [own.start, push.start]
        waits += [_remote(other, other, send_sems, recv_sems, i, sibling).wait_recv, push.wait_send, own.wait]
    return starts, waits


def _swap_scratch(n):
    return [pltpu.SemaphoreType.DMA((n,)), pltpu.SemaphoreType.DMA((n,)), pltpu.SemaphoreType.DMA((n,))]


def _exchange_halves(halves):
    ns = len(halves)

    def body(*refs):
        starts, waits = _swap_copies(refs[:ns], refs[ns:2 * ns], *refs[2 * ns:])
        for call in starts + waits:
            call()

    return pl.pallas_call(
        body, in_specs=[_ANY] * ns, out_specs=[_ANY] * ns, out_shape=[_swapped_shape(h) for h in halves],
        scratch_shapes=_swap_scratch(ns), name="exchange_halves",
    )(*halves)


def _row_tile(rows, row_bytes, budget):
    tr = rows
    while tr * row_bytes > budget and tr % 16 == 0:
        tr //= 2
    return tr


def _sum_slots(parts, name):
    _, rows, cols = parts.shape
    tr = _row_tile(rows, 8 * cols * 4, 2 * 1024 * 1024)

    def body(p_ref, o_ref):
        acc = p_ref[0].astype(F32)
        for d in range(1, 8):
            acc = acc + p_ref[d].astype(F32)
        o_ref[...] = acc

    return pl.pallas_call(
        body, grid=(rows // tr,), in_specs=[pl.BlockSpec((8, tr, cols), lambda i: (0, i, 0))],
        out_specs=pl.BlockSpec((tr, cols), lambda i: (i, 0)), out_shape=_sds((rows, cols), F32),
        compiler_params=_params(("parallel",)), name=name,
    )(parts)


def _adamw(w, g, m, v, name):
    rows, cols = w.shape[0], w.shape[-1]
    if w.ndim == 3:
        tr = max(d for d in range(1, rows + 1) if rows % d == 0 and d * 8 * cols * 4 * 14 <= VMEM_LIMIT // 2)
    else:
        tr = _row_tile(rows, 7 * cols * 4, 4 * 1024 * 1024)

    def body(w_ref, g_ref, m_ref, v_ref, d_ref, mo_ref, vo_ref):
        g = g_ref[...]
        m = ADAM_B1 * m_ref[...] + (1.0 - ADAM_B1) * g
        v = ADAM_B2 * v_ref[...] + (1.0 - ADAM_B2) * jnp.square(g)
        m_hat = m / (1.0 - ADAM_B1 ** ADAM_STEP)
        v_hat = v / (1.0 - ADAM_B2 ** ADAM_STEP)
        d_ref[...] = -ADAM_LR * (m_hat / (jnp.sqrt(v_hat) + ADAM_EPS) + ADAM_WD * w_ref[...])
        mo_ref[...] = m
        vo_ref[...] = v

    block = (tr,) + w.shape[1:]
    spec = pl.BlockSpec(block, lambda i: (i,) + (0,) * (len(block) - 1))
    return pl.pallas_call(
        body, grid=(rows // tr,), in_specs=[spec] * 4, out_specs=[spec] * 3, out_shape=[_sds(w.shape, F32)] * 3,
        compiler_params=_params(("parallel",)), name=name,
    )(w, g, m, v)


def kernel(x, positions, attn_norm_w, w_in, q_lat_norm_w, w_uq, kv_lat_norm_w, w_ukv, q_norm_w, k_norm_w, mla_out_norm_w, conv_w, a_log, dt_bias, gdn_norm_w, w_out, mlp_norm_w, w_up, w_down, loss_target, m_attn_norm_w, m_w_in, m_q_lat_norm_w, m_w_uq, m_kv_lat_norm_w, m_w_ukv, m_q_norm_w, m_k_norm_w, m_mla_out_norm_w, m_conv_w, m_a_log, m_dt_bias, m_gdn_norm_w, m_w_out, m_mlp_norm_w, m_w_up, m_w_down, v_attn_norm_w, v_w_in, v_q_lat_norm_w, v_w_uq, v_kv_lat_norm_w, v_w_ukv, v_q_norm_w, v_k_norm_w, v_mla_out_norm_w, v_conv_w, v_a_log, v_dt_bias, v_gdn_norm_w, v_w_out, v_mlp_norm_w, v_w_up, v_w_down):
    w = dict(zip(WEIGHTS, (attn_norm_w, w_in, q_lat_norm_w, w_uq, kv_lat_norm_w, w_ukv, q_norm_w, k_norm_w, mla_out_norm_w, conv_w,
                           a_log, dt_bias, gdn_norm_w, w_out, mlp_norm_w, w_up, w_down)))
    m = dict(zip(WEIGHTS, (m_attn_norm_w, m_w_in, m_q_lat_norm_w, m_w_uq, m_kv_lat_norm_w, m_w_ukv, m_q_norm_w, m_k_norm_w,
                           m_mla_out_norm_w, m_conv_w, m_a_log, m_dt_bias, m_gdn_norm_w, m_w_out, m_mlp_norm_w, m_w_up, m_w_down)))
    v = dict(zip(WEIGHTS, (v_attn_norm_w, v_w_in, v_q_lat_norm_w, v_w_uq, v_kv_lat_norm_w, v_w_ukv, v_q_norm_w, v_k_norm_w,
                           v_mla_out_norm_w, v_conv_w, v_a_log, v_dt_bias, v_gdn_norm_w, v_w_out, v_mlp_norm_w, v_w_up, v_w_down)))
    B, S, D = x.shape
    T = B * S
    x2, pos, target = x.reshape(T, D), positions.reshape(T, 1), loss_target.reshape(T, D)
    seq = lambda a: a.reshape(B, S, a.shape[-1])
    tok = lambda a: a.reshape(T, a.shape[-1])
    local = {n: w[n][0] for n in SHARDED}

    g_in, g_uq, g_ukv, g_conv = _all_gather([jnp.swapaxes(w_in, 1, 2)[0].astype(BF16), local["w_uq"].astype(BF16),
                                             local["w_ukv"].astype(BF16), local["conv_w"]], "gather_first_weights")
    w_in_p = _widen_w_in_t(g_in.reshape(-1, D))
    w_mla = _stack_mla(_from_column_shards(g_uq), _from_column_shards(g_ukv))
    conv_full = _from_column_shards(g_conv)
    ln_w = jnp.concatenate([q_lat_norm_w, kv_lat_norm_w], axis=0)
    qk_nw = _qk_norm_rows(q_norm_w, k_norm_w)
    rope_rows = _rope_rows()
    scal = _rows8([jnp.pad(a_log, ((0, 0), (0, 128 - HEADS))), jnp.pad(dt_bias, ((0, 0), (0, 128 - HEADS)))])
    mix_nw = _rows8([mla_out_norm_w[0], gdn_norm_w])

    xn, lat, gqkv, gz, gab = _in_proj_fwd(x2, attn_norm_w, w_in_p)
    q, k, v_att = _mla_pre_fwd(lat, pos, ln_w, w_mla, qk_nw, rope_rows)
    ao, lse, g_down = _attn_fwd(seq(q), seq(k), seq(v_att), [local["w_down"].astype(BF16)])
    gq, gk, gv = _gdn_pre_fwd(seq(gqkv), conv_full)
    go, states, g_out, w_up_b = _gdn_chunk_fwd(gq, gk, gv, seq(gab), scal, [local["w_out"].astype(BF16), local["w_up"].astype(BF16)])
    w_out_b = g_out.reshape(-1, D)
    w_down_b = g_down.reshape(-1, D)
    mix, h2 = _mix_fwd(tok(ao), tok(go), gz, mix_nw, w_out_b, x2)
    hn, dy, sq = _mlp_fwd(h2, mlp_norm_w, w_up_b, w_down_b, target)
    loss = lax.psum(jnp.sum(sq[:, 0, 0]) * (0.5 / D), ("x", "y", "c"))

    dh, act, d_up, d_mlp_nw = _mlp_bwd(h2, mlp_norm_w, hn, w_up_b, w_down_b, dy)
    p_down = _wgrad(act, dy, "wgrad_down").reshape(4, -1, D)
    p_up = _wgrad(hn, d_up, "wgrad_up", column_shards=4)
    d_ao, d_go, d_gz, d_mix_nw = _mix_bwd(tok(ao), tok(go), gz, mix_nw, w_out_b, dh)
    p_out = _wgrad(mix, dh, "wgrad_out").reshape(4, -1, D)
    d_gq, d_gk, d_gv, d_gab, d_scal, s_up, s_down, s_out = _gdn_chunk_bwd(gq, gk, gv, seq(gab), scal, states, seq(d_go),
                                                                          [p_up, p_down, p_out])
    dxq, dxk, dxv, dcq, dck, dcv = _gdn_pre_bwd(seq(gqkv), conv_full, d_gq, d_gk, d_gv)
    early = ("w_up", "w_down", "w_out")
    early_halves = [_sum_slots(s, "sum_" + n) for n, s in zip(early, (s_up, s_down, s_out))]
    dq, dk, dv, *early_grads = _attn_bwd(seq(q), seq(k), seq(v_att), ao, lse, seq(d_ao), early_halves)
    d_lat, d_ln, d_w_mla, d_qk_nw = _mla_pre_bwd(lat, pos, ln_w, w_mla, qk_nw, rope_rows, tok(dq), tok(dk), tok(dv))
    grad_x2, d_proj, d_attn_nw = _in_proj_bwd([d_lat, tok(dxq), tok(dxk), tok(dxv), d_gz, tok(d_gab)], w_in_p, x2, attn_norm_w, dh)
    p_in = _narrow_w_in_t(_wgrad(d_proj, xn, "wgrad_in")).reshape(4, -1, D)
    p_uq, p_ukv = (_column_shards(a).astype(BF16) for a in _unstack_mla(d_w_mla))
    small_partial = {
        "attn_norm_w": d_attn_nw, "q_lat_norm_w": d_ln[0:1], "kv_lat_norm_w": d_ln[1:2],
        "q_norm_w": jnp.concatenate([d_qk_nw[0:1], _rope_unpad(d_qk_nw[1:2])], axis=-1),
        "k_norm_w": jnp.concatenate([d_qk_nw[2:3], _rope_unpad(d_qk_nw[3:4])], axis=-1),
        "mla_out_norm_w": d_mix_nw[None, :HEADS], "a_log": d_scal[0:1, :HEADS], "dt_bias": d_scal[1:2, :HEADS],
        "gdn_norm_w": d_mix_nw[HEADS:HEADS + 1], "mlp_norm_w": d_mlp_nw,
    }
    conv_partial = jnp.concatenate([dcq, dck, dcv], axis=-1)
    s_in, s_uq, s_ukv, s_small = _scatter([p_in, p_uq, p_ukv], [_pack_small([small_partial[n] for n in SMALL] + [conv_partial])],
                                          "scatter_last_partials")

    late = ("w_in", "w_uq", "w_ukv")
    late_grads = _exchange_halves([_sum_slots(s, "sum_" + n) for n, s in zip(late, (s_in, s_uq, s_ukv))])
    names = early + late
    grad = {n: g.reshape(-1, g.shape[-1]) for n, g in zip(names, list(early_grads) + list(late_grads))}
    small_shapes = [w[n].shape for n in SMALL]
    *g_small, g_conv_all = _unpack_small(_sum_slots(s_small, "sum_small"), small_shapes + [conv_partial.shape])
    grad.update(zip(SMALL, g_small))
    conv_cols = local["conv_w"].shape[1]
    grad["conv_w"] = lax.dynamic_slice_in_dim(g_conv_all, (2 * lax.axis_index("x") + lax.axis_index("y")) * conv_cols, conv_cols, axis=1)

    delta, new_m, new_v = {}, {}, {}
    for n in names:
        if n == "w_in":
            stored = lambda a: jnp.transpose(a, (2, 0, 1))
            outs = _adamw(stored(w[n]), grad[n][:, None, :], stored(m[n]), stored(v[n]), "adamw_" + n)
            grad[n], delta[n], new_m[n], new_v[n] = (jnp.transpose(a, (1, 2, 0)) for a in (grad[n][:, None, :], *outs))
        else:
            delta[n], new_m[n], new_v[n] = _adamw(local[n], grad[n], m[n][0], v[n][0], "adamw_" + n)
    packed_names = SMALL + ("conv_w",)
    packed_shapes = small_shapes + [local["conv_w"].shape]
    take = lambda d: _pack_small([d[n][0] if n == "conv_w" and d[n].ndim == 3 else d[n] for n in packed_names])
    outs = _adamw(take(w), take(grad), take(m), take(v), "adamw_small")
    for d, packed in zip((delta, new_m, new_v), outs):
        d.update(zip(packed_names, _unpack_small(packed, packed_shapes)))

    def in_order(d):
        return [d[n].reshape(w[n].shape) for n in WEIGHTS]

    return (loss, grad_x2.reshape(B, S, D), *in_order(grad), *in_order(delta), *in_order(new_m), *in_order(new_v))
```

```python
import functools
import math

import jax
import jax.numpy as jnp
from jax import lax
from jax.experimental import pallas as pl
from jax.experimental.pallas import tpu as pltpu

F32 = jnp.float32
BF16 = jnp.bfloat16
MESH = pl.DeviceIdType.MESH

EPS = 1e-6
HEADS = 4
HEAD_DIM = 128
ROPE_DIM = 64
ROPE_HALF = 32
QK_DIM = 192
QK_PAD = 256
LORA = 256
CHUNK = 64
CONV_TAPS = 4
ROPE_THETA = 10000.0
ATTN_SCALE = QK_DIM ** -0.5

LAT_W = 640
GQKV_W = 3 * HEADS * HEAD_DIM
GZ_W = HEADS * HEAD_DIM
GAB_W = 128
PROJ_SPLITS = ((0, LAT_W), (LAT_W, LAT_W + GQKV_W), (LAT_W + GQKV_W, LAT_W + GQKV_W + GZ_W),
               (LAT_W + GQKV_W + GZ_W, LAT_W + GQKV_W + GZ_W + GAB_W))
PROJ_W = PROJ_SPLITS[-1][1]

ADAM_LR = 0.001
ADAM_B1 = 0.9
ADAM_B2 = 0.999
ADAM_EPS = 1e-08
ADAM_WD = 0.01
ADAM_STEP = 10

TOKEN_TILE = 512
MLP_TOKEN_TILE = 1024
FF_TILE = 512
ATTN_TILE = 512
ATTN_HEADS_PER_STEP = 2
WGRAD_OUT_BYTES = 8 * 1024 * 1024
VMEM_LIMIT = 48 * 1024 * 1024

SHARDED = ("w_in", "w_uq", "w_ukv", "conv_w", "w_out", "w_up", "w_down")
SMALL = ("attn_norm_w", "q_lat_norm_w", "kv_lat_norm_w", "q_norm_w", "k_norm_w", "mla_out_norm_w", "a_log", "dt_bias",
         "gdn_norm_w", "mlp_norm_w")
WEIGHTS = ("attn_norm_w", "w_in", "q_lat_norm_w", "w_uq", "kv_lat_norm_w", "w_ukv", "q_norm_w", "k_norm_w", "mla_out_norm_w",
           "conv_w", "a_log", "dt_bias", "gdn_norm_w", "w_out", "mlp_norm_w", "w_up", "w_down")


def _sds(shape, dtype):
    return jax.ShapeDtypeStruct(shape, dtype)


def _params(semantics):
    return pltpu.CompilerParams(dimension_semantics=semantics, vmem_limit_bytes=VMEM_LIMIT)


def _block(n):
    for b in (512, 256, 128):
        if n % b == 0:
            return b
    return n


def _dg(a, b, ca, cb, prec):
    lead = a.ndim - 2
    batch = (tuple(range(lead)),) * 2
    return lax.dot_general(a, b, (((ca + lead,), (cb + lead,)), batch), precision=prec, preferred_element_type=F32)


def _split_bf16(a):
    hi = a.astype(BF16)
    return hi, (a - hi.astype(F32)).astype(BF16)


def _dot_bf16(a, b, ca, cb):
    return _dg(a.astype(BF16), b.astype(BF16), ca, cb, None)


def _dot_bf16x3(a, b, ca, cb):
    a_hi, a_lo = _split_bf16(a)
    b_hi, b_lo = _split_bf16(b)
    lead = a.ndim - 2
    return _dg(jnp.concatenate([a_hi, a_hi, a_lo], axis=ca + lead), jnp.concatenate([b_hi, b_lo, b_hi], axis=cb + lead), ca, cb, None)


def _matmul_family(dot):
    def nn_raw(a, b):
        return dot(a, b, 1, 0)

    def nt_raw(a, b):
        return dot(a, b, 1, 1)

    def tn_raw(a, b):
        return dot(a, b, 0, 0)

    @jax.custom_vjp
    def nn(a, b):
        return nn_raw(a, b)

    nn.defvjp(lambda a, b: (nn_raw(a, b), (a, b)), lambda r, g: (nt_raw(g, r[1]), tn_raw(r[0], g)))

    @jax.custom_vjp
    def nt(a, b):
        return nt_raw(a, b)

    nt.defvjp(lambda a, b: (nt_raw(a, b), (a, b)), lambda r, g: (nn_raw(g, r[1]), tn_raw(g, r[0])))

    @jax.custom_vjp
    def tn(a, b):
        return tn_raw(a, b)

    tn.defvjp(lambda a, b: (tn_raw(a, b), (a, b)), lambda r, g: (nt_raw(r[1], g), nn_raw(r[0], g)))
    return nn, nt, tn


_bf_nn, _bf_nt, _bf_tn = _matmul_family(_dot_bf16)
_hi_nn, _hi_nt, _hi_tn = _matmul_family(_dot_bf16x3)


@jax.custom_vjp
def _lane_halves(x):
    n = x.shape[-1] // 2
    return x[..., :n], x[..., n:]


_lane_halves.defvjp(lambda x: (_lane_halves(x), None), lambda _, g: (jnp.concatenate(g, axis=-1),))


@jax.custom_vjp
def _row_halves(x):
    n = x.shape[-2] // 2
    return x[..., :n, :], x[..., n:, :]


_row_halves.defvjp(lambda x: (_row_halves(x), None), lambda _, g: (jnp.concatenate(g, axis=-2),))


@jax.custom_vjp
def _swap_halves(t):
    return pltpu.roll(t, 64, 1)


_swap_halves.defvjp(lambda t: (pltpu.roll(t, 64, 1), None), lambda _, g: (pltpu.roll(g, 64, 1),))


@functools.partial(jax.custom_vjp, nondiff_argnums=(2,))
def _shift_rows(x, keep, s):
    return pltpu.roll(x, s, 0) * keep


def _shift_rows_fwd(x, keep, s):
    return pltpu.roll(x, s, 0) * keep, keep


def _shift_rows_bwd(s, keep, g):
    return pltpu.roll(g * keep, keep.shape[0] - s, 0), jnp.zeros_like(keep)


_shift_rows.defvjp(_shift_rows_fwd, _shift_rows_bwd)


def _sigmoid(x):
    return 0.5 * jnp.tanh(0.5 * x) + 0.5


def _softplus(x):
    return jnp.maximum(x, 0.0) + jnp.log(1.0 + jnp.exp(jnp.minimum(x, -x)))


def _silu(x):
    return x * _sigmoid(x)


def _rms(x, w, n=None):
    n = x.shape[-1] if n is None else n
    r = lax.rsqrt(jnp.sum(x * x, axis=-1, keepdims=True) * (1.0 / n) + EPS)
    return x * r * w


def _rope(t, cos_f, sin_f):
    return t * cos_f + _swap_halves(t) * sin_f


def _rope_tables(pos_col, freq_row, sign_row):
    ang = pos_col.astype(F32) * freq_row
    return jnp.cos(ang), jnp.sin(ang) * sign_row


def _onehot_row(lane):
    return (lax.broadcasted_iota(jnp.int32, (1, 128), 1) == lane).astype(F32)


def _row_spec(tm, w):
    return pl.BlockSpec((tm, w), lambda i: (i, 0))


def _const_spec(shape):
    return pl.BlockSpec(shape, lambda *_: (0,) * len(shape))


def _in_proj_fwd(x2, w_an, w_in_p):
    T, D = x2.shape
    tm = min(TOKEN_TILE, T)

    def body(x_ref, wn_ref, w_ref, xn_ref, lat_ref, gqkv_ref, gz_ref, gab_ref):
        x = x_ref[...]
        r = lax.rsqrt(jnp.mean(x * x, axis=-1, keepdims=True) + EPS)
        xn = (x * r * wn_ref[...]).astype(BF16)
        xn_ref[...] = xn
        for ref, (a, b) in zip((lat_ref, gqkv_ref, gz_ref, gab_ref), PROJ_SPLITS):
            ref[...] = _dg(xn, w_ref[a:b, :], 1, 1, None)

    widths = [b - a for a, b in PROJ_SPLITS]
    return pl.pallas_call(
        body, grid=(T // tm,),
        in_specs=[_row_spec(tm, D), _const_spec((1, D)), _const_spec((PROJ_W, D))],
        out_specs=[_row_spec(tm, D)] + [_row_spec(tm, w) for w in widths],
        out_shape=[_sds((T, D), BF16)] + [_sds((T, w), F32) for w in widths],
        compiler_params=_params(("parallel",)), name="in_proj_fwd",
    )(x2, w_an, w_in_p)


def _in_proj_bwd(pieces, w_in_p, x2, w_an, dh):
    T, D = x2.shape
    tm = min(TOKEN_TILE, T)
    widths = [p.shape[1] for p in pieces]
    starts = [sum(widths[:i]) for i in range(len(widths))]
    assert sum(widths) == PROJ_W

    def body(*refs):
        piece_refs = refs[:len(pieces)]
        w_ref, x_ref, wn_ref, dh_ref, dx_ref, dp_ref, dwn_ref = refs[len(pieces):]

        @pl.when(pl.program_id(0) == 0)
        def _():
            dwn_ref[...] = jnp.zeros_like(dwn_ref)

        dxn = jnp.zeros((tm, D), F32)
        for ref, a, width in zip(piece_refs, starts, widths):
            piece = ref[...].astype(BF16)
            dp_ref[:, a:a + width] = piece
            dxn += _dg(piece, w_ref[a:a + width, :], 1, 0, None)
        _, pull = jax.vjp(_rms, x_ref[...], wn_ref[...])
        dx, dwn = pull(dxn)
        dx_ref[...] = dx + dh_ref[...]
        dwn_ref[...] += dwn

    return pl.pallas_call(
        body, grid=(T // tm,),
        in_specs=[_row_spec(tm, w) for w in widths] + [_const_spec((PROJ_W, D)), _row_spec(tm, D), _const_spec((1, D)),
                                                       _row_spec(tm, D)],
        out_specs=[_row_spec(tm, D), _row_spec(tm, PROJ_W), _const_spec((1, D))],
        out_shape=[_sds((T, D), F32), _sds((T, PROJ_W), BF16), _sds((1, D), F32)],
        compiler_params=_params(("arbitrary",)), name="in_proj_bwd",
    )(*pieces, w_in_p, x2, w_an, dh)


def _wgrad(a, b, name, column_shards=1, out_dtype=BF16):
    T, k1 = a.shape
    k2 = b.shape[1]
    per_shard = k2 // column_shards
    tt = min(TOKEN_TILE, T)
    b1 = k1
    while b1 * k2 * 4 > WGRAD_OUT_BYTES and b1 % 256 == 0:
        b1 //= 2
    step = _block(per_shard)

    def body(a_ref, b_ref, o_ref, acc_ref):
        t = pl.program_id(1)

        @pl.when(t == 0)
        def _():
            acc_ref[...] = jnp.zeros_like(acc_ref)

        a_t = a_ref[...].astype(BF16).T
        for c0 in range(0, k2, step):
            part = jnp.dot(a_t, b_ref[:, c0:c0 + step].astype(BF16), preferred_element_type=F32)
            if column_shards == 1:
                acc_ref[:, c0:c0 + step] += part
            else:
                acc_ref[c0 // per_shard, :, c0 % per_shard:c0 % per_shard + step] += part

        @pl.when(t == T // tt - 1)
        def _():
            o_ref[...] = acc_ref[...].astype(o_ref.dtype)

    if column_shards == 1:
        block, out_spec, out_shape = (b1, k2), pl.BlockSpec((b1, k2), lambda i, t: (i, 0)), _sds((k1, k2), out_dtype)
    else:
        block = (column_shards, b1, per_shard)
        out_spec, out_shape = pl.BlockSpec(block, lambda i, t: (0, i, 0)), _sds((column_shards, k1, per_shard), out_dtype)
    return pl.pallas_call(
        body, grid=(k1 // b1, T // tt),
        in_specs=[pl.BlockSpec((tt, b1), lambda i, t: (t, i)), pl.BlockSpec((tt, k2), lambda i, t: (t, 0))],
        out_specs=out_spec, out_shape=out_shape, scratch_shapes=[pltpu.VMEM(block, F32)],
        compiler_params=_params(("parallel", "arbitrary")), name=name,
    )(a, b)


def _mla_pre_fn(q_lat, kv_lat, kpe, ln_q, ln_kv, w_list, qn_n, qn_p, kn_n, kn_p, cos_f, sin_f):
    qn = _rms(q_lat, ln_q)
    kvn = _rms(kv_lat, ln_kv)
    kp = _rope(_rms(kpe, kn_p, ROPE_DIM), cos_f, sin_f)
    outs = []
    for h in range(HEADS):
        outs.append(_rms(_bf_nn(qn, w_list[h]), qn_n))
        outs.append(_rope(_rms(_bf_nn(qn, w_list[HEADS + h]), qn_p, ROPE_DIM), cos_f, sin_f))
        outs.append(_rms(_bf_nn(kvn, w_list[2 * HEADS + h]), kn_n))
        outs.append(_bf_nn(kvn, w_list[3 * HEADS + h]))
    return tuple(outs) + (kp,)


def _mla_pre_operands(lat_ref, pos_ref, ln_ref, w_ref, nw_ref, rope_ref):
    cos_f, sin_f = _rope_tables(pos_ref[...], rope_ref[0:1, :], rope_ref[1:2, :])
    diff = (lat_ref[:, 0:LORA], lat_ref[:, LORA:2 * LORA], lat_ref[:, 2 * LORA:LAT_W], ln_ref[0:1, :], ln_ref[1:2, :],
            [w_ref[i].astype(F32) for i in range(4 * HEADS)], nw_ref[0:1, :], nw_ref[1:2, :], nw_ref[2:3, :], nw_ref[3:4, :])
    return diff, cos_f, sin_f


def _mla_pre_fwd(lat, pos, ln_w, w_mla, nw, rope_rows):
    T = lat.shape[0]
    tm = min(TOKEN_TILE, T)

    def body(lat_ref, pos_ref, ln_ref, w_ref, nw_ref, rope_ref, q_ref, k_ref, v_ref):
        diff, cos_f, sin_f = _mla_pre_operands(lat_ref, pos_ref, ln_ref, w_ref, nw_ref, rope_ref)
        outs = _mla_pre_fn(*diff, cos_f, sin_f)
        kp = outs[-1].astype(BF16)
        for h in range(HEADS):
            q_n, q_p, k_n, v = outs[4 * h:4 * h + 4]
            q_ref[:, h * QK_PAD:h * QK_PAD + HEAD_DIM] = q_n.astype(BF16)
            q_ref[:, h * QK_PAD + HEAD_DIM:(h + 1) * QK_PAD] = q_p.astype(BF16)
            k_ref[:, h * QK_PAD:h * QK_PAD + HEAD_DIM] = k_n.astype(BF16)
            k_ref[:, h * QK_PAD + HEAD_DIM:(h + 1) * QK_PAD] = kp
            v_ref[:, h * HEAD_DIM:(h + 1) * HEAD_DIM] = v.astype(BF16)

    return pl.pallas_call(
        body, grid=(T // tm,),
        in_specs=[_row_spec(tm, LAT_W), _row_spec(tm, 1), _const_spec((2, LORA)), _const_spec((4 * HEADS, LORA, 128)),
                  _const_spec((8, 128)), _const_spec((8, 128))],
        out_specs=[_row_spec(tm, HEADS * QK_PAD), _row_spec(tm, HEADS * QK_PAD), _row_spec(tm, HEADS * HEAD_DIM)],
        out_shape=[_sds((T, HEADS * QK_PAD), BF16), _sds((T, HEADS * QK_PAD), BF16), _sds((T, HEADS * HEAD_DIM), BF16)],
        compiler_params=_params(("parallel",)), name="mla_pre_fwd",
    )(lat, pos, ln_w, w_mla, nw, rope_rows)


def _mla_pre_bwd(lat, pos, ln_w, w_mla, nw, rope_rows, dq, dk, dv):
    T = lat.shape[0]
    tm = min(TOKEN_TILE, T)

    def body(lat_ref, pos_ref, ln_ref, w_ref, nw_ref, rope_ref, dq_ref, dk_ref, dv_ref, dlat_ref, dln_ref, dw_ref, dnw_ref):
        @pl.when(pl.program_id(0) == 0)
        def _():
            dln_ref[...] = jnp.zeros_like(dln_ref)
            dw_ref[...] = jnp.zeros_like(dw_ref)
            dnw_ref[...] = jnp.zeros_like(dnw_ref)

        diff, cos_f, sin_f = _mla_pre_operands(lat_ref, pos_ref, ln_ref, w_ref, nw_ref, rope_ref)
        _, pull = jax.vjp(lambda *a: _mla_pre_fn(*a, cos_f, sin_f), *diff)
        cts = []
        d_kp = jnp.zeros((tm, 128), F32)
        for h in range(HEADS):
            cts.append(dq_ref[:, h * QK_PAD:h * QK_PAD + HEAD_DIM])
            cts.append(dq_ref[:, h * QK_PAD + HEAD_DIM:(h + 1) * QK_PAD])
            cts.append(dk_ref[:, h * QK_PAD:h * QK_PAD + HEAD_DIM])
            cts.append(dv_ref[:, h * HEAD_DIM:(h + 1) * HEAD_DIM])
            d_kp += dk_ref[:, h * QK_PAD + HEAD_DIM:(h + 1) * QK_PAD]
        d_ql, d_kvl, d_kpe, d_lnq, d_lnkv, d_w, d_qn_n, d_qn_p, d_kn_n, d_kn_p = pull(tuple(cts) + (d_kp,))
        dlat_ref[:, 0:LORA] = d_ql
        dlat_ref[:, LORA:2 * LORA] = d_kvl
        dlat_ref[:, 2 * LORA:LAT_W] = d_kpe
        dln_ref[0:1, :] += d_lnq
        dln_ref[1:2, :] += d_lnkv
        for i in range(4 * HEADS):
            dw_ref[i] += d_w[i]
        for i, d in enumerate((d_qn_n, d_qn_p, d_kn_n, d_kn_p)):
            dnw_ref[i:i + 1, :] += d

    return pl.pallas_call(
        body, grid=(T // tm,),
        in_specs=[_row_spec(tm, LAT_W), _row_spec(tm, 1), _const_spec((2, LORA)), _const_spec((4 * HEADS, LORA, 128)),
                  _const_spec((8, 128)), _const_spec((8, 128)),
                  _row_spec(tm, HEADS * QK_PAD), _row_spec(tm, HEADS * QK_PAD), _row_spec(tm, HEADS * HEAD_DIM)],
        out_specs=[_row_spec(tm, LAT_W), _const_spec((2, LORA)), _const_spec((4 * HEADS, LORA, 128)), _const_spec((8, 128))],
        out_shape=[_sds((T, LAT_W), F32), _sds((2, LORA), F32), _sds((4 * HEADS, LORA, 128), F32), _sds((8, 128), F32)],
        compiler_params=_params(("arbitrary",)), name="mla_pre_bwd",
    )(lat, pos, ln_w, w_mla, nw, rope_rows, dq, dk, dv)


def _causal_mask(i, j, tq, tk):
    row = i * tq + lax.broadcasted_iota(jnp.int32, (tq, tk), 0)
    col = j * tk + lax.broadcasted_iota(jnp.int32, (tq, tk), 1)
    return col <= row


def _attn_fwd(q, k, v, shards):
    B, S, _ = q.shape
    t = min(ATTN_TILE, S)
    nq = S // t
    ns = len(shards)

    hp = ATTN_HEADS_PER_STEP
    qk = lambda h: slice(h * QK_PAD, (h + 1) * QK_PAD)
    vd = lambda h: slice(h * HEAD_DIM, (h + 1) * HEAD_DIM)

    def body(*refs):
        q_ref, k_ref, v_ref = refs[:3]
        src_refs = refs[3:3 + ns]
        o_ref, lse_ref = refs[3 + ns:5 + ns]
        dst_refs = refs[5 + ns:5 + 2 * ns]
        sems = refs[5 + 2 * ns:]
        b, g, i = pl.program_id(0), pl.program_id(1), pl.program_id(2)
        qb = [q_ref[0, :, qk(h)] for h in range(hp)]

        step_no = (b * (HEADS // hp) + g) * nq + i
        for phase, at in enumerate((0, (3 * B * (HEADS // hp) * nq) // 4)):
            @pl.when(step_no == at)
            def _(phase=phase):
                for call in _gather_copies(src_refs, dst_refs, *sems)[phase]:
                    call()

        def step(j, carry, diagonal):
            rows = pl.ds(pl.multiple_of(j * t, t), t)
            s = [_dg(qb[h], k_ref[0, rows, qk(h)], 1, 1, None) * ATTN_SCALE for h in range(hp)]
            if diagonal:
                keep = _causal_mask(0, 0, t, t)
                s = [jnp.where(keep, x, -1e30) for x in s]
            m_new = [jnp.maximum(carry[h][0], jnp.max(s[h], axis=-1, keepdims=True)) for h in range(hp)]
            p = [jnp.exp(s[h] - m_new[h]) for h in range(hp)]
            alpha = [jnp.exp(carry[h][0] - m_new[h]) for h in range(hp)]
            l = [alpha[h] * carry[h][1] + jnp.sum(p[h], axis=-1, keepdims=True) for h in range(hp)]
            pv = [jnp.dot(p[h].astype(BF16), v_ref[0, rows, vd(h)], preferred_element_type=F32) for h in range(hp)]
            return tuple((m_new[h], l[h], alpha[h] * carry[h][2] + pv[h]) for h in range(hp))

        init = tuple((jnp.full((t, 1), -1e30, F32), jnp.zeros((t, 1), F32), jnp.zeros((t, HEAD_DIM), F32)) for _ in range(hp))
        below = lax.fori_loop(0, i, lambda j, carry: step(j, carry, False), init)
        for h, (m, l, acc) in enumerate(step(i, below, True)):
            o_ref[0, :, vd(h)] = acc / l
            lse_ref[0, h] = m + jnp.log(l)

        @pl.when((b == B - 1) & (g == HEADS // hp - 1) & (i == nq - 1))
        def _():
            for wait in _gather_copies(src_refs, dst_refs, *sems)[2]:
                wait()

    return pl.pallas_call(
        body, grid=(B, HEADS // hp, nq),
        in_specs=[pl.BlockSpec((1, t, hp * QK_PAD), lambda b, g, i: (b, i, g)),
                  pl.BlockSpec((1, S, hp * QK_PAD), lambda b, g, i: (b, 0, g)),
                  pl.BlockSpec((1, S, hp * HEAD_DIM), lambda b, g, i: (b, 0, g))] + [_ANY] * ns,
        out_specs=[pl.BlockSpec((1, t, hp * HEAD_DIM), lambda b, g, i: (b, i, g)),
                   pl.BlockSpec((1, hp, t, 1), lambda b, g, i: (b, g, i, 0))] + [_ANY] * ns,
        out_shape=[_sds((B, S, HEADS * HEAD_DIM), F32), _sds((B, HEADS, S, 1), F32)] + [_sds((4,) + s.shape, s.dtype) for s in shards],
        scratch_shapes=_gather_scratch(ns),
        compiler_params=_params(("arbitrary", "arbitrary", "arbitrary")), name="attn_fwd",
    )(q, k, v, *shards)


def _attn_bwd(q, k, v, o, lse, do, halves):
    B, S, _ = q.shape
    t = min(ATTN_TILE, S)
    nq = S // t
    ns = len(halves)

    hp = ATTN_HEADS_PER_STEP
    qk = lambda h: slice(h * QK_PAD, (h + 1) * QK_PAD)
    vd = lambda h: slice(h * HEAD_DIM, (h + 1) * HEAD_DIM)
    heads = range(hp)

    def body(*refs):
        q_ref, k_ref, v_ref, o_ref, lse_ref, do_ref = refs[:6]
        src_refs = refs[6:6 + ns]
        dq_ref, dk_ref, dv_ref = refs[6 + ns:9 + ns]
        dst_refs = refs[9 + ns:9 + 2 * ns]
        dsum_ref, send_sems, recv_sems, local_sems = refs[9 + 2 * ns:]
        b, g, j = pl.program_id(0), pl.program_id(1), pl.program_id(2)

        @pl.when((b == 0) & (g == 0) & (j == 0))
        def _():
            for start in _swap_copies(src_refs, dst_refs, send_sems, recv_sems, local_sems)[0]:
                start()

        @pl.when(j == 0)
        def _():
            dq_ref[...] = jnp.zeros_like(dq_ref)
            for h in heads:
                dsum_ref[h] = jnp.sum(do_ref[0, :, vd(h)] * o_ref[0, :, vd(h)], axis=-1, keepdims=True)

        kb = [k_ref[0, :, qk(h)] for h in heads]
        vb = [v_ref[0, :, vd(h)] for h in heads]

        def step(i, carry, diagonal):
            rows = pl.ds(pl.multiple_of(i * t, t), t)
            qb = [q_ref[0, rows, qk(h)] for h in heads]
            dob = [do_ref[0, rows, vd(h)].astype(BF16) for h in heads]
            s = [_dg(qb[h], kb[h], 1, 1, None) * ATTN_SCALE for h in heads]
            p = [jnp.exp(s[h] - lse_ref[0, h, rows, :]) for h in heads]
            if diagonal:
                keep = _causal_mask(0, 0, t, t)
                p = [jnp.where(keep, x, 0.0) for x in p]
            dp = [_dg(dob[h], vb[h], 1, 1, None) for h in heads]
            dv = [carry[h][1] + _dg(p[h].astype(BF16), dob[h], 0, 0, None) for h in heads]
            ds = [(p[h] * (dp[h] - dsum_ref[h, rows, :]) * ATTN_SCALE).astype(BF16) for h in heads]
            for h in heads:
                dq_ref[0, rows, qk(h)] += jnp.dot(ds[h], kb[h], preferred_element_type=F32)
            return tuple((carry[h][0] + _dg(ds[h], qb[h], 0, 0, None), dv[h]) for h in heads)

        zeros = tuple((jnp.zeros((t, QK_PAD), F32), jnp.zeros((t, HEAD_DIM), F32)) for _ in heads)
        on_diagonal = step(j, zeros, True)
        done = lax.fori_loop(j + 1, nq, lambda i, carry: step(i, carry, False), on_diagonal)
        for h, (dk, dv) in enumerate(done):
            dk_ref[0, :, qk(h)] = dk
            dv_ref[0, :, vd(h)] = dv

        @pl.when((b == B - 1) & (g == HEADS // hp - 1) & (j == nq - 1))
        def _():
            for wait in _swap_copies(src_refs, dst_refs, send_sems, recv_sems, local_sems)[1]:
                wait()

    return pl.pallas_call(
        body, grid=(B, HEADS // hp, nq),
        in_specs=[pl.BlockSpec((1, S, hp * QK_PAD), lambda b, g, j: (b, 0, g)),
                  pl.BlockSpec((1, t, hp * QK_PAD), lambda b, g, j: (b, j, g)),
                  pl.BlockSpec((1, t, hp * HEAD_DIM), lambda b, g, j: (b, j, g)),
                  pl.BlockSpec((1, S, hp * HEAD_DIM), lambda b, g, j: (b, 0, g)),
                  pl.BlockSpec((1, hp, S, 1), lambda b, g, j: (b, g, 0, 0)),
                  pl.BlockSpec((1, S, hp * HEAD_DIM), lambda b, g, j: (b, 0, g))] + [_ANY] * ns,
        out_specs=[pl.BlockSpec((1, S, hp * QK_PAD), lambda b, g, j: (b, 0, g)),
                   pl.BlockSpec((1, t, hp * QK_PAD), lambda b, g, j: (b, j, g)),
                   pl.BlockSpec((1, t, hp * HEAD_DIM), lambda b, g, j: (b, j, g))] + [_ANY] * ns,
        out_shape=[_sds((B, S, HEADS * QK_PAD), F32), _sds((B, S, HEADS * QK_PAD), F32), _sds((B, S, HEADS * HEAD_DIM), F32)]
                  + [_swapped_shape(s) for s in halves],
        scratch_shapes=[pltpu.VMEM((hp, S, 1), F32)] + _swap_scratch(ns),
        compiler_params=_params(("arbitrary", "arbitrary", "arbitrary")), name="attn_bwd",
    )(q, k, v, o, lse, do, *halves)


def _gdn_pre_fn(xq, xk, xv, wq, wk, wv, keeps):
    def conv_silu(x, w):
        acc = x * w[3]
        for s in (1, 2, 3):
            acc = acc + _shift_rows(x, keeps[s - 1], s) * w[3 - s]
        return _silu(acc)

    def l2(x):
        return x * lax.rsqrt(jnp.sum(x * x, axis=-1, keepdims=True) + EPS)

    return l2(conv_silu(xq, wq)) * (HEAD_DIM ** -0.5), l2(conv_silu(xk, wk)), conv_silu(xv, wv)


def _gdn_pre_specs(S):
    x_specs = [pl.BlockSpec((1, S, HEAD_DIM), lambda h, b, g=g: (b, 0, g * HEADS + h)) for g in range(3)]
    w_specs = [pl.BlockSpec((CONV_TAPS, HEAD_DIM), lambda h, b, g=g: (0, g * HEADS + h)) for g in range(3)]
    out_spec = pl.BlockSpec((1, S, HEAD_DIM), lambda h, b: (b, 0, h))
    return x_specs, w_specs, out_spec


def _row_keeps(S):
    t = lax.broadcasted_iota(jnp.int32, (S, HEAD_DIM), 0)
    return [(t >= s).astype(F32) for s in (1, 2, 3)]


def _gdn_pre_fwd(gqkv, conv_w):
    B, S, _ = gqkv.shape
    x_specs, w_specs, out_spec = _gdn_pre_specs(S)

    def body(xq_ref, xk_ref, xv_ref, wq_ref, wk_ref, wv_ref, q_ref, k_ref, v_ref):
        taps = [[w[i:i + 1, :] for i in range(CONV_TAPS)] for w in (wq_ref, wk_ref, wv_ref)]
        q, k, v = _gdn_pre_fn(xq_ref[0], xk_ref[0], xv_ref[0], *taps, _row_keeps(S))
        q_ref[0], k_ref[0], v_ref[0] = q, k, v

    return pl.pallas_call(
        body, grid=(HEADS, B), in_specs=x_specs + w_specs, out_specs=[out_spec] * 3,
        out_shape=[_sds((B, S, HEADS * HEAD_DIM), F32)] * 3,
        compiler_params=_params(("parallel", "parallel")), name="gdn_pre_fwd",
    )(gqkv, gqkv, gqkv, conv_w, conv_w, conv_w)


def _gdn_pre_bwd(gqkv, conv_w, dq, dk, dv):
    B, S, _ = gqkv.shape
    x_specs, w_specs, out_spec = _gdn_pre_specs(S)
    dw_spec = pl.BlockSpec((CONV_TAPS, HEAD_DIM), lambda h, b: (0, h))

    def body(xq_ref, xk_ref, xv_ref, wq_ref, wk_ref, wv_ref, dq_ref, dk_ref, dv_ref,
             dxq_ref, dxk_ref, dxv_ref, dwq_ref, dwk_ref, dwv_ref):
        @pl.when(pl.program_id(1) == 0)
        def _():
            for r in (dwq_ref, dwk_ref, dwv_ref):
                r[...] = jnp.zeros_like(r)

        taps = [[w[i:i + 1, :] for i in range(CONV_TAPS)] for w in (wq_ref, wk_ref, wv_ref)]
        keeps = _row_keeps(S)
        _, pull = jax.vjp(lambda *a: _gdn_pre_fn(*a, keeps), xq_ref[0], xk_ref[0], xv_ref[0], *taps)
        dxq, dxk, dxv, dwq, dwk, dwv = pull((dq_ref[0], dk_ref[0], dv_ref[0]))
        dxq_ref[0], dxk_ref[0], dxv_ref[0] = dxq, dxk, dxv
        for ref, dw in ((dwq_ref, dwq), (dwk_ref, dwk), (dwv_ref, dwv)):
            for i in range(CONV_TAPS):
                ref[i:i + 1, :] += dw[i]

    hw = HEADS * HEAD_DIM
    return pl.pallas_call(
        body, grid=(HEADS, B), in_specs=x_specs + w_specs + [out_spec] * 3,
        out_specs=[out_spec] * 3 + [dw_spec] * 3,
        out_shape=[_sds((B, S, hw), F32)] * 3 + [_sds((CONV_TAPS, hw), F32)] * 3,
        compiler_params=_params(("parallel", "arbitrary")), name="gdn_pre_bwd",
    )(gqkv, gqkv, gqkv, conv_w, conv_w, conv_w, dq, dk, dv)


def _chunk_masks():
    i = lax.broadcasted_iota(jnp.int32, (CHUNK, CHUNK), 0)
    j = lax.broadcasted_iota(jnp.int32, (CHUNK, CHUNK), 1)
    lower, after = (j <= i).astype(F32), (j > i).astype(F32)
    return {"le": lower, "le_gt": jnp.concatenate([lower, after], axis=0), "strict": (j < i).astype(F32)}


def _gdn_chunk_fn(groups, masks):
    lane = lax.broadcasted_iota(jnp.int32, (groups, 1, 128), 2)
    head = lax.broadcasted_iota(jnp.int32, (groups, 1, 128), 0) % HEADS
    pick_a, pick_b = (lane == head).astype(F32), (lane == head + HEADS).astype(F32)
    lower, lower_after, strict = (jnp.broadcast_to(masks[n], (groups,) + masks[n].shape) for n in ("le", "le_gt", "strict"))
    ones_row = jnp.ones((1, 1, HEAD_DIM), F32)

    def f(q, k, v, gab, a_row, dt_row, state):
        ga = jnp.sum(gab * pick_a, axis=2, keepdims=True)
        gb = jnp.sum(gab * pick_b, axis=2, keepdims=True)
        a_log = jnp.sum(a_row * pick_a, axis=2, keepdims=True)
        dt_bias = jnp.sum(dt_row * pick_a, axis=2, keepdims=True)
        beta = _sigmoid(gb)
        g = -jnp.exp(a_log) * _softplus(ga + dt_bias)
        g_wide = g * ones_row
        cum, rest = _row_halves(_hi_nn(lower_after, g_wide))
        total = jnp.sum(g_wide, axis=1, keepdims=True)
        diff = _hi_nn(lower, g * strict)
        decay = lower * jnp.exp(diff)
        e_cum = jnp.exp(cum)
        lmat = strict * (beta * _bf_nt(k, k) * decay)
        rhs = jnp.concatenate([v * beta, k * (beta * e_cum)], axis=2)
        rhs = rhs - _hi_nn(lmat, rhs)
        power = lmat
        for _ in range(5):
            power = _hi_nn(power, power)
            rhs = rhs + _hi_nn(power, rhs)
        u, w = _lane_halves(rhs)
        attn = _bf_nt(q, k) * decay
        v_new = u - _bf_nn(w, state)
        o = _bf_nn(q * e_cum, state) + _bf_nn(attn, v_new)
        new_state = state * jnp.exp(total) + _bf_tn(k * jnp.exp(rest), v_new)
        return o, new_state

    return f


def _gdn_chunk_fwd(q, k, v, gab, scal, shards):
    B, S, W = q.shape
    N = S // CHUNK
    ns = len(shards)

    def body(*refs):
        q_ref, k_ref, v_ref, gab_ref, sc_ref = refs[:5]
        src_refs = refs[5:5 + ns]
        o_ref, st_ref = refs[5 + ns:7 + ns]
        dst_refs = refs[7 + ns:7 + 2 * ns]
        state_ref, send_sems, recv_sems, local_sems = refs[7 + 2 * ns:]
        n = pl.program_id(0)

        @pl.when(n == 0)
        def _():
            for start in _gather_copies(src_refs, dst_refs, send_sems, recv_sems, local_sems)[0]:
                start()
            state_ref[...] = jnp.zeros_like(state_ref)

        @pl.when(n == (2 * N) // 3)
        def _():
            for pass_on in _gather_copies(src_refs, dst_refs, send_sems, recv_sems, local_sems)[1]:
                pass_on()

        groups = [(b, h) for b in range(B) for h in range(HEADS)]
        gather = lambda ref: jnp.stack([ref[b, :, h * HEAD_DIM:(h + 1) * HEAD_DIM] for b, h in groups])
        state = state_ref[...]
        for i, (b, h) in enumerate(groups):
            st_ref[b, 0, h] = state[i]
        o, new_state = _gdn_chunk_fn(len(groups), _chunk_masks())(
            gather(q_ref), gather(k_ref), gather(v_ref), jnp.stack([gab_ref[b] for b, _ in groups]), sc_ref[0:1, :], sc_ref[1:2, :], state)
        for i, (b, h) in enumerate(groups):
            o_ref[b, :, h * HEAD_DIM:(h + 1) * HEAD_DIM] = o[i]
        state_ref[...] = new_state

        @pl.when(n == N - 1)
        def _():
            for wait in _gather_copies(src_refs, dst_refs, send_sems, recv_sems, local_sems)[2]:
                wait()

    seq = pl.BlockSpec((B, CHUNK, W), lambda n: (0, n, 0))
    return pl.pallas_call(
        body, grid=(N,),
        in_specs=[seq, seq, seq, pl.BlockSpec((B, CHUNK, GAB_W), lambda n: (0, n, 0)), _const_spec((8, 128))] + [_ANY] * ns,
        out_specs=[seq, pl.BlockSpec((B, 1, HEADS, HEAD_DIM, HEAD_DIM), lambda n: (0, n, 0, 0, 0))] + [_ANY] * ns,
        out_shape=[_sds((B, S, W), F32), _sds((B, N, HEADS, HEAD_DIM, HEAD_DIM), F32)] + [_sds((4,) + s.shape, s.dtype) for s in shards],
        scratch_shapes=[pltpu.VMEM((B * HEADS, HEAD_DIM, HEAD_DIM), F32)] + _gather_scratch(ns),
        compiler_params=_params(("arbitrary",)), name="gdn_chunk_fwd",
    )(q, k, v, gab, scal, *shards)


def _gdn_chunk_bwd(q, k, v, gab, scal, states, do, partials):
    B, S, W = q.shape
    N = S // CHUNK
    ns = len(partials)

    def body(*refs):
        q_ref, k_ref, v_ref, gab_ref, sc_ref, st_ref, do_ref = refs[:7]
        src_refs = refs[7:7 + ns]
        dq_ref, dk_ref, dv_ref, dgab_ref, dsc_ref = refs[7 + ns:12 + ns]
        dst_refs = refs[12 + ns:12 + 2 * ns]
        dstate_ref, send_sems, recv_sems, local_sems = refs[12 + 2 * ns:]
        n = pl.program_id(0)

        @pl.when(n == 0)
        def _():
            for start in _scatter_copies(src_refs, dst_refs, send_sems, recv_sems, local_sems)[0]:
                start()
            dstate_ref[...] = jnp.zeros_like(dstate_ref)
            dsc_ref[...] = jnp.zeros_like(dsc_ref)

        groups = [(b, h) for b in range(B) for h in range(HEADS)]
        gather = lambda ref: jnp.stack([ref[b, :, h * HEAD_DIM:(h + 1) * HEAD_DIM] for b, h in groups])
        _, pull = jax.vjp(_gdn_chunk_fn(len(groups), _chunk_masks()), gather(q_ref), gather(k_ref), gather(v_ref),
                          jnp.stack([gab_ref[b] for b, _ in groups]), sc_ref[0:1, :], sc_ref[1:2, :],
                          jnp.stack([st_ref[b, 0, h] for b, h in groups]))
        dq, dk, dv, dg, d_a, d_dt, dstate = pull((gather(do_ref), dstate_ref[...]))
        for i, (b, h) in enumerate(groups):
            lanes = slice(h * HEAD_DIM, (h + 1) * HEAD_DIM)
            dq_ref[b, :, lanes] = dq[i]
            dk_ref[b, :, lanes] = dk[i]
            dv_ref[b, :, lanes] = dv[i]
        for b in range(B):
            dgab_ref[b] = sum(dg[b * HEADS + h] for h in range(HEADS))
        dstate_ref[...] = dstate
        dsc_ref[0:1, :] += d_a
        dsc_ref[1:2, :] += d_dt

        @pl.when(n == N - 1)
        def _():
            for wait in _scatter_copies(src_refs, dst_refs, send_sems, recv_sems, local_sems)[1]:
                wait()

    seq = pl.BlockSpec((B, CHUNK, W), lambda n: (0, N - 1 - n, 0))
    gab_spec = pl.BlockSpec((B, CHUNK, GAB_W), lambda n: (0, N - 1 - n, 0))
    return pl.pallas_call(
        body, grid=(N,),
        in_specs=[seq, seq, seq, gab_spec, _const_spec((8, 128)),
                  pl.BlockSpec((B, 1, HEADS, HEAD_DIM, HEAD_DIM), lambda n: (0, N - 1 - n, 0, 0, 0)), seq] + [_ANY] * ns,
        out_specs=[seq, seq, seq, gab_spec, _const_spec((8, 128))] + [_ANY] * ns,
        out_shape=[_sds((B, S, W), F32)] * 3 + [_sds((B, S, GAB_W), F32), _sds((8, 128), F32)] + [_scattered_shape(p) for p in partials],
        scratch_shapes=[pltpu.VMEM((B * HEADS, HEAD_DIM, HEAD_DIM), F32)] + _scatter_scratch(ns),
        compiler_params=_params(("arbitrary",)), name="gdn_chunk_bwd",
    )(q, k, v, gab, scal, states, do, *partials)


def _mix_fn(ao, go, gz, w_mla, w_gdn):
    return tuple(_rms(ao[h], w_mla[h]) for h in range(HEADS)) + tuple(_rms(go[h], w_gdn) * _silu(gz[h]) for h in range(HEADS))


def _mix_operands(ao_ref, go_ref, gz_ref, nw_ref):
    blocks = lambda ref: [ref[:, h * HEAD_DIM:(h + 1) * HEAD_DIM] for h in range(HEADS)]
    return blocks(ao_ref), blocks(go_ref), blocks(gz_ref), [nw_ref[h:h + 1, :] for h in range(HEADS)], nw_ref[HEADS:HEADS + 1, :]


def _mix_fwd(ao, go, gz, nw, w_out, x2):
    T, D = x2.shape
    tm = min(TOKEN_TILE, T)
    MW = 2 * HEADS * HEAD_DIM

    def body(ao_ref, go_ref, gz_ref, nw_ref, w_ref, x_ref, mix_ref, h_ref):
        outs = _mix_fn(*_mix_operands(ao_ref, go_ref, gz_ref, nw_ref))
        for i, piece in enumerate(outs):
            mix_ref[:, i * HEAD_DIM:(i + 1) * HEAD_DIM] = piece.astype(BF16)
        h_ref[...] = x_ref[...] + jnp.dot(mix_ref[...], w_ref[...], preferred_element_type=F32)

    half = HEADS * HEAD_DIM
    return pl.pallas_call(
        body, grid=(T // tm,),
        in_specs=[_row_spec(tm, half), _row_spec(tm, half), _row_spec(tm, half), _const_spec((8, 128)), _const_spec((MW, D)),
                  _row_spec(tm, D)],
        out_specs=[_row_spec(tm, MW), _row_spec(tm, D)],
        out_shape=[_sds((T, MW), BF16), _sds((T, D), F32)],
        compiler_params=_params(("parallel",)), name="mix_fwd",
    )(ao, go, gz, nw, w_out, x2)


def _mix_bwd(ao, go, gz, nw, w_out, dh):
    T, D = dh.shape
    tm = min(TOKEN_TILE, T)
    MW = 2 * HEADS * HEAD_DIM
    half = HEADS * HEAD_DIM

    def body(ao_ref, go_ref, gz_ref, nw_ref, w_ref, dh_ref, dao_ref, dgo_ref, dgz_ref, dnw_ref):
        @pl.when(pl.program_id(0) == 0)
        def _():
            dnw_ref[...] = jnp.zeros_like(dnw_ref)

        d_mix = _dg(dh_ref[...].astype(BF16), w_ref[...], 1, 1, None)
        cts = tuple(d_mix[:, i * HEAD_DIM:(i + 1) * HEAD_DIM] for i in range(2 * HEADS))
        _, pull = jax.vjp(_mix_fn, *_mix_operands(ao_ref, go_ref, gz_ref, nw_ref))
        d_ao, d_go, d_gz, d_wm, d_wg = pull(cts)
        for h in range(HEADS):
            lanes = slice(h * HEAD_DIM, (h + 1) * HEAD_DIM)
            dao_ref[:, lanes] = d_ao[h]
            dgo_ref[:, lanes] = d_go[h]
            dgz_ref[:, lanes] = d_gz[h]
            dnw_ref[h:h + 1, :] += d_wm[h]
        dnw_ref[HEADS:HEADS + 1, :] += d_wg

    return pl.pallas_call(
        body, grid=(T // tm,),
        in_specs=[_row_spec(tm, half), _row_spec(tm, half), _row_spec(tm, half), _const_spec((8, 128)), _const_spec((MW, D)),
                  _row_spec(tm, D)],
        out_specs=[_row_spec(tm, half)] * 3 + [_const_spec((8, 128))],
        out_shape=[_sds((T, half), F32)] * 3 + [_sds((8, 128), F32)],
        compiler_params=_params(("arbitrary",)), name="mix_bwd",
    )(ao, go, gz, nw, w_out, dh)


def _up_spec(w_up, tf):
    per_shard = w_up.shape[2] // tf
    return pl.BlockSpec((None, w_up.shape[1], tf), lambda i, j: (j // per_shard, 0, j % per_shard))


def _mlp_fwd(h2, w_mn, w_up, w_down, target):
    T, D = h2.shape
    FF = w_down.shape[0]
    tm, tf = min(MLP_TOKEN_TILE, T), min(FF_TILE, w_up.shape[2])
    nf = FF // tf

    def body(h_ref, wn_ref, wu_ref, wd_ref, t_ref, hn_ref, dy_ref, sq_ref, acc_ref):
        j = pl.program_id(1)

        @pl.when(j == 0)
        def _():
            hn_ref[...] = _rms(h_ref[...], wn_ref[...]).astype(BF16)
            acc_ref[...] = jnp.zeros_like(acc_ref)

        up = jnp.dot(hn_ref[...], wu_ref[...], preferred_element_type=F32)
        act = jnp.square(jnp.maximum(up, 0.0)).astype(BF16)
        acc_ref[...] += jnp.dot(act, wd_ref[...], preferred_element_type=F32)

        @pl.when(j == nf - 1)
        def _():
            err = h_ref[...] + acc_ref[...] - t_ref[...]
            dy_ref[...] = err * (1.0 / D)
            sq_ref[...] = jnp.zeros_like(sq_ref) + jnp.sum(err * err)

    tok = lambda w: pl.BlockSpec((tm, w), lambda i, j: (i, 0))
    return pl.pallas_call(
        body, grid=(T // tm, nf),
        in_specs=[tok(D), _const_spec((1, D)), _up_spec(w_up, tf), pl.BlockSpec((tf, D), lambda i, j: (j, 0)), tok(D)],
        out_specs=[tok(D), tok(D), pl.BlockSpec((1, 8, 128), lambda i, j: (i, 0, 0))],
        out_shape=[_sds((T, D), BF16), _sds((T, D), F32), _sds((T // tm, 8, 128), F32)],
        scratch_shapes=[pltpu.VMEM((tm, D), F32)],
        compiler_params=_params(("parallel", "arbitrary")), name="mlp_fwd",
    )(h2, w_mn, w_up, w_down, target)


def _mlp_bwd(h2, w_mn, hn, w_up, w_down, dy):
    T, D = h2.shape
    FF = w_down.shape[0]
    tm, tf = min(MLP_TOKEN_TILE, T), min(FF_TILE, w_up.shape[2])
    nf = FF // tf

    def body(h_ref, wn_ref, hn_ref, wu_ref, wd_ref, dy_ref, dh_ref, act_ref, dup_ref, dwn_ref, acc_ref):
        i, j = pl.program_id(0), pl.program_id(1)

        @pl.when((i == 0) & (j == 0))
        def _():
            dwn_ref[...] = jnp.zeros_like(dwn_ref)

        @pl.when(j == 0)
        def _():
            acc_ref[...] = jnp.zeros_like(acc_ref)

        r = jnp.maximum(jnp.dot(hn_ref[...], wu_ref[...], preferred_element_type=F32), 0.0)
        act_ref[...] = (r * r).astype(BF16)
        d_act = _dg(dy_ref[...].astype(BF16), wd_ref[...], 1, 1, None)
        d_up = (d_act * (2.0 * r)).astype(BF16)
        dup_ref[...] = d_up
        acc_ref[...] += _dg(d_up, wu_ref[...], 1, 1, None)

        @pl.when(j == nf - 1)
        def _():
            _, pull = jax.vjp(_rms, h_ref[...], wn_ref[...])
            dh, dwn = pull(acc_ref[...])
            dh_ref[...] = dh + dy_ref[...]
            dwn_ref[...] += dwn

    tok = lambda w: pl.BlockSpec((tm, w), lambda i, j: (i, 0))
    ff = pl.BlockSpec((tm, tf), lambda i, j: (i, j))
    return pl.pallas_call(
        body, grid=(T // tm, nf),
        in_specs=[tok(D), _const_spec((1, D)), tok(D), _up_spec(w_up, tf), pl.BlockSpec((tf, D), lambda i, j: (j, 0)), tok(D)],
        out_specs=[tok(D), ff, ff, _const_spec((1, D))],
        out_shape=[_sds((T, D), F32), _sds((T, FF), BF16), _sds((T, FF), BF16), _sds((1, D), F32)],
        scratch_shapes=[pltpu.VMEM((tm, D), F32)],
        compiler_params=_params(("arbitrary", "arbitrary")), name="mlp_bwd",
    )(h2, w_mn, hn, w_up, w_down, dy)


def _rope_pad(a):
    z = jnp.zeros(a.shape[:-1] + (ROPE_HALF,), a.dtype)
    return jnp.concatenate([a[..., :ROPE_HALF], z, a[..., ROPE_HALF:], z], axis=-1)


def _rope_unpad(a):
    return jnp.concatenate([a[..., :ROPE_HALF], a[..., 2 * ROPE_HALF:3 * ROPE_HALF]], axis=-1)


_G0 = 2 * LORA + ROPE_DIM
W_IN_COLS = _G0 + GQKV_W + GZ_W + 2 * HEADS


def _widen_w_in_t(w_t):
    z = jnp.zeros((ROPE_HALF, w_t.shape[1]), w_t.dtype)
    pad = jnp.zeros((GAB_W - 2 * HEADS, w_t.shape[1]), w_t.dtype)
    return jnp.concatenate([w_t[:2 * LORA + ROPE_HALF], z, w_t[2 * LORA + ROPE_HALF:_G0], z, w_t[_G0:], pad], axis=0)


def _narrow_w_in_t(w_t):
    return jnp.concatenate([w_t[:2 * LORA + ROPE_HALF], w_t[2 * LORA + 2 * ROPE_HALF:2 * LORA + 3 * ROPE_HALF],
                            w_t[LAT_W:LAT_W + W_IN_COLS - _G0]], axis=0)


def _stack_mla(w_uq, w_ukv):
    uq = w_uq.reshape(LORA, HEADS, QK_DIM)
    ukv = w_ukv.reshape(LORA, HEADS, 2 * HEAD_DIM)
    parts = [uq[:, :, :HEAD_DIM], _rope_pad(uq[:, :, HEAD_DIM:]), ukv[:, :, :HEAD_DIM], ukv[:, :, HEAD_DIM:]]
    return jnp.concatenate([p.transpose(1, 0, 2) for p in parts], axis=0)


def _unstack_mla(w):
    p = [w[i * HEADS:(i + 1) * HEADS].transpose(1, 0, 2) for i in range(4)]
    uq = jnp.concatenate([p[0], _rope_unpad(p[1])], axis=-1).reshape(LORA, HEADS * QK_DIM)
    ukv = jnp.concatenate([p[2], p[3]], axis=-1).reshape(LORA, HEADS * 2 * HEAD_DIM)
    return uq, ukv


def _rows8(rows):
    a = jnp.concatenate(rows, axis=0)
    return jnp.pad(a, ((0, 8 - a.shape[0]), (0, 0)))


def _qk_norm_rows(q_norm_w, k_norm_w):
    return _rows8([q_norm_w[:, :HEAD_DIM], _rope_pad(q_norm_w[:, HEAD_DIM:]), k_norm_w[:, :HEAD_DIM], _rope_pad(k_norm_w[:, HEAD_DIM:])])


def _rope_rows():
    inv_freq = ROPE_THETA ** (-jnp.arange(ROPE_HALF, dtype=F32) / ROPE_HALF)
    z = jnp.zeros((ROPE_HALF,), F32)
    freq = jnp.concatenate([inv_freq, z, inv_freq, z])
    sign = jnp.concatenate([-jnp.ones((ROPE_HALF,), F32), z, jnp.ones((ROPE_HALF,), F32), z])
    return _rows8([freq[None], sign[None]])


def _round_up(n, m):
    return -(-n // m) * m


def _column_shards(a):
    return a.reshape(a.shape[0], 4, a.shape[1] // 4).transpose(1, 0, 2)


def _from_column_shards(a):
    return a.transpose(1, 0, 2).reshape(a.shape[1], 4 * a.shape[2])


def _pack_small(arrays):
    rows = [jnp.pad(a.reshape(-1), (0, _round_up(a.size, 128) - a.size)).reshape(-1, 128) for a in arrays]
    packed = jnp.concatenate(rows, axis=0)
    return jnp.pad(packed, ((0, _round_up(packed.shape[0], 8) - packed.shape[0]), (0, 0)))


def _unpack_small(packed, shapes):
    out, r = [], 0
    for s in shapes:
        n = math.prod(s)
        nr = _round_up(n, 128) // 128
        out.append(packed[r:r + nr].reshape(-1)[:n].reshape(s))
        r += nr
    return out


_ANY = pl.BlockSpec(memory_space=pl.ANY)
_OTHER_CHIPS = ((1, 0), (0, 1), (1, 1))


def _here():
    return lax.axis_index("x"), lax.axis_index("y"), lax.axis_index("c")


def _flip(v, bit):
    return 1 - v if bit else v


def _remote(src, dst, send_sems, recv_sems, k, to):
    return pltpu.make_async_remote_copy(src_ref=src, dst_ref=dst, send_sem=send_sems.at[k], recv_sem=recv_sems.at[k],
                                        device_id=to, device_id_type=MESH)


def _half_of(ref, k, shape):
    r, c = shape
    if (r // 2) % 16 == 0:
        return ref.at[pl.ds(pl.multiple_of(k * (r // 2), 16), r // 2)]
    if (c // 2) % 128 == 0:
        return ref.at[:, pl.ds(pl.multiple_of(k * (c // 2), 128), c // 2)]
    return None


def _gather_copies(srcs, dsts, send_sems, recv_sems, local_sems):
    x, y, c = _here()
    slot, sibling, n = 2 * x + y, (x, y, 1 - c), len(srcs)
    starts, passes, waits = [], [], []
    for i, (src, dst) in enumerate(zip(srcs, dsts)):
        own = pltpu.make_async_copy(src, dst.at[slot], local_sems.at[i])
        starts.append(own.start)
        waits.append(own.wait)
        halves = _half_of(src, c, src.shape) is not None
        for j, (fx, fy) in enumerate(_OTHER_CHIPS):
            cx, cy = _flip(x, fx), _flip(y, fy)
            there = dst.at[2 * cx + cy]
            if halves:
                push = _remote(_half_of(src, c, src.shape), _half_of(dst.at[slot], c, src.shape), send_sems, recv_sems, 3 * i + j, (cx, cy, c))
                landed, other = _half_of(there, c, src.shape), _half_of(there, 1 - c, src.shape)
                onward = _remote(landed, landed, send_sems, recv_sems, 3 * n + 3 * i + j, sibling)
                passes += [_remote(landed, landed, send_sems, recv_sems, 3 * i + j, (cx, cy, c)).wait_recv, onward.start]
                waits += [_remote(other, other, send_sems, recv_sems, 3 * n + 3 * i + j, sibling).wait_recv, onward.wait_send]
            else:
                push = _remote(src, dst.at[slot], send_sems, recv_sems, 3 * i + j, (cx, cy, c))
                waits.append(_remote(there, there, send_sems, recv_sems, 3 * i + j, (cx, cy, c)).wait_recv)
            starts.append(push.start)
            waits.append(push.wait_send)
    return starts, passes, waits


def _gather_scratch(n):
    return [pltpu.SemaphoreType.DMA((6 * n,)), pltpu.SemaphoreType.DMA((6 * n,)), pltpu.SemaphoreType.DMA((n,))]


def _all_gather(shards, name):
    ns = len(shards)

    def body(*refs):
        starts, passes, waits = _gather_copies(refs[:ns], refs[ns:2 * ns], *refs[2 * ns:])
        for call in starts + passes + waits:
            call()

    return pl.pallas_call(
        body, in_specs=[_ANY] * ns, out_specs=[_ANY] * ns, out_shape=[_sds((4,) + s.shape, s.dtype) for s in shards],
        scratch_shapes=_gather_scratch(ns), name=name,
    )(*shards)


def _by_lanes(shape):
    return (shape[-2] // 2) % 16 != 0


def _scattered_shape(p):
    r, c = p.shape[1:]
    return _sds((8, r, c // 2) if _by_lanes(p.shape) else (8, r // 2, c), p.dtype)


def _scatter_copies(srcs, dsts, send_sems, recv_sems, local_sems, whole=0):
    x, y, c = _here()
    me = 4 * x + 2 * y + c
    starts, waits = [], []
    for i, (src, dst) in enumerate(zip(srcs, dsts)):
        def piece(px, py, pc, src=src, entire=i >= len(srcs) - whole):
            if entire:
                return src
            if _by_lanes(src.shape):
                half = src.shape[2] // 2
                return src.at[2 * px + py, :, pl.ds(pl.multiple_of(pc * half, 128), half)]
            half = src.shape[1] // 2
            return src.at[2 * px + py, pl.ds(pl.multiple_of(pc * half, 16), half)]

        own = pltpu.make_async_copy(piece(x, y, c), dst.at[me], local_sems.at[i])
        starts.append(own.start)
        waits.append(own.wait)
        for k in range(1, 8):
            px, py, pc = _flip(x, k & 4), _flip(y, k & 2), _flip(c, k & 1)
            push = _remote(piece(px, py, pc), dst.at[me], send_sems, recv_sems, 7 * i + k - 1, (px, py, pc))
            landed = dst.at[4 * px + 2 * py + pc]
            starts.append(push.start)
            waits += [_remote(landed, landed, send_sems, recv_sems, 7 * i + k - 1, (px, py, pc)).wait_recv, push.wait_send]
    return starts, waits


def _scatter_scratch(n):
    return [pltpu.SemaphoreType.DMA((7 * n,)), pltpu.SemaphoreType.DMA((7 * n,)), pltpu.SemaphoreType.DMA((n,))]


def _scatter(partials, wholes, name):
    ns = len(partials) + len(wholes)

    def body(*refs):
        starts, waits = _scatter_copies(refs[:ns], refs[ns:2 * ns], *refs[2 * ns:], whole=len(wholes))
        for call in starts + waits:
            call()

    return pl.pallas_call(
        body, in_specs=[_ANY] * ns, out_specs=[_ANY] * ns,
        out_shape=[_scattered_shape(p) for p in partials] + [_sds((8,) + s.shape, s.dtype) for s in wholes],
        scratch_shapes=_scatter_scratch(ns), name=name,
    )(*partials, *wholes)


def _swapped_shape(half):
    r, c = half.shape
    return _sds((r, 2 * c) if _by_lanes((r, 2 * c)) else (2, r, c), half.dtype)


def _swap_copies(srcs, dsts, send_sems, recv_sems, local_sems):
    x, y, c = _here()
    sibling = (x, y, 1 - c)
    starts, waits = [], []
    for i, (src, dst) in enumerate(zip(srcs, dsts)):
        if len(dst.shape) == 2:
            lanes = src.shape[1]
            mine, other = (dst.at[:, pl.ds(pl.multiple_of(k * lanes, 128), lanes)] for k in (c, 1 - c))
        else:
            mine, other = dst.at[c], dst.at[1 - c]
        own = pltpu.make_async_copy(src, mine, local_sems.at[i])
        push = _remote(src, mine, send_sems, recv_sems, i, sibling)
        starts += [own.start, push.start]
        waits += [_remote(other, other, send_sems, recv_sems, i, sibling).wait_recv, push.wait_send, own.wait]
    return starts, waits


def _swap_scratch(n):
    return [pltpu.SemaphoreType.DMA((n,)), pltpu.SemaphoreType.DMA((n,)), pltpu.SemaphoreType.DMA((n,))]


def _exchange_halves(halves):
    ns = len(halves)

    def body(*refs):
        starts, waits = _swap_copies(refs[:ns], refs[ns:2 * ns], *refs[2 * ns:])
        for call in starts + waits:
            call()

    return pl.pallas_call(
        body, in_specs=[_ANY] * ns, out_specs=[_ANY] * ns, out_shape=[_swapped_shape(h) for h in halves],
        scratch_shapes=_swap_scratch(ns), name="exchange_halves",
    )(*halves)


def _row_tile(rows, row_bytes, budget):
    tr = rows
    while tr * row_bytes > budget and tr % 16 == 0:
        tr //= 2
    return tr


def _sum_slots(parts, name):
    _, rows, cols = parts.shape
    tr = _row_tile(rows, 8 * cols * 4, 2 * 1024 * 1024)

    def body(p_ref, o_ref):
        acc = p_ref[0].astype(F32)
        for d in range(1, 8):
            acc = acc + p_ref[d].astype(F32)
        o_ref[...] = acc

    return pl.pallas_call(
        body, grid=(rows // tr,), in_specs=[pl.BlockSpec((8, tr, cols), lambda i: (0, i, 0))],
        out_specs=pl.BlockSpec((tr, cols), lambda i: (i, 0)), out_shape=_sds((rows, cols), F32),
        compiler_params=_params(("parallel",)), name=name,
    )(parts)


def _adamw(w, g, m, v, name):
    rows, cols = w.shape[0], w.shape[-1]
    if w.ndim == 3:
        tr = max(d for d in range(1, rows + 1) if rows % d == 0 and d * 8 * cols * 4 * 14 <= VMEM_LIMIT // 2)
    else:
        tr = _row_tile(rows, 7 * cols * 4, 4 * 1024 * 1024)

    def body(w_ref, g_ref, m_ref, v_ref, d_ref, mo_ref, vo_ref):
        g = g_ref[...]
        m = ADAM_B1 * m_ref[...] + (1.0 - ADAM_B1) * g
        v = ADAM_B2 * v_ref[...] + (1.0 - ADAM_B2) * jnp.square(g)
        m_hat = m / (1.0 - ADAM_B1 ** ADAM_STEP)
        v_hat = v / (1.0 - ADAM_B2 ** ADAM_STEP)
        d_ref[...] = -ADAM_LR * (m_hat / (jnp.sqrt(v_hat) + ADAM_EPS) + ADAM_WD * w_ref[...])
        mo_ref[...] = m
        vo_ref[...] = v

    block = (tr,) + w.shape[1:]
    spec = pl.BlockSpec(block, lambda i: (i,) + (0,) * (len(block) - 1))
    return pl.pallas_call(
        body, grid=(rows // tr,), in_specs=[spec] * 4, out_specs=[spec] * 3, out_shape=[_sds(w.shape, F32)] * 3,
        compiler_params=_params(("parallel",)), name=name,
    )(w, g, m, v)


def kernel(x, positions, attn_norm_w, w_in, q_lat_norm_w, w_uq, kv_lat_norm_w, w_ukv, q_norm_w, k_norm_w, mla_out_norm_w, conv_w, a_log, dt_bias, gdn_norm_w, w_out, mlp_norm_w, w_up, w_down, loss_target, m_attn_norm_w, m_w_in, m_q_lat_norm_w, m_w_uq, m_kv_lat_norm_w, m_w_ukv, m_q_norm_w, m_k_norm_w, m_mla_out_norm_w, m_conv_w, m_a_log, m_dt_bias, m_gdn_norm_w, m_w_out, m_mlp_norm_w, m_w_up, m_w_down, v_attn_norm_w, v_w_in, v_q_lat_norm_w, v_w_uq, v_kv_lat_norm_w, v_w_ukv, v_q_norm_w, v_k_norm_w, v_mla_out_norm_w, v_conv_w, v_a_log, v_dt_bias, v_gdn_norm_w, v_w_out, v_mlp_norm_w, v_w_up, v_w_down):
    w = dict(zip(WEIGHTS, (attn_norm_w, w_in, q_lat_norm_w, w_uq, kv_lat_norm_w, w_ukv, q_norm_w, k_norm_w, mla_out_norm_w, conv_w,
                           a_log, dt_bias, gdn_norm_w, w_out, mlp_norm_w, w_up, w_down)))
    m = dict(zip(WEIGHTS, (m_attn_norm_w, m_w_in, m_q_lat_norm_w, m_w_uq, m_kv_lat_norm_w, m_w_ukv, m_q_norm_w, m_k_norm_w,
                           m_mla_out_norm_w, m_conv_w, m_a_log, m_dt_bias, m_gdn_norm_w, m_w_out, m_mlp_norm_w, m_w_up, m_w_down)))
    v = dict(zip(WEIGHTS, (v_attn_norm_w, v_w_in, v_q_lat_norm_w, v_w_uq, v_kv_lat_norm_w, v_w_ukv, v_q_norm_w, v_k_norm_w,
                           v_mla_out_norm_w, v_conv_w, v_a_log, v_dt_bias, v_gdn_norm_w, v_w_out, v_mlp_norm_w, v_w_up, v_w_down)))
    B, S, D = x.shape
    T = B * S
    x2, pos, target = x.reshape(T, D), positions.reshape(T, 1), loss_target.reshape(T, D)
    seq = lambda a: a.reshape(B, S, a.shape[-1])
    tok = lambda a: a.reshape(T, a.shape[-1])
    local = {n: w[n][0] for n in SHARDED}

    g_in, g_uq, g_ukv, g_conv = _all_gather([jnp.swapaxes(w_in, 1, 2)[0].astype(BF16), local["w_uq"].astype(BF16),
                                             local["w_ukv"].astype(BF16), local["conv_w"]], "gather_first_weights")
    w_in_p = _widen_w_in_t(g_in.reshape(-1, D))
    w_mla = _stack_mla(_from_column_shards(g_uq), _from_column_shards(g_ukv))
    conv_full = _from_column_shards(g_conv)
    ln_w = jnp.concatenate([q_lat_norm_w, kv_lat_norm_w], axis=0)
    qk_nw = _qk_norm_rows(q_norm_w, k_norm_w)
    rope_rows = _rope_rows()
    scal = _rows8([jnp.pad(a_log, ((0, 0), (0, 128 - HEADS))), jnp.pad(dt_bias, ((0, 0), (0, 128 - HEADS)))])
    mix_nw = _rows8([mla_out_norm_w[0], gdn_norm_w])

    xn, lat, gqkv, gz, gab = _in_proj_fwd(x2, attn_norm_w, w_in_p)
    q, k, v_att = _mla_pre_fwd(lat, pos, ln_w, w_mla, qk_nw, rope_rows)
    ao, lse, g_down = _attn_fwd(seq(q), seq(k), seq(v_att), [local["w_down"].astype(BF16)])
    gq, gk, gv = _gdn_pre_fwd(seq(gqkv), conv_full)
    go, states, g_out, w_up_b = _gdn_chunk_fwd(gq, gk, gv, seq(gab), scal, [local["w_out"].astype(BF16), local["w_up"].astype(BF16)])
    w_out_b = g_out.reshape(-1, D)
    w_down_b = g_down.reshape(-1, D)
    mix, h2 = _mix_fwd(tok(ao), tok(go), gz, mix_nw, w_out_b, x2)
    hn, dy, sq = _mlp_fwd(h2, mlp_norm_w, w_up_b, w_down_b, target)
    loss = lax.psum(jnp.sum(sq[:, 0, 0]) * (0.5 / D), ("x", "y", "c"))

    dh, act, d_up, d_mlp_nw = _mlp_bwd(h2, mlp_norm_w, hn, w_up_b, w_down_b, dy)
    p_down = _wgrad(act, dy, "wgrad_down").reshape(4, -1, D)
    p_up = _wgrad(hn, d_up, "wgrad_up", column_shards=4)
    d_ao, d_go, d_gz, d_mix_nw = _mix_bwd(tok(ao), tok(go), gz, mix_nw, w_out_b, dh)
    p_out = _wgrad(mix, dh, "wgrad_out").reshape(4, -1, D)
    d_gq, d_gk, d_gv, d_gab, d_scal, s_up, s_down, s_out = _gdn_chunk_bwd(gq, gk, gv, seq(gab), scal, states, seq(d_go),
                                                                          [p_up, p_down, p_out])
    dxq, dxk, dxv, dcq, dck, dcv = _gdn_pre_bwd(seq(gqkv), conv_full, d_gq, d_gk, d_gv)
    early = ("w_up", "w_down", "w_out")
    early_halves = [_sum_slots(s, "sum_" + n) for n, s in zip(early, (s_up, s_down, s_out))]
    dq, dk, dv, *early_grads = _attn_bwd(seq(q), seq(k), seq(v_att), ao, lse, seq(d_ao), early_halves)
    d_lat, d_ln, d_w_mla, d_qk_nw = _mla_pre_bwd(lat, pos, ln_w, w_mla, qk_nw, rope_rows, tok(dq), tok(dk), tok(dv))
    grad_x2, d_proj, d_attn_nw = _in_proj_bwd([d_lat, tok(dxq), tok(dxk), tok(dxv), d_gz, tok(d_gab)], w_in_p, x2, attn_norm_w, dh)
    p_in = _narrow_w_in_t(_wgrad(d_proj, xn, "wgrad_in")).reshape(4, -1, D)
    p_uq, p_ukv = (_column_shards(a).astype(BF16) for a in _unstack_mla(d_w_mla))
    small_partial = {
        "attn_norm_w": d_attn_nw, "q_lat_norm_w": d_ln[0:1], "kv_lat_norm_w": d_ln[1:2],
        "q_norm_w": jnp.concatenate([d_qk_nw[0:1], _rope_unpad(d_qk_nw[1:2])], axis=-1),
        "k_norm_w": jnp.concatenate([d_qk_nw[2:3], _rope_unpad(d_qk_nw[3:4])], axis=-1),
        "mla_out_norm_w": d_mix_nw[None, :HEADS], "a_log": d_scal[0:1, :HEADS], "dt_bias": d_scal[1:2, :HEADS],
        "gdn_norm_w": d_mix_nw[HEADS:HEADS + 1], "mlp_norm_w": d_mlp_nw,
    }
    conv_partial = jnp.concatenate([dcq, dck, dcv], axis=-1)
    s_in, s_uq, s_ukv, s_small = _scatter([p_in, p_uq, p_ukv], [_pack_small([small_partial[n] for n in SMALL] + [conv_partial])],
                                          "scatter_last_partials")

    late = ("w_in", "w_uq", "w_ukv")
    late_grads = _exchange_halves([_sum_slots(s, "sum_" + n) for n, s in zip(late, (s_in, s_uq, s_ukv))])
    names = early + late
    grad = {n: g.reshape(-1, g.shape[-1]) for n, g in zip(names, list(early_grads) + list(late_grads))}
    small_shapes = [w[n].shape for n in SMALL]
    *g_small, g_conv_all = _unpack_small(_sum_slots(s_small, "sum_small"), small_shapes + [conv_partial.shape])
    grad.update(zip(SMALL, g_small))
    conv_cols = local["conv_w"].shape[1]
    grad["conv_w"] = lax.dynamic_slice_in_dim(g_conv_all, (2 * lax.axis_index("x") + lax.axis_index("y")) * conv_cols, conv_cols, axis=1)

    delta, new_m, new_v = {}, {}, {}
    for n in names:
        if n == "w_in":
            stored = lambda a: jnp.transpose(a, (2, 0, 1))
            outs = _adamw(stored(w[n]), grad[n][:, None, :], stored(m[n]), stored(v[n]), "adamw_" + n)
            grad[n], delta[n], new_m[n], new_v[n] = (jnp.transpose(a, (1, 2, 0)) for a in (grad[n][:, None, :], *outs))
        else:
            delta[n], new_m[n], new_v[n] = _adamw(local[n], grad[n], m[n][0], v[n][0], "adamw_" + n)
    packed_names = SMALL + ("conv_w",)
    packed_shapes = small_shapes + [local["conv_w"].shape]
    take = lambda d: _pack_small([d[n][0] if n == "conv_w" and d[n].ndim == 3 else d[n] for n in packed_names])
    outs = _adamw(take(w), take(grad), take(m), take(v), "adamw_small")
    for d, packed in zip((delta, new_m, new_v), outs):
        d.update(zip(packed_names, _unpack_small(packed, packed_shapes)))

    def in_order(d):
        return [d[n].reshape(w[n].shape) for n in WEIGHTS]

    return (loss, grad_x2.reshape(B, S, D), *in_order(grad), *in_order(delta), *in_order(new_m), *in_order(new_v))
```

```python
import functools
import math

import jax
import jax.numpy as jnp
from jax import lax
from jax.experimental import pallas as pl
from jax.experimental.pallas import tpu as pltpu

F32 = jnp.float32
BF16 = jnp.bfloat16
MESH = pl.DeviceIdType.MESH

EPS = 1e-6
HEADS = 4
HEAD_DIM = 128
ROPE_DIM = 64
ROPE_HALF = 32
QK_DIM = 192
QK_PAD = 256
LORA = 256
CHUNK = 64
CONV_TAPS = 4
ROPE_THETA = 10000.0
ATTN_SCALE = QK_DIM ** -0.5

LAT_W = 640
GQKV_W = 3 * HEADS * HEAD_DIM
GZ_W = HEADS * HEAD_DIM
GAB_W = 128
PROJ_SPLITS = ((0, LAT_W), (LAT_W, LAT_W + GQKV_W), (LAT_W + GQKV_W, LAT_W + GQKV_W + GZ_W),
               (LAT_W + GQKV_W + GZ_W, LAT_W + GQKV_W + GZ_W + GAB_W))
PROJ_W = PROJ_SPLITS[-1][1]

ADAM_LR = 0.001
ADAM_B1 = 0.9
ADAM_B2 = 0.999
ADAM_EPS = 1e-08
ADAM_WD = 0.01
ADAM_STEP = 10

TOKEN_TILE = 512
MLP_TOKEN_TILE = 1024
FF_TILE = 512
ATTN_TILE = 512
ATTN_HEADS_PER_STEP = 2
WGRAD_OUT_BYTES = 8 * 1024 * 1024
VMEM_LIMIT = 48 * 1024 * 1024

SHARDED = ("w_in", "w_uq", "w_ukv", "conv_w", "w_out", "w_up", "w_down")
SMALL = ("attn_norm_w", "q_lat_norm_w", "kv_lat_norm_w", "q_norm_w", "k_norm_w", "mla_out_norm_w", "a_log", "dt_bias",
         "gdn_norm_w", "mlp_norm_w")
WEIGHTS = ("attn_norm_w", "w_in", "q_lat_norm_w", "w_uq", "kv_lat_norm_w", "w_ukv", "q_norm_w", "k_norm_w", "mla_out_norm_w",
           "conv_w", "a_log", "dt_bias", "gdn_norm_w", "w_out", "mlp_norm_w", "w_up", "w_down")


def _sds(shape, dtype):
    return jax.ShapeDtypeStruct(shape, dtype)


def _params(semantics):
    return pltpu.CompilerParams(dimension_semantics=semantics, vmem_limit_bytes=VMEM_LIMIT)


def _block(n):
    for b in (512, 256, 128):
        if n % b == 0:
            return b
    return n


def _dg(a, b, ca, cb, prec):
    lead = a.ndim - 2
    batch = (tuple(range(lead)),) * 2
    return lax.dot_general(a, b, (((ca + lead,), (cb + lead,)), batch), precision=prec, preferred_element_type=F32)


def _split_bf16(a):
    hi = a.astype(BF16)
    return hi, (a - hi.astype(F32)).astype(BF16)


def _dot_bf16(a, b, ca, cb):
    return _dg(a.astype(BF16), b.astype(BF16), ca, cb, None)


def _dot_bf16x3(a, b, ca, cb):
    a_hi, a_lo = _split_bf16(a)
    b_hi, b_lo = _split_bf16(b)
    lead = a.ndim - 2
    return _dg(jnp.concatenate([a_hi, a_hi, a_lo], axis=ca + lead), jnp.concatenate([b_hi, b_lo, b_hi], axis=cb + lead), ca, cb, None)


def _matmul_family(dot):
    def nn_raw(a, b):
        return dot(a, b, 1, 0)

    def nt_raw(a, b):
        return dot(a, b, 1, 1)

    def tn_raw(a, b):
        return dot(a, b, 0, 0)

    @jax.custom_vjp
    def nn(a, b):
        return nn_raw(a, b)

    nn.defvjp(lambda a, b: (nn_raw(a, b), (a, b)), lambda r, g: (nt_raw(g, r[1]), tn_raw(r[0], g)))

    @jax.custom_vjp
    def nt(a, b):
        return nt_raw(a, b)

    nt.defvjp(lambda a, b: (nt_raw(a, b), (a, b)), lambda r, g: (nn_raw(g, r[1]), tn_raw(g, r[0])))

    @jax.custom_vjp
    def tn(a, b):
        return tn_raw(a, b)

    tn.defvjp(lambda a, b: (tn_raw(a, b), (a, b)), lambda r, g: (nt_raw(r[1], g), nn_raw(r[0], g)))
    return nn, nt, tn


_bf_nn, _bf_nt, _bf_tn = _matmul_family(_dot_bf16)
_hi_nn, _hi_nt, _hi_tn = _matmul_family(_dot_bf16x3)


def _lower_powers(lmat):
    powers = []
    while 2 ** (len(powers) + 1) < lmat.shape[-1]:
        powers.append(_dot_bf16x3(powers[-1] if powers else lmat, powers[-1] if powers else lmat, 1, 0))
    return powers


@jax.custom_vjp
def _unit_lower_solve(lmat, rhs):
    return _unit_lower_solve_fwd(lmat, rhs)[0]


def _unit_lower_solve_fwd(lmat, rhs):
    powers = _lower_powers(lmat)
    x = rhs - _dot_bf16x3(lmat, rhs, 1, 0)
    for p in powers:
        x = x + _dot_bf16x3(p, x, 1, 0)
    return x, (lmat, powers, x)


def _unit_lower_solve_bwd(res, g):
    lmat, powers, x = res
    y = g - _dot_bf16x3(lmat, g, 0, 0)
    for p in powers:
        y = y + _dot_bf16x3(p, y, 0, 0)
    return -_dot_bf16x3(y, x, 1, 1), y


_unit_lower_solve.defvjp(_unit_lower_solve_fwd, _unit_lower_solve_bwd)


@jax.custom_vjp
def _lane_halves(x):
    n = x.shape[-1] // 2
    return x[..., :n], x[..., n:]


_lane_halves.defvjp(lambda x: (_lane_halves(x), None), lambda _, g: (jnp.concatenate(g, axis=-1),))


@jax.custom_vjp
def _row_halves(x):
    n = x.shape[-2] // 2
    return x[..., :n, :], x[..., n:, :]


_row_halves.defvjp(lambda x: (_row_halves(x), None), lambda _, g: (jnp.concatenate(g, axis=-2),))


@jax.custom_vjp
def _swap_halves(t):
    return pltpu.roll(t, 64, 1)


_swap_halves.defvjp(lambda t: (pltpu.roll(t, 64, 1), None), lambda _, g: (pltpu.roll(g, 64, 1),))


@functools.partial(jax.custom_vjp, nondiff_argnums=(2,))
def _shift_rows(x, keep, s):
    return pltpu.roll(x, s, 0) * keep


def _shift_rows_fwd(x, keep, s):
    return pltpu.roll(x, s, 0) * keep, keep


def _shift_rows_bwd(s, keep, g):
    return pltpu.roll(g * keep, keep.shape[0] - s, 0), jnp.zeros_like(keep)


_shift_rows.defvjp(_shift_rows_fwd, _shift_rows_bwd)


def _sigmoid(x):
    return 0.5 * jnp.tanh(0.5 * x) + 0.5


def _softplus(x):
    return jnp.maximum(x, 0.0) + jnp.log(1.0 + jnp.exp(jnp.minimum(x, -x)))


def _silu(x):
    return x * _sigmoid(x)


def _rms(x, w, n=None):
    n = x.shape[-1] if n is None else n
    r = lax.rsqrt(jnp.sum(x * x, axis=-1, keepdims=True) * (1.0 / n) + EPS)
    return x * r * w


def _rope(t, cos_f, sin_f):
    return t * cos_f + _swap_halves(t) * sin_f


def _rope_tables(pos_col, freq_row, sign_row):
    ang = pos_col.astype(F32) * freq_row
    return jnp.cos(ang), jnp.sin(ang) * sign_row


def _onehot_row(lane):
    return (lax.broadcasted_iota(jnp.int32, (1, 128), 1) == lane).astype(F32)


def _row_spec(tm, w):
    return pl.BlockSpec((tm, w), lambda i: (i, 0))


def _const_spec(shape):
    return pl.BlockSpec(shape, lambda *_: (0,) * len(shape))


def _in_proj_fwd(x2, w_an, w_in_p):
    T, D = x2.shape
    tm = min(TOKEN_TILE, T)

    def body(x_ref, wn_ref, w_ref, xn_ref, lat_ref, gqkv_ref, gz_ref, gab_ref):
        x = x_ref[...]
        r = lax.rsqrt(jnp.mean(x * x, axis=-1, keepdims=True) + EPS)
        xn = (x * r * wn_ref[...]).astype(BF16)
        xn_ref[...] = xn
        for ref, (a, b) in zip((lat_ref, gqkv_ref, gz_ref, gab_ref), PROJ_SPLITS):
            ref[...] = _dg(xn, w_ref[a:b, :], 1, 1, None)

    widths = [b - a for a, b in PROJ_SPLITS]
    return pl.pallas_call(
        body, grid=(T // tm,),
        in_specs=[_row_spec(tm, D), _const_spec((1, D)), _const_spec((PROJ_W, D))],
        out_specs=[_row_spec(tm, D)] + [_row_spec(tm, w) for w in widths],
        out_shape=[_sds((T, D), BF16)] + [_sds((T, w), F32) for w in widths],
        compiler_params=_params(("parallel",)), name="in_proj_fwd",
    )(x2, w_an, w_in_p)


def _in_proj_bwd(pieces, w_in_p, x2, w_an, dh):
    T, D = x2.shape
    tm = min(TOKEN_TILE, T)
    widths = [p.shape[1] for p in pieces]
    starts = [sum(widths[:i]) for i in range(len(widths))]
    assert sum(widths) == PROJ_W

    def body(*refs):
        piece_refs = refs[:len(pieces)]
        w_ref, x_ref, wn_ref, dh_ref, dx_ref, dp_ref, dwn_ref = refs[len(pieces):]

        @pl.when(pl.program_id(0) == 0)
        def _():
            dwn_ref[...] = jnp.zeros_like(dwn_ref)

        dxn = jnp.zeros((tm, D), F32)
        for ref, a, width in zip(piece_refs, starts, widths):
            piece = ref[...].astype(BF16)
            dp_ref[:, a:a + width] = piece
            dxn += _dg(piece, w_ref[a:a + width, :], 1, 0, None)
        _, pull = jax.vjp(_rms, x_ref[...], wn_ref[...])
        dx, dwn = pull(dxn)
        dx_ref[...] = dx + dh_ref[...]
        dwn_ref[...] += dwn

    return pl.pallas_call(
        body, grid=(T // tm,),
        in_specs=[_row_spec(tm, w) for w in widths] + [_const_spec((PROJ_W, D)), _row_spec(tm, D), _const_spec((1, D)),
                                                       _row_spec(tm, D)],
        out_specs=[_row_spec(tm, D), _row_spec(tm, PROJ_W), _const_spec((1, D))],
        out_shape=[_sds((T, D), F32), _sds((T, PROJ_W), BF16), _sds((1, D), F32)],
        compiler_params=_params(("arbitrary",)), name="in_proj_bwd",
    )(*pieces, w_in_p, x2, w_an, dh)


def _wgrad(a, b, name, column_shards=1, out_dtype=BF16):
    T, k1 = a.shape
    k2 = b.shape[1]
    per_shard = k2 // column_shards
    tt = min(TOKEN_TILE, T)
    b1 = k1
    while b1 * k2 * 4 > WGRAD_OUT_BYTES and b1 % 256 == 0:
        b1 //= 2
    step = _block(per_shard)

    def body(a_ref, b_ref, o_ref, acc_ref):
        t = pl.program_id(1)

        @pl.when(t == 0)
        def _():
            acc_ref[...] = jnp.zeros_like(acc_ref)

        a_t = a_ref[...].astype(BF16).T
        for c0 in range(0, k2, step):
            part = jnp.dot(a_t, b_ref[:, c0:c0 + step].astype(BF16), preferred_element_type=F32)
            if column_shards == 1:
                acc_ref[:, c0:c0 + step] += part
            else:
                acc_ref[c0 // per_shard, :, c0 % per_shard:c0 % per_shard + step] += part

        @pl.when(t == T // tt - 1)
        def _():
            o_ref[...] = acc_ref[...].astype(o_ref.dtype)

    if column_shards == 1:
        block, out_spec, out_shape = (b1, k2), pl.BlockSpec((b1, k2), lambda i, t: (i, 0)), _sds((k1, k2), out_dtype)
    else:
        block = (column_shards, b1, per_shard)
        out_spec, out_shape = pl.BlockSpec(block, lambda i, t: (0, i, 0)), _sds((column_shards, k1, per_shard), out_dtype)
    return pl.pallas_call(
        body, grid=(k1 // b1, T // tt),
        in_specs=[pl.BlockSpec((tt, b1), lambda i, t: (t, i)), pl.BlockSpec((tt, k2), lambda i, t: (t, 0))],
        out_specs=out_spec, out_shape=out_shape, scratch_shapes=[pltpu.VMEM(block, F32)],
        compiler_params=_params(("parallel", "arbitrary")), name=name,
    )(a, b)


def _mla_pre_fn(q_lat, kv_lat, kpe, ln_q, ln_kv, w_list, qn_n, qn_p, kn_n, kn_p, cos_f, sin_f):
    qn = _rms(q_lat, ln_q)
    kvn = _rms(kv_lat, ln_kv)
    kp = _rope(_rms(kpe, kn_p, ROPE_DIM), cos_f, sin_f)
    outs = []
    for h in range(HEADS):
        outs.append(_rms(_bf_nn(qn, w_list[h]), qn_n))
        outs.append(_rope(_rms(_bf_nn(qn, w_list[HEADS + h]), qn_p, ROPE_DIM), cos_f, sin_f))
        outs.append(_rms(_bf_nn(kvn, w_list[2 * HEADS + h]), kn_n))
        outs.append(_bf_nn(kvn, w_list[3 * HEADS + h]))
    return tuple(outs) + (kp,)


def _mla_pre_operands(lat_ref, pos_ref, ln_ref, w_ref, nw_ref, rope_ref):
    cos_f, sin_f = _rope_tables(pos_ref[...], rope_ref[0:1, :], rope_ref[1:2, :])
    diff = (lat_ref[:, 0:LORA], lat_ref[:, LORA:2 * LORA], lat_ref[:, 2 * LORA:LAT_W], ln_ref[0:1, :], ln_ref[1:2, :],
            [w_ref[i].astype(F32) for i in range(4 * HEADS)], nw_ref[0:1, :], nw_ref[1:2, :], nw_ref[2:3, :], nw_ref[3:4, :])
    return diff, cos_f, sin_f


def _mla_pre_fwd(lat, pos, ln_w, w_mla, nw, rope_rows):
    T = lat.shape[0]
    tm = min(TOKEN_TILE, T)

    def body(lat_ref, pos_ref, ln_ref, w_ref, nw_ref, rope_ref, q_ref, k_ref, v_ref):
        diff, cos_f, sin_f = _mla_pre_operands(lat_ref, pos_ref, ln_ref, w_ref, nw_ref, rope_ref)
        outs = _mla_pre_fn(*diff, cos_f, sin_f)
        kp = outs[-1].astype(BF16)
        for h in range(HEADS):
            q_n, q_p, k_n, v = outs[4 * h:4 * h + 4]
            q_ref[:, h * QK_PAD:h * QK_PAD + HEAD_DIM] = q_n.astype(BF16)
            q_ref[:, h * QK_PAD + HEAD_DIM:(h + 1) * QK_PAD] = q_p.astype(BF16)
            k_ref[:, h * QK_PAD:h * QK_PAD + HEAD_DIM] = k_n.astype(BF16)
            k_ref[:, h * QK_PAD + HEAD_DIM:(h + 1) * QK_PAD] = kp
            v_ref[:, h * HEAD_DIM:(h + 1) * HEAD_DIM] = v.astype(BF16)

    return pl.pallas_call(
        body, grid=(T // tm,),
        in_specs=[_row_spec(tm, LAT_W), _row_spec(tm, 1), _const_spec((2, LORA)), _const_spec((4 * HEADS, LORA, 128)),
                  _const_spec((8, 128)), _const_spec((8, 128))],
        out_specs=[_row_spec(tm, HEADS * QK_PAD), _row_spec(tm, HEADS * QK_PAD), _row_spec(tm, HEADS * HEAD_DIM)],
        out_shape=[_sds((T, HEADS * QK_PAD), BF16), _sds((T, HEADS * QK_PAD), BF16), _sds((T, HEADS * HEAD_DIM), BF16)],
        compiler_params=_params(("parallel",)), name="mla_pre_fwd",
    )(lat, pos, ln_w, w_mla, nw, rope_rows)


def _mla_pre_bwd(lat, pos, ln_w, w_mla, nw, rope_rows, dq, dk, dv, halves):
    T = lat.shape[0]
    tm = min(TOKEN_TILE, T)
    ns = len(halves)

    def body(*refs):
        lat_ref, pos_ref, ln_ref, w_ref, nw_ref, rope_ref, dq_ref, dk_ref, dv_ref = refs[:9]
        src_refs = refs[9:9 + ns]
        dlat_ref, dln_ref, dw_ref, dnw_ref = refs[9 + ns:13 + ns]
        dst_refs = refs[13 + ns:13 + 2 * ns]
        sems = refs[13 + 2 * ns:]

        @pl.when(pl.program_id(0) == 0)
        def _():
            for start in _swap_copies(src_refs, dst_refs, *sems)[0]:
                start()
            dln_ref[...] = jnp.zeros_like(dln_ref)
            dw_ref[...] = jnp.zeros_like(dw_ref)
            dnw_ref[...] = jnp.zeros_like(dnw_ref)

        diff, cos_f, sin_f = _mla_pre_operands(lat_ref, pos_ref, ln_ref, w_ref, nw_ref, rope_ref)
        _, pull = jax.vjp(lambda *a: _mla_pre_fn(*a, cos_f, sin_f), *diff)
        cts = []
        d_kp = jnp.zeros((tm, 128), F32)
        for h in range(HEADS):
            cts.append(dq_ref[:, h * QK_PAD:h * QK_PAD + HEAD_DIM])
            cts.append(dq_ref[:, h * QK_PAD + HEAD_DIM:(h + 1) * QK_PAD])
            cts.append(dk_ref[:, h * QK_PAD:h * QK_PAD + HEAD_DIM])
            cts.append(dv_ref[:, h * HEAD_DIM:(h + 1) * HEAD_DIM])
            d_kp += dk_ref[:, h * QK_PAD + HEAD_DIM:(h + 1) * QK_PAD]
        d_ql, d_kvl, d_kpe, d_lnq, d_lnkv, d_w, d_qn_n, d_qn_p, d_kn_n, d_kn_p = pull(tuple(cts) + (d_kp,))
        dlat_ref[:, 0:LORA] = d_ql
        dlat_ref[:, LORA:2 * LORA] = d_kvl
        dlat_ref[:, 2 * LORA:LAT_W] = d_kpe
        dln_ref[0:1, :] += d_lnq
        dln_ref[1:2, :] += d_lnkv
        for i in range(4 * HEADS):
            dw_ref[i] += d_w[i]
        for i, d in enumerate((d_qn_n, d_qn_p, d_kn_n, d_kn_p)):
            dnw_ref[i:i + 1, :] += d

        @pl.when(pl.program_id(0) == T // tm - 1)
        def _():
            for wait in _swap_copies(src_refs, dst_refs, *sems)[1]:
                wait()

    return pl.pallas_call(
        body, grid=(T // tm,),
        in_specs=[_row_spec(tm, LAT_W), _row_spec(tm, 1), _const_spec((2, LORA)), _const_spec((4 * HEADS, LORA, 128)),
                  _const_spec((8, 128)), _const_spec((8, 128)),
                  _row_spec(tm, HEADS * QK_PAD), _row_spec(tm, HEADS * QK_PAD), _row_spec(tm, HEADS * HEAD_DIM)] + [_ANY] * ns,
        out_specs=[_row_spec(tm, LAT_W), _const_spec((2, LORA)), _const_spec((4 * HEADS, LORA, 128)), _const_spec((8, 128))]
                  + [_ANY] * ns,
        out_shape=[_sds((T, LAT_W), F32), _sds((2, LORA), F32), _sds((4 * HEADS, LORA, 128), F32), _sds((8, 128), F32)]
                  + [_swapped_shape(h) for h in halves],
        scratch_shapes=_swap_scratch(ns),
        compiler_params=_params(("arbitrary",)), name="mla_pre_bwd",
    )(lat, pos, ln_w, w_mla, nw, rope_rows, dq, dk, dv, *halves)


def _causal_mask(i, j, tq, tk):
    row = i * tq + lax.broadcasted_iota(jnp.int32, (tq, tk), 0)
    col = j * tk + lax.broadcasted_iota(jnp.int32, (tq, tk), 1)
    return col <= row


def _attn_fwd(q, k, v, shards):
    B, S, _ = q.shape
    t = min(ATTN_TILE, S)
    nq = S // t
    ns = len(shards)

    hp = ATTN_HEADS_PER_STEP
    qk = lambda h: slice(h * QK_PAD, (h + 1) * QK_PAD)
    vd = lambda h: slice(h * HEAD_DIM, (h + 1) * HEAD_DIM)

    def body(*refs):
        q_ref, k_ref, v_ref = refs[:3]
        src_refs = refs[3:3 + ns]
        o_ref, lse_ref = refs[3 + ns:5 + ns]
        dst_refs = refs[5 + ns:5 + 2 * ns]
        sems = refs[5 + 2 * ns:]
        b, g, i = pl.program_id(0), pl.program_id(1), pl.program_id(2)
        qb = [q_ref[0, :, qk(h)] for h in range(hp)]

        step_no = (b * (HEADS // hp) + g) * nq + i
        for phase, at in enumerate((0, (3 * B * (HEADS // hp) * nq) // 4)):
            @pl.when(step_no == at)
            def _(phase=phase):
                for call in _gather_copies(src_refs, dst_refs, *sems)[phase]:
                    call()

        def step(j, carry, diagonal):
            rows = pl.ds(pl.multiple_of(j * t, t), t)
            s = [_dg(qb[h], k_ref[0, rows, qk(h)], 1, 1, None) * ATTN_SCALE for h in range(hp)]
            if diagonal:
                keep = _causal_mask(0, 0, t, t)
                s = [jnp.where(keep, x, -1e30) for x in s]
            m_new = [jnp.maximum(carry[h][0], jnp.max(s[h], axis=-1, keepdims=True)) for h in range(hp)]
            p = [jnp.exp(s[h] - m_new[h]) for h in range(hp)]
            alpha = [jnp.exp(carry[h][0] - m_new[h]) for h in range(hp)]
            l = [alpha[h] * carry[h][1] + jnp.sum(p[h], axis=-1, keepdims=True) for h in range(hp)]
            pv = [jnp.dot(p[h].astype(BF16), v_ref[0, rows, vd(h)], preferred_element_type=F32) for h in range(hp)]
            return tuple((m_new[h], l[h], alpha[h] * carry[h][2] + pv[h]) for h in range(hp))

        init = tuple((jnp.full((t, 1), -1e30, F32), jnp.zeros((t, 1), F32), jnp.zeros((t, HEAD_DIM), F32)) for _ in range(hp))
        below = lax.fori_loop(0, i, lambda j, carry: step(j, carry, False), init)
        for h, (m, l, acc) in enumerate(step(i, below, True)):
            o_ref[0, :, vd(h)] = acc / l
            lse_ref[0, h] = m + jnp.log(l)

        @pl.when((b == B - 1) & (g == HEADS // hp - 1) & (i == nq - 1))
        def _():
            for wait in _gather_copies(src_refs, dst_refs, *sems)[2]:
                wait()

    return pl.pallas_call(
        body, grid=(B, HEADS // hp, nq),
        in_specs=[pl.BlockSpec((1, t, hp * QK_PAD), lambda b, g, i: (b, i, g)),
                  pl.BlockSpec((1, S, hp * QK_PAD), lambda b, g, i: (b, 0, g)),
                  pl.BlockSpec((1, S, hp * HEAD_DIM), lambda b, g, i: (b, 0, g))] + [_ANY] * ns,
        out_specs=[pl.BlockSpec((1, t, hp * HEAD_DIM), lambda b, g, i: (b, i, g)),
                   pl.BlockSpec((1, hp, t, 1), lambda b, g, i: (b, g, i, 0))] + [_ANY] * ns,
        out_shape=[_sds((B, S, HEADS * HEAD_DIM), F32), _sds((B, HEADS, S, 1), F32)] + [_sds((4,) + s.shape, s.dtype) for s in shards],
        scratch_shapes=_gather_scratch(ns),
        compiler_params=_params(("arbitrary", "arbitrary", "arbitrary")), name="attn_fwd",
    )(q, k, v, *shards)


def _attn_bwd(q, k, v, o, lse, do, partials):
    B, S, _ = q.shape
    t = min(ATTN_TILE, S)
    nq = S // t
    ns = len(partials)

    hp = ATTN_HEADS_PER_STEP
    qk = lambda h: slice(h * QK_PAD, (h + 1) * QK_PAD)
    vd = lambda h: slice(h * HEAD_DIM, (h + 1) * HEAD_DIM)
    heads = range(hp)

    def body(*refs):
        q_ref, k_ref, v_ref, o_ref, lse_ref, do_ref = refs[:6]
        src_refs = refs[6:6 + ns]
        dq_ref, dk_ref, dv_ref = refs[6 + ns:9 + ns]
        dst_refs = refs[9 + ns:9 + 2 * ns]
        dsum_ref, send_sems, recv_sems, local_sems = refs[9 + 2 * ns:]
        b, g, j = pl.program_id(0), pl.program_id(1), pl.program_id(2)

        @pl.when((b == 0) & (g == 0) & (j == 0))
        def _():
            for start in _scatter_copies(src_refs, dst_refs, send_sems, recv_sems, local_sems)[0]:
                start()

        @pl.when(j == 0)
        def _():
            dq_ref[...] = jnp.zeros_like(dq_ref)
            for h in heads:
                dsum_ref[h] = jnp.sum(do_ref[0, :, vd(h)] * o_ref[0, :, vd(h)], axis=-1, keepdims=True)

        kb = [k_ref[0, :, qk(h)] for h in heads]
        vb = [v_ref[0, :, vd(h)] for h in heads]

        def step(i, carry, diagonal):
            rows = pl.ds(pl.multiple_of(i * t, t), t)
            qb = [q_ref[0, rows, qk(h)] for h in heads]
            dob = [do_ref[0, rows, vd(h)].astype(BF16) for h in heads]
            s = [_dg(qb[h], kb[h], 1, 1, None) * ATTN_SCALE for h in heads]
            p = [jnp.exp(s[h] - lse_ref[0, h, rows, :]) for h in heads]
            if diagonal:
                keep = _causal_mask(0, 0, t, t)
                p = [jnp.where(keep, x, 0.0) for x in p]
            dp = [_dg(dob[h], vb[h], 1, 1, None) for h in heads]
            dv = [carry[h][1] + _dg(p[h].astype(BF16), dob[h], 0, 0, None) for h in heads]
            ds = [(p[h] * (dp[h] - dsum_ref[h, rows, :]) * ATTN_SCALE).astype(BF16) for h in heads]
            for h in heads:
                dq_ref[0, rows, qk(h)] += jnp.dot(ds[h], kb[h], preferred_element_type=F32)
            return tuple((carry[h][0] + _dg(ds[h], qb[h], 0, 0, None), dv[h]) for h in heads)

        zeros = tuple((jnp.zeros((t, QK_PAD), F32), jnp.zeros((t, HEAD_DIM), F32)) for _ in heads)
        on_diagonal = step(j, zeros, True)
        done = lax.fori_loop(j + 1, nq, lambda i, carry: step(i, carry, False), on_diagonal)
        for h, (dk, dv) in enumerate(done):
            dk_ref[0, :, qk(h)] = dk
            dv_ref[0, :, vd(h)] = dv

        @pl.when((b == B - 1) & (g == HEADS // hp - 1) & (j == nq - 1))
        def _():
            for wait in _scatter_copies(src_refs, dst_refs, send_sems, recv_sems, local_sems)[1]:
                wait()

    return pl.pallas_call(
        body, grid=(B, HEADS // hp, nq),
        in_specs=[pl.BlockSpec((1, S, hp * QK_PAD), lambda b, g, j: (b, 0, g)),
                  pl.BlockSpec((1, t, hp * QK_PAD), lambda b, g, j: (b, j, g)),
                  pl.BlockSpec((1, t, hp * HEAD_DIM), lambda b, g, j: (b, j, g)),
                  pl.BlockSpec((1, S, hp * HEAD_DIM), lambda b, g, j: (b, 0, g)),
                  pl.BlockSpec((1, hp, S, 1), lambda b, g, j: (b, g, 0, 0)),
                  pl.BlockSpec((1, S, hp * HEAD_DIM), lambda b, g, j: (b, 0, g))] + [_ANY] * ns,
        out_specs=[pl.BlockSpec((1, S, hp * QK_PAD), lambda b, g, j: (b, 0, g)),
                   pl.BlockSpec((1, t, hp * QK_PAD), lambda b, g, j: (b, j, g)),
                   pl.BlockSpec((1, t, hp * HEAD_DIM), lambda b, g, j: (b, j, g))] + [_ANY] * ns,
        out_shape=[_sds((B, S, HEADS * QK_PAD), F32), _sds((B, S, HEADS * QK_PAD), F32), _sds((B, S, HEADS * HEAD_DIM), F32)]
                  + [_scattered_shape(p) for p in partials],
        scratch_shapes=[pltpu.VMEM((hp, S, 1), F32)] + _scatter_scratch(ns),
        compiler_params=_params(("arbitrary", "arbitrary", "arbitrary")), name="attn_bwd",
    )(q, k, v, o, lse, do, *partials)


def _gdn_pre_fn(xq, xk, xv, wq, wk, wv, keeps):
    def conv_silu(x, w):
        acc = x * w[3]
        for s in (1, 2, 3):
            acc = acc + _shift_rows(x, keeps[s - 1], s) * w[3 - s]
        return _silu(acc)

    def l2(x):
        return x * lax.rsqrt(jnp.sum(x * x, axis=-1, keepdims=True) + EPS)

    return l2(conv_silu(xq, wq)) * (HEAD_DIM ** -0.5), l2(conv_silu(xk, wk)), conv_silu(xv, wv)


def _gdn_pre_specs(S):
    x_specs = [pl.BlockSpec((1, S, HEAD_DIM), lambda h, b, g=g: (b, 0, g * HEADS + h)) for g in range(3)]
    w_specs = [pl.BlockSpec((CONV_TAPS, HEAD_DIM), lambda h, b, g=g: (0, g * HEADS + h)) for g in range(3)]
    out_spec = pl.BlockSpec((1, S, HEAD_DIM), lambda h, b: (b, 0, h))
    return x_specs, w_specs, out_spec


def _row_keeps(S):
    t = lax.broadcasted_iota(jnp.int32, (S, HEAD_DIM), 0)
    return [(t >= s).astype(F32) for s in (1, 2, 3)]


def _gdn_pre_fwd(gqkv, conv_w):
    B, S, _ = gqkv.shape
    x_specs, w_specs, out_spec = _gdn_pre_specs(S)

    def body(xq_ref, xk_ref, xv_ref, wq_ref, wk_ref, wv_ref, q_ref, k_ref, v_ref):
        taps = [[w[i:i + 1, :] for i in range(CONV_TAPS)] for w in (wq_ref, wk_ref, wv_ref)]
        q, k, v = _gdn_pre_fn(xq_ref[0], xk_ref[0], xv_ref[0], *taps, _row_keeps(S))
        q_ref[0], k_ref[0], v_ref[0] = q, k, v

    return pl.pallas_call(
        body, grid=(HEADS, B), in_specs=x_specs + w_specs, out_specs=[out_spec] * 3,
        out_shape=[_sds((B, S, HEADS * HEAD_DIM), F32)] * 3,
        compiler_params=_params(("parallel", "parallel")), name="gdn_pre_fwd",
    )(gqkv, gqkv, gqkv, conv_w, conv_w, conv_w)


def _gdn_pre_bwd(gqkv, conv_w, dq, dk, dv):
    B, S, _ = gqkv.shape
    x_specs, w_specs, out_spec = _gdn_pre_specs(S)
    dw_spec = pl.BlockSpec((CONV_TAPS, HEAD_DIM), lambda h, b: (0, h))

    def body(xq_ref, xk_ref, xv_ref, wq_ref, wk_ref, wv_ref, dq_ref, dk_ref, dv_ref,
             dxq_ref, dxk_ref, dxv_ref, dwq_ref, dwk_ref, dwv_ref):
        @pl.when(pl.program_id(1) == 0)
        def _():
            for r in (dwq_ref, dwk_ref, dwv_ref):
                r[...] = jnp.zeros_like(r)

        taps = [[w[i:i + 1, :] for i in range(CONV_TAPS)] for w in (wq_ref, wk_ref, wv_ref)]
        keeps = _row_keeps(S)
        _, pull = jax.vjp(lambda *a: _gdn_pre_fn(*a, keeps), xq_ref[0], xk_ref[0], xv_ref[0], *taps)
        dxq, dxk, dxv, dwq, dwk, dwv = pull((dq_ref[0], dk_ref[0], dv_ref[0]))
        dxq_ref[0], dxk_ref[0], dxv_ref[0] = dxq, dxk, dxv
        for ref, dw in ((dwq_ref, dwq), (dwk_ref, dwk), (dwv_ref, dwv)):
            for i in range(CONV_TAPS):
                ref[i:i + 1, :] += dw[i]

    hw = HEADS * HEAD_DIM
    return pl.pallas_call(
        body, grid=(HEADS, B), in_specs=x_specs + w_specs + [out_spec] * 3,
        out_specs=[out_spec] * 3 + [dw_spec] * 3,
        out_shape=[_sds((B, S, hw), F32)] * 3 + [_sds((CONV_TAPS, hw), F32)] * 3,
        compiler_params=_params(("parallel", "arbitrary")), name="gdn_pre_bwd",
    )(gqkv, gqkv, gqkv, conv_w, conv_w, conv_w, dq, dk, dv)


def _chunk_masks():
    i = lax.broadcasted_iota(jnp.int32, (CHUNK, CHUNK), 0)
    j = lax.broadcasted_iota(jnp.int32, (CHUNK, CHUNK), 1)
    lower, after = (j <= i).astype(F32), (j > i).astype(F32)
    return {"le": lower, "le_gt": jnp.concatenate([lower, after], axis=0), "strict": (j < i).astype(F32)}


def _gdn_chunk_fn(groups, masks):
    lane = lax.broadcasted_iota(jnp.int32, (groups, 1, 128), 2)
    head = lax.broadcasted_iota(jnp.int32, (groups, 1, 128), 0) % HEADS
    pick_a, pick_b = (lane == head).astype(F32), (lane == head + HEADS).astype(F32)
    lower, lower_after, strict = (jnp.broadcast_to(masks[n], (groups,) + masks[n].shape) for n in ("le", "le_gt", "strict"))
    ones_row = jnp.ones((1, 1, HEAD_DIM), F32)

    def f(q, k, v, gab, a_row, dt_row, state):
        ga = jnp.sum(gab * pick_a, axis=2, keepdims=True)
        gb = jnp.sum(gab * pick_b, axis=2, keepdims=True)
        a_log = jnp.sum(a_row * pick_a, axis=2, keepdims=True)
        dt_bias = jnp.sum(dt_row * pick_a, axis=2, keepdims=True)
        beta = _sigmoid(gb)
        g = -jnp.exp(a_log) * _softplus(ga + dt_bias)
        g_wide = g * ones_row
        cum, rest = _row_halves(_hi_nn(lower_after, g_wide))
        total = jnp.sum(g_wide, axis=1, keepdims=True)
        diff = _hi_nn(lower, g * strict)
        decay = lower * jnp.exp(diff)
        e_cum = jnp.exp(cum)
        lmat = strict * (beta * _bf_nt(k, k) * decay)
        u, w = _lane_halves(_unit_lower_solve(lmat, jnp.concatenate([v * beta, k * (beta * e_cum)], axis=2)))
        attn = _bf_nt(q, k) * decay
        v_new = u - _bf_nn(w, state)
        o = _bf_nn(q * e_cum, state) + _bf_nn(attn, v_new)
        new_state = state * jnp.exp(total) + _bf_tn(k * jnp.exp(rest), v_new)
        return o, new_state

    return f


def _gdn_chunk_fwd(q, k, v, gab, scal, shards):
    B, S, W = q.shape
    N = S // CHUNK
    ns = len(shards)

    def body(*refs):
        q_ref, k_ref, v_ref, gab_ref, sc_ref = refs[:5]
        src_refs = refs[5:5 + ns]
        o_ref, st_ref = refs[5 + ns:7 + ns]
        dst_refs = refs[7 + ns:7 + 2 * ns]
        state_ref, send_sems, recv_sems, local_sems = refs[7 + 2 * ns:]
        n = pl.program_id(0)

        @pl.when(n == 0)
        def _():
            for start in _gather_copies(src_refs, dst_refs, send_sems, recv_sems, local_sems)[0]:
                start()
            state_ref[...] = jnp.zeros_like(state_ref)

        @pl.when(n == (2 * N) // 3)
        def _():
            for pass_on in _gather_copies(src_refs, dst_refs, send_sems, recv_sems, local_sems)[1]:
                pass_on()

        groups = [(b, h) for b in range(B) for h in range(HEADS)]
        gather = lambda ref: jnp.stack([ref[b, :, h * HEAD_DIM:(h + 1) * HEAD_DIM] for b, h in groups])
        state = state_ref[...]
        for i, (b, h) in enumerate(groups):
            st_ref[b, 0, h] = state[i]
        o, new_state = _gdn_chunk_fn(len(groups), _chunk_masks())(
            gather(q_ref), gather(k_ref), gather(v_ref), jnp.stack([gab_ref[b] for b, _ in groups]), sc_ref[0:1, :], sc_ref[1:2, :], state)
        for i, (b, h) in enumerate(groups):
            o_ref[b, :, h * HEAD_DIM:(h + 1) * HEAD_DIM] = o[i]
        state_ref[...] = new_state

        @pl.when(n == N - 1)
        def _():
            for wait in _gather_copies(src_refs, dst_refs, send_sems, recv_sems, local_sems)[2]:
                wait()

    seq = pl.BlockSpec((B, CHUNK, W), lambda n: (0, n, 0))
    return pl.pallas_call(
        body, grid=(N,),
        in_specs=[seq, seq, seq, pl.BlockSpec((B, CHUNK, GAB_W), lambda n: (0, n, 0)), _const_spec((8, 128))] + [_ANY] * ns,
        out_specs=[seq, pl.BlockSpec((B, 1, HEADS, HEAD_DIM, HEAD_DIM), lambda n: (0, n, 0, 0, 0))] + [_ANY] * ns,
        out_shape=[_sds((B, S, W), F32), _sds((B, N, HEADS, HEAD_DIM, HEAD_DIM), F32)] + [_sds((4,) + s.shape, s.dtype) for s in shards],
        scratch_shapes=[pltpu.VMEM((B * HEADS, HEAD_DIM, HEAD_DIM), F32)] + _gather_scratch(ns),
        compiler_params=_params(("arbitrary",)), name="gdn_chunk_fwd",
    )(q, k, v, gab, scal, *shards)


def _gdn_chunk_bwd(q, k, v, gab, scal, states, do, partials):
    B, S, W = q.shape
    N = S // CHUNK
    ns = len(partials)

    def body(*refs):
        q_ref, k_ref, v_ref, gab_ref, sc_ref, st_ref, do_ref = refs[:7]
        src_refs = refs[7:7 + ns]
        dq_ref, dk_ref, dv_ref, dgab_ref, dsc_ref = refs[7 + ns:12 + ns]
        dst_refs = refs[12 + ns:12 + 2 * ns]
        dstate_ref, send_sems, recv_sems, local_sems = refs[12 + 2 * ns:]
        n = pl.program_id(0)

        @pl.when(n == 0)
        def _():
            for start in _scatter_copies(src_refs, dst_refs, send_sems, recv_sems, local_sems)[0]:
                start()
            dstate_ref[...] = jnp.zeros_like(dstate_ref)
            dsc_ref[...] = jnp.zeros_like(dsc_ref)

        groups = [(b, h) for b in range(B) for h in range(HEADS)]
        gather = lambda ref: jnp.stack([ref[b, :, h * HEAD_DIM:(h + 1) * HEAD_DIM] for b, h in groups])
        _, pull = jax.vjp(_gdn_chunk_fn(len(groups), _chunk_masks()), gather(q_ref), gather(k_ref), gather(v_ref),
                          jnp.stack([gab_ref[b] for b, _ in groups]), sc_ref[0:1, :], sc_ref[1:2, :],
                          jnp.stack([st_ref[b, 0, h] for b, h in groups]))
        dq, dk, dv, dg, d_a, d_dt, dstate = pull((gather(do_ref), dstate_ref[...]))
        for i, (b, h) in enumerate(groups):
            lanes = slice(h * HEAD_DIM, (h + 1) * HEAD_DIM)
            dq_ref[b, :, lanes] = dq[i]
            dk_ref[b, :, lanes] = dk[i]
            dv_ref[b, :, lanes] = dv[i]
        for b in range(B):
            dgab_ref[b] = sum(dg[b * HEADS + h] for h in range(HEADS))
        dstate_ref[...] = dstate
        dsc_ref[0:1, :] += d_a
        dsc_ref[1:2, :] += d_dt

        @pl.when(n == N - 1)
        def _():
            for wait in _scatter_copies(src_refs, dst_refs, send_sems, recv_sems, local_sems)[1]:
                wait()

    seq = pl.BlockSpec((B, CHUNK, W), lambda n: (0, N - 1 - n, 0))
    gab_spec = pl.BlockSpec((B, CHUNK, GAB_W), lambda n: (0, N - 1 - n, 0))
    return pl.pallas_call(
        body, grid=(N,),
        in_specs=[seq, seq, seq, gab_spec, _const_spec((8, 128)),
                  pl.BlockSpec((B, 1, HEADS, HEAD_DIM, HEAD_DIM), lambda n: (0, N - 1 - n, 0, 0, 0)), seq] + [_ANY] * ns,
        out_specs=[seq, seq, seq, gab_spec, _const_spec((8, 128))] + [_ANY] * ns,
        out_shape=[_sds((B, S, W), F32)] * 3 + [_sds((B, S, GAB_W), F32), _sds((8, 128), F32)] + [_scattered_shape(p) for p in partials],
        scratch_shapes=[pltpu.VMEM((B * HEADS, HEAD_DIM, HEAD_DIM), F32)] + _scatter_scratch(ns),
        compiler_params=_params(("arbitrary",)), name="gdn_chunk_bwd",
    )(q, k, v, gab, scal, states, do, *partials)


def _mix_fn(ao, go, gz, w_mla, w_gdn):
    return tuple(_rms(ao[h], w_mla[h]) for h in range(HEADS)) + tuple(_rms(go[h], w_gdn) * _silu(gz[h]) for h in range(HEADS))


def _mix_operands(ao_ref, go_ref, gz_ref, nw_ref):
    blocks = lambda ref: [ref[:, h * HEAD_DIM:(h + 1) * HEAD_DIM] for h in range(HEADS)]
    return blocks(ao_ref), blocks(go_ref), blocks(gz_ref), [nw_ref[h:h + 1, :] for h in range(HEADS)], nw_ref[HEADS:HEADS + 1, :]


def _mix_fwd(ao, go, gz, nw, w_out, x2):
    T, D = x2.shape
    tm = min(TOKEN_TILE, T)
    MW = 2 * HEADS * HEAD_DIM

    def body(ao_ref, go_ref, gz_ref, nw_ref, w_ref, x_ref, mix_ref, h_ref):
        outs = _mix_fn(*_mix_operands(ao_ref, go_ref, gz_ref, nw_ref))
        for i, piece in enumerate(outs):
            mix_ref[:, i * HEAD_DIM:(i + 1) * HEAD_DIM] = piece.astype(BF16)
        h_ref[...] = x_ref[...] + jnp.dot(mix_ref[...], w_ref[...], preferred_element_type=F32)

    half = HEADS * HEAD_DIM
    return pl.pallas_call(
        body, grid=(T // tm,),
        in_specs=[_row_spec(tm, half), _row_spec(tm, half), _row_spec(tm, half), _const_spec((8, 128)), _const_spec((MW, D)),
                  _row_spec(tm, D)],
        out_specs=[_row_spec(tm, MW), _row_spec(tm, D)],
        out_shape=[_sds((T, MW), BF16), _sds((T, D), F32)],
        compiler_params=_params(("parallel",)), name="mix_fwd",
    )(ao, go, gz, nw, w_out, x2)


def _mix_bwd(ao, go, gz, nw, w_out, dh):
    T, D = dh.shape
    tm = min(TOKEN_TILE, T)
    MW = 2 * HEADS * HEAD_DIM
    half = HEADS * HEAD_DIM

    def body(ao_ref, go_ref, gz_ref, nw_ref, w_ref, dh_ref, dao_ref, dgo_ref, dgz_ref, dnw_ref):
        @pl.when(pl.program_id(0) == 0)
        def _():
            dnw_ref[...] = jnp.zeros_like(dnw_ref)

        d_mix = _dg(dh_ref[...].astype(BF16), w_ref[...], 1, 1, None)
        cts = tuple(d_mix[:, i * HEAD_DIM:(i + 1) * HEAD_DIM] for i in range(2 * HEADS))
        _, pull = jax.vjp(_mix_fn, *_mix_operands(ao_ref, go_ref, gz_ref, nw_ref))
        d_ao, d_go, d_gz, d_wm, d_wg = pull(cts)
        for h in range(HEADS):
            lanes = slice(h * HEAD_DIM, (h + 1) * HEAD_DIM)
            dao_ref[:, lanes] = d_ao[h]
            dgo_ref[:, lanes] = d_go[h]
            dgz_ref[:, lanes] = d_gz[h]
            dnw_ref[h:h + 1, :] += d_wm[h]
        dnw_ref[HEADS:HEADS + 1, :] += d_wg

    return pl.pallas_call(
        body, grid=(T // tm,),
        in_specs=[_row_spec(tm, half), _row_spec(tm, half), _row_spec(tm, half), _const_spec((8, 128)), _const_spec((MW, D)),
                  _row_spec(tm, D)],
        out_specs=[_row_spec(tm, half)] * 3 + [_const_spec((8, 128))],
        out_shape=[_sds((T, half), F32)] * 3 + [_sds((8, 128), F32)],
        compiler_params=_params(("arbitrary",)), name="mix_bwd",
    )(ao, go, gz, nw, w_out, dh)


def _up_spec(w_up, tf):
    per_shard = w_up.shape[2] // tf
    return pl.BlockSpec((None, w_up.shape[1], tf), lambda i, j: (j // per_shard, 0, j % per_shard))


def _mlp_fwd(h2, w_mn, w_up, w_down, target):
    T, D = h2.shape
    FF = w_down.shape[0]
    tm, tf = min(MLP_TOKEN_TILE, T), min(FF_TILE, w_up.shape[2])
    nf = FF // tf

    def body(h_ref, wn_ref, wu_ref, wd_ref, t_ref, hn_ref, dy_ref, sq_ref, acc_ref):
        j = pl.program_id(1)

        @pl.when(j == 0)
        def _():
            hn_ref[...] = _rms(h_ref[...], wn_ref[...]).astype(BF16)
            acc_ref[...] = jnp.zeros_like(acc_ref)

        up = jnp.dot(hn_ref[...], wu_ref[...], preferred_element_type=F32)
        act = jnp.square(jnp.maximum(up, 0.0)).astype(BF16)
        acc_ref[...] += jnp.dot(act, wd_ref[...], preferred_element_type=F32)

        @pl.when(j == nf - 1)
        def _():
            err = h_ref[...] + acc_ref[...] - t_ref[...]
            dy_ref[...] = err * (1.0 / D)
            sq_ref[...] = jnp.zeros_like(sq_ref) + jnp.sum(err * err)

    tok = lambda w: pl.BlockSpec((tm, w), lambda i, j: (i, 0))
    return pl.pallas_call(
        body, grid=(T // tm, nf),
        in_specs=[tok(D), _const_spec((1, D)), _up_spec(w_up, tf), pl.BlockSpec((tf, D), lambda i, j: (j, 0)), tok(D)],
        out_specs=[tok(D), tok(D), pl.BlockSpec((1, 8, 128), lambda i, j: (i, 0, 0))],
        out_shape=[_sds((T, D), BF16), _sds((T, D), F32), _sds((T // tm, 8, 128), F32)],
        scratch_shapes=[pltpu.VMEM((tm, D), F32)],
        compiler_params=_params(("parallel", "arbitrary")), name="mlp_fwd",
    )(h2, w_mn, w_up, w_down, target)


def _mlp_bwd(h2, w_mn, hn, w_up, w_down, dy):
    T, D = h2.shape
    FF = w_down.shape[0]
    tm, tf = min(MLP_TOKEN_TILE, T), min(FF_TILE, w_up.shape[2])
    nf = FF // tf

    def body(h_ref, wn_ref, hn_ref, wu_ref, wd_ref, dy_ref, dh_ref, act_ref, dup_ref, dwn_ref, acc_ref):
        i, j = pl.program_id(0), pl.program_id(1)

        @pl.when((i == 0) & (j == 0))
        def _():
            dwn_ref[...] = jnp.zeros_like(dwn_ref)

        @pl.when(j == 0)
        def _():
            acc_ref[...] = jnp.zeros_like(acc_ref)

        r = jnp.maximum(jnp.dot(hn_ref[...], wu_ref[...], preferred_element_type=F32), 0.0)
        act_ref[...] = (r * r).astype(BF16)
        d_act = _dg(dy_ref[...].astype(BF16), wd_ref[...], 1, 1, None)
        d_up = (d_act * (2.0 * r)).astype(BF16)
        dup_ref[...] = d_up
        acc_ref[...] += _dg(d_up, wu_ref[...], 1, 1, None)

        @pl.when(j == nf - 1)
        def _():
            _, pull = jax.vjp(_rms, h_ref[...], wn_ref[...])
            dh, dwn = pull(acc_ref[...])
            dh_ref[...] = dh + dy_ref[...]
            dwn_ref[...] += dwn

    tok = lambda w: pl.BlockSpec((tm, w), lambda i, j: (i, 0))
    ff = pl.BlockSpec((tm, tf), lambda i, j: (i, j))
    return pl.pallas_call(
        body, grid=(T // tm, nf),
        in_specs=[tok(D), _const_spec((1, D)), tok(D), _up_spec(w_up, tf), pl.BlockSpec((tf, D), lambda i, j: (j, 0)), tok(D)],
        out_specs=[tok(D), ff, ff, _const_spec((1, D))],
        out_shape=[_sds((T, D), F32), _sds((T, FF), BF16), _sds((T, FF), BF16), _sds((1, D), F32)],
        scratch_shapes=[pltpu.VMEM((tm, D), F32)],
        compiler_params=_params(("arbitrary", "arbitrary")), name="mlp_bwd",
    )(h2, w_mn, hn, w_up, w_down, dy)


def _rope_pad(a):
    z = jnp.zeros(a.shape[:-1] + (ROPE_HALF,), a.dtype)
    return jnp.concatenate([a[..., :ROPE_HALF], z, a[..., ROPE_HALF:], z], axis=-1)


def _rope_unpad(a):
    return jnp.concatenate([a[..., :ROPE_HALF], a[..., 2 * ROPE_HALF:3 * ROPE_HALF]], axis=-1)


_G0 = 2 * LORA + ROPE_DIM
W_IN_COLS = _G0 + GQKV_W + GZ_W + 2 * HEADS


def _widen_w_in_t(w_t):
    z = jnp.zeros((ROPE_HALF, w_t.shape[1]), w_t.dtype)
    pad = jnp.zeros((GAB_W - 2 * HEADS, w_t.shape[1]), w_t.dtype)
    return jnp.concatenate([w_t[:2 * LORA + ROPE_HALF], z, w_t[2 * LORA + ROPE_HALF:_G0], z, w_t[_G0:], pad], axis=0)


def _narrow_w_in_t(w_t):
    return jnp.concatenate([w_t[:2 * LORA + ROPE_HALF], w_t[2 * LORA + 2 * ROPE_HALF:2 * LORA + 3 * ROPE_HALF],
                            w_t[LAT_W:LAT_W + W_IN_COLS - _G0]], axis=0)


def _stack_mla(w_uq, w_ukv):
    uq = w_uq.reshape(LORA, HEADS, QK_DIM)
    ukv = w_ukv.reshape(LORA, HEADS, 2 * HEAD_DIM)
    parts = [uq[:, :, :HEAD_DIM], _rope_pad(uq[:, :, HEAD_DIM:]), ukv[:, :, :HEAD_DIM], ukv[:, :, HEAD_DIM:]]
    return jnp.concatenate([p.transpose(1, 0, 2) for p in parts], axis=0)


def _unstack_mla(w):
    p = [w[i * HEADS:(i + 1) * HEADS].transpose(1, 0, 2) for i in range(4)]
    uq = jnp.concatenate([p[0], _rope_unpad(p[1])], axis=-1).reshape(LORA, HEADS * QK_DIM)
    ukv = jnp.concatenate([p[2], p[3]], axis=-1).reshape(LORA, HEADS * 2 * HEAD_DIM)
    return uq, ukv


def _rows8(rows):
    a = jnp.concatenate(rows, axis=0)
    return jnp.pad(a, ((0, 8 - a.shape[0]), (0, 0)))


def _qk_norm_rows(q_norm_w, k_norm_w):
    return _rows8([q_norm_w[:, :HEAD_DIM], _rope_pad(q_norm_w[:, HEAD_DIM:]), k_norm_w[:, :HEAD_DIM], _rope_pad(k_norm_w[:, HEAD_DIM:])])


def _rope_rows():
    inv_freq = ROPE_THETA ** (-jnp.arange(ROPE_HALF, dtype=F32) / ROPE_HALF)
    z = jnp.zeros((ROPE_HALF,), F32)
    freq = jnp.concatenate([inv_freq, z, inv_freq, z])
    sign = jnp.concatenate([-jnp.ones((ROPE_HALF,), F32), z, jnp.ones((ROPE_HALF,), F32), z])
    return _rows8([freq[None], sign[None]])


def _round_up(n, m):
    return -(-n // m) * m


def _column_shards(a):
    return a.reshape(a.shape[0], 4, a.shape[1] // 4).transpose(1, 0, 2)


def _from_column_shards(a):
    return a.transpose(1, 0, 2).reshape(a.shape[1], 4 * a.shape[2])


def _pack_small(arrays):
    rows = [jnp.pad(a.reshape(-1), (0, _round_up(a.size, 128) - a.size)).reshape(-1, 128) for a in arrays]
    packed = jnp.concatenate(rows, axis=0)
    return jnp.pad(packed, ((0, _round_up(packed.shape[0], 8) - packed.shape[0]), (0, 0)))


def _unpack_small(packed, shapes):
    out, r = [], 0
    for s in shapes:
        n = math.prod(s)
        nr = _round_up(n, 128) // 128
        out.append(packed[r:r + nr].reshape(-1)[:n].reshape(s))
        r += nr
    return out


_ANY = pl.BlockSpec(memory_space=pl.ANY)
_OTHER_CHIPS = ((1, 0), (0, 1), (1, 1))


def _here():
    return lax.axis_index("x"), lax.axis_index("y"), lax.axis_index("c")


def _flip(v, bit):
    return 1 - v if bit else v


def _remote(src, dst, send_sems, recv_sems, k, to):
    return pltpu.make_async_remote_copy(src_ref=src, dst_ref=dst, send_sem=send_sems.at[k], recv_sem=recv_sems.at[k],
                                        device_id=to, device_id_type=MESH)


def _half_of(ref, k, shape):
    r, c = shape
    if (r // 2) % 16 == 0:
        return ref.at[pl.ds(pl.multiple_of(k * (r // 2), 16), r // 2)]
    if (c // 2) % 128 == 0:
        return ref.at[:, pl.ds(pl.multiple_of(k * (c // 2), 128), c // 2)]
    return None


def _gather_copies(srcs, dsts, send_sems, recv_sems, local_sems):
    x, y, c = _here()
    slot, sibling, n = 2 * x + y, (x, y, 1 - c), len(srcs)
    starts, passes, waits = [], [], []
    for i, (src, dst) in enumerate(zip(srcs, dsts)):
        own = pltpu.make_async_copy(src, dst.at[slot], local_sems.at[i])
        starts.append(own.start)
        waits.append(own.wait)
        halves = _half_of(src, c, src.shape) is not None
        for j, (fx, fy) in enumerate(_OTHER_CHIPS):
            cx, cy = _flip(x, fx), _flip(y, fy)
            there = dst.at[2 * cx + cy]
            if halves:
                push = _remote(_half_of(src, c, src.shape), _half_of(dst.at[slot], c, src.shape), send_sems, recv_sems, 3 * i + j, (cx, cy, c))
                landed, other = _half_of(there, c, src.shape), _half_of(there, 1 - c, src.shape)
                onward = _remote(landed, landed, send_sems, recv_sems, 3 * n + 3 * i + j, sibling)
                passes += [_remote(landed, landed, send_sems, recv_sems, 3 * i + j, (cx, cy, c)).wait_recv, onward.start]
                waits += [_remote(other, other, send_sems, recv_sems, 3 * n + 3 * i + j, sibling).wait_recv, onward.wait_send]
            else:
                push = _remote(src, dst.at[slot], send_sems, recv_sems, 3 * i + j, (cx, cy, c))
                waits.append(_remote(there, there, send_sems, recv_sems, 3 * i + j, (cx, cy, c)).wait_recv)
            starts.append(push.start)
            waits.append(push.wait_send)
    return starts, passes, waits


def _gather_scratch(n):
    return [pltpu.SemaphoreType.DMA((6 * n,)), pltpu.SemaphoreType.DMA((6 * n,)), pltpu.SemaphoreType.DMA((n,))]


def _all_gather(shards, name):
    ns = len(shards)

    def body(*refs):
        starts, passes, waits = _gather_copies(refs[:ns], refs[ns:2 * ns], *refs[2 * ns:])
        for call in starts + passes + waits:
            call()

    return pl.pallas_call(
        body, in_specs=[_ANY] * ns, out_specs=[_ANY] * ns, out_shape=[_sds((4,) + s.shape, s.dtype) for s in shards],
        scratch_shapes=_gather_scratch(ns), name=name,
    )(*shards)


def _by_lanes(shape):
    return (shape[-2] // 2) % 16 != 0


def _scattered_shape(p):
    r, c = p.shape[1:]
    return _sds((8, r, c // 2) if _by_lanes(p.shape) else (8, r // 2, c), p.dtype)


def _scatter_copies(srcs, dsts, send_sems, recv_sems, local_sems, whole=0):
    x, y, c = _here()
    me = 4 * x + 2 * y + c
    starts, waits = [], []
    for i, (src, dst) in enumerate(zip(srcs, dsts)):
        def piece(px, py, pc, src=src, entire=i >= len(srcs) - whole):
            if entire:
                return src
            if _by_lanes(src.shape):
                half = src.shape[2] // 2
                return src.at[2 * px + py, :, pl.ds(pl.multiple_of(pc * half, 128), half)]
            half = src.shape[1] // 2
            return src.at[2 * px + py, pl.ds(pl.multiple_of(pc * half, 16), half)]

        own = pltpu.make_async_copy(piece(x, y, c), dst.at[me], local_sems.at[i])
        starts.append(own.start)
        waits.append(own.wait)
        for k in range(1, 8):
            px, py, pc = _flip(x, k & 4), _flip(y, k & 2), _flip(c, k & 1)
            push = _remote(piece(px, py, pc), dst.at[me], send_sems, recv_sems, 7 * i + k - 1, (px, py, pc))
            landed = dst.at[4 * px + 2 * py + pc]
            starts.append(push.start)
            waits += [_remote(landed, landed, send_sems, recv_sems, 7 * i + k - 1, (px, py, pc)).wait_recv, push.wait_send]
    return starts, waits


def _scatter_scratch(n):
    return [pltpu.SemaphoreType.DMA((7 * n,)), pltpu.SemaphoreType.DMA((7 * n,)), pltpu.SemaphoreType.DMA((n,))]


def _scatter(partials, wholes, name):
    ns = len(partials) + len(wholes)

    def body(*refs):
        starts, waits = _scatter_copies(refs[:ns], refs[ns:2 * ns], *refs[2 * ns:], whole=len(wholes))
        for call in starts + waits:
            call()

    return pl.pallas_call(
        body, in_specs=[_ANY] * ns, out_specs=[_ANY] * ns,
        out_shape=[_scattered_shape(p) for p in partials] + [_sds((8,) + s.shape, s.dtype) for s in wholes],
        scratch_shapes=_scatter_scratch(ns), name=name,
    )(*partials, *wholes)


def _swapped_shape(half):
    r, c = half.shape
    return _sds((r, 2 * c) if _by_lanes((r, 2 * c)) else (2, r, c), half.dtype)


def _swap_copies(srcs, dsts, send_sems, recv_sems, local_sems):
    x, y, c = _here()
    sibling = (x, y, 1 - c)
    starts, waits = [], []
    for i, (src, dst) in enumerate(zip(srcs, dsts)):
        if len(dst.shape) == 2:
            lanes = src.shape[1]
            mine, other = (dst.at[:, pl.ds(pl.multiple_of(k * lanes, 128), lanes)] for k in (c, 1 - c))
        else:
            mine, other = dst.at[c], dst.at[1 - c]
        own = pltpu.make_async_copy(src, mine, local_sems.at[i])
        push = _remote(src, mine, send_sems, recv_sems, i, sibling)
        starts += [own.start, push.start]
        waits += [_remote(other, other, send_sems, recv_sems, i, sibling).wait_recv, push.wait_send, own.wait]
    return starts, waits


def _swap_scratch(n):
    return [pltpu.SemaphoreType.DMA((n,)), pltpu.SemaphoreType.DMA((n,)), pltpu.SemaphoreType.DMA((n,))]


def _exchange_halves(halves):
    ns = len(halves)

    def body(*refs):
        starts, waits = _swap_copies(refs[:ns], refs[ns:2 * ns], *refs[2 * ns:])
        for call in starts + waits:
            call()

    return pl.pallas_call(
        body, in_specs=[_ANY] * ns, out_specs=[_ANY] * ns, out_shape=[_swapped_shape(h) for h in halves],
        scratch_shapes=_swap_scratch(ns), name="exchange_halves",
    )(*halves)


def _row_tile(rows, row_bytes, budget):
    tr = rows
    while tr * row_bytes > budget and tr % 16 == 0:
        tr //= 2
    return tr


def _sum_slots(parts, name):
    _, rows, cols = parts.shape
    tr = _row_tile(rows, 8 * cols * 4, 2 * 1024 * 1024)

    def body(p_ref, o_ref):
        acc = p_ref[0].astype(F32)
        for d in range(1, 8):
            acc = acc + p_ref[d].astype(F32)
        o_ref[...] = acc

    return pl.pallas_call(
        body, grid=(rows // tr,), in_specs=[pl.BlockSpec((8, tr, cols), lambda i: (0, i, 0))],
        out_specs=pl.BlockSpec((tr, cols), lambda i: (i, 0)), out_shape=_sds((rows, cols), F32),
        compiler_params=_params(("parallel",)), name=name,
    )(parts)


def _adamw(w, g, m, v, name):
    rows, cols = w.shape[0], w.shape[-1]
    if w.ndim == 3:
        tr = max(d for d in range(1, rows + 1) if rows % d == 0 and d * 8 * cols * 4 * 14 <= VMEM_LIMIT // 2)
    else:
        tr = _row_tile(rows, 7 * cols * 4, 4 * 1024 * 1024)

    def body(w_ref, g_ref, m_ref, v_ref, d_ref, mo_ref, vo_ref):
        g = g_ref[...]
        m = ADAM_B1 * m_ref[...] + (1.0 - ADAM_B1) * g
        v = ADAM_B2 * v_ref[...] + (1.0 - ADAM_B2) * jnp.square(g)
        m_hat = m / (1.0 - ADAM_B1 ** ADAM_STEP)
        v_hat = v / (1.0 - ADAM_B2 ** ADAM_STEP)
        d_ref[...] = -ADAM_LR * (m_hat / (jnp.sqrt(v_hat) + ADAM_EPS) + ADAM_WD * w_ref[...])
        mo_ref[...] = m
        vo_ref[...] = v

    block = (tr,) + w.shape[1:]
    spec = pl.BlockSpec(block, lambda i: (i,) + (0,) * (len(block) - 1))
    return pl.pallas_call(
        body, grid=(rows // tr,), in_specs=[spec] * 4, out_specs=[spec] * 3, out_shape=[_sds(w.shape, F32)] * 3,
        compiler_params=_params(("parallel",)), name=name,
    )(w, g, m, v)


def kernel(x, positions, attn_norm_w, w_in, q_lat_norm_w, w_uq, kv_lat_norm_w, w_ukv, q_norm_w, k_norm_w, mla_out_norm_w, conv_w, a_log, dt_bias, gdn_norm_w, w_out, mlp_norm_w, w_up, w_down, loss_target, m_attn_norm_w, m_w_in, m_q_lat_norm_w, m_w_uq, m_kv_lat_norm_w, m_w_ukv, m_q_norm_w, m_k_norm_w, m_mla_out_norm_w, m_conv_w, m_a_log, m_dt_bias, m_gdn_norm_w, m_w_out, m_mlp_norm_w, m_w_up, m_w_down, v_attn_norm_w, v_w_in, v_q_lat_norm_w, v_w_uq, v_kv_lat_norm_w, v_w_ukv, v_q_norm_w, v_k_norm_w, v_mla_out_norm_w, v_conv_w, v_a_log, v_dt_bias, v_gdn_norm_w, v_w_out, v_mlp_norm_w, v_w_up, v_w_down):
    w = dict(zip(WEIGHTS, (attn_norm_w, w_in, q_lat_norm_w, w_uq, kv_lat_norm_w, w_ukv, q_norm_w, k_norm_w, mla_out_norm_w, conv_w,
                           a_log, dt_bias, gdn_norm_w, w_out, mlp_norm_w, w_up, w_down)))
    m = dict(zip(WEIGHTS, (m_attn_norm_w, m_w_in, m_q_lat_norm_w, m_w_uq, m_kv_lat_norm_w, m_w_ukv, m_q_norm_w, m_k_norm_w,
                           m_mla_out_norm_w, m_conv_w, m_a_log, m_dt_bias, m_gdn_norm_w, m_w_out, m_mlp_norm_w, m_w_up, m_w_down)))
    v = dict(zip(WEIGHTS, (v_attn_norm_w, v_w_in, v_q_lat_norm_w, v_w_uq, v_kv_lat_norm_w, v_w_ukv, v_q_norm_w, v_k_norm_w,
                           v_mla_out_norm_w, v_conv_w, v_a_log, v_dt_bias, v_gdn_norm_w, v_w_out, v_mlp_norm_w, v_w_up, v_w_down)))
    B, S, D = x.shape
    T = B * S
    x2, pos, target = x.reshape(T, D), positions.reshape(T, 1), loss_target.reshape(T, D)
    seq = lambda a: a.reshape(B, S, a.shape[-1])
    tok = lambda a: a.reshape(T, a.shape[-1])
    local = {n: w[n][0] for n in SHARDED}

    g_in, g_uq, g_ukv, g_conv = _all_gather([jnp.swapaxes(w_in, 1, 2)[0].astype(BF16), local["w_uq"].astype(BF16),
                                             local["w_ukv"].astype(BF16), local["conv_w"]], "gather_first_weights")
    w_in_p = _widen_w_in_t(g_in.reshape(-1, D))
    w_mla = _stack_mla(_from_column_shards(g_uq), _from_column_shards(g_ukv))
    conv_full = _from_column_shards(g_conv)
    ln_w = jnp.concatenate([q_lat_norm_w, kv_lat_norm_w], axis=0)
    qk_nw = _qk_norm_rows(q_norm_w, k_norm_w)
    rope_rows = _rope_rows()
    scal = _rows8([jnp.pad(a_log, ((0, 0), (0, 128 - HEADS))), jnp.pad(dt_bias, ((0, 0), (0, 128 - HEADS)))])
    mix_nw = _rows8([mla_out_norm_w[0], gdn_norm_w])

    xn, lat, gqkv, gz, gab = _in_proj_fwd(x2, attn_norm_w, w_in_p)
    q, k, v_att = _mla_pre_fwd(lat, pos, ln_w, w_mla, qk_nw, rope_rows)
    ao, lse, g_down = _attn_fwd(seq(q), seq(k), seq(v_att), [local["w_down"].astype(BF16)])
    gq, gk, gv = _gdn_pre_fwd(seq(gqkv), conv_full)
    go, states, g_out, w_up_b = _gdn_chunk_fwd(gq, gk, gv, seq(gab), scal, [local["w_out"].astype(BF16), local["w_up"].astype(BF16)])
    w_out_b = g_out.reshape(-1, D)
    w_down_b = g_down.reshape(-1, D)
    mix, h2 = _mix_fwd(tok(ao), tok(go), gz, mix_nw, w_out_b, x2)
    hn, dy, sq = _mlp_fwd(h2, mlp_norm_w, w_up_b, w_down_b, target)
    loss = lax.psum(jnp.sum(sq[:, 0, 0]) * (0.5 / D), ("x", "y", "c"))

    dh, act, d_up, d_mlp_nw = _mlp_bwd(h2, mlp_norm_w, hn, w_up_b, w_down_b, dy)
    p_down = _wgrad(act, dy, "wgrad_down").reshape(4, -1, D)
    p_up = _wgrad(hn, d_up, "wgrad_up", column_shards=4)
    d_ao, d_go, d_gz, d_mix_nw = _mix_bwd(tok(ao), tok(go), gz, mix_nw, w_out_b, dh)
    p_out = _wgrad(mix, dh, "wgrad_out").reshape(4, -1, D)
    d_gq, d_gk, d_gv, d_gab, d_scal, s_up, s_out = _gdn_chunk_bwd(gq, gk, gv, seq(gab), scal, states, seq(d_go), [p_up, p_out])
    dxq, dxk, dxv, dcq, dck, dcv = _gdn_pre_bwd(seq(gqkv), conv_full, d_gq, d_gk, d_gv)
    dq, dk, dv, s_down = _attn_bwd(seq(q), seq(k), seq(v_att), ao, lse, seq(d_ao), [p_down])
    early = ("w_up", "w_down", "w_out")
    early_halves = [_sum_slots(s, "sum_" + n) for n, s in zip(early, (s_up, s_down, s_out))]
    d_lat, d_ln, d_w_mla, d_qk_nw, *early_grads = _mla_pre_bwd(lat, pos, ln_w, w_mla, qk_nw, rope_rows, tok(dq), tok(dk), tok(dv),
                                                               early_halves)
    grad_x2, d_proj, d_attn_nw = _in_proj_bwd([d_lat, tok(dxq), tok(dxk), tok(dxv), d_gz, tok(d_gab)], w_in_p, x2, attn_norm_w, dh)
    p_in = _narrow_w_in_t(_wgrad(d_proj, xn, "wgrad_in")).reshape(4, -1, D)
    p_uq, p_ukv = (_column_shards(a).astype(BF16) for a in _unstack_mla(d_w_mla))
    small_partial = {
        "attn_norm_w": d_attn_nw, "q_lat_norm_w": d_ln[0:1], "kv_lat_norm_w": d_ln[1:2],
        "q_norm_w": jnp.concatenate([d_qk_nw[0:1], _rope_unpad(d_qk_nw[1:2])], axis=-1),
        "k_norm_w": jnp.concatenate([d_qk_nw[2:3], _rope_unpad(d_qk_nw[3:4])], axis=-1),
        "mla_out_norm_w": d_mix_nw[None, :HEADS], "a_log": d_scal[0:1, :HEADS], "dt_bias": d_scal[1:2, :HEADS],
        "gdn_norm_w": d_mix_nw[HEADS:HEADS + 1], "mlp_norm_w": d_mlp_nw,
    }
    conv_partial = jnp.concatenate([dcq, dck, dcv], axis=-1)
    s_in, s_uq, s_ukv, s_small = _scatter([p_in, p_uq, p_ukv], [_pack_small([small_partial[n] for n in SMALL] + [conv_partial])],
                                          "scatter_last_partials")

    late = ("w_in", "w_uq", "w_ukv")
    late_grads = _exchange_halves([_sum_slots(s, "sum_" + n) for n, s in zip(late, (s_in, s_uq, s_ukv))])
    names = early + late
    grad = {n: g.reshape(-1, g.shape[-1]) for n, g in zip(names, list(early_grads) + list(late_grads))}
    small_shapes = [w[n].shape for n in SMALL]
    *g_small, g_conv_all = _unpack_small(_sum_slots(s_small, "sum_small"), small_shapes + [conv_partial.shape])
    grad.update(zip(SMALL, g_small))
    conv_cols = local["conv_w"].shape[1]
    grad["conv_w"] = lax.dynamic_slice_in_dim(g_conv_all, (2 * lax.axis_index("x") + lax.axis_index("y")) * conv_cols, conv_cols, axis=1)

    delta, new_m, new_v = {}, {}, {}
    for n in names:
        if n == "w_in":
            stored = lambda a: jnp.transpose(a, (2, 0, 1))
            outs = _adamw(stored(w[n]), grad[n][:, None, :], stored(m[n]), stored(v[n]), "adamw_" + n)
            grad[n], delta[n], new_m[n], new_v[n] = (jnp.transpose(a, (1, 2, 0)) for a in (grad[n][:, None, :], *outs))
        else:
            delta[n], new_m[n], new_v[n] = _adamw(local[n], grad[n], m[n][0], v[n][0], "adamw_" + n)
    packed_names = SMALL + ("conv_w",)
    packed_shapes = small_shapes + [local["conv_w"].shape]
    take = lambda d: _pack_small([d[n][0] if n == "conv_w" and d[n].ndim == 3 else d[n] for n in packed_names])
    outs = _adamw(take(w), take(grad), take(m), take(v), "adamw_small")
    for d, packed in zip((delta, new_m, new_v), outs):
        d.update(zip(packed_names, _unpack_small(packed, packed_shapes)))

    def in_order(d):
        return [d[n].reshape(w[n].shape) for n in WEIGHTS]

    return (loss, grad_x2.reshape(B, S, D), *in_order(grad), *in_order(delta), *in_order(new_m), *in_order(new_v))
```

```python
import functools
import math

import jax
import jax.numpy as jnp
from jax import lax
from jax.experimental import pallas as pl
from jax.experimental.pallas import tpu as pltpu

F32 = jnp.float32
BF16 = jnp.bfloat16
MESH = pl.DeviceIdType.MESH

EPS = 1e-6
HEADS = 4
HEAD_DIM = 128
ROPE_DIM = 64
ROPE_HALF = 32
QK_DIM = 192
QK_PAD = 256
LORA = 256
CHUNK = 64
CONV_TAPS = 4
ROPE_THETA = 10000.0
ATTN_SCALE = QK_DIM ** -0.5

LAT_W = 640
GQKV_W = 3 * HEADS * HEAD_DIM
GZ_W = HEADS * HEAD_DIM
GAB_W = 128
PROJ_SPLITS = ((0, LAT_W), (LAT_W, LAT_W + GQKV_W), (LAT_W + GQKV_W, LAT_W + GQKV_W + GZ_W),
               (LAT_W + GQKV_W + GZ_W, LAT_W + GQKV_W + GZ_W + GAB_W))
PROJ_W = PROJ_SPLITS[-1][1]

ADAM_LR = 0.001
ADAM_B1 = 0.9
ADAM_B2 = 0.999
ADAM_EPS = 1e-08
ADAM_WD = 0.01
ADAM_STEP = 10

TOKEN_TILE = 512
MLP_TOKEN_TILE = 1024
FF_TILE = 512
ATTN_TILE = 512
ATTN_HEADS_PER_STEP = 2
WGRAD_OUT_BYTES = 8 * 1024 * 1024
VMEM_LIMIT = 48 * 1024 * 1024

SHARDED = ("w_in", "w_uq", "w_ukv", "conv_w", "w_out", "w_up", "w_down")
SMALL = ("attn_norm_w", "q_lat_norm_w", "kv_lat_norm_w", "q_norm_w", "k_norm_w", "mla_out_norm_w", "a_log", "dt_bias",
         "gdn_norm_w", "mlp_norm_w")
WEIGHTS = ("attn_norm_w", "w_in", "q_lat_norm_w", "w_uq", "kv_lat_norm_w", "w_ukv", "q_norm_w", "k_norm_w", "mla_out_norm_w",
           "conv_w", "a_log", "dt_bias", "gdn_norm_w", "w_out", "mlp_norm_w", "w_up", "w_down")


def _sds(shape, dtype):
    return jax.ShapeDtypeStruct(shape, dtype)


def _params(semantics):
    return pltpu.CompilerParams(dimension_semantics=semantics, vmem_limit_bytes=VMEM_LIMIT)


def _block(n):
    for b in (512, 256, 128):
        if n % b == 0:
            return b
    return n


def _dg(a, b, ca, cb, prec):
    lead = a.ndim - 2
    batch = (tuple(range(lead)),) * 2
    return lax.dot_general(a, b, (((ca + lead,), (cb + lead,)), batch), precision=prec, preferred_element_type=F32)


def _split_bf16(a):
    hi = a.astype(BF16)
    return hi, (a - hi.astype(F32)).astype(BF16)


def _dot_bf16(a, b, ca, cb):
    return _dg(a.astype(BF16), b.astype(BF16), ca, cb, None)


def _dot_bf16x3(a, b, ca, cb):
    a_hi, a_lo = _split_bf16(a)
    b_hi, b_lo = _split_bf16(b)
    lead = a.ndim - 2
    return _dg(jnp.concatenate([a_hi, a_hi, a_lo], axis=ca + lead), jnp.concatenate([b_hi, b_lo, b_hi], axis=cb + lead), ca, cb, None)


def _matmul_family(dot):
    def nn_raw(a, b):
        return dot(a, b, 1, 0)

    def nt_raw(a, b):
        return dot(a, b, 1, 1)

    def tn_raw(a, b):
        return dot(a, b, 0, 0)

    @jax.custom_vjp
    def nn(a, b):
        return nn_raw(a, b)

    nn.defvjp(lambda a, b: (nn_raw(a, b), (a, b)), lambda r, g: (nt_raw(g, r[1]), tn_raw(r[0], g)))

    @jax.custom_vjp
    def nt(a, b):
        return nt_raw(a, b)

    nt.defvjp(lambda a, b: (nt_raw(a, b), (a, b)), lambda r, g: (nn_raw(g, r[1]), tn_raw(g, r[0])))

    @jax.custom_vjp
    def tn(a, b):
        return tn_raw(a, b)

    tn.defvjp(lambda a, b: (tn_raw(a, b), (a, b)), lambda r, g: (nt_raw(r[1], g), nn_raw(r[0], g)))
    return nn, nt, tn


_bf_nn, _bf_nt, _bf_tn = _matmul_family(_dot_bf16)
_hi_nn, _hi_nt, _hi_tn = _matmul_family(_dot_bf16x3)


def _lower_powers(lmat):
    powers = []
    while 2 ** (len(powers) + 1) < lmat.shape[-1]:
        powers.append(_dot_bf16x3(powers[-1] if powers else lmat, powers[-1] if powers else lmat, 1, 0))
    return powers


@jax.custom_vjp
def _unit_lower_solve(lmat, rhs):
    return _unit_lower_solve_fwd(lmat, rhs)[0]


def _unit_lower_solve_fwd(lmat, rhs):
    powers = _lower_powers(lmat)
    x = rhs - _dot_bf16x3(lmat, rhs, 1, 0)
    for p in powers:
        x = x + _dot_bf16x3(p, x, 1, 0)
    return x, (lmat, powers, x)


def _unit_lower_solve_bwd(res, g):
    lmat, powers, x = res
    y = g - _dot_bf16x3(lmat, g, 0, 0)
    for p in powers:
        y = y + _dot_bf16x3(p, y, 0, 0)
    return -_dot_bf16x3(y, x, 1, 1), y


_unit_lower_solve.defvjp(_unit_lower_solve_fwd, _unit_lower_solve_bwd)


@jax.custom_vjp
def _lane_halves(x):
    n = x.shape[-1] // 2
    return x[..., :n], x[..., n:]


_lane_halves.defvjp(lambda x: (_lane_halves(x), None), lambda _, g: (jnp.concatenate(g, axis=-1),))


@jax.custom_vjp
def _row_halves(x):
    n = x.shape[-2] // 2
    return x[..., :n, :], x[..., n:, :]


_row_halves.defvjp(lambda x: (_row_halves(x), None), lambda _, g: (jnp.concatenate(g, axis=-2),))


@jax.custom_vjp
def _swap_halves(t):
    return pltpu.roll(t, 64, 1)


_swap_halves.defvjp(lambda t: (pltpu.roll(t, 64, 1), None), lambda _, g: (pltpu.roll(g, 64, 1),))


@functools.partial(jax.custom_vjp, nondiff_argnums=(2,))
def _shift_rows(x, keep, s):
    return pltpu.roll(x, s, 0) * keep


def _shift_rows_fwd(x, keep, s):
    return pltpu.roll(x, s, 0) * keep, keep


def _shift_rows_bwd(s, keep, g):
    return pltpu.roll(g * keep, keep.shape[0] - s, 0), jnp.zeros_like(keep)


_shift_rows.defvjp(_shift_rows_fwd, _shift_rows_bwd)


def _sigmoid(x):
    return 0.5 * jnp.tanh(0.5 * x) + 0.5


def _softplus(x):
    return jnp.maximum(x, 0.0) + jnp.log(1.0 + jnp.exp(jnp.minimum(x, -x)))


def _silu(x):
    return x * _sigmoid(x)


def _rms(x, w, n=None):
    n = x.shape[-1] if n is None else n
    r = lax.rsqrt(jnp.sum(x * x, axis=-1, keepdims=True) * (1.0 / n) + EPS)
    return x * r * w


def _rope(t, cos_f, sin_f):
    return t * cos_f + _swap_halves(t) * sin_f


def _rope_tables(pos_col, freq_row, sign_row):
    ang = pos_col.astype(F32) * freq_row
    return jnp.cos(ang), jnp.sin(ang) * sign_row


def _onehot_row(lane):
    return (lax.broadcasted_iota(jnp.int32, (1, 128), 1) == lane).astype(F32)


def _row_spec(tm, w):
    return pl.BlockSpec((tm, w), lambda i: (i, 0))


def _const_spec(shape):
    return pl.BlockSpec(shape, lambda *_: (0,) * len(shape))


def _in_proj_fwd(x2, w_an, w_in_p):
    T, D = x2.shape
    tm = min(TOKEN_TILE, T)

    def body(x_ref, wn_ref, w_ref, xn_ref, lat_ref, gqkv_ref, gz_ref, gab_ref):
        x = x_ref[...]
        r = lax.rsqrt(jnp.mean(x * x, axis=-1, keepdims=True) + EPS)
        xn = (x * r * wn_ref[...]).astype(BF16)
        xn_ref[...] = xn
        for ref, (a, b) in zip((lat_ref, gqkv_ref, gz_ref, gab_ref), PROJ_SPLITS):
            ref[...] = _dg(xn, w_ref[a:b, :], 1, 1, None)

    widths = [b - a for a, b in PROJ_SPLITS]
    return pl.pallas_call(
        body, grid=(T // tm,),
        in_specs=[_row_spec(tm, D), _const_spec((1, D)), _const_spec((PROJ_W, D))],
        out_specs=[_row_spec(tm, D)] + [_row_spec(tm, w) for w in widths],
        out_shape=[_sds((T, D), BF16)] + [_sds((T, w), F32) for w in widths],
        compiler_params=_params(("parallel",)), name="in_proj_fwd",
    )(x2, w_an, w_in_p)


def _in_proj_bwd(pieces, w_in_p, x2, w_an, dh):
    T, D = x2.shape
    tm = min(TOKEN_TILE, T)
    widths = [p.shape[1] for p in pieces]
    starts = [sum(widths[:i]) for i in range(len(widths))]
    assert sum(widths) == PROJ_W

    def body(*refs):
        piece_refs = refs[:len(pieces)]
        w_ref, x_ref, wn_ref, dh_ref, dx_ref, dp_ref, dwn_ref = refs[len(pieces):]

        @pl.when(pl.program_id(0) == 0)
        def _():
            dwn_ref[...] = jnp.zeros_like(dwn_ref)

        dxn = jnp.zeros((tm, D), F32)
        for ref, a, width in zip(piece_refs, starts, widths):
            piece = ref[...].astype(BF16)
            dp_ref[:, a:a + width] = piece
            dxn += _dg(piece, w_ref[a:a + width, :], 1, 0, None)
        _, pull = jax.vjp(_rms, x_ref[...], wn_ref[...])
        dx, dwn = pull(dxn)
        dx_ref[...] = dx + dh_ref[...]
        dwn_ref[...] += dwn

    return pl.pallas_call(
        body, grid=(T // tm,),
        in_specs=[_row_spec(tm, w) for w in widths] + [_const_spec((PROJ_W, D)), _row_spec(tm, D), _const_spec((1, D)),
                                                       _row_spec(tm, D)],
        out_specs=[_row_spec(tm, D), _row_spec(tm, PROJ_W), _const_spec((1, D))],
        out_shape=[_sds((T, D), F32), _sds((T, PROJ_W), BF16), _sds((1, D), F32)],
        compiler_params=_params(("arbitrary",)), name="in_proj_bwd",
    )(*pieces, w_in_p, x2, w_an, dh)


def _wgrad(a, b, name, column_shards=1, out_dtype=BF16):
    T, k1 = a.shape
    k2 = b.shape[1]
    per_shard = k2 // column_shards
    tt = min(TOKEN_TILE, T)
    b1 = k1
    while b1 * k2 * 4 > WGRAD_OUT_BYTES and b1 % 256 == 0:
        b1 //= 2
    step = _block(per_shard)

    def body(a_ref, b_ref, o_ref, acc_ref):
        t = pl.program_id(1)

        @pl.when(t == 0)
        def _():
            acc_ref[...] = jnp.zeros_like(acc_ref)

        a_t = a_ref[...].astype(BF16).T
        for c0 in range(0, k2, step):
            part = jnp.dot(a_t, b_ref[:, c0:c0 + step].astype(BF16), preferred_element_type=F32)
            if column_shards == 1:
                acc_ref[:, c0:c0 + step] += part
            else:
                acc_ref[c0 // per_shard, :, c0 % per_shard:c0 % per_shard + step] += part

        @pl.when(t == T // tt - 1)
        def _():
            o_ref[...] = acc_ref[...].astype(o_ref.dtype)

    if column_shards == 1:
        block, out_spec, out_shape = (b1, k2), pl.BlockSpec((b1, k2), lambda i, t: (i, 0)), _sds((k1, k2), out_dtype)
    else:
        block = (column_shards, b1, per_shard)
        out_spec, out_shape = pl.BlockSpec(block, lambda i, t: (0, i, 0)), _sds((column_shards, k1, per_shard), out_dtype)
    return pl.pallas_call(
        body, grid=(k1 // b1, T // tt),
        in_specs=[pl.BlockSpec((tt, b1), lambda i, t: (t, i)), pl.BlockSpec((tt, k2), lambda i, t: (t, 0))],
        out_specs=out_spec, out_shape=out_shape, scratch_shapes=[pltpu.VMEM(block, F32)],
        compiler_params=_params(("parallel", "arbitrary")), name=name,
    )(a, b)


def _mla_pre_fn(q_lat, kv_lat, kpe, ln_q, ln_kv, w_list, qn_n, qn_p, kn_n, kn_p, cos_f, sin_f):
    qn = _rms(q_lat, ln_q)
    kvn = _rms(kv_lat, ln_kv)
    kp = _rope(_rms(kpe, kn_p, ROPE_DIM), cos_f, sin_f)
    outs = []
    for h in range(HEADS):
        outs.append(_rms(_bf_nn(qn, w_list[h]), qn_n))
        outs.append(_rope(_rms(_bf_nn(qn, w_list[HEADS + h]), qn_p, ROPE_DIM), cos_f, sin_f))
        outs.append(_rms(_bf_nn(kvn, w_list[2 * HEADS + h]), kn_n))
        outs.append(_bf_nn(kvn, w_list[3 * HEADS + h]))
    return tuple(outs) + (kp,)


def _mla_pre_operands(lat_ref, pos_ref, ln_ref, w_ref, nw_ref, rope_ref):
    cos_f, sin_f = _rope_tables(pos_ref[...], rope_ref[0:1, :], rope_ref[1:2, :])
    diff = (lat_ref[:, 0:LORA], lat_ref[:, LORA:2 * LORA], lat_ref[:, 2 * LORA:LAT_W], ln_ref[0:1, :], ln_ref[1:2, :],
            [w_ref[i].astype(F32) for i in range(4 * HEADS)], nw_ref[0:1, :], nw_ref[1:2, :], nw_ref[2:3, :], nw_ref[3:4, :])
    return diff, cos_f, sin_f


def _mla_pre_fwd(lat, pos, ln_w, w_mla, nw, rope_rows):
    T = lat.shape[0]
    tm = min(TOKEN_TILE, T)

    def body(lat_ref, pos_ref, ln_ref, w_ref, nw_ref, rope_ref, q_ref, k_ref, v_ref):
        diff, cos_f, sin_f = _mla_pre_operands(lat_ref, pos_ref, ln_ref, w_ref, nw_ref, rope_ref)
        outs = _mla_pre_fn(*diff, cos_f, sin_f)
        kp = outs[-1].astype(BF16)
        for h in range(HEADS):
            q_n, q_p, k_n, v = outs[4 * h:4 * h + 4]
            q_ref[:, h * QK_PAD:h * QK_PAD + HEAD_DIM] = q_n.astype(BF16)
            q_ref[:, h * QK_PAD + HEAD_DIM:(h + 1) * QK_PAD] = q_p.astype(BF16)
            k_ref[:, h * QK_PAD:h * QK_PAD + HEAD_DIM] = k_n.astype(BF16)
            k_ref[:, h * QK_PAD + HEAD_DIM:(h + 1) * QK_PAD] = kp
            v_ref[:, h * HEAD_DIM:(h + 1) * HEAD_DIM] = v.astype(BF16)

    return pl.pallas_call(
        body, grid=(T // tm,),
        in_specs=[_row_spec(tm, LAT_W), _row_spec(tm, 1), _const_spec((2, LORA)), _const_spec((4 * HEADS, LORA, 128)),
                  _const_spec((8, 128)), _const_spec((8, 128))],
        out_specs=[_row_spec(tm, HEADS * QK_PAD), _row_spec(tm, HEADS * QK_PAD), _row_spec(tm, HEADS * HEAD_DIM)],
        out_shape=[_sds((T, HEADS * QK_PAD), BF16), _sds((T, HEADS * QK_PAD), BF16), _sds((T, HEADS * HEAD_DIM), BF16)],
        compiler_params=_params(("parallel",)), name="mla_pre_fwd",
    )(lat, pos, ln_w, w_mla, nw, rope_rows)


def _mla_pre_bwd(lat, pos, ln_w, w_mla, nw, rope_rows, dq, dk, dv, halves):
    T = lat.shape[0]
    tm = min(TOKEN_TILE, T)
    ns = len(halves)

    def body(*refs):
        lat_ref, pos_ref, ln_ref, w_ref, nw_ref, rope_ref, dq_ref, dk_ref, dv_ref = refs[:9]
        src_refs = refs[9:9 + ns]
        dlat_ref, dln_ref, dw_ref, dnw_ref = refs[9 + ns:13 + ns]
        dst_refs = refs[13 + ns:13 + 2 * ns]
        sems = refs[13 + 2 * ns:]

        @pl.when(pl.program_id(0) == 0)
        def _():
            for start in _swap_copies(src_refs, dst_refs, *sems)[0]:
                start()
            dln_ref[...] = jnp.zeros_like(dln_ref)
            dw_ref[...] = jnp.zeros_like(dw_ref)
            dnw_ref[...] = jnp.zeros_like(dnw_ref)

        diff, cos_f, sin_f = _mla_pre_operands(lat_ref, pos_ref, ln_ref, w_ref, nw_ref, rope_ref)
        _, pull = jax.vjp(lambda *a: _mla_pre_fn(*a, cos_f, sin_f), *diff)
        cts = []
        d_kp = jnp.zeros((tm, 128), F32)
        for h in range(HEADS):
            cts.append(dq_ref[:, h * QK_PAD:h * QK_PAD + HEAD_DIM])
            cts.append(dq_ref[:, h * QK_PAD + HEAD_DIM:(h + 1) * QK_PAD])
            cts.append(dk_ref[:, h * QK_PAD:h * QK_PAD + HEAD_DIM])
            cts.append(dv_ref[:, h * HEAD_DIM:(h + 1) * HEAD_DIM])
            d_kp += dk_ref[:, h * QK_PAD + HEAD_DIM:(h + 1) * QK_PAD]
        d_ql, d_kvl, d_kpe, d_lnq, d_lnkv, d_w, d_qn_n, d_qn_p, d_kn_n, d_kn_p = pull(tuple(cts) + (d_kp,))
        dlat_ref[:, 0:LORA] = d_ql
        dlat_ref[:, LORA:2 * LORA] = d_kvl
        dlat_ref[:, 2 * LORA:LAT_W] = d_kpe
        dln_ref[0:1, :] += d_lnq
        dln_ref[1:2, :] += d_lnkv
        for i in range(4 * HEADS):
            dw_ref[i] += d_w[i]
        for i, d in enumerate((d_qn_n, d_qn_p, d_kn_n, d_kn_p)):
            dnw_ref[i:i + 1, :] += d

        @pl.when(pl.program_id(0) == T // tm - 1)
        def _():
            for wait in _swap_copies(src_refs, dst_refs, *sems)[1]:
                wait()

    return pl.pallas_call(
        body, grid=(T // tm,),
        in_specs=[_row_spec(tm, LAT_W), _row_spec(tm, 1), _const_spec((2, LORA)), _const_spec((4 * HEADS, LORA, 128)),
                  _const_spec((8, 128)), _const_spec((8, 128)),
                  _row_spec(tm, HEADS * QK_PAD), _row_spec(tm, HEADS * QK_PAD), _row_spec(tm, HEADS * HEAD_DIM)] + [_ANY] * ns,
        out_specs=[_row_spec(tm, LAT_W), _const_spec((2, LORA)), _const_spec((4 * HEADS, LORA, 128)), _const_spec((8, 128))]
                  + [_ANY] * ns,
        out_shape=[_sds((T, LAT_W), F32), _sds((2, LORA), F32), _sds((4 * HEADS, LORA, 128), F32), _sds((8, 128), F32)]
                  + [_swapped_shape(h) for h in halves],
        scratch_shapes=_swap_scratch(ns),
        compiler_params=_params(("arbitrary",)), name="mla_pre_bwd",
    )(lat, pos, ln_w, w_mla, nw, rope_rows, dq, dk, dv, *halves)


def _causal_mask(i, j, tq, tk):
    row = i * tq + lax.broadcasted_iota(jnp.int32, (tq, tk), 0)
    col = j * tk + lax.broadcasted_iota(jnp.int32, (tq, tk), 1)
    return col <= row


def _attn_fwd(q, k, v, shards):
    B, S, _ = q.shape
    t = min(ATTN_TILE, S)
    nq = S // t
    ns = len(shards)

    hp = ATTN_HEADS_PER_STEP
    qk = lambda h: slice(h * QK_PAD, (h + 1) * QK_PAD)
    vd = lambda h: slice(h * HEAD_DIM, (h + 1) * HEAD_DIM)

    def body(*refs):
        q_ref, k_ref, v_ref = refs[:3]
        src_refs = refs[3:3 + ns]
        o_ref, lse_ref = refs[3 + ns:5 + ns]
        dst_refs = refs[5 + ns:5 + 2 * ns]
        sems = refs[5 + 2 * ns:]
        b, g, i = pl.program_id(0), pl.program_id(1), pl.program_id(2)
        qb = [q_ref[0, :, qk(h)] for h in range(hp)]

        step_no = (b * (HEADS // hp) + g) * nq + i
        for phase, at in enumerate((0, (3 * B * (HEADS // hp) * nq) // 4)):
            @pl.when(step_no == at)
            def _(phase=phase):
                for call in _gather_copies(src_refs, dst_refs, *sems)[phase]:
                    call()

        def step(j, carry, diagonal):
            rows = pl.ds(pl.multiple_of(j * t, t), t)
            s = [_dg(qb[h], k_ref[0, rows, qk(h)], 1, 1, None) * ATTN_SCALE for h in range(hp)]
            if diagonal:
                keep = _causal_mask(0, 0, t, t)
                s = [jnp.where(keep, x, -1e30) for x in s]
            m_new = [jnp.maximum(carry[h][0], jnp.max(s[h], axis=-1, keepdims=True)) for h in range(hp)]
            p = [jnp.exp(s[h] - m_new[h]) for h in range(hp)]
            alpha = [jnp.exp(carry[h][0] - m_new[h]) for h in range(hp)]
            l = [alpha[h] * carry[h][1] + jnp.sum(p[h], axis=-1, keepdims=True) for h in range(hp)]
            pv = [jnp.dot(p[h].astype(BF16), v_ref[0, rows, vd(h)], preferred_element_type=F32) for h in range(hp)]
            return tuple((m_new[h], l[h], alpha[h] * carry[h][2] + pv[h]) for h in range(hp))

        init = tuple((jnp.full((t, 1), -1e30, F32), jnp.zeros((t, 1), F32), jnp.zeros((t, HEAD_DIM), F32)) for _ in range(hp))
        below = lax.fori_loop(0, i, lambda j, carry: step(j, carry, False), init)
        for h, (m, l, acc) in enumerate(step(i, below, True)):
            o_ref[0, :, vd(h)] = acc / l
            lse_ref[0, h] = m + jnp.log(l)

        @pl.when((b == B - 1) & (g == HEADS // hp - 1) & (i == nq - 1))
        def _():
            for wait in _gather_copies(src_refs, dst_refs, *sems)[2]:
                wait()

    return pl.pallas_call(
        body, grid=(B, HEADS // hp, nq),
        in_specs=[pl.BlockSpec((1, t, hp * QK_PAD), lambda b, g, i: (b, i, g)),
                  pl.BlockSpec((1, S, hp * QK_PAD), lambda b, g, i: (b, 0, g)),
                  pl.BlockSpec((1, S, hp * HEAD_DIM), lambda b, g, i: (b, 0, g))] + [_ANY] * ns,
        out_specs=[pl.BlockSpec((1, t, hp * HEAD_DIM), lambda b, g, i: (b, i, g)),
                   pl.BlockSpec((1, hp, t, 1), lambda b, g, i: (b, g, i, 0))] + [_ANY] * ns,
        out_shape=[_sds((B, S, HEADS * HEAD_DIM), F32), _sds((B, HEADS, S, 1), F32)] + [_sds((4,) + s.shape, s.dtype) for s in shards],
        scratch_shapes=_gather_scratch(ns),
        compiler_params=_params(("arbitrary", "arbitrary", "arbitrary")), name="attn_fwd",
    )(q, k, v, *shards)


def _attn_bwd(q, k, v, o, lse, do, partials):
    B, S, _ = q.shape
    t = min(ATTN_TILE, S)
    nq = S // t
    ns = len(partials)

    hp = ATTN_HEADS_PER_STEP
    qk = lambda h: slice(h * QK_PAD, (h + 1) * QK_PAD)
    vd = lambda h: slice(h * HEAD_DIM, (h + 1) * HEAD_DIM)
    heads = range(hp)

    def body(*refs):
        q_ref, k_ref, v_ref, o_ref, lse_ref, do_ref = refs[:6]
        src_refs = refs[6:6 + ns]
        dq_ref, dk_ref, dv_ref = refs[6 + ns:9 + ns]
        dst_refs = refs[9 + ns:9 + 2 * ns]
        dsum_ref, send_sems, recv_sems, local_sems = refs[9 + 2 * ns:]
        b, g, j = pl.program_id(0), pl.program_id(1), pl.program_id(2)

        @pl.when((b == 0) & (g == 0) & (j == 0))
        def _():
            for start in _scatter_copies(src_refs, dst_refs, send_sems, recv_sems, local_sems)[0]:
                start()

        @pl.when(j == 0)
        def _():
            dq_ref[...] = jnp.zeros_like(dq_ref)
            for h in heads:
                dsum_ref[h] = jnp.sum(do_ref[0, :, vd(h)] * o_ref[0, :, vd(h)], axis=-1, keepdims=True)

        kb = [k_ref[0, :, qk(h)] for h in heads]
        vb = [v_ref[0, :, vd(h)] for h in heads]

        def step(i, carry, diagonal):
            rows = pl.ds(pl.multiple_of(i * t, t), t)
            qb = [q_ref[0, rows, qk(h)] for h in heads]
            dob = [do_ref[0, rows, vd(h)].astype(BF16) for h in heads]
            s = [_dg(qb[h], kb[h], 1, 1, None) * ATTN_SCALE for h in heads]
            p = [jnp.exp(s[h] - lse_ref[0, h, rows, :]) for h in heads]
            if diagonal:
                keep = _causal_mask(0, 0, t, t)
                p = [jnp.where(keep, x, 0.0) for x in p]
            dp = [_dg(dob[h], vb[h], 1, 1, None) for h in heads]
            dv = [carry[h][1] + _dg(p[h].astype(BF16), dob[h], 0, 0, None) for h in heads]
            ds = [(p[h] * (dp[h] - dsum_ref[h, rows, :]) * ATTN_SCALE).astype(BF16) for h in heads]
            for h in heads:
                dq_ref[0, rows, qk(h)] += jnp.dot(ds[h], kb[h], preferred_element_type=F32)
            return tuple((carry[h][0] + _dg(ds[h], qb[h], 0, 0, None), dv[h]) for h in heads)

        zeros = tuple((jnp.zeros((t, QK_PAD), F32), jnp.zeros((t, HEAD_DIM), F32)) for _ in heads)
        on_diagonal = step(j, zeros, True)
        done = lax.fori_loop(j + 1, nq, lambda i, carry: step(i, carry, False), on_diagonal)
        for h, (dk, dv) in enumerate(done):
            dk_ref[0, :, qk(h)] = dk
            dv_ref[0, :, vd(h)] = dv

        @pl.when((b == B - 1) & (g == HEADS // hp - 1) & (j == nq - 1))
        def _():
            for wait in _scatter_copies(src_refs, dst_refs, send_sems, recv_sems, local_sems)[1]:
                wait()

    return pl.pallas_call(
        body, grid=(B, HEADS // hp, nq),
        in_specs=[pl.BlockSpec((1, S, hp * QK_PAD), lambda b, g, j: (b, 0, g)),
                  pl.BlockSpec((1, t, hp * QK_PAD), lambda b, g, j: (b, j, g)),
                  pl.BlockSpec((1, t, hp * HEAD_DIM), lambda b, g, j: (b, j, g)),
                  pl.BlockSpec((1, S, hp * HEAD_DIM), lambda b, g, j: (b, 0, g)),
                  pl.BlockSpec((1, hp, S, 1), lambda b, g, j: (b, g, 0, 0)),
                  pl.BlockSpec((1, S, hp * HEAD_DIM), lambda b, g, j: (b, 0, g))] + [_ANY] * ns,
        out_specs=[pl.BlockSpec((1, S, hp * QK_PAD), lambda b, g, j: (b, 0, g)),
                   pl.BlockSpec((1, t, hp * QK_PAD), lambda b, g, j: (b, j, g)),
                   pl.BlockSpec((1, t, hp * HEAD_DIM), lambda b, g, j: (b, j, g))] + [_ANY] * ns,
        out_shape=[_sds((B, S, HEADS * QK_PAD), F32), _sds((B, S, HEADS * QK_PAD), F32), _sds((B, S, HEADS * HEAD_DIM), F32)]
                  + [_scattered_shape(p) for p in partials],
        scratch_shapes=[pltpu.VMEM((hp, S, 1), F32)] + _scatter_scratch(ns),
        compiler_params=_params(("arbitrary", "arbitrary", "arbitrary")), name="attn_bwd",
    )(q, k, v, o, lse, do, *partials)


def _gdn_pre_fn(xq, xk, xv, wq, wk, wv, keeps):
    def conv_silu(x, w):
        acc = x * w[3]
        for s in (1, 2, 3):
            acc = acc + _shift_rows(x, keeps[s - 1], s) * w[3 - s]
        return _silu(acc)

    def l2(x):
        return x * lax.rsqrt(jnp.sum(x * x, axis=-1, keepdims=True) + EPS)

    return l2(conv_silu(xq, wq)) * (HEAD_DIM ** -0.5), l2(conv_silu(xk, wk)), conv_silu(xv, wv)


def _gdn_pre_specs(S):
    x_specs = [pl.BlockSpec((1, S, HEAD_DIM), lambda h, b, g=g: (b, 0, g * HEADS + h)) for g in range(3)]
    w_specs = [pl.BlockSpec((CONV_TAPS, HEAD_DIM), lambda h, b, g=g: (0, g * HEADS + h)) for g in range(3)]
    out_spec = pl.BlockSpec((1, S, HEAD_DIM), lambda h, b: (b, 0, h))
    return x_specs, w_specs, out_spec


def _row_keeps(S):
    t = lax.broadcasted_iota(jnp.int32, (S, HEAD_DIM), 0)
    return [(t >= s).astype(F32) for s in (1, 2, 3)]


def _gdn_pre_fwd(gqkv, conv_w):
    B, S, _ = gqkv.shape
    x_specs, w_specs, out_spec = _gdn_pre_specs(S)

    def body(xq_ref, xk_ref, xv_ref, wq_ref, wk_ref, wv_ref, q_ref, k_ref, v_ref):
        taps = [[w[i:i + 1, :] for i in range(CONV_TAPS)] for w in (wq_ref, wk_ref, wv_ref)]
        q, k, v = _gdn_pre_fn(xq_ref[0], xk_ref[0], xv_ref[0], *taps, _row_keeps(S))
        q_ref[0], k_ref[0], v_ref[0] = q, k, v

    return pl.pallas_call(
        body, grid=(HEADS, B), in_specs=x_specs + w_specs, out_specs=[out_spec] * 3,
        out_shape=[_sds((B, S, HEADS * HEAD_DIM), F32)] * 3,
        compiler_params=_params(("parallel", "parallel")), name="gdn_pre_fwd",
    )(gqkv, gqkv, gqkv, conv_w, conv_w, conv_w)


def _gdn_pre_bwd(gqkv, conv_w, dq, dk, dv, halves):
    B, S, _ = gqkv.shape
    x_specs, w_specs, out_spec = _gdn_pre_specs(S)
    dw_spec = pl.BlockSpec((CONV_TAPS, HEAD_DIM), lambda h, b: (0, h))
    ns = len(halves)

    def body(*refs):
        xq_ref, xk_ref, xv_ref, wq_ref, wk_ref, wv_ref, dq_ref, dk_ref, dv_ref = refs[:9]
        src_refs = refs[9:9 + ns]
        dxq_ref, dxk_ref, dxv_ref, dwq_ref, dwk_ref, dwv_ref = refs[9 + ns:15 + ns]
        dst_refs = refs[15 + ns:15 + 2 * ns]
        sems = refs[15 + 2 * ns:]
        first = (pl.program_id(0) == 0) & (pl.program_id(1) == 0)
        last = (pl.program_id(0) == HEADS - 1) & (pl.program_id(1) == B - 1)

        @pl.when(first)
        def _():
            for start in _swap_copies(src_refs, dst_refs, *sems)[0]:
                start()

        @pl.when(pl.program_id(1) == 0)
        def _():
            for r in (dwq_ref, dwk_ref, dwv_ref):
                r[...] = jnp.zeros_like(r)

        taps = [[w[i:i + 1, :] for i in range(CONV_TAPS)] for w in (wq_ref, wk_ref, wv_ref)]
        keeps = _row_keeps(S)
        _, pull = jax.vjp(lambda *a: _gdn_pre_fn(*a, keeps), xq_ref[0], xk_ref[0], xv_ref[0], *taps)
        dxq, dxk, dxv, dwq, dwk, dwv = pull((dq_ref[0], dk_ref[0], dv_ref[0]))
        dxq_ref[0], dxk_ref[0], dxv_ref[0] = dxq, dxk, dxv
        for ref, dw in ((dwq_ref, dwq), (dwk_ref, dwk), (dwv_ref, dwv)):
            for i in range(CONV_TAPS):
                ref[i:i + 1, :] += dw[i]

        @pl.when(last)
        def _():
            for wait in _swap_copies(src_refs, dst_refs, *sems)[1]:
                wait()

    hw = HEADS * HEAD_DIM
    return pl.pallas_call(
        body, grid=(HEADS, B), in_specs=x_specs + w_specs + [out_spec] * 3 + [_ANY] * ns,
        out_specs=[out_spec] * 3 + [dw_spec] * 3 + [_ANY] * ns,
        out_shape=[_sds((B, S, hw), F32)] * 3 + [_sds((CONV_TAPS, hw), F32)] * 3 + [_swapped_shape(h) for h in halves],
        scratch_shapes=_swap_scratch(ns),
        compiler_params=_params(("arbitrary", "arbitrary")), name="gdn_pre_bwd",
    )(gqkv, gqkv, gqkv, conv_w, conv_w, conv_w, dq, dk, dv, *halves)


def _chunk_masks():
    i = lax.broadcasted_iota(jnp.int32, (CHUNK, CHUNK), 0)
    j = lax.broadcasted_iota(jnp.int32, (CHUNK, CHUNK), 1)
    lower, after = (j <= i).astype(F32), (j > i).astype(F32)
    return {"le": lower, "le_gt": jnp.concatenate([lower, after], axis=0), "strict": (j < i).astype(F32)}


def _gdn_chunk_fn(groups, masks):
    lane = lax.broadcasted_iota(jnp.int32, (groups, 1, 128), 2)
    head = lax.broadcasted_iota(jnp.int32, (groups, 1, 128), 0) % HEADS
    pick_a, pick_b = (lane == head).astype(F32), (lane == head + HEADS).astype(F32)
    lower, lower_after, strict = (jnp.broadcast_to(masks[n], (groups,) + masks[n].shape) for n in ("le", "le_gt", "strict"))
    ones_row = jnp.ones((1, 1, HEAD_DIM), F32)

    def f(q, k, v, gab, a_row, dt_row, state):
        ga = jnp.sum(gab * pick_a, axis=2, keepdims=True)
        gb = jnp.sum(gab * pick_b, axis=2, keepdims=True)
        a_log = jnp.sum(a_row * pick_a, axis=2, keepdims=True)
        dt_bias = jnp.sum(dt_row * pick_a, axis=2, keepdims=True)
        beta = _sigmoid(gb)
        g = -jnp.exp(a_log) * _softplus(ga + dt_bias)
        g_wide = g * ones_row
        cum, rest = _row_halves(_hi_nn(lower_after, g_wide))
        total = jnp.sum(g_wide, axis=1, keepdims=True)
        diff = _hi_nn(lower, g * strict)
        decay = lower * jnp.exp(diff)
        e_cum = jnp.exp(cum)
        lmat = strict * (beta * _bf_nt(k, k) * decay)
        u, w = _lane_halves(_unit_lower_solve(lmat, jnp.concatenate([v * beta, k * (beta * e_cum)], axis=2)))
        attn = _bf_nt(q, k) * decay
        v_new = u - _bf_nn(w, state)
        o = _bf_nn(q * e_cum, state) + _bf_nn(attn, v_new)
        new_state = state * jnp.exp(total) + _bf_tn(k * jnp.exp(rest), v_new)
        return o, new_state

    return f


def _gdn_chunk_fwd(q, k, v, gab, scal, shards):
    B, S, W = q.shape
    N = S // CHUNK
    ns = len(shards)

    def body(*refs):
        q_ref, k_ref, v_ref, gab_ref, sc_ref = refs[:5]
        src_refs = refs[5:5 + ns]
        o_ref, st_ref = refs[5 + ns:7 + ns]
        dst_refs = refs[7 + ns:7 + 2 * ns]
        state_ref, send_sems, recv_sems, local_sems = refs[7 + 2 * ns:]
        n = pl.program_id(0)

        @pl.when(n == 0)
        def _():
            for start in _gather_copies(src_refs, dst_refs, send_sems, recv_sems, local_sems)[0]:
                start()
            state_ref[...] = jnp.zeros_like(state_ref)

        @pl.when(n == (2 * N) // 3)
        def _():
            for pass_on in _gather_copies(src_refs, dst_refs, send_sems, recv_sems, local_sems)[1]:
                pass_on()

        groups = [(b, h) for b in range(B) for h in range(HEADS)]
        gather = lambda ref: jnp.stack([ref[b, :, h * HEAD_DIM:(h + 1) * HEAD_DIM] for b, h in groups])
        state = state_ref[...]
        for i, (b, h) in enumerate(groups):
            st_ref[b, 0, h] = state[i]
        o, new_state = _gdn_chunk_fn(len(groups), _chunk_masks())(
            gather(q_ref), gather(k_ref), gather(v_ref), jnp.stack([gab_ref[b] for b, _ in groups]), sc_ref[0:1, :], sc_ref[1:2, :], state)
        for i, (b, h) in enumerate(groups):
            o_ref[b, :, h * HEAD_DIM:(h + 1) * HEAD_DIM] = o[i]
        state_ref[...] = new_state

        @pl.when(n == N - 1)
        def _():
            for wait in _gather_copies(src_refs, dst_refs, send_sems, recv_sems, local_sems)[2]:
                wait()

    seq = pl.BlockSpec((B, CHUNK, W), lambda n: (0, n, 0))
    return pl.pallas_call(
        body, grid=(N,),
        in_specs=[seq, seq, seq, pl.BlockSpec((B, CHUNK, GAB_W), lambda n: (0, n, 0)), _const_spec((8, 128))] + [_ANY] * ns,
        out_specs=[seq, pl.BlockSpec((B, 1, HEADS, HEAD_DIM, HEAD_DIM), lambda n: (0, n, 0, 0, 0))] + [_ANY] * ns,
        out_shape=[_sds((B, S, W), F32), _sds((B, N, HEADS, HEAD_DIM, HEAD_DIM), F32)] + [_sds((4,) + s.shape, s.dtype) for s in shards],
        scratch_shapes=[pltpu.VMEM((B * HEADS, HEAD_DIM, HEAD_DIM), F32)] + _gather_scratch(ns),
        compiler_params=_params(("arbitrary",)), name="gdn_chunk_fwd",
    )(q, k, v, gab, scal, *shards)


def _gdn_chunk_bwd(q, k, v, gab, scal, states, do, partials):
    B, S, W = q.shape
    N = S // CHUNK
    ns = len(partials)

    def body(*refs):
        q_ref, k_ref, v_ref, gab_ref, sc_ref, st_ref, do_ref = refs[:7]
        src_refs = refs[7:7 + ns]
        dq_ref, dk_ref, dv_ref, dgab_ref, dsc_ref = refs[7 + ns:12 + ns]
        dst_refs = refs[12 + ns:12 + 2 * ns]
        dstate_ref, send_sems, recv_sems, local_sems = refs[12 + 2 * ns:]
        n = pl.program_id(0)

        @pl.when(n == 0)
        def _():
            for start in _scatter_copies(src_refs, dst_refs, send_sems, recv_sems, local_sems)[0]:
                start()
            dstate_ref[...] = jnp.zeros_like(dstate_ref)
            dsc_ref[...] = jnp.zeros_like(dsc_ref)

        groups = [(b, h) for b in range(B) for h in range(HEADS)]
        gather = lambda ref: jnp.stack([ref[b, :, h * HEAD_DIM:(h + 1) * HEAD_DIM] for b, h in groups])
        _, pull = jax.vjp(_gdn_chunk_fn(len(groups), _chunk_masks()), gather(q_ref), gather(k_ref), gather(v_ref),
                          jnp.stack([gab_ref[b] for b, _ in groups]), sc_ref[0:1, :], sc_ref[1:2, :],
                          jnp.stack([st_ref[b, 0, h] for b, h in groups]))
        dq, dk, dv, dg, d_a, d_dt, dstate = pull((gather(do_ref), dstate_ref[...]))
        for i, (b, h) in enumerate(groups):
            lanes = slice(h * HEAD_DIM, (h + 1) * HEAD_DIM)
            dq_ref[b, :, lanes] = dq[i]
            dk_ref[b, :, lanes] = dk[i]
            dv_ref[b, :, lanes] = dv[i]
        for b in range(B):
            dgab_ref[b] = sum(dg[b * HEADS + h] for h in range(HEADS))
        dstate_ref[...] = dstate
        dsc_ref[0:1, :] += d_a
        dsc_ref[1:2, :] += d_dt

        @pl.when(n == N - 1)
        def _():
            for wait in _scatter_copies(src_refs, dst_refs, send_sems, recv_sems, local_sems)[1]:
                wait()

    seq = pl.BlockSpec((B, CHUNK, W), lambda n: (0, N - 1 - n, 0))
    gab_spec = pl.BlockSpec((B, CHUNK, GAB_W), lambda n: (0, N - 1 - n, 0))
    return pl.pallas_call(
        body, grid=(N,),
        in_specs=[seq, seq, seq, gab_spec, _const_spec((8, 128)),
                  pl.BlockSpec((B, 1, HEADS, HEAD_DIM, HEAD_DIM), lambda n: (0, N - 1 - n, 0, 0, 0)), seq] + [_ANY] * ns,
        out_specs=[seq, seq, seq, gab_spec, _const_spec((8, 128))] + [_ANY] * ns,
        out_shape=[_sds((B, S, W), F32)] * 3 + [_sds((B, S, GAB_W), F32), _sds((8, 128), F32)] + [_scattered_shape(p) for p in partials],
        scratch_shapes=[pltpu.VMEM((B * HEADS, HEAD_DIM, HEAD_DIM), F32)] + _scatter_scratch(ns),
        compiler_params=_params(("arbitrary",)), name="gdn_chunk_bwd",
    )(q, k, v, gab, scal, states, do, *partials)


def _mix_fn(ao, go, gz, w_mla, w_gdn):
    return tuple(_rms(ao[h], w_mla[h]) for h in range(HEADS)) + tuple(_rms(go[h], w_gdn) * _silu(gz[h]) for h in range(HEADS))


def _mix_operands(ao_ref, go_ref, gz_ref, nw_ref):
    blocks = lambda ref: [ref[:, h * HEAD_DIM:(h + 1) * HEAD_DIM] for h in range(HEADS)]
    return blocks(ao_ref), blocks(go_ref), blocks(gz_ref), [nw_ref[h:h + 1, :] for h in range(HEADS)], nw_ref[HEADS:HEADS + 1, :]


def _mix_fwd(ao, go, gz, nw, w_out, x2):
    T, D = x2.shape
    tm = min(TOKEN_TILE, T)
    MW = 2 * HEADS * HEAD_DIM

    def body(ao_ref, go_ref, gz_ref, nw_ref, w_ref, x_ref, mix_ref, h_ref):
        outs = _mix_fn(*_mix_operands(ao_ref, go_ref, gz_ref, nw_ref))
        for i, piece in enumerate(outs):
            mix_ref[:, i * HEAD_DIM:(i + 1) * HEAD_DIM] = piece.astype(BF16)
        h_ref[...] = x_ref[...] + jnp.dot(mix_ref[...], w_ref[...], preferred_element_type=F32)

    half = HEADS * HEAD_DIM
    return pl.pallas_call(
        body, grid=(T // tm,),
        in_specs=[_row_spec(tm, half), _row_spec(tm, half), _row_spec(tm, half), _const_spec((8, 128)), _const_spec((MW, D)),
                  _row_spec(tm, D)],
        out_specs=[_row_spec(tm, MW), _row_spec(tm, D)],
        out_shape=[_sds((T, MW), BF16), _sds((T, D), F32)],
        compiler_params=_params(("parallel",)), name="mix_fwd",
    )(ao, go, gz, nw, w_out, x2)


def _mix_bwd(ao, go, gz, nw, w_out, dh):
    T, D = dh.shape
    tm = min(TOKEN_TILE, T)
    MW = 2 * HEADS * HEAD_DIM
    half = HEADS * HEAD_DIM

    def body(ao_ref, go_ref, gz_ref, nw_ref, w_ref, dh_ref, dao_ref, dgo_ref, dgz_ref, dnw_ref):
        @pl.when(pl.program_id(0) == 0)
        def _():
            dnw_ref[...] = jnp.zeros_like(dnw_ref)

        d_mix = _dg(dh_ref[...].astype(BF16), w_ref[...], 1, 1, None)
        cts = tuple(d_mix[:, i * HEAD_DIM:(i + 1) * HEAD_DIM] for i in range(2 * HEADS))
        _, pull = jax.vjp(_mix_fn, *_mix_operands(ao_ref, go_ref, gz_ref, nw_ref))
        d_ao, d_go, d_gz, d_wm, d_wg = pull(cts)
        for h in range(HEADS):
            lanes = slice(h * HEAD_DIM, (h + 1) * HEAD_DIM)
            dao_ref[:, lanes] = d_ao[h]
            dgo_ref[:, lanes] = d_go[h]
            dgz_ref[:, lanes] = d_gz[h]
            dnw_ref[h:h + 1, :] += d_wm[h]
        dnw_ref[HEADS:HEADS + 1, :] += d_wg

    return pl.pallas_call(
        body, grid=(T // tm,),
        in_specs=[_row_spec(tm, half), _row_spec(tm, half), _row_spec(tm, half), _const_spec((8, 128)), _const_spec((MW, D)),
                  _row_spec(tm, D)],
        out_specs=[_row_spec(tm, half)] * 3 + [_const_spec((8, 128))],
        out_shape=[_sds((T, half), F32)] * 3 + [_sds((8, 128), F32)],
        compiler_params=_params(("arbitrary",)), name="mix_bwd",
    )(ao, go, gz, nw, w_out, dh)


def _up_spec(w_up, tf):
    per_shard = w_up.shape[2] // tf
    return pl.BlockSpec((None, w_up.shape[1], tf), lambda i, j: (j // per_shard, 0, j % per_shard))


def _mlp_fwd(h2, w_mn, w_up, w_down, target):
    T, D = h2.shape
    FF = w_down.shape[0]
    tm, tf = min(MLP_TOKEN_TILE, T), min(FF_TILE, w_up.shape[2])
    nf = FF // tf

    def body(h_ref, wn_ref, wu_ref, wd_ref, t_ref, hn_ref, dy_ref, sq_ref, acc_ref):
        j = pl.program_id(1)

        @pl.when(j == 0)
        def _():
            hn_ref[...] = _rms(h_ref[...], wn_ref[...]).astype(BF16)
            acc_ref[...] = jnp.zeros_like(acc_ref)

        up = jnp.dot(hn_ref[...], wu_ref[...], preferred_element_type=F32)
        act = jnp.square(jnp.maximum(up, 0.0)).astype(BF16)
        acc_ref[...] += jnp.dot(act, wd_ref[...], preferred_element_type=F32)

        @pl.when(j == nf - 1)
        def _():
            err = h_ref[...] + acc_ref[...] - t_ref[...]
            dy_ref[...] = err * (1.0 / D)
            sq_ref[...] = jnp.zeros_like(sq_ref) + jnp.sum(err * err)

    tok = lambda w: pl.BlockSpec((tm, w), lambda i, j: (i, 0))
    return pl.pallas_call(
        body, grid=(T // tm, nf),
        in_specs=[tok(D), _const_spec((1, D)), _up_spec(w_up, tf), pl.BlockSpec((tf, D), lambda i, j: (j, 0)), tok(D)],
        out_specs=[tok(D), tok(D), pl.BlockSpec((1, 8, 128), lambda i, j: (i, 0, 0))],
        out_shape=[_sds((T, D), BF16), _sds((T, D), F32), _sds((T // tm, 8, 128), F32)],
        scratch_shapes=[pltpu.VMEM((tm, D), F32)],
        compiler_params=_params(("parallel", "arbitrary")), name="mlp_fwd",
    )(h2, w_mn, w_up, w_down, target)


def _mlp_bwd(h2, w_mn, hn, w_up, w_down, dy):
    T, D = h2.shape
    FF = w_down.shape[0]
    tm, tf = min(MLP_TOKEN_TILE, T), min(FF_TILE, w_up.shape[2])
    nf = FF // tf

    def body(h_ref, wn_ref, hn_ref, wu_ref, wd_ref, dy_ref, dh_ref, act_ref, dup_ref, dwn_ref, acc_ref):
        i, j = pl.program_id(0), pl.program_id(1)

        @pl.when((i == 0) & (j == 0))
        def _():
            dwn_ref[...] = jnp.zeros_like(dwn_ref)

        @pl.when(j == 0)
        def _():
            acc_ref[...] = jnp.zeros_like(acc_ref)

        r = jnp.maximum(jnp.dot(hn_ref[...], wu_ref[...], preferred_element_type=F32), 0.0)
        act_ref[...] = (r * r).astype(BF16)
        d_act = _dg(dy_ref[...].astype(BF16), wd_ref[...], 1, 1, None)
        d_up = (d_act * (2.0 * r)).astype(BF16)
        dup_ref[...] = d_up
        acc_ref[...] += _dg(d_up, wu_ref[...], 1, 1, None)

        @pl.when(j == nf - 1)
        def _():
            _, pull = jax.vjp(_rms, h_ref[...], wn_ref[...])
            dh, dwn = pull(acc_ref[...])
            dh_ref[...] = dh + dy_ref[...]
            dwn_ref[...] += dwn

    tok = lambda w: pl.BlockSpec((tm, w), lambda i, j: (i, 0))
    ff = pl.BlockSpec((tm, tf), lambda i, j: (i, j))
    return pl.pallas_call(
        body, grid=(T // tm, nf),
        in_specs=[tok(D), _const_spec((1, D)), tok(D), _up_spec(w_up, tf), pl.BlockSpec((tf, D), lambda i, j: (j, 0)), tok(D)],
        out_specs=[tok(D), ff, ff, _const_spec((1, D))],
        out_shape=[_sds((T, D), F32), _sds((T, FF), BF16), _sds((T, FF), BF16), _sds((1, D), F32)],
        scratch_shapes=[pltpu.VMEM((tm, D), F32)],
        compiler_params=_params(("arbitrary", "arbitrary")), name="mlp_bwd",
    )(h2, w_mn, hn, w_up, w_down, dy)


def _rope_pad(a):
    z = jnp.zeros(a.shape[:-1] + (ROPE_HALF,), a.dtype)
    return jnp.concatenate([a[..., :ROPE_HALF], z, a[..., ROPE_HALF:], z], axis=-1)


def _rope_unpad(a):
    return jnp.concatenate([a[..., :ROPE_HALF], a[..., 2 * ROPE_HALF:3 * ROPE_HALF]], axis=-1)


_G0 = 2 * LORA + ROPE_DIM
W_IN_COLS = _G0 + GQKV_W + GZ_W + 2 * HEADS


def _widen_w_in_t(w_t):
    z = jnp.zeros((ROPE_HALF, w_t.shape[1]), w_t.dtype)
    pad = jnp.zeros((GAB_W - 2 * HEADS, w_t.shape[1]), w_t.dtype)
    return jnp.concatenate([w_t[:2 * LORA + ROPE_HALF], z, w_t[2 * LORA + ROPE_HALF:_G0], z, w_t[_G0:], pad], axis=0)


def _narrow_w_in_t(w_t):
    return jnp.concatenate([w_t[:2 * LORA + ROPE_HALF], w_t[2 * LORA + 2 * ROPE_HALF:2 * LORA + 3 * ROPE_HALF],
                            w_t[LAT_W:LAT_W + W_IN_COLS - _G0]], axis=0)


def _stack_mla(w_uq, w_ukv):
    uq = w_uq.reshape(LORA, HEADS, QK_DIM)
    ukv = w_ukv.reshape(LORA, HEADS, 2 * HEAD_DIM)
    parts = [uq[:, :, :HEAD_DIM], _rope_pad(uq[:, :, HEAD_DIM:]), ukv[:, :, :HEAD_DIM], ukv[:, :, HEAD_DIM:]]
    return jnp.concatenate([p.transpose(1, 0, 2) for p in parts], axis=0)


def _unstack_mla(w):
    p = [w[i * HEADS:(i + 1) * HEADS].transpose(1, 0, 2) for i in range(4)]
    uq = jnp.concatenate([p[0], _rope_unpad(p[1])], axis=-1).reshape(LORA, HEADS * QK_DIM)
    ukv = jnp.concatenate([p[2], p[3]], axis=-1).reshape(LORA, HEADS * 2 * HEAD_DIM)
    return uq, ukv


def _rows8(rows):
    a = jnp.concatenate(rows, axis=0)
    return jnp.pad(a, ((0, 8 - a.shape[0]), (0, 0)))


def _qk_norm_rows(q_norm_w, k_norm_w):
    return _rows8([q_norm_w[:, :HEAD_DIM], _rope_pad(q_norm_w[:, HEAD_DIM:]), k_norm_w[:, :HEAD_DIM], _rope_pad(k_norm_w[:, HEAD_DIM:])])


def _rope_rows():
    inv_freq = ROPE_THETA ** (-jnp.arange(ROPE_HALF, dtype=F32) / ROPE_HALF)
    z = jnp.zeros((ROPE_HALF,), F32)
    freq = jnp.concatenate([inv_freq, z, inv_freq, z])
    sign = jnp.concatenate([-jnp.ones((ROPE_HALF,), F32), z, jnp.ones((ROPE_HALF,), F32), z])
    return _rows8([freq[None], sign[None]])


def _round_up(n, m):
    return -(-n // m) * m


def _column_shards(a):
    return a.reshape(a.shape[0], 4, a.shape[1] // 4).transpose(1, 0, 2)


def _from_column_shards(a):
    return a.transpose(1, 0, 2).reshape(a.shape[1], 4 * a.shape[2])


def _pack_small(arrays):
    rows = [jnp.pad(a.reshape(-1), (0, _round_up(a.size, 128) - a.size)).reshape(-1, 128) for a in arrays]
    packed = jnp.concatenate(rows, axis=0)
    return jnp.pad(packed, ((0, _round_up(packed.shape[0], 8) - packed.shape[0]), (0, 0)))


def _unpack_small(packed, shapes):
    out, r = [], 0
    for s in shapes:
        n = math.prod(s)
        nr = _round_up(n, 128) // 128
        out.append(packed[r:r + nr].reshape(-1)[:n].reshape(s))
        r += nr
    return out


_ANY = pl.BlockSpec(memory_space=pl.ANY)
_OTHER_CHIPS = ((1, 0), (0, 1), (1, 1))


def _here():
    return lax.axis_index("x"), lax.axis_index("y"), lax.axis_index("c")


def _flip(v, bit):
    return 1 - v if bit else v


def _remote(src, dst, send_sems, recv_sems, k, to):
    return pltpu.make_async_remote_copy(src_ref=src, dst_ref=dst, send_sem=send_sems.at[k], recv_sem=recv_sems.at[k],
                                        device_id=to, device_id_type=MESH)


def _half_of(ref, k, shape):
    r, c = shape
    if (r // 2) % 16 == 0:
        return ref.at[pl.ds(pl.multiple_of(k * (r // 2), 16), r // 2)]
    if (c // 2) % 128 == 0:
        return ref.at[:, pl.ds(pl.multiple_of(k * (c // 2), 128), c // 2)]
    return None


def _gather_copies(srcs, dsts, send_sems, recv_sems, local_sems):
    x, y, c = _here()
    slot, sibling, n = 2 * x + y, (x, y, 1 - c), len(srcs)
    starts, passes, waits = [], [], []
    for i, (src, dst) in enumerate(zip(srcs, dsts)):
        own = pltpu.make_async_copy(src, dst.at[slot], local_sems.at[i])
        starts.append(own.start)
        waits.append(own.wait)
        halves = _half_of(src, c, src.shape) is not None
        for j, (fx, fy) in enumerate(_OTHER_CHIPS):
            cx, cy = _flip(x, fx), _flip(y, fy)
            there = dst.at[2 * cx + cy]
            if halves:
                push = _remote(_half_of(src, c, src.shape), _half_of(dst.at[slot], c, src.shape), send_sems, recv_sems, 3 * i + j, (cx, cy, c))
                landed, other = _half_of(there, c, src.shape), _half_of(there, 1 - c, src.shape)
                onward = _remote(landed, landed, send_sems, recv_sems, 3 * n + 3 * i + j, sibling)
                passes += [_remote(landed, landed, send_sems, recv_sems, 3 * i + j, (cx, cy, c)).wait_recv, onward.start]
                waits += [_remote(other, other, send_sems, recv_sems, 3 * n + 3 * i + j, sibling).wait_recv, onward.wait_send]
            else:
                push = _remote(src, dst.at[slot], send_sems, recv_sems, 3 * i + j, (cx, cy, c))
                waits.append(_remote(there, there, send_sems, recv_sems, 3 * i + j, (cx, cy, c)).wait_recv)
            starts.append(push.start)
            waits.append(push.wait_send)
    return starts, passes, waits


def _gather_scratch(n):
    return [pltpu.SemaphoreType.DMA((6 * n,)), pltpu.SemaphoreType.DMA((6 * n,)), pltpu.SemaphoreType.DMA((n,))]


def _all_gather(shards, name):
    ns = len(shards)

    def body(*refs):
        starts, passes, waits = _gather_copies(refs[:ns], refs[ns:2 * ns], *refs[2 * ns:])
        for call in starts + passes + waits:
            call()

    return pl.pallas_call(
        body, in_specs=[_ANY] * ns, out_specs=[_ANY] * ns, out_shape=[_sds((4,) + s.shape, s.dtype) for s in shards],
        scratch_shapes=_gather_scratch(ns), name=name,
    )(*shards)


def _by_lanes(shape):
    return (shape[-2] // 2) % 16 != 0


def _scattered_shape(p):
    r, c = p.shape[1:]
    return _sds((8, r, c // 2) if _by_lanes(p.shape) else (8, r // 2, c), p.dtype)


def _scatter_copies(srcs, dsts, send_sems, recv_sems, local_sems, whole=0):
    x, y, c = _here()
    me = 4 * x + 2 * y + c
    starts, waits = [], []
    for i, (src, dst) in enumerate(zip(srcs, dsts)):
        def piece(px, py, pc, src=src, entire=i >= len(srcs) - whole):
            if entire:
                return src
            if _by_lanes(src.shape):
                half = src.shape[2] // 2
                return src.at[2 * px + py, :, pl.ds(pl.multiple_of(pc * half, 128), half)]
            half = src.shape[1] // 2
            return src.at[2 * px + py, pl.ds(pl.multiple_of(pc * half, 16), half)]

        own = pltpu.make_async_copy(piece(x, y, c), dst.at[me], local_sems.at[i])
        starts.append(own.start)
        waits.append(own.wait)
        for k in range(1, 8):
            px, py, pc = _flip(x, k & 4), _flip(y, k & 2), _flip(c, k & 1)
            push = _remote(piece(px, py, pc), dst.at[me], send_sems, recv_sems, 7 * i + k - 1, (px, py, pc))
            landed = dst.at[4 * px + 2 * py + pc]
            starts.append(push.start)
            waits += [_remote(landed, landed, send_sems, recv_sems, 7 * i + k - 1, (px, py, pc)).wait_recv, push.wait_send]
    return starts, waits


def _scatter_scratch(n):
    return [pltpu.SemaphoreType.DMA((7 * n,)), pltpu.SemaphoreType.DMA((7 * n,)), pltpu.SemaphoreType.DMA((n,))]


def _scatter(partials, wholes, name):
    ns = len(partials) + len(wholes)

    def body(*refs):
        starts, waits = _scatter_copies(refs[:ns], refs[ns:2 * ns], *refs[2 * ns:], whole=len(wholes))
        for call in starts + waits:
            call()

    return pl.pallas_call(
        body, in_specs=[_ANY] * ns, out_specs=[_ANY] * ns,
        out_shape=[_scattered_shape(p) for p in partials] + [_sds((8,) + s.shape, s.dtype) for s in wholes],
        scratch_shapes=_scatter_scratch(ns), name=name,
    )(*partials, *wholes)


def _swapped_shape(half):
    r, c = half.shape
    return _sds((r, 2 * c) if _by_lanes((r, 2 * c)) else (2, r, c), half.dtype)


def _swap_copies(srcs, dsts, send_sems, recv_sems, local_sems):
    x, y, c = _here()
    sibling = (x, y, 1 - c)
    starts, waits = [], []
    for i, (src, dst) in enumerate(zip(srcs, dsts)):
        if len(dst.shape) == 2:
            lanes = src.shape[1]
            mine, other = (dst.at[:, pl.ds(pl.multiple_of(k * lanes, 128), lanes)] for k in (c, 1 - c))
        else:
            mine, other = dst.at[c], dst.at[1 - c]
        own = pltpu.make_async_copy(src, mine, local_sems.at[i])
        push = _remote(src, mine, send_sems, recv_sems, i, sibling)
        starts += [own.start, push.start]
        waits += [_remote(other, other, send_sems, recv_sems, i, sibling).wait_recv, push.wait_send, own.wait]
    return starts, waits


def _swap_scratch(n):
    return [pltpu.SemaphoreType.DMA((n,)), pltpu.SemaphoreType.DMA((n,)), pltpu.SemaphoreType.DMA((n,))]


def _exchange_halves(halves):
    ns = len(halves)

    def body(*refs):
        starts, waits = _swap_copies(refs[:ns], refs[ns:2 * ns], *refs[2 * ns:])
        for call in starts + waits:
            call()

    return pl.pallas_call(
        body, in_specs=[_ANY] * ns, out_specs=[_ANY] * ns, out_shape=[_swapped_shape(h) for h in halves],
        scratch_shapes=_swap_scratch(ns), name="exchange_halves",
    )(*halves)


def _row_tile(rows, row_bytes, budget):
    tr = rows
    while tr * row_bytes > budget and tr % 16 == 0:
        tr //= 2
    return tr


def _sum_slots(parts, name):
    _, rows, cols = parts.shape
    tr = _row_tile(rows, 8 * cols * 4, 2 * 1024 * 1024)

    def body(p_ref, o_ref):
        acc = p_ref[0].astype(F32)
        for d in range(1, 8):
            acc = acc + p_ref[d].astype(F32)
        o_ref[...] = acc

    return pl.pallas_call(
        body, grid=(rows // tr,), in_specs=[pl.BlockSpec((8, tr, cols), lambda i: (0, i, 0))],
        out_specs=pl.BlockSpec((tr, cols), lambda i: (i, 0)), out_shape=_sds((rows, cols), F32),
        compiler_params=_params(("parallel",)), name=name,
    )(parts)


def _adamw(w, g, m, v, name):
    rows, cols = w.shape[0], w.shape[-1]
    if w.ndim == 3:
        tr = max(d for d in range(1, rows + 1) if rows % d == 0 and d * 8 * cols * 4 * 14 <= VMEM_LIMIT // 2)
    else:
        tr = _row_tile(rows, 7 * cols * 4, 4 * 1024 * 1024)

    def body(w_ref, g_ref, m_ref, v_ref, d_ref, mo_ref, vo_ref):
        g = g_ref[...]
        m = ADAM_B1 * m_ref[...] + (1.0 - ADAM_B1) * g
        v = ADAM_B2 * v_ref[...] + (1.0 - ADAM_B2) * jnp.square(g)
        m_hat = m / (1.0 - ADAM_B1 ** ADAM_STEP)
        v_hat = v / (1.0 - ADAM_B2 ** ADAM_STEP)
        d_ref[...] = -ADAM_LR * (m_hat / (jnp.sqrt(v_hat) + ADAM_EPS) + ADAM_WD * w_ref[...])
        mo_ref[...] = m
        vo_ref[...] = v

    block = (tr,) + w.shape[1:]
    spec = pl.BlockSpec(block, lambda i: (i,) + (0,) * (len(block) - 1))
    return pl.pallas_call(
        body, grid=(rows // tr,), in_specs=[spec] * 4, out_specs=[spec] * 3, out_shape=[_sds(w.shape, F32)] * 3,
        compiler_params=_params(("parallel",)), name=name,
    )(w, g, m, v)


def kernel(x, positions, attn_norm_w, w_in, q_lat_norm_w, w_uq, kv_lat_norm_w, w_ukv, q_norm_w, k_norm_w, mla_out_norm_w, conv_w, a_log, dt_bias, gdn_norm_w, w_out, mlp_norm_w, w_up, w_down, loss_target, m_attn_norm_w, m_w_in, m_q_lat_norm_w, m_w_uq, m_kv_lat_norm_w, m_w_ukv, m_q_norm_w, m_k_norm_w, m_mla_out_norm_w, m_conv_w, m_a_log, m_dt_bias, m_gdn_norm_w, m_w_out, m_mlp_norm_w, m_w_up, m_w_down, v_attn_norm_w, v_w_in, v_q_lat_norm_w, v_w_uq, v_kv_lat_norm_w, v_w_ukv, v_q_norm_w, v_k_norm_w, v_mla_out_norm_w, v_conv_w, v_a_log, v_dt_bias, v_gdn_norm_w, v_w_out, v_mlp_norm_w, v_w_up, v_w_down):
    w = dict(zip(WEIGHTS, (attn_norm_w, w_in, q_lat_norm_w, w_uq, kv_lat_norm_w, w_ukv, q_norm_w, k_norm_w, mla_out_norm_w, conv_w,
                           a_log, dt_bias, gdn_norm_w, w_out, mlp_norm_w, w_up, w_down)))
    m = dict(zip(WEIGHTS, (m_attn_norm_w, m_w_in, m_q_lat_norm_w, m_w_uq, m_kv_lat_norm_w, m_w_ukv, m_q_norm_w, m_k_norm_w,
                           m_mla_out_norm_w, m_conv_w, m_a_log, m_dt_bias, m_gdn_norm_w, m_w_out, m_mlp_norm_w, m_w_up, m_w_down)))
    v = dict(zip(WEIGHTS, (v_attn_norm_w, v_w_in, v_q_lat_norm_w, v_w_uq, v_kv_lat_norm_w, v_w_ukv, v_q_norm_w, v_k_norm_w,
                           v_mla_out_norm_w, v_conv_w, v_a_log, v_dt_bias, v_gdn_norm_w, v_w_out, v_mlp_norm_w, v_w_up, v_w_down)))
    B, S, D = x.shape
    T = B * S
    x2, pos, target = x.reshape(T, D), positions.reshape(T, 1), loss_target.reshape(T, D)
    seq = lambda a: a.reshape(B, S, a.shape[-1])
    tok = lambda a: a.reshape(T, a.shape[-1])
    local = {n: w[n][0] for n in SHARDED}

    g_in, g_uq, g_ukv, g_conv = _all_gather([jnp.swapaxes(w_in, 1, 2)[0].astype(BF16), local["w_uq"].astype(BF16),
                                             local["w_ukv"].astype(BF16), local["conv_w"]], "gather_first_weights")
    w_in_p = _widen_w_in_t(g_in.reshape(-1, D))
    w_mla = _stack_mla(_from_column_shards(g_uq), _from_column_shards(g_ukv))
    conv_full = _from_column_shards(g_conv)
    ln_w = jnp.concatenate([q_lat_norm_w, kv_lat_norm_w], axis=0)
    qk_nw = _qk_norm_rows(q_norm_w, k_norm_w)
    rope_rows = _rope_rows()
    scal = _rows8([jnp.pad(a_log, ((0, 0), (0, 128 - HEADS))), jnp.pad(dt_bias, ((0, 0), (0, 128 - HEADS)))])
    mix_nw = _rows8([mla_out_norm_w[0], gdn_norm_w])

    xn, lat, gqkv, gz, gab = _in_proj_fwd(x2, attn_norm_w, w_in_p)
    q, k, v_att = _mla_pre_fwd(lat, pos, ln_w, w_mla, qk_nw, rope_rows)
    ao, lse, g_down = _attn_fwd(seq(q), seq(k), seq(v_att), [local["w_down"].astype(BF16)])
    gq, gk, gv = _gdn_pre_fwd(seq(gqkv), conv_full)
    go, states, g_out, w_up_b = _gdn_chunk_fwd(gq, gk, gv, seq(gab), scal, [local["w_out"].astype(BF16), local["w_up"].astype(BF16)])
    w_out_b = g_out.reshape(-1, D)
    w_down_b = g_down.reshape(-1, D)
    mix, h2 = _mix_fwd(tok(ao), tok(go), gz, mix_nw, w_out_b, x2)
    hn, dy, sq = _mlp_fwd(h2, mlp_norm_w, w_up_b, w_down_b, target)
    loss = lax.psum(jnp.sum(sq[:, 0, 0]) * (0.5 / D), ("x", "y", "c"))

    dh, act, d_up, d_mlp_nw = _mlp_bwd(h2, mlp_norm_w, hn, w_up_b, w_down_b, dy)
    p_down = _wgrad(act, dy, "wgrad_down").reshape(4, -1, D)
    p_up = _wgrad(hn, d_up, "wgrad_up", column_shards=4)
    d_ao, d_go, d_gz, d_mix_nw = _mix_bwd(tok(ao), tok(go), gz, mix_nw, w_out_b, dh)
    p_out = _wgrad(mix, dh, "wgrad_out").reshape(4, -1, D)
    d_gq, d_gk, d_gv, d_gab, d_scal, s_up, s_out = _gdn_chunk_bwd(gq, gk, gv, seq(gab), scal, states, seq(d_go), [p_up, p_out])
    early = ("w_up", "w_out", "w_down")
    dxq, dxk, dxv, dcq, dck, dcv, g_up, g_out = _gdn_pre_bwd(seq(gqkv), conv_full, d_gq, d_gk, d_gv,
                                                             [_sum_slots(s_up, "sum_w_up"), _sum_slots(s_out, "sum_w_out")])
    dq, dk, dv, s_down = _attn_bwd(seq(q), seq(k), seq(v_att), ao, lse, seq(d_ao), [p_down])
    d_lat, d_ln, d_w_mla, d_qk_nw, g_down = _mla_pre_bwd(lat, pos, ln_w, w_mla, qk_nw, rope_rows, tok(dq), tok(dk), tok(dv),
                                                         [_sum_slots(s_down, "sum_w_down")])
    early_grads = [g_up, g_out, g_down]
    grad_x2, d_proj, d_attn_nw = _in_proj_bwd([d_lat, tok(dxq), tok(dxk), tok(dxv), d_gz, tok(d_gab)], w_in_p, x2, attn_norm_w, dh)
    p_in = _narrow_w_in_t(_wgrad(d_proj, xn, "wgrad_in")).reshape(4, -1, D)
    p_uq, p_ukv = (_column_shards(a).astype(BF16) for a in _unstack_mla(d_w_mla))
    small_partial = {
        "attn_norm_w": d_attn_nw, "q_lat_norm_w": d_ln[0:1], "kv_lat_norm_w": d_ln[1:2],
        "q_norm_w": jnp.concatenate([d_qk_nw[0:1], _rope_unpad(d_qk_nw[1:2])], axis=-1),
        "k_norm_w": jnp.concatenate([d_qk_nw[2:3], _rope_unpad(d_qk_nw[3:4])], axis=-1),
        "mla_out_norm_w": d_mix_nw[None, :HEADS], "a_log": d_scal[0:1, :HEADS], "dt_bias": d_scal[1:2, :HEADS],
        "gdn_norm_w": d_mix_nw[HEADS:HEADS + 1], "mlp_norm_w": d_mlp_nw,
    }
    conv_partial = jnp.concatenate([dcq, dck, dcv], axis=-1)
    s_in, s_uq, s_ukv, s_small = _scatter([p_in, p_uq, p_ukv], [_pack_small([small_partial[n] for n in SMALL] + [conv_partial])],
                                          "scatter_last_partials")

    late = ("w_in", "w_uq", "w_ukv")
    late_grads = _exchange_halves([_sum_slots(s, "sum_" + n) for n, s in zip(late, (s_in, s_uq, s_ukv))])
    names = early + late
    grad = {n: g.reshape(-1, g.shape[-1]) for n, g in zip(names, list(early_grads) + list(late_grads))}
    small_shapes = [w[n].shape for n in SMALL]
    *g_small, g_conv_all = _unpack_small(_sum_slots(s_small, "sum_small"), small_shapes + [conv_partial.shape])
    grad.update(zip(SMALL, g_small))
    conv_cols = local["conv_w"].shape[1]
    grad["conv_w"] = lax.dynamic_slice_in_dim(g_conv_all, (2 * lax.axis_index("x") + lax.axis_index("y")) * conv_cols, conv_cols, axis=1)

    delta, new_m, new_v = {}, {}, {}
    for n in names:
        if n == "w_in":
            stored = lambda a: jnp.transpose(a, (2, 0, 1))
            outs = _adamw(stored(w[n]), grad[n][:, None, :], stored(m[n]), stored(v[n]), "adamw_" + n)
            grad[n], delta[n], new_m[n], new_v[n] = (jnp.transpose(a, (1, 2, 0)) for a in (grad[n][:, None, :], *outs))
        else:
            delta[n], new_m[n], new_v[n] = _adamw(local[n], grad[n], m[n][0], v[n][0], "adamw_" + n)
    packed_names = SMALL + ("conv_w",)
    packed_shapes = small_shapes + [local["conv_w"].shape]
    take = lambda d: _pack_small([d[n][0] if n == "conv_w" and d[n].ndim == 3 else d[n] for n in packed_names])
    outs = _adamw(take(w), take(grad), take(m), take(v), "adamw_small")
    for d, packed in zip((delta, new_m, new_v), outs):
        d.update(zip(packed_names, _unpack_small(packed, packed_shapes)))

    def in_order(d):
        return [d[n].reshape(w[n].shape) for n in WEIGHTS]

    return (loss, grad_x2.reshape(B, S, D), *in_order(grad), *in_order(delta), *in_order(new_m), *in_order(new_v))
```

```python
import functools
import math

import jax
import jax.numpy as jnp
from jax import lax
from jax.experimental import pallas as pl
from jax.experimental.pallas import tpu as pltpu

F32 = jnp.float32
BF16 = jnp.bfloat16
MESH = pl.DeviceIdType.MESH

EPS = 1e-6
HEADS = 4
HEAD_DIM = 128
ROPE_DIM = 64
ROPE_HALF = 32
QK_DIM = 192
QK_PAD = 256
LORA = 256
CHUNK = 64
CONV_TAPS = 4
ROPE_THETA = 10000.0
ATTN_SCALE = QK_DIM ** -0.5

LAT_W = 640
GQKV_W = 3 * HEADS * HEAD_DIM
GZ_W = HEADS * HEAD_DIM
GAB_W = 128
PROJ_SPLITS = ((0, LAT_W), (LAT_W, LAT_W + GQKV_W), (LAT_W + GQKV_W, LAT_W + GQKV_W + GZ_W),
               (LAT_W + GQKV_W + GZ_W, LAT_W + GQKV_W + GZ_W + GAB_W))
PROJ_W = PROJ_SPLITS[-1][1]

ADAM_LR = 0.001
ADAM_B1 = 0.9
ADAM_B2 = 0.999
ADAM_EPS = 1e-08
ADAM_WD = 0.01
ADAM_STEP = 10

TOKEN_TILE = 512
MLP_TOKEN_TILE = 1024
FF_TILE = 512
ATTN_TILE = 512
ATTN_HEADS_PER_STEP = 2
WGRAD_OUT_BYTES = 8 * 1024 * 1024
VMEM_LIMIT = 48 * 1024 * 1024

SHARDED = ("w_in", "w_uq", "w_ukv", "conv_w", "w_out", "w_up", "w_down")
WEIGHTS = ("attn_norm_w", "w_in", "q_lat_norm_w", "w_uq", "kv_lat_norm_w", "w_ukv", "q_norm_w", "k_norm_w", "mla_out_norm_w",
           "conv_w", "a_log", "dt_bias", "gdn_norm_w", "w_out", "mlp_norm_w", "w_up", "w_down")


def _sds(shape, dtype):
    return jax.ShapeDtypeStruct(shape, dtype)


def _params(semantics):
    return pltpu.CompilerParams(dimension_semantics=semantics, vmem_limit_bytes=VMEM_LIMIT)


def _block(n):
    for b in (512, 256, 128):
        if n % b == 0:
            return b
    return n


def _dg(a, b, ca, cb, prec):
    lead = a.ndim - 2
    batch = (tuple(range(lead)),) * 2
    return lax.dot_general(a, b, (((ca + lead,), (cb + lead,)), batch), precision=prec, preferred_element_type=F32)


def _split_bf16(a):
    hi = a.astype(BF16)
    return hi, (a - hi.astype(F32)).astype(BF16)


def _dot_bf16(a, b, ca, cb):
    return _dg(a.astype(BF16), b.astype(BF16), ca, cb, None)


def _dot_bf16x3(a, b, ca, cb):
    a_hi, a_lo = _split_bf16(a)
    b_hi, b_lo = _split_bf16(b)
    lead = a.ndim - 2
    return _dg(jnp.concatenate([a_hi, a_hi, a_lo], axis=ca + lead), jnp.concatenate([b_hi, b_lo, b_hi], axis=cb + lead), ca, cb, None)


def _matmul_family(dot):
    def nn_raw(a, b):
        return dot(a, b, 1, 0)

    def nt_raw(a, b):
        return dot(a, b, 1, 1)

    def tn_raw(a, b):
        return dot(a, b, 0, 0)

    @jax.custom_vjp
    def nn(a, b):
        return nn_raw(a, b)

    nn.defvjp(lambda a, b: (nn_raw(a, b), (a, b)), lambda r, g: (nt_raw(g, r[1]), tn_raw(r[0], g)))

    @jax.custom_vjp
    def nt(a, b):
        return nt_raw(a, b)

    nt.defvjp(lambda a, b: (nt_raw(a, b), (a, b)), lambda r, g: (nn_raw(g, r[1]), tn_raw(g, r[0])))

    @jax.custom_vjp
    def tn(a, b):
        return tn_raw(a, b)

    tn.defvjp(lambda a, b: (tn_raw(a, b), (a, b)), lambda r, g: (nt_raw(r[1], g), nn_raw(r[0], g)))
    return nn, nt, tn


_bf_nn, _bf_nt, _bf_tn = _matmul_family(_dot_bf16)
_hi_nn, _hi_nt, _hi_tn = _matmul_family(_dot_bf16x3)


def _lower_powers(lmat):
    powers = []
    while 2 ** (len(powers) + 1) < lmat.shape[-1]:
        powers.append(_dot_bf16x3(powers[-1] if powers else lmat, powers[-1] if powers else lmat, 1, 0))
    return powers


@jax.custom_vjp
def _unit_lower_solve(lmat, rhs):
    return _unit_lower_solve_fwd(lmat, rhs)[0]


def _unit_lower_solve_fwd(lmat, rhs):
    powers = _lower_powers(lmat)
    x = rhs - _dot_bf16x3(lmat, rhs, 1, 0)
    for p in powers:
        x = x + _dot_bf16x3(p, x, 1, 0)
    return x, (lmat, powers, x)


def _unit_lower_solve_bwd(res, g):
    lmat, powers, x = res
    y = g - _dot_bf16x3(lmat, g, 0, 0)
    for p in powers:
        y = y + _dot_bf16x3(p, y, 0, 0)
    return -_dot_bf16x3(y, x, 1, 1), y


_unit_lower_solve.defvjp(_unit_lower_solve_fwd, _unit_lower_solve_bwd)


@jax.custom_vjp
def _lane_halves(x):
    n = x.shape[-1] // 2
    return x[..., :n], x[..., n:]


_lane_halves.defvjp(lambda x: (_lane_halves(x), None), lambda _, g: (jnp.concatenate(g, axis=-1),))


@jax.custom_vjp
def _row_halves(x):
    n = x.shape[-2] // 2
    return x[..., :n, :], x[..., n:, :]


_row_halves.defvjp(lambda x: (_row_halves(x), None), lambda _, g: (jnp.concatenate(g, axis=-2),))


@jax.custom_vjp
def _swap_halves(t):
    return pltpu.roll(t, 64, 1)


_swap_halves.defvjp(lambda t: (pltpu.roll(t, 64, 1), None), lambda _, g: (pltpu.roll(g, 64, 1),))


@functools.partial(jax.custom_vjp, nondiff_argnums=(2,))
def _shift_rows(x, keep, s):
    return pltpu.roll(x, s, 0) * keep


def _shift_rows_fwd(x, keep, s):
    return pltpu.roll(x, s, 0) * keep, keep


def _shift_rows_bwd(s, keep, g):
    return pltpu.roll(g * keep, keep.shape[0] - s, 0), jnp.zeros_like(keep)


_shift_rows.defvjp(_shift_rows_fwd, _shift_rows_bwd)


def _sigmoid(x):
    return 0.5 * jnp.tanh(0.5 * x) + 0.5


def _softplus(x):
    return jnp.maximum(x, 0.0) + jnp.log(1.0 + jnp.exp(jnp.minimum(x, -x)))


def _silu(x):
    return x * _sigmoid(x)


def _rms(x, w, n=None):
    n = x.shape[-1] if n is None else n
    r = lax.rsqrt(jnp.sum(x * x, axis=-1, keepdims=True) * (1.0 / n) + EPS)
    return x * r * w


def _rope(t, cos_f, sin_f):
    return t * cos_f + _swap_halves(t) * sin_f


def _rope_tables(pos_col, freq_row, sign_row):
    ang = pos_col.astype(F32) * freq_row
    return jnp.cos(ang), jnp.sin(ang) * sign_row


def _onehot_row(lane):
    return (lax.broadcasted_iota(jnp.int32, (1, 128), 1) == lane).astype(F32)


def _row_spec(tm, w):
    return pl.BlockSpec((tm, w), lambda i: (i, 0))


def _const_spec(shape):
    return pl.BlockSpec(shape, lambda *_: (0,) * len(shape))


def _in_proj_fwd(x2, w_an, w_in_p):
    T, D = x2.shape
    tm = min(TOKEN_TILE, T)

    def body(x_ref, wn_ref, w_ref, xn_ref, lat_ref, gqkv_ref, gz_ref, gab_ref):
        x = x_ref[...]
        r = lax.rsqrt(jnp.mean(x * x, axis=-1, keepdims=True) + EPS)
        xn = (x * r * wn_ref[...]).astype(BF16)
        xn_ref[...] = xn
        for ref, (a, b) in zip((lat_ref, gqkv_ref, gz_ref, gab_ref), PROJ_SPLITS):
            ref[...] = _dg(xn, w_ref[a:b, :], 1, 1, None)

    widths = [b - a for a, b in PROJ_SPLITS]
    return pl.pallas_call(
        body, grid=(T // tm,),
        in_specs=[_row_spec(tm, D), _const_spec((1, D)), _const_spec((PROJ_W, D))],
        out_specs=[_row_spec(tm, D)] + [_row_spec(tm, w) for w in widths],
        out_shape=[_sds((T, D), BF16)] + [_sds((T, w), F32) for w in widths],
        compiler_params=_params(("parallel",)), name="in_proj_fwd",
    )(x2, w_an, w_in_p)


def _in_proj_bwd(pieces, w_in_p, x2, w_an, dh):
    T, D = x2.shape
    tm = min(TOKEN_TILE, T)
    widths = [p.shape[1] for p in pieces]
    starts = [sum(widths[:i]) for i in range(len(widths))]
    assert sum(widths) == PROJ_W

    def body(*refs):
        piece_refs = refs[:len(pieces)]
        w_ref, x_ref, wn_ref, dh_ref, dx_ref, dp_ref, dwn_ref = refs[len(pieces):]

        @pl.when(pl.program_id(0) == 0)
        def _():
            dwn_ref[...] = jnp.zeros_like(dwn_ref)

        dxn = jnp.zeros((tm, D), F32)
        for ref, a, width in zip(piece_refs, starts, widths):
            piece = ref[...].astype(BF16)
            dp_ref[:, a:a + width] = piece
            dxn += _dg(piece, w_ref[a:a + width, :], 1, 0, None)
        _, pull = jax.vjp(_rms, x_ref[...], wn_ref[...])
        dx, dwn = pull(dxn)
        dx_ref[...] = dx + dh_ref[...]
        dwn_ref[...] += dwn

    return pl.pallas_call(
        body, grid=(T // tm,),
        in_specs=[_row_spec(tm, w) for w in widths] + [_const_spec((PROJ_W, D)), _row_spec(tm, D), _const_spec((1, D)),
                                                       _row_spec(tm, D)],
        out_specs=[_row_spec(tm, D), _row_spec(tm, PROJ_W), _const_spec((1, D))],
        out_shape=[_sds((T, D), F32), _sds((T, PROJ_W), BF16), _sds((1, D), F32)],
        compiler_params=_params(("arbitrary",)), name="in_proj_bwd",
    )(*pieces, w_in_p, x2, w_an, dh)


def _wgrad(a, b, name, column_shards=1, out_dtype=BF16):
    T, k1 = a.shape
    k2 = b.shape[1]
    per_shard = k2 // column_shards
    tt = min(TOKEN_TILE, T)
    b1 = k1
    while b1 * k2 * 4 > WGRAD_OUT_BYTES and b1 % 256 == 0:
        b1 //= 2
    step = _block(per_shard)

    def body(a_ref, b_ref, o_ref, acc_ref):
        t = pl.program_id(1)

        @pl.when(t == 0)
        def _():
            acc_ref[...] = jnp.zeros_like(acc_ref)

        a_t = a_ref[...].astype(BF16).T
        for c0 in range(0, k2, step):
            part = jnp.dot(a_t, b_ref[:, c0:c0 + step].astype(BF16), preferred_element_type=F32)
            if column_shards == 1:
                acc_ref[:, c0:c0 + step] += part
            else:
                acc_ref[c0 // per_shard, :, c0 % per_shard:c0 % per_shard + step] += part

        @pl.when(t == T // tt - 1)
        def _():
            o_ref[...] = acc_ref[...].astype(o_ref.dtype)

    if column_shards == 1:
        block, out_spec, out_shape = (b1, k2), pl.BlockSpec((b1, k2), lambda i, t: (i, 0)), _sds((k1, k2), out_dtype)
    else:
        block = (column_shards, b1, per_shard)
        out_spec, out_shape = pl.BlockSpec(block, lambda i, t: (0, i, 0)), _sds((column_shards, k1, per_shard), out_dtype)
    return pl.pallas_call(
        body, grid=(k1 // b1, T // tt),
        in_specs=[pl.BlockSpec((tt, b1), lambda i, t: (t, i)), pl.BlockSpec((tt, k2), lambda i, t: (t, 0))],
        out_specs=out_spec, out_shape=out_shape, scratch_shapes=[pltpu.VMEM(block, F32)],
        compiler_params=_params(("parallel", "arbitrary")), name=name,
    )(a, b)


def _mla_pre_fn(q_lat, kv_lat, kpe, ln_q, ln_kv, w_list, qn_n, qn_p, kn_n, kn_p, cos_f, sin_f):
    qn = _rms(q_lat, ln_q)
    kvn = _rms(kv_lat, ln_kv)
    kp = _rope(_rms(kpe, kn_p, ROPE_DIM), cos_f, sin_f)
    outs = []
    for h in range(HEADS):
        outs.append(_rms(_bf_nn(qn, w_list[h]), qn_n))
        outs.append(_rope(_rms(_bf_nn(qn, w_list[HEADS + h]), qn_p, ROPE_DIM), cos_f, sin_f))
        outs.append(_rms(_bf_nn(kvn, w_list[2 * HEADS + h]), kn_n))
        outs.append(_bf_nn(kvn, w_list[3 * HEADS + h]))
    return tuple(outs) + (kp,)


def _mla_pre_operands(lat_ref, pos_ref, ln_ref, w_ref, nw_ref, rope_ref):
    cos_f, sin_f = _rope_tables(pos_ref[...], rope_ref[0:1, :], rope_ref[1:2, :])
    diff = (lat_ref[:, 0:LORA], lat_ref[:, LORA:2 * LORA], lat_ref[:, 2 * LORA:LAT_W], ln_ref[0:1, :], ln_ref[1:2, :],
            [w_ref[i].astype(F32) for i in range(4 * HEADS)], nw_ref[0:1, :], nw_ref[1:2, :], nw_ref[2:3, :], nw_ref[3:4, :])
    return diff, cos_f, sin_f


def _mla_pre_fwd(lat, pos, ln_w, w_mla, nw, rope_rows):
    T = lat.shape[0]
    tm = min(TOKEN_TILE, T)

    def body(lat_ref, pos_ref, ln_ref, w_ref, nw_ref, rope_ref, q_ref, k_ref, v_ref):
        diff, cos_f, sin_f = _mla_pre_operands(lat_ref, pos_ref, ln_ref, w_ref, nw_ref, rope_ref)
        outs = _mla_pre_fn(*diff, cos_f, sin_f)
        kp = outs[-1].astype(BF16)
        for h in range(HEADS):
            q_n, q_p, k_n, v = outs[4 * h:4 * h + 4]
            q_ref[:, h * QK_PAD:h * QK_PAD + HEAD_DIM] = q_n.astype(BF16)
            q_ref[:, h * QK_PAD + HEAD_DIM:(h + 1) * QK_PAD] = q_p.astype(BF16)
            k_ref[:, h * QK_PAD:h * QK_PAD + HEAD_DIM] = k_n.astype(BF16)
            k_ref[:, h * QK_PAD + HEAD_DIM:(h + 1) * QK_PAD] = kp
            v_ref[:, h * HEAD_DIM:(h + 1) * HEAD_DIM] = v.astype(BF16)

    return pl.pallas_call(
        body, grid=(T // tm,),
        in_specs=[_row_spec(tm, LAT_W), _row_spec(tm, 1), _const_spec((2, LORA)), _const_spec((4 * HEADS, LORA, 128)),
                  _const_spec((8, 128)), _const_spec((8, 128))],
        out_specs=[_row_spec(tm, HEADS * QK_PAD), _row_spec(tm, HEADS * QK_PAD), _row_spec(tm, HEADS * HEAD_DIM)],
        out_shape=[_sds((T, HEADS * QK_PAD), BF16), _sds((T, HEADS * QK_PAD), BF16), _sds((T, HEADS * HEAD_DIM), BF16)],
        compiler_params=_params(("parallel",)), name="mla_pre_fwd",
    )(lat, pos, ln_w, w_mla, nw, rope_rows)


def _mla_pre_bwd(lat, pos, ln_w, w_mla, nw, rope_rows, dq, dk, dv, halves):
    T = lat.shape[0]
    tm = min(TOKEN_TILE, T)
    ns = len(halves)

    def body(*refs):
        lat_ref, pos_ref, ln_ref, w_ref, nw_ref, rope_ref, dq_ref, dk_ref, dv_ref = refs[:9]
        src_refs = refs[9:9 + ns]
        dlat_ref, dln_ref, dw_ref, dnw_ref = refs[9 + ns:13 + ns]
        dst_refs = refs[13 + ns:13 + 2 * ns]
        sems = refs[13 + 2 * ns:]

        @pl.when(pl.program_id(0) == 0)
        def _():
            for start in _swap_copies(src_refs, dst_refs, *sems)[0]:
                start()
            dln_ref[...] = jnp.zeros_like(dln_ref)
            dw_ref[...] = jnp.zeros_like(dw_ref)
            dnw_ref[...] = jnp.zeros_like(dnw_ref)

        diff, cos_f, sin_f = _mla_pre_operands(lat_ref, pos_ref, ln_ref, w_ref, nw_ref, rope_ref)
        _, pull = jax.vjp(lambda *a: _mla_pre_fn(*a, cos_f, sin_f), *diff)
        cts = []
        d_kp = jnp.zeros((tm, 128), F32)
        for h in range(HEADS):
            cts.append(dq_ref[:, h * QK_PAD:h * QK_PAD + HEAD_DIM])
            cts.append(dq_ref[:, h * QK_PAD + HEAD_DIM:(h + 1) * QK_PAD])
            cts.append(dk_ref[:, h * QK_PAD:h * QK_PAD + HEAD_DIM])
            cts.append(dv_ref[:, h * HEAD_DIM:(h + 1) * HEAD_DIM])
            d_kp += dk_ref[:, h * QK_PAD + HEAD_DIM:(h + 1) * QK_PAD]
        d_ql, d_kvl, d_kpe, d_lnq, d_lnkv, d_w, d_qn_n, d_qn_p, d_kn_n, d_kn_p = pull(tuple(cts) + (d_kp,))
        dlat_ref[:, 0:LORA] = d_ql
        dlat_ref[:, LORA:2 * LORA] = d_kvl
        dlat_ref[:, 2 * LORA:LAT_W] = d_kpe
        dln_ref[0:1, :] += d_lnq
        dln_ref[1:2, :] += d_lnkv
        for i in range(4 * HEADS):
            dw_ref[i] += d_w[i]
        for i, d in enumerate((d_qn_n, d_qn_p, d_kn_n, d_kn_p)):
            dnw_ref[i:i + 1, :] += d

        @pl.when(pl.program_id(0) == T // tm - 1)
        def _():
            for wait in _swap_copies(src_refs, dst_refs, *sems)[1]:
                wait()

    return pl.pallas_call(
        body, grid=(T // tm,),
        in_specs=[_row_spec(tm, LAT_W), _row_spec(tm, 1), _const_spec((2, LORA)), _const_spec((4 * HEADS, LORA, 128)),
                  _const_spec((8, 128)), _const_spec((8, 128)),
                  _row_spec(tm, HEADS * QK_PAD), _row_spec(tm, HEADS * QK_PAD), _row_spec(tm, HEADS * HEAD_DIM)] + [_ANY] * ns,
        out_specs=[_row_spec(tm, LAT_W), _const_spec((2, LORA)), _const_spec((4 * HEADS, LORA, 128)), _const_spec((8, 128))]
                  + [_ANY] * ns,
        out_shape=[_sds((T, LAT_W), F32), _sds((2, LORA), F32), _sds((4 * HEADS, LORA, 128), F32), _sds((8, 128), F32)]
                  + [_swapped_shape(h) for h in halves],
        scratch_shapes=_swap_scratch(ns),
        compiler_params=_params(("arbitrary",)), name="mla_pre_bwd",
    )(lat, pos, ln_w, w_mla, nw, rope_rows, dq, dk, dv, *halves)


def _causal_mask(i, j, tq, tk):
    row = i * tq + lax.broadcasted_iota(jnp.int32, (tq, tk), 0)
    col = j * tk + lax.broadcasted_iota(jnp.int32, (tq, tk), 1)
    return col <= row


def _attn_fwd(q, k, v, shards):
    B, S, _ = q.shape
    t = min(ATTN_TILE, S)
    nq = S // t
    ns = len(shards)

    hp = ATTN_HEADS_PER_STEP
    qk = lambda h: slice(h * QK_PAD, (h + 1) * QK_PAD)
    vd = lambda h: slice(h * HEAD_DIM, (h + 1) * HEAD_DIM)

    def body(*refs):
        q_ref, k_ref, v_ref = refs[:3]
        src_refs = refs[3:3 + ns]
        o_ref, lse_ref = refs[3 + ns:5 + ns]
        dst_refs = refs[5 + ns:5 + 2 * ns]
        sems = refs[5 + 2 * ns:]
        b, g, i = pl.program_id(0), pl.program_id(1), pl.program_id(2)
        qb = [q_ref[0, :, qk(h)] for h in range(hp)]

        step_no = (b * (HEADS // hp) + g) * nq + i
        for phase, at in enumerate((0, (3 * B * (HEADS // hp) * nq) // 4)):
            @pl.when(step_no == at)
            def _(phase=phase):
                for call in _gather_copies(src_refs, dst_refs, *sems)[phase]:
                    call()

        def step(j, carry, diagonal):
            rows = pl.ds(pl.multiple_of(j * t, t), t)
            s = [_dg(qb[h], k_ref[0, rows, qk(h)], 1, 1, None) * ATTN_SCALE for h in range(hp)]
            if diagonal:
                keep = _causal_mask(0, 0, t, t)
                s = [jnp.where(keep, x, -1e30) for x in s]
            m_new = [jnp.maximum(carry[h][0], jnp.max(s[h], axis=-1, keepdims=True)) for h in range(hp)]
            p = [jnp.exp(s[h] - m_new[h]) for h in range(hp)]
            alpha = [jnp.exp(carry[h][0] - m_new[h]) for h in range(hp)]
            l = [alpha[h] * carry[h][1] + jnp.sum(p[h], axis=-1, keepdims=True) for h in range(hp)]
            pv = [jnp.dot(p[h].astype(BF16), v_ref[0, rows, vd(h)], preferred_element_type=F32) for h in range(hp)]
            return tuple((m_new[h], l[h], alpha[h] * carry[h][2] + pv[h]) for h in range(hp))

        init = tuple((jnp.full((t, 1), -1e30, F32), jnp.zeros((t, 1), F32), jnp.zeros((t, HEAD_DIM), F32)) for _ in range(hp))
        below = lax.fori_loop(0, i, lambda j, carry: step(j, carry, False), init)
        for h, (m, l, acc) in enumerate(step(i, below, True)):
            o_ref[0, :, vd(h)] = acc / l
            lse_ref[0, h] = m + jnp.log(l)

        @pl.when((b == B - 1) & (g == HEADS // hp - 1) & (i == nq - 1))
        def _():
            for wait in _gather_copies(src_refs, dst_refs, *sems)[2]:
                wait()

    return pl.pallas_call(
        body, grid=(B, HEADS // hp, nq),
        in_specs=[pl.BlockSpec((1, t, hp * QK_PAD), lambda b, g, i: (b, i, g)),
                  pl.BlockSpec((1, S, hp * QK_PAD), lambda b, g, i: (b, 0, g)),
                  pl.BlockSpec((1, S, hp * HEAD_DIM), lambda b, g, i: (b, 0, g))] + [_ANY] * ns,
        out_specs=[pl.BlockSpec((1, t, hp * HEAD_DIM), lambda b, g, i: (b, i, g)),
                   pl.BlockSpec((1, hp, t, 1), lambda b, g, i: (b, g, i, 0))] + [_ANY] * ns,
        out_shape=[_sds((B, S, HEADS * HEAD_DIM), F32), _sds((B, HEADS, S, 1), F32)] + [_sds((4,) + s.shape, s.dtype) for s in shards],
        scratch_shapes=_gather_scratch(ns),
        compiler_params=_params(("arbitrary", "arbitrary", "arbitrary")), name="attn_fwd",
    )(q, k, v, *shards)


def _attn_bwd(q, k, v, o, lse, do, partials):
    B, S, _ = q.shape
    t = min(ATTN_TILE, S)
    nq = S // t
    ns = len(partials)

    hp = ATTN_HEADS_PER_STEP
    qk = lambda h: slice(h * QK_PAD, (h + 1) * QK_PAD)
    vd = lambda h: slice(h * HEAD_DIM, (h + 1) * HEAD_DIM)
    heads = range(hp)

    def body(*refs):
        q_ref, k_ref, v_ref, o_ref, lse_ref, do_ref = refs[:6]
        src_refs = refs[6:6 + ns]
        dq_ref, dk_ref, dv_ref = refs[6 + ns:9 + ns]
        dst_refs = refs[9 + ns:9 + 2 * ns]
        dsum_ref, send_sems, recv_sems, local_sems = refs[9 + 2 * ns:]
        b, g, j = pl.program_id(0), pl.program_id(1), pl.program_id(2)

        @pl.when((b == 0) & (g == 0) & (j == 0))
        def _():
            for start in _scatter_copies(src_refs, dst_refs, send_sems, recv_sems, local_sems)[0]:
                start()

        @pl.when(j == 0)
        def _():
            dq_ref[...] = jnp.zeros_like(dq_ref)
            for h in heads:
                dsum_ref[h] = jnp.sum(do_ref[0, :, vd(h)] * o_ref[0, :, vd(h)], axis=-1, keepdims=True)

        kb = [k_ref[0, :, qk(h)] for h in heads]
        vb = [v_ref[0, :, vd(h)] for h in heads]

        def step(i, carry, diagonal):
            rows = pl.ds(pl.multiple_of(i * t, t), t)
            qb = [q_ref[0, rows, qk(h)] for h in heads]
            dob = [do_ref[0, rows, vd(h)].astype(BF16) for h in heads]
            s = [_dg(qb[h], kb[h], 1, 1, None) * ATTN_SCALE for h in heads]
            p = [jnp.exp(s[h] - lse_ref[0, h, rows, :]) for h in heads]
            if diagonal:
                keep = _causal_mask(0, 0, t, t)
                p = [jnp.where(keep, x, 0.0) for x in p]
            dp = [_dg(dob[h], vb[h], 1, 1, None) for h in heads]
            dv = [carry[h][1] + _dg(p[h].astype(BF16), dob[h], 0, 0, None) for h in heads]
            ds = [(p[h] * (dp[h] - dsum_ref[h, rows, :]) * ATTN_SCALE).astype(BF16) for h in heads]
            for h in heads:
                dq_ref[0, rows, qk(h)] += jnp.dot(ds[h], kb[h], preferred_element_type=F32)
            return tuple((carry[h][0] + _dg(ds[h], qb[h], 0, 0, None), dv[h]) for h in heads)

        zeros = tuple((jnp.zeros((t, QK_PAD), F32), jnp.zeros((t, HEAD_DIM), F32)) for _ in heads)
        on_diagonal = step(j, zeros, True)
        done = lax.fori_loop(j + 1, nq, lambda i, carry: step(i, carry, False), on_diagonal)
        for h, (dk, dv) in enumerate(done):
            dk_ref[0, :, qk(h)] = dk
            dv_ref[0, :, vd(h)] = dv

        @pl.when((b == B - 1) & (g == HEADS // hp - 1) & (j == nq - 1))
        def _():
            for wait in _scatter_copies(src_refs, dst_refs, send_sems, recv_sems, local_sems)[1]:
                wait()

    return pl.pallas_call(
        body, grid=(B, HEADS // hp, nq),
        in_specs=[pl.BlockSpec((1, S, hp * QK_PAD), lambda b, g, j: (b, 0, g)),
                  pl.BlockSpec((1, t, hp * QK_PAD), lambda b, g, j: (b, j, g)),
                  pl.BlockSpec((1, t, hp * HEAD_DIM), lambda b, g, j: (b, j, g)),
                  pl.BlockSpec((1, S, hp * HEAD_DIM), lambda b, g, j: (b, 0, g)),
                  pl.BlockSpec((1, hp, S, 1), lambda b, g, j: (b, g, 0, 0)),
                  pl.BlockSpec((1, S, hp * HEAD_DIM), lambda b, g, j: (b, 0, g))] + [_ANY] * ns,
        out_specs=[pl.BlockSpec((1, S, hp * QK_PAD), lambda b, g, j: (b, 0, g)),
                   pl.BlockSpec((1, t, hp * QK_PAD), lambda b, g, j: (b, j, g)),
                   pl.BlockSpec((1, t, hp * HEAD_DIM), lambda b, g, j: (b, j, g))] + [_ANY] * ns,
        out_shape=[_sds((B, S, HEADS * QK_PAD), F32), _sds((B, S, HEADS * QK_PAD), F32), _sds((B, S, HEADS * HEAD_DIM), F32)]
                  + [_scattered_shape(p) for p in partials],
        scratch_shapes=[pltpu.VMEM((hp, S, 1), F32)] + _scatter_scratch(ns),
        compiler_params=_params(("arbitrary", "arbitrary", "arbitrary")), name="attn_bwd",
    )(q, k, v, o, lse, do, *partials)


def _gdn_pre_fn(xq, xk, xv, wq, wk, wv, keeps):
    def conv_silu(x, w):
        acc = x * w[3]
        for s in (1, 2, 3):
            acc = acc + _shift_rows(x, keeps[s - 1], s) * w[3 - s]
        return _silu(acc)

    def l2(x):
        return x * lax.rsqrt(jnp.sum(x * x, axis=-1, keepdims=True) + EPS)

    return l2(conv_silu(xq, wq)) * (HEAD_DIM ** -0.5), l2(conv_silu(xk, wk)), conv_silu(xv, wv)


def _gdn_pre_specs(S):
    x_specs = [pl.BlockSpec((1, S, HEAD_DIM), lambda h, b, g=g: (b, 0, g * HEADS + h)) for g in range(3)]
    w_specs = [pl.BlockSpec((CONV_TAPS, HEAD_DIM), lambda h, b, g=g: (0, g * HEADS + h)) for g in range(3)]
    out_spec = pl.BlockSpec((1, S, HEAD_DIM), lambda h, b: (b, 0, h))
    return x_specs, w_specs, out_spec


def _row_keeps(S):
    t = lax.broadcasted_iota(jnp.int32, (S, HEAD_DIM), 0)
    return [(t >= s).astype(F32) for s in (1, 2, 3)]


def _gdn_pre_fwd(gqkv, conv_w):
    B, S, _ = gqkv.shape
    x_specs, w_specs, out_spec = _gdn_pre_specs(S)

    def body(xq_ref, xk_ref, xv_ref, wq_ref, wk_ref, wv_ref, q_ref, k_ref, v_ref):
        taps = [[w[i:i + 1, :] for i in range(CONV_TAPS)] for w in (wq_ref, wk_ref, wv_ref)]
        q, k, v = _gdn_pre_fn(xq_ref[0], xk_ref[0], xv_ref[0], *taps, _row_keeps(S))
        q_ref[0], k_ref[0], v_ref[0] = q, k, v

    return pl.pallas_call(
        body, grid=(HEADS, B), in_specs=x_specs + w_specs, out_specs=[out_spec] * 3,
        out_shape=[_sds((B, S, HEADS * HEAD_DIM), F32)] * 3,
        compiler_params=_params(("parallel", "parallel")), name="gdn_pre_fwd",
    )(gqkv, gqkv, gqkv, conv_w, conv_w, conv_w)


def _gdn_pre_bwd(gqkv, conv_w, dq, dk, dv, halves):
    B, S, _ = gqkv.shape
    x_specs, w_specs, out_spec = _gdn_pre_specs(S)
    dw_spec = pl.BlockSpec((CONV_TAPS, HEAD_DIM), lambda h, b: (0, h))
    ns = len(halves)

    def body(*refs):
        xq_ref, xk_ref, xv_ref, wq_ref, wk_ref, wv_ref, dq_ref, dk_ref, dv_ref = refs[:9]
        src_refs = refs[9:9 + ns]
        dxq_ref, dxk_ref, dxv_ref, dwq_ref, dwk_ref, dwv_ref = refs[9 + ns:15 + ns]
        dst_refs = refs[15 + ns:15 + 2 * ns]
        sems = refs[15 + 2 * ns:]
        first = (pl.program_id(0) == 0) & (pl.program_id(1) == 0)
        last = (pl.program_id(0) == HEADS - 1) & (pl.program_id(1) == B - 1)

        @pl.when(first)
        def _():
            for start in _swap_copies(src_refs, dst_refs, *sems)[0]:
                start()

        @pl.when(pl.program_id(1) == 0)
        def _():
            for r in (dwq_ref, dwk_ref, dwv_ref):
                r[...] = jnp.zeros_like(r)

        taps = [[w[i:i + 1, :] for i in range(CONV_TAPS)] for w in (wq_ref, wk_ref, wv_ref)]
        keeps = _row_keeps(S)
        _, pull = jax.vjp(lambda *a: _gdn_pre_fn(*a, keeps), xq_ref[0], xk_ref[0], xv_ref[0], *taps)
        dxq, dxk, dxv, dwq, dwk, dwv = pull((dq_ref[0], dk_ref[0], dv_ref[0]))
        dxq_ref[0], dxk_ref[0], dxv_ref[0] = dxq, dxk, dxv
        for ref, dw in ((dwq_ref, dwq), (dwk_ref, dwk), (dwv_ref, dwv)):
            for i in range(CONV_TAPS):
                ref[i:i + 1, :] += dw[i]

        @pl.when(last)
        def _():
            for wait in _swap_copies(src_refs, dst_refs, *sems)[1]:
                wait()

    hw = HEADS * HEAD_DIM
    return pl.pallas_call(
        body, grid=(HEADS, B), in_specs=x_specs + w_specs + [out_spec] * 3 + [_ANY] * ns,
        out_specs=[out_spec] * 3 + [dw_spec] * 3 + [_ANY] * ns,
        out_shape=[_sds((B, S, hw), F32)] * 3 + [_sds((CONV_TAPS, hw), F32)] * 3 + [_swapped_shape(h) for h in halves],
        scratch_shapes=_swap_scratch(ns),
        compiler_params=_params(("arbitrary", "arbitrary")), name="gdn_pre_bwd",
    )(gqkv, gqkv, gqkv, conv_w, conv_w, conv_w, dq, dk, dv, *halves)


def _chunk_masks():
    i = lax.broadcasted_iota(jnp.int32, (CHUNK, CHUNK), 0)
    j = lax.broadcasted_iota(jnp.int32, (CHUNK, CHUNK), 1)
    lower, after = (j <= i).astype(F32), (j > i).astype(F32)
    return {"le": lower, "le_gt": jnp.concatenate([lower, after], axis=0), "strict": (j < i).astype(F32)}


def _gdn_chunk_fn(groups, masks):
    lane = lax.broadcasted_iota(jnp.int32, (groups, 1, 128), 2)
    head = lax.broadcasted_iota(jnp.int32, (groups, 1, 128), 0) % HEADS
    pick_a, pick_b = (lane == head).astype(F32), (lane == head + HEADS).astype(F32)
    lower, lower_after, strict = (jnp.broadcast_to(masks[n], (groups,) + masks[n].shape) for n in ("le", "le_gt", "strict"))
    ones_row = jnp.ones((1, 1, HEAD_DIM), F32)

    def f(q, k, v, gab, a_row, dt_row, state):
        ga = jnp.sum(gab * pick_a, axis=2, keepdims=True)
        gb = jnp.sum(gab * pick_b, axis=2, keepdims=True)
        a_log = jnp.sum(a_row * pick_a, axis=2, keepdims=True)
        dt_bias = jnp.sum(dt_row * pick_a, axis=2, keepdims=True)
        beta = _sigmoid(gb)
        g = -jnp.exp(a_log) * _softplus(ga + dt_bias)
        g_wide = g * ones_row
        cum, rest = _row_halves(_hi_nn(lower_after, g_wide))
        total = jnp.sum(g_wide, axis=1, keepdims=True)
        diff = _hi_nn(lower, g * strict)
        decay = lower * jnp.exp(diff)
        e_cum = jnp.exp(cum)
        lmat = strict * (beta * _bf_nt(k, k) * decay)
        u, w = _lane_halves(_unit_lower_solve(lmat, jnp.concatenate([v * beta, k * (beta * e_cum)], axis=2)))
        attn = _bf_nt(q, k) * decay
        v_new = u - _bf_nn(w, state)
        o = _bf_nn(q * e_cum, state) + _bf_nn(attn, v_new)
        new_state = state * jnp.exp(total) + _bf_tn(k * jnp.exp(rest), v_new)
        return o, new_state

    return f


def _gdn_chunk_fwd(q, k, v, gab, scal, shards):
    B, S, W = q.shape
    N = S // CHUNK
    ns = len(shards)

    def body(*refs):
        q_ref, k_ref, v_ref, gab_ref, sc_ref = refs[:5]
        src_refs = refs[5:5 + ns]
        o_ref, st_ref = refs[5 + ns:7 + ns]
        dst_refs = refs[7 + ns:7 + 2 * ns]
        state_ref, send_sems, recv_sems, local_sems = refs[7 + 2 * ns:]
        n = pl.program_id(0)

        @pl.when(n == 0)
        def _():
            for start in _gather_copies(src_refs, dst_refs, send_sems, recv_sems, local_sems)[0]:
                start()
            state_ref[...] = jnp.zeros_like(state_ref)

        @pl.when(n == (2 * N) // 3)
        def _():
            for pass_on in _gather_copies(src_refs, dst_refs, send_sems, recv_sems, local_sems)[1]:
                pass_on()

        groups = [(b, h) for b in range(B) for h in range(HEADS)]
        gather = lambda ref: jnp.stack([ref[b, :, h * HEAD_DIM:(h + 1) * HEAD_DIM] for b, h in groups])
        state = state_ref[...]
        for i, (b, h) in enumerate(groups):
            st_ref[b, 0, h] = state[i]
        o, new_state = _gdn_chunk_fn(len(groups), _chunk_masks())(
            gather(q_ref), gather(k_ref), gather(v_ref), jnp.stack([gab_ref[b] for b, _ in groups]), sc_ref[0:1, :], sc_ref[1:2, :], state)
        for i, (b, h) in enumerate(groups):
            o_ref[b, :, h * HEAD_DIM:(h + 1) * HEAD_DIM] = o[i]
        state_ref[...] = new_state

        @pl.when(n == N - 1)
        def _():
            for wait in _gather_copies(src_refs, dst_refs, send_sems, recv_sems, local_sems)[2]:
                wait()

    seq = pl.BlockSpec((B, CHUNK, W), lambda n: (0, n, 0))
    return pl.pallas_call(
        body, grid=(N,),
        in_specs=[seq, seq, seq, pl.BlockSpec((B, CHUNK, GAB_W), lambda n: (0, n, 0)), _const_spec((8, 128))] + [_ANY] * ns,
        out_specs=[seq, pl.BlockSpec((B, 1, HEADS, HEAD_DIM, HEAD_DIM), lambda n: (0, n, 0, 0, 0))] + [_ANY] * ns,
        out_shape=[_sds((B, S, W), F32), _sds((B, N, HEADS, HEAD_DIM, HEAD_DIM), F32)] + [_sds((4,) + s.shape, s.dtype) for s in shards],
        scratch_shapes=[pltpu.VMEM((B * HEADS, HEAD_DIM, HEAD_DIM), F32)] + _gather_scratch(ns),
        compiler_params=_params(("arbitrary",)), name="gdn_chunk_fwd",
    )(q, k, v, gab, scal, *shards)


def _gdn_chunk_bwd(q, k, v, gab, scal, states, do, partials):
    B, S, W = q.shape
    N = S // CHUNK
    ns = len(partials)

    def body(*refs):
        q_ref, k_ref, v_ref, gab_ref, sc_ref, st_ref, do_ref = refs[:7]
        src_refs = refs[7:7 + ns]
        dq_ref, dk_ref, dv_ref, dgab_ref, dsc_ref = refs[7 + ns:12 + ns]
        dst_refs = refs[12 + ns:12 + 2 * ns]
        dstate_ref, send_sems, recv_sems, local_sems = refs[12 + 2 * ns:]
        n = pl.program_id(0)

        @pl.when(n == 0)
        def _():
            for start in _scatter_copies(src_refs, dst_refs, send_sems, recv_sems, local_sems)[0]:
                start()
            dstate_ref[...] = jnp.zeros_like(dstate_ref)
            dsc_ref[...] = jnp.zeros_like(dsc_ref)

        groups = [(b, h) for b in range(B) for h in range(HEADS)]
        gather = lambda ref: jnp.stack([ref[b, :, h * HEAD_DIM:(h + 1) * HEAD_DIM] for b, h in groups])
        _, pull = jax.vjp(_gdn_chunk_fn(len(groups), _chunk_masks()), gather(q_ref), gather(k_ref), gather(v_ref),
                          jnp.stack([gab_ref[b] for b, _ in groups]), sc_ref[0:1, :], sc_ref[1:2, :],
                          jnp.stack([st_ref[b, 0, h] for b, h in groups]))
        dq, dk, dv, dg, d_a, d_dt, dstate = pull((gather(do_ref), dstate_ref[...]))
        for i, (b, h) in enumerate(groups):
            lanes = slice(h * HEAD_DIM, (h + 1) * HEAD_DIM)
            dq_ref[b, :, lanes] = dq[i]
            dk_ref[b, :, lanes] = dk[i]
            dv_ref[b, :, lanes] = dv[i]
        for b in range(B):
            dgab_ref[b] = sum(dg[b * HEADS + h] for h in range(HEADS))
        dstate_ref[...] = dstate
        dsc_ref[0:1, :] += d_a
        dsc_ref[1:2, :] += d_dt

        @pl.when(n == N - 1)
        def _():
            for wait in _scatter_copies(src_refs, dst_refs, send_sems, recv_sems, local_sems)[1]:
                wait()

    seq = pl.BlockSpec((B, CHUNK, W), lambda n: (0, N - 1 - n, 0))
    gab_spec = pl.BlockSpec((B, CHUNK, GAB_W), lambda n: (0, N - 1 - n, 0))
    return pl.pallas_call(
        body, grid=(N,),
        in_specs=[seq, seq, seq, gab_spec, _const_spec((8, 128)),
                  pl.BlockSpec((B, 1, HEADS, HEAD_DIM, HEAD_DIM), lambda n: (0, N - 1 - n, 0, 0, 0)), seq] + [_ANY] * ns,
        out_specs=[seq, seq, seq, gab_spec, _const_spec((8, 128))] + [_ANY] * ns,
        out_shape=[_sds((B, S, W), F32)] * 3 + [_sds((B, S, GAB_W), F32), _sds((8, 128), F32)] + [_scattered_shape(p) for p in partials],
        scratch_shapes=[pltpu.VMEM((B * HEADS, HEAD_DIM, HEAD_DIM), F32)] + _scatter_scratch(ns),
        compiler_params=_params(("arbitrary",)), name="gdn_chunk_bwd",
    )(q, k, v, gab, scal, states, do, *partials)


def _mix_fn(ao, go, gz, w_mla, w_gdn):
    return tuple(_rms(ao[h], w_mla[h]) for h in range(HEADS)) + tuple(_rms(go[h], w_gdn) * _silu(gz[h]) for h in range(HEADS))


def _mix_operands(ao_ref, go_ref, gz_ref, nw_ref):
    blocks = lambda ref: [ref[:, h * HEAD_DIM:(h + 1) * HEAD_DIM] for h in range(HEADS)]
    return blocks(ao_ref), blocks(go_ref), blocks(gz_ref), [nw_ref[h:h + 1, :] for h in range(HEADS)], nw_ref[HEADS:HEADS + 1, :]


def _mix_fwd(ao, go, gz, nw, w_out, x2):
    T, D = x2.shape
    tm = min(TOKEN_TILE, T)
    MW = 2 * HEADS * HEAD_DIM

    def body(ao_ref, go_ref, gz_ref, nw_ref, w_ref, x_ref, mix_ref, h_ref):
        outs = _mix_fn(*_mix_operands(ao_ref, go_ref, gz_ref, nw_ref))
        for i, piece in enumerate(outs):
            mix_ref[:, i * HEAD_DIM:(i + 1) * HEAD_DIM] = piece.astype(BF16)
        h_ref[...] = x_ref[...] + jnp.dot(mix_ref[...], w_ref[...], preferred_element_type=F32)

    half = HEADS * HEAD_DIM
    return pl.pallas_call(
        body, grid=(T // tm,),
        in_specs=[_row_spec(tm, half), _row_spec(tm, half), _row_spec(tm, half), _const_spec((8, 128)), _const_spec((MW, D)),
                  _row_spec(tm, D)],
        out_specs=[_row_spec(tm, MW), _row_spec(tm, D)],
        out_shape=[_sds((T, MW), BF16), _sds((T, D), F32)],
        compiler_params=_params(("parallel",)), name="mix_fwd",
    )(ao, go, gz, nw, w_out, x2)


def _mix_bwd(ao, go, gz, nw, w_out, dh):
    T, D = dh.shape
    tm = min(TOKEN_TILE, T)
    MW = 2 * HEADS * HEAD_DIM
    half = HEADS * HEAD_DIM

    def body(ao_ref, go_ref, gz_ref, nw_ref, w_ref, dh_ref, dao_ref, dgo_ref, dgz_ref, dnw_ref):
        @pl.when(pl.program_id(0) == 0)
        def _():
            dnw_ref[...] = jnp.zeros_like(dnw_ref)

        d_mix = _dg(dh_ref[...].astype(BF16), w_ref[...], 1, 1, None)
        cts = tuple(d_mix[:, i * HEAD_DIM:(i + 1) * HEAD_DIM] for i in range(2 * HEADS))
        _, pull = jax.vjp(_mix_fn, *_mix_operands(ao_ref, go_ref, gz_ref, nw_ref))
        d_ao, d_go, d_gz, d_wm, d_wg = pull(cts)
        for h in range(HEADS):
            lanes = slice(h * HEAD_DIM, (h + 1) * HEAD_DIM)
            dao_ref[:, lanes] = d_ao[h]
            dgo_ref[:, lanes] = d_go[h]
            dgz_ref[:, lanes] = d_gz[h]
            dnw_ref[h:h + 1, :] += d_wm[h]
        dnw_ref[HEADS:HEADS + 1, :] += d_wg

    return pl.pallas_call(
        body, grid=(T // tm,),
        in_specs=[_row_spec(tm, half), _row_spec(tm, half), _row_spec(tm, half), _const_spec((8, 128)), _const_spec((MW, D)),
                  _row_spec(tm, D)],
        out_specs=[_row_spec(tm, half)] * 3 + [_const_spec((8, 128))],
        out_shape=[_sds((T, half), F32)] * 3 + [_sds((8, 128), F32)],
        compiler_params=_params(("arbitrary",)), name="mix_bwd",
    )(ao, go, gz, nw, w_out, dh)


def _up_spec(w_up, tf):
    per_shard = w_up.shape[2] // tf
    return pl.BlockSpec((None, w_up.shape[1], tf), lambda i, j: (j // per_shard, 0, j % per_shard))


def _mlp_fwd(h2, w_mn, w_up, w_down, target):
    T, D = h2.shape
    FF = w_down.shape[0]
    tm, tf = min(MLP_TOKEN_TILE, T), min(FF_TILE, w_up.shape[2])
    nf = FF // tf

    def body(h_ref, wn_ref, wu_ref, wd_ref, t_ref, hn_ref, dy_ref, sq_ref, acc_ref):
        j = pl.program_id(1)

        @pl.when(j == 0)
        def _():
            hn_ref[...] = _rms(h_ref[...], wn_ref[...]).astype(BF16)
            acc_ref[...] = jnp.zeros_like(acc_ref)

        up = jnp.dot(hn_ref[...], wu_ref[...], preferred_element_type=F32)
        act = jnp.square(jnp.maximum(up, 0.0)).astype(BF16)
        acc_ref[...] += jnp.dot(act, wd_ref[...], preferred_element_type=F32)

        @pl.when(j == nf - 1)
        def _():
            err = h_ref[...] + acc_ref[...] - t_ref[...]
            dy_ref[...] = err * (1.0 / D)
            sq_ref[...] = jnp.zeros_like(sq_ref) + jnp.sum(err * err)

    tok = lambda w: pl.BlockSpec((tm, w), lambda i, j: (i, 0))
    return pl.pallas_call(
        body, grid=(T // tm, nf),
        in_specs=[tok(D), _const_spec((1, D)), _up_spec(w_up, tf), pl.BlockSpec((tf, D), lambda i, j: (j, 0)), tok(D)],
        out_specs=[tok(D), tok(D), pl.BlockSpec((1, 8, 128), lambda i, j: (i, 0, 0))],
        out_shape=[_sds((T, D), BF16), _sds((T, D), F32), _sds((T // tm, 8, 128), F32)],
        scratch_shapes=[pltpu.VMEM((tm, D), F32)],
        compiler_params=_params(("parallel", "arbitrary")), name="mlp_fwd",
    )(h2, w_mn, w_up, w_down, target)


def _mlp_bwd(h2, w_mn, hn, w_up, w_down, dy):
    T, D = h2.shape
    FF = w_down.shape[0]
    tm, tf = min(MLP_TOKEN_TILE, T), min(FF_TILE, w_up.shape[2])
    nf = FF // tf

    def body(h_ref, wn_ref, hn_ref, wu_ref, wd_ref, dy_ref, dh_ref, act_ref, dup_ref, dwn_ref, acc_ref):
        i, j = pl.program_id(0), pl.program_id(1)

        @pl.when((i == 0) & (j == 0))
        def _():
            dwn_ref[...] = jnp.zeros_like(dwn_ref)

        @pl.when(j == 0)
        def _():
            acc_ref[...] = jnp.zeros_like(acc_ref)

        r = jnp.maximum(jnp.dot(hn_ref[...], wu_ref[...], preferred_element_type=F32), 0.0)
        act_ref[...] = (r * r).astype(BF16)
        d_act = _dg(dy_ref[...].astype(BF16), wd_ref[...], 1, 1, None)
        d_up = (d_act * (2.0 * r)).astype(BF16)
        dup_ref[...] = d_up
        acc_ref[...] += _dg(d_up, wu_ref[...], 1, 1, None)

        @pl.when(j == nf - 1)
        def _():
            _, pull = jax.vjp(_rms, h_ref[...], wn_ref[...])
            dh, dwn = pull(acc_ref[...])
            dh_ref[...] = dh + dy_ref[...]
            dwn_ref[...] += dwn

    tok = lambda w: pl.BlockSpec((tm, w), lambda i, j: (i, 0))
    ff = pl.BlockSpec((tm, tf), lambda i, j: (i, j))
    return pl.pallas_call(
        body, grid=(T // tm, nf),
        in_specs=[tok(D), _const_spec((1, D)), tok(D), _up_spec(w_up, tf), pl.BlockSpec((tf, D), lambda i, j: (j, 0)), tok(D)],
        out_specs=[tok(D), ff, ff, _const_spec((1, D))],
        out_shape=[_sds((T, D), F32), _sds((T, FF), BF16), _sds((T, FF), BF16), _sds((1, D), F32)],
        scratch_shapes=[pltpu.VMEM((tm, D), F32)],
        compiler_params=_params(("arbitrary", "arbitrary")), name="mlp_bwd",
    )(h2, w_mn, hn, w_up, w_down, dy)


def _rope_pad(a):
    z = jnp.zeros(a.shape[:-1] + (ROPE_HALF,), a.dtype)
    return jnp.concatenate([a[..., :ROPE_HALF], z, a[..., ROPE_HALF:], z], axis=-1)


def _rope_unpad(a):
    return jnp.concatenate([a[..., :ROPE_HALF], a[..., 2 * ROPE_HALF:3 * ROPE_HALF]], axis=-1)


_G0 = 2 * LORA + ROPE_DIM
W_IN_COLS = _G0 + GQKV_W + GZ_W + 2 * HEADS


def _widen_w_in_t(w_t):
    z = jnp.zeros((ROPE_HALF, w_t.shape[1]), w_t.dtype)
    pad = jnp.zeros((GAB_W - 2 * HEADS, w_t.shape[1]), w_t.dtype)
    return jnp.concatenate([w_t[:2 * LORA + ROPE_HALF], z, w_t[2 * LORA + ROPE_HALF:_G0], z, w_t[_G0:], pad], axis=0)


def _narrow_w_in_t(w_t):
    return jnp.concatenate([w_t[:2 * LORA + ROPE_HALF], w_t[2 * LORA + 2 * ROPE_HALF:2 * LORA + 3 * ROPE_HALF],
                            w_t[LAT_W:LAT_W + W_IN_COLS - _G0]], axis=0)


def _stack_mla(w_uq, w_ukv):
    uq = w_uq.reshape(LORA, HEADS, QK_DIM)
    ukv = w_ukv.reshape(LORA, HEADS, 2 * HEAD_DIM)
    parts = [uq[:, :, :HEAD_DIM], _rope_pad(uq[:, :, HEAD_DIM:]), ukv[:, :, :HEAD_DIM], ukv[:, :, HEAD_DIM:]]
    return jnp.concatenate([p.transpose(1, 0, 2) for p in parts], axis=0)


def _unstack_mla(w):
    p = [w[i * HEADS:(i + 1) * HEADS].transpose(1, 0, 2) for i in range(4)]
    uq = jnp.concatenate([p[0], _rope_unpad(p[1])], axis=-1).reshape(LORA, HEADS * QK_DIM)
    ukv = jnp.concatenate([p[2], p[3]], axis=-1).reshape(LORA, HEADS * 2 * HEAD_DIM)
    return uq, ukv


def _rows8(rows):
    a = jnp.concatenate(rows, axis=0)
    return jnp.pad(a, ((0, 8 - a.shape[0]), (0, 0)))


def _qk_norm_rows(q_norm_w, k_norm_w):
    return _rows8([q_norm_w[:, :HEAD_DIM], _rope_pad(q_norm_w[:, HEAD_DIM:]), k_norm_w[:, :HEAD_DIM], _rope_pad(k_norm_w[:, HEAD_DIM:])])


def _rope_rows():
    inv_freq = ROPE_THETA ** (-jnp.arange(ROPE_HALF, dtype=F32) / ROPE_HALF)
    z = jnp.zeros((ROPE_HALF,), F32)
    freq = jnp.concatenate([inv_freq, z, inv_freq, z])
    sign = jnp.concatenate([-jnp.ones((ROPE_HALF,), F32), z, jnp.ones((ROPE_HALF,), F32), z])
    return _rows8([freq[None], sign[None]])


def _column_shards(a):
    return a.reshape(a.shape[0], 4, a.shape[1] // 4).transpose(1, 0, 2)


def _from_column_shards(a):
    return a.transpose(1, 0, 2).reshape(a.shape[1], 4 * a.shape[2])


_ANY = pl.BlockSpec(memory_space=pl.ANY)
_OTHER_CHIPS = ((1, 0), (0, 1), (1, 1))


def _here():
    return lax.axis_index("x"), lax.axis_index("y"), lax.axis_index("c")


def _flip(v, bit):
    return 1 - v if bit else v


def _remote(src, dst, send_sems, recv_sems, k, to):
    return pltpu.make_async_remote_copy(src_ref=src, dst_ref=dst, send_sem=send_sems.at[k], recv_sem=recv_sems.at[k],
                                        device_id=to, device_id_type=MESH)


def _half_of(ref, k, shape):
    r, c = shape
    if (r // 2) % 16 == 0:
        return ref.at[pl.ds(pl.multiple_of(k * (r // 2), 16), r // 2)]
    if (c // 2) % 128 == 0:
        return ref.at[:, pl.ds(pl.multiple_of(k * (c // 2), 128), c // 2)]
    return None


def _gather_copies(srcs, dsts, send_sems, recv_sems, local_sems):
    x, y, c = _here()
    slot, sibling, n = 2 * x + y, (x, y, 1 - c), len(srcs)
    starts, passes, waits = [], [], []
    for i, (src, dst) in enumerate(zip(srcs, dsts)):
        own = pltpu.make_async_copy(src, dst.at[slot], local_sems.at[i])
        starts.append(own.start)
        waits.append(own.wait)
        halves = _half_of(src, c, src.shape) is not None
        for j, (fx, fy) in enumerate(_OTHER_CHIPS):
            cx, cy = _flip(x, fx), _flip(y, fy)
            there = dst.at[2 * cx + cy]
            if halves:
                push = _remote(_half_of(src, c, src.shape), _half_of(dst.at[slot], c, src.shape), send_sems, recv_sems, 3 * i + j, (cx, cy, c))
                landed, other = _half_of(there, c, src.shape), _half_of(there, 1 - c, src.shape)
                onward = _remote(landed, landed, send_sems, recv_sems, 3 * n + 3 * i + j, sibling)
                passes += [_remote(landed, landed, send_sems, recv_sems, 3 * i + j, (cx, cy, c)).wait_recv, onward.start]
                waits += [_remote(other, other, send_sems, recv_sems, 3 * n + 3 * i + j, sibling).wait_recv, onward.wait_send]
            else:
                push = _remote(src, dst.at[slot], send_sems, recv_sems, 3 * i + j, (cx, cy, c))
                waits.append(_remote(there, there, send_sems, recv_sems, 3 * i + j, (cx, cy, c)).wait_recv)
            starts.append(push.start)
            waits.append(push.wait_send)
    return starts, passes, waits


def _gather_scratch(n):
    return [pltpu.SemaphoreType.DMA((6 * n,)), pltpu.SemaphoreType.DMA((6 * n,)), pltpu.SemaphoreType.DMA((n,))]


def _all_gather(shards, name):
    ns = len(shards)

    def body(*refs):
        starts, passes, waits = _gather_copies(refs[:ns], refs[ns:2 * ns], *refs[2 * ns:])
        for call in starts + passes + waits:
            call()

    return pl.pallas_call(
        body, in_specs=[_ANY] * ns, out_specs=[_ANY] * ns, out_shape=[_sds((4,) + s.shape, s.dtype) for s in shards],
        scratch_shapes=_gather_scratch(ns), name=name,
    )(*shards)


def _by_lanes(shape):
    return (shape[-2] // 2) % 16 != 0


def _scattered_shape(p):
    r, c = p.shape[1:]
    return _sds((8, r, c // 2) if _by_lanes(p.shape) else (8, r // 2, c), p.dtype)


def _scatter_copies(srcs, dsts, send_sems, recv_sems, local_sems, whole=0):
    x, y, c = _here()
    me = 4 * x + 2 * y + c
    starts, waits = [], []
    for i, (src, dst) in enumerate(zip(srcs, dsts)):
        def piece(px, py, pc, src=src, entire=i >= len(srcs) - whole):
            if entire:
                return src
            if _by_lanes(src.shape):
                half = src.shape[2] // 2
                return src.at[2 * px + py, :, pl.ds(pl.multiple_of(pc * half, 128), half)]
            half = src.shape[1] // 2
            return src.at[2 * px + py, pl.ds(pl.multiple_of(pc * half, 16), half)]

        own = pltpu.make_async_copy(piece(x, y, c), dst.at[me], local_sems.at[i])
        starts.append(own.start)
        waits.append(own.wait)
        for k in range(1, 8):
            px, py, pc = _flip(x, k & 4), _flip(y, k & 2), _flip(c, k & 1)
            push = _remote(piece(px, py, pc), dst.at[me], send_sems, recv_sems, 7 * i + k - 1, (px, py, pc))
            landed = dst.at[4 * px + 2 * py + pc]
            starts.append(push.start)
            waits += [_remote(landed, landed, send_sems, recv_sems, 7 * i + k - 1, (px, py, pc)).wait_recv, push.wait_send]
    return starts, waits


def _scatter_scratch(n):
    return [pltpu.SemaphoreType.DMA((7 * n,)), pltpu.SemaphoreType.DMA((7 * n,)), pltpu.SemaphoreType.DMA((n,))]


def _scatter(partials, wholes, name):
    ns = len(partials) + len(wholes)

    def body(*refs):
        starts, waits = _scatter_copies(refs[:ns], refs[ns:2 * ns], *refs[2 * ns:], whole=len(wholes))
        for call in starts + waits:
            call()

    return pl.pallas_call(
        body, in_specs=[_ANY] * ns, out_specs=[_ANY] * ns,
        out_shape=[_scattered_shape(p) for p in partials] + [_sds((8,) + s.shape, s.dtype) for s in wholes],
        scratch_shapes=_scatter_scratch(ns), name=name,
    )(*partials, *wholes)


def _swapped_shape(half):
    r, c = half.shape
    return _sds((r, 2 * c) if _by_lanes((r, 2 * c)) else (2, r, c), half.dtype)


def _swap_copies(srcs, dsts, send_sems, recv_sems, local_sems):
    x, y, c = _here()
    sibling = (x, y, 1 - c)
    starts, waits = [], []
    for i, (src, dst) in enumerate(zip(srcs, dsts)):
        if len(dst.shape) == 2:
            lanes = src.shape[1]
            mine, other = (dst.at[:, pl.ds(pl.multiple_of(k * lanes, 128), lanes)] for k in (c, 1 - c))
        else:
            mine, other = dst.at[c], dst.at[1 - c]
        own = pltpu.make_async_copy(src, mine, local_sems.at[i])
        push = _remote(src, mine, send_sems, recv_sems, i, sibling)
        starts += [own.start, push.start]
        waits += [_remote(other, other, send_sems, recv_sems, i, sibling).wait_recv, push.wait_send, own.wait]
    return starts, waits


def _swap_scratch(n):
    return [pltpu.SemaphoreType.DMA((n,)), pltpu.SemaphoreType.DMA((n,)), pltpu.SemaphoreType.DMA((n,))]


def _exchange_halves(halves):
    ns = len(halves)

    def body(*refs):
        starts, waits = _swap_copies(refs[:ns], refs[ns:2 * ns], *refs[2 * ns:])
        for call in starts + waits:
            call()

    return pl.pallas_call(
        body, in_specs=[_ANY] * ns, out_specs=[_ANY] * ns, out_shape=[_swapped_shape(h) for h in halves],
        scratch_shapes=_swap_scratch(ns), name="exchange_halves",
    )(*halves)


def _row_tile(rows, row_bytes, budget):
    tr = rows
    while tr * row_bytes > budget and tr % 16 == 0:
        tr //= 2
    return tr


def _sum_slots(parts, name):
    _, rows, cols = parts.shape
    tr = _row_tile(rows, 8 * cols * 4, 2 * 1024 * 1024)

    def body(p_ref, o_ref):
        acc = p_ref[0].astype(F32)
        for d in range(1, 8):
            acc = acc + p_ref[d].astype(F32)
        o_ref[...] = acc

    return pl.pallas_call(
        body, grid=(rows // tr,), in_specs=[pl.BlockSpec((8, tr, cols), lambda i: (0, i, 0))],
        out_specs=pl.BlockSpec((tr, cols), lambda i: (i, 0)), out_shape=_sds((rows, cols), F32),
        compiler_params=_params(("parallel",)), name=name,
    )(parts)


def _adam_update(w, g, m, v):
    m = ADAM_B1 * m + (1.0 - ADAM_B1) * g
    v = ADAM_B2 * v + (1.0 - ADAM_B2) * jnp.square(g)
    m_hat = m / (1.0 - ADAM_B1 ** ADAM_STEP)
    v_hat = v / (1.0 - ADAM_B2 ** ADAM_STEP)
    return -ADAM_LR * (m_hat / (jnp.sqrt(v_hat) + ADAM_EPS) + ADAM_WD * w), m, v


SMALL_ROWS = {"attn_norm_w": 0, "mlp_norm_w": 1, "q_lat_norm_w": 2, "kv_lat_norm_w": 3, "q_norm_w": 4, "k_norm_w": 5,
              "mla_out_norm_w": 6, "gdn_norm_w": 10, "a_log": 11, "dt_bias": 12}
LOSS_ROW = 13
SMALL_SHAPE = (16, 1024)


def _pack_small_partials(d_attn_nw, d_mlp_nw, d_ln, d_qk_nw, d_mix_nw, d_scal, conv_parts, sq):
    D = d_attn_nw.shape[1]

    def body(an_ref, mn_ref, ln_ref, qk_ref, mix_ref, sc_ref, cq_ref, ck_ref, cv_ref, sq_ref, a_ref, c_ref):
        a_ref[...] = jnp.zeros_like(a_ref)
        a_ref[0:1, :D] = an_ref[...]
        a_ref[1:2, :D] = mn_ref[...]
        a_ref[2:4, :LORA] = ln_ref[...]
        for row, base in ((4, 0), (5, 2)):
            rope = qk_ref[base + 1:base + 2, :]
            a_ref[row:row + 1, :QK_DIM] = jnp.concatenate(
                [qk_ref[base:base + 1, :], rope[:, :ROPE_HALF], rope[:, 2 * ROPE_HALF:3 * ROPE_HALF]], axis=1)
        a_ref[6:6 + HEADS, :HEAD_DIM] = mix_ref[0:HEADS, :]
        a_ref[10:11, :HEAD_DIM] = mix_ref[HEADS:HEADS + 1, :]
        a_ref[11:13, :128] = sc_ref[0:2, :]
        a_ref[LOSS_ROW:LOSS_ROW + 1, :128] = jnp.zeros((1, 128), F32) + jnp.sum(sq_ref[:, 0:1, 0:1]) * (0.5 / D)
        c_ref[...] = jnp.concatenate([cq_ref[...], ck_ref[...], cv_ref[...]], axis=1)

    return pl.pallas_call(
        body, out_shape=[_sds(SMALL_SHAPE, F32), _sds((CONV_TAPS, GQKV_W), F32)], name="pack_small_partials",
    )(d_attn_nw, d_mlp_nw, d_ln, d_qk_nw, d_mix_nw, d_scal, *conv_parts, sq)


def _adamw_small(parts, conv_parts, w, m, v):
    names = tuple(SMALL_ROWS) + ("conv_w",)
    cols = w["conv_w"].shape[2]

    def body(*refs):
        p_ref, c_ref = refs[:2]
        n = len(names)
        w_refs, m_refs, v_refs = (dict(zip(names, refs[2 + k * n:2 + (k + 1) * n])) for k in range(3))
        loss_ref = refs[2 + 3 * n]
        out = [dict(zip(names, refs[3 + (3 + k) * n:3 + (4 + k) * n])) for k in range(4)]
        acc_ref, cacc_ref = refs[3 + 7 * n:]
        acc, cacc = p_ref[0], c_ref[0]
        for d in range(1, 8):
            acc, cacc = acc + p_ref[d], cacc + c_ref[d]
        acc_ref[...] = acc
        cacc_ref[...] = cacc
        loss_ref[...] = acc_ref[LOSS_ROW:LOSS_ROW + 1, 0:1]
        chip = 2 * lax.axis_index("x") + lax.axis_index("y")
        for name in names:
            shape = w_refs[name].shape
            if name == "conv_w":
                g = sum(jnp.where(chip == s, cacc_ref[:, s * cols:(s + 1) * cols], 0.0) for s in range(4))[None]
            else:
                row = SMALL_ROWS[name]
                g = acc_ref[row:row + math.prod(shape[:-1]), 0:shape[-1]].reshape(shape)
            delta, new_m, new_v = _adam_update(w_refs[name][...], g, m_refs[name][...], v_refs[name][...])
            for ref, val in zip((o[name] for o in out), (g, delta, new_m, new_v)):
                ref[...] = val

    ins = [x[n] for x in (w, m, v) for n in names]
    shapes = [_sds(w[n].shape, F32) for n in names]
    outs = pl.pallas_call(
        body, out_shape=[_sds((1, 1), F32)] + shapes * 4,
        scratch_shapes=[pltpu.VMEM(parts.shape[1:], F32), pltpu.VMEM(conv_parts.shape[1:], F32)], name="adamw_small",
    )(parts, conv_parts, *ins)
    n = len(names)
    return (outs[0],) + tuple(dict(zip(names, outs[1 + k * n:1 + (k + 1) * n])) for k in range(4))


def _adamw(w, g, m, v, name):
    rows, cols = w.shape[0], w.shape[-1]
    if w.ndim == 3:
        tr = max(d for d in range(1, rows + 1) if rows % d == 0 and d * 8 * cols * 4 * 14 <= VMEM_LIMIT // 2)
    else:
        tr = _row_tile(rows, 7 * cols * 4, 4 * 1024 * 1024)

    def body(w_ref, g_ref, m_ref, v_ref, d_ref, mo_ref, vo_ref):
        d_ref[...], mo_ref[...], vo_ref[...] = _adam_update(w_ref[...], g_ref[...], m_ref[...], v_ref[...])

    block = (tr,) + w.shape[1:]
    spec = pl.BlockSpec(block, lambda i: (i,) + (0,) * (len(block) - 1))
    return pl.pallas_call(
        body, grid=(rows // tr,), in_specs=[spec] * 4, out_specs=[spec] * 3, out_shape=[_sds(w.shape, F32)] * 3,
        compiler_params=_params(("parallel",)), name=name,
    )(w, g, m, v)


def kernel(x, positions, attn_norm_w, w_in, q_lat_norm_w, w_uq, kv_lat_norm_w, w_ukv, q_norm_w, k_norm_w, mla_out_norm_w, conv_w, a_log, dt_bias, gdn_norm_w, w_out, mlp_norm_w, w_up, w_down, loss_target, m_attn_norm_w, m_w_in, m_q_lat_norm_w, m_w_uq, m_kv_lat_norm_w, m_w_ukv, m_q_norm_w, m_k_norm_w, m_mla_out_norm_w, m_conv_w, m_a_log, m_dt_bias, m_gdn_norm_w, m_w_out, m_mlp_norm_w, m_w_up, m_w_down, v_attn_norm_w, v_w_in, v_q_lat_norm_w, v_w_uq, v_kv_lat_norm_w, v_w_ukv, v_q_norm_w, v_k_norm_w, v_mla_out_norm_w, v_conv_w, v_a_log, v_dt_bias, v_gdn_norm_w, v_w_out, v_mlp_norm_w, v_w_up, v_w_down):
    w = dict(zip(WEIGHTS, (attn_norm_w, w_in, q_lat_norm_w, w_uq, kv_lat_norm_w, w_ukv, q_norm_w, k_norm_w, mla_out_norm_w, conv_w,
                           a_log, dt_bias, gdn_norm_w, w_out, mlp_norm_w, w_up, w_down)))
    m = dict(zip(WEIGHTS, (m_attn_norm_w, m_w_in, m_q_lat_norm_w, m_w_uq, m_kv_lat_norm_w, m_w_ukv, m_q_norm_w, m_k_norm_w,
                           m_mla_out_norm_w, m_conv_w, m_a_log, m_dt_bias, m_gdn_norm_w, m_w_out, m_mlp_norm_w, m_w_up, m_w_down)))
    v = dict(zip(WEIGHTS, (v_attn_norm_w, v_w_in, v_q_lat_norm_w, v_w_uq, v_kv_lat_norm_w, v_w_ukv, v_q_norm_w, v_k_norm_w,
                           v_mla_out_norm_w, v_conv_w, v_a_log, v_dt_bias, v_gdn_norm_w, v_w_out, v_mlp_norm_w, v_w_up, v_w_down)))
    B, S, D = x.shape
    T = B * S
    x2, pos, target = x.reshape(T, D), positions.reshape(T, 1), loss_target.reshape(T, D)
    seq = lambda a: a.reshape(B, S, a.shape[-1])
    tok = lambda a: a.reshape(T, a.shape[-1])
    local = {n: w[n][0] for n in SHARDED}

    g_in, g_uq, g_ukv, g_conv = _all_gather([jnp.swapaxes(w_in, 1, 2)[0].astype(BF16), local["w_uq"].astype(BF16),
                                             local["w_ukv"].astype(BF16), local["conv_w"]], "gather_first_weights")
    w_in_p = _widen_w_in_t(g_in.reshape(-1, D))
    w_mla = _stack_mla(_from_column_shards(g_uq), _from_column_shards(g_ukv))
    conv_full = _from_column_shards(g_conv)
    ln_w = jnp.concatenate([q_lat_norm_w, kv_lat_norm_w], axis=0)
    qk_nw = _qk_norm_rows(q_norm_w, k_norm_w)
    rope_rows = _rope_rows()
    scal = _rows8([jnp.pad(a_log, ((0, 0), (0, 128 - HEADS))), jnp.pad(dt_bias, ((0, 0), (0, 128 - HEADS)))])
    mix_nw = _rows8([mla_out_norm_w[0], gdn_norm_w])

    xn, lat, gqkv, gz, gab = _in_proj_fwd(x2, attn_norm_w, w_in_p)
    q, k, v_att = _mla_pre_fwd(lat, pos, ln_w, w_mla, qk_nw, rope_rows)
    ao, lse, g_down = _attn_fwd(seq(q), seq(k), seq(v_att), [local["w_down"].astype(BF16)])
    gq, gk, gv = _gdn_pre_fwd(seq(gqkv), conv_full)
    go, states, g_out, w_up_b = _gdn_chunk_fwd(gq, gk, gv, seq(gab), scal, [local["w_out"].astype(BF16), local["w_up"].astype(BF16)])
    w_out_b = g_out.reshape(-1, D)
    w_down_b = g_down.reshape(-1, D)
    mix, h2 = _mix_fwd(tok(ao), tok(go), gz, mix_nw, w_out_b, x2)
    hn, dy, sq = _mlp_fwd(h2, mlp_norm_w, w_up_b, w_down_b, target)

    dh, act, d_up, d_mlp_nw = _mlp_bwd(h2, mlp_norm_w, hn, w_up_b, w_down_b, dy)
    p_down = _wgrad(act, dy, "wgrad_down").reshape(4, -1, D)
    p_up = _wgrad(hn, d_up, "wgrad_up", column_shards=4)
    d_ao, d_go, d_gz, d_mix_nw = _mix_bwd(tok(ao), tok(go), gz, mix_nw, w_out_b, dh)
    p_out = _wgrad(mix, dh, "wgrad_out").reshape(4, -1, D)
    d_gq, d_gk, d_gv, d_gab, d_scal, s_up, s_out = _gdn_chunk_bwd(gq, gk, gv, seq(gab), scal, states, seq(d_go), [p_up, p_out])
    early = ("w_up", "w_out", "w_down")
    dxq, dxk, dxv, dcq, dck, dcv, g_up, g_out = _gdn_pre_bwd(seq(gqkv), conv_full, d_gq, d_gk, d_gv,
                                                             [_sum_slots(s_up, "sum_w_up"), _sum_slots(s_out, "sum_w_out")])
    dq, dk, dv, s_down = _attn_bwd(seq(q), seq(k), seq(v_att), ao, lse, seq(d_ao), [p_down])
    d_lat, d_ln, d_w_mla, d_qk_nw, g_down = _mla_pre_bwd(lat, pos, ln_w, w_mla, qk_nw, rope_rows, tok(dq), tok(dk), tok(dv),
                                                         [_sum_slots(s_down, "sum_w_down")])
    early_grads = [g_up, g_out, g_down]
    grad_x2, d_proj, d_attn_nw = _in_proj_bwd([d_lat, tok(dxq), tok(dxk), tok(dxv), d_gz, tok(d_gab)], w_in_p, x2, attn_norm_w, dh)
    p_in = _narrow_w_in_t(_wgrad(d_proj, xn, "wgrad_in")).reshape(4, -1, D)
    p_uq, p_ukv = (_column_shards(a).astype(BF16) for a in _unstack_mla(d_w_mla))
    small_buf, conv_buf = _pack_small_partials(d_attn_nw, d_mlp_nw, d_ln, d_qk_nw, d_mix_nw, d_scal, (dcq, dck, dcv), sq)
    s_in, s_uq, s_ukv, s_small, s_conv = _scatter([p_in, p_uq, p_ukv], [small_buf, conv_buf], "scatter_last_partials")

    late = ("w_in", "w_uq", "w_ukv")
    late_grads = _exchange_halves([_sum_slots(s, "sum_" + n) for n, s in zip(late, (s_in, s_uq, s_ukv))])
    names = early + late
    grad = {n: g.reshape(-1, g.shape[-1]) for n, g in zip(names, list(early_grads) + list(late_grads))}

    loss, g_small, delta, new_m, new_v = _adamw_small(s_small, s_conv, w, m, v)
    grad.update(g_small)
    for n in names:
        if n == "w_in":
            stored = lambda a: jnp.transpose(a, (2, 0, 1))
            outs = _adamw(stored(w[n]), grad[n][:, None, :], stored(m[n]), stored(v[n]), "adamw_" + n)
            grad[n], delta[n], new_m[n], new_v[n] = (jnp.transpose(a, (1, 2, 0)) for a in (grad[n][:, None, :], *outs))
        else:
            delta[n], new_m[n], new_v[n] = _adamw(local[n], grad[n], m[n][0], v[n][0], "adamw_" + n)
    def in_order(d):
        return [d[n].reshape(w[n].shape) for n in WEIGHTS]

    return (loss.reshape(()), grad_x2.reshape(B, S, D), *in_order(grad), *in_order(delta), *in_order(new_m), *in_order(new_v))
```

```python
import functools
import math

import jax
import jax.numpy as jnp
from jax import lax
from jax.experimental import pallas as pl
from jax.experimental.pallas import tpu as pltpu

F32 = jnp.float32
BF16 = jnp.bfloat16
MESH = pl.DeviceIdType.MESH

EPS = 1e-6
HEADS = 4
HEAD_DIM = 128
ROPE_DIM = 64
ROPE_HALF = 32
QK_DIM = 192
QK_PAD = 256
LORA = 256
CHUNK = 64
CONV_TAPS = 4
ROPE_THETA = 10000.0
ATTN_SCALE = QK_DIM ** -0.5

LAT_W = 640
GQKV_W = 3 * HEADS * HEAD_DIM
GZ_W = HEADS * HEAD_DIM
GAB_W = 128
PROJ_SPLITS = ((0, LAT_W), (LAT_W, LAT_W + GQKV_W), (LAT_W + GQKV_W, LAT_W + GQKV_W + GZ_W),
               (LAT_W + GQKV_W + GZ_W, LAT_W + GQKV_W + GZ_W + GAB_W))
PROJ_W = PROJ_SPLITS[-1][1]

ADAM_LR = 0.001
ADAM_B1 = 0.9
ADAM_B2 = 0.999
ADAM_EPS = 1e-08
ADAM_WD = 0.01
ADAM_STEP = 10

TOKEN_TILE = 512
MLP_TOKEN_TILE = 1024
FF_TILE = 512
ATTN_TILE = 512
ATTN_HEADS_PER_STEP = 2
WGRAD_OUT_BYTES = 8 * 1024 * 1024
VMEM_LIMIT = 48 * 1024 * 1024

SHARDED = ("w_in", "w_uq", "w_ukv", "conv_w", "w_out", "w_up", "w_down")
WEIGHTS = ("attn_norm_w", "w_in", "q_lat_norm_w", "w_uq", "kv_lat_norm_w", "w_ukv", "q_norm_w", "k_norm_w", "mla_out_norm_w",
           "conv_w", "a_log", "dt_bias", "gdn_norm_w", "w_out", "mlp_norm_w", "w_up", "w_down")


def _sds(shape, dtype):
    return jax.ShapeDtypeStruct(shape, dtype)


def _params(semantics):
    return pltpu.CompilerParams(dimension_semantics=semantics, vmem_limit_bytes=VMEM_LIMIT)


def _block(n):
    for b in (512, 256, 128):
        if n % b == 0:
            return b
    return n


def _dg(a, b, ca, cb, prec):
    lead = a.ndim - 2
    batch = (tuple(range(lead)),) * 2
    return lax.dot_general(a, b, (((ca + lead,), (cb + lead,)), batch), precision=prec, preferred_element_type=F32)


def _split_bf16(a):
    hi = a.astype(BF16)
    return hi, (a - hi.astype(F32)).astype(BF16)


def _dot_bf16(a, b, ca, cb):
    return _dg(a.astype(BF16), b.astype(BF16), ca, cb, None)


def _dot_bf16x3(a, b, ca, cb):
    a_hi, a_lo = _split_bf16(a)
    b_hi, b_lo = _split_bf16(b)
    lead = a.ndim - 2
    return _dg(jnp.concatenate([a_hi, a_hi, a_lo], axis=ca + lead), jnp.concatenate([b_hi, b_lo, b_hi], axis=cb + lead), ca, cb, None)


def _matmul_family(dot):
    def nn_raw(a, b):
        return dot(a, b, 1, 0)

    def nt_raw(a, b):
        return dot(a, b, 1, 1)

    def tn_raw(a, b):
        return dot(a, b, 0, 0)

    @jax.custom_vjp
    def nn(a, b):
        return nn_raw(a, b)

    nn.defvjp(lambda a, b: (nn_raw(a, b), (a, b)), lambda r, g: (nt_raw(g, r[1]), tn_raw(r[0], g)))

    @jax.custom_vjp
    def nt(a, b):
        return nt_raw(a, b)

    nt.defvjp(lambda a, b: (nt_raw(a, b), (a, b)), lambda r, g: (nn_raw(g, r[1]), tn_raw(g, r[0])))

    @jax.custom_vjp
    def tn(a, b):
        return tn_raw(a, b)

    tn.defvjp(lambda a, b: (tn_raw(a, b), (a, b)), lambda r, g: (nt_raw(r[1], g), nn_raw(r[0], g)))
    return nn, nt, tn


_bf_nn, _bf_nt, _bf_tn = _matmul_family(_dot_bf16)
_hi_nn, _hi_nt, _hi_tn = _matmul_family(_dot_bf16x3)


def _lower_powers(lmat):
    powers = []
    while 2 ** (len(powers) + 1) < lmat.shape[-1]:
        powers.append(_dot_bf16x3(powers[-1] if powers else lmat, powers[-1] if powers else lmat, 1, 0))
    return powers


@jax.custom_vjp
def _unit_lower_solve(lmat, rhs):
    return _unit_lower_solve_fwd(lmat, rhs)[0]


def _unit_lower_solve_fwd(lmat, rhs):
    powers = _lower_powers(lmat)
    x = rhs - _dot_bf16x3(lmat, rhs, 1, 0)
    for p in powers:
        x = x + _dot_bf16x3(p, x, 1, 0)
    return x, (lmat, powers, x)


def _unit_lower_solve_bwd(res, g):
    lmat, powers, x = res
    y = g - _dot_bf16x3(lmat, g, 0, 0)
    for p in powers:
        y = y + _dot_bf16x3(p, y, 0, 0)
    return -_dot_bf16x3(y, x, 1, 1), y


_unit_lower_solve.defvjp(_unit_lower_solve_fwd, _unit_lower_solve_bwd)


@jax.custom_vjp
def _lane_halves(x):
    n = x.shape[-1] // 2
    return x[..., :n], x[..., n:]


_lane_halves.defvjp(lambda x: (_lane_halves(x), None), lambda _, g: (jnp.concatenate(g, axis=-1),))


@jax.custom_vjp
def _row_halves(x):
    n = x.shape[-2] // 2
    return x[..., :n, :], x[..., n:, :]


_row_halves.defvjp(lambda x: (_row_halves(x), None), lambda _, g: (jnp.concatenate(g, axis=-2),))


@jax.custom_vjp
def _swap_halves(t):
    return pltpu.roll(t, 64, 1)


_swap_halves.defvjp(lambda t: (pltpu.roll(t, 64, 1), None), lambda _, g: (pltpu.roll(g, 64, 1),))


@functools.partial(jax.custom_vjp, nondiff_argnums=(2,))
def _shift_rows(x, keep, s):
    return pltpu.roll(x, s, 0) * keep


def _shift_rows_fwd(x, keep, s):
    return pltpu.roll(x, s, 0) * keep, keep


def _shift_rows_bwd(s, keep, g):
    return pltpu.roll(g * keep, keep.shape[0] - s, 0), jnp.zeros_like(keep)


_shift_rows.defvjp(_shift_rows_fwd, _shift_rows_bwd)


def _sigmoid(x):
    return 0.5 * jnp.tanh(0.5 * x) + 0.5


def _softplus(x):
    return jnp.maximum(x, 0.0) + jnp.log(1.0 + jnp.exp(jnp.minimum(x, -x)))


def _silu(x):
    return x * _sigmoid(x)


def _rms(x, w, n=None):
    n = x.shape[-1] if n is None else n
    r = lax.rsqrt(jnp.sum(x * x, axis=-1, keepdims=True) * (1.0 / n) + EPS)
    return x * r * w


def _rope(t, cos_f, sin_f):
    return t * cos_f + _swap_halves(t) * sin_f


def _rope_tables(pos_col, freq_row, sign_row):
    ang = pos_col.astype(F32) * freq_row
    return jnp.cos(ang), jnp.sin(ang) * sign_row


def _onehot_row(lane):
    return (lax.broadcasted_iota(jnp.int32, (1, 128), 1) == lane).astype(F32)


def _row_spec(tm, w):
    return pl.BlockSpec((tm, w), lambda i: (i, 0))


def _const_spec(shape):
    return pl.BlockSpec(shape, lambda *_: (0,) * len(shape))


def _in_proj_fwd(x2, w_an, w_in_p):
    T, D = x2.shape
    tm = min(TOKEN_TILE, T)

    def body(x_ref, wn_ref, w_ref, xn_ref, lat_ref, gqkv_ref, gz_ref, gab_ref):
        x = x_ref[...]
        r = lax.rsqrt(jnp.mean(x * x, axis=-1, keepdims=True) + EPS)
        xn = (x * r * wn_ref[...]).astype(BF16)
        xn_ref[...] = xn
        for ref, (a, b) in zip((lat_ref, gqkv_ref, gz_ref, gab_ref), PROJ_SPLITS):
            ref[...] = _dg(xn, w_ref[a:b, :], 1, 1, None)

    widths = [b - a for a, b in PROJ_SPLITS]
    return pl.pallas_call(
        body, grid=(T // tm,),
        in_specs=[_row_spec(tm, D), _const_spec((1, D)), _const_spec((PROJ_W, D))],
        out_specs=[_row_spec(tm, D)] + [_row_spec(tm, w) for w in widths],
        out_shape=[_sds((T, D), BF16)] + [_sds((T, w), F32) for w in widths],
        compiler_params=_params(("parallel",)), name="in_proj_fwd",
    )(x2, w_an, w_in_p)


def _in_proj_bwd(pieces, w_in_p, x2, w_an, dh, partials):
    T, D = x2.shape
    tm = min(TOKEN_TILE, T)
    widths = [p.shape[1] for p in pieces]
    starts = [sum(widths[:i]) for i in range(len(widths))]
    assert sum(widths) == PROJ_W
    npc, ns = len(pieces), len(partials)

    def body(*refs):
        piece_refs = refs[:npc]
        w_ref, x_ref, wn_ref, dh_ref = refs[npc:npc + 4]
        src_refs = refs[npc + 4:npc + 4 + ns]
        dx_ref, dwn_ref = refs[npc + 4 + ns:npc + 6 + ns]
        dst_refs = refs[npc + 6 + ns:npc + 6 + 2 * ns]
        sems = refs[npc + 6 + 2 * ns:]

        @pl.when(pl.program_id(0) == 0)
        def _():
            for start in _scatter_copies(src_refs, dst_refs, *sems)[0]:
                start()
            dwn_ref[...] = jnp.zeros_like(dwn_ref)

        dxn = jnp.zeros((tm, D), F32)
        for ref, a, width in zip(piece_refs, starts, widths):
            dxn += _dg(ref[...], w_ref[a:a + width, :], 1, 0, None)
        _, pull = jax.vjp(_rms, x_ref[...], wn_ref[...])
        dx, dwn = pull(dxn)
        dx_ref[...] = dx + dh_ref[...]
        dwn_ref[...] += dwn

        @pl.when(pl.program_id(0) == T // tm - 1)
        def _():
            for wait in _scatter_copies(src_refs, dst_refs, *sems)[1]:
                wait()

    return pl.pallas_call(
        body, grid=(T // tm,),
        in_specs=[_row_spec(tm, w) for w in widths] + [_const_spec((PROJ_W, D)), _row_spec(tm, D), _const_spec((1, D)),
                                                       _row_spec(tm, D)] + [_ANY] * ns,
        out_specs=[_row_spec(tm, D), _const_spec((1, D))] + [_ANY] * ns,
        out_shape=[_sds((T, D), F32), _sds((1, D), F32)] + [_scattered_shape(p) for p in partials],
        scratch_shapes=_scatter_scratch(ns),
        compiler_params=_params(("arbitrary",)), name="in_proj_bwd",
    )(*pieces, w_in_p, x2, w_an, dh, *partials)


def _wgrad_pieces(pieces, b, name):
    T, k2 = b.shape
    tt = min(TOKEN_TILE, T)
    widths = [p.shape[1] for p in pieces]
    starts = [sum(widths[:i]) for i in range(len(widths))]
    k1 = sum(widths)

    def body(*refs):
        piece_refs, (b_ref, o_ref, acc_ref) = refs[:len(pieces)], refs[len(pieces):]
        t = pl.program_id(0)

        @pl.when(t == 0)
        def _():
            acc_ref[...] = jnp.zeros_like(acc_ref)

        bt = b_ref[...].astype(BF16)
        for ref, r0, width in zip(piece_refs, starts, widths):
            acc_ref[r0:r0 + width, :] += jnp.dot(ref[...].T, bt, preferred_element_type=F32)

        @pl.when(t == T // tt - 1)
        def _():
            o_ref[...] = acc_ref[...].astype(o_ref.dtype)

    return pl.pallas_call(
        body, grid=(T // tt,),
        in_specs=[pl.BlockSpec((tt, w), lambda t: (t, 0)) for w in widths] + [pl.BlockSpec((tt, k2), lambda t: (t, 0))],
        out_specs=_const_spec((k1, k2)), out_shape=_sds((k1, k2), BF16), scratch_shapes=[pltpu.VMEM((k1, k2), F32)],
        compiler_params=_params(("arbitrary",)), name=name,
    )(*pieces, b)


def _wgrad(a, b, name, column_shards=1, out_dtype=BF16):
    T, k1 = a.shape
    k2 = b.shape[1]
    per_shard = k2 // column_shards
    tt = min(TOKEN_TILE, T)
    b1 = k1
    while b1 * k2 * 4 > WGRAD_OUT_BYTES and b1 % 256 == 0:
        b1 //= 2
    step = _block(per_shard)

    def body(a_ref, b_ref, o_ref, acc_ref):
        t = pl.program_id(1)

        @pl.when(t == 0)
        def _():
            acc_ref[...] = jnp.zeros_like(acc_ref)

        a_t = a_ref[...].astype(BF16).T
        for c0 in range(0, k2, step):
            part = jnp.dot(a_t, b_ref[:, c0:c0 + step].astype(BF16), preferred_element_type=F32)
            if column_shards == 1:
                acc_ref[:, c0:c0 + step] += part
            else:
                acc_ref[c0 // per_shard, :, c0 % per_shard:c0 % per_shard + step] += part

        @pl.when(t == T // tt - 1)
        def _():
            o_ref[...] = acc_ref[...].astype(o_ref.dtype)

    if column_shards == 1:
        block, out_spec, out_shape = (b1, k2), pl.BlockSpec((b1, k2), lambda i, t: (i, 0)), _sds((k1, k2), out_dtype)
    else:
        block = (column_shards, b1, per_shard)
        out_spec, out_shape = pl.BlockSpec(block, lambda i, t: (0, i, 0)), _sds((column_shards, k1, per_shard), out_dtype)
    return pl.pallas_call(
        body, grid=(k1 // b1, T // tt),
        in_specs=[pl.BlockSpec((tt, b1), lambda i, t: (t, i)), pl.BlockSpec((tt, k2), lambda i, t: (t, 0))],
        out_specs=out_spec, out_shape=out_shape, scratch_shapes=[pltpu.VMEM(block, F32)],
        compiler_params=_params(("parallel", "arbitrary")), name=name,
    )(a, b)


def _mla_pre_fn(q_lat, kv_lat, kpe, ln_q, ln_kv, w_list, qn_n, qn_p, kn_n, kn_p, cos_f, sin_f):
    qn = _rms(q_lat, ln_q)
    kvn = _rms(kv_lat, ln_kv)
    kp = _rope(_rms(kpe, kn_p, ROPE_DIM), cos_f, sin_f)
    outs = []
    for h in range(HEADS):
        outs.append(_rms(_bf_nn(qn, w_list[h]), qn_n))
        outs.append(_rope(_rms(_bf_nn(qn, w_list[HEADS + h]), qn_p, ROPE_DIM), cos_f, sin_f))
        outs.append(_rms(_bf_nn(kvn, w_list[2 * HEADS + h]), kn_n))
        outs.append(_bf_nn(kvn, w_list[3 * HEADS + h]))
    return tuple(outs) + (kp,)


def _mla_pre_operands(lat_ref, pos_ref, ln_ref, w_ref, nw_ref, rope_ref):
    cos_f, sin_f = _rope_tables(pos_ref[...], rope_ref[0:1, :], rope_ref[1:2, :])
    diff = (lat_ref[:, 0:LORA], lat_ref[:, LORA:2 * LORA], lat_ref[:, 2 * LORA:LAT_W], ln_ref[0:1, :], ln_ref[1:2, :],
            [w_ref[i].astype(F32) for i in range(4 * HEADS)], nw_ref[0:1, :], nw_ref[1:2, :], nw_ref[2:3, :], nw_ref[3:4, :])
    return diff, cos_f, sin_f


def _mla_pre_fwd(lat, pos, ln_w, w_mla, nw, rope_rows):
    T = lat.shape[0]
    tm = min(TOKEN_TILE, T)

    def body(lat_ref, pos_ref, ln_ref, w_ref, nw_ref, rope_ref, q_ref, k_ref, v_ref):
        diff, cos_f, sin_f = _mla_pre_operands(lat_ref, pos_ref, ln_ref, w_ref, nw_ref, rope_ref)
        outs = _mla_pre_fn(*diff, cos_f, sin_f)
        kp = outs[-1].astype(BF16)
        for h in range(HEADS):
            q_n, q_p, k_n, v = outs[4 * h:4 * h + 4]
            q_ref[:, h * QK_PAD:h * QK_PAD + HEAD_DIM] = q_n.astype(BF16)
            q_ref[:, h * QK_PAD + HEAD_DIM:(h + 1) * QK_PAD] = q_p.astype(BF16)
            k_ref[:, h * QK_PAD:h * QK_PAD + HEAD_DIM] = k_n.astype(BF16)
            k_ref[:, h * QK_PAD + HEAD_DIM:(h + 1) * QK_PAD] = kp
            v_ref[:, h * HEAD_DIM:(h + 1) * HEAD_DIM] = v.astype(BF16)

    return pl.pallas_call(
        body, grid=(T // tm,),
        in_specs=[_row_spec(tm, LAT_W), _row_spec(tm, 1), _const_spec((2, LORA)), _const_spec((4 * HEADS, LORA, 128)),
                  _const_spec((8, 128)), _const_spec((8, 128))],
        out_specs=[_row_spec(tm, HEADS * QK_PAD), _row_spec(tm, HEADS * QK_PAD), _row_spec(tm, HEADS * HEAD_DIM)],
        out_shape=[_sds((T, HEADS * QK_PAD), BF16), _sds((T, HEADS * QK_PAD), BF16), _sds((T, HEADS * HEAD_DIM), BF16)],
        compiler_params=_params(("parallel",)), name="mla_pre_fwd",
    )(lat, pos, ln_w, w_mla, nw, rope_rows)


def _mla_pre_bwd(lat, pos, ln_w, w_mla, nw, rope_rows, dq, dk, dv, halves):
    T = lat.shape[0]
    tm = min(TOKEN_TILE, T)
    ns = len(halves)

    def body(*refs):
        lat_ref, pos_ref, ln_ref, w_ref, nw_ref, rope_ref, dq_ref, dk_ref, dv_ref = refs[:9]
        src_refs = refs[9:9 + ns]
        dlat_ref, dln_ref, dw_ref, dnw_ref = refs[9 + ns:13 + ns]
        dst_refs = refs[13 + ns:13 + 2 * ns]
        sems = refs[13 + 2 * ns:]

        @pl.when(pl.program_id(0) == 0)
        def _():
            for start in _swap_copies(src_refs, dst_refs, *sems)[0]:
                start()
            dln_ref[...] = jnp.zeros_like(dln_ref)
            dw_ref[...] = jnp.zeros_like(dw_ref)
            dnw_ref[...] = jnp.zeros_like(dnw_ref)

        diff, cos_f, sin_f = _mla_pre_operands(lat_ref, pos_ref, ln_ref, w_ref, nw_ref, rope_ref)
        _, pull = jax.vjp(lambda *a: _mla_pre_fn(*a, cos_f, sin_f), *diff)
        cts = []
        d_kp = jnp.zeros((tm, 128), F32)
        for h in range(HEADS):
            cts.append(dq_ref[:, h * QK_PAD:h * QK_PAD + HEAD_DIM])
            cts.append(dq_ref[:, h * QK_PAD + HEAD_DIM:(h + 1) * QK_PAD])
            cts.append(dk_ref[:, h * QK_PAD:h * QK_PAD + HEAD_DIM])
            cts.append(dv_ref[:, h * HEAD_DIM:(h + 1) * HEAD_DIM])
            d_kp += dk_ref[:, h * QK_PAD + HEAD_DIM:(h + 1) * QK_PAD]
        d_ql, d_kvl, d_kpe, d_lnq, d_lnkv, d_w, d_qn_n, d_qn_p, d_kn_n, d_kn_p = pull(tuple(cts) + (d_kp,))
        dlat_ref[:, 0:LORA] = d_ql.astype(BF16)
        dlat_ref[:, LORA:2 * LORA] = d_kvl.astype(BF16)
        dlat_ref[:, 2 * LORA:LAT_W] = d_kpe.astype(BF16)
        dln_ref[0:1, :] += d_lnq
        dln_ref[1:2, :] += d_lnkv
        for i in range(4 * HEADS):
            dw_ref[i] += d_w[i]
        for i, d in enumerate((d_qn_n, d_qn_p, d_kn_n, d_kn_p)):
            dnw_ref[i:i + 1, :] += d

        @pl.when(pl.program_id(0) == T // tm - 1)
        def _():
            for wait in _swap_copies(src_refs, dst_refs, *sems)[1]:
                wait()

    return pl.pallas_call(
        body, grid=(T // tm,),
        in_specs=[_row_spec(tm, LAT_W), _row_spec(tm, 1), _const_spec((2, LORA)), _const_spec((4 * HEADS, LORA, 128)),
                  _const_spec((8, 128)), _const_spec((8, 128)),
                  _row_spec(tm, HEADS * QK_PAD), _row_spec(tm, HEADS * QK_PAD), _row_spec(tm, HEADS * HEAD_DIM)] + [_ANY] * ns,
        out_specs=[_row_spec(tm, LAT_W), _const_spec((2, LORA)), _const_spec((4 * HEADS, LORA, 128)), _const_spec((8, 128))]
                  + [_ANY] * ns,
        out_shape=[_sds((T, LAT_W), BF16), _sds((2, LORA), F32), _sds((4 * HEADS, LORA, 128), F32), _sds((8, 128), F32)]
                  + [_swapped_shape(h) for h in halves],
        scratch_shapes=_swap_scratch(ns),
        compiler_params=_params(("arbitrary",)), name="mla_pre_bwd",
    )(lat, pos, ln_w, w_mla, nw, rope_rows, dq, dk, dv, *halves)


def _causal_mask(i, j, tq, tk):
    row = i * tq + lax.broadcasted_iota(jnp.int32, (tq, tk), 0)
    col = j * tk + lax.broadcasted_iota(jnp.int32, (tq, tk), 1)
    return col <= row


def _attn_fwd(q, k, v, shards):
    B, S, _ = q.shape
    t = min(ATTN_TILE, S)
    nq = S // t
    ns = len(shards)

    hp = ATTN_HEADS_PER_STEP
    qk = lambda h: slice(h * QK_PAD, (h + 1) * QK_PAD)
    vd = lambda h: slice(h * HEAD_DIM, (h + 1) * HEAD_DIM)

    def body(*refs):
        q_ref, k_ref, v_ref = refs[:3]
        src_refs = refs[3:3 + ns]
        o_ref, lse_ref = refs[3 + ns:5 + ns]
        dst_refs = refs[5 + ns:5 + 2 * ns]
        sems = refs[5 + 2 * ns:]
        b, g, i = pl.program_id(0), pl.program_id(1), pl.program_id(2)
        qb = [q_ref[0, :, qk(h)] for h in range(hp)]

        step_no = (b * (HEADS // hp) + g) * nq + i
        for phase, at in enumerate((0, (3 * B * (HEADS // hp) * nq) // 4)):
            @pl.when(step_no == at)
            def _(phase=phase):
                for call in _gather_copies(src_refs, dst_refs, *sems)[phase]:
                    call()

        def step(j, carry, diagonal):
            rows = pl.ds(pl.multiple_of(j * t, t), t)
            s = [_dg(qb[h], k_ref[0, rows, qk(h)], 1, 1, None) * ATTN_SCALE for h in range(hp)]
            if diagonal:
                keep = _causal_mask(0, 0, t, t)
                s = [jnp.where(keep, x, -1e30) for x in s]
            m_new = [jnp.maximum(carry[h][0], jnp.max(s[h], axis=-1, keepdims=True)) for h in range(hp)]
            p = [jnp.exp(s[h] - m_new[h]) for h in range(hp)]
            alpha = [jnp.exp(carry[h][0] - m_new[h]) for h in range(hp)]
            l = [alpha[h] * carry[h][1] + jnp.sum(p[h], axis=-1, keepdims=True) for h in range(hp)]
            pv = [jnp.dot(p[h].astype(BF16), v_ref[0, rows, vd(h)], preferred_element_type=F32) for h in range(hp)]
            return tuple((m_new[h], l[h], alpha[h] * carry[h][2] + pv[h]) for h in range(hp))

        init = tuple((jnp.full((t, 1), -1e30, F32), jnp.zeros((t, 1), F32), jnp.zeros((t, HEAD_DIM), F32)) for _ in range(hp))
        below = lax.fori_loop(0, i, lambda j, carry: step(j, carry, False), init)
        for h, (m, l, acc) in enumerate(step(i, below, True)):
            o_ref[0, :, vd(h)] = acc / l
            lse_ref[0, h] = m + jnp.log(l)

        @pl.when((b == B - 1) & (g == HEADS // hp - 1) & (i == nq - 1))
        def _():
            for wait in _gather_copies(src_refs, dst_refs, *sems)[2]:
                wait()

    return pl.pallas_call(
        body, grid=(B, HEADS // hp, nq),
        in_specs=[pl.BlockSpec((1, t, hp * QK_PAD), lambda b, g, i: (b, i, g)),
                  pl.BlockSpec((1, S, hp * QK_PAD), lambda b, g, i: (b, 0, g)),
                  pl.BlockSpec((1, S, hp * HEAD_DIM), lambda b, g, i: (b, 0, g))] + [_ANY] * ns,
        out_specs=[pl.BlockSpec((1, t, hp * HEAD_DIM), lambda b, g, i: (b, i, g)),
                   pl.BlockSpec((1, hp, t, 1), lambda b, g, i: (b, g, i, 0))] + [_ANY] * ns,
        out_shape=[_sds((B, S, HEADS * HEAD_DIM), F32), _sds((B, HEADS, S, 1), F32)] + [_sds((4,) + s.shape, s.dtype) for s in shards],
        scratch_shapes=_gather_scratch(ns),
        compiler_params=_params(("arbitrary", "arbitrary", "arbitrary")), name="attn_fwd",
    )(q, k, v, *shards)


def _attn_bwd(q, k, v, o, lse, do, partials):
    B, S, _ = q.shape
    t = min(ATTN_TILE, S)
    nq = S // t
    ns = len(partials)

    hp = ATTN_HEADS_PER_STEP
    qk = lambda h: slice(h * QK_PAD, (h + 1) * QK_PAD)
    vd = lambda h: slice(h * HEAD_DIM, (h + 1) * HEAD_DIM)
    heads = range(hp)

    def body(*refs):
        q_ref, k_ref, v_ref, o_ref, lse_ref, do_ref = refs[:6]
        src_refs = refs[6:6 + ns]
        dq_ref, dk_ref, dv_ref = refs[6 + ns:9 + ns]
        dst_refs = refs[9 + ns:9 + 2 * ns]
        dsum_ref, send_sems, recv_sems, local_sems = refs[9 + 2 * ns:]
        b, g, j = pl.program_id(0), pl.program_id(1), pl.program_id(2)

        @pl.when((b == 0) & (g == 0) & (j == 0))
        def _():
            for start in _scatter_copies(src_refs, dst_refs, send_sems, recv_sems, local_sems)[0]:
                start()

        @pl.when(j == 0)
        def _():
            dq_ref[...] = jnp.zeros_like(dq_ref)
            for h in heads:
                dsum_ref[h] = jnp.sum(do_ref[0, :, vd(h)] * o_ref[0, :, vd(h)], axis=-1, keepdims=True)

        kb = [k_ref[0, :, qk(h)] for h in heads]
        vb = [v_ref[0, :, vd(h)] for h in heads]

        def step(i, carry, diagonal):
            rows = pl.ds(pl.multiple_of(i * t, t), t)
            qb = [q_ref[0, rows, qk(h)] for h in heads]
            dob = [do_ref[0, rows, vd(h)].astype(BF16) for h in heads]
            s = [_dg(qb[h], kb[h], 1, 1, None) * ATTN_SCALE for h in heads]
            p = [jnp.exp(s[h] - lse_ref[0, h, rows, :]) for h in heads]
            if diagonal:
                keep = _causal_mask(0, 0, t, t)
                p = [jnp.where(keep, x, 0.0) for x in p]
            dp = [_dg(dob[h], vb[h], 1, 1, None) for h in heads]
            dv = [carry[h][1] + _dg(p[h].astype(BF16), dob[h], 0, 0, None) for h in heads]
            ds = [(p[h] * (dp[h] - dsum_ref[h, rows, :]) * ATTN_SCALE).astype(BF16) for h in heads]
            for h in heads:
                dq_ref[0, rows, qk(h)] += jnp.dot(ds[h], kb[h], preferred_element_type=F32)
            return tuple((carry[h][0] + _dg(ds[h], qb[h], 0, 0, None), dv[h]) for h in heads)

        zeros = tuple((jnp.zeros((t, QK_PAD), F32), jnp.zeros((t, HEAD_DIM), F32)) for _ in heads)
        on_diagonal = step(j, zeros, True)
        done = lax.fori_loop(j + 1, nq, lambda i, carry: step(i, carry, False), on_diagonal)
        for h, (dk, dv) in enumerate(done):
            dk_ref[0, :, qk(h)] = dk
            dv_ref[0, :, vd(h)] = dv

        @pl.when((b == B - 1) & (g == HEADS // hp - 1) & (j == nq - 1))
        def _():
            for wait in _scatter_copies(src_refs, dst_refs, send_sems, recv_sems, local_sems)[1]:
                wait()

    return pl.pallas_call(
        body, grid=(B, HEADS // hp, nq),
        in_specs=[pl.BlockSpec((1, S, hp * QK_PAD), lambda b, g, j: (b, 0, g)),
                  pl.BlockSpec((1, t, hp * QK_PAD), lambda b, g, j: (b, j, g)),
                  pl.BlockSpec((1, t, hp * HEAD_DIM), lambda b, g, j: (b, j, g)),
                  pl.BlockSpec((1, S, hp * HEAD_DIM), lambda b, g, j: (b, 0, g)),
                  pl.BlockSpec((1, hp, S, 1), lambda b, g, j: (b, g, 0, 0)),
                  pl.BlockSpec((1, S, hp * HEAD_DIM), lambda b, g, j: (b, 0, g))] + [_ANY] * ns,
        out_specs=[pl.BlockSpec((1, S, hp * QK_PAD), lambda b, g, j: (b, 0, g)),
                   pl.BlockSpec((1, t, hp * QK_PAD), lambda b, g, j: (b, j, g)),
                   pl.BlockSpec((1, t, hp * HEAD_DIM), lambda b, g, j: (b, j, g))] + [_ANY] * ns,
        out_shape=[_sds((B, S, HEADS * QK_PAD), F32), _sds((B, S, HEADS * QK_PAD), F32), _sds((B, S, HEADS * HEAD_DIM), F32)]
                  + [_scattered_shape(p) for p in partials],
        scratch_shapes=[pltpu.VMEM((hp, S, 1), F32)] + _scatter_scratch(ns),
        compiler_params=_params(("arbitrary", "arbitrary", "arbitrary")), name="attn_bwd",
    )(q, k, v, o, lse, do, *partials)


def _gdn_pre_fn(xq, xk, xv, wq, wk, wv, keeps):
    def conv_silu(x, w):
        acc = x * w[3]
        for s in (1, 2, 3):
            acc = acc + _shift_rows(x, keeps[s - 1], s) * w[3 - s]
        return _silu(acc)

    def l2(x):
        return x * lax.rsqrt(jnp.sum(x * x, axis=-1, keepdims=True) + EPS)

    return l2(conv_silu(xq, wq)) * (HEAD_DIM ** -0.5), l2(conv_silu(xk, wk)), conv_silu(xv, wv)


def _gdn_pre_specs(S):
    x_specs = [pl.BlockSpec((1, S, HEAD_DIM), lambda h, b, g=g: (b, 0, g * HEADS + h)) for g in range(3)]
    w_specs = [pl.BlockSpec((CONV_TAPS, HEAD_DIM), lambda h, b, g=g: (0, g * HEADS + h)) for g in range(3)]
    out_spec = pl.BlockSpec((1, S, HEAD_DIM), lambda h, b: (b, 0, h))
    return x_specs, w_specs, out_spec


def _row_keeps(S):
    t = lax.broadcasted_iota(jnp.int32, (S, HEAD_DIM), 0)
    return [(t >= s).astype(F32) for s in (1, 2, 3)]


def _gdn_pre_fwd(gqkv, conv_w):
    B, S, _ = gqkv.shape
    x_specs, w_specs, out_spec = _gdn_pre_specs(S)

    def body(xq_ref, xk_ref, xv_ref, wq_ref, wk_ref, wv_ref, q_ref, k_ref, v_ref):
        taps = [[w[i:i + 1, :] for i in range(CONV_TAPS)] for w in (wq_ref, wk_ref, wv_ref)]
        q, k, v = _gdn_pre_fn(xq_ref[0], xk_ref[0], xv_ref[0], *taps, _row_keeps(S))
        q_ref[0], k_ref[0], v_ref[0] = q, k, v

    return pl.pallas_call(
        body, grid=(HEADS, B), in_specs=x_specs + w_specs, out_specs=[out_spec] * 3,
        out_shape=[_sds((B, S, HEADS * HEAD_DIM), F32)] * 3,
        compiler_params=_params(("parallel", "parallel")), name="gdn_pre_fwd",
    )(gqkv, gqkv, gqkv, conv_w, conv_w, conv_w)


def _gdn_pre_bwd(gqkv, conv_w, dq, dk, dv, halves):
    B, S, _ = gqkv.shape
    x_specs, w_specs, out_spec = _gdn_pre_specs(S)
    dw_spec = pl.BlockSpec((CONV_TAPS, HEAD_DIM), lambda h, b: (0, h))
    ns = len(halves)

    def body(*refs):
        xq_ref, xk_ref, xv_ref, wq_ref, wk_ref, wv_ref, dq_ref, dk_ref, dv_ref = refs[:9]
        src_refs = refs[9:9 + ns]
        dxq_ref, dxk_ref, dxv_ref, dwq_ref, dwk_ref, dwv_ref = refs[9 + ns:15 + ns]
        dst_refs = refs[15 + ns:15 + 2 * ns]
        sems = refs[15 + 2 * ns:]
        first = (pl.program_id(0) == 0) & (pl.program_id(1) == 0)
        last = (pl.program_id(0) == HEADS - 1) & (pl.program_id(1) == B - 1)

        @pl.when(first)
        def _():
            for start in _swap_copies(src_refs, dst_refs, *sems)[0]:
                start()

        @pl.when(pl.program_id(1) == 0)
        def _():
            for r in (dwq_ref, dwk_ref, dwv_ref):
                r[...] = jnp.zeros_like(r)

        taps = [[w[i:i + 1, :] for i in range(CONV_TAPS)] for w in (wq_ref, wk_ref, wv_ref)]
        keeps = _row_keeps(S)
        _, pull = jax.vjp(lambda *a: _gdn_pre_fn(*a, keeps), xq_ref[0], xk_ref[0], xv_ref[0], *taps)
        dxq, dxk, dxv, dwq, dwk, dwv = pull((dq_ref[0], dk_ref[0], dv_ref[0]))
        dxq_ref[0], dxk_ref[0], dxv_ref[0] = dxq.astype(BF16), dxk.astype(BF16), dxv.astype(BF16)
        for ref, dw in ((dwq_ref, dwq), (dwk_ref, dwk), (dwv_ref, dwv)):
            for i in range(CONV_TAPS):
                ref[i:i + 1, :] += dw[i]

        @pl.when(last)
        def _():
            for wait in _swap_copies(src_refs, dst_refs, *sems)[1]:
                wait()

    hw = HEADS * HEAD_DIM
    return pl.pallas_call(
        body, grid=(HEADS, B), in_specs=x_specs + w_specs + [out_spec] * 3 + [_ANY] * ns,
        out_specs=[out_spec] * 3 + [dw_spec] * 3 + [_ANY] * ns,
        out_shape=[_sds((B, S, hw), BF16)] * 3 + [_sds((CONV_TAPS, hw), F32)] * 3 + [_swapped_shape(h) for h in halves],
        scratch_shapes=_swap_scratch(ns),
        compiler_params=_params(("arbitrary", "arbitrary")), name="gdn_pre_bwd",
    )(gqkv, gqkv, gqkv, conv_w, conv_w, conv_w, dq, dk, dv, *halves)


def _chunk_masks():
    i = lax.broadcasted_iota(jnp.int32, (CHUNK, CHUNK), 0)
    j = lax.broadcasted_iota(jnp.int32, (CHUNK, CHUNK), 1)
    lower, after = (j <= i).astype(F32), (j > i).astype(F32)
    return {"le": lower, "le_gt": jnp.concatenate([lower, after], axis=0), "strict": (j < i).astype(F32)}


def _gdn_chunk_fn(groups, masks):
    lane = lax.broadcasted_iota(jnp.int32, (groups, 1, 128), 2)
    head = lax.broadcasted_iota(jnp.int32, (groups, 1, 128), 0) % HEADS
    pick_a, pick_b = (lane == head).astype(F32), (lane == head + HEADS).astype(F32)
    lower, lower_after, strict = (jnp.broadcast_to(masks[n], (groups,) + masks[n].shape) for n in ("le", "le_gt", "strict"))
    ones_row = jnp.ones((1, 1, HEAD_DIM), F32)

    def f(q, k, v, gab, a_row, dt_row, state):
        ga = jnp.sum(gab * pick_a, axis=2, keepdims=True)
        gb = jnp.sum(gab * pick_b, axis=2, keepdims=True)
        a_log = jnp.sum(a_row * pick_a, axis=2, keepdims=True)
        dt_bias = jnp.sum(dt_row * pick_a, axis=2, keepdims=True)
        beta = _sigmoid(gb)
        g = -jnp.exp(a_log) * _softplus(ga + dt_bias)
        g_wide = g * ones_row
        cum, rest = _row_halves(_hi_nn(lower_after, g_wide))
        total = jnp.sum(g_wide, axis=1, keepdims=True)
        diff = _hi_nn(lower, g * strict)
        decay = lower * jnp.exp(diff)
        e_cum = jnp.exp(cum)
        lmat = strict * (beta * _bf_nt(k, k) * decay)
        u, w = _lane_halves(_unit_lower_solve(lmat, jnp.concatenate([v * beta, k * (beta * e_cum)], axis=2)))
        attn = _bf_nt(q, k) * decay
        v_new = u - _bf_nn(w, state)
        o = _bf_nn(q * e_cum, state) + _bf_nn(attn, v_new)
        new_state = state * jnp.exp(total) + _bf_tn(k * jnp.exp(rest), v_new)
        return o, new_state

    return f


def _gdn_chunk_fwd(q, k, v, gab, scal, shards):
    B, S, W = q.shape
    N = S // CHUNK
    ns = len(shards)

    def body(*refs):
        q_ref, k_ref, v_ref, gab_ref, sc_ref = refs[:5]
        src_refs = refs[5:5 + ns]
        o_ref, st_ref = refs[5 + ns:7 + ns]
        dst_refs = refs[7 + ns:7 + 2 * ns]
        state_ref, send_sems, recv_sems, local_sems = refs[7 + 2 * ns:]
        n = pl.program_id(0)

        @pl.when(n == 0)
        def _():
            for start in _gather_copies(src_refs, dst_refs, send_sems, recv_sems, local_sems)[0]:
                start()
            state_ref[...] = jnp.zeros_like(state_ref)

        @pl.when(n == (2 * N) // 3)
        def _():
            for pass_on in _gather_copies(src_refs, dst_refs, send_sems, recv_sems, local_sems)[1]:
                pass_on()

        groups = [(b, h) for b in range(B) for h in range(HEADS)]
        gather = lambda ref: jnp.stack([ref[b, :, h * HEAD_DIM:(h + 1) * HEAD_DIM] for b, h in groups])
        state = state_ref[...]
        for i, (b, h) in enumerate(groups):
            st_ref[b, 0, h] = state[i]
        o, new_state = _gdn_chunk_fn(len(groups), _chunk_masks())(
            gather(q_ref), gather(k_ref), gather(v_ref), jnp.stack([gab_ref[b] for b, _ in groups]), sc_ref[0:1, :], sc_ref[1:2, :], state)
        for i, (b, h) in enumerate(groups):
            o_ref[b, :, h * HEAD_DIM:(h + 1) * HEAD_DIM] = o[i]
        state_ref[...] = new_state

        @pl.when(n == N - 1)
        def _():
            for wait in _gather_copies(src_refs, dst_refs, send_sems, recv_sems, local_sems)[2]:
                wait()

    seq = pl.BlockSpec((B, CHUNK, W), lambda n: (0, n, 0))
    return pl.pallas_call(
        body, grid=(N,),
        in_specs=[seq, seq, seq, pl.BlockSpec((B, CHUNK, GAB_W), lambda n: (0, n, 0)), _const_spec((8, 128))] + [_ANY] * ns,
        out_specs=[seq, pl.BlockSpec((B, 1, HEADS, HEAD_DIM, HEAD_DIM), lambda n: (0, n, 0, 0, 0))] + [_ANY] * ns,
        out_shape=[_sds((B, S, W), F32), _sds((B, N, HEADS, HEAD_DIM, HEAD_DIM), F32)] + [_sds((4,) + s.shape, s.dtype) for s in shards],
        scratch_shapes=[pltpu.VMEM((B * HEADS, HEAD_DIM, HEAD_DIM), F32)] + _gather_scratch(ns),
        compiler_params=_params(("arbitrary",)), name="gdn_chunk_fwd",
    )(q, k, v, gab, scal, *shards)


def _gdn_chunk_bwd(q, k, v, gab, scal, states, do, partials):
    B, S, W = q.shape
    N = S // CHUNK
    ns = len(partials)

    def body(*refs):
        q_ref, k_ref, v_ref, gab_ref, sc_ref, st_ref, do_ref = refs[:7]
        src_refs = refs[7:7 + ns]
        dq_ref, dk_ref, dv_ref, dgab_ref, dsc_ref = refs[7 + ns:12 + ns]
        dst_refs = refs[12 + ns:12 + 2 * ns]
        dstate_ref, send_sems, recv_sems, local_sems = refs[12 + 2 * ns:]
        n = pl.program_id(0)

        @pl.when(n == 0)
        def _():
            for start in _scatter_copies(src_refs, dst_refs, send_sems, recv_sems, local_sems)[0]:
                start()
            dstate_ref[...] = jnp.zeros_like(dstate_ref)
            dsc_ref[...] = jnp.zeros_like(dsc_ref)

        groups = [(b, h) for b in range(B) for h in range(HEADS)]
        gather = lambda ref: jnp.stack([ref[b, :, h * HEAD_DIM:(h + 1) * HEAD_DIM] for b, h in groups])
        _, pull = jax.vjp(_gdn_chunk_fn(len(groups), _chunk_masks()), gather(q_ref), gather(k_ref), gather(v_ref),
                          jnp.stack([gab_ref[b] for b, _ in groups]), sc_ref[0:1, :], sc_ref[1:2, :],
                          jnp.stack([st_ref[b, 0, h] for b, h in groups]))
        dq, dk, dv, dg, d_a, d_dt, dstate = pull((gather(do_ref), dstate_ref[...]))
        for i, (b, h) in enumerate(groups):
            lanes = slice(h * HEAD_DIM, (h + 1) * HEAD_DIM)
            dq_ref[b, :, lanes] = dq[i]
            dk_ref[b, :, lanes] = dk[i]
            dv_ref[b, :, lanes] = dv[i]
        for b in range(B):
            dgab_ref[b] = sum(dg[b * HEADS + h] for h in range(HEADS)).astype(BF16)
        dstate_ref[...] = dstate
        dsc_ref[0:1, :] += d_a
        dsc_ref[1:2, :] += d_dt

        @pl.when(n == N - 1)
        def _():
            for wait in _scatter_copies(src_refs, dst_refs, send_sems, recv_sems, local_sems)[1]:
                wait()

    seq = pl.BlockSpec((B, CHUNK, W), lambda n: (0, N - 1 - n, 0))
    gab_spec = pl.BlockSpec((B, CHUNK, GAB_W), lambda n: (0, N - 1 - n, 0))
    return pl.pallas_call(
        body, grid=(N,),
        in_specs=[seq, seq, seq, gab_spec, _const_spec((8, 128)),
                  pl.BlockSpec((B, 1, HEADS, HEAD_DIM, HEAD_DIM), lambda n: (0, N - 1 - n, 0, 0, 0)), seq] + [_ANY] * ns,
        out_specs=[seq, seq, seq, gab_spec, _const_spec((8, 128))] + [_ANY] * ns,
        out_shape=[_sds((B, S, W), F32)] * 3 + [_sds((B, S, GAB_W), BF16), _sds((8, 128), F32)] + [_scattered_shape(p) for p in partials],
        scratch_shapes=[pltpu.VMEM((B * HEADS, HEAD_DIM, HEAD_DIM), F32)] + _scatter_scratch(ns),
        compiler_params=_params(("arbitrary",)), name="gdn_chunk_bwd",
    )(q, k, v, gab, scal, states, do, *partials)


def _mix_fn(ao, go, gz, w_mla, w_gdn):
    return tuple(_rms(ao[h], w_mla[h]) for h in range(HEADS)) + tuple(_rms(go[h], w_gdn) * _silu(gz[h]) for h in range(HEADS))


def _mix_operands(ao_ref, go_ref, gz_ref, nw_ref):
    blocks = lambda ref: [ref[:, h * HEAD_DIM:(h + 1) * HEAD_DIM] for h in range(HEADS)]
    return blocks(ao_ref), blocks(go_ref), blocks(gz_ref), [nw_ref[h:h + 1, :] for h in range(HEADS)], nw_ref[HEADS:HEADS + 1, :]


def _mix_fwd(ao, go, gz, nw, w_out, x2):
    T, D = x2.shape
    tm = min(TOKEN_TILE, T)
    MW = 2 * HEADS * HEAD_DIM

    def body(ao_ref, go_ref, gz_ref, nw_ref, w_ref, x_ref, mix_ref, h_ref):
        outs = _mix_fn(*_mix_operands(ao_ref, go_ref, gz_ref, nw_ref))
        for i, piece in enumerate(outs):
            mix_ref[:, i * HEAD_DIM:(i + 1) * HEAD_DIM] = piece.astype(BF16)
        h_ref[...] = x_ref[...] + jnp.dot(mix_ref[...], w_ref[...], preferred_element_type=F32)

    half = HEADS * HEAD_DIM
    return pl.pallas_call(
        body, grid=(T // tm,),
        in_specs=[_row_spec(tm, half), _row_spec(tm, half), _row_spec(tm, half), _const_spec((8, 128)), _const_spec((MW, D)),
                  _row_spec(tm, D)],
        out_specs=[_row_spec(tm, MW), _row_spec(tm, D)],
        out_shape=[_sds((T, MW), BF16), _sds((T, D), F32)],
        compiler_params=_params(("parallel",)), name="mix_fwd",
    )(ao, go, gz, nw, w_out, x2)


def _mix_bwd(ao, go, gz, nw, w_out, dh):
    T, D = dh.shape
    tm = min(TOKEN_TILE, T)
    MW = 2 * HEADS * HEAD_DIM
    half = HEADS * HEAD_DIM

    def body(ao_ref, go_ref, gz_ref, nw_ref, w_ref, dh_ref, dao_ref, dgo_ref, dgz_ref, dnw_ref):
        @pl.when(pl.program_id(0) == 0)
        def _():
            dnw_ref[...] = jnp.zeros_like(dnw_ref)

        d_mix = _dg(dh_ref[...].astype(BF16), w_ref[...], 1, 1, None)
        cts = tuple(d_mix[:, i * HEAD_DIM:(i + 1) * HEAD_DIM] for i in range(2 * HEADS))
        _, pull = jax.vjp(_mix_fn, *_mix_operands(ao_ref, go_ref, gz_ref, nw_ref))
        d_ao, d_go, d_gz, d_wm, d_wg = pull(cts)
        for h in range(HEADS):
            lanes = slice(h * HEAD_DIM, (h + 1) * HEAD_DIM)
            dao_ref[:, lanes] = d_ao[h]
            dgo_ref[:, lanes] = d_go[h]
            dgz_ref[:, lanes] = d_gz[h].astype(BF16)
            dnw_ref[h:h + 1, :] += d_wm[h]
        dnw_ref[HEADS:HEADS + 1, :] += d_wg

    return pl.pallas_call(
        body, grid=(T // tm,),
        in_specs=[_row_spec(tm, half), _row_spec(tm, half), _row_spec(tm, half), _const_spec((8, 128)), _const_spec((MW, D)),
                  _row_spec(tm, D)],
        out_specs=[_row_spec(tm, half)] * 3 + [_const_spec((8, 128))],
        out_shape=[_sds((T, half), F32)] * 2 + [_sds((T, half), BF16), _sds((8, 128), F32)],
        compiler_params=_params(("arbitrary",)), name="mix_bwd",
    )(ao, go, gz, nw, w_out, dh)


def _up_spec(w_up, tf):
    per_shard = w_up.shape[2] // tf
    return pl.BlockSpec((None, w_up.shape[1], tf), lambda i, j: (j // per_shard, 0, j % per_shard))


def _mlp_fwd(h2, w_mn, w_up, w_down, target):
    T, D = h2.shape
    FF = w_down.shape[0]
    tm, tf = min(MLP_TOKEN_TILE, T), min(FF_TILE, w_up.shape[2])
    nf = FF // tf

    def body(h_ref, wn_ref, wu_ref, wd_ref, t_ref, hn_ref, act_ref, dy_ref, sq_ref, acc_ref):
        j = pl.program_id(1)

        @pl.when(j == 0)
        def _():
            hn_ref[...] = _rms(h_ref[...], wn_ref[...]).astype(BF16)
            acc_ref[...] = jnp.zeros_like(acc_ref)

        up = jnp.dot(hn_ref[...], wu_ref[...], preferred_element_type=F32)
        act = jnp.square(jnp.maximum(up, 0.0)).astype(BF16)
        act_ref[...] = act
        acc_ref[...] += jnp.dot(act, wd_ref[...], preferred_element_type=F32)

        @pl.when(j == nf - 1)
        def _():
            err = h_ref[...] + acc_ref[...] - t_ref[...]
            dy_ref[...] = err * (1.0 / D)
            sq_ref[...] = jnp.zeros_like(sq_ref) + jnp.sum(err * err)

    tok = lambda w: pl.BlockSpec((tm, w), lambda i, j: (i, 0))
    return pl.pallas_call(
        body, grid=(T // tm, nf),
        in_specs=[tok(D), _const_spec((1, D)), _up_spec(w_up, tf), pl.BlockSpec((tf, D), lambda i, j: (j, 0)), tok(D)],
        out_specs=[tok(D), pl.BlockSpec((tm, tf), lambda i, j: (i, j)), tok(D), pl.BlockSpec((1, 8, 128), lambda i, j: (i, 0, 0))],
        out_shape=[_sds((T, D), BF16), _sds((T, FF), BF16), _sds((T, D), F32), _sds((T // tm, 8, 128), F32)],
        scratch_shapes=[pltpu.VMEM((tm, D), F32)],
        compiler_params=_params(("parallel", "arbitrary")), name="mlp_fwd",
    )(h2, w_mn, w_up, w_down, target)


def _mlp_bwd(h2, w_mn, act, w_up, w_down, dy):
    T, D = h2.shape
    FF = w_down.shape[0]
    tm, tf = min(MLP_TOKEN_TILE, T), min(FF_TILE, w_up.shape[2])
    nf = FF // tf

    def body(h_ref, wn_ref, act_ref, wu_ref, wd_ref, dy_ref, dh_ref, dup_ref, dwn_ref, acc_ref):
        i, j = pl.program_id(0), pl.program_id(1)

        @pl.when((i == 0) & (j == 0))
        def _():
            dwn_ref[...] = jnp.zeros_like(dwn_ref)

        @pl.when(j == 0)
        def _():
            acc_ref[...] = jnp.zeros_like(acc_ref)

        r = jnp.sqrt(act_ref[...].astype(F32))
        d_act = _dg(dy_ref[...].astype(BF16), wd_ref[...], 1, 1, None)
        d_up = (d_act * (2.0 * r)).astype(BF16)
        dup_ref[...] = d_up
        acc_ref[...] += _dg(d_up, wu_ref[...], 1, 1, None)

        @pl.when(j == nf - 1)
        def _():
            _, pull = jax.vjp(_rms, h_ref[...], wn_ref[...])
            dh, dwn = pull(acc_ref[...])
            dh_ref[...] = dh + dy_ref[...]
            dwn_ref[...] += dwn

    tok = lambda w: pl.BlockSpec((tm, w), lambda i, j: (i, 0))
    ff = pl.BlockSpec((tm, tf), lambda i, j: (i, j))
    return pl.pallas_call(
        body, grid=(T // tm, nf),
        in_specs=[tok(D), _const_spec((1, D)), ff, _up_spec(w_up, tf), pl.BlockSpec((tf, D), lambda i, j: (j, 0)), tok(D)],
        out_specs=[tok(D), ff, _const_spec((1, D))],
        out_shape=[_sds((T, D), F32), _sds((T, FF), BF16), _sds((1, D), F32)],
        scratch_shapes=[pltpu.VMEM((tm, D), F32)],
        compiler_params=_params(("arbitrary", "arbitrary")), name="mlp_bwd",
    )(h2, w_mn, act, w_up, w_down, dy)


def _rope_pad(a):
    z = jnp.zeros(a.shape[:-1] + (ROPE_HALF,), a.dtype)
    return jnp.concatenate([a[..., :ROPE_HALF], z, a[..., ROPE_HALF:], z], axis=-1)


def _rope_unpad(a):
    return jnp.concatenate([a[..., :ROPE_HALF], a[..., 2 * ROPE_HALF:3 * ROPE_HALF]], axis=-1)


_G0 = 2 * LORA + ROPE_DIM
W_IN_COLS = _G0 + GQKV_W + GZ_W + 2 * HEADS


def _widen_w_in_t(w_t):
    z = jnp.zeros((ROPE_HALF, w_t.shape[1]), w_t.dtype)
    pad = jnp.zeros((GAB_W - 2 * HEADS, w_t.shape[1]), w_t.dtype)
    return jnp.concatenate([w_t[:2 * LORA + ROPE_HALF], z, w_t[2 * LORA + ROPE_HALF:_G0], z, w_t[_G0:], pad], axis=0)


def _narrow_w_in_t(w_t):
    return jnp.concatenate([w_t[:2 * LORA + ROPE_HALF], w_t[2 * LORA + 2 * ROPE_HALF:2 * LORA + 3 * ROPE_HALF],
                            w_t[LAT_W:LAT_W + W_IN_COLS - _G0]], axis=0)


def _stack_mla(w_uq, w_ukv):
    uq = w_uq.reshape(LORA, HEADS, QK_DIM)
    ukv = w_ukv.reshape(LORA, HEADS, 2 * HEAD_DIM)
    parts = [uq[:, :, :HEAD_DIM], _rope_pad(uq[:, :, HEAD_DIM:]), ukv[:, :, :HEAD_DIM], ukv[:, :, HEAD_DIM:]]
    return jnp.concatenate([p.transpose(1, 0, 2) for p in parts], axis=0)


def _unstack_mla(w):
    p = [w[i * HEADS:(i + 1) * HEADS].transpose(1, 0, 2) for i in range(4)]
    uq = jnp.concatenate([p[0], _rope_unpad(p[1])], axis=-1).reshape(LORA, HEADS * QK_DIM)
    ukv = jnp.concatenate([p[2], p[3]], axis=-1).reshape(LORA, HEADS * 2 * HEAD_DIM)
    return uq, ukv


def _rows8(rows):
    a = jnp.concatenate(rows, axis=0)
    return jnp.pad(a, ((0, 8 - a.shape[0]), (0, 0)))


def _qk_norm_rows(q_norm_w, k_norm_w):
    return _rows8([q_norm_w[:, :HEAD_DIM], _rope_pad(q_norm_w[:, HEAD_DIM:]), k_norm_w[:, :HEAD_DIM], _rope_pad(k_norm_w[:, HEAD_DIM:])])


def _rope_rows():
    inv_freq = ROPE_THETA ** (-jnp.arange(ROPE_HALF, dtype=F32) / ROPE_HALF)
    z = jnp.zeros((ROPE_HALF,), F32)
    freq = jnp.concatenate([inv_freq, z, inv_freq, z])
    sign = jnp.concatenate([-jnp.ones((ROPE_HALF,), F32), z, jnp.ones((ROPE_HALF,), F32), z])
    return _rows8([freq[None], sign[None]])


def _column_shards(a):
    return a.reshape(a.shape[0], 4, a.shape[1] // 4).transpose(1, 0, 2)


def _from_column_shards(a):
    return a.transpose(1, 0, 2).reshape(a.shape[1], 4 * a.shape[2])


_ANY = pl.BlockSpec(memory_space=pl.ANY)
_OTHER_CHIPS = ((1, 0), (0, 1), (1, 1))


def _here():
    return lax.axis_index("x"), lax.axis_index("y"), lax.axis_index("c")


def _flip(v, bit):
    return 1 - v if bit else v


def _remote(src, dst, send_sems, recv_sems, k, to):
    return pltpu.make_async_remote_copy(src_ref=src, dst_ref=dst, send_sem=send_sems.at[k], recv_sem=recv_sems.at[k],
                                        device_id=to, device_id_type=MESH)


def _half_of(ref, k, shape):
    r, c = shape
    if (r // 2) % 16 == 0:
        return ref.at[pl.ds(pl.multiple_of(k * (r // 2), 16), r // 2)]
    if (c // 2) % 128 == 0:
        return ref.at[:, pl.ds(pl.multiple_of(k * (c // 2), 128), c // 2)]
    return None


def _gather_copies(srcs, dsts, send_sems, recv_sems, local_sems):
    x, y, c = _here()
    slot, sibling, n = 2 * x + y, (x, y, 1 - c), len(srcs)
    starts, passes, waits = [], [], []
    for i, (src, dst) in enumerate(zip(srcs, dsts)):
        own = pltpu.make_async_copy(src, dst.at[slot], local_sems.at[i])
        starts.append(own.start)
        waits.append(own.wait)
        halves = _half_of(src, c, src.shape) is not None
        for j, (fx, fy) in enumerate(_OTHER_CHIPS):
            cx, cy = _flip(x, fx), _flip(y, fy)
            there = dst.at[2 * cx + cy]
            if halves:
                push = _remote(_half_of(src, c, src.shape), _half_of(dst.at[slot], c, src.shape), send_sems, recv_sems, 3 * i + j, (cx, cy, c))
                landed, other = _half_of(there, c, src.shape), _half_of(there, 1 - c, src.shape)
                onward = _remote(landed, landed, send_sems, recv_sems, 3 * n + 3 * i + j, sibling)
                passes += [_remote(landed, landed, send_sems, recv_sems, 3 * i + j, (cx, cy, c)).wait_recv, onward.start]
                waits += [_remote(other, other, send_sems, recv_sems, 3 * n + 3 * i + j, sibling).wait_recv, onward.wait_send]
            else:
                push = _remote(src, dst.at[slot], send_sems, recv_sems, 3 * i + j, (cx, cy, c))
                waits.append(_remote(there, there, send_sems, recv_sems, 3 * i + j, (cx, cy, c)).wait_recv)
            starts.append(push.start)
            waits.append(push.wait_send)
    return starts, passes, waits


def _gather_scratch(n):
    return [pltpu.SemaphoreType.DMA((6 * n,)), pltpu.SemaphoreType.DMA((6 * n,)), pltpu.SemaphoreType.DMA((n,))]


def _all_gather(shards, name):
    ns = len(shards)

    def body(*refs):
        starts, passes, waits = _gather_copies(refs[:ns], refs[ns:2 * ns], *refs[2 * ns:])
        for call in starts + passes + waits:
            call()

    return pl.pallas_call(
        body, in_specs=[_ANY] * ns, out_specs=[_ANY] * ns, out_shape=[_sds((4,) + s.shape, s.dtype) for s in shards],
        scratch_shapes=_gather_scratch(ns), name=name,
    )(*shards)


def _by_lanes(shape):
    return (shape[-2] // 2) % 16 != 0


def _scattered_shape(p):
    r, c = p.shape[1:]
    return _sds((8, r, c // 2) if _by_lanes(p.shape) else (8, r // 2, c), p.dtype)


def _scatter_copies(srcs, dsts, send_sems, recv_sems, local_sems, whole=0):
    x, y, c = _here()
    me = 4 * x + 2 * y + c
    starts, waits = [], []
    for i, (src, dst) in enumerate(zip(srcs, dsts)):
        def piece(px, py, pc, src=src, entire=i >= len(srcs) - whole):
            if entire:
                return src
            if _by_lanes(src.shape):
                half = src.shape[2] // 2
                return src.at[2 * px + py, :, pl.ds(pl.multiple_of(pc * half, 128), half)]
            half = src.shape[1] // 2
            return src.at[2 * px + py, pl.ds(pl.multiple_of(pc * half, 16), half)]

        own = pltpu.make_async_copy(piece(x, y, c), dst.at[me], local_sems.at[i])
        starts.append(own.start)
        waits.append(own.wait)
        for k in range(1, 8):
            px, py, pc = _flip(x, k & 4), _flip(y, k & 2), _flip(c, k & 1)
            push = _remote(piece(px, py, pc), dst.at[me], send_sems, recv_sems, 7 * i + k - 1, (px, py, pc))
            landed = dst.at[4 * px + 2 * py + pc]
            starts.append(push.start)
            waits += [_remote(landed, landed, send_sems, recv_sems, 7 * i + k - 1, (px, py, pc)).wait_recv, push.wait_send]
    return starts, waits


def _scatter_scratch(n):
    return [pltpu.SemaphoreType.DMA((7 * n,)), pltpu.SemaphoreType.DMA((7 * n,)), pltpu.SemaphoreType.DMA((n,))]


def _swapped_shape(half):
    r, c = half.shape
    return _sds((r, 2 * c) if _by_lanes((r, 2 * c)) else (2, r, c), half.dtype)


def _swap_copies(srcs, dsts, send_sems, recv_sems, local_sems):
    x, y, c = _here()
    sibling = (x, y, 1 - c)
    starts, waits = [], []
    for i, (src, dst) in enumerate(zip(srcs, dsts)):
        if len(dst.shape) == 2:
            lanes = src.shape[1]
            mine, other = (dst.at[:, pl.ds(pl.multiple_of(k * lanes, 128), lanes)] for k in (c, 1 - c))
        else:
            mine, other = dst.at[c], dst.at[1 - c]
        own = pltpu.make_async_copy(src, mine, local_sems.at[i])
        push = _remote(src, mine, send_sems, recv_sems, i, sibling)
        starts += [own.start, push.start]
        waits += [_remote(other, other, send_sems, recv_sems, i, sibling).wait_recv, push.wait_send, own.wait]
    return starts, waits


def _swap_scratch(n):
    return [pltpu.SemaphoreType.DMA((n,)), pltpu.SemaphoreType.DMA((n,)), pltpu.SemaphoreType.DMA((n,))]


def _exchange_halves(halves, wholes):
    ns, nw = len(halves), len(wholes)

    def body(*refs):
        srcs, dsts = refs[:ns + nw], refs[ns + nw:2 * (ns + nw)]
        sems = refs[2 * (ns + nw):]
        starts, waits = _swap_copies(srcs[:ns], dsts[:ns], *sems[:3])
        more = _scatter_copies(srcs[ns:], dsts[ns:], *sems[3:], whole=nw)
        for call in starts + more[0] + waits + more[1]:
            call()

    return pl.pallas_call(
        body, in_specs=[_ANY] * (ns + nw), out_specs=[_ANY] * (ns + nw),
        out_shape=[_swapped_shape(h) for h in halves] + [_sds((8,) + a.shape, a.dtype) for a in wholes],
        scratch_shapes=_swap_scratch(ns) + _scatter_scratch(nw), name="exchange_halves",
    )(*halves, *wholes)


def _row_tile(rows, row_bytes, budget):
    tr = rows
    while tr * row_bytes > budget and tr % 16 == 0:
        tr //= 2
    return tr


def _sum_slots(parts, name):
    _, rows, cols = parts.shape
    tr = _row_tile(rows, 8 * cols * 4, 2 * 1024 * 1024)

    def body(p_ref, o_ref):
        acc = p_ref[0].astype(F32)
        for d in range(1, 8):
            acc = acc + p_ref[d].astype(F32)
        o_ref[...] = acc

    return pl.pallas_call(
        body, grid=(rows // tr,), in_specs=[pl.BlockSpec((8, tr, cols), lambda i: (0, i, 0))],
        out_specs=pl.BlockSpec((tr, cols), lambda i: (i, 0)), out_shape=_sds((rows, cols), F32),
        compiler_params=_params(("parallel",)), name=name,
    )(parts)


def _adam_update(w, g, m, v):
    m = ADAM_B1 * m + (1.0 - ADAM_B1) * g
    v = ADAM_B2 * v + (1.0 - ADAM_B2) * jnp.square(g)
    m_hat = m / (1.0 - ADAM_B1 ** ADAM_STEP)
    v_hat = v / (1.0 - ADAM_B2 ** ADAM_STEP)
    return -ADAM_LR * (m_hat / (jnp.sqrt(v_hat) + ADAM_EPS) + ADAM_WD * w), m, v


SMALL_ROWS = {"attn_norm_w": 0, "mlp_norm_w": 1, "q_lat_norm_w": 2, "kv_lat_norm_w": 3, "q_norm_w": 4, "k_norm_w": 5,
              "mla_out_norm_w": 6, "gdn_norm_w": 10, "a_log": 11, "dt_bias": 12}
LOSS_ROW = 13
SMALL_SHAPE = (16, 1024)


def _pack_small_partials(d_attn_nw, d_mlp_nw, d_ln, d_qk_nw, d_mix_nw, d_scal, conv_parts, sq):
    D = d_attn_nw.shape[1]

    def body(an_ref, mn_ref, ln_ref, qk_ref, mix_ref, sc_ref, cq_ref, ck_ref, cv_ref, sq_ref, a_ref, c_ref):
        a_ref[...] = jnp.zeros_like(a_ref)
        a_ref[0:1, :D] = an_ref[...]
        a_ref[1:2, :D] = mn_ref[...]
        a_ref[2:4, :LORA] = ln_ref[...]
        for row, base in ((4, 0), (5, 2)):
            rope = qk_ref[base + 1:base + 2, :]
            a_ref[row:row + 1, :QK_DIM] = jnp.concatenate(
                [qk_ref[base:base + 1, :], rope[:, :ROPE_HALF], rope[:, 2 * ROPE_HALF:3 * ROPE_HALF]], axis=1)
        a_ref[6:6 + HEADS, :HEAD_DIM] = mix_ref[0:HEADS, :]
        a_ref[10:11, :HEAD_DIM] = mix_ref[HEADS:HEADS + 1, :]
        a_ref[11:13, :128] = sc_ref[0:2, :]
        a_ref[LOSS_ROW:LOSS_ROW + 1, :128] = jnp.zeros((1, 128), F32) + jnp.sum(sq_ref[:, 0:1, 0:1]) * (0.5 / D)
        c_ref[...] = jnp.concatenate([cq_ref[...], ck_ref[...], cv_ref[...]], axis=1)

    return pl.pallas_call(
        body, out_shape=[_sds(SMALL_SHAPE, F32), _sds((CONV_TAPS, GQKV_W), F32)], name="pack_small_partials",
    )(d_attn_nw, d_mlp_nw, d_ln, d_qk_nw, d_mix_nw, d_scal, *conv_parts, sq)


def _adamw_small(parts, conv_parts, w, m, v):
    names = tuple(SMALL_ROWS) + ("conv_w",)
    cols = w["conv_w"].shape[2]

    def body(*refs):
        p_ref, c_ref = refs[:2]
        n = len(names)
        w_refs, m_refs, v_refs = (dict(zip(names, refs[2 + k * n:2 + (k + 1) * n])) for k in range(3))
        loss_ref = refs[2 + 3 * n]
        out = [dict(zip(names, refs[3 + (3 + k) * n:3 + (4 + k) * n])) for k in range(4)]
        acc_ref, cacc_ref = refs[3 + 7 * n:]
        acc, cacc = p_ref[0], c_ref[0]
        for d in range(1, 8):
            acc, cacc = acc + p_ref[d], cacc + c_ref[d]
        acc_ref[...] = acc
        cacc_ref[...] = cacc
        loss_ref[...] = acc_ref[LOSS_ROW:LOSS_ROW + 1, 0:1]
        chip = 2 * lax.axis_index("x") + lax.axis_index("y")
        for name in names:
            shape = w_refs[name].shape
            if name == "conv_w":
                g = sum(jnp.where(chip == s, cacc_ref[:, s * cols:(s + 1) * cols], 0.0) for s in range(4))[None]
            else:
                row = SMALL_ROWS[name]
                g = acc_ref[row:row + math.prod(shape[:-1]), 0:shape[-1]].reshape(shape)
            delta, new_m, new_v = _adam_update(w_refs[name][...], g, m_refs[name][...], v_refs[name][...])
            for ref, val in zip((o[name] for o in out), (g, delta, new_m, new_v)):
                ref[...] = val

    ins = [x[n] for x in (w, m, v) for n in names]
    shapes = [_sds(w[n].shape, F32) for n in names]
    outs = pl.pallas_call(
        body, out_shape=[_sds((1, 1), F32)] + shapes * 4,
        scratch_shapes=[pltpu.VMEM(parts.shape[1:], F32), pltpu.VMEM(conv_parts.shape[1:], F32)], name="adamw_small",
    )(parts, conv_parts, *ins)
    n = len(names)
    return (outs[0],) + tuple(dict(zip(names, outs[1 + k * n:1 + (k + 1) * n])) for k in range(4))


def _adamw(w, g, m, v, name):
    rows, cols = w.shape[0], w.shape[-1]
    if w.ndim == 3:
        tr = max(d for d in range(1, rows + 1) if rows % d == 0 and d * 8 * cols * 4 * 14 <= VMEM_LIMIT // 2)
    else:
        tr = _row_tile(rows, 7 * cols * 4, 4 * 1024 * 1024)

    def body(w_ref, g_ref, m_ref, v_ref, d_ref, mo_ref, vo_ref):
        d_ref[...], mo_ref[...], vo_ref[...] = _adam_update(w_ref[...], g_ref[...], m_ref[...], v_ref[...])

    block = (tr,) + w.shape[1:]
    spec = pl.BlockSpec(block, lambda i: (i,) + (0,) * (len(block) - 1))
    return pl.pallas_call(
        body, grid=(rows // tr,), in_specs=[spec] * 4, out_specs=[spec] * 3, out_shape=[_sds(w.shape, F32)] * 3,
        compiler_params=_params(("parallel",)), name=name,
    )(w, g, m, v)


def kernel(x, positions, attn_norm_w, w_in, q_lat_norm_w, w_uq, kv_lat_norm_w, w_ukv, q_norm_w, k_norm_w, mla_out_norm_w, conv_w, a_log, dt_bias, gdn_norm_w, w_out, mlp_norm_w, w_up, w_down, loss_target, m_attn_norm_w, m_w_in, m_q_lat_norm_w, m_w_uq, m_kv_lat_norm_w, m_w_ukv, m_q_norm_w, m_k_norm_w, m_mla_out_norm_w, m_conv_w, m_a_log, m_dt_bias, m_gdn_norm_w, m_w_out, m_mlp_norm_w, m_w_up, m_w_down, v_attn_norm_w, v_w_in, v_q_lat_norm_w, v_w_uq, v_kv_lat_norm_w, v_w_ukv, v_q_norm_w, v_k_norm_w, v_mla_out_norm_w, v_conv_w, v_a_log, v_dt_bias, v_gdn_norm_w, v_w_out, v_mlp_norm_w, v_w_up, v_w_down):
    w = dict(zip(WEIGHTS, (attn_norm_w, w_in, q_lat_norm_w, w_uq, kv_lat_norm_w, w_ukv, q_norm_w, k_norm_w, mla_out_norm_w, conv_w,
                           a_log, dt_bias, gdn_norm_w, w_out, mlp_norm_w, w_up, w_down)))
    m = dict(zip(WEIGHTS, (m_attn_norm_w, m_w_in, m_q_lat_norm_w, m_w_uq, m_kv_lat_norm_w, m_w_ukv, m_q_norm_w, m_k_norm_w,
                           m_mla_out_norm_w, m_conv_w, m_a_log, m_dt_bias, m_gdn_norm_w, m_w_out, m_mlp_norm_w, m_w_up, m_w_down)))
    v = dict(zip(WEIGHTS, (v_attn_norm_w, v_w_in, v_q_lat_norm_w, v_w_uq, v_kv_lat_norm_w, v_w_ukv, v_q_norm_w, v_k_norm_w,
                           v_mla_out_norm_w, v_conv_w, v_a_log, v_dt_bias, v_gdn_norm_w, v_w_out, v_mlp_norm_w, v_w_up, v_w_down)))
    B, S, D = x.shape
    T = B * S
    x2, pos, target = x.reshape(T, D), positions.reshape(T, 1), loss_target.reshape(T, D)
    seq = lambda a: a.reshape(B, S, a.shape[-1])
    tok = lambda a: a.reshape(T, a.shape[-1])
    local = {n: w[n][0] for n in SHARDED}

    g_in, g_uq, g_ukv, g_conv = _all_gather([jnp.swapaxes(w_in, 1, 2)[0].astype(BF16), local["w_uq"].astype(BF16),
                                             local["w_ukv"].astype(BF16), local["conv_w"]], "gather_first_weights")
    w_in_p = _widen_w_in_t(g_in.reshape(-1, D))
    w_mla = _stack_mla(_from_column_shards(g_uq), _from_column_shards(g_ukv))
    conv_full = _from_column_shards(g_conv)
    ln_w = jnp.concatenate([q_lat_norm_w, kv_lat_norm_w], axis=0)
    qk_nw = _qk_norm_rows(q_norm_w, k_norm_w)
    rope_rows = _rope_rows()
    scal = _rows8([jnp.pad(a_log, ((0, 0), (0, 128 - HEADS))), jnp.pad(dt_bias, ((0, 0), (0, 128 - HEADS)))])
    mix_nw = _rows8([mla_out_norm_w[0], gdn_norm_w])

    xn, lat, gqkv, gz, gab = _in_proj_fwd(x2, attn_norm_w, w_in_p)
    q, k, v_att = _mla_pre_fwd(lat, pos, ln_w, w_mla, qk_nw, rope_rows)
    ao, lse, g_down = _attn_fwd(seq(q), seq(k), seq(v_att), [local["w_down"].astype(BF16)])
    gq, gk, gv = _gdn_pre_fwd(seq(gqkv), conv_full)
    go, states, g_out, w_up_b = _gdn_chunk_fwd(gq, gk, gv, seq(gab), scal, [local["w_out"].astype(BF16), local["w_up"].astype(BF16)])
    w_out_b = g_out.reshape(-1, D)
    w_down_b = g_down.reshape(-1, D)
    mix, h2 = _mix_fwd(tok(ao), tok(go), gz, mix_nw, w_out_b, x2)
    hn, act, dy, sq = _mlp_fwd(h2, mlp_norm_w, w_up_b, w_down_b, target)

    dh, d_up, d_mlp_nw = _mlp_bwd(h2, mlp_norm_w, act, w_up_b, w_down_b, dy)
    p_down = _wgrad(act, dy, "wgrad_down").reshape(4, -1, D)
    p_up = _wgrad(hn, d_up, "wgrad_up", column_shards=4)
    d_ao, d_go, d_gz, d_mix_nw = _mix_bwd(tok(ao), tok(go), gz, mix_nw, w_out_b, dh)
    p_out = _wgrad(mix, dh, "wgrad_out").reshape(4, -1, D)
    d_gq, d_gk, d_gv, d_gab, d_scal, s_up, s_out = _gdn_chunk_bwd(gq, gk, gv, seq(gab), scal, states, seq(d_go), [p_up, p_out])
    early = ("w_up", "w_out", "w_down")
    dxq, dxk, dxv, dcq, dck, dcv, g_up, g_out = _gdn_pre_bwd(seq(gqkv), conv_full, d_gq, d_gk, d_gv,
                                                             [_sum_slots(s_up, "sum_w_up"), _sum_slots(s_out, "sum_w_out")])
    dq, dk, dv, s_down = _attn_bwd(seq(q), seq(k), seq(v_att), ao, lse, seq(d_ao), [p_down])
    d_lat, d_ln, d_w_mla, d_qk_nw, g_down = _mla_pre_bwd(lat, pos, ln_w, w_mla, qk_nw, rope_rows, tok(dq), tok(dk), tok(dv),
                                                         [_sum_slots(s_down, "sum_w_down")])
    early_grads = [g_up, g_out, g_down]
    d_pieces = [d_lat, tok(dxq), tok(dxk), tok(dxv), d_gz, tok(d_gab)]
    p_in = _narrow_w_in_t(_wgrad_pieces(d_pieces, xn, "wgrad_in")).reshape(4, -1, D)
    p_uq, p_ukv = (_column_shards(a).astype(BF16) for a in _unstack_mla(d_w_mla))
    grad_x2, d_attn_nw, s_in, s_uq, s_ukv = _in_proj_bwd(d_pieces, w_in_p, x2, attn_norm_w, dh, [p_in, p_uq, p_ukv])
    small_buf, conv_buf = _pack_small_partials(d_attn_nw, d_mlp_nw, d_ln, d_qk_nw, d_mix_nw, d_scal, (dcq, dck, dcv), sq)

    late = ("w_in", "w_uq", "w_ukv")
    *late_grads, s_small, s_conv = _exchange_halves([_sum_slots(s, "sum_" + n) for n, s in zip(late, (s_in, s_uq, s_ukv))],
                                                    [small_buf, conv_buf])
    names = early + late
    grad = {n: g.reshape(-1, g.shape[-1]) for n, g in zip(names, list(early_grads) + list(late_grads))}

    loss, g_small, delta, new_m, new_v = _adamw_small(s_small, s_conv, w, m, v)
    grad.update(g_small)
    for n in names:
        if n == "w_in":
            stored = lambda a: jnp.transpose(a, (2, 0, 1))
            outs = _adamw(stored(w[n]), grad[n][:, None, :], stored(m[n]), stored(v[n]), "adamw_" + n)
            grad[n], delta[n], new_m[n], new_v[n] = (jnp.transpose(a, (1, 2, 0)) for a in (grad[n][:, None, :], *outs))
        else:
            delta[n], new_m[n], new_v[n] = _adamw(local[n], grad[n], m[n][0], v[n][0], "adamw_" + n)
    def in_order(d):
        return [d[n].reshape(w[n].shape) for n in WEIGHTS]

    return (loss.reshape(()), grad_x2.reshape(B, S, D), *in_order(grad), *in_order(delta), *in_order(new_m), *in_order(new_v))
```

```python
import functools
import math

import jax
import jax.numpy as jnp
from jax import lax
from jax.experimental import pallas as pl
from jax.experimental.pallas import tpu as pltpu

F32 = jnp.float32
BF16 = jnp.bfloat16
MESH = pl.DeviceIdType.MESH

EPS = 1e-6
HEADS = 4
HEAD_DIM = 128
ROPE_DIM = 64
ROPE_HALF = 32
QK_DIM = 192
QK_PAD = 256
LORA = 256
CHUNK = 64
CONV_TAPS = 4
ROPE_THETA = 10000.0
ATTN_SCALE = QK_DIM ** -0.5

LAT_W = 640
GQKV_W = 3 * HEADS * HEAD_DIM
GZ_W = HEADS * HEAD_DIM
GAB_W = 128
PROJ_SPLITS = ((0, LAT_W), (LAT_W, LAT_W + GQKV_W), (LAT_W + GQKV_W, LAT_W + GQKV_W + GZ_W),
               (LAT_W + GQKV_W + GZ_W, LAT_W + GQKV_W + GZ_W + GAB_W))
PROJ_W = PROJ_SPLITS[-1][1]

ADAM_LR = 0.001
ADAM_B1 = 0.9
ADAM_B2 = 0.999
ADAM_EPS = 1e-08
ADAM_WD = 0.01
ADAM_STEP = 10

TOKEN_TILE = 512
MLP_TOKEN_TILE = 512
FF_TILE = 1024
ATTN_TILE = 512
ATTN_HEADS_PER_STEP = 2
WGRAD_OUT_BYTES = 8 * 1024 * 1024
VMEM_LIMIT = 48 * 1024 * 1024

SHARDED = ("w_in", "w_uq", "w_ukv", "conv_w", "w_out", "w_up", "w_down")
WEIGHTS = ("attn_norm_w", "w_in", "q_lat_norm_w", "w_uq", "kv_lat_norm_w", "w_ukv", "q_norm_w", "k_norm_w", "mla_out_norm_w",
           "conv_w", "a_log", "dt_bias", "gdn_norm_w", "w_out", "mlp_norm_w", "w_up", "w_down")


def _sds(shape, dtype):
    return jax.ShapeDtypeStruct(shape, dtype)


def _params(semantics):
    return pltpu.CompilerParams(dimension_semantics=semantics, vmem_limit_bytes=VMEM_LIMIT)


def _block(n):
    for b in (512, 256, 128):
        if n % b == 0:
            return b
    return n


def _dg(a, b, ca, cb, prec):
    lead = a.ndim - 2
    batch = (tuple(range(lead)),) * 2
    return lax.dot_general(a, b, (((ca + lead,), (cb + lead,)), batch), precision=prec, preferred_element_type=F32)


def _split_bf16(a):
    hi = a.astype(BF16)
    return hi, (a - hi.astype(F32)).astype(BF16)


def _dot_bf16(a, b, ca, cb):
    return _dg(a.astype(BF16), b.astype(BF16), ca, cb, None)


def _dot_bf16x3(a, b, ca, cb):
    a_hi, a_lo = _split_bf16(a)
    b_hi, b_lo = _split_bf16(b)
    lead = a.ndim - 2
    return _dg(jnp.concatenate([a_hi, a_hi, a_lo], axis=ca + lead), jnp.concatenate([b_hi, b_lo, b_hi], axis=cb + lead), ca, cb, None)


def _matmul_family(dot):
    def nn_raw(a, b):
        return dot(a, b, 1, 0)

    def nt_raw(a, b):
        return dot(a, b, 1, 1)

    def tn_raw(a, b):
        return dot(a, b, 0, 0)

    @jax.custom_vjp
    def nn(a, b):
        return nn_raw(a, b)

    nn.defvjp(lambda a, b: (nn_raw(a, b), (a, b)), lambda r, g: (nt_raw(g, r[1]), tn_raw(r[0], g)))

    @jax.custom_vjp
    def nt(a, b):
        return nt_raw(a, b)

    nt.defvjp(lambda a, b: (nt_raw(a, b), (a, b)), lambda r, g: (nn_raw(g, r[1]), tn_raw(g, r[0])))

    @jax.custom_vjp
    def tn(a, b):
        return tn_raw(a, b)

    tn.defvjp(lambda a, b: (tn_raw(a, b), (a, b)), lambda r, g: (nt_raw(r[1], g), nn_raw(r[0], g)))
    return nn, nt, tn


_bf_nn, _bf_nt, _bf_tn = _matmul_family(_dot_bf16)
_hi_nn, _hi_nt, _hi_tn = _matmul_family(_dot_bf16x3)


def _lower_powers(lmat):
    powers = []
    while 2 ** (len(powers) + 1) < lmat.shape[-1]:
        powers.append(_dot_bf16x3(powers[-1] if powers else lmat, powers[-1] if powers else lmat, 1, 0))
    return powers


@jax.custom_vjp
def _unit_lower_solve(lmat, rhs):
    return _unit_lower_solve_fwd(lmat, rhs)[0]


def _unit_lower_solve_fwd(lmat, rhs):
    powers = _lower_powers(lmat)
    x = rhs - _dot_bf16x3(lmat, rhs, 1, 0)
    for p in powers:
        x = x + _dot_bf16x3(p, x, 1, 0)
    return x, (lmat, powers, x)


def _unit_lower_solve_bwd(res, g):
    lmat, powers, x = res
    y = g - _dot_bf16x3(lmat, g, 0, 0)
    for p in powers:
        y = y + _dot_bf16x3(p, y, 0, 0)
    return -_dot_bf16x3(y, x, 1, 1), y


_unit_lower_solve.defvjp(_unit_lower_solve_fwd, _unit_lower_solve_bwd)


@jax.custom_vjp
def _lane_halves(x):
    n = x.shape[-1] // 2
    return x[..., :n], x[..., n:]


_lane_halves.defvjp(lambda x: (_lane_halves(x), None), lambda _, g: (jnp.concatenate(g, axis=-1),))


@jax.custom_vjp
def _row_halves(x):
    n = x.shape[-2] // 2
    return x[..., :n, :], x[..., n:, :]


_row_halves.defvjp(lambda x: (_row_halves(x), None), lambda _, g: (jnp.concatenate(g, axis=-2),))


@jax.custom_vjp
def _swap_halves(t):
    return pltpu.roll(t, 64, 1)


_swap_halves.defvjp(lambda t: (pltpu.roll(t, 64, 1), None), lambda _, g: (pltpu.roll(g, 64, 1),))


@functools.partial(jax.custom_vjp, nondiff_argnums=(2,))
def _shift_rows(x, keep, s):
    return pltpu.roll(x, s, 0) * keep


def _shift_rows_fwd(x, keep, s):
    return pltpu.roll(x, s, 0) * keep, keep


def _shift_rows_bwd(s, keep, g):
    return pltpu.roll(g * keep, keep.shape[0] - s, 0), jnp.zeros_like(keep)


_shift_rows.defvjp(_shift_rows_fwd, _shift_rows_bwd)


def _sigmoid(x):
    return 0.5 * jnp.tanh(0.5 * x) + 0.5


def _softplus(x):
    return jnp.maximum(x, 0.0) + jnp.log(1.0 + jnp.exp(jnp.minimum(x, -x)))


def _silu(x):
    return x * _sigmoid(x)


def _rms(x, w, n=None):
    n = x.shape[-1] if n is None else n
    r = lax.rsqrt(jnp.sum(x * x, axis=-1, keepdims=True) * (1.0 / n) + EPS)
    return x * r * w


def _rope(t, cos_f, sin_f):
    return t * cos_f + _swap_halves(t) * sin_f


def _rope_tables(pos_col, freq_row, sign_row):
    ang = pos_col.astype(F32) * freq_row
    return jnp.cos(ang), jnp.sin(ang) * sign_row


def _onehot_row(lane):
    return (lax.broadcasted_iota(jnp.int32, (1, 128), 1) == lane).astype(F32)


def _row_spec(tm, w):
    return pl.BlockSpec((tm, w), lambda i: (i, 0))


def _const_spec(shape):
    return pl.BlockSpec(shape, lambda *_: (0,) * len(shape))


def _in_proj_fwd(x2, w_an, w_in_p):
    T, D = x2.shape
    tm = min(TOKEN_TILE, T)

    def body(x_ref, wn_ref, w_ref, xn_ref, lat_ref, gqkv_ref, gz_ref, gab_ref):
        x = x_ref[...]
        r = lax.rsqrt(jnp.mean(x * x, axis=-1, keepdims=True) + EPS)
        xn = (x * r * wn_ref[...]).astype(BF16)
        xn_ref[...] = xn
        for ref, (a, b) in zip((lat_ref, gqkv_ref, gz_ref, gab_ref), PROJ_SPLITS):
            ref[...] = _dg(xn, w_ref[a:b, :], 1, 1, None)

    widths = [b - a for a, b in PROJ_SPLITS]
    return pl.pallas_call(
        body, grid=(T // tm,),
        in_specs=[_row_spec(tm, D), _const_spec((1, D)), _const_spec((PROJ_W, D))],
        out_specs=[_row_spec(tm, D)] + [_row_spec(tm, w) for w in widths],
        out_shape=[_sds((T, D), BF16)] + [_sds((T, w), F32) for w in widths],
        compiler_params=_params(("parallel",)), name="in_proj_fwd",
    )(x2, w_an, w_in_p)


def _in_proj_bwd(pieces, w_in_p, x2, w_an, dh, partials):
    T, D = x2.shape
    tm = min(TOKEN_TILE, T)
    widths = [p.shape[1] for p in pieces]
    starts = [sum(widths[:i]) for i in range(len(widths))]
    assert sum(widths) == PROJ_W
    npc, ns = len(pieces), len(partials)

    def body(*refs):
        piece_refs = refs[:npc]
        w_ref, x_ref, wn_ref, dh_ref = refs[npc:npc + 4]
        src_refs = refs[npc + 4:npc + 4 + ns]
        dx_ref, dwn_ref = refs[npc + 4 + ns:npc + 6 + ns]
        dst_refs = refs[npc + 6 + ns:npc + 6 + 2 * ns]
        sems = refs[npc + 6 + 2 * ns:]

        @pl.when(pl.program_id(0) == 0)
        def _():
            for start in _scatter_copies(src_refs, dst_refs, *sems)[0]:
                start()
            dwn_ref[...] = jnp.zeros_like(dwn_ref)

        dxn = jnp.zeros((tm, D), F32)
        for ref, a, width in zip(piece_refs, starts, widths):
            dxn += _dg(ref[...], w_ref[a:a + width, :], 1, 0, None)
        _, pull = jax.vjp(_rms, x_ref[...], wn_ref[...])
        dx, dwn = pull(dxn)
        dx_ref[...] = dx + dh_ref[...]
        dwn_ref[...] += dwn

        @pl.when(pl.program_id(0) == T // tm - 1)
        def _():
            for wait in _scatter_copies(src_refs, dst_refs, *sems)[1]:
                wait()

    return pl.pallas_call(
        body, grid=(T // tm,),
        in_specs=[_row_spec(tm, w) for w in widths] + [_const_spec((PROJ_W, D)), _row_spec(tm, D), _const_spec((1, D)),
                                                       _row_spec(tm, D)] + [_ANY] * ns,
        out_specs=[_row_spec(tm, D), _const_spec((1, D))] + [_ANY] * ns,
        out_shape=[_sds((T, D), F32), _sds((1, D), F32)] + [_scattered_shape(p) for p in partials],
        scratch_shapes=_scatter_scratch(ns),
        compiler_params=_params(("arbitrary",)), name="in_proj_bwd",
    )(*pieces, w_in_p, x2, w_an, dh, *partials)


def _wgrad_pieces(pieces, b, name):
    T, k2 = b.shape
    tt = min(TOKEN_TILE, T)
    widths = [p.shape[1] for p in pieces]
    starts = [sum(widths[:i]) for i in range(len(widths))]
    k1 = sum(widths)

    def body(*refs):
        piece_refs, (b_ref, o_ref, acc_ref) = refs[:len(pieces)], refs[len(pieces):]
        t = pl.program_id(0)

        @pl.when(t == 0)
        def _():
            acc_ref[...] = jnp.zeros_like(acc_ref)

        bt = b_ref[...].astype(BF16)
        for ref, r0, width in zip(piece_refs, starts, widths):
            acc_ref[r0:r0 + width, :] += jnp.dot(ref[...].T, bt, preferred_element_type=F32)

        @pl.when(t == T // tt - 1)
        def _():
            o_ref[...] = acc_ref[...].astype(o_ref.dtype)

    return pl.pallas_call(
        body, grid=(T // tt,),
        in_specs=[pl.BlockSpec((tt, w), lambda t: (t, 0)) for w in widths] + [pl.BlockSpec((tt, k2), lambda t: (t, 0))],
        out_specs=_const_spec((k1, k2)), out_shape=_sds((k1, k2), BF16), scratch_shapes=[pltpu.VMEM((k1, k2), F32)],
        compiler_params=_params(("arbitrary",)), name=name,
    )(*pieces, b)


def _wgrad(a, b, name, column_shards=1, out_dtype=BF16):
    T, k1 = a.shape
    k2 = b.shape[1]
    per_shard = k2 // column_shards
    tt = min(TOKEN_TILE, T)
    b1 = k1
    while b1 * k2 * 4 > WGRAD_OUT_BYTES and b1 % 256 == 0:
        b1 //= 2
    step = _block(per_shard)

    def body(a_ref, b_ref, o_ref, acc_ref):
        t = pl.program_id(1)

        @pl.when(t == 0)
        def _():
            acc_ref[...] = jnp.zeros_like(acc_ref)

        a_t = a_ref[...].astype(BF16).T
        for c0 in range(0, k2, step):
            part = jnp.dot(a_t, b_ref[:, c0:c0 + step].astype(BF16), preferred_element_type=F32)
            if column_shards == 1:
                acc_ref[:, c0:c0 + step] += part
            else:
                acc_ref[c0 // per_shard, :, c0 % per_shard:c0 % per_shard + step] += part

        @pl.when(t == T // tt - 1)
        def _():
            o_ref[...] = acc_ref[...].astype(o_ref.dtype)

    if column_shards == 1:
        block, out_spec, out_shape = (b1, k2), pl.BlockSpec((b1, k2), lambda i, t: (i, 0)), _sds((k1, k2), out_dtype)
    else:
        block = (column_shards, b1, per_shard)
        out_spec, out_shape = pl.BlockSpec(block, lambda i, t: (0, i, 0)), _sds((column_shards, k1, per_shard), out_dtype)
    return pl.pallas_call(
        body, grid=(k1 // b1, T // tt),
        in_specs=[pl.BlockSpec((tt, b1), lambda i, t: (t, i)), pl.BlockSpec((tt, k2), lambda i, t: (t, 0))],
        out_specs=out_spec, out_shape=out_shape, scratch_shapes=[pltpu.VMEM(block, F32)],
        compiler_params=_params(("parallel", "arbitrary")), name=name,
    )(a, b)


def _mla_pre_fn(q_lat, kv_lat, kpe, ln_q, ln_kv, w_list, qn_n, qn_p, kn_n, kn_p, cos_f, sin_f):
    qn = _rms(q_lat, ln_q)
    kvn = _rms(kv_lat, ln_kv)
    kp = _rope(_rms(kpe, kn_p, ROPE_DIM), cos_f, sin_f)
    outs = []
    for h in range(HEADS):
        outs.append(_rms(_bf_nn(qn, w_list[h]), qn_n))
        outs.append(_rope(_rms(_bf_nn(qn, w_list[HEADS + h]), qn_p, ROPE_DIM), cos_f, sin_f))
        outs.append(_rms(_bf_nn(kvn, w_list[2 * HEADS + h]), kn_n))
        outs.append(_bf_nn(kvn, w_list[3 * HEADS + h]))
    return tuple(outs) + (kp,)


def _mla_pre_operands(lat_ref, pos_ref, ln_ref, w_ref, nw_ref, rope_ref):
    cos_f, sin_f = _rope_tables(pos_ref[...], rope_ref[0:1, :], rope_ref[1:2, :])
    diff = (lat_ref[:, 0:LORA], lat_ref[:, LORA:2 * LORA], lat_ref[:, 2 * LORA:LAT_W], ln_ref[0:1, :], ln_ref[1:2, :],
            [w_ref[i].astype(F32) for i in range(4 * HEADS)], nw_ref[0:1, :], nw_ref[1:2, :], nw_ref[2:3, :], nw_ref[3:4, :])
    return diff, cos_f, sin_f


def _mla_pre_fwd(lat, pos, ln_w, w_mla, nw, rope_rows):
    T = lat.shape[0]
    tm = min(TOKEN_TILE, T)

    def body(lat_ref, pos_ref, ln_ref, w_ref, nw_ref, rope_ref, q_ref, k_ref, v_ref):
        diff, cos_f, sin_f = _mla_pre_operands(lat_ref, pos_ref, ln_ref, w_ref, nw_ref, rope_ref)
        outs = _mla_pre_fn(*diff, cos_f, sin_f)
        kp = outs[-1].astype(BF16)
        for h in range(HEADS):
            q_n, q_p, k_n, v = outs[4 * h:4 * h + 4]
            q_ref[:, h * QK_PAD:h * QK_PAD + HEAD_DIM] = q_n.astype(BF16)
            q_ref[:, h * QK_PAD + HEAD_DIM:(h + 1) * QK_PAD] = q_p.astype(BF16)
            k_ref[:, h * QK_PAD:h * QK_PAD + HEAD_DIM] = k_n.astype(BF16)
            k_ref[:, h * QK_PAD + HEAD_DIM:(h + 1) * QK_PAD] = kp
            v_ref[:, h * HEAD_DIM:(h + 1) * HEAD_DIM] = v.astype(BF16)

    return pl.pallas_call(
        body, grid=(T // tm,),
        in_specs=[_row_spec(tm, LAT_W), _row_spec(tm, 1), _const_spec((2, LORA)), _const_spec((4 * HEADS, LORA, 128)),
                  _const_spec((8, 128)), _const_spec((8, 128))],
        out_specs=[_row_spec(tm, HEADS * QK_PAD), _row_spec(tm, HEADS * QK_PAD), _row_spec(tm, HEADS * HEAD_DIM)],
        out_shape=[_sds((T, HEADS * QK_PAD), BF16), _sds((T, HEADS * QK_PAD), BF16), _sds((T, HEADS * HEAD_DIM), BF16)],
        compiler_params=_params(("parallel",)), name="mla_pre_fwd",
    )(lat, pos, ln_w, w_mla, nw, rope_rows)


def _mla_pre_bwd(lat, pos, ln_w, w_mla, nw, rope_rows, dq, dk, dv, halves):
    T = lat.shape[0]
    tm = min(TOKEN_TILE, T)
    ns = len(halves)

    def body(*refs):
        lat_ref, pos_ref, ln_ref, w_ref, nw_ref, rope_ref, dq_ref, dk_ref, dv_ref = refs[:9]
        src_refs = refs[9:9 + ns]
        dlat_ref, dln_ref, dw_ref, dnw_ref = refs[9 + ns:13 + ns]
        dst_refs = refs[13 + ns:13 + 2 * ns]
        sems = refs[13 + 2 * ns:]

        @pl.when(pl.program_id(0) == 0)
        def _():
            for start in _swap_copies(src_refs, dst_refs, *sems)[0]:
                start()
            dln_ref[...] = jnp.zeros_like(dln_ref)
            dw_ref[...] = jnp.zeros_like(dw_ref)
            dnw_ref[...] = jnp.zeros_like(dnw_ref)

        diff, cos_f, sin_f = _mla_pre_operands(lat_ref, pos_ref, ln_ref, w_ref, nw_ref, rope_ref)
        _, pull = jax.vjp(lambda *a: _mla_pre_fn(*a, cos_f, sin_f), *diff)
        cts = []
        d_kp = jnp.zeros((tm, 128), F32)
        for h in range(HEADS):
            cts.append(dq_ref[:, h * QK_PAD:h * QK_PAD + HEAD_DIM])
            cts.append(dq_ref[:, h * QK_PAD + HEAD_DIM:(h + 1) * QK_PAD])
            cts.append(dk_ref[:, h * QK_PAD:h * QK_PAD + HEAD_DIM])
            cts.append(dv_ref[:, h * HEAD_DIM:(h + 1) * HEAD_DIM])
            d_kp += dk_ref[:, h * QK_PAD + HEAD_DIM:(h + 1) * QK_PAD]
        d_ql, d_kvl, d_kpe, d_lnq, d_lnkv, d_w, d_qn_n, d_qn_p, d_kn_n, d_kn_p = pull(tuple(cts) + (d_kp,))
        dlat_ref[:, 0:LORA] = d_ql.astype(BF16)
        dlat_ref[:, LORA:2 * LORA] = d_kvl.astype(BF16)
        dlat_ref[:, 2 * LORA:LAT_W] = d_kpe.astype(BF16)
        dln_ref[0:1, :] += d_lnq
        dln_ref[1:2, :] += d_lnkv
        for i in range(4 * HEADS):
            dw_ref[i] += d_w[i]
        for i, d in enumerate((d_qn_n, d_qn_p, d_kn_n, d_kn_p)):
            dnw_ref[i:i + 1, :] += d

        @pl.when(pl.program_id(0) == T // tm - 1)
        def _():
            for wait in _swap_copies(src_refs, dst_refs, *sems)[1]:
                wait()

    return pl.pallas_call(
        body, grid=(T // tm,),
        in_specs=[_row_spec(tm, LAT_W), _row_spec(tm, 1), _const_spec((2, LORA)), _const_spec((4 * HEADS, LORA, 128)),
                  _const_spec((8, 128)), _const_spec((8, 128)),
                  _row_spec(tm, HEADS * QK_PAD), _row_spec(tm, HEADS * QK_PAD), _row_spec(tm, HEADS * HEAD_DIM)] + [_ANY] * ns,
        out_specs=[_row_spec(tm, LAT_W), _const_spec((2, LORA)), _const_spec((4 * HEADS, LORA, 128)), _const_spec((8, 128))]
                  + [_ANY] * ns,
        out_shape=[_sds((T, LAT_W), BF16), _sds((2, LORA), F32), _sds((4 * HEADS, LORA, 128), F32), _sds((8, 128), F32)]
                  + [_swapped_shape(h) for h in halves],
        scratch_shapes=_swap_scratch(ns),
        compiler_params=_params(("arbitrary",)), name="mla_pre_bwd",
    )(lat, pos, ln_w, w_mla, nw, rope_rows, dq, dk, dv, *halves)


def _causal_mask(i, j, tq, tk):
    row = i * tq + lax.broadcasted_iota(jnp.int32, (tq, tk), 0)
    col = j * tk + lax.broadcasted_iota(jnp.int32, (tq, tk), 1)
    return col <= row


def _attn_fwd(q, k, v, shards):
    B, S, _ = q.shape
    t = min(ATTN_TILE, S)
    nq = S // t
    ns = len(shards)

    hp = ATTN_HEADS_PER_STEP
    qk = lambda h: slice(h * QK_PAD, (h + 1) * QK_PAD)
    vd = lambda h: slice(h * HEAD_DIM, (h + 1) * HEAD_DIM)

    def body(*refs):
        q_ref, k_ref, v_ref = refs[:3]
        src_refs = refs[3:3 + ns]
        o_ref, lse_ref = refs[3 + ns:5 + ns]
        dst_refs = refs[5 + ns:5 + 2 * ns]
        sems = refs[5 + 2 * ns:]
        b, g, i = pl.program_id(0), pl.program_id(1), pl.program_id(2)
        qb = [q_ref[0, :, qk(h)] for h in range(hp)]

        step_no = (b * (HEADS // hp) + g) * nq + i
        for phase, at in enumerate((0, (3 * B * (HEADS // hp) * nq) // 4)):
            @pl.when(step_no == at)
            def _(phase=phase):
                for call in _gather_copies(src_refs, dst_refs, *sems)[phase]:
                    call()

        def step(j, carry, diagonal):
            rows = pl.ds(pl.multiple_of(j * t, t), t)
            s = [_dg(qb[h], k_ref[0, rows, qk(h)], 1, 1, None) * ATTN_SCALE for h in range(hp)]
            if diagonal:
                keep = _causal_mask(0, 0, t, t)
                s = [jnp.where(keep, x, -1e30) for x in s]
            m_new = [jnp.maximum(carry[h][0], jnp.max(s[h], axis=-1, keepdims=True)) for h in range(hp)]
            p = [jnp.exp(s[h] - m_new[h]) for h in range(hp)]
            alpha = [jnp.exp(carry[h][0] - m_new[h]) for h in range(hp)]
            l = [alpha[h] * carry[h][1] + jnp.sum(p[h], axis=-1, keepdims=True) for h in range(hp)]
            pv = [jnp.dot(p[h].astype(BF16), v_ref[0, rows, vd(h)], preferred_element_type=F32) for h in range(hp)]
            return tuple((m_new[h], l[h], alpha[h] * carry[h][2] + pv[h]) for h in range(hp))

        init = tuple((jnp.full((t, 1), -1e30, F32), jnp.zeros((t, 1), F32), jnp.zeros((t, HEAD_DIM), F32)) for _ in range(hp))
        below = lax.fori_loop(0, i, lambda j, carry: step(j, carry, False), init)
        for h, (m, l, acc) in enumerate(step(i, below, True)):
            o_ref[0, :, vd(h)] = acc / l
            lse_ref[0, h] = m + jnp.log(l)

        @pl.when((b == B - 1) & (g == HEADS // hp - 1) & (i == nq - 1))
        def _():
            for wait in _gather_copies(src_refs, dst_refs, *sems)[2]:
                wait()

    return pl.pallas_call(
        body, grid=(B, HEADS // hp, nq),
        in_specs=[pl.BlockSpec((1, t, hp * QK_PAD), lambda b, g, i: (b, i, g)),
                  pl.BlockSpec((1, S, hp * QK_PAD), lambda b, g, i: (b, 0, g)),
                  pl.BlockSpec((1, S, hp * HEAD_DIM), lambda b, g, i: (b, 0, g))] + [_ANY] * ns,
        out_specs=[pl.BlockSpec((1, t, hp * HEAD_DIM), lambda b, g, i: (b, i, g)),
                   pl.BlockSpec((1, hp, t, 1), lambda b, g, i: (b, g, i, 0))] + [_ANY] * ns,
        out_shape=[_sds((B, S, HEADS * HEAD_DIM), F32), _sds((B, HEADS, S, 1), F32)] + [_sds((4,) + s.shape, s.dtype) for s in shards],
        scratch_shapes=_gather_scratch(ns),
        compiler_params=_params(("arbitrary", "arbitrary", "arbitrary")), name="attn_fwd",
    )(q, k, v, *shards)


def _attn_bwd(q, k, v, o, lse, do, partials):
    B, S, _ = q.shape
    t = min(ATTN_TILE, S)
    nq = S // t
    ns = len(partials)

    hp = ATTN_HEADS_PER_STEP
    qk = lambda h: slice(h * QK_PAD, (h + 1) * QK_PAD)
    vd = lambda h: slice(h * HEAD_DIM, (h + 1) * HEAD_DIM)
    heads = range(hp)

    def body(*refs):
        q_ref, k_ref, v_ref, o_ref, lse_ref, do_ref = refs[:6]
        src_refs = refs[6:6 + ns]
        dq_ref, dk_ref, dv_ref = refs[6 + ns:9 + ns]
        dst_refs = refs[9 + ns:9 + 2 * ns]
        dsum_ref, send_sems, recv_sems, local_sems = refs[9 + 2 * ns:]
        b, g, j = pl.program_id(0), pl.program_id(1), pl.program_id(2)

        @pl.when((b == 0) & (g == 0) & (j == 0))
        def _():
            for start in _scatter_copies(src_refs, dst_refs, send_sems, recv_sems, local_sems)[0]:
                start()

        @pl.when(j == 0)
        def _():
            dq_ref[...] = jnp.zeros_like(dq_ref)
            for h in heads:
                dsum_ref[h] = jnp.sum(do_ref[0, :, vd(h)] * o_ref[0, :, vd(h)], axis=-1, keepdims=True)

        kb = [k_ref[0, :, qk(h)] for h in heads]
        vb = [v_ref[0, :, vd(h)] for h in heads]

        def step(i, carry, diagonal):
            rows = pl.ds(pl.multiple_of(i * t, t), t)
            qb = [q_ref[0, rows, qk(h)] for h in heads]
            dob = [do_ref[0, rows, vd(h)].astype(BF16) for h in heads]
            s = [_dg(qb[h], kb[h], 1, 1, None) * ATTN_SCALE for h in heads]
            p = [jnp.exp(s[h] - lse_ref[0, h, rows, :]) for h in heads]
            if diagonal:
                keep = _causal_mask(0, 0, t, t)
                p = [jnp.where(keep, x, 0.0) for x in p]
            dp = [_dg(dob[h], vb[h], 1, 1, None) for h in heads]
            dv = [carry[h][1] + _dg(p[h].astype(BF16), dob[h], 0, 0, None) for h in heads]
            ds = [(p[h] * (dp[h] - dsum_ref[h, rows, :]) * ATTN_SCALE).astype(BF16) for h in heads]
            for h in heads:
                dq_ref[0, rows, qk(h)] += jnp.dot(ds[h], kb[h], preferred_element_type=F32)
            return tuple((carry[h][0] + _dg(ds[h], qb[h], 0, 0, None), dv[h]) for h in heads)

        zeros = tuple((jnp.zeros((t, QK_PAD), F32), jnp.zeros((t, HEAD_DIM), F32)) for _ in heads)
        on_diagonal = step(j, zeros, True)
        done = lax.fori_loop(j + 1, nq, lambda i, carry: step(i, carry, False), on_diagonal)
        for h, (dk, dv) in enumerate(done):
            dk_ref[0, :, qk(h)] = dk
            dv_ref[0, :, vd(h)] = dv

        @pl.when((b == B - 1) & (g == HEADS // hp - 1) & (j == nq - 1))
        def _():
            for wait in _scatter_copies(src_refs, dst_refs, send_sems, recv_sems, local_sems)[1]:
                wait()

    return pl.pallas_call(
        body, grid=(B, HEADS // hp, nq),
        in_specs=[pl.BlockSpec((1, S, hp * QK_PAD), lambda b, g, j: (b, 0, g)),
                  pl.BlockSpec((1, t, hp * QK_PAD), lambda b, g, j: (b, j, g)),
                  pl.BlockSpec((1, t, hp * HEAD_DIM), lambda b, g, j: (b, j, g)),
                  pl.BlockSpec((1, S, hp * HEAD_DIM), lambda b, g, j: (b, 0, g)),
                  pl.BlockSpec((1, hp, S, 1), lambda b, g, j: (b, g, 0, 0)),
                  pl.BlockSpec((1, S, hp * HEAD_DIM), lambda b, g, j: (b, 0, g))] + [_ANY] * ns,
        out_specs=[pl.BlockSpec((1, S, hp * QK_PAD), lambda b, g, j: (b, 0, g)),
                   pl.BlockSpec((1, t, hp * QK_PAD), lambda b, g, j: (b, j, g)),
                   pl.BlockSpec((1, t, hp * HEAD_DIM), lambda b, g, j: (b, j, g))] + [_ANY] * ns,
        out_shape=[_sds((B, S, HEADS * QK_PAD), F32), _sds((B, S, HEADS * QK_PAD), F32), _sds((B, S, HEADS * HEAD_DIM), F32)]
                  + [_scattered_shape(p) for p in partials],
        scratch_shapes=[pltpu.VMEM((hp, S, 1), F32)] + _scatter_scratch(ns),
        compiler_params=_params(("arbitrary", "arbitrary", "arbitrary")), name="attn_bwd",
    )(q, k, v, o, lse, do, *partials)


def _gdn_pre_fn(xq, xk, xv, wq, wk, wv, keeps):
    def conv_silu(x, w):
        acc = x * w[3]
        for s in (1, 2, 3):
            acc = acc + _shift_rows(x, keeps[s - 1], s) * w[3 - s]
        return _silu(acc)

    def l2(x):
        return x * lax.rsqrt(jnp.sum(x * x, axis=-1, keepdims=True) + EPS)

    return l2(conv_silu(xq, wq)) * (HEAD_DIM ** -0.5), l2(conv_silu(xk, wk)), conv_silu(xv, wv)


def _gdn_pre_specs(S):
    x_specs = [pl.BlockSpec((1, S, HEAD_DIM), lambda h, b, g=g: (b, 0, g * HEADS + h)) for g in range(3)]
    w_specs = [pl.BlockSpec((CONV_TAPS, HEAD_DIM), lambda h, b, g=g: (0, g * HEADS + h)) for g in range(3)]
    out_spec = pl.BlockSpec((1, S, HEAD_DIM), lambda h, b: (b, 0, h))
    return x_specs, w_specs, out_spec


def _row_keeps(S):
    t = lax.broadcasted_iota(jnp.int32, (S, HEAD_DIM), 0)
    return [(t >= s).astype(F32) for s in (1, 2, 3)]


def _gdn_pre_fwd(gqkv, conv_w):
    B, S, _ = gqkv.shape
    x_specs, w_specs, out_spec = _gdn_pre_specs(S)

    def body(xq_ref, xk_ref, xv_ref, wq_ref, wk_ref, wv_ref, q_ref, k_ref, v_ref):
        taps = [[w[i:i + 1, :] for i in range(CONV_TAPS)] for w in (wq_ref, wk_ref, wv_ref)]
        q, k, v = _gdn_pre_fn(xq_ref[0], xk_ref[0], xv_ref[0], *taps, _row_keeps(S))
        q_ref[0], k_ref[0], v_ref[0] = q, k, v

    return pl.pallas_call(
        body, grid=(HEADS, B), in_specs=x_specs + w_specs, out_specs=[out_spec] * 3,
        out_shape=[_sds((B, S, HEADS * HEAD_DIM), F32)] * 3,
        compiler_params=_params(("parallel", "parallel")), name="gdn_pre_fwd",
    )(gqkv, gqkv, gqkv, conv_w, conv_w, conv_w)


def _gdn_pre_bwd(gqkv, conv_w, dq, dk, dv, halves):
    B, S, _ = gqkv.shape
    x_specs, w_specs, out_spec = _gdn_pre_specs(S)
    dw_spec = pl.BlockSpec((CONV_TAPS, HEAD_DIM), lambda h, b: (0, h))
    ns = len(halves)

    def body(*refs):
        xq_ref, xk_ref, xv_ref, wq_ref, wk_ref, wv_ref, dq_ref, dk_ref, dv_ref = refs[:9]
        src_refs = refs[9:9 + ns]
        dxq_ref, dxk_ref, dxv_ref, dwq_ref, dwk_ref, dwv_ref = refs[9 + ns:15 + ns]
        dst_refs = refs[15 + ns:15 + 2 * ns]
        sems = refs[15 + 2 * ns:]
        first = (pl.program_id(0) == 0) & (pl.program_id(1) == 0)
        last = (pl.program_id(0) == HEADS - 1) & (pl.program_id(1) == B - 1)

        @pl.when(first)
        def _():
            for start in _swap_copies(src_refs, dst_refs, *sems)[0]:
                start()

        @pl.when(pl.program_id(1) == 0)
        def _():
            for r in (dwq_ref, dwk_ref, dwv_ref):
                r[...] = jnp.zeros_like(r)

        taps = [[w[i:i + 1, :] for i in range(CONV_TAPS)] for w in (wq_ref, wk_ref, wv_ref)]
        keeps = _row_keeps(S)
        _, pull = jax.vjp(lambda *a: _gdn_pre_fn(*a, keeps), xq_ref[0], xk_ref[0], xv_ref[0], *taps)
        dxq, dxk, dxv, dwq, dwk, dwv = pull((dq_ref[0], dk_ref[0], dv_ref[0]))
        dxq_ref[0], dxk_ref[0], dxv_ref[0] = dxq.astype(BF16), dxk.astype(BF16), dxv.astype(BF16)
        for ref, dw in ((dwq_ref, dwq), (dwk_ref, dwk), (dwv_ref, dwv)):
            for i in range(CONV_TAPS):
                ref[i:i + 1, :] += dw[i]

        @pl.when(last)
        def _():
            for wait in _swap_copies(src_refs, dst_refs, *sems)[1]:
                wait()

    hw = HEADS * HEAD_DIM
    return pl.pallas_call(
        body, grid=(HEADS, B), in_specs=x_specs + w_specs + [out_spec] * 3 + [_ANY] * ns,
        out_specs=[out_spec] * 3 + [dw_spec] * 3 + [_ANY] * ns,
        out_shape=[_sds((B, S, hw), BF16)] * 3 + [_sds((CONV_TAPS, hw), F32)] * 3 + [_swapped_shape(h) for h in halves],
        scratch_shapes=_swap_scratch(ns),
        compiler_params=_params(("arbitrary", "arbitrary")), name="gdn_pre_bwd",
    )(gqkv, gqkv, gqkv, conv_w, conv_w, conv_w, dq, dk, dv, *halves)


def _chunk_masks():
    i = lax.broadcasted_iota(jnp.int32, (CHUNK, CHUNK), 0)
    j = lax.broadcasted_iota(jnp.int32, (CHUNK, CHUNK), 1)
    lower, after = (j <= i).astype(F32), (j > i).astype(F32)
    return {"le": lower, "le_gt": jnp.concatenate([lower, after], axis=0), "strict": (j < i).astype(F32)}


def _gdn_chunk_fn(groups, masks):
    lane = lax.broadcasted_iota(jnp.int32, (groups, 1, 128), 2)
    head = lax.broadcasted_iota(jnp.int32, (groups, 1, 128), 0) % HEADS
    pick_a, pick_b = (lane == head).astype(F32), (lane == head + HEADS).astype(F32)
    lower, lower_after, strict = (jnp.broadcast_to(masks[n], (groups,) + masks[n].shape) for n in ("le", "le_gt", "strict"))
    ones_row = jnp.ones((1, 1, HEAD_DIM), F32)

    def f(q, k, v, gab, a_row, dt_row, state):
        ga = jnp.sum(gab * pick_a, axis=2, keepdims=True)
        gb = jnp.sum(gab * pick_b, axis=2, keepdims=True)
        a_log = jnp.sum(a_row * pick_a, axis=2, keepdims=True)
        dt_bias = jnp.sum(dt_row * pick_a, axis=2, keepdims=True)
        beta = _sigmoid(gb)
        g = -jnp.exp(a_log) * _softplus(ga + dt_bias)
        g_wide = g * ones_row
        cum, rest = _row_halves(_hi_nn(lower_after, g_wide))
        total = jnp.sum(g_wide, axis=1, keepdims=True)
        diff = _hi_nn(lower, g * strict)
        decay = lower * jnp.exp(diff)
        e_cum = jnp.exp(cum)
        lmat = strict * (beta * _bf_nt(k, k) * decay)
        u, w = _lane_halves(_unit_lower_solve(lmat, jnp.concatenate([v * beta, k * (beta * e_cum)], axis=2)))
        attn = _bf_nt(q, k) * decay
        v_new = u - _bf_nn(w, state)
        o = _bf_nn(q * e_cum, state) + _bf_nn(attn, v_new)
        new_state = state * jnp.exp(total) + _bf_tn(k * jnp.exp(rest), v_new)
        return o, new_state

    return f


def _gdn_chunk_fwd(q, k, v, gab, scal, shards):
    B, S, W = q.shape
    N = S // CHUNK
    ns = len(shards)

    def body(*refs):
        q_ref, k_ref, v_ref, gab_ref, sc_ref = refs[:5]
        src_refs = refs[5:5 + ns]
        o_ref, st_ref = refs[5 + ns:7 + ns]
        dst_refs = refs[7 + ns:7 + 2 * ns]
        state_ref, send_sems, recv_sems, local_sems = refs[7 + 2 * ns:]
        n = pl.program_id(0)

        @pl.when(n == 0)
        def _():
            for start in _gather_copies(src_refs, dst_refs, send_sems, recv_sems, local_sems)[0]:
                start()
            state_ref[...] = jnp.zeros_like(state_ref)

        @pl.when(n == (2 * N) // 3)
        def _():
            for pass_on in _gather_copies(src_refs, dst_refs, send_sems, recv_sems, local_sems)[1]:
                pass_on()

        groups = [(b, h) for b in range(B) for h in range(HEADS)]
        gather = lambda ref: jnp.stack([ref[b, :, h * HEAD_DIM:(h + 1) * HEAD_DIM] for b, h in groups])
        state = state_ref[...]
        for i, (b, h) in enumerate(groups):
            st_ref[b, 0, h] = state[i]
        o, new_state = _gdn_chunk_fn(len(groups), _chunk_masks())(
            gather(q_ref), gather(k_ref), gather(v_ref), jnp.stack([gab_ref[b] for b, _ in groups]), sc_ref[0:1, :], sc_ref[1:2, :], state)
        for i, (b, h) in enumerate(groups):
            o_ref[b, :, h * HEAD_DIM:(h + 1) * HEAD_DIM] = o[i]
        state_ref[...] = new_state

        @pl.when(n == N - 1)
        def _():
            for wait in _gather_copies(src_refs, dst_refs, send_sems, recv_sems, local_sems)[2]:
                wait()

    seq = pl.BlockSpec((B, CHUNK, W), lambda n: (0, n, 0))
    return pl.pallas_call(
        body, grid=(N,),
        in_specs=[seq, seq, seq, pl.BlockSpec((B, CHUNK, GAB_W), lambda n: (0, n, 0)), _const_spec((8, 128))] + [_ANY] * ns,
        out_specs=[seq, pl.BlockSpec((B, 1, HEADS, HEAD_DIM, HEAD_DIM), lambda n: (0, n, 0, 0, 0))] + [_ANY] * ns,
        out_shape=[_sds((B, S, W), F32), _sds((B, N, HEADS, HEAD_DIM, HEAD_DIM), F32)] + [_sds((4,) + s.shape, s.dtype) for s in shards],
        scratch_shapes=[pltpu.VMEM((B * HEADS, HEAD_DIM, HEAD_DIM), F32)] + _gather_scratch(ns),
        compiler_params=_params(("arbitrary",)), name="gdn_chunk_fwd",
    )(q, k, v, gab, scal, *shards)


def _gdn_chunk_bwd(q, k, v, gab, scal, states, do, partials):
    B, S, W = q.shape
    N = S // CHUNK
    ns = len(partials)

    def body(*refs):
        q_ref, k_ref, v_ref, gab_ref, sc_ref, st_ref, do_ref = refs[:7]
        src_refs = refs[7:7 + ns]
        dq_ref, dk_ref, dv_ref, dgab_ref, dsc_ref = refs[7 + ns:12 + ns]
        dst_refs = refs[12 + ns:12 + 2 * ns]
        dstate_ref, send_sems, recv_sems, local_sems = refs[12 + 2 * ns:]
        n = pl.program_id(0)

        @pl.when(n == 0)
        def _():
            for start in _scatter_copies(src_refs, dst_refs, send_sems, recv_sems, local_sems)[0]:
                start()
            dstate_ref[...] = jnp.zeros_like(dstate_ref)
            dsc_ref[...] = jnp.zeros_like(dsc_ref)

        groups = [(b, h) for b in range(B) for h in range(HEADS)]
        gather = lambda ref: jnp.stack([ref[b, :, h * HEAD_DIM:(h + 1) * HEAD_DIM] for b, h in groups])
        _, pull = jax.vjp(_gdn_chunk_fn(len(groups), _chunk_masks()), gather(q_ref), gather(k_ref), gather(v_ref),
                          jnp.stack([gab_ref[b] for b, _ in groups]), sc_ref[0:1, :], sc_ref[1:2, :],
                          jnp.stack([st_ref[b, 0, h] for b, h in groups]))
        dq, dk, dv, dg, d_a, d_dt, dstate = pull((gather(do_ref), dstate_ref[...]))
        for i, (b, h) in enumerate(groups):
            lanes = slice(h * HEAD_DIM, (h + 1) * HEAD_DIM)
            dq_ref[b, :, lanes] = dq[i]
            dk_ref[b, :, lanes] = dk[i]
            dv_ref[b, :, lanes] = dv[i]
        for b in range(B):
            dgab_ref[b] = sum(dg[b * HEADS + h] for h in range(HEADS)).astype(BF16)
        dstate_ref[...] = dstate
        dsc_ref[0:1, :] += d_a
        dsc_ref[1:2, :] += d_dt

        @pl.when(n == N - 1)
        def _():
            for wait in _scatter_copies(src_refs, dst_refs, send_sems, recv_sems, local_sems)[1]:
                wait()

    seq = pl.BlockSpec((B, CHUNK, W), lambda n: (0, N - 1 - n, 0))
    gab_spec = pl.BlockSpec((B, CHUNK, GAB_W), lambda n: (0, N - 1 - n, 0))
    return pl.pallas_call(
        body, grid=(N,),
        in_specs=[seq, seq, seq, gab_spec, _const_spec((8, 128)),
                  pl.BlockSpec((B, 1, HEADS, HEAD_DIM, HEAD_DIM), lambda n: (0, N - 1 - n, 0, 0, 0)), seq] + [_ANY] * ns,
        out_specs=[seq, seq, seq, gab_spec, _const_spec((8, 128))] + [_ANY] * ns,
        out_shape=[_sds((B, S, W), F32)] * 3 + [_sds((B, S, GAB_W), BF16), _sds((8, 128), F32)] + [_scattered_shape(p) for p in partials],
        scratch_shapes=[pltpu.VMEM((B * HEADS, HEAD_DIM, HEAD_DIM), F32)] + _scatter_scratch(ns),
        compiler_params=_params(("arbitrary",)), name="gdn_chunk_bwd",
    )(q, k, v, gab, scal, states, do, *partials)


def _mix_fn(ao, go, gz, w_mla, w_gdn):
    return tuple(_rms(ao[h], w_mla[h]) for h in range(HEADS)) + tuple(_rms(go[h], w_gdn) * _silu(gz[h]) for h in range(HEADS))


def _mix_operands(ao_ref, go_ref, gz_ref, nw_ref):
    blocks = lambda ref: [ref[:, h * HEAD_DIM:(h + 1) * HEAD_DIM] for h in range(HEADS)]
    return blocks(ao_ref), blocks(go_ref), blocks(gz_ref), [nw_ref[h:h + 1, :] for h in range(HEADS)], nw_ref[HEADS:HEADS + 1, :]


def _mix_fwd(ao, go, gz, nw, w_out, x2):
    T, D = x2.shape
    tm = min(TOKEN_TILE, T)
    MW = 2 * HEADS * HEAD_DIM

    def body(ao_ref, go_ref, gz_ref, nw_ref, w_ref, x_ref, mix_ref, h_ref):
        outs = _mix_fn(*_mix_operands(ao_ref, go_ref, gz_ref, nw_ref))
        for i, piece in enumerate(outs):
            mix_ref[:, i * HEAD_DIM:(i + 1) * HEAD_DIM] = piece.astype(BF16)
        h_ref[...] = x_ref[...] + jnp.dot(mix_ref[...], w_ref[...], preferred_element_type=F32)

    half = HEADS * HEAD_DIM
    return pl.pallas_call(
        body, grid=(T // tm,),
        in_specs=[_row_spec(tm, half), _row_spec(tm, half), _row_spec(tm, half), _const_spec((8, 128)), _const_spec((MW, D)),
                  _row_spec(tm, D)],
        out_specs=[_row_spec(tm, MW), _row_spec(tm, D)],
        out_shape=[_sds((T, MW), BF16), _sds((T, D), F32)],
        compiler_params=_params(("parallel",)), name="mix_fwd",
    )(ao, go, gz, nw, w_out, x2)


def _mix_bwd(ao, go, gz, nw, w_out, dh):
    T, D = dh.shape
    tm = min(TOKEN_TILE, T)
    MW = 2 * HEADS * HEAD_DIM
    half = HEADS * HEAD_DIM

    def body(ao_ref, go_ref, gz_ref, nw_ref, w_ref, dh_ref, dao_ref, dgo_ref, dgz_ref, dnw_ref):
        @pl.when(pl.program_id(0) == 0)
        def _():
            dnw_ref[...] = jnp.zeros_like(dnw_ref)

        d_mix = _dg(dh_ref[...].astype(BF16), w_ref[...], 1, 1, None)
        cts = tuple(d_mix[:, i * HEAD_DIM:(i + 1) * HEAD_DIM] for i in range(2 * HEADS))
        _, pull = jax.vjp(_mix_fn, *_mix_operands(ao_ref, go_ref, gz_ref, nw_ref))
        d_ao, d_go, d_gz, d_wm, d_wg = pull(cts)
        for h in range(HEADS):
            lanes = slice(h * HEAD_DIM, (h + 1) * HEAD_DIM)
            dao_ref[:, lanes] = d_ao[h]
            dgo_ref[:, lanes] = d_go[h]
            dgz_ref[:, lanes] = d_gz[h].astype(BF16)
            dnw_ref[h:h + 1, :] += d_wm[h]
        dnw_ref[HEADS:HEADS + 1, :] += d_wg

    return pl.pallas_call(
        body, grid=(T // tm,),
        in_specs=[_row_spec(tm, half), _row_spec(tm, half), _row_spec(tm, half), _const_spec((8, 128)), _const_spec((MW, D)),
                  _row_spec(tm, D)],
        out_specs=[_row_spec(tm, half)] * 3 + [_const_spec((8, 128))],
        out_shape=[_sds((T, half), F32)] * 2 + [_sds((T, half), BF16), _sds((8, 128), F32)],
        compiler_params=_params(("arbitrary",)), name="mix_bwd",
    )(ao, go, gz, nw, w_out, dh)


def _up_spec(w_up, tf):
    per_shard = w_up.shape[2] // tf
    return pl.BlockSpec((None, w_up.shape[1], tf), lambda i, j: (j // per_shard, 0, j % per_shard))


def _mlp_fwd(h2, w_mn, w_up, w_down, target):
    T, D = h2.shape
    FF = w_down.shape[0]
    tm, tf = min(MLP_TOKEN_TILE, T), min(FF_TILE, w_up.shape[2])
    nf = FF // tf

    def body(h_ref, wn_ref, wu_ref, wd_ref, t_ref, hn_ref, act_ref, dy_ref, sq_ref, acc_ref):
        j = pl.program_id(1)

        @pl.when(j == 0)
        def _():
            hn_ref[...] = _rms(h_ref[...], wn_ref[...]).astype(BF16)
            acc_ref[...] = jnp.zeros_like(acc_ref)

        up = jnp.dot(hn_ref[...], wu_ref[...], preferred_element_type=F32)
        act = jnp.square(jnp.maximum(up, 0.0)).astype(BF16)
        act_ref[...] = act
        acc_ref[...] += jnp.dot(act, wd_ref[...], preferred_element_type=F32)

        @pl.when(j == nf - 1)
        def _():
            err = h_ref[...] + acc_ref[...] - t_ref[...]
            dy_ref[...] = err * (1.0 / D)
            sq_ref[...] = jnp.zeros_like(sq_ref) + jnp.sum(err * err)

    tok = lambda w: pl.BlockSpec((tm, w), lambda i, j: (i, 0))
    return pl.pallas_call(
        body, grid=(T // tm, nf),
        in_specs=[tok(D), _const_spec((1, D)), _up_spec(w_up, tf), pl.BlockSpec((tf, D), lambda i, j: (j, 0)), tok(D)],
        out_specs=[tok(D), pl.BlockSpec((tm, tf), lambda i, j: (i, j)), tok(D), pl.BlockSpec((1, 8, 128), lambda i, j: (i, 0, 0))],
        out_shape=[_sds((T, D), BF16), _sds((T, FF), BF16), _sds((T, D), F32), _sds((T // tm, 8, 128), F32)],
        scratch_shapes=[pltpu.VMEM((tm, D), F32)],
        compiler_params=_params(("parallel", "arbitrary")), name="mlp_fwd",
    )(h2, w_mn, w_up, w_down, target)


def _mlp_bwd(h2, w_mn, act, w_up, w_down, dy):
    T, D = h2.shape
    FF = w_down.shape[0]
    tm, tf = min(MLP_TOKEN_TILE, T), min(FF_TILE, w_up.shape[2])
    nf = FF // tf

    def body(h_ref, wn_ref, act_ref, wu_ref, wd_ref, dy_ref, dh_ref, dup_ref, dwn_ref, acc_ref, dyb_ref):
        i, j = pl.program_id(0), pl.program_id(1)

        @pl.when((i == 0) & (j == 0))
        def _():
            dwn_ref[...] = jnp.zeros_like(dwn_ref)

        @pl.when(j == 0)
        def _():
            acc_ref[...] = jnp.zeros_like(acc_ref)
            dyb_ref[...] = dy_ref[...].astype(BF16)

        r = jnp.sqrt(act_ref[...].astype(F32))
        d_act = _dg(dyb_ref[...], wd_ref[...], 1, 1, None)
        d_up = (d_act * (2.0 * r)).astype(BF16)
        dup_ref[...] = d_up
        acc_ref[...] += _dg(d_up, wu_ref[...], 1, 1, None)

        @pl.when(j == nf - 1)
        def _():
            _, pull = jax.vjp(_rms, h_ref[...], wn_ref[...])
            dh, dwn = pull(acc_ref[...])
            dh_ref[...] = dh + dy_ref[...]
            dwn_ref[...] += dwn

    tok = lambda w: pl.BlockSpec((tm, w), lambda i, j: (i, 0))
    ff = pl.BlockSpec((tm, tf), lambda i, j: (i, j))
    return pl.pallas_call(
        body, grid=(T // tm, nf),
        in_specs=[tok(D), _const_spec((1, D)), ff, _up_spec(w_up, tf), pl.BlockSpec((tf, D), lambda i, j: (j, 0)), tok(D)],
        out_specs=[tok(D), ff, _const_spec((1, D))],
        out_shape=[_sds((T, D), F32), _sds((T, FF), BF16), _sds((1, D), F32)],
        scratch_shapes=[pltpu.VMEM((tm, D), F32), pltpu.VMEM((tm, D), BF16)],
        compiler_params=_params(("arbitrary", "arbitrary")), name="mlp_bwd",
    )(h2, w_mn, act, w_up, w_down, dy)


def _rope_pad(a):
    z = jnp.zeros(a.shape[:-1] + (ROPE_HALF,), a.dtype)
    return jnp.concatenate([a[..., :ROPE_HALF], z, a[..., ROPE_HALF:], z], axis=-1)


def _rope_unpad(a):
    return jnp.concatenate([a[..., :ROPE_HALF], a[..., 2 * ROPE_HALF:3 * ROPE_HALF]], axis=-1)


_G0 = 2 * LORA + ROPE_DIM
W_IN_COLS = _G0 + GQKV_W + GZ_W + 2 * HEADS


def _widen_w_in_t(w_t):
    z = jnp.zeros((ROPE_HALF, w_t.shape[1]), w_t.dtype)
    pad = jnp.zeros((GAB_W - 2 * HEADS, w_t.shape[1]), w_t.dtype)
    return jnp.concatenate([w_t[:2 * LORA + ROPE_HALF], z, w_t[2 * LORA + ROPE_HALF:_G0], z, w_t[_G0:], pad], axis=0)


def _narrow_w_in_t(w_t):
    return jnp.concatenate([w_t[:2 * LORA + ROPE_HALF], w_t[2 * LORA + 2 * ROPE_HALF:2 * LORA + 3 * ROPE_HALF],
                            w_t[LAT_W:LAT_W + W_IN_COLS - _G0]], axis=0)


def _stack_mla(w_uq, w_ukv):
    uq = w_uq.reshape(LORA, HEADS, QK_DIM)
    ukv = w_ukv.reshape(LORA, HEADS, 2 * HEAD_DIM)
    parts = [uq[:, :, :HEAD_DIM], _rope_pad(uq[:, :, HEAD_DIM:]), ukv[:, :, :HEAD_DIM], ukv[:, :, HEAD_DIM:]]
    return jnp.concatenate([p.transpose(1, 0, 2) for p in parts], axis=0)


def _unstack_mla(w):
    p = [w[i * HEADS:(i + 1) * HEADS].transpose(1, 0, 2) for i in range(4)]
    uq = jnp.concatenate([p[0], _rope_unpad(p[1])], axis=-1).reshape(LORA, HEADS * QK_DIM)
    ukv = jnp.concatenate([p[2], p[3]], axis=-1).reshape(LORA, HEADS * 2 * HEAD_DIM)
    return uq, ukv


def _rows8(rows):
    a = jnp.concatenate(rows, axis=0)
    return jnp.pad(a, ((0, 8 - a.shape[0]), (0, 0)))


def _qk_norm_rows(q_norm_w, k_norm_w):
    return _rows8([q_norm_w[:, :HEAD_DIM], _rope_pad(q_norm_w[:, HEAD_DIM:]), k_norm_w[:, :HEAD_DIM], _rope_pad(k_norm_w[:, HEAD_DIM:])])


def _rope_rows():
    inv_freq = ROPE_THETA ** (-jnp.arange(ROPE_HALF, dtype=F32) / ROPE_HALF)
    z = jnp.zeros((ROPE_HALF,), F32)
    freq = jnp.concatenate([inv_freq, z, inv_freq, z])
    sign = jnp.concatenate([-jnp.ones((ROPE_HALF,), F32), z, jnp.ones((ROPE_HALF,), F32), z])
    return _rows8([freq[None], sign[None]])


def _column_shards(a):
    return a.reshape(a.shape[0], 4, a.shape[1] // 4).transpose(1, 0, 2)


def _from_column_shards(a):
    return a.transpose(1, 0, 2).reshape(a.shape[1], 4 * a.shape[2])


_ANY = pl.BlockSpec(memory_space=pl.ANY)
_OTHER_CHIPS = ((1, 0), (0, 1), (1, 1))


def _here():
    return lax.axis_index("x"), lax.axis_index("y"), lax.axis_index("c")


def _flip(v, bit):
    return 1 - v if bit else v


def _remote(src, dst, send_sems, recv_sems, k, to):
    return pltpu.make_async_remote_copy(src_ref=src, dst_ref=dst, send_sem=send_sems.at[k], recv_sem=recv_sems.at[k],
                                        device_id=to, device_id_type=MESH)


def _half_of(ref, k, shape):
    r, c = shape
    if (r // 2) % 16 == 0:
        return ref.at[pl.ds(pl.multiple_of(k * (r // 2), 16), r // 2)]
    if (c // 2) % 128 == 0:
        return ref.at[:, pl.ds(pl.multiple_of(k * (c // 2), 128), c // 2)]
    return None


def _gather_copies(srcs, dsts, send_sems, recv_sems, local_sems):
    x, y, c = _here()
    slot, sibling, n = 2 * x + y, (x, y, 1 - c), len(srcs)
    starts, passes, waits = [], [], []
    for i, (src, dst) in enumerate(zip(srcs, dsts)):
        own = pltpu.make_async_copy(src, dst.at[slot], local_sems.at[i])
        starts.append(own.start)
        waits.append(own.wait)
        halves = _half_of(src, c, src.shape) is not None
        for j, (fx, fy) in enumerate(_OTHER_CHIPS):
            cx, cy = _flip(x, fx), _flip(y, fy)
            there = dst.at[2 * cx + cy]
            if halves:
                push = _remote(_half_of(src, c, src.shape), _half_of(dst.at[slot], c, src.shape), send_sems, recv_sems, 3 * i + j, (cx, cy, c))
                landed, other = _half_of(there, c, src.shape), _half_of(there, 1 - c, src.shape)
                onward = _remote(landed, landed, send_sems, recv_sems, 3 * n + 3 * i + j, sibling)
                passes += [_remote(landed, landed, send_sems, recv_sems, 3 * i + j, (cx, cy, c)).wait_recv, onward.start]
                waits += [_remote(other, other, send_sems, recv_sems, 3 * n + 3 * i + j, sibling).wait_recv, onward.wait_send]
            else:
                push = _remote(src, dst.at[slot], send_sems, recv_sems, 3 * i + j, (cx, cy, c))
                waits.append(_remote(there, there, send_sems, recv_sems, 3 * i + j, (cx, cy, c)).wait_recv)
            starts.append(push.start)
            waits.append(push.wait_send)
    return starts, passes, waits


def _gather_scratch(n):
    return [pltpu.SemaphoreType.DMA((6 * n,)), pltpu.SemaphoreType.DMA((6 * n,)), pltpu.SemaphoreType.DMA((n,))]


def _all_gather(shards, name):
    ns = len(shards)

    def body(*refs):
        starts, passes, waits = _gather_copies(refs[:ns], refs[ns:2 * ns], *refs[2 * ns:])
        for call in starts + passes + waits:
            call()

    return pl.pallas_call(
        body, in_specs=[_ANY] * ns, out_specs=[_ANY] * ns, out_shape=[_sds((4,) + s.shape, s.dtype) for s in shards],
        scratch_shapes=_gather_scratch(ns), name=name,
    )(*shards)


def _by_lanes(shape):
    return (shape[-2] // 2) % 16 != 0


def _scattered_shape(p):
    r, c = p.shape[1:]
    return _sds((8, r, c // 2) if _by_lanes(p.shape) else (8, r // 2, c), p.dtype)


def _scatter_copies(srcs, dsts, send_sems, recv_sems, local_sems, whole=0):
    x, y, c = _here()
    me = 4 * x + 2 * y + c
    starts, waits = [], []
    for i, (src, dst) in enumerate(zip(srcs, dsts)):
        def piece(px, py, pc, src=src, entire=i >= len(srcs) - whole):
            if entire:
                return src
            if _by_lanes(src.shape):
                half = src.shape[2] // 2
                return src.at[2 * px + py, :, pl.ds(pl.multiple_of(pc * half, 128), half)]
            half = src.shape[1] // 2
            return src.at[2 * px + py, pl.ds(pl.multiple_of(pc * half, 16), half)]

        own = pltpu.make_async_copy(piece(x, y, c), dst.at[me], local_sems.at[i])
        starts.append(own.start)
        waits.append(own.wait)
        for k in range(1, 8):
            px, py, pc = _flip(x, k & 4), _flip(y, k & 2), _flip(c, k & 1)
            push = _remote(piece(px, py, pc), dst.at[me], send_sems, recv_sems, 7 * i + k - 1, (px, py, pc))
            landed = dst.at[4 * px + 2 * py + pc]
            starts.append(push.start)
            waits += [_remote(landed, landed, send_sems, recv_sems, 7 * i + k - 1, (px, py, pc)).wait_recv, push.wait_send]
    return starts, waits


def _scatter_scratch(n):
    return [pltpu.SemaphoreType.DMA((7 * n,)), pltpu.SemaphoreType.DMA((7 * n,)), pltpu.SemaphoreType.DMA((n,))]


def _swapped_shape(half):
    r, c = half.shape
    return _sds((r, 2 * c) if _by_lanes((r, 2 * c)) else (2, r, c), half.dtype)


def _swap_copies(srcs, dsts, send_sems, recv_sems, local_sems):
    x, y, c = _here()
    sibling = (x, y, 1 - c)
    starts, waits = [], []
    for i, (src, dst) in enumerate(zip(srcs, dsts)):
        if len(dst.shape) == 2:
            lanes = src.shape[1]
            mine, other = (dst.at[:, pl.ds(pl.multiple_of(k * lanes, 128), lanes)] for k in (c, 1 - c))
        else:
            mine, other = dst.at[c], dst.at[1 - c]
        own = pltpu.make_async_copy(src, mine, local_sems.at[i])
        push = _remote(src, mine, send_sems, recv_sems, i, sibling)
        starts += [own.start, push.start]
        waits += [_remote(other, other, send_sems, recv_sems, i, sibling).wait_recv, push.wait_send, own.wait]
    return starts, waits


def _swap_scratch(n):
    return [pltpu.SemaphoreType.DMA((n,)), pltpu.SemaphoreType.DMA((n,)), pltpu.SemaphoreType.DMA((n,))]


def _exchange_halves(halves, wholes):
    ns, nw = len(halves), len(wholes)

    def body(*refs):
        srcs, dsts = refs[:ns + nw], refs[ns + nw:2 * (ns + nw)]
        sems = refs[2 * (ns + nw):]
        starts, waits = _swap_copies(srcs[:ns], dsts[:ns], *sems[:3])
        more = _scatter_copies(srcs[ns:], dsts[ns:], *sems[3:], whole=nw)
        for call in starts + more[0] + waits + more[1]:
            call()

    return pl.pallas_call(
        body, in_specs=[_ANY] * (ns + nw), out_specs=[_ANY] * (ns + nw),
        out_shape=[_swapped_shape(h) for h in halves] + [_sds((8,) + a.shape, a.dtype) for a in wholes],
        scratch_shapes=_swap_scratch(ns) + _scatter_scratch(nw), name="exchange_halves",
    )(*halves, *wholes)


def _row_tile(rows, row_bytes, budget):
    tr = rows
    while tr * row_bytes > budget and tr % 16 == 0:
        tr //= 2
    return tr


def _sum_slots(parts, name):
    _, rows, cols = parts.shape
    tr = _row_tile(rows, 8 * cols * 4, 2 * 1024 * 1024)

    def body(p_ref, o_ref):
        acc = p_ref[0].astype(F32)
        for d in range(1, 8):
            acc = acc + p_ref[d].astype(F32)
        o_ref[...] = acc

    return pl.pallas_call(
        body, grid=(rows // tr,), in_specs=[pl.BlockSpec((8, tr, cols), lambda i: (0, i, 0))],
        out_specs=pl.BlockSpec((tr, cols), lambda i: (i, 0)), out_shape=_sds((rows, cols), F32),
        compiler_params=_params(("parallel",)), name=name,
    )(parts)


def _adam_update(w, g, m, v):
    m = ADAM_B1 * m + (1.0 - ADAM_B1) * g
    v = ADAM_B2 * v + (1.0 - ADAM_B2) * jnp.square(g)
    m_hat = m / (1.0 - ADAM_B1 ** ADAM_STEP)
    v_hat = v / (1.0 - ADAM_B2 ** ADAM_STEP)
    return -ADAM_LR * (m_hat / (jnp.sqrt(v_hat) + ADAM_EPS) + ADAM_WD * w), m, v


SMALL_ROWS = {"attn_norm_w": 0, "mlp_norm_w": 1, "q_lat_norm_w": 2, "kv_lat_norm_w": 3, "q_norm_w": 4, "k_norm_w": 5,
              "mla_out_norm_w": 6, "gdn_norm_w": 10, "a_log": 11, "dt_bias": 12}
LOSS_ROW = 13
SMALL_SHAPE = (16, 1024)


def _pack_small_partials(d_attn_nw, d_mlp_nw, d_ln, d_qk_nw, d_mix_nw, d_scal, conv_parts, sq):
    D = d_attn_nw.shape[1]

    def body(an_ref, mn_ref, ln_ref, qk_ref, mix_ref, sc_ref, cq_ref, ck_ref, cv_ref, sq_ref, a_ref, c_ref):
        a_ref[...] = jnp.zeros_like(a_ref)
        a_ref[0:1, :D] = an_ref[...]
        a_ref[1:2, :D] = mn_ref[...]
        a_ref[2:4, :LORA] = ln_ref[...]
        for row, base in ((4, 0), (5, 2)):
            rope = qk_ref[base + 1:base + 2, :]
            a_ref[row:row + 1, :QK_DIM] = jnp.concatenate(
                [qk_ref[base:base + 1, :], rope[:, :ROPE_HALF], rope[:, 2 * ROPE_HALF:3 * ROPE_HALF]], axis=1)
        a_ref[6:6 + HEADS, :HEAD_DIM] = mix_ref[0:HEADS, :]
        a_ref[10:11, :HEAD_DIM] = mix_ref[HEADS:HEADS + 1, :]
        a_ref[11:13, :128] = sc_ref[0:2, :]
        a_ref[LOSS_ROW:LOSS_ROW + 1, :128] = jnp.zeros((1, 128), F32) + jnp.sum(sq_ref[:, 0:1, 0:1]) * (0.5 / D)
        c_ref[...] = jnp.concatenate([cq_ref[...], ck_ref[...], cv_ref[...]], axis=1)

    return pl.pallas_call(
        body, out_shape=[_sds(SMALL_SHAPE, F32), _sds((CONV_TAPS, GQKV_W), F32)], name="pack_small_partials",
    )(d_attn_nw, d_mlp_nw, d_ln, d_qk_nw, d_mix_nw, d_scal, *conv_parts, sq)


def _adamw_small(parts, conv_parts, w, m, v):
    names = tuple(SMALL_ROWS) + ("conv_w",)
    cols = w["conv_w"].shape[2]

    def body(*refs):
        p_ref, c_ref = refs[:2]
        n = len(names)
        w_refs, m_refs, v_refs = (dict(zip(names, refs[2 + k * n:2 + (k + 1) * n])) for k in range(3))
        loss_ref = refs[2 + 3 * n]
        out = [dict(zip(names, refs[3 + (3 + k) * n:3 + (4 + k) * n])) for k in range(4)]
        acc_ref, cacc_ref = refs[3 + 7 * n:]
        acc, cacc = p_ref[0], c_ref[0]
        for d in range(1, 8):
            acc, cacc = acc + p_ref[d], cacc + c_ref[d]
        acc_ref[...] = acc
        cacc_ref[...] = cacc
        loss_ref[...] = acc_ref[LOSS_ROW:LOSS_ROW + 1, 0:1]
        chip = 2 * lax.axis_index("x") + lax.axis_index("y")
        for name in names:
            shape = w_refs[name].shape
            if name == "conv_w":
                g = sum(jnp.where(chip == s, cacc_ref[:, s * cols:(s + 1) * cols], 0.0) for s in range(4))[None]
            else:
                row = SMALL_ROWS[name]
                g = acc_ref[row:row + math.prod(shape[:-1]), 0:shape[-1]].reshape(shape)
            delta, new_m, new_v = _adam_update(w_refs[name][...], g, m_refs[name][...], v_refs[name][...])
            for ref, val in zip((o[name] for o in out), (g, delta, new_m, new_v)):
                ref[...] = val

    ins = [x[n] for x in (w, m, v) for n in names]
    shapes = [_sds(w[n].shape, F32) for n in names]
    outs = pl.pallas_call(
        body, out_shape=[_sds((1, 1), F32)] + shapes * 4,
        scratch_shapes=[pltpu.VMEM(parts.shape[1:], F32), pltpu.VMEM(conv_parts.shape[1:], F32)], name="adamw_small",
    )(parts, conv_parts, *ins)
    n = len(names)
    return (outs[0],) + tuple(dict(zip(names, outs[1 + k * n:1 + (k + 1) * n])) for k in range(4))


def _adamw(w, g, m, v, name):
    rows, cols = w.shape[0], w.shape[-1]
    if w.ndim == 3:
        tr = max(d for d in range(1, rows + 1) if rows % d == 0 and d * 8 * cols * 4 * 14 <= VMEM_LIMIT // 2)
    else:
        tr = _row_tile(rows, 7 * cols * 4, 4 * 1024 * 1024)

    def body(w_ref, g_ref, m_ref, v_ref, d_ref, mo_ref, vo_ref):
        d_ref[...], mo_ref[...], vo_ref[...] = _adam_update(w_ref[...], g_ref[...], m_ref[...], v_ref[...])

    block = (tr,) + w.shape[1:]
    spec = pl.BlockSpec(block, lambda i: (i,) + (0,) * (len(block) - 1))
    return pl.pallas_call(
        body, grid=(rows // tr,), in_specs=[spec] * 4, out_specs=[spec] * 3, out_shape=[_sds(w.shape, F32)] * 3,
        compiler_params=_params(("parallel",)), name=name,
    )(w, g, m, v)


def kernel(x, positions, attn_norm_w, w_in, q_lat_norm_w, w_uq, kv_lat_norm_w, w_ukv, q_norm_w, k_norm_w, mla_out_norm_w, conv_w, a_log, dt_bias, gdn_norm_w, w_out, mlp_norm_w, w_up, w_down, loss_target, m_attn_norm_w, m_w_in, m_q_lat_norm_w, m_w_uq, m_kv_lat_norm_w, m_w_ukv, m_q_norm_w, m_k_norm_w, m_mla_out_norm_w, m_conv_w, m_a_log, m_dt_bias, m_gdn_norm_w, m_w_out, m_mlp_norm_w, m_w_up, m_w_down, v_attn_norm_w, v_w_in, v_q_lat_norm_w, v_w_uq, v_kv_lat_norm_w, v_w_ukv, v_q_norm_w, v_k_norm_w, v_mla_out_norm_w, v_conv_w, v_a_log, v_dt_bias, v_gdn_norm_w, v_w_out, v_mlp_norm_w, v_w_up, v_w_down):
    w = dict(zip(WEIGHTS, (attn_norm_w, w_in, q_lat_norm_w, w_uq, kv_lat_norm_w, w_ukv, q_norm_w, k_norm_w, mla_out_norm_w, conv_w,
                           a_log, dt_bias, gdn_norm_w, w_out, mlp_norm_w, w_up, w_down)))
    m = dict(zip(WEIGHTS, (m_attn_norm_w, m_w_in, m_q_lat_norm_w, m_w_uq, m_kv_lat_norm_w, m_w_ukv, m_q_norm_w, m_k_norm_w,
                           m_mla_out_norm_w, m_conv_w, m_a_log, m_dt_bias, m_gdn_norm_w, m_w_out, m_mlp_norm_w, m_w_up, m_w_down)))
    v = dict(zip(WEIGHTS, (v_attn_norm_w, v_w_in, v_q_lat_norm_w, v_w_uq, v_kv_lat_norm_w, v_w_ukv, v_q_norm_w, v_k_norm_w,
                           v_mla_out_norm_w, v_conv_w, v_a_log, v_dt_bias, v_gdn_norm_w, v_w_out, v_mlp_norm_w, v_w_up, v_w_down)))
    B, S, D = x.shape
    T = B * S
    x2, pos, target = x.reshape(T, D), positions.reshape(T, 1), loss_target.reshape(T, D)
    seq = lambda a: a.reshape(B, S, a.shape[-1])
    tok = lambda a: a.reshape(T, a.shape[-1])
    local = {n: w[n][0] for n in SHARDED}

    g_in, g_uq, g_ukv, g_conv = _all_gather([jnp.swapaxes(w_in, 1, 2)[0].astype(BF16), local["w_uq"].astype(BF16),
                                             local["w_ukv"].astype(BF16), local["conv_w"]], "gather_first_weights")
    w_in_p = _widen_w_in_t(g_in.reshape(-1, D))
    w_mla = _stack_mla(_from_column_shards(g_uq), _from_column_shards(g_ukv))
    conv_full = _from_column_shards(g_conv)
    ln_w = jnp.concatenate([q_lat_norm_w, kv_lat_norm_w], axis=0)
    qk_nw = _qk_norm_rows(q_norm_w, k_norm_w)
    rope_rows = _rope_rows()
    scal = _rows8([jnp.pad(a_log, ((0, 0), (0, 128 - HEADS))), jnp.pad(dt_bias, ((0, 0), (0, 128 - HEADS)))])
    mix_nw = _rows8([mla_out_norm_w[0], gdn_norm_w])

    xn, lat, gqkv, gz, gab = _in_proj_fwd(x2, attn_norm_w, w_in_p)
    q, k, v_att = _mla_pre_fwd(lat, pos, ln_w, w_mla, qk_nw, rope_rows)
    ao, lse, g_down = _attn_fwd(seq(q), seq(k), seq(v_att), [local["w_down"].astype(BF16)])
    gq, gk, gv = _gdn_pre_fwd(seq(gqkv), conv_full)
    go, states, g_out, w_up_b = _gdn_chunk_fwd(gq, gk, gv, seq(gab), scal, [local["w_out"].astype(BF16), local["w_up"].astype(BF16)])
    w_out_b = g_out.reshape(-1, D)
    w_down_b = g_down.reshape(-1, D)
    mix, h2 = _mix_fwd(tok(ao), tok(go), gz, mix_nw, w_out_b, x2)
    hn, act, dy, sq = _mlp_fwd(h2, mlp_norm_w, w_up_b, w_down_b, target)

    dh, d_up, d_mlp_nw = _mlp_bwd(h2, mlp_norm_w, act, w_up_b, w_down_b, dy)
    p_down = _wgrad(act, dy, "wgrad_down").reshape(4, -1, D)
    p_up = _wgrad(hn, d_up, "wgrad_up", column_shards=4)
    d_ao, d_go, d_gz, d_mix_nw = _mix_bwd(tok(ao), tok(go), gz, mix_nw, w_out_b, dh)
    p_out = _wgrad(mix, dh, "wgrad_out").reshape(4, -1, D)
    d_gq, d_gk, d_gv, d_gab, d_scal, s_up, s_out = _gdn_chunk_bwd(gq, gk, gv, seq(gab), scal, states, seq(d_go), [p_up, p_out])
    early = ("w_up", "w_out", "w_down")
    dxq, dxk, dxv, dcq, dck, dcv, g_up, g_out = _gdn_pre_bwd(seq(gqkv), conv_full, d_gq, d_gk, d_gv,
                                                             [_sum_slots(s_up, "sum_w_up"), _sum_slots(s_out, "sum_w_out")])
    dq, dk, dv, s_down = _attn_bwd(seq(q), seq(k), seq(v_att), ao, lse, seq(d_ao), [p_down])
    d_lat, d_ln, d_w_mla, d_qk_nw, g_down = _mla_pre_bwd(lat, pos, ln_w, w_mla, qk_nw, rope_rows, tok(dq), tok(dk), tok(dv),
                                                         [_sum_slots(s_down, "sum_w_down")])
    early_grads = [g_up, g_out, g_down]
    d_pieces = [d_lat, tok(dxq), tok(dxk), tok(dxv), d_gz, tok(d_gab)]
    p_in = _narrow_w_in_t(_wgrad_pieces(d_pieces, xn, "wgrad_in")).reshape(4, -1, D)
    p_uq, p_ukv = (_column_shards(a).astype(BF16) for a in _unstack_mla(d_w_mla))
    grad_x2, d_attn_nw, s_in, s_uq, s_ukv = _in_proj_bwd(d_pieces, w_in_p, x2, attn_norm_w, dh, [p_in, p_uq, p_ukv])
    small_buf, conv_buf = _pack_small_partials(d_attn_nw, d_mlp_nw, d_ln, d_qk_nw, d_mix_nw, d_scal, (dcq, dck, dcv), sq)

    late = ("w_in", "w_uq", "w_ukv")
    *late_grads, s_small, s_conv = _exchange_halves([_sum_slots(s, "sum_" + n) for n, s in zip(late, (s_in, s_uq, s_ukv))],
                                                    [small_buf, conv_buf])
    names = early + late
    grad = {n: g.reshape(-1, g.shape[-1]) for n, g in zip(names, list(early_grads) + list(late_grads))}

    loss, g_small, delta, new_m, new_v = _adamw_small(s_small, s_conv, w, m, v)
    grad.update(g_small)
    for n in names:
        if n == "w_in":
            stored = lambda a: jnp.transpose(a, (2, 0, 1))
            outs = _adamw(stored(w[n]), grad[n][:, None, :], stored(m[n]), stored(v[n]), "adamw_" + n)
            grad[n], delta[n], new_m[n], new_v[n] = (jnp.transpose(a, (1, 2, 0)) for a in (grad[n][:, None, :], *outs))
        else:
            delta[n], new_m[n], new_v[n] = _adamw(local[n], grad[n], m[n][0], v[n][0], "adamw_" + n)
    def in_order(d):
        return [d[n].reshape(w[n].shape) for n in WEIGHTS]

    return (loss.reshape(()), grad_x2.reshape(B, S, D), *in_order(grad), *in_order(delta), *in_order(new_m), *in_order(new_v))
```

```python
import functools
import math

import jax
import jax.numpy as jnp
from jax import lax
from jax.experimental import pallas as pl
from jax.experimental.pallas import tpu as pltpu

F32 = jnp.float32
BF16 = jnp.bfloat16
MESH = pl.DeviceIdType.MESH

EPS = 1e-6
HEADS = 4
HEAD_DIM = 128
ROPE_DIM = 64
ROPE_HALF = 32
QK_DIM = 192
QK_PAD = 256
LORA = 256
CHUNK = 64
CONV_TAPS = 4
ROPE_THETA = 10000.0
ATTN_SCALE = QK_DIM ** -0.5

LAT_W = 640
GQKV_W = 3 * HEADS * HEAD_DIM
GZ_W = HEADS * HEAD_DIM
GAB_W = 128
PROJ_SPLITS = ((0, LAT_W), (LAT_W, LAT_W + GQKV_W), (LAT_W + GQKV_W, LAT_W + GQKV_W + GZ_W),
               (LAT_W + GQKV_W + GZ_W, LAT_W + GQKV_W + GZ_W + GAB_W))
PROJ_W = PROJ_SPLITS[-1][1]

ADAM_LR = 0.001
ADAM_B1 = 0.9
ADAM_B2 = 0.999
ADAM_EPS = 1e-08
ADAM_WD = 0.01
ADAM_STEP = 10

TOKEN_TILE = 512
MLP_TOKEN_TILE = 512
FF_TILE = 1024
ATTN_TILE = 512
ATTN_HEADS_PER_STEP = 2
WGRAD_OUT_BYTES = 8 * 1024 * 1024
VMEM_LIMIT = 48 * 1024 * 1024

SHARDED = ("w_in", "w_uq", "w_ukv", "conv_w", "w_out", "w_up", "w_down")
WEIGHTS = ("attn_norm_w", "w_in", "q_lat_norm_w", "w_uq", "kv_lat_norm_w", "w_ukv", "q_norm_w", "k_norm_w", "mla_out_norm_w",
           "conv_w", "a_log", "dt_bias", "gdn_norm_w", "w_out", "mlp_norm_w", "w_up", "w_down")


def _sds(shape, dtype):
    return jax.ShapeDtypeStruct(shape, dtype)


def _params(semantics):
    return pltpu.CompilerParams(dimension_semantics=semantics, vmem_limit_bytes=VMEM_LIMIT)


def _block(n):
    for b in (512, 256, 128):
        if n % b == 0:
            return b
    return n


def _dg(a, b, ca, cb, prec):
    lead = a.ndim - 2
    batch = (tuple(range(lead)),) * 2
    return lax.dot_general(a, b, (((ca + lead,), (cb + lead,)), batch), precision=prec, preferred_element_type=F32)


def _split_bf16(a):
    hi = a.astype(BF16)
    return hi, (a - hi.astype(F32)).astype(BF16)


def _dot_bf16(a, b, ca, cb):
    return _dg(a.astype(BF16), b.astype(BF16), ca, cb, None)


def _dot_bf16x3(a, b, ca, cb):
    a_hi, a_lo = _split_bf16(a)
    b_hi, b_lo = _split_bf16(b)
    lead = a.ndim - 2
    return _dg(jnp.concatenate([a_hi, a_hi, a_lo], axis=ca + lead), jnp.concatenate([b_hi, b_lo, b_hi], axis=cb + lead), ca, cb, None)


def _matmul_family(dot):
    def nn_raw(a, b):
        return dot(a, b, 1, 0)

    def nt_raw(a, b):
        return dot(a, b, 1, 1)

    def tn_raw(a, b):
        return dot(a, b, 0, 0)

    @jax.custom_vjp
    def nn(a, b):
        return nn_raw(a, b)

    nn.defvjp(lambda a, b: (nn_raw(a, b), (a, b)), lambda r, g: (nt_raw(g, r[1]), tn_raw(r[0], g)))

    @jax.custom_vjp
    def nt(a, b):
        return nt_raw(a, b)

    nt.defvjp(lambda a, b: (nt_raw(a, b), (a, b)), lambda r, g: (nn_raw(g, r[1]), tn_raw(g, r[0])))

    @jax.custom_vjp
    def tn(a, b):
        return tn_raw(a, b)

    tn.defvjp(lambda a, b: (tn_raw(a, b), (a, b)), lambda r, g: (nt_raw(r[1], g), nn_raw(r[0], g)))
    return nn, nt, tn


_bf_nn, _bf_nt, _bf_tn = _matmul_family(_dot_bf16)
_hi_nn, _hi_nt, _hi_tn = _matmul_family(_dot_bf16x3)


def _lower_powers(lmat):
    powers = []
    while 2 ** (len(powers) + 1) < lmat.shape[-1]:
        powers.append(_dot_bf16x3(powers[-1] if powers else lmat, powers[-1] if powers else lmat, 1, 0))
    return powers


@jax.custom_vjp
def _unit_lower_solve(lmat, rhs):
    return _unit_lower_solve_fwd(lmat, rhs)[0]


def _unit_lower_solve_fwd(lmat, rhs):
    powers = _lower_powers(lmat)
    x = rhs - _dot_bf16x3(lmat, rhs, 1, 0)
    for p in powers:
        x = x + _dot_bf16x3(p, x, 1, 0)
    return x, (lmat, powers, x)


def _unit_lower_solve_bwd(res, g):
    lmat, powers, x = res
    y = g - _dot_bf16x3(lmat, g, 0, 0)
    for p in powers:
        y = y + _dot_bf16x3(p, y, 0, 0)
    return -_dot_bf16x3(y, x, 1, 1), y


_unit_lower_solve.defvjp(_unit_lower_solve_fwd, _unit_lower_solve_bwd)


@jax.custom_vjp
def _lane_halves(x):
    n = x.shape[-1] // 2
    return x[..., :n], x[..., n:]


_lane_halves.defvjp(lambda x: (_lane_halves(x), None), lambda _, g: (jnp.concatenate(g, axis=-1),))


@jax.custom_vjp
def _row_halves(x):
    n = x.shape[-2] // 2
    return x[..., :n, :], x[..., n:, :]


_row_halves.defvjp(lambda x: (_row_halves(x), None), lambda _, g: (jnp.concatenate(g, axis=-2),))


@jax.custom_vjp
def _swap_halves(t):
    return pltpu.roll(t, 64, 1)


_swap_halves.defvjp(lambda t: (pltpu.roll(t, 64, 1), None), lambda _, g: (pltpu.roll(g, 64, 1),))


@functools.partial(jax.custom_vjp, nondiff_argnums=(2,))
def _shift_rows(x, keep, s):
    return pltpu.roll(x, s, 0) * keep


def _shift_rows_fwd(x, keep, s):
    return pltpu.roll(x, s, 0) * keep, keep


def _shift_rows_bwd(s, keep, g):
    return pltpu.roll(g * keep, keep.shape[0] - s, 0), jnp.zeros_like(keep)


_shift_rows.defvjp(_shift_rows_fwd, _shift_rows_bwd)


def _sigmoid(x):
    return 0.5 * jnp.tanh(0.5 * x) + 0.5


def _softplus(x):
    return jnp.maximum(x, 0.0) + jnp.log(1.0 + jnp.exp(jnp.minimum(x, -x)))


def _silu(x):
    return x * _sigmoid(x)


def _rms(x, w, n=None):
    n = x.shape[-1] if n is None else n
    r = lax.rsqrt(jnp.sum(x * x, axis=-1, keepdims=True) * (1.0 / n) + EPS)
    return x * r * w


def _rope(t, cos_f, sin_f):
    return t * cos_f + _swap_halves(t) * sin_f


def _rope_tables(pos_col, freq_row, sign_row):
    ang = pos_col.astype(F32) * freq_row
    return jnp.cos(ang), jnp.sin(ang) * sign_row


def _onehot_row(lane):
    return (lax.broadcasted_iota(jnp.int32, (1, 128), 1) == lane).astype(F32)


def _row_spec(tm, w):
    return pl.BlockSpec((tm, w), lambda i: (i, 0))


def _const_spec(shape):
    return pl.BlockSpec(shape, lambda *_: (0,) * len(shape))


def _in_proj_fwd(x2, w_an, w_in_p):
    T, D = x2.shape
    tm = min(TOKEN_TILE, T)

    def body(x_ref, wn_ref, w_ref, xn_ref, lat_ref, gqkv_ref, gz_ref, gab_ref):
        x = x_ref[...]
        r = lax.rsqrt(jnp.mean(x * x, axis=-1, keepdims=True) + EPS)
        xn = (x * r * wn_ref[...]).astype(BF16)
        xn_ref[...] = xn
        for ref, (a, b) in zip((lat_ref, gqkv_ref, gz_ref, gab_ref), PROJ_SPLITS):
            ref[...] = _dg(xn, w_ref[a:b, :], 1, 1, None)

    widths = [b - a for a, b in PROJ_SPLITS]
    return pl.pallas_call(
        body, grid=(T // tm,),
        in_specs=[_row_spec(tm, D), _const_spec((1, D)), _const_spec((PROJ_W, D))],
        out_specs=[_row_spec(tm, D)] + [_row_spec(tm, w) for w in widths],
        out_shape=[_sds((T, D), BF16)] + [_sds((T, w), F32) for w in widths],
        compiler_params=_params(("parallel",)), name="in_proj_fwd",
    )(x2, w_an, w_in_p)


def _in_proj_bwd(pieces, w_in_p, x2, w_an, dh, partials):
    T, D = x2.shape
    tm = min(TOKEN_TILE, T)
    widths = [p.shape[1] for p in pieces]
    starts = [sum(widths[:i]) for i in range(len(widths))]
    assert sum(widths) == PROJ_W
    npc, ns = len(pieces), len(partials)

    def body(*refs):
        piece_refs = refs[:npc]
        w_ref, x_ref, wn_ref, dh_ref = refs[npc:npc + 4]
        src_refs = refs[npc + 4:npc + 4 + ns]
        dx_ref, dwn_ref = refs[npc + 4 + ns:npc + 6 + ns]
        dst_refs = refs[npc + 6 + ns:npc + 6 + 2 * ns]
        sems = refs[npc + 6 + 2 * ns:]

        @pl.when(pl.program_id(0) == 0)
        def _():
            for start in _scatter_copies(src_refs, dst_refs, *sems)[0]:
                start()
            dwn_ref[...] = jnp.zeros_like(dwn_ref)

        dxn = jnp.zeros((tm, D), F32)
        for ref, a, width in zip(piece_refs, starts, widths):
            dxn += _dg(ref[...], w_ref[a:a + width, :], 1, 0, None)
        _, pull = jax.vjp(_rms, x_ref[...], wn_ref[...])
        dx, dwn = pull(dxn)
        dx_ref[...] = dx + dh_ref[...]
        dwn_ref[...] += dwn

        @pl.when(pl.program_id(0) == T // tm - 1)
        def _():
            for wait in _scatter_copies(src_refs, dst_refs, *sems)[1]:
                wait()

    return pl.pallas_call(
        body, grid=(T // tm,),
        in_specs=[_row_spec(tm, w) for w in widths] + [_const_spec((PROJ_W, D)), _row_spec(tm, D), _const_spec((1, D)),
                                                       _row_spec(tm, D)] + [_ANY] * ns,
        out_specs=[_row_spec(tm, D), _const_spec((1, D))] + [_ANY] * ns,
        out_shape=[_sds((T, D), F32), _sds((1, D), F32)] + [_scattered_shape(p) for p in partials],
        scratch_shapes=_scatter_scratch(ns),
        compiler_params=_params(("arbitrary",)), name="in_proj_bwd",
    )(*pieces, w_in_p, x2, w_an, dh, *partials)


def _wgrad_pieces(pieces, b, name):
    T, k2 = b.shape
    tt = min(TOKEN_TILE, T)
    widths = [p.shape[1] for p in pieces]
    starts = [sum(widths[:i]) for i in range(len(widths))]
    k1 = sum(widths)

    def body(*refs):
        piece_refs, (b_ref, o_ref, acc_ref) = refs[:len(pieces)], refs[len(pieces):]
        t = pl.program_id(0)

        @pl.when(t == 0)
        def _():
            acc_ref[...] = jnp.zeros_like(acc_ref)

        bt = b_ref[...].astype(BF16)
        for ref, r0, width in zip(piece_refs, starts, widths):
            acc_ref[r0:r0 + width, :] += jnp.dot(ref[...].T, bt, preferred_element_type=F32)

        @pl.when(t == T // tt - 1)
        def _():
            o_ref[...] = acc_ref[...].astype(o_ref.dtype)

    return pl.pallas_call(
        body, grid=(T // tt,),
        in_specs=[pl.BlockSpec((tt, w), lambda t: (t, 0)) for w in widths] + [pl.BlockSpec((tt, k2), lambda t: (t, 0))],
        out_specs=_const_spec((k1, k2)), out_shape=_sds((k1, k2), BF16), scratch_shapes=[pltpu.VMEM((k1, k2), F32)],
        compiler_params=_params(("arbitrary",)), name=name,
    )(*pieces, b)


def _wgrad(a, b, name, column_shards=1, out_dtype=BF16):
    T, k1 = a.shape
    k2 = b.shape[1]
    per_shard = k2 // column_shards
    tt = min(TOKEN_TILE, T)
    b1 = k1
    while b1 * k2 * 4 > WGRAD_OUT_BYTES and b1 % 256 == 0:
        b1 //= 2
    step = _block(per_shard)

    def body(a_ref, b_ref, o_ref, acc_ref):
        t = pl.program_id(1)

        @pl.when(t == 0)
        def _():
            acc_ref[...] = jnp.zeros_like(acc_ref)

        a_t = a_ref[...].astype(BF16).T
        for c0 in range(0, k2, step):
            part = jnp.dot(a_t, b_ref[:, c0:c0 + step].astype(BF16), preferred_element_type=F32)
            if column_shards == 1:
                acc_ref[:, c0:c0 + step] += part
            else:
                acc_ref[c0 // per_shard, :, c0 % per_shard:c0 % per_shard + step] += part

        @pl.when(t == T // tt - 1)
        def _():
            o_ref[...] = acc_ref[...].astype(o_ref.dtype)

    if column_shards == 1:
        block, out_spec, out_shape = (b1, k2), pl.BlockSpec((b1, k2), lambda i, t: (i, 0)), _sds((k1, k2), out_dtype)
    else:
        block = (column_shards, b1, per_shard)
        out_spec, out_shape = pl.BlockSpec(block, lambda i, t: (0, i, 0)), _sds((column_shards, k1, per_shard), out_dtype)
    return pl.pallas_call(
        body, grid=(k1 // b1, T // tt),
        in_specs=[pl.BlockSpec((tt, b1), lambda i, t: (t, i)), pl.BlockSpec((tt, k2), lambda i, t: (t, 0))],
        out_specs=out_spec, out_shape=out_shape, scratch_shapes=[pltpu.VMEM(block, F32)],
        compiler_params=_params(("parallel", "arbitrary")), name=name,
    )(a, b)


def _mla_pre_fn(q_lat, kv_lat, kpe, ln_q, ln_kv, w_list, qn_n, qn_p, kn_n, kn_p, cos_f, sin_f):
    qn = _rms(q_lat, ln_q)
    kvn = _rms(kv_lat, ln_kv)
    kp = _rope(_rms(kpe, kn_p, ROPE_DIM), cos_f, sin_f)
    outs = []
    for h in range(HEADS):
        outs.append(_rms(_bf_nn(qn, w_list[h]), qn_n))
        outs.append(_rope(_rms(_bf_nn(qn, w_list[HEADS + h]), qn_p, ROPE_DIM), cos_f, sin_f))
        outs.append(_rms(_bf_nn(kvn, w_list[2 * HEADS + h]), kn_n))
        outs.append(_bf_nn(kvn, w_list[3 * HEADS + h]))
    return tuple(outs) + (kp,)


def _mla_pre_operands(lat_ref, pos_ref, ln_ref, w_ref, nw_ref, rope_ref):
    cos_f, sin_f = _rope_tables(pos_ref[...], rope_ref[0:1, :], rope_ref[1:2, :])
    diff = (lat_ref[:, 0:LORA], lat_ref[:, LORA:2 * LORA], lat_ref[:, 2 * LORA:LAT_W], ln_ref[0:1, :], ln_ref[1:2, :],
            [w_ref[i].astype(F32) for i in range(4 * HEADS)], nw_ref[0:1, :], nw_ref[1:2, :], nw_ref[2:3, :], nw_ref[3:4, :])
    return diff, cos_f, sin_f


def _mla_pre_fwd(lat, pos, ln_w, w_mla, nw, rope_rows):
    T = lat.shape[0]
    tm = min(TOKEN_TILE, T)

    def body(lat_ref, pos_ref, ln_ref, w_ref, nw_ref, rope_ref, q_ref, k_ref, v_ref):
        diff, cos_f, sin_f = _mla_pre_operands(lat_ref, pos_ref, ln_ref, w_ref, nw_ref, rope_ref)
        outs = _mla_pre_fn(*diff, cos_f, sin_f)
        kp = outs[-1].astype(BF16)
        for h in range(HEADS):
            q_n, q_p, k_n, v = outs[4 * h:4 * h + 4]
            q_ref[:, h * QK_PAD:h * QK_PAD + HEAD_DIM] = q_n.astype(BF16)
            q_ref[:, h * QK_PAD + HEAD_DIM:(h + 1) * QK_PAD] = q_p.astype(BF16)
            k_ref[:, h * QK_PAD:h * QK_PAD + HEAD_DIM] = k_n.astype(BF16)
            k_ref[:, h * QK_PAD + HEAD_DIM:(h + 1) * QK_PAD] = kp
            v_ref[:, h * HEAD_DIM:(h + 1) * HEAD_DIM] = v.astype(BF16)

    return pl.pallas_call(
        body, grid=(T // tm,),
        in_specs=[_row_spec(tm, LAT_W), _row_spec(tm, 1), _const_spec((2, LORA)), _const_spec((4 * HEADS, LORA, 128)),
                  _const_spec((8, 128)), _const_spec((8, 128))],
        out_specs=[_row_spec(tm, HEADS * QK_PAD), _row_spec(tm, HEADS * QK_PAD), _row_spec(tm, HEADS * HEAD_DIM)],
        out_shape=[_sds((T, HEADS * QK_PAD), BF16), _sds((T, HEADS * QK_PAD), BF16), _sds((T, HEADS * HEAD_DIM), BF16)],
        compiler_params=_params(("parallel",)), name="mla_pre_fwd",
    )(lat, pos, ln_w, w_mla, nw, rope_rows)


def _mla_pre_bwd(lat, pos, ln_w, w_mla, nw, rope_rows, dq, dk, dv, halves):
    T = lat.shape[0]
    tm = min(TOKEN_TILE, T)
    ns = len(halves)

    def body(*refs):
        lat_ref, pos_ref, ln_ref, w_ref, nw_ref, rope_ref, dq_ref, dk_ref, dv_ref = refs[:9]
        src_refs = refs[9:9 + ns]
        dlat_ref, dln_ref, dw_ref, dnw_ref = refs[9 + ns:13 + ns]
        dst_refs = refs[13 + ns:13 + 2 * ns]
        sems = refs[13 + 2 * ns:]

        @pl.when(pl.program_id(0) == 0)
        def _():
            for start in _swap_copies(src_refs, dst_refs, *sems)[0]:
                start()
            dln_ref[...] = jnp.zeros_like(dln_ref)
            dw_ref[...] = jnp.zeros_like(dw_ref)
            dnw_ref[...] = jnp.zeros_like(dnw_ref)

        diff, cos_f, sin_f = _mla_pre_operands(lat_ref, pos_ref, ln_ref, w_ref, nw_ref, rope_ref)
        _, pull = jax.vjp(lambda *a: _mla_pre_fn(*a, cos_f, sin_f), *diff)
        cts = []
        d_kp = jnp.zeros((tm, 128), F32)
        for h in range(HEADS):
            cts.append(dq_ref[:, h * QK_PAD:h * QK_PAD + HEAD_DIM])
            cts.append(dq_ref[:, h * QK_PAD + HEAD_DIM:(h + 1) * QK_PAD])
            cts.append(dk_ref[:, h * QK_PAD:h * QK_PAD + HEAD_DIM])
            cts.append(dv_ref[:, h * HEAD_DIM:(h + 1) * HEAD_DIM])
            d_kp += dk_ref[:, h * QK_PAD + HEAD_DIM:(h + 1) * QK_PAD]
        d_ql, d_kvl, d_kpe, d_lnq, d_lnkv, d_w, d_qn_n, d_qn_p, d_kn_n, d_kn_p = pull(tuple(cts) + (d_kp,))
        dlat_ref[:, 0:LORA] = d_ql.astype(BF16)
        dlat_ref[:, LORA:2 * LORA] = d_kvl.astype(BF16)
        dlat_ref[:, 2 * LORA:LAT_W] = d_kpe.astype(BF16)
        dln_ref[0:1, :] += d_lnq
        dln_ref[1:2, :] += d_lnkv
        for i in range(4 * HEADS):
            dw_ref[i] += d_w[i]
        for i, d in enumerate((d_qn_n, d_qn_p, d_kn_n, d_kn_p)):
            dnw_ref[i:i + 1, :] += d

        @pl.when(pl.program_id(0) == T // tm - 1)
        def _():
            for wait in _swap_copies(src_refs, dst_refs, *sems)[1]:
                wait()

    return pl.pallas_call(
        body, grid=(T // tm,),
        in_specs=[_row_spec(tm, LAT_W), _row_spec(tm, 1), _const_spec((2, LORA)), _const_spec((4 * HEADS, LORA, 128)),
                  _const_spec((8, 128)), _const_spec((8, 128)),
                  _row_spec(tm, HEADS * QK_PAD), _row_spec(tm, HEADS * QK_PAD), _row_spec(tm, HEADS * HEAD_DIM)] + [_ANY] * ns,
        out_specs=[_row_spec(tm, LAT_W), _const_spec((2, LORA)), _const_spec((4 * HEADS, LORA, 128)), _const_spec((8, 128))]
                  + [_ANY] * ns,
        out_shape=[_sds((T, LAT_W), BF16), _sds((2, LORA), F32), _sds((4 * HEADS, LORA, 128), F32), _sds((8, 128), F32)]
                  + [_swapped_shape(h) for h in halves],
        scratch_shapes=_swap_scratch(ns),
        compiler_params=_params(("arbitrary",)), name="mla_pre_bwd",
    )(lat, pos, ln_w, w_mla, nw, rope_rows, dq, dk, dv, *halves)


def _causal_mask(i, j, tq, tk):
    row = i * tq + lax.broadcasted_iota(jnp.int32, (tq, tk), 0)
    col = j * tk + lax.broadcasted_iota(jnp.int32, (tq, tk), 1)
    return col <= row


def _attn_fwd(q, k, v, shards):
    B, S, _ = q.shape
    t = min(ATTN_TILE, S)
    nq = S // t
    ns = len(shards)

    hp = ATTN_HEADS_PER_STEP
    qk = lambda h: slice(h * QK_PAD, (h + 1) * QK_PAD)
    vd = lambda h: slice(h * HEAD_DIM, (h + 1) * HEAD_DIM)

    def body(*refs):
        q_ref, k_ref, v_ref = refs[:3]
        src_refs = refs[3:3 + ns]
        o_ref, lse_ref = refs[3 + ns:5 + ns]
        dst_refs = refs[5 + ns:5 + 2 * ns]
        sems = refs[5 + 2 * ns:]
        b, g, i = pl.program_id(0), pl.program_id(1), pl.program_id(2)
        qb = [q_ref[0, :, qk(h)] for h in range(hp)]

        step_no = (b * (HEADS // hp) + g) * nq + i
        for phase, at in enumerate((0, (3 * B * (HEADS // hp) * nq) // 4)):
            @pl.when(step_no == at)
            def _(phase=phase):
                for call in _gather_copies(src_refs, dst_refs, *sems)[phase]:
                    call()

        def step(j, carry, diagonal):
            rows = pl.ds(pl.multiple_of(j * t, t), t)
            s = [_dg(qb[h], k_ref[0, rows, qk(h)], 1, 1, None) * ATTN_SCALE for h in range(hp)]
            if diagonal:
                keep = _causal_mask(0, 0, t, t)
                s = [jnp.where(keep, x, -1e30) for x in s]
            m_new = [jnp.maximum(carry[h][0], jnp.max(s[h], axis=-1, keepdims=True)) for h in range(hp)]
            p = [jnp.exp(s[h] - m_new[h]) for h in range(hp)]
            alpha = [jnp.exp(carry[h][0] - m_new[h]) for h in range(hp)]
            l = [alpha[h] * carry[h][1] + jnp.sum(p[h], axis=-1, keepdims=True) for h in range(hp)]
            pv = [jnp.dot(p[h].astype(BF16), v_ref[0, rows, vd(h)], preferred_element_type=F32) for h in range(hp)]
            return tuple((m_new[h], l[h], alpha[h] * carry[h][2] + pv[h]) for h in range(hp))

        init = tuple((jnp.full((t, 1), -1e30, F32), jnp.zeros((t, 1), F32), jnp.zeros((t, HEAD_DIM), F32)) for _ in range(hp))
        below = lax.fori_loop(0, i, lambda j, carry: step(j, carry, False), init)
        for h, (m, l, acc) in enumerate(step(i, below, True)):
            o_ref[0, :, vd(h)] = acc / l
            lse_ref[0, h] = m + jnp.log(l)

        @pl.when((b == B - 1) & (g == HEADS // hp - 1) & (i == nq - 1))
        def _():
            for wait in _gather_copies(src_refs, dst_refs, *sems)[2]:
                wait()

    return pl.pallas_call(
        body, grid=(B, HEADS // hp, nq),
        in_specs=[pl.BlockSpec((1, t, hp * QK_PAD), lambda b, g, i: (b, i, g)),
                  pl.BlockSpec((1, S, hp * QK_PAD), lambda b, g, i: (b, 0, g)),
                  pl.BlockSpec((1, S, hp * HEAD_DIM), lambda b, g, i: (b, 0, g))] + [_ANY] * ns,
        out_specs=[pl.BlockSpec((1, t, hp * HEAD_DIM), lambda b, g, i: (b, i, g)),
                   pl.BlockSpec((1, hp, t, 1), lambda b, g, i: (b, g, i, 0))] + [_ANY] * ns,
        out_shape=[_sds((B, S, HEADS * HEAD_DIM), F32), _sds((B, HEADS, S, 1), F32)] + [_sds((4,) + s.shape, s.dtype) for s in shards],
        scratch_shapes=_gather_scratch(ns),
        compiler_params=_params(("arbitrary", "arbitrary", "arbitrary")), name="attn_fwd",
    )(q, k, v, *shards)


def _attn_bwd(q, k, v, o, lse, do, partials):
    B, S, _ = q.shape
    t = min(ATTN_TILE, S)
    nq = S // t
    ns = len(partials)

    hp = ATTN_HEADS_PER_STEP
    qk = lambda h: slice(h * QK_PAD, (h + 1) * QK_PAD)
    vd = lambda h: slice(h * HEAD_DIM, (h + 1) * HEAD_DIM)
    heads = range(hp)

    def body(*refs):
        q_ref, k_ref, v_ref, o_ref, lse_ref, do_ref = refs[:6]
        src_refs = refs[6:6 + ns]
        dq_ref, dk_ref, dv_ref = refs[6 + ns:9 + ns]
        dst_refs = refs[9 + ns:9 + 2 * ns]
        dsum_ref, send_sems, recv_sems, local_sems = refs[9 + 2 * ns:]
        b, g, j = pl.program_id(0), pl.program_id(1), pl.program_id(2)

        @pl.when((b == 0) & (g == 0) & (j == 0))
        def _():
            for start in _scatter_copies(src_refs, dst_refs, send_sems, recv_sems, local_sems)[0]:
                start()

        @pl.when(j == 0)
        def _():
            dq_ref[...] = jnp.zeros_like(dq_ref)
            for h in heads:
                dsum_ref[h] = jnp.sum(do_ref[0, :, vd(h)] * o_ref[0, :, vd(h)], axis=-1, keepdims=True)

        kb = [k_ref[0, :, qk(h)] for h in heads]
        vb = [v_ref[0, :, vd(h)] for h in heads]

        def step(i, carry, diagonal):
            rows = pl.ds(pl.multiple_of(i * t, t), t)
            qb = [q_ref[0, rows, qk(h)] for h in heads]
            dob = [do_ref[0, rows, vd(h)].astype(BF16) for h in heads]
            s = [_dg(qb[h], kb[h], 1, 1, None) * ATTN_SCALE for h in heads]
            p = [jnp.exp(s[h] - lse_ref[0, h, rows, :]) for h in heads]
            if diagonal:
                keep = _causal_mask(0, 0, t, t)
                p = [jnp.where(keep, x, 0.0) for x in p]
            dp = [_dg(dob[h], vb[h], 1, 1, None) for h in heads]
            dv = [carry[h][1] + _dg(p[h].astype(BF16), dob[h], 0, 0, None) for h in heads]
            ds = [(p[h] * (dp[h] - dsum_ref[h, rows, :]) * ATTN_SCALE).astype(BF16) for h in heads]
            for h in heads:
                dq_ref[0, rows, qk(h)] += jnp.dot(ds[h], kb[h], preferred_element_type=F32)
            return tuple((carry[h][0] + _dg(ds[h], qb[h], 0, 0, None), dv[h]) for h in heads)

        zeros = tuple((jnp.zeros((t, QK_PAD), F32), jnp.zeros((t, HEAD_DIM), F32)) for _ in heads)
        on_diagonal = step(j, zeros, True)
        done = lax.fori_loop(j + 1, nq, lambda i, carry: step(i, carry, False), on_diagonal)
        for h, (dk, dv) in enumerate(done):
            dk_ref[0, :, qk(h)] = dk
            dv_ref[0, :, vd(h)] = dv

        @pl.when((b == B - 1) & (g == HEADS // hp - 1) & (j == nq - 1))
        def _():
            for wait in _scatter_copies(src_refs, dst_refs, send_sems, recv_sems, local_sems)[1]:
                wait()

    return pl.pallas_call(
        body, grid=(B, HEADS // hp, nq),
        in_specs=[pl.BlockSpec((1, S, hp * QK_PAD), lambda b, g, j: (b, 0, g)),
                  pl.BlockSpec((1, t, hp * QK_PAD), lambda b, g, j: (b, j, g)),
                  pl.BlockSpec((1, t, hp * HEAD_DIM), lambda b, g, j: (b, j, g)),
                  pl.BlockSpec((1, S, hp * HEAD_DIM), lambda b, g, j: (b, 0, g)),
                  pl.BlockSpec((1, hp, S, 1), lambda b, g, j: (b, g, 0, 0)),
                  pl.BlockSpec((1, S, hp * HEAD_DIM), lambda b, g, j: (b, 0, g))] + [_ANY] * ns,
        out_specs=[pl.BlockSpec((1, S, hp * QK_PAD), lambda b, g, j: (b, 0, g)),
                   pl.BlockSpec((1, t, hp * QK_PAD), lambda b, g, j: (b, j, g)),
                   pl.BlockSpec((1, t, hp * HEAD_DIM), lambda b, g, j: (b, j, g))] + [_ANY] * ns,
        out_shape=[_sds((B, S, HEADS * QK_PAD), F32), _sds((B, S, HEADS * QK_PAD), F32), _sds((B, S, HEADS * HEAD_DIM), F32)]
                  + [_scattered_shape(p) for p in partials],
        scratch_shapes=[pltpu.VMEM((hp, S, 1), F32)] + _scatter_scratch(ns),
        compiler_params=_params(("arbitrary", "arbitrary", "arbitrary")), name="attn_bwd",
    )(q, k, v, o, lse, do, *partials)


def _gdn_pre_fn(xq, xk, xv, wq, wk, wv, keeps):
    def conv_silu(x, w):
        acc = x * w[3]
        for s in (1, 2, 3):
            acc = acc + _shift_rows(x, keeps[s - 1], s) * w[3 - s]
        return _silu(acc)

    def l2(x):
        return x * lax.rsqrt(jnp.sum(x * x, axis=-1, keepdims=True) + EPS)

    return l2(conv_silu(xq, wq)) * (HEAD_DIM ** -0.5), l2(conv_silu(xk, wk)), conv_silu(xv, wv)


def _gdn_pre_specs(S):
    x_specs = [pl.BlockSpec((1, S, HEAD_DIM), lambda h, b, g=g: (b, 0, g * HEADS + h)) for g in range(3)]
    w_specs = [pl.BlockSpec((CONV_TAPS, HEAD_DIM), lambda h, b, g=g: (0, g * HEADS + h)) for g in range(3)]
    out_spec = pl.BlockSpec((1, S, HEAD_DIM), lambda h, b: (b, 0, h))
    return x_specs, w_specs, out_spec


def _row_keeps(S):
    t = lax.broadcasted_iota(jnp.int32, (S, HEAD_DIM), 0)
    return [(t >= s).astype(F32) for s in (1, 2, 3)]


def _gdn_pre_fwd(gqkv, conv_w):
    B, S, _ = gqkv.shape
    x_specs, w_specs, out_spec = _gdn_pre_specs(S)

    def body(xq_ref, xk_ref, xv_ref, wq_ref, wk_ref, wv_ref, q_ref, k_ref, v_ref):
        taps = [[w[i:i + 1, :] for i in range(CONV_TAPS)] for w in (wq_ref, wk_ref, wv_ref)]
        q, k, v = _gdn_pre_fn(xq_ref[0], xk_ref[0], xv_ref[0], *taps, _row_keeps(S))
        q_ref[0], k_ref[0], v_ref[0] = q, k, v

    return pl.pallas_call(
        body, grid=(HEADS, B), in_specs=x_specs + w_specs, out_specs=[out_spec] * 3,
        out_shape=[_sds((B, S, HEADS * HEAD_DIM), F32)] * 3,
        compiler_params=_params(("parallel", "parallel")), name="gdn_pre_fwd",
    )(gqkv, gqkv, gqkv, conv_w, conv_w, conv_w)


def _gdn_pre_bwd(gqkv, conv_w, dq, dk, dv, halves):
    B, S, _ = gqkv.shape
    x_specs, w_specs, out_spec = _gdn_pre_specs(S)
    dw_spec = pl.BlockSpec((CONV_TAPS, HEAD_DIM), lambda h, b: (0, h))
    ns = len(halves)

    def body(*refs):
        xq_ref, xk_ref, xv_ref, wq_ref, wk_ref, wv_ref, dq_ref, dk_ref, dv_ref = refs[:9]
        src_refs = refs[9:9 + ns]
        dxq_ref, dxk_ref, dxv_ref, dwq_ref, dwk_ref, dwv_ref = refs[9 + ns:15 + ns]
        dst_refs = refs[15 + ns:15 + 2 * ns]
        sems = refs[15 + 2 * ns:]
        first = (pl.program_id(0) == 0) & (pl.program_id(1) == 0)
        last = (pl.program_id(0) == HEADS - 1) & (pl.program_id(1) == B - 1)

        @pl.when(first)
        def _():
            for start in _swap_copies(src_refs, dst_refs, *sems)[0]:
                start()

        @pl.when(pl.program_id(1) == 0)
        def _():
            for r in (dwq_ref, dwk_ref, dwv_ref):
                r[...] = jnp.zeros_like(r)

        taps = [[w[i:i + 1, :] for i in range(CONV_TAPS)] for w in (wq_ref, wk_ref, wv_ref)]
        keeps = _row_keeps(S)
        _, pull = jax.vjp(lambda *a: _gdn_pre_fn(*a, keeps), xq_ref[0], xk_ref[0], xv_ref[0], *taps)
        dxq, dxk, dxv, dwq, dwk, dwv = pull((dq_ref[0], dk_ref[0], dv_ref[0]))
        dxq_ref[0], dxk_ref[0], dxv_ref[0] = dxq.astype(BF16), dxk.astype(BF16), dxv.astype(BF16)
        for ref, dw in ((dwq_ref, dwq), (dwk_ref, dwk), (dwv_ref, dwv)):
            for i in range(CONV_TAPS):
                ref[i:i + 1, :] += dw[i]

        @pl.when(last)
        def _():
            for wait in _swap_copies(src_refs, dst_refs, *sems)[1]:
                wait()

    hw = HEADS * HEAD_DIM
    return pl.pallas_call(
        body, grid=(HEADS, B), in_specs=x_specs + w_specs + [out_spec] * 3 + [_ANY] * ns,
        out_specs=[out_spec] * 3 + [dw_spec] * 3 + [_ANY] * ns,
        out_shape=[_sds((B, S, hw), BF16)] * 3 + [_sds((CONV_TAPS, hw), F32)] * 3 + [_swapped_shape(h) for h in halves],
        scratch_shapes=_swap_scratch(ns),
        compiler_params=_params(("arbitrary", "arbitrary")), name="gdn_pre_bwd",
    )(gqkv, gqkv, gqkv, conv_w, conv_w, conv_w, dq, dk, dv, *halves)


def _chunk_masks():
    i = lax.broadcasted_iota(jnp.int32, (CHUNK, CHUNK), 0)
    j = lax.broadcasted_iota(jnp.int32, (CHUNK, CHUNK), 1)
    lower, after = (j <= i).astype(F32), (j > i).astype(F32)
    return {"le": lower, "le_gt": jnp.concatenate([lower, after], axis=0), "strict": (j < i).astype(F32)}


def _gdn_chunk_fn(groups, masks):
    lane = lax.broadcasted_iota(jnp.int32, (groups, 1, 128), 2)
    head = lax.broadcasted_iota(jnp.int32, (groups, 1, 128), 0) % HEADS
    pick_a, pick_b = (lane == head).astype(F32), (lane == head + HEADS).astype(F32)
    lower, lower_after, strict = (jnp.broadcast_to(masks[n], (groups,) + masks[n].shape) for n in ("le", "le_gt", "strict"))
    ones_row = jnp.ones((1, 1, HEAD_DIM), F32)

    def f(q, k, v, gab, a_row, dt_row, state):
        ga = jnp.sum(gab * pick_a, axis=2, keepdims=True)
        gb = jnp.sum(gab * pick_b, axis=2, keepdims=True)
        a_log = jnp.sum(a_row * pick_a, axis=2, keepdims=True)
        dt_bias = jnp.sum(dt_row * pick_a, axis=2, keepdims=True)
        beta = _sigmoid(gb)
        g = -jnp.exp(a_log) * _softplus(ga + dt_bias)
        g_wide = g * ones_row
        cum, rest = _row_halves(_hi_nn(lower_after, g_wide))
        total = jnp.sum(g_wide, axis=1, keepdims=True)
        diff = _hi_nn(lower, g * strict)
        decay = lower * jnp.exp(diff)
        e_cum = jnp.exp(cum)
        kk, qk = _row_halves(_bf_nt(jnp.concatenate([k, q], axis=1), k))
        lmat = strict * (beta * kk * decay)
        u, w = _lane_halves(_unit_lower_solve(lmat, jnp.concatenate([v * beta, k * (beta * e_cum)], axis=2)))
        w_state, q_state = _row_halves(_bf_nn(jnp.concatenate([w, q * e_cum], axis=1), state))
        v_new = u - w_state
        o = q_state + _bf_nn(qk * decay, v_new)
        new_state = state * jnp.exp(total) + _bf_tn(k * jnp.exp(rest), v_new)
        return o, new_state

    return f


def _gdn_chunk_fwd(q, k, v, gab, scal, shards):
    B, S, W = q.shape
    N = S // CHUNK
    ns = len(shards)

    def body(*refs):
        q_ref, k_ref, v_ref, gab_ref, sc_ref = refs[:5]
        src_refs = refs[5:5 + ns]
        o_ref, st_ref = refs[5 + ns:7 + ns]
        dst_refs = refs[7 + ns:7 + 2 * ns]
        state_ref, send_sems, recv_sems, local_sems = refs[7 + 2 * ns:]
        n = pl.program_id(0)

        @pl.when(n == 0)
        def _():
            for start in _gather_copies(src_refs, dst_refs, send_sems, recv_sems, local_sems)[0]:
                start()
            state_ref[...] = jnp.zeros_like(state_ref)

        @pl.when(n == (2 * N) // 3)
        def _():
            for pass_on in _gather_copies(src_refs, dst_refs, send_sems, recv_sems, local_sems)[1]:
                pass_on()

        groups = [(b, h) for b in range(B) for h in range(HEADS)]
        gather = lambda ref: jnp.stack([ref[b, :, h * HEAD_DIM:(h + 1) * HEAD_DIM] for b, h in groups])
        state = state_ref[...]
        for i, (b, h) in enumerate(groups):
            st_ref[b, 0, h] = state[i]
        o, new_state = _gdn_chunk_fn(len(groups), _chunk_masks())(
            gather(q_ref), gather(k_ref), gather(v_ref), jnp.stack([gab_ref[b] for b, _ in groups]), sc_ref[0:1, :], sc_ref[1:2, :], state)
        for i, (b, h) in enumerate(groups):
            o_ref[b, :, h * HEAD_DIM:(h + 1) * HEAD_DIM] = o[i]
        state_ref[...] = new_state

        @pl.when(n == N - 1)
        def _():
            for wait in _gather_copies(src_refs, dst_refs, send_sems, recv_sems, local_sems)[2]:
                wait()

    seq = pl.BlockSpec((B, CHUNK, W), lambda n: (0, n, 0))
    return pl.pallas_call(
        body, grid=(N,),
        in_specs=[seq, seq, seq, pl.BlockSpec((B, CHUNK, GAB_W), lambda n: (0, n, 0)), _const_spec((8, 128))] + [_ANY] * ns,
        out_specs=[seq, pl.BlockSpec((B, 1, HEADS, HEAD_DIM, HEAD_DIM), lambda n: (0, n, 0, 0, 0))] + [_ANY] * ns,
        out_shape=[_sds((B, S, W), F32), _sds((B, N, HEADS, HEAD_DIM, HEAD_DIM), F32)] + [_sds((4,) + s.shape, s.dtype) for s in shards],
        scratch_shapes=[pltpu.VMEM((B * HEADS, HEAD_DIM, HEAD_DIM), F32)] + _gather_scratch(ns),
        compiler_params=_params(("arbitrary",)), name="gdn_chunk_fwd",
    )(q, k, v, gab, scal, *shards)


def _gdn_chunk_bwd(q, k, v, gab, scal, states, do, partials):
    B, S, W = q.shape
    N = S // CHUNK
    ns = len(partials)

    def body(*refs):
        q_ref, k_ref, v_ref, gab_ref, sc_ref, st_ref, do_ref = refs[:7]
        src_refs = refs[7:7 + ns]
        dq_ref, dk_ref, dv_ref, dgab_ref, dsc_ref = refs[7 + ns:12 + ns]
        dst_refs = refs[12 + ns:12 + 2 * ns]
        dstate_ref, send_sems, recv_sems, local_sems = refs[12 + 2 * ns:]
        n = pl.program_id(0)

        @pl.when(n == 0)
        def _():
            for start in _scatter_copies(src_refs, dst_refs, send_sems, recv_sems, local_sems)[0]:
                start()
            dstate_ref[...] = jnp.zeros_like(dstate_ref)
            dsc_ref[...] = jnp.zeros_like(dsc_ref)

        groups = [(b, h) for b in range(B) for h in range(HEADS)]
        gather = lambda ref: jnp.stack([ref[b, :, h * HEAD_DIM:(h + 1) * HEAD_DIM] for b, h in groups])
        _, pull = jax.vjp(_gdn_chunk_fn(len(groups), _chunk_masks()), gather(q_ref), gather(k_ref), gather(v_ref),
                          jnp.stack([gab_ref[b] for b, _ in groups]), sc_ref[0:1, :], sc_ref[1:2, :],
                          jnp.stack([st_ref[b, 0, h] for b, h in groups]))
        dq, dk, dv, dg, d_a, d_dt, dstate = pull((gather(do_ref), dstate_ref[...]))
        for i, (b, h) in enumerate(groups):
            lanes = slice(h * HEAD_DIM, (h + 1) * HEAD_DIM)
            dq_ref[b, :, lanes] = dq[i]
            dk_ref[b, :, lanes] = dk[i]
            dv_ref[b, :, lanes] = dv[i]
        for b in range(B):
            dgab_ref[b] = sum(dg[b * HEADS + h] for h in range(HEADS)).astype(BF16)
        dstate_ref[...] = dstate
        dsc_ref[0:1, :] += d_a
        dsc_ref[1:2, :] += d_dt

        @pl.when(n == N - 1)
        def _():
            for wait in _scatter_copies(src_refs, dst_refs, send_sems, recv_sems, local_sems)[1]:
                wait()

    seq = pl.BlockSpec((B, CHUNK, W), lambda n: (0, N - 1 - n, 0))
    gab_spec = pl.BlockSpec((B, CHUNK, GAB_W), lambda n: (0, N - 1 - n, 0))
    return pl.pallas_call(
        body, grid=(N,),
        in_specs=[seq, seq, seq, gab_spec, _const_spec((8, 128)),
                  pl.BlockSpec((B, 1, HEADS, HEAD_DIM, HEAD_DIM), lambda n: (0, N - 1 - n, 0, 0, 0)), seq] + [_ANY] * ns,
        out_specs=[seq, seq, seq, gab_spec, _const_spec((8, 128))] + [_ANY] * ns,
        out_shape=[_sds((B, S, W), F32)] * 3 + [_sds((B, S, GAB_W), BF16), _sds((8, 128), F32)] + [_scattered_shape(p) for p in partials],
        scratch_shapes=[pltpu.VMEM((B * HEADS, HEAD_DIM, HEAD_DIM), F32)] + _scatter_scratch(ns),
        compiler_params=_params(("arbitrary",)), name="gdn_chunk_bwd",
    )(q, k, v, gab, scal, states, do, *partials)


def _mix_fn(ao, go, gz, w_mla, w_gdn):
    return tuple(_rms(ao[h], w_mla[h]) for h in range(HEADS)) + tuple(_rms(go[h], w_gdn) * _silu(gz[h]) for h in range(HEADS))


def _mix_operands(ao_ref, go_ref, gz_ref, nw_ref):
    blocks = lambda ref: [ref[:, h * HEAD_DIM:(h + 1) * HEAD_DIM] for h in range(HEADS)]
    return blocks(ao_ref), blocks(go_ref), blocks(gz_ref), [nw_ref[h:h + 1, :] for h in range(HEADS)], nw_ref[HEADS:HEADS + 1, :]


def _mix_fwd(ao, go, gz, nw, w_out, x2):
    T, D = x2.shape
    tm = min(TOKEN_TILE, T)
    MW = 2 * HEADS * HEAD_DIM

    def body(ao_ref, go_ref, gz_ref, nw_ref, w_ref, x_ref, mix_ref, h_ref):
        outs = _mix_fn(*_mix_operands(ao_ref, go_ref, gz_ref, nw_ref))
        for i, piece in enumerate(outs):
            mix_ref[:, i * HEAD_DIM:(i + 1) * HEAD_DIM] = piece.astype(BF16)
        h_ref[...] = x_ref[...] + jnp.dot(mix_ref[...], w_ref[...], preferred_element_type=F32)

    half = HEADS * HEAD_DIM
    return pl.pallas_call(
        body, grid=(T // tm,),
        in_specs=[_row_spec(tm, half), _row_spec(tm, half), _row_spec(tm, half), _const_spec((8, 128)), _const_spec((MW, D)),
                  _row_spec(tm, D)],
        out_specs=[_row_spec(tm, MW), _row_spec(tm, D)],
        out_shape=[_sds((T, MW), BF16), _sds((T, D), F32)],
        compiler_params=_params(("parallel",)), name="mix_fwd",
    )(ao, go, gz, nw, w_out, x2)


def _mix_bwd(ao, go, gz, nw, w_out, dh):
    T, D = dh.shape
    tm = min(TOKEN_TILE, T)
    MW = 2 * HEADS * HEAD_DIM
    half = HEADS * HEAD_DIM

    def body(ao_ref, go_ref, gz_ref, nw_ref, w_ref, dh_ref, dao_ref, dgo_ref, dgz_ref, dnw_ref):
        @pl.when(pl.program_id(0) == 0)
        def _():
            dnw_ref[...] = jnp.zeros_like(dnw_ref)

        d_mix = _dg(dh_ref[...].astype(BF16), w_ref[...], 1, 1, None)
        cts = tuple(d_mix[:, i * HEAD_DIM:(i + 1) * HEAD_DIM] for i in range(2 * HEADS))
        _, pull = jax.vjp(_mix_fn, *_mix_operands(ao_ref, go_ref, gz_ref, nw_ref))
        d_ao, d_go, d_gz, d_wm, d_wg = pull(cts)
        for h in range(HEADS):
            lanes = slice(h * HEAD_DIM, (h + 1) * HEAD_DIM)
            dao_ref[:, lanes] = d_ao[h]
            dgo_ref[:, lanes] = d_go[h]
            dgz_ref[:, lanes] = d_gz[h].astype(BF16)
            dnw_ref[h:h + 1, :] += d_wm[h]
        dnw_ref[HEADS:HEADS + 1, :] += d_wg

    return pl.pallas_call(
        body, grid=(T // tm,),
        in_specs=[_row_spec(tm, half), _row_spec(tm, half), _row_spec(tm, half), _const_spec((8, 128)), _const_spec((MW, D)),
                  _row_spec(tm, D)],
        out_specs=[_row_spec(tm, half)] * 3 + [_const_spec((8, 128))],
        out_shape=[_sds((T, half), F32)] * 2 + [_sds((T, half), BF16), _sds((8, 128), F32)],
        compiler_params=_params(("arbitrary",)), name="mix_bwd",
    )(ao, go, gz, nw, w_out, dh)


def _up_spec(w_up, tf):
    per_shard = w_up.shape[2] // tf
    return pl.BlockSpec((None, w_up.shape[1], tf), lambda i, j: (j // per_shard, 0, j % per_shard))


def _mlp_fwd(h2, w_mn, w_up, w_down, target):
    T, D = h2.shape
    FF = w_down.shape[0]
    tm, tf = min(MLP_TOKEN_TILE, T), min(FF_TILE, w_up.shape[2])
    nf = FF // tf

    def body(h_ref, wn_ref, wu_ref, wd_ref, t_ref, hn_ref, act_ref, dy_ref, sq_ref, acc_ref):
        j = pl.program_id(1)

        @pl.when(j == 0)
        def _():
            hn_ref[...] = _rms(h_ref[...], wn_ref[...]).astype(BF16)
            acc_ref[...] = jnp.zeros_like(acc_ref)

        up = jnp.dot(hn_ref[...], wu_ref[...], preferred_element_type=F32)
        act = jnp.square(jnp.maximum(up, 0.0)).astype(BF16)
        act_ref[...] = act
        acc_ref[...] += jnp.dot(act, wd_ref[...], preferred_element_type=F32)

        @pl.when(j == nf - 1)
        def _():
            err = h_ref[...] + acc_ref[...] - t_ref[...]
            dy_ref[...] = err * (1.0 / D)
            sq_ref[...] = jnp.zeros_like(sq_ref) + jnp.sum(err * err)

    tok = lambda w: pl.BlockSpec((tm, w), lambda i, j: (i, 0))
    return pl.pallas_call(
        body, grid=(T // tm, nf),
        in_specs=[tok(D), _const_spec((1, D)), _up_spec(w_up, tf), pl.BlockSpec((tf, D), lambda i, j: (j, 0)), tok(D)],
        out_specs=[tok(D), pl.BlockSpec((tm, tf), lambda i, j: (i, j)), tok(D), pl.BlockSpec((1, 8, 128), lambda i, j: (i, 0, 0))],
        out_shape=[_sds((T, D), BF16), _sds((T, FF), BF16), _sds((T, D), F32), _sds((T // tm, 8, 128), F32)],
        scratch_shapes=[pltpu.VMEM((tm, D), F32)],
        compiler_params=_params(("parallel", "arbitrary")), name="mlp_fwd",
    )(h2, w_mn, w_up, w_down, target)


def _mlp_bwd(h2, w_mn, act, w_up, w_down, dy):
    T, D = h2.shape
    FF = w_down.shape[0]
    tm, tf = min(MLP_TOKEN_TILE, T), min(FF_TILE, w_up.shape[2])
    nf = FF // tf

    def body(h_ref, wn_ref, act_ref, wu_ref, wd_ref, dy_ref, dh_ref, dup_ref, dwn_ref, acc_ref, dyb_ref):
        i, j = pl.program_id(0), pl.program_id(1)

        @pl.when((i == 0) & (j == 0))
        def _():
            dwn_ref[...] = jnp.zeros_like(dwn_ref)

        @pl.when(j == 0)
        def _():
            acc_ref[...] = jnp.zeros_like(acc_ref)
            dyb_ref[...] = dy_ref[...].astype(BF16)

        r = jnp.sqrt(act_ref[...].astype(F32))
        d_act = _dg(dyb_ref[...], wd_ref[...], 1, 1, None)
        d_up = (d_act * (2.0 * r)).astype(BF16)
        dup_ref[...] = d_up
        acc_ref[...] += _dg(d_up, wu_ref[...], 1, 1, None)

        @pl.when(j == nf - 1)
        def _():
            _, pull = jax.vjp(_rms, h_ref[...], wn_ref[...])
            dh, dwn = pull(acc_ref[...])
            dh_ref[...] = dh + dy_ref[...]
            dwn_ref[...] += dwn

    tok = lambda w: pl.BlockSpec((tm, w), lambda i, j: (i, 0))
    ff = pl.BlockSpec((tm, tf), lambda i, j: (i, j))
    return pl.pallas_call(
        body, grid=(T // tm, nf),
        in_specs=[tok(D), _const_spec((1, D)), ff, _up_spec(w_up, tf), pl.BlockSpec((tf, D), lambda i, j: (j, 0)), tok(D)],
        out_specs=[tok(D), ff, _const_spec((1, D))],
        out_shape=[_sds((T, D), F32), _sds((T, FF), BF16), _sds((1, D), F32)],
        scratch_shapes=[pltpu.VMEM((tm, D), F32), pltpu.VMEM((tm, D), BF16)],
        compiler_params=_params(("arbitrary", "arbitrary")), name="mlp_bwd",
    )(h2, w_mn, act, w_up, w_down, dy)


def _rope_pad(a):
    z = jnp.zeros(a.shape[:-1] + (ROPE_HALF,), a.dtype)
    return jnp.concatenate([a[..., :ROPE_HALF], z, a[..., ROPE_HALF:], z], axis=-1)


def _rope_unpad(a):
    return jnp.concatenate([a[..., :ROPE_HALF], a[..., 2 * ROPE_HALF:3 * ROPE_HALF]], axis=-1)


_G0 = 2 * LORA + ROPE_DIM
W_IN_COLS = _G0 + GQKV_W + GZ_W + 2 * HEADS


def _widen_w_in_t(w_t):
    z = jnp.zeros((ROPE_HALF, w_t.shape[1]), w_t.dtype)
    pad = jnp.zeros((GAB_W - 2 * HEADS, w_t.shape[1]), w_t.dtype)
    return jnp.concatenate([w_t[:2 * LORA + ROPE_HALF], z, w_t[2 * LORA + ROPE_HALF:_G0], z, w_t[_G0:], pad], axis=0)


def _narrow_w_in_t(w_t):
    return jnp.concatenate([w_t[:2 * LORA + ROPE_HALF], w_t[2 * LORA + 2 * ROPE_HALF:2 * LORA + 3 * ROPE_HALF],
                            w_t[LAT_W:LAT_W + W_IN_COLS - _G0]], axis=0)


def _stack_mla(w_uq, w_ukv):
    uq = w_uq.reshape(LORA, HEADS, QK_DIM)
    ukv = w_ukv.reshape(LORA, HEADS, 2 * HEAD_DIM)
    parts = [uq[:, :, :HEAD_DIM], _rope_pad(uq[:, :, HEAD_DIM:]), ukv[:, :, :HEAD_DIM], ukv[:, :, HEAD_DIM:]]
    return jnp.concatenate([p.transpose(1, 0, 2) for p in parts], axis=0)


def _unstack_mla(w):
    p = [w[i * HEADS:(i + 1) * HEADS].transpose(1, 0, 2) for i in range(4)]
    uq = jnp.concatenate([p[0], _rope_unpad(p[1])], axis=-1).reshape(LORA, HEADS * QK_DIM)
    ukv = jnp.concatenate([p[2], p[3]], axis=-1).reshape(LORA, HEADS * 2 * HEAD_DIM)
    return uq, ukv


def _rows8(rows):
    a = jnp.concatenate(rows, axis=0)
    return jnp.pad(a, ((0, 8 - a.shape[0]), (0, 0)))


def _qk_norm_rows(q_norm_w, k_norm_w):
    return _rows8([q_norm_w[:, :HEAD_DIM], _rope_pad(q_norm_w[:, HEAD_DIM:]), k_norm_w[:, :HEAD_DIM], _rope_pad(k_norm_w[:, HEAD_DIM:])])


def _rope_rows():
    inv_freq = ROPE_THETA ** (-jnp.arange(ROPE_HALF, dtype=F32) / ROPE_HALF)
    z = jnp.zeros((ROPE_HALF,), F32)
    freq = jnp.concatenate([inv_freq, z, inv_freq, z])
    sign = jnp.concatenate([-jnp.ones((ROPE_HALF,), F32), z, jnp.ones((ROPE_HALF,), F32), z])
    return _rows8([freq[None], sign[None]])


def _column_shards(a):
    return a.reshape(a.shape[0], 4, a.shape[1] // 4).transpose(1, 0, 2)


def _from_column_shards(a):
    return a.transpose(1, 0, 2).reshape(a.shape[1], 4 * a.shape[2])


_ANY = pl.BlockSpec(memory_space=pl.ANY)
_OTHER_CHIPS = ((1, 0), (0, 1), (1, 1))


def _here():
    return lax.axis_index("x"), lax.axis_index("y"), lax.axis_index("c")


def _flip(v, bit):
    return 1 - v if bit else v


def _remote(src, dst, send_sems, recv_sems, k, to):
    return pltpu.make_async_remote_copy(src_ref=src, dst_ref=dst, send_sem=send_sems.at[k], recv_sem=recv_sems.at[k],
                                        device_id=to, device_id_type=MESH)


def _half_of(ref, k, shape):
    r, c = shape
    if (r // 2) % 16 == 0:
        return ref.at[pl.ds(pl.multiple_of(k * (r // 2), 16), r // 2)]
    if (c // 2) % 128 == 0:
        return ref.at[:, pl.ds(pl.multiple_of(k * (c // 2), 128), c // 2)]
    return None


def _gather_copies(srcs, dsts, send_sems, recv_sems, local_sems):
    x, y, c = _here()
    slot, sibling, n = 2 * x + y, (x, y, 1 - c), len(srcs)
    starts, passes, waits = [], [], []
    for i, (src, dst) in enumerate(zip(srcs, dsts)):
        own = pltpu.make_async_copy(src, dst.at[slot], local_sems.at[i])
        starts.append(own.start)
        waits.append(own.wait)
        halves = _half_of(src, c, src.shape) is not None
        for j, (fx, fy) in enumerate(_OTHER_CHIPS):
            cx, cy = _flip(x, fx), _flip(y, fy)
            there = dst.at[2 * cx + cy]
            if halves:
                push = _remote(_half_of(src, c, src.shape), _half_of(dst.at[slot], c, src.shape), send_sems, recv_sems, 3 * i + j, (cx, cy, c))
                landed, other = _half_of(there, c, src.shape), _half_of(there, 1 - c, src.shape)
                onward = _remote(landed, landed, send_sems, recv_sems, 3 * n + 3 * i + j, sibling)
                passes += [_remote(landed, landed, send_sems, recv_sems, 3 * i + j, (cx, cy, c)).wait_recv, onward.start]
                waits += [_remote(other, other, send_sems, recv_sems, 3 * n + 3 * i + j, sibling).wait_recv, onward.wait_send]
            else:
                push = _remote(src, dst.at[slot], send_sems, recv_sems, 3 * i + j, (cx, cy, c))
                waits.append(_remote(there, there, send_sems, recv_sems, 3 * i + j, (cx, cy, c)).wait_recv)
            starts.append(push.start)
            waits.append(push.wait_send)
    return starts, passes, waits


def _gather_scratch(n):
    return [pltpu.SemaphoreType.DMA((6 * n,)), pltpu.SemaphoreType.DMA((6 * n,)), pltpu.SemaphoreType.DMA((n,))]


def _all_gather(shards, name):
    ns = len(shards)

    def body(*refs):
        starts, passes, waits = _gather_copies(refs[:ns], refs[ns:2 * ns], *refs[2 * ns:])
        for call in starts + passes + waits:
            call()

    return pl.pallas_call(
        body, in_specs=[_ANY] * ns, out_specs=[_ANY] * ns, out_shape=[_sds((4,) + s.shape, s.dtype) for s in shards],
        scratch_shapes=_gather_scratch(ns), name=name,
    )(*shards)


def _by_lanes(shape):
    return (shape[-2] // 2) % 16 != 0


def _scattered_shape(p):
    r, c = p.shape[1:]
    return _sds((8, r, c // 2) if _by_lanes(p.shape) else (8, r // 2, c), p.dtype)


def _scatter_copies(srcs, dsts, send_sems, recv_sems, local_sems, whole=0):
    x, y, c = _here()
    me = 4 * x + 2 * y + c
    starts, waits = [], []
    for i, (src, dst) in enumerate(zip(srcs, dsts)):
        def piece(px, py, pc, src=src, entire=i >= len(srcs) - whole):
            if entire:
                return src
            if _by_lanes(src.shape):
                half = src.shape[2] // 2
                return src.at[2 * px + py, :, pl.ds(pl.multiple_of(pc * half, 128), half)]
            half = src.shape[1] // 2
            return src.at[2 * px + py, pl.ds(pl.multiple_of(pc * half, 16), half)]

        own = pltpu.make_async_copy(piece(x, y, c), dst.at[me], local_sems.at[i])
        starts.append(own.start)
        waits.append(own.wait)
        for k in range(1, 8):
            px, py, pc = _flip(x, k & 4), _flip(y, k & 2), _flip(c, k & 1)
            push = _remote(piece(px, py, pc), dst.at[me], send_sems, recv_sems, 7 * i + k - 1, (px, py, pc))
            landed = dst.at[4 * px + 2 * py + pc]
            starts.append(push.start)
            waits += [_remote(landed, landed, send_sems, recv_sems, 7 * i + k - 1, (px, py, pc)).wait_recv, push.wait_send]
    return starts, waits


def _scatter_scratch(n):
    return [pltpu.SemaphoreType.DMA((7 * n,)), pltpu.SemaphoreType.DMA((7 * n,)), pltpu.SemaphoreType.DMA((n,))]


def _swapped_shape(half):
    r, c = half.shape
    return _sds((r, 2 * c) if _by_lanes((r, 2 * c)) else (2, r, c), half.dtype)


def _swap_copies(srcs, dsts, send_sems, recv_sems, local_sems):
    x, y, c = _here()
    sibling = (x, y, 1 - c)
    starts, waits = [], []
    for i, (src, dst) in enumerate(zip(srcs, dsts)):
        if len(dst.shape) == 2:
            lanes = src.shape[1]
            mine, other = (dst.at[:, pl.ds(pl.multiple_of(k * lanes, 128), lanes)] for k in (c, 1 - c))
        else:
            mine, other = dst.at[c], dst.at[1 - c]
        own = pltpu.make_async_copy(src, mine, local_sems.at[i])
        push = _remote(src, mine, send_sems, recv_sems, i, sibling)
        starts += [own.start, push.start]
        waits += [_remote(other, other, send_sems, recv_sems, i, sibling).wait_recv, push.wait_send, own.wait]
    return starts, waits


def _swap_scratch(n):
    return [pltpu.SemaphoreType.DMA((n,)), pltpu.SemaphoreType.DMA((n,)), pltpu.SemaphoreType.DMA((n,))]


def _exchange_halves(halves, wholes):
    ns, nw = len(halves), len(wholes)

    def body(*refs):
        srcs, dsts = refs[:ns + nw], refs[ns + nw:2 * (ns + nw)]
        sems = refs[2 * (ns + nw):]
        starts, waits = _swap_copies(srcs[:ns], dsts[:ns], *sems[:3])
        more = _scatter_copies(srcs[ns:], dsts[ns:], *sems[3:], whole=nw)
        for call in starts + more[0] + waits + more[1]:
            call()

    return pl.pallas_call(
        body, in_specs=[_ANY] * (ns + nw), out_specs=[_ANY] * (ns + nw),
        out_shape=[_swapped_shape(h) for h in halves] + [_sds((8,) + a.shape, a.dtype) for a in wholes],
        scratch_shapes=_swap_scratch(ns) + _scatter_scratch(nw), name="exchange_halves",
    )(*halves, *wholes)


def _row_tile(rows, row_bytes, budget):
    tr = rows
    while tr * row_bytes > budget and tr % 16 == 0:
        tr //= 2
    return tr


def _sum_slots(parts, name):
    _, rows, cols = parts.shape
    tr = _row_tile(rows, 8 * cols * 4, 2 * 1024 * 1024)

    def body(p_ref, o_ref):
        acc = p_ref[0].astype(F32)
        for d in range(1, 8):
            acc = acc + p_ref[d].astype(F32)
        o_ref[...] = acc

    return pl.pallas_call(
        body, grid=(rows // tr,), in_specs=[pl.BlockSpec((8, tr, cols), lambda i: (0, i, 0))],
        out_specs=pl.BlockSpec((tr, cols), lambda i: (i, 0)), out_shape=_sds((rows, cols), F32),
        compiler_params=_params(("parallel",)), name=name,
    )(parts)


def _adam_update(w, g, m, v):
    m = ADAM_B1 * m + (1.0 - ADAM_B1) * g
    v = ADAM_B2 * v + (1.0 - ADAM_B2) * jnp.square(g)
    m_hat = m / (1.0 - ADAM_B1 ** ADAM_STEP)
    v_hat = v / (1.0 - ADAM_B2 ** ADAM_STEP)
    return -ADAM_LR * (m_hat / (jnp.sqrt(v_hat) + ADAM_EPS) + ADAM_WD * w), m, v


SMALL_ROWS = {"attn_norm_w": 0, "mlp_norm_w": 1, "q_lat_norm_w": 2, "kv_lat_norm_w": 3, "q_norm_w": 4, "k_norm_w": 5,
              "mla_out_norm_w": 6, "gdn_norm_w": 10, "a_log": 11, "dt_bias": 12}
LOSS_ROW = 13
SMALL_SHAPE = (16, 1024)


def _pack_small_partials(d_attn_nw, d_mlp_nw, d_ln, d_qk_nw, d_mix_nw, d_scal, conv_parts, sq):
    D = d_attn_nw.shape[1]

    def body(an_ref, mn_ref, ln_ref, qk_ref, mix_ref, sc_ref, cq_ref, ck_ref, cv_ref, sq_ref, a_ref, c_ref):
        a_ref[...] = jnp.zeros_like(a_ref)
        a_ref[0:1, :D] = an_ref[...]
        a_ref[1:2, :D] = mn_ref[...]
        a_ref[2:4, :LORA] = ln_ref[...]
        for row, base in ((4, 0), (5, 2)):
            rope = qk_ref[base + 1:base + 2, :]
            a_ref[row:row + 1, :QK_DIM] = jnp.concatenate(
                [qk_ref[base:base + 1, :], rope[:, :ROPE_HALF], rope[:, 2 * ROPE_HALF:3 * ROPE_HALF]], axis=1)
        a_ref[6:6 + HEADS, :HEAD_DIM] = mix_ref[0:HEADS, :]
        a_ref[10:11, :HEAD_DIM] = mix_ref[HEADS:HEADS + 1, :]
        a_ref[11:13, :128] = sc_ref[0:2, :]
        a_ref[LOSS_ROW:LOSS_ROW + 1, :128] = jnp.zeros((1, 128), F32) + jnp.sum(sq_ref[:, 0:1, 0:1]) * (0.5 / D)
        c_ref[...] = jnp.concatenate([cq_ref[...], ck_ref[...], cv_ref[...]], axis=1)

    return pl.pallas_call(
        body, out_shape=[_sds(SMALL_SHAPE, F32), _sds((CONV_TAPS, GQKV_W), F32)], name="pack_small_partials",
    )(d_attn_nw, d_mlp_nw, d_ln, d_qk_nw, d_mix_nw, d_scal, *conv_parts, sq)


def _adamw_small(parts, conv_parts, w, m, v):
    names = tuple(SMALL_ROWS) + ("conv_w",)
    cols = w["conv_w"].shape[2]

    def body(*refs):
        p_ref, c_ref = refs[:2]
        n = len(names)
        w_refs, m_refs, v_refs = (dict(zip(names, refs[2 + k * n:2 + (k + 1) * n])) for k in range(3))
        loss_ref = refs[2 + 3 * n]
        out = [dict(zip(names, refs[3 + (3 + k) * n:3 + (4 + k) * n])) for k in range(4)]
        acc_ref, cacc_ref = refs[3 + 7 * n:]
        acc, cacc = p_ref[0], c_ref[0]
        for d in range(1, 8):
            acc, cacc = acc + p_ref[d], cacc + c_ref[d]
        acc_ref[...] = acc
        cacc_ref[...] = cacc
        loss_ref[...] = acc_ref[LOSS_ROW:LOSS_ROW + 1, 0:1]
        chip = 2 * lax.axis_index("x") + lax.axis_index("y")
        for name in names:
            shape = w_refs[name].shape
            if name == "conv_w":
                g = sum(jnp.where(chip == s, cacc_ref[:, s * cols:(s + 1) * cols], 0.0) for s in range(4))[None]
            else:
                row = SMALL_ROWS[name]
                g = acc_ref[row:row + math.prod(shape[:-1]), 0:shape[-1]].reshape(shape)
            delta, new_m, new_v = _adam_update(w_refs[name][...], g, m_refs[name][...], v_refs[name][...])
            for ref, val in zip((o[name] for o in out), (g, delta, new_m, new_v)):
                ref[...] = val

    ins = [x[n] for x in (w, m, v) for n in names]
    shapes = [_sds(w[n].shape, F32) for n in names]
    outs = pl.pallas_call(
        body, out_shape=[_sds((1, 1), F32)] + shapes * 4,
        scratch_shapes=[pltpu.VMEM(parts.shape[1:], F32), pltpu.VMEM(conv_parts.shape[1:], F32)], name="adamw_small",
    )(parts, conv_parts, *ins)
    n = len(names)
    return (outs[0],) + tuple(dict(zip(names, outs[1 + k * n:1 + (k + 1) * n])) for k in range(4))


def _adamw(w, g, m, v, name):
    rows, cols = w.shape[0], w.shape[-1]
    if w.ndim == 3:
        tr = max(d for d in range(1, rows + 1) if rows % d == 0 and d * 8 * cols * 4 * 14 <= VMEM_LIMIT // 2)
    else:
        tr = _row_tile(rows, 7 * cols * 4, 4 * 1024 * 1024)

    def body(w_ref, g_ref, m_ref, v_ref, d_ref, mo_ref, vo_ref):
        d_ref[...], mo_ref[...], vo_ref[...] = _adam_update(w_ref[...], g_ref[...], m_ref[...], v_ref[...])

    block = (tr,) + w.shape[1:]
    spec = pl.BlockSpec(block, lambda i: (i,) + (0,) * (len(block) - 1))
    return pl.pallas_call(
        body, grid=(rows // tr,), in_specs=[spec] * 4, out_specs=[spec] * 3, out_shape=[_sds(w.shape, F32)] * 3,
        compiler_params=_params(("parallel",)), name=name,
    )(w, g, m, v)


def kernel(x, positions, attn_norm_w, w_in, q_lat_norm_w, w_uq, kv_lat_norm_w, w_ukv, q_norm_w, k_norm_w, mla_out_norm_w, conv_w, a_log, dt_bias, gdn_norm_w, w_out, mlp_norm_w, w_up, w_down, loss_target, m_attn_norm_w, m_w_in, m_q_lat_norm_w, m_w_uq, m_kv_lat_norm_w, m_w_ukv, m_q_norm_w, m_k_norm_w, m_mla_out_norm_w, m_conv_w, m_a_log, m_dt_bias, m_gdn_norm_w, m_w_out, m_mlp_norm_w, m_w_up, m_w_down, v_attn_norm_w, v_w_in, v_q_lat_norm_w, v_w_uq, v_kv_lat_norm_w, v_w_ukv, v_q_norm_w, v_k_norm_w, v_mla_out_norm_w, v_conv_w, v_a_log, v_dt_bias, v_gdn_norm_w, v_w_out, v_mlp_norm_w, v_w_up, v_w_down):
    w = dict(zip(WEIGHTS, (attn_norm_w, w_in, q_lat_norm_w, w_uq, kv_lat_norm_w, w_ukv, q_norm_w, k_norm_w, mla_out_norm_w, conv_w,
                           a_log, dt_bias, gdn_norm_w, w_out, mlp_norm_w, w_up, w_down)))
    m = dict(zip(WEIGHTS, (m_attn_norm_w, m_w_in, m_q_lat_norm_w, m_w_uq, m_kv_lat_norm_w, m_w_ukv, m_q_norm_w, m_k_norm_w,
                           m_mla_out_norm_w, m_conv_w, m_a_log, m_dt_bias, m_gdn_norm_w, m_w_out, m_mlp_norm_w, m_w_up, m_w_down)))
    v = dict(zip(WEIGHTS, (v_attn_norm_w, v_w_in, v_q_lat_norm_w, v_w_uq, v_kv_lat_norm_w, v_w_ukv, v_q_norm_w, v_k_norm_w,
                           v_mla_out_norm_w, v_conv_w, v_a_log, v_dt_bias, v_gdn_norm_w, v_w_out, v_mlp_norm_w, v_w_up, v_w_down)))
    B, S, D = x.shape
    T = B * S
    x2, pos, target = x.reshape(T, D), positions.reshape(T, 1), loss_target.reshape(T, D)
    seq = lambda a: a.reshape(B, S, a.shape[-1])
    tok = lambda a: a.reshape(T, a.shape[-1])
    local = {n: w[n][0] for n in SHARDED}

    g_in, g_uq, g_ukv, g_conv = _all_gather([jnp.swapaxes(w_in, 1, 2)[0].astype(BF16), local["w_uq"].astype(BF16),
                                             local["w_ukv"].astype(BF16), local["conv_w"]], "gather_first_weights")
    w_in_p = _widen_w_in_t(g_in.reshape(-1, D))
    w_mla = _stack_mla(_from_column_shards(g_uq), _from_column_shards(g_ukv))
    conv_full = _from_column_shards(g_conv)
    ln_w = jnp.concatenate([q_lat_norm_w, kv_lat_norm_w], axis=0)
    qk_nw = _qk_norm_rows(q_norm_w, k_norm_w)
    rope_rows = _rope_rows()
    scal = _rows8([jnp.pad(a_log, ((0, 0), (0, 128 - HEADS))), jnp.pad(dt_bias, ((0, 0), (0, 128 - HEADS)))])
    mix_nw = _rows8([mla_out_norm_w[0], gdn_norm_w])

    xn, lat, gqkv, gz, gab = _in_proj_fwd(x2, attn_norm_w, w_in_p)
    q, k, v_att = _mla_pre_fwd(lat, pos, ln_w, w_mla, qk_nw, rope_rows)
    ao, lse, g_down = _attn_fwd(seq(q), seq(k), seq(v_att), [local["w_down"].astype(BF16)])
    gq, gk, gv = _gdn_pre_fwd(seq(gqkv), conv_full)
    go, states, g_out, w_up_b = _gdn_chunk_fwd(gq, gk, gv, seq(gab), scal, [local["w_out"].astype(BF16), local["w_up"].astype(BF16)])
    w_out_b = g_out.reshape(-1, D)
    w_down_b = g_down.reshape(-1, D)
    mix, h2 = _mix_fwd(tok(ao), tok(go), gz, mix_nw, w_out_b, x2)
    hn, act, dy, sq = _mlp_fwd(h2, mlp_norm_w, w_up_b, w_down_b, target)

    dh, d_up, d_mlp_nw = _mlp_bwd(h2, mlp_norm_w, act, w_up_b, w_down_b, dy)
    p_down = _wgrad(act, dy, "wgrad_down").reshape(4, -1, D)
    p_up = _wgrad(hn, d_up, "wgrad_up", column_shards=4)
    d_ao, d_go, d_gz, d_mix_nw = _mix_bwd(tok(ao), tok(go), gz, mix_nw, w_out_b, dh)
    p_out = _wgrad(mix, dh, "wgrad_out").reshape(4, -1, D)
    d_gq, d_gk, d_gv, d_gab, d_scal, s_up, s_out = _gdn_chunk_bwd(gq, gk, gv, seq(gab), scal, states, seq(d_go), [p_up, p_out])
    early = ("w_up", "w_out", "w_down")
    dxq, dxk, dxv, dcq, dck, dcv, g_up, g_out = _gdn_pre_bwd(seq(gqkv), conv_full, d_gq, d_gk, d_gv,
                                                             [_sum_slots(s_up, "sum_w_up"), _sum_slots(s_out, "sum_w_out")])
    dq, dk, dv, s_down = _attn_bwd(seq(q), seq(k), seq(v_att), ao, lse, seq(d_ao), [p_down])
    d_lat, d_ln, d_w_mla, d_qk_nw, g_down = _mla_pre_bwd(lat, pos, ln_w, w_mla, qk_nw, rope_rows, tok(dq), tok(dk), tok(dv),
                                                         [_sum_slots(s_down, "sum_w_down")])
    early_grads = [g_up, g_out, g_down]
    d_pieces = [d_lat, tok(dxq), tok(dxk), tok(dxv), d_gz, tok(d_gab)]
    p_in = _narrow_w_in_t(_wgrad_pieces(d_pieces, xn, "wgrad_in")).reshape(4, -1, D)
    p_uq, p_ukv = (_column_shards(a).astype(BF16) for a in _unstack_mla(d_w_mla))
    grad_x2, d_attn_nw, s_in, s_uq, s_ukv = _in_proj_bwd(d_pieces, w_in_p, x2, attn_norm_w, dh, [p_in, p_uq, p_ukv])
    small_buf, conv_buf = _pack_small_partials(d_attn_nw, d_mlp_nw, d_ln, d_qk_nw, d_mix_nw, d_scal, (dcq, dck, dcv), sq)

    late = ("w_in", "w_uq", "w_ukv")
    *late_grads, s_small, s_conv = _exchange_halves([_sum_slots(s, "sum_" + n) for n, s in zip(late, (s_in, s_uq, s_ukv))],
                                                    [small_buf, conv_buf])
    names = early + late
    grad = {n: g.reshape(-1, g.shape[-1]) for n, g in zip(names, list(early_grads) + list(late_grads))}

    loss, g_small, delta, new_m, new_v = _adamw_small(s_small, s_conv, w, m, v)
    grad.update(g_small)
    for n in names:
        if n == "w_in":
            stored = lambda a: jnp.transpose(a, (2, 0, 1))
            outs = _adamw(stored(w[n]), grad[n][:, None, :], stored(m[n]), stored(v[n]), "adamw_" + n)
            grad[n], delta[n], new_m[n], new_v[n] = (jnp.transpose(a, (1, 2, 0)) for a in (grad[n][:, None, :], *outs))
        else:
            delta[n], new_m[n], new_v[n] = _adamw(local[n], grad[n], m[n][0], v[n][0], "adamw_" + n)
    def in_order(d):
        return [d[n].reshape(w[n].shape) for n in WEIGHTS]

    return (loss.reshape(()), grad_x2.reshape(B, S, D), *in_order(grad), *in_order(delta), *in_order(new_m), *in_order(new_v))
```

```python
import functools
import math

import jax
import jax.numpy as jnp
from jax import lax
from jax.experimental import pallas as pl
from jax.experimental.pallas import tpu as pltpu

F32 = jnp.float32
BF16 = jnp.bfloat16
MESH = pl.DeviceIdType.MESH

EPS = 1e-6
HEADS = 4
HEAD_DIM = 128
ROPE_DIM = 64
ROPE_HALF = 32
QK_DIM = 192
QK_PAD = 256
LORA = 256
CHUNK = 64
CONV_TAPS = 4
ROPE_THETA = 10000.0
ATTN_SCALE = QK_DIM ** -0.5

LAT_W = 640
GQKV_W = 3 * HEADS * HEAD_DIM
GZ_W = HEADS * HEAD_DIM
GAB_W = 128
PROJ_SPLITS = ((0, LAT_W), (LAT_W, LAT_W + GQKV_W), (LAT_W + GQKV_W, LAT_W + GQKV_W + GZ_W),
               (LAT_W + GQKV_W + GZ_W, LAT_W + GQKV_W + GZ_W + GAB_W))
PROJ_W = PROJ_SPLITS[-1][1]

ADAM_LR = 0.001
ADAM_B1 = 0.9
ADAM_B2 = 0.999
ADAM_EPS = 1e-08
ADAM_WD = 0.01
ADAM_STEP = 10

TOKEN_TILE = 512
MLP_TOKEN_TILE = 512
FF_TILE = 1024
ATTN_TILE = 512
ATTN_HEADS_PER_STEP = 2
WGRAD_OUT_BYTES = 8 * 1024 * 1024
VMEM_LIMIT = 48 * 1024 * 1024

SHARDED = ("w_in", "w_uq", "w_ukv", "conv_w", "w_out", "w_up", "w_down")
WEIGHTS = ("attn_norm_w", "w_in", "q_lat_norm_w", "w_uq", "kv_lat_norm_w", "w_ukv", "q_norm_w", "k_norm_w", "mla_out_norm_w",
           "conv_w", "a_log", "dt_bias", "gdn_norm_w", "w_out", "mlp_norm_w", "w_up", "w_down")


def _sds(shape, dtype):
    return jax.ShapeDtypeStruct(shape, dtype)


def _params(semantics):
    return pltpu.CompilerParams(dimension_semantics=semantics, vmem_limit_bytes=VMEM_LIMIT)


def _block(n):
    for b in (512, 256, 128):
        if n % b == 0:
            return b
    return n


def _dg(a, b, ca, cb, prec):
    lead = a.ndim - 2
    batch = (tuple(range(lead)),) * 2
    return lax.dot_general(a, b, (((ca + lead,), (cb + lead,)), batch), precision=prec, preferred_element_type=F32)


def _split_bf16(a):
    hi = a.astype(BF16)
    return hi, (a - hi.astype(F32)).astype(BF16)


def _dot_bf16(a, b, ca, cb):
    return _dg(a.astype(BF16), b.astype(BF16), ca, cb, None)


def _dot_bf16x3(a, b, ca, cb):
    a_hi, a_lo = _split_bf16(a)
    b_hi, b_lo = _split_bf16(b)
    lead = a.ndim - 2
    return _dg(jnp.concatenate([a_hi, a_hi, a_lo], axis=ca + lead), jnp.concatenate([b_hi, b_lo, b_hi], axis=cb + lead), ca, cb, None)


def _matmul_family(dot):
    def nn_raw(a, b):
        return dot(a, b, 1, 0)

    def nt_raw(a, b):
        return dot(a, b, 1, 1)

    def tn_raw(a, b):
        return dot(a, b, 0, 0)

    @jax.custom_vjp
    def nn(a, b):
        return nn_raw(a, b)

    nn.defvjp(lambda a, b: (nn_raw(a, b), (a, b)), lambda r, g: (nt_raw(g, r[1]), tn_raw(r[0], g)))

    @jax.custom_vjp
    def nt(a, b):
        return nt_raw(a, b)

    nt.defvjp(lambda a, b: (nt_raw(a, b), (a, b)), lambda r, g: (nn_raw(g, r[1]), tn_raw(g, r[0])))

    @jax.custom_vjp
    def tn(a, b):
        return tn_raw(a, b)

    tn.defvjp(lambda a, b: (tn_raw(a, b), (a, b)), lambda r, g: (nt_raw(r[1], g), nn_raw(r[0], g)))
    return nn, nt, tn


_bf_nn, _bf_nt, _bf_tn = _matmul_family(_dot_bf16)
_hi_nn, _hi_nt, _hi_tn = _matmul_family(_dot_bf16x3)


def _lower_powers(lmat):
    powers = []
    while 2 ** (len(powers) + 1) < lmat.shape[-1]:
        powers.append(_dot_bf16x3(powers[-1] if powers else lmat, powers[-1] if powers else lmat, 1, 0))
    return powers


@jax.custom_vjp
def _unit_lower_solve(lmat, rhs):
    return _unit_lower_solve_fwd(lmat, rhs)[0]


def _unit_lower_solve_fwd(lmat, rhs):
    powers = _lower_powers(lmat)
    x = rhs - _dot_bf16x3(lmat, rhs, 1, 0)
    for p in powers:
        x = x + _dot_bf16x3(p, x, 1, 0)
    return x, (lmat, powers, x)


def _unit_lower_solve_bwd(res, g):
    lmat, powers, x = res
    y = g - _dot_bf16x3(lmat, g, 0, 0)
    for p in powers:
        y = y + _dot_bf16x3(p, y, 0, 0)
    return -_dot_bf16x3(y, x, 1, 1), y


_unit_lower_solve.defvjp(_unit_lower_solve_fwd, _unit_lower_solve_bwd)


@jax.custom_vjp
def _lane_halves(x):
    n = x.shape[-1] // 2
    return x[..., :n], x[..., n:]


_lane_halves.defvjp(lambda x: (_lane_halves(x), None), lambda _, g: (jnp.concatenate(g, axis=-1),))


@jax.custom_vjp
def _row_halves(x):
    n = x.shape[-2] // 2
    return x[..., :n, :], x[..., n:, :]


_row_halves.defvjp(lambda x: (_row_halves(x), None), lambda _, g: (jnp.concatenate(g, axis=-2),))


@jax.custom_vjp
def _swap_halves(t):
    return pltpu.roll(t, 64, 1)


_swap_halves.defvjp(lambda t: (pltpu.roll(t, 64, 1), None), lambda _, g: (pltpu.roll(g, 64, 1),))


@functools.partial(jax.custom_vjp, nondiff_argnums=(2,))
def _shift_rows(x, keep, s):
    return pltpu.roll(x, s, 0) * keep


def _shift_rows_fwd(x, keep, s):
    return pltpu.roll(x, s, 0) * keep, keep


def _shift_rows_bwd(s, keep, g):
    return pltpu.roll(g * keep, keep.shape[0] - s, 0), jnp.zeros_like(keep)


_shift_rows.defvjp(_shift_rows_fwd, _shift_rows_bwd)


def _sigmoid(x):
    return 0.5 * jnp.tanh(0.5 * x) + 0.5


def _softplus(x):
    return jnp.maximum(x, 0.0) + jnp.log(1.0 + jnp.exp(jnp.minimum(x, -x)))


def _silu(x):
    return x * _sigmoid(x)


def _rms(x, w, n=None):
    n = x.shape[-1] if n is None else n
    r = lax.rsqrt(jnp.sum(x * x, axis=-1, keepdims=True) * (1.0 / n) + EPS)
    return x * r * w


def _rope(t, cos_f, sin_f):
    return t * cos_f + _swap_halves(t) * sin_f


def _rope_tables(pos_col, freq_row, sign_row):
    ang = pos_col.astype(F32) * freq_row
    return jnp.cos(ang), jnp.sin(ang) * sign_row


def _onehot_row(lane):
    return (lax.broadcasted_iota(jnp.int32, (1, 128), 1) == lane).astype(F32)


def _row_spec(tm, w):
    return pl.BlockSpec((tm, w), lambda i: (i, 0))


def _const_spec(shape):
    return pl.BlockSpec(shape, lambda *_: (0,) * len(shape))


def _in_proj_fwd(x2, w_an, w_in_p):
    T, D = x2.shape
    tm = min(TOKEN_TILE, T)

    def body(x_ref, wn_ref, w_ref, xn_ref, lat_ref, gqkv_ref, gz_ref, gab_ref):
        x = x_ref[...]
        r = lax.rsqrt(jnp.mean(x * x, axis=-1, keepdims=True) + EPS)
        xn = (x * r * wn_ref[...]).astype(BF16)
        xn_ref[...] = xn
        for ref, (a, b) in zip((lat_ref, gqkv_ref, gz_ref, gab_ref), PROJ_SPLITS):
            ref[...] = _dg(xn, w_ref[a:b, :], 1, 1, None)

    widths = [b - a for a, b in PROJ_SPLITS]
    return pl.pallas_call(
        body, grid=(T // tm,),
        in_specs=[_row_spec(tm, D), _const_spec((1, D)), _const_spec((PROJ_W, D))],
        out_specs=[_row_spec(tm, D)] + [_row_spec(tm, w) for w in widths],
        out_shape=[_sds((T, D), BF16)] + [_sds((T, w), F32) for w in widths],
        compiler_params=_params(("parallel",)), name="in_proj_fwd",
    )(x2, w_an, w_in_p)


def _in_proj_bwd(pieces, w_in_p, x2, w_an, dh, partials):
    T, D = x2.shape
    tm = min(TOKEN_TILE, T)
    widths = [p.shape[1] for p in pieces]
    starts = [sum(widths[:i]) for i in range(len(widths))]
    assert sum(widths) == PROJ_W
    npc, ns = len(pieces), len(partials)

    def body(*refs):
        piece_refs = refs[:npc]
        w_ref, x_ref, wn_ref, dh_ref = refs[npc:npc + 4]
        src_refs = refs[npc + 4:npc + 4 + ns]
        dx_ref, dwn_ref = refs[npc + 4 + ns:npc + 6 + ns]
        dst_refs = refs[npc + 6 + ns:npc + 6 + 2 * ns]
        sems = refs[npc + 6 + 2 * ns:]

        @pl.when(pl.program_id(0) == 0)
        def _():
            for start in _scatter_copies(src_refs, dst_refs, *sems)[0]:
                start()
            dwn_ref[...] = jnp.zeros_like(dwn_ref)

        dxn = jnp.zeros((tm, D), F32)
        for ref, a, width in zip(piece_refs, starts, widths):
            dxn += _dg(ref[...], w_ref[a:a + width, :], 1, 0, None)
        _, pull = jax.vjp(_rms, x_ref[...], wn_ref[...])
        dx, dwn = pull(dxn)
        dx_ref[...] = dx + dh_ref[...]
        dwn_ref[...] += dwn

        @pl.when(pl.program_id(0) == T // tm - 1)
        def _():
            for wait in _scatter_copies(src_refs, dst_refs, *sems)[1]:
                wait()

    return pl.pallas_call(
        body, grid=(T // tm,),
        in_specs=[_row_spec(tm, w) for w in widths] + [_const_spec((PROJ_W, D)), _row_spec(tm, D), _const_spec((1, D)),
                                                       _row_spec(tm, D)] + [_ANY] * ns,
        out_specs=[_row_spec(tm, D), _const_spec((1, D))] + [_ANY] * ns,
        out_shape=[_sds((T, D), F32), _sds((1, D), F32)] + [_scattered_shape(p) for p in partials],
        scratch_shapes=_scatter_scratch(ns),
        compiler_params=_params(("arbitrary",)), name="in_proj_bwd",
    )(*pieces, w_in_p, x2, w_an, dh, *partials)


def _wgrad_pieces(pieces, b, name):
    T, k2 = b.shape
    tt = min(TOKEN_TILE, T)
    widths = [p.shape[1] for p in pieces]
    starts = [sum(widths[:i]) for i in range(len(widths))]
    k1 = sum(widths)

    def body(*refs):
        piece_refs, (b_ref, o_ref, acc_ref) = refs[:len(pieces)], refs[len(pieces):]
        t = pl.program_id(0)

        @pl.when(t == 0)
        def _():
            acc_ref[...] = jnp.zeros_like(acc_ref)

        bt = b_ref[...].astype(BF16)
        for ref, r0, width in zip(piece_refs, starts, widths):
            acc_ref[r0:r0 + width, :] += jnp.dot(ref[...].T, bt, preferred_element_type=F32)

        @pl.when(t == T // tt - 1)
        def _():
            o_ref[...] = acc_ref[...].astype(o_ref.dtype)

    return pl.pallas_call(
        body, grid=(T // tt,),
        in_specs=[pl.BlockSpec((tt, w), lambda t: (t, 0)) for w in widths] + [pl.BlockSpec((tt, k2), lambda t: (t, 0))],
        out_specs=_const_spec((k1, k2)), out_shape=_sds((k1, k2), BF16), scratch_shapes=[pltpu.VMEM((k1, k2), F32)],
        compiler_params=_params(("arbitrary",)), name=name,
    )(*pieces, b)


def _wgrad(a, b, name, column_shards=1, out_dtype=BF16):
    T, k1 = a.shape
    k2 = b.shape[1]
    per_shard = k2 // column_shards
    tt = min(TOKEN_TILE, T)
    b1 = k1
    while b1 * k2 * 4 > WGRAD_OUT_BYTES and b1 % 256 == 0:
        b1 //= 2
    step = _block(per_shard)

    def body(a_ref, b_ref, o_ref, acc_ref):
        t = pl.program_id(1)

        @pl.when(t == 0)
        def _():
            acc_ref[...] = jnp.zeros_like(acc_ref)

        a_t = a_ref[...].astype(BF16).T
        for c0 in range(0, k2, step):
            part = jnp.dot(a_t, b_ref[:, c0:c0 + step].astype(BF16), preferred_element_type=F32)
            if column_shards == 1:
                acc_ref[:, c0:c0 + step] += part
            else:
                acc_ref[c0 // per_shard, :, c0 % per_shard:c0 % per_shard + step] += part

        @pl.when(t == T // tt - 1)
        def _():
            o_ref[...] = acc_ref[...].astype(o_ref.dtype)

    if column_shards == 1:
        block, out_spec, out_shape = (b1, k2), pl.BlockSpec((b1, k2), lambda i, t: (i, 0)), _sds((k1, k2), out_dtype)
    else:
        block = (column_shards, b1, per_shard)
        out_spec, out_shape = pl.BlockSpec(block, lambda i, t: (0, i, 0)), _sds((column_shards, k1, per_shard), out_dtype)
    return pl.pallas_call(
        body, grid=(k1 // b1, T // tt),
        in_specs=[pl.BlockSpec((tt, b1), lambda i, t: (t, i)), pl.BlockSpec((tt, k2), lambda i, t: (t, 0))],
        out_specs=out_spec, out_shape=out_shape, scratch_shapes=[pltpu.VMEM(block, F32)],
        compiler_params=_params(("parallel", "arbitrary")), name=name,
    )(a, b)


def _mla_pre_fn(q_lat, kv_lat, kpe, ln_q, ln_kv, w_list, qn_n, qn_p, kn_n, kn_p, cos_f, sin_f):
    qn = _rms(q_lat, ln_q)
    kvn = _rms(kv_lat, ln_kv)
    kp = _rope(_rms(kpe, kn_p, ROPE_DIM), cos_f, sin_f)
    outs = []
    for h in range(HEADS):
        outs.append(_rms(_bf_nn(qn, w_list[h]), qn_n))
        outs.append(_rope(_rms(_bf_nn(qn, w_list[HEADS + h]), qn_p, ROPE_DIM), cos_f, sin_f))
        outs.append(_rms(_bf_nn(kvn, w_list[2 * HEADS + h]), kn_n))
        outs.append(_bf_nn(kvn, w_list[3 * HEADS + h]))
    return tuple(outs) + (kp,)


def _mla_pre_operands(lat_ref, pos_ref, ln_ref, w_ref, nw_ref, rope_ref):
    cos_f, sin_f = _rope_tables(pos_ref[...], rope_ref[0:1, :], rope_ref[1:2, :])
    diff = (lat_ref[:, 0:LORA], lat_ref[:, LORA:2 * LORA], lat_ref[:, 2 * LORA:LAT_W], ln_ref[0:1, :], ln_ref[1:2, :],
            [w_ref[i].astype(F32) for i in range(4 * HEADS)], nw_ref[0:1, :], nw_ref[1:2, :], nw_ref[2:3, :], nw_ref[3:4, :])
    return diff, cos_f, sin_f


def _mla_pre_fwd(lat, pos, ln_w, w_mla, nw, rope_rows):
    T = lat.shape[0]
    tm = min(TOKEN_TILE, T)

    def body(lat_ref, pos_ref, ln_ref, w_ref, nw_ref, rope_ref, q_ref, k_ref, v_ref):
        diff, cos_f, sin_f = _mla_pre_operands(lat_ref, pos_ref, ln_ref, w_ref, nw_ref, rope_ref)
        outs = _mla_pre_fn(*diff, cos_f, sin_f)
        kp = outs[-1].astype(BF16)
        for h in range(HEADS):
            q_n, q_p, k_n, v = outs[4 * h:4 * h + 4]
            q_ref[:, h * QK_PAD:h * QK_PAD + HEAD_DIM] = q_n.astype(BF16)
            q_ref[:, h * QK_PAD + HEAD_DIM:(h + 1) * QK_PAD] = q_p.astype(BF16)
            k_ref[:, h * QK_PAD:h * QK_PAD + HEAD_DIM] = k_n.astype(BF16)
            k_ref[:, h * QK_PAD + HEAD_DIM:(h + 1) * QK_PAD] = kp
            v_ref[:, h * HEAD_DIM:(h + 1) * HEAD_DIM] = v.astype(BF16)

    return pl.pallas_call(
        body, grid=(T // tm,),
        in_specs=[_row_spec(tm, LAT_W), _row_spec(tm, 1), _const_spec((2, LORA)), _const_spec((4 * HEADS, LORA, 128)),
                  _const_spec((8, 128)), _const_spec((8, 128))],
        out_specs=[_row_spec(tm, HEADS * QK_PAD), _row_spec(tm, HEADS * QK_PAD), _row_spec(tm, HEADS * HEAD_DIM)],
        out_shape=[_sds((T, HEADS * QK_PAD), BF16), _sds((T, HEADS * QK_PAD), BF16), _sds((T, HEADS * HEAD_DIM), BF16)],
        compiler_params=_params(("parallel",)), name="mla_pre_fwd",
    )(lat, pos, ln_w, w_mla, nw, rope_rows)


def _mla_pre_bwd(lat, pos, ln_w, w_mla, nw, rope_rows, dq, dk, dv, halves):
    T = lat.shape[0]
    tm = min(TOKEN_TILE, T)
    ns = len(halves)

    def body(*refs):
        lat_ref, pos_ref, ln_ref, w_ref, nw_ref, rope_ref, dq_ref, dk_ref, dv_ref = refs[:9]
        src_refs = refs[9:9 + ns]
        dlat_ref, dln_ref, dw_ref, dnw_ref = refs[9 + ns:13 + ns]
        dst_refs = refs[13 + ns:13 + 2 * ns]
        sems = refs[13 + 2 * ns:]

        @pl.when(pl.program_id(0) == 0)
        def _():
            for start in _swap_copies(src_refs, dst_refs, *sems)[0]:
                start()
            dln_ref[...] = jnp.zeros_like(dln_ref)
            dw_ref[...] = jnp.zeros_like(dw_ref)
            dnw_ref[...] = jnp.zeros_like(dnw_ref)

        diff, cos_f, sin_f = _mla_pre_operands(lat_ref, pos_ref, ln_ref, w_ref, nw_ref, rope_ref)
        _, pull = jax.vjp(lambda *a: _mla_pre_fn(*a, cos_f, sin_f), *diff)
        cts = []
        d_kp = jnp.zeros((tm, 128), F32)
        for h in range(HEADS):
            cts.append(dq_ref[:, h * QK_PAD:h * QK_PAD + HEAD_DIM])
            cts.append(dq_ref[:, h * QK_PAD + HEAD_DIM:(h + 1) * QK_PAD])
            cts.append(dk_ref[:, h * QK_PAD:h * QK_PAD + HEAD_DIM])
            cts.append(dv_ref[:, h * HEAD_DIM:(h + 1) * HEAD_DIM])
            d_kp += dk_ref[:, h * QK_PAD + HEAD_DIM:(h + 1) * QK_PAD]
        d_ql, d_kvl, d_kpe, d_lnq, d_lnkv, d_w, d_qn_n, d_qn_p, d_kn_n, d_kn_p = pull(tuple(cts) + (d_kp,))
        dlat_ref[:, 0:LORA] = d_ql.astype(BF16)
        dlat_ref[:, LORA:2 * LORA] = d_kvl.astype(BF16)
        dlat_ref[:, 2 * LORA:LAT_W] = d_kpe.astype(BF16)
        dln_ref[0:1, :] += d_lnq
        dln_ref[1:2, :] += d_lnkv
        for i in range(4 * HEADS):
            dw_ref[i] += d_w[i]
        for i, d in enumerate((d_qn_n, d_qn_p, d_kn_n, d_kn_p)):
            dnw_ref[i:i + 1, :] += d

        @pl.when(pl.program_id(0) == T // tm - 1)
        def _():
            for wait in _swap_copies(src_refs, dst_refs, *sems)[1]:
                wait()

    return pl.pallas_call(
        body, grid=(T // tm,),
        in_specs=[_row_spec(tm, LAT_W), _row_spec(tm, 1), _const_spec((2, LORA)), _const_spec((4 * HEADS, LORA, 128)),
                  _const_spec((8, 128)), _const_spec((8, 128)),
                  _row_spec(tm, HEADS * QK_PAD), _row_spec(tm, HEADS * QK_PAD), _row_spec(tm, HEADS * HEAD_DIM)] + [_ANY] * ns,
        out_specs=[_row_spec(tm, LAT_W), _const_spec((2, LORA)), _const_spec((4 * HEADS, LORA, 128)), _const_spec((8, 128))]
                  + [_ANY] * ns,
        out_shape=[_sds((T, LAT_W), BF16), _sds((2, LORA), F32), _sds((4 * HEADS, LORA, 128), F32), _sds((8, 128), F32)]
                  + [_swapped_shape(h) for h in halves],
        scratch_shapes=_swap_scratch(ns),
        compiler_params=_params(("arbitrary",)), name="mla_pre_bwd",
    )(lat, pos, ln_w, w_mla, nw, rope_rows, dq, dk, dv, *halves)


def _causal_mask(i, j, tq, tk):
    row = i * tq + lax.broadcasted_iota(jnp.int32, (tq, tk), 0)
    col = j * tk + lax.broadcasted_iota(jnp.int32, (tq, tk), 1)
    return col <= row


def _attn_fwd(q, k, v, shards):
    B, S, _ = q.shape
    t = min(ATTN_TILE, S)
    nq = S // t
    ns = len(shards)

    hp = ATTN_HEADS_PER_STEP
    qk = lambda h: slice(h * QK_PAD, (h + 1) * QK_PAD)
    vd = lambda h: slice(h * HEAD_DIM, (h + 1) * HEAD_DIM)

    def body(*refs):
        q_ref, k_ref, v_ref = refs[:3]
        src_refs = refs[3:3 + ns]
        o_ref, lse_ref = refs[3 + ns:5 + ns]
        dst_refs = refs[5 + ns:5 + 2 * ns]
        sems = refs[5 + 2 * ns:]
        b, g, i = pl.program_id(0), pl.program_id(1), pl.program_id(2)
        qb = [q_ref[0, :, qk(h)] for h in range(hp)]

        step_no = (b * (HEADS // hp) + g) * nq + i
        for phase, at in enumerate((0, (3 * B * (HEADS // hp) * nq) // 4)):
            @pl.when(step_no == at)
            def _(phase=phase):
                for call in _gather_copies(src_refs, dst_refs, *sems)[phase]:
                    call()

        def step(j, carry, diagonal):
            rows = pl.ds(pl.multiple_of(j * t, t), t)
            s = [_dg(qb[h], k_ref[0, rows, qk(h)], 1, 1, None) * ATTN_SCALE for h in range(hp)]
            if diagonal:
                keep = _causal_mask(0, 0, t, t)
                s = [jnp.where(keep, x, -1e30) for x in s]
            m_new = [jnp.maximum(carry[h][0], jnp.max(s[h], axis=-1, keepdims=True)) for h in range(hp)]
            p = [jnp.exp(s[h] - m_new[h]) for h in range(hp)]
            alpha = [jnp.exp(carry[h][0] - m_new[h]) for h in range(hp)]
            l = [alpha[h] * carry[h][1] + jnp.sum(p[h], axis=-1, keepdims=True) for h in range(hp)]
            pv = [jnp.dot(p[h].astype(BF16), v_ref[0, rows, vd(h)], preferred_element_type=F32) for h in range(hp)]
            return tuple((m_new[h], l[h], alpha[h] * carry[h][2] + pv[h]) for h in range(hp))

        init = tuple((jnp.full((t, 1), -1e30, F32), jnp.zeros((t, 1), F32), jnp.zeros((t, HEAD_DIM), F32)) for _ in range(hp))
        below = lax.fori_loop(0, i, lambda j, carry: step(j, carry, False), init)
        for h, (m, l, acc) in enumerate(step(i, below, True)):
            o_ref[0, :, vd(h)] = acc / l
            lse_ref[0, h, 0] = (m + jnp.log(l)).T

        @pl.when((b == B - 1) & (g == HEADS // hp - 1) & (i == nq - 1))
        def _():
            for wait in _gather_copies(src_refs, dst_refs, *sems)[2]:
                wait()

    return pl.pallas_call(
        body, grid=(B, HEADS // hp, nq),
        in_specs=[pl.BlockSpec((1, t, hp * QK_PAD), lambda b, g, i: (b, i, g)),
                  pl.BlockSpec((1, S, hp * QK_PAD), lambda b, g, i: (b, 0, g)),
                  pl.BlockSpec((1, S, hp * HEAD_DIM), lambda b, g, i: (b, 0, g))] + [_ANY] * ns,
        out_specs=[pl.BlockSpec((1, t, hp * HEAD_DIM), lambda b, g, i: (b, i, g)),
                   pl.BlockSpec((1, hp, 1, 1, t), lambda b, g, i: (b, g, i, 0, 0))] + [_ANY] * ns,
        out_shape=[_sds((B, S, HEADS * HEAD_DIM), F32), _sds((B, HEADS, nq, 1, t), F32)] + [_sds((4,) + s.shape, s.dtype) for s in shards],
        scratch_shapes=_gather_scratch(ns),
        compiler_params=_params(("arbitrary", "arbitrary", "arbitrary")), name="attn_fwd",
    )(q, k, v, *shards)


def _attn_bwd(q, k, v, o, lse, do, partials):
    B, S, _ = q.shape
    t = min(ATTN_TILE, S)
    nq = S // t
    ns = len(partials)

    hp = ATTN_HEADS_PER_STEP
    qk = lambda h: slice(h * QK_PAD, (h + 1) * QK_PAD)
    vd = lambda h: slice(h * HEAD_DIM, (h + 1) * HEAD_DIM)
    heads = range(hp)

    def body(*refs):
        q_ref, k_ref, v_ref, o_ref, lse_ref, do_ref = refs[:6]
        src_refs = refs[6:6 + ns]
        dq_ref, dk_ref, dv_ref = refs[6 + ns:9 + ns]
        dst_refs = refs[9 + ns:9 + 2 * ns]
        dsum_ref, send_sems, recv_sems, local_sems = refs[9 + 2 * ns:]
        b, g, j = pl.program_id(0), pl.program_id(1), pl.program_id(2)

        @pl.when((b == 0) & (g == 0) & (j == 0))
        def _():
            for start in _scatter_copies(src_refs, dst_refs, send_sems, recv_sems, local_sems)[0]:
                start()

        @pl.when(j == 0)
        def _():
            dq_ref[...] = jnp.zeros_like(dq_ref)
            for h in heads:
                for blk in range(nq):
                    rows = slice(blk * t, (blk + 1) * t)
                    dsum_ref[h, blk] = jnp.sum(do_ref[0, rows, vd(h)] * o_ref[0, rows, vd(h)], axis=-1, keepdims=True).T

        kb = [k_ref[0, :, qk(h)] for h in heads]
        vb = [v_ref[0, :, vd(h)] for h in heads]

        def step(i, carry, diagonal):
            rows = pl.ds(pl.multiple_of(i * t, t), t)
            qb = [q_ref[0, rows, qk(h)] for h in heads]
            dob = [do_ref[0, rows, vd(h)].astype(BF16) for h in heads]
            s = [_dg(kb[h], qb[h], 1, 1, None) * ATTN_SCALE for h in heads]
            p = [jnp.exp(s[h] - lse_ref[0, h, i]) for h in heads]
            if diagonal:
                key = lax.broadcasted_iota(jnp.int32, (t, t), 0)
                query = lax.broadcasted_iota(jnp.int32, (t, t), 1)
                p = [jnp.where(key <= query, x, 0.0) for x in p]
            dp = [_dg(vb[h], dob[h], 1, 1, None) for h in heads]
            dv = [carry[h][1] + jnp.dot(p[h].astype(BF16), dob[h], preferred_element_type=F32) for h in heads]
            ds = [(p[h] * (dp[h] - dsum_ref[h, i]) * ATTN_SCALE).astype(BF16) for h in heads]
            for h in heads:
                dq_ref[0, rows, qk(h)] += _dg(ds[h], kb[h], 0, 0, None)
            return tuple((carry[h][0] + jnp.dot(ds[h], qb[h], preferred_element_type=F32), dv[h]) for h in heads)

        zeros = tuple((jnp.zeros((t, QK_PAD), F32), jnp.zeros((t, HEAD_DIM), F32)) for _ in heads)
        on_diagonal = step(j, zeros, True)
        done = lax.fori_loop(j + 1, nq, lambda i, carry: step(i, carry, False), on_diagonal)
        for h, (dk, dv) in enumerate(done):
            dk_ref[0, :, qk(h)] = dk
            dv_ref[0, :, vd(h)] = dv

        @pl.when((b == B - 1) & (g == HEADS // hp - 1) & (j == nq - 1))
        def _():
            for wait in _scatter_copies(src_refs, dst_refs, send_sems, recv_sems, local_sems)[1]:
                wait()

    return pl.pallas_call(
        body, grid=(B, HEADS // hp, nq),
        in_specs=[pl.BlockSpec((1, S, hp * QK_PAD), lambda b, g, j: (b, 0, g)),
                  pl.BlockSpec((1, t, hp * QK_PAD), lambda b, g, j: (b, j, g)),
                  pl.BlockSpec((1, t, hp * HEAD_DIM), lambda b, g, j: (b, j, g)),
                  pl.BlockSpec((1, S, hp * HEAD_DIM), lambda b, g, j: (b, 0, g)),
                  pl.BlockSpec((1, hp, nq, 1, t), lambda b, g, j: (b, g, 0, 0, 0)),
                  pl.BlockSpec((1, S, hp * HEAD_DIM), lambda b, g, j: (b, 0, g))] + [_ANY] * ns,
        out_specs=[pl.BlockSpec((1, S, hp * QK_PAD), lambda b, g, j: (b, 0, g)),
                   pl.BlockSpec((1, t, hp * QK_PAD), lambda b, g, j: (b, j, g)),
                   pl.BlockSpec((1, t, hp * HEAD_DIM), lambda b, g, j: (b, j, g))] + [_ANY] * ns,
        out_shape=[_sds((B, S, HEADS * QK_PAD), F32), _sds((B, S, HEADS * QK_PAD), F32), _sds((B, S, HEADS * HEAD_DIM), F32)]
                  + [_scattered_shape(p) for p in partials],
        scratch_shapes=[pltpu.VMEM((hp, nq, 1, t), F32)] + _scatter_scratch(ns),
        compiler_params=_params(("arbitrary", "arbitrary", "arbitrary")), name="attn_bwd",
    )(q, k, v, o, lse, do, *partials)


def _gdn_pre_fn(xq, xk, xv, wq, wk, wv, keeps):
    def conv_silu(x, w):
        acc = x * w[3]
        for s in (1, 2, 3):
            acc = acc + _shift_rows(x, keeps[s - 1], s) * w[3 - s]
        return _silu(acc)

    def l2(x):
        return x * lax.rsqrt(jnp.sum(x * x, axis=-1, keepdims=True) + EPS)

    return l2(conv_silu(xq, wq)) * (HEAD_DIM ** -0.5), l2(conv_silu(xk, wk)), conv_silu(xv, wv)


def _gdn_pre_specs(S):
    x_specs = [pl.BlockSpec((1, S, HEAD_DIM), lambda h, b, g=g: (b, 0, g * HEADS + h)) for g in range(3)]
    w_specs = [pl.BlockSpec((CONV_TAPS, HEAD_DIM), lambda h, b, g=g: (0, g * HEADS + h)) for g in range(3)]
    out_spec = pl.BlockSpec((1, S, HEAD_DIM), lambda h, b: (b, 0, h))
    return x_specs, w_specs, out_spec


def _row_keeps(S):
    t = lax.broadcasted_iota(jnp.int32, (S, HEAD_DIM), 0)
    return [(t >= s).astype(F32) for s in (1, 2, 3)]


def _gdn_pre_fwd(gqkv, conv_w):
    B, S, _ = gqkv.shape
    x_specs, w_specs, out_spec = _gdn_pre_specs(S)

    def body(xq_ref, xk_ref, xv_ref, wq_ref, wk_ref, wv_ref, q_ref, k_ref, v_ref):
        taps = [[w[i:i + 1, :] for i in range(CONV_TAPS)] for w in (wq_ref, wk_ref, wv_ref)]
        q, k, v = _gdn_pre_fn(xq_ref[0], xk_ref[0], xv_ref[0], *taps, _row_keeps(S))
        q_ref[0], k_ref[0], v_ref[0] = q, k, v

    return pl.pallas_call(
        body, grid=(HEADS, B), in_specs=x_specs + w_specs, out_specs=[out_spec] * 3,
        out_shape=[_sds((B, S, HEADS * HEAD_DIM), F32)] * 3,
        compiler_params=_params(("parallel", "parallel")), name="gdn_pre_fwd",
    )(gqkv, gqkv, gqkv, conv_w, conv_w, conv_w)


def _gdn_pre_bwd(gqkv, conv_w, dq, dk, dv, halves):
    B, S, _ = gqkv.shape
    x_specs, w_specs, out_spec = _gdn_pre_specs(S)
    dw_spec = pl.BlockSpec((CONV_TAPS, HEAD_DIM), lambda h, b: (0, h))
    ns = len(halves)

    def body(*refs):
        xq_ref, xk_ref, xv_ref, wq_ref, wk_ref, wv_ref, dq_ref, dk_ref, dv_ref = refs[:9]
        src_refs = refs[9:9 + ns]
        dxq_ref, dxk_ref, dxv_ref, dwq_ref, dwk_ref, dwv_ref = refs[9 + ns:15 + ns]
        dst_refs = refs[15 + ns:15 + 2 * ns]
        sems = refs[15 + 2 * ns:]
        first = (pl.program_id(0) == 0) & (pl.program_id(1) == 0)
        last = (pl.program_id(0) == HEADS - 1) & (pl.program_id(1) == B - 1)

        @pl.when(first)
        def _():
            for start in _swap_copies(src_refs, dst_refs, *sems)[0]:
                start()

        @pl.when(pl.program_id(1) == 0)
        def _():
            for r in (dwq_ref, dwk_ref, dwv_ref):
                r[...] = jnp.zeros_like(r)

        taps = [[w[i:i + 1, :] for i in range(CONV_TAPS)] for w in (wq_ref, wk_ref, wv_ref)]
        keeps = _row_keeps(S)
        _, pull = jax.vjp(lambda *a: _gdn_pre_fn(*a, keeps), xq_ref[0], xk_ref[0], xv_ref[0], *taps)
        dxq, dxk, dxv, dwq, dwk, dwv = pull((dq_ref[0], dk_ref[0], dv_ref[0]))
        dxq_ref[0], dxk_ref[0], dxv_ref[0] = dxq.astype(BF16), dxk.astype(BF16), dxv.astype(BF16)
        for ref, dw in ((dwq_ref, dwq), (dwk_ref, dwk), (dwv_ref, dwv)):
            for i in range(CONV_TAPS):
                ref[i:i + 1, :] += dw[i]

        @pl.when(last)
        def _():
            for wait in _swap_copies(src_refs, dst_refs, *sems)[1]:
                wait()

    hw = HEADS * HEAD_DIM
    return pl.pallas_call(
        body, grid=(HEADS, B), in_specs=x_specs + w_specs + [out_spec] * 3 + [_ANY] * ns,
        out_specs=[out_spec] * 3 + [dw_spec] * 3 + [_ANY] * ns,
        out_shape=[_sds((B, S, hw), BF16)] * 3 + [_sds((CONV_TAPS, hw), F32)] * 3 + [_swapped_shape(h) for h in halves],
        scratch_shapes=_swap_scratch(ns),
        compiler_params=_params(("arbitrary", "arbitrary")), name="gdn_pre_bwd",
    )(gqkv, gqkv, gqkv, conv_w, conv_w, conv_w, dq, dk, dv, *halves)


def _chunk_masks():
    i = lax.broadcasted_iota(jnp.int32, (CHUNK, CHUNK), 0)
    j = lax.broadcasted_iota(jnp.int32, (CHUNK, CHUNK), 1)
    lower, after = (j <= i).astype(F32), (j > i).astype(F32)
    return {"le": lower, "le_gt": jnp.concatenate([lower, after], axis=0), "strict": (j < i).astype(F32)}


def _gdn_chunk_fn(groups, masks):
    lane = lax.broadcasted_iota(jnp.int32, (groups, 1, 128), 2)
    head = lax.broadcasted_iota(jnp.int32, (groups, 1, 128), 0) % HEADS
    pick_a, pick_b = (lane == head).astype(F32), (lane == head + HEADS).astype(F32)
    lower, lower_after, strict = (jnp.broadcast_to(masks[n], (groups,) + masks[n].shape) for n in ("le", "le_gt", "strict"))
    ones_row = jnp.ones((1, 1, HEAD_DIM), F32)

    def f(q, k, v, gab, a_row, dt_row, state):
        ga = jnp.sum(gab * pick_a, axis=2, keepdims=True)
        gb = jnp.sum(gab * pick_b, axis=2, keepdims=True)
        a_log = jnp.sum(a_row * pick_a, axis=2, keepdims=True)
        dt_bias = jnp.sum(dt_row * pick_a, axis=2, keepdims=True)
        beta = _sigmoid(gb)
        g = -jnp.exp(a_log) * _softplus(ga + dt_bias)
        g_wide = g * ones_row
        cum, rest = _row_halves(_hi_nn(lower_after, g_wide))
        total = jnp.sum(g_wide, axis=1, keepdims=True)
        diff = _hi_nn(lower, g * strict)
        decay = lower * jnp.exp(diff)
        e_cum = jnp.exp(cum)
        kk, qk = _row_halves(_bf_nt(jnp.concatenate([k, q], axis=1), k))
        lmat = strict * (beta * kk * decay)
        u, w = _lane_halves(_unit_lower_solve(lmat, jnp.concatenate([v * beta, k * (beta * e_cum)], axis=2)))
        w_state, q_state = _row_halves(_bf_nn(jnp.concatenate([w, q * e_cum], axis=1), state))
        v_new = u - w_state
        o = q_state + _bf_nn(qk * decay, v_new)
        new_state = state * jnp.exp(total) + _bf_tn(k * jnp.exp(rest), v_new)
        return o, new_state

    return f


def _gdn_chunk_fwd(q, k, v, gab, scal, shards):
    B, S, W = q.shape
    N = S // CHUNK
    ns = len(shards)

    def body(*refs):
        q_ref, k_ref, v_ref, gab_ref, sc_ref = refs[:5]
        src_refs = refs[5:5 + ns]
        o_ref, st_ref = refs[5 + ns:7 + ns]
        dst_refs = refs[7 + ns:7 + 2 * ns]
        state_ref, send_sems, recv_sems, local_sems = refs[7 + 2 * ns:]
        n = pl.program_id(0)

        @pl.when(n == 0)
        def _():
            for start in _gather_copies(src_refs, dst_refs, send_sems, recv_sems, local_sems)[0]:
                start()
            state_ref[...] = jnp.zeros_like(state_ref)

        @pl.when(n == (2 * N) // 3)
        def _():
            for pass_on in _gather_copies(src_refs, dst_refs, send_sems, recv_sems, local_sems)[1]:
                pass_on()

        groups = [(b, h) for b in range(B) for h in range(HEADS)]
        gather = lambda ref: jnp.stack([ref[b, :, h * HEAD_DIM:(h + 1) * HEAD_DIM] for b, h in groups])
        state = state_ref[...]
        for i, (b, h) in enumerate(groups):
            st_ref[b, 0, h] = state[i]
        o, new_state = _gdn_chunk_fn(len(groups), _chunk_masks())(
            gather(q_ref), gather(k_ref), gather(v_ref), jnp.stack([gab_ref[b] for b, _ in groups]), sc_ref[0:1, :], sc_ref[1:2, :], state)
        for i, (b, h) in enumerate(groups):
            o_ref[b, :, h * HEAD_DIM:(h + 1) * HEAD_DIM] = o[i]
        state_ref[...] = new_state

        @pl.when(n == N - 1)
        def _():
            for wait in _gather_copies(src_refs, dst_refs, send_sems, recv_sems, local_sems)[2]:
                wait()

    seq = pl.BlockSpec((B, CHUNK, W), lambda n: (0, n, 0))
    return pl.pallas_call(
        body, grid=(N,),
        in_specs=[seq, seq, seq, pl.BlockSpec((B, CHUNK, GAB_W), lambda n: (0, n, 0)), _const_spec((8, 128))] + [_ANY] * ns,
        out_specs=[seq, pl.BlockSpec((B, 1, HEADS, HEAD_DIM, HEAD_DIM), lambda n: (0, n, 0, 0, 0))] + [_ANY] * ns,
        out_shape=[_sds((B, S, W), F32), _sds((B, N, HEADS, HEAD_DIM, HEAD_DIM), F32)] + [_sds((4,) + s.shape, s.dtype) for s in shards],
        scratch_shapes=[pltpu.VMEM((B * HEADS, HEAD_DIM, HEAD_DIM), F32)] + _gather_scratch(ns),
        compiler_params=_params(("arbitrary",)), name="gdn_chunk_fwd",
    )(q, k, v, gab, scal, *shards)


def _gdn_chunk_bwd(q, k, v, gab, scal, states, do, partials):
    B, S, W = q.shape
    N = S // CHUNK
    ns = len(partials)

    def body(*refs):
        q_ref, k_ref, v_ref, gab_ref, sc_ref, st_ref, do_ref = refs[:7]
        src_refs = refs[7:7 + ns]
        dq_ref, dk_ref, dv_ref, dgab_ref, dsc_ref = refs[7 + ns:12 + ns]
        dst_refs = refs[12 + ns:12 + 2 * ns]
        dstate_ref, send_sems, recv_sems, local_sems = refs[12 + 2 * ns:]
        n = pl.program_id(0)

        @pl.when(n == 0)
        def _():
            for start in _scatter_copies(src_refs, dst_refs, send_sems, recv_sems, local_sems)[0]:
                start()
            dstate_ref[...] = jnp.zeros_like(dstate_ref)
            dsc_ref[...] = jnp.zeros_like(dsc_ref)

        groups = [(b, h) for b in range(B) for h in range(HEADS)]
        gather = lambda ref: jnp.stack([ref[b, :, h * HEAD_DIM:(h + 1) * HEAD_DIM] for b, h in groups])
        _, pull = jax.vjp(_gdn_chunk_fn(len(groups), _chunk_masks()), gather(q_ref), gather(k_ref), gather(v_ref),
                          jnp.stack([gab_ref[b] for b, _ in groups]), sc_ref[0:1, :], sc_ref[1:2, :],
                          jnp.stack([st_ref[b, 0, h] for b, h in groups]))
        dq, dk, dv, dg, d_a, d_dt, dstate = pull((gather(do_ref), dstate_ref[...]))
        for i, (b, h) in enumerate(groups):
            lanes = slice(h * HEAD_DIM, (h + 1) * HEAD_DIM)
            dq_ref[b, :, lanes] = dq[i]
            dk_ref[b, :, lanes] = dk[i]
            dv_ref[b, :, lanes] = dv[i]
        for b in range(B):
            dgab_ref[b] = sum(dg[b * HEADS + h] for h in range(HEADS)).astype(BF16)
        dstate_ref[...] = dstate
        dsc_ref[0:1, :] += d_a
        dsc_ref[1:2, :] += d_dt

        @pl.when(n == N - 1)
        def _():
            for wait in _scatter_copies(src_refs, dst_refs, send_sems, recv_sems, local_sems)[1]:
                wait()

    seq = pl.BlockSpec((B, CHUNK, W), lambda n: (0, N - 1 - n, 0))
    gab_spec = pl.BlockSpec((B, CHUNK, GAB_W), lambda n: (0, N - 1 - n, 0))
    return pl.pallas_call(
        body, grid=(N,),
        in_specs=[seq, seq, seq, gab_spec, _const_spec((8, 128)),
                  pl.BlockSpec((B, 1, HEADS, HEAD_DIM, HEAD_DIM), lambda n: (0, N - 1 - n, 0, 0, 0)), seq] + [_ANY] * ns,
        out_specs=[seq, seq, seq, gab_spec, _const_spec((8, 128))] + [_ANY] * ns,
        out_shape=[_sds((B, S, W), F32)] * 3 + [_sds((B, S, GAB_W), BF16), _sds((8, 128), F32)] + [_scattered_shape(p) for p in partials],
        scratch_shapes=[pltpu.VMEM((B * HEADS, HEAD_DIM, HEAD_DIM), F32)] + _scatter_scratch(ns),
        compiler_params=_params(("arbitrary",)), name="gdn_chunk_bwd",
    )(q, k, v, gab, scal, states, do, *partials)


def _mix_fn(ao, go, gz, w_mla, w_gdn):
    return tuple(_rms(ao[h], w_mla[h]) for h in range(HEADS)) + tuple(_rms(go[h], w_gdn) * _silu(gz[h]) for h in range(HEADS))


def _mix_operands(ao_ref, go_ref, gz_ref, nw_ref):
    blocks = lambda ref: [ref[:, h * HEAD_DIM:(h + 1) * HEAD_DIM] for h in range(HEADS)]
    return blocks(ao_ref), blocks(go_ref), blocks(gz_ref), [nw_ref[h:h + 1, :] for h in range(HEADS)], nw_ref[HEADS:HEADS + 1, :]


def _mix_fwd(ao, go, gz, nw, w_out, x2):
    T, D = x2.shape
    tm = min(TOKEN_TILE, T)
    MW = 2 * HEADS * HEAD_DIM

    def body(ao_ref, go_ref, gz_ref, nw_ref, w_ref, x_ref, mix_ref, h_ref):
        outs = _mix_fn(*_mix_operands(ao_ref, go_ref, gz_ref, nw_ref))
        for i, piece in enumerate(outs):
            mix_ref[:, i * HEAD_DIM:(i + 1) * HEAD_DIM] = piece.astype(BF16)
        h_ref[...] = x_ref[...] + jnp.dot(mix_ref[...], w_ref[...], preferred_element_type=F32)

    half = HEADS * HEAD_DIM
    return pl.pallas_call(
        body, grid=(T // tm,),
        in_specs=[_row_spec(tm, half), _row_spec(tm, half), _row_spec(tm, half), _const_spec((8, 128)), _const_spec((MW, D)),
                  _row_spec(tm, D)],
        out_specs=[_row_spec(tm, MW), _row_spec(tm, D)],
        out_shape=[_sds((T, MW), BF16), _sds((T, D), F32)],
        compiler_params=_params(("parallel",)), name="mix_fwd",
    )(ao, go, gz, nw, w_out, x2)


def _mix_bwd(ao, go, gz, nw, w_out, dh):
    T, D = dh.shape
    tm = min(TOKEN_TILE, T)
    MW = 2 * HEADS * HEAD_DIM
    half = HEADS * HEAD_DIM

    def body(ao_ref, go_ref, gz_ref, nw_ref, w_ref, dh_ref, dao_ref, dgo_ref, dgz_ref, dnw_ref):
        @pl.when(pl.program_id(0) == 0)
        def _():
            dnw_ref[...] = jnp.zeros_like(dnw_ref)

        d_mix = _dg(dh_ref[...].astype(BF16), w_ref[...], 1, 1, None)
        cts = tuple(d_mix[:, i * HEAD_DIM:(i + 1) * HEAD_DIM] for i in range(2 * HEADS))
        _, pull = jax.vjp(_mix_fn, *_mix_operands(ao_ref, go_ref, gz_ref, nw_ref))
        d_ao, d_go, d_gz, d_wm, d_wg = pull(cts)
        for h in range(HEADS):
            lanes = slice(h * HEAD_DIM, (h + 1) * HEAD_DIM)
            dao_ref[:, lanes] = d_ao[h]
            dgo_ref[:, lanes] = d_go[h]
            dgz_ref[:, lanes] = d_gz[h].astype(BF16)
            dnw_ref[h:h + 1, :] += d_wm[h]
        dnw_ref[HEADS:HEADS + 1, :] += d_wg

    return pl.pallas_call(
        body, grid=(T // tm,),
        in_specs=[_row_spec(tm, half), _row_spec(tm, half), _row_spec(tm, half), _const_spec((8, 128)), _const_spec((MW, D)),
                  _row_spec(tm, D)],
        out_specs=[_row_spec(tm, half)] * 3 + [_const_spec((8, 128))],
        out_shape=[_sds((T, half), F32)] * 2 + [_sds((T, half), BF16), _sds((8, 128), F32)],
        compiler_params=_params(("arbitrary",)), name="mix_bwd",
    )(ao, go, gz, nw, w_out, dh)


def _up_spec(w_up, tf):
    per_shard = w_up.shape[2] // tf
    return pl.BlockSpec((None, w_up.shape[1], tf), lambda i, j: (j // per_shard, 0, j % per_shard))


def _mlp_fwd(h2, w_mn, w_up, w_down, target):
    T, D = h2.shape
    FF = w_down.shape[0]
    tm, tf = min(MLP_TOKEN_TILE, T), min(FF_TILE, w_up.shape[2])
    nf = FF // tf

    def body(h_ref, wn_ref, wu_ref, wd_ref, t_ref, hn_ref, act_ref, dy_ref, sq_ref, acc_ref):
        j = pl.program_id(1)

        @pl.when(j == 0)
        def _():
            hn_ref[...] = _rms(h_ref[...], wn_ref[...]).astype(BF16)
            acc_ref[...] = jnp.zeros_like(acc_ref)

        up = jnp.dot(hn_ref[...], wu_ref[...], preferred_element_type=F32)
        act = jnp.square(jnp.maximum(up, 0.0)).astype(BF16)
        act_ref[...] = act
        acc_ref[...] += jnp.dot(act, wd_ref[...], preferred_element_type=F32)

        @pl.when(j == nf - 1)
        def _():
            err = h_ref[...] + acc_ref[...] - t_ref[...]
            dy_ref[...] = err * (1.0 / D)
            sq_ref[...] = jnp.zeros_like(sq_ref) + jnp.sum(err * err)

    tok = lambda w: pl.BlockSpec((tm, w), lambda i, j: (i, 0))
    return pl.pallas_call(
        body, grid=(T // tm, nf),
        in_specs=[tok(D), _const_spec((1, D)), _up_spec(w_up, tf), pl.BlockSpec((tf, D), lambda i, j: (j, 0)), tok(D)],
        out_specs=[tok(D), pl.BlockSpec((tm, tf), lambda i, j: (i, j)), tok(D), pl.BlockSpec((1, 8, 128), lambda i, j: (i, 0, 0))],
        out_shape=[_sds((T, D), BF16), _sds((T, FF), BF16), _sds((T, D), F32), _sds((T // tm, 8, 128), F32)],
        scratch_shapes=[pltpu.VMEM((tm, D), F32)],
        compiler_params=_params(("parallel", "arbitrary")), name="mlp_fwd",
    )(h2, w_mn, w_up, w_down, target)


def _mlp_bwd(h2, w_mn, act, w_up, w_down, dy):
    T, D = h2.shape
    FF = w_down.shape[0]
    tm, tf = min(MLP_TOKEN_TILE, T), min(FF_TILE, w_up.shape[2])
    nf = FF // tf

    def body(h_ref, wn_ref, act_ref, wu_ref, wd_ref, dy_ref, dh_ref, dup_ref, dwn_ref, acc_ref, dyb_ref):
        i, j = pl.program_id(0), pl.program_id(1)

        @pl.when((i == 0) & (j == 0))
        def _():
            dwn_ref[...] = jnp.zeros_like(dwn_ref)

        @pl.when(j == 0)
        def _():
            acc_ref[...] = jnp.zeros_like(acc_ref)
            dyb_ref[...] = dy_ref[...].astype(BF16)

        r = jnp.sqrt(act_ref[...].astype(F32))
        d_act = _dg(dyb_ref[...], wd_ref[...], 1, 1, None)
        d_up = (d_act * (2.0 * r)).astype(BF16)
        dup_ref[...] = d_up
        acc_ref[...] += _dg(d_up, wu_ref[...], 1, 1, None)

        @pl.when(j == nf - 1)
        def _():
            _, pull = jax.vjp(_rms, h_ref[...], wn_ref[...])
            dh, dwn = pull(acc_ref[...])
            dh_ref[...] = dh + dy_ref[...]
            dwn_ref[...] += dwn

    tok = lambda w: pl.BlockSpec((tm, w), lambda i, j: (i, 0))
    ff = pl.BlockSpec((tm, tf), lambda i, j: (i, j))
    return pl.pallas_call(
        body, grid=(T // tm, nf),
        in_specs=[tok(D), _const_spec((1, D)), ff, _up_spec(w_up, tf), pl.BlockSpec((tf, D), lambda i, j: (j, 0)), tok(D)],
        out_specs=[tok(D), ff, _const_spec((1, D))],
        out_shape=[_sds((T, D), F32), _sds((T, FF), BF16), _sds((1, D), F32)],
        scratch_shapes=[pltpu.VMEM((tm, D), F32), pltpu.VMEM((tm, D), BF16)],
        compiler_params=_params(("arbitrary", "arbitrary")), name="mlp_bwd",
    )(h2, w_mn, act, w_up, w_down, dy)


def _rope_pad(a):
    z = jnp.zeros(a.shape[:-1] + (ROPE_HALF,), a.dtype)
    return jnp.concatenate([a[..., :ROPE_HALF], z, a[..., ROPE_HALF:], z], axis=-1)


def _rope_unpad(a):
    return jnp.concatenate([a[..., :ROPE_HALF], a[..., 2 * ROPE_HALF:3 * ROPE_HALF]], axis=-1)


_G0 = 2 * LORA + ROPE_DIM
W_IN_COLS = _G0 + GQKV_W + GZ_W + 2 * HEADS


def _widen_w_in_t(w_t):
    z = jnp.zeros((ROPE_HALF, w_t.shape[1]), w_t.dtype)
    pad = jnp.zeros((GAB_W - 2 * HEADS, w_t.shape[1]), w_t.dtype)
    return jnp.concatenate([w_t[:2 * LORA + ROPE_HALF], z, w_t[2 * LORA + ROPE_HALF:_G0], z, w_t[_G0:], pad], axis=0)


def _narrow_w_in_t(w_t):
    return jnp.concatenate([w_t[:2 * LORA + ROPE_HALF], w_t[2 * LORA + 2 * ROPE_HALF:2 * LORA + 3 * ROPE_HALF],
                            w_t[LAT_W:LAT_W + W_IN_COLS - _G0]], axis=0)


def _stack_mla(w_uq, w_ukv):
    uq = w_uq.reshape(LORA, HEADS, QK_DIM)
    ukv = w_ukv.reshape(LORA, HEADS, 2 * HEAD_DIM)
    parts = [uq[:, :, :HEAD_DIM], _rope_pad(uq[:, :, HEAD_DIM:]), ukv[:, :, :HEAD_DIM], ukv[:, :, HEAD_DIM:]]
    return jnp.concatenate([p.transpose(1, 0, 2) for p in parts], axis=0)


def _unstack_mla(w):
    p = [w[i * HEADS:(i + 1) * HEADS].transpose(1, 0, 2) for i in range(4)]
    uq = jnp.concatenate([p[0], _rope_unpad(p[1])], axis=-1).reshape(LORA, HEADS * QK_DIM)
    ukv = jnp.concatenate([p[2], p[3]], axis=-1).reshape(LORA, HEADS * 2 * HEAD_DIM)
    return uq, ukv


def _rows8(rows):
    a = jnp.concatenate(rows, axis=0)
    return jnp.pad(a, ((0, 8 - a.shape[0]), (0, 0)))


def _qk_norm_rows(q_norm_w, k_norm_w):
    return _rows8([q_norm_w[:, :HEAD_DIM], _rope_pad(q_norm_w[:, HEAD_DIM:]), k_norm_w[:, :HEAD_DIM], _rope_pad(k_norm_w[:, HEAD_DIM:])])


def _rope_rows():
    inv_freq = ROPE_THETA ** (-jnp.arange(ROPE_HALF, dtype=F32) / ROPE_HALF)
    z = jnp.zeros((ROPE_HALF,), F32)
    freq = jnp.concatenate([inv_freq, z, inv_freq, z])
    sign = jnp.concatenate([-jnp.ones((ROPE_HALF,), F32), z, jnp.ones((ROPE_HALF,), F32), z])
    return _rows8([freq[None], sign[None]])


def _column_shards(a):
    return a.reshape(a.shape[0], 4, a.shape[1] // 4).transpose(1, 0, 2)


def _from_column_shards(a):
    return a.transpose(1, 0, 2).reshape(a.shape[1], 4 * a.shape[2])


_ANY = pl.BlockSpec(memory_space=pl.ANY)
_OTHER_CHIPS = ((1, 0), (0, 1), (1, 1))


def _here():
    return lax.axis_index("x"), lax.axis_index("y"), lax.axis_index("c")


def _flip(v, bit):
    return 1 - v if bit else v


def _remote(src, dst, send_sems, recv_sems, k, to):
    return pltpu.make_async_remote_copy(src_ref=src, dst_ref=dst, send_sem=send_sems.at[k], recv_sem=recv_sems.at[k],
                                        device_id=to, device_id_type=MESH)


def _half_of(ref, k, shape):
    r, c = shape
    if (r // 2) % 16 == 0:
        return ref.at[pl.ds(pl.multiple_of(k * (r // 2), 16), r // 2)]
    if (c // 2) % 128 == 0:
        return ref.at[:, pl.ds(pl.multiple_of(k * (c // 2), 128), c // 2)]
    return None


def _gather_copies(srcs, dsts, send_sems, recv_sems, local_sems):
    x, y, c = _here()
    slot, sibling, n = 2 * x + y, (x, y, 1 - c), len(srcs)
    starts, passes, waits = [], [], []
    for i, (src, dst) in enumerate(zip(srcs, dsts)):
        own = pltpu.make_async_copy(src, dst.at[slot], local_sems.at[i])
        starts.append(own.start)
        waits.append(own.wait)
        halves = _half_of(src, c, src.shape) is not None
        for j, (fx, fy) in enumerate(_OTHER_CHIPS):
            cx, cy = _flip(x, fx), _flip(y, fy)
            there = dst.at[2 * cx + cy]
            if halves:
                push = _remote(_half_of(src, c, src.shape), _half_of(dst.at[slot], c, src.shape), send_sems, recv_sems, 3 * i + j, (cx, cy, c))
                landed, other = _half_of(there, c, src.shape), _half_of(there, 1 - c, src.shape)
                onward = _remote(landed, landed, send_sems, recv_sems, 3 * n + 3 * i + j, sibling)
                passes += [_remote(landed, landed, send_sems, recv_sems, 3 * i + j, (cx, cy, c)).wait_recv, onward.start]
                waits += [_remote(other, other, send_sems, recv_sems, 3 * n + 3 * i + j, sibling).wait_recv, onward.wait_send]
            else:
                push = _remote(src, dst.at[slot], send_sems, recv_sems, 3 * i + j, (cx, cy, c))
                waits.append(_remote(there, there, send_sems, recv_sems, 3 * i + j, (cx, cy, c)).wait_recv)
            starts.append(push.start)
            waits.append(push.wait_send)
    return starts, passes, waits


def _gather_scratch(n):
    return [pltpu.SemaphoreType.DMA((6 * n,)), pltpu.SemaphoreType.DMA((6 * n,)), pltpu.SemaphoreType.DMA((n,))]


def _all_gather(shards, name):
    ns = len(shards)

    def body(*refs):
        starts, passes, waits = _gather_copies(refs[:ns], refs[ns:2 * ns], *refs[2 * ns:])
        for call in starts + passes + waits:
            call()

    return pl.pallas_call(
        body, in_specs=[_ANY] * ns, out_specs=[_ANY] * ns, out_shape=[_sds((4,) + s.shape, s.dtype) for s in shards],
        scratch_shapes=_gather_scratch(ns), name=name,
    )(*shards)


def _by_lanes(shape):
    return (shape[-2] // 2) % 16 != 0


def _scattered_shape(p):
    r, c = p.shape[1:]
    return _sds((8, r, c // 2) if _by_lanes(p.shape) else (8, r // 2, c), p.dtype)


def _scatter_copies(srcs, dsts, send_sems, recv_sems, local_sems, whole=0):
    x, y, c = _here()
    me = 4 * x + 2 * y + c
    starts, waits = [], []
    for i, (src, dst) in enumerate(zip(srcs, dsts)):
        def piece(px, py, pc, src=src, entire=i >= len(srcs) - whole):
            if entire:
                return src
            if _by_lanes(src.shape):
                half = src.shape[2] // 2
                return src.at[2 * px + py, :, pl.ds(pl.multiple_of(pc * half, 128), half)]
            half = src.shape[1] // 2
            return src.at[2 * px + py, pl.ds(pl.multiple_of(pc * half, 16), half)]

        own = pltpu.make_async_copy(piece(x, y, c), dst.at[me], local_sems.at[i])
        starts.append(own.start)
        waits.append(own.wait)
        for k in range(1, 8):
            px, py, pc = _flip(x, k & 4), _flip(y, k & 2), _flip(c, k & 1)
            push = _remote(piece(px, py, pc), dst.at[me], send_sems, recv_sems, 7 * i + k - 1, (px, py, pc))
            landed = dst.at[4 * px + 2 * py + pc]
            starts.append(push.start)
            waits += [_remote(landed, landed, send_sems, recv_sems, 7 * i + k - 1, (px, py, pc)).wait_recv, push.wait_send]
    return starts, waits


def _scatter_scratch(n):
    return [pltpu.SemaphoreType.DMA((7 * n,)), pltpu.SemaphoreType.DMA((7 * n,)), pltpu.SemaphoreType.DMA((n,))]


def _swapped_shape(half):
    r, c = half.shape
    return _sds((r, 2 * c) if _by_lanes((r, 2 * c)) else (2, r, c), half.dtype)


def _swap_copies(srcs, dsts, send_sems, recv_sems, local_sems):
    x, y, c = _here()
    sibling = (x, y, 1 - c)
    starts, waits = [], []
    for i, (src, dst) in enumerate(zip(srcs, dsts)):
        if len(dst.shape) == 2:
            lanes = src.shape[1]
            mine, other = (dst.at[:, pl.ds(pl.multiple_of(k * lanes, 128), lanes)] for k in (c, 1 - c))
        else:
            mine, other = dst.at[c], dst.at[1 - c]
        own = pltpu.make_async_copy(src, mine, local_sems.at[i])
        push = _remote(src, mine, send_sems, recv_sems, i, sibling)
        starts += [own.start, push.start]
        waits += [_remote(other, other, send_sems, recv_sems, i, sibling).wait_recv, push.wait_send, own.wait]
    return starts, waits


def _swap_scratch(n):
    return [pltpu.SemaphoreType.DMA((n,)), pltpu.SemaphoreType.DMA((n,)), pltpu.SemaphoreType.DMA((n,))]


def _exchange_halves(halves, wholes):
    ns, nw = len(halves), len(wholes)

    def body(*refs):
        srcs, dsts = refs[:ns + nw], refs[ns + nw:2 * (ns + nw)]
        sems = refs[2 * (ns + nw):]
        starts, waits = _swap_copies(srcs[:ns], dsts[:ns], *sems[:3])
        more = _scatter_copies(srcs[ns:], dsts[ns:], *sems[3:], whole=nw)
        for call in starts + more[0] + waits + more[1]:
            call()

    return pl.pallas_call(
        body, in_specs=[_ANY] * (ns + nw), out_specs=[_ANY] * (ns + nw),
        out_shape=[_swapped_shape(h) for h in halves] + [_sds((8,) + a.shape, a.dtype) for a in wholes],
        scratch_shapes=_swap_scratch(ns) + _scatter_scratch(nw), name="exchange_halves",
    )(*halves, *wholes)


def _row_tile(rows, row_bytes, budget):
    tr = rows
    while tr * row_bytes > budget and tr % 16 == 0:
        tr //= 2
    return tr


def _sum_slots(parts, name):
    _, rows, cols = parts.shape
    tr = _row_tile(rows, 8 * cols * 4, 2 * 1024 * 1024)

    def body(p_ref, o_ref):
        acc = p_ref[0].astype(F32)
        for d in range(1, 8):
            acc = acc + p_ref[d].astype(F32)
        o_ref[...] = acc

    return pl.pallas_call(
        body, grid=(rows // tr,), in_specs=[pl.BlockSpec((8, tr, cols), lambda i: (0, i, 0))],
        out_specs=pl.BlockSpec((tr, cols), lambda i: (i, 0)), out_shape=_sds((rows, cols), F32),
        compiler_params=_params(("parallel",)), name=name,
    )(parts)


def _adam_update(w, g, m, v):
    m = ADAM_B1 * m + (1.0 - ADAM_B1) * g
    v = ADAM_B2 * v + (1.0 - ADAM_B2) * jnp.square(g)
    m_hat = m / (1.0 - ADAM_B1 ** ADAM_STEP)
    v_hat = v / (1.0 - ADAM_B2 ** ADAM_STEP)
    return -ADAM_LR * (m_hat / (jnp.sqrt(v_hat) + ADAM_EPS) + ADAM_WD * w), m, v


SMALL_ROWS = {"attn_norm_w": 0, "mlp_norm_w": 1, "q_lat_norm_w": 2, "kv_lat_norm_w": 3, "q_norm_w": 4, "k_norm_w": 5,
              "mla_out_norm_w": 6, "gdn_norm_w": 10, "a_log": 11, "dt_bias": 12}
LOSS_ROW = 13
SMALL_SHAPE = (16, 1024)


def _pack_small_partials(d_attn_nw, d_mlp_nw, d_ln, d_qk_nw, d_mix_nw, d_scal, conv_parts, sq):
    D = d_attn_nw.shape[1]

    def body(an_ref, mn_ref, ln_ref, qk_ref, mix_ref, sc_ref, cq_ref, ck_ref, cv_ref, sq_ref, a_ref, c_ref):
        a_ref[...] = jnp.zeros_like(a_ref)
        a_ref[0:1, :D] = an_ref[...]
        a_ref[1:2, :D] = mn_ref[...]
        a_ref[2:4, :LORA] = ln_ref[...]
        for row, base in ((4, 0), (5, 2)):
            rope = qk_ref[base + 1:base + 2, :]
            a_ref[row:row + 1, :QK_DIM] = jnp.concatenate(
                [qk_ref[base:base + 1, :], rope[:, :ROPE_HALF], rope[:, 2 * ROPE_HALF:3 * ROPE_HALF]], axis=1)
        a_ref[6:6 + HEADS, :HEAD_DIM] = mix_ref[0:HEADS, :]
        a_ref[10:11, :HEAD_DIM] = mix_ref[HEADS:HEADS + 1, :]
        a_ref[11:13, :128] = sc_ref[0:2, :]
        a_ref[LOSS_ROW:LOSS_ROW + 1, :128] = jnp.zeros((1, 128), F32) + jnp.sum(sq_ref[:, 0:1, 0:1]) * (0.5 / D)
        c_ref[...] = jnp.concatenate([cq_ref[...], ck_ref[...], cv_ref[...]], axis=1)

    return pl.pallas_call(
        body, out_shape=[_sds(SMALL_SHAPE, F32), _sds((CONV_TAPS, GQKV_W), F32)], name="pack_small_partials",
    )(d_attn_nw, d_mlp_nw, d_ln, d_qk_nw, d_mix_nw, d_scal, *conv_parts, sq)


def _adamw_small(parts, conv_parts, w, m, v):
    names = tuple(SMALL_ROWS) + ("conv_w",)
    cols = w["conv_w"].shape[2]

    def body(*refs):
        p_ref, c_ref = refs[:2]
        n = len(names)
        w_refs, m_refs, v_refs = (dict(zip(names, refs[2 + k * n:2 + (k + 1) * n])) for k in range(3))
        loss_ref = refs[2 + 3 * n]
        out = [dict(zip(names, refs[3 + (3 + k) * n:3 + (4 + k) * n])) for k in range(4)]
        acc_ref, cacc_ref = refs[3 + 7 * n:]
        acc, cacc = p_ref[0], c_ref[0]
        for d in range(1, 8):
            acc, cacc = acc + p_ref[d], cacc + c_ref[d]
        acc_ref[...] = acc
        cacc_ref[...] = cacc
        loss_ref[...] = acc_ref[LOSS_ROW:LOSS_ROW + 1, 0:1]
        chip = 2 * lax.axis_index("x") + lax.axis_index("y")
        for name in names:
            shape = w_refs[name].shape
            if name == "conv_w":
                g = sum(jnp.where(chip == s, cacc_ref[:, s * cols:(s + 1) * cols], 0.0) for s in range(4))[None]
            else:
                row = SMALL_ROWS[name]
                g = acc_ref[row:row + math.prod(shape[:-1]), 0:shape[-1]].reshape(shape)
            delta, new_m, new_v = _adam_update(w_refs[name][...], g, m_refs[name][...], v_refs[name][...])
            for ref, val in zip((o[name] for o in out), (g, delta, new_m, new_v)):
                ref[...] = val

    ins = [x[n] for x in (w, m, v) for n in names]
    shapes = [_sds(w[n].shape, F32) for n in names]
    outs = pl.pallas_call(
        body, out_shape=[_sds((1, 1), F32)] + shapes * 4,
        scratch_shapes=[pltpu.VMEM(parts.shape[1:], F32), pltpu.VMEM(conv_parts.shape[1:], F32)], name="adamw_small",
    )(parts, conv_parts, *ins)
    n = len(names)
    return (outs[0],) + tuple(dict(zip(names, outs[1 + k * n:1 + (k + 1) * n])) for k in range(4))


def _adamw(w, g, m, v, name):
    rows, cols = w.shape[0], w.shape[-1]
    if w.ndim == 3:
        tr = max(d for d in range(1, rows + 1) if rows % d == 0 and d * 8 * cols * 4 * 14 <= VMEM_LIMIT // 2)
    else:
        tr = _row_tile(rows, 7 * cols * 4, 4 * 1024 * 1024)

    def body(w_ref, g_ref, m_ref, v_ref, d_ref, mo_ref, vo_ref):
        d_ref[...], mo_ref[...], vo_ref[...] = _adam_update(w_ref[...], g_ref[...], m_ref[...], v_ref[...])

    block = (tr,) + w.shape[1:]
    spec = pl.BlockSpec(block, lambda i: (i,) + (0,) * (len(block) - 1))
    return pl.pallas_call(
        body, grid=(rows // tr,), in_specs=[spec] * 4, out_specs=[spec] * 3, out_shape=[_sds(w.shape, F32)] * 3,
        compiler_params=_params(("parallel",)), name=name,
    )(w, g, m, v)


def kernel(x, positions, attn_norm_w, w_in, q_lat_norm_w, w_uq, kv_lat_norm_w, w_ukv, q_norm_w, k_norm_w, mla_out_norm_w, conv_w, a_log, dt_bias, gdn_norm_w, w_out, mlp_norm_w, w_up, w_down, loss_target, m_attn_norm_w, m_w_in, m_q_lat_norm_w, m_w_uq, m_kv_lat_norm_w, m_w_ukv, m_q_norm_w, m_k_norm_w, m_mla_out_norm_w, m_conv_w, m_a_log, m_dt_bias, m_gdn_norm_w, m_w_out, m_mlp_norm_w, m_w_up, m_w_down, v_attn_norm_w, v_w_in, v_q_lat_norm_w, v_w_uq, v_kv_lat_norm_w, v_w_ukv, v_q_norm_w, v_k_norm_w, v_mla_out_norm_w, v_conv_w, v_a_log, v_dt_bias, v_gdn_norm_w, v_w_out, v_mlp_norm_w, v_w_up, v_w_down):
    w = dict(zip(WEIGHTS, (attn_norm_w, w_in, q_lat_norm_w, w_uq, kv_lat_norm_w, w_ukv, q_norm_w, k_norm_w, mla_out_norm_w, conv_w,
                           a_log, dt_bias, gdn_norm_w, w_out, mlp_norm_w, w_up, w_down)))
    m = dict(zip(WEIGHTS, (m_attn_norm_w, m_w_in, m_q_lat_norm_w, m_w_uq, m_kv_lat_norm_w, m_w_ukv, m_q_norm_w, m_k_norm_w,
                           m_mla_out_norm_w, m_conv_w, m_a_log, m_dt_bias, m_gdn_norm_w, m_w_out, m_mlp_norm_w, m_w_up, m_w_down)))
    v = dict(zip(WEIGHTS, (v_attn_norm_w, v_w_in, v_q_lat_norm_w, v_w_uq, v_kv_lat_norm_w, v_w_ukv, v_q_norm_w, v_k_norm_w,
                           v_mla_out_norm_w, v_conv_w, v_a_log, v_dt_bias, v_gdn_norm_w, v_w_out, v_mlp_norm_w, v_w_up, v_w_down)))
    B, S, D = x.shape
    T = B * S
    x2, pos, target = x.reshape(T, D), positions.reshape(T, 1), loss_target.reshape(T, D)
    seq = lambda a: a.reshape(B, S, a.shape[-1])
    tok = lambda a: a.reshape(T, a.shape[-1])
    local = {n: w[n][0] for n in SHARDED}

    g_in, g_uq, g_ukv, g_conv = _all_gather([jnp.swapaxes(w_in, 1, 2)[0].astype(BF16), local["w_uq"].astype(BF16),
                                             local["w_ukv"].astype(BF16), local["conv_w"]], "gather_first_weights")
    w_in_p = _widen_w_in_t(g_in.reshape(-1, D))
    w_mla = _stack_mla(_from_column_shards(g_uq), _from_column_shards(g_ukv))
    conv_full = _from_column_shards(g_conv)
    ln_w = jnp.concatenate([q_lat_norm_w, kv_lat_norm_w], axis=0)
    qk_nw = _qk_norm_rows(q_norm_w, k_norm_w)
    rope_rows = _rope_rows()
    scal = _rows8([jnp.pad(a_log, ((0, 0), (0, 128 - HEADS))), jnp.pad(dt_bias, ((0, 0), (0, 128 - HEADS)))])
    mix_nw = _rows8([mla_out_norm_w[0], gdn_norm_w])

    xn, lat, gqkv, gz, gab = _in_proj_fwd(x2, attn_norm_w, w_in_p)
    q, k, v_att = _mla_pre_fwd(lat, pos, ln_w, w_mla, qk_nw, rope_rows)
    ao, lse, g_down = _attn_fwd(seq(q), seq(k), seq(v_att), [local["w_down"].astype(BF16)])
    gq, gk, gv = _gdn_pre_fwd(seq(gqkv), conv_full)
    go, states, g_out, w_up_b = _gdn_chunk_fwd(gq, gk, gv, seq(gab), scal, [local["w_out"].astype(BF16), local["w_up"].astype(BF16)])
    w_out_b = g_out.reshape(-1, D)
    w_down_b = g_down.reshape(-1, D)
    mix, h2 = _mix_fwd(tok(ao), tok(go), gz, mix_nw, w_out_b, x2)
    hn, act, dy, sq = _mlp_fwd(h2, mlp_norm_w, w_up_b, w_down_b, target)

    dh, d_up, d_mlp_nw = _mlp_bwd(h2, mlp_norm_w, act, w_up_b, w_down_b, dy)
    p_down = _wgrad(act, dy, "wgrad_down").reshape(4, -1, D)
    p_up = _wgrad(hn, d_up, "wgrad_up", column_shards=4)
    d_ao, d_go, d_gz, d_mix_nw = _mix_bwd(tok(ao), tok(go), gz, mix_nw, w_out_b, dh)
    p_out = _wgrad(mix, dh, "wgrad_out").reshape(4, -1, D)
    d_gq, d_gk, d_gv, d_gab, d_scal, s_up, s_out = _gdn_chunk_bwd(gq, gk, gv, seq(gab), scal, states, seq(d_go), [p_up, p_out])
    early = ("w_up", "w_out", "w_down")
    dxq, dxk, dxv, dcq, dck, dcv, g_up, g_out = _gdn_pre_bwd(seq(gqkv), conv_full, d_gq, d_gk, d_gv,
                                                             [_sum_slots(s_up, "sum_w_up"), _sum_slots(s_out, "sum_w_out")])
    dq, dk, dv, s_down = _attn_bwd(seq(q), seq(k), seq(v_att), ao, lse, seq(d_ao), [p_down])
    d_lat, d_ln, d_w_mla, d_qk_nw, g_down = _mla_pre_bwd(lat, pos, ln_w, w_mla, qk_nw, rope_rows, tok(dq), tok(dk), tok(dv),
                                                         [_sum_slots(s_down, "sum_w_down")])
    early_grads = [g_up, g_out, g_down]
    d_pieces = [d_lat, tok(dxq), tok(dxk), tok(dxv), d_gz, tok(d_gab)]
    p_in = _narrow_w_in_t(_wgrad_pieces(d_pieces, xn, "wgrad_in")).reshape(4, -1, D)
    p_uq, p_ukv = (_column_shards(a).astype(BF16) for a in _unstack_mla(d_w_mla))
    grad_x2, d_attn_nw, s_in, s_uq, s_ukv = _in_proj_bwd(d_pieces, w_in_p, x2, attn_norm_w, dh, [p_in, p_uq, p_ukv])
    small_buf, conv_buf = _pack_small_partials(d_attn_nw, d_mlp_nw, d_ln, d_qk_nw, d_mix_nw, d_scal, (dcq, dck, dcv), sq)

    late = ("w_in", "w_uq", "w_ukv")
    *late_grads, s_small, s_conv = _exchange_halves([_sum_slots(s, "sum_" + n) for n, s in zip(late, (s_in, s_uq, s_ukv))],
                                                    [small_buf, conv_buf])
    names = early + late
    grad = {n: g.reshape(-1, g.shape[-1]) for n, g in zip(names, list(early_grads) + list(late_grads))}

    loss, g_small, delta, new_m, new_v = _adamw_small(s_small, s_conv, w, m, v)
    grad.update(g_small)
    for n in names:
        if n == "w_in":
            stored = lambda a: jnp.transpose(a, (2, 0, 1))
            outs = _adamw(stored(w[n]), grad[n][:, None, :], stored(m[n]), stored(v[n]), "adamw_" + n)
            grad[n], delta[n], new_m[n], new_v[n] = (jnp.transpose(a, (1, 2, 0)) for a in (grad[n][:, None, :], *outs))
        else:
            delta[n], new_m[n], new_v[n] = _adamw(local[n], grad[n], m[n][0], v[n][0], "adamw_" + n)
    def in_order(d):
        return [d[n].reshape(w[n].shape) for n in WEIGHTS]

    return (loss.reshape(()), grad_x2.reshape(B, S, D), *in_order(grad), *in_order(delta), *in_order(new_m), *in_order(new_v))
```

```python
import functools
import math

import jax
import jax.numpy as jnp
from jax import lax
from jax.experimental import pallas as pl
from jax.experimental.pallas import tpu as pltpu

F32 = jnp.float32
BF16 = jnp.bfloat16
MESH = pl.DeviceIdType.MESH

EPS = 1e-6
HEADS = 4
HEAD_DIM = 128
ROPE_DIM = 64
ROPE_HALF = 32
QK_DIM = 192
QK_PAD = 256
LORA = 256
CHUNK = 64
CONV_TAPS = 4
ROPE_THETA = 10000.0
ATTN_SCALE = QK_DIM ** -0.5

LAT_W = 640
GQKV_W = 3 * HEADS * HEAD_DIM
GZ_W = HEADS * HEAD_DIM
GAB_W = 128
PROJ_SPLITS = ((0, LAT_W), (LAT_W, LAT_W + GQKV_W), (LAT_W + GQKV_W, LAT_W + GQKV_W + GZ_W),
               (LAT_W + GQKV_W + GZ_W, LAT_W + GQKV_W + GZ_W + GAB_W))
PROJ_W = PROJ_SPLITS[-1][1]

ADAM_LR = 0.001
ADAM_B1 = 0.9
ADAM_B2 = 0.999
ADAM_EPS = 1e-08
ADAM_WD = 0.01
ADAM_STEP = 10

TOKEN_TILE = 512
MLP_TOKEN_TILE = 512
FF_TILE = 1024
ATTN_TILE = 512
ATTN_HEADS_PER_STEP = 2
WGRAD_OUT_BYTES = 8 * 1024 * 1024
VMEM_LIMIT = 48 * 1024 * 1024

SHARDED = ("w_in", "w_uq", "w_ukv", "conv_w", "w_out", "w_up", "w_down")
WEIGHTS = ("attn_norm_w", "w_in", "q_lat_norm_w", "w_uq", "kv_lat_norm_w", "w_ukv", "q_norm_w", "k_norm_w", "mla_out_norm_w",
           "conv_w", "a_log", "dt_bias", "gdn_norm_w", "w_out", "mlp_norm_w", "w_up", "w_down")


def _sds(shape, dtype):
    return jax.ShapeDtypeStruct(shape, dtype)


def _params(semantics):
    return pltpu.CompilerParams(dimension_semantics=semantics, vmem_limit_bytes=VMEM_LIMIT)


def _block(n):
    for b in (512, 256, 128):
        if n % b == 0:
            return b
    return n


def _dg(a, b, ca, cb, prec):
    lead = a.ndim - 2
    batch = (tuple(range(lead)),) * 2
    return lax.dot_general(a, b, (((ca + lead,), (cb + lead,)), batch), precision=prec, preferred_element_type=F32)


def _split_bf16(a):
    hi = a.astype(BF16)
    return hi, (a - hi.astype(F32)).astype(BF16)


def _dot_bf16(a, b, ca, cb):
    return _dg(a.astype(BF16), b.astype(BF16), ca, cb, None)


def _dot_bf16x3(a, b, ca, cb):
    a_hi, a_lo = _split_bf16(a)
    b_hi, b_lo = _split_bf16(b)
    lead = a.ndim - 2
    return _dg(jnp.concatenate([a_hi, a_hi, a_lo], axis=ca + lead), jnp.concatenate([b_hi, b_lo, b_hi], axis=cb + lead), ca, cb, None)


def _matmul_family(dot):
    def nn_raw(a, b):
        return dot(a, b, 1, 0)

    def nt_raw(a, b):
        return dot(a, b, 1, 1)

    def tn_raw(a, b):
        return dot(a, b, 0, 0)

    @jax.custom_vjp
    def nn(a, b):
        return nn_raw(a, b)

    nn.defvjp(lambda a, b: (nn_raw(a, b), (a, b)), lambda r, g: (nt_raw(g, r[1]), tn_raw(r[0], g)))

    @jax.custom_vjp
    def nt(a, b):
        return nt_raw(a, b)

    nt.defvjp(lambda a, b: (nt_raw(a, b), (a, b)), lambda r, g: (nn_raw(g, r[1]), tn_raw(g, r[0])))

    @jax.custom_vjp
    def tn(a, b):
        return tn_raw(a, b)

    tn.defvjp(lambda a, b: (tn_raw(a, b), (a, b)), lambda r, g: (nt_raw(r[1], g), nn_raw(r[0], g)))
    return nn, nt, tn


_bf_nn, _bf_nt, _bf_tn = _matmul_family(_dot_bf16)
_hi_nn, _hi_nt, _hi_tn = _matmul_family(_dot_bf16x3)


def _lower_powers(lmat):
    powers = []
    while 2 ** (len(powers) + 1) < lmat.shape[-1]:
        powers.append(_dot_bf16x3(powers[-1] if powers else lmat, powers[-1] if powers else lmat, 1, 0))
    return powers


@jax.custom_vjp
def _unit_lower_solve(lmat, rhs):
    return _unit_lower_solve_fwd(lmat, rhs)[0]


def _unit_lower_solve_fwd(lmat, rhs):
    powers = _lower_powers(lmat)
    x = rhs - _dot_bf16x3(lmat, rhs, 1, 0)
    for p in powers:
        x = x + _dot_bf16x3(p, x, 1, 0)
    return x, (lmat, powers, x)


def _unit_lower_solve_bwd(res, g):
    lmat, powers, x = res
    y = g - _dot_bf16x3(lmat, g, 0, 0)
    for p in powers:
        y = y + _dot_bf16x3(p, y, 0, 0)
    return -_dot_bf16x3(y, x, 1, 1), y


_unit_lower_solve.defvjp(_unit_lower_solve_fwd, _unit_lower_solve_bwd)


@jax.custom_vjp
def _lane_halves(x):
    n = x.shape[-1] // 2
    return x[..., :n], x[..., n:]


_lane_halves.defvjp(lambda x: (_lane_halves(x), None), lambda _, g: (jnp.concatenate(g, axis=-1),))


@jax.custom_vjp
def _row_halves(x):
    n = x.shape[-2] // 2
    return x[..., :n, :], x[..., n:, :]


_row_halves.defvjp(lambda x: (_row_halves(x), None), lambda _, g: (jnp.concatenate(g, axis=-2),))


@jax.custom_vjp
def _swap_halves(t):
    return pltpu.roll(t, 64, 1)


_swap_halves.defvjp(lambda t: (pltpu.roll(t, 64, 1), None), lambda _, g: (pltpu.roll(g, 64, 1),))


@functools.partial(jax.custom_vjp, nondiff_argnums=(2,))
def _shift_rows(x, keep, s):
    return pltpu.roll(x, s, 0) * keep


def _shift_rows_fwd(x, keep, s):
    return pltpu.roll(x, s, 0) * keep, keep


def _shift_rows_bwd(s, keep, g):
    return pltpu.roll(g * keep, keep.shape[0] - s, 0), jnp.zeros_like(keep)


_shift_rows.defvjp(_shift_rows_fwd, _shift_rows_bwd)


def _sigmoid(x):
    return 0.5 * jnp.tanh(0.5 * x) + 0.5


def _softplus(x):
    return jnp.maximum(x, 0.0) + jnp.log(1.0 + jnp.exp(jnp.minimum(x, -x)))


def _silu(x):
    return x * _sigmoid(x)


def _rms(x, w, n=None):
    n = x.shape[-1] if n is None else n
    r = lax.rsqrt(jnp.sum(x * x, axis=-1, keepdims=True) * (1.0 / n) + EPS)
    return x * r * w


def _rope(t, cos_f, sin_f):
    return t * cos_f + _swap_halves(t) * sin_f


def _rope_tables(pos_col, freq_row, sign_row):
    ang = pos_col.astype(F32) * freq_row
    return jnp.cos(ang), jnp.sin(ang) * sign_row


def _onehot_row(lane):
    return (lax.broadcasted_iota(jnp.int32, (1, 128), 1) == lane).astype(F32)


def _row_spec(tm, w):
    return pl.BlockSpec((tm, w), lambda i: (i, 0))


def _const_spec(shape):
    return pl.BlockSpec(shape, lambda *_: (0,) * len(shape))


def _in_proj_fwd(x2, w_an, w_in_p):
    T, D = x2.shape
    tm = min(TOKEN_TILE, T)

    def body(x_ref, wn_ref, w_ref, xn_ref, lat_ref, gqkv_ref, gz_ref, gab_ref):
        x = x_ref[...]
        r = lax.rsqrt(jnp.mean(x * x, axis=-1, keepdims=True) + EPS)
        xn = (x * r * wn_ref[...]).astype(BF16)
        xn_ref[...] = xn
        for ref, (a, b) in zip((lat_ref, gqkv_ref, gz_ref, gab_ref), PROJ_SPLITS):
            ref[...] = _dg(xn, w_ref[a:b, :], 1, 1, None)

    widths = [b - a for a, b in PROJ_SPLITS]
    return pl.pallas_call(
        body, grid=(T // tm,),
        in_specs=[_row_spec(tm, D), _const_spec((1, D)), _const_spec((PROJ_W, D))],
        out_specs=[_row_spec(tm, D)] + [_row_spec(tm, w) for w in widths],
        out_shape=[_sds((T, D), BF16)] + [_sds((T, w), F32) for w in widths],
        compiler_params=_params(("parallel",)), name="in_proj_fwd",
    )(x2, w_an, w_in_p)


def _in_proj_bwd(pieces, w_in_p, x2, w_an, dh, partials):
    T, D = x2.shape
    tm = min(TOKEN_TILE, T)
    widths = [p.shape[1] for p in pieces]
    starts = [sum(widths[:i]) for i in range(len(widths))]
    assert sum(widths) == PROJ_W
    npc, ns = len(pieces), len(partials)

    def body(*refs):
        piece_refs = refs[:npc]
        w_ref, x_ref, wn_ref, dh_ref = refs[npc:npc + 4]
        src_refs = refs[npc + 4:npc + 4 + ns]
        dx_ref, dwn_ref = refs[npc + 4 + ns:npc + 6 + ns]
        dst_refs = refs[npc + 6 + ns:npc + 6 + 2 * ns]
        sems = refs[npc + 6 + 2 * ns:]

        @pl.when(pl.program_id(0) == 0)
        def _():
            for start in _scatter_copies(src_refs, dst_refs, *sems)[0]:
                start()
            dwn_ref[...] = jnp.zeros_like(dwn_ref)

        dxn = jnp.zeros((tm, D), F32)
        for ref, a, width in zip(piece_refs, starts, widths):
            dxn += _dg(ref[...], w_ref[a:a + width, :], 1, 0, None)
        _, pull = jax.vjp(_rms, x_ref[...], wn_ref[...])
        dx, dwn = pull(dxn)
        dx_ref[...] = dx + dh_ref[...]
        dwn_ref[...] += dwn

        @pl.when(pl.program_id(0) == T // tm - 1)
        def _():
            for wait in _scatter_copies(src_refs, dst_refs, *sems)[1]:
                wait()

    return pl.pallas_call(
        body, grid=(T // tm,),
        in_specs=[_row_spec(tm, w) for w in widths] + [_const_spec((PROJ_W, D)), _row_spec(tm, D), _const_spec((1, D)),
                                                       _row_spec(tm, D)] + [_ANY] * ns,
        out_specs=[_row_spec(tm, D), _const_spec((1, D))] + [_ANY] * ns,
        out_shape=[_sds((T, D), F32), _sds((1, D), F32)] + [_scattered_shape(p) for p in partials],
        scratch_shapes=_scatter_scratch(ns),
        compiler_params=_params(("arbitrary",)), name="in_proj_bwd",
    )(*pieces, w_in_p, x2, w_an, dh, *partials)


def _wgrad_pieces(pieces, b, name):
    T, k2 = b.shape
    tt = min(TOKEN_TILE, T)
    widths = [p.shape[1] for p in pieces]
    starts = [sum(widths[:i]) for i in range(len(widths))]
    k1 = sum(widths)

    def body(*refs):
        piece_refs, (b_ref, o_ref, acc_ref) = refs[:len(pieces)], refs[len(pieces):]
        t = pl.program_id(0)

        @pl.when(t == 0)
        def _():
            acc_ref[...] = jnp.zeros_like(acc_ref)

        bt = b_ref[...].astype(BF16)
        for ref, r0, width in zip(piece_refs, starts, widths):
            acc_ref[r0:r0 + width, :] += jnp.dot(ref[...].T, bt, preferred_element_type=F32)

        @pl.when(t == T // tt - 1)
        def _():
            o_ref[...] = acc_ref[...].astype(o_ref.dtype)

    return pl.pallas_call(
        body, grid=(T // tt,),
        in_specs=[pl.BlockSpec((tt, w), lambda t: (t, 0)) for w in widths] + [pl.BlockSpec((tt, k2), lambda t: (t, 0))],
        out_specs=_const_spec((k1, k2)), out_shape=_sds((k1, k2), BF16), scratch_shapes=[pltpu.VMEM((k1, k2), F32)],
        compiler_params=_params(("arbitrary",)), name=name,
    )(*pieces, b)


def _wgrad(a, b, name, column_shards=1, out_dtype=BF16, lane_halves=False):
    T, k1 = a.shape
    k2 = b.shape[1]
    per_shard = k2 // column_shards
    tt = min(TOKEN_TILE, T)
    b1 = k1
    while b1 * k2 * 4 > WGRAD_OUT_BYTES and b1 % 256 == 0:
        b1 //= 2
    step = _block(per_shard)

    def body(a_ref, b_ref, *rest):
        o_refs, acc_ref = rest[:-1], rest[-1]
        t = pl.program_id(1)

        @pl.when(t == 0)
        def _():
            acc_ref[...] = jnp.zeros_like(acc_ref)

        a_t = a_ref[...].astype(BF16).T
        for c0 in range(0, k2, step):
            part = jnp.dot(a_t, b_ref[:, c0:c0 + step].astype(BF16), preferred_element_type=F32)
            if column_shards == 1:
                acc_ref[:, c0:c0 + step] += part
            else:
                acc_ref[c0 // per_shard, :, c0 % per_shard:c0 % per_shard + step] += part

        @pl.when(t == T // tt - 1)
        def _():
            if lane_halves:
                for k, o_ref in enumerate(o_refs):
                    o_ref[...] = acc_ref[:, k * (k2 // 2):(k + 1) * (k2 // 2)].astype(o_ref.dtype)
            else:
                o_refs[0][...] = acc_ref[...].astype(o_refs[0].dtype)

    if lane_halves:
        block = (b1, k2)
        out_spec = [pl.BlockSpec((b1, k2 // 2), lambda i, t: (i, 0))] * 2
        out_shape = [_sds((k1, k2 // 2), out_dtype)] * 2
    elif column_shards == 1:
        block, out_spec, out_shape = (b1, k2), pl.BlockSpec((b1, k2), lambda i, t: (i, 0)), _sds((k1, k2), out_dtype)
    else:
        block = (column_shards, b1, per_shard)
        out_spec, out_shape = pl.BlockSpec(block, lambda i, t: (0, i, 0)), _sds((column_shards, k1, per_shard), out_dtype)
    return pl.pallas_call(
        body, grid=(k1 // b1, T // tt),
        in_specs=[pl.BlockSpec((tt, b1), lambda i, t: (t, i)), pl.BlockSpec((tt, k2), lambda i, t: (t, 0))],
        out_specs=out_spec, out_shape=out_shape, scratch_shapes=[pltpu.VMEM(block, F32)],
        compiler_params=_params(("parallel", "arbitrary")), name=name,
    )(a, b)


def _mla_pre_fn(q_lat, kv_lat, kpe, ln_q, ln_kv, w_list, qn_n, qn_p, kn_n, kn_p, cos_f, sin_f):
    qn = _rms(q_lat, ln_q)
    kvn = _rms(kv_lat, ln_kv)
    kp = _rope(_rms(kpe, kn_p, ROPE_DIM), cos_f, sin_f)
    outs = []
    for h in range(HEADS):
        outs.append(_rms(_bf_nn(qn, w_list[h]), qn_n))
        outs.append(_rope(_rms(_bf_nn(qn, w_list[HEADS + h]), qn_p, ROPE_DIM), cos_f, sin_f))
        outs.append(_rms(_bf_nn(kvn, w_list[2 * HEADS + h]), kn_n))
        outs.append(_bf_nn(kvn, w_list[3 * HEADS + h]))
    return tuple(outs) + (kp,)


def _mla_pre_operands(lat_ref, pos_ref, ln_ref, w_ref, nw_ref, rope_ref):
    cos_f, sin_f = _rope_tables(pos_ref[...], rope_ref[0:1, :], rope_ref[1:2, :])
    diff = (lat_ref[:, 0:LORA], lat_ref[:, LORA:2 * LORA], lat_ref[:, 2 * LORA:LAT_W], ln_ref[0:1, :], ln_ref[1:2, :],
            [w_ref[i].astype(F32) for i in range(4 * HEADS)], nw_ref[0:1, :], nw_ref[1:2, :], nw_ref[2:3, :], nw_ref[3:4, :])
    return diff, cos_f, sin_f


def _mla_pre_fwd(lat, pos, ln_w, w_mla, nw, rope_rows):
    T = lat.shape[0]
    tm = min(TOKEN_TILE, T)

    def body(lat_ref, pos_ref, ln_ref, w_ref, nw_ref, rope_ref, q_ref, k_ref, v_ref):
        diff, cos_f, sin_f = _mla_pre_operands(lat_ref, pos_ref, ln_ref, w_ref, nw_ref, rope_ref)
        outs = _mla_pre_fn(*diff, cos_f, sin_f)
        kp = outs[-1].astype(BF16)
        for h in range(HEADS):
            q_n, q_p, k_n, v = outs[4 * h:4 * h + 4]
            q_ref[:, h * QK_PAD:h * QK_PAD + HEAD_DIM] = q_n.astype(BF16)
            q_ref[:, h * QK_PAD + HEAD_DIM:(h + 1) * QK_PAD] = q_p.astype(BF16)
            k_ref[:, h * QK_PAD:h * QK_PAD + HEAD_DIM] = k_n.astype(BF16)
            k_ref[:, h * QK_PAD + HEAD_DIM:(h + 1) * QK_PAD] = kp
            v_ref[:, h * HEAD_DIM:(h + 1) * HEAD_DIM] = v.astype(BF16)

    return pl.pallas_call(
        body, grid=(T // tm,),
        in_specs=[_row_spec(tm, LAT_W), _row_spec(tm, 1), _const_spec((2, LORA)), _const_spec((4 * HEADS, LORA, 128)),
                  _const_spec((8, 128)), _const_spec((8, 128))],
        out_specs=[_row_spec(tm, HEADS * QK_PAD), _row_spec(tm, HEADS * QK_PAD), _row_spec(tm, HEADS * HEAD_DIM)],
        out_shape=[_sds((T, HEADS * QK_PAD), BF16), _sds((T, HEADS * QK_PAD), BF16), _sds((T, HEADS * HEAD_DIM), BF16)],
        compiler_params=_params(("parallel",)), name="mla_pre_fwd",
    )(lat, pos, ln_w, w_mla, nw, rope_rows)


def _mla_pre_bwd(lat, pos, ln_w, w_mla, nw, rope_rows, dq, dk, dv, halves):
    T = lat.shape[0]
    tm = min(TOKEN_TILE, T)
    ns = len(halves)

    def body(*refs):
        lat_ref, pos_ref, ln_ref, w_ref, nw_ref, rope_ref, dq_ref, dk_ref, dv_ref = refs[:9]
        src_refs = refs[9:9 + ns]
        dlat_ref, dln_ref, dw_ref, dnw_ref = refs[9 + ns:13 + ns]
        dst_refs = refs[13 + ns:13 + 2 * ns]
        sems = refs[13 + 2 * ns:]

        @pl.when(pl.program_id(0) == 0)
        def _():
            for start in _swap_copies(src_refs, dst_refs, *sems)[0]:
                start()
            dln_ref[...] = jnp.zeros_like(dln_ref)
            dw_ref[...] = jnp.zeros_like(dw_ref)
            dnw_ref[...] = jnp.zeros_like(dnw_ref)

        diff, cos_f, sin_f = _mla_pre_operands(lat_ref, pos_ref, ln_ref, w_ref, nw_ref, rope_ref)
        _, pull = jax.vjp(lambda *a: _mla_pre_fn(*a, cos_f, sin_f), *diff)
        cts = []
        d_kp = jnp.zeros((tm, 128), F32)
        for h in range(HEADS):
            cts.append(dq_ref[:, h * QK_PAD:h * QK_PAD + HEAD_DIM])
            cts.append(dq_ref[:, h * QK_PAD + HEAD_DIM:(h + 1) * QK_PAD])
            cts.append(dk_ref[:, h * QK_PAD:h * QK_PAD + HEAD_DIM])
            cts.append(dv_ref[:, h * HEAD_DIM:(h + 1) * HEAD_DIM])
            d_kp += dk_ref[:, h * QK_PAD + HEAD_DIM:(h + 1) * QK_PAD]
        d_ql, d_kvl, d_kpe, d_lnq, d_lnkv, d_w, d_qn_n, d_qn_p, d_kn_n, d_kn_p = pull(tuple(cts) + (d_kp,))
        dlat_ref[:, 0:LORA] = d_ql.astype(BF16)
        dlat_ref[:, LORA:2 * LORA] = d_kvl.astype(BF16)
        dlat_ref[:, 2 * LORA:LAT_W] = d_kpe.astype(BF16)
        dln_ref[0:1, :] += d_lnq
        dln_ref[1:2, :] += d_lnkv
        for i in range(4 * HEADS):
            dw_ref[i] += d_w[i]
        for i, d in enumerate((d_qn_n, d_qn_p, d_kn_n, d_kn_p)):
            dnw_ref[i:i + 1, :] += d

        @pl.when(pl.program_id(0) == T // tm - 1)
        def _():
            for wait in _swap_copies(src_refs, dst_refs, *sems)[1]:
                wait()

    return pl.pallas_call(
        body, grid=(T // tm,),
        in_specs=[_row_spec(tm, LAT_W), _row_spec(tm, 1), _const_spec((2, LORA)), _const_spec((4 * HEADS, LORA, 128)),
                  _const_spec((8, 128)), _const_spec((8, 128)),
                  _row_spec(tm, HEADS * QK_PAD), _row_spec(tm, HEADS * QK_PAD), _row_spec(tm, HEADS * HEAD_DIM)] + [_ANY] * ns,
        out_specs=[_row_spec(tm, LAT_W), _const_spec((2, LORA)), _const_spec((4 * HEADS, LORA, 128)), _const_spec((8, 128))]
                  + [_ANY] * ns,
        out_shape=[_sds((T, LAT_W), BF16), _sds((2, LORA), F32), _sds((4 * HEADS, LORA, 128), F32), _sds((8, 128), F32)]
                  + [_swapped_shape(h) for h in halves],
        scratch_shapes=_swap_scratch(ns),
        compiler_params=_params(("arbitrary",)), name="mla_pre_bwd",
    )(lat, pos, ln_w, w_mla, nw, rope_rows, dq, dk, dv, *halves)


def _causal_mask(i, j, tq, tk):
    row = i * tq + lax.broadcasted_iota(jnp.int32, (tq, tk), 0)
    col = j * tk + lax.broadcasted_iota(jnp.int32, (tq, tk), 1)
    return col <= row


def _attn_fwd(q, k, v, shards):
    B, S, _ = q.shape
    t = min(ATTN_TILE, S)
    nq = S // t
    ns = len(shards)

    hp = ATTN_HEADS_PER_STEP
    qk = lambda h: slice(h * QK_PAD, (h + 1) * QK_PAD)
    vd = lambda h: slice(h * HEAD_DIM, (h + 1) * HEAD_DIM)

    def body(*refs):
        q_ref, k_ref, v_ref = refs[:3]
        src_refs = refs[3:3 + ns]
        o_ref, lse_ref = refs[3 + ns:5 + ns]
        dst_refs = refs[5 + ns:5 + 2 * ns]
        sems = refs[5 + 2 * ns:]
        b, g, i = pl.program_id(0), pl.program_id(1), pl.program_id(2)
        qb = [q_ref[0, :, qk(h)] for h in range(hp)]

        step_no = (b * (HEADS // hp) + g) * nq + i
        for phase, at in enumerate((0, (3 * B * (HEADS // hp) * nq) // 4)):
            @pl.when(step_no == at)
            def _(phase=phase):
                for call in _gather_copies(src_refs, dst_refs, *sems)[phase]:
                    call()

        def step(j, carry, diagonal):
            rows = pl.ds(pl.multiple_of(j * t, t), t)
            s = [_dg(qb[h], k_ref[0, rows, qk(h)], 1, 1, None) * ATTN_SCALE for h in range(hp)]
            if diagonal:
                keep = _causal_mask(0, 0, t, t)
                s = [jnp.where(keep, x, -1e30) for x in s]
            m_new = [jnp.maximum(carry[h][0], jnp.max(s[h], axis=-1, keepdims=True)) for h in range(hp)]
            p = [jnp.exp(s[h] - m_new[h]) for h in range(hp)]
            alpha = [jnp.exp(carry[h][0] - m_new[h]) for h in range(hp)]
            l = [alpha[h] * carry[h][1] + jnp.sum(p[h], axis=-1, keepdims=True) for h in range(hp)]
            pv = [jnp.dot(p[h].astype(BF16), v_ref[0, rows, vd(h)], preferred_element_type=F32) for h in range(hp)]
            return tuple((m_new[h], l[h], alpha[h] * carry[h][2] + pv[h]) for h in range(hp))

        init = tuple((jnp.full((t, 1), -1e30, F32), jnp.zeros((t, 1), F32), jnp.zeros((t, HEAD_DIM), F32)) for _ in range(hp))
        below = lax.fori_loop(0, i, lambda j, carry: step(j, carry, False), init)
        for h, (m, l, acc) in enumerate(step(i, below, True)):
            o_ref[0, :, vd(h)] = acc / l
            lse_ref[0, h, 0] = (m + jnp.log(l)).T

        @pl.when((b == B - 1) & (g == HEADS // hp - 1) & (i == nq - 1))
        def _():
            for wait in _gather_copies(src_refs, dst_refs, *sems)[2]:
                wait()

    return pl.pallas_call(
        body, grid=(B, HEADS // hp, nq),
        in_specs=[pl.BlockSpec((1, t, hp * QK_PAD), lambda b, g, i: (b, i, g)),
                  pl.BlockSpec((1, S, hp * QK_PAD), lambda b, g, i: (b, 0, g)),
                  pl.BlockSpec((1, S, hp * HEAD_DIM), lambda b, g, i: (b, 0, g))] + [_ANY] * ns,
        out_specs=[pl.BlockSpec((1, t, hp * HEAD_DIM), lambda b, g, i: (b, i, g)),
                   pl.BlockSpec((1, hp, 1, 1, t), lambda b, g, i: (b, g, i, 0, 0))] + [_ANY] * ns,
        out_shape=[_sds((B, S, HEADS * HEAD_DIM), F32), _sds((B, HEADS, nq, 1, t), F32)] + [_sds((4,) + s.shape, s.dtype) for s in shards],
        scratch_shapes=_gather_scratch(ns),
        compiler_params=_params(("arbitrary", "arbitrary", "arbitrary")), name="attn_fwd",
    )(q, k, v, *shards)


def _attn_bwd(q, k, v, o, lse, do, partials):
    B, S, _ = q.shape
    t = min(ATTN_TILE, S)
    nq = S // t
    ns = len(partials)

    hp = ATTN_HEADS_PER_STEP
    qk = lambda h: slice(h * QK_PAD, (h + 1) * QK_PAD)
    vd = lambda h: slice(h * HEAD_DIM, (h + 1) * HEAD_DIM)
    heads = range(hp)

    def body(*refs):
        q_ref, k_ref, v_ref, o_ref, lse_ref, do_ref = refs[:6]
        src_refs = refs[6:6 + ns]
        dq_ref, dk_ref, dv_ref = refs[6 + ns:9 + ns]
        dst_refs = refs[9 + ns:9 + 2 * ns]
        dsum_ref, send_sems, recv_sems, local_sems = refs[9 + 2 * ns:]
        b, g, j = pl.program_id(0), pl.program_id(1), pl.program_id(2)

        @pl.when((b == 0) & (g == 0) & (j == 0))
        def _():
            for start in _scatter_copies(src_refs, dst_refs, send_sems, recv_sems, local_sems)[0]:
                start()

        @pl.when(j == 0)
        def _():
            dq_ref[...] = jnp.zeros_like(dq_ref)
            for h in heads:
                for blk in range(nq):
                    rows = slice(blk * t, (blk + 1) * t)
                    dsum_ref[h, blk] = jnp.sum(do_ref[0, rows, vd(h)] * o_ref[0, rows, vd(h)], axis=-1, keepdims=True).T

        kb = [k_ref[0, :, qk(h)] for h in heads]
        vb = [v_ref[0, :, vd(h)] for h in heads]

        def step(i, carry, diagonal):
            rows = pl.ds(pl.multiple_of(i * t, t), t)
            qb = [q_ref[0, rows, qk(h)] for h in heads]
            dob = [do_ref[0, rows, vd(h)].astype(BF16) for h in heads]
            s = [_dg(kb[h], qb[h], 1, 1, None) * ATTN_SCALE for h in heads]
            p = [jnp.exp(s[h] - lse_ref[0, h, i]) for h in heads]
            if diagonal:
                key = lax.broadcasted_iota(jnp.int32, (t, t), 0)
                query = lax.broadcasted_iota(jnp.int32, (t, t), 1)
                p = [jnp.where(key <= query, x, 0.0) for x in p]
            dp = [_dg(vb[h], dob[h], 1, 1, None) for h in heads]
            dv = [carry[h][1] + jnp.dot(p[h].astype(BF16), dob[h], preferred_element_type=F32) for h in heads]
            ds = [(p[h] * (dp[h] - dsum_ref[h, i]) * ATTN_SCALE).astype(BF16) for h in heads]
            for h in heads:
                dq_ref[0, rows, qk(h)] += _dg(ds[h], kb[h], 0, 0, None)
            return tuple((carry[h][0] + jnp.dot(ds[h], qb[h], preferred_element_type=F32), dv[h]) for h in heads)

        zeros = tuple((jnp.zeros((t, QK_PAD), F32), jnp.zeros((t, HEAD_DIM), F32)) for _ in heads)
        on_diagonal = step(j, zeros, True)
        done = lax.fori_loop(j + 1, nq, lambda i, carry: step(i, carry, False), on_diagonal)
        for h, (dk, dv) in enumerate(done):
            dk_ref[0, :, qk(h)] = dk
            dv_ref[0, :, vd(h)] = dv

        @pl.when((b == B - 1) & (g == HEADS // hp - 1) & (j == nq - 1))
        def _():
            for wait in _scatter_copies(src_refs, dst_refs, send_sems, recv_sems, local_sems)[1]:
                wait()

    return pl.pallas_call(
        body, grid=(B, HEADS // hp, nq),
        in_specs=[pl.BlockSpec((1, S, hp * QK_PAD), lambda b, g, j: (b, 0, g)),
                  pl.BlockSpec((1, t, hp * QK_PAD), lambda b, g, j: (b, j, g)),
                  pl.BlockSpec((1, t, hp * HEAD_DIM), lambda b, g, j: (b, j, g)),
                  pl.BlockSpec((1, S, hp * HEAD_DIM), lambda b, g, j: (b, 0, g)),
                  pl.BlockSpec((1, hp, nq, 1, t), lambda b, g, j: (b, g, 0, 0, 0)),
                  pl.BlockSpec((1, S, hp * HEAD_DIM), lambda b, g, j: (b, 0, g))] + [_ANY] * ns,
        out_specs=[pl.BlockSpec((1, S, hp * QK_PAD), lambda b, g, j: (b, 0, g)),
                   pl.BlockSpec((1, t, hp * QK_PAD), lambda b, g, j: (b, j, g)),
                   pl.BlockSpec((1, t, hp * HEAD_DIM), lambda b, g, j: (b, j, g))] + [_ANY] * ns,
        out_shape=[_sds((B, S, HEADS * QK_PAD), F32), _sds((B, S, HEADS * QK_PAD), F32), _sds((B, S, HEADS * HEAD_DIM), F32)]
                  + [_scattered_shape(p) for p in partials],
        scratch_shapes=[pltpu.VMEM((hp, nq, 1, t), F32)] + _scatter_scratch(ns),
        compiler_params=_params(("arbitrary", "arbitrary", "arbitrary")), name="attn_bwd",
    )(q, k, v, o, lse, do, *partials)


def _gdn_pre_fn(xq, xk, xv, wq, wk, wv, keeps):
    def conv_silu(x, w):
        acc = x * w[3]
        for s in (1, 2, 3):
            acc = acc + _shift_rows(x, keeps[s - 1], s) * w[3 - s]
        return _silu(acc)

    def l2(x):
        return x * lax.rsqrt(jnp.sum(x * x, axis=-1, keepdims=True) + EPS)

    return l2(conv_silu(xq, wq)) * (HEAD_DIM ** -0.5), l2(conv_silu(xk, wk)), conv_silu(xv, wv)


def _gdn_pre_specs(S):
    x_specs = [pl.BlockSpec((1, S, HEAD_DIM), lambda h, b, g=g: (b, 0, g * HEADS + h)) for g in range(3)]
    w_specs = [pl.BlockSpec((CONV_TAPS, HEAD_DIM), lambda h, b, g=g: (0, g * HEADS + h)) for g in range(3)]
    out_spec = pl.BlockSpec((1, S, HEAD_DIM), lambda h, b: (b, 0, h))
    return x_specs, w_specs, out_spec


def _row_keeps(S):
    t = lax.broadcasted_iota(jnp.int32, (S, HEAD_DIM), 0)
    return [(t >= s).astype(F32) for s in (1, 2, 3)]


def _gdn_pre_fwd(gqkv, conv_w):
    B, S, _ = gqkv.shape
    x_specs, w_specs, out_spec = _gdn_pre_specs(S)

    def body(xq_ref, xk_ref, xv_ref, wq_ref, wk_ref, wv_ref, q_ref, k_ref, v_ref):
        taps = [[w[i:i + 1, :] for i in range(CONV_TAPS)] for w in (wq_ref, wk_ref, wv_ref)]
        q, k, v = _gdn_pre_fn(xq_ref[0], xk_ref[0], xv_ref[0], *taps, _row_keeps(S))
        q_ref[0], k_ref[0], v_ref[0] = q, k, v

    return pl.pallas_call(
        body, grid=(HEADS, B), in_specs=x_specs + w_specs, out_specs=[out_spec] * 3,
        out_shape=[_sds((B, S, HEADS * HEAD_DIM), F32)] * 3,
        compiler_params=_params(("parallel", "parallel")), name="gdn_pre_fwd",
    )(gqkv, gqkv, gqkv, conv_w, conv_w, conv_w)


def _gdn_pre_bwd(gqkv, conv_w, dq, dk, dv, halves):
    B, S, _ = gqkv.shape
    x_specs, w_specs, out_spec = _gdn_pre_specs(S)
    dw_spec = pl.BlockSpec((CONV_TAPS, HEAD_DIM), lambda h, b: (0, h))
    ns = len(halves)

    def body(*refs):
        xq_ref, xk_ref, xv_ref, wq_ref, wk_ref, wv_ref, dq_ref, dk_ref, dv_ref = refs[:9]
        src_refs = refs[9:9 + ns]
        dxq_ref, dxk_ref, dxv_ref, dwq_ref, dwk_ref, dwv_ref = refs[9 + ns:15 + ns]
        dst_refs = refs[15 + ns:15 + 2 * ns]
        sems = refs[15 + 2 * ns:]
        first = (pl.program_id(0) == 0) & (pl.program_id(1) == 0)
        last = (pl.program_id(0) == HEADS - 1) & (pl.program_id(1) == B - 1)

        @pl.when(first)
        def _():
            for start in _swap_copies(src_refs, dst_refs, *sems)[0]:
                start()

        @pl.when(pl.program_id(1) == 0)
        def _():
            for r in (dwq_ref, dwk_ref, dwv_ref):
                r[...] = jnp.zeros_like(r)

        taps = [[w[i:i + 1, :] for i in range(CONV_TAPS)] for w in (wq_ref, wk_ref, wv_ref)]
        keeps = _row_keeps(S)
        _, pull = jax.vjp(lambda *a: _gdn_pre_fn(*a, keeps), xq_ref[0], xk_ref[0], xv_ref[0], *taps)
        dxq, dxk, dxv, dwq, dwk, dwv = pull((dq_ref[0], dk_ref[0], dv_ref[0]))
        dxq_ref[0], dxk_ref[0], dxv_ref[0] = dxq.astype(BF16), dxk.astype(BF16), dxv.astype(BF16)
        for ref, dw in ((dwq_ref, dwq), (dwk_ref, dwk), (dwv_ref, dwv)):
            for i in range(CONV_TAPS):
                ref[i:i + 1, :] += dw[i]

        @pl.when(last)
        def _():
            for wait in _swap_copies(src_refs, dst_refs, *sems)[1]:
                wait()

    hw = HEADS * HEAD_DIM
    return pl.pallas_call(
        body, grid=(HEADS, B), in_specs=x_specs + w_specs + [out_spec] * 3 + [_ANY] * ns,
        out_specs=[out_spec] * 3 + [dw_spec] * 3 + [_ANY] * ns,
        out_shape=[_sds((B, S, hw), BF16)] * 3 + [_sds((CONV_TAPS, hw), F32)] * 3 + [_swapped_shape(h) for h in halves],
        scratch_shapes=_swap_scratch(ns),
        compiler_params=_params(("arbitrary", "arbitrary")), name="gdn_pre_bwd",
    )(gqkv, gqkv, gqkv, conv_w, conv_w, conv_w, dq, dk, dv, *halves)


def _chunk_masks():
    i = lax.broadcasted_iota(jnp.int32, (CHUNK, CHUNK), 0)
    j = lax.broadcasted_iota(jnp.int32, (CHUNK, CHUNK), 1)
    lower, after = (j <= i).astype(F32), (j > i).astype(F32)
    return {"le": lower, "le_gt": jnp.concatenate([lower, after], axis=0), "strict": (j < i).astype(F32)}


def _gdn_chunk_fn(groups, masks):
    lane = lax.broadcasted_iota(jnp.int32, (groups, 1, 128), 2)
    head = lax.broadcasted_iota(jnp.int32, (groups, 1, 128), 0) % HEADS
    pick_a, pick_b = (lane == head).astype(F32), (lane == head + HEADS).astype(F32)
    lower, lower_after, strict = (jnp.broadcast_to(masks[n], (groups,) + masks[n].shape) for n in ("le", "le_gt", "strict"))
    ones_row = jnp.ones((1, 1, HEAD_DIM), F32)

    def f(q, k, v, gab, a_row, dt_row, state):
        ga = jnp.sum(gab * pick_a, axis=2, keepdims=True)
        gb = jnp.sum(gab * pick_b, axis=2, keepdims=True)
        a_log = jnp.sum(a_row * pick_a, axis=2, keepdims=True)
        dt_bias = jnp.sum(dt_row * pick_a, axis=2, keepdims=True)
        beta = _sigmoid(gb)
        g = -jnp.exp(a_log) * _softplus(ga + dt_bias)
        g_wide = g * ones_row
        cum, rest = _row_halves(_hi_nn(lower_after, g_wide))
        total = jnp.sum(g_wide, axis=1, keepdims=True)
        diff = _hi_nn(lower, g * strict)
        decay = lower * jnp.exp(diff)
        e_cum = jnp.exp(cum)
        kk, qk = _row_halves(_bf_nt(jnp.concatenate([k, q], axis=1), k))
        lmat = strict * (beta * kk * decay)
        u, w = _lane_halves(_unit_lower_solve(lmat, jnp.concatenate([v * beta, k * (beta * e_cum)], axis=2)))
        w_state, q_state = _row_halves(_bf_nn(jnp.concatenate([w, q * e_cum], axis=1), state))
        v_new = u - w_state
        o = q_state + _bf_nn(qk * decay, v_new)
        new_state = state * jnp.exp(total) + _bf_tn(k * jnp.exp(rest), v_new)
        return o, new_state

    return f


def _gdn_chunk_fwd(q, k, v, gab, scal, shards):
    B, S, W = q.shape
    N = S // CHUNK
    ns = len(shards)

    def body(*refs):
        q_ref, k_ref, v_ref, gab_ref, sc_ref = refs[:5]
        src_refs = refs[5:5 + ns]
        o_ref, st_ref = refs[5 + ns:7 + ns]
        dst_refs = refs[7 + ns:7 + 2 * ns]
        state_ref, send_sems, recv_sems, local_sems = refs[7 + 2 * ns:]
        n = pl.program_id(0)

        @pl.when(n == 0)
        def _():
            for start in _gather_copies(src_refs, dst_refs, send_sems, recv_sems, local_sems)[0]:
                start()
            state_ref[...] = jnp.zeros_like(state_ref)

        @pl.when(n == (2 * N) // 3)
        def _():
            for pass_on in _gather_copies(src_refs, dst_refs, send_sems, recv_sems, local_sems)[1]:
                pass_on()

        groups = [(b, h) for b in range(B) for h in range(HEADS)]
        gather = lambda ref: jnp.stack([ref[b, :, h * HEAD_DIM:(h + 1) * HEAD_DIM] for b, h in groups])
        state = state_ref[...]
        for i, (b, h) in enumerate(groups):
            st_ref[b, 0, h] = state[i]
        o, new_state = _gdn_chunk_fn(len(groups), _chunk_masks())(
            gather(q_ref), gather(k_ref), gather(v_ref), jnp.stack([gab_ref[b] for b, _ in groups]), sc_ref[0:1, :], sc_ref[1:2, :], state)
        for i, (b, h) in enumerate(groups):
            o_ref[b, :, h * HEAD_DIM:(h + 1) * HEAD_DIM] = o[i]
        state_ref[...] = new_state

        @pl.when(n == N - 1)
        def _():
            for wait in _gather_copies(src_refs, dst_refs, send_sems, recv_sems, local_sems)[2]:
                wait()

    seq = pl.BlockSpec((B, CHUNK, W), lambda n: (0, n, 0))
    return pl.pallas_call(
        body, grid=(N,),
        in_specs=[seq, seq, seq, pl.BlockSpec((B, CHUNK, GAB_W), lambda n: (0, n, 0)), _const_spec((8, 128))] + [_ANY] * ns,
        out_specs=[seq, pl.BlockSpec((B, 1, HEADS, HEAD_DIM, HEAD_DIM), lambda n: (0, n, 0, 0, 0))] + [_ANY] * ns,
        out_shape=[_sds((B, S, W), F32), _sds((B, N, HEADS, HEAD_DIM, HEAD_DIM), F32)] + [_sds((4,) + s.shape, s.dtype) for s in shards],
        scratch_shapes=[pltpu.VMEM((B * HEADS, HEAD_DIM, HEAD_DIM), F32)] + _gather_scratch(ns),
        compiler_params=_params(("arbitrary",)), name="gdn_chunk_fwd",
    )(q, k, v, gab, scal, *shards)


def _gdn_chunk_bwd(q, k, v, gab, scal, states, do, partials):
    B, S, W = q.shape
    N = S // CHUNK
    ns = len(partials)

    def body(*refs):
        q_ref, k_ref, v_ref, gab_ref, sc_ref, st_ref, do_ref = refs[:7]
        src_refs = refs[7:7 + ns]
        dq_ref, dk_ref, dv_ref, dgab_ref, dsc_ref = refs[7 + ns:12 + ns]
        dst_refs = refs[12 + ns:12 + 2 * ns]
        dstate_ref, send_sems, recv_sems, local_sems = refs[12 + 2 * ns:]
        n = pl.program_id(0)

        @pl.when(n == 0)
        def _():
            for start in _scatter_copies(src_refs, dst_refs, send_sems, recv_sems, local_sems)[0]:
                start()
            dstate_ref[...] = jnp.zeros_like(dstate_ref)
            dsc_ref[...] = jnp.zeros_like(dsc_ref)

        groups = [(b, h) for b in range(B) for h in range(HEADS)]
        gather = lambda ref: jnp.stack([ref[b, :, h * HEAD_DIM:(h + 1) * HEAD_DIM] for b, h in groups])
        _, pull = jax.vjp(_gdn_chunk_fn(len(groups), _chunk_masks()), gather(q_ref), gather(k_ref), gather(v_ref),
                          jnp.stack([gab_ref[b] for b, _ in groups]), sc_ref[0:1, :], sc_ref[1:2, :],
                          jnp.stack([st_ref[b, 0, h] for b, h in groups]))
        dq, dk, dv, dg, d_a, d_dt, dstate = pull((gather(do_ref), dstate_ref[...]))
        for i, (b, h) in enumerate(groups):
            lanes = slice(h * HEAD_DIM, (h + 1) * HEAD_DIM)
            dq_ref[b, :, lanes] = dq[i]
            dk_ref[b, :, lanes] = dk[i]
            dv_ref[b, :, lanes] = dv[i]
        for b in range(B):
            dgab_ref[b] = sum(dg[b * HEADS + h] for h in range(HEADS)).astype(BF16)
        dstate_ref[...] = dstate
        dsc_ref[0:1, :] += d_a
        dsc_ref[1:2, :] += d_dt

        @pl.when(n == N - 1)
        def _():
            for wait in _scatter_copies(src_refs, dst_refs, send_sems, recv_sems, local_sems)[1]:
                wait()

    seq = pl.BlockSpec((B, CHUNK, W), lambda n: (0, N - 1 - n, 0))
    gab_spec = pl.BlockSpec((B, CHUNK, GAB_W), lambda n: (0, N - 1 - n, 0))
    return pl.pallas_call(
        body, grid=(N,),
        in_specs=[seq, seq, seq, gab_spec, _const_spec((8, 128)),
                  pl.BlockSpec((B, 1, HEADS, HEAD_DIM, HEAD_DIM), lambda n: (0, N - 1 - n, 0, 0, 0)), seq] + [_ANY] * ns,
        out_specs=[seq, seq, seq, gab_spec, _const_spec((8, 128))] + [_ANY] * ns,
        out_shape=[_sds((B, S, W), F32)] * 3 + [_sds((B, S, GAB_W), BF16), _sds((8, 128), F32)] + [_scattered_shape(p) for p in partials],
        scratch_shapes=[pltpu.VMEM((B * HEADS, HEAD_DIM, HEAD_DIM), F32)] + _scatter_scratch(ns),
        compiler_params=_params(("arbitrary",)), name="gdn_chunk_bwd",
    )(q, k, v, gab, scal, states, do, *partials)


def _mix_fn(ao, go, gz, w_mla, w_gdn):
    return tuple(_rms(ao[h], w_mla[h]) for h in range(HEADS)) + tuple(_rms(go[h], w_gdn) * _silu(gz[h]) for h in range(HEADS))


def _mix_operands(ao_ref, go_ref, gz_ref, nw_ref):
    blocks = lambda ref: [ref[:, h * HEAD_DIM:(h + 1) * HEAD_DIM] for h in range(HEADS)]
    return blocks(ao_ref), blocks(go_ref), blocks(gz_ref), [nw_ref[h:h + 1, :] for h in range(HEADS)], nw_ref[HEADS:HEADS + 1, :]


def _mix_fwd(ao, go, gz, nw, w_out, x2):
    T, D = x2.shape
    tm = min(TOKEN_TILE, T)
    MW = 2 * HEADS * HEAD_DIM

    def body(ao_ref, go_ref, gz_ref, nw_ref, w_ref, x_ref, mix_ref, h_ref):
        outs = _mix_fn(*_mix_operands(ao_ref, go_ref, gz_ref, nw_ref))
        for i, piece in enumerate(outs):
            mix_ref[:, i * HEAD_DIM:(i + 1) * HEAD_DIM] = piece.astype(BF16)
        h_ref[...] = x_ref[...] + jnp.dot(mix_ref[...], w_ref[...], preferred_element_type=F32)

    half = HEADS * HEAD_DIM
    return pl.pallas_call(
        body, grid=(T // tm,),
        in_specs=[_row_spec(tm, half), _row_spec(tm, half), _row_spec(tm, half), _const_spec((8, 128)), _const_spec((MW, D)),
                  _row_spec(tm, D)],
        out_specs=[_row_spec(tm, MW), _row_spec(tm, D)],
        out_shape=[_sds((T, MW), BF16), _sds((T, D), F32)],
        compiler_params=_params(("parallel",)), name="mix_fwd",
    )(ao, go, gz, nw, w_out, x2)


def _mix_bwd(ao, go, gz, nw, w_out, dh):
    T, D = dh.shape
    tm = min(TOKEN_TILE, T)
    MW = 2 * HEADS * HEAD_DIM
    half = HEADS * HEAD_DIM

    def body(ao_ref, go_ref, gz_ref, nw_ref, w_ref, dh_ref, dao_ref, dgo_ref, dgz_ref, dnw_ref):
        @pl.when(pl.program_id(0) == 0)
        def _():
            dnw_ref[...] = jnp.zeros_like(dnw_ref)

        d_mix = _dg(dh_ref[...].astype(BF16), w_ref[...], 1, 1, None)
        cts = tuple(d_mix[:, i * HEAD_DIM:(i + 1) * HEAD_DIM] for i in range(2 * HEADS))
        _, pull = jax.vjp(_mix_fn, *_mix_operands(ao_ref, go_ref, gz_ref, nw_ref))
        d_ao, d_go, d_gz, d_wm, d_wg = pull(cts)
        for h in range(HEADS):
            lanes = slice(h * HEAD_DIM, (h + 1) * HEAD_DIM)
            dao_ref[:, lanes] = d_ao[h]
            dgo_ref[:, lanes] = d_go[h]
            dgz_ref[:, lanes] = d_gz[h].astype(BF16)
            dnw_ref[h:h + 1, :] += d_wm[h]
        dnw_ref[HEADS:HEADS + 1, :] += d_wg

    return pl.pallas_call(
        body, grid=(T // tm,),
        in_specs=[_row_spec(tm, half), _row_spec(tm, half), _row_spec(tm, half), _const_spec((8, 128)), _const_spec((MW, D)),
                  _row_spec(tm, D)],
        out_specs=[_row_spec(tm, half)] * 3 + [_const_spec((8, 128))],
        out_shape=[_sds((T, half), F32)] * 2 + [_sds((T, half), BF16), _sds((8, 128), F32)],
        compiler_params=_params(("arbitrary",)), name="mix_bwd",
    )(ao, go, gz, nw, w_out, dh)


def _up_spec(w_up, tf):
    per_shard = w_up.shape[2] // tf
    return pl.BlockSpec((None, w_up.shape[1], tf), lambda i, j: (j // per_shard, 0, j % per_shard))


def _mlp_fwd(h2, w_mn, w_up, w_down, target):
    T, D = h2.shape
    FF = w_down.shape[0]
    tm, tf = min(MLP_TOKEN_TILE, T), min(FF_TILE, w_up.shape[2])
    nf = FF // tf

    def body(h_ref, wn_ref, wu_ref, wd_ref, t_ref, hn_ref, act_ref, dy_ref, sq_ref, acc_ref):
        j = pl.program_id(1)

        @pl.when(j == 0)
        def _():
            hn_ref[...] = _rms(h_ref[...], wn_ref[...]).astype(BF16)
            acc_ref[...] = jnp.zeros_like(acc_ref)

        up = jnp.dot(hn_ref[...], wu_ref[...], preferred_element_type=F32)
        act = jnp.square(jnp.maximum(up, 0.0)).astype(BF16)
        act_ref[...] = act
        acc_ref[...] += jnp.dot(act, wd_ref[...], preferred_element_type=F32)

        @pl.when(j == nf - 1)
        def _():
            err = h_ref[...] + acc_ref[...] - t_ref[...]
            dy_ref[...] = err * (1.0 / D)
            sq_ref[...] = jnp.zeros_like(sq_ref) + jnp.sum(err * err)

    tok = lambda w: pl.BlockSpec((tm, w), lambda i, j: (i, 0))
    return pl.pallas_call(
        body, grid=(T // tm, nf),
        in_specs=[tok(D), _const_spec((1, D)), _up_spec(w_up, tf), pl.BlockSpec((tf, D), lambda i, j: (j, 0)), tok(D)],
        out_specs=[tok(D), pl.BlockSpec((tm, tf), lambda i, j: (i, j)), tok(D), pl.BlockSpec((1, 8, 128), lambda i, j: (i, 0, 0))],
        out_shape=[_sds((T, D), BF16), _sds((T, FF), BF16), _sds((T, D), F32), _sds((T // tm, 8, 128), F32)],
        scratch_shapes=[pltpu.VMEM((tm, D), F32)],
        compiler_params=_params(("parallel", "arbitrary")), name="mlp_fwd",
    )(h2, w_mn, w_up, w_down, target)


def _mlp_bwd(h2, w_mn, act, w_up, w_down, dy):
    T, D = h2.shape
    FF = w_down.shape[0]
    tm, tf = min(MLP_TOKEN_TILE, T), min(FF_TILE, w_up.shape[2])
    nf = FF // tf

    def body(h_ref, wn_ref, act_ref, wu_ref, wd_ref, dy_ref, dh_ref, dup_ref, dwn_ref, acc_ref, dyb_ref):
        i, j = pl.program_id(0), pl.program_id(1)

        @pl.when((i == 0) & (j == 0))
        def _():
            dwn_ref[...] = jnp.zeros_like(dwn_ref)

        @pl.when(j == 0)
        def _():
            acc_ref[...] = jnp.zeros_like(acc_ref)
            dyb_ref[...] = dy_ref[...].astype(BF16)

        r = jnp.sqrt(act_ref[...].astype(F32))
        d_act = _dg(dyb_ref[...], wd_ref[...], 1, 1, None)
        d_up = (d_act * (2.0 * r)).astype(BF16)
        dup_ref[...] = d_up
        acc_ref[...] += _dg(d_up, wu_ref[...], 1, 1, None)

        @pl.when(j == nf - 1)
        def _():
            _, pull = jax.vjp(_rms, h_ref[...], wn_ref[...])
            dh, dwn = pull(acc_ref[...])
            dh_ref[...] = dh + dy_ref[...]
            dwn_ref[...] += dwn

    tok = lambda w: pl.BlockSpec((tm, w), lambda i, j: (i, 0))
    ff = pl.BlockSpec((tm, tf), lambda i, j: (i, j))
    return pl.pallas_call(
        body, grid=(T // tm, nf),
        in_specs=[tok(D), _const_spec((1, D)), ff, _up_spec(w_up, tf), pl.BlockSpec((tf, D), lambda i, j: (j, 0)), tok(D)],
        out_specs=[tok(D), ff, _const_spec((1, D))],
        out_shape=[_sds((T, D), F32), _sds((T, FF), BF16), _sds((1, D), F32)],
        scratch_shapes=[pltpu.VMEM((tm, D), F32), pltpu.VMEM((tm, D), BF16)],
        compiler_params=_params(("arbitrary", "arbitrary")), name="mlp_bwd",
    )(h2, w_mn, act, w_up, w_down, dy)


def _rope_pad(a):
    z = jnp.zeros(a.shape[:-1] + (ROPE_HALF,), a.dtype)
    return jnp.concatenate([a[..., :ROPE_HALF], z, a[..., ROPE_HALF:], z], axis=-1)


def _rope_unpad(a):
    return jnp.concatenate([a[..., :ROPE_HALF], a[..., 2 * ROPE_HALF:3 * ROPE_HALF]], axis=-1)


_G0 = 2 * LORA + ROPE_DIM
W_IN_COLS = _G0 + GQKV_W + GZ_W + 2 * HEADS


def _widen_w_in_t(w_t):
    z = jnp.zeros((ROPE_HALF, w_t.shape[1]), w_t.dtype)
    pad = jnp.zeros((GAB_W - 2 * HEADS, w_t.shape[1]), w_t.dtype)
    return jnp.concatenate([w_t[:2 * LORA + ROPE_HALF], z, w_t[2 * LORA + ROPE_HALF:_G0], z, w_t[_G0:], pad], axis=0)


def _narrow_w_in_t(w_t):
    return jnp.concatenate([w_t[:2 * LORA + ROPE_HALF], w_t[2 * LORA + 2 * ROPE_HALF:2 * LORA + 3 * ROPE_HALF],
                            w_t[LAT_W:LAT_W + W_IN_COLS - _G0]], axis=0)


def _stack_mla(w_uq, w_ukv):
    uq = w_uq.reshape(LORA, HEADS, QK_DIM)
    ukv = w_ukv.reshape(LORA, HEADS, 2 * HEAD_DIM)
    parts = [uq[:, :, :HEAD_DIM], _rope_pad(uq[:, :, HEAD_DIM:]), ukv[:, :, :HEAD_DIM], ukv[:, :, HEAD_DIM:]]
    return jnp.concatenate([p.transpose(1, 0, 2) for p in parts], axis=0)


def _unstack_mla(w):
    p = [w[i * HEADS:(i + 1) * HEADS].transpose(1, 0, 2) for i in range(4)]
    uq = jnp.concatenate([p[0], _rope_unpad(p[1])], axis=-1).reshape(LORA, HEADS * QK_DIM)
    ukv = jnp.concatenate([p[2], p[3]], axis=-1).reshape(LORA, HEADS * 2 * HEAD_DIM)
    return uq, ukv


def _rows8(rows):
    a = jnp.concatenate(rows, axis=0)
    return jnp.pad(a, ((0, 8 - a.shape[0]), (0, 0)))


def _qk_norm_rows(q_norm_w, k_norm_w):
    return _rows8([q_norm_w[:, :HEAD_DIM], _rope_pad(q_norm_w[:, HEAD_DIM:]), k_norm_w[:, :HEAD_DIM], _rope_pad(k_norm_w[:, HEAD_DIM:])])


def _rope_rows():
    inv_freq = ROPE_THETA ** (-jnp.arange(ROPE_HALF, dtype=F32) / ROPE_HALF)
    z = jnp.zeros((ROPE_HALF,), F32)
    freq = jnp.concatenate([inv_freq, z, inv_freq, z])
    sign = jnp.concatenate([-jnp.ones((ROPE_HALF,), F32), z, jnp.ones((ROPE_HALF,), F32), z])
    return _rows8([freq[None], sign[None]])


def _column_shards(a):
    return a.reshape(a.shape[0], 4, a.shape[1] // 4).transpose(1, 0, 2)


def _from_column_shards(a):
    return a.transpose(1, 0, 2).reshape(a.shape[1], 4 * a.shape[2])


_ANY = pl.BlockSpec(memory_space=pl.ANY)
_OTHER_CHIPS = ((1, 0), (0, 1), (1, 1))


def _here():
    return lax.axis_index("x"), lax.axis_index("y"), lax.axis_index("c")


def _flip(v, bit):
    return 1 - v if bit else v


def _remote(src, dst, send_sems, recv_sems, k, to):
    return pltpu.make_async_remote_copy(src_ref=src, dst_ref=dst, send_sem=send_sems.at[k], recv_sem=recv_sems.at[k],
                                        device_id=to, device_id_type=MESH)


def _half_of(ref, k, shape):
    r, c = shape
    if (r // 2) % 16 == 0:
        return ref.at[pl.ds(pl.multiple_of(k * (r // 2), 16), r // 2)]
    if (c // 2) % 128 == 0:
        return ref.at[:, pl.ds(pl.multiple_of(k * (c // 2), 128), c // 2)]
    return None


def _gather_copies(srcs, dsts, send_sems, recv_sems, local_sems):
    x, y, c = _here()
    slot, sibling, n = 2 * x + y, (x, y, 1 - c), len(srcs)
    starts, passes, waits = [], [], []
    for i, (src, dst) in enumerate(zip(srcs, dsts)):
        own = pltpu.make_async_copy(src, dst.at[slot], local_sems.at[i])
        starts.append(own.start)
        waits.append(own.wait)
        halves = _half_of(src, c, src.shape) is not None
        for j, (fx, fy) in enumerate(_OTHER_CHIPS):
            cx, cy = _flip(x, fx), _flip(y, fy)
            there = dst.at[2 * cx + cy]
            if halves:
                push = _remote(_half_of(src, c, src.shape), _half_of(dst.at[slot], c, src.shape), send_sems, recv_sems, 3 * i + j, (cx, cy, c))
                landed, other = _half_of(there, c, src.shape), _half_of(there, 1 - c, src.shape)
                onward = _remote(landed, landed, send_sems, recv_sems, 3 * n + 3 * i + j, sibling)
                passes += [_remote(landed, landed, send_sems, recv_sems, 3 * i + j, (cx, cy, c)).wait_recv, onward.start]
                waits += [_remote(other, other, send_sems, recv_sems, 3 * n + 3 * i + j, sibling).wait_recv, onward.wait_send]
            else:
                push = _remote(src, dst.at[slot], send_sems, recv_sems, 3 * i + j, (cx, cy, c))
                waits.append(_remote(there, there, send_sems, recv_sems, 3 * i + j, (cx, cy, c)).wait_recv)
            starts.append(push.start)
            waits.append(push.wait_send)
    return starts, passes, waits


def _gather_scratch(n):
    return [pltpu.SemaphoreType.DMA((6 * n,)), pltpu.SemaphoreType.DMA((6 * n,)), pltpu.SemaphoreType.DMA((n,))]


def _all_gather(shards, name):
    ns = len(shards)

    def body(*refs):
        starts, passes, waits = _gather_copies(refs[:ns], refs[ns:2 * ns], *refs[2 * ns:])
        for call in starts + passes + waits:
            call()

    return pl.pallas_call(
        body, in_specs=[_ANY] * ns, out_specs=[_ANY] * ns, out_shape=[_sds((4,) + s.shape, s.dtype) for s in shards],
        scratch_shapes=_gather_scratch(ns), name=name,
    )(*shards)


def _by_lanes(shape):
    return (shape[-2] // 2) % 16 != 0


def _scattered_shape(p):
    r, c = p.shape[1:]
    return _sds((8, r, c // 2) if _by_lanes(p.shape) else (8, r // 2, c), p.dtype)


def _scatter_copies(srcs, dsts, send_sems, recv_sems, local_sems, whole=0):
    x, y, c = _here()
    me = 4 * x + 2 * y + c
    starts, waits = [], []
    for i, (src, dst) in enumerate(zip(srcs, dsts)):
        def piece(px, py, pc, src=src, entire=i >= len(srcs) - whole):
            if entire:
                return src
            if _by_lanes(src.shape):
                half = src.shape[2] // 2
                return src.at[2 * px + py, :, pl.ds(pl.multiple_of(pc * half, 128), half)]
            half = src.shape[1] // 2
            return src.at[2 * px + py, pl.ds(pl.multiple_of(pc * half, 16), half)]

        own = pltpu.make_async_copy(piece(x, y, c), dst.at[me], local_sems.at[i])
        starts.append(own.start)
        waits.append(own.wait)
        for k in range(1, 8):
            px, py, pc = _flip(x, k & 4), _flip(y, k & 2), _flip(c, k & 1)
            push = _remote(piece(px, py, pc), dst.at[me], send_sems, recv_sems, 7 * i + k - 1, (px, py, pc))
            landed = dst.at[4 * px + 2 * py + pc]
            starts.append(push.start)
            waits += [_remote(landed, landed, send_sems, recv_sems, 7 * i + k - 1, (px, py, pc)).wait_recv, push.wait_send]
    return starts, waits


def _scatter_scratch(n):
    return [pltpu.SemaphoreType.DMA((7 * n,)), pltpu.SemaphoreType.DMA((7 * n,)), pltpu.SemaphoreType.DMA((n,))]


def _swapped_shape(half):
    r, c = half.shape
    return _sds((r, 2 * c) if _by_lanes((r, 2 * c)) else (2, r, c), half.dtype)


def _swap_copies(srcs, dsts, send_sems, recv_sems, local_sems):
    x, y, c = _here()
    sibling = (x, y, 1 - c)
    starts, waits = [], []
    for i, (src, dst) in enumerate(zip(srcs, dsts)):
        if len(dst.shape) == 2:
            lanes = src.shape[1]
            mine, other = (dst.at[:, pl.ds(pl.multiple_of(k * lanes, 128), lanes)] for k in (c, 1 - c))
        else:
            mine, other = dst.at[c], dst.at[1 - c]
        own = pltpu.make_async_copy(src, mine, local_sems.at[i])
        push = _remote(src, mine, send_sems, recv_sems, i, sibling)
        starts += [own.start, push.start]
        waits += [_remote(other, other, send_sems, recv_sems, i, sibling).wait_recv, push.wait_send, own.wait]
    return starts, waits


def _swap_scratch(n):
    return [pltpu.SemaphoreType.DMA((n,)), pltpu.SemaphoreType.DMA((n,)), pltpu.SemaphoreType.DMA((n,))]


def _exchange_halves(halves, wholes):
    ns, nw = len(halves), len(wholes)

    def body(*refs):
        srcs, dsts = refs[:ns + nw], refs[ns + nw:2 * (ns + nw)]
        sems = refs[2 * (ns + nw):]
        starts, waits = _swap_copies(srcs[:ns], dsts[:ns], *sems[:3])
        more = _scatter_copies(srcs[ns:], dsts[ns:], *sems[3:], whole=nw)
        for call in starts + more[0] + waits + more[1]:
            call()

    return pl.pallas_call(
        body, in_specs=[_ANY] * (ns + nw), out_specs=[_ANY] * (ns + nw),
        out_shape=[_swapped_shape(h) for h in halves] + [_sds((8,) + a.shape, a.dtype) for a in wholes],
        scratch_shapes=_swap_scratch(ns) + _scatter_scratch(nw), name="exchange_halves",
    )(*halves, *wholes)


def _row_tile(rows, row_bytes, budget):
    tr = rows
    while tr * row_bytes > budget and tr % 16 == 0:
        tr //= 2
    return tr


def _sum_slots(parts, name):
    parts = list(parts) if isinstance(parts, (list, tuple)) else [parts]
    rows = parts[0].shape[1]
    widths = [p.shape[2] for p in parts]
    cols = sum(widths)
    tr = _row_tile(rows, 8 * cols * 4, 2 * 1024 * 1024)

    def body(*refs):
        o_ref, c0 = refs[-1], 0
        for p_ref, width in zip(refs[:-1], widths):
            acc = p_ref[0].astype(F32)
            for d in range(1, 8):
                acc = acc + p_ref[d].astype(F32)
            o_ref[:, c0:c0 + width] = acc
            c0 += width

    return pl.pallas_call(
        body, grid=(rows // tr,), in_specs=[pl.BlockSpec((8, tr, w), lambda i: (0, i, 0)) for w in widths],
        out_specs=pl.BlockSpec((tr, cols), lambda i: (i, 0)), out_shape=_sds((rows, cols), F32),
        compiler_params=_params(("parallel",)), name=name,
    )(*parts)


def _adam_update(w, g, m, v):
    m = ADAM_B1 * m + (1.0 - ADAM_B1) * g
    v = ADAM_B2 * v + (1.0 - ADAM_B2) * jnp.square(g)
    m_hat = m / (1.0 - ADAM_B1 ** ADAM_STEP)
    v_hat = v / (1.0 - ADAM_B2 ** ADAM_STEP)
    return -ADAM_LR * (m_hat / (jnp.sqrt(v_hat) + ADAM_EPS) + ADAM_WD * w), m, v


SMALL_ROWS = {"attn_norm_w": 0, "mlp_norm_w": 1, "q_lat_norm_w": 2, "kv_lat_norm_w": 3, "q_norm_w": 4, "k_norm_w": 5,
              "mla_out_norm_w": 6, "gdn_norm_w": 10, "a_log": 11, "dt_bias": 12}
LOSS_ROW = 13
SMALL_SHAPE = (16, 1024)


def _pack_small_partials(d_attn_nw, d_mlp_nw, d_ln, d_qk_nw, d_mix_nw, d_scal, conv_parts, sq):
    D = d_attn_nw.shape[1]

    def body(an_ref, mn_ref, ln_ref, qk_ref, mix_ref, sc_ref, cq_ref, ck_ref, cv_ref, sq_ref, a_ref, c_ref):
        a_ref[...] = jnp.zeros_like(a_ref)
        a_ref[0:1, :D] = an_ref[...]
        a_ref[1:2, :D] = mn_ref[...]
        a_ref[2:4, :LORA] = ln_ref[...]
        for row, base in ((4, 0), (5, 2)):
            rope = qk_ref[base + 1:base + 2, :]
            a_ref[row:row + 1, :QK_DIM] = jnp.concatenate(
                [qk_ref[base:base + 1, :], rope[:, :ROPE_HALF], rope[:, 2 * ROPE_HALF:3 * ROPE_HALF]], axis=1)
        a_ref[6:6 + HEADS, :HEAD_DIM] = mix_ref[0:HEADS, :]
        a_ref[10:11, :HEAD_DIM] = mix_ref[HEADS:HEADS + 1, :]
        a_ref[11:13, :128] = sc_ref[0:2, :]
        a_ref[LOSS_ROW:LOSS_ROW + 1, :128] = jnp.zeros((1, 128), F32) + jnp.sum(sq_ref[:, 0:1, 0:1]) * (0.5 / D)
        c_ref[...] = jnp.concatenate([cq_ref[...], ck_ref[...], cv_ref[...]], axis=1)

    return pl.pallas_call(
        body, out_shape=[_sds(SMALL_SHAPE, F32), _sds((CONV_TAPS, GQKV_W), F32)], name="pack_small_partials",
    )(d_attn_nw, d_mlp_nw, d_ln, d_qk_nw, d_mix_nw, d_scal, *conv_parts, sq)


def _adamw_small(parts, conv_parts, w, m, v):
    names = tuple(SMALL_ROWS) + ("conv_w",)
    cols = w["conv_w"].shape[2]

    def body(*refs):
        p_ref, c_ref = refs[:2]
        n = len(names)
        w_refs, m_refs, v_refs = (dict(zip(names, refs[2 + k * n:2 + (k + 1) * n])) for k in range(3))
        loss_ref = refs[2 + 3 * n]
        out = [dict(zip(names, refs[3 + (3 + k) * n:3 + (4 + k) * n])) for k in range(4)]
        acc_ref, cacc_ref = refs[3 + 7 * n:]
        acc, cacc = p_ref[0], c_ref[0]
        for d in range(1, 8):
            acc, cacc = acc + p_ref[d], cacc + c_ref[d]
        acc_ref[...] = acc
        cacc_ref[...] = cacc
        loss_ref[...] = acc_ref[LOSS_ROW:LOSS_ROW + 1, 0:1]
        chip = 2 * lax.axis_index("x") + lax.axis_index("y")
        for name in names:
            shape = w_refs[name].shape
            if name == "conv_w":
                g = sum(jnp.where(chip == s, cacc_ref[:, s * cols:(s + 1) * cols], 0.0) for s in range(4))[None]
            else:
                row = SMALL_ROWS[name]
                g = acc_ref[row:row + math.prod(shape[:-1]), 0:shape[-1]].reshape(shape)
            delta, new_m, new_v = _adam_update(w_refs[name][...], g, m_refs[name][...], v_refs[name][...])
            for ref, val in zip((o[name] for o in out), (g, delta, new_m, new_v)):
                ref[...] = val

    ins = [x[n] for x in (w, m, v) for n in names]
    shapes = [_sds(w[n].shape, F32) for n in names]
    outs = pl.pallas_call(
        body, out_shape=[_sds((1, 1), F32)] + shapes * 4,
        scratch_shapes=[pltpu.VMEM(parts.shape[1:], F32), pltpu.VMEM(conv_parts.shape[1:], F32)], name="adamw_small",
    )(parts, conv_parts, *ins)
    n = len(names)
    return (outs[0],) + tuple(dict(zip(names, outs[1 + k * n:1 + (k + 1) * n])) for k in range(4))


def _adamw(w, g, m, v, name):
    rows, cols = w.shape[0], w.shape[-1]
    if w.ndim == 3:
        tr = max(d for d in range(1, rows + 1) if rows % d == 0 and d * 8 * cols * 4 * 14 <= VMEM_LIMIT // 2)
    else:
        tr = _row_tile(rows, 7 * cols * 4, 4 * 1024 * 1024)

    def body(w_ref, g_ref, m_ref, v_ref, d_ref, mo_ref, vo_ref):
        d_ref[...], mo_ref[...], vo_ref[...] = _adam_update(w_ref[...], g_ref[...], m_ref[...], v_ref[...])

    block = (tr,) + w.shape[1:]
    spec = pl.BlockSpec(block, lambda i: (i,) + (0,) * (len(block) - 1))
    return pl.pallas_call(
        body, grid=(rows // tr,), in_specs=[spec] * 4, out_specs=[spec] * 3, out_shape=[_sds(w.shape, F32)] * 3,
        compiler_params=_params(("parallel",)), name=name,
    )(w, g, m, v)


def kernel(x, positions, attn_norm_w, w_in, q_lat_norm_w, w_uq, kv_lat_norm_w, w_ukv, q_norm_w, k_norm_w, mla_out_norm_w, conv_w, a_log, dt_bias, gdn_norm_w, w_out, mlp_norm_w, w_up, w_down, loss_target, m_attn_norm_w, m_w_in, m_q_lat_norm_w, m_w_uq, m_kv_lat_norm_w, m_w_ukv, m_q_norm_w, m_k_norm_w, m_mla_out_norm_w, m_conv_w, m_a_log, m_dt_bias, m_gdn_norm_w, m_w_out, m_mlp_norm_w, m_w_up, m_w_down, v_attn_norm_w, v_w_in, v_q_lat_norm_w, v_w_uq, v_kv_lat_norm_w, v_w_ukv, v_q_norm_w, v_k_norm_w, v_mla_out_norm_w, v_conv_w, v_a_log, v_dt_bias, v_gdn_norm_w, v_w_out, v_mlp_norm_w, v_w_up, v_w_down):
    w = dict(zip(WEIGHTS, (attn_norm_w, w_in, q_lat_norm_w, w_uq, kv_lat_norm_w, w_ukv, q_norm_w, k_norm_w, mla_out_norm_w, conv_w,
                           a_log, dt_bias, gdn_norm_w, w_out, mlp_norm_w, w_up, w_down)))
    m = dict(zip(WEIGHTS, (m_attn_norm_w, m_w_in, m_q_lat_norm_w, m_w_uq, m_kv_lat_norm_w, m_w_ukv, m_q_norm_w, m_k_norm_w,
                           m_mla_out_norm_w, m_conv_w, m_a_log, m_dt_bias, m_gdn_norm_w, m_w_out, m_mlp_norm_w, m_w_up, m_w_down)))
    v = dict(zip(WEIGHTS, (v_attn_norm_w, v_w_in, v_q_lat_norm_w, v_w_uq, v_kv_lat_norm_w, v_w_ukv, v_q_norm_w, v_k_norm_w,
                           v_mla_out_norm_w, v_conv_w, v_a_log, v_dt_bias, v_gdn_norm_w, v_w_out, v_mlp_norm_w, v_w_up, v_w_down)))
    B, S, D = x.shape
    T = B * S
    x2, pos, target = x.reshape(T, D), positions.reshape(T, 1), loss_target.reshape(T, D)
    seq = lambda a: a.reshape(B, S, a.shape[-1])
    tok = lambda a: a.reshape(T, a.shape[-1])
    local = {n: w[n][0] for n in SHARDED}

    g_in, g_uq, g_ukv, g_conv = _all_gather([jnp.swapaxes(w_in, 1, 2)[0].astype(BF16), local["w_uq"].astype(BF16),
                                             local["w_ukv"].astype(BF16), local["conv_w"]], "gather_first_weights")
    w_in_p = _widen_w_in_t(g_in.reshape(-1, D))
    w_mla = _stack_mla(_from_column_shards(g_uq), _from_column_shards(g_ukv))
    conv_full = _from_column_shards(g_conv)
    ln_w = jnp.concatenate([q_lat_norm_w, kv_lat_norm_w], axis=0)
    qk_nw = _qk_norm_rows(q_norm_w, k_norm_w)
    rope_rows = _rope_rows()
    scal = _rows8([jnp.pad(a_log, ((0, 0), (0, 128 - HEADS))), jnp.pad(dt_bias, ((0, 0), (0, 128 - HEADS)))])
    mix_nw = _rows8([mla_out_norm_w[0], gdn_norm_w])

    xn, lat, gqkv, gz, gab = _in_proj_fwd(x2, attn_norm_w, w_in_p)
    q, k, v_att = _mla_pre_fwd(lat, pos, ln_w, w_mla, qk_nw, rope_rows)
    ao, lse, g_down = _attn_fwd(seq(q), seq(k), seq(v_att), [local["w_down"].astype(BF16)])
    gq, gk, gv = _gdn_pre_fwd(seq(gqkv), conv_full)
    go, states, g_out, w_up_b = _gdn_chunk_fwd(gq, gk, gv, seq(gab), scal, [local["w_out"].astype(BF16), local["w_up"].astype(BF16)])
    w_out_b = g_out.reshape(-1, D)
    w_down_b = g_down.reshape(-1, D)
    mix, h2 = _mix_fwd(tok(ao), tok(go), gz, mix_nw, w_out_b, x2)
    hn, act, dy, sq = _mlp_fwd(h2, mlp_norm_w, w_up_b, w_down_b, target)

    dh, d_up, d_mlp_nw = _mlp_bwd(h2, mlp_norm_w, act, w_up_b, w_down_b, dy)
    p_down_a, p_down_b = (p.reshape(4, -1, D // 2) for p in _wgrad(act, dy, "wgrad_down", lane_halves=True))
    p_up = _wgrad(hn, d_up, "wgrad_up", column_shards=4)
    d_ao, d_go, d_gz, d_mix_nw = _mix_bwd(tok(ao), tok(go), gz, mix_nw, w_out_b, dh)
    p_out = _wgrad(mix, dh, "wgrad_out").reshape(4, -1, D)
    d_gq, d_gk, d_gv, d_gab, d_scal, s_up, s_out, s_down_a = _gdn_chunk_bwd(gq, gk, gv, seq(gab), scal, states, seq(d_go),
                                                                            [p_up, p_out, p_down_a])
    early = ("w_up", "w_out", "w_down")
    dxq, dxk, dxv, dcq, dck, dcv, g_up, g_out = _gdn_pre_bwd(seq(gqkv), conv_full, d_gq, d_gk, d_gv,
                                                             [_sum_slots(s_up, "sum_w_up"), _sum_slots(s_out, "sum_w_out")])
    dq, dk, dv, s_down_b = _attn_bwd(seq(q), seq(k), seq(v_att), ao, lse, seq(d_ao), [p_down_b])
    d_lat, d_ln, d_w_mla, d_qk_nw, g_down = _mla_pre_bwd(lat, pos, ln_w, w_mla, qk_nw, rope_rows, tok(dq), tok(dk), tok(dv),
                                                         [_sum_slots([s_down_a, s_down_b], "sum_w_down")])
    early_grads = [g_up, g_out, g_down]
    d_pieces = [d_lat, tok(dxq), tok(dxk), tok(dxv), d_gz, tok(d_gab)]
    p_in = _narrow_w_in_t(_wgrad_pieces(d_pieces, xn, "wgrad_in")).reshape(4, -1, D)
    p_uq, p_ukv = (_column_shards(a).astype(BF16) for a in _unstack_mla(d_w_mla))
    grad_x2, d_attn_nw, s_in, s_uq, s_ukv = _in_proj_bwd(d_pieces, w_in_p, x2, attn_norm_w, dh, [p_in, p_uq, p_ukv])
    small_buf, conv_buf = _pack_small_partials(d_attn_nw, d_mlp_nw, d_ln, d_qk_nw, d_mix_nw, d_scal, (dcq, dck, dcv), sq)

    late = ("w_in", "w_uq", "w_ukv")
    *late_grads, s_small, s_conv = _exchange_halves([_sum_slots(s, "sum_" + n) for n, s in zip(late, (s_in, s_uq, s_ukv))],
                                                    [small_buf, conv_buf])
    names = early + late
    grad = {n: g.reshape(-1, g.shape[-1]) for n, g in zip(names, list(early_grads) + list(late_grads))}

    loss, g_small, delta, new_m, new_v = _adamw_small(s_small, s_conv, w, m, v)
    grad.update(g_small)
    for n in names:
        if n == "w_in":
            stored = lambda a: jnp.transpose(a, (2, 0, 1))
            outs = _adamw(stored(w[n]), grad[n][:, None, :], stored(m[n]), stored(v[n]), "adamw_" + n)
            grad[n], delta[n], new_m[n], new_v[n] = (jnp.transpose(a, (1, 2, 0)) for a in (grad[n][:, None, :], *outs))
        else:
            delta[n], new_m[n], new_v[n] = _adamw(local[n], grad[n], m[n][0], v[n][0], "adamw_" + n)
    def in_order(d):
        return [d[n].reshape(w[n].shape) for n in WEIGHTS]

    return (loss.reshape(()), grad_x2.reshape(B, S, D), *in_order(grad), *in_order(delta), *in_order(new_m), *in_order(new_v))
```

```python
import functools
import math

import jax
import jax.numpy as jnp
from jax import lax
from jax.experimental import pallas as pl
from jax.experimental.pallas import tpu as pltpu

F32 = jnp.float32
BF16 = jnp.bfloat16
MESH = pl.DeviceIdType.MESH

EPS = 1e-6
HEADS = 4
HEAD_DIM = 128
ROPE_DIM = 64
ROPE_HALF = 32
QK_DIM = 192
QK_PAD = 256
LORA = 256
CHUNK = 64
CONV_TAPS = 4
ROPE_THETA = 10000.0
ATTN_SCALE = QK_DIM ** -0.5

LAT_W = 640
GQKV_W = 3 * HEADS * HEAD_DIM
GZ_W = HEADS * HEAD_DIM
GAB_W = 128
PROJ_SPLITS = ((0, LAT_W), (LAT_W, LAT_W + GQKV_W), (LAT_W + GQKV_W, LAT_W + GQKV_W + GZ_W),
               (LAT_W + GQKV_W + GZ_W, LAT_W + GQKV_W + GZ_W + GAB_W))
PROJ_W = PROJ_SPLITS[-1][1]

ADAM_LR = 0.001
ADAM_B1 = 0.9
ADAM_B2 = 0.999
ADAM_EPS = 1e-08
ADAM_WD = 0.01
ADAM_STEP = 10

TOKEN_TILE = 512
MLP_TOKEN_TILE = 512
FF_TILE = 1024
ATTN_TILE = 512
ATTN_HEADS_PER_STEP = 2
SWAP_CHUNKS = 4
WGRAD_OUT_BYTES = 8 * 1024 * 1024
VMEM_LIMIT = 48 * 1024 * 1024

SHARDED = ("w_in", "w_uq", "w_ukv", "conv_w", "w_out", "w_up", "w_down")
WEIGHTS = ("attn_norm_w", "w_in", "q_lat_norm_w", "w_uq", "kv_lat_norm_w", "w_ukv", "q_norm_w", "k_norm_w", "mla_out_norm_w",
           "conv_w", "a_log", "dt_bias", "gdn_norm_w", "w_out", "mlp_norm_w", "w_up", "w_down")


def _sds(shape, dtype):
    return jax.ShapeDtypeStruct(shape, dtype)


def _params(semantics):
    return pltpu.CompilerParams(dimension_semantics=semantics, vmem_limit_bytes=VMEM_LIMIT)


def _block(n):
    for b in (512, 256, 128):
        if n % b == 0:
            return b
    return n


def _dg(a, b, ca, cb, prec):
    lead = a.ndim - 2
    batch = (tuple(range(lead)),) * 2
    return lax.dot_general(a, b, (((ca + lead,), (cb + lead,)), batch), precision=prec, preferred_element_type=F32)


def _split_bf16(a):
    hi = a.astype(BF16)
    return hi, (a - hi.astype(F32)).astype(BF16)


def _dot_bf16(a, b, ca, cb):
    return _dg(a.astype(BF16), b.astype(BF16), ca, cb, None)


def _dot_bf16x3(a, b, ca, cb):
    a_hi, a_lo = _split_bf16(a)
    b_hi, b_lo = _split_bf16(b)
    lead = a.ndim - 2
    return _dg(jnp.concatenate([a_hi, a_hi, a_lo], axis=ca + lead), jnp.concatenate([b_hi, b_lo, b_hi], axis=cb + lead), ca, cb, None)


def _matmul_family(dot):
    def nn_raw(a, b):
        return dot(a, b, 1, 0)

    def nt_raw(a, b):
        return dot(a, b, 1, 1)

    def tn_raw(a, b):
        return dot(a, b, 0, 0)

    @jax.custom_vjp
    def nn(a, b):
        return nn_raw(a, b)

    nn.defvjp(lambda a, b: (nn_raw(a, b), (a, b)), lambda r, g: (nt_raw(g, r[1]), tn_raw(r[0], g)))

    @jax.custom_vjp
    def nt(a, b):
        return nt_raw(a, b)

    nt.defvjp(lambda a, b: (nt_raw(a, b), (a, b)), lambda r, g: (nn_raw(g, r[1]), tn_raw(g, r[0])))

    @jax.custom_vjp
    def tn(a, b):
        return tn_raw(a, b)

    tn.defvjp(lambda a, b: (tn_raw(a, b), (a, b)), lambda r, g: (nt_raw(r[1], g), nn_raw(r[0], g)))
    return nn, nt, tn


_bf_nn, _bf_nt, _bf_tn = _matmul_family(_dot_bf16)
_hi_nn, _hi_nt, _hi_tn = _matmul_family(_dot_bf16x3)


def _lower_powers(lmat):
    powers = []
    while 2 ** (len(powers) + 1) < lmat.shape[-1]:
        powers.append(_dot_bf16x3(powers[-1] if powers else lmat, powers[-1] if powers else lmat, 1, 0))
    return powers


@jax.custom_vjp
def _unit_lower_solve(lmat, rhs):
    return _unit_lower_solve_fwd(lmat, rhs)[0]


def _unit_lower_solve_fwd(lmat, rhs):
    powers = _lower_powers(lmat)
    x = rhs - _dot_bf16x3(lmat, rhs, 1, 0)
    for p in powers:
        x = x + _dot_bf16x3(p, x, 1, 0)
    return x, (lmat, powers, x)


def _unit_lower_solve_bwd(res, g):
    lmat, powers, x = res
    y = g - _dot_bf16x3(lmat, g, 0, 0)
    for p in powers:
        y = y + _dot_bf16x3(p, y, 0, 0)
    return -_dot_bf16x3(y, x, 1, 1), y


_unit_lower_solve.defvjp(_unit_lower_solve_fwd, _unit_lower_solve_bwd)


@jax.custom_vjp
def _lane_halves(x):
    n = x.shape[-1] // 2
    return x[..., :n], x[..., n:]


_lane_halves.defvjp(lambda x: (_lane_halves(x), None), lambda _, g: (jnp.concatenate(g, axis=-1),))


@jax.custom_vjp
def _row_halves(x):
    n = x.shape[-2] // 2
    return x[..., :n, :], x[..., n:, :]


_row_halves.defvjp(lambda x: (_row_halves(x), None), lambda _, g: (jnp.concatenate(g, axis=-2),))


@jax.custom_vjp
def _swap_halves(t):
    return pltpu.roll(t, 64, 1)


_swap_halves.defvjp(lambda t: (pltpu.roll(t, 64, 1), None), lambda _, g: (pltpu.roll(g, 64, 1),))


@functools.partial(jax.custom_vjp, nondiff_argnums=(2,))
def _shift_rows(x, keep, s):
    return pltpu.roll(x, s, 0) * keep


def _shift_rows_fwd(x, keep, s):
    return pltpu.roll(x, s, 0) * keep, keep


def _shift_rows_bwd(s, keep, g):
    return pltpu.roll(g * keep, keep.shape[0] - s, 0), jnp.zeros_like(keep)


_shift_rows.defvjp(_shift_rows_fwd, _shift_rows_bwd)


def _sigmoid(x):
    return 0.5 * jnp.tanh(0.5 * x) + 0.5


def _softplus(x):
    return jnp.maximum(x, 0.0) + jnp.log(1.0 + jnp.exp(jnp.minimum(x, -x)))


def _silu(x):
    return x * _sigmoid(x)


def _rms(x, w, n=None):
    n = x.shape[-1] if n is None else n
    r = lax.rsqrt(jnp.sum(x * x, axis=-1, keepdims=True) * (1.0 / n) + EPS)
    return x * r * w


def _rope(t, cos_f, sin_f):
    return t * cos_f + _swap_halves(t) * sin_f


def _rope_tables(pos_col, freq_row, sign_row):
    ang = pos_col.astype(F32) * freq_row
    return jnp.cos(ang), jnp.sin(ang) * sign_row


def _onehot_row(lane):
    return (lax.broadcasted_iota(jnp.int32, (1, 128), 1) == lane).astype(F32)


def _row_spec(tm, w):
    return pl.BlockSpec((tm, w), lambda i: (i, 0))


def _const_spec(shape):
    return pl.BlockSpec(shape, lambda *_: (0,) * len(shape))


def _in_proj_fwd(x2, w_an, w_in_p):
    T, D = x2.shape
    tm = min(TOKEN_TILE, T)

    def body(x_ref, wn_ref, w_ref, xn_ref, lat_ref, gqkv_ref, gz_ref, gab_ref):
        x = x_ref[...]
        r = lax.rsqrt(jnp.mean(x * x, axis=-1, keepdims=True) + EPS)
        xn = (x * r * wn_ref[...]).astype(BF16)
        xn_ref[...] = xn
        for ref, (a, b) in zip((lat_ref, gqkv_ref, gz_ref, gab_ref), PROJ_SPLITS):
            ref[...] = _dg(xn, w_ref[a:b, :], 1, 1, None)

    widths = [b - a for a, b in PROJ_SPLITS]
    return pl.pallas_call(
        body, grid=(T // tm,),
        in_specs=[_row_spec(tm, D), _const_spec((1, D)), _const_spec((PROJ_W, D))],
        out_specs=[_row_spec(tm, D)] + [_row_spec(tm, w) for w in widths],
        out_shape=[_sds((T, D), BF16)] + [_sds((T, w), F32) for w in widths],
        compiler_params=_params(("parallel",)), name="in_proj_fwd",
    )(x2, w_an, w_in_p)


def _in_proj_bwd(pieces, w_in_p, x2, w_an, dh, partials):
    T, D = x2.shape
    tm = min(TOKEN_TILE, T)
    widths = [p.shape[1] for p in pieces]
    starts = [sum(widths[:i]) for i in range(len(widths))]
    assert sum(widths) == PROJ_W
    npc, ns = len(pieces), len(partials)

    def body(*refs):
        piece_refs = refs[:npc]
        w_ref, x_ref, wn_ref, dh_ref = refs[npc:npc + 4]
        src_refs = refs[npc + 4:npc + 4 + ns]
        dx_ref, dwn_ref = refs[npc + 4 + ns:npc + 6 + ns]
        dst_refs = refs[npc + 6 + ns:npc + 6 + 2 * ns]
        sems = refs[npc + 6 + 2 * ns:]

        @pl.when(pl.program_id(0) == 0)
        def _():
            for start in _scatter_copies(src_refs, dst_refs, *sems)[0]:
                start()
            dwn_ref[...] = jnp.zeros_like(dwn_ref)

        dxn = jnp.zeros((tm, D), F32)
        for ref, a, width in zip(piece_refs, starts, widths):
            dxn += _dg(ref[...], w_ref[a:a + width, :], 1, 0, None)
        _, pull = jax.vjp(_rms, x_ref[...], wn_ref[...])
        dx, dwn = pull(dxn)
        dx_ref[...] = dx + dh_ref[...]
        dwn_ref[...] += dwn

        @pl.when(pl.program_id(0) == T // tm - 1)
        def _():
            for wait in _scatter_copies(src_refs, dst_refs, *sems)[1]:
                wait()

    return pl.pallas_call(
        body, grid=(T // tm,),
        in_specs=[_row_spec(tm, w) for w in widths] + [_const_spec((PROJ_W, D)), _row_spec(tm, D), _const_spec((1, D)),
                                                       _row_spec(tm, D)] + [_ANY] * ns,
        out_specs=[_row_spec(tm, D), _const_spec((1, D))] + [_ANY] * ns,
        out_shape=[_sds((T, D), F32), _sds((1, D), F32)] + [_scattered_shape(p) for p in partials],
        scratch_shapes=_scatter_scratch(ns),
        compiler_params=_params(("arbitrary",)), name="in_proj_bwd",
    )(*pieces, w_in_p, x2, w_an, dh, *partials)


def _wgrad_pieces(pieces, b, name):
    T, k2 = b.shape
    tt = min(TOKEN_TILE, T)
    widths = [p.shape[1] for p in pieces]
    starts = [sum(widths[:i]) for i in range(len(widths))]
    k1 = sum(widths)

    def body(*refs):
        piece_refs, (b_ref, o_ref, acc_ref) = refs[:len(pieces)], refs[len(pieces):]
        t = pl.program_id(0)

        @pl.when(t == 0)
        def _():
            acc_ref[...] = jnp.zeros_like(acc_ref)

        bt = b_ref[...].astype(BF16)
        for ref, r0, width in zip(piece_refs, starts, widths):
            acc_ref[r0:r0 + width, :] += jnp.dot(ref[...].T, bt, preferred_element_type=F32)

        @pl.when(t == T // tt - 1)
        def _():
            o_ref[...] = acc_ref[...].astype(o_ref.dtype)

    return pl.pallas_call(
        body, grid=(T // tt,),
        in_specs=[pl.BlockSpec((tt, w), lambda t: (t, 0)) for w in widths] + [pl.BlockSpec((tt, k2), lambda t: (t, 0))],
        out_specs=_const_spec((k1, k2)), out_shape=_sds((k1, k2), BF16), scratch_shapes=[pltpu.VMEM((k1, k2), F32)],
        compiler_params=_params(("arbitrary",)), name=name,
    )(*pieces, b)


def _wgrad(a, b, name, column_shards=1, out_dtype=BF16, lane_halves=False):
    T, k1 = a.shape
    k2 = b.shape[1]
    per_shard = k2 // column_shards
    tt = min(TOKEN_TILE, T)
    b1 = k1
    while b1 * k2 * 4 > WGRAD_OUT_BYTES and b1 % 256 == 0:
        b1 //= 2
    step = _block(per_shard)

    def body(a_ref, b_ref, *rest):
        o_refs, acc_ref = rest[:-1], rest[-1]
        t = pl.program_id(1)

        @pl.when(t == 0)
        def _():
            acc_ref[...] = jnp.zeros_like(acc_ref)

        a_t = a_ref[...].astype(BF16).T
        for c0 in range(0, k2, step):
            part = jnp.dot(a_t, b_ref[:, c0:c0 + step].astype(BF16), preferred_element_type=F32)
            if column_shards == 1:
                acc_ref[:, c0:c0 + step] += part
            else:
                acc_ref[c0 // per_shard, :, c0 % per_shard:c0 % per_shard + step] += part

        @pl.when(t == T // tt - 1)
        def _():
            if lane_halves:
                for k, o_ref in enumerate(o_refs):
                    o_ref[...] = acc_ref[:, k * (k2 // 2):(k + 1) * (k2 // 2)].astype(o_ref.dtype)
            else:
                o_refs[0][...] = acc_ref[...].astype(o_refs[0].dtype)

    if lane_halves:
        block = (b1, k2)
        out_spec = [pl.BlockSpec((b1, k2 // 2), lambda i, t: (i, 0))] * 2
        out_shape = [_sds((k1, k2 // 2), out_dtype)] * 2
    elif column_shards == 1:
        block, out_spec, out_shape = (b1, k2), pl.BlockSpec((b1, k2), lambda i, t: (i, 0)), _sds((k1, k2), out_dtype)
    else:
        block = (column_shards, b1, per_shard)
        out_spec, out_shape = pl.BlockSpec(block, lambda i, t: (0, i, 0)), _sds((column_shards, k1, per_shard), out_dtype)
    return pl.pallas_call(
        body, grid=(k1 // b1, T // tt),
        in_specs=[pl.BlockSpec((tt, b1), lambda i, t: (t, i)), pl.BlockSpec((tt, k2), lambda i, t: (t, 0))],
        out_specs=out_spec, out_shape=out_shape, scratch_shapes=[pltpu.VMEM(block, F32)],
        compiler_params=_params(("parallel", "arbitrary")), name=name,
    )(a, b)


def _mla_pre_fn(q_lat, kv_lat, kpe, ln_q, ln_kv, w_list, qn_n, qn_p, kn_n, kn_p, cos_f, sin_f):
    qn = _rms(q_lat, ln_q)
    kvn = _rms(kv_lat, ln_kv)
    kp = _rope(_rms(kpe, kn_p, ROPE_DIM), cos_f, sin_f)
    outs = []
    for h in range(HEADS):
        outs.append(_rms(_bf_nn(qn, w_list[h]), qn_n))
        outs.append(_rope(_rms(_bf_nn(qn, w_list[HEADS + h]), qn_p, ROPE_DIM), cos_f, sin_f))
        outs.append(_rms(_bf_nn(kvn, w_list[2 * HEADS + h]), kn_n))
        outs.append(_bf_nn(kvn, w_list[3 * HEADS + h]))
    return tuple(outs) + (kp,)


def _mla_pre_operands(lat_ref, pos_ref, ln_ref, w_ref, nw_ref, rope_ref):
    cos_f, sin_f = _rope_tables(pos_ref[...], rope_ref[0:1, :], rope_ref[1:2, :])
    diff = (lat_ref[:, 0:LORA], lat_ref[:, LORA:2 * LORA], lat_ref[:, 2 * LORA:LAT_W], ln_ref[0:1, :], ln_ref[1:2, :],
            [w_ref[i].astype(F32) for i in range(4 * HEADS)], nw_ref[0:1, :], nw_ref[1:2, :], nw_ref[2:3, :], nw_ref[3:4, :])
    return diff, cos_f, sin_f


def _mla_pre_fwd(lat, pos, ln_w, w_mla, nw, rope_rows):
    T = lat.shape[0]
    tm = min(TOKEN_TILE, T)

    def body(lat_ref, pos_ref, ln_ref, w_ref, nw_ref, rope_ref, q_ref, k_ref, v_ref):
        diff, cos_f, sin_f = _mla_pre_operands(lat_ref, pos_ref, ln_ref, w_ref, nw_ref, rope_ref)
        outs = _mla_pre_fn(*diff, cos_f, sin_f)
        kp = outs[-1].astype(BF16)
        for h in range(HEADS):
            q_n, q_p, k_n, v = outs[4 * h:4 * h + 4]
            q_ref[:, h * QK_PAD:h * QK_PAD + HEAD_DIM] = q_n.astype(BF16)
            q_ref[:, h * QK_PAD + HEAD_DIM:(h + 1) * QK_PAD] = q_p.astype(BF16)
            k_ref[:, h * QK_PAD:h * QK_PAD + HEAD_DIM] = k_n.astype(BF16)
            k_ref[:, h * QK_PAD + HEAD_DIM:(h + 1) * QK_PAD] = kp
            v_ref[:, h * HEAD_DIM:(h + 1) * HEAD_DIM] = v.astype(BF16)

    return pl.pallas_call(
        body, grid=(T // tm,),
        in_specs=[_row_spec(tm, LAT_W), _row_spec(tm, 1), _const_spec((2, LORA)), _const_spec((4 * HEADS, LORA, 128)),
                  _const_spec((8, 128)), _const_spec((8, 128))],
        out_specs=[_row_spec(tm, HEADS * QK_PAD), _row_spec(tm, HEADS * QK_PAD), _row_spec(tm, HEADS * HEAD_DIM)],
        out_shape=[_sds((T, HEADS * QK_PAD), BF16), _sds((T, HEADS * QK_PAD), BF16), _sds((T, HEADS * HEAD_DIM), BF16)],
        compiler_params=_params(("parallel",)), name="mla_pre_fwd",
    )(lat, pos, ln_w, w_mla, nw, rope_rows)


def _mla_pre_bwd(lat, pos, ln_w, w_mla, nw, rope_rows, dq, dk, dv, halves):
    T = lat.shape[0]
    tm = min(TOKEN_TILE, T)
    ns = len(halves)

    def body(*refs):
        lat_ref, pos_ref, ln_ref, w_ref, nw_ref, rope_ref, dq_ref, dk_ref, dv_ref = refs[:9]
        src_refs = refs[9:9 + ns]
        dlat_ref, dln_ref, dw_ref, dnw_ref = refs[9 + ns:13 + ns]
        dst_refs = refs[13 + ns:13 + 2 * ns]
        sems = refs[13 + 2 * ns:]

        @pl.when(pl.program_id(0) == 0)
        def _():
            for start in _swap_copies(src_refs, dst_refs, *sems)[0]:
                start()
            dln_ref[...] = jnp.zeros_like(dln_ref)
            dw_ref[...] = jnp.zeros_like(dw_ref)
            dnw_ref[...] = jnp.zeros_like(dnw_ref)

        diff, cos_f, sin_f = _mla_pre_operands(lat_ref, pos_ref, ln_ref, w_ref, nw_ref, rope_ref)
        _, pull = jax.vjp(lambda *a: _mla_pre_fn(*a, cos_f, sin_f), *diff)
        cts = []
        d_kp = jnp.zeros((tm, 128), F32)
        for h in range(HEADS):
            cts.append(dq_ref[:, h * QK_PAD:h * QK_PAD + HEAD_DIM])
            cts.append(dq_ref[:, h * QK_PAD + HEAD_DIM:(h + 1) * QK_PAD])
            cts.append(dk_ref[:, h * QK_PAD:h * QK_PAD + HEAD_DIM])
            cts.append(dv_ref[:, h * HEAD_DIM:(h + 1) * HEAD_DIM])
            d_kp += dk_ref[:, h * QK_PAD + HEAD_DIM:(h + 1) * QK_PAD]
        d_ql, d_kvl, d_kpe, d_lnq, d_lnkv, d_w, d_qn_n, d_qn_p, d_kn_n, d_kn_p = pull(tuple(cts) + (d_kp,))
        dlat_ref[:, 0:LORA] = d_ql.astype(BF16)
        dlat_ref[:, LORA:2 * LORA] = d_kvl.astype(BF16)
        dlat_ref[:, 2 * LORA:LAT_W] = d_kpe.astype(BF16)
        dln_ref[0:1, :] += d_lnq
        dln_ref[1:2, :] += d_lnkv
        for i in range(4 * HEADS):
            dw_ref[i] += d_w[i]
        for i, d in enumerate((d_qn_n, d_qn_p, d_kn_n, d_kn_p)):
            dnw_ref[i:i + 1, :] += d

        @pl.when(pl.program_id(0) == T // tm - 1)
        def _():
            for wait in _swap_copies(src_refs, dst_refs, *sems)[1]:
                wait()

    return pl.pallas_call(
        body, grid=(T // tm,),
        in_specs=[_row_spec(tm, LAT_W), _row_spec(tm, 1), _const_spec((2, LORA)), _const_spec((4 * HEADS, LORA, 128)),
                  _const_spec((8, 128)), _const_spec((8, 128)),
                  _row_spec(tm, HEADS * QK_PAD), _row_spec(tm, HEADS * QK_PAD), _row_spec(tm, HEADS * HEAD_DIM)] + [_ANY] * ns,
        out_specs=[_row_spec(tm, LAT_W), _const_spec((2, LORA)), _const_spec((4 * HEADS, LORA, 128)), _const_spec((8, 128))]
                  + [_ANY] * ns,
        out_shape=[_sds((T, LAT_W), BF16), _sds((2, LORA), F32), _sds((4 * HEADS, LORA, 128), F32), _sds((8, 128), F32)]
                  + [_swapped_shape(h) for h in halves],
        scratch_shapes=_swap_scratch(ns),
        compiler_params=_params(("arbitrary",)), name="mla_pre_bwd",
    )(lat, pos, ln_w, w_mla, nw, rope_rows, dq, dk, dv, *halves)


def _causal_mask(i, j, tq, tk):
    row = i * tq + lax.broadcasted_iota(jnp.int32, (tq, tk), 0)
    col = j * tk + lax.broadcasted_iota(jnp.int32, (tq, tk), 1)
    return col <= row


def _attn_fwd(q, k, v, shards):
    B, S, _ = q.shape
    t = min(ATTN_TILE, S)
    nq = S // t
    ns = len(shards)

    hp = ATTN_HEADS_PER_STEP
    qk = lambda h: slice(h * QK_PAD, (h + 1) * QK_PAD)
    vd = lambda h: slice(h * HEAD_DIM, (h + 1) * HEAD_DIM)

    def body(*refs):
        q_ref, k_ref, v_ref = refs[:3]
        src_refs = refs[3:3 + ns]
        o_ref, lse_ref = refs[3 + ns:5 + ns]
        dst_refs = refs[5 + ns:5 + 2 * ns]
        sems = refs[5 + 2 * ns:]
        b, g, i = pl.program_id(0), pl.program_id(1), pl.program_id(2)
        qb = [q_ref[0, :, qk(h)] for h in range(hp)]

        step_no = (b * (HEADS // hp) + g) * nq + i
        for phase, at in enumerate((0, (3 * B * (HEADS // hp) * nq) // 4)):
            @pl.when(step_no == at)
            def _(phase=phase):
                for call in _gather_copies(src_refs, dst_refs, *sems)[phase]:
                    call()

        def step(j, carry, diagonal):
            rows = pl.ds(pl.multiple_of(j * t, t), t)
            s = [_dg(qb[h], k_ref[0, rows, qk(h)], 1, 1, None) * ATTN_SCALE for h in range(hp)]
            if diagonal:
                keep = _causal_mask(0, 0, t, t)
                s = [jnp.where(keep, x, -1e30) for x in s]
            m_new = [jnp.maximum(carry[h][0], jnp.max(s[h], axis=-1, keepdims=True)) for h in range(hp)]
            p = [jnp.exp(s[h] - m_new[h]) for h in range(hp)]
            alpha = [jnp.exp(carry[h][0] - m_new[h]) for h in range(hp)]
            l = [alpha[h] * carry[h][1] + jnp.sum(p[h], axis=-1, keepdims=True) for h in range(hp)]
            pv = [jnp.dot(p[h].astype(BF16), v_ref[0, rows, vd(h)], preferred_element_type=F32) for h in range(hp)]
            return tuple((m_new[h], l[h], alpha[h] * carry[h][2] + pv[h]) for h in range(hp))

        init = tuple((jnp.full((t, 1), -1e30, F32), jnp.zeros((t, 1), F32), jnp.zeros((t, HEAD_DIM), F32)) for _ in range(hp))
        below = lax.fori_loop(0, i, lambda j, carry: step(j, carry, False), init)
        for h, (m, l, acc) in enumerate(step(i, below, True)):
            o_ref[0, :, vd(h)] = acc / l
            lse_ref[0, h, 0] = (m + jnp.log(l)).T

        @pl.when((b == B - 1) & (g == HEADS // hp - 1) & (i == nq - 1))
        def _():
            for wait in _gather_copies(src_refs, dst_refs, *sems)[2]:
                wait()

    return pl.pallas_call(
        body, grid=(B, HEADS // hp, nq),
        in_specs=[pl.BlockSpec((1, t, hp * QK_PAD), lambda b, g, i: (b, i, g)),
                  pl.BlockSpec((1, S, hp * QK_PAD), lambda b, g, i: (b, 0, g)),
                  pl.BlockSpec((1, S, hp * HEAD_DIM), lambda b, g, i: (b, 0, g))] + [_ANY] * ns,
        out_specs=[pl.BlockSpec((1, t, hp * HEAD_DIM), lambda b, g, i: (b, i, g)),
                   pl.BlockSpec((1, hp, 1, 1, t), lambda b, g, i: (b, g, i, 0, 0))] + [_ANY] * ns,
        out_shape=[_sds((B, S, HEADS * HEAD_DIM), F32), _sds((B, HEADS, nq, 1, t), F32)] + [_sds((4,) + s.shape, s.dtype) for s in shards],
        scratch_shapes=_gather_scratch(ns),
        compiler_params=_params(("arbitrary", "arbitrary", "arbitrary")), name="attn_fwd",
    )(q, k, v, *shards)


def _attn_bwd(q, k, v, o, lse, do, partials):
    B, S, _ = q.shape
    t = min(ATTN_TILE, S)
    nq = S // t
    ns = len(partials)

    hp = ATTN_HEADS_PER_STEP
    qk = lambda h: slice(h * QK_PAD, (h + 1) * QK_PAD)
    vd = lambda h: slice(h * HEAD_DIM, (h + 1) * HEAD_DIM)
    heads = range(hp)

    def body(*refs):
        q_ref, k_ref, v_ref, o_ref, lse_ref, do_ref = refs[:6]
        src_refs = refs[6:6 + ns]
        dq_ref, dk_ref, dv_ref = refs[6 + ns:9 + ns]
        dst_refs = refs[9 + ns:9 + 2 * ns]
        dsum_ref, send_sems, recv_sems, local_sems = refs[9 + 2 * ns:]
        b, g, j = pl.program_id(0), pl.program_id(1), pl.program_id(2)

        @pl.when((b == 0) & (g == 0) & (j == 0))
        def _():
            for start in _scatter_copies(src_refs, dst_refs, send_sems, recv_sems, local_sems)[0]:
                start()

        @pl.when(j == 0)
        def _():
            dq_ref[...] = jnp.zeros_like(dq_ref)
            for h in heads:
                for blk in range(nq):
                    rows = slice(blk * t, (blk + 1) * t)
                    dsum_ref[h, blk] = jnp.sum(do_ref[0, rows, vd(h)] * o_ref[0, rows, vd(h)], axis=-1, keepdims=True).T

        kb = [k_ref[0, :, qk(h)] for h in heads]
        vb = [v_ref[0, :, vd(h)] for h in heads]

        def step(i, carry, diagonal):
            rows = pl.ds(pl.multiple_of(i * t, t), t)
            qb = [q_ref[0, rows, qk(h)] for h in heads]
            dob = [do_ref[0, rows, vd(h)].astype(BF16) for h in heads]
            s = [_dg(kb[h], qb[h], 1, 1, None) * ATTN_SCALE for h in heads]
            p = [jnp.exp(s[h] - lse_ref[0, h, i]) for h in heads]
            if diagonal:
                key = lax.broadcasted_iota(jnp.int32, (t, t), 0)
                query = lax.broadcasted_iota(jnp.int32, (t, t), 1)
                p = [jnp.where(key <= query, x, 0.0) for x in p]
            dp = [_dg(vb[h], dob[h], 1, 1, None) for h in heads]
            dv = [carry[h][1] + jnp.dot(p[h].astype(BF16), dob[h], preferred_element_type=F32) for h in heads]
            ds = [(p[h] * (dp[h] - dsum_ref[h, i]) * ATTN_SCALE).astype(BF16) for h in heads]
            for h in heads:
                dq_ref[0, rows, qk(h)] += _dg(ds[h], kb[h], 0, 0, None)
            return tuple((carry[h][0] + jnp.dot(ds[h], qb[h], preferred_element_type=F32), dv[h]) for h in heads)

        zeros = tuple((jnp.zeros((t, QK_PAD), F32), jnp.zeros((t, HEAD_DIM), F32)) for _ in heads)
        on_diagonal = step(j, zeros, True)
        done = lax.fori_loop(j + 1, nq, lambda i, carry: step(i, carry, False), on_diagonal)
        for h, (dk, dv) in enumerate(done):
            dk_ref[0, :, qk(h)] = dk
            dv_ref[0, :, vd(h)] = dv

        @pl.when((b == B - 1) & (g == HEADS // hp - 1) & (j == nq - 1))
        def _():
            for wait in _scatter_copies(src_refs, dst_refs, send_sems, recv_sems, local_sems)[1]:
                wait()

    return pl.pallas_call(
        body, grid=(B, HEADS // hp, nq),
        in_specs=[pl.BlockSpec((1, S, hp * QK_PAD), lambda b, g, j: (b, 0, g)),
                  pl.BlockSpec((1, t, hp * QK_PAD), lambda b, g, j: (b, j, g)),
                  pl.BlockSpec((1, t, hp * HEAD_DIM), lambda b, g, j: (b, j, g)),
                  pl.BlockSpec((1, S, hp * HEAD_DIM), lambda b, g, j: (b, 0, g)),
                  pl.BlockSpec((1, hp, nq, 1, t), lambda b, g, j: (b, g, 0, 0, 0)),
                  pl.BlockSpec((1, S, hp * HEAD_DIM), lambda b, g, j: (b, 0, g))] + [_ANY] * ns,
        out_specs=[pl.BlockSpec((1, S, hp * QK_PAD), lambda b, g, j: (b, 0, g)),
                   pl.BlockSpec((1, t, hp * QK_PAD), lambda b, g, j: (b, j, g)),
                   pl.BlockSpec((1, t, hp * HEAD_DIM), lambda b, g, j: (b, j, g))] + [_ANY] * ns,
        out_shape=[_sds((B, S, HEADS * QK_PAD), F32), _sds((B, S, HEADS * QK_PAD), F32), _sds((B, S, HEADS * HEAD_DIM), F32)]
                  + [_scattered_shape(p) for p in partials],
        scratch_shapes=[pltpu.VMEM((hp, nq, 1, t), F32)] + _scatter_scratch(ns),
        compiler_params=_params(("arbitrary", "arbitrary", "arbitrary")), name="attn_bwd",
    )(q, k, v, o, lse, do, *partials)


def _gdn_pre_fn(xq, xk, xv, wq, wk, wv, keeps):
    def conv_silu(x, w):
        acc = x * w[3]
        for s in (1, 2, 3):
            acc = acc + _shift_rows(x, keeps[s - 1], s) * w[3 - s]
        return _silu(acc)

    def l2(x):
        return x * lax.rsqrt(jnp.sum(x * x, axis=-1, keepdims=True) + EPS)

    return l2(conv_silu(xq, wq)) * (HEAD_DIM ** -0.5), l2(conv_silu(xk, wk)), conv_silu(xv, wv)


def _gdn_pre_specs(S):
    x_specs = [pl.BlockSpec((1, S, HEAD_DIM), lambda h, b, g=g: (b, 0, g * HEADS + h)) for g in range(3)]
    w_specs = [pl.BlockSpec((CONV_TAPS, HEAD_DIM), lambda h, b, g=g: (0, g * HEADS + h)) for g in range(3)]
    out_spec = pl.BlockSpec((1, S, HEAD_DIM), lambda h, b: (b, 0, h))
    return x_specs, w_specs, out_spec


def _row_keeps(S):
    t = lax.broadcasted_iota(jnp.int32, (S, HEAD_DIM), 0)
    return [(t >= s).astype(F32) for s in (1, 2, 3)]


def _gdn_pre_fwd(gqkv, conv_w):
    B, S, _ = gqkv.shape
    x_specs, w_specs, out_spec = _gdn_pre_specs(S)

    def body(xq_ref, xk_ref, xv_ref, wq_ref, wk_ref, wv_ref, q_ref, k_ref, v_ref):
        taps = [[w[i:i + 1, :] for i in range(CONV_TAPS)] for w in (wq_ref, wk_ref, wv_ref)]
        q, k, v = _gdn_pre_fn(xq_ref[0], xk_ref[0], xv_ref[0], *taps, _row_keeps(S))
        q_ref[0], k_ref[0], v_ref[0] = q, k, v

    return pl.pallas_call(
        body, grid=(HEADS, B), in_specs=x_specs + w_specs, out_specs=[out_spec] * 3,
        out_shape=[_sds((B, S, HEADS * HEAD_DIM), F32)] * 3,
        compiler_params=_params(("parallel", "parallel")), name="gdn_pre_fwd",
    )(gqkv, gqkv, gqkv, conv_w, conv_w, conv_w)


def _gdn_pre_bwd(gqkv, conv_w, dq, dk, dv, halves):
    B, S, _ = gqkv.shape
    x_specs, w_specs, out_spec = _gdn_pre_specs(S)
    dw_spec = pl.BlockSpec((CONV_TAPS, HEAD_DIM), lambda h, b: (0, h))
    ns = len(halves)

    def body(*refs):
        xq_ref, xk_ref, xv_ref, wq_ref, wk_ref, wv_ref, dq_ref, dk_ref, dv_ref = refs[:9]
        src_refs = refs[9:9 + ns]
        dxq_ref, dxk_ref, dxv_ref, dwq_ref, dwk_ref, dwv_ref = refs[9 + ns:15 + ns]
        dst_refs = refs[15 + ns:15 + 2 * ns]
        sems = refs[15 + 2 * ns:]
        first = (pl.program_id(0) == 0) & (pl.program_id(1) == 0)
        last = (pl.program_id(0) == HEADS - 1) & (pl.program_id(1) == B - 1)

        @pl.when(first)
        def _():
            for start in _swap_copies(src_refs, dst_refs, *sems)[0]:
                start()

        @pl.when(pl.program_id(1) == 0)
        def _():
            for r in (dwq_ref, dwk_ref, dwv_ref):
                r[...] = jnp.zeros_like(r)

        taps = [[w[i:i + 1, :] for i in range(CONV_TAPS)] for w in (wq_ref, wk_ref, wv_ref)]
        keeps = _row_keeps(S)
        _, pull = jax.vjp(lambda *a: _gdn_pre_fn(*a, keeps), xq_ref[0], xk_ref[0], xv_ref[0], *taps)
        dxq, dxk, dxv, dwq, dwk, dwv = pull((dq_ref[0], dk_ref[0], dv_ref[0]))
        dxq_ref[0], dxk_ref[0], dxv_ref[0] = dxq.astype(BF16), dxk.astype(BF16), dxv.astype(BF16)
        for ref, dw in ((dwq_ref, dwq), (dwk_ref, dwk), (dwv_ref, dwv)):
            for i in range(CONV_TAPS):
                ref[i:i + 1, :] += dw[i]

        @pl.when(last)
        def _():
            for wait in _swap_copies(src_refs, dst_refs, *sems)[1]:
                wait()

    hw = HEADS * HEAD_DIM
    return pl.pallas_call(
        body, grid=(HEADS, B), in_specs=x_specs + w_specs + [out_spec] * 3 + [_ANY] * ns,
        out_specs=[out_spec] * 3 + [dw_spec] * 3 + [_ANY] * ns,
        out_shape=[_sds((B, S, hw), BF16)] * 3 + [_sds((CONV_TAPS, hw), F32)] * 3 + [_swapped_shape(h) for h in halves],
        scratch_shapes=_swap_scratch(ns),
        compiler_params=_params(("arbitrary", "arbitrary")), name="gdn_pre_bwd",
    )(gqkv, gqkv, gqkv, conv_w, conv_w, conv_w, dq, dk, dv, *halves)


def _chunk_masks():
    i = lax.broadcasted_iota(jnp.int32, (CHUNK, CHUNK), 0)
    j = lax.broadcasted_iota(jnp.int32, (CHUNK, CHUNK), 1)
    lower, after = (j <= i).astype(F32), (j > i).astype(F32)
    return {"le": lower, "le_gt": jnp.concatenate([lower, after], axis=0), "strict": (j < i).astype(F32)}


def _gdn_chunk_fn(groups, masks):
    lane = lax.broadcasted_iota(jnp.int32, (groups, 1, 128), 2)
    head = lax.broadcasted_iota(jnp.int32, (groups, 1, 128), 0) % HEADS
    pick_a, pick_b = (lane == head).astype(F32), (lane == head + HEADS).astype(F32)
    lower, lower_after, strict = (jnp.broadcast_to(masks[n], (groups,) + masks[n].shape) for n in ("le", "le_gt", "strict"))
    ones_row = jnp.ones((1, 1, HEAD_DIM), F32)

    def f(q, k, v, gab, a_row, dt_row, state):
        ga = jnp.sum(gab * pick_a, axis=2, keepdims=True)
        gb = jnp.sum(gab * pick_b, axis=2, keepdims=True)
        a_log = jnp.sum(a_row * pick_a, axis=2, keepdims=True)
        dt_bias = jnp.sum(dt_row * pick_a, axis=2, keepdims=True)
        beta = _sigmoid(gb)
        g = -jnp.exp(a_log) * _softplus(ga + dt_bias)
        g_wide = g * ones_row
        cum, rest = _row_halves(_hi_nn(lower_after, g_wide))
        total = jnp.sum(g_wide, axis=1, keepdims=True)
        diff = _hi_nn(lower, g * strict)
        decay = lower * jnp.exp(diff)
        e_cum = jnp.exp(cum)
        kk, qk = _row_halves(_bf_nt(jnp.concatenate([k, q], axis=1), k))
        lmat = strict * (beta * kk * decay)
        u, w = _lane_halves(_unit_lower_solve(lmat, jnp.concatenate([v * beta, k * (beta * e_cum)], axis=2)))
        w_state, q_state = _row_halves(_bf_nn(jnp.concatenate([w, q * e_cum], axis=1), state))
        v_new = u - w_state
        o = q_state + _bf_nn(qk * decay, v_new)
        new_state = state * jnp.exp(total) + _bf_tn(k * jnp.exp(rest), v_new)
        return o, new_state

    return f


def _gdn_chunk_fwd(q, k, v, gab, scal, shards):
    B, S, W = q.shape
    N = S // CHUNK
    ns = len(shards)

    def body(*refs):
        q_ref, k_ref, v_ref, gab_ref, sc_ref = refs[:5]
        src_refs = refs[5:5 + ns]
        o_ref, st_ref = refs[5 + ns:7 + ns]
        dst_refs = refs[7 + ns:7 + 2 * ns]
        state_ref, send_sems, recv_sems, local_sems = refs[7 + 2 * ns:]
        n = pl.program_id(0)

        @pl.when(n == 0)
        def _():
            for start in _gather_copies(src_refs, dst_refs, send_sems, recv_sems, local_sems)[0]:
                start()
            state_ref[...] = jnp.zeros_like(state_ref)

        @pl.when(n == (2 * N) // 3)
        def _():
            for pass_on in _gather_copies(src_refs, dst_refs, send_sems, recv_sems, local_sems)[1]:
                pass_on()

        groups = [(b, h) for b in range(B) for h in range(HEADS)]
        gather = lambda ref: jnp.stack([ref[b, :, h * HEAD_DIM:(h + 1) * HEAD_DIM] for b, h in groups])
        state = state_ref[...]
        for i, (b, h) in enumerate(groups):
            st_ref[b, 0, h] = state[i]
        o, new_state = _gdn_chunk_fn(len(groups), _chunk_masks())(
            gather(q_ref), gather(k_ref), gather(v_ref), jnp.stack([gab_ref[b] for b, _ in groups]), sc_ref[0:1, :], sc_ref[1:2, :], state)
        for i, (b, h) in enumerate(groups):
            o_ref[b, :, h * HEAD_DIM:(h + 1) * HEAD_DIM] = o[i]
        state_ref[...] = new_state

        @pl.when(n == N - 1)
        def _():
            for wait in _gather_copies(src_refs, dst_refs, send_sems, recv_sems, local_sems)[2]:
                wait()

    seq = pl.BlockSpec((B, CHUNK, W), lambda n: (0, n, 0))
    return pl.pallas_call(
        body, grid=(N,),
        in_specs=[seq, seq, seq, pl.BlockSpec((B, CHUNK, GAB_W), lambda n: (0, n, 0)), _const_spec((8, 128))] + [_ANY] * ns,
        out_specs=[seq, pl.BlockSpec((B, 1, HEADS, HEAD_DIM, HEAD_DIM), lambda n: (0, n, 0, 0, 0))] + [_ANY] * ns,
        out_shape=[_sds((B, S, W), F32), _sds((B, N, HEADS, HEAD_DIM, HEAD_DIM), F32)] + [_sds((4,) + s.shape, s.dtype) for s in shards],
        scratch_shapes=[pltpu.VMEM((B * HEADS, HEAD_DIM, HEAD_DIM), F32)] + _gather_scratch(ns),
        compiler_params=_params(("arbitrary",)), name="gdn_chunk_fwd",
    )(q, k, v, gab, scal, *shards)


def _gdn_chunk_bwd(q, k, v, gab, scal, states, do, partials):
    B, S, W = q.shape
    N = S // CHUNK
    ns = len(partials)

    def body(*refs):
        q_ref, k_ref, v_ref, gab_ref, sc_ref, st_ref, do_ref = refs[:7]
        src_refs = refs[7:7 + ns]
        dq_ref, dk_ref, dv_ref, dgab_ref, dsc_ref = refs[7 + ns:12 + ns]
        dst_refs = refs[12 + ns:12 + 2 * ns]
        dstate_ref, send_sems, recv_sems, local_sems = refs[12 + 2 * ns:]
        n = pl.program_id(0)

        @pl.when(n == 0)
        def _():
            for start in _scatter_copies(src_refs, dst_refs, send_sems, recv_sems, local_sems)[0]:
                start()
            dstate_ref[...] = jnp.zeros_like(dstate_ref)
            dsc_ref[...] = jnp.zeros_like(dsc_ref)

        groups = [(b, h) for b in range(B) for h in range(HEADS)]
        gather = lambda ref: jnp.stack([ref[b, :, h * HEAD_DIM:(h + 1) * HEAD_DIM] for b, h in groups])
        _, pull = jax.vjp(_gdn_chunk_fn(len(groups), _chunk_masks()), gather(q_ref), gather(k_ref), gather(v_ref),
                          jnp.stack([gab_ref[b] for b, _ in groups]), sc_ref[0:1, :], sc_ref[1:2, :],
                          jnp.stack([st_ref[b, 0, h] for b, h in groups]))
        dq, dk, dv, dg, d_a, d_dt, dstate = pull((gather(do_ref), dstate_ref[...]))
        for i, (b, h) in enumerate(groups):
            lanes = slice(h * HEAD_DIM, (h + 1) * HEAD_DIM)
            dq_ref[b, :, lanes] = dq[i]
            dk_ref[b, :, lanes] = dk[i]
            dv_ref[b, :, lanes] = dv[i]
        for b in range(B):
            dgab_ref[b] = sum(dg[b * HEADS + h] for h in range(HEADS)).astype(BF16)
        dstate_ref[...] = dstate
        dsc_ref[0:1, :] += d_a
        dsc_ref[1:2, :] += d_dt

        @pl.when(n == N - 1)
        def _():
            for wait in _scatter_copies(src_refs, dst_refs, send_sems, recv_sems, local_sems)[1]:
                wait()

    seq = pl.BlockSpec((B, CHUNK, W), lambda n: (0, N - 1 - n, 0))
    gab_spec = pl.BlockSpec((B, CHUNK, GAB_W), lambda n: (0, N - 1 - n, 0))
    return pl.pallas_call(
        body, grid=(N,),
        in_specs=[seq, seq, seq, gab_spec, _const_spec((8, 128)),
                  pl.BlockSpec((B, 1, HEADS, HEAD_DIM, HEAD_DIM), lambda n: (0, N - 1 - n, 0, 0, 0)), seq] + [_ANY] * ns,
        out_specs=[seq, seq, seq, gab_spec, _const_spec((8, 128))] + [_ANY] * ns,
        out_shape=[_sds((B, S, W), F32)] * 3 + [_sds((B, S, GAB_W), BF16), _sds((8, 128), F32)] + [_scattered_shape(p) for p in partials],
        scratch_shapes=[pltpu.VMEM((B * HEADS, HEAD_DIM, HEAD_DIM), F32)] + _scatter_scratch(ns),
        compiler_params=_params(("arbitrary",)), name="gdn_chunk_bwd",
    )(q, k, v, gab, scal, states, do, *partials)


def _mix_fn(ao, go, gz, w_mla, w_gdn):
    return tuple(_rms(ao[h], w_mla[h]) for h in range(HEADS)) + tuple(_rms(go[h], w_gdn) * _silu(gz[h]) for h in range(HEADS))


def _mix_operands(ao_ref, go_ref, gz_ref, nw_ref):
    blocks = lambda ref: [ref[:, h * HEAD_DIM:(h + 1) * HEAD_DIM] for h in range(HEADS)]
    return blocks(ao_ref), blocks(go_ref), blocks(gz_ref), [nw_ref[h:h + 1, :] for h in range(HEADS)], nw_ref[HEADS:HEADS + 1, :]


def _mix_fwd(ao, go, gz, nw, w_out, x2):
    T, D = x2.shape
    tm = min(TOKEN_TILE, T)
    MW = 2 * HEADS * HEAD_DIM

    def body(ao_ref, go_ref, gz_ref, nw_ref, w_ref, x_ref, mix_ref, h_ref):
        outs = _mix_fn(*_mix_operands(ao_ref, go_ref, gz_ref, nw_ref))
        for i, piece in enumerate(outs):
            mix_ref[:, i * HEAD_DIM:(i + 1) * HEAD_DIM] = piece.astype(BF16)
        h_ref[...] = x_ref[...] + jnp.dot(mix_ref[...], w_ref[...], preferred_element_type=F32)

    half = HEADS * HEAD_DIM
    return pl.pallas_call(
        body, grid=(T // tm,),
        in_specs=[_row_spec(tm, half), _row_spec(tm, half), _row_spec(tm, half), _const_spec((8, 128)), _const_spec((MW, D)),
                  _row_spec(tm, D)],
        out_specs=[_row_spec(tm, MW), _row_spec(tm, D)],
        out_shape=[_sds((T, MW), BF16), _sds((T, D), F32)],
        compiler_params=_params(("parallel",)), name="mix_fwd",
    )(ao, go, gz, nw, w_out, x2)


def _mix_bwd(ao, go, gz, nw, w_out, dh):
    T, D = dh.shape
    tm = min(TOKEN_TILE, T)
    MW = 2 * HEADS * HEAD_DIM
    half = HEADS * HEAD_DIM

    def body(ao_ref, go_ref, gz_ref, nw_ref, w_ref, dh_ref, dao_ref, dgo_ref, dgz_ref, dnw_ref):
        @pl.when(pl.program_id(0) == 0)
        def _():
            dnw_ref[...] = jnp.zeros_like(dnw_ref)

        d_mix = _dg(dh_ref[...].astype(BF16), w_ref[...], 1, 1, None)
        cts = tuple(d_mix[:, i * HEAD_DIM:(i + 1) * HEAD_DIM] for i in range(2 * HEADS))
        _, pull = jax.vjp(_mix_fn, *_mix_operands(ao_ref, go_ref, gz_ref, nw_ref))
        d_ao, d_go, d_gz, d_wm, d_wg = pull(cts)
        for h in range(HEADS):
            lanes = slice(h * HEAD_DIM, (h + 1) * HEAD_DIM)
            dao_ref[:, lanes] = d_ao[h]
            dgo_ref[:, lanes] = d_go[h]
            dgz_ref[:, lanes] = d_gz[h].astype(BF16)
            dnw_ref[h:h + 1, :] += d_wm[h]
        dnw_ref[HEADS:HEADS + 1, :] += d_wg

    return pl.pallas_call(
        body, grid=(T // tm,),
        in_specs=[_row_spec(tm, half), _row_spec(tm, half), _row_spec(tm, half), _const_spec((8, 128)), _const_spec((MW, D)),
                  _row_spec(tm, D)],
        out_specs=[_row_spec(tm, half)] * 3 + [_const_spec((8, 128))],
        out_shape=[_sds((T, half), F32)] * 2 + [_sds((T, half), BF16), _sds((8, 128), F32)],
        compiler_params=_params(("arbitrary",)), name="mix_bwd",
    )(ao, go, gz, nw, w_out, dh)


def _up_spec(w_up, tf):
    per_shard = w_up.shape[2] // tf
    return pl.BlockSpec((None, w_up.shape[1], tf), lambda i, j: (j // per_shard, 0, j % per_shard))


def _mlp_fwd(h2, w_mn, w_up, w_down, target):
    T, D = h2.shape
    FF = w_down.shape[0]
    tm, tf = min(MLP_TOKEN_TILE, T), min(FF_TILE, w_up.shape[2])
    nf = FF // tf

    def body(h_ref, wn_ref, wu_ref, wd_ref, t_ref, hn_ref, act_ref, dy_ref, sq_ref, acc_ref):
        j = pl.program_id(1)

        @pl.when(j == 0)
        def _():
            hn_ref[...] = _rms(h_ref[...], wn_ref[...]).astype(BF16)
            acc_ref[...] = jnp.zeros_like(acc_ref)

        up = jnp.dot(hn_ref[...], wu_ref[...], preferred_element_type=F32)
        act = jnp.square(jnp.maximum(up, 0.0)).astype(BF16)
        act_ref[...] = act
        acc_ref[...] += jnp.dot(act, wd_ref[...], preferred_element_type=F32)

        @pl.when(j == nf - 1)
        def _():
            err = h_ref[...] + acc_ref[...] - t_ref[...]
            dy_ref[...] = err * (1.0 / D)
            sq_ref[...] = jnp.zeros_like(sq_ref) + jnp.sum(err * err)

    tok = lambda w: pl.BlockSpec((tm, w), lambda i, j: (i, 0))
    return pl.pallas_call(
        body, grid=(T // tm, nf),
        in_specs=[tok(D), _const_spec((1, D)), _up_spec(w_up, tf), pl.BlockSpec((tf, D), lambda i, j: (j, 0)), tok(D)],
        out_specs=[tok(D), pl.BlockSpec((tm, tf), lambda i, j: (i, j)), tok(D), pl.BlockSpec((1, 8, 128), lambda i, j: (i, 0, 0))],
        out_shape=[_sds((T, D), BF16), _sds((T, FF), BF16), _sds((T, D), F32), _sds((T // tm, 8, 128), F32)],
        scratch_shapes=[pltpu.VMEM((tm, D), F32)],
        compiler_params=_params(("parallel", "arbitrary")), name="mlp_fwd",
    )(h2, w_mn, w_up, w_down, target)


def _mlp_bwd(h2, w_mn, act, w_up, w_down, dy):
    T, D = h2.shape
    FF = w_down.shape[0]
    tm, tf = min(MLP_TOKEN_TILE, T), min(FF_TILE, w_up.shape[2])
    nf = FF // tf

    def body(h_ref, wn_ref, act_ref, wu_ref, wd_ref, dy_ref, dh_ref, dup_ref, dwn_ref, acc_ref, dyb_ref):
        i, j = pl.program_id(0), pl.program_id(1)

        @pl.when((i == 0) & (j == 0))
        def _():
            dwn_ref[...] = jnp.zeros_like(dwn_ref)

        @pl.when(j == 0)
        def _():
            acc_ref[...] = jnp.zeros_like(acc_ref)
            dyb_ref[...] = dy_ref[...].astype(BF16)

        r = jnp.sqrt(act_ref[...].astype(F32))
        d_act = _dg(dyb_ref[...], wd_ref[...], 1, 1, None)
        d_up = (d_act * (2.0 * r)).astype(BF16)
        dup_ref[...] = d_up
        acc_ref[...] += _dg(d_up, wu_ref[...], 1, 1, None)

        @pl.when(j == nf - 1)
        def _():
            _, pull = jax.vjp(_rms, h_ref[...], wn_ref[...])
            dh, dwn = pull(acc_ref[...])
            dh_ref[...] = dh + dy_ref[...]
            dwn_ref[...] += dwn

    tok = lambda w: pl.BlockSpec((tm, w), lambda i, j: (i, 0))
    ff = pl.BlockSpec((tm, tf), lambda i, j: (i, j))
    return pl.pallas_call(
        body, grid=(T // tm, nf),
        in_specs=[tok(D), _const_spec((1, D)), ff, _up_spec(w_up, tf), pl.BlockSpec((tf, D), lambda i, j: (j, 0)), tok(D)],
        out_specs=[tok(D), ff, _const_spec((1, D))],
        out_shape=[_sds((T, D), F32), _sds((T, FF), BF16), _sds((1, D), F32)],
        scratch_shapes=[pltpu.VMEM((tm, D), F32), pltpu.VMEM((tm, D), BF16)],
        compiler_params=_params(("arbitrary", "arbitrary")), name="mlp_bwd",
    )(h2, w_mn, act, w_up, w_down, dy)


def _rope_pad(a):
    z = jnp.zeros(a.shape[:-1] + (ROPE_HALF,), a.dtype)
    return jnp.concatenate([a[..., :ROPE_HALF], z, a[..., ROPE_HALF:], z], axis=-1)


def _rope_unpad(a):
    return jnp.concatenate([a[..., :ROPE_HALF], a[..., 2 * ROPE_HALF:3 * ROPE_HALF]], axis=-1)


_G0 = 2 * LORA + ROPE_DIM
W_IN_COLS = _G0 + GQKV_W + GZ_W + 2 * HEADS


def _widen_w_in_t(w_t):
    z = jnp.zeros((ROPE_HALF, w_t.shape[1]), w_t.dtype)
    pad = jnp.zeros((GAB_W - 2 * HEADS, w_t.shape[1]), w_t.dtype)
    return jnp.concatenate([w_t[:2 * LORA + ROPE_HALF], z, w_t[2 * LORA + ROPE_HALF:_G0], z, w_t[_G0:], pad], axis=0)


def _narrow_w_in_t(w_t):
    return jnp.concatenate([w_t[:2 * LORA + ROPE_HALF], w_t[2 * LORA + 2 * ROPE_HALF:2 * LORA + 3 * ROPE_HALF],
                            w_t[LAT_W:LAT_W + W_IN_COLS - _G0]], axis=0)


def _stack_mla(w_uq, w_ukv):
    uq = w_uq.reshape(LORA, HEADS, QK_DIM)
    ukv = w_ukv.reshape(LORA, HEADS, 2 * HEAD_DIM)
    parts = [uq[:, :, :HEAD_DIM], _rope_pad(uq[:, :, HEAD_DIM:]), ukv[:, :, :HEAD_DIM], ukv[:, :, HEAD_DIM:]]
    return jnp.concatenate([p.transpose(1, 0, 2) for p in parts], axis=0)


def _unstack_mla(w):
    p = [w[i * HEADS:(i + 1) * HEADS].transpose(1, 0, 2) for i in range(4)]
    uq = jnp.concatenate([p[0], _rope_unpad(p[1])], axis=-1).reshape(LORA, HEADS * QK_DIM)
    ukv = jnp.concatenate([p[2], p[3]], axis=-1).reshape(LORA, HEADS * 2 * HEAD_DIM)
    return uq, ukv


def _rows8(rows):
    a = jnp.concatenate(rows, axis=0)
    return jnp.pad(a, ((0, 8 - a.shape[0]), (0, 0)))


def _qk_norm_rows(q_norm_w, k_norm_w):
    return _rows8([q_norm_w[:, :HEAD_DIM], _rope_pad(q_norm_w[:, HEAD_DIM:]), k_norm_w[:, :HEAD_DIM], _rope_pad(k_norm_w[:, HEAD_DIM:])])


def _rope_rows():
    inv_freq = ROPE_THETA ** (-jnp.arange(ROPE_HALF, dtype=F32) / ROPE_HALF)
    z = jnp.zeros((ROPE_HALF,), F32)
    freq = jnp.concatenate([inv_freq, z, inv_freq, z])
    sign = jnp.concatenate([-jnp.ones((ROPE_HALF,), F32), z, jnp.ones((ROPE_HALF,), F32), z])
    return _rows8([freq[None], sign[None]])


def _column_shards(a):
    return a.reshape(a.shape[0], 4, a.shape[1] // 4).transpose(1, 0, 2)


def _from_column_shards(a):
    return a.transpose(1, 0, 2).reshape(a.shape[1], 4 * a.shape[2])


_ANY = pl.BlockSpec(memory_space=pl.ANY)
_OTHER_CHIPS = ((1, 0), (0, 1), (1, 1))


def _here():
    return lax.axis_index("x"), lax.axis_index("y"), lax.axis_index("c")


def _flip(v, bit):
    return 1 - v if bit else v


def _remote(src, dst, send_sems, recv_sems, k, to):
    return pltpu.make_async_remote_copy(src_ref=src, dst_ref=dst, send_sem=send_sems.at[k], recv_sem=recv_sems.at[k],
                                        device_id=to, device_id_type=MESH)


def _half_of(ref, k, shape):
    r, c = shape
    if (r // 2) % 16 == 0:
        return ref.at[pl.ds(pl.multiple_of(k * (r // 2), 16), r // 2)]
    if (c // 2) % 128 == 0:
        return ref.at[:, pl.ds(pl.multiple_of(k * (c // 2), 128), c // 2)]
    return None


def _gather_copies(srcs, dsts, send_sems, recv_sems, local_sems):
    x, y, c = _here()
    slot, sibling, n = 2 * x + y, (x, y, 1 - c), len(srcs)
    starts, passes, waits = [], [], []
    for i, (src, dst) in enumerate(zip(srcs, dsts)):
        own = pltpu.make_async_copy(src, dst.at[slot], local_sems.at[i])
        starts.append(own.start)
        waits.append(own.wait)
        halves = _half_of(src, c, src.shape) is not None
        for j, (fx, fy) in enumerate(_OTHER_CHIPS):
            cx, cy = _flip(x, fx), _flip(y, fy)
            there = dst.at[2 * cx + cy]
            if halves:
                push = _remote(_half_of(src, c, src.shape), _half_of(dst.at[slot], c, src.shape), send_sems, recv_sems, 3 * i + j, (cx, cy, c))
                landed, other = _half_of(there, c, src.shape), _half_of(there, 1 - c, src.shape)
                passes.append(_remote(landed, landed, send_sems, recv_sems, 3 * i + j, (cx, cy, c)).wait_recv)
                rows, lanes = landed.shape
                by_rows = (rows // SWAP_CHUNKS) % 16 == 0
                chunks = SWAP_CHUNKS if by_rows or (lanes // SWAP_CHUNKS) % 128 == 0 else 1
                step = (rows if by_rows else lanes) // chunks
                for q in range(chunks):
                    cut = (lambda r: r.at[q * step:(q + 1) * step]) if by_rows else (lambda r: r.at[:, q * step:(q + 1) * step])
                    k = 3 * n + (3 * i + j) * SWAP_CHUNKS + q
                    onward = _remote(cut(landed), cut(landed), send_sems, recv_sems, k, sibling)
                    passes.append(onward.start)
                    waits += [_remote(cut(other), cut(other), send_sems, recv_sems, k, sibling).wait_recv, onward.wait_send]
            else:
                push = _remote(src, dst.at[slot], send_sems, recv_sems, 3 * i + j, (cx, cy, c))
                waits.append(_remote(there, there, send_sems, recv_sems, 3 * i + j, (cx, cy, c)).wait_recv)
            starts.append(push.start)
            waits.append(push.wait_send)
    return starts, passes, waits


def _gather_scratch(n):
    count = 3 * n * (1 + SWAP_CHUNKS)
    return [pltpu.SemaphoreType.DMA((count,)), pltpu.SemaphoreType.DMA((count,)), pltpu.SemaphoreType.DMA((n,))]


def _all_gather(shards, name):
    ns = len(shards)

    def body(*refs):
        starts, passes, waits = _gather_copies(refs[:ns], refs[ns:2 * ns], *refs[2 * ns:])
        for call in starts + passes + waits:
            call()

    return pl.pallas_call(
        body, in_specs=[_ANY] * ns, out_specs=[_ANY] * ns, out_shape=[_sds((4,) + s.shape, s.dtype) for s in shards],
        scratch_shapes=_gather_scratch(ns), name=name,
    )(*shards)


def _by_lanes(shape):
    return (shape[-2] // 2) % 16 != 0


def _scattered_shape(p):
    r, c = p.shape[1:]
    return _sds((8, r, c // 2) if _by_lanes(p.shape) else (8, r // 2, c), p.dtype)


def _scatter_copies(srcs, dsts, send_sems, recv_sems, local_sems, whole=0):
    x, y, c = _here()
    me = 4 * x + 2 * y + c
    starts, waits = [], []
    for i, (src, dst) in enumerate(zip(srcs, dsts)):
        def piece(px, py, pc, src=src, entire=i >= len(srcs) - whole):
            if entire:
                return src
            if _by_lanes(src.shape):
                half = src.shape[2] // 2
                return src.at[2 * px + py, :, pl.ds(pl.multiple_of(pc * half, 128), half)]
            half = src.shape[1] // 2
            return src.at[2 * px + py, pl.ds(pl.multiple_of(pc * half, 16), half)]

        own = pltpu.make_async_copy(piece(x, y, c), dst.at[me], local_sems.at[i])
        starts.append(own.start)
        waits.append(own.wait)
        for k in range(1, 8):
            px, py, pc = _flip(x, k & 4), _flip(y, k & 2), _flip(c, k & 1)
            push = _remote(piece(px, py, pc), dst.at[me], send_sems, recv_sems, 7 * i + k - 1, (px, py, pc))
            landed = dst.at[4 * px + 2 * py + pc]
            starts.append(push.start)
            waits += [_remote(landed, landed, send_sems, recv_sems, 7 * i + k - 1, (px, py, pc)).wait_recv, push.wait_send]
    return starts, waits


def _scatter_scratch(n):
    return [pltpu.SemaphoreType.DMA((7 * n,)), pltpu.SemaphoreType.DMA((7 * n,)), pltpu.SemaphoreType.DMA((n,))]


def _swapped_shape(half):
    r, c = half.shape
    return _sds((r, 2 * c) if _by_lanes((r, 2 * c)) else (2, r, c), half.dtype)


def _swap_copies(srcs, dsts, send_sems, recv_sems, local_sems):
    x, y, c = _here()
    sibling = (x, y, 1 - c)
    starts, waits = [], []
    for i, (src, dst) in enumerate(zip(srcs, dsts)):
        rows, lanes = src.shape
        if len(dst.shape) == 2:
            mine, other = (dst.at[:, pl.ds(pl.multiple_of(k * lanes, 128), lanes)] for k in (c, 1 - c))
        else:
            mine, other = dst.at[c], dst.at[1 - c]
        own = pltpu.make_async_copy(src, mine, local_sems.at[i])
        starts.append(own.start)
        waits.append(own.wait)
        if len(dst.shape) == 2 and (lanes // SWAP_CHUNKS) % 128 == 0:
            step = lanes // SWAP_CHUNKS
            part = lambda ref, q: ref.at[:, q * step:(q + 1) * step]
        elif len(dst.shape) == 3 and (rows // SWAP_CHUNKS) % 8 == 0:
            step = rows // SWAP_CHUNKS
            part = lambda ref, q: ref.at[q * step:(q + 1) * step]
        else:
            part = None
        for q in range(SWAP_CHUNKS if part else 1):
            pieces = [part(r, q) if part else r for r in (src, mine, other)]
            push = _remote(pieces[0], pieces[1], send_sems, recv_sems, i * SWAP_CHUNKS + q, sibling)
            starts.append(push.start)
            waits += [_remote(pieces[2], pieces[2], send_sems, recv_sems, i * SWAP_CHUNKS + q, sibling).wait_recv, push.wait_send]
    return starts, waits


def _swap_scratch(n):
    return [pltpu.SemaphoreType.DMA((n * SWAP_CHUNKS,)), pltpu.SemaphoreType.DMA((n * SWAP_CHUNKS,)), pltpu.SemaphoreType.DMA((n,))]


def _exchange_halves(halves, wholes):
    ns, nw = len(halves), len(wholes)

    def body(*refs):
        srcs, dsts = refs[:ns + nw], refs[ns + nw:2 * (ns + nw)]
        sems = refs[2 * (ns + nw):]
        starts, waits = _swap_copies(srcs[:ns], dsts[:ns], *sems[:3])
        more = _scatter_copies(srcs[ns:], dsts[ns:], *sems[3:], whole=nw)
        for call in starts + more[0] + waits + more[1]:
            call()

    return pl.pallas_call(
        body, in_specs=[_ANY] * (ns + nw), out_specs=[_ANY] * (ns + nw),
        out_shape=[_swapped_shape(h) for h in halves] + [_sds((8,) + a.shape, a.dtype) for a in wholes],
        scratch_shapes=_swap_scratch(ns) + _scatter_scratch(nw), name="exchange_halves",
    )(*halves, *wholes)


def _row_tile(rows, row_bytes, budget):
    tr = rows
    while tr * row_bytes > budget and tr % 16 == 0:
        tr //= 2
    return tr


def _sum_slots(parts, name):
    parts = list(parts) if isinstance(parts, (list, tuple)) else [parts]
    rows = parts[0].shape[1]
    widths = [p.shape[2] for p in parts]
    cols = sum(widths)
    tr = _row_tile(rows, 8 * cols * 4, 2 * 1024 * 1024)

    def body(*refs):
        o_ref, c0 = refs[-1], 0
        for p_ref, width in zip(refs[:-1], widths):
            acc = p_ref[0].astype(F32)
            for d in range(1, 8):
                acc = acc + p_ref[d].astype(F32)
            o_ref[:, c0:c0 + width] = acc
            c0 += width

    return pl.pallas_call(
        body, grid=(rows // tr,), in_specs=[pl.BlockSpec((8, tr, w), lambda i: (0, i, 0)) for w in widths],
        out_specs=pl.BlockSpec((tr, cols), lambda i: (i, 0)), out_shape=_sds((rows, cols), F32),
        compiler_params=_params(("parallel",)), name=name,
    )(*parts)


def _adam_update(w, g, m, v):
    m = ADAM_B1 * m + (1.0 - ADAM_B1) * g
    v = ADAM_B2 * v + (1.0 - ADAM_B2) * jnp.square(g)
    m_hat = m / (1.0 - ADAM_B1 ** ADAM_STEP)
    v_hat = v / (1.0 - ADAM_B2 ** ADAM_STEP)
    return -ADAM_LR * (m_hat / (jnp.sqrt(v_hat) + ADAM_EPS) + ADAM_WD * w), m, v


SMALL_ROWS = {"attn_norm_w": 0, "mlp_norm_w": 1, "q_lat_norm_w": 2, "kv_lat_norm_w": 3, "q_norm_w": 4, "k_norm_w": 5,
              "mla_out_norm_w": 6, "gdn_norm_w": 10, "a_log": 11, "dt_bias": 12}
LOSS_ROW = 13
SMALL_SHAPE = (16, 1024)


def _pack_small_partials(d_attn_nw, d_mlp_nw, d_ln, d_qk_nw, d_mix_nw, d_scal, conv_parts, sq):
    D = d_attn_nw.shape[1]

    def body(an_ref, mn_ref, ln_ref, qk_ref, mix_ref, sc_ref, cq_ref, ck_ref, cv_ref, sq_ref, a_ref, c_ref):
        a_ref[...] = jnp.zeros_like(a_ref)
        a_ref[0:1, :D] = an_ref[...]
        a_ref[1:2, :D] = mn_ref[...]
        a_ref[2:4, :LORA] = ln_ref[...]
        for row, base in ((4, 0), (5, 2)):
            rope = qk_ref[base + 1:base + 2, :]
            a_ref[row:row + 1, :QK_DIM] = jnp.concatenate(
                [qk_ref[base:base + 1, :], rope[:, :ROPE_HALF], rope[:, 2 * ROPE_HALF:3 * ROPE_HALF]], axis=1)
        a_ref[6:6 + HEADS, :HEAD_DIM] = mix_ref[0:HEADS, :]
        a_ref[10:11, :HEAD_DIM] = mix_ref[HEADS:HEADS + 1, :]
        a_ref[11:13, :128] = sc_ref[0:2, :]
        a_ref[LOSS_ROW:LOSS_ROW + 1, :128] = jnp.zeros((1, 128), F32) + jnp.sum(sq_ref[:, 0:1, 0:1]) * (0.5 / D)
        c_ref[...] = jnp.concatenate([cq_ref[...], ck_ref[...], cv_ref[...]], axis=1)

    return pl.pallas_call(
        body, out_shape=[_sds(SMALL_SHAPE, F32), _sds((CONV_TAPS, GQKV_W), F32)], name="pack_small_partials",
    )(d_attn_nw, d_mlp_nw, d_ln, d_qk_nw, d_mix_nw, d_scal, *conv_parts, sq)


def _adamw_small(parts, conv_parts, w, m, v):
    names = tuple(SMALL_ROWS) + ("conv_w",)
    cols = w["conv_w"].shape[2]

    def body(*refs):
        p_ref, c_ref = refs[:2]
        n = len(names)
        w_refs, m_refs, v_refs = (dict(zip(names, refs[2 + k * n:2 + (k + 1) * n])) for k in range(3))
        loss_ref = refs[2 + 3 * n]
        out = [dict(zip(names, refs[3 + (3 + k) * n:3 + (4 + k) * n])) for k in range(4)]
        acc_ref, cacc_ref = refs[3 + 7 * n:]
        acc, cacc = p_ref[0], c_ref[0]
        for d in range(1, 8):
            acc, cacc = acc + p_ref[d], cacc + c_ref[d]
        acc_ref[...] = acc
        cacc_ref[...] = cacc
        loss_ref[...] = acc_ref[LOSS_ROW:LOSS_ROW + 1, 0:1]
        chip = 2 * lax.axis_index("x") + lax.axis_index("y")
        for name in names:
            shape = w_refs[name].shape
            if name == "conv_w":
                g = sum(jnp.where(chip == s, cacc_ref[:, s * cols:(s + 1) * cols], 0.0) for s in range(4))[None]
            else:
                row = SMALL_ROWS[name]
                g = acc_ref[row:row + math.prod(shape[:-1]), 0:shape[-1]].reshape(shape)
            delta, new_m, new_v = _adam_update(w_refs[name][...], g, m_refs[name][...], v_refs[name][...])
            for ref, val in zip((o[name] for o in out), (g, delta, new_m, new_v)):
                ref[...] = val

    ins = [x[n] for x in (w, m, v) for n in names]
    shapes = [_sds(w[n].shape, F32) for n in names]
    outs = pl.pallas_call(
        body, out_shape=[_sds((1, 1), F32)] + shapes * 4,
        scratch_shapes=[pltpu.VMEM(parts.shape[1:], F32), pltpu.VMEM(conv_parts.shape[1:], F32)], name="adamw_small",
    )(parts, conv_parts, *ins)
    n = len(names)
    return (outs[0],) + tuple(dict(zip(names, outs[1 + k * n:1 + (k + 1) * n])) for k in range(4))


def _adamw(w, g, m, v, name):
    rows, cols = w.shape[0], w.shape[-1]
    if w.ndim == 3:
        tr = max(d for d in range(1, rows + 1) if rows % d == 0 and d * 8 * cols * 4 * 14 <= VMEM_LIMIT // 2)
    else:
        tr = _row_tile(rows, 7 * cols * 4, 4 * 1024 * 1024)

    def body(w_ref, g_ref, m_ref, v_ref, d_ref, mo_ref, vo_ref):
        d_ref[...], mo_ref[...], vo_ref[...] = _adam_update(w_ref[...], g_ref[...], m_ref[...], v_ref[...])

    block = (tr,) + w.shape[1:]
    spec = pl.BlockSpec(block, lambda i: (i,) + (0,) * (len(block) - 1))
    return pl.pallas_call(
        body, grid=(rows // tr,), in_specs=[spec] * 4, out_specs=[spec] * 3, out_shape=[_sds(w.shape, F32)] * 3,
        compiler_params=_params(("parallel",)), name=name,
    )(w, g, m, v)


def kernel(x, positions, attn_norm_w, w_in, q_lat_norm_w, w_uq, kv_lat_norm_w, w_ukv, q_norm_w, k_norm_w, mla_out_norm_w, conv_w, a_log, dt_bias, gdn_norm_w, w_out, mlp_norm_w, w_up, w_down, loss_target, m_attn_norm_w, m_w_in, m_q_lat_norm_w, m_w_uq, m_kv_lat_norm_w, m_w_ukv, m_q_norm_w, m_k_norm_w, m_mla_out_norm_w, m_conv_w, m_a_log, m_dt_bias, m_gdn_norm_w, m_w_out, m_mlp_norm_w, m_w_up, m_w_down, v_attn_norm_w, v_w_in, v_q_lat_norm_w, v_w_uq, v_kv_lat_norm_w, v_w_ukv, v_q_norm_w, v_k_norm_w, v_mla_out_norm_w, v_conv_w, v_a_log, v_dt_bias, v_gdn_norm_w, v_w_out, v_mlp_norm_w, v_w_up, v_w_down):
    w = dict(zip(WEIGHTS, (attn_norm_w, w_in, q_lat_norm_w, w_uq, kv_lat_norm_w, w_ukv, q_norm_w, k_norm_w, mla_out_norm_w, conv_w,
                           a_log, dt_bias, gdn_norm_w, w_out, mlp_norm_w, w_up, w_down)))
    m = dict(zip(WEIGHTS, (m_attn_norm_w, m_w_in, m_q_lat_norm_w, m_w_uq, m_kv_lat_norm_w, m_w_ukv, m_q_norm_w, m_k_norm_w,
                           m_mla_out_norm_w, m_conv_w, m_a_log, m_dt_bias, m_gdn_norm_w, m_w_out, m_mlp_norm_w, m_w_up, m_w_down)))
    v = dict(zip(WEIGHTS, (v_attn_norm_w, v_w_in, v_q_lat_norm_w, v_w_uq, v_kv_lat_norm_w, v_w_ukv, v_q_norm_w, v_k_norm_w,
                           v_mla_out_norm_w, v_conv_w, v_a_log, v_dt_bias, v_gdn_norm_w, v_w_out, v_mlp_norm_w, v_w_up, v_w_down)))
    B, S, D = x.shape
    T = B * S
    x2, pos, target = x.reshape(T, D), positions.reshape(T, 1), loss_target.reshape(T, D)
    seq = lambda a: a.reshape(B, S, a.shape[-1])
    tok = lambda a: a.reshape(T, a.shape[-1])
    local = {n: w[n][0] for n in SHARDED}

    g_in, g_uq, g_ukv, g_conv = _all_gather([jnp.swapaxes(w_in, 1, 2)[0].astype(BF16), local["w_uq"].astype(BF16),
                                             local["w_ukv"].astype(BF16), local["conv_w"]], "gather_first_weights")
    w_in_p = _widen_w_in_t(g_in.reshape(-1, D))
    w_mla = _stack_mla(_from_column_shards(g_uq), _from_column_shards(g_ukv))
    conv_full = _from_column_shards(g_conv)
    ln_w = jnp.concatenate([q_lat_norm_w, kv_lat_norm_w], axis=0)
    qk_nw = _qk_norm_rows(q_norm_w, k_norm_w)
    rope_rows = _rope_rows()
    scal = _rows8([jnp.pad(a_log, ((0, 0), (0, 128 - HEADS))), jnp.pad(dt_bias, ((0, 0), (0, 128 - HEADS)))])
    mix_nw = _rows8([mla_out_norm_w[0], gdn_norm_w])

    xn, lat, gqkv, gz, gab = _in_proj_fwd(x2, attn_norm_w, w_in_p)
    q, k, v_att = _mla_pre_fwd(lat, pos, ln_w, w_mla, qk_nw, rope_rows)
    ao, lse, g_down = _attn_fwd(seq(q), seq(k), seq(v_att), [local["w_down"].astype(BF16)])
    gq, gk, gv = _gdn_pre_fwd(seq(gqkv), conv_full)
    go, states, g_out, w_up_b = _gdn_chunk_fwd(gq, gk, gv, seq(gab), scal, [local["w_out"].astype(BF16), local["w_up"].astype(BF16)])
    w_out_b = g_out.reshape(-1, D)
    w_down_b = g_down.reshape(-1, D)
    mix, h2 = _mix_fwd(tok(ao), tok(go), gz, mix_nw, w_out_b, x2)
    hn, act, dy, sq = _mlp_fwd(h2, mlp_norm_w, w_up_b, w_down_b, target)

    dh, d_up, d_mlp_nw = _mlp_bwd(h2, mlp_norm_w, act, w_up_b, w_down_b, dy)
    p_down = _wgrad(act, dy, "wgrad_down").reshape(4, -1, D)
    p_up = _wgrad(hn, d_up, "wgrad_up", column_shards=4)
    d_ao, d_go, d_gz, d_mix_nw = _mix_bwd(tok(ao), tok(go), gz, mix_nw, w_out_b, dh)
    p_out = _wgrad(mix, dh, "wgrad_out").reshape(4, -1, D)
    d_gq, d_gk, d_gv, d_gab, d_scal, s_up, s_out = _gdn_chunk_bwd(gq, gk, gv, seq(gab), scal, states, seq(d_go), [p_up, p_out])
    early = ("w_up", "w_out", "w_down")
    dxq, dxk, dxv, dcq, dck, dcv, g_up, g_out = _gdn_pre_bwd(seq(gqkv), conv_full, d_gq, d_gk, d_gv,
                                                             [_sum_slots(s_up, "sum_w_up"), _sum_slots(s_out, "sum_w_out")])
    dq, dk, dv, s_down = _attn_bwd(seq(q), seq(k), seq(v_att), ao, lse, seq(d_ao), [p_down])
    d_lat, d_ln, d_w_mla, d_qk_nw, g_down = _mla_pre_bwd(lat, pos, ln_w, w_mla, qk_nw, rope_rows, tok(dq), tok(dk), tok(dv),
                                                         [_sum_slots(s_down, "sum_w_down")])
    early_grads = [g_up, g_out, g_down]
    d_pieces = [d_lat, tok(dxq), tok(dxk), tok(dxv), d_gz, tok(d_gab)]
    p_in = _narrow_w_in_t(_wgrad_pieces(d_pieces, xn, "wgrad_in")).reshape(4, -1, D)
    p_uq, p_ukv = (_column_shards(a).astype(BF16) for a in _unstack_mla(d_w_mla))
    grad_x2, d_attn_nw, s_in, s_uq, s_ukv = _in_proj_bwd(d_pieces, w_in_p, x2, attn_norm_w, dh, [p_in, p_uq, p_ukv])
    small_buf, conv_buf = _pack_small_partials(d_attn_nw, d_mlp_nw, d_ln, d_qk_nw, d_mix_nw, d_scal, (dcq, dck, dcv), sq)

    late = ("w_in", "w_uq", "w_ukv")
    *late_grads, s_small, s_conv = _exchange_halves([_sum_slots(s, "sum_" + n) for n, s in zip(late, (s_in, s_uq, s_ukv))],
                                                    [small_buf, conv_buf])
    names = early + late
    grad = {n: g.reshape(-1, g.shape[-1]) for n, g in zip(names, list(early_grads) + list(late_grads))}

    loss, g_small, delta, new_m, new_v = _adamw_small(s_small, s_conv, w, m, v)
    grad.update(g_small)
    for n in names:
        if n == "w_in":
            stored = lambda a: jnp.transpose(a, (2, 0, 1))
            outs = _adamw(stored(w[n]), grad[n][:, None, :], stored(m[n]), stored(v[n]), "adamw_" + n)
            grad[n], delta[n], new_m[n], new_v[n] = (jnp.transpose(a, (1, 2, 0)) for a in (grad[n][:, None, :], *outs))
        else:
            delta[n], new_m[n], new_v[n] = _adamw(local[n], grad[n], m[n][0], v[n][0], "adamw_" + n)
    def in_order(d):
        return [d[n].reshape(w[n].shape) for n in WEIGHTS]

    return (loss.reshape(()), grad_x2.reshape(B, S, D), *in_order(grad), *in_order(delta), *in_order(new_m), *in_order(new_v))
```

```python
import functools
import math

import jax
import jax.numpy as jnp
from jax import lax
from jax.experimental import pallas as pl
from jax.experimental.pallas import tpu as pltpu

F32 = jnp.float32
BF16 = jnp.bfloat16
MESH = pl.DeviceIdType.MESH

EPS = 1e-6
HEADS = 4
HEAD_DIM = 128
ROPE_DIM = 64
ROPE_HALF = 32
QK_DIM = 192
QK_PAD = 256
LORA = 256
CHUNK = 64
CONV_TAPS = 4
ROPE_THETA = 10000.0
ATTN_SCALE = QK_DIM ** -0.5

LAT_W = 640
GQKV_W = 3 * HEADS * HEAD_DIM
GZ_W = HEADS * HEAD_DIM
GAB_W = 128
PROJ_SPLITS = ((0, LAT_W), (LAT_W, LAT_W + GQKV_W), (LAT_W + GQKV_W, LAT_W + GQKV_W + GZ_W),
               (LAT_W + GQKV_W + GZ_W, LAT_W + GQKV_W + GZ_W + GAB_W))
PROJ_W = PROJ_SPLITS[-1][1]

ADAM_LR = 0.001
ADAM_B1 = 0.9
ADAM_B2 = 0.999
ADAM_EPS = 1e-08
ADAM_WD = 0.01
ADAM_STEP = 10

TOKEN_TILE = 512
MLP_TOKEN_TILE = 512
FF_TILE = 1024
ATTN_TILE = 512
ATTN_HEADS_PER_STEP = 2
WGRAD_OUT_BYTES = 8 * 1024 * 1024
VMEM_LIMIT = 48 * 1024 * 1024

SHARDED = ("w_in", "w_uq", "w_ukv", "conv_w", "w_out", "w_up", "w_down")
WEIGHTS = ("attn_norm_w", "w_in", "q_lat_norm_w", "w_uq", "kv_lat_norm_w", "w_ukv", "q_norm_w", "k_norm_w", "mla_out_norm_w",
           "conv_w", "a_log", "dt_bias", "gdn_norm_w", "w_out", "mlp_norm_w", "w_up", "w_down")


def _sds(shape, dtype):
    return jax.ShapeDtypeStruct(shape, dtype)


def _params(semantics):
    return pltpu.CompilerParams(dimension_semantics=semantics, vmem_limit_bytes=VMEM_LIMIT)


def _block(n):
    for b in (512, 256, 128):
        if n % b == 0:
            return b
    return n


def _dg(a, b, ca, cb, prec):
    lead = a.ndim - 2
    batch = (tuple(range(lead)),) * 2
    return lax.dot_general(a, b, (((ca + lead,), (cb + lead,)), batch), precision=prec, preferred_element_type=F32)


def _split_bf16(a):
    hi = a.astype(BF16)
    return hi, (a - hi.astype(F32)).astype(BF16)


def _dot_bf16(a, b, ca, cb):
    return _dg(a.astype(BF16), b.astype(BF16), ca, cb, None)


def _dot_bf16x3(a, b, ca, cb):
    a_hi, a_lo = _split_bf16(a)
    b_hi, b_lo = _split_bf16(b)
    lead = a.ndim - 2
    return _dg(jnp.concatenate([a_hi, a_hi, a_lo], axis=ca + lead), jnp.concatenate([b_hi, b_lo, b_hi], axis=cb + lead), ca, cb, None)


def _matmul_family(dot):
    def nn_raw(a, b):
        return dot(a, b, 1, 0)

    def nt_raw(a, b):
        return dot(a, b, 1, 1)

    def tn_raw(a, b):
        return dot(a, b, 0, 0)

    @jax.custom_vjp
    def nn(a, b):
        return nn_raw(a, b)

    nn.defvjp(lambda a, b: (nn_raw(a, b), (a, b)), lambda r, g: (nt_raw(g, r[1]), tn_raw(r[0], g)))

    @jax.custom_vjp
    def nt(a, b):
        return nt_raw(a, b)

    nt.defvjp(lambda a, b: (nt_raw(a, b), (a, b)), lambda r, g: (nn_raw(g, r[1]), tn_raw(g, r[0])))

    @jax.custom_vjp
    def tn(a, b):
        return tn_raw(a, b)

    tn.defvjp(lambda a, b: (tn_raw(a, b), (a, b)), lambda r, g: (nt_raw(r[1], g), nn_raw(r[0], g)))
    return nn, nt, tn


_bf_nn, _bf_nt, _bf_tn = _matmul_family(_dot_bf16)
_hi_nn, _hi_nt, _hi_tn = _matmul_family(_dot_bf16x3)


def _lower_powers(lmat):
    powers = []
    while 2 ** (len(powers) + 1) < lmat.shape[-1]:
        powers.append(_dot_bf16x3(powers[-1] if powers else lmat, powers[-1] if powers else lmat, 1, 0))
    return powers


@jax.custom_vjp
def _unit_lower_solve(lmat, rhs):
    return _unit_lower_solve_fwd(lmat, rhs)[0]


def _unit_lower_solve_fwd(lmat, rhs):
    powers = _lower_powers(lmat)
    x = rhs - _dot_bf16x3(lmat, rhs, 1, 0)
    for p in powers:
        x = x + _dot_bf16x3(p, x, 1, 0)
    return x, (lmat, powers, x)


def _unit_lower_solve_bwd(res, g):
    lmat, powers, x = res
    y = g - _dot_bf16x3(lmat, g, 0, 0)
    for p in powers:
        y = y + _dot_bf16x3(p, y, 0, 0)
    return -_dot_bf16x3(y, x, 1, 1), y


_unit_lower_solve.defvjp(_unit_lower_solve_fwd, _unit_lower_solve_bwd)


@jax.custom_vjp
def _lane_halves(x):
    n = x.shape[-1] // 2
    return x[..., :n], x[..., n:]


_lane_halves.defvjp(lambda x: (_lane_halves(x), None), lambda _, g: (jnp.concatenate(g, axis=-1),))


@jax.custom_vjp
def _row_halves(x):
    n = x.shape[-2] // 2
    return x[..., :n, :], x[..., n:, :]


_row_halves.defvjp(lambda x: (_row_halves(x), None), lambda _, g: (jnp.concatenate(g, axis=-2),))


@jax.custom_vjp
def _swap_halves(t):
    return pltpu.roll(t, 64, 1)


_swap_halves.defvjp(lambda t: (pltpu.roll(t, 64, 1), None), lambda _, g: (pltpu.roll(g, 64, 1),))


@functools.partial(jax.custom_vjp, nondiff_argnums=(2,))
def _shift_rows(x, keep, s):
    return pltpu.roll(x, s, 0) * keep


def _shift_rows_fwd(x, keep, s):
    return pltpu.roll(x, s, 0) * keep, keep


def _shift_rows_bwd(s, keep, g):
    return pltpu.roll(g * keep, keep.shape[0] - s, 0), jnp.zeros_like(keep)


_shift_rows.defvjp(_shift_rows_fwd, _shift_rows_bwd)


def _sigmoid(x):
    return 0.5 * jnp.tanh(0.5 * x) + 0.5


def _softplus(x):
    return jnp.maximum(x, 0.0) + jnp.log(1.0 + jnp.exp(jnp.minimum(x, -x)))


def _silu(x):
    return x * _sigmoid(x)


def _rms(x, w, n=None):
    n = x.shape[-1] if n is None else n
    r = lax.rsqrt(jnp.sum(x * x, axis=-1, keepdims=True) * (1.0 / n) + EPS)
    return x * r * w


def _rope(t, cos_f, sin_f):
    return t * cos_f + _swap_halves(t) * sin_f


def _rope_tables(pos_col, freq_row, sign_row):
    ang = pos_col.astype(F32) * freq_row
    return jnp.cos(ang), jnp.sin(ang) * sign_row


def _onehot_row(lane):
    return (lax.broadcasted_iota(jnp.int32, (1, 128), 1) == lane).astype(F32)


def _row_spec(tm, w):
    return pl.BlockSpec((tm, w), lambda i: (i, 0))


def _const_spec(shape):
    return pl.BlockSpec(shape, lambda *_: (0,) * len(shape))


def _in_proj_fwd(x2, w_an, w_in_p, shards):
    T, D = x2.shape
    tm = min(TOKEN_TILE, T)
    nt, ns = T // tm, len(shards)

    def body(*refs):
        x_ref, wn_ref, w_ref = refs[:3]
        src_refs = refs[3:3 + ns]
        xn_ref, lat_ref, gqkv_ref, gz_ref, gab_ref = refs[3 + ns:8 + ns]
        dst_refs = refs[8 + ns:8 + 2 * ns]
        sems = refs[8 + 2 * ns:]
        for phase, at in enumerate((0, nt // 2)):
            @pl.when(pl.program_id(0) == at)
            def _(phase=phase):
                for call in _gather_copies(src_refs, dst_refs, *sems)[phase]:
                    call()

        x = x_ref[...]
        r = lax.rsqrt(jnp.mean(x * x, axis=-1, keepdims=True) + EPS)
        xn = (x * r * wn_ref[...]).astype(BF16)
        xn_ref[...] = xn
        for ref, (a, b) in zip((lat_ref, gqkv_ref, gz_ref, gab_ref), PROJ_SPLITS):
            ref[...] = _dg(xn, w_ref[a:b, :], 1, 1, None)

        @pl.when(pl.program_id(0) == nt - 1)
        def _():
            for wait in _gather_copies(src_refs, dst_refs, *sems)[2]:
                wait()

    widths = [b - a for a, b in PROJ_SPLITS]
    return pl.pallas_call(
        body, grid=(nt,),
        in_specs=[_row_spec(tm, D), _const_spec((1, D)), _const_spec((PROJ_W, D))] + [_ANY] * ns,
        out_specs=[_row_spec(tm, D)] + [_row_spec(tm, w) for w in widths] + [_ANY] * ns,
        out_shape=[_sds((T, D), BF16)] + [_sds((T, w), F32) for w in widths] + [_sds((4,) + s.shape, s.dtype) for s in shards],
        scratch_shapes=_gather_scratch(ns),
        compiler_params=_params(("arbitrary",)), name="in_proj_fwd",
    )(x2, w_an, w_in_p, *shards)


def _in_proj_bwd(pieces, w_in_p, x2, w_an, dh, partials):
    T, D = x2.shape
    tm = min(TOKEN_TILE, T)
    widths = [p.shape[1] for p in pieces]
    starts = [sum(widths[:i]) for i in range(len(widths))]
    assert sum(widths) == PROJ_W
    npc, ns = len(pieces), len(partials)

    def body(*refs):
        piece_refs = refs[:npc]
        w_ref, x_ref, wn_ref, dh_ref = refs[npc:npc + 4]
        src_refs = refs[npc + 4:npc + 4 + ns]
        dx_ref, dwn_ref = refs[npc + 4 + ns:npc + 6 + ns]
        dst_refs = refs[npc + 6 + ns:npc + 6 + 2 * ns]
        sems = refs[npc + 6 + 2 * ns:]

        @pl.when(pl.program_id(0) == 0)
        def _():
            for start in _scatter_copies(src_refs, dst_refs, *sems)[0]:
                start()
            dwn_ref[...] = jnp.zeros_like(dwn_ref)

        dxn = jnp.zeros((tm, D), F32)
        for ref, a, width in zip(piece_refs, starts, widths):
            dxn += _dg(ref[...], w_ref[a:a + width, :], 1, 0, None)
        _, pull = jax.vjp(_rms, x_ref[...], wn_ref[...])
        dx, dwn = pull(dxn)
        dx_ref[...] = dx + dh_ref[...]
        dwn_ref[...] += dwn

        @pl.when(pl.program_id(0) == T // tm - 1)
        def _():
            for wait in _scatter_copies(src_refs, dst_refs, *sems)[1]:
                wait()

    return pl.pallas_call(
        body, grid=(T // tm,),
        in_specs=[_row_spec(tm, w) for w in widths] + [_const_spec((PROJ_W, D)), _row_spec(tm, D), _const_spec((1, D)),
                                                       _row_spec(tm, D)] + [_ANY] * ns,
        out_specs=[_row_spec(tm, D), _const_spec((1, D))] + [_ANY] * ns,
        out_shape=[_sds((T, D), F32), _sds((1, D), F32)] + [_scattered_shape(p) for p in partials],
        scratch_shapes=_scatter_scratch(ns),
        compiler_params=_params(("arbitrary",)), name="in_proj_bwd",
    )(*pieces, w_in_p, x2, w_an, dh, *partials)


def _wgrad_pieces(pieces, b, name):
    T, k2 = b.shape
    tt = min(TOKEN_TILE, T)
    widths = [p.shape[1] for p in pieces]
    starts = [sum(widths[:i]) for i in range(len(widths))]
    k1 = sum(widths)

    def body(*refs):
        piece_refs, (b_ref, o_ref, acc_ref) = refs[:len(pieces)], refs[len(pieces):]
        t = pl.program_id(0)

        @pl.when(t == 0)
        def _():
            acc_ref[...] = jnp.zeros_like(acc_ref)

        bt = b_ref[...].astype(BF16)
        for ref, r0, width in zip(piece_refs, starts, widths):
            acc_ref[r0:r0 + width, :] += jnp.dot(ref[...].T, bt, preferred_element_type=F32)

        @pl.when(t == T // tt - 1)
        def _():
            o_ref[...] = acc_ref[...].astype(o_ref.dtype)

    return pl.pallas_call(
        body, grid=(T // tt,),
        in_specs=[pl.BlockSpec((tt, w), lambda t: (t, 0)) for w in widths] + [pl.BlockSpec((tt, k2), lambda t: (t, 0))],
        out_specs=_const_spec((k1, k2)), out_shape=_sds((k1, k2), BF16), scratch_shapes=[pltpu.VMEM((k1, k2), F32)],
        compiler_params=_params(("arbitrary",)), name=name,
    )(*pieces, b)


def _wgrad(a, b, name, column_shards=1, out_dtype=BF16):
    T, k1 = a.shape
    k2 = b.shape[1]
    per_shard = k2 // column_shards
    tt = min(TOKEN_TILE, T)
    b1 = k1
    while b1 * k2 * 4 > WGRAD_OUT_BYTES and b1 % 256 == 0:
        b1 //= 2
    step = _block(per_shard)

    def body(a_ref, b_ref, o_ref, acc_ref):
        t = pl.program_id(1)

        @pl.when(t == 0)
        def _():
            acc_ref[...] = jnp.zeros_like(acc_ref)

        a_t = a_ref[...].astype(BF16).T
        for c0 in range(0, k2, step):
            part = jnp.dot(a_t, b_ref[:, c0:c0 + step].astype(BF16), preferred_element_type=F32)
            if column_shards == 1:
                acc_ref[:, c0:c0 + step] += part
            else:
                acc_ref[c0 // per_shard, :, c0 % per_shard:c0 % per_shard + step] += part

        @pl.when(t == T // tt - 1)
        def _():
            o_ref[...] = acc_ref[...].astype(o_ref.dtype)

    if column_shards == 1:
        block, out_spec, out_shape = (b1, k2), pl.BlockSpec((b1, k2), lambda i, t: (i, 0)), _sds((k1, k2), out_dtype)
    else:
        block = (column_shards, b1, per_shard)
        out_spec, out_shape = pl.BlockSpec(block, lambda i, t: (0, i, 0)), _sds((column_shards, k1, per_shard), out_dtype)
    return pl.pallas_call(
        body, grid=(k1 // b1, T // tt),
        in_specs=[pl.BlockSpec((tt, b1), lambda i, t: (t, i)), pl.BlockSpec((tt, k2), lambda i, t: (t, 0))],
        out_specs=out_spec, out_shape=out_shape, scratch_shapes=[pltpu.VMEM(block, F32)],
        compiler_params=_params(("parallel", "arbitrary")), name=name,
    )(a, b)


def _mla_pre_fn(q_lat, kv_lat, kpe, ln_q, ln_kv, w_list, qn_n, qn_p, kn_n, kn_p, cos_f, sin_f):
    qn = _rms(q_lat, ln_q)
    kvn = _rms(kv_lat, ln_kv)
    kp = _rope(_rms(kpe, kn_p, ROPE_DIM), cos_f, sin_f)
    outs = []
    for h in range(HEADS):
        outs.append(_rms(_bf_nn(qn, w_list[h]), qn_n))
        outs.append(_rope(_rms(_bf_nn(qn, w_list[HEADS + h]), qn_p, ROPE_DIM), cos_f, sin_f))
        outs.append(_rms(_bf_nn(kvn, w_list[2 * HEADS + h]), kn_n))
        outs.append(_bf_nn(kvn, w_list[3 * HEADS + h]))
    return tuple(outs) + (kp,)


def _mla_pre_operands(lat_ref, pos_ref, ln_ref, w_ref, nw_ref, rope_ref):
    cos_f, sin_f = _rope_tables(pos_ref[...], rope_ref[0:1, :], rope_ref[1:2, :])
    diff = (lat_ref[:, 0:LORA], lat_ref[:, LORA:2 * LORA], lat_ref[:, 2 * LORA:LAT_W], ln_ref[0:1, :], ln_ref[1:2, :],
            [w_ref[i].astype(F32) for i in range(4 * HEADS)], nw_ref[0:1, :], nw_ref[1:2, :], nw_ref[2:3, :], nw_ref[3:4, :])
    return diff, cos_f, sin_f


def _mla_pre_fwd(lat, pos, ln_w, w_mla, nw, rope_rows):
    T = lat.shape[0]
    tm = min(TOKEN_TILE, T)

    def body(lat_ref, pos_ref, ln_ref, w_ref, nw_ref, rope_ref, q_ref, k_ref, v_ref):
        diff, cos_f, sin_f = _mla_pre_operands(lat_ref, pos_ref, ln_ref, w_ref, nw_ref, rope_ref)
        outs = _mla_pre_fn(*diff, cos_f, sin_f)
        kp = outs[-1].astype(BF16)
        for h in range(HEADS):
            q_n, q_p, k_n, v = outs[4 * h:4 * h + 4]
            q_ref[:, h * QK_PAD:h * QK_PAD + HEAD_DIM] = q_n.astype(BF16)
            q_ref[:, h * QK_PAD + HEAD_DIM:(h + 1) * QK_PAD] = q_p.astype(BF16)
            k_ref[:, h * QK_PAD:h * QK_PAD + HEAD_DIM] = k_n.astype(BF16)
            k_ref[:, h * QK_PAD + HEAD_DIM:(h + 1) * QK_PAD] = kp
            v_ref[:, h * HEAD_DIM:(h + 1) * HEAD_DIM] = v.astype(BF16)

    return pl.pallas_call(
        body, grid=(T // tm,),
        in_specs=[_row_spec(tm, LAT_W), _row_spec(tm, 1), _const_spec((2, LORA)), _const_spec((4 * HEADS, LORA, 128)),
                  _const_spec((8, 128)), _const_spec((8, 128))],
        out_specs=[_row_spec(tm, HEADS * QK_PAD), _row_spec(tm, HEADS * QK_PAD), _row_spec(tm, HEADS * HEAD_DIM)],
        out_shape=[_sds((T, HEADS * QK_PAD), BF16), _sds((T, HEADS * QK_PAD), BF16), _sds((T, HEADS * HEAD_DIM), BF16)],
        compiler_params=_params(("parallel",)), name="mla_pre_fwd",
    )(lat, pos, ln_w, w_mla, nw, rope_rows)


def _mla_pre_bwd(lat, pos, ln_w, w_mla, nw, rope_rows, dq, dk, dv, halves):
    T = lat.shape[0]
    tm = min(TOKEN_TILE, T)
    ns = len(halves)

    def body(*refs):
        lat_ref, pos_ref, ln_ref, w_ref, nw_ref, rope_ref, dq_ref, dk_ref, dv_ref = refs[:9]
        src_refs = refs[9:9 + ns]
        dlat_ref, dln_ref, dw_ref, dnw_ref = refs[9 + ns:13 + ns]
        dst_refs = refs[13 + ns:13 + 2 * ns]
        sems = refs[13 + 2 * ns:]

        @pl.when(pl.program_id(0) == 0)
        def _():
            for start in _swap_copies(src_refs, dst_refs, *sems)[0]:
                start()
            dln_ref[...] = jnp.zeros_like(dln_ref)
            dw_ref[...] = jnp.zeros_like(dw_ref)
            dnw_ref[...] = jnp.zeros_like(dnw_ref)

        diff, cos_f, sin_f = _mla_pre_operands(lat_ref, pos_ref, ln_ref, w_ref, nw_ref, rope_ref)
        _, pull = jax.vjp(lambda *a: _mla_pre_fn(*a, cos_f, sin_f), *diff)
        cts = []
        d_kp = jnp.zeros((tm, 128), F32)
        for h in range(HEADS):
            cts.append(dq_ref[:, h * QK_PAD:h * QK_PAD + HEAD_DIM])
            cts.append(dq_ref[:, h * QK_PAD + HEAD_DIM:(h + 1) * QK_PAD])
            cts.append(dk_ref[:, h * QK_PAD:h * QK_PAD + HEAD_DIM])
            cts.append(dv_ref[:, h * HEAD_DIM:(h + 1) * HEAD_DIM])
            d_kp += dk_ref[:, h * QK_PAD + HEAD_DIM:(h + 1) * QK_PAD]
        d_ql, d_kvl, d_kpe, d_lnq, d_lnkv, d_w, d_qn_n, d_qn_p, d_kn_n, d_kn_p = pull(tuple(cts) + (d_kp,))
        dlat_ref[:, 0:LORA] = d_ql.astype(BF16)
        dlat_ref[:, LORA:2 * LORA] = d_kvl.astype(BF16)
        dlat_ref[:, 2 * LORA:LAT_W] = d_kpe.astype(BF16)
        dln_ref[0:1, :] += d_lnq
        dln_ref[1:2, :] += d_lnkv
        for i in range(4 * HEADS):
            dw_ref[i] += d_w[i]
        for i, d in enumerate((d_qn_n, d_qn_p, d_kn_n, d_kn_p)):
            dnw_ref[i:i + 1, :] += d

        @pl.when(pl.program_id(0) == T // tm - 1)
        def _():
            for wait in _swap_copies(src_refs, dst_refs, *sems)[1]:
                wait()

    return pl.pallas_call(
        body, grid=(T // tm,),
        in_specs=[_row_spec(tm, LAT_W), _row_spec(tm, 1), _const_spec((2, LORA)), _const_spec((4 * HEADS, LORA, 128)),
                  _const_spec((8, 128)), _const_spec((8, 128)),
                  _row_spec(tm, HEADS * QK_PAD), _row_spec(tm, HEADS * QK_PAD), _row_spec(tm, HEADS * HEAD_DIM)] + [_ANY] * ns,
        out_specs=[_row_spec(tm, LAT_W), _const_spec((2, LORA)), _const_spec((4 * HEADS, LORA, 128)), _const_spec((8, 128))]
                  + [_ANY] * ns,
        out_shape=[_sds((T, LAT_W), BF16), _sds((2, LORA), F32), _sds((4 * HEADS, LORA, 128), F32), _sds((8, 128), F32)]
                  + [_swapped_shape(h) for h in halves],
        scratch_shapes=_swap_scratch(ns),
        compiler_params=_params(("arbitrary",)), name="mla_pre_bwd",
    )(lat, pos, ln_w, w_mla, nw, rope_rows, dq, dk, dv, *halves)


def _causal_mask(i, j, tq, tk):
    row = i * tq + lax.broadcasted_iota(jnp.int32, (tq, tk), 0)
    col = j * tk + lax.broadcasted_iota(jnp.int32, (tq, tk), 1)
    return col <= row


def _attn_fwd(q, k, v, shards):
    B, S, _ = q.shape
    t = min(ATTN_TILE, S)
    nq = S // t
    ns = len(shards)

    hp = ATTN_HEADS_PER_STEP
    qk = lambda h: slice(h * QK_PAD, (h + 1) * QK_PAD)
    vd = lambda h: slice(h * HEAD_DIM, (h + 1) * HEAD_DIM)

    def body(*refs):
        q_ref, k_ref, v_ref = refs[:3]
        src_refs = refs[3:3 + ns]
        o_ref, lse_ref = refs[3 + ns:5 + ns]
        dst_refs = refs[5 + ns:5 + 2 * ns]
        sems = refs[5 + 2 * ns:]
        b, g, i = pl.program_id(0), pl.program_id(1), pl.program_id(2)
        qb = [q_ref[0, :, qk(h)] for h in range(hp)]

        step_no = (b * (HEADS // hp) + g) * nq + i
        for phase, at in enumerate((0, (3 * B * (HEADS // hp) * nq) // 4)):
            @pl.when(step_no == at)
            def _(phase=phase):
                for call in _gather_copies(src_refs, dst_refs, *sems)[phase]:
                    call()

        def step(j, carry, diagonal):
            rows = pl.ds(pl.multiple_of(j * t, t), t)
            s = [_dg(qb[h], k_ref[0, rows, qk(h)], 1, 1, None) * ATTN_SCALE for h in range(hp)]
            if diagonal:
                keep = _causal_mask(0, 0, t, t)
                s = [jnp.where(keep, x, -1e30) for x in s]
            m_new = [jnp.maximum(carry[h][0], jnp.max(s[h], axis=-1, keepdims=True)) for h in range(hp)]
            p = [jnp.exp(s[h] - m_new[h]) for h in range(hp)]
            alpha = [jnp.exp(carry[h][0] - m_new[h]) for h in range(hp)]
            l = [alpha[h] * carry[h][1] + jnp.sum(p[h], axis=-1, keepdims=True) for h in range(hp)]
            pv = [jnp.dot(p[h].astype(BF16), v_ref[0, rows, vd(h)], preferred_element_type=F32) for h in range(hp)]
            return tuple((m_new[h], l[h], alpha[h] * carry[h][2] + pv[h]) for h in range(hp))

        init = tuple((jnp.full((t, 1), -1e30, F32), jnp.zeros((t, 1), F32), jnp.zeros((t, HEAD_DIM), F32)) for _ in range(hp))
        below = lax.fori_loop(0, i, lambda j, carry: step(j, carry, False), init)
        for h, (m, l, acc) in enumerate(step(i, below, True)):
            o_ref[0, :, vd(h)] = acc / l
            lse_ref[0, h, 0] = (m + jnp.log(l)).T

        @pl.when((b == B - 1) & (g == HEADS // hp - 1) & (i == nq - 1))
        def _():
            for wait in _gather_copies(src_refs, dst_refs, *sems)[2]:
                wait()

    return pl.pallas_call(
        body, grid=(B, HEADS // hp, nq),
        in_specs=[pl.BlockSpec((1, t, hp * QK_PAD), lambda b, g, i: (b, i, g)),
                  pl.BlockSpec((1, S, hp * QK_PAD), lambda b, g, i: (b, 0, g)),
                  pl.BlockSpec((1, S, hp * HEAD_DIM), lambda b, g, i: (b, 0, g))] + [_ANY] * ns,
        out_specs=[pl.BlockSpec((1, t, hp * HEAD_DIM), lambda b, g, i: (b, i, g)),
                   pl.BlockSpec((1, hp, 1, 1, t), lambda b, g, i: (b, g, i, 0, 0))] + [_ANY] * ns,
        out_shape=[_sds((B, S, HEADS * HEAD_DIM), F32), _sds((B, HEADS, nq, 1, t), F32)] + [_sds((4,) + s.shape, s.dtype) for s in shards],
        scratch_shapes=_gather_scratch(ns),
        compiler_params=_params(("arbitrary", "arbitrary", "arbitrary")), name="attn_fwd",
    )(q, k, v, *shards)


def _attn_bwd(q, k, v, o, lse, do, partials):
    B, S, _ = q.shape
    t = min(ATTN_TILE, S)
    nq = S // t
    ns = len(partials)

    hp = ATTN_HEADS_PER_STEP
    qk = lambda h: slice(h * QK_PAD, (h + 1) * QK_PAD)
    vd = lambda h: slice(h * HEAD_DIM, (h + 1) * HEAD_DIM)
    heads = range(hp)

    def body(*refs):
        q_ref, k_ref, v_ref, o_ref, lse_ref, do_ref = refs[:6]
        src_refs = refs[6:6 + ns]
        dq_ref, dk_ref, dv_ref = refs[6 + ns:9 + ns]
        dst_refs = refs[9 + ns:9 + 2 * ns]
        dsum_ref, send_sems, recv_sems, local_sems = refs[9 + 2 * ns:]
        b, g, j = pl.program_id(0), pl.program_id(1), pl.program_id(2)

        @pl.when((b == 0) & (g == 0) & (j == 0))
        def _():
            for start in _scatter_copies(src_refs, dst_refs, send_sems, recv_sems, local_sems)[0]:
                start()

        @pl.when(j == 0)
        def _():
            dq_ref[...] = jnp.zeros_like(dq_ref)
            for h in heads:
                for blk in range(nq):
                    rows = slice(blk * t, (blk + 1) * t)
                    dsum_ref[h, blk] = jnp.sum(do_ref[0, rows, vd(h)] * o_ref[0, rows, vd(h)], axis=-1, keepdims=True).T

        kb = [k_ref[0, :, qk(h)] for h in heads]
        vb = [v_ref[0, :, vd(h)] for h in heads]

        def step(i, carry, diagonal):
            rows = pl.ds(pl.multiple_of(i * t, t), t)
            qb = [q_ref[0, rows, qk(h)] for h in heads]
            dob = [do_ref[0, rows, vd(h)].astype(BF16) for h in heads]
            s = [_dg(kb[h], qb[h], 1, 1, None) * ATTN_SCALE for h in heads]
            p = [jnp.exp(s[h] - lse_ref[0, h, i]) for h in heads]
            if diagonal:
                key = lax.broadcasted_iota(jnp.int32, (t, t), 0)
                query = lax.broadcasted_iota(jnp.int32, (t, t), 1)
                p = [jnp.where(key <= query, x, 0.0) for x in p]
            dp = [_dg(vb[h], dob[h], 1, 1, None) for h in heads]
            dv = [carry[h][1] + jnp.dot(p[h].astype(BF16), dob[h], preferred_element_type=F32) for h in heads]
            ds = [(p[h] * (dp[h] - dsum_ref[h, i]) * ATTN_SCALE).astype(BF16) for h in heads]
            for h in heads:
                dq_ref[0, rows, qk(h)] += _dg(ds[h], kb[h], 0, 0, None)
            return tuple((carry[h][0] + jnp.dot(ds[h], qb[h], preferred_element_type=F32), dv[h]) for h in heads)

        zeros = tuple((jnp.zeros((t, QK_PAD), F32), jnp.zeros((t, HEAD_DIM), F32)) for _ in heads)
        on_diagonal = step(j, zeros, True)
        done = lax.fori_loop(j + 1, nq, lambda i, carry: step(i, carry, False), on_diagonal)
        for h, (dk, dv) in enumerate(done):
            dk_ref[0, :, qk(h)] = dk
            dv_ref[0, :, vd(h)] = dv

        @pl.when((b == B - 1) & (g == HEADS // hp - 1) & (j == nq - 1))
        def _():
            for wait in _scatter_copies(src_refs, dst_refs, send_sems, recv_sems, local_sems)[1]:
                wait()

    return pl.pallas_call(
        body, grid=(B, HEADS // hp, nq),
        in_specs=[pl.BlockSpec((1, S, hp * QK_PAD), lambda b, g, j: (b, 0, g)),
                  pl.BlockSpec((1, t, hp * QK_PAD), lambda b, g, j: (b, j, g)),
                  pl.BlockSpec((1, t, hp * HEAD_DIM), lambda b, g, j: (b, j, g)),
                  pl.BlockSpec((1, S, hp * HEAD_DIM), lambda b, g, j: (b, 0, g)),
                  pl.BlockSpec((1, hp, nq, 1, t), lambda b, g, j: (b, g, 0, 0, 0)),
                  pl.BlockSpec((1, S, hp * HEAD_DIM), lambda b, g, j: (b, 0, g))] + [_ANY] * ns,
        out_specs=[pl.BlockSpec((1, S, hp * QK_PAD), lambda b, g, j: (b, 0, g)),
                   pl.BlockSpec((1, t, hp * QK_PAD), lambda b, g, j: (b, j, g)),
                   pl.BlockSpec((1, t, hp * HEAD_DIM), lambda b, g, j: (b, j, g))] + [_ANY] * ns,
        out_shape=[_sds((B, S, HEADS * QK_PAD), F32), _sds((B, S, HEADS * QK_PAD), F32), _sds((B, S, HEADS * HEAD_DIM), F32)]
                  + [_scattered_shape(p) for p in partials],
        scratch_shapes=[pltpu.VMEM((hp, nq, 1, t), F32)] + _scatter_scratch(ns),
        compiler_params=_params(("arbitrary", "arbitrary", "arbitrary")), name="attn_bwd",
    )(q, k, v, o, lse, do, *partials)


def _gdn_pre_fn(xq, xk, xv, wq, wk, wv, keeps):
    def conv_silu(x, w):
        acc = x * w[3]
        for s in (1, 2, 3):
            acc = acc + _shift_rows(x, keeps[s - 1], s) * w[3 - s]
        return _silu(acc)

    def l2(x):
        return x * lax.rsqrt(jnp.sum(x * x, axis=-1, keepdims=True) + EPS)

    return l2(conv_silu(xq, wq)) * (HEAD_DIM ** -0.5), l2(conv_silu(xk, wk)), conv_silu(xv, wv)


def _gdn_pre_specs(S):
    x_specs = [pl.BlockSpec((1, S, HEAD_DIM), lambda h, b, g=g: (b, 0, g * HEADS + h)) for g in range(3)]
    w_specs = [pl.BlockSpec((CONV_TAPS, HEAD_DIM), lambda h, b, g=g: (0, g * HEADS + h)) for g in range(3)]
    out_spec = pl.BlockSpec((1, S, HEAD_DIM), lambda h, b: (b, 0, h))
    return x_specs, w_specs, out_spec


def _row_keeps(S):
    t = lax.broadcasted_iota(jnp.int32, (S, HEAD_DIM), 0)
    return [(t >= s).astype(F32) for s in (1, 2, 3)]


def _gdn_pre_fwd(gqkv, conv_w):
    B, S, _ = gqkv.shape
    x_specs, w_specs, out_spec = _gdn_pre_specs(S)

    def body(xq_ref, xk_ref, xv_ref, wq_ref, wk_ref, wv_ref, q_ref, k_ref, v_ref):
        taps = [[w[i:i + 1, :] for i in range(CONV_TAPS)] for w in (wq_ref, wk_ref, wv_ref)]
        q, k, v = _gdn_pre_fn(xq_ref[0], xk_ref[0], xv_ref[0], *taps, _row_keeps(S))
        q_ref[0], k_ref[0], v_ref[0] = q, k, v

    return pl.pallas_call(
        body, grid=(HEADS, B), in_specs=x_specs + w_specs, out_specs=[out_spec] * 3,
        out_shape=[_sds((B, S, HEADS * HEAD_DIM), F32)] * 3,
        compiler_params=_params(("parallel", "parallel")), name="gdn_pre_fwd",
    )(gqkv, gqkv, gqkv, conv_w, conv_w, conv_w)


def _gdn_pre_bwd(gqkv, conv_w, dq, dk, dv, halves):
    B, S, _ = gqkv.shape
    x_specs, w_specs, out_spec = _gdn_pre_specs(S)
    dw_spec = pl.BlockSpec((CONV_TAPS, HEAD_DIM), lambda h, b: (0, h))
    ns = len(halves)

    def body(*refs):
        xq_ref, xk_ref, xv_ref, wq_ref, wk_ref, wv_ref, dq_ref, dk_ref, dv_ref = refs[:9]
        src_refs = refs[9:9 + ns]
        dxq_ref, dxk_ref, dxv_ref, dwq_ref, dwk_ref, dwv_ref = refs[9 + ns:15 + ns]
        dst_refs = refs[15 + ns:15 + 2 * ns]
        sems = refs[15 + 2 * ns:]
        first = (pl.program_id(0) == 0) & (pl.program_id(1) == 0)
        last = (pl.program_id(0) == HEADS - 1) & (pl.program_id(1) == B - 1)

        @pl.when(first)
        def _():
            for start in _swap_copies(src_refs, dst_refs, *sems)[0]:
                start()

        @pl.when(pl.program_id(1) == 0)
        def _():
            for r in (dwq_ref, dwk_ref, dwv_ref):
                r[...] = jnp.zeros_like(r)

        taps = [[w[i:i + 1, :] for i in range(CONV_TAPS)] for w in (wq_ref, wk_ref, wv_ref)]
        keeps = _row_keeps(S)
        _, pull = jax.vjp(lambda *a: _gdn_pre_fn(*a, keeps), xq_ref[0], xk_ref[0], xv_ref[0], *taps)
        dxq, dxk, dxv, dwq, dwk, dwv = pull((dq_ref[0], dk_ref[0], dv_ref[0]))
        dxq_ref[0], dxk_ref[0], dxv_ref[0] = dxq.astype(BF16), dxk.astype(BF16), dxv.astype(BF16)
        for ref, dw in ((dwq_ref, dwq), (dwk_ref, dwk), (dwv_ref, dwv)):
            for i in range(CONV_TAPS):
                ref[i:i + 1, :] += dw[i]

        @pl.when(last)
        def _():
            for wait in _swap_copies(src_refs, dst_refs, *sems)[1]:
                wait()

    hw = HEADS * HEAD_DIM
    return pl.pallas_call(
        body, grid=(HEADS, B), in_specs=x_specs + w_specs + [out_spec] * 3 + [_ANY] * ns,
        out_specs=[out_spec] * 3 + [dw_spec] * 3 + [_ANY] * ns,
        out_shape=[_sds((B, S, hw), BF16)] * 3 + [_sds((CONV_TAPS, hw), F32)] * 3 + [_swapped_shape(h) for h in halves],
        scratch_shapes=_swap_scratch(ns),
        compiler_params=_params(("arbitrary", "arbitrary")), name="gdn_pre_bwd",
    )(gqkv, gqkv, gqkv, conv_w, conv_w, conv_w, dq, dk, dv, *halves)


def _chunk_masks():
    i = lax.broadcasted_iota(jnp.int32, (CHUNK, CHUNK), 0)
    j = lax.broadcasted_iota(jnp.int32, (CHUNK, CHUNK), 1)
    lower, after = (j <= i).astype(F32), (j > i).astype(F32)
    return {"le": lower, "le_gt": jnp.concatenate([lower, after], axis=0), "strict": (j < i).astype(F32)}


def _gdn_chunk_fn(groups, masks):
    lane = lax.broadcasted_iota(jnp.int32, (groups, 1, 128), 2)
    head = lax.broadcasted_iota(jnp.int32, (groups, 1, 128), 0) % HEADS
    pick_a, pick_b = (lane == head).astype(F32), (lane == head + HEADS).astype(F32)
    lower, lower_after, strict = (jnp.broadcast_to(masks[n], (groups,) + masks[n].shape) for n in ("le", "le_gt", "strict"))
    ones_row = jnp.ones((1, 1, HEAD_DIM), F32)

    def f(q, k, v, gab, a_row, dt_row, state):
        ga = jnp.sum(gab * pick_a, axis=2, keepdims=True)
        gb = jnp.sum(gab * pick_b, axis=2, keepdims=True)
        a_log = jnp.sum(a_row * pick_a, axis=2, keepdims=True)
        dt_bias = jnp.sum(dt_row * pick_a, axis=2, keepdims=True)
        beta = _sigmoid(gb)
        g = -jnp.exp(a_log) * _softplus(ga + dt_bias)
        g_wide = g * ones_row
        cum, rest = _row_halves(_hi_nn(lower_after, g_wide))
        total = jnp.sum(g_wide, axis=1, keepdims=True)
        diff = _hi_nn(lower, g * strict)
        decay = lower * jnp.exp(diff)
        e_cum = jnp.exp(cum)
        kk, qk = _row_halves(_bf_nt(jnp.concatenate([k, q], axis=1), k))
        lmat = strict * (beta * kk * decay)
        u, w = _lane_halves(_unit_lower_solve(lmat, jnp.concatenate([v * beta, k * (beta * e_cum)], axis=2)))
        w_state, q_state = _row_halves(_bf_nn(jnp.concatenate([w, q * e_cum], axis=1), state))
        v_new = u - w_state
        o = q_state + _bf_nn(qk * decay, v_new)
        new_state = state * jnp.exp(total) + _bf_tn(k * jnp.exp(rest), v_new)
        return o, new_state

    return f


def _gdn_chunk_fwd(q, k, v, gab, scal, shards):
    B, S, W = q.shape
    N = S // CHUNK
    ns = len(shards)

    def body(*refs):
        q_ref, k_ref, v_ref, gab_ref, sc_ref = refs[:5]
        src_refs = refs[5:5 + ns]
        o_ref, st_ref = refs[5 + ns:7 + ns]
        dst_refs = refs[7 + ns:7 + 2 * ns]
        state_ref, send_sems, recv_sems, local_sems = refs[7 + 2 * ns:]
        n = pl.program_id(0)

        @pl.when(n == 0)
        def _():
            for start in _gather_copies(src_refs, dst_refs, send_sems, recv_sems, local_sems)[0]:
                start()
            state_ref[...] = jnp.zeros_like(state_ref)

        @pl.when(n == (2 * N) // 3)
        def _():
            for pass_on in _gather_copies(src_refs, dst_refs, send_sems, recv_sems, local_sems)[1]:
                pass_on()

        groups = [(b, h) for b in range(B) for h in range(HEADS)]
        gather = lambda ref: jnp.stack([ref[b, :, h * HEAD_DIM:(h + 1) * HEAD_DIM] for b, h in groups])
        state = state_ref[...]
        for i, (b, h) in enumerate(groups):
            st_ref[b, 0, h] = state[i]
        o, new_state = _gdn_chunk_fn(len(groups), _chunk_masks())(
            gather(q_ref), gather(k_ref), gather(v_ref), jnp.stack([gab_ref[b] for b, _ in groups]), sc_ref[0:1, :], sc_ref[1:2, :], state)
        for i, (b, h) in enumerate(groups):
            o_ref[b, :, h * HEAD_DIM:(h + 1) * HEAD_DIM] = o[i]
        state_ref[...] = new_state

        @pl.when(n == N - 1)
        def _():
            for wait in _gather_copies(src_refs, dst_refs, send_sems, recv_sems, local_sems)[2]:
                wait()

    seq = pl.BlockSpec((B, CHUNK, W), lambda n: (0, n, 0))
    return pl.pallas_call(
        body, grid=(N,),
        in_specs=[seq, seq, seq, pl.BlockSpec((B, CHUNK, GAB_W), lambda n: (0, n, 0)), _const_spec((8, 128))] + [_ANY] * ns,
        out_specs=[seq, pl.BlockSpec((B, 1, HEADS, HEAD_DIM, HEAD_DIM), lambda n: (0, n, 0, 0, 0))] + [_ANY] * ns,
        out_shape=[_sds((B, S, W), F32), _sds((B, N, HEADS, HEAD_DIM, HEAD_DIM), F32)] + [_sds((4,) + s.shape, s.dtype) for s in shards],
        scratch_shapes=[pltpu.VMEM((B * HEADS, HEAD_DIM, HEAD_DIM), F32)] + _gather_scratch(ns),
        compiler_params=_params(("arbitrary",)), name="gdn_chunk_fwd",
    )(q, k, v, gab, scal, *shards)


def _gdn_chunk_bwd(q, k, v, gab, scal, states, do, partials):
    B, S, W = q.shape
    N = S // CHUNK
    ns = len(partials)

    def body(*refs):
        q_ref, k_ref, v_ref, gab_ref, sc_ref, st_ref, do_ref = refs[:7]
        src_refs = refs[7:7 + ns]
        dq_ref, dk_ref, dv_ref, dgab_ref, dsc_ref = refs[7 + ns:12 + ns]
        dst_refs = refs[12 + ns:12 + 2 * ns]
        dstate_ref, send_sems, recv_sems, local_sems = refs[12 + 2 * ns:]
        n = pl.program_id(0)

        @pl.when(n == 0)
        def _():
            for start in _scatter_copies(src_refs, dst_refs, send_sems, recv_sems, local_sems)[0]:
                start()
            dstate_ref[...] = jnp.zeros_like(dstate_ref)
            dsc_ref[...] = jnp.zeros_like(dsc_ref)

        groups = [(b, h) for b in range(B) for h in range(HEADS)]
        gather = lambda ref: jnp.stack([ref[b, :, h * HEAD_DIM:(h + 1) * HEAD_DIM] for b, h in groups])
        _, pull = jax.vjp(_gdn_chunk_fn(len(groups), _chunk_masks()), gather(q_ref), gather(k_ref), gather(v_ref),
                          jnp.stack([gab_ref[b] for b, _ in groups]), sc_ref[0:1, :], sc_ref[1:2, :],
                          jnp.stack([st_ref[b, 0, h] for b, h in groups]))
        dq, dk, dv, dg, d_a, d_dt, dstate = pull((gather(do_ref), dstate_ref[...]))
        for i, (b, h) in enumerate(groups):
            lanes = slice(h * HEAD_DIM, (h + 1) * HEAD_DIM)
            dq_ref[b, :, lanes] = dq[i]
            dk_ref[b, :, lanes] = dk[i]
            dv_ref[b, :, lanes] = dv[i]
        for b in range(B):
            dgab_ref[b] = sum(dg[b * HEADS + h] for h in range(HEADS)).astype(BF16)
        dstate_ref[...] = dstate
        dsc_ref[0:1, :] += d_a
        dsc_ref[1:2, :] += d_dt

        @pl.when(n == N - 1)
        def _():
            for wait in _scatter_copies(src_refs, dst_refs, send_sems, recv_sems, local_sems)[1]:
                wait()

    seq = pl.BlockSpec((B, CHUNK, W), lambda n: (0, N - 1 - n, 0))
    gab_spec = pl.BlockSpec((B, CHUNK, GAB_W), lambda n: (0, N - 1 - n, 0))
    return pl.pallas_call(
        body, grid=(N,),
        in_specs=[seq, seq, seq, gab_spec, _const_spec((8, 128)),
                  pl.BlockSpec((B, 1, HEADS, HEAD_DIM, HEAD_DIM), lambda n: (0, N - 1 - n, 0, 0, 0)), seq] + [_ANY] * ns,
        out_specs=[seq, seq, seq, gab_spec, _const_spec((8, 128))] + [_ANY] * ns,
        out_shape=[_sds((B, S, W), F32)] * 3 + [_sds((B, S, GAB_W), BF16), _sds((8, 128), F32)] + [_scattered_shape(p) for p in partials],
        scratch_shapes=[pltpu.VMEM((B * HEADS, HEAD_DIM, HEAD_DIM), F32)] + _scatter_scratch(ns),
        compiler_params=_params(("arbitrary",)), name="gdn_chunk_bwd",
    )(q, k, v, gab, scal, states, do, *partials)


def _mix_fn(ao, go, gz, w_mla, w_gdn):
    return tuple(_rms(ao[h], w_mla[h]) for h in range(HEADS)) + tuple(_rms(go[h], w_gdn) * _silu(gz[h]) for h in range(HEADS))


def _mix_operands(ao_ref, go_ref, gz_ref, nw_ref):
    blocks = lambda ref: [ref[:, h * HEAD_DIM:(h + 1) * HEAD_DIM] for h in range(HEADS)]
    return blocks(ao_ref), blocks(go_ref), blocks(gz_ref), [nw_ref[h:h + 1, :] for h in range(HEADS)], nw_ref[HEADS:HEADS + 1, :]


def _mix_fwd(ao, go, gz, nw, w_out, x2):
    T, D = x2.shape
    tm = min(TOKEN_TILE, T)
    MW = 2 * HEADS * HEAD_DIM

    def body(ao_ref, go_ref, gz_ref, nw_ref, w_ref, x_ref, mix_ref, h_ref):
        outs = _mix_fn(*_mix_operands(ao_ref, go_ref, gz_ref, nw_ref))
        for i, piece in enumerate(outs):
            mix_ref[:, i * HEAD_DIM:(i + 1) * HEAD_DIM] = piece.astype(BF16)
        h_ref[...] = x_ref[...] + jnp.dot(mix_ref[...], w_ref[...], preferred_element_type=F32)

    half = HEADS * HEAD_DIM
    return pl.pallas_call(
        body, grid=(T // tm,),
        in_specs=[_row_spec(tm, half), _row_spec(tm, half), _row_spec(tm, half), _const_spec((8, 128)), _const_spec((MW, D)),
                  _row_spec(tm, D)],
        out_specs=[_row_spec(tm, MW), _row_spec(tm, D)],
        out_shape=[_sds((T, MW), BF16), _sds((T, D), F32)],
        compiler_params=_params(("parallel",)), name="mix_fwd",
    )(ao, go, gz, nw, w_out, x2)


def _mix_bwd(ao, go, gz, nw, w_out, dh):
    T, D = dh.shape
    tm = min(TOKEN_TILE, T)
    MW = 2 * HEADS * HEAD_DIM
    half = HEADS * HEAD_DIM

    def body(ao_ref, go_ref, gz_ref, nw_ref, w_ref, dh_ref, dao_ref, dgo_ref, dgz_ref, dnw_ref):
        @pl.when(pl.program_id(0) == 0)
        def _():
            dnw_ref[...] = jnp.zeros_like(dnw_ref)

        d_mix = _dg(dh_ref[...].astype(BF16), w_ref[...], 1, 1, None)
        cts = tuple(d_mix[:, i * HEAD_DIM:(i + 1) * HEAD_DIM] for i in range(2 * HEADS))
        _, pull = jax.vjp(_mix_fn, *_mix_operands(ao_ref, go_ref, gz_ref, nw_ref))
        d_ao, d_go, d_gz, d_wm, d_wg = pull(cts)
        for h in range(HEADS):
            lanes = slice(h * HEAD_DIM, (h + 1) * HEAD_DIM)
            dao_ref[:, lanes] = d_ao[h]
            dgo_ref[:, lanes] = d_go[h]
            dgz_ref[:, lanes] = d_gz[h].astype(BF16)
            dnw_ref[h:h + 1, :] += d_wm[h]
        dnw_ref[HEADS:HEADS + 1, :] += d_wg

    return pl.pallas_call(
        body, grid=(T // tm,),
        in_specs=[_row_spec(tm, half), _row_spec(tm, half), _row_spec(tm, half), _const_spec((8, 128)), _const_spec((MW, D)),
                  _row_spec(tm, D)],
        out_specs=[_row_spec(tm, half)] * 3 + [_const_spec((8, 128))],
        out_shape=[_sds((T, half), F32)] * 2 + [_sds((T, half), BF16), _sds((8, 128), F32)],
        compiler_params=_params(("arbitrary",)), name="mix_bwd",
    )(ao, go, gz, nw, w_out, dh)


def _up_spec(w_up, tf):
    per_shard = w_up.shape[2] // tf
    return pl.BlockSpec((None, w_up.shape[1], tf), lambda i, j: (j // per_shard, 0, j % per_shard))


def _mlp_fwd(h2, w_mn, w_up, w_down, target):
    T, D = h2.shape
    FF = w_down.shape[0]
    tm, tf = min(MLP_TOKEN_TILE, T), min(FF_TILE, w_up.shape[2])
    nf = FF // tf

    def body(h_ref, wn_ref, wu_ref, wd_ref, t_ref, hn_ref, act_ref, dy_ref, sq_ref, acc_ref):
        j = pl.program_id(1)

        @pl.when(j == 0)
        def _():
            hn_ref[...] = _rms(h_ref[...], wn_ref[...]).astype(BF16)
            acc_ref[...] = jnp.zeros_like(acc_ref)

        up = jnp.dot(hn_ref[...], wu_ref[...], preferred_element_type=F32)
        act = jnp.square(jnp.maximum(up, 0.0)).astype(BF16)
        act_ref[...] = act
        acc_ref[...] += jnp.dot(act, wd_ref[...], preferred_element_type=F32)

        @pl.when(j == nf - 1)
        def _():
            err = h_ref[...] + acc_ref[...] - t_ref[...]
            dy_ref[...] = err * (1.0 / D)
            sq_ref[...] = jnp.zeros_like(sq_ref) + jnp.sum(err * err)

    tok = lambda w: pl.BlockSpec((tm, w), lambda i, j: (i, 0))
    return pl.pallas_call(
        body, grid=(T // tm, nf),
        in_specs=[tok(D), _const_spec((1, D)), _up_spec(w_up, tf), pl.BlockSpec((tf, D), lambda i, j: (j, 0)), tok(D)],
        out_specs=[tok(D), pl.BlockSpec((tm, tf), lambda i, j: (i, j)), tok(D), pl.BlockSpec((1, 8, 128), lambda i, j: (i, 0, 0))],
        out_shape=[_sds((T, D), BF16), _sds((T, FF), BF16), _sds((T, D), F32), _sds((T // tm, 8, 128), F32)],
        scratch_shapes=[pltpu.VMEM((tm, D), F32)],
        compiler_params=_params(("parallel", "arbitrary")), name="mlp_fwd",
    )(h2, w_mn, w_up, w_down, target)


def _mlp_bwd(h2, w_mn, act, w_up, w_down, dy):
    T, D = h2.shape
    FF = w_down.shape[0]
    tm, tf = min(MLP_TOKEN_TILE, T), min(FF_TILE, w_up.shape[2])
    nf = FF // tf

    def body(h_ref, wn_ref, act_ref, wu_ref, wd_ref, dy_ref, dh_ref, dup_ref, dwn_ref, acc_ref, dyb_ref):
        i, j = pl.program_id(0), pl.program_id(1)

        @pl.when((i == 0) & (j == 0))
        def _():
            dwn_ref[...] = jnp.zeros_like(dwn_ref)

        @pl.when(j == 0)
        def _():
            acc_ref[...] = jnp.zeros_like(acc_ref)
            dyb_ref[...] = dy_ref[...].astype(BF16)

        r = jnp.sqrt(act_ref[...].astype(F32))
        d_act = _dg(dyb_ref[...], wd_ref[...], 1, 1, None)
        d_up = (d_act * (2.0 * r)).astype(BF16)
        dup_ref[...] = d_up
        acc_ref[...] += _dg(d_up, wu_ref[...], 1, 1, None)

        @pl.when(j == nf - 1)
        def _():
            _, pull = jax.vjp(_rms, h_ref[...], wn_ref[...])
            dh, dwn = pull(acc_ref[...])
            dh_ref[...] = dh + dy_ref[...]
            dwn_ref[...] += dwn

    tok = lambda w: pl.BlockSpec((tm, w), lambda i, j: (i, 0))
    ff = pl.BlockSpec((tm, tf), lambda i, j: (i, j))
    return pl.pallas_call(
        body, grid=(T // tm, nf),
        in_specs=[tok(D), _const_spec((1, D)), ff, _up_spec(w_up, tf), pl.BlockSpec((tf, D), lambda i, j: (j, 0)), tok(D)],
        out_specs=[tok(D), ff, _const_spec((1, D))],
        out_shape=[_sds((T, D), F32), _sds((T, FF), BF16), _sds((1, D), F32)],
        scratch_shapes=[pltpu.VMEM((tm, D), F32), pltpu.VMEM((tm, D), BF16)],
        compiler_params=_params(("arbitrary", "arbitrary")), name="mlp_bwd",
    )(h2, w_mn, act, w_up, w_down, dy)


def _rope_pad(a):
    z = jnp.zeros(a.shape[:-1] + (ROPE_HALF,), a.dtype)
    return jnp.concatenate([a[..., :ROPE_HALF], z, a[..., ROPE_HALF:], z], axis=-1)


def _rope_unpad(a):
    return jnp.concatenate([a[..., :ROPE_HALF], a[..., 2 * ROPE_HALF:3 * ROPE_HALF]], axis=-1)


_G0 = 2 * LORA + ROPE_DIM
W_IN_COLS = _G0 + GQKV_W + GZ_W + 2 * HEADS


def _widen_w_in_t(w_t):
    z = jnp.zeros((ROPE_HALF, w_t.shape[1]), w_t.dtype)
    pad = jnp.zeros((GAB_W - 2 * HEADS, w_t.shape[1]), w_t.dtype)
    return jnp.concatenate([w_t[:2 * LORA + ROPE_HALF], z, w_t[2 * LORA + ROPE_HALF:_G0], z, w_t[_G0:], pad], axis=0)


def _narrow_w_in_t(w_t):
    return jnp.concatenate([w_t[:2 * LORA + ROPE_HALF], w_t[2 * LORA + 2 * ROPE_HALF:2 * LORA + 3 * ROPE_HALF],
                            w_t[LAT_W:LAT_W + W_IN_COLS - _G0]], axis=0)


def _stack_mla(w_uq, w_ukv):
    uq = w_uq.reshape(LORA, HEADS, QK_DIM)
    ukv = w_ukv.reshape(LORA, HEADS, 2 * HEAD_DIM)
    parts = [uq[:, :, :HEAD_DIM], _rope_pad(uq[:, :, HEAD_DIM:]), ukv[:, :, :HEAD_DIM], ukv[:, :, HEAD_DIM:]]
    return jnp.concatenate([p.transpose(1, 0, 2) for p in parts], axis=0)


def _unstack_mla(w):
    p = [w[i * HEADS:(i + 1) * HEADS].transpose(1, 0, 2) for i in range(4)]
    uq = jnp.concatenate([p[0], _rope_unpad(p[1])], axis=-1).reshape(LORA, HEADS * QK_DIM)
    ukv = jnp.concatenate([p[2], p[3]], axis=-1).reshape(LORA, HEADS * 2 * HEAD_DIM)
    return uq, ukv


def _rows8(rows):
    a = jnp.concatenate(rows, axis=0)
    return jnp.pad(a, ((0, 8 - a.shape[0]), (0, 0)))


def _qk_norm_rows(q_norm_w, k_norm_w):
    return _rows8([q_norm_w[:, :HEAD_DIM], _rope_pad(q_norm_w[:, HEAD_DIM:]), k_norm_w[:, :HEAD_DIM], _rope_pad(k_norm_w[:, HEAD_DIM:])])


def _rope_rows():
    inv_freq = ROPE_THETA ** (-jnp.arange(ROPE_HALF, dtype=F32) / ROPE_HALF)
    z = jnp.zeros((ROPE_HALF,), F32)
    freq = jnp.concatenate([inv_freq, z, inv_freq, z])
    sign = jnp.concatenate([-jnp.ones((ROPE_HALF,), F32), z, jnp.ones((ROPE_HALF,), F32), z])
    return _rows8([freq[None], sign[None]])


def _column_shards(a):
    return a.reshape(a.shape[0], 4, a.shape[1] // 4).transpose(1, 0, 2)


def _from_column_shards(a):
    return a.transpose(1, 0, 2).reshape(a.shape[1], 4 * a.shape[2])


_ANY = pl.BlockSpec(memory_space=pl.ANY)
_OTHER_CHIPS = ((1, 0), (0, 1), (1, 1))


def _here():
    return lax.axis_index("x"), lax.axis_index("y"), lax.axis_index("c")


def _flip(v, bit):
    return 1 - v if bit else v


def _remote(src, dst, send_sems, recv_sems, k, to):
    return pltpu.make_async_remote_copy(src_ref=src, dst_ref=dst, send_sem=send_sems.at[k], recv_sem=recv_sems.at[k],
                                        device_id=to, device_id_type=MESH)


def _half_of(ref, k, shape):
    r, c = shape
    if (r // 2) % 16 == 0:
        return ref.at[pl.ds(pl.multiple_of(k * (r // 2), 16), r // 2)]
    if (c // 2) % 128 == 0:
        return ref.at[:, pl.ds(pl.multiple_of(k * (c // 2), 128), c // 2)]
    return None


def _gather_copies(srcs, dsts, send_sems, recv_sems, local_sems):
    x, y, c = _here()
    slot, sibling, n = 2 * x + y, (x, y, 1 - c), len(srcs)
    starts, passes, waits = [], [], []
    for i, (src, dst) in enumerate(zip(srcs, dsts)):
        own = pltpu.make_async_copy(src, dst.at[slot], local_sems.at[i])
        starts.append(own.start)
        waits.append(own.wait)
        halves = _half_of(src, c, src.shape) is not None
        for j, (fx, fy) in enumerate(_OTHER_CHIPS):
            cx, cy = _flip(x, fx), _flip(y, fy)
            there = dst.at[2 * cx + cy]
            if halves:
                push = _remote(_half_of(src, c, src.shape), _half_of(dst.at[slot], c, src.shape), send_sems, recv_sems, 3 * i + j, (cx, cy, c))
                landed, other = _half_of(there, c, src.shape), _half_of(there, 1 - c, src.shape)
                onward = _remote(landed, landed, send_sems, recv_sems, 3 * n + 3 * i + j, sibling)
                passes += [_remote(landed, landed, send_sems, recv_sems, 3 * i + j, (cx, cy, c)).wait_recv, onward.start]
                waits += [_remote(other, other, send_sems, recv_sems, 3 * n + 3 * i + j, sibling).wait_recv, onward.wait_send]
            else:
                push = _remote(src, dst.at[slot], send_sems, recv_sems, 3 * i + j, (cx, cy, c))
                waits.append(_remote(there, there, send_sems, recv_sems, 3 * i + j, (cx, cy, c)).wait_recv)
            starts.append(push.start)
            waits.append(push.wait_send)
    return starts, passes, waits


def _gather_scratch(n):
    return [pltpu.SemaphoreType.DMA((6 * n,)), pltpu.SemaphoreType.DMA((6 * n,)), pltpu.SemaphoreType.DMA((n,))]


def _all_gather(shards, name):
    ns = len(shards)

    def body(*refs):
        starts, passes, waits = _gather_copies(refs[:ns], refs[ns:2 * ns], *refs[2 * ns:])
        for call in starts + passes + waits:
            call()

    return pl.pallas_call(
        body, in_specs=[_ANY] * ns, out_specs=[_ANY] * ns, out_shape=[_sds((4,) + s.shape, s.dtype) for s in shards],
        scratch_shapes=_gather_scratch(ns), name=name,
    )(*shards)


def _by_lanes(shape):
    return (shape[-2] // 2) % 16 != 0


def _scattered_shape(p):
    r, c = p.shape[1:]
    return _sds((8, r, c // 2) if _by_lanes(p.shape) else (8, r // 2, c), p.dtype)


def _scatter_copies(srcs, dsts, send_sems, recv_sems, local_sems, whole=0):
    x, y, c = _here()
    me = 4 * x + 2 * y + c
    starts, waits = [], []
    for i, (src, dst) in enumerate(zip(srcs, dsts)):
        def piece(px, py, pc, src=src, entire=i >= len(srcs) - whole):
            if entire:
                return src
            if _by_lanes(src.shape):
                half = src.shape[2] // 2
                return src.at[2 * px + py, :, pl.ds(pl.multiple_of(pc * half, 128), half)]
            half = src.shape[1] // 2
            return src.at[2 * px + py, pl.ds(pl.multiple_of(pc * half, 16), half)]

        own = pltpu.make_async_copy(piece(x, y, c), dst.at[me], local_sems.at[i])
        starts.append(own.start)
        waits.append(own.wait)
        for k in range(1, 8):
            px, py, pc = _flip(x, k & 4), _flip(y, k & 2), _flip(c, k & 1)
            push = _remote(piece(px, py, pc), dst.at[me], send_sems, recv_sems, 7 * i + k - 1, (px, py, pc))
            landed = dst.at[4 * px + 2 * py + pc]
            starts.append(push.start)
            waits += [_remote(landed, landed, send_sems, recv_sems, 7 * i + k - 1, (px, py, pc)).wait_recv, push.wait_send]
    return starts, waits


def _scatter_scratch(n):
    return [pltpu.SemaphoreType.DMA((7 * n,)), pltpu.SemaphoreType.DMA((7 * n,)), pltpu.SemaphoreType.DMA((n,))]


def _swapped_shape(half):
    r, c = half.shape
    return _sds((r, 2 * c) if _by_lanes((r, 2 * c)) else (2, r, c), half.dtype)


def _swap_copies(srcs, dsts, send_sems, recv_sems, local_sems):
    x, y, c = _here()
    sibling = (x, y, 1 - c)
    starts, waits = [], []
    for i, (src, dst) in enumerate(zip(srcs, dsts)):
        if len(dst.shape) == 2:
            lanes = src.shape[1]
            mine, other = (dst.at[:, pl.ds(pl.multiple_of(k * lanes, 128), lanes)] for k in (c, 1 - c))
        else:
            mine, other = dst.at[c], dst.at[1 - c]
        own = pltpu.make_async_copy(src, mine, local_sems.at[i])
        push = _remote(src, mine, send_sems, recv_sems, i, sibling)
        starts += [own.start, push.start]
        waits += [_remote(other, other, send_sems, recv_sems, i, sibling).wait_recv, push.wait_send, own.wait]
    return starts, waits


def _swap_scratch(n):
    return [pltpu.SemaphoreType.DMA((n,)), pltpu.SemaphoreType.DMA((n,)), pltpu.SemaphoreType.DMA((n,))]


def _exchange_halves(halves, wholes):
    ns, nw = len(halves), len(wholes)

    def body(*refs):
        srcs, dsts = refs[:ns + nw], refs[ns + nw:2 * (ns + nw)]
        sems = refs[2 * (ns + nw):]
        starts, waits = _swap_copies(srcs[:ns], dsts[:ns], *sems[:3])
        more = _scatter_copies(srcs[ns:], dsts[ns:], *sems[3:], whole=nw)
        for call in starts + more[0] + waits + more[1]:
            call()

    return pl.pallas_call(
        body, in_specs=[_ANY] * (ns + nw), out_specs=[_ANY] * (ns + nw),
        out_shape=[_swapped_shape(h) for h in halves] + [_sds((8,) + a.shape, a.dtype) for a in wholes],
        scratch_shapes=_swap_scratch(ns) + _scatter_scratch(nw), name="exchange_halves",
    )(*halves, *wholes)


def _row_tile(rows, row_bytes, budget):
    tr = rows
    while tr * row_bytes > budget and tr % 16 == 0:
        tr //= 2
    return tr


def _sum_slots(parts, name):
    _, rows, cols = parts.shape
    tr = _row_tile(rows, 8 * cols * 4, 2 * 1024 * 1024)

    def body(p_ref, o_ref):
        acc = p_ref[0].astype(F32)
        for d in range(1, 8):
            acc = acc + p_ref[d].astype(F32)
        o_ref[...] = acc

    return pl.pallas_call(
        body, grid=(rows // tr,), in_specs=[pl.BlockSpec((8, tr, cols), lambda i: (0, i, 0))],
        out_specs=pl.BlockSpec((tr, cols), lambda i: (i, 0)), out_shape=_sds((rows, cols), F32),
        compiler_params=_params(("parallel",)), name=name,
    )(parts)


def _adam_update(w, g, m, v):
    m = ADAM_B1 * m + (1.0 - ADAM_B1) * g
    v = ADAM_B2 * v + (1.0 - ADAM_B2) * jnp.square(g)
    m_hat = m / (1.0 - ADAM_B1 ** ADAM_STEP)
    v_hat = v / (1.0 - ADAM_B2 ** ADAM_STEP)
    return -ADAM_LR * (m_hat / (jnp.sqrt(v_hat) + ADAM_EPS) + ADAM_WD * w), m, v


SMALL_ROWS = {"attn_norm_w": 0, "mlp_norm_w": 1, "q_lat_norm_w": 2, "kv_lat_norm_w": 3, "q_norm_w": 4, "k_norm_w": 5,
              "mla_out_norm_w": 6, "gdn_norm_w": 10, "a_log": 11, "dt_bias": 12}
LOSS_ROW = 13
SMALL_SHAPE = (16, 1024)


def _pack_small_partials(d_attn_nw, d_mlp_nw, d_ln, d_qk_nw, d_mix_nw, d_scal, conv_parts, sq):
    D = d_attn_nw.shape[1]

    def body(an_ref, mn_ref, ln_ref, qk_ref, mix_ref, sc_ref, cq_ref, ck_ref, cv_ref, sq_ref, a_ref, c_ref):
        a_ref[...] = jnp.zeros_like(a_ref)
        a_ref[0:1, :D] = an_ref[...]
        a_ref[1:2, :D] = mn_ref[...]
        a_ref[2:4, :LORA] = ln_ref[...]
        for row, base in ((4, 0), (5, 2)):
            rope = qk_ref[base + 1:base + 2, :]
            a_ref[row:row + 1, :QK_DIM] = jnp.concatenate(
                [qk_ref[base:base + 1, :], rope[:, :ROPE_HALF], rope[:, 2 * ROPE_HALF:3 * ROPE_HALF]], axis=1)
        a_ref[6:6 + HEADS, :HEAD_DIM] = mix_ref[0:HEADS, :]
        a_ref[10:11, :HEAD_DIM] = mix_ref[HEADS:HEADS + 1, :]
        a_ref[11:13, :128] = sc_ref[0:2, :]
        a_ref[LOSS_ROW:LOSS_ROW + 1, :128] = jnp.zeros((1, 128), F32) + jnp.sum(sq_ref[:, 0:1, 0:1]) * (0.5 / D)
        c_ref[...] = jnp.concatenate([cq_ref[...], ck_ref[...], cv_ref[...]], axis=1)

    return pl.pallas_call(
        body, out_shape=[_sds(SMALL_SHAPE, F32), _sds((CONV_TAPS, GQKV_W), F32)], name="pack_small_partials",
    )(d_attn_nw, d_mlp_nw, d_ln, d_qk_nw, d_mix_nw, d_scal, *conv_parts, sq)


def _adamw_small(parts, conv_parts, w, m, v):
    names = tuple(SMALL_ROWS) + ("conv_w",)
    cols = w["conv_w"].shape[2]

    def body(*refs):
        p_ref, c_ref = refs[:2]
        n = len(names)
        w_refs, m_refs, v_refs = (dict(zip(names, refs[2 + k * n:2 + (k + 1) * n])) for k in range(3))
        loss_ref = refs[2 + 3 * n]
        out = [dict(zip(names, refs[3 + (3 + k) * n:3 + (4 + k) * n])) for k in range(4)]
        acc_ref, cacc_ref = refs[3 + 7 * n:]
        acc, cacc = p_ref[0], c_ref[0]
        for d in range(1, 8):
            acc, cacc = acc + p_ref[d], cacc + c_ref[d]
        acc_ref[...] = acc
        cacc_ref[...] = cacc
        loss_ref[...] = acc_ref[LOSS_ROW:LOSS_ROW + 1, 0:1]
        chip = 2 * lax.axis_index("x") + lax.axis_index("y")
        for name in names:
            shape = w_refs[name].shape
            if name == "conv_w":
                g = sum(jnp.where(chip == s, cacc_ref[:, s * cols:(s + 1) * cols], 0.0) for s in range(4))[None]
            else:
                row = SMALL_ROWS[name]
                g = acc_ref[row:row + math.prod(shape[:-1]), 0:shape[-1]].reshape(shape)
            delta, new_m, new_v = _adam_update(w_refs[name][...], g, m_refs[name][...], v_refs[name][...])
            for ref, val in zip((o[name] for o in out), (g, delta, new_m, new_v)):
                ref[...] = val

    ins = [x[n] for x in (w, m, v) for n in names]
    shapes = [_sds(w[n].shape, F32) for n in names]
    outs = pl.pallas_call(
        body, out_shape=[_sds((1, 1), F32)] + shapes * 4,
        scratch_shapes=[pltpu.VMEM(parts.shape[1:], F32), pltpu.VMEM(conv_parts.shape[1:], F32)], name="adamw_small",
    )(parts, conv_parts, *ins)
    n = len(names)
    return (outs[0],) + tuple(dict(zip(names, outs[1 + k * n:1 + (k + 1) * n])) for k in range(4))


def _adamw(w, g, m, v, name):
    rows, cols = w.shape[0], w.shape[-1]
    if w.ndim == 3:
        tr = max(d for d in range(1, rows + 1) if rows % d == 0 and d * 8 * cols * 4 * 14 <= VMEM_LIMIT // 2)
    else:
        tr = _row_tile(rows, 7 * cols * 4, 4 * 1024 * 1024)

    def body(w_ref, g_ref, m_ref, v_ref, d_ref, mo_ref, vo_ref):
        d_ref[...], mo_ref[...], vo_ref[...] = _adam_update(w_ref[...], g_ref[...], m_ref[...], v_ref[...])

    block = (tr,) + w.shape[1:]
    spec = pl.BlockSpec(block, lambda i: (i,) + (0,) * (len(block) - 1))
    return pl.pallas_call(
        body, grid=(rows // tr,), in_specs=[spec] * 4, out_specs=[spec] * 3, out_shape=[_sds(w.shape, F32)] * 3,
        compiler_params=_params(("parallel",)), name=name,
    )(w, g, m, v)


def kernel(x, positions, attn_norm_w, w_in, q_lat_norm_w, w_uq, kv_lat_norm_w, w_ukv, q_norm_w, k_norm_w, mla_out_norm_w, conv_w, a_log, dt_bias, gdn_norm_w, w_out, mlp_norm_w, w_up, w_down, loss_target, m_attn_norm_w, m_w_in, m_q_lat_norm_w, m_w_uq, m_kv_lat_norm_w, m_w_ukv, m_q_norm_w, m_k_norm_w, m_mla_out_norm_w, m_conv_w, m_a_log, m_dt_bias, m_gdn_norm_w, m_w_out, m_mlp_norm_w, m_w_up, m_w_down, v_attn_norm_w, v_w_in, v_q_lat_norm_w, v_w_uq, v_kv_lat_norm_w, v_w_ukv, v_q_norm_w, v_k_norm_w, v_mla_out_norm_w, v_conv_w, v_a_log, v_dt_bias, v_gdn_norm_w, v_w_out, v_mlp_norm_w, v_w_up, v_w_down):
    w = dict(zip(WEIGHTS, (attn_norm_w, w_in, q_lat_norm_w, w_uq, kv_lat_norm_w, w_ukv, q_norm_w, k_norm_w, mla_out_norm_w, conv_w,
                           a_log, dt_bias, gdn_norm_w, w_out, mlp_norm_w, w_up, w_down)))
    m = dict(zip(WEIGHTS, (m_attn_norm_w, m_w_in, m_q_lat_norm_w, m_w_uq, m_kv_lat_norm_w, m_w_ukv, m_q_norm_w, m_k_norm_w,
                           m_mla_out_norm_w, m_conv_w, m_a_log, m_dt_bias, m_gdn_norm_w, m_w_out, m_mlp_norm_w, m_w_up, m_w_down)))
    v = dict(zip(WEIGHTS, (v_attn_norm_w, v_w_in, v_q_lat_norm_w, v_w_uq, v_kv_lat_norm_w, v_w_ukv, v_q_norm_w, v_k_norm_w,
                           v_mla_out_norm_w, v_conv_w, v_a_log, v_dt_bias, v_gdn_norm_w, v_w_out, v_mlp_norm_w, v_w_up, v_w_down)))
    B, S, D = x.shape
    T = B * S
    x2, pos, target = x.reshape(T, D), positions.reshape(T, 1), loss_target.reshape(T, D)
    seq = lambda a: a.reshape(B, S, a.shape[-1])
    tok = lambda a: a.reshape(T, a.shape[-1])
    local = {n: w[n][0] for n in SHARDED}

    (g_in,) = _all_gather([jnp.swapaxes(w_in, 1, 2)[0].astype(BF16)], "gather_first_weights")
    w_in_p = _widen_w_in_t(g_in.reshape(-1, D))
    ln_w = jnp.concatenate([q_lat_norm_w, kv_lat_norm_w], axis=0)
    qk_nw = _qk_norm_rows(q_norm_w, k_norm_w)
    rope_rows = _rope_rows()
    scal = _rows8([jnp.pad(a_log, ((0, 0), (0, 128 - HEADS))), jnp.pad(dt_bias, ((0, 0), (0, 128 - HEADS)))])
    mix_nw = _rows8([mla_out_norm_w[0], gdn_norm_w])

    xn, lat, gqkv, gz, gab, g_uq, g_ukv, g_conv, g_out = _in_proj_fwd(
        x2, attn_norm_w, w_in_p, [local["w_uq"].astype(BF16), local["w_ukv"].astype(BF16), local["conv_w"], local["w_out"].astype(BF16)])
    w_mla = _stack_mla(_from_column_shards(g_uq), _from_column_shards(g_ukv))
    conv_full = _from_column_shards(g_conv)
    q, k, v_att = _mla_pre_fwd(lat, pos, ln_w, w_mla, qk_nw, rope_rows)
    ao, lse, g_down = _attn_fwd(seq(q), seq(k), seq(v_att), [local["w_down"].astype(BF16)])
    gq, gk, gv = _gdn_pre_fwd(seq(gqkv), conv_full)
    go, states, w_up_b = _gdn_chunk_fwd(gq, gk, gv, seq(gab), scal, [local["w_up"].astype(BF16)])
    w_out_b = g_out.reshape(-1, D)
    w_down_b = g_down.reshape(-1, D)
    mix, h2 = _mix_fwd(tok(ao), tok(go), gz, mix_nw, w_out_b, x2)
    hn, act, dy, sq = _mlp_fwd(h2, mlp_norm_w, w_up_b, w_down_b, target)

    dh, d_up, d_mlp_nw = _mlp_bwd(h2, mlp_norm_w, act, w_up_b, w_down_b, dy)
    p_down = _wgrad(act, dy, "wgrad_down").reshape(4, -1, D)
    p_up = _wgrad(hn, d_up, "wgrad_up", column_shards=4)
    d_ao, d_go, d_gz, d_mix_nw = _mix_bwd(tok(ao), tok(go), gz, mix_nw, w_out_b, dh)
    p_out = _wgrad(mix, dh, "wgrad_out").reshape(4, -1, D)
    d_gq, d_gk, d_gv, d_gab, d_scal, s_up, s_out = _gdn_chunk_bwd(gq, gk, gv, seq(gab), scal, states, seq(d_go), [p_up, p_out])
    early = ("w_up", "w_out", "w_down")
    dxq, dxk, dxv, dcq, dck, dcv, g_up, g_out = _gdn_pre_bwd(seq(gqkv), conv_full, d_gq, d_gk, d_gv,
                                                             [_sum_slots(s_up, "sum_w_up"), _sum_slots(s_out, "sum_w_out")])
    dq, dk, dv, s_down = _attn_bwd(seq(q), seq(k), seq(v_att), ao, lse, seq(d_ao), [p_down])
    d_lat, d_ln, d_w_mla, d_qk_nw, g_down = _mla_pre_bwd(lat, pos, ln_w, w_mla, qk_nw, rope_rows, tok(dq), tok(dk), tok(dv),
                                                         [_sum_slots(s_down, "sum_w_down")])
    early_grads = [g_up, g_out, g_down]
    d_pieces = [d_lat, tok(dxq), tok(dxk), tok(dxv), d_gz, tok(d_gab)]
    p_in = _narrow_w_in_t(_wgrad_pieces(d_pieces, xn, "wgrad_in")).reshape(4, -1, D)
    p_uq, p_ukv = (_column_shards(a).astype(BF16) for a in _unstack_mla(d_w_mla))
    grad_x2, d_attn_nw, s_in, s_uq, s_ukv = _in_proj_bwd(d_pieces, w_in_p, x2, attn_norm_w, dh, [p_in, p_uq, p_ukv])
    small_buf, conv_buf = _pack_small_partials(d_attn_nw, d_mlp_nw, d_ln, d_qk_nw, d_mix_nw, d_scal, (dcq, dck, dcv), sq)

    late = ("w_in", "w_uq", "w_ukv")
    *late_grads, s_small, s_conv = _exchange_halves([_sum_slots(s, "sum_" + n) for n, s in zip(late, (s_in, s_uq, s_ukv))],
                                                    [small_buf, conv_buf])
    names = early + late
    grad = {n: g.reshape(-1, g.shape[-1]) for n, g in zip(names, list(early_grads) + list(late_grads))}

    loss, g_small, delta, new_m, new_v = _adamw_small(s_small, s_conv, w, m, v)
    grad.update(g_small)
    for n in names:
        if n == "w_in":
            stored = lambda a: jnp.transpose(a, (2, 0, 1))
            outs = _adamw(stored(w[n]), grad[n][:, None, :], stored(m[n]), stored(v[n]), "adamw_" + n)
            grad[n], delta[n], new_m[n], new_v[n] = (jnp.transpose(a, (1, 2, 0)) for a in (grad[n][:, None, :], *outs))
        else:
            delta[n], new_m[n], new_v[n] = _adamw(local[n], grad[n], m[n][0], v[n][0], "adamw_" + n)
    def in_order(d):
        return [d[n].reshape(w[n].shape) for n in WEIGHTS]

    return (loss.reshape(()), grad_x2.reshape(B, S, D), *in_order(grad), *in_order(delta), *in_order(new_m), *in_order(new_v))
```

```python
import functools
import math

import jax
import jax.numpy as jnp
from jax import lax
from jax.experimental import pallas as pl
from jax.experimental.pallas import tpu as pltpu

F32 = jnp.float32
BF16 = jnp.bfloat16
MESH = pl.DeviceIdType.MESH

EPS = 1e-6
HEADS = 4
HEAD_DIM = 128
ROPE_DIM = 64
ROPE_HALF = 32
QK_DIM = 192
QK_PAD = 256
LORA = 256
CHUNK = 64
SOLVE_POWERS = 5
CONV_TAPS = 4
ROPE_THETA = 10000.0
ATTN_SCALE = QK_DIM ** -0.5

LAT_W = 640
GQKV_W = 3 * HEADS * HEAD_DIM
GZ_W = HEADS * HEAD_DIM
GAB_W = 128
PROJ_SPLITS = ((0, LAT_W), (LAT_W, LAT_W + GQKV_W), (LAT_W + GQKV_W, LAT_W + GQKV_W + GZ_W),
               (LAT_W + GQKV_W + GZ_W, LAT_W + GQKV_W + GZ_W + GAB_W))
PROJ_W = PROJ_SPLITS[-1][1]

ADAM_LR = 0.001
ADAM_B1 = 0.9
ADAM_B2 = 0.999
ADAM_EPS = 1e-08
ADAM_WD = 0.01
ADAM_STEP = 10

TOKEN_TILE = 512
MLP_TOKEN_TILE = 512
FF_TILE = 1024
ATTN_TILE = 512
ATTN_HEADS_PER_STEP = 2
WGRAD_OUT_BYTES = 8 * 1024 * 1024
VMEM_LIMIT = 48 * 1024 * 1024

SHARDED = ("w_in", "w_uq", "w_ukv", "conv_w", "w_out", "w_up", "w_down")
WEIGHTS = ("attn_norm_w", "w_in", "q_lat_norm_w", "w_uq", "kv_lat_norm_w", "w_ukv", "q_norm_w", "k_norm_w", "mla_out_norm_w",
           "conv_w", "a_log", "dt_bias", "gdn_norm_w", "w_out", "mlp_norm_w", "w_up", "w_down")


def _sds(shape, dtype):
    return jax.ShapeDtypeStruct(shape, dtype)


def _params(semantics):
    return pltpu.CompilerParams(dimension_semantics=semantics, vmem_limit_bytes=VMEM_LIMIT)


def _block(n):
    for b in (512, 256, 128):
        if n % b == 0:
            return b
    return n


def _dg(a, b, ca, cb, prec):
    lead = a.ndim - 2
    batch = (tuple(range(lead)),) * 2
    return lax.dot_general(a, b, (((ca + lead,), (cb + lead,)), batch), precision=prec, preferred_element_type=F32)


def _split_bf16(a):
    hi = a.astype(BF16)
    return hi, (a - hi.astype(F32)).astype(BF16)


def _dot_bf16(a, b, ca, cb):
    return _dg(a.astype(BF16), b.astype(BF16), ca, cb, None)


def _dot_bf16x3(a, b, ca, cb):
    a_hi, a_lo = _split_bf16(a)
    b_hi, b_lo = _split_bf16(b)
    lead = a.ndim - 2
    return _dg(jnp.concatenate([a_hi, a_hi, a_lo], axis=ca + lead), jnp.concatenate([b_hi, b_lo, b_hi], axis=cb + lead), ca, cb, None)


def _matmul_family(dot):
    def nn_raw(a, b):
        return dot(a, b, 1, 0)

    def nt_raw(a, b):
        return dot(a, b, 1, 1)

    def tn_raw(a, b):
        return dot(a, b, 0, 0)

    @jax.custom_vjp
    def nn(a, b):
        return nn_raw(a, b)

    nn.defvjp(lambda a, b: (nn_raw(a, b), (a, b)), lambda r, g: (nt_raw(g, r[1]), tn_raw(r[0], g)))

    @jax.custom_vjp
    def nt(a, b):
        return nt_raw(a, b)

    nt.defvjp(lambda a, b: (nt_raw(a, b), (a, b)), lambda r, g: (nn_raw(g, r[1]), tn_raw(g, r[0])))

    @jax.custom_vjp
    def tn(a, b):
        return tn_raw(a, b)

    tn.defvjp(lambda a, b: (tn_raw(a, b), (a, b)), lambda r, g: (nt_raw(r[1], g), nn_raw(r[0], g)))
    return nn, nt, tn


_bf_nn, _bf_nt, _bf_tn = _matmul_family(_dot_bf16)
_hi_nn, _hi_nt, _hi_tn = _matmul_family(_dot_bf16x3)


def _lower_powers(lmat):
    powers = []
    while 2 ** (len(powers) + 1) < lmat.shape[-1]:
        powers.append(_dot_bf16x3(powers[-1] if powers else lmat, powers[-1] if powers else lmat, 1, 0))
    return powers


@jax.custom_vjp
def _unit_lower_solve(lmat, rhs):
    return _unit_lower_solve_fwd(lmat, rhs)[0]


def _unit_lower_solve_fwd(lmat, rhs):
    powers = _lower_powers(lmat)
    x = rhs - _dot_bf16x3(lmat, rhs, 1, 0)
    for p in powers:
        x = x + _dot_bf16x3(p, x, 1, 0)
    return x, (lmat, powers, x)


def _unit_lower_solve_bwd(res, g):
    lmat, powers, x = res
    y = g - _dot_bf16x3(lmat, g, 0, 0)
    for p in powers:
        y = y + _dot_bf16x3(p, y, 0, 0)
    return -_dot_bf16x3(y, x, 1, 1), y


_unit_lower_solve.defvjp(_unit_lower_solve_fwd, _unit_lower_solve_bwd)


@jax.custom_vjp
def _unit_lower_solve_kept(lmat, rhs, powers, x):
    return x


_unit_lower_solve_kept.defvjp(
    lambda lmat, rhs, powers, x: (x, (lmat, powers, x)),
    lambda res, g: _unit_lower_solve_bwd(res, g) + ([jnp.zeros_like(p) for p in res[1]], jnp.zeros_like(res[2])))


@jax.custom_vjp
def _lane_halves(x):
    n = x.shape[-1] // 2
    return x[..., :n], x[..., n:]


_lane_halves.defvjp(lambda x: (_lane_halves(x), None), lambda _, g: (jnp.concatenate(g, axis=-1),))


@jax.custom_vjp
def _row_halves(x):
    n = x.shape[-2] // 2
    return x[..., :n, :], x[..., n:, :]


_row_halves.defvjp(lambda x: (_row_halves(x), None), lambda _, g: (jnp.concatenate(g, axis=-2),))


@jax.custom_vjp
def _swap_halves(t):
    return pltpu.roll(t, 64, 1)


_swap_halves.defvjp(lambda t: (pltpu.roll(t, 64, 1), None), lambda _, g: (pltpu.roll(g, 64, 1),))


@functools.partial(jax.custom_vjp, nondiff_argnums=(2,))
def _shift_rows(x, keep, s):
    return pltpu.roll(x, s, 0) * keep


def _shift_rows_fwd(x, keep, s):
    return pltpu.roll(x, s, 0) * keep, keep


def _shift_rows_bwd(s, keep, g):
    return pltpu.roll(g * keep, keep.shape[0] - s, 0), jnp.zeros_like(keep)


_shift_rows.defvjp(_shift_rows_fwd, _shift_rows_bwd)


def _sigmoid(x):
    return 0.5 * jnp.tanh(0.5 * x) + 0.5


def _softplus(x):
    return jnp.maximum(x, 0.0) + jnp.log(1.0 + jnp.exp(jnp.minimum(x, -x)))


def _silu(x):
    return x * _sigmoid(x)


def _rms(x, w, n=None):
    n = x.shape[-1] if n is None else n
    r = lax.rsqrt(jnp.sum(x * x, axis=-1, keepdims=True) * (1.0 / n) + EPS)
    return x * r * w


def _rope(t, cos_f, sin_f):
    return t * cos_f + _swap_halves(t) * sin_f


def _rope_tables(pos_col, freq_row, sign_row):
    ang = pos_col.astype(F32) * freq_row
    return jnp.cos(ang), jnp.sin(ang) * sign_row


def _onehot_row(lane):
    return (lax.broadcasted_iota(jnp.int32, (1, 128), 1) == lane).astype(F32)


def _row_spec(tm, w):
    return pl.BlockSpec((tm, w), lambda i: (i, 0))


def _const_spec(shape):
    return pl.BlockSpec(shape, lambda *_: (0,) * len(shape))


def _in_proj_fwd(x2, w_an, w_in_p):
    T, D = x2.shape
    tm = min(TOKEN_TILE, T)

    def body(x_ref, wn_ref, w_ref, xn_ref, lat_ref, gqkv_ref, gz_ref, gab_ref):
        x = x_ref[...]
        r = lax.rsqrt(jnp.mean(x * x, axis=-1, keepdims=True) + EPS)
        xn = (x * r * wn_ref[...]).astype(BF16)
        xn_ref[...] = xn
        for ref, (a, b) in zip((lat_ref, gqkv_ref, gz_ref, gab_ref), PROJ_SPLITS):
            ref[...] = _dg(xn, w_ref[a:b, :], 1, 1, None)

    widths = [b - a for a, b in PROJ_SPLITS]
    return pl.pallas_call(
        body, grid=(T // tm,),
        in_specs=[_row_spec(tm, D), _const_spec((1, D)), _const_spec((PROJ_W, D))],
        out_specs=[_row_spec(tm, D)] + [_row_spec(tm, w) for w in widths],
        out_shape=[_sds((T, D), BF16)] + [_sds((T, w), F32) for w in widths],
        compiler_params=_params(("parallel",)), name="in_proj_fwd",
    )(x2, w_an, w_in_p)


def _in_proj_bwd(pieces, w_in_p, x2, w_an, dh, partials):
    T, D = x2.shape
    tm = min(TOKEN_TILE, T)
    widths = [p.shape[1] for p in pieces]
    starts = [sum(widths[:i]) for i in range(len(widths))]
    assert sum(widths) == PROJ_W
    npc, ns = len(pieces), len(partials)

    def body(*refs):
        piece_refs = refs[:npc]
        w_ref, x_ref, wn_ref, dh_ref = refs[npc:npc + 4]
        src_refs = refs[npc + 4:npc + 4 + ns]
        dx_ref, dwn_ref = refs[npc + 4 + ns:npc + 6 + ns]
        dst_refs = refs[npc + 6 + ns:npc + 6 + 2 * ns]
        sems = refs[npc + 6 + 2 * ns:]

        @pl.when(pl.program_id(0) == 0)
        def _():
            for start in _scatter_copies(src_refs, dst_refs, *sems)[0]:
                start()
            dwn_ref[...] = jnp.zeros_like(dwn_ref)

        dxn = jnp.zeros((tm, D), F32)
        for ref, a, width in zip(piece_refs, starts, widths):
            dxn += _dg(ref[...], w_ref[a:a + width, :], 1, 0, None)
        _, pull = jax.vjp(_rms, x_ref[...], wn_ref[...])
        dx, dwn = pull(dxn)
        dx_ref[...] = dx + dh_ref[...]
        dwn_ref[...] += dwn

        @pl.when(pl.program_id(0) == T // tm - 1)
        def _():
            for wait in _scatter_copies(src_refs, dst_refs, *sems)[1]:
                wait()

    return pl.pallas_call(
        body, grid=(T // tm,),
        in_specs=[_row_spec(tm, w) for w in widths] + [_const_spec((PROJ_W, D)), _row_spec(tm, D), _const_spec((1, D)),
                                                       _row_spec(tm, D)] + [_ANY] * ns,
        out_specs=[_row_spec(tm, D), _const_spec((1, D))] + [_ANY] * ns,
        out_shape=[_sds((T, D), F32), _sds((1, D), F32)] + [_scattered_shape(p) for p in partials],
        scratch_shapes=_scatter_scratch(ns),
        compiler_params=_params(("arbitrary",)), name="in_proj_bwd",
    )(*pieces, w_in_p, x2, w_an, dh, *partials)


def _wgrad_pieces(pieces, b, name):
    T, k2 = b.shape
    tt = min(TOKEN_TILE, T)
    widths = [p.shape[1] for p in pieces]
    starts = [sum(widths[:i]) for i in range(len(widths))]
    k1 = sum(widths)

    def body(*refs):
        piece_refs, (b_ref, o_ref, acc_ref) = refs[:len(pieces)], refs[len(pieces):]
        t = pl.program_id(0)

        @pl.when(t == 0)
        def _():
            acc_ref[...] = jnp.zeros_like(acc_ref)

        bt = b_ref[...].astype(BF16)
        for ref, r0, width in zip(piece_refs, starts, widths):
            acc_ref[r0:r0 + width, :] += jnp.dot(ref[...].T, bt, preferred_element_type=F32)

        @pl.when(t == T // tt - 1)
        def _():
            o_ref[...] = acc_ref[...].astype(o_ref.dtype)

    return pl.pallas_call(
        body, grid=(T // tt,),
        in_specs=[pl.BlockSpec((tt, w), lambda t: (t, 0)) for w in widths] + [pl.BlockSpec((tt, k2), lambda t: (t, 0))],
        out_specs=_const_spec((k1, k2)), out_shape=_sds((k1, k2), BF16), scratch_shapes=[pltpu.VMEM((k1, k2), F32)],
        compiler_params=_params(("arbitrary",)), name=name,
    )(*pieces, b)


def _wgrad(a, b, name, column_shards=1, out_dtype=BF16):
    T, k1 = a.shape
    k2 = b.shape[1]
    per_shard = k2 // column_shards
    tt = min(TOKEN_TILE, T)
    b1 = k1
    while b1 * k2 * 4 > WGRAD_OUT_BYTES and b1 % 256 == 0:
        b1 //= 2
    step = _block(per_shard)

    def body(a_ref, b_ref, o_ref, acc_ref):
        t = pl.program_id(1)

        @pl.when(t == 0)
        def _():
            acc_ref[...] = jnp.zeros_like(acc_ref)

        a_t = a_ref[...].astype(BF16).T
        for c0 in range(0, k2, step):
            part = jnp.dot(a_t, b_ref[:, c0:c0 + step].astype(BF16), preferred_element_type=F32)
            if column_shards == 1:
                acc_ref[:, c0:c0 + step] += part
            else:
                acc_ref[c0 // per_shard, :, c0 % per_shard:c0 % per_shard + step] += part

        @pl.when(t == T // tt - 1)
        def _():
            o_ref[...] = acc_ref[...].astype(o_ref.dtype)

    if column_shards == 1:
        block, out_spec, out_shape = (b1, k2), pl.BlockSpec((b1, k2), lambda i, t: (i, 0)), _sds((k1, k2), out_dtype)
    else:
        block = (column_shards, b1, per_shard)
        out_spec, out_shape = pl.BlockSpec(block, lambda i, t: (0, i, 0)), _sds((column_shards, k1, per_shard), out_dtype)
    return pl.pallas_call(
        body, grid=(k1 // b1, T // tt),
        in_specs=[pl.BlockSpec((tt, b1), lambda i, t: (t, i)), pl.BlockSpec((tt, k2), lambda i, t: (t, 0))],
        out_specs=out_spec, out_shape=out_shape, scratch_shapes=[pltpu.VMEM(block, F32)],
        compiler_params=_params(("parallel", "arbitrary")), name=name,
    )(a, b)


def _mla_pre_fn(q_lat, kv_lat, kpe, ln_q, ln_kv, w_list, qn_n, qn_p, kn_n, kn_p, cos_f, sin_f):
    qn = _rms(q_lat, ln_q)
    kvn = _rms(kv_lat, ln_kv)
    kp = _rope(_rms(kpe, kn_p, ROPE_DIM), cos_f, sin_f)
    outs = []
    for h in range(HEADS):
        outs.append(_rms(_bf_nn(qn, w_list[h]), qn_n))
        outs.append(_rope(_rms(_bf_nn(qn, w_list[HEADS + h]), qn_p, ROPE_DIM), cos_f, sin_f))
        outs.append(_rms(_bf_nn(kvn, w_list[2 * HEADS + h]), kn_n))
        outs.append(_bf_nn(kvn, w_list[3 * HEADS + h]))
    return tuple(outs) + (kp,)


def _mla_pre_operands(lat_ref, pos_ref, ln_ref, w_ref, nw_ref, rope_ref):
    cos_f, sin_f = _rope_tables(pos_ref[...], rope_ref[0:1, :], rope_ref[1:2, :])
    diff = (lat_ref[:, 0:LORA], lat_ref[:, LORA:2 * LORA], lat_ref[:, 2 * LORA:LAT_W], ln_ref[0:1, :], ln_ref[1:2, :],
            [w_ref[i].astype(F32) for i in range(4 * HEADS)], nw_ref[0:1, :], nw_ref[1:2, :], nw_ref[2:3, :], nw_ref[3:4, :])
    return diff, cos_f, sin_f


def _mla_pre_fwd(lat, pos, ln_w, w_mla, nw, rope_rows):
    T = lat.shape[0]
    tm = min(TOKEN_TILE, T)

    def body(lat_ref, pos_ref, ln_ref, w_ref, nw_ref, rope_ref, q_ref, k_ref, v_ref):
        diff, cos_f, sin_f = _mla_pre_operands(lat_ref, pos_ref, ln_ref, w_ref, nw_ref, rope_ref)
        outs = _mla_pre_fn(*diff, cos_f, sin_f)
        kp = outs[-1].astype(BF16)
        for h in range(HEADS):
            q_n, q_p, k_n, v = outs[4 * h:4 * h + 4]
            q_ref[:, h * QK_PAD:h * QK_PAD + HEAD_DIM] = q_n.astype(BF16)
            q_ref[:, h * QK_PAD + HEAD_DIM:(h + 1) * QK_PAD] = q_p.astype(BF16)
            k_ref[:, h * QK_PAD:h * QK_PAD + HEAD_DIM] = k_n.astype(BF16)
            k_ref[:, h * QK_PAD + HEAD_DIM:(h + 1) * QK_PAD] = kp
            v_ref[:, h * HEAD_DIM:(h + 1) * HEAD_DIM] = v.astype(BF16)

    return pl.pallas_call(
        body, grid=(T // tm,),
        in_specs=[_row_spec(tm, LAT_W), _row_spec(tm, 1), _const_spec((2, LORA)), _const_spec((4 * HEADS, LORA, 128)),
                  _const_spec((8, 128)), _const_spec((8, 128))],
        out_specs=[_row_spec(tm, HEADS * QK_PAD), _row_spec(tm, HEADS * QK_PAD), _row_spec(tm, HEADS * HEAD_DIM)],
        out_shape=[_sds((T, HEADS * QK_PAD), BF16), _sds((T, HEADS * QK_PAD), BF16), _sds((T, HEADS * HEAD_DIM), BF16)],
        compiler_params=_params(("parallel",)), name="mla_pre_fwd",
    )(lat, pos, ln_w, w_mla, nw, rope_rows)


def _mla_pre_bwd(lat, pos, ln_w, w_mla, nw, rope_rows, dq, dk, dv, halves):
    T = lat.shape[0]
    tm = min(TOKEN_TILE, T)
    ns = len(halves)

    def body(*refs):
        lat_ref, pos_ref, ln_ref, w_ref, nw_ref, rope_ref, dq_ref, dk_ref, dv_ref = refs[:9]
        src_refs = refs[9:9 + ns]
        dlat_ref, dln_ref, dw_ref, dnw_ref = refs[9 + ns:13 + ns]
        dst_refs = refs[13 + ns:13 + 2 * ns]
        sems = refs[13 + 2 * ns:]

        @pl.when(pl.program_id(0) == 0)
        def _():
            for start in _swap_copies(src_refs, dst_refs, *sems)[0]:
                start()
            dln_ref[...] = jnp.zeros_like(dln_ref)
            dw_ref[...] = jnp.zeros_like(dw_ref)
            dnw_ref[...] = jnp.zeros_like(dnw_ref)

        diff, cos_f, sin_f = _mla_pre_operands(lat_ref, pos_ref, ln_ref, w_ref, nw_ref, rope_ref)
        _, pull = jax.vjp(lambda *a: _mla_pre_fn(*a, cos_f, sin_f), *diff)
        cts = []
        d_kp = jnp.zeros((tm, 128), F32)
        for h in range(HEADS):
            cts.append(dq_ref[:, h * QK_PAD:h * QK_PAD + HEAD_DIM])
            cts.append(dq_ref[:, h * QK_PAD + HEAD_DIM:(h + 1) * QK_PAD])
            cts.append(dk_ref[:, h * QK_PAD:h * QK_PAD + HEAD_DIM])
            cts.append(dv_ref[:, h * HEAD_DIM:(h + 1) * HEAD_DIM])
            d_kp += dk_ref[:, h * QK_PAD + HEAD_DIM:(h + 1) * QK_PAD]
        d_ql, d_kvl, d_kpe, d_lnq, d_lnkv, d_w, d_qn_n, d_qn_p, d_kn_n, d_kn_p = pull(tuple(cts) + (d_kp,))
        dlat_ref[:, 0:LORA] = d_ql.astype(BF16)
        dlat_ref[:, LORA:2 * LORA] = d_kvl.astype(BF16)
        dlat_ref[:, 2 * LORA:LAT_W] = d_kpe.astype(BF16)
        dln_ref[0:1, :] += d_lnq
        dln_ref[1:2, :] += d_lnkv
        for i in range(4 * HEADS):
            dw_ref[i] += d_w[i]
        for i, d in enumerate((d_qn_n, d_qn_p, d_kn_n, d_kn_p)):
            dnw_ref[i:i + 1, :] += d

        @pl.when(pl.program_id(0) == T // tm - 1)
        def _():
            for wait in _swap_copies(src_refs, dst_refs, *sems)[1]:
                wait()

    return pl.pallas_call(
        body, grid=(T // tm,),
        in_specs=[_row_spec(tm, LAT_W), _row_spec(tm, 1), _const_spec((2, LORA)), _const_spec((4 * HEADS, LORA, 128)),
                  _const_spec((8, 128)), _const_spec((8, 128)),
                  _row_spec(tm, HEADS * QK_PAD), _row_spec(tm, HEADS * QK_PAD), _row_spec(tm, HEADS * HEAD_DIM)] + [_ANY] * ns,
        out_specs=[_row_spec(tm, LAT_W), _const_spec((2, LORA)), _const_spec((4 * HEADS, LORA, 128)), _const_spec((8, 128))]
                  + [_ANY] * ns,
        out_shape=[_sds((T, LAT_W), BF16), _sds((2, LORA), F32), _sds((4 * HEADS, LORA, 128), F32), _sds((8, 128), F32)]
                  + [_swapped_shape(h) for h in halves],
        scratch_shapes=_swap_scratch(ns),
        compiler_params=_params(("arbitrary",)), name="mla_pre_bwd",
    )(lat, pos, ln_w, w_mla, nw, rope_rows, dq, dk, dv, *halves)


def _causal_mask(i, j, tq, tk):
    row = i * tq + lax.broadcasted_iota(jnp.int32, (tq, tk), 0)
    col = j * tk + lax.broadcasted_iota(jnp.int32, (tq, tk), 1)
    return col <= row


def _attn_fwd(q, k, v, shards):
    B, S, _ = q.shape
    t = min(ATTN_TILE, S)
    nq = S // t
    ns = len(shards)

    hp = ATTN_HEADS_PER_STEP
    qk = lambda h: slice(h * QK_PAD, (h + 1) * QK_PAD)
    vd = lambda h: slice(h * HEAD_DIM, (h + 1) * HEAD_DIM)

    def body(*refs):
        q_ref, k_ref, v_ref = refs[:3]
        src_refs = refs[3:3 + ns]
        o_ref, lse_ref = refs[3 + ns:5 + ns]
        dst_refs = refs[5 + ns:5 + 2 * ns]
        sems = refs[5 + 2 * ns:]
        b, g, i = pl.program_id(0), pl.program_id(1), pl.program_id(2)
        qb = [q_ref[0, :, qk(h)] for h in range(hp)]

        step_no = (b * (HEADS // hp) + g) * nq + i
        for phase, at in enumerate((0, (3 * B * (HEADS // hp) * nq) // 4)):
            @pl.when(step_no == at)
            def _(phase=phase):
                for call in _gather_copies(src_refs, dst_refs, *sems)[phase]:
                    call()

        def step(j, carry, diagonal):
            rows = pl.ds(pl.multiple_of(j * t, t), t)
            s = [_dg(qb[h], k_ref[0, rows, qk(h)], 1, 1, None) * ATTN_SCALE for h in range(hp)]
            if diagonal:
                keep = _causal_mask(0, 0, t, t)
                s = [jnp.where(keep, x, -1e30) for x in s]
            m_new = [jnp.maximum(carry[h][0], jnp.max(s[h], axis=-1, keepdims=True)) for h in range(hp)]
            p = [jnp.exp(s[h] - m_new[h]) for h in range(hp)]
            alpha = [jnp.exp(carry[h][0] - m_new[h]) for h in range(hp)]
            l = [alpha[h] * carry[h][1] + jnp.sum(p[h], axis=-1, keepdims=True) for h in range(hp)]
            pv = [jnp.dot(p[h].astype(BF16), v_ref[0, rows, vd(h)], preferred_element_type=F32) for h in range(hp)]
            return tuple((m_new[h], l[h], alpha[h] * carry[h][2] + pv[h]) for h in range(hp))

        init = tuple((jnp.full((t, 1), -1e30, F32), jnp.zeros((t, 1), F32), jnp.zeros((t, HEAD_DIM), F32)) for _ in range(hp))
        below = lax.fori_loop(0, i, lambda j, carry: step(j, carry, False), init)
        for h, (m, l, acc) in enumerate(step(i, below, True)):
            o_ref[0, :, vd(h)] = acc / l
            lse_ref[0, h, 0] = (m + jnp.log(l)).T

        @pl.when((b == B - 1) & (g == HEADS // hp - 1) & (i == nq - 1))
        def _():
            for wait in _gather_copies(src_refs, dst_refs, *sems)[2]:
                wait()

    return pl.pallas_call(
        body, grid=(B, HEADS // hp, nq),
        in_specs=[pl.BlockSpec((1, t, hp * QK_PAD), lambda b, g, i: (b, i, g)),
                  pl.BlockSpec((1, S, hp * QK_PAD), lambda b, g, i: (b, 0, g)),
                  pl.BlockSpec((1, S, hp * HEAD_DIM), lambda b, g, i: (b, 0, g))] + [_ANY] * ns,
        out_specs=[pl.BlockSpec((1, t, hp * HEAD_DIM), lambda b, g, i: (b, i, g)),
                   pl.BlockSpec((1, hp, 1, 1, t), lambda b, g, i: (b, g, i, 0, 0))] + [_ANY] * ns,
        out_shape=[_sds((B, S, HEADS * HEAD_DIM), F32), _sds((B, HEADS, nq, 1, t), F32)] + [_sds((4,) + s.shape, s.dtype) for s in shards],
        scratch_shapes=_gather_scratch(ns),
        compiler_params=_params(("arbitrary", "arbitrary", "arbitrary")), name="attn_fwd",
    )(q, k, v, *shards)


def _attn_bwd(q, k, v, o, lse, do, partials):
    B, S, _ = q.shape
    t = min(ATTN_TILE, S)
    nq = S // t
    ns = len(partials)

    hp = ATTN_HEADS_PER_STEP
    qk = lambda h: slice(h * QK_PAD, (h + 1) * QK_PAD)
    vd = lambda h: slice(h * HEAD_DIM, (h + 1) * HEAD_DIM)
    heads = range(hp)

    def body(*refs):
        q_ref, k_ref, v_ref, o_ref, lse_ref, do_ref = refs[:6]
        src_refs = refs[6:6 + ns]
        dq_ref, dk_ref, dv_ref = refs[6 + ns:9 + ns]
        dst_refs = refs[9 + ns:9 + 2 * ns]
        dsum_ref, send_sems, recv_sems, local_sems = refs[9 + 2 * ns:]
        b, g, j = pl.program_id(0), pl.program_id(1), pl.program_id(2)

        @pl.when((b == 0) & (g == 0) & (j == 0))
        def _():
            for start in _scatter_copies(src_refs, dst_refs, send_sems, recv_sems, local_sems)[0]:
                start()

        @pl.when(j == 0)
        def _():
            dq_ref[...] = jnp.zeros_like(dq_ref)
            for h in heads:
                for blk in range(nq):
                    rows = slice(blk * t, (blk + 1) * t)
                    dsum_ref[h, blk] = jnp.sum(do_ref[0, rows, vd(h)] * o_ref[0, rows, vd(h)], axis=-1, keepdims=True).T

        kb = [k_ref[0, :, qk(h)] for h in heads]
        vb = [v_ref[0, :, vd(h)] for h in heads]

        def step(i, carry, diagonal):
            rows = pl.ds(pl.multiple_of(i * t, t), t)
            qb = [q_ref[0, rows, qk(h)] for h in heads]
            dob = [do_ref[0, rows, vd(h)].astype(BF16) for h in heads]
            s = [_dg(kb[h], qb[h], 1, 1, None) * ATTN_SCALE for h in heads]
            p = [jnp.exp(s[h] - lse_ref[0, h, i]) for h in heads]
            if diagonal:
                key = lax.broadcasted_iota(jnp.int32, (t, t), 0)
                query = lax.broadcasted_iota(jnp.int32, (t, t), 1)
                p = [jnp.where(key <= query, x, 0.0) for x in p]
            dp = [_dg(vb[h], dob[h], 1, 1, None) for h in heads]
            dv = [carry[h][1] + jnp.dot(p[h].astype(BF16), dob[h], preferred_element_type=F32) for h in heads]
            ds = [(p[h] * (dp[h] - dsum_ref[h, i]) * ATTN_SCALE).astype(BF16) for h in heads]
            for h in heads:
                dq_ref[0, rows, qk(h)] += _dg(ds[h], kb[h], 0, 0, None)
            return tuple((carry[h][0] + jnp.dot(ds[h], qb[h], preferred_element_type=F32), dv[h]) for h in heads)

        zeros = tuple((jnp.zeros((t, QK_PAD), F32), jnp.zeros((t, HEAD_DIM), F32)) for _ in heads)
        on_diagonal = step(j, zeros, True)
        done = lax.fori_loop(j + 1, nq, lambda i, carry: step(i, carry, False), on_diagonal)
        for h, (dk, dv) in enumerate(done):
            dk_ref[0, :, qk(h)] = dk
            dv_ref[0, :, vd(h)] = dv

        @pl.when((b == B - 1) & (g == HEADS // hp - 1) & (j == nq - 1))
        def _():
            for wait in _scatter_copies(src_refs, dst_refs, send_sems, recv_sems, local_sems)[1]:
                wait()

    return pl.pallas_call(
        body, grid=(B, HEADS // hp, nq),
        in_specs=[pl.BlockSpec((1, S, hp * QK_PAD), lambda b, g, j: (b, 0, g)),
                  pl.BlockSpec((1, t, hp * QK_PAD), lambda b, g, j: (b, j, g)),
                  pl.BlockSpec((1, t, hp * HEAD_DIM), lambda b, g, j: (b, j, g)),
                  pl.BlockSpec((1, S, hp * HEAD_DIM), lambda b, g, j: (b, 0, g)),
                  pl.BlockSpec((1, hp, nq, 1, t), lambda b, g, j: (b, g, 0, 0, 0)),
                  pl.BlockSpec((1, S, hp * HEAD_DIM), lambda b, g, j: (b, 0, g))] + [_ANY] * ns,
        out_specs=[pl.BlockSpec((1, S, hp * QK_PAD), lambda b, g, j: (b, 0, g)),
                   pl.BlockSpec((1, t, hp * QK_PAD), lambda b, g, j: (b, j, g)),
                   pl.BlockSpec((1, t, hp * HEAD_DIM), lambda b, g, j: (b, j, g))] + [_ANY] * ns,
        out_shape=[_sds((B, S, HEADS * QK_PAD), F32), _sds((B, S, HEADS * QK_PAD), F32), _sds((B, S, HEADS * HEAD_DIM), F32)]
                  + [_scattered_shape(p) for p in partials],
        scratch_shapes=[pltpu.VMEM((hp, nq, 1, t), F32)] + _scatter_scratch(ns),
        compiler_params=_params(("arbitrary", "arbitrary", "arbitrary")), name="attn_bwd",
    )(q, k, v, o, lse, do, *partials)


def _gdn_pre_fn(xq, xk, xv, wq, wk, wv, keeps):
    def conv_silu(x, w):
        acc = x * w[3]
        for s in (1, 2, 3):
            acc = acc + _shift_rows(x, keeps[s - 1], s) * w[3 - s]
        return _silu(acc)

    def l2(x):
        return x * lax.rsqrt(jnp.sum(x * x, axis=-1, keepdims=True) + EPS)

    return l2(conv_silu(xq, wq)) * (HEAD_DIM ** -0.5), l2(conv_silu(xk, wk)), conv_silu(xv, wv)


def _gdn_pre_specs(S):
    x_specs = [pl.BlockSpec((1, S, HEAD_DIM), lambda h, b, g=g: (b, 0, g * HEADS + h)) for g in range(3)]
    w_specs = [pl.BlockSpec((CONV_TAPS, HEAD_DIM), lambda h, b, g=g: (0, g * HEADS + h)) for g in range(3)]
    out_spec = pl.BlockSpec((1, S, HEAD_DIM), lambda h, b: (b, 0, h))
    return x_specs, w_specs, out_spec


def _row_keeps(S):
    t = lax.broadcasted_iota(jnp.int32, (S, HEAD_DIM), 0)
    return [(t >= s).astype(F32) for s in (1, 2, 3)]


def _gdn_pre_fwd(gqkv, conv_w):
    B, S, _ = gqkv.shape
    x_specs, w_specs, out_spec = _gdn_pre_specs(S)

    def body(xq_ref, xk_ref, xv_ref, wq_ref, wk_ref, wv_ref, q_ref, k_ref, v_ref):
        taps = [[w[i:i + 1, :] for i in range(CONV_TAPS)] for w in (wq_ref, wk_ref, wv_ref)]
        q, k, v = _gdn_pre_fn(xq_ref[0], xk_ref[0], xv_ref[0], *taps, _row_keeps(S))
        q_ref[0], k_ref[0], v_ref[0] = q, k, v

    return pl.pallas_call(
        body, grid=(HEADS, B), in_specs=x_specs + w_specs, out_specs=[out_spec] * 3,
        out_shape=[_sds((B, S, HEADS * HEAD_DIM), F32)] * 3,
        compiler_params=_params(("parallel", "parallel")), name="gdn_pre_fwd",
    )(gqkv, gqkv, gqkv, conv_w, conv_w, conv_w)


def _gdn_pre_bwd(gqkv, conv_w, dq, dk, dv, halves):
    B, S, _ = gqkv.shape
    x_specs, w_specs, out_spec = _gdn_pre_specs(S)
    dw_spec = pl.BlockSpec((CONV_TAPS, HEAD_DIM), lambda h, b: (0, h))
    ns = len(halves)

    def body(*refs):
        xq_ref, xk_ref, xv_ref, wq_ref, wk_ref, wv_ref, dq_ref, dk_ref, dv_ref = refs[:9]
        src_refs = refs[9:9 + ns]
        dxq_ref, dxk_ref, dxv_ref, dwq_ref, dwk_ref, dwv_ref = refs[9 + ns:15 + ns]
        dst_refs = refs[15 + ns:15 + 2 * ns]
        sems = refs[15 + 2 * ns:]
        first = (pl.program_id(0) == 0) & (pl.program_id(1) == 0)
        last = (pl.program_id(0) == HEADS - 1) & (pl.program_id(1) == B - 1)

        @pl.when(first)
        def _():
            for start in _swap_copies(src_refs, dst_refs, *sems)[0]:
                start()

        @pl.when(pl.program_id(1) == 0)
        def _():
            for r in (dwq_ref, dwk_ref, dwv_ref):
                r[...] = jnp.zeros_like(r)

        taps = [[w[i:i + 1, :] for i in range(CONV_TAPS)] for w in (wq_ref, wk_ref, wv_ref)]
        keeps = _row_keeps(S)
        _, pull = jax.vjp(lambda *a: _gdn_pre_fn(*a, keeps), xq_ref[0], xk_ref[0], xv_ref[0], *taps)
        dxq, dxk, dxv, dwq, dwk, dwv = pull((dq_ref[0], dk_ref[0], dv_ref[0]))
        dxq_ref[0], dxk_ref[0], dxv_ref[0] = dxq.astype(BF16), dxk.astype(BF16), dxv.astype(BF16)
        for ref, dw in ((dwq_ref, dwq), (dwk_ref, dwk), (dwv_ref, dwv)):
            for i in range(CONV_TAPS):
                ref[i:i + 1, :] += dw[i]

        @pl.when(last)
        def _():
            for wait in _swap_copies(src_refs, dst_refs, *sems)[1]:
                wait()

    hw = HEADS * HEAD_DIM
    return pl.pallas_call(
        body, grid=(HEADS, B), in_specs=x_specs + w_specs + [out_spec] * 3 + [_ANY] * ns,
        out_specs=[out_spec] * 3 + [dw_spec] * 3 + [_ANY] * ns,
        out_shape=[_sds((B, S, hw), BF16)] * 3 + [_sds((CONV_TAPS, hw), F32)] * 3 + [_swapped_shape(h) for h in halves],
        scratch_shapes=_swap_scratch(ns),
        compiler_params=_params(("arbitrary", "arbitrary")), name="gdn_pre_bwd",
    )(gqkv, gqkv, gqkv, conv_w, conv_w, conv_w, dq, dk, dv, *halves)


def _chunk_masks():
    i = lax.broadcasted_iota(jnp.int32, (CHUNK, CHUNK), 0)
    j = lax.broadcasted_iota(jnp.int32, (CHUNK, CHUNK), 1)
    lower, after = (j <= i).astype(F32), (j > i).astype(F32)
    return {"le": lower, "le_gt": jnp.concatenate([lower, after], axis=0), "strict": (j < i).astype(F32)}


def _gdn_chunk_fn(groups, masks, solve=_unit_lower_solve):
    lane = lax.broadcasted_iota(jnp.int32, (groups, 1, 128), 2)
    head = lax.broadcasted_iota(jnp.int32, (groups, 1, 128), 0) % HEADS
    pick_a, pick_b = (lane == head).astype(F32), (lane == head + HEADS).astype(F32)
    lower, lower_after, strict = (jnp.broadcast_to(masks[n], (groups,) + masks[n].shape) for n in ("le", "le_gt", "strict"))
    ones_row = jnp.ones((1, 1, HEAD_DIM), F32)

    def f(q, k, v, gab, a_row, dt_row, state):
        ga = jnp.sum(gab * pick_a, axis=2, keepdims=True)
        gb = jnp.sum(gab * pick_b, axis=2, keepdims=True)
        a_log = jnp.sum(a_row * pick_a, axis=2, keepdims=True)
        dt_bias = jnp.sum(dt_row * pick_a, axis=2, keepdims=True)
        beta = _sigmoid(gb)
        g = -jnp.exp(a_log) * _softplus(ga + dt_bias)
        g_wide = g * ones_row
        cum, rest = _row_halves(_hi_nn(lower_after, g_wide))
        total = jnp.sum(g_wide, axis=1, keepdims=True)
        diff = _hi_nn(lower, g * strict)
        decay = lower * jnp.exp(diff)
        e_cum = jnp.exp(cum)
        kk, qk = _row_halves(_bf_nt(jnp.concatenate([k, q], axis=1), k))
        lmat = strict * (beta * kk * decay)
        u, w = _lane_halves(solve(lmat, jnp.concatenate([v * beta, k * (beta * e_cum)], axis=2)))
        w_state, q_state = _row_halves(_bf_nn(jnp.concatenate([w, q * e_cum], axis=1), state))
        v_new = u - w_state
        o = q_state + _bf_nn(qk * decay, v_new)
        new_state = state * jnp.exp(total) + _bf_tn(k * jnp.exp(rest), v_new)
        return o, new_state

    return f


def _gdn_chunk_fwd(q, k, v, gab, scal, shards):
    B, S, W = q.shape
    N = S // CHUNK
    ns = len(shards)

    def body(*refs):
        q_ref, k_ref, v_ref, gab_ref, sc_ref = refs[:5]
        src_refs = refs[5:5 + ns]
        o_ref, st_ref, pw_ref, sol_ref = refs[5 + ns:9 + ns]
        dst_refs = refs[9 + ns:9 + 2 * ns]
        state_ref, send_sems, recv_sems, local_sems = refs[9 + 2 * ns:]
        n = pl.program_id(0)
        kept = {}

        @pl.when(n == 0)
        def _():
            for start in _gather_copies(src_refs, dst_refs, send_sems, recv_sems, local_sems)[0]:
                start()
            state_ref[...] = jnp.zeros_like(state_ref)

        @pl.when(n == (2 * N) // 3)
        def _():
            for pass_on in _gather_copies(src_refs, dst_refs, send_sems, recv_sems, local_sems)[1]:
                pass_on()

        groups = [(b, h) for b in range(B) for h in range(HEADS)]
        gather = lambda ref: jnp.stack([ref[b, :, h * HEAD_DIM:(h + 1) * HEAD_DIM] for b, h in groups])
        state = state_ref[...]
        for i, (b, h) in enumerate(groups):
            st_ref[b, 0, h] = state[i]
        def solve_and_keep(lmat, rhs):
            kept["x"], (_, kept["powers"], _) = _unit_lower_solve_fwd(lmat, rhs)
            return kept["x"]

        o, new_state = _gdn_chunk_fn(len(groups), _chunk_masks(), solve_and_keep)(
            gather(q_ref), gather(k_ref), gather(v_ref), jnp.stack([gab_ref[b] for b, _ in groups]), sc_ref[0:1, :], sc_ref[1:2, :], state)
        for i, (b, h) in enumerate(groups):
            o_ref[b, :, h * HEAD_DIM:(h + 1) * HEAD_DIM] = o[i]
            sol_ref[b, 0, h] = kept["x"][i]
            for p, power in enumerate(kept["powers"]):
                pw_ref[b, 0, h, p] = power[i]
        state_ref[...] = new_state

        @pl.when(n == N - 1)
        def _():
            for wait in _gather_copies(src_refs, dst_refs, send_sems, recv_sems, local_sems)[2]:
                wait()

    seq = pl.BlockSpec((B, CHUNK, W), lambda n: (0, n, 0))
    return pl.pallas_call(
        body, grid=(N,),
        in_specs=[seq, seq, seq, pl.BlockSpec((B, CHUNK, GAB_W), lambda n: (0, n, 0)), _const_spec((8, 128))] + [_ANY] * ns,
        out_specs=[seq, pl.BlockSpec((B, 1, HEADS, HEAD_DIM, HEAD_DIM), lambda n: (0, n, 0, 0, 0)),
                   pl.BlockSpec((B, 1, HEADS, SOLVE_POWERS, CHUNK, CHUNK), lambda n: (0, n, 0, 0, 0, 0)),
                   pl.BlockSpec((B, 1, HEADS, CHUNK, 2 * HEAD_DIM), lambda n: (0, n, 0, 0, 0))] + [_ANY] * ns,
        out_shape=[_sds((B, S, W), F32), _sds((B, N, HEADS, HEAD_DIM, HEAD_DIM), F32),
                   _sds((B, N, HEADS, SOLVE_POWERS, CHUNK, CHUNK), F32), _sds((B, N, HEADS, CHUNK, 2 * HEAD_DIM), F32)]
                  + [_sds((4,) + s.shape, s.dtype) for s in shards],
        scratch_shapes=[pltpu.VMEM((B * HEADS, HEAD_DIM, HEAD_DIM), F32)] + _gather_scratch(ns),
        compiler_params=_params(("arbitrary",)), name="gdn_chunk_fwd",
    )(q, k, v, gab, scal, *shards)


def _gdn_chunk_bwd(q, k, v, gab, scal, states, powers, solutions, do, partials):
    B, S, W = q.shape
    N = S // CHUNK
    ns = len(partials)

    def body(*refs):
        q_ref, k_ref, v_ref, gab_ref, sc_ref, st_ref, pw_ref, sol_ref, do_ref = refs[:9]
        src_refs = refs[9:9 + ns]
        dq_ref, dk_ref, dv_ref, dgab_ref, dsc_ref = refs[9 + ns:14 + ns]
        dst_refs = refs[14 + ns:14 + 2 * ns]
        dstate_ref, send_sems, recv_sems, local_sems = refs[14 + 2 * ns:]
        n = pl.program_id(0)

        @pl.when(n == 0)
        def _():
            for start in _scatter_copies(src_refs, dst_refs, send_sems, recv_sems, local_sems)[0]:
                start()
            dstate_ref[...] = jnp.zeros_like(dstate_ref)
            dsc_ref[...] = jnp.zeros_like(dsc_ref)

        groups = [(b, h) for b in range(B) for h in range(HEADS)]
        gather = lambda ref: jnp.stack([ref[b, :, h * HEAD_DIM:(h + 1) * HEAD_DIM] for b, h in groups])
        kept_powers = [jnp.stack([pw_ref[b, 0, h, p] for b, h in groups]) for p in range(SOLVE_POWERS)]
        kept_x = jnp.stack([sol_ref[b, 0, h] for b, h in groups])
        solve = lambda lmat, rhs: _unit_lower_solve_kept(lmat, rhs, kept_powers, kept_x)
        _, pull = jax.vjp(_gdn_chunk_fn(len(groups), _chunk_masks(), solve), gather(q_ref), gather(k_ref), gather(v_ref),
                          jnp.stack([gab_ref[b] for b, _ in groups]), sc_ref[0:1, :], sc_ref[1:2, :],
                          jnp.stack([st_ref[b, 0, h] for b, h in groups]))
        dq, dk, dv, dg, d_a, d_dt, dstate = pull((gather(do_ref), dstate_ref[...]))
        for i, (b, h) in enumerate(groups):
            lanes = slice(h * HEAD_DIM, (h + 1) * HEAD_DIM)
            dq_ref[b, :, lanes] = dq[i]
            dk_ref[b, :, lanes] = dk[i]
            dv_ref[b, :, lanes] = dv[i]
        for b in range(B):
            dgab_ref[b] = sum(dg[b * HEADS + h] for h in range(HEADS)).astype(BF16)
        dstate_ref[...] = dstate
        dsc_ref[0:1, :] += d_a
        dsc_ref[1:2, :] += d_dt

        @pl.when(n == N - 1)
        def _():
            for wait in _scatter_copies(src_refs, dst_refs, send_sems, recv_sems, local_sems)[1]:
                wait()

    seq = pl.BlockSpec((B, CHUNK, W), lambda n: (0, N - 1 - n, 0))
    gab_spec = pl.BlockSpec((B, CHUNK, GAB_W), lambda n: (0, N - 1 - n, 0))
    return pl.pallas_call(
        body, grid=(N,),
        in_specs=[seq, seq, seq, gab_spec, _const_spec((8, 128)),
                  pl.BlockSpec((B, 1, HEADS, HEAD_DIM, HEAD_DIM), lambda n: (0, N - 1 - n, 0, 0, 0)),
                  pl.BlockSpec((B, 1, HEADS, SOLVE_POWERS, CHUNK, CHUNK), lambda n: (0, N - 1 - n, 0, 0, 0, 0)),
                  pl.BlockSpec((B, 1, HEADS, CHUNK, 2 * HEAD_DIM), lambda n: (0, N - 1 - n, 0, 0, 0)), seq] + [_ANY] * ns,
        out_specs=[seq, seq, seq, gab_spec, _const_spec((8, 128))] + [_ANY] * ns,
        out_shape=[_sds((B, S, W), F32)] * 3 + [_sds((B, S, GAB_W), BF16), _sds((8, 128), F32)] + [_scattered_shape(p) for p in partials],
        scratch_shapes=[pltpu.VMEM((B * HEADS, HEAD_DIM, HEAD_DIM), F32)] + _scatter_scratch(ns),
        compiler_params=_params(("arbitrary",)), name="gdn_chunk_bwd",
    )(q, k, v, gab, scal, states, powers, solutions, do, *partials)


def _mix_fn(ao, go, gz, w_mla, w_gdn):
    return tuple(_rms(ao[h], w_mla[h]) for h in range(HEADS)) + tuple(_rms(go[h], w_gdn) * _silu(gz[h]) for h in range(HEADS))


def _mix_operands(ao_ref, go_ref, gz_ref, nw_ref):
    blocks = lambda ref: [ref[:, h * HEAD_DIM:(h + 1) * HEAD_DIM] for h in range(HEADS)]
    return blocks(ao_ref), blocks(go_ref), blocks(gz_ref), [nw_ref[h:h + 1, :] for h in range(HEADS)], nw_ref[HEADS:HEADS + 1, :]


def _mix_fwd(ao, go, gz, nw, w_out, x2):
    T, D = x2.shape
    tm = min(TOKEN_TILE, T)
    MW = 2 * HEADS * HEAD_DIM

    def body(ao_ref, go_ref, gz_ref, nw_ref, w_ref, x_ref, mix_ref, h_ref):
        outs = _mix_fn(*_mix_operands(ao_ref, go_ref, gz_ref, nw_ref))
        for i, piece in enumerate(outs):
            mix_ref[:, i * HEAD_DIM:(i + 1) * HEAD_DIM] = piece.astype(BF16)
        h_ref[...] = x_ref[...] + jnp.dot(mix_ref[...], w_ref[...], preferred_element_type=F32)

    half = HEADS * HEAD_DIM
    return pl.pallas_call(
        body, grid=(T // tm,),
        in_specs=[_row_spec(tm, half), _row_spec(tm, half), _row_spec(tm, half), _const_spec((8, 128)), _const_spec((MW, D)),
                  _row_spec(tm, D)],
        out_specs=[_row_spec(tm, MW), _row_spec(tm, D)],
        out_shape=[_sds((T, MW), BF16), _sds((T, D), F32)],
        compiler_params=_params(("parallel",)), name="mix_fwd",
    )(ao, go, gz, nw, w_out, x2)


def _mix_bwd(ao, go, gz, nw, w_out, dh):
    T, D = dh.shape
    tm = min(TOKEN_TILE, T)
    MW = 2 * HEADS * HEAD_DIM
    half = HEADS * HEAD_DIM

    def body(ao_ref, go_ref, gz_ref, nw_ref, w_ref, dh_ref, dao_ref, dgo_ref, dgz_ref, dnw_ref):
        @pl.when(pl.program_id(0) == 0)
        def _():
            dnw_ref[...] = jnp.zeros_like(dnw_ref)

        d_mix = _dg(dh_ref[...].astype(BF16), w_ref[...], 1, 1, None)
        cts = tuple(d_mix[:, i * HEAD_DIM:(i + 1) * HEAD_DIM] for i in range(2 * HEADS))
        _, pull = jax.vjp(_mix_fn, *_mix_operands(ao_ref, go_ref, gz_ref, nw_ref))
        d_ao, d_go, d_gz, d_wm, d_wg = pull(cts)
        for h in range(HEADS):
            lanes = slice(h * HEAD_DIM, (h + 1) * HEAD_DIM)
            dao_ref[:, lanes] = d_ao[h]
            dgo_ref[:, lanes] = d_go[h]
            dgz_ref[:, lanes] = d_gz[h].astype(BF16)
            dnw_ref[h:h + 1, :] += d_wm[h]
        dnw_ref[HEADS:HEADS + 1, :] += d_wg

    return pl.pallas_call(
        body, grid=(T // tm,),
        in_specs=[_row_spec(tm, half), _row_spec(tm, half), _row_spec(tm, half), _const_spec((8, 128)), _const_spec((MW, D)),
                  _row_spec(tm, D)],
        out_specs=[_row_spec(tm, half)] * 3 + [_const_spec((8, 128))],
        out_shape=[_sds((T, half), F32)] * 2 + [_sds((T, half), BF16), _sds((8, 128), F32)],
        compiler_params=_params(("arbitrary",)), name="mix_bwd",
    )(ao, go, gz, nw, w_out, dh)


def _up_spec(w_up, tf):
    per_shard = w_up.shape[2] // tf
    return pl.BlockSpec((None, w_up.shape[1], tf), lambda i, j: (j // per_shard, 0, j % per_shard))


def _mlp_fwd(h2, w_mn, w_up, w_down, target):
    T, D = h2.shape
    FF = w_down.shape[0]
    tm, tf = min(MLP_TOKEN_TILE, T), min(FF_TILE, w_up.shape[2])
    nf = FF // tf

    def body(h_ref, wn_ref, wu_ref, wd_ref, t_ref, hn_ref, act_ref, dy_ref, sq_ref, acc_ref):
        j = pl.program_id(1)

        @pl.when(j == 0)
        def _():
            hn_ref[...] = _rms(h_ref[...], wn_ref[...]).astype(BF16)
            acc_ref[...] = jnp.zeros_like(acc_ref)

        up = jnp.dot(hn_ref[...], wu_ref[...], preferred_element_type=F32)
        act = jnp.square(jnp.maximum(up, 0.0)).astype(BF16)
        act_ref[...] = act
        acc_ref[...] += jnp.dot(act, wd_ref[...], preferred_element_type=F32)

        @pl.when(j == nf - 1)
        def _():
            err = h_ref[...] + acc_ref[...] - t_ref[...]
            dy_ref[...] = err * (1.0 / D)
            sq_ref[...] = jnp.zeros_like(sq_ref) + jnp.sum(err * err)

    tok = lambda w: pl.BlockSpec((tm, w), lambda i, j: (i, 0))
    return pl.pallas_call(
        body, grid=(T // tm, nf),
        in_specs=[tok(D), _const_spec((1, D)), _up_spec(w_up, tf), pl.BlockSpec((tf, D), lambda i, j: (j, 0)), tok(D)],
        out_specs=[tok(D), pl.BlockSpec((tm, tf), lambda i, j: (i, j)), tok(D), pl.BlockSpec((1, 8, 128), lambda i, j: (i, 0, 0))],
        out_shape=[_sds((T, D), BF16), _sds((T, FF), BF16), _sds((T, D), F32), _sds((T // tm, 8, 128), F32)],
        scratch_shapes=[pltpu.VMEM((tm, D), F32)],
        compiler_params=_params(("parallel", "arbitrary")), name="mlp_fwd",
    )(h2, w_mn, w_up, w_down, target)


def _mlp_bwd(h2, w_mn, act, w_up, w_down, dy):
    T, D = h2.shape
    FF = w_down.shape[0]
    tm, tf = min(MLP_TOKEN_TILE, T), min(FF_TILE, w_up.shape[2])
    nf = FF // tf

    def body(h_ref, wn_ref, act_ref, wu_ref, wd_ref, dy_ref, dh_ref, dup_ref, dwn_ref, acc_ref, dyb_ref):
        i, j = pl.program_id(0), pl.program_id(1)

        @pl.when((i == 0) & (j == 0))
        def _():
            dwn_ref[...] = jnp.zeros_like(dwn_ref)

        @pl.when(j == 0)
        def _():
            acc_ref[...] = jnp.zeros_like(acc_ref)
            dyb_ref[...] = dy_ref[...].astype(BF16)

        r = jnp.sqrt(act_ref[...].astype(F32))
        d_act = _dg(dyb_ref[...], wd_ref[...], 1, 1, None)
        d_up = (d_act * (2.0 * r)).astype(BF16)
        dup_ref[...] = d_up
        acc_ref[...] += _dg(d_up, wu_ref[...], 1, 1, None)

        @pl.when(j == nf - 1)
        def _():
            _, pull = jax.vjp(_rms, h_ref[...], wn_ref[...])
            dh, dwn = pull(acc_ref[...])
            dh_ref[...] = dh + dy_ref[...]
            dwn_ref[...] += dwn

    tok = lambda w: pl.BlockSpec((tm, w), lambda i, j: (i, 0))
    ff = pl.BlockSpec((tm, tf), lambda i, j: (i, j))
    return pl.pallas_call(
        body, grid=(T // tm, nf),
        in_specs=[tok(D), _const_spec((1, D)), ff, _up_spec(w_up, tf), pl.BlockSpec((tf, D), lambda i, j: (j, 0)), tok(D)],
        out_specs=[tok(D), ff, _const_spec((1, D))],
        out_shape=[_sds((T, D), F32), _sds((T, FF), BF16), _sds((1, D), F32)],
        scratch_shapes=[pltpu.VMEM((tm, D), F32), pltpu.VMEM((tm, D), BF16)],
        compiler_params=_params(("arbitrary", "arbitrary")), name="mlp_bwd",
    )(h2, w_mn, act, w_up, w_down, dy)


def _rope_pad(a):
    z = jnp.zeros(a.shape[:-1] + (ROPE_HALF,), a.dtype)
    return jnp.concatenate([a[..., :ROPE_HALF], z, a[..., ROPE_HALF:], z], axis=-1)


def _rope_unpad(a):
    return jnp.concatenate([a[..., :ROPE_HALF], a[..., 2 * ROPE_HALF:3 * ROPE_HALF]], axis=-1)


_G0 = 2 * LORA + ROPE_DIM
W_IN_COLS = _G0 + GQKV_W + GZ_W + 2 * HEADS


def _widen_w_in_t(w_t):
    z = jnp.zeros((ROPE_HALF, w_t.shape[1]), w_t.dtype)
    pad = jnp.zeros((GAB_W - 2 * HEADS, w_t.shape[1]), w_t.dtype)
    return jnp.concatenate([w_t[:2 * LORA + ROPE_HALF], z, w_t[2 * LORA + ROPE_HALF:_G0], z, w_t[_G0:], pad], axis=0)


def _narrow_w_in_t(w_t):
    return jnp.concatenate([w_t[:2 * LORA + ROPE_HALF], w_t[2 * LORA + 2 * ROPE_HALF:2 * LORA + 3 * ROPE_HALF],
                            w_t[LAT_W:LAT_W + W_IN_COLS - _G0]], axis=0)


def _stack_mla(w_uq, w_ukv):
    uq = w_uq.reshape(LORA, HEADS, QK_DIM)
    ukv = w_ukv.reshape(LORA, HEADS, 2 * HEAD_DIM)
    parts = [uq[:, :, :HEAD_DIM], _rope_pad(uq[:, :, HEAD_DIM:]), ukv[:, :, :HEAD_DIM], ukv[:, :, HEAD_DIM:]]
    return jnp.concatenate([p.transpose(1, 0, 2) for p in parts], axis=0)


def _unstack_mla(w):
    p = [w[i * HEADS:(i + 1) * HEADS].transpose(1, 0, 2) for i in range(4)]
    uq = jnp.concatenate([p[0], _rope_unpad(p[1])], axis=-1).reshape(LORA, HEADS * QK_DIM)
    ukv = jnp.concatenate([p[2], p[3]], axis=-1).reshape(LORA, HEADS * 2 * HEAD_DIM)
    return uq, ukv


def _rows8(rows):
    a = jnp.concatenate(rows, axis=0)
    return jnp.pad(a, ((0, 8 - a.shape[0]), (0, 0)))


def _qk_norm_rows(q_norm_w, k_norm_w):
    return _rows8([q_norm_w[:, :HEAD_DIM], _rope_pad(q_norm_w[:, HEAD_DIM:]), k_norm_w[:, :HEAD_DIM], _rope_pad(k_norm_w[:, HEAD_DIM:])])


def _rope_rows():
    inv_freq = ROPE_THETA ** (-jnp.arange(ROPE_HALF, dtype=F32) / ROPE_HALF)
    z = jnp.zeros((ROPE_HALF,), F32)
    freq = jnp.concatenate([inv_freq, z, inv_freq, z])
    sign = jnp.concatenate([-jnp.ones((ROPE_HALF,), F32), z, jnp.ones((ROPE_HALF,), F32), z])
    return _rows8([freq[None], sign[None]])


def _column_shards(a):
    return a.reshape(a.shape[0], 4, a.shape[1] // 4).transpose(1, 0, 2)


def _from_column_shards(a):
    return a.transpose(1, 0, 2).reshape(a.shape[1], 4 * a.shape[2])


_ANY = pl.BlockSpec(memory_space=pl.ANY)
_OTHER_CHIPS = ((1, 0), (0, 1), (1, 1))


def _here():
    return lax.axis_index("x"), lax.axis_index("y"), lax.axis_index("c")


def _flip(v, bit):
    return 1 - v if bit else v


def _remote(src, dst, send_sems, recv_sems, k, to):
    return pltpu.make_async_remote_copy(src_ref=src, dst_ref=dst, send_sem=send_sems.at[k], recv_sem=recv_sems.at[k],
                                        device_id=to, device_id_type=MESH)


def _half_of(ref, k, shape):
    r, c = shape
    if (r // 2) % 16 == 0:
        return ref.at[pl.ds(pl.multiple_of(k * (r // 2), 16), r // 2)]
    if (c // 2) % 128 == 0:
        return ref.at[:, pl.ds(pl.multiple_of(k * (c // 2), 128), c // 2)]
    return None


def _gather_copies(srcs, dsts, send_sems, recv_sems, local_sems):
    x, y, c = _here()
    slot, sibling, n = 2 * x + y, (x, y, 1 - c), len(srcs)
    starts, passes, waits = [], [], []
    for i, (src, dst) in enumerate(zip(srcs, dsts)):
        own = pltpu.make_async_copy(src, dst.at[slot], local_sems.at[i])
        starts.append(own.start)
        waits.append(own.wait)
        halves = _half_of(src, c, src.shape) is not None
        for j, (fx, fy) in enumerate(_OTHER_CHIPS):
            cx, cy = _flip(x, fx), _flip(y, fy)
            there = dst.at[2 * cx + cy]
            if halves:
                push = _remote(_half_of(src, c, src.shape), _half_of(dst.at[slot], c, src.shape), send_sems, recv_sems, 3 * i + j, (cx, cy, c))
                landed, other = _half_of(there, c, src.shape), _half_of(there, 1 - c, src.shape)
                onward = _remote(landed, landed, send_sems, recv_sems, 3 * n + 3 * i + j, sibling)
                passes += [_remote(landed, landed, send_sems, recv_sems, 3 * i + j, (cx, cy, c)).wait_recv, onward.start]
                waits += [_remote(other, other, send_sems, recv_sems, 3 * n + 3 * i + j, sibling).wait_recv, onward.wait_send]
            else:
                push = _remote(src, dst.at[slot], send_sems, recv_sems, 3 * i + j, (cx, cy, c))
                waits.append(_remote(there, there, send_sems, recv_sems, 3 * i + j, (cx, cy, c)).wait_recv)
            starts.append(push.start)
            waits.append(push.wait_send)
    return starts, passes, waits


def _gather_scratch(n):
    return [pltpu.SemaphoreType.DMA((6 * n,)), pltpu.SemaphoreType.DMA((6 * n,)), pltpu.SemaphoreType.DMA((n,))]


def _all_gather(shards, name):
    ns = len(shards)

    def body(*refs):
        starts, passes, waits = _gather_copies(refs[:ns], refs[ns:2 * ns], *refs[2 * ns:])
        for call in starts + passes + waits:
            call()

    return pl.pallas_call(
        body, in_specs=[_ANY] * ns, out_specs=[_ANY] * ns, out_shape=[_sds((4,) + s.shape, s.dtype) for s in shards],
        scratch_shapes=_gather_scratch(ns), name=name,
    )(*shards)


def _by_lanes(shape):
    return (shape[-2] // 2) % 16 != 0


def _scattered_shape(p):
    r, c = p.shape[1:]
    return _sds((8, r, c // 2) if _by_lanes(p.shape) else (8, r // 2, c), p.dtype)


def _scatter_copies(srcs, dsts, send_sems, recv_sems, local_sems, whole=0):
    x, y, c = _here()
    me = 4 * x + 2 * y + c
    starts, waits = [], []
    for i, (src, dst) in enumerate(zip(srcs, dsts)):
        def piece(px, py, pc, src=src, entire=i >= len(srcs) - whole):
            if entire:
                return src
            if _by_lanes(src.shape):
                half = src.shape[2] // 2
                return src.at[2 * px + py, :, pl.ds(pl.multiple_of(pc * half, 128), half)]
            half = src.shape[1] // 2
            return src.at[2 * px + py, pl.ds(pl.multiple_of(pc * half, 16), half)]

        own = pltpu.make_async_copy(piece(x, y, c), dst.at[me], local_sems.at[i])
        starts.append(own.start)
        waits.append(own.wait)
        for k in range(1, 8):
            px, py, pc = _flip(x, k & 4), _flip(y, k & 2), _flip(c, k & 1)
            push = _remote(piece(px, py, pc), dst.at[me], send_sems, recv_sems, 7 * i + k - 1, (px, py, pc))
            landed = dst.at[4 * px + 2 * py + pc]
            starts.append(push.start)
            waits += [_remote(landed, landed, send_sems, recv_sems, 7 * i + k - 1, (px, py, pc)).wait_recv, push.wait_send]
    return starts, waits


def _scatter_scratch(n):
    return [pltpu.SemaphoreType.DMA((7 * n,)), pltpu.SemaphoreType.DMA((7 * n,)), pltpu.SemaphoreType.DMA((n,))]


def _swapped_shape(half):
    r, c = half.shape
    return _sds((r, 2 * c) if _by_lanes((r, 2 * c)) else (2, r, c), half.dtype)


def _swap_copies(srcs, dsts, send_sems, recv_sems, local_sems):
    x, y, c = _here()
    sibling = (x, y, 1 - c)
    starts, waits = [], []
    for i, (src, dst) in enumerate(zip(srcs, dsts)):
        if len(dst.shape) == 2:
            lanes = src.shape[1]
            mine, other = (dst.at[:, pl.ds(pl.multiple_of(k * lanes, 128), lanes)] for k in (c, 1 - c))
        else:
            mine, other = dst.at[c], dst.at[1 - c]
        own = pltpu.make_async_copy(src, mine, local_sems.at[i])
        push = _remote(src, mine, send_sems, recv_sems, i, sibling)
        starts += [own.start, push.start]
        waits += [_remote(other, other, send_sems, recv_sems, i, sibling).wait_recv, push.wait_send, own.wait]
    return starts, waits


def _swap_scratch(n):
    return [pltpu.SemaphoreType.DMA((n,)), pltpu.SemaphoreType.DMA((n,)), pltpu.SemaphoreType.DMA((n,))]


def _exchange_halves(halves, wholes):
    ns, nw = len(halves), len(wholes)

    def body(*refs):
        srcs, dsts = refs[:ns + nw], refs[ns + nw:2 * (ns + nw)]
        sems = refs[2 * (ns + nw):]
        starts, waits = _swap_copies(srcs[:ns], dsts[:ns], *sems[:3])
        more = _scatter_copies(srcs[ns:], dsts[ns:], *sems[3:], whole=nw)
        for call in starts + more[0] + waits + more[1]:
            call()

    return pl.pallas_call(
        body, in_specs=[_ANY] * (ns + nw), out_specs=[_ANY] * (ns + nw),
        out_shape=[_swapped_shape(h) for h in halves] + [_sds((8,) + a.shape, a.dtype) for a in wholes],
        scratch_shapes=_swap_scratch(ns) + _scatter_scratch(nw), name="exchange_halves",
    )(*halves, *wholes)


def _row_tile(rows, row_bytes, budget):
    tr = rows
    while tr * row_bytes > budget and tr % 16 == 0:
        tr //= 2
    return tr


def _sum_slots(parts, name):
    _, rows, cols = parts.shape
    tr = _row_tile(rows, 8 * cols * 4, 2 * 1024 * 1024)

    def body(p_ref, o_ref):
        acc = p_ref[0].astype(F32)
        for d in range(1, 8):
            acc = acc + p_ref[d].astype(F32)
        o_ref[...] = acc

    return pl.pallas_call(
        body, grid=(rows // tr,), in_specs=[pl.BlockSpec((8, tr, cols), lambda i: (0, i, 0))],
        out_specs=pl.BlockSpec((tr, cols), lambda i: (i, 0)), out_shape=_sds((rows, cols), F32),
        compiler_params=_params(("parallel",)), name=name,
    )(parts)


def _adam_update(w, g, m, v):
    m = ADAM_B1 * m + (1.0 - ADAM_B1) * g
    v = ADAM_B2 * v + (1.0 - ADAM_B2) * jnp.square(g)
    m_hat = m / (1.0 - ADAM_B1 ** ADAM_STEP)
    v_hat = v / (1.0 - ADAM_B2 ** ADAM_STEP)
    return -ADAM_LR * (m_hat / (jnp.sqrt(v_hat) + ADAM_EPS) + ADAM_WD * w), m, v


SMALL_ROWS = {"attn_norm_w": 0, "mlp_norm_w": 1, "q_lat_norm_w": 2, "kv_lat_norm_w": 3, "q_norm_w": 4, "k_norm_w": 5,
              "mla_out_norm_w": 6, "gdn_norm_w": 10, "a_log": 11, "dt_bias": 12}
LOSS_ROW = 13
SMALL_SHAPE = (16, 1024)


def _pack_small_partials(d_attn_nw, d_mlp_nw, d_ln, d_qk_nw, d_mix_nw, d_scal, conv_parts, sq):
    D = d_attn_nw.shape[1]

    def body(an_ref, mn_ref, ln_ref, qk_ref, mix_ref, sc_ref, cq_ref, ck_ref, cv_ref, sq_ref, a_ref, c_ref):
        a_ref[...] = jnp.zeros_like(a_ref)
        a_ref[0:1, :D] = an_ref[...]
        a_ref[1:2, :D] = mn_ref[...]
        a_ref[2:4, :LORA] = ln_ref[...]
        for row, base in ((4, 0), (5, 2)):
            rope = qk_ref[base + 1:base + 2, :]
            a_ref[row:row + 1, :QK_DIM] = jnp.concatenate(
                [qk_ref[base:base + 1, :], rope[:, :ROPE_HALF], rope[:, 2 * ROPE_HALF:3 * ROPE_HALF]], axis=1)
        a_ref[6:6 + HEADS, :HEAD_DIM] = mix_ref[0:HEADS, :]
        a_ref[10:11, :HEAD_DIM] = mix_ref[HEADS:HEADS + 1, :]
        a_ref[11:13, :128] = sc_ref[0:2, :]
        a_ref[LOSS_ROW:LOSS_ROW + 1, :128] = jnp.zeros((1, 128), F32) + jnp.sum(sq_ref[:, 0:1, 0:1]) * (0.5 / D)
        c_ref[...] = jnp.concatenate([cq_ref[...], ck_ref[...], cv_ref[...]], axis=1)

    return pl.pallas_call(
        body, out_shape=[_sds(SMALL_SHAPE, F32), _sds((CONV_TAPS, GQKV_W), F32)], name="pack_small_partials",
    )(d_attn_nw, d_mlp_nw, d_ln, d_qk_nw, d_mix_nw, d_scal, *conv_parts, sq)


def _adamw_small(parts, conv_parts, w, m, v):
    names = tuple(SMALL_ROWS) + ("conv_w",)
    cols = w["conv_w"].shape[2]

    def body(*refs):
        p_ref, c_ref = refs[:2]
        n = len(names)
        w_refs, m_refs, v_refs = (dict(zip(names, refs[2 + k * n:2 + (k + 1) * n])) for k in range(3))
        loss_ref = refs[2 + 3 * n]
        out = [dict(zip(names, refs[3 + (3 + k) * n:3 + (4 + k) * n])) for k in range(4)]
        acc_ref, cacc_ref = refs[3 + 7 * n:]
        acc, cacc = p_ref[0], c_ref[0]
        for d in range(1, 8):
            acc, cacc = acc + p_ref[d], cacc + c_ref[d]
        acc_ref[...] = acc
        cacc_ref[...] = cacc
        loss_ref[...] = acc_ref[LOSS_ROW:LOSS_ROW + 1, 0:1]
        chip = 2 * lax.axis_index("x") + lax.axis_index("y")
        for name in names:
            shape = w_refs[name].shape
            if name == "conv_w":
                g = sum(jnp.where(chip == s, cacc_ref[:, s * cols:(s + 1) * cols], 0.0) for s in range(4))[None]
            else:
                row = SMALL_ROWS[name]
                g = acc_ref[row:row + math.prod(shape[:-1]), 0:shape[-1]].reshape(shape)
            delta, new_m, new_v = _adam_update(w_refs[name][...], g, m_refs[name][...], v_refs[name][...])
            for ref, val in zip((o[name] for o in out), (g, delta, new_m, new_v)):
                ref[...] = val

    ins = [x[n] for x in (w, m, v) for n in names]
    shapes = [_sds(w[n].shape, F32) for n in names]
    outs = pl.pallas_call(
        body, out_shape=[_sds((1, 1), F32)] + shapes * 4,
        scratch_shapes=[pltpu.VMEM(parts.shape[1:], F32), pltpu.VMEM(conv_parts.shape[1:], F32)], name="adamw_small",
    )(parts, conv_parts, *ins)
    n = len(names)
    return (outs[0],) + tuple(dict(zip(names, outs[1 + k * n:1 + (k + 1) * n])) for k in range(4))


def _adamw(w, g, m, v, name):
    rows, cols = w.shape[0], w.shape[-1]
    if w.ndim == 3:
        tr = max(d for d in range(1, rows + 1) if rows % d == 0 and d * 8 * cols * 4 * 14 <= VMEM_LIMIT // 2)
    else:
        tr = _row_tile(rows, 7 * cols * 4, 4 * 1024 * 1024)

    def body(w_ref, g_ref, m_ref, v_ref, d_ref, mo_ref, vo_ref):
        d_ref[...], mo_ref[...], vo_ref[...] = _adam_update(w_ref[...], g_ref[...], m_ref[...], v_ref[...])

    block = (tr,) + w.shape[1:]
    spec = pl.BlockSpec(block, lambda i: (i,) + (0,) * (len(block) - 1))
    return pl.pallas_call(
        body, grid=(rows // tr,), in_specs=[spec] * 4, out_specs=[spec] * 3, out_shape=[_sds(w.shape, F32)] * 3,
        compiler_params=_params(("parallel",)), name=name,
    )(w, g, m, v)


def kernel(x, positions, attn_norm_w, w_in, q_lat_norm_w, w_uq, kv_lat_norm_w, w_ukv, q_norm_w, k_norm_w, mla_out_norm_w, conv_w, a_log, dt_bias, gdn_norm_w, w_out, mlp_norm_w, w_up, w_down, loss_target, m_attn_norm_w, m_w_in, m_q_lat_norm_w, m_w_uq, m_kv_lat_norm_w, m_w_ukv, m_q_norm_w, m_k_norm_w, m_mla_out_norm_w, m_conv_w, m_a_log, m_dt_bias, m_gdn_norm_w, m_w_out, m_mlp_norm_w, m_w_up, m_w_down, v_attn_norm_w, v_w_in, v_q_lat_norm_w, v_w_uq, v_kv_lat_norm_w, v_w_ukv, v_q_norm_w, v_k_norm_w, v_mla_out_norm_w, v_conv_w, v_a_log, v_dt_bias, v_gdn_norm_w, v_w_out, v_mlp_norm_w, v_w_up, v_w_down):
    w = dict(zip(WEIGHTS, (attn_norm_w, w_in, q_lat_norm_w, w_uq, kv_lat_norm_w, w_ukv, q_norm_w, k_norm_w, mla_out_norm_w, conv_w,
                           a_log, dt_bias, gdn_norm_w, w_out, mlp_norm_w, w_up, w_down)))
    m = dict(zip(WEIGHTS, (m_attn_norm_w, m_w_in, m_q_lat_norm_w, m_w_uq, m_kv_lat_norm_w, m_w_ukv, m_q_norm_w, m_k_norm_w,
                           m_mla_out_norm_w, m_conv_w, m_a_log, m_dt_bias, m_gdn_norm_w, m_w_out, m_mlp_norm_w, m_w_up, m_w_down)))
    v = dict(zip(WEIGHTS, (v_attn_norm_w, v_w_in, v_q_lat_norm_w, v_w_uq, v_kv_lat_norm_w, v_w_ukv, v_q_norm_w, v_k_norm_w,
                           v_mla_out_norm_w, v_conv_w, v_a_log, v_dt_bias, v_gdn_norm_w, v_w_out, v_mlp_norm_w, v_w_up, v_w_down)))
    B, S, D = x.shape
    T = B * S
    x2, pos, target = x.reshape(T, D), positions.reshape(T, 1), loss_target.reshape(T, D)
    seq = lambda a: a.reshape(B, S, a.shape[-1])
    tok = lambda a: a.reshape(T, a.shape[-1])
    local = {n: w[n][0] for n in SHARDED}

    g_in, g_uq, g_ukv, g_conv = _all_gather([jnp.swapaxes(w_in, 1, 2)[0].astype(BF16), local["w_uq"].astype(BF16),
                                             local["w_ukv"].astype(BF16), local["conv_w"]], "gather_first_weights")
    w_in_p = _widen_w_in_t(g_in.reshape(-1, D))
    w_mla = _stack_mla(_from_column_shards(g_uq), _from_column_shards(g_ukv))
    conv_full = _from_column_shards(g_conv)
    ln_w = jnp.concatenate([q_lat_norm_w, kv_lat_norm_w], axis=0)
    qk_nw = _qk_norm_rows(q_norm_w, k_norm_w)
    rope_rows = _rope_rows()
    scal = _rows8([jnp.pad(a_log, ((0, 0), (0, 128 - HEADS))), jnp.pad(dt_bias, ((0, 0), (0, 128 - HEADS)))])
    mix_nw = _rows8([mla_out_norm_w[0], gdn_norm_w])

    xn, lat, gqkv, gz, gab = _in_proj_fwd(x2, attn_norm_w, w_in_p)
    q, k, v_att = _mla_pre_fwd(lat, pos, ln_w, w_mla, qk_nw, rope_rows)
    ao, lse, g_down = _attn_fwd(seq(q), seq(k), seq(v_att), [local["w_down"].astype(BF16)])
    gq, gk, gv = _gdn_pre_fwd(seq(gqkv), conv_full)
    go, states, powers, solutions, g_out, w_up_b = _gdn_chunk_fwd(gq, gk, gv, seq(gab), scal,
                                                                  [local["w_out"].astype(BF16), local["w_up"].astype(BF16)])
    w_out_b = g_out.reshape(-1, D)
    w_down_b = g_down.reshape(-1, D)
    mix, h2 = _mix_fwd(tok(ao), tok(go), gz, mix_nw, w_out_b, x2)
    hn, act, dy, sq = _mlp_fwd(h2, mlp_norm_w, w_up_b, w_down_b, target)

    dh, d_up, d_mlp_nw = _mlp_bwd(h2, mlp_norm_w, act, w_up_b, w_down_b, dy)
    p_down = _wgrad(act, dy, "wgrad_down").reshape(4, -1, D)
    p_up = _wgrad(hn, d_up, "wgrad_up", column_shards=4)
    d_ao, d_go, d_gz, d_mix_nw = _mix_bwd(tok(ao), tok(go), gz, mix_nw, w_out_b, dh)
    p_out = _wgrad(mix, dh, "wgrad_out").reshape(4, -1, D)
    d_gq, d_gk, d_gv, d_gab, d_scal, s_up, s_out = _gdn_chunk_bwd(gq, gk, gv, seq(gab), scal, states, powers, solutions, seq(d_go),
                                                                  [p_up, p_out])
    early = ("w_up", "w_out", "w_down")
    dxq, dxk, dxv, dcq, dck, dcv, g_up, g_out = _gdn_pre_bwd(seq(gqkv), conv_full, d_gq, d_gk, d_gv,
                                                             [_sum_slots(s_up, "sum_w_up"), _sum_slots(s_out, "sum_w_out")])
    dq, dk, dv, s_down = _attn_bwd(seq(q), seq(k), seq(v_att), ao, lse, seq(d_ao), [p_down])
    d_lat, d_ln, d_w_mla, d_qk_nw, g_down = _mla_pre_bwd(lat, pos, ln_w, w_mla, qk_nw, rope_rows, tok(dq), tok(dk), tok(dv),
                                                         [_sum_slots(s_down, "sum_w_down")])
    early_grads = [g_up, g_out, g_down]
    d_pieces = [d_lat, tok(dxq), tok(dxk), tok(dxv), d_gz, tok(d_gab)]
    p_in = _narrow_w_in_t(_wgrad_pieces(d_pieces, xn, "wgrad_in")).reshape(4, -1, D)
    p_uq, p_ukv = (_column_shards(a).astype(BF16) for a in _unstack_mla(d_w_mla))
    grad_x2, d_attn_nw, s_in, s_uq, s_ukv = _in_proj_bwd(d_pieces, w_in_p, x2, attn_norm_w, dh, [p_in, p_uq, p_ukv])
    small_buf, conv_buf = _pack_small_partials(d_attn_nw, d_mlp_nw, d_ln, d_qk_nw, d_mix_nw, d_scal, (dcq, dck, dcv), sq)

    late = ("w_in", "w_uq", "w_ukv")
    *late_grads, s_small, s_conv = _exchange_halves([_sum_slots(s, "sum_" + n) for n, s in zip(late, (s_in, s_uq, s_ukv))],
                                                    [small_buf, conv_buf])
    names = early + late
    grad = {n: g.reshape(-1, g.shape[-1]) for n, g in zip(names, list(early_grads) + list(late_grads))}

    loss, g_small, delta, new_m, new_v = _adamw_small(s_small, s_conv, w, m, v)
    grad.update(g_small)
    for n in names:
        if n == "w_in":
            stored = lambda a: jnp.transpose(a, (2, 0, 1))
            outs = _adamw(stored(w[n]), grad[n][:, None, :], stored(m[n]), stored(v[n]), "adamw_" + n)
            grad[n], delta[n], new_m[n], new_v[n] = (jnp.transpose(a, (1, 2, 0)) for a in (grad[n][:, None, :], *outs))
        else:
            delta[n], new_m[n], new_v[n] = _adamw(local[n], grad[n], m[n][0], v[n][0], "adamw_" + n)
    def in_order(d):
        return [d[n].reshape(w[n].shape) for n in WEIGHTS]

    return (loss.reshape(()), grad_x2.reshape(B, S, D), *in_order(grad), *in_order(delta), *in_order(new_m), *in_order(new_v))
```

```python
import functools
import math

import jax
import jax.numpy as jnp
from jax import lax
from jax.experimental import pallas as pl
from jax.experimental.pallas import tpu as pltpu

F32 = jnp.float32
BF16 = jnp.bfloat16
MESH = pl.DeviceIdType.MESH

EPS = 1e-6
HEADS = 4
HEAD_DIM = 128
ROPE_DIM = 64
ROPE_HALF = 32
QK_DIM = 192
QK_PAD = 256
LORA = 256
CHUNK = 64
SOLVE_POWERS = 5
CONV_TAPS = 4
ROPE_THETA = 10000.0
ATTN_SCALE = QK_DIM ** -0.5

LAT_W = 640
GQKV_W = 3 * HEADS * HEAD_DIM
GZ_W = HEADS * HEAD_DIM
GAB_W = 128
PROJ_SPLITS = ((0, LAT_W), (LAT_W, LAT_W + GQKV_W), (LAT_W + GQKV_W, LAT_W + GQKV_W + GZ_W),
               (LAT_W + GQKV_W + GZ_W, LAT_W + GQKV_W + GZ_W + GAB_W))
PROJ_W = PROJ_SPLITS[-1][1]

ADAM_LR = 0.001
ADAM_B1 = 0.9
ADAM_B2 = 0.999
ADAM_EPS = 1e-08
ADAM_WD = 0.01
ADAM_STEP = 10

TOKEN_TILE = 512
WGRAD_TOKEN_TILE = 1024
MLP_TOKEN_TILE = 512
FF_TILE = 1024
ATTN_TILE = 512
ATTN_HEADS_PER_STEP = 2
WGRAD_OUT_BYTES = 8 * 1024 * 1024
VMEM_LIMIT = 48 * 1024 * 1024

SHARDED = ("w_in", "w_uq", "w_ukv", "conv_w", "w_out", "w_up", "w_down")
WEIGHTS = ("attn_norm_w", "w_in", "q_lat_norm_w", "w_uq", "kv_lat_norm_w", "w_ukv", "q_norm_w", "k_norm_w", "mla_out_norm_w",
           "conv_w", "a_log", "dt_bias", "gdn_norm_w", "w_out", "mlp_norm_w", "w_up", "w_down")


def _sds(shape, dtype):
    return jax.ShapeDtypeStruct(shape, dtype)


def _params(semantics):
    return pltpu.CompilerParams(dimension_semantics=semantics, vmem_limit_bytes=VMEM_LIMIT)


def _block(n):
    for b in (512, 256, 128):
        if n % b == 0:
            return b
    return n


def _dg(a, b, ca, cb, prec):
    lead = a.ndim - 2
    batch = (tuple(range(lead)),) * 2
    return lax.dot_general(a, b, (((ca + lead,), (cb + lead,)), batch), precision=prec, preferred_element_type=F32)


def _split_bf16(a):
    hi = a.astype(BF16)
    return hi, (a - hi.astype(F32)).astype(BF16)


def _dot_bf16(a, b, ca, cb):
    return _dg(a.astype(BF16), b.astype(BF16), ca, cb, None)


def _dot_bf16x3(a, b, ca, cb):
    a_hi, a_lo = _split_bf16(a)
    b_hi, b_lo = _split_bf16(b)
    lead = a.ndim - 2
    return _dg(jnp.concatenate([a_hi, a_hi, a_lo], axis=ca + lead), jnp.concatenate([b_hi, b_lo, b_hi], axis=cb + lead), ca, cb, None)


def _matmul_family(dot):
    def nn_raw(a, b):
        return dot(a, b, 1, 0)

    def nt_raw(a, b):
        return dot(a, b, 1, 1)

    def tn_raw(a, b):
        return dot(a, b, 0, 0)

    @jax.custom_vjp
    def nn(a, b):
        return nn_raw(a, b)

    nn.defvjp(lambda a, b: (nn_raw(a, b), (a, b)), lambda r, g: (nt_raw(g, r[1]), tn_raw(r[0], g)))

    @jax.custom_vjp
    def nt(a, b):
        return nt_raw(a, b)

    nt.defvjp(lambda a, b: (nt_raw(a, b), (a, b)), lambda r, g: (nn_raw(g, r[1]), tn_raw(g, r[0])))

    @jax.custom_vjp
    def tn(a, b):
        return tn_raw(a, b)

    tn.defvjp(lambda a, b: (tn_raw(a, b), (a, b)), lambda r, g: (nt_raw(r[1], g), nn_raw(r[0], g)))
    return nn, nt, tn


_bf_nn, _bf_nt, _bf_tn = _matmul_family(_dot_bf16)
_hi_nn, _hi_nt, _hi_tn = _matmul_family(_dot_bf16x3)


def _lower_powers(lmat):
    powers = []
    while 2 ** (len(powers) + 1) < lmat.shape[-1]:
        powers.append(_dot_bf16x3(powers[-1] if powers else lmat, powers[-1] if powers else lmat, 1, 0))
    return powers


@jax.custom_vjp
def _unit_lower_solve(lmat, rhs):
    return _unit_lower_solve_fwd(lmat, rhs)[0]


def _unit_lower_solve_fwd(lmat, rhs):
    powers = _lower_powers(lmat)
    x = rhs - _dot_bf16x3(lmat, rhs, 1, 0)
    for p in powers:
        x = x + _dot_bf16x3(p, x, 1, 0)
    return x, (lmat, powers, x)


def _unit_lower_solve_bwd(res, g):
    lmat, powers, x = res
    y = g - _dot_bf16x3(lmat, g, 0, 0)
    for p in powers:
        y = y + _dot_bf16x3(p, y, 0, 0)
    return -_dot_bf16x3(y, x, 1, 1), y


_unit_lower_solve.defvjp(_unit_lower_solve_fwd, _unit_lower_solve_bwd)


@jax.custom_vjp
def _unit_lower_solve_kept(lmat, rhs, powers, x):
    return x


_unit_lower_solve_kept.defvjp(
    lambda lmat, rhs, powers, x: (x, (lmat, powers, x)),
    lambda res, g: _unit_lower_solve_bwd(res, g) + ([jnp.zeros_like(p) for p in res[1]], jnp.zeros_like(res[2])))


@jax.custom_vjp
def _lane_halves(x):
    n = x.shape[-1] // 2
    return x[..., :n], x[..., n:]


_lane_halves.defvjp(lambda x: (_lane_halves(x), None), lambda _, g: (jnp.concatenate(g, axis=-1),))


@jax.custom_vjp
def _row_halves(x):
    n = x.shape[-2] // 2
    return x[..., :n, :], x[..., n:, :]


_row_halves.defvjp(lambda x: (_row_halves(x), None), lambda _, g: (jnp.concatenate(g, axis=-2),))


@jax.custom_vjp
def _swap_halves(t):
    return pltpu.roll(t, 64, 1)


_swap_halves.defvjp(lambda t: (pltpu.roll(t, 64, 1), None), lambda _, g: (pltpu.roll(g, 64, 1),))


@functools.partial(jax.custom_vjp, nondiff_argnums=(2,))
def _shift_rows(x, keep, s):
    return pltpu.roll(x, s, 0) * keep


def _shift_rows_fwd(x, keep, s):
    return pltpu.roll(x, s, 0) * keep, keep


def _shift_rows_bwd(s, keep, g):
    return pltpu.roll(g * keep, keep.shape[0] - s, 0), jnp.zeros_like(keep)


_shift_rows.defvjp(_shift_rows_fwd, _shift_rows_bwd)


def _sigmoid(x):
    return 0.5 * jnp.tanh(0.5 * x) + 0.5


def _softplus(x):
    return jnp.maximum(x, 0.0) + jnp.log(1.0 + jnp.exp(jnp.minimum(x, -x)))


def _silu(x):
    return x * _sigmoid(x)


def _rms(x, w, n=None):
    n = x.shape[-1] if n is None else n
    r = lax.rsqrt(jnp.sum(x * x, axis=-1, keepdims=True) * (1.0 / n) + EPS)
    return x * r * w


def _rope(t, cos_f, sin_f):
    return t * cos_f + _swap_halves(t) * sin_f


def _rope_tables(pos_col, freq_row, sign_row):
    ang = pos_col.astype(F32) * freq_row
    return jnp.cos(ang), jnp.sin(ang) * sign_row


def _onehot_row(lane):
    return (lax.broadcasted_iota(jnp.int32, (1, 128), 1) == lane).astype(F32)


def _row_spec(tm, w):
    return pl.BlockSpec((tm, w), lambda i: (i, 0))


def _const_spec(shape):
    return pl.BlockSpec(shape, lambda *_: (0,) * len(shape))


def _in_proj_fwd(x2, w_an, w_in_p):
    T, D = x2.shape
    tm = min(TOKEN_TILE, T)

    def body(x_ref, wn_ref, w_ref, xn_ref, lat_ref, gqkv_ref, gz_ref, gab_ref):
        x = x_ref[...]
        r = lax.rsqrt(jnp.mean(x * x, axis=-1, keepdims=True) + EPS)
        xn = (x * r * wn_ref[...]).astype(BF16)
        xn_ref[...] = xn
        for ref, (a, b) in zip((lat_ref, gqkv_ref, gz_ref, gab_ref), PROJ_SPLITS):
            ref[...] = _dg(xn, w_ref[a:b, :], 1, 1, None)

    widths = [b - a for a, b in PROJ_SPLITS]
    return pl.pallas_call(
        body, grid=(T // tm,),
        in_specs=[_row_spec(tm, D), _const_spec((1, D)), _const_spec((PROJ_W, D))],
        out_specs=[_row_spec(tm, D)] + [_row_spec(tm, w) for w in widths],
        out_shape=[_sds((T, D), BF16)] + [_sds((T, w), F32) for w in widths],
        compiler_params=_params(("parallel",)), name="in_proj_fwd",
    )(x2, w_an, w_in_p)


def _in_proj_bwd(pieces, w_in_p, x2, w_an, dh, partials):
    T, D = x2.shape
    tm = min(TOKEN_TILE, T)
    widths = [p.shape[1] for p in pieces]
    starts = [sum(widths[:i]) for i in range(len(widths))]
    assert sum(widths) == PROJ_W
    npc, ns = len(pieces), len(partials)

    def body(*refs):
        piece_refs = refs[:npc]
        w_ref, x_ref, wn_ref, dh_ref = refs[npc:npc + 4]
        src_refs = refs[npc + 4:npc + 4 + ns]
        dx_ref, dwn_ref = refs[npc + 4 + ns:npc + 6 + ns]
        dst_refs = refs[npc + 6 + ns:npc + 6 + 2 * ns]
        sems = refs[npc + 6 + 2 * ns:]

        @pl.when(pl.program_id(0) == 0)
        def _():
            for start in _scatter_copies(src_refs, dst_refs, *sems)[0]:
                start()
            dwn_ref[...] = jnp.zeros_like(dwn_ref)

        dxn = jnp.zeros((tm, D), F32)
        for ref, a, width in zip(piece_refs, starts, widths):
            dxn += _dg(ref[...], w_ref[a:a + width, :], 1, 0, None)
        _, pull = jax.vjp(_rms, x_ref[...], wn_ref[...])
        dx, dwn = pull(dxn)
        dx_ref[...] = dx + dh_ref[...]
        dwn_ref[...] += dwn

        @pl.when(pl.program_id(0) == T // tm - 1)
        def _():
            for wait in _scatter_copies(src_refs, dst_refs, *sems)[1]:
                wait()

    return pl.pallas_call(
        body, grid=(T // tm,),
        in_specs=[_row_spec(tm, w) for w in widths] + [_const_spec((PROJ_W, D)), _row_spec(tm, D), _const_spec((1, D)),
                                                       _row_spec(tm, D)] + [_ANY] * ns,
        out_specs=[_row_spec(tm, D), _const_spec((1, D))] + [_ANY] * ns,
        out_shape=[_sds((T, D), F32), _sds((1, D), F32)] + [_scattered_shape(p) for p in partials],
        scratch_shapes=_scatter_scratch(ns),
        compiler_params=_params(("arbitrary",)), name="in_proj_bwd",
    )(*pieces, w_in_p, x2, w_an, dh, *partials)


def _wgrad_pieces(pieces, b, name):
    T, k2 = b.shape
    tt = min(WGRAD_TOKEN_TILE, T)
    widths = [p.shape[1] for p in pieces]
    starts = [sum(widths[:i]) for i in range(len(widths))]
    k1 = sum(widths)

    def body(*refs):
        piece_refs, (b_ref, o_ref, acc_ref) = refs[:len(pieces)], refs[len(pieces):]
        t = pl.program_id(0)

        @pl.when(t == 0)
        def _():
            acc_ref[...] = jnp.zeros_like(acc_ref)

        bt = b_ref[...].astype(BF16)
        for ref, r0, width in zip(piece_refs, starts, widths):
            acc_ref[r0:r0 + width, :] += jnp.dot(ref[...].T, bt, preferred_element_type=F32)

        @pl.when(t == T // tt - 1)
        def _():
            o_ref[...] = acc_ref[...].astype(o_ref.dtype)

    return pl.pallas_call(
        body, grid=(T // tt,),
        in_specs=[pl.BlockSpec((tt, w), lambda t: (t, 0)) for w in widths] + [pl.BlockSpec((tt, k2), lambda t: (t, 0))],
        out_specs=_const_spec((k1, k2)), out_shape=_sds((k1, k2), BF16), scratch_shapes=[pltpu.VMEM((k1, k2), F32)],
        compiler_params=_params(("arbitrary",)), name=name,
    )(*pieces, b)


def _wgrad(a, b, name, column_shards=1, out_dtype=BF16):
    T, k1 = a.shape
    k2 = b.shape[1]
    per_shard = k2 // column_shards
    tt = min(WGRAD_TOKEN_TILE, T)
    b1 = k1
    while b1 * k2 * 4 > WGRAD_OUT_BYTES and b1 % 256 == 0:
        b1 //= 2
    step = _block(per_shard)

    def body(a_ref, b_ref, o_ref, acc_ref):
        t = pl.program_id(1)

        @pl.when(t == 0)
        def _():
            acc_ref[...] = jnp.zeros_like(acc_ref)

        a_t = a_ref[...].astype(BF16).T
        for c0 in range(0, k2, step):
            part = jnp.dot(a_t, b_ref[:, c0:c0 + step].astype(BF16), preferred_element_type=F32)
            if column_shards == 1:
                acc_ref[:, c0:c0 + step] += part
            else:
                acc_ref[c0 // per_shard, :, c0 % per_shard:c0 % per_shard + step] += part

        @pl.when(t == T // tt - 1)
        def _():
            o_ref[...] = acc_ref[...].astype(o_ref.dtype)

    if column_shards == 1:
        block, out_spec, out_shape = (b1, k2), pl.BlockSpec((b1, k2), lambda i, t: (i, 0)), _sds((k1, k2), out_dtype)
    else:
        block = (column_shards, b1, per_shard)
        out_spec, out_shape = pl.BlockSpec(block, lambda i, t: (0, i, 0)), _sds((column_shards, k1, per_shard), out_dtype)
    return pl.pallas_call(
        body, grid=(k1 // b1, T // tt),
        in_specs=[pl.BlockSpec((tt, b1), lambda i, t: (t, i)), pl.BlockSpec((tt, k2), lambda i, t: (t, 0))],
        out_specs=out_spec, out_shape=out_shape, scratch_shapes=[pltpu.VMEM(block, F32)],
        compiler_params=_params(("parallel", "arbitrary")), name=name,
    )(a, b)


def _mla_pre_fn(q_lat, kv_lat, kpe, ln_q, ln_kv, w_list, qn_n, qn_p, kn_n, kn_p, cos_f, sin_f):
    qn = _rms(q_lat, ln_q)
    kvn = _rms(kv_lat, ln_kv)
    kp = _rope(_rms(kpe, kn_p, ROPE_DIM), cos_f, sin_f)
    outs = []
    for h in range(HEADS):
        outs.append(_rms(_bf_nn(qn, w_list[h]), qn_n))
        outs.append(_rope(_rms(_bf_nn(qn, w_list[HEADS + h]), qn_p, ROPE_DIM), cos_f, sin_f))
        outs.append(_rms(_bf_nn(kvn, w_list[2 * HEADS + h]), kn_n))
        outs.append(_bf_nn(kvn, w_list[3 * HEADS + h]))
    return tuple(outs) + (kp,)


def _mla_pre_operands(lat_ref, pos_ref, ln_ref, w_ref, nw_ref, rope_ref):
    cos_f, sin_f = _rope_tables(pos_ref[...], rope_ref[0:1, :], rope_ref[1:2, :])
    diff = (lat_ref[:, 0:LORA], lat_ref[:, LORA:2 * LORA], lat_ref[:, 2 * LORA:LAT_W], ln_ref[0:1, :], ln_ref[1:2, :],
            [w_ref[i].astype(F32) for i in range(4 * HEADS)], nw_ref[0:1, :], nw_ref[1:2, :], nw_ref[2:3, :], nw_ref[3:4, :])
    return diff, cos_f, sin_f


def _mla_pre_fwd(lat, pos, ln_w, w_mla, nw, rope_rows):
    T = lat.shape[0]
    tm = min(TOKEN_TILE, T)

    def body(lat_ref, pos_ref, ln_ref, w_ref, nw_ref, rope_ref, q_ref, k_ref, v_ref):
        diff, cos_f, sin_f = _mla_pre_operands(lat_ref, pos_ref, ln_ref, w_ref, nw_ref, rope_ref)
        outs = _mla_pre_fn(*diff, cos_f, sin_f)
        kp = outs[-1].astype(BF16)
        for h in range(HEADS):
            q_n, q_p, k_n, v = outs[4 * h:4 * h + 4]
            q_ref[:, h * QK_PAD:h * QK_PAD + HEAD_DIM] = q_n.astype(BF16)
            q_ref[:, h * QK_PAD + HEAD_DIM:(h + 1) * QK_PAD] = q_p.astype(BF16)
            k_ref[:, h * QK_PAD:h * QK_PAD + HEAD_DIM] = k_n.astype(BF16)
            k_ref[:, h * QK_PAD + HEAD_DIM:(h + 1) * QK_PAD] = kp
            v_ref[:, h * HEAD_DIM:(h + 1) * HEAD_DIM] = v.astype(BF16)

    return pl.pallas_call(
        body, grid=(T // tm,),
        in_specs=[_row_spec(tm, LAT_W), _row_spec(tm, 1), _const_spec((2, LORA)), _const_spec((4 * HEADS, LORA, 128)),
                  _const_spec((8, 128)), _const_spec((8, 128))],
        out_specs=[_row_spec(tm, HEADS * QK_PAD), _row_spec(tm, HEADS * QK_PAD), _row_spec(tm, HEADS * HEAD_DIM)],
        out_shape=[_sds((T, HEADS * QK_PAD), BF16), _sds((T, HEADS * QK_PAD), BF16), _sds((T, HEADS * HEAD_DIM), BF16)],
        compiler_params=_params(("parallel",)), name="mla_pre_fwd",
    )(lat, pos, ln_w, w_mla, nw, rope_rows)


def _mla_pre_bwd(lat, pos, ln_w, w_mla, nw, rope_rows, dq, dk, dv, halves):
    T = lat.shape[0]
    tm = min(TOKEN_TILE, T)
    ns = len(halves)

    def body(*refs):
        lat_ref, pos_ref, ln_ref, w_ref, nw_ref, rope_ref, dq_ref, dk_ref, dv_ref = refs[:9]
        src_refs = refs[9:9 + ns]
        dlat_ref, dln_ref, dw_ref, dnw_ref = refs[9 + ns:13 + ns]
        dst_refs = refs[13 + ns:13 + 2 * ns]
        sems = refs[13 + 2 * ns:]

        @pl.when(pl.program_id(0) == 0)
        def _():
            for start in _swap_copies(src_refs, dst_refs, *sems)[0]:
                start()
            dln_ref[...] = jnp.zeros_like(dln_ref)
            dw_ref[...] = jnp.zeros_like(dw_ref)
            dnw_ref[...] = jnp.zeros_like(dnw_ref)

        diff, cos_f, sin_f = _mla_pre_operands(lat_ref, pos_ref, ln_ref, w_ref, nw_ref, rope_ref)
        _, pull = jax.vjp(lambda *a: _mla_pre_fn(*a, cos_f, sin_f), *diff)
        cts = []
        d_kp = jnp.zeros((tm, 128), F32)
        for h in range(HEADS):
            cts.append(dq_ref[:, h * QK_PAD:h * QK_PAD + HEAD_DIM])
            cts.append(dq_ref[:, h * QK_PAD + HEAD_DIM:(h + 1) * QK_PAD])
            cts.append(dk_ref[:, h * QK_PAD:h * QK_PAD + HEAD_DIM])
            cts.append(dv_ref[:, h * HEAD_DIM:(h + 1) * HEAD_DIM])
            d_kp += dk_ref[:, h * QK_PAD + HEAD_DIM:(h + 1) * QK_PAD]
        d_ql, d_kvl, d_kpe, d_lnq, d_lnkv, d_w, d_qn_n, d_qn_p, d_kn_n, d_kn_p = pull(tuple(cts) + (d_kp,))
        dlat_ref[:, 0:LORA] = d_ql.astype(BF16)
        dlat_ref[:, LORA:2 * LORA] = d_kvl.astype(BF16)
        dlat_ref[:, 2 * LORA:LAT_W] = d_kpe.astype(BF16)
        dln_ref[0:1, :] += d_lnq
        dln_ref[1:2, :] += d_lnkv
        for i in range(4 * HEADS):
            dw_ref[i] += d_w[i]
        for i, d in enumerate((d_qn_n, d_qn_p, d_kn_n, d_kn_p)):
            dnw_ref[i:i + 1, :] += d

        @pl.when(pl.program_id(0) == T // tm - 1)
        def _():
            for wait in _swap_copies(src_refs, dst_refs, *sems)[1]:
                wait()

    return pl.pallas_call(
        body, grid=(T // tm,),
        in_specs=[_row_spec(tm, LAT_W), _row_spec(tm, 1), _const_spec((2, LORA)), _const_spec((4 * HEADS, LORA, 128)),
                  _const_spec((8, 128)), _const_spec((8, 128)),
                  _row_spec(tm, HEADS * QK_PAD), _row_spec(tm, HEADS * QK_PAD), _row_spec(tm, HEADS * HEAD_DIM)] + [_ANY] * ns,
        out_specs=[_row_spec(tm, LAT_W), _const_spec((2, LORA)), _const_spec((4 * HEADS, LORA, 128)), _const_spec((8, 128))]
                  + [_ANY] * ns,
        out_shape=[_sds((T, LAT_W), BF16), _sds((2, LORA), F32), _sds((4 * HEADS, LORA, 128), F32), _sds((8, 128), F32)]
                  + [_swapped_shape(h) for h in halves],
        scratch_shapes=_swap_scratch(ns),
        compiler_params=_params(("arbitrary",)), name="mla_pre_bwd",
    )(lat, pos, ln_w, w_mla, nw, rope_rows, dq, dk, dv, *halves)


def _causal_mask(i, j, tq, tk):
    row = i * tq + lax.broadcasted_iota(jnp.int32, (tq, tk), 0)
    col = j * tk + lax.broadcasted_iota(jnp.int32, (tq, tk), 1)
    return col <= row


def _attn_fwd(q, k, v, shards):
    B, S, _ = q.shape
    t = min(ATTN_TILE, S)
    nq = S // t
    ns = len(shards)

    hp = ATTN_HEADS_PER_STEP
    qk = lambda h: slice(h * QK_PAD, (h + 1) * QK_PAD)
    vd = lambda h: slice(h * HEAD_DIM, (h + 1) * HEAD_DIM)

    def body(*refs):
        q_ref, k_ref, v_ref = refs[:3]
        src_refs = refs[3:3 + ns]
        o_ref, lse_ref = refs[3 + ns:5 + ns]
        dst_refs = refs[5 + ns:5 + 2 * ns]
        sems = refs[5 + 2 * ns:]
        b, g, i = pl.program_id(0), pl.program_id(1), pl.program_id(2)
        qb = [q_ref[0, :, qk(h)] for h in range(hp)]

        step_no = (b * (HEADS // hp) + g) * nq + i
        for phase, at in enumerate((0, (3 * B * (HEADS // hp) * nq) // 4)):
            @pl.when(step_no == at)
            def _(phase=phase):
                for call in _gather_copies(src_refs, dst_refs, *sems)[phase]:
                    call()

        def step(j, carry, diagonal):
            rows = pl.ds(pl.multiple_of(j * t, t), t)
            s = [_dg(qb[h], k_ref[0, rows, qk(h)], 1, 1, None) * ATTN_SCALE for h in range(hp)]
            if diagonal:
                keep = _causal_mask(0, 0, t, t)
                s = [jnp.where(keep, x, -1e30) for x in s]
            m_new = [jnp.maximum(carry[h][0], jnp.max(s[h], axis=-1, keepdims=True)) for h in range(hp)]
            p = [jnp.exp(s[h] - m_new[h]) for h in range(hp)]
            alpha = [jnp.exp(carry[h][0] - m_new[h]) for h in range(hp)]
            l = [alpha[h] * carry[h][1] + jnp.sum(p[h], axis=-1, keepdims=True) for h in range(hp)]
            pv = [jnp.dot(p[h].astype(BF16), v_ref[0, rows, vd(h)], preferred_element_type=F32) for h in range(hp)]
            return tuple((m_new[h], l[h], alpha[h] * carry[h][2] + pv[h]) for h in range(hp))

        init = tuple((jnp.full((t, 1), -1e30, F32), jnp.zeros((t, 1), F32), jnp.zeros((t, HEAD_DIM), F32)) for _ in range(hp))
        below = lax.fori_loop(0, i, lambda j, carry: step(j, carry, False), init)
        for h, (m, l, acc) in enumerate(step(i, below, True)):
            o_ref[0, :, vd(h)] = acc / l
            lse_ref[0, h, 0] = (m + jnp.log(l)).T

        @pl.when((b == B - 1) & (g == HEADS // hp - 1) & (i == nq - 1))
        def _():
            for wait in _gather_copies(src_refs, dst_refs, *sems)[2]:
                wait()

    return pl.pallas_call(
        body, grid=(B, HEADS // hp, nq),
        in_specs=[pl.BlockSpec((1, t, hp * QK_PAD), lambda b, g, i: (b, i, g)),
                  pl.BlockSpec((1, S, hp * QK_PAD), lambda b, g, i: (b, 0, g)),
                  pl.BlockSpec((1, S, hp * HEAD_DIM), lambda b, g, i: (b, 0, g))] + [_ANY] * ns,
        out_specs=[pl.BlockSpec((1, t, hp * HEAD_DIM), lambda b, g, i: (b, i, g)),
                   pl.BlockSpec((1, hp, 1, 1, t), lambda b, g, i: (b, g, i, 0, 0))] + [_ANY] * ns,
        out_shape=[_sds((B, S, HEADS * HEAD_DIM), F32), _sds((B, HEADS, nq, 1, t), F32)] + [_sds((4,) + s.shape, s.dtype) for s in shards],
        scratch_shapes=_gather_scratch(ns),
        compiler_params=_params(("arbitrary", "arbitrary", "arbitrary")), name="attn_fwd",
    )(q, k, v, *shards)


def _attn_bwd(q, k, v, o, lse, do, partials):
    B, S, _ = q.shape
    t = min(ATTN_TILE, S)
    nq = S // t
    ns = len(partials)

    hp = ATTN_HEADS_PER_STEP
    qk = lambda h: slice(h * QK_PAD, (h + 1) * QK_PAD)
    vd = lambda h: slice(h * HEAD_DIM, (h + 1) * HEAD_DIM)
    heads = range(hp)

    def body(*refs):
        q_ref, k_ref, v_ref, o_ref, lse_ref, do_ref = refs[:6]
        src_refs = refs[6:6 + ns]
        dq_ref, dk_ref, dv_ref = refs[6 + ns:9 + ns]
        dst_refs = refs[9 + ns:9 + 2 * ns]
        dsum_ref, send_sems, recv_sems, local_sems = refs[9 + 2 * ns:]
        b, g, j = pl.program_id(0), pl.program_id(1), pl.program_id(2)

        @pl.when((b == 0) & (g == 0) & (j == 0))
        def _():
            for start in _scatter_copies(src_refs, dst_refs, send_sems, recv_sems, local_sems)[0]:
                start()

        @pl.when(j == 0)
        def _():
            dq_ref[...] = jnp.zeros_like(dq_ref)
            for h in heads:
                for blk in range(nq):
                    rows = slice(blk * t, (blk + 1) * t)
                    dsum_ref[h, blk] = jnp.sum(do_ref[0, rows, vd(h)] * o_ref[0, rows, vd(h)], axis=-1, keepdims=True).T

        kb = [k_ref[0, :, qk(h)] for h in heads]
        vb = [v_ref[0, :, vd(h)] for h in heads]

        def step(i, carry, diagonal):
            rows = pl.ds(pl.multiple_of(i * t, t), t)
            qb = [q_ref[0, rows, qk(h)] for h in heads]
            dob = [do_ref[0, rows, vd(h)].astype(BF16) for h in heads]
            s = [_dg(kb[h], qb[h], 1, 1, None) * ATTN_SCALE for h in heads]
            p = [jnp.exp(s[h] - lse_ref[0, h, i]) for h in heads]
            if diagonal:
                key = lax.broadcasted_iota(jnp.int32, (t, t), 0)
                query = lax.broadcasted_iota(jnp.int32, (t, t), 1)
                p = [jnp.where(key <= query, x, 0.0) for x in p]
            dp = [_dg(vb[h], dob[h], 1, 1, None) for h in heads]
            dv = [carry[h][1] + jnp.dot(p[h].astype(BF16), dob[h], preferred_element_type=F32) for h in heads]
            ds = [(p[h] * (dp[h] - dsum_ref[h, i]) * ATTN_SCALE).astype(BF16) for h in heads]
            for h in heads:
                dq_ref[0, rows, qk(h)] += _dg(ds[h], kb[h], 0, 0, None)
            return tuple((carry[h][0] + jnp.dot(ds[h], qb[h], preferred_element_type=F32), dv[h]) for h in heads)

        zeros = tuple((jnp.zeros((t, QK_PAD), F32), jnp.zeros((t, HEAD_DIM), F32)) for _ in heads)
        on_diagonal = step(j, zeros, True)
        done = lax.fori_loop(j + 1, nq, lambda i, carry: step(i, carry, False), on_diagonal)
        for h, (dk, dv) in enumerate(done):
            dk_ref[0, :, qk(h)] = dk
            dv_ref[0, :, vd(h)] = dv

        @pl.when((b == B - 1) & (g == HEADS // hp - 1) & (j == nq - 1))
        def _():
            for wait in _scatter_copies(src_refs, dst_refs, send_sems, recv_sems, local_sems)[1]:
                wait()

    return pl.pallas_call(
        body, grid=(B, HEADS // hp, nq),
        in_specs=[pl.BlockSpec((1, S, hp * QK_PAD), lambda b, g, j: (b, 0, g)),
                  pl.BlockSpec((1, t, hp * QK_PAD), lambda b, g, j: (b, j, g)),
                  pl.BlockSpec((1, t, hp * HEAD_DIM), lambda b, g, j: (b, j, g)),
                  pl.BlockSpec((1, S, hp * HEAD_DIM), lambda b, g, j: (b, 0, g)),
                  pl.BlockSpec((1, hp, nq, 1, t), lambda b, g, j: (b, g, 0, 0, 0)),
                  pl.BlockSpec((1, S, hp * HEAD_DIM), lambda b, g, j: (b, 0, g))] + [_ANY] * ns,
        out_specs=[pl.BlockSpec((1, S, hp * QK_PAD), lambda b, g, j: (b, 0, g)),
                   pl.BlockSpec((1, t, hp * QK_PAD), lambda b, g, j: (b, j, g)),
                   pl.BlockSpec((1, t, hp * HEAD_DIM), lambda b, g, j: (b, j, g))] + [_ANY] * ns,
        out_shape=[_sds((B, S, HEADS * QK_PAD), F32), _sds((B, S, HEADS * QK_PAD), F32), _sds((B, S, HEADS * HEAD_DIM), F32)]
                  + [_scattered_shape(p) for p in partials],
        scratch_shapes=[pltpu.VMEM((hp, nq, 1, t), F32)] + _scatter_scratch(ns),
        compiler_params=_params(("arbitrary", "arbitrary", "arbitrary")), name="attn_bwd",
    )(q, k, v, o, lse, do, *partials)


def _gdn_pre_fn(xq, xk, xv, wq, wk, wv, keeps):
    def conv_silu(x, w):
        acc = x * w[3]
        for s in (1, 2, 3):
            acc = acc + _shift_rows(x, keeps[s - 1], s) * w[3 - s]
        return _silu(acc)

    def l2(x):
        return x * lax.rsqrt(jnp.sum(x * x, axis=-1, keepdims=True) + EPS)

    return l2(conv_silu(xq, wq)) * (HEAD_DIM ** -0.5), l2(conv_silu(xk, wk)), conv_silu(xv, wv)


def _gdn_pre_specs(S):
    x_specs = [pl.BlockSpec((1, S, HEAD_DIM), lambda h, b, g=g: (b, 0, g * HEADS + h)) for g in range(3)]
    w_specs = [pl.BlockSpec((CONV_TAPS, HEAD_DIM), lambda h, b, g=g: (0, g * HEADS + h)) for g in range(3)]
    out_spec = pl.BlockSpec((1, S, HEAD_DIM), lambda h, b: (b, 0, h))
    return x_specs, w_specs, out_spec


def _row_keeps(S):
    t = lax.broadcasted_iota(jnp.int32, (S, HEAD_DIM), 0)
    return [(t >= s).astype(F32) for s in (1, 2, 3)]


def _gdn_pre_fwd(gqkv, conv_w):
    B, S, _ = gqkv.shape
    x_specs, w_specs, out_spec = _gdn_pre_specs(S)

    def body(xq_ref, xk_ref, xv_ref, wq_ref, wk_ref, wv_ref, q_ref, k_ref, v_ref):
        taps = [[w[i:i + 1, :] for i in range(CONV_TAPS)] for w in (wq_ref, wk_ref, wv_ref)]
        q, k, v = _gdn_pre_fn(xq_ref[0], xk_ref[0], xv_ref[0], *taps, _row_keeps(S))
        q_ref[0], k_ref[0], v_ref[0] = q, k, v

    return pl.pallas_call(
        body, grid=(HEADS, B), in_specs=x_specs + w_specs, out_specs=[out_spec] * 3,
        out_shape=[_sds((B, S, HEADS * HEAD_DIM), F32)] * 3,
        compiler_params=_params(("parallel", "parallel")), name="gdn_pre_fwd",
    )(gqkv, gqkv, gqkv, conv_w, conv_w, conv_w)


def _gdn_pre_bwd(gqkv, conv_w, dq, dk, dv, halves):
    B, S, _ = gqkv.shape
    x_specs, w_specs, out_spec = _gdn_pre_specs(S)
    dw_spec = pl.BlockSpec((CONV_TAPS, HEAD_DIM), lambda h, b: (0, h))
    ns = len(halves)

    def body(*refs):
        xq_ref, xk_ref, xv_ref, wq_ref, wk_ref, wv_ref, dq_ref, dk_ref, dv_ref = refs[:9]
        src_refs = refs[9:9 + ns]
        dxq_ref, dxk_ref, dxv_ref, dwq_ref, dwk_ref, dwv_ref = refs[9 + ns:15 + ns]
        dst_refs = refs[15 + ns:15 + 2 * ns]
        sems = refs[15 + 2 * ns:]
        first = (pl.program_id(0) == 0) & (pl.program_id(1) == 0)
        last = (pl.program_id(0) == HEADS - 1) & (pl.program_id(1) == B - 1)

        @pl.when(first)
        def _():
            for start in _swap_copies(src_refs, dst_refs, *sems)[0]:
                start()

        @pl.when(pl.program_id(1) == 0)
        def _():
            for r in (dwq_ref, dwk_ref, dwv_ref):
                r[...] = jnp.zeros_like(r)

        taps = [[w[i:i + 1, :] for i in range(CONV_TAPS)] for w in (wq_ref, wk_ref, wv_ref)]
        keeps = _row_keeps(S)
        _, pull = jax.vjp(lambda *a: _gdn_pre_fn(*a, keeps), xq_ref[0], xk_ref[0], xv_ref[0], *taps)
        dxq, dxk, dxv, dwq, dwk, dwv = pull((dq_ref[0], dk_ref[0], dv_ref[0]))
        dxq_ref[0], dxk_ref[0], dxv_ref[0] = dxq.astype(BF16), dxk.astype(BF16), dxv.astype(BF16)
        for ref, dw in ((dwq_ref, dwq), (dwk_ref, dwk), (dwv_ref, dwv)):
            for i in range(CONV_TAPS):
                ref[i:i + 1, :] += dw[i]

        @pl.when(last)
        def _():
            for wait in _swap_copies(src_refs, dst_refs, *sems)[1]:
                wait()

    hw = HEADS * HEAD_DIM
    return pl.pallas_call(
        body, grid=(HEADS, B), in_specs=x_specs + w_specs + [out_spec] * 3 + [_ANY] * ns,
        out_specs=[out_spec] * 3 + [dw_spec] * 3 + [_ANY] * ns,
        out_shape=[_sds((B, S, hw), BF16)] * 3 + [_sds((CONV_TAPS, hw), F32)] * 3 + [_swapped_shape(h) for h in halves],
        scratch_shapes=_swap_scratch(ns),
        compiler_params=_params(("arbitrary", "arbitrary")), name="gdn_pre_bwd",
    )(gqkv, gqkv, gqkv, conv_w, conv_w, conv_w, dq, dk, dv, *halves)


def _chunk_masks():
    i = lax.broadcasted_iota(jnp.int32, (CHUNK, CHUNK), 0)
    j = lax.broadcasted_iota(jnp.int32, (CHUNK, CHUNK), 1)
    lower, after = (j <= i).astype(F32), (j > i).astype(F32)
    return {"le": lower, "le_gt": jnp.concatenate([lower, after], axis=0), "strict": (j < i).astype(F32)}


def _gdn_chunk_fn(groups, masks, solve=_unit_lower_solve):
    lane = lax.broadcasted_iota(jnp.int32, (groups, 1, 128), 2)
    head = lax.broadcasted_iota(jnp.int32, (groups, 1, 128), 0) % HEADS
    pick_a, pick_b = (lane == head).astype(F32), (lane == head + HEADS).astype(F32)
    lower, lower_after, strict = (jnp.broadcast_to(masks[n], (groups,) + masks[n].shape) for n in ("le", "le_gt", "strict"))
    ones_row = jnp.ones((1, 1, HEAD_DIM), F32)

    def f(q, k, v, gab, a_row, dt_row, state):
        ga = jnp.sum(gab * pick_a, axis=2, keepdims=True)
        gb = jnp.sum(gab * pick_b, axis=2, keepdims=True)
        a_log = jnp.sum(a_row * pick_a, axis=2, keepdims=True)
        dt_bias = jnp.sum(dt_row * pick_a, axis=2, keepdims=True)
        beta = _sigmoid(gb)
        g = -jnp.exp(a_log) * _softplus(ga + dt_bias)
        g_wide = g * ones_row
        cum, rest = _row_halves(_hi_nn(lower_after, g_wide))
        total = jnp.sum(g_wide, axis=1, keepdims=True)
        diff = _hi_nn(lower, g * strict)
        decay = lower * jnp.exp(diff)
        e_cum = jnp.exp(cum)
        kk, qk = _row_halves(_bf_nt(jnp.concatenate([k, q], axis=1), k))
        lmat = strict * (beta * kk * decay)
        u, w = _lane_halves(solve(lmat, jnp.concatenate([v * beta, k * (beta * e_cum)], axis=2)))
        w_state, q_state = _row_halves(_bf_nn(jnp.concatenate([w, q * e_cum], axis=1), state))
        v_new = u - w_state
        o = q_state + _bf_nn(qk * decay, v_new)
        new_state = state * jnp.exp(total) + _bf_tn(k * jnp.exp(rest), v_new)
        return o, new_state

    return f


def _gdn_chunk_fwd(q, k, v, gab, scal, shards):
    B, S, W = q.shape
    N = S // CHUNK
    ns = len(shards)

    def body(*refs):
        q_ref, k_ref, v_ref, gab_ref, sc_ref = refs[:5]
        src_refs = refs[5:5 + ns]
        o_ref, st_ref, pw_ref, sol_ref = refs[5 + ns:9 + ns]
        dst_refs = refs[9 + ns:9 + 2 * ns]
        state_ref, send_sems, recv_sems, local_sems = refs[9 + 2 * ns:]
        n = pl.program_id(0)
        kept = {}

        @pl.when(n == 0)
        def _():
            for start in _gather_copies(src_refs, dst_refs, send_sems, recv_sems, local_sems)[0]:
                start()
            state_ref[...] = jnp.zeros_like(state_ref)

        @pl.when(n == (2 * N) // 3)
        def _():
            for pass_on in _gather_copies(src_refs, dst_refs, send_sems, recv_sems, local_sems)[1]:
                pass_on()

        groups = [(b, h) for b in range(B) for h in range(HEADS)]
        gather = lambda ref: jnp.stack([ref[b, :, h * HEAD_DIM:(h + 1) * HEAD_DIM] for b, h in groups])
        state = state_ref[...]
        for i, (b, h) in enumerate(groups):
            st_ref[b, 0, h] = state[i]
        def solve_and_keep(lmat, rhs):
            kept["x"], (_, kept["powers"], _) = _unit_lower_solve_fwd(lmat, rhs)
            return kept["x"]

        o, new_state = _gdn_chunk_fn(len(groups), _chunk_masks(), solve_and_keep)(
            gather(q_ref), gather(k_ref), gather(v_ref), jnp.stack([gab_ref[b] for b, _ in groups]), sc_ref[0:1, :], sc_ref[1:2, :], state)
        for i, (b, h) in enumerate(groups):
            o_ref[b, :, h * HEAD_DIM:(h + 1) * HEAD_DIM] = o[i]
            sol_ref[b, 0, h] = kept["x"][i]
            for p, power in enumerate(kept["powers"]):
                pw_ref[b, 0, h, p] = power[i]
        state_ref[...] = new_state

        @pl.when(n == N - 1)
        def _():
            for wait in _gather_copies(src_refs, dst_refs, send_sems, recv_sems, local_sems)[2]:
                wait()

    seq = pl.BlockSpec((B, CHUNK, W), lambda n: (0, n, 0))
    return pl.pallas_call(
        body, grid=(N,),
        in_specs=[seq, seq, seq, pl.BlockSpec((B, CHUNK, GAB_W), lambda n: (0, n, 0)), _const_spec((8, 128))] + [_ANY] * ns,
        out_specs=[seq, pl.BlockSpec((B, 1, HEADS, HEAD_DIM, HEAD_DIM), lambda n: (0, n, 0, 0, 0)),
                   pl.BlockSpec((B, 1, HEADS, SOLVE_POWERS, CHUNK, CHUNK), lambda n: (0, n, 0, 0, 0, 0)),
                   pl.BlockSpec((B, 1, HEADS, CHUNK, 2 * HEAD_DIM), lambda n: (0, n, 0, 0, 0))] + [_ANY] * ns,
        out_shape=[_sds((B, S, W), F32), _sds((B, N, HEADS, HEAD_DIM, HEAD_DIM), F32),
                   _sds((B, N, HEADS, SOLVE_POWERS, CHUNK, CHUNK), F32), _sds((B, N, HEADS, CHUNK, 2 * HEAD_DIM), F32)]
                  + [_sds((4,) + s.shape, s.dtype) for s in shards],
        scratch_shapes=[pltpu.VMEM((B * HEADS, HEAD_DIM, HEAD_DIM), F32)] + _gather_scratch(ns),
        compiler_params=_params(("arbitrary",)), name="gdn_chunk_fwd",
    )(q, k, v, gab, scal, *shards)


def _gdn_chunk_bwd(q, k, v, gab, scal, states, powers, solutions, do, partials):
    B, S, W = q.shape
    N = S // CHUNK
    ns = len(partials)

    def body(*refs):
        q_ref, k_ref, v_ref, gab_ref, sc_ref, st_ref, pw_ref, sol_ref, do_ref = refs[:9]
        src_refs = refs[9:9 + ns]
        dq_ref, dk_ref, dv_ref, dgab_ref, dsc_ref = refs[9 + ns:14 + ns]
        dst_refs = refs[14 + ns:14 + 2 * ns]
        dstate_ref, send_sems, recv_sems, local_sems = refs[14 + 2 * ns:]
        n = pl.program_id(0)

        @pl.when(n == 0)
        def _():
            for start in _scatter_copies(src_refs, dst_refs, send_sems, recv_sems, local_sems)[0]:
                start()
            dstate_ref[...] = jnp.zeros_like(dstate_ref)
            dsc_ref[...] = jnp.zeros_like(dsc_ref)

        groups = [(b, h) for b in range(B) for h in range(HEADS)]
        gather = lambda ref: jnp.stack([ref[b, :, h * HEAD_DIM:(h + 1) * HEAD_DIM] for b, h in groups])
        kept_powers = [jnp.stack([pw_ref[b, 0, h, p] for b, h in groups]) for p in range(SOLVE_POWERS)]
        kept_x = jnp.stack([sol_ref[b, 0, h] for b, h in groups])
        solve = lambda lmat, rhs: _unit_lower_solve_kept(lmat, rhs, kept_powers, kept_x)
        _, pull = jax.vjp(_gdn_chunk_fn(len(groups), _chunk_masks(), solve), gather(q_ref), gather(k_ref), gather(v_ref),
                          jnp.stack([gab_ref[b] for b, _ in groups]), sc_ref[0:1, :], sc_ref[1:2, :],
                          jnp.stack([st_ref[b, 0, h] for b, h in groups]))
        dq, dk, dv, dg, d_a, d_dt, dstate = pull((gather(do_ref), dstate_ref[...]))
        for i, (b, h) in enumerate(groups):
            lanes = slice(h * HEAD_DIM, (h + 1) * HEAD_DIM)
            dq_ref[b, :, lanes] = dq[i]
            dk_ref[b, :, lanes] = dk[i]
            dv_ref[b, :, lanes] = dv[i]
        for b in range(B):
            dgab_ref[b] = sum(dg[b * HEADS + h] for h in range(HEADS)).astype(BF16)
        dstate_ref[...] = dstate
        dsc_ref[0:1, :] += d_a
        dsc_ref[1:2, :] += d_dt

        @pl.when(n == N - 1)
        def _():
            for wait in _scatter_copies(src_refs, dst_refs, send_sems, recv_sems, local_sems)[1]:
                wait()

    seq = pl.BlockSpec((B, CHUNK, W), lambda n: (0, N - 1 - n, 0))
    gab_spec = pl.BlockSpec((B, CHUNK, GAB_W), lambda n: (0, N - 1 - n, 0))
    return pl.pallas_call(
        body, grid=(N,),
        in_specs=[seq, seq, seq, gab_spec, _const_spec((8, 128)),
                  pl.BlockSpec((B, 1, HEADS, HEAD_DIM, HEAD_DIM), lambda n: (0, N - 1 - n, 0, 0, 0)),
                  pl.BlockSpec((B, 1, HEADS, SOLVE_POWERS, CHUNK, CHUNK), lambda n: (0, N - 1 - n, 0, 0, 0, 0)),
                  pl.BlockSpec((B, 1, HEADS, CHUNK, 2 * HEAD_DIM), lambda n: (0, N - 1 - n, 0, 0, 0)), seq] + [_ANY] * ns,
        out_specs=[seq, seq, seq, gab_spec, _const_spec((8, 128))] + [_ANY] * ns,
        out_shape=[_sds((B, S, W), F32)] * 3 + [_sds((B, S, GAB_W), BF16), _sds((8, 128), F32)] + [_scattered_shape(p) for p in partials],
        scratch_shapes=[pltpu.VMEM((B * HEADS, HEAD_DIM, HEAD_DIM), F32)] + _scatter_scratch(ns),
        compiler_params=_params(("arbitrary",)), name="gdn_chunk_bwd",
    )(q, k, v, gab, scal, states, powers, solutions, do, *partials)


def _mix_fn(ao, go, gz, w_mla, w_gdn):
    return tuple(_rms(ao[h], w_mla[h]) for h in range(HEADS)) + tuple(_rms(go[h], w_gdn) * _silu(gz[h]) for h in range(HEADS))


def _mix_operands(ao_ref, go_ref, gz_ref, nw_ref):
    blocks = lambda ref: [ref[:, h * HEAD_DIM:(h + 1) * HEAD_DIM] for h in range(HEADS)]
    return blocks(ao_ref), blocks(go_ref), blocks(gz_ref), [nw_ref[h:h + 1, :] for h in range(HEADS)], nw_ref[HEADS:HEADS + 1, :]


def _mix_fwd(ao, go, gz, nw, w_out, x2):
    T, D = x2.shape
    tm = min(TOKEN_TILE, T)
    MW = 2 * HEADS * HEAD_DIM

    def body(ao_ref, go_ref, gz_ref, nw_ref, w_ref, x_ref, mix_ref, h_ref):
        outs = _mix_fn(*_mix_operands(ao_ref, go_ref, gz_ref, nw_ref))
        for i, piece in enumerate(outs):
            mix_ref[:, i * HEAD_DIM:(i + 1) * HEAD_DIM] = piece.astype(BF16)
        h_ref[...] = x_ref[...] + jnp.dot(mix_ref[...], w_ref[...], preferred_element_type=F32)

    half = HEADS * HEAD_DIM
    return pl.pallas_call(
        body, grid=(T // tm,),
        in_specs=[_row_spec(tm, half), _row_spec(tm, half), _row_spec(tm, half), _const_spec((8, 128)), _const_spec((MW, D)),
                  _row_spec(tm, D)],
        out_specs=[_row_spec(tm, MW), _row_spec(tm, D)],
        out_shape=[_sds((T, MW), BF16), _sds((T, D), F32)],
        compiler_params=_params(("parallel",)), name="mix_fwd",
    )(ao, go, gz, nw, w_out, x2)


def _mix_bwd(ao, go, gz, nw, w_out, dh):
    T, D = dh.shape
    tm = min(TOKEN_TILE, T)
    MW = 2 * HEADS * HEAD_DIM
    half = HEADS * HEAD_DIM

    def body(ao_ref, go_ref, gz_ref, nw_ref, w_ref, dh_ref, dao_ref, dgo_ref, dgz_ref, dnw_ref):
        @pl.when(pl.program_id(0) == 0)
        def _():
            dnw_ref[...] = jnp.zeros_like(dnw_ref)

        d_mix = _dg(dh_ref[...].astype(BF16), w_ref[...], 1, 1, None)
        cts = tuple(d_mix[:, i * HEAD_DIM:(i + 1) * HEAD_DIM] for i in range(2 * HEADS))
        _, pull = jax.vjp(_mix_fn, *_mix_operands(ao_ref, go_ref, gz_ref, nw_ref))
        d_ao, d_go, d_gz, d_wm, d_wg = pull(cts)
        for h in range(HEADS):
            lanes = slice(h * HEAD_DIM, (h + 1) * HEAD_DIM)
            dao_ref[:, lanes] = d_ao[h]
            dgo_ref[:, lanes] = d_go[h]
            dgz_ref[:, lanes] = d_gz[h].astype(BF16)
            dnw_ref[h:h + 1, :] += d_wm[h]
        dnw_ref[HEADS:HEADS + 1, :] += d_wg

    return pl.pallas_call(
        body, grid=(T // tm,),
        in_specs=[_row_spec(tm, half), _row_spec(tm, half), _row_spec(tm, half), _const_spec((8, 128)), _const_spec((MW, D)),
                  _row_spec(tm, D)],
        out_specs=[_row_spec(tm, half)] * 3 + [_const_spec((8, 128))],
        out_shape=[_sds((T, half), F32)] * 2 + [_sds((T, half), BF16), _sds((8, 128), F32)],
        compiler_params=_params(("arbitrary",)), name="mix_bwd",
    )(ao, go, gz, nw, w_out, dh)


def _up_spec(w_up, tf):
    per_shard = w_up.shape[2] // tf
    return pl.BlockSpec((None, w_up.shape[1], tf), lambda i, j: (j // per_shard, 0, j % per_shard))


def _mlp_fwd(h2, w_mn, w_up, w_down, target):
    T, D = h2.shape
    FF = w_down.shape[0]
    tm, tf = min(MLP_TOKEN_TILE, T), min(FF_TILE, w_up.shape[2])
    nf = FF // tf

    def body(h_ref, wn_ref, wu_ref, wd_ref, t_ref, hn_ref, act_ref, dy_ref, sq_ref, acc_ref):
        j = pl.program_id(1)

        @pl.when(j == 0)
        def _():
            hn_ref[...] = _rms(h_ref[...], wn_ref[...]).astype(BF16)
            acc_ref[...] = jnp.zeros_like(acc_ref)

        up = jnp.dot(hn_ref[...], wu_ref[...], preferred_element_type=F32)
        act = jnp.square(jnp.maximum(up, 0.0)).astype(BF16)
        act_ref[...] = act
        acc_ref[...] += jnp.dot(act, wd_ref[...], preferred_element_type=F32)

        @pl.when(j == nf - 1)
        def _():
            err = h_ref[...] + acc_ref[...] - t_ref[...]
            dy_ref[...] = err * (1.0 / D)
            sq_ref[...] = jnp.zeros_like(sq_ref) + jnp.sum(err * err)

    tok = lambda w: pl.BlockSpec((tm, w), lambda i, j: (i, 0))
    return pl.pallas_call(
        body, grid=(T // tm, nf),
        in_specs=[tok(D), _const_spec((1, D)), _up_spec(w_up, tf), pl.BlockSpec((tf, D), lambda i, j: (j, 0)), tok(D)],
        out_specs=[tok(D), pl.BlockSpec((tm, tf), lambda i, j: (i, j)), tok(D), pl.BlockSpec((1, 8, 128), lambda i, j: (i, 0, 0))],
        out_shape=[_sds((T, D), BF16), _sds((T, FF), BF16), _sds((T, D), F32), _sds((T // tm, 8, 128), F32)],
        scratch_shapes=[pltpu.VMEM((tm, D), F32)],
        compiler_params=_params(("parallel", "arbitrary")), name="mlp_fwd",
    )(h2, w_mn, w_up, w_down, target)


def _mlp_bwd(h2, w_mn, act, w_up, w_down, dy):
    T, D = h2.shape
    FF = w_down.shape[0]
    tm, tf = min(MLP_TOKEN_TILE, T), min(FF_TILE, w_up.shape[2])
    nf = FF // tf

    def body(h_ref, wn_ref, act_ref, wu_ref, wd_ref, dy_ref, dh_ref, dup_ref, dwn_ref, acc_ref, dyb_ref):
        i, j = pl.program_id(0), pl.program_id(1)

        @pl.when((i == 0) & (j == 0))
        def _():
            dwn_ref[...] = jnp.zeros_like(dwn_ref)

        @pl.when(j == 0)
        def _():
            acc_ref[...] = jnp.zeros_like(acc_ref)
            dyb_ref[...] = dy_ref[...].astype(BF16)

        r = jnp.sqrt(act_ref[...].astype(F32))
        d_act = _dg(dyb_ref[...], wd_ref[...], 1, 1, None)
        d_up = (d_act * (2.0 * r)).astype(BF16)
        dup_ref[...] = d_up
        acc_ref[...] += _dg(d_up, wu_ref[...], 1, 1, None)

        @pl.when(j == nf - 1)
        def _():
            _, pull = jax.vjp(_rms, h_ref[...], wn_ref[...])
            dh, dwn = pull(acc_ref[...])
            dh_ref[...] = dh + dy_ref[...]
            dwn_ref[...] += dwn

    tok = lambda w: pl.BlockSpec((tm, w), lambda i, j: (i, 0))
    ff = pl.BlockSpec((tm, tf), lambda i, j: (i, j))
    return pl.pallas_call(
        body, grid=(T // tm, nf),
        in_specs=[tok(D), _const_spec((1, D)), ff, _up_spec(w_up, tf), pl.BlockSpec((tf, D), lambda i, j: (j, 0)), tok(D)],
        out_specs=[tok(D), ff, _const_spec((1, D))],
        out_shape=[_sds((T, D), F32), _sds((T, FF), BF16), _sds((1, D), F32)],
        scratch_shapes=[pltpu.VMEM((tm, D), F32), pltpu.VMEM((tm, D), BF16)],
        compiler_params=_params(("arbitrary", "arbitrary")), name="mlp_bwd",
    )(h2, w_mn, act, w_up, w_down, dy)


def _rope_pad(a):
    z = jnp.zeros(a.shape[:-1] + (ROPE_HALF,), a.dtype)
    return jnp.concatenate([a[..., :ROPE_HALF], z, a[..., ROPE_HALF:], z], axis=-1)


def _rope_unpad(a):
    return jnp.concatenate([a[..., :ROPE_HALF], a[..., 2 * ROPE_HALF:3 * ROPE_HALF]], axis=-1)


_G0 = 2 * LORA + ROPE_DIM
W_IN_COLS = _G0 + GQKV_W + GZ_W + 2 * HEADS


def _widen_w_in_t(w_t):
    z = jnp.zeros((ROPE_HALF, w_t.shape[1]), w_t.dtype)
    pad = jnp.zeros((GAB_W - 2 * HEADS, w_t.shape[1]), w_t.dtype)
    return jnp.concatenate([w_t[:2 * LORA + ROPE_HALF], z, w_t[2 * LORA + ROPE_HALF:_G0], z, w_t[_G0:], pad], axis=0)


def _narrow_w_in_t(w_t):
    return jnp.concatenate([w_t[:2 * LORA + ROPE_HALF], w_t[2 * LORA + 2 * ROPE_HALF:2 * LORA + 3 * ROPE_HALF],
                            w_t[LAT_W:LAT_W + W_IN_COLS - _G0]], axis=0)


def _stack_mla(w_uq, w_ukv):
    uq = w_uq.reshape(LORA, HEADS, QK_DIM)
    ukv = w_ukv.reshape(LORA, HEADS, 2 * HEAD_DIM)
    parts = [uq[:, :, :HEAD_DIM], _rope_pad(uq[:, :, HEAD_DIM:]), ukv[:, :, :HEAD_DIM], ukv[:, :, HEAD_DIM:]]
    return jnp.concatenate([p.transpose(1, 0, 2) for p in parts], axis=0)


def _unstack_mla(w):
    p = [w[i * HEADS:(i + 1) * HEADS].transpose(1, 0, 2) for i in range(4)]
    uq = jnp.concatenate([p[0], _rope_unpad(p[1])], axis=-1).reshape(LORA, HEADS * QK_DIM)
    ukv = jnp.concatenate([p[2], p[3]], axis=-1).reshape(LORA, HEADS * 2 * HEAD_DIM)
    return uq, ukv


def _rows8(rows):
    a = jnp.concatenate(rows, axis=0)
    return jnp.pad(a, ((0, 8 - a.shape[0]), (0, 0)))


def _qk_norm_rows(q_norm_w, k_norm_w):
    return _rows8([q_norm_w[:, :HEAD_DIM], _rope_pad(q_norm_w[:, HEAD_DIM:]), k_norm_w[:, :HEAD_DIM], _rope_pad(k_norm_w[:, HEAD_DIM:])])


def _rope_rows():
    inv_freq = ROPE_THETA ** (-jnp.arange(ROPE_HALF, dtype=F32) / ROPE_HALF)
    z = jnp.zeros((ROPE_HALF,), F32)
    freq = jnp.concatenate([inv_freq, z, inv_freq, z])
    sign = jnp.concatenate([-jnp.ones((ROPE_HALF,), F32), z, jnp.ones((ROPE_HALF,), F32), z])
    return _rows8([freq[None], sign[None]])


def _column_shards(a):
    return a.reshape(a.shape[0], 4, a.shape[1] // 4).transpose(1, 0, 2)


def _from_column_shards(a):
    return a.transpose(1, 0, 2).reshape(a.shape[1], 4 * a.shape[2])


_ANY = pl.BlockSpec(memory_space=pl.ANY)
_OTHER_CHIPS = ((1, 0), (0, 1), (1, 1))


def _here():
    return lax.axis_index("x"), lax.axis_index("y"), lax.axis_index("c")


def _flip(v, bit):
    return 1 - v if bit else v


def _remote(src, dst, send_sems, recv_sems, k, to):
    return pltpu.make_async_remote_copy(src_ref=src, dst_ref=dst, send_sem=send_sems.at[k], recv_sem=recv_sems.at[k],
                                        device_id=to, device_id_type=MESH)


def _half_of(ref, k, shape):
    r, c = shape
    if (r // 2) % 16 == 0:
        return ref.at[pl.ds(pl.multiple_of(k * (r // 2), 16), r // 2)]
    if (c // 2) % 128 == 0:
        return ref.at[:, pl.ds(pl.multiple_of(k * (c // 2), 128), c // 2)]
    return None


def _gather_copies(srcs, dsts, send_sems, recv_sems, local_sems):
    x, y, c = _here()
    slot, sibling, n = 2 * x + y, (x, y, 1 - c), len(srcs)
    starts, passes, waits = [], [], []
    for i, (src, dst) in enumerate(zip(srcs, dsts)):
        own = pltpu.make_async_copy(src, dst.at[slot], local_sems.at[i])
        starts.append(own.start)
        waits.append(own.wait)
        halves = _half_of(src, c, src.shape) is not None
        for j, (fx, fy) in enumerate(_OTHER_CHIPS):
            cx, cy = _flip(x, fx), _flip(y, fy)
            there = dst.at[2 * cx + cy]
            if halves:
                push = _remote(_half_of(src, c, src.shape), _half_of(dst.at[slot], c, src.shape), send_sems, recv_sems, 3 * i + j, (cx, cy, c))
                landed, other = _half_of(there, c, src.shape), _half_of(there, 1 - c, src.shape)
                onward = _remote(landed, landed, send_sems, recv_sems, 3 * n + 3 * i + j, sibling)
                passes += [_remote(landed, landed, send_sems, recv_sems, 3 * i + j, (cx, cy, c)).wait_recv, onward.start]
                waits += [_remote(other, other, send_sems, recv_sems, 3 * n + 3 * i + j, sibling).wait_recv, onward.wait_send]
            else:
                push = _remote(src, dst.at[slot], send_sems, recv_sems, 3 * i + j, (cx, cy, c))
                waits.append(_remote(there, there, send_sems, recv_sems, 3 * i + j, (cx, cy, c)).wait_recv)
            starts.append(push.start)
            waits.append(push.wait_send)
    return starts, passes, waits


def _gather_scratch(n):
    return [pltpu.SemaphoreType.DMA((6 * n,)), pltpu.SemaphoreType.DMA((6 * n,)), pltpu.SemaphoreType.DMA((n,))]


def _all_gather(shards, name):
    ns = len(shards)

    def body(*refs):
        starts, passes, waits = _gather_copies(refs[:ns], refs[ns:2 * ns], *refs[2 * ns:])
        for call in starts + passes + waits:
            call()

    return pl.pallas_call(
        body, in_specs=[_ANY] * ns, out_specs=[_ANY] * ns, out_shape=[_sds((4,) + s.shape, s.dtype) for s in shards],
        scratch_shapes=_gather_scratch(ns), name=name,
    )(*shards)


def _by_lanes(shape):
    return (shape[-2] // 2) % 16 != 0


def _scattered_shape(p):
    r, c = p.shape[1:]
    return _sds((8, r, c // 2) if _by_lanes(p.shape) else (8, r // 2, c), p.dtype)


def _scatter_copies(srcs, dsts, send_sems, recv_sems, local_sems, whole=0):
    x, y, c = _here()
    me = 4 * x + 2 * y + c
    starts, waits = [], []
    for i, (src, dst) in enumerate(zip(srcs, dsts)):
        def piece(px, py, pc, src=src, entire=i >= len(srcs) - whole):
            if entire:
                return src
            if _by_lanes(src.shape):
                half = src.shape[2] // 2
                return src.at[2 * px + py, :, pl.ds(pl.multiple_of(pc * half, 128), half)]
            half = src.shape[1] // 2
            return src.at[2 * px + py, pl.ds(pl.multiple_of(pc * half, 16), half)]

        own = pltpu.make_async_copy(piece(x, y, c), dst.at[me], local_sems.at[i])
        starts.append(own.start)
        waits.append(own.wait)
        for k in range(1, 8):
            px, py, pc = _flip(x, k & 4), _flip(y, k & 2), _flip(c, k & 1)
            push = _remote(piece(px, py, pc), dst.at[me], send_sems, recv_sems, 7 * i + k - 1, (px, py, pc))
            landed = dst.at[4 * px + 2 * py + pc]
            starts.append(push.start)
            waits += [_remote(landed, landed, send_sems, recv_sems, 7 * i + k - 1, (px, py, pc)).wait_recv, push.wait_send]
    return starts, waits


def _scatter_scratch(n):
    return [pltpu.SemaphoreType.DMA((7 * n,)), pltpu.SemaphoreType.DMA((7 * n,)), pltpu.SemaphoreType.DMA((n,))]


def _swapped_shape(half):
    r, c = half.shape
    return _sds((r, 2 * c) if _by_lanes((r, 2 * c)) else (2, r, c), half.dtype)


def _swap_copies(srcs, dsts, send_sems, recv_sems, local_sems):
    x, y, c = _here()
    sibling = (x, y, 1 - c)
    starts, waits = [], []
    for i, (src, dst) in enumerate(zip(srcs, dsts)):
        if len(dst.shape) == 2:
            lanes = src.shape[1]
            mine, other = (dst.at[:, pl.ds(pl.multiple_of(k * lanes, 128), lanes)] for k in (c, 1 - c))
        else:
            mine, other = dst.at[c], dst.at[1 - c]
        own = pltpu.make_async_copy(src, mine, local_sems.at[i])
        push = _remote(src, mine, send_sems, recv_sems, i, sibling)
        starts += [own.start, push.start]
        waits += [_remote(other, other, send_sems, recv_sems, i, sibling).wait_recv, push.wait_send, own.wait]
    return starts, waits


def _swap_scratch(n):
    return [pltpu.SemaphoreType.DMA((n,)), pltpu.SemaphoreType.DMA((n,)), pltpu.SemaphoreType.DMA((n,))]


def _exchange_halves(halves, wholes):
    ns, nw = len(halves), len(wholes)

    def body(*refs):
        srcs, dsts = refs[:ns + nw], refs[ns + nw:2 * (ns + nw)]
        sems = refs[2 * (ns + nw):]
        starts, waits = _swap_copies(srcs[:ns], dsts[:ns], *sems[:3])
        more = _scatter_copies(srcs[ns:], dsts[ns:], *sems[3:], whole=nw)
        for call in starts + more[0] + waits + more[1]:
            call()

    return pl.pallas_call(
        body, in_specs=[_ANY] * (ns + nw), out_specs=[_ANY] * (ns + nw),
        out_shape=[_swapped_shape(h) for h in halves] + [_sds((8,) + a.shape, a.dtype) for a in wholes],
        scratch_shapes=_swap_scratch(ns) + _scatter_scratch(nw), name="exchange_halves",
    )(*halves, *wholes)


def _row_tile(rows, row_bytes, budget):
    tr = rows
    while tr * row_bytes > budget and tr % 16 == 0:
        tr //= 2
    return tr


def _sum_slots(parts, name):
    _, rows, cols = parts.shape
    tr = _row_tile(rows, 8 * cols * 4, 2 * 1024 * 1024)

    def body(p_ref, o_ref):
        acc = p_ref[0].astype(F32)
        for d in range(1, 8):
            acc = acc + p_ref[d].astype(F32)
        o_ref[...] = acc

    return pl.pallas_call(
        body, grid=(rows // tr,), in_specs=[pl.BlockSpec((8, tr, cols), lambda i: (0, i, 0))],
        out_specs=pl.BlockSpec((tr, cols), lambda i: (i, 0)), out_shape=_sds((rows, cols), F32),
        compiler_params=_params(("parallel",)), name=name,
    )(parts)


def _adam_update(w, g, m, v):
    m = ADAM_B1 * m + (1.0 - ADAM_B1) * g
    v = ADAM_B2 * v + (1.0 - ADAM_B2) * jnp.square(g)
    m_hat = m / (1.0 - ADAM_B1 ** ADAM_STEP)
    v_hat = v / (1.0 - ADAM_B2 ** ADAM_STEP)
    return -ADAM_LR * (m_hat / (jnp.sqrt(v_hat) + ADAM_EPS) + ADAM_WD * w), m, v


SMALL_ROWS = {"attn_norm_w": 0, "mlp_norm_w": 1, "q_lat_norm_w": 2, "kv_lat_norm_w": 3, "q_norm_w": 4, "k_norm_w": 5,
              "mla_out_norm_w": 6, "gdn_norm_w": 10, "a_log": 11, "dt_bias": 12}
LOSS_ROW = 13
SMALL_SHAPE = (16, 1024)


def _pack_small_partials(d_attn_nw, d_mlp_nw, d_ln, d_qk_nw, d_mix_nw, d_scal, conv_parts, sq):
    D = d_attn_nw.shape[1]

    def body(an_ref, mn_ref, ln_ref, qk_ref, mix_ref, sc_ref, cq_ref, ck_ref, cv_ref, sq_ref, a_ref, c_ref):
        a_ref[...] = jnp.zeros_like(a_ref)
        a_ref[0:1, :D] = an_ref[...]
        a_ref[1:2, :D] = mn_ref[...]
        a_ref[2:4, :LORA] = ln_ref[...]
        for row, base in ((4, 0), (5, 2)):
            rope = qk_ref[base + 1:base + 2, :]
            a_ref[row:row + 1, :QK_DIM] = jnp.concatenate(
                [qk_ref[base:base + 1, :], rope[:, :ROPE_HALF], rope[:, 2 * ROPE_HALF:3 * ROPE_HALF]], axis=1)
        a_ref[6:6 + HEADS, :HEAD_DIM] = mix_ref[0:HEADS, :]
        a_ref[10:11, :HEAD_DIM] = mix_ref[HEADS:HEADS + 1, :]
        a_ref[11:13, :128] = sc_ref[0:2, :]
        a_ref[LOSS_ROW:LOSS_ROW + 1, :128] = jnp.zeros((1, 128), F32) + jnp.sum(sq_ref[:, 0:1, 0:1]) * (0.5 / D)
        c_ref[...] = jnp.concatenate([cq_ref[...], ck_ref[...], cv_ref[...]], axis=1)

    return pl.pallas_call(
        body, out_shape=[_sds(SMALL_SHAPE, F32), _sds((CONV_TAPS, GQKV_W), F32)], name="pack_small_partials",
    )(d_attn_nw, d_mlp_nw, d_ln, d_qk_nw, d_mix_nw, d_scal, *conv_parts, sq)


def _adamw_small(parts, conv_parts, w, m, v):
    names = tuple(SMALL_ROWS) + ("conv_w",)
    cols = w["conv_w"].shape[2]

    def body(*refs):
        p_ref, c_ref = refs[:2]
        n = len(names)
        w_refs, m_refs, v_refs = (dict(zip(names, refs[2 + k * n:2 + (k + 1) * n])) for k in range(3))
        loss_ref = refs[2 + 3 * n]
        out = [dict(zip(names, refs[3 + (3 + k) * n:3 + (4 + k) * n])) for k in range(4)]
        acc_ref, cacc_ref = refs[3 + 7 * n:]
        acc, cacc = p_ref[0], c_ref[0]
        for d in range(1, 8):
            acc, cacc = acc + p_ref[d], cacc + c_ref[d]
        acc_ref[...] = acc
        cacc_ref[...] = cacc
        loss_ref[...] = acc_ref[LOSS_ROW:LOSS_ROW + 1, 0:1]
        chip = 2 * lax.axis_index("x") + lax.axis_index("y")
        for name in names:
            shape = w_refs[name].shape
            if name == "conv_w":
                g = sum(jnp.where(chip == s, cacc_ref[:, s * cols:(s + 1) * cols], 0.0) for s in range(4))[None]
            else:
                row = SMALL_ROWS[name]
                g = acc_ref[row:row + math.prod(shape[:-1]), 0:shape[-1]].reshape(shape)
            delta, new_m, new_v = _adam_update(w_refs[name][...], g, m_refs[name][...], v_refs[name][...])
            for ref, val in zip((o[name] for o in out), (g, delta, new_m, new_v)):
                ref[...] = val

    ins = [x[n] for x in (w, m, v) for n in names]
    shapes = [_sds(w[n].shape, F32) for n in names]
    outs = pl.pallas_call(
        body, out_shape=[_sds((1, 1), F32)] + shapes * 4,
        scratch_shapes=[pltpu.VMEM(parts.shape[1:], F32), pltpu.VMEM(conv_parts.shape[1:], F32)], name="adamw_small",
    )(parts, conv_parts, *ins)
    n = len(names)
    return (outs[0],) + tuple(dict(zip(names, outs[1 + k * n:1 + (k + 1) * n])) for k in range(4))


def _adamw(w, g, m, v, name):
    rows, cols = w.shape[0], w.shape[-1]
    if w.ndim == 3:
        tr = max(d for d in range(1, rows + 1) if rows % d == 0 and d * 8 * cols * 4 * 14 <= VMEM_LIMIT // 2)
    else:
        tr = _row_tile(rows, 7 * cols * 4, 4 * 1024 * 1024)

    def body(w_ref, g_ref, m_ref, v_ref, d_ref, mo_ref, vo_ref):
        d_ref[...], mo_ref[...], vo_ref[...] = _adam_update(w_ref[...], g_ref[...], m_ref[...], v_ref[...])

    block = (tr,) + w.shape[1:]
    spec = pl.BlockSpec(block, lambda i: (i,) + (0,) * (len(block) - 1))
    return pl.pallas_call(
        body, grid=(rows // tr,), in_specs=[spec] * 4, out_specs=[spec] * 3, out_shape=[_sds(w.shape, F32)] * 3,
        compiler_params=_params(("parallel",)), name=name,
    )(w, g, m, v)


def kernel(x, positions, attn_norm_w, w_in, q_lat_norm_w, w_uq, kv_lat_norm_w, w_ukv, q_norm_w, k_norm_w, mla_out_norm_w, conv_w, a_log, dt_bias, gdn_norm_w, w_out, mlp_norm_w, w_up, w_down, loss_target, m_attn_norm_w, m_w_in, m_q_lat_norm_w, m_w_uq, m_kv_lat_norm_w, m_w_ukv, m_q_norm_w, m_k_norm_w, m_mla_out_norm_w, m_conv_w, m_a_log, m_dt_bias, m_gdn_norm_w, m_w_out, m_mlp_norm_w, m_w_up, m_w_down, v_attn_norm_w, v_w_in, v_q_lat_norm_w, v_w_uq, v_kv_lat_norm_w, v_w_ukv, v_q_norm_w, v_k_norm_w, v_mla_out_norm_w, v_conv_w, v_a_log, v_dt_bias, v_gdn_norm_w, v_w_out, v_mlp_norm_w, v_w_up, v_w_down):
    w = dict(zip(WEIGHTS, (attn_norm_w, w_in, q_lat_norm_w, w_uq, kv_lat_norm_w, w_ukv, q_norm_w, k_norm_w, mla_out_norm_w, conv_w,
                           a_log, dt_bias, gdn_norm_w, w_out, mlp_norm_w, w_up, w_down)))
    m = dict(zip(WEIGHTS, (m_attn_norm_w, m_w_in, m_q_lat_norm_w, m_w_uq, m_kv_lat_norm_w, m_w_ukv, m_q_norm_w, m_k_norm_w,
                           m_mla_out_norm_w, m_conv_w, m_a_log, m_dt_bias, m_gdn_norm_w, m_w_out, m_mlp_norm_w, m_w_up, m_w_down)))
    v = dict(zip(WEIGHTS, (v_attn_norm_w, v_w_in, v_q_lat_norm_w, v_w_uq, v_kv_lat_norm_w, v_w_ukv, v_q_norm_w, v_k_norm_w,
                           v_mla_out_norm_w, v_conv_w, v_a_log, v_dt_bias, v_gdn_norm_w, v_w_out, v_mlp_norm_w, v_w_up, v_w_down)))
    B, S, D = x.shape
    T = B * S
    x2, pos, target = x.reshape(T, D), positions.reshape(T, 1), loss_target.reshape(T, D)
    seq = lambda a: a.reshape(B, S, a.shape[-1])
    tok = lambda a: a.reshape(T, a.shape[-1])
    local = {n: w[n][0] for n in SHARDED}

    g_in, g_uq, g_ukv, g_conv = _all_gather([jnp.swapaxes(w_in, 1, 2)[0].astype(BF16), local["w_uq"].astype(BF16),
                                             local["w_ukv"].astype(BF16), local["conv_w"]], "gather_first_weights")
    w_in_p = _widen_w_in_t(g_in.reshape(-1, D))
    w_mla = _stack_mla(_from_column_shards(g_uq), _from_column_shards(g_ukv))
    conv_full = _from_column_shards(g_conv)
    ln_w = jnp.concatenate([q_lat_norm_w, kv_lat_norm_w], axis=0)
    qk_nw = _qk_norm_rows(q_norm_w, k_norm_w)
    rope_rows = _rope_rows()
    scal = _rows8([jnp.pad(a_log, ((0, 0), (0, 128 - HEADS))), jnp.pad(dt_bias, ((0, 0), (0, 128 - HEADS)))])
    mix_nw = _rows8([mla_out_norm_w[0], gdn_norm_w])

    xn, lat, gqkv, gz, gab = _in_proj_fwd(x2, attn_norm_w, w_in_p)
    q, k, v_att = _mla_pre_fwd(lat, pos, ln_w, w_mla, qk_nw, rope_rows)
    ao, lse, g_down = _attn_fwd(seq(q), seq(k), seq(v_att), [local["w_down"].astype(BF16)])
    gq, gk, gv = _gdn_pre_fwd(seq(gqkv), conv_full)
    go, states, powers, solutions, g_out, w_up_b = _gdn_chunk_fwd(gq, gk, gv, seq(gab), scal,
                                                                  [local["w_out"].astype(BF16), local["w_up"].astype(BF16)])
    w_out_b = g_out.reshape(-1, D)
    w_down_b = g_down.reshape(-1, D)
    mix, h2 = _mix_fwd(tok(ao), tok(go), gz, mix_nw, w_out_b, x2)
    hn, act, dy, sq = _mlp_fwd(h2, mlp_norm_w, w_up_b, w_down_b, target)

    dh, d_up, d_mlp_nw = _mlp_bwd(h2, mlp_norm_w, act, w_up_b, w_down_b, dy)
    p_down = _wgrad(act, dy, "wgrad_down").reshape(4, -1, D)
    p_up = _wgrad(hn, d_up, "wgrad_up", column_shards=4)
    d_ao, d_go, d_gz, d_mix_nw = _mix_bwd(tok(ao), tok(go), gz, mix_nw, w_out_b, dh)
    p_out = _wgrad(mix, dh, "wgrad_out").reshape(4, -1, D)
    d_gq, d_gk, d_gv, d_gab, d_scal, s_up, s_out = _gdn_chunk_bwd(gq, gk, gv, seq(gab), scal, states, powers, solutions, seq(d_go),
                                                                  [p_up, p_out])
    early = ("w_up", "w_out", "w_down")
    dxq, dxk, dxv, dcq, dck, dcv, g_up, g_out = _gdn_pre_bwd(seq(gqkv), conv_full, d_gq, d_gk, d_gv,
                                                             [_sum_slots(s_up, "sum_w_up"), _sum_slots(s_out, "sum_w_out")])
    dq, dk, dv, s_down = _attn_bwd(seq(q), seq(k), seq(v_att), ao, lse, seq(d_ao), [p_down])
    d_lat, d_ln, d_w_mla, d_qk_nw, g_down = _mla_pre_bwd(lat, pos, ln_w, w_mla, qk_nw, rope_rows, tok(dq), tok(dk), tok(dv),
                                                         [_sum_slots(s_down, "sum_w_down")])
    early_grads = [g_up, g_out, g_down]
    d_pieces = [d_lat, tok(dxq), tok(dxk), tok(dxv), d_gz, tok(d_gab)]
    p_in = _narrow_w_in_t(_wgrad_pieces(d_pieces, xn, "wgrad_in")).reshape(4, -1, D)
    p_uq, p_ukv = (_column_shards(a).astype(BF16) for a in _unstack_mla(d_w_mla))
    grad_x2, d_attn_nw, s_in, s_uq, s_ukv = _in_proj_bwd(d_pieces, w_in_p, x2, attn_norm_w, dh, [p_in, p_uq, p_ukv])
    small_buf, conv_buf = _pack_small_partials(d_attn_nw, d_mlp_nw, d_ln, d_qk_nw, d_mix_nw, d_scal, (dcq, dck, dcv), sq)

    late = ("w_in", "w_uq", "w_ukv")
    *late_grads, s_small, s_conv = _exchange_halves([_sum_slots(s, "sum_" + n) for n, s in zip(late, (s_in, s_uq, s_ukv))],
                                                    [small_buf, conv_buf])
    names = early + late
    grad = {n: g.reshape(-1, g.shape[-1]) for n, g in zip(names, list(early_grads) + list(late_grads))}

    loss, g_small, delta, new_m, new_v = _adamw_small(s_small, s_conv, w, m, v)
    grad.update(g_small)
    for n in names:
        if n == "w_in":
            stored = lambda a: jnp.transpose(a, (2, 0, 1))
            outs = _adamw(stored(w[n]), grad[n][:, None, :], stored(m[n]), stored(v[n]), "adamw_" + n)
            grad[n], delta[n], new_m[n], new_v[n] = (jnp.transpose(a, (1, 2, 0)) for a in (grad[n][:, None, :], *outs))
        else:
            delta[n], new_m[n], new_v[n] = _adamw(local[n], grad[n], m[n][0], v[n][0], "adamw_" + n)
    def in_order(d):
        return [d[n].reshape(w[n].shape) for n in WEIGHTS]

    return (loss.reshape(()), grad_x2.reshape(B, S, D), *in_order(grad), *in_order(delta), *in_order(new_m), *in_order(new_v))
```

```python
import functools
import math

import jax
import jax.numpy as jnp
from jax import lax
from jax.experimental import pallas as pl
from jax.experimental.pallas import tpu as pltpu

F32 = jnp.float32
BF16 = jnp.bfloat16
MESH = pl.DeviceIdType.MESH

EPS = 1e-6
HEADS = 4
HEAD_DIM = 128
ROPE_DIM = 64
ROPE_HALF = 32
QK_DIM = 192
QK_PAD = 256
LORA = 256
CHUNK = 64
SOLVE_POWERS = 5
CONV_TAPS = 4
ROPE_THETA = 10000.0
ATTN_SCALE = QK_DIM ** -0.5

LAT_W = 640
GQKV_W = 3 * HEADS * HEAD_DIM
GZ_W = HEADS * HEAD_DIM
GAB_W = 128
PROJ_SPLITS = ((0, LAT_W), (LAT_W, LAT_W + GQKV_W), (LAT_W + GQKV_W, LAT_W + GQKV_W + GZ_W),
               (LAT_W + GQKV_W + GZ_W, LAT_W + GQKV_W + GZ_W + GAB_W))
PROJ_W = PROJ_SPLITS[-1][1]

ADAM_LR = 0.001
ADAM_B1 = 0.9
ADAM_B2 = 0.999
ADAM_EPS = 1e-08
ADAM_WD = 0.01
ADAM_STEP = 10

TOKEN_TILE = 512
WGRAD_TOKEN_TILE = 1024
MLP_TOKEN_TILE = 512
FF_TILE = 1024
ATTN_TILE = 512
ATTN_HEADS_PER_STEP = 2
WGRAD_OUT_BYTES = 8 * 1024 * 1024
VMEM_LIMIT = 48 * 1024 * 1024

SHARDED = ("w_in", "w_uq", "w_ukv", "conv_w", "w_out", "w_up", "w_down")
WEIGHTS = ("attn_norm_w", "w_in", "q_lat_norm_w", "w_uq", "kv_lat_norm_w", "w_ukv", "q_norm_w", "k_norm_w", "mla_out_norm_w",
           "conv_w", "a_log", "dt_bias", "gdn_norm_w", "w_out", "mlp_norm_w", "w_up", "w_down")


def _sds(shape, dtype):
    return jax.ShapeDtypeStruct(shape, dtype)


def _params(semantics):
    return pltpu.CompilerParams(dimension_semantics=semantics, vmem_limit_bytes=VMEM_LIMIT)


def _block(n):
    for b in (512, 256, 128):
        if n % b == 0:
            return b
    return n


def _dg(a, b, ca, cb, prec):
    lead = a.ndim - 2
    batch = (tuple(range(lead)),) * 2
    return lax.dot_general(a, b, (((ca + lead,), (cb + lead,)), batch), precision=prec, preferred_element_type=F32)


def _split_bf16(a):
    hi = a.astype(BF16)
    return hi, (a - hi.astype(F32)).astype(BF16)


def _dot_bf16(a, b, ca, cb):
    return _dg(a.astype(BF16), b.astype(BF16), ca, cb, None)


def _dot_bf16x3(a, b, ca, cb):
    a_hi, a_lo = _split_bf16(a)
    b_hi, b_lo = _split_bf16(b)
    lead = a.ndim - 2
    return _dg(jnp.concatenate([a_hi, a_hi, a_lo], axis=ca + lead), jnp.concatenate([b_hi, b_lo, b_hi], axis=cb + lead), ca, cb, None)


def _matmul_family(dot):
    def nn_raw(a, b):
        return dot(a, b, 1, 0)

    def nt_raw(a, b):
        return dot(a, b, 1, 1)

    def tn_raw(a, b):
        return dot(a, b, 0, 0)

    @jax.custom_vjp
    def nn(a, b):
        return nn_raw(a, b)

    nn.defvjp(lambda a, b: (nn_raw(a, b), (a, b)), lambda r, g: (nt_raw(g, r[1]), tn_raw(r[0], g)))

    @jax.custom_vjp
    def nt(a, b):
        return nt_raw(a, b)

    nt.defvjp(lambda a, b: (nt_raw(a, b), (a, b)), lambda r, g: (nn_raw(g, r[1]), tn_raw(g, r[0])))

    @jax.custom_vjp
    def tn(a, b):
        return tn_raw(a, b)

    tn.defvjp(lambda a, b: (tn_raw(a, b), (a, b)), lambda r, g: (nt_raw(r[1], g), nn_raw(r[0], g)))
    return nn, nt, tn


_bf_nn, _bf_nt, _bf_tn = _matmul_family(_dot_bf16)
_hi_nn, _hi_nt, _hi_tn = _matmul_family(_dot_bf16x3)


def _lower_powers(lmat):
    powers = []
    while 2 ** (len(powers) + 1) < lmat.shape[-1]:
        powers.append(_dot_bf16x3(powers[-1] if powers else lmat, powers[-1] if powers else lmat, 1, 0))
    return powers


@jax.custom_vjp
def _unit_lower_solve(lmat, rhs):
    return _unit_lower_solve_fwd(lmat, rhs)[0]


def _unit_lower_solve_fwd(lmat, rhs):
    powers = _lower_powers(lmat)
    x = rhs - _dot_bf16x3(lmat, rhs, 1, 0)
    for p in powers:
        x = x + _dot_bf16x3(p, x, 1, 0)
    return x, (lmat, powers, x)


def _unit_lower_solve_bwd(res, g):
    lmat, powers, x = res
    y = g - _dot_bf16x3(lmat, g, 0, 0)
    for p in powers:
        y = y + _dot_bf16x3(p, y, 0, 0)
    return -_dot_bf16x3(y, x, 1, 1), y


_unit_lower_solve.defvjp(_unit_lower_solve_fwd, _unit_lower_solve_bwd)


@jax.custom_vjp
def _unit_lower_solve_kept(lmat, rhs, powers, x):
    return x


_unit_lower_solve_kept.defvjp(
    lambda lmat, rhs, powers, x: (x, (lmat, powers, x)),
    lambda res, g: _unit_lower_solve_bwd(res, g) + ([jnp.zeros_like(p) for p in res[1]], jnp.zeros_like(res[2])))


@jax.custom_vjp
def _lane_halves(x):
    n = x.shape[-1] // 2
    return x[..., :n], x[..., n:]


_lane_halves.defvjp(lambda x: (_lane_halves(x), None), lambda _, g: (jnp.concatenate(g, axis=-1),))


@jax.custom_vjp
def _row_halves(x):
    n = x.shape[-2] // 2
    return x[..., :n, :], x[..., n:, :]


_row_halves.defvjp(lambda x: (_row_halves(x), None), lambda _, g: (jnp.concatenate(g, axis=-2),))


@jax.custom_vjp
def _swap_halves(t):
    return pltpu.roll(t, 64, 1)


_swap_halves.defvjp(lambda t: (pltpu.roll(t, 64, 1), None), lambda _, g: (pltpu.roll(g, 64, 1),))


@functools.partial(jax.custom_vjp, nondiff_argnums=(2,))
def _shift_rows(x, keep, s):
    return pltpu.roll(x, s, 0) * keep


def _shift_rows_fwd(x, keep, s):
    return pltpu.roll(x, s, 0) * keep, keep


def _shift_rows_bwd(s, keep, g):
    return pltpu.roll(g * keep, keep.shape[0] - s, 0), jnp.zeros_like(keep)


_shift_rows.defvjp(_shift_rows_fwd, _shift_rows_bwd)


def _sigmoid(x):
    return 0.5 * jnp.tanh(0.5 * x) + 0.5


def _softplus(x):
    return jnp.maximum(x, 0.0) + jnp.log(1.0 + jnp.exp(jnp.minimum(x, -x)))


def _silu(x):
    return x * _sigmoid(x)


def _rms(x, w, n=None):
    n = x.shape[-1] if n is None else n
    r = lax.rsqrt(jnp.sum(x * x, axis=-1, keepdims=True) * (1.0 / n) + EPS)
    return x * r * w


def _rope(t, cos_f, sin_f):
    return t * cos_f + _swap_halves(t) * sin_f


def _rope_tables(pos_col, freq_row, sign_row):
    ang = pos_col.astype(F32) * freq_row
    return jnp.cos(ang), jnp.sin(ang) * sign_row


def _onehot_row(lane):
    return (lax.broadcasted_iota(jnp.int32, (1, 128), 1) == lane).astype(F32)


def _row_spec(tm, w):
    return pl.BlockSpec((tm, w), lambda i: (i, 0))


def _const_spec(shape):
    return pl.BlockSpec(shape, lambda *_: (0,) * len(shape))


def _in_proj_fwd(x2, w_an, w_in_p):
    T, D = x2.shape
    tm = min(TOKEN_TILE, T)

    def body(x_ref, wn_ref, w_ref, xn_ref, lat_ref, gqkv_ref, gz_ref, gab_ref):
        x = x_ref[...]
        r = lax.rsqrt(jnp.mean(x * x, axis=-1, keepdims=True) + EPS)
        xn = (x * r * wn_ref[...]).astype(BF16)
        xn_ref[...] = xn
        for ref, (a, b) in zip((lat_ref, gqkv_ref, gz_ref, gab_ref), PROJ_SPLITS):
            ref[...] = _dg(xn, w_ref[a:b, :], 1, 1, None)

    widths = [b - a for a, b in PROJ_SPLITS]
    return pl.pallas_call(
        body, grid=(T // tm,),
        in_specs=[_row_spec(tm, D), _const_spec((1, D)), _const_spec((PROJ_W, D))],
        out_specs=[_row_spec(tm, D)] + [_row_spec(tm, w) for w in widths],
        out_shape=[_sds((T, D), BF16)] + [_sds((T, w), F32) for w in widths],
        compiler_params=_params(("parallel",)), name="in_proj_fwd",
    )(x2, w_an, w_in_p)


def _in_proj_bwd(pieces, w_in_p, x2, w_an, dh, partials):
    T, D = x2.shape
    tm = min(TOKEN_TILE, T)
    widths = [p.shape[1] for p in pieces]
    starts = [sum(widths[:i]) for i in range(len(widths))]
    assert sum(widths) == PROJ_W
    npc, ns = len(pieces), len(partials)

    def body(*refs):
        piece_refs = refs[:npc]
        w_ref, x_ref, wn_ref, dh_ref = refs[npc:npc + 4]
        src_refs = refs[npc + 4:npc + 4 + ns]
        dx_ref, dwn_ref = refs[npc + 4 + ns:npc + 6 + ns]
        dst_refs = refs[npc + 6 + ns:npc + 6 + 2 * ns]
        sems = refs[npc + 6 + 2 * ns:]

        @pl.when(pl.program_id(0) == 0)
        def _():
            for start in _scatter_copies(src_refs, dst_refs, *sems)[0]:
                start()
            dwn_ref[...] = jnp.zeros_like(dwn_ref)

        dxn = jnp.zeros((tm, D), F32)
        for ref, a, width in zip(piece_refs, starts, widths):
            dxn += _dg(ref[...], w_ref[a:a + width, :], 1, 0, None)
        _, pull = jax.vjp(_rms, x_ref[...], wn_ref[...])
        dx, dwn = pull(dxn)
        dx_ref[...] = dx + dh_ref[...]
        dwn_ref[...] += dwn

        @pl.when(pl.program_id(0) == T // tm - 1)
        def _():
            for wait in _scatter_copies(src_refs, dst_refs, *sems)[1]:
                wait()

    return pl.pallas_call(
        body, grid=(T // tm,),
        in_specs=[_row_spec(tm, w) for w in widths] + [_const_spec((PROJ_W, D)), _row_spec(tm, D), _const_spec((1, D)),
                                                       _row_spec(tm, D)] + [_ANY] * ns,
        out_specs=[_row_spec(tm, D), _const_spec((1, D))] + [_ANY] * ns,
        out_shape=[_sds((T, D), F32), _sds((1, D), F32)] + [_scattered_shape(p) for p in partials],
        scratch_shapes=_scatter_scratch(ns),
        compiler_params=_params(("arbitrary",)), name="in_proj_bwd",
    )(*pieces, w_in_p, x2, w_an, dh, *partials)


def _wgrad_pieces(pieces, b, name):
    T, k2 = b.shape
    tt = min(WGRAD_TOKEN_TILE, T)
    widths = [p.shape[1] for p in pieces]
    starts = [sum(widths[:i]) for i in range(len(widths))]
    k1 = sum(widths)

    def body(*refs):
        piece_refs, (b_ref, o_ref, acc_ref) = refs[:len(pieces)], refs[len(pieces):]
        t = pl.program_id(0)

        @pl.when(t == 0)
        def _():
            acc_ref[...] = jnp.zeros_like(acc_ref)

        bt = b_ref[...].astype(BF16)
        for ref, r0, width in zip(piece_refs, starts, widths):
            acc_ref[r0:r0 + width, :] += jnp.dot(ref[...].T, bt, preferred_element_type=F32)

        @pl.when(t == T // tt - 1)
        def _():
            o_ref[...] = acc_ref[...].astype(o_ref.dtype)

    return pl.pallas_call(
        body, grid=(T // tt,),
        in_specs=[pl.BlockSpec((tt, w), lambda t: (t, 0)) for w in widths] + [pl.BlockSpec((tt, k2), lambda t: (t, 0))],
        out_specs=_const_spec((k1, k2)), out_shape=_sds((k1, k2), BF16), scratch_shapes=[pltpu.VMEM((k1, k2), F32)],
        compiler_params=_params(("arbitrary",)), name=name,
    )(*pieces, b)


def _wgrad(a, b, name, column_shards=1, out_dtype=BF16):
    T, k1 = a.shape
    k2 = b.shape[1]
    per_shard = k2 // column_shards
    tt = min(WGRAD_TOKEN_TILE, T)
    b1 = k1
    while b1 * k2 * 4 > WGRAD_OUT_BYTES and b1 % 256 == 0:
        b1 //= 2
    step = _block(per_shard)

    def body(a_ref, b_ref, o_ref, acc_ref):
        t = pl.program_id(1)

        @pl.when(t == 0)
        def _():
            acc_ref[...] = jnp.zeros_like(acc_ref)

        a_t = a_ref[...].astype(BF16).T
        for c0 in range(0, k2, step):
            part = jnp.dot(a_t, b_ref[:, c0:c0 + step].astype(BF16), preferred_element_type=F32)
            if column_shards == 1:
                acc_ref[:, c0:c0 + step] += part
            else:
                acc_ref[c0 // per_shard, :, c0 % per_shard:c0 % per_shard + step] += part

        @pl.when(t == T // tt - 1)
        def _():
            o_ref[...] = acc_ref[...].astype(o_ref.dtype)

    if column_shards == 1:
        block, out_spec, out_shape = (b1, k2), pl.BlockSpec((b1, k2), lambda i, t: (i, 0)), _sds((k1, k2), out_dtype)
    else:
        block = (column_shards, b1, per_shard)
        out_spec, out_shape = pl.BlockSpec(block, lambda i, t: (0, i, 0)), _sds((column_shards, k1, per_shard), out_dtype)
    return pl.pallas_call(
        body, grid=(k1 // b1, T // tt),
        in_specs=[pl.BlockSpec((tt, b1), lambda i, t: (t, i)), pl.BlockSpec((tt, k2), lambda i, t: (t, 0))],
        out_specs=out_spec, out_shape=out_shape, scratch_shapes=[pltpu.VMEM(block, F32)],
        compiler_params=_params(("parallel", "arbitrary")), name=name,
    )(a, b)


@jax.custom_vjp
def _lane_blocks(x):
    return tuple(x[:, i:i + 128] for i in range(0, x.shape[1], 128))


_lane_blocks.defvjp(lambda x: (_lane_blocks(x), None), lambda _, g: (jnp.concatenate(g, axis=1),))


def _mla_pre_fn(q_lat, kv_lat, kpe, ln_q, ln_kv, w_q, w_kv, qn_n, qn_p, kn_n, kn_p, cos_f, sin_f):
    qn = _rms(q_lat, ln_q)
    kvn = _rms(kv_lat, ln_kv)
    kp = _rope(_rms(kpe, kn_p, ROPE_DIM), cos_f, sin_f)
    q_blocks = _lane_blocks(_bf_nn(qn, w_q))
    kv_blocks = _lane_blocks(_bf_nn(kvn, w_kv))
    outs = []
    for h in range(HEADS):
        outs.append(_rms(q_blocks[h], qn_n))
        outs.append(_rope(_rms(q_blocks[HEADS + h], qn_p, ROPE_DIM), cos_f, sin_f))
        outs.append(_rms(kv_blocks[h], kn_n))
        outs.append(kv_blocks[HEADS + h])
    return tuple(outs) + (kp,)


def _mla_pre_operands(lat_ref, pos_ref, ln_ref, w_ref, nw_ref, rope_ref):
    cos_f, sin_f = _rope_tables(pos_ref[...], rope_ref[0:1, :], rope_ref[1:2, :])
    side_by_side = lambda blocks: jnp.concatenate([w_ref[i].astype(F32) for i in blocks], axis=1)
    diff = (lat_ref[:, 0:LORA], lat_ref[:, LORA:2 * LORA], lat_ref[:, 2 * LORA:LAT_W], ln_ref[0:1, :], ln_ref[1:2, :],
            side_by_side(range(2 * HEADS)), side_by_side(range(2 * HEADS, 4 * HEADS)),
            nw_ref[0:1, :], nw_ref[1:2, :], nw_ref[2:3, :], nw_ref[3:4, :])
    return diff, cos_f, sin_f


def _mla_pre_fwd(lat, pos, ln_w, w_mla, nw, rope_rows):
    T = lat.shape[0]
    tm = min(TOKEN_TILE, T)

    def body(lat_ref, pos_ref, ln_ref, w_ref, nw_ref, rope_ref, q_ref, k_ref, v_ref):
        diff, cos_f, sin_f = _mla_pre_operands(lat_ref, pos_ref, ln_ref, w_ref, nw_ref, rope_ref)
        outs = _mla_pre_fn(*diff, cos_f, sin_f)
        kp = outs[-1].astype(BF16)
        for h in range(HEADS):
            q_n, q_p, k_n, v = outs[4 * h:4 * h + 4]
            q_ref[:, h * QK_PAD:h * QK_PAD + HEAD_DIM] = q_n.astype(BF16)
            q_ref[:, h * QK_PAD + HEAD_DIM:(h + 1) * QK_PAD] = q_p.astype(BF16)
            k_ref[:, h * QK_PAD:h * QK_PAD + HEAD_DIM] = k_n.astype(BF16)
            k_ref[:, h * QK_PAD + HEAD_DIM:(h + 1) * QK_PAD] = kp
            v_ref[:, h * HEAD_DIM:(h + 1) * HEAD_DIM] = v.astype(BF16)

    return pl.pallas_call(
        body, grid=(T // tm,),
        in_specs=[_row_spec(tm, LAT_W), _row_spec(tm, 1), _const_spec((2, LORA)), _const_spec((4 * HEADS, LORA, 128)),
                  _const_spec((8, 128)), _const_spec((8, 128))],
        out_specs=[_row_spec(tm, HEADS * QK_PAD), _row_spec(tm, HEADS * QK_PAD), _row_spec(tm, HEADS * HEAD_DIM)],
        out_shape=[_sds((T, HEADS * QK_PAD), BF16), _sds((T, HEADS * QK_PAD), BF16), _sds((T, HEADS * HEAD_DIM), BF16)],
        compiler_params=_params(("parallel",)), name="mla_pre_fwd",
    )(lat, pos, ln_w, w_mla, nw, rope_rows)


def _mla_pre_bwd(lat, pos, ln_w, w_mla, nw, rope_rows, dq, dk, dv, halves):
    T = lat.shape[0]
    tm = min(TOKEN_TILE, T)
    ns = len(halves)

    def body(*refs):
        lat_ref, pos_ref, ln_ref, w_ref, nw_ref, rope_ref, dq_ref, dk_ref, dv_ref = refs[:9]
        src_refs = refs[9:9 + ns]
        dlat_ref, dln_ref, dw_ref, dnw_ref = refs[9 + ns:13 + ns]
        dst_refs = refs[13 + ns:13 + 2 * ns]
        sems = refs[13 + 2 * ns:]

        @pl.when(pl.program_id(0) == 0)
        def _():
            for start in _swap_copies(src_refs, dst_refs, *sems)[0]:
                start()
            dln_ref[...] = jnp.zeros_like(dln_ref)
            dw_ref[...] = jnp.zeros_like(dw_ref)
            dnw_ref[...] = jnp.zeros_like(dnw_ref)

        diff, cos_f, sin_f = _mla_pre_operands(lat_ref, pos_ref, ln_ref, w_ref, nw_ref, rope_ref)
        _, pull = jax.vjp(lambda *a: _mla_pre_fn(*a, cos_f, sin_f), *diff)
        cts = []
        d_kp = jnp.zeros((tm, 128), F32)
        for h in range(HEADS):
            cts.append(dq_ref[:, h * QK_PAD:h * QK_PAD + HEAD_DIM])
            cts.append(dq_ref[:, h * QK_PAD + HEAD_DIM:(h + 1) * QK_PAD])
            cts.append(dk_ref[:, h * QK_PAD:h * QK_PAD + HEAD_DIM])
            cts.append(dv_ref[:, h * HEAD_DIM:(h + 1) * HEAD_DIM])
            d_kp += dk_ref[:, h * QK_PAD + HEAD_DIM:(h + 1) * QK_PAD]
        d_ql, d_kvl, d_kpe, d_lnq, d_lnkv, d_wq, d_wkv, d_qn_n, d_qn_p, d_kn_n, d_kn_p = pull(tuple(cts) + (d_kp,))
        d_w = [d[:, i:i + 128] for d in (d_wq, d_wkv) for i in range(0, d.shape[1], 128)]
        dlat_ref[:, 0:LORA] = d_ql.astype(BF16)
        dlat_ref[:, LORA:2 * LORA] = d_kvl.astype(BF16)
        dlat_ref[:, 2 * LORA:LAT_W] = d_kpe.astype(BF16)
        dln_ref[0:1, :] += d_lnq
        dln_ref[1:2, :] += d_lnkv
        for i in range(4 * HEADS):
            dw_ref[i] += d_w[i]
        for i, d in enumerate((d_qn_n, d_qn_p, d_kn_n, d_kn_p)):
            dnw_ref[i:i + 1, :] += d

        @pl.when(pl.program_id(0) == T // tm - 1)
        def _():
            for wait in _swap_copies(src_refs, dst_refs, *sems)[1]:
                wait()

    return pl.pallas_call(
        body, grid=(T // tm,),
        in_specs=[_row_spec(tm, LAT_W), _row_spec(tm, 1), _const_spec((2, LORA)), _const_spec((4 * HEADS, LORA, 128)),
                  _const_spec((8, 128)), _const_spec((8, 128)),
                  _row_spec(tm, HEADS * QK_PAD), _row_spec(tm, HEADS * QK_PAD), _row_spec(tm, HEADS * HEAD_DIM)] + [_ANY] * ns,
        out_specs=[_row_spec(tm, LAT_W), _const_spec((2, LORA)), _const_spec((4 * HEADS, LORA, 128)), _const_spec((8, 128))]
                  + [_ANY] * ns,
        out_shape=[_sds((T, LAT_W), BF16), _sds((2, LORA), F32), _sds((4 * HEADS, LORA, 128), F32), _sds((8, 128), F32)]
                  + [_swapped_shape(h) for h in halves],
        scratch_shapes=_swap_scratch(ns),
        compiler_params=_params(("arbitrary",)), name="mla_pre_bwd",
    )(lat, pos, ln_w, w_mla, nw, rope_rows, dq, dk, dv, *halves)


def _causal_mask(i, j, tq, tk):
    row = i * tq + lax.broadcasted_iota(jnp.int32, (tq, tk), 0)
    col = j * tk + lax.broadcasted_iota(jnp.int32, (tq, tk), 1)
    return col <= row


def _attn_fwd(q, k, v, shards):
    B, S, _ = q.shape
    t = min(ATTN_TILE, S)
    nq = S // t
    ns = len(shards)

    hp = ATTN_HEADS_PER_STEP
    qk = lambda h: slice(h * QK_PAD, (h + 1) * QK_PAD)
    vd = lambda h: slice(h * HEAD_DIM, (h + 1) * HEAD_DIM)

    def body(*refs):
        q_ref, k_ref, v_ref = refs[:3]
        src_refs = refs[3:3 + ns]
        o_ref, lse_ref = refs[3 + ns:5 + ns]
        dst_refs = refs[5 + ns:5 + 2 * ns]
        sems = refs[5 + 2 * ns:]
        b, g, i = pl.program_id(0), pl.program_id(1), pl.program_id(2)
        qb = [q_ref[0, :, qk(h)] for h in range(hp)]

        step_no = (b * (HEADS // hp) + g) * nq + i
        for phase, at in enumerate((0, (3 * B * (HEADS // hp) * nq) // 4)):
            @pl.when(step_no == at)
            def _(phase=phase):
                for call in _gather_copies(src_refs, dst_refs, *sems)[phase]:
                    call()

        def step(j, carry, diagonal):
            rows = pl.ds(pl.multiple_of(j * t, t), t)
            s = [_dg(qb[h], k_ref[0, rows, qk(h)], 1, 1, None) * ATTN_SCALE for h in range(hp)]
            if diagonal:
                keep = _causal_mask(0, 0, t, t)
                s = [jnp.where(keep, x, -1e30) for x in s]
            m_new = [jnp.maximum(carry[h][0], jnp.max(s[h], axis=-1, keepdims=True)) for h in range(hp)]
            p = [jnp.exp(s[h] - m_new[h]) for h in range(hp)]
            alpha = [jnp.exp(carry[h][0] - m_new[h]) for h in range(hp)]
            l = [alpha[h] * carry[h][1] + jnp.sum(p[h], axis=-1, keepdims=True) for h in range(hp)]
            pv = [jnp.dot(p[h].astype(BF16), v_ref[0, rows, vd(h)], preferred_element_type=F32) for h in range(hp)]
            return tuple((m_new[h], l[h], alpha[h] * carry[h][2] + pv[h]) for h in range(hp))

        init = tuple((jnp.full((t, 1), -1e30, F32), jnp.zeros((t, 1), F32), jnp.zeros((t, HEAD_DIM), F32)) for _ in range(hp))
        below = lax.fori_loop(0, i, lambda j, carry: step(j, carry, False), init)
        for h, (m, l, acc) in enumerate(step(i, below, True)):
            o_ref[0, :, vd(h)] = acc / l
            lse_ref[0, h, 0] = (m + jnp.log(l)).T

        @pl.when((b == B - 1) & (g == HEADS // hp - 1) & (i == nq - 1))
        def _():
            for wait in _gather_copies(src_refs, dst_refs, *sems)[2]:
                wait()

    return pl.pallas_call(
        body, grid=(B, HEADS // hp, nq),
        in_specs=[pl.BlockSpec((1, t, hp * QK_PAD), lambda b, g, i: (b, i, g)),
                  pl.BlockSpec((1, S, hp * QK_PAD), lambda b, g, i: (b, 0, g)),
                  pl.BlockSpec((1, S, hp * HEAD_DIM), lambda b, g, i: (b, 0, g))] + [_ANY] * ns,
        out_specs=[pl.BlockSpec((1, t, hp * HEAD_DIM), lambda b, g, i: (b, i, g)),
                   pl.BlockSpec((1, hp, 1, 1, t), lambda b, g, i: (b, g, i, 0, 0))] + [_ANY] * ns,
        out_shape=[_sds((B, S, HEADS * HEAD_DIM), F32), _sds((B, HEADS, nq, 1, t), F32)] + [_sds((4,) + s.shape, s.dtype) for s in shards],
        scratch_shapes=_gather_scratch(ns),
        compiler_params=_params(("arbitrary", "arbitrary", "arbitrary")), name="attn_fwd",
    )(q, k, v, *shards)


def _attn_bwd(q, k, v, o, lse, do, partials):
    B, S, _ = q.shape
    t = min(ATTN_TILE, S)
    nq = S // t
    ns = len(partials)

    hp = ATTN_HEADS_PER_STEP
    qk = lambda h: slice(h * QK_PAD, (h + 1) * QK_PAD)
    vd = lambda h: slice(h * HEAD_DIM, (h + 1) * HEAD_DIM)
    heads = range(hp)

    def body(*refs):
        q_ref, k_ref, v_ref, o_ref, lse_ref, do_ref = refs[:6]
        src_refs = refs[6:6 + ns]
        dq_ref, dk_ref, dv_ref = refs[6 + ns:9 + ns]
        dst_refs = refs[9 + ns:9 + 2 * ns]
        dsum_ref, send_sems, recv_sems, local_sems = refs[9 + 2 * ns:]
        b, g, j = pl.program_id(0), pl.program_id(1), pl.program_id(2)

        @pl.when((b == 0) & (g == 0) & (j == 0))
        def _():
            for start in _scatter_copies(src_refs, dst_refs, send_sems, recv_sems, local_sems)[0]:
                start()

        @pl.when(j == 0)
        def _():
            dq_ref[...] = jnp.zeros_like(dq_ref)
            for h in heads:
                for blk in range(nq):
                    rows = slice(blk * t, (blk + 1) * t)
                    dsum_ref[h, blk] = jnp.sum(do_ref[0, rows, vd(h)] * o_ref[0, rows, vd(h)], axis=-1, keepdims=True).T

        kb = [k_ref[0, :, qk(h)] for h in heads]
        vb = [v_ref[0, :, vd(h)] for h in heads]

        def step(i, carry, diagonal):
            rows = pl.ds(pl.multiple_of(i * t, t), t)
            qb = [q_ref[0, rows, qk(h)] for h in heads]
            dob = [do_ref[0, rows, vd(h)].astype(BF16) for h in heads]
            s = [_dg(kb[h], qb[h], 1, 1, None) * ATTN_SCALE for h in heads]
            p = [jnp.exp(s[h] - lse_ref[0, h, i]) for h in heads]
            if diagonal:
                key = lax.broadcasted_iota(jnp.int32, (t, t), 0)
                query = lax.broadcasted_iota(jnp.int32, (t, t), 1)
                p = [jnp.where(key <= query, x, 0.0) for x in p]
            dp = [_dg(vb[h], dob[h], 1, 1, None) for h in heads]
            dv = [carry[h][1] + jnp.dot(p[h].astype(BF16), dob[h], preferred_element_type=F32) for h in heads]
            ds = [(p[h] * (dp[h] - dsum_ref[h, i]) * ATTN_SCALE).astype(BF16) for h in heads]
            for h in heads:
                dq_ref[0, rows, qk(h)] += _dg(ds[h], kb[h], 0, 0, None)
            return tuple((carry[h][0] + jnp.dot(ds[h], qb[h], preferred_element_type=F32), dv[h]) for h in heads)

        zeros = tuple((jnp.zeros((t, QK_PAD), F32), jnp.zeros((t, HEAD_DIM), F32)) for _ in heads)
        on_diagonal = step(j, zeros, True)
        done = lax.fori_loop(j + 1, nq, lambda i, carry: step(i, carry, False), on_diagonal)
        for h, (dk, dv) in enumerate(done):
            dk_ref[0, :, qk(h)] = dk
            dv_ref[0, :, vd(h)] = dv

        @pl.when((b == B - 1) & (g == HEADS // hp - 1) & (j == nq - 1))
        def _():
            for wait in _scatter_copies(src_refs, dst_refs, send_sems, recv_sems, local_sems)[1]:
                wait()

    return pl.pallas_call(
        body, grid=(B, HEADS // hp, nq),
        in_specs=[pl.BlockSpec((1, S, hp * QK_PAD), lambda b, g, j: (b, 0, g)),
                  pl.BlockSpec((1, t, hp * QK_PAD), lambda b, g, j: (b, j, g)),
                  pl.BlockSpec((1, t, hp * HEAD_DIM), lambda b, g, j: (b, j, g)),
                  pl.BlockSpec((1, S, hp * HEAD_DIM), lambda b, g, j: (b, 0, g)),
                  pl.BlockSpec((1, hp, nq, 1, t), lambda b, g, j: (b, g, 0, 0, 0)),
                  pl.BlockSpec((1, S, hp * HEAD_DIM), lambda b, g, j: (b, 0, g))] + [_ANY] * ns,
        out_specs=[pl.BlockSpec((1, S, hp * QK_PAD), lambda b, g, j: (b, 0, g)),
                   pl.BlockSpec((1, t, hp * QK_PAD), lambda b, g, j: (b, j, g)),
                   pl.BlockSpec((1, t, hp * HEAD_DIM), lambda b, g, j: (b, j, g))] + [_ANY] * ns,
        out_shape=[_sds((B, S, HEADS * QK_PAD), F32), _sds((B, S, HEADS * QK_PAD), F32), _sds((B, S, HEADS * HEAD_DIM), F32)]
                  + [_scattered_shape(p) for p in partials],
        scratch_shapes=[pltpu.VMEM((hp, nq, 1, t), F32)] + _scatter_scratch(ns),
        compiler_params=_params(("arbitrary", "arbitrary", "arbitrary")), name="attn_bwd",
    )(q, k, v, o, lse, do, *partials)


def _gdn_pre_fn(xq, xk, xv, wq, wk, wv, keeps):
    def conv_silu(x, w):
        acc = x * w[3]
        for s in (1, 2, 3):
            acc = acc + _shift_rows(x, keeps[s - 1], s) * w[3 - s]
        return _silu(acc)

    def l2(x):
        return x * lax.rsqrt(jnp.sum(x * x, axis=-1, keepdims=True) + EPS)

    return l2(conv_silu(xq, wq)) * (HEAD_DIM ** -0.5), l2(conv_silu(xk, wk)), conv_silu(xv, wv)


def _gdn_pre_specs(S):
    x_specs = [pl.BlockSpec((1, S, HEAD_DIM), lambda h, b, g=g: (b, 0, g * HEADS + h)) for g in range(3)]
    w_specs = [pl.BlockSpec((CONV_TAPS, HEAD_DIM), lambda h, b, g=g: (0, g * HEADS + h)) for g in range(3)]
    out_spec = pl.BlockSpec((1, S, HEAD_DIM), lambda h, b: (b, 0, h))
    return x_specs, w_specs, out_spec


def _row_keeps(S):
    t = lax.broadcasted_iota(jnp.int32, (S, HEAD_DIM), 0)
    return [(t >= s).astype(F32) for s in (1, 2, 3)]


def _gdn_pre_fwd(gqkv, conv_w):
    B, S, _ = gqkv.shape
    x_specs, w_specs, out_spec = _gdn_pre_specs(S)

    def body(xq_ref, xk_ref, xv_ref, wq_ref, wk_ref, wv_ref, q_ref, k_ref, v_ref):
        taps = [[w[i:i + 1, :] for i in range(CONV_TAPS)] for w in (wq_ref, wk_ref, wv_ref)]
        q, k, v = _gdn_pre_fn(xq_ref[0], xk_ref[0], xv_ref[0], *taps, _row_keeps(S))
        q_ref[0], k_ref[0], v_ref[0] = q, k, v

    return pl.pallas_call(
        body, grid=(HEADS, B), in_specs=x_specs + w_specs, out_specs=[out_spec] * 3,
        out_shape=[_sds((B, S, HEADS * HEAD_DIM), F32)] * 3,
        compiler_params=_params(("parallel", "parallel")), name="gdn_pre_fwd",
    )(gqkv, gqkv, gqkv, conv_w, conv_w, conv_w)


def _gdn_pre_bwd(gqkv, conv_w, dq, dk, dv, halves):
    B, S, _ = gqkv.shape
    x_specs, w_specs, out_spec = _gdn_pre_specs(S)
    dw_spec = pl.BlockSpec((CONV_TAPS, HEAD_DIM), lambda h, b: (0, h))
    ns = len(halves)

    def body(*refs):
        xq_ref, xk_ref, xv_ref, wq_ref, wk_ref, wv_ref, dq_ref, dk_ref, dv_ref = refs[:9]
        src_refs = refs[9:9 + ns]
        dxq_ref, dxk_ref, dxv_ref, dwq_ref, dwk_ref, dwv_ref = refs[9 + ns:15 + ns]
        dst_refs = refs[15 + ns:15 + 2 * ns]
        sems = refs[15 + 2 * ns:]
        first = (pl.program_id(0) == 0) & (pl.program_id(1) == 0)
        last = (pl.program_id(0) == HEADS - 1) & (pl.program_id(1) == B - 1)

        @pl.when(first)
        def _():
            for start in _swap_copies(src_refs, dst_refs, *sems)[0]:
                start()

        @pl.when(pl.program_id(1) == 0)
        def _():
            for r in (dwq_ref, dwk_ref, dwv_ref):
                r[...] = jnp.zeros_like(r)

        taps = [[w[i:i + 1, :] for i in range(CONV_TAPS)] for w in (wq_ref, wk_ref, wv_ref)]
        keeps = _row_keeps(S)
        _, pull = jax.vjp(lambda *a: _gdn_pre_fn(*a, keeps), xq_ref[0], xk_ref[0], xv_ref[0], *taps)
        dxq, dxk, dxv, dwq, dwk, dwv = pull((dq_ref[0], dk_ref[0], dv_ref[0]))
        dxq_ref[0], dxk_ref[0], dxv_ref[0] = dxq.astype(BF16), dxk.astype(BF16), dxv.astype(BF16)
        for ref, dw in ((dwq_ref, dwq), (dwk_ref, dwk), (dwv_ref, dwv)):
            for i in range(CONV_TAPS):
                ref[i:i + 1, :] += dw[i]

        @pl.when(last)
        def _():
            for wait in _swap_copies(src_refs, dst_refs, *sems)[1]:
                wait()

    hw = HEADS * HEAD_DIM
    return pl.pallas_call(
        body, grid=(HEADS, B), in_specs=x_specs + w_specs + [out_spec] * 3 + [_ANY] * ns,
        out_specs=[out_spec] * 3 + [dw_spec] * 3 + [_ANY] * ns,
        out_shape=[_sds((B, S, hw), BF16)] * 3 + [_sds((CONV_TAPS, hw), F32)] * 3 + [_swapped_shape(h) for h in halves],
        scratch_shapes=_swap_scratch(ns),
        compiler_params=_params(("arbitrary", "arbitrary")), name="gdn_pre_bwd",
    )(gqkv, gqkv, gqkv, conv_w, conv_w, conv_w, dq, dk, dv, *halves)


def _chunk_masks():
    i = lax.broadcasted_iota(jnp.int32, (CHUNK, CHUNK), 0)
    j = lax.broadcasted_iota(jnp.int32, (CHUNK, CHUNK), 1)
    lower, after = (j <= i).astype(F32), (j > i).astype(F32)
    return {"le": lower, "le_gt": jnp.concatenate([lower, after], axis=0), "strict": (j < i).astype(F32)}


def _gdn_chunk_fn(groups, masks, solve=_unit_lower_solve):
    lane = lax.broadcasted_iota(jnp.int32, (groups, 1, 128), 2)
    head = lax.broadcasted_iota(jnp.int32, (groups, 1, 128), 0) % HEADS
    pick_a, pick_b = (lane == head).astype(F32), (lane == head + HEADS).astype(F32)
    lower, lower_after, strict = (jnp.broadcast_to(masks[n], (groups,) + masks[n].shape) for n in ("le", "le_gt", "strict"))
    ones_row = jnp.ones((1, 1, HEAD_DIM), F32)

    def f(q, k, v, gab, a_row, dt_row, state):
        ga = jnp.sum(gab * pick_a, axis=2, keepdims=True)
        gb = jnp.sum(gab * pick_b, axis=2, keepdims=True)
        a_log = jnp.sum(a_row * pick_a, axis=2, keepdims=True)
        dt_bias = jnp.sum(dt_row * pick_a, axis=2, keepdims=True)
        beta = _sigmoid(gb)
        g = -jnp.exp(a_log) * _softplus(ga + dt_bias)
        g_wide = g * ones_row
        cum, rest = _row_halves(_hi_nn(lower_after, g_wide))
        total = jnp.sum(g_wide, axis=1, keepdims=True)
        diff = _hi_nn(lower, g * strict)
        decay = lower * jnp.exp(diff)
        e_cum = jnp.exp(cum)
        kk, qk = _row_halves(_bf_nt(jnp.concatenate([k, q], axis=1), k))
        lmat = strict * (beta * kk * decay)
        u, w = _lane_halves(solve(lmat, jnp.concatenate([v * beta, k * (beta * e_cum)], axis=2)))
        w_state, q_state = _row_halves(_bf_nn(jnp.concatenate([w, q * e_cum], axis=1), state))
        v_new = u - w_state
        o = q_state + _bf_nn(qk * decay, v_new)
        new_state = state * jnp.exp(total) + _bf_tn(k * jnp.exp(rest), v_new)
        return o, new_state

    return f


def _gdn_chunk_fwd(q, k, v, gab, scal, shards):
    B, S, W = q.shape
    N = S // CHUNK
    ns = len(shards)

    def body(*refs):
        q_ref, k_ref, v_ref, gab_ref, sc_ref = refs[:5]
        src_refs = refs[5:5 + ns]
        o_ref, st_ref, pw_ref, sol_ref = refs[5 + ns:9 + ns]
        dst_refs = refs[9 + ns:9 + 2 * ns]
        state_ref, send_sems, recv_sems, local_sems = refs[9 + 2 * ns:]
        n = pl.program_id(0)
        kept = {}

        @pl.when(n == 0)
        def _():
            for start in _gather_copies(src_refs, dst_refs, send_sems, recv_sems, local_sems)[0]:
                start()
            state_ref[...] = jnp.zeros_like(state_ref)

        @pl.when(n == (2 * N) // 3)
        def _():
            for pass_on in _gather_copies(src_refs, dst_refs, send_sems, recv_sems, local_sems)[1]:
                pass_on()

        groups = [(b, h) for b in range(B) for h in range(HEADS)]
        gather = lambda ref: jnp.stack([ref[b, :, h * HEAD_DIM:(h + 1) * HEAD_DIM] for b, h in groups])
        state = state_ref[...]
        for i, (b, h) in enumerate(groups):
            st_ref[b, 0, h] = state[i]
        def solve_and_keep(lmat, rhs):
            kept["x"], (_, kept["powers"], _) = _unit_lower_solve_fwd(lmat, rhs)
            return kept["x"]

        o, new_state = _gdn_chunk_fn(len(groups), _chunk_masks(), solve_and_keep)(
            gather(q_ref), gather(k_ref), gather(v_ref), jnp.stack([gab_ref[b] for b, _ in groups]), sc_ref[0:1, :], sc_ref[1:2, :], state)
        for i, (b, h) in enumerate(groups):
            o_ref[b, :, h * HEAD_DIM:(h + 1) * HEAD_DIM] = o[i]
            sol_ref[b, 0, h] = kept["x"][i]
            for p, power in enumerate(kept["powers"]):
                pw_ref[b, 0, h, p] = power[i]
        state_ref[...] = new_state

        @pl.when(n == N - 1)
        def _():
            for wait in _gather_copies(src_refs, dst_refs, send_sems, recv_sems, local_sems)[2]:
                wait()

    seq = pl.BlockSpec((B, CHUNK, W), lambda n: (0, n, 0))
    return pl.pallas_call(
        body, grid=(N,),
        in_specs=[seq, seq, seq, pl.BlockSpec((B, CHUNK, GAB_W), lambda n: (0, n, 0)), _const_spec((8, 128))] + [_ANY] * ns,
        out_specs=[seq, pl.BlockSpec((B, 1, HEADS, HEAD_DIM, HEAD_DIM), lambda n: (0, n, 0, 0, 0)),
                   pl.BlockSpec((B, 1, HEADS, SOLVE_POWERS, CHUNK, CHUNK), lambda n: (0, n, 0, 0, 0, 0)),
                   pl.BlockSpec((B, 1, HEADS, CHUNK, 2 * HEAD_DIM), lambda n: (0, n, 0, 0, 0))] + [_ANY] * ns,
        out_shape=[_sds((B, S, W), F32), _sds((B, N, HEADS, HEAD_DIM, HEAD_DIM), F32),
                   _sds((B, N, HEADS, SOLVE_POWERS, CHUNK, CHUNK), F32), _sds((B, N, HEADS, CHUNK, 2 * HEAD_DIM), F32)]
                  + [_sds((4,) + s.shape, s.dtype) for s in shards],
        scratch_shapes=[pltpu.VMEM((B * HEADS, HEAD_DIM, HEAD_DIM), F32)] + _gather_scratch(ns),
        compiler_params=_params(("arbitrary",)), name="gdn_chunk_fwd",
    )(q, k, v, gab, scal, *shards)


def _gdn_chunk_bwd(q, k, v, gab, scal, states, powers, solutions, do, partials):
    B, S, W = q.shape
    N = S // CHUNK
    ns = len(partials)

    def body(*refs):
        q_ref, k_ref, v_ref, gab_ref, sc_ref, st_ref, pw_ref, sol_ref, do_ref = refs[:9]
        src_refs = refs[9:9 + ns]
        dq_ref, dk_ref, dv_ref, dgab_ref, dsc_ref = refs[9 + ns:14 + ns]
        dst_refs = refs[14 + ns:14 + 2 * ns]
        dstate_ref, send_sems, recv_sems, local_sems = refs[14 + 2 * ns:]
        n = pl.program_id(0)

        @pl.when(n == 0)
        def _():
            for start in _scatter_copies(src_refs, dst_refs, send_sems, recv_sems, local_sems)[0]:
                start()
            dstate_ref[...] = jnp.zeros_like(dstate_ref)
            dsc_ref[...] = jnp.zeros_like(dsc_ref)

        groups = [(b, h) for b in range(B) for h in range(HEADS)]
        gather = lambda ref: jnp.stack([ref[b, :, h * HEAD_DIM:(h + 1) * HEAD_DIM] for b, h in groups])
        kept_powers = [jnp.stack([pw_ref[b, 0, h, p] for b, h in groups]) for p in range(SOLVE_POWERS)]
        kept_x = jnp.stack([sol_ref[b, 0, h] for b, h in groups])
        solve = lambda lmat, rhs: _unit_lower_solve_kept(lmat, rhs, kept_powers, kept_x)
        _, pull = jax.vjp(_gdn_chunk_fn(len(groups), _chunk_masks(), solve), gather(q_ref), gather(k_ref), gather(v_ref),
                          jnp.stack([gab_ref[b] for b, _ in groups]), sc_ref[0:1, :], sc_ref[1:2, :],
                          jnp.stack([st_ref[b, 0, h] for b, h in groups]))
        dq, dk, dv, dg, d_a, d_dt, dstate = pull((gather(do_ref), dstate_ref[...]))
        for i, (b, h) in enumerate(groups):
            lanes = slice(h * HEAD_DIM, (h + 1) * HEAD_DIM)
            dq_ref[b, :, lanes] = dq[i]
            dk_ref[b, :, lanes] = dk[i]
            dv_ref[b, :, lanes] = dv[i]
        for b in range(B):
            dgab_ref[b] = sum(dg[b * HEADS + h] for h in range(HEADS)).astype(BF16)
        dstate_ref[...] = dstate
        dsc_ref[0:1, :] += d_a
        dsc_ref[1:2, :] += d_dt

        @pl.when(n == N - 1)
        def _():
            for wait in _scatter_copies(src_refs, dst_refs, send_sems, recv_sems, local_sems)[1]:
                wait()

    seq = pl.BlockSpec((B, CHUNK, W), lambda n: (0, N - 1 - n, 0))
    gab_spec = pl.BlockSpec((B, CHUNK, GAB_W), lambda n: (0, N - 1 - n, 0))
    return pl.pallas_call(
        body, grid=(N,),
        in_specs=[seq, seq, seq, gab_spec, _const_spec((8, 128)),
                  pl.BlockSpec((B, 1, HEADS, HEAD_DIM, HEAD_DIM), lambda n: (0, N - 1 - n, 0, 0, 0)),
                  pl.BlockSpec((B, 1, HEADS, SOLVE_POWERS, CHUNK, CHUNK), lambda n: (0, N - 1 - n, 0, 0, 0, 0)),
                  pl.BlockSpec((B, 1, HEADS, CHUNK, 2 * HEAD_DIM), lambda n: (0, N - 1 - n, 0, 0, 0)), seq] + [_ANY] * ns,
        out_specs=[seq, seq, seq, gab_spec, _const_spec((8, 128))] + [_ANY] * ns,
        out_shape=[_sds((B, S, W), F32)] * 3 + [_sds((B, S, GAB_W), BF16), _sds((8, 128), F32)] + [_scattered_shape(p) for p in partials],
        scratch_shapes=[pltpu.VMEM((B * HEADS, HEAD_DIM, HEAD_DIM), F32)] + _scatter_scratch(ns),
        compiler_params=_params(("arbitrary",)), name="gdn_chunk_bwd",
    )(q, k, v, gab, scal, states, powers, solutions, do, *partials)


def _mix_fn(ao, go, gz, w_mla, w_gdn):
    return tuple(_rms(ao[h], w_mla[h]) for h in range(HEADS)) + tuple(_rms(go[h], w_gdn) * _silu(gz[h]) for h in range(HEADS))


def _mix_operands(ao_ref, go_ref, gz_ref, nw_ref):
    blocks = lambda ref: [ref[:, h * HEAD_DIM:(h + 1) * HEAD_DIM] for h in range(HEADS)]
    return blocks(ao_ref), blocks(go_ref), blocks(gz_ref), [nw_ref[h:h + 1, :] for h in range(HEADS)], nw_ref[HEADS:HEADS + 1, :]


def _mix_fwd(ao, go, gz, nw, w_out, x2):
    T, D = x2.shape
    tm = min(TOKEN_TILE, T)
    MW = 2 * HEADS * HEAD_DIM

    def body(ao_ref, go_ref, gz_ref, nw_ref, w_ref, x_ref, mix_ref, h_ref):
        outs = _mix_fn(*_mix_operands(ao_ref, go_ref, gz_ref, nw_ref))
        for i, piece in enumerate(outs):
            mix_ref[:, i * HEAD_DIM:(i + 1) * HEAD_DIM] = piece.astype(BF16)
        h_ref[...] = x_ref[...] + jnp.dot(mix_ref[...], w_ref[...], preferred_element_type=F32)

    half = HEADS * HEAD_DIM
    return pl.pallas_call(
        body, grid=(T // tm,),
        in_specs=[_row_spec(tm, half), _row_spec(tm, half), _row_spec(tm, half), _const_spec((8, 128)), _const_spec((MW, D)),
                  _row_spec(tm, D)],
        out_specs=[_row_spec(tm, MW), _row_spec(tm, D)],
        out_shape=[_sds((T, MW), BF16), _sds((T, D), F32)],
        compiler_params=_params(("parallel",)), name="mix_fwd",
    )(ao, go, gz, nw, w_out, x2)


def _mix_bwd(ao, go, gz, nw, w_out, dh):
    T, D = dh.shape
    tm = min(TOKEN_TILE, T)
    MW = 2 * HEADS * HEAD_DIM
    half = HEADS * HEAD_DIM

    def body(ao_ref, go_ref, gz_ref, nw_ref, w_ref, dh_ref, dao_ref, dgo_ref, dgz_ref, dnw_ref):
        @pl.when(pl.program_id(0) == 0)
        def _():
            dnw_ref[...] = jnp.zeros_like(dnw_ref)

        d_mix = _dg(dh_ref[...].astype(BF16), w_ref[...], 1, 1, None)
        cts = tuple(d_mix[:, i * HEAD_DIM:(i + 1) * HEAD_DIM] for i in range(2 * HEADS))
        _, pull = jax.vjp(_mix_fn, *_mix_operands(ao_ref, go_ref, gz_ref, nw_ref))
        d_ao, d_go, d_gz, d_wm, d_wg = pull(cts)
        for h in range(HEADS):
            lanes = slice(h * HEAD_DIM, (h + 1) * HEAD_DIM)
            dao_ref[:, lanes] = d_ao[h]
            dgo_ref[:, lanes] = d_go[h]
            dgz_ref[:, lanes] = d_gz[h].astype(BF16)
            dnw_ref[h:h + 1, :] += d_wm[h]
        dnw_ref[HEADS:HEADS + 1, :] += d_wg

    return pl.pallas_call(
        body, grid=(T // tm,),
        in_specs=[_row_spec(tm, half), _row_spec(tm, half), _row_spec(tm, half), _const_spec((8, 128)), _const_spec((MW, D)),
                  _row_spec(tm, D)],
        out_specs=[_row_spec(tm, half)] * 3 + [_const_spec((8, 128))],
        out_shape=[_sds((T, half), F32)] * 2 + [_sds((T, half), BF16), _sds((8, 128), F32)],
        compiler_params=_params(("arbitrary",)), name="mix_bwd",
    )(ao, go, gz, nw, w_out, dh)


def _up_spec(w_up, tf):
    per_shard = w_up.shape[2] // tf
    return pl.BlockSpec((None, w_up.shape[1], tf), lambda i, j: (j // per_shard, 0, j % per_shard))


def _mlp_fwd(h2, w_mn, w_up, w_down, target):
    T, D = h2.shape
    FF = w_down.shape[0]
    tm, tf = min(MLP_TOKEN_TILE, T), min(FF_TILE, w_up.shape[2])
    nf = FF // tf

    def body(h_ref, wn_ref, wu_ref, wd_ref, t_ref, hn_ref, act_ref, dy_ref, sq_ref, acc_ref):
        j = pl.program_id(1)

        @pl.when(j == 0)
        def _():
            hn_ref[...] = _rms(h_ref[...], wn_ref[...]).astype(BF16)
            acc_ref[...] = jnp.zeros_like(acc_ref)

        up = jnp.dot(hn_ref[...], wu_ref[...], preferred_element_type=F32)
        act = jnp.square(jnp.maximum(up, 0.0)).astype(BF16)
        act_ref[...] = act
        acc_ref[...] += jnp.dot(act, wd_ref[...], preferred_element_type=F32)

        @pl.when(j == nf - 1)
        def _():
            err = h_ref[...] + acc_ref[...] - t_ref[...]
            dy_ref[...] = err * (1.0 / D)
            sq_ref[...] = jnp.zeros_like(sq_ref) + jnp.sum(err * err)

    tok = lambda w: pl.BlockSpec((tm, w), lambda i, j: (i, 0))
    return pl.pallas_call(
        body, grid=(T // tm, nf),
        in_specs=[tok(D), _const_spec((1, D)), _up_spec(w_up, tf), pl.BlockSpec((tf, D), lambda i, j: (j, 0)), tok(D)],
        out_specs=[tok(D), pl.BlockSpec((tm, tf), lambda i, j: (i, j)), tok(D), pl.BlockSpec((1, 8, 128), lambda i, j: (i, 0, 0))],
        out_shape=[_sds((T, D), BF16), _sds((T, FF), BF16), _sds((T, D), F32), _sds((T // tm, 8, 128), F32)],
        scratch_shapes=[pltpu.VMEM((tm, D), F32)],
        compiler_params=_params(("parallel", "arbitrary")), name="mlp_fwd",
    )(h2, w_mn, w_up, w_down, target)


def _mlp_bwd(h2, w_mn, act, w_up, w_down, dy):
    T, D = h2.shape
    FF = w_down.shape[0]
    tm, tf = min(MLP_TOKEN_TILE, T), min(FF_TILE, w_up.shape[2])
    nf = FF // tf

    def body(h_ref, wn_ref, act_ref, wu_ref, wd_ref, dy_ref, dh_ref, dup_ref, dwn_ref, acc_ref, dyb_ref):
        i, j = pl.program_id(0), pl.program_id(1)

        @pl.when((i == 0) & (j == 0))
        def _():
            dwn_ref[...] = jnp.zeros_like(dwn_ref)

        @pl.when(j == 0)
        def _():
            acc_ref[...] = jnp.zeros_like(acc_ref)
            dyb_ref[...] = dy_ref[...].astype(BF16)

        r = jnp.sqrt(act_ref[...].astype(F32))
        d_act = _dg(dyb_ref[...], wd_ref[...], 1, 1, None)
        d_up = (d_act * (2.0 * r)).astype(BF16)
        dup_ref[...] = d_up
        acc_ref[...] += _dg(d_up, wu_ref[...], 1, 1, None)

        @pl.when(j == nf - 1)
        def _():
            _, pull = jax.vjp(_rms, h_ref[...], wn_ref[...])
            dh, dwn = pull(acc_ref[...])
            dh_ref[...] = dh + dy_ref[...]
            dwn_ref[...] += dwn

    tok = lambda w: pl.BlockSpec((tm, w), lambda i, j: (i, 0))
    ff = pl.BlockSpec((tm, tf), lambda i, j: (i, j))
    return pl.pallas_call(
        body, grid=(T // tm, nf),
        in_specs=[tok(D), _const_spec((1, D)), ff, _up_spec(w_up, tf), pl.BlockSpec((tf, D), lambda i, j: (j, 0)), tok(D)],
        out_specs=[tok(D), ff, _const_spec((1, D))],
        out_shape=[_sds((T, D), F32), _sds((T, FF), BF16), _sds((1, D), F32)],
        scratch_shapes=[pltpu.VMEM((tm, D), F32), pltpu.VMEM((tm, D), BF16)],
        compiler_params=_params(("arbitrary", "arbitrary")), name="mlp_bwd",
    )(h2, w_mn, act, w_up, w_down, dy)


def _rope_pad(a):
    z = jnp.zeros(a.shape[:-1] + (ROPE_HALF,), a.dtype)
    return jnp.concatenate([a[..., :ROPE_HALF], z, a[..., ROPE_HALF:], z], axis=-1)


def _rope_unpad(a):
    return jnp.concatenate([a[..., :ROPE_HALF], a[..., 2 * ROPE_HALF:3 * ROPE_HALF]], axis=-1)


_G0 = 2 * LORA + ROPE_DIM
W_IN_COLS = _G0 + GQKV_W + GZ_W + 2 * HEADS


def _widen_w_in_t(w_t):
    z = jnp.zeros((ROPE_HALF, w_t.shape[1]), w_t.dtype)
    pad = jnp.zeros((GAB_W - 2 * HEADS, w_t.shape[1]), w_t.dtype)
    return jnp.concatenate([w_t[:2 * LORA + ROPE_HALF], z, w_t[2 * LORA + ROPE_HALF:_G0], z, w_t[_G0:], pad], axis=0)


def _narrow_w_in_t(w_t):
    return jnp.concatenate([w_t[:2 * LORA + ROPE_HALF], w_t[2 * LORA + 2 * ROPE_HALF:2 * LORA + 3 * ROPE_HALF],
                            w_t[LAT_W:LAT_W + W_IN_COLS - _G0]], axis=0)


def _stack_mla(w_uq, w_ukv):
    uq = w_uq.reshape(LORA, HEADS, QK_DIM)
    ukv = w_ukv.reshape(LORA, HEADS, 2 * HEAD_DIM)
    parts = [uq[:, :, :HEAD_DIM], _rope_pad(uq[:, :, HEAD_DIM:]), ukv[:, :, :HEAD_DIM], ukv[:, :, HEAD_DIM:]]
    return jnp.concatenate([p.transpose(1, 0, 2) for p in parts], axis=0)


def _unstack_mla(w):
    p = [w[i * HEADS:(i + 1) * HEADS].transpose(1, 0, 2) for i in range(4)]
    uq = jnp.concatenate([p[0], _rope_unpad(p[1])], axis=-1).reshape(LORA, HEADS * QK_DIM)
    ukv = jnp.concatenate([p[2], p[3]], axis=-1).reshape(LORA, HEADS * 2 * HEAD_DIM)
    return uq, ukv


def _rows8(rows):
    a = jnp.concatenate(rows, axis=0)
    return jnp.pad(a, ((0, 8 - a.shape[0]), (0, 0)))


def _qk_norm_rows(q_norm_w, k_norm_w):
    return _rows8([q_norm_w[:, :HEAD_DIM], _rope_pad(q_norm_w[:, HEAD_DIM:]), k_norm_w[:, :HEAD_DIM], _rope_pad(k_norm_w[:, HEAD_DIM:])])


def _rope_rows():
    inv_freq = ROPE_THETA ** (-jnp.arange(ROPE_HALF, dtype=F32) / ROPE_HALF)
    z = jnp.zeros((ROPE_HALF,), F32)
    freq = jnp.concatenate([inv_freq, z, inv_freq, z])
    sign = jnp.concatenate([-jnp.ones((ROPE_HALF,), F32), z, jnp.ones((ROPE_HALF,), F32), z])
    return _rows8([freq[None], sign[None]])


def _column_shards(a):
    return a.reshape(a.shape[0], 4, a.shape[1] // 4).transpose(1, 0, 2)


def _from_column_shards(a):
    return a.transpose(1, 0, 2).reshape(a.shape[1], 4 * a.shape[2])


_ANY = pl.BlockSpec(memory_space=pl.ANY)
_OTHER_CHIPS = ((1, 0), (0, 1), (1, 1))


def _here():
    return lax.axis_index("x"), lax.axis_index("y"), lax.axis_index("c")


def _flip(v, bit):
    return 1 - v if bit else v


def _remote(src, dst, send_sems, recv_sems, k, to):
    return pltpu.make_async_remote_copy(src_ref=src, dst_ref=dst, send_sem=send_sems.at[k], recv_sem=recv_sems.at[k],
                                        device_id=to, device_id_type=MESH)


def _half_of(ref, k, shape):
    r, c = shape
    if (r // 2) % 16 == 0:
        return ref.at[pl.ds(pl.multiple_of(k * (r // 2), 16), r // 2)]
    if (c // 2) % 128 == 0:
        return ref.at[:, pl.ds(pl.multiple_of(k * (c // 2), 128), c // 2)]
    return None


def _gather_copies(srcs, dsts, send_sems, recv_sems, local_sems):
    x, y, c = _here()
    slot, sibling, n = 2 * x + y, (x, y, 1 - c), len(srcs)
    starts, passes, waits = [], [], []
    for i, (src, dst) in enumerate(zip(srcs, dsts)):
        own = pltpu.make_async_copy(src, dst.at[slot], local_sems.at[i])
        starts.append(own.start)
        waits.append(own.wait)
        halves = _half_of(src, c, src.shape) is not None
        for j, (fx, fy) in enumerate(_OTHER_CHIPS):
            cx, cy = _flip(x, fx), _flip(y, fy)
            there = dst.at[2 * cx + cy]
            if halves:
                push = _remote(_half_of(src, c, src.shape), _half_of(dst.at[slot], c, src.shape), send_sems, recv_sems, 3 * i + j, (cx, cy, c))
                landed, other = _half_of(there, c, src.shape), _half_of(there, 1 - c, src.shape)
                onward = _remote(landed, landed, send_sems, recv_sems, 3 * n + 3 * i + j, sibling)
                passes += [_remote(landed, landed, send_sems, recv_sems, 3 * i + j, (cx, cy, c)).wait_recv, onward.start]
                waits += [_remote(other, other, send_sems, recv_sems, 3 * n + 3 * i + j, sibling).wait_recv, onward.wait_send]
            else:
                push = _remote(src, dst.at[slot], send_sems, recv_sems, 3 * i + j, (cx, cy, c))
                waits.append(_remote(there, there, send_sems, recv_sems, 3 * i + j, (cx, cy, c)).wait_recv)
            starts.append(push.start)
            waits.append(push.wait_send)
    return starts, passes, waits


def _gather_scratch(n):
    return [pltpu.SemaphoreType.DMA((6 * n,)), pltpu.SemaphoreType.DMA((6 * n,)), pltpu.SemaphoreType.DMA((n,))]


def _all_gather(shards, name):
    ns = len(shards)

    def body(*refs):
        starts, passes, waits = _gather_copies(refs[:ns], refs[ns:2 * ns], *refs[2 * ns:])
        for call in starts + passes + waits:
            call()

    return pl.pallas_call(
        body, in_specs=[_ANY] * ns, out_specs=[_ANY] * ns, out_shape=[_sds((4,) + s.shape, s.dtype) for s in shards],
        scratch_shapes=_gather_scratch(ns), name=name,
    )(*shards)


def _by_lanes(shape):
    return (shape[-2] // 2) % 16 != 0


def _scattered_shape(p):
    r, c = p.shape[1:]
    return _sds((8, r, c // 2) if _by_lanes(p.shape) else (8, r // 2, c), p.dtype)


def _scatter_copies(srcs, dsts, send_sems, recv_sems, local_sems, whole=0):
    x, y, c = _here()
    me = 4 * x + 2 * y + c
    starts, waits = [], []
    for i, (src, dst) in enumerate(zip(srcs, dsts)):
        def piece(px, py, pc, src=src, entire=i >= len(srcs) - whole):
            if entire:
                return src
            if _by_lanes(src.shape):
                half = src.shape[2] // 2
                return src.at[2 * px + py, :, pl.ds(pl.multiple_of(pc * half, 128), half)]
            half = src.shape[1] // 2
            return src.at[2 * px + py, pl.ds(pl.multiple_of(pc * half, 16), half)]

        own = pltpu.make_async_copy(piece(x, y, c), dst.at[me], local_sems.at[i])
        starts.append(own.start)
        waits.append(own.wait)
        for k in range(1, 8):
            px, py, pc = _flip(x, k & 4), _flip(y, k & 2), _flip(c, k & 1)
            push = _remote(piece(px, py, pc), dst.at[me], send_sems, recv_sems, 7 * i + k - 1, (px, py, pc))
            landed = dst.at[4 * px + 2 * py + pc]
            starts.append(push.start)
            waits += [_remote(landed, landed, send_sems, recv_sems, 7 * i + k - 1, (px, py, pc)).wait_recv, push.wait_send]
    return starts, waits


def _scatter_scratch(n):
    return [pltpu.SemaphoreType.DMA((7 * n,)), pltpu.SemaphoreType.DMA((7 * n,)), pltpu.SemaphoreType.DMA((n,))]


def _swapped_shape(half):
    r, c = half.shape
    return _sds((r, 2 * c) if _by_lanes((r, 2 * c)) else (2, r, c), half.dtype)


def _swap_copies(srcs, dsts, send_sems, recv_sems, local_sems):
    x, y, c = _here()
    sibling = (x, y, 1 - c)
    starts, waits = [], []
    for i, (src, dst) in enumerate(zip(srcs, dsts)):
        if len(dst.shape) == 2:
            lanes = src.shape[1]
            mine, other = (dst.at[:, pl.ds(pl.multiple_of(k * lanes, 128), lanes)] for k in (c, 1 - c))
        else:
            mine, other = dst.at[c], dst.at[1 - c]
        own = pltpu.make_async_copy(src, mine, local_sems.at[i])
        push = _remote(src, mine, send_sems, recv_sems, i, sibling)
        starts += [own.start, push.start]
        waits += [_remote(other, other, send_sems, recv_sems, i, sibling).wait_recv, push.wait_send, own.wait]
    return starts, waits


def _swap_scratch(n):
    return [pltpu.SemaphoreType.DMA((n,)), pltpu.SemaphoreType.DMA((n,)), pltpu.SemaphoreType.DMA((n,))]


def _exchange_halves(halves, wholes):
    ns, nw = len(halves), len(wholes)

    def body(*refs):
        srcs, dsts = refs[:ns + nw], refs[ns + nw:2 * (ns + nw)]
        sems = refs[2 * (ns + nw):]
        starts, waits = _swap_copies(srcs[:ns], dsts[:ns], *sems[:3])
        more = _scatter_copies(srcs[ns:], dsts[ns:], *sems[3:], whole=nw)
        for call in starts + more[0] + waits + more[1]:
            call()

    return pl.pallas_call(
        body, in_specs=[_ANY] * (ns + nw), out_specs=[_ANY] * (ns + nw),
        out_shape=[_swapped_shape(h) for h in halves] + [_sds((8,) + a.shape, a.dtype) for a in wholes],
        scratch_shapes=_swap_scratch(ns) + _scatter_scratch(nw), name="exchange_halves",
    )(*halves, *wholes)


def _row_tile(rows, row_bytes, budget):
    tr = rows
    while tr * row_bytes > budget and tr % 16 == 0:
        tr //= 2
    return tr


def _sum_slots(parts, name):
    _, rows, cols = parts.shape
    tr = _row_tile(rows, 8 * cols * 4, 2 * 1024 * 1024)

    def body(p_ref, o_ref):
        acc = p_ref[0].astype(F32)
        for d in range(1, 8):
            acc = acc + p_ref[d].astype(F32)
        o_ref[...] = acc

    return pl.pallas_call(
        body, grid=(rows // tr,), in_specs=[pl.BlockSpec((8, tr, cols), lambda i: (0, i, 0))],
        out_specs=pl.BlockSpec((tr, cols), lambda i: (i, 0)), out_shape=_sds((rows, cols), F32),
        compiler_params=_params(("parallel",)), name=name,
    )(parts)


def _adam_update(w, g, m, v):
    m = ADAM_B1 * m + (1.0 - ADAM_B1) * g
    v = ADAM_B2 * v + (1.0 - ADAM_B2) * jnp.square(g)
    m_hat = m / (1.0 - ADAM_B1 ** ADAM_STEP)
    v_hat = v / (1.0 - ADAM_B2 ** ADAM_STEP)
    return -ADAM_LR * (m_hat / (jnp.sqrt(v_hat) + ADAM_EPS) + ADAM_WD * w), m, v


SMALL_ROWS = {"attn_norm_w": 0, "mlp_norm_w": 1, "q_lat_norm_w": 2, "kv_lat_norm_w": 3, "q_norm_w": 4, "k_norm_w": 5,
              "mla_out_norm_w": 6, "gdn_norm_w": 10, "a_log": 11, "dt_bias": 12}
LOSS_ROW = 13
SMALL_SHAPE = (16, 1024)


def _pack_small_partials(d_attn_nw, d_mlp_nw, d_ln, d_qk_nw, d_mix_nw, d_scal, conv_parts, sq):
    D = d_attn_nw.shape[1]

    def body(an_ref, mn_ref, ln_ref, qk_ref, mix_ref, sc_ref, cq_ref, ck_ref, cv_ref, sq_ref, a_ref, c_ref):
        a_ref[...] = jnp.zeros_like(a_ref)
        a_ref[0:1, :D] = an_ref[...]
        a_ref[1:2, :D] = mn_ref[...]
        a_ref[2:4, :LORA] = ln_ref[...]
        for row, base in ((4, 0), (5, 2)):
            rope = qk_ref[base + 1:base + 2, :]
            a_ref[row:row + 1, :QK_DIM] = jnp.concatenate(
                [qk_ref[base:base + 1, :], rope[:, :ROPE_HALF], rope[:, 2 * ROPE_HALF:3 * ROPE_HALF]], axis=1)
        a_ref[6:6 + HEADS, :HEAD_DIM] = mix_ref[0:HEADS, :]
        a_ref[10:11, :HEAD_DIM] = mix_ref[HEADS:HEADS + 1, :]
        a_ref[11:13, :128] = sc_ref[0:2, :]
        a_ref[LOSS_ROW:LOSS_ROW + 1, :128] = jnp.zeros((1, 128), F32) + jnp.sum(sq_ref[:, 0:1, 0:1]) * (0.5 / D)
        c_ref[...] = jnp.concatenate([cq_ref[...], ck_ref[...], cv_ref[...]], axis=1)

    return pl.pallas_call(
        body, out_shape=[_sds(SMALL_SHAPE, F32), _sds((CONV_TAPS, GQKV_W), F32)], name="pack_small_partials",
    )(d_attn_nw, d_mlp_nw, d_ln, d_qk_nw, d_mix_nw, d_scal, *conv_parts, sq)


def _adamw_small(parts, conv_parts, w, m, v):
    names = tuple(SMALL_ROWS) + ("conv_w",)
    cols = w["conv_w"].shape[2]

    def body(*refs):
        p_ref, c_ref = refs[:2]
        n = len(names)
        w_refs, m_refs, v_refs = (dict(zip(names, refs[2 + k * n:2 + (k + 1) * n])) for k in range(3))
        loss_ref = refs[2 + 3 * n]
        out = [dict(zip(names, refs[3 + (3 + k) * n:3 + (4 + k) * n])) for k in range(4)]
        acc_ref, cacc_ref = refs[3 + 7 * n:]
        acc, cacc = p_ref[0], c_ref[0]
        for d in range(1, 8):
            acc, cacc = acc + p_ref[d], cacc + c_ref[d]
        acc_ref[...] = acc
        cacc_ref[...] = cacc
        loss_ref[...] = acc_ref[LOSS_ROW:LOSS_ROW + 1, 0:1]
        chip = 2 * lax.axis_index("x") + lax.axis_index("y")
        for name in names:
            shape = w_refs[name].shape
            if name == "conv_w":
                g = sum(jnp.where(chip == s, cacc_ref[:, s * cols:(s + 1) * cols], 0.0) for s in range(4))[None]
            else:
                row = SMALL_ROWS[name]
                g = acc_ref[row:row + math.prod(shape[:-1]), 0:shape[-1]].reshape(shape)
            delta, new_m, new_v = _adam_update(w_refs[name][...], g, m_refs[name][...], v_refs[name][...])
            for ref, val in zip((o[name] for o in out), (g, delta, new_m, new_v)):
                ref[...] = val

    ins = [x[n] for x in (w, m, v) for n in names]
    shapes = [_sds(w[n].shape, F32) for n in names]
    outs = pl.pallas_call(
        body, out_shape=[_sds((1, 1), F32)] + shapes * 4,
        scratch_shapes=[pltpu.VMEM(parts.shape[1:], F32), pltpu.VMEM(conv_parts.shape[1:], F32)], name="adamw_small",
    )(parts, conv_parts, *ins)
    n = len(names)
    return (outs[0],) + tuple(dict(zip(names, outs[1 + k * n:1 + (k + 1) * n])) for k in range(4))


def _adamw(w, g, m, v, name):
    rows, cols = w.shape[0], w.shape[-1]
    if w.ndim == 3:
        tr = max(d for d in range(1, rows + 1) if rows % d == 0 and d * 8 * cols * 4 * 14 <= VMEM_LIMIT // 2)
    else:
        tr = _row_tile(rows, 7 * cols * 4, 4 * 1024 * 1024)

    def body(w_ref, g_ref, m_ref, v_ref, d_ref, mo_ref, vo_ref):
        d_ref[...], mo_ref[...], vo_ref[...] = _adam_update(w_ref[...], g_ref[...], m_ref[...], v_ref[...])

    block = (tr,) + w.shape[1:]
    spec = pl.BlockSpec(block, lambda i: (i,) + (0,) * (len(block) - 1))
    return pl.pallas_call(
        body, grid=(rows // tr,), in_specs=[spec] * 4, out_specs=[spec] * 3, out_shape=[_sds(w.shape, F32)] * 3,
        compiler_params=_params(("parallel",)), name=name,
    )(w, g, m, v)


def kernel(x, positions, attn_norm_w, w_in, q_lat_norm_w, w_uq, kv_lat_norm_w, w_ukv, q_norm_w, k_norm_w, mla_out_norm_w, conv_w, a_log, dt_bias, gdn_norm_w, w_out, mlp_norm_w, w_up, w_down, loss_target, m_attn_norm_w, m_w_in, m_q_lat_norm_w, m_w_uq, m_kv_lat_norm_w, m_w_ukv, m_q_norm_w, m_k_norm_w, m_mla_out_norm_w, m_conv_w, m_a_log, m_dt_bias, m_gdn_norm_w, m_w_out, m_mlp_norm_w, m_w_up, m_w_down, v_attn_norm_w, v_w_in, v_q_lat_norm_w, v_w_uq, v_kv_lat_norm_w, v_w_ukv, v_q_norm_w, v_k_norm_w, v_mla_out_norm_w, v_conv_w, v_a_log, v_dt_bias, v_gdn_norm_w, v_w_out, v_mlp_norm_w, v_w_up, v_w_down):
    w = dict(zip(WEIGHTS, (attn_norm_w, w_in, q_lat_norm_w, w_uq, kv_lat_norm_w, w_ukv, q_norm_w, k_norm_w, mla_out_norm_w, conv_w,
                           a_log, dt_bias, gdn_norm_w, w_out, mlp_norm_w, w_up, w_down)))
    m = dict(zip(WEIGHTS, (m_attn_norm_w, m_w_in, m_q_lat_norm_w, m_w_uq, m_kv_lat_norm_w, m_w_ukv, m_q_norm_w, m_k_norm_w,
                           m_mla_out_norm_w, m_conv_w, m_a_log, m_dt_bias, m_gdn_norm_w, m_w_out, m_mlp_norm_w, m_w_up, m_w_down)))
    v = dict(zip(WEIGHTS, (v_attn_norm_w, v_w_in, v_q_lat_norm_w, v_w_uq, v_kv_lat_norm_w, v_w_ukv, v_q_norm_w, v_k_norm_w,
                           v_mla_out_norm_w, v_conv_w, v_a_log, v_dt_bias, v_gdn_norm_w, v_w_out, v_mlp_norm_w, v_w_up, v_w_down)))
    B, S, D = x.shape
    T = B * S
    x2, pos, target = x.reshape(T, D), positions.reshape(T, 1), loss_target.reshape(T, D)
    seq = lambda a: a.reshape(B, S, a.shape[-1])
    tok = lambda a: a.reshape(T, a.shape[-1])
    local = {n: w[n][0] for n in SHARDED}

    g_in, g_uq, g_ukv, g_conv = _all_gather([jnp.swapaxes(w_in, 1, 2)[0].astype(BF16), local["w_uq"].astype(BF16),
                                             local["w_ukv"].astype(BF16), local["conv_w"]], "gather_first_weights")
    w_in_p = _widen_w_in_t(g_in.reshape(-1, D))
    w_mla = _stack_mla(_from_column_shards(g_uq), _from_column_shards(g_ukv))
    conv_full = _from_column_shards(g_conv)
    ln_w = jnp.concatenate([q_lat_norm_w, kv_lat_norm_w], axis=0)
    qk_nw = _qk_norm_rows(q_norm_w, k_norm_w)
    rope_rows = _rope_rows()
    scal = _rows8([jnp.pad(a_log, ((0, 0), (0, 128 - HEADS))), jnp.pad(dt_bias, ((0, 0), (0, 128 - HEADS)))])
    mix_nw = _rows8([mla_out_norm_w[0], gdn_norm_w])

    xn, lat, gqkv, gz, gab = _in_proj_fwd(x2, attn_norm_w, w_in_p)
    q, k, v_att = _mla_pre_fwd(lat, pos, ln_w, w_mla, qk_nw, rope_rows)
    ao, lse, g_down = _attn_fwd(seq(q), seq(k), seq(v_att), [local["w_down"].astype(BF16)])
    gq, gk, gv = _gdn_pre_fwd(seq(gqkv), conv_full)
    go, states, powers, solutions, g_out, w_up_b = _gdn_chunk_fwd(gq, gk, gv, seq(gab), scal,
                                                                  [local["w_out"].astype(BF16), local["w_up"].astype(BF16)])
    w_out_b = g_out.reshape(-1, D)
    w_down_b = g_down.reshape(-1, D)
    mix, h2 = _mix_fwd(tok(ao), tok(go), gz, mix_nw, w_out_b, x2)
    hn, act, dy, sq = _mlp_fwd(h2, mlp_norm_w, w_up_b, w_down_b, target)

    dh, d_up, d_mlp_nw = _mlp_bwd(h2, mlp_norm_w, act, w_up_b, w_down_b, dy)
    p_down = _wgrad(act, dy, "wgrad_down").reshape(4, -1, D)
    p_up = _wgrad(hn, d_up, "wgrad_up", column_shards=4)
    d_ao, d_go, d_gz, d_mix_nw = _mix_bwd(tok(ao), tok(go), gz, mix_nw, w_out_b, dh)
    p_out = _wgrad(mix, dh, "wgrad_out").reshape(4, -1, D)
    d_gq, d_gk, d_gv, d_gab, d_scal, s_up, s_out = _gdn_chunk_bwd(gq, gk, gv, seq(gab), scal, states, powers, solutions, seq(d_go),
                                                                  [p_up, p_out])
    early = ("w_up", "w_out", "w_down")
    dxq, dxk, dxv, dcq, dck, dcv, g_up, g_out = _gdn_pre_bwd(seq(gqkv), conv_full, d_gq, d_gk, d_gv,
                                                             [_sum_slots(s_up, "sum_w_up"), _sum_slots(s_out, "sum_w_out")])
    dq, dk, dv, s_down = _attn_bwd(seq(q), seq(k), seq(v_att), ao, lse, seq(d_ao), [p_down])
    d_lat, d_ln, d_w_mla, d_qk_nw, g_down = _mla_pre_bwd(lat, pos, ln_w, w_mla, qk_nw, rope_rows, tok(dq), tok(dk), tok(dv),
                                                         [_sum_slots(s_down, "sum_w_down")])
    early_grads = [g_up, g_out, g_down]
    d_pieces = [d_lat, tok(dxq), tok(dxk), tok(dxv), d_gz, tok(d_gab)]
    p_in = _narrow_w_in_t(_wgrad_pieces(d_pieces, xn, "wgrad_in")).reshape(4, -1, D)
    p_uq, p_ukv = (_column_shards(a).astype(BF16) for a in _unstack_mla(d_w_mla))
    grad_x2, d_attn_nw, s_in, s_uq, s_ukv = _in_proj_bwd(d_pieces, w_in_p, x2, attn_norm_w, dh, [p_in, p_uq, p_ukv])
    small_buf, conv_buf = _pack_small_partials(d_attn_nw, d_mlp_nw, d_ln, d_qk_nw, d_mix_nw, d_scal, (dcq, dck, dcv), sq)

    late = ("w_in", "w_uq", "w_ukv")
    *late_grads, s_small, s_conv = _exchange_halves([_sum_slots(s, "sum_" + n) for n, s in zip(late, (s_in, s_uq, s_ukv))],
                                                    [small_buf, conv_buf])
    names = early + late
    grad = {n: g.reshape(-1, g.shape[-1]) for n, g in zip(names, list(early_grads) + list(late_grads))}

    loss, g_small, delta, new_m, new_v = _adamw_small(s_small, s_conv, w, m, v)
    grad.update(g_small)
    for n in names:
        if n == "w_in":
            stored = lambda a: jnp.transpose(a, (2, 0, 1))
            outs = _adamw(stored(w[n]), grad[n][:, None, :], stored(m[n]), stored(v[n]), "adamw_" + n)
            grad[n], delta[n], new_m[n], new_v[n] = (jnp.transpose(a, (1, 2, 0)) for a in (grad[n][:, None, :], *outs))
        else:
            delta[n], new_m[n], new_v[n] = _adamw(local[n], grad[n], m[n][0], v[n][0], "adamw_" + n)
    def in_order(d):
        return [d[n].reshape(w[n].shape) for n in WEIGHTS]

    return (loss.reshape(()), grad_x2.reshape(B, S, D), *in_order(grad), *in_order(delta), *in_order(new_m), *in_order(new_v))
```

```python
import functools
import math

import jax
import jax.numpy as jnp
from jax import lax
from jax.experimental import pallas as pl
from jax.experimental.pallas import tpu as pltpu

F32 = jnp.float32
BF16 = jnp.bfloat16
MESH = pl.DeviceIdType.MESH

EPS = 1e-6
HEADS = 4
HEAD_DIM = 128
ROPE_DIM = 64
ROPE_HALF = 32
QK_DIM = 192
QK_PAD = 256
LORA = 256
CHUNK = 64
SOLVE_POWERS = 5
CONV_TAPS = 4
ROPE_THETA = 10000.0
ATTN_SCALE = QK_DIM ** -0.5

LAT_W = 640
GQKV_W = 3 * HEADS * HEAD_DIM
GZ_W = HEADS * HEAD_DIM
GAB_W = 128
PROJ_SPLITS = ((0, LAT_W), (LAT_W, LAT_W + GQKV_W), (LAT_W + GQKV_W, LAT_W + GQKV_W + GZ_W),
               (LAT_W + GQKV_W + GZ_W, LAT_W + GQKV_W + GZ_W + GAB_W))
PROJ_W = PROJ_SPLITS[-1][1]

ADAM_LR = 0.001
ADAM_B1 = 0.9
ADAM_B2 = 0.999
ADAM_EPS = 1e-08
ADAM_WD = 0.01
ADAM_STEP = 10

TOKEN_TILE = 512
WGRAD_TOKEN_TILE = 1024
MLP_TOKEN_TILE = 512
FF_TILE = 1024
ATTN_TILE = 512
ATTN_HEADS_PER_STEP = 2
WGRAD_OUT_BYTES = 8 * 1024 * 1024
VMEM_LIMIT = 48 * 1024 * 1024

SHARDED = ("w_in", "w_uq", "w_ukv", "conv_w", "w_out", "w_up", "w_down")
WEIGHTS = ("attn_norm_w", "w_in", "q_lat_norm_w", "w_uq", "kv_lat_norm_w", "w_ukv", "q_norm_w", "k_norm_w", "mla_out_norm_w",
           "conv_w", "a_log", "dt_bias", "gdn_norm_w", "w_out", "mlp_norm_w", "w_up", "w_down")


def _sds(shape, dtype):
    return jax.ShapeDtypeStruct(shape, dtype)


def _params(semantics):
    return pltpu.CompilerParams(dimension_semantics=semantics, vmem_limit_bytes=VMEM_LIMIT)


def _block(n):
    for b in (512, 256, 128):
        if n % b == 0:
            return b
    return n


def _dg(a, b, ca, cb, prec):
    lead = a.ndim - 2
    batch = (tuple(range(lead)),) * 2
    return lax.dot_general(a, b, (((ca + lead,), (cb + lead,)), batch), precision=prec, preferred_element_type=F32)


def _split_bf16(a):
    hi = a.astype(BF16)
    return hi, (a - hi.astype(F32)).astype(BF16)


def _dot_bf16(a, b, ca, cb):
    return _dg(a.astype(BF16), b.astype(BF16), ca, cb, None)


def _dot_bf16x3(a, b, ca, cb):
    a_hi, a_lo = _split_bf16(a)
    b_hi, b_lo = _split_bf16(b)
    lead = a.ndim - 2
    return _dg(jnp.concatenate([a_hi, a_hi, a_lo], axis=ca + lead), jnp.concatenate([b_hi, b_lo, b_hi], axis=cb + lead), ca, cb, None)


def _matmul_family(dot):
    def nn_raw(a, b):
        return dot(a, b, 1, 0)

    def nt_raw(a, b):
        return dot(a, b, 1, 1)

    def tn_raw(a, b):
        return dot(a, b, 0, 0)

    @jax.custom_vjp
    def nn(a, b):
        return nn_raw(a, b)

    nn.defvjp(lambda a, b: (nn_raw(a, b), (a, b)), lambda r, g: (nt_raw(g, r[1]), tn_raw(r[0], g)))

    @jax.custom_vjp
    def nt(a, b):
        return nt_raw(a, b)

    nt.defvjp(lambda a, b: (nt_raw(a, b), (a, b)), lambda r, g: (nn_raw(g, r[1]), tn_raw(g, r[0])))

    @jax.custom_vjp
    def tn(a, b):
        return tn_raw(a, b)

    tn.defvjp(lambda a, b: (tn_raw(a, b), (a, b)), lambda r, g: (nt_raw(r[1], g), nn_raw(r[0], g)))
    return nn, nt, tn


_bf_nn, _bf_nt, _bf_tn = _matmul_family(_dot_bf16)
_hi_nn, _hi_nt, _hi_tn = _matmul_family(_dot_bf16x3)


def _lower_powers(lmat):
    powers = []
    while 2 ** (len(powers) + 1) < lmat.shape[-1]:
        powers.append(_dot_bf16x3(powers[-1] if powers else lmat, powers[-1] if powers else lmat, 1, 0))
    return powers


@jax.custom_vjp
def _unit_lower_solve(lmat, rhs):
    return _unit_lower_solve_fwd(lmat, rhs)[0]


def _unit_lower_solve_fwd(lmat, rhs):
    powers = _lower_powers(lmat)
    x = rhs - _dot_bf16x3(lmat, rhs, 1, 0)
    for p in powers:
        x = x + _dot_bf16x3(p, x, 1, 0)
    return x, (lmat, powers, x)


def _unit_lower_solve_bwd(res, g):
    lmat, powers, x = res
    y = g - _dot_bf16x3(lmat, g, 0, 0)
    for p in powers:
        y = y + _dot_bf16x3(p, y, 0, 0)
    return -_dot_bf16x3(y, x, 1, 1), y


_unit_lower_solve.defvjp(_unit_lower_solve_fwd, _unit_lower_solve_bwd)


@jax.custom_vjp
def _unit_lower_solve_kept(lmat, rhs, powers, x):
    return x


_unit_lower_solve_kept.defvjp(
    lambda lmat, rhs, powers, x: (x, (lmat, powers, x)),
    lambda res, g: _unit_lower_solve_bwd(res, g) + ([jnp.zeros_like(p) for p in res[1]], jnp.zeros_like(res[2])))


@jax.custom_vjp
def _lane_halves(x):
    n = x.shape[-1] // 2
    return x[..., :n], x[..., n:]


_lane_halves.defvjp(lambda x: (_lane_halves(x), None), lambda _, g: (jnp.concatenate(g, axis=-1),))


@jax.custom_vjp
def _row_halves(x):
    n = x.shape[-2] // 2
    return x[..., :n, :], x[..., n:, :]


_row_halves.defvjp(lambda x: (_row_halves(x), None), lambda _, g: (jnp.concatenate(g, axis=-2),))


@jax.custom_vjp
def _swap_halves(t):
    return pltpu.roll(t, 64, 1)


_swap_halves.defvjp(lambda t: (pltpu.roll(t, 64, 1), None), lambda _, g: (pltpu.roll(g, 64, 1),))


@functools.partial(jax.custom_vjp, nondiff_argnums=(2,))
def _shift_rows(x, keep, s):
    return pltpu.roll(x, s, 0) * keep


def _shift_rows_fwd(x, keep, s):
    return pltpu.roll(x, s, 0) * keep, keep


def _shift_rows_bwd(s, keep, g):
    return pltpu.roll(g * keep, keep.shape[0] - s, 0), jnp.zeros_like(keep)


_shift_rows.defvjp(_shift_rows_fwd, _shift_rows_bwd)


def _sigmoid(x):
    return 0.5 * jnp.tanh(0.5 * x) + 0.5


def _softplus(x):
    return jnp.maximum(x, 0.0) + jnp.log(1.0 + jnp.exp(jnp.minimum(x, -x)))


def _silu(x):
    return x * _sigmoid(x)


def _rms(x, w, n=None):
    n = x.shape[-1] if n is None else n
    r = lax.rsqrt(jnp.sum(x * x, axis=-1, keepdims=True) * (1.0 / n) + EPS)
    return x * r * w


def _rope(t, cos_f, sin_f):
    return t * cos_f + _swap_halves(t) * sin_f


def _rope_tables(pos_col, freq_row, sign_row):
    ang = pos_col.astype(F32) * freq_row
    return jnp.cos(ang), jnp.sin(ang) * sign_row


def _onehot_row(lane):
    return (lax.broadcasted_iota(jnp.int32, (1, 128), 1) == lane).astype(F32)


def _row_spec(tm, w):
    return pl.BlockSpec((tm, w), lambda i: (i, 0))


def _const_spec(shape):
    return pl.BlockSpec(shape, lambda *_: (0,) * len(shape))


def _in_proj_fwd(x2, w_an, w_in_p):
    T, D = x2.shape
    tm = min(TOKEN_TILE, T)

    def body(x_ref, wn_ref, w_ref, xn_ref, lat_ref, gqkv_ref, gz_ref, gab_ref):
        x = x_ref[...]
        r = lax.rsqrt(jnp.mean(x * x, axis=-1, keepdims=True) + EPS)
        xn = (x * r * wn_ref[...]).astype(BF16)
        xn_ref[...] = xn
        for ref, (a, b) in zip((lat_ref, gqkv_ref, gz_ref, gab_ref), PROJ_SPLITS):
            ref[...] = _dg(xn, w_ref[a:b, :], 1, 1, None)

    widths = [b - a for a, b in PROJ_SPLITS]
    return pl.pallas_call(
        body, grid=(T // tm,),
        in_specs=[_row_spec(tm, D), _const_spec((1, D)), _const_spec((PROJ_W, D))],
        out_specs=[_row_spec(tm, D)] + [_row_spec(tm, w) for w in widths],
        out_shape=[_sds((T, D), BF16)] + [_sds((T, w), F32) for w in widths],
        compiler_params=_params(("parallel",)), name="in_proj_fwd",
    )(x2, w_an, w_in_p)


def _in_proj_bwd(pieces, w_in_p, x2, w_an, dh, partials, halves):
    T, D = x2.shape
    tm = min(TOKEN_TILE, T)
    widths = [p.shape[1] for p in pieces]
    starts = [sum(widths[:i]) for i in range(len(widths))]
    assert sum(widths) == PROJ_W
    npc, ns, nh = len(pieces), len(partials), len(halves)

    def body(*refs):
        piece_refs = refs[:npc]
        w_ref, x_ref, wn_ref, dh_ref = refs[npc:npc + 4]
        src_refs = refs[npc + 4:npc + 4 + ns + nh]
        dx_ref, dwn_ref = refs[npc + 4 + ns + nh:npc + 6 + ns + nh]
        dst_refs = refs[npc + 6 + ns + nh:npc + 6 + 2 * (ns + nh)]
        sems = refs[npc + 6 + 2 * (ns + nh):]

        def riders():
            scatter = _scatter_copies(src_refs[:ns], dst_refs[:ns], *sems[:3])
            swap = _swap_copies(src_refs[ns:], dst_refs[ns:], *sems[3:])
            return scatter[0] + swap[0], scatter[1] + swap[1]

        @pl.when(pl.program_id(0) == 0)
        def _():
            for start in riders()[0]:
                start()
            dwn_ref[...] = jnp.zeros_like(dwn_ref)

        dxn = jnp.zeros((tm, D), F32)
        for ref, a, width in zip(piece_refs, starts, widths):
            dxn += _dg(ref[...], w_ref[a:a + width, :], 1, 0, None)
        _, pull = jax.vjp(_rms, x_ref[...], wn_ref[...])
        dx, dwn = pull(dxn)
        dx_ref[...] = dx + dh_ref[...]
        dwn_ref[...] += dwn

        @pl.when(pl.program_id(0) == T // tm - 1)
        def _():
            for wait in riders()[1]:
                wait()

    return pl.pallas_call(
        body, grid=(T // tm,),
        in_specs=[_row_spec(tm, w) for w in widths] + [_const_spec((PROJ_W, D)), _row_spec(tm, D), _const_spec((1, D)),
                                                       _row_spec(tm, D)] + [_ANY] * (ns + nh),
        out_specs=[_row_spec(tm, D), _const_spec((1, D))] + [_ANY] * (ns + nh),
        out_shape=[_sds((T, D), F32), _sds((1, D), F32)] + [_scattered_shape(p) for p in partials]
                  + [_swapped_shape(h) for h in halves],
        scratch_shapes=_scatter_scratch(ns) + _swap_scratch(nh),
        compiler_params=_params(("arbitrary",)), name="in_proj_bwd",
    )(*pieces, w_in_p, x2, w_an, dh, *partials, *halves)


def _wgrad_pieces(pieces, b, name):
    T, k2 = b.shape
    tt = min(WGRAD_TOKEN_TILE, T)
    widths = [p.shape[1] for p in pieces]
    starts = [sum(widths[:i]) for i in range(len(widths))]
    k1 = sum(widths)

    def body(*refs):
        piece_refs, (b_ref, o_ref, acc_ref) = refs[:len(pieces)], refs[len(pieces):]
        t = pl.program_id(0)

        @pl.when(t == 0)
        def _():
            acc_ref[...] = jnp.zeros_like(acc_ref)

        bt = b_ref[...].astype(BF16)
        for ref, r0, width in zip(piece_refs, starts, widths):
            acc_ref[r0:r0 + width, :] += jnp.dot(ref[...].T, bt, preferred_element_type=F32)

        @pl.when(t == T // tt - 1)
        def _():
            o_ref[...] = acc_ref[...].astype(o_ref.dtype)

    return pl.pallas_call(
        body, grid=(T // tt,),
        in_specs=[pl.BlockSpec((tt, w), lambda t: (t, 0)) for w in widths] + [pl.BlockSpec((tt, k2), lambda t: (t, 0))],
        out_specs=_const_spec((k1, k2)), out_shape=_sds((k1, k2), BF16), scratch_shapes=[pltpu.VMEM((k1, k2), F32)],
        compiler_params=_params(("arbitrary",)), name=name,
    )(*pieces, b)


def _wgrad(a, b, name, column_shards=1, out_dtype=BF16):
    T, k1 = a.shape
    k2 = b.shape[1]
    per_shard = k2 // column_shards
    tt = min(WGRAD_TOKEN_TILE, T)
    b1 = k1
    while b1 * k2 * 4 > WGRAD_OUT_BYTES and b1 % 256 == 0:
        b1 //= 2
    step = _block(per_shard)

    def body(a_ref, b_ref, o_ref, acc_ref):
        t = pl.program_id(1)

        @pl.when(t == 0)
        def _():
            acc_ref[...] = jnp.zeros_like(acc_ref)

        a_t = a_ref[...].astype(BF16).T
        for c0 in range(0, k2, step):
            part = jnp.dot(a_t, b_ref[:, c0:c0 + step].astype(BF16), preferred_element_type=F32)
            if column_shards == 1:
                acc_ref[:, c0:c0 + step] += part
            else:
                acc_ref[c0 // per_shard, :, c0 % per_shard:c0 % per_shard + step] += part

        @pl.when(t == T // tt - 1)
        def _():
            o_ref[...] = acc_ref[...].astype(o_ref.dtype)

    if column_shards == 1:
        block, out_spec, out_shape = (b1, k2), pl.BlockSpec((b1, k2), lambda i, t: (i, 0)), _sds((k1, k2), out_dtype)
    else:
        block = (column_shards, b1, per_shard)
        out_spec, out_shape = pl.BlockSpec(block, lambda i, t: (0, i, 0)), _sds((column_shards, k1, per_shard), out_dtype)
    return pl.pallas_call(
        body, grid=(k1 // b1, T // tt),
        in_specs=[pl.BlockSpec((tt, b1), lambda i, t: (t, i)), pl.BlockSpec((tt, k2), lambda i, t: (t, 0))],
        out_specs=out_spec, out_shape=out_shape, scratch_shapes=[pltpu.VMEM(block, F32)],
        compiler_params=_params(("parallel", "arbitrary")), name=name,
    )(a, b)


@jax.custom_vjp
def _lane_blocks(x):
    return tuple(x[:, i:i + 128] for i in range(0, x.shape[1], 128))


_lane_blocks.defvjp(lambda x: (_lane_blocks(x), None), lambda _, g: (jnp.concatenate(g, axis=1),))


def _mla_pre_fn(q_lat, kv_lat, kpe, ln_q, ln_kv, w_q, w_kv, qn_n, qn_p, kn_n, kn_p, cos_f, sin_f):
    qn = _rms(q_lat, ln_q)
    kvn = _rms(kv_lat, ln_kv)
    kp = _rope(_rms(kpe, kn_p, ROPE_DIM), cos_f, sin_f)
    q_blocks = _lane_blocks(_bf_nn(qn, w_q))
    kv_blocks = _lane_blocks(_bf_nn(kvn, w_kv))
    outs = []
    for h in range(HEADS):
        outs.append(_rms(q_blocks[h], qn_n))
        outs.append(_rope(_rms(q_blocks[HEADS + h], qn_p, ROPE_DIM), cos_f, sin_f))
        outs.append(_rms(kv_blocks[h], kn_n))
        outs.append(kv_blocks[HEADS + h])
    return tuple(outs) + (kp,)


def _mla_pre_operands(lat_ref, pos_ref, ln_ref, w_ref, nw_ref, rope_ref):
    cos_f, sin_f = _rope_tables(pos_ref[...], rope_ref[0:1, :], rope_ref[1:2, :])
    side_by_side = lambda blocks: jnp.concatenate([w_ref[i].astype(F32) for i in blocks], axis=1)
    diff = (lat_ref[:, 0:LORA], lat_ref[:, LORA:2 * LORA], lat_ref[:, 2 * LORA:LAT_W], ln_ref[0:1, :], ln_ref[1:2, :],
            side_by_side(range(2 * HEADS)), side_by_side(range(2 * HEADS, 4 * HEADS)),
            nw_ref[0:1, :], nw_ref[1:2, :], nw_ref[2:3, :], nw_ref[3:4, :])
    return diff, cos_f, sin_f


def _mla_pre_fwd(lat, pos, ln_w, w_mla, nw, rope_rows):
    T = lat.shape[0]
    tm = min(TOKEN_TILE, T)

    def body(lat_ref, pos_ref, ln_ref, w_ref, nw_ref, rope_ref, q_ref, k_ref, v_ref):
        diff, cos_f, sin_f = _mla_pre_operands(lat_ref, pos_ref, ln_ref, w_ref, nw_ref, rope_ref)
        outs = _mla_pre_fn(*diff, cos_f, sin_f)
        kp = outs[-1].astype(BF16)
        for h in range(HEADS):
            q_n, q_p, k_n, v = outs[4 * h:4 * h + 4]
            q_ref[:, h * QK_PAD:h * QK_PAD + HEAD_DIM] = q_n.astype(BF16)
            q_ref[:, h * QK_PAD + HEAD_DIM:(h + 1) * QK_PAD] = q_p.astype(BF16)
            k_ref[:, h * QK_PAD:h * QK_PAD + HEAD_DIM] = k_n.astype(BF16)
            k_ref[:, h * QK_PAD + HEAD_DIM:(h + 1) * QK_PAD] = kp
            v_ref[:, h * HEAD_DIM:(h + 1) * HEAD_DIM] = v.astype(BF16)

    return pl.pallas_call(
        body, grid=(T // tm,),
        in_specs=[_row_spec(tm, LAT_W), _row_spec(tm, 1), _const_spec((2, LORA)), _const_spec((4 * HEADS, LORA, 128)),
                  _const_spec((8, 128)), _const_spec((8, 128))],
        out_specs=[_row_spec(tm, HEADS * QK_PAD), _row_spec(tm, HEADS * QK_PAD), _row_spec(tm, HEADS * HEAD_DIM)],
        out_shape=[_sds((T, HEADS * QK_PAD), BF16), _sds((T, HEADS * QK_PAD), BF16), _sds((T, HEADS * HEAD_DIM), BF16)],
        compiler_params=_params(("parallel",)), name="mla_pre_fwd",
    )(lat, pos, ln_w, w_mla, nw, rope_rows)


def _mla_pre_bwd(lat, pos, ln_w, w_mla, nw, rope_rows, dq, dk, dv, halves):
    T = lat.shape[0]
    tm = min(TOKEN_TILE, T)
    ns = len(halves)

    def body(*refs):
        lat_ref, pos_ref, ln_ref, w_ref, nw_ref, rope_ref, dq_ref, dk_ref, dv_ref = refs[:9]
        src_refs = refs[9:9 + ns]
        dlat_ref, dln_ref, dw_ref, dnw_ref = refs[9 + ns:13 + ns]
        dst_refs = refs[13 + ns:13 + 2 * ns]
        sems = refs[13 + 2 * ns:]

        @pl.when(pl.program_id(0) == 0)
        def _():
            for start in _swap_copies(src_refs, dst_refs, *sems)[0]:
                start()
            dln_ref[...] = jnp.zeros_like(dln_ref)
            dw_ref[...] = jnp.zeros_like(dw_ref)
            dnw_ref[...] = jnp.zeros_like(dnw_ref)

        diff, cos_f, sin_f = _mla_pre_operands(lat_ref, pos_ref, ln_ref, w_ref, nw_ref, rope_ref)
        _, pull = jax.vjp(lambda *a: _mla_pre_fn(*a, cos_f, sin_f), *diff)
        cts = []
        d_kp = jnp.zeros((tm, 128), F32)
        for h in range(HEADS):
            cts.append(dq_ref[:, h * QK_PAD:h * QK_PAD + HEAD_DIM])
            cts.append(dq_ref[:, h * QK_PAD + HEAD_DIM:(h + 1) * QK_PAD])
            cts.append(dk_ref[:, h * QK_PAD:h * QK_PAD + HEAD_DIM])
            cts.append(dv_ref[:, h * HEAD_DIM:(h + 1) * HEAD_DIM])
            d_kp += dk_ref[:, h * QK_PAD + HEAD_DIM:(h + 1) * QK_PAD]
        d_ql, d_kvl, d_kpe, d_lnq, d_lnkv, d_wq, d_wkv, d_qn_n, d_qn_p, d_kn_n, d_kn_p = pull(tuple(cts) + (d_kp,))
        d_w = [d[:, i:i + 128] for d in (d_wq, d_wkv) for i in range(0, d.shape[1], 128)]
        dlat_ref[:, 0:LORA] = d_ql.astype(BF16)
        dlat_ref[:, LORA:2 * LORA] = d_kvl.astype(BF16)
        dlat_ref[:, 2 * LORA:LAT_W] = d_kpe.astype(BF16)
        dln_ref[0:1, :] += d_lnq
        dln_ref[1:2, :] += d_lnkv
        for i in range(4 * HEADS):
            dw_ref[i] += d_w[i]
        for i, d in enumerate((d_qn_n, d_qn_p, d_kn_n, d_kn_p)):
            dnw_ref[i:i + 1, :] += d

        @pl.when(pl.program_id(0) == T // tm - 1)
        def _():
            for wait in _swap_copies(src_refs, dst_refs, *sems)[1]:
                wait()

    return pl.pallas_call(
        body, grid=(T // tm,),
        in_specs=[_row_spec(tm, LAT_W), _row_spec(tm, 1), _const_spec((2, LORA)), _const_spec((4 * HEADS, LORA, 128)),
                  _const_spec((8, 128)), _const_spec((8, 128)),
                  _row_spec(tm, HEADS * QK_PAD), _row_spec(tm, HEADS * QK_PAD), _row_spec(tm, HEADS * HEAD_DIM)] + [_ANY] * ns,
        out_specs=[_row_spec(tm, LAT_W), _const_spec((2, LORA)), _const_spec((4 * HEADS, LORA, 128)), _const_spec((8, 128))]
                  + [_ANY] * ns,
        out_shape=[_sds((T, LAT_W), BF16), _sds((2, LORA), F32), _sds((4 * HEADS, LORA, 128), F32), _sds((8, 128), F32)]
                  + [_swapped_shape(h) for h in halves],
        scratch_shapes=_swap_scratch(ns),
        compiler_params=_params(("arbitrary",)), name="mla_pre_bwd",
    )(lat, pos, ln_w, w_mla, nw, rope_rows, dq, dk, dv, *halves)


def _causal_mask(i, j, tq, tk):
    row = i * tq + lax.broadcasted_iota(jnp.int32, (tq, tk), 0)
    col = j * tk + lax.broadcasted_iota(jnp.int32, (tq, tk), 1)
    return col <= row


def _attn_fwd(q, k, v, shards):
    B, S, _ = q.shape
    t = min(ATTN_TILE, S)
    nq = S // t
    ns = len(shards)

    hp = ATTN_HEADS_PER_STEP
    qk = lambda h: slice(h * QK_PAD, (h + 1) * QK_PAD)
    vd = lambda h: slice(h * HEAD_DIM, (h + 1) * HEAD_DIM)

    def body(*refs):
        q_ref, k_ref, v_ref = refs[:3]
        src_refs = refs[3:3 + ns]
        o_ref, lse_ref = refs[3 + ns:5 + ns]
        dst_refs = refs[5 + ns:5 + 2 * ns]
        sems = refs[5 + 2 * ns:]
        b, g, i = pl.program_id(0), pl.program_id(1), pl.program_id(2)
        qb = [q_ref[0, :, qk(h)] for h in range(hp)]

        step_no = (b * (HEADS // hp) + g) * nq + i
        for phase, at in enumerate((0, (3 * B * (HEADS // hp) * nq) // 4)):
            @pl.when(step_no == at)
            def _(phase=phase):
                for call in _gather_copies(src_refs, dst_refs, *sems)[phase]:
                    call()

        def step(j, carry, diagonal):
            rows = pl.ds(pl.multiple_of(j * t, t), t)
            s = [_dg(qb[h], k_ref[0, rows, qk(h)], 1, 1, None) * ATTN_SCALE for h in range(hp)]
            if diagonal:
                keep = _causal_mask(0, 0, t, t)
                s = [jnp.where(keep, x, -1e30) for x in s]
            m_new = [jnp.maximum(carry[h][0], jnp.max(s[h], axis=-1, keepdims=True)) for h in range(hp)]
            p = [jnp.exp(s[h] - m_new[h]) for h in range(hp)]
            alpha = [jnp.exp(carry[h][0] - m_new[h]) for h in range(hp)]
            l = [alpha[h] * carry[h][1] + jnp.sum(p[h], axis=-1, keepdims=True) for h in range(hp)]
            pv = [jnp.dot(p[h].astype(BF16), v_ref[0, rows, vd(h)], preferred_element_type=F32) for h in range(hp)]
            return tuple((m_new[h], l[h], alpha[h] * carry[h][2] + pv[h]) for h in range(hp))

        init = tuple((jnp.full((t, 1), -1e30, F32), jnp.zeros((t, 1), F32), jnp.zeros((t, HEAD_DIM), F32)) for _ in range(hp))
        below = lax.fori_loop(0, i, lambda j, carry: step(j, carry, False), init)
        for h, (m, l, acc) in enumerate(step(i, below, True)):
            o_ref[0, :, vd(h)] = acc / l
            lse_ref[0, h, 0] = (m + jnp.log(l)).T

        @pl.when((b == B - 1) & (g == HEADS // hp - 1) & (i == nq - 1))
        def _():
            for wait in _gather_copies(src_refs, dst_refs, *sems)[2]:
                wait()

    return pl.pallas_call(
        body, grid=(B, HEADS // hp, nq),
        in_specs=[pl.BlockSpec((1, t, hp * QK_PAD), lambda b, g, i: (b, i, g)),
                  pl.BlockSpec((1, S, hp * QK_PAD), lambda b, g, i: (b, 0, g)),
                  pl.BlockSpec((1, S, hp * HEAD_DIM), lambda b, g, i: (b, 0, g))] + [_ANY] * ns,
        out_specs=[pl.BlockSpec((1, t, hp * HEAD_DIM), lambda b, g, i: (b, i, g)),
                   pl.BlockSpec((1, hp, 1, 1, t), lambda b, g, i: (b, g, i, 0, 0))] + [_ANY] * ns,
        out_shape=[_sds((B, S, HEADS * HEAD_DIM), F32), _sds((B, HEADS, nq, 1, t), F32)] + [_sds((4,) + s.shape, s.dtype) for s in shards],
        scratch_shapes=_gather_scratch(ns),
        compiler_params=_params(("arbitrary", "arbitrary", "arbitrary")), name="attn_fwd",
    )(q, k, v, *shards)


def _attn_bwd(q, k, v, o, lse, do, partials):
    B, S, _ = q.shape
    t = min(ATTN_TILE, S)
    nq = S // t
    ns = len(partials)

    hp = ATTN_HEADS_PER_STEP
    qk = lambda h: slice(h * QK_PAD, (h + 1) * QK_PAD)
    vd = lambda h: slice(h * HEAD_DIM, (h + 1) * HEAD_DIM)
    heads = range(hp)

    def body(*refs):
        q_ref, k_ref, v_ref, o_ref, lse_ref, do_ref = refs[:6]
        src_refs = refs[6:6 + ns]
        dq_ref, dk_ref, dv_ref = refs[6 + ns:9 + ns]
        dst_refs = refs[9 + ns:9 + 2 * ns]
        dsum_ref, send_sems, recv_sems, local_sems = refs[9 + 2 * ns:]
        b, g, j = pl.program_id(0), pl.program_id(1), pl.program_id(2)

        @pl.when((b == 0) & (g == 0) & (j == 0))
        def _():
            for start in _scatter_copies(src_refs, dst_refs, send_sems, recv_sems, local_sems)[0]:
                start()

        @pl.when(j == 0)
        def _():
            dq_ref[...] = jnp.zeros_like(dq_ref)
            for h in heads:
                for blk in range(nq):
                    rows = slice(blk * t, (blk + 1) * t)
                    dsum_ref[h, blk] = jnp.sum(do_ref[0, rows, vd(h)] * o_ref[0, rows, vd(h)], axis=-1, keepdims=True).T

        kb = [k_ref[0, :, qk(h)] for h in heads]
        vb = [v_ref[0, :, vd(h)] for h in heads]

        def step(i, carry, diagonal):
            rows = pl.ds(pl.multiple_of(i * t, t), t)
            qb = [q_ref[0, rows, qk(h)] for h in heads]
            dob = [do_ref[0, rows, vd(h)].astype(BF16) for h in heads]
            s = [_dg(kb[h], qb[h], 1, 1, None) * ATTN_SCALE for h in heads]
            p = [jnp.exp(s[h] - lse_ref[0, h, i]) for h in heads]
            if diagonal:
                key = lax.broadcasted_iota(jnp.int32, (t, t), 0)
                query = lax.broadcasted_iota(jnp.int32, (t, t), 1)
                p = [jnp.where(key <= query, x, 0.0) for x in p]
            dp = [_dg(vb[h], dob[h], 1, 1, None) for h in heads]
            dv = [carry[h][1] + jnp.dot(p[h].astype(BF16), dob[h], preferred_element_type=F32) for h in heads]
            ds = [(p[h] * (dp[h] - dsum_ref[h, i]) * ATTN_SCALE).astype(BF16) for h in heads]
            for h in heads:
                dq_ref[0, rows, qk(h)] += _dg(ds[h], kb[h], 0, 0, None)
            return tuple((carry[h][0] + jnp.dot(ds[h], qb[h], preferred_element_type=F32), dv[h]) for h in heads)

        zeros = tuple((jnp.zeros((t, QK_PAD), F32), jnp.zeros((t, HEAD_DIM), F32)) for _ in heads)
        on_diagonal = step(j, zeros, True)
        done = lax.fori_loop(j + 1, nq, lambda i, carry: step(i, carry, False), on_diagonal)
        for h, (dk, dv) in enumerate(done):
            dk_ref[0, :, qk(h)] = dk
            dv_ref[0, :, vd(h)] = dv

        @pl.when((b == B - 1) & (g == HEADS // hp - 1) & (j == nq - 1))
        def _():
            for wait in _scatter_copies(src_refs, dst_refs, send_sems, recv_sems, local_sems)[1]:
                wait()

    return pl.pallas_call(
        body, grid=(B, HEADS // hp, nq),
        in_specs=[pl.BlockSpec((1, S, hp * QK_PAD), lambda b, g, j: (b, 0, g)),
                  pl.BlockSpec((1, t, hp * QK_PAD), lambda b, g, j: (b, j, g)),
                  pl.BlockSpec((1, t, hp * HEAD_DIM), lambda b, g, j: (b, j, g)),
                  pl.BlockSpec((1, S, hp * HEAD_DIM), lambda b, g, j: (b, 0, g)),
                  pl.BlockSpec((1, hp, nq, 1, t), lambda b, g, j: (b, g, 0, 0, 0)),
                  pl.BlockSpec((1, S, hp * HEAD_DIM), lambda b, g, j: (b, 0, g))] + [_ANY] * ns,
        out_specs=[pl.BlockSpec((1, S, hp * QK_PAD), lambda b, g, j: (b, 0, g)),
                   pl.BlockSpec((1, t, hp * QK_PAD), lambda b, g, j: (b, j, g)),
                   pl.BlockSpec((1, t, hp * HEAD_DIM), lambda b, g, j: (b, j, g))] + [_ANY] * ns,
        out_shape=[_sds((B, S, HEADS * QK_PAD), F32), _sds((B, S, HEADS * QK_PAD), F32), _sds((B, S, HEADS * HEAD_DIM), F32)]
                  + [_scattered_shape(p) for p in partials],
        scratch_shapes=[pltpu.VMEM((hp, nq, 1, t), F32)] + _scatter_scratch(ns),
        compiler_params=_params(("arbitrary", "arbitrary", "arbitrary")), name="attn_bwd",
    )(q, k, v, o, lse, do, *partials)


def _gdn_pre_fn(xq, xk, xv, wq, wk, wv, keeps):
    def conv_silu(x, w):
        acc = x * w[3]
        for s in (1, 2, 3):
            acc = acc + _shift_rows(x, keeps[s - 1], s) * w[3 - s]
        return _silu(acc)

    def l2(x):
        return x * lax.rsqrt(jnp.sum(x * x, axis=-1, keepdims=True) + EPS)

    return l2(conv_silu(xq, wq)) * (HEAD_DIM ** -0.5), l2(conv_silu(xk, wk)), conv_silu(xv, wv)


def _gdn_pre_specs(S):
    x_specs = [pl.BlockSpec((1, S, HEAD_DIM), lambda h, b, g=g: (b, 0, g * HEADS + h)) for g in range(3)]
    w_specs = [pl.BlockSpec((CONV_TAPS, HEAD_DIM), lambda h, b, g=g: (0, g * HEADS + h)) for g in range(3)]
    out_spec = pl.BlockSpec((1, S, HEAD_DIM), lambda h, b: (b, 0, h))
    return x_specs, w_specs, out_spec


def _row_keeps(S):
    t = lax.broadcasted_iota(jnp.int32, (S, HEAD_DIM), 0)
    return [(t >= s).astype(F32) for s in (1, 2, 3)]


def _gdn_pre_fwd(gqkv, conv_w):
    B, S, _ = gqkv.shape
    x_specs, w_specs, out_spec = _gdn_pre_specs(S)

    def body(xq_ref, xk_ref, xv_ref, wq_ref, wk_ref, wv_ref, q_ref, k_ref, v_ref):
        taps = [[w[i:i + 1, :] for i in range(CONV_TAPS)] for w in (wq_ref, wk_ref, wv_ref)]
        q, k, v = _gdn_pre_fn(xq_ref[0], xk_ref[0], xv_ref[0], *taps, _row_keeps(S))
        q_ref[0], k_ref[0], v_ref[0] = q, k, v

    return pl.pallas_call(
        body, grid=(HEADS, B), in_specs=x_specs + w_specs, out_specs=[out_spec] * 3,
        out_shape=[_sds((B, S, HEADS * HEAD_DIM), F32)] * 3,
        compiler_params=_params(("parallel", "parallel")), name="gdn_pre_fwd",
    )(gqkv, gqkv, gqkv, conv_w, conv_w, conv_w)


def _gdn_pre_bwd(gqkv, conv_w, dq, dk, dv, halves):
    B, S, _ = gqkv.shape
    x_specs, w_specs, out_spec = _gdn_pre_specs(S)
    dw_spec = pl.BlockSpec((CONV_TAPS, HEAD_DIM), lambda h, b: (0, h))
    ns = len(halves)

    def body(*refs):
        xq_ref, xk_ref, xv_ref, wq_ref, wk_ref, wv_ref, dq_ref, dk_ref, dv_ref = refs[:9]
        src_refs = refs[9:9 + ns]
        dxq_ref, dxk_ref, dxv_ref, dwq_ref, dwk_ref, dwv_ref = refs[9 + ns:15 + ns]
        dst_refs = refs[15 + ns:15 + 2 * ns]
        sems = refs[15 + 2 * ns:]
        first = (pl.program_id(0) == 0) & (pl.program_id(1) == 0)
        last = (pl.program_id(0) == HEADS - 1) & (pl.program_id(1) == B - 1)

        @pl.when(first)
        def _():
            for start in _swap_copies(src_refs, dst_refs, *sems)[0]:
                start()

        @pl.when(pl.program_id(1) == 0)
        def _():
            for r in (dwq_ref, dwk_ref, dwv_ref):
                r[...] = jnp.zeros_like(r)

        taps = [[w[i:i + 1, :] for i in range(CONV_TAPS)] for w in (wq_ref, wk_ref, wv_ref)]
        keeps = _row_keeps(S)
        _, pull = jax.vjp(lambda *a: _gdn_pre_fn(*a, keeps), xq_ref[0], xk_ref[0], xv_ref[0], *taps)
        dxq, dxk, dxv, dwq, dwk, dwv = pull((dq_ref[0], dk_ref[0], dv_ref[0]))
        dxq_ref[0], dxk_ref[0], dxv_ref[0] = dxq.astype(BF16), dxk.astype(BF16), dxv.astype(BF16)
        for ref, dw in ((dwq_ref, dwq), (dwk_ref, dwk), (dwv_ref, dwv)):
            for i in range(CONV_TAPS):
                ref[i:i + 1, :] += dw[i]

        @pl.when(last)
        def _():
            for wait in _swap_copies(src_refs, dst_refs, *sems)[1]:
                wait()

    hw = HEADS * HEAD_DIM
    return pl.pallas_call(
        body, grid=(HEADS, B), in_specs=x_specs + w_specs + [out_spec] * 3 + [_ANY] * ns,
        out_specs=[out_spec] * 3 + [dw_spec] * 3 + [_ANY] * ns,
        out_shape=[_sds((B, S, hw), BF16)] * 3 + [_sds((CONV_TAPS, hw), F32)] * 3 + [_swapped_shape(h) for h in halves],
        scratch_shapes=_swap_scratch(ns),
        compiler_params=_params(("arbitrary", "arbitrary")), name="gdn_pre_bwd",
    )(gqkv, gqkv, gqkv, conv_w, conv_w, conv_w, dq, dk, dv, *halves)


def _chunk_masks():
    i = lax.broadcasted_iota(jnp.int32, (CHUNK, CHUNK), 0)
    j = lax.broadcasted_iota(jnp.int32, (CHUNK, CHUNK), 1)
    lower, after = (j <= i).astype(F32), (j > i).astype(F32)
    return {"le": lower, "le_gt": jnp.concatenate([lower, after], axis=0), "strict": (j < i).astype(F32)}


def _gdn_chunk_fn(groups, masks, solve=_unit_lower_solve):
    lane = lax.broadcasted_iota(jnp.int32, (groups, 1, 128), 2)
    head = lax.broadcasted_iota(jnp.int32, (groups, 1, 128), 0) % HEADS
    pick_a, pick_b = (lane == head).astype(F32), (lane == head + HEADS).astype(F32)
    lower, lower_after, strict = (jnp.broadcast_to(masks[n], (groups,) + masks[n].shape) for n in ("le", "le_gt", "strict"))
    ones_row = jnp.ones((1, 1, HEAD_DIM), F32)

    def f(q, k, v, gab, a_row, dt_row, state):
        ga = jnp.sum(gab * pick_a, axis=2, keepdims=True)
        gb = jnp.sum(gab * pick_b, axis=2, keepdims=True)
        a_log = jnp.sum(a_row * pick_a, axis=2, keepdims=True)
        dt_bias = jnp.sum(dt_row * pick_a, axis=2, keepdims=True)
        beta = _sigmoid(gb)
        g = -jnp.exp(a_log) * _softplus(ga + dt_bias)
        g_wide = g * ones_row
        cum, rest = _row_halves(_hi_nn(lower_after, g_wide))
        total = jnp.sum(g_wide, axis=1, keepdims=True)
        diff = _hi_nn(lower, g * strict)
        decay = lower * jnp.exp(diff)
        e_cum = jnp.exp(cum)
        kk, qk = _row_halves(_bf_nt(jnp.concatenate([k, q], axis=1), k))
        lmat = strict * (beta * kk * decay)
        u, w = _lane_halves(solve(lmat, jnp.concatenate([v * beta, k * (beta * e_cum)], axis=2)))
        w_state, q_state = _row_halves(_bf_nn(jnp.concatenate([w, q * e_cum], axis=1), state))
        v_new = u - w_state
        o = q_state + _bf_nn(qk * decay, v_new)
        new_state = state * jnp.exp(total) + _bf_tn(k * jnp.exp(rest), v_new)
        return o, new_state

    return f


def _gdn_chunk_fwd(q, k, v, gab, scal, shards):
    B, S, W = q.shape
    N = S // CHUNK
    ns = len(shards)

    def body(*refs):
        q_ref, k_ref, v_ref, gab_ref, sc_ref = refs[:5]
        src_refs = refs[5:5 + ns]
        o_ref, st_ref, pw_ref, sol_ref = refs[5 + ns:9 + ns]
        dst_refs = refs[9 + ns:9 + 2 * ns]
        state_ref, send_sems, recv_sems, local_sems = refs[9 + 2 * ns:]
        n = pl.program_id(0)
        kept = {}

        @pl.when(n == 0)
        def _():
            for start in _gather_copies(src_refs, dst_refs, send_sems, recv_sems, local_sems)[0]:
                start()
            state_ref[...] = jnp.zeros_like(state_ref)

        @pl.when(n == (2 * N) // 3)
        def _():
            for pass_on in _gather_copies(src_refs, dst_refs, send_sems, recv_sems, local_sems)[1]:
                pass_on()

        groups = [(b, h) for b in range(B) for h in range(HEADS)]
        gather = lambda ref: jnp.stack([ref[b, :, h * HEAD_DIM:(h + 1) * HEAD_DIM] for b, h in groups])
        state = state_ref[...]
        for i, (b, h) in enumerate(groups):
            st_ref[b, 0, h] = state[i]
        def solve_and_keep(lmat, rhs):
            kept["x"], (_, kept["powers"], _) = _unit_lower_solve_fwd(lmat, rhs)
            return kept["x"]

        o, new_state = _gdn_chunk_fn(len(groups), _chunk_masks(), solve_and_keep)(
            gather(q_ref), gather(k_ref), gather(v_ref), jnp.stack([gab_ref[b] for b, _ in groups]), sc_ref[0:1, :], sc_ref[1:2, :], state)
        for i, (b, h) in enumerate(groups):
            o_ref[b, :, h * HEAD_DIM:(h + 1) * HEAD_DIM] = o[i]
            sol_ref[b, 0, h] = kept["x"][i]
            for p, power in enumerate(kept["powers"]):
                pw_ref[b, 0, h, p] = power[i]
        state_ref[...] = new_state

        @pl.when(n == N - 1)
        def _():
            for wait in _gather_copies(src_refs, dst_refs, send_sems, recv_sems, local_sems)[2]:
                wait()

    seq = pl.BlockSpec((B, CHUNK, W), lambda n: (0, n, 0))
    return pl.pallas_call(
        body, grid=(N,),
        in_specs=[seq, seq, seq, pl.BlockSpec((B, CHUNK, GAB_W), lambda n: (0, n, 0)), _const_spec((8, 128))] + [_ANY] * ns,
        out_specs=[seq, pl.BlockSpec((B, 1, HEADS, HEAD_DIM, HEAD_DIM), lambda n: (0, n, 0, 0, 0)),
                   pl.BlockSpec((B, 1, HEADS, SOLVE_POWERS, CHUNK, CHUNK), lambda n: (0, n, 0, 0, 0, 0)),
                   pl.BlockSpec((B, 1, HEADS, CHUNK, 2 * HEAD_DIM), lambda n: (0, n, 0, 0, 0))] + [_ANY] * ns,
        out_shape=[_sds((B, S, W), F32), _sds((B, N, HEADS, HEAD_DIM, HEAD_DIM), F32),
                   _sds((B, N, HEADS, SOLVE_POWERS, CHUNK, CHUNK), F32), _sds((B, N, HEADS, CHUNK, 2 * HEAD_DIM), F32)]
                  + [_sds((4,) + s.shape, s.dtype) for s in shards],
        scratch_shapes=[pltpu.VMEM((B * HEADS, HEAD_DIM, HEAD_DIM), F32)] + _gather_scratch(ns),
        compiler_params=_params(("arbitrary",)), name="gdn_chunk_fwd",
    )(q, k, v, gab, scal, *shards)


def _gdn_chunk_bwd(q, k, v, gab, scal, states, powers, solutions, do, partials):
    B, S, W = q.shape
    N = S // CHUNK
    ns = len(partials)

    def body(*refs):
        q_ref, k_ref, v_ref, gab_ref, sc_ref, st_ref, pw_ref, sol_ref, do_ref = refs[:9]
        src_refs = refs[9:9 + ns]
        dq_ref, dk_ref, dv_ref, dgab_ref, dsc_ref = refs[9 + ns:14 + ns]
        dst_refs = refs[14 + ns:14 + 2 * ns]
        dstate_ref, send_sems, recv_sems, local_sems = refs[14 + 2 * ns:]
        n = pl.program_id(0)

        @pl.when(n == 0)
        def _():
            for start in _scatter_copies(src_refs, dst_refs, send_sems, recv_sems, local_sems)[0]:
                start()
            dstate_ref[...] = jnp.zeros_like(dstate_ref)
            dsc_ref[...] = jnp.zeros_like(dsc_ref)

        groups = [(b, h) for b in range(B) for h in range(HEADS)]
        gather = lambda ref: jnp.stack([ref[b, :, h * HEAD_DIM:(h + 1) * HEAD_DIM] for b, h in groups])
        kept_powers = [jnp.stack([pw_ref[b, 0, h, p] for b, h in groups]) for p in range(SOLVE_POWERS)]
        kept_x = jnp.stack([sol_ref[b, 0, h] for b, h in groups])
        solve = lambda lmat, rhs: _unit_lower_solve_kept(lmat, rhs, kept_powers, kept_x)
        _, pull = jax.vjp(_gdn_chunk_fn(len(groups), _chunk_masks(), solve), gather(q_ref), gather(k_ref), gather(v_ref),
                          jnp.stack([gab_ref[b] for b, _ in groups]), sc_ref[0:1, :], sc_ref[1:2, :],
                          jnp.stack([st_ref[b, 0, h] for b, h in groups]))
        dq, dk, dv, dg, d_a, d_dt, dstate = pull((gather(do_ref), dstate_ref[...]))
        for i, (b, h) in enumerate(groups):
            lanes = slice(h * HEAD_DIM, (h + 1) * HEAD_DIM)
            dq_ref[b, :, lanes] = dq[i]
            dk_ref[b, :, lanes] = dk[i]
            dv_ref[b, :, lanes] = dv[i]
        for b in range(B):
            dgab_ref[b] = sum(dg[b * HEADS + h] for h in range(HEADS)).astype(BF16)
        dstate_ref[...] = dstate
        dsc_ref[0:1, :] += d_a
        dsc_ref[1:2, :] += d_dt

        @pl.when(n == N - 1)
        def _():
            for wait in _scatter_copies(src_refs, dst_refs, send_sems, recv_sems, local_sems)[1]:
                wait()

    seq = pl.BlockSpec((B, CHUNK, W), lambda n: (0, N - 1 - n, 0))
    gab_spec = pl.BlockSpec((B, CHUNK, GAB_W), lambda n: (0, N - 1 - n, 0))
    return pl.pallas_call(
        body, grid=(N,),
        in_specs=[seq, seq, seq, gab_spec, _const_spec((8, 128)),
                  pl.BlockSpec((B, 1, HEADS, HEAD_DIM, HEAD_DIM), lambda n: (0, N - 1 - n, 0, 0, 0)),
                  pl.BlockSpec((B, 1, HEADS, SOLVE_POWERS, CHUNK, CHUNK), lambda n: (0, N - 1 - n, 0, 0, 0, 0)),
                  pl.BlockSpec((B, 1, HEADS, CHUNK, 2 * HEAD_DIM), lambda n: (0, N - 1 - n, 0, 0, 0)), seq] + [_ANY] * ns,
        out_specs=[seq, seq, seq, gab_spec, _const_spec((8, 128))] + [_ANY] * ns,
        out_shape=[_sds((B, S, W), F32)] * 3 + [_sds((B, S, GAB_W), BF16), _sds((8, 128), F32)] + [_scattered_shape(p) for p in partials],
        scratch_shapes=[pltpu.VMEM((B * HEADS, HEAD_DIM, HEAD_DIM), F32)] + _scatter_scratch(ns),
        compiler_params=_params(("arbitrary",)), name="gdn_chunk_bwd",
    )(q, k, v, gab, scal, states, powers, solutions, do, *partials)


def _mix_fn(ao, go, gz, w_mla, w_gdn):
    return tuple(_rms(ao[h], w_mla[h]) for h in range(HEADS)) + tuple(_rms(go[h], w_gdn) * _silu(gz[h]) for h in range(HEADS))


def _mix_operands(ao_ref, go_ref, gz_ref, nw_ref):
    blocks = lambda ref: [ref[:, h * HEAD_DIM:(h + 1) * HEAD_DIM] for h in range(HEADS)]
    return blocks(ao_ref), blocks(go_ref), blocks(gz_ref), [nw_ref[h:h + 1, :] for h in range(HEADS)], nw_ref[HEADS:HEADS + 1, :]


def _mix_fwd(ao, go, gz, nw, w_out, x2):
    T, D = x2.shape
    tm = min(TOKEN_TILE, T)
    MW = 2 * HEADS * HEAD_DIM

    def body(ao_ref, go_ref, gz_ref, nw_ref, w_ref, x_ref, mix_ref, h_ref):
        outs = _mix_fn(*_mix_operands(ao_ref, go_ref, gz_ref, nw_ref))
        for i, piece in enumerate(outs):
            mix_ref[:, i * HEAD_DIM:(i + 1) * HEAD_DIM] = piece.astype(BF16)
        h_ref[...] = x_ref[...] + jnp.dot(mix_ref[...], w_ref[...], preferred_element_type=F32)

    half = HEADS * HEAD_DIM
    return pl.pallas_call(
        body, grid=(T // tm,),
        in_specs=[_row_spec(tm, half), _row_spec(tm, half), _row_spec(tm, half), _const_spec((8, 128)), _const_spec((MW, D)),
                  _row_spec(tm, D)],
        out_specs=[_row_spec(tm, MW), _row_spec(tm, D)],
        out_shape=[_sds((T, MW), BF16), _sds((T, D), F32)],
        compiler_params=_params(("parallel",)), name="mix_fwd",
    )(ao, go, gz, nw, w_out, x2)


def _mix_bwd(ao, go, gz, nw, w_out, dh):
    T, D = dh.shape
    tm = min(TOKEN_TILE, T)
    MW = 2 * HEADS * HEAD_DIM
    half = HEADS * HEAD_DIM

    def body(ao_ref, go_ref, gz_ref, nw_ref, w_ref, dh_ref, dao_ref, dgo_ref, dgz_ref, dnw_ref):
        @pl.when(pl.program_id(0) == 0)
        def _():
            dnw_ref[...] = jnp.zeros_like(dnw_ref)

        d_mix = _dg(dh_ref[...].astype(BF16), w_ref[...], 1, 1, None)
        cts = tuple(d_mix[:, i * HEAD_DIM:(i + 1) * HEAD_DIM] for i in range(2 * HEADS))
        _, pull = jax.vjp(_mix_fn, *_mix_operands(ao_ref, go_ref, gz_ref, nw_ref))
        d_ao, d_go, d_gz, d_wm, d_wg = pull(cts)
        for h in range(HEADS):
            lanes = slice(h * HEAD_DIM, (h + 1) * HEAD_DIM)
            dao_ref[:, lanes] = d_ao[h]
            dgo_ref[:, lanes] = d_go[h]
            dgz_ref[:, lanes] = d_gz[h].astype(BF16)
            dnw_ref[h:h + 1, :] += d_wm[h]
        dnw_ref[HEADS:HEADS + 1, :] += d_wg

    return pl.pallas_call(
        body, grid=(T // tm,),
        in_specs=[_row_spec(tm, half), _row_spec(tm, half), _row_spec(tm, half), _const_spec((8, 128)), _const_spec((MW, D)),
                  _row_spec(tm, D)],
        out_specs=[_row_spec(tm, half)] * 3 + [_const_spec((8, 128))],
        out_shape=[_sds((T, half), F32)] * 2 + [_sds((T, half), BF16), _sds((8, 128), F32)],
        compiler_params=_params(("arbitrary",)), name="mix_bwd",
    )(ao, go, gz, nw, w_out, dh)


def _up_spec(w_up, tf):
    per_shard = w_up.shape[2] // tf
    return pl.BlockSpec((None, w_up.shape[1], tf), lambda i, j: (j // per_shard, 0, j % per_shard))


def _mlp_fwd(h2, w_mn, w_up, w_down, target):
    T, D = h2.shape
    FF = w_down.shape[0]
    tm, tf = min(MLP_TOKEN_TILE, T), min(FF_TILE, w_up.shape[2])
    nf = FF // tf

    def body(h_ref, wn_ref, wu_ref, wd_ref, t_ref, hn_ref, act_ref, dy_ref, sq_ref, acc_ref):
        j = pl.program_id(1)

        @pl.when(j == 0)
        def _():
            hn_ref[...] = _rms(h_ref[...], wn_ref[...]).astype(BF16)
            acc_ref[...] = jnp.zeros_like(acc_ref)

        up = jnp.dot(hn_ref[...], wu_ref[...], preferred_element_type=F32)
        act = jnp.square(jnp.maximum(up, 0.0)).astype(BF16)
        act_ref[...] = act
        acc_ref[...] += jnp.dot(act, wd_ref[...], preferred_element_type=F32)

        @pl.when(j == nf - 1)
        def _():
            err = h_ref[...] + acc_ref[...] - t_ref[...]
            dy_ref[...] = err * (1.0 / D)
            sq_ref[...] = jnp.zeros_like(sq_ref) + jnp.sum(err * err)

    tok = lambda w: pl.BlockSpec((tm, w), lambda i, j: (i, 0))
    return pl.pallas_call(
        body, grid=(T // tm, nf),
        in_specs=[tok(D), _const_spec((1, D)), _up_spec(w_up, tf), pl.BlockSpec((tf, D), lambda i, j: (j, 0)), tok(D)],
        out_specs=[tok(D), pl.BlockSpec((tm, tf), lambda i, j: (i, j)), tok(D), pl.BlockSpec((1, 8, 128), lambda i, j: (i, 0, 0))],
        out_shape=[_sds((T, D), BF16), _sds((T, FF), BF16), _sds((T, D), F32), _sds((T // tm, 8, 128), F32)],
        scratch_shapes=[pltpu.VMEM((tm, D), F32)],
        compiler_params=_params(("parallel", "arbitrary")), name="mlp_fwd",
    )(h2, w_mn, w_up, w_down, target)


def _mlp_bwd(h2, w_mn, act, w_up, w_down, dy):
    T, D = h2.shape
    FF = w_down.shape[0]
    tm, tf = min(MLP_TOKEN_TILE, T), min(FF_TILE, w_up.shape[2])
    nf = FF // tf

    def body(h_ref, wn_ref, act_ref, wu_ref, wd_ref, dy_ref, dh_ref, dup_ref, dwn_ref, acc_ref, dyb_ref):
        i, j = pl.program_id(0), pl.program_id(1)

        @pl.when((i == 0) & (j == 0))
        def _():
            dwn_ref[...] = jnp.zeros_like(dwn_ref)

        @pl.when(j == 0)
        def _():
            acc_ref[...] = jnp.zeros_like(acc_ref)
            dyb_ref[...] = dy_ref[...].astype(BF16)

        r = jnp.sqrt(act_ref[...].astype(F32))
        d_act = _dg(dyb_ref[...], wd_ref[...], 1, 1, None)
        d_up = (d_act * (2.0 * r)).astype(BF16)
        dup_ref[...] = d_up
        acc_ref[...] += _dg(d_up, wu_ref[...], 1, 1, None)

        @pl.when(j == nf - 1)
        def _():
            _, pull = jax.vjp(_rms, h_ref[...], wn_ref[...])
            dh, dwn = pull(acc_ref[...])
            dh_ref[...] = dh + dy_ref[...]
            dwn_ref[...] += dwn

    tok = lambda w: pl.BlockSpec((tm, w), lambda i, j: (i, 0))
    ff = pl.BlockSpec((tm, tf), lambda i, j: (i, j))
    return pl.pallas_call(
        body, grid=(T // tm, nf),
        in_specs=[tok(D), _const_spec((1, D)), ff, _up_spec(w_up, tf), pl.BlockSpec((tf, D), lambda i, j: (j, 0)), tok(D)],
        out_specs=[tok(D), ff, _const_spec((1, D))],
        out_shape=[_sds((T, D), F32), _sds((T, FF), BF16), _sds((1, D), F32)],
        scratch_shapes=[pltpu.VMEM((tm, D), F32), pltpu.VMEM((tm, D), BF16)],
        compiler_params=_params(("arbitrary", "arbitrary")), name="mlp_bwd",
    )(h2, w_mn, act, w_up, w_down, dy)


def _rope_pad(a):
    z = jnp.zeros(a.shape[:-1] + (ROPE_HALF,), a.dtype)
    return jnp.concatenate([a[..., :ROPE_HALF], z, a[..., ROPE_HALF:], z], axis=-1)


def _rope_unpad(a):
    return jnp.concatenate([a[..., :ROPE_HALF], a[..., 2 * ROPE_HALF:3 * ROPE_HALF]], axis=-1)


_G0 = 2 * LORA + ROPE_DIM
W_IN_COLS = _G0 + GQKV_W + GZ_W + 2 * HEADS


def _widen_w_in_t(w_t):
    z = jnp.zeros((ROPE_HALF, w_t.shape[1]), w_t.dtype)
    pad = jnp.zeros((GAB_W - 2 * HEADS, w_t.shape[1]), w_t.dtype)
    return jnp.concatenate([w_t[:2 * LORA + ROPE_HALF], z, w_t[2 * LORA + ROPE_HALF:_G0], z, w_t[_G0:], pad], axis=0)


def _narrow_w_in_t(w_t):
    return jnp.concatenate([w_t[:2 * LORA + ROPE_HALF], w_t[2 * LORA + 2 * ROPE_HALF:2 * LORA + 3 * ROPE_HALF],
                            w_t[LAT_W:LAT_W + W_IN_COLS - _G0]], axis=0)


def _stack_mla(w_uq, w_ukv):
    uq = w_uq.reshape(LORA, HEADS, QK_DIM)
    ukv = w_ukv.reshape(LORA, HEADS, 2 * HEAD_DIM)
    parts = [uq[:, :, :HEAD_DIM], _rope_pad(uq[:, :, HEAD_DIM:]), ukv[:, :, :HEAD_DIM], ukv[:, :, HEAD_DIM:]]
    return jnp.concatenate([p.transpose(1, 0, 2) for p in parts], axis=0)


def _unstack_mla(w):
    p = [w[i * HEADS:(i + 1) * HEADS].transpose(1, 0, 2) for i in range(4)]
    uq = jnp.concatenate([p[0], _rope_unpad(p[1])], axis=-1).reshape(LORA, HEADS * QK_DIM)
    ukv = jnp.concatenate([p[2], p[3]], axis=-1).reshape(LORA, HEADS * 2 * HEAD_DIM)
    return uq, ukv


def _rows8(rows):
    a = jnp.concatenate(rows, axis=0)
    return jnp.pad(a, ((0, 8 - a.shape[0]), (0, 0)))


def _qk_norm_rows(q_norm_w, k_norm_w):
    return _rows8([q_norm_w[:, :HEAD_DIM], _rope_pad(q_norm_w[:, HEAD_DIM:]), k_norm_w[:, :HEAD_DIM], _rope_pad(k_norm_w[:, HEAD_DIM:])])


def _rope_rows():
    inv_freq = ROPE_THETA ** (-jnp.arange(ROPE_HALF, dtype=F32) / ROPE_HALF)
    z = jnp.zeros((ROPE_HALF,), F32)
    freq = jnp.concatenate([inv_freq, z, inv_freq, z])
    sign = jnp.concatenate([-jnp.ones((ROPE_HALF,), F32), z, jnp.ones((ROPE_HALF,), F32), z])
    return _rows8([freq[None], sign[None]])


def _column_shards(a):
    return a.reshape(a.shape[0], 4, a.shape[1] // 4).transpose(1, 0, 2)


def _from_column_shards(a):
    return a.transpose(1, 0, 2).reshape(a.shape[1], 4 * a.shape[2])


_ANY = pl.BlockSpec(memory_space=pl.ANY)
_OTHER_CHIPS = ((1, 0), (0, 1), (1, 1))


def _here():
    return lax.axis_index("x"), lax.axis_index("y"), lax.axis_index("c")


def _flip(v, bit):
    return 1 - v if bit else v


def _remote(src, dst, send_sems, recv_sems, k, to):
    return pltpu.make_async_remote_copy(src_ref=src, dst_ref=dst, send_sem=send_sems.at[k], recv_sem=recv_sems.at[k],
                                        device_id=to, device_id_type=MESH)


def _half_of(ref, k, shape):
    r, c = shape
    if (r // 2) % 16 == 0:
        return ref.at[pl.ds(pl.multiple_of(k * (r // 2), 16), r // 2)]
    if (c // 2) % 128 == 0:
        return ref.at[:, pl.ds(pl.multiple_of(k * (c // 2), 128), c // 2)]
    return None


def _gather_copies(srcs, dsts, send_sems, recv_sems, local_sems):
    x, y, c = _here()
    slot, sibling, n = 2 * x + y, (x, y, 1 - c), len(srcs)
    starts, passes, waits = [], [], []
    for i, (src, dst) in enumerate(zip(srcs, dsts)):
        own = pltpu.make_async_copy(src, dst.at[slot], local_sems.at[i])
        starts.append(own.start)
        waits.append(own.wait)
        halves = _half_of(src, c, src.shape) is not None
        for j, (fx, fy) in enumerate(_OTHER_CHIPS):
            cx, cy = _flip(x, fx), _flip(y, fy)
            there = dst.at[2 * cx + cy]
            if halves:
                push = _remote(_half_of(src, c, src.shape), _half_of(dst.at[slot], c, src.shape), send_sems, recv_sems, 3 * i + j, (cx, cy, c))
                landed, other = _half_of(there, c, src.shape), _half_of(there, 1 - c, src.shape)
                onward = _remote(landed, landed, send_sems, recv_sems, 3 * n + 3 * i + j, sibling)
                passes += [_remote(landed, landed, send_sems, recv_sems, 3 * i + j, (cx, cy, c)).wait_recv, onward.start]
                waits += [_remote(other, other, send_sems, recv_sems, 3 * n + 3 * i + j, sibling).wait_recv, onward.wait_send]
            else:
                push = _remote(src, dst.at[slot], send_sems, recv_sems, 3 * i + j, (cx, cy, c))
                waits.append(_remote(there, there, send_sems, recv_sems, 3 * i + j, (cx, cy, c)).wait_recv)
            starts.append(push.start)
            waits.append(push.wait_send)
    return starts, passes, waits


def _gather_scratch(n):
    return [pltpu.SemaphoreType.DMA((6 * n,)), pltpu.SemaphoreType.DMA((6 * n,)), pltpu.SemaphoreType.DMA((n,))]


def _all_gather(shards, name):
    ns = len(shards)

    def body(*refs):
        starts, passes, waits = _gather_copies(refs[:ns], refs[ns:2 * ns], *refs[2 * ns:])
        for call in starts + passes + waits:
            call()

    return pl.pallas_call(
        body, in_specs=[_ANY] * ns, out_specs=[_ANY] * ns, out_shape=[_sds((4,) + s.shape, s.dtype) for s in shards],
        scratch_shapes=_gather_scratch(ns), name=name,
    )(*shards)


def _by_lanes(shape):
    return (shape[-2] // 2) % 16 != 0


def _scattered_shape(p):
    r, c = p.shape[1:]
    return _sds((8, r, c // 2) if _by_lanes(p.shape) else (8, r // 2, c), p.dtype)


def _scatter_copies(srcs, dsts, send_sems, recv_sems, local_sems, whole=0):
    x, y, c = _here()
    me = 4 * x + 2 * y + c
    starts, waits = [], []
    for i, (src, dst) in enumerate(zip(srcs, dsts)):
        def piece(px, py, pc, src=src, entire=i >= len(srcs) - whole):
            if entire:
                return src
            if _by_lanes(src.shape):
                half = src.shape[2] // 2
                return src.at[2 * px + py, :, pl.ds(pl.multiple_of(pc * half, 128), half)]
            half = src.shape[1] // 2
            return src.at[2 * px + py, pl.ds(pl.multiple_of(pc * half, 16), half)]

        own = pltpu.make_async_copy(piece(x, y, c), dst.at[me], local_sems.at[i])
        starts.append(own.start)
        waits.append(own.wait)
        for k in range(1, 8):
            px, py, pc = _flip(x, k & 4), _flip(y, k & 2), _flip(c, k & 1)
            push = _remote(piece(px, py, pc), dst.at[me], send_sems, recv_sems, 7 * i + k - 1, (px, py, pc))
            landed = dst.at[4 * px + 2 * py + pc]
            starts.append(push.start)
            waits += [_remote(landed, landed, send_sems, recv_sems, 7 * i + k - 1, (px, py, pc)).wait_recv, push.wait_send]
    return starts, waits


def _scatter_scratch(n):
    return [pltpu.SemaphoreType.DMA((7 * n,)), pltpu.SemaphoreType.DMA((7 * n,)), pltpu.SemaphoreType.DMA((n,))]


def _swapped_shape(half):
    r, c = half.shape
    return _sds((r, 2 * c) if _by_lanes((r, 2 * c)) else (2, r, c), half.dtype)


def _swap_copies(srcs, dsts, send_sems=None, recv_sems=None, local_sems=None):
    if not srcs:
        return [], []
    x, y, c = _here()
    sibling = (x, y, 1 - c)
    starts, waits = [], []
    for i, (src, dst) in enumerate(zip(srcs, dsts)):
        if len(dst.shape) == 2:
            lanes = src.shape[1]
            mine, other = (dst.at[:, pl.ds(pl.multiple_of(k * lanes, 128), lanes)] for k in (c, 1 - c))
        else:
            mine, other = dst.at[c], dst.at[1 - c]
        own = pltpu.make_async_copy(src, mine, local_sems.at[i])
        push = _remote(src, mine, send_sems, recv_sems, i, sibling)
        starts += [own.start, push.start]
        waits += [_remote(other, other, send_sems, recv_sems, i, sibling).wait_recv, push.wait_send, own.wait]
    return starts, waits


def _swap_scratch(n):
    return [pltpu.SemaphoreType.DMA((n,)), pltpu.SemaphoreType.DMA((n,)), pltpu.SemaphoreType.DMA((n,))] if n else []


def _exchange_halves(halves, wholes):
    ns, nw = len(halves), len(wholes)

    def body(*refs):
        srcs, dsts = refs[:ns + nw], refs[ns + nw:2 * (ns + nw)]
        sems = refs[2 * (ns + nw):]
        starts, waits = _swap_copies(srcs[:ns], dsts[:ns], *sems[:3])
        more = _scatter_copies(srcs[ns:], dsts[ns:], *sems[3:], whole=nw)
        for call in starts + more[0] + waits + more[1]:
            call()

    return pl.pallas_call(
        body, in_specs=[_ANY] * (ns + nw), out_specs=[_ANY] * (ns + nw),
        out_shape=[_swapped_shape(h) for h in halves] + [_sds((8,) + a.shape, a.dtype) for a in wholes],
        scratch_shapes=_swap_scratch(ns) + _scatter_scratch(nw), name="exchange_halves",
    )(*halves, *wholes)


def _row_tile(rows, row_bytes, budget):
    tr = rows
    while tr * row_bytes > budget and tr % 16 == 0:
        tr //= 2
    return tr


def _sum_slots(parts, name):
    _, rows, cols = parts.shape
    tr = _row_tile(rows, 8 * cols * 4, 2 * 1024 * 1024)

    def body(p_ref, o_ref):
        acc = p_ref[0].astype(F32)
        for d in range(1, 8):
            acc = acc + p_ref[d].astype(F32)
        o_ref[...] = acc

    return pl.pallas_call(
        body, grid=(rows // tr,), in_specs=[pl.BlockSpec((8, tr, cols), lambda i: (0, i, 0))],
        out_specs=pl.BlockSpec((tr, cols), lambda i: (i, 0)), out_shape=_sds((rows, cols), F32),
        compiler_params=_params(("parallel",)), name=name,
    )(parts)


def _adam_update(w, g, m, v):
    m = ADAM_B1 * m + (1.0 - ADAM_B1) * g
    v = ADAM_B2 * v + (1.0 - ADAM_B2) * jnp.square(g)
    m_hat = m / (1.0 - ADAM_B1 ** ADAM_STEP)
    v_hat = v / (1.0 - ADAM_B2 ** ADAM_STEP)
    return -ADAM_LR * (m_hat / (jnp.sqrt(v_hat) + ADAM_EPS) + ADAM_WD * w), m, v


SMALL_ROWS = {"attn_norm_w": 0, "mlp_norm_w": 1, "q_lat_norm_w": 2, "kv_lat_norm_w": 3, "q_norm_w": 4, "k_norm_w": 5,
              "mla_out_norm_w": 6, "gdn_norm_w": 10, "a_log": 11, "dt_bias": 12}
LOSS_ROW = 13
SMALL_SHAPE = (16, 1024)


def _pack_small_partials(d_attn_nw, d_mlp_nw, d_ln, d_qk_nw, d_mix_nw, d_scal, conv_parts, sq):
    D = d_attn_nw.shape[1]

    def body(an_ref, mn_ref, ln_ref, qk_ref, mix_ref, sc_ref, cq_ref, ck_ref, cv_ref, sq_ref, a_ref, c_ref):
        a_ref[...] = jnp.zeros_like(a_ref)
        a_ref[0:1, :D] = an_ref[...]
        a_ref[1:2, :D] = mn_ref[...]
        a_ref[2:4, :LORA] = ln_ref[...]
        for row, base in ((4, 0), (5, 2)):
            rope = qk_ref[base + 1:base + 2, :]
            a_ref[row:row + 1, :QK_DIM] = jnp.concatenate(
                [qk_ref[base:base + 1, :], rope[:, :ROPE_HALF], rope[:, 2 * ROPE_HALF:3 * ROPE_HALF]], axis=1)
        a_ref[6:6 + HEADS, :HEAD_DIM] = mix_ref[0:HEADS, :]
        a_ref[10:11, :HEAD_DIM] = mix_ref[HEADS:HEADS + 1, :]
        a_ref[11:13, :128] = sc_ref[0:2, :]
        a_ref[LOSS_ROW:LOSS_ROW + 1, :128] = jnp.zeros((1, 128), F32) + jnp.sum(sq_ref[:, 0:1, 0:1]) * (0.5 / D)
        c_ref[...] = jnp.concatenate([cq_ref[...], ck_ref[...], cv_ref[...]], axis=1)

    return pl.pallas_call(
        body, out_shape=[_sds(SMALL_SHAPE, F32), _sds((CONV_TAPS, GQKV_W), F32)], name="pack_small_partials",
    )(d_attn_nw, d_mlp_nw, d_ln, d_qk_nw, d_mix_nw, d_scal, *conv_parts, sq)


def _adamw_small(parts, conv_parts, w, m, v):
    names = tuple(SMALL_ROWS) + ("conv_w",)
    cols = w["conv_w"].shape[2]

    def body(*refs):
        p_ref, c_ref = refs[:2]
        n = len(names)
        w_refs, m_refs, v_refs = (dict(zip(names, refs[2 + k * n:2 + (k + 1) * n])) for k in range(3))
        loss_ref = refs[2 + 3 * n]
        out = [dict(zip(names, refs[3 + (3 + k) * n:3 + (4 + k) * n])) for k in range(4)]
        acc_ref, cacc_ref = refs[3 + 7 * n:]
        acc, cacc = p_ref[0], c_ref[0]
        for d in range(1, 8):
            acc, cacc = acc + p_ref[d], cacc + c_ref[d]
        acc_ref[...] = acc
        cacc_ref[...] = cacc
        loss_ref[...] = acc_ref[LOSS_ROW:LOSS_ROW + 1, 0:1]
        chip = 2 * lax.axis_index("x") + lax.axis_index("y")
        for name in names:
            shape = w_refs[name].shape
            if name == "conv_w":
                g = sum(jnp.where(chip == s, cacc_ref[:, s * cols:(s + 1) * cols], 0.0) for s in range(4))[None]
            else:
                row = SMALL_ROWS[name]
                g = acc_ref[row:row + math.prod(shape[:-1]), 0:shape[-1]].reshape(shape)
            delta, new_m, new_v = _adam_update(w_refs[name][...], g, m_refs[name][...], v_refs[name][...])
            for ref, val in zip((o[name] for o in out), (g, delta, new_m, new_v)):
                ref[...] = val

    ins = [x[n] for x in (w, m, v) for n in names]
    shapes = [_sds(w[n].shape, F32) for n in names]
    outs = pl.pallas_call(
        body, out_shape=[_sds((1, 1), F32)] + shapes * 4,
        scratch_shapes=[pltpu.VMEM(parts.shape[1:], F32), pltpu.VMEM(conv_parts.shape[1:], F32)], name="adamw_small",
    )(parts, conv_parts, *ins)
    n = len(names)
    return (outs[0],) + tuple(dict(zip(names, outs[1 + k * n:1 + (k + 1) * n])) for k in range(4))


def _adamw(w, g, m, v, name):
    rows, cols = w.shape[0], w.shape[-1]
    if w.ndim == 3:
        tr = max(d for d in range(1, rows + 1) if rows % d == 0 and d * 8 * cols * 4 * 14 <= VMEM_LIMIT // 2)
    else:
        tr = _row_tile(rows, 7 * cols * 4, 4 * 1024 * 1024)

    def body(w_ref, g_ref, m_ref, v_ref, d_ref, mo_ref, vo_ref):
        d_ref[...], mo_ref[...], vo_ref[...] = _adam_update(w_ref[...], g_ref[...], m_ref[...], v_ref[...])

    block = (tr,) + w.shape[1:]
    spec = pl.BlockSpec(block, lambda i: (i,) + (0,) * (len(block) - 1))
    return pl.pallas_call(
        body, grid=(rows // tr,), in_specs=[spec] * 4, out_specs=[spec] * 3, out_shape=[_sds(w.shape, F32)] * 3,
        compiler_params=_params(("parallel",)), name=name,
    )(w, g, m, v)


def kernel(x, positions, attn_norm_w, w_in, q_lat_norm_w, w_uq, kv_lat_norm_w, w_ukv, q_norm_w, k_norm_w, mla_out_norm_w, conv_w, a_log, dt_bias, gdn_norm_w, w_out, mlp_norm_w, w_up, w_down, loss_target, m_attn_norm_w, m_w_in, m_q_lat_norm_w, m_w_uq, m_kv_lat_norm_w, m_w_ukv, m_q_norm_w, m_k_norm_w, m_mla_out_norm_w, m_conv_w, m_a_log, m_dt_bias, m_gdn_norm_w, m_w_out, m_mlp_norm_w, m_w_up, m_w_down, v_attn_norm_w, v_w_in, v_q_lat_norm_w, v_w_uq, v_kv_lat_norm_w, v_w_ukv, v_q_norm_w, v_k_norm_w, v_mla_out_norm_w, v_conv_w, v_a_log, v_dt_bias, v_gdn_norm_w, v_w_out, v_mlp_norm_w, v_w_up, v_w_down):
    w = dict(zip(WEIGHTS, (attn_norm_w, w_in, q_lat_norm_w, w_uq, kv_lat_norm_w, w_ukv, q_norm_w, k_norm_w, mla_out_norm_w, conv_w,
                           a_log, dt_bias, gdn_norm_w, w_out, mlp_norm_w, w_up, w_down)))
    m = dict(zip(WEIGHTS, (m_attn_norm_w, m_w_in, m_q_lat_norm_w, m_w_uq, m_kv_lat_norm_w, m_w_ukv, m_q_norm_w, m_k_norm_w,
                           m_mla_out_norm_w, m_conv_w, m_a_log, m_dt_bias, m_gdn_norm_w, m_w_out, m_mlp_norm_w, m_w_up, m_w_down)))
    v = dict(zip(WEIGHTS, (v_attn_norm_w, v_w_in, v_q_lat_norm_w, v_w_uq, v_kv_lat_norm_w, v_w_ukv, v_q_norm_w, v_k_norm_w,
                           v_mla_out_norm_w, v_conv_w, v_a_log, v_dt_bias, v_gdn_norm_w, v_w_out, v_mlp_norm_w, v_w_up, v_w_down)))
    B, S, D = x.shape
    T = B * S
    x2, pos, target = x.reshape(T, D), positions.reshape(T, 1), loss_target.reshape(T, D)
    seq = lambda a: a.reshape(B, S, a.shape[-1])
    tok = lambda a: a.reshape(T, a.shape[-1])
    local = {n: w[n][0] for n in SHARDED}

    g_in, g_uq, g_ukv, g_conv = _all_gather([jnp.swapaxes(w_in, 1, 2)[0].astype(BF16), local["w_uq"].astype(BF16),
                                             local["w_ukv"].astype(BF16), local["conv_w"]], "gather_first_weights")
    w_in_p = _widen_w_in_t(g_in.reshape(-1, D))
    w_mla = _stack_mla(_from_column_shards(g_uq), _from_column_shards(g_ukv))
    conv_full = _from_column_shards(g_conv)
    ln_w = jnp.concatenate([q_lat_norm_w, kv_lat_norm_w], axis=0)
    qk_nw = _qk_norm_rows(q_norm_w, k_norm_w)
    rope_rows = _rope_rows()
    scal = _rows8([jnp.pad(a_log, ((0, 0), (0, 128 - HEADS))), jnp.pad(dt_bias, ((0, 0), (0, 128 - HEADS)))])
    mix_nw = _rows8([mla_out_norm_w[0], gdn_norm_w])

    xn, lat, gqkv, gz, gab = _in_proj_fwd(x2, attn_norm_w, w_in_p)
    q, k, v_att = _mla_pre_fwd(lat, pos, ln_w, w_mla, qk_nw, rope_rows)
    ao, lse, g_down = _attn_fwd(seq(q), seq(k), seq(v_att), [local["w_down"].astype(BF16)])
    gq, gk, gv = _gdn_pre_fwd(seq(gqkv), conv_full)
    go, states, powers, solutions, g_out, w_up_b = _gdn_chunk_fwd(gq, gk, gv, seq(gab), scal,
                                                                  [local["w_out"].astype(BF16), local["w_up"].astype(BF16)])
    w_out_b = g_out.reshape(-1, D)
    w_down_b = g_down.reshape(-1, D)
    mix, h2 = _mix_fwd(tok(ao), tok(go), gz, mix_nw, w_out_b, x2)
    hn, act, dy, sq = _mlp_fwd(h2, mlp_norm_w, w_up_b, w_down_b, target)

    dh, d_up, d_mlp_nw = _mlp_bwd(h2, mlp_norm_w, act, w_up_b, w_down_b, dy)
    p_down = _wgrad(act, dy, "wgrad_down").reshape(4, -1, D)
    p_up = _wgrad(hn, d_up, "wgrad_up", column_shards=4)
    d_ao, d_go, d_gz, d_mix_nw = _mix_bwd(tok(ao), tok(go), gz, mix_nw, w_out_b, dh)
    p_out = _wgrad(mix, dh, "wgrad_out").reshape(4, -1, D)
    d_gq, d_gk, d_gv, d_gab, d_scal, s_up, s_out = _gdn_chunk_bwd(gq, gk, gv, seq(gab), scal, states, powers, solutions, seq(d_go),
                                                                  [p_up, p_out])
    early = ("w_up", "w_out", "w_down")
    dxq, dxk, dxv, dcq, dck, dcv, g_up, g_out = _gdn_pre_bwd(seq(gqkv), conv_full, d_gq, d_gk, d_gv,
                                                             [_sum_slots(s_up, "sum_w_up"), _sum_slots(s_out, "sum_w_out")])
    dq, dk, dv, s_down = _attn_bwd(seq(q), seq(k), seq(v_att), ao, lse, seq(d_ao), [p_down])
    d_lat, d_ln, d_w_mla, d_qk_nw = _mla_pre_bwd(lat, pos, ln_w, w_mla, qk_nw, rope_rows, tok(dq), tok(dk), tok(dv), [])
    d_pieces = [d_lat, tok(dxq), tok(dxk), tok(dxv), d_gz, tok(d_gab)]
    p_in = _narrow_w_in_t(_wgrad_pieces(d_pieces, xn, "wgrad_in")).reshape(4, -1, D)
    p_uq, p_ukv = (_column_shards(a).astype(BF16) for a in _unstack_mla(d_w_mla))
    grad_x2, d_attn_nw, s_in, s_uq, s_ukv, g_down = _in_proj_bwd(d_pieces, w_in_p, x2, attn_norm_w, dh, [p_in, p_uq, p_ukv],
                                                                 [_sum_slots(s_down, "sum_w_down")])
    early_grads = [g_up, g_out, g_down]
    small_buf, conv_buf = _pack_small_partials(d_attn_nw, d_mlp_nw, d_ln, d_qk_nw, d_mix_nw, d_scal, (dcq, dck, dcv), sq)

    late = ("w_in", "w_uq", "w_ukv")
    *late_grads, s_small, s_conv = _exchange_halves([_sum_slots(s, "sum_" + n) for n, s in zip(late, (s_in, s_uq, s_ukv))],
                                                    [small_buf, conv_buf])
    names = early + late
    grad = {n: g.reshape(-1, g.shape[-1]) for n, g in zip(names, list(early_grads) + list(late_grads))}

    loss, g_small, delta, new_m, new_v = _adamw_small(s_small, s_conv, w, m, v)
    grad.update(g_small)
    for n in names:
        if n == "w_in":
            stored = lambda a: jnp.transpose(a, (2, 0, 1))
            outs = _adamw(stored(w[n]), grad[n][:, None, :], stored(m[n]), stored(v[n]), "adamw_" + n)
            grad[n], delta[n], new_m[n], new_v[n] = (jnp.transpose(a, (1, 2, 0)) for a in (grad[n][:, None, :], *outs))
        else:
            delta[n], new_m[n], new_v[n] = _adamw(local[n], grad[n], m[n][0], v[n][0], "adamw_" + n)
    def in_order(d):
        return [d[n].reshape(w[n].shape) for n in WEIGHTS]

    return (loss.reshape(()), grad_x2.reshape(B, S, D), *in_order(grad), *in_order(delta), *in_order(new_m), *in_order(new_v))
```

```python
import functools
import math

import jax
import jax.numpy as jnp
from jax import lax
from jax.experimental import pallas as pl
from jax.experimental.pallas import tpu as pltpu

F32 = jnp.float32
BF16 = jnp.bfloat16
MESH = pl.DeviceIdType.MESH

EPS = 1e-6
HEADS = 4
HEAD_DIM = 128
ROPE_DIM = 64
ROPE_HALF = 32
QK_DIM = 192
QK_PAD = 256
LORA = 256
CHUNK = 64
SOLVE_POWERS = 5
CONV_TAPS = 4
ROPE_THETA = 10000.0
ATTN_SCALE = QK_DIM ** -0.5

LAT_W = 640
GQKV_W = 3 * HEADS * HEAD_DIM
GZ_W = HEADS * HEAD_DIM
GAB_W = 128
PROJ_SPLITS = ((0, LAT_W), (LAT_W, LAT_W + GQKV_W), (LAT_W + GQKV_W, LAT_W + GQKV_W + GZ_W),
               (LAT_W + GQKV_W + GZ_W, LAT_W + GQKV_W + GZ_W + GAB_W))
PROJ_W = PROJ_SPLITS[-1][1]

ADAM_LR = 0.001
ADAM_B1 = 0.9
ADAM_B2 = 0.999
ADAM_EPS = 1e-08
ADAM_WD = 0.01
ADAM_STEP = 10

TOKEN_TILE = 512
WGRAD_TOKEN_TILE = 1024
MLP_TOKEN_TILE = 512
FF_TILE = 1024
ATTN_TILE = 512
ATTN_HEADS_PER_STEP = 2
WGRAD_OUT_BYTES = 8 * 1024 * 1024
VMEM_LIMIT = 48 * 1024 * 1024

SHARDED = ("w_in", "w_uq", "w_ukv", "conv_w", "w_out", "w_up", "w_down")
WEIGHTS = ("attn_norm_w", "w_in", "q_lat_norm_w", "w_uq", "kv_lat_norm_w", "w_ukv", "q_norm_w", "k_norm_w", "mla_out_norm_w",
           "conv_w", "a_log", "dt_bias", "gdn_norm_w", "w_out", "mlp_norm_w", "w_up", "w_down")


def _sds(shape, dtype):
    return jax.ShapeDtypeStruct(shape, dtype)


def _params(semantics):
    return pltpu.CompilerParams(dimension_semantics=semantics, vmem_limit_bytes=VMEM_LIMIT)


def _block(n):
    for b in (512, 256, 128):
        if n % b == 0:
            return b
    return n


def _dg(a, b, ca, cb, prec):
    lead = a.ndim - 2
    batch = (tuple(range(lead)),) * 2
    return lax.dot_general(a, b, (((ca + lead,), (cb + lead,)), batch), precision=prec, preferred_element_type=F32)


def _split_bf16(a):
    hi = a.astype(BF16)
    return hi, (a - hi.astype(F32)).astype(BF16)


def _dot_bf16(a, b, ca, cb):
    return _dg(a.astype(BF16), b.astype(BF16), ca, cb, None)


def _dot_bf16x3(a, b, ca, cb):
    a_hi, a_lo = _split_bf16(a)
    b_hi, b_lo = _split_bf16(b)
    lead = a.ndim - 2
    return _dg(jnp.concatenate([a_hi, a_hi, a_lo], axis=ca + lead), jnp.concatenate([b_hi, b_lo, b_hi], axis=cb + lead), ca, cb, None)


def _matmul_family(dot):
    def nn_raw(a, b):
        return dot(a, b, 1, 0)

    def nt_raw(a, b):
        return dot(a, b, 1, 1)

    def tn_raw(a, b):
        return dot(a, b, 0, 0)

    @jax.custom_vjp
    def nn(a, b):
        return nn_raw(a, b)

    nn.defvjp(lambda a, b: (nn_raw(a, b), (a, b)), lambda r, g: (nt_raw(g, r[1]), tn_raw(r[0], g)))

    @jax.custom_vjp
    def nt(a, b):
        return nt_raw(a, b)

    nt.defvjp(lambda a, b: (nt_raw(a, b), (a, b)), lambda r, g: (nn_raw(g, r[1]), tn_raw(g, r[0])))

    @jax.custom_vjp
    def tn(a, b):
        return tn_raw(a, b)

    tn.defvjp(lambda a, b: (tn_raw(a, b), (a, b)), lambda r, g: (nt_raw(r[1], g), nn_raw(r[0], g)))
    return nn, nt, tn


_bf_nn, _bf_nt, _bf_tn = _matmul_family(_dot_bf16)
_hi_nn, _hi_nt, _hi_tn = _matmul_family(_dot_bf16x3)


def _lower_powers(lmat):
    powers = []
    while 2 ** (len(powers) + 1) < lmat.shape[-1]:
        powers.append(_dot_bf16x3(powers[-1] if powers else lmat, powers[-1] if powers else lmat, 1, 0))
    return powers


@jax.custom_vjp
def _unit_lower_solve(lmat, rhs):
    return _unit_lower_solve_fwd(lmat, rhs)[0]


def _unit_lower_solve_fwd(lmat, rhs):
    powers = _lower_powers(lmat)
    x = rhs - _dot_bf16x3(lmat, rhs, 1, 0)
    for p in powers:
        x = x + _dot_bf16x3(p, x, 1, 0)
    return x, (lmat, powers, x)


def _unit_lower_solve_bwd(res, g):
    lmat, powers, x = res
    y = g - _dot_bf16x3(lmat, g, 0, 0)
    for p in powers:
        y = y + _dot_bf16x3(p, y, 0, 0)
    return -_dot_bf16x3(y, x, 1, 1), y


_unit_lower_solve.defvjp(_unit_lower_solve_fwd, _unit_lower_solve_bwd)


@jax.custom_vjp
def _unit_lower_solve_kept(lmat, rhs, powers, x):
    return x


_unit_lower_solve_kept.defvjp(
    lambda lmat, rhs, powers, x: (x, (lmat, powers, x)),
    lambda res, g: _unit_lower_solve_bwd(res, g) + ([jnp.zeros_like(p) for p in res[1]], jnp.zeros_like(res[2])))


@jax.custom_vjp
def _lane_halves(x):
    n = x.shape[-1] // 2
    return x[..., :n], x[..., n:]


_lane_halves.defvjp(lambda x: (_lane_halves(x), None), lambda _, g: (jnp.concatenate(g, axis=-1),))


@jax.custom_vjp
def _row_halves(x):
    n = x.shape[-2] // 2
    return x[..., :n, :], x[..., n:, :]


_row_halves.defvjp(lambda x: (_row_halves(x), None), lambda _, g: (jnp.concatenate(g, axis=-2),))


@jax.custom_vjp
def _swap_halves(t):
    return pltpu.roll(t, 64, 1)


_swap_halves.defvjp(lambda t: (pltpu.roll(t, 64, 1), None), lambda _, g: (pltpu.roll(g, 64, 1),))


@functools.partial(jax.custom_vjp, nondiff_argnums=(2,))
def _shift_rows(x, keep, s):
    return pltpu.roll(x, s, 0) * keep


def _shift_rows_fwd(x, keep, s):
    return pltpu.roll(x, s, 0) * keep, keep


def _shift_rows_bwd(s, keep, g):
    return pltpu.roll(g * keep, keep.shape[0] - s, 0), jnp.zeros_like(keep)


_shift_rows.defvjp(_shift_rows_fwd, _shift_rows_bwd)


def _sigmoid(x):
    return 0.5 * jnp.tanh(0.5 * x) + 0.5


def _softplus(x):
    return jnp.maximum(x, 0.0) + jnp.log(1.0 + jnp.exp(jnp.minimum(x, -x)))


def _silu(x):
    return x * _sigmoid(x)


def _rms(x, w, n=None):
    n = x.shape[-1] if n is None else n
    r = lax.rsqrt(jnp.sum(x * x, axis=-1, keepdims=True) * (1.0 / n) + EPS)
    return x * r * w


def _rope(t, cos_f, sin_f):
    return t * cos_f + _swap_halves(t) * sin_f


def _rope_tables(pos_col, freq_row, sign_row):
    ang = pos_col.astype(F32) * freq_row
    return jnp.cos(ang), jnp.sin(ang) * sign_row


def _onehot_row(lane):
    return (lax.broadcasted_iota(jnp.int32, (1, 128), 1) == lane).astype(F32)


def _row_spec(tm, w):
    return pl.BlockSpec((tm, w), lambda i: (i, 0))


def _const_spec(shape):
    return pl.BlockSpec(shape, lambda *_: (0,) * len(shape))


def _in_proj_fwd(x2, w_an, w_in_p):
    T, D = x2.shape
    tm = min(TOKEN_TILE, T)

    def body(x_ref, wn_ref, w_ref, xn_ref, lat_ref, gqkv_ref, gz_ref, gab_ref):
        x = x_ref[...]
        r = lax.rsqrt(jnp.mean(x * x, axis=-1, keepdims=True) + EPS)
        xn = (x * r * wn_ref[...]).astype(BF16)
        xn_ref[...] = xn
        for ref, (a, b) in zip((lat_ref, gqkv_ref, gz_ref, gab_ref), PROJ_SPLITS):
            ref[...] = _dg(xn, w_ref[a:b, :], 1, 1, None)

    widths = [b - a for a, b in PROJ_SPLITS]
    return pl.pallas_call(
        body, grid=(T // tm,),
        in_specs=[_row_spec(tm, D), _const_spec((1, D)), _const_spec((PROJ_W, D))],
        out_specs=[_row_spec(tm, D)] + [_row_spec(tm, w) for w in widths],
        out_shape=[_sds((T, D), BF16)] + [_sds((T, w), F32) for w in widths],
        compiler_params=_params(("parallel",)), name="in_proj_fwd",
    )(x2, w_an, w_in_p)


def _in_proj_bwd(pieces, w_in_p, x2, w_an, dh, partials, halves):
    T, D = x2.shape
    tm = min(TOKEN_TILE, T)
    widths = [p.shape[1] for p in pieces]
    starts = [sum(widths[:i]) for i in range(len(widths))]
    assert sum(widths) == PROJ_W
    npc, ns, nh = len(pieces), len(partials), len(halves)

    def body(*refs):
        piece_refs = refs[:npc]
        w_ref, x_ref, wn_ref, dh_ref = refs[npc:npc + 4]
        src_refs = refs[npc + 4:npc + 4 + ns + nh]
        dx_ref, dwn_ref = refs[npc + 4 + ns + nh:npc + 6 + ns + nh]
        dst_refs = refs[npc + 6 + ns + nh:npc + 6 + 2 * (ns + nh)]
        sems = refs[npc + 6 + 2 * (ns + nh):]

        def riders():
            scatter = _scatter_copies(src_refs[:ns], dst_refs[:ns], *sems[:3])
            swap = _swap_copies(src_refs[ns:], dst_refs[ns:], *sems[3:])
            return scatter[0] + swap[0], scatter[1] + swap[1]

        @pl.when(pl.program_id(0) == 0)
        def _():
            for start in riders()[0]:
                start()
            dwn_ref[...] = jnp.zeros_like(dwn_ref)

        dxn = jnp.zeros((tm, D), F32)
        for ref, a, width in zip(piece_refs, starts, widths):
            dxn += _dg(ref[...], w_ref[a:a + width, :], 1, 0, None)
        _, pull = jax.vjp(_rms, x_ref[...], wn_ref[...])
        dx, dwn = pull(dxn)
        dx_ref[...] = dx + dh_ref[...]
        dwn_ref[...] += dwn

        @pl.when(pl.program_id(0) == T // tm - 1)
        def _():
            for wait in riders()[1]:
                wait()

    return pl.pallas_call(
        body, grid=(T // tm,),
        in_specs=[_row_spec(tm, w) for w in widths] + [_const_spec((PROJ_W, D)), _row_spec(tm, D), _const_spec((1, D)),
                                                       _row_spec(tm, D)] + [_ANY] * (ns + nh),
        out_specs=[_row_spec(tm, D), _const_spec((1, D))] + [_ANY] * (ns + nh),
        out_shape=[_sds((T, D), F32), _sds((1, D), F32)] + [_scattered_shape(p) for p in partials]
                  + [_swapped_shape(h) for h in halves],
        scratch_shapes=_scatter_scratch(ns) + _swap_scratch(nh),
        compiler_params=_params(("arbitrary",)), name="in_proj_bwd",
    )(*pieces, w_in_p, x2, w_an, dh, *partials, *halves)


def _wgrad_pieces(pieces, b, name):
    T, k2 = b.shape
    tt = min(WGRAD_TOKEN_TILE, T)
    widths = [p.shape[1] for p in pieces]
    starts = [sum(widths[:i]) for i in range(len(widths))]
    k1 = sum(widths)

    def body(*refs):
        piece_refs, (b_ref, o_ref, acc_ref) = refs[:len(pieces)], refs[len(pieces):]
        t = pl.program_id(0)

        @pl.when(t == 0)
        def _():
            acc_ref[...] = jnp.zeros_like(acc_ref)

        bt = b_ref[...].astype(BF16)
        for ref, r0, width in zip(piece_refs, starts, widths):
            acc_ref[r0:r0 + width, :] += jnp.dot(ref[...].T, bt, preferred_element_type=F32)

        @pl.when(t == T // tt - 1)
        def _():
            o_ref[...] = acc_ref[...].astype(o_ref.dtype)

    return pl.pallas_call(
        body, grid=(T // tt,),
        in_specs=[pl.BlockSpec((tt, w), lambda t: (t, 0)) for w in widths] + [pl.BlockSpec((tt, k2), lambda t: (t, 0))],
        out_specs=_const_spec((k1, k2)), out_shape=_sds((k1, k2), BF16), scratch_shapes=[pltpu.VMEM((k1, k2), F32)],
        compiler_params=_params(("arbitrary",)), name=name,
    )(*pieces, b)


def _wgrad(a, b, name, column_shards=1, out_dtype=BF16):
    T, k1 = a.shape
    k2 = b.shape[1]
    per_shard = k2 // column_shards
    tt = min(WGRAD_TOKEN_TILE, T)
    b1 = k1
    while b1 * k2 * 4 > WGRAD_OUT_BYTES and b1 % 256 == 0:
        b1 //= 2
    step = _block(per_shard)

    def body(a_ref, b_ref, o_ref, acc_ref):
        t = pl.program_id(1)

        @pl.when(t == 0)
        def _():
            acc_ref[...] = jnp.zeros_like(acc_ref)

        a_t = a_ref[...].astype(BF16).T
        for c0 in range(0, k2, step):
            part = jnp.dot(a_t, b_ref[:, c0:c0 + step].astype(BF16), preferred_element_type=F32)
            if column_shards == 1:
                acc_ref[:, c0:c0 + step] += part
            else:
                acc_ref[c0 // per_shard, :, c0 % per_shard:c0 % per_shard + step] += part

        @pl.when(t == T // tt - 1)
        def _():
            o_ref[...] = acc_ref[...].astype(o_ref.dtype)

    if column_shards == 1:
        block, out_spec, out_shape = (b1, k2), pl.BlockSpec((b1, k2), lambda i, t: (i, 0)), _sds((k1, k2), out_dtype)
    else:
        block = (column_shards, b1, per_shard)
        out_spec, out_shape = pl.BlockSpec(block, lambda i, t: (0, i, 0)), _sds((column_shards, k1, per_shard), out_dtype)
    return pl.pallas_call(
        body, grid=(k1 // b1, T // tt),
        in_specs=[pl.BlockSpec((tt, b1), lambda i, t: (t, i)), pl.BlockSpec((tt, k2), lambda i, t: (t, 0))],
        out_specs=out_spec, out_shape=out_shape, scratch_shapes=[pltpu.VMEM(block, F32)],
        compiler_params=_params(("parallel", "arbitrary")), name=name,
    )(a, b)


@jax.custom_vjp
def _lane_blocks(x):
    return tuple(x[:, i:i + 128] for i in range(0, x.shape[1], 128))


_lane_blocks.defvjp(lambda x: (_lane_blocks(x), None), lambda _, g: (jnp.concatenate(g, axis=1),))


def _mla_pre_fn(q_lat, kv_lat, kpe, ln_q, ln_kv, w_q, w_kv, qn_n, qn_p, kn_n, kn_p, cos_f, sin_f):
    qn = _rms(q_lat, ln_q)
    kvn = _rms(kv_lat, ln_kv)
    kp = _rope(_rms(kpe, kn_p, ROPE_DIM), cos_f, sin_f)
    q_blocks = _lane_blocks(_bf_nn(qn, w_q))
    kv_blocks = _lane_blocks(_bf_nn(kvn, w_kv))
    outs = []
    for h in range(HEADS):
        outs.append(_rms(q_blocks[h], qn_n))
        outs.append(_rope(_rms(q_blocks[HEADS + h], qn_p, ROPE_DIM), cos_f, sin_f))
        outs.append(_rms(kv_blocks[h], kn_n))
        outs.append(kv_blocks[HEADS + h])
    return tuple(outs) + (kp,)


def _mla_pre_operands(lat_ref, pos_ref, ln_ref, w_ref, nw_ref, rope_ref):
    cos_f, sin_f = _rope_tables(pos_ref[...], rope_ref[0:1, :], rope_ref[1:2, :])
    side_by_side = lambda blocks: jnp.concatenate([w_ref[i].astype(F32) for i in blocks], axis=1)
    diff = (lat_ref[:, 0:LORA], lat_ref[:, LORA:2 * LORA], lat_ref[:, 2 * LORA:LAT_W], ln_ref[0:1, :], ln_ref[1:2, :],
            side_by_side(range(2 * HEADS)), side_by_side(range(2 * HEADS, 4 * HEADS)),
            nw_ref[0:1, :], nw_ref[1:2, :], nw_ref[2:3, :], nw_ref[3:4, :])
    return diff, cos_f, sin_f


def _mla_pre_fwd(lat, pos, ln_w, w_mla, nw, rope_rows):
    T = lat.shape[0]
    tm = min(TOKEN_TILE, T)

    def body(lat_ref, pos_ref, ln_ref, w_ref, nw_ref, rope_ref, q_ref, k_ref, v_ref):
        diff, cos_f, sin_f = _mla_pre_operands(lat_ref, pos_ref, ln_ref, w_ref, nw_ref, rope_ref)
        outs = _mla_pre_fn(*diff, cos_f, sin_f)
        kp = outs[-1].astype(BF16)
        for h in range(HEADS):
            q_n, q_p, k_n, v = outs[4 * h:4 * h + 4]
            q_ref[:, h * QK_PAD:h * QK_PAD + HEAD_DIM] = q_n.astype(BF16)
            q_ref[:, h * QK_PAD + HEAD_DIM:(h + 1) * QK_PAD] = q_p.astype(BF16)
            k_ref[:, h * QK_PAD:h * QK_PAD + HEAD_DIM] = k_n.astype(BF16)
            k_ref[:, h * QK_PAD + HEAD_DIM:(h + 1) * QK_PAD] = kp
            v_ref[:, h * HEAD_DIM:(h + 1) * HEAD_DIM] = v.astype(BF16)

    return pl.pallas_call(
        body, grid=(T // tm,),
        in_specs=[_row_spec(tm, LAT_W), _row_spec(tm, 1), _const_spec((2, LORA)), _const_spec((4 * HEADS, LORA, 128)),
                  _const_spec((8, 128)), _const_spec((8, 128))],
        out_specs=[_row_spec(tm, HEADS * QK_PAD), _row_spec(tm, HEADS * QK_PAD), _row_spec(tm, HEADS * HEAD_DIM)],
        out_shape=[_sds((T, HEADS * QK_PAD), BF16), _sds((T, HEADS * QK_PAD), BF16), _sds((T, HEADS * HEAD_DIM), BF16)],
        compiler_params=_params(("parallel",)), name="mla_pre_fwd",
    )(lat, pos, ln_w, w_mla, nw, rope_rows)


def _mla_pre_bwd(lat, pos, ln_w, w_mla, nw, rope_rows, dq, dk, dv, halves):
    T = lat.shape[0]
    tm = min(TOKEN_TILE, T)
    ns = len(halves)

    def body(*refs):
        lat_ref, pos_ref, ln_ref, w_ref, nw_ref, rope_ref, dq_ref, dk_ref, dv_ref = refs[:9]
        src_refs = refs[9:9 + ns]
        dlat_ref, dln_ref, dw_ref, dnw_ref = refs[9 + ns:13 + ns]
        dst_refs = refs[13 + ns:13 + 2 * ns]
        sems = refs[13 + 2 * ns:]

        @pl.when(pl.program_id(0) == 0)
        def _():
            for start in _swap_copies(src_refs, dst_refs, *sems)[0]:
                start()
            dln_ref[...] = jnp.zeros_like(dln_ref)
            dw_ref[...] = jnp.zeros_like(dw_ref)
            dnw_ref[...] = jnp.zeros_like(dnw_ref)

        diff, cos_f, sin_f = _mla_pre_operands(lat_ref, pos_ref, ln_ref, w_ref, nw_ref, rope_ref)
        _, pull = jax.vjp(lambda *a: _mla_pre_fn(*a, cos_f, sin_f), *diff)
        cts = []
        d_kp = jnp.zeros((tm, 128), F32)
        for h in range(HEADS):
            cts.append(dq_ref[:, h * QK_PAD:h * QK_PAD + HEAD_DIM])
            cts.append(dq_ref[:, h * QK_PAD + HEAD_DIM:(h + 1) * QK_PAD])
            cts.append(dk_ref[:, h * QK_PAD:h * QK_PAD + HEAD_DIM])
            cts.append(dv_ref[:, h * HEAD_DIM:(h + 1) * HEAD_DIM])
            d_kp += dk_ref[:, h * QK_PAD + HEAD_DIM:(h + 1) * QK_PAD]
        d_ql, d_kvl, d_kpe, d_lnq, d_lnkv, d_wq, d_wkv, d_qn_n, d_qn_p, d_kn_n, d_kn_p = pull(tuple(cts) + (d_kp,))
        d_w = [d[:, i:i + 128] for d in (d_wq, d_wkv) for i in range(0, d.shape[1], 128)]
        dlat_ref[:, 0:LORA] = d_ql.astype(BF16)
        dlat_ref[:, LORA:2 * LORA] = d_kvl.astype(BF16)
        dlat_ref[:, 2 * LORA:LAT_W] = d_kpe.astype(BF16)
        dln_ref[0:1, :] += d_lnq
        dln_ref[1:2, :] += d_lnkv
        for i in range(4 * HEADS):
            dw_ref[i] += d_w[i]
        for i, d in enumerate((d_qn_n, d_qn_p, d_kn_n, d_kn_p)):
            dnw_ref[i:i + 1, :] += d

        @pl.when(pl.program_id(0) == T // tm - 1)
        def _():
            for wait in _swap_copies(src_refs, dst_refs, *sems)[1]:
                wait()

    return pl.pallas_call(
        body, grid=(T // tm,),
        in_specs=[_row_spec(tm, LAT_W), _row_spec(tm, 1), _const_spec((2, LORA)), _const_spec((4 * HEADS, LORA, 128)),
                  _const_spec((8, 128)), _const_spec((8, 128)),
                  _row_spec(tm, HEADS * QK_PAD), _row_spec(tm, HEADS * QK_PAD), _row_spec(tm, HEADS * HEAD_DIM)] + [_ANY] * ns,
        out_specs=[_row_spec(tm, LAT_W), _const_spec((2, LORA)), _const_spec((4 * HEADS, LORA, 128)), _const_spec((8, 128))]
                  + [_ANY] * ns,
        out_shape=[_sds((T, LAT_W), BF16), _sds((2, LORA), F32), _sds((4 * HEADS, LORA, 128), F32), _sds((8, 128), F32)]
                  + [_swapped_shape(h) for h in halves],
        scratch_shapes=_swap_scratch(ns),
        compiler_params=_params(("arbitrary",)), name="mla_pre_bwd",
    )(lat, pos, ln_w, w_mla, nw, rope_rows, dq, dk, dv, *halves)


def _causal_mask(i, j, tq, tk):
    row = i * tq + lax.broadcasted_iota(jnp.int32, (tq, tk), 0)
    col = j * tk + lax.broadcasted_iota(jnp.int32, (tq, tk), 1)
    return col <= row


def _attn_fwd(q, k, v, shards):
    B, S, _ = q.shape
    t = min(ATTN_TILE, S)
    nq = S // t
    ns = len(shards)

    hp = ATTN_HEADS_PER_STEP
    qk = lambda h: slice(h * QK_PAD, (h + 1) * QK_PAD)
    vd = lambda h: slice(h * HEAD_DIM, (h + 1) * HEAD_DIM)

    def body(*refs):
        q_ref, k_ref, v_ref = refs[:3]
        src_refs = refs[3:3 + ns]
        o_ref, lse_ref = refs[3 + ns:5 + ns]
        dst_refs = refs[5 + ns:5 + 2 * ns]
        sems = refs[5 + 2 * ns:]
        b, g, i = pl.program_id(0), pl.program_id(1), pl.program_id(2)
        qb = [q_ref[0, :, qk(h)] for h in range(hp)]

        step_no = (b * (HEADS // hp) + g) * nq + i
        for phase, at in enumerate((0, (3 * B * (HEADS // hp) * nq) // 4)):
            @pl.when(step_no == at)
            def _(phase=phase):
                for call in _gather_copies(src_refs, dst_refs, *sems)[phase]:
                    call()

        def step(j, carry, diagonal):
            rows = pl.ds(pl.multiple_of(j * t, t), t)
            s = [_dg(qb[h], k_ref[0, rows, qk(h)], 1, 1, None) * ATTN_SCALE for h in range(hp)]
            if diagonal:
                keep = _causal_mask(0, 0, t, t)
                s = [jnp.where(keep, x, -1e30) for x in s]
            m_new = [jnp.maximum(carry[h][0], jnp.max(s[h], axis=-1, keepdims=True)) for h in range(hp)]
            p = [jnp.exp(s[h] - m_new[h]) for h in range(hp)]
            alpha = [jnp.exp(carry[h][0] - m_new[h]) for h in range(hp)]
            l = [alpha[h] * carry[h][1] + jnp.sum(p[h], axis=-1, keepdims=True) for h in range(hp)]
            pv = [jnp.dot(p[h].astype(BF16), v_ref[0, rows, vd(h)], preferred_element_type=F32) for h in range(hp)]
            return tuple((m_new[h], l[h], alpha[h] * carry[h][2] + pv[h]) for h in range(hp))

        init = tuple((jnp.full((t, 1), -1e30, F32), jnp.zeros((t, 1), F32), jnp.zeros((t, HEAD_DIM), F32)) for _ in range(hp))
        below = lax.fori_loop(0, i, lambda j, carry: step(j, carry, False), init)
        for h, (m, l, acc) in enumerate(step(i, below, True)):
            o_ref[0, :, vd(h)] = acc / l
            lse_ref[0, h, 0] = (m + jnp.log(l)).T

        @pl.when((b == B - 1) & (g == HEADS // hp - 1) & (i == nq - 1))
        def _():
            for wait in _gather_copies(src_refs, dst_refs, *sems)[2]:
                wait()

    return pl.pallas_call(
        body, grid=(B, HEADS // hp, nq),
        in_specs=[pl.BlockSpec((1, t, hp * QK_PAD), lambda b, g, i: (b, i, g)),
                  pl.BlockSpec((1, S, hp * QK_PAD), lambda b, g, i: (b, 0, g)),
                  pl.BlockSpec((1, S, hp * HEAD_DIM), lambda b, g, i: (b, 0, g))] + [_ANY] * ns,
        out_specs=[pl.BlockSpec((1, t, hp * HEAD_DIM), lambda b, g, i: (b, i, g)),
                   pl.BlockSpec((1, hp, 1, 1, t), lambda b, g, i: (b, g, i, 0, 0))] + [_ANY] * ns,
        out_shape=[_sds((B, S, HEADS * HEAD_DIM), F32), _sds((B, HEADS, nq, 1, t), F32)] + [_sds((4,) + s.shape, s.dtype) for s in shards],
        scratch_shapes=_gather_scratch(ns),
        compiler_params=_params(("arbitrary", "arbitrary", "arbitrary")), name="attn_fwd",
    )(q, k, v, *shards)


def _attn_bwd(q, k, v, o, lse, do, partials, halves):
    B, S, _ = q.shape
    t = min(ATTN_TILE, S)
    nq = S // t
    ns, nh = len(partials), len(halves)

    hp = ATTN_HEADS_PER_STEP
    qk = lambda h: slice(h * QK_PAD, (h + 1) * QK_PAD)
    vd = lambda h: slice(h * HEAD_DIM, (h + 1) * HEAD_DIM)
    heads = range(hp)

    def body(*refs):
        q_ref, k_ref, v_ref, o_ref, lse_ref, do_ref = refs[:6]
        src_refs = refs[6:6 + ns + nh]
        dq_ref, dk_ref, dv_ref = refs[6 + ns + nh:9 + ns + nh]
        dst_refs = refs[9 + ns + nh:9 + 2 * (ns + nh)]
        dsum_ref, *sems = refs[9 + 2 * (ns + nh):]
        b, g, j = pl.program_id(0), pl.program_id(1), pl.program_id(2)

        def riders():
            scatter = _scatter_copies(src_refs[:ns], dst_refs[:ns], *sems[:3])
            swap = _swap_copies(src_refs[ns:], dst_refs[ns:], *sems[3:])
            return scatter[0] + swap[0], scatter[1] + swap[1]

        @pl.when((b == 0) & (g == 0) & (j == 0))
        def _():
            for start in riders()[0]:
                start()

        @pl.when(j == 0)
        def _():
            dq_ref[...] = jnp.zeros_like(dq_ref)
            for h in heads:
                for blk in range(nq):
                    rows = slice(blk * t, (blk + 1) * t)
                    dsum_ref[h, blk] = jnp.sum(do_ref[0, rows, vd(h)] * o_ref[0, rows, vd(h)], axis=-1, keepdims=True).T

        kb = [k_ref[0, :, qk(h)] for h in heads]
        vb = [v_ref[0, :, vd(h)] for h in heads]

        def step(i, carry, diagonal):
            rows = pl.ds(pl.multiple_of(i * t, t), t)
            qb = [q_ref[0, rows, qk(h)] for h in heads]
            dob = [do_ref[0, rows, vd(h)].astype(BF16) for h in heads]
            s = [_dg(kb[h], qb[h], 1, 1, None) * ATTN_SCALE for h in heads]
            p = [jnp.exp(s[h] - lse_ref[0, h, i]) for h in heads]
            if diagonal:
                key = lax.broadcasted_iota(jnp.int32, (t, t), 0)
                query = lax.broadcasted_iota(jnp.int32, (t, t), 1)
                p = [jnp.where(key <= query, x, 0.0) for x in p]
            dp = [_dg(vb[h], dob[h], 1, 1, None) for h in heads]
            dv = [carry[h][1] + jnp.dot(p[h].astype(BF16), dob[h], preferred_element_type=F32) for h in heads]
            ds = [(p[h] * (dp[h] - dsum_ref[h, i]) * ATTN_SCALE).astype(BF16) for h in heads]
            for h in heads:
                dq_ref[0, rows, qk(h)] += _dg(ds[h], kb[h], 0, 0, None)
            return tuple((carry[h][0] + jnp.dot(ds[h], qb[h], preferred_element_type=F32), dv[h]) for h in heads)

        zeros = tuple((jnp.zeros((t, QK_PAD), F32), jnp.zeros((t, HEAD_DIM), F32)) for _ in heads)
        on_diagonal = step(j, zeros, True)
        done = lax.fori_loop(j + 1, nq, lambda i, carry: step(i, carry, False), on_diagonal)
        for h, (dk, dv) in enumerate(done):
            dk_ref[0, :, qk(h)] = dk
            dv_ref[0, :, vd(h)] = dv

        @pl.when((b == B - 1) & (g == HEADS // hp - 1) & (j == nq - 1))
        def _():
            for wait in riders()[1]:
                wait()

    return pl.pallas_call(
        body, grid=(B, HEADS // hp, nq),
        in_specs=[pl.BlockSpec((1, S, hp * QK_PAD), lambda b, g, j: (b, 0, g)),
                  pl.BlockSpec((1, t, hp * QK_PAD), lambda b, g, j: (b, j, g)),
                  pl.BlockSpec((1, t, hp * HEAD_DIM), lambda b, g, j: (b, j, g)),
                  pl.BlockSpec((1, S, hp * HEAD_DIM), lambda b, g, j: (b, 0, g)),
                  pl.BlockSpec((1, hp, nq, 1, t), lambda b, g, j: (b, g, 0, 0, 0)),
                  pl.BlockSpec((1, S, hp * HEAD_DIM), lambda b, g, j: (b, 0, g))] + [_ANY] * (ns + nh),
        out_specs=[pl.BlockSpec((1, S, hp * QK_PAD), lambda b, g, j: (b, 0, g)),
                   pl.BlockSpec((1, t, hp * QK_PAD), lambda b, g, j: (b, j, g)),
                   pl.BlockSpec((1, t, hp * HEAD_DIM), lambda b, g, j: (b, j, g))] + [_ANY] * (ns + nh),
        out_shape=[_sds((B, S, HEADS * QK_PAD), F32), _sds((B, S, HEADS * QK_PAD), F32), _sds((B, S, HEADS * HEAD_DIM), F32)]
                  + [_scattered_shape(p) for p in partials] + [_swapped_shape(h) for h in halves],
        scratch_shapes=[pltpu.VMEM((hp, nq, 1, t), F32)] + _scatter_scratch(ns) + _swap_scratch(nh),
        compiler_params=_params(("arbitrary", "arbitrary", "arbitrary")), name="attn_bwd",
    )(q, k, v, o, lse, do, *partials, *halves)


def _gdn_pre_fn(xq, xk, xv, wq, wk, wv, keeps):
    def conv_silu(x, w):
        acc = x * w[3]
        for s in (1, 2, 3):
            acc = acc + _shift_rows(x, keeps[s - 1], s) * w[3 - s]
        return _silu(acc)

    def l2(x):
        return x * lax.rsqrt(jnp.sum(x * x, axis=-1, keepdims=True) + EPS)

    return l2(conv_silu(xq, wq)) * (HEAD_DIM ** -0.5), l2(conv_silu(xk, wk)), conv_silu(xv, wv)


def _gdn_pre_specs(S):
    x_specs = [pl.BlockSpec((1, S, HEAD_DIM), lambda h, b, g=g: (b, 0, g * HEADS + h)) for g in range(3)]
    w_specs = [pl.BlockSpec((CONV_TAPS, HEAD_DIM), lambda h, b, g=g: (0, g * HEADS + h)) for g in range(3)]
    out_spec = pl.BlockSpec((1, S, HEAD_DIM), lambda h, b: (b, 0, h))
    return x_specs, w_specs, out_spec


def _row_keeps(S):
    t = lax.broadcasted_iota(jnp.int32, (S, HEAD_DIM), 0)
    return [(t >= s).astype(F32) for s in (1, 2, 3)]


def _gdn_pre_fwd(gqkv, conv_w):
    B, S, _ = gqkv.shape
    x_specs, w_specs, out_spec = _gdn_pre_specs(S)

    def body(xq_ref, xk_ref, xv_ref, wq_ref, wk_ref, wv_ref, q_ref, k_ref, v_ref):
        taps = [[w[i:i + 1, :] for i in range(CONV_TAPS)] for w in (wq_ref, wk_ref, wv_ref)]
        q, k, v = _gdn_pre_fn(xq_ref[0], xk_ref[0], xv_ref[0], *taps, _row_keeps(S))
        q_ref[0], k_ref[0], v_ref[0] = q, k, v

    return pl.pallas_call(
        body, grid=(HEADS, B), in_specs=x_specs + w_specs, out_specs=[out_spec] * 3,
        out_shape=[_sds((B, S, HEADS * HEAD_DIM), F32)] * 3,
        compiler_params=_params(("parallel", "parallel")), name="gdn_pre_fwd",
    )(gqkv, gqkv, gqkv, conv_w, conv_w, conv_w)


def _gdn_pre_bwd(gqkv, conv_w, dq, dk, dv, halves):
    B, S, _ = gqkv.shape
    x_specs, w_specs, out_spec = _gdn_pre_specs(S)
    dw_spec = pl.BlockSpec((CONV_TAPS, HEAD_DIM), lambda h, b: (0, h))
    ns = len(halves)

    def body(*refs):
        xq_ref, xk_ref, xv_ref, wq_ref, wk_ref, wv_ref, dq_ref, dk_ref, dv_ref = refs[:9]
        src_refs = refs[9:9 + ns]
        dxq_ref, dxk_ref, dxv_ref, dwq_ref, dwk_ref, dwv_ref = refs[9 + ns:15 + ns]
        dst_refs = refs[15 + ns:15 + 2 * ns]
        sems = refs[15 + 2 * ns:]
        first = (pl.program_id(0) == 0) & (pl.program_id(1) == 0)
        last = (pl.program_id(0) == HEADS - 1) & (pl.program_id(1) == B - 1)

        @pl.when(first)
        def _():
            for start in _swap_copies(src_refs, dst_refs, *sems)[0]:
                start()

        @pl.when(pl.program_id(1) == 0)
        def _():
            for r in (dwq_ref, dwk_ref, dwv_ref):
                r[...] = jnp.zeros_like(r)

        taps = [[w[i:i + 1, :] for i in range(CONV_TAPS)] for w in (wq_ref, wk_ref, wv_ref)]
        keeps = _row_keeps(S)
        _, pull = jax.vjp(lambda *a: _gdn_pre_fn(*a, keeps), xq_ref[0], xk_ref[0], xv_ref[0], *taps)
        dxq, dxk, dxv, dwq, dwk, dwv = pull((dq_ref[0], dk_ref[0], dv_ref[0]))
        dxq_ref[0], dxk_ref[0], dxv_ref[0] = dxq.astype(BF16), dxk.astype(BF16), dxv.astype(BF16)
        for ref, dw in ((dwq_ref, dwq), (dwk_ref, dwk), (dwv_ref, dwv)):
            for i in range(CONV_TAPS):
                ref[i:i + 1, :] += dw[i]

        @pl.when(last)
        def _():
            for wait in _swap_copies(src_refs, dst_refs, *sems)[1]:
                wait()

    hw = HEADS * HEAD_DIM
    return pl.pallas_call(
        body, grid=(HEADS, B), in_specs=x_specs + w_specs + [out_spec] * 3 + [_ANY] * ns,
        out_specs=[out_spec] * 3 + [dw_spec] * 3 + [_ANY] * ns,
        out_shape=[_sds((B, S, hw), BF16)] * 3 + [_sds((CONV_TAPS, hw), F32)] * 3 + [_swapped_shape(h) for h in halves],
        scratch_shapes=_swap_scratch(ns),
        compiler_params=_params(("arbitrary", "arbitrary")), name="gdn_pre_bwd",
    )(gqkv, gqkv, gqkv, conv_w, conv_w, conv_w, dq, dk, dv, *halves)


def _chunk_masks():
    i = lax.broadcasted_iota(jnp.int32, (CHUNK, CHUNK), 0)
    j = lax.broadcasted_iota(jnp.int32, (CHUNK, CHUNK), 1)
    lower, after = (j <= i).astype(F32), (j > i).astype(F32)
    return {"le": lower, "le_gt": jnp.concatenate([lower, after], axis=0), "strict": (j < i).astype(F32)}


def _gdn_chunk_fn(groups, masks, solve=_unit_lower_solve):
    lane = lax.broadcasted_iota(jnp.int32, (groups, 1, 128), 2)
    head = lax.broadcasted_iota(jnp.int32, (groups, 1, 128), 0) % HEADS
    pick_a, pick_b = (lane == head).astype(F32), (lane == head + HEADS).astype(F32)
    lower, lower_after, strict = (jnp.broadcast_to(masks[n], (groups,) + masks[n].shape) for n in ("le", "le_gt", "strict"))
    ones_row = jnp.ones((1, 1, HEAD_DIM), F32)

    def f(q, k, v, gab, a_row, dt_row, state):
        ga = jnp.sum(gab * pick_a, axis=2, keepdims=True)
        gb = jnp.sum(gab * pick_b, axis=2, keepdims=True)
        a_log = jnp.sum(a_row * pick_a, axis=2, keepdims=True)
        dt_bias = jnp.sum(dt_row * pick_a, axis=2, keepdims=True)
        beta = _sigmoid(gb)
        g = -jnp.exp(a_log) * _softplus(ga + dt_bias)
        g_wide = g * ones_row
        cum, rest = _row_halves(_hi_nn(lower_after, g_wide))
        total = jnp.sum(g_wide, axis=1, keepdims=True)
        diff = _hi_nn(lower, g * strict)
        decay = lower * jnp.exp(diff)
        e_cum = jnp.exp(cum)
        kk, qk = _row_halves(_bf_nt(jnp.concatenate([k, q], axis=1), k))
        lmat = strict * (beta * kk * decay)
        u, w = _lane_halves(solve(lmat, jnp.concatenate([v * beta, k * (beta * e_cum)], axis=2)))
        w_state, q_state = _row_halves(_bf_nn(jnp.concatenate([w, q * e_cum], axis=1), state))
        v_new = u - w_state
        o = q_state + _bf_nn(qk * decay, v_new)
        new_state = state * jnp.exp(total) + _bf_tn(k * jnp.exp(rest), v_new)
        return o, new_state

    return f


def _gdn_chunk_fwd(q, k, v, gab, scal, shards):
    B, S, W = q.shape
    N = S // CHUNK
    ns = len(shards)

    def body(*refs):
        q_ref, k_ref, v_ref, gab_ref, sc_ref = refs[:5]
        src_refs = refs[5:5 + ns]
        o_ref, st_ref, pw_ref, sol_ref = refs[5 + ns:9 + ns]
        dst_refs = refs[9 + ns:9 + 2 * ns]
        state_ref, send_sems, recv_sems, local_sems = refs[9 + 2 * ns:]
        n = pl.program_id(0)
        kept = {}

        @pl.when(n == 0)
        def _():
            for start in _gather_copies(src_refs, dst_refs, send_sems, recv_sems, local_sems)[0]:
                start()
            state_ref[...] = jnp.zeros_like(state_ref)

        @pl.when(n == (2 * N) // 3)
        def _():
            for pass_on in _gather_copies(src_refs, dst_refs, send_sems, recv_sems, local_sems)[1]:
                pass_on()

        groups = [(b, h) for b in range(B) for h in range(HEADS)]
        gather = lambda ref: jnp.stack([ref[b, :, h * HEAD_DIM:(h + 1) * HEAD_DIM] for b, h in groups])
        state = state_ref[...]
        for i, (b, h) in enumerate(groups):
            st_ref[b, 0, h] = state[i]
        def solve_and_keep(lmat, rhs):
            kept["x"], (_, kept["powers"], _) = _unit_lower_solve_fwd(lmat, rhs)
            return kept["x"]

        o, new_state = _gdn_chunk_fn(len(groups), _chunk_masks(), solve_and_keep)(
            gather(q_ref), gather(k_ref), gather(v_ref), jnp.stack([gab_ref[b] for b, _ in groups]), sc_ref[0:1, :], sc_ref[1:2, :], state)
        for i, (b, h) in enumerate(groups):
            o_ref[b, :, h * HEAD_DIM:(h + 1) * HEAD_DIM] = o[i]
            sol_ref[b, 0, h] = kept["x"][i]
            for p, power in enumerate(kept["powers"]):
                pw_ref[b, 0, h, p] = power[i]
        state_ref[...] = new_state

        @pl.when(n == N - 1)
        def _():
            for wait in _gather_copies(src_refs, dst_refs, send_sems, recv_sems, local_sems)[2]:
                wait()

    seq = pl.BlockSpec((B, CHUNK, W), lambda n: (0, n, 0))
    return pl.pallas_call(
        body, grid=(N,),
        in_specs=[seq, seq, seq, pl.BlockSpec((B, CHUNK, GAB_W), lambda n: (0, n, 0)), _const_spec((8, 128))] + [_ANY] * ns,
        out_specs=[seq, pl.BlockSpec((B, 1, HEADS, HEAD_DIM, HEAD_DIM), lambda n: (0, n, 0, 0, 0)),
                   pl.BlockSpec((B, 1, HEADS, SOLVE_POWERS, CHUNK, CHUNK), lambda n: (0, n, 0, 0, 0, 0)),
                   pl.BlockSpec((B, 1, HEADS, CHUNK, 2 * HEAD_DIM), lambda n: (0, n, 0, 0, 0))] + [_ANY] * ns,
        out_shape=[_sds((B, S, W), F32), _sds((B, N, HEADS, HEAD_DIM, HEAD_DIM), F32),
                   _sds((B, N, HEADS, SOLVE_POWERS, CHUNK, CHUNK), F32), _sds((B, N, HEADS, CHUNK, 2 * HEAD_DIM), F32)]
                  + [_sds((4,) + s.shape, s.dtype) for s in shards],
        scratch_shapes=[pltpu.VMEM((B * HEADS, HEAD_DIM, HEAD_DIM), F32)] + _gather_scratch(ns),
        compiler_params=_params(("arbitrary",)), name="gdn_chunk_fwd",
    )(q, k, v, gab, scal, *shards)


def _gdn_chunk_bwd(q, k, v, gab, scal, states, powers, solutions, do, partials):
    B, S, W = q.shape
    N = S // CHUNK
    ns = len(partials)

    def body(*refs):
        q_ref, k_ref, v_ref, gab_ref, sc_ref, st_ref, pw_ref, sol_ref, do_ref = refs[:9]
        src_refs = refs[9:9 + ns]
        dq_ref, dk_ref, dv_ref, dgab_ref, dsc_ref = refs[9 + ns:14 + ns]
        dst_refs = refs[14 + ns:14 + 2 * ns]
        dstate_ref, send_sems, recv_sems, local_sems = refs[14 + 2 * ns:]
        n = pl.program_id(0)

        @pl.when(n == 0)
        def _():
            for start in _scatter_copies(src_refs, dst_refs, send_sems, recv_sems, local_sems)[0]:
                start()
            dstate_ref[...] = jnp.zeros_like(dstate_ref)
            dsc_ref[...] = jnp.zeros_like(dsc_ref)

        groups = [(b, h) for b in range(B) for h in range(HEADS)]
        gather = lambda ref: jnp.stack([ref[b, :, h * HEAD_DIM:(h + 1) * HEAD_DIM] for b, h in groups])
        kept_powers = [jnp.stack([pw_ref[b, 0, h, p] for b, h in groups]) for p in range(SOLVE_POWERS)]
        kept_x = jnp.stack([sol_ref[b, 0, h] for b, h in groups])
        solve = lambda lmat, rhs: _unit_lower_solve_kept(lmat, rhs, kept_powers, kept_x)
        _, pull = jax.vjp(_gdn_chunk_fn(len(groups), _chunk_masks(), solve), gather(q_ref), gather(k_ref), gather(v_ref),
                          jnp.stack([gab_ref[b] for b, _ in groups]), sc_ref[0:1, :], sc_ref[1:2, :],
                          jnp.stack([st_ref[b, 0, h] for b, h in groups]))
        dq, dk, dv, dg, d_a, d_dt, dstate = pull((gather(do_ref), dstate_ref[...]))
        for i, (b, h) in enumerate(groups):
            lanes = slice(h * HEAD_DIM, (h + 1) * HEAD_DIM)
            dq_ref[b, :, lanes] = dq[i]
            dk_ref[b, :, lanes] = dk[i]
            dv_ref[b, :, lanes] = dv[i]
        for b in range(B):
            dgab_ref[b] = sum(dg[b * HEADS + h] for h in range(HEADS)).astype(BF16)
        dstate_ref[...] = dstate
        dsc_ref[0:1, :] += d_a
        dsc_ref[1:2, :] += d_dt

        @pl.when(n == N - 1)
        def _():
            for wait in _scatter_copies(src_refs, dst_refs, send_sems, recv_sems, local_sems)[1]:
                wait()

    seq = pl.BlockSpec((B, CHUNK, W), lambda n: (0, N - 1 - n, 0))
    gab_spec = pl.BlockSpec((B, CHUNK, GAB_W), lambda n: (0, N - 1 - n, 0))
    return pl.pallas_call(
        body, grid=(N,),
        in_specs=[seq, seq, seq, gab_spec, _const_spec((8, 128)),
                  pl.BlockSpec((B, 1, HEADS, HEAD_DIM, HEAD_DIM), lambda n: (0, N - 1 - n, 0, 0, 0)),
                  pl.BlockSpec((B, 1, HEADS, SOLVE_POWERS, CHUNK, CHUNK), lambda n: (0, N - 1 - n, 0, 0, 0, 0)),
                  pl.BlockSpec((B, 1, HEADS, CHUNK, 2 * HEAD_DIM), lambda n: (0, N - 1 - n, 0, 0, 0)), seq] + [_ANY] * ns,
        out_specs=[seq, seq, seq, gab_spec, _const_spec((8, 128))] + [_ANY] * ns,
        out_shape=[_sds((B, S, W), F32)] * 3 + [_sds((B, S, GAB_W), BF16), _sds((8, 128), F32)] + [_scattered_shape(p) for p in partials],
        scratch_shapes=[pltpu.VMEM((B * HEADS, HEAD_DIM, HEAD_DIM), F32)] + _scatter_scratch(ns),
        compiler_params=_params(("arbitrary",)), name="gdn_chunk_bwd",
    )(q, k, v, gab, scal, states, powers, solutions, do, *partials)


def _mix_fn(ao, go, gz, w_mla, w_gdn):
    return tuple(_rms(ao[h], w_mla[h]) for h in range(HEADS)) + tuple(_rms(go[h], w_gdn) * _silu(gz[h]) for h in range(HEADS))


def _mix_operands(ao_ref, go_ref, gz_ref, nw_ref):
    blocks = lambda ref: [ref[:, h * HEAD_DIM:(h + 1) * HEAD_DIM] for h in range(HEADS)]
    return blocks(ao_ref), blocks(go_ref), blocks(gz_ref), [nw_ref[h:h + 1, :] for h in range(HEADS)], nw_ref[HEADS:HEADS + 1, :]


def _mix_fwd(ao, go, gz, nw, w_out, x2):
    T, D = x2.shape
    tm = min(TOKEN_TILE, T)
    MW = 2 * HEADS * HEAD_DIM

    def body(ao_ref, go_ref, gz_ref, nw_ref, w_ref, x_ref, mix_ref, h_ref):
        outs = _mix_fn(*_mix_operands(ao_ref, go_ref, gz_ref, nw_ref))
        for i, piece in enumerate(outs):
            mix_ref[:, i * HEAD_DIM:(i + 1) * HEAD_DIM] = piece.astype(BF16)
        h_ref[...] = x_ref[...] + jnp.dot(mix_ref[...], w_ref[...], preferred_element_type=F32)

    half = HEADS * HEAD_DIM
    return pl.pallas_call(
        body, grid=(T // tm,),
        in_specs=[_row_spec(tm, half), _row_spec(tm, half), _row_spec(tm, half), _const_spec((8, 128)), _const_spec((MW, D)),
                  _row_spec(tm, D)],
        out_specs=[_row_spec(tm, MW), _row_spec(tm, D)],
        out_shape=[_sds((T, MW), BF16), _sds((T, D), F32)],
        compiler_params=_params(("parallel",)), name="mix_fwd",
    )(ao, go, gz, nw, w_out, x2)


def _mix_bwd(ao, go, gz, nw, w_out, dh):
    T, D = dh.shape
    tm = min(TOKEN_TILE, T)
    MW = 2 * HEADS * HEAD_DIM
    half = HEADS * HEAD_DIM

    def body(ao_ref, go_ref, gz_ref, nw_ref, w_ref, dh_ref, dao_ref, dgo_ref, dgz_ref, dnw_ref):
        @pl.when(pl.program_id(0) == 0)
        def _():
            dnw_ref[...] = jnp.zeros_like(dnw_ref)

        d_mix = _dg(dh_ref[...].astype(BF16), w_ref[...], 1, 1, None)
        cts = tuple(d_mix[:, i * HEAD_DIM:(i + 1) * HEAD_DIM] for i in range(2 * HEADS))
        _, pull = jax.vjp(_mix_fn, *_mix_operands(ao_ref, go_ref, gz_ref, nw_ref))
        d_ao, d_go, d_gz, d_wm, d_wg = pull(cts)
        for h in range(HEADS):
            lanes = slice(h * HEAD_DIM, (h + 1) * HEAD_DIM)
            dao_ref[:, lanes] = d_ao[h]
            dgo_ref[:, lanes] = d_go[h]
            dgz_ref[:, lanes] = d_gz[h].astype(BF16)
            dnw_ref[h:h + 1, :] += d_wm[h]
        dnw_ref[HEADS:HEADS + 1, :] += d_wg

    return pl.pallas_call(
        body, grid=(T // tm,),
        in_specs=[_row_spec(tm, half), _row_spec(tm, half), _row_spec(tm, half), _const_spec((8, 128)), _const_spec((MW, D)),
                  _row_spec(tm, D)],
        out_specs=[_row_spec(tm, half)] * 3 + [_const_spec((8, 128))],
        out_shape=[_sds((T, half), F32)] * 2 + [_sds((T, half), BF16), _sds((8, 128), F32)],
        compiler_params=_params(("arbitrary",)), name="mix_bwd",
    )(ao, go, gz, nw, w_out, dh)


def _up_spec(w_up, tf):
    per_shard = w_up.shape[2] // tf
    return pl.BlockSpec((None, w_up.shape[1], tf), lambda i, j: (j // per_shard, 0, j % per_shard))


def _mlp_fwd(h2, w_mn, w_up, w_down, target):
    T, D = h2.shape
    FF = w_down.shape[0]
    tm, tf = min(MLP_TOKEN_TILE, T), min(FF_TILE, w_up.shape[2])
    nf = FF // tf

    def body(h_ref, wn_ref, wu_ref, wd_ref, t_ref, hn_ref, act_ref, dy_ref, sq_ref, acc_ref):
        j = pl.program_id(1)

        @pl.when(j == 0)
        def _():
            hn_ref[...] = _rms(h_ref[...], wn_ref[...]).astype(BF16)
            acc_ref[...] = jnp.zeros_like(acc_ref)

        up = jnp.dot(hn_ref[...], wu_ref[...], preferred_element_type=F32)
        act = jnp.square(jnp.maximum(up, 0.0)).astype(BF16)
        act_ref[...] = act
        acc_ref[...] += jnp.dot(act, wd_ref[...], preferred_element_type=F32)

        @pl.when(j == nf - 1)
        def _():
            err = h_ref[...] + acc_ref[...] - t_ref[...]
            dy_ref[...] = err * (1.0 / D)
            sq_ref[...] = jnp.zeros_like(sq_ref) + jnp.sum(err * err)

    tok = lambda w: pl.BlockSpec((tm, w), lambda i, j: (i, 0))
    return pl.pallas_call(
        body, grid=(T // tm, nf),
        in_specs=[tok(D), _const_spec((1, D)), _up_spec(w_up, tf), pl.BlockSpec((tf, D), lambda i, j: (j, 0)), tok(D)],
        out_specs=[tok(D), pl.BlockSpec((tm, tf), lambda i, j: (i, j)), tok(D), pl.BlockSpec((1, 8, 128), lambda i, j: (i, 0, 0))],
        out_shape=[_sds((T, D), BF16), _sds((T, FF), BF16), _sds((T, D), F32), _sds((T // tm, 8, 128), F32)],
        scratch_shapes=[pltpu.VMEM((tm, D), F32)],
        compiler_params=_params(("parallel", "arbitrary")), name="mlp_fwd",
    )(h2, w_mn, w_up, w_down, target)


def _mlp_bwd(h2, w_mn, act, w_up, w_down, dy):
    T, D = h2.shape
    FF = w_down.shape[0]
    tm, tf = min(MLP_TOKEN_TILE, T), min(FF_TILE, w_up.shape[2])
    nf = FF // tf

    def body(h_ref, wn_ref, act_ref, wu_ref, wd_ref, dy_ref, dh_ref, dup_ref, dwn_ref, acc_ref, dyb_ref):
        i, j = pl.program_id(0), pl.program_id(1)

        @pl.when((i == 0) & (j == 0))
        def _():
            dwn_ref[...] = jnp.zeros_like(dwn_ref)

        @pl.when(j == 0)
        def _():
            acc_ref[...] = jnp.zeros_like(acc_ref)
            dyb_ref[...] = dy_ref[...].astype(BF16)

        r = jnp.sqrt(act_ref[...].astype(F32))
        d_act = _dg(dyb_ref[...], wd_ref[...], 1, 1, None)
        d_up = (d_act * (2.0 * r)).astype(BF16)
        dup_ref[...] = d_up
        acc_ref[...] += _dg(d_up, wu_ref[...], 1, 1, None)

        @pl.when(j == nf - 1)
        def _():
            _, pull = jax.vjp(_rms, h_ref[...], wn_ref[...])
            dh, dwn = pull(acc_ref[...])
            dh_ref[...] = dh + dy_ref[...]
            dwn_ref[...] += dwn

    tok = lambda w: pl.BlockSpec((tm, w), lambda i, j: (i, 0))
    ff = pl.BlockSpec((tm, tf), lambda i, j: (i, j))
    return pl.pallas_call(
        body, grid=(T // tm, nf),
        in_specs=[tok(D), _const_spec((1, D)), ff, _up_spec(w_up, tf), pl.BlockSpec((tf, D), lambda i, j: (j, 0)), tok(D)],
        out_specs=[tok(D), ff, _const_spec((1, D))],
        out_shape=[_sds((T, D), F32), _sds((T, FF), BF16), _sds((1, D), F32)],
        scratch_shapes=[pltpu.VMEM((tm, D), F32), pltpu.VMEM((tm, D), BF16)],
        compiler_params=_params(("arbitrary", "arbitrary")), name="mlp_bwd",
    )(h2, w_mn, act, w_up, w_down, dy)


def _rope_pad(a):
    z = jnp.zeros(a.shape[:-1] + (ROPE_HALF,), a.dtype)
    return jnp.concatenate([a[..., :ROPE_HALF], z, a[..., ROPE_HALF:], z], axis=-1)


def _rope_unpad(a):
    return jnp.concatenate([a[..., :ROPE_HALF], a[..., 2 * ROPE_HALF:3 * ROPE_HALF]], axis=-1)


_G0 = 2 * LORA + ROPE_DIM
W_IN_COLS = _G0 + GQKV_W + GZ_W + 2 * HEADS


def _widen_w_in_t(w_t):
    z = jnp.zeros((ROPE_HALF, w_t.shape[1]), w_t.dtype)
    pad = jnp.zeros((GAB_W - 2 * HEADS, w_t.shape[1]), w_t.dtype)
    return jnp.concatenate([w_t[:2 * LORA + ROPE_HALF], z, w_t[2 * LORA + ROPE_HALF:_G0], z, w_t[_G0:], pad], axis=0)


def _narrow_w_in_t(w_t):
    return jnp.concatenate([w_t[:2 * LORA + ROPE_HALF], w_t[2 * LORA + 2 * ROPE_HALF:2 * LORA + 3 * ROPE_HALF],
                            w_t[LAT_W:LAT_W + W_IN_COLS - _G0]], axis=0)


def _stack_mla(w_uq, w_ukv):
    uq = w_uq.reshape(LORA, HEADS, QK_DIM)
    ukv = w_ukv.reshape(LORA, HEADS, 2 * HEAD_DIM)
    parts = [uq[:, :, :HEAD_DIM], _rope_pad(uq[:, :, HEAD_DIM:]), ukv[:, :, :HEAD_DIM], ukv[:, :, HEAD_DIM:]]
    return jnp.concatenate([p.transpose(1, 0, 2) for p in parts], axis=0)


def _unstack_mla(w):
    p = [w[i * HEADS:(i + 1) * HEADS].transpose(1, 0, 2) for i in range(4)]
    uq = jnp.concatenate([p[0], _rope_unpad(p[1])], axis=-1).reshape(LORA, HEADS * QK_DIM)
    ukv = jnp.concatenate([p[2], p[3]], axis=-1).reshape(LORA, HEADS * 2 * HEAD_DIM)
    return uq, ukv


def _rows8(rows):
    a = jnp.concatenate(rows, axis=0)
    return jnp.pad(a, ((0, 8 - a.shape[0]), (0, 0)))


def _qk_norm_rows(q_norm_w, k_norm_w):
    return _rows8([q_norm_w[:, :HEAD_DIM], _rope_pad(q_norm_w[:, HEAD_DIM:]), k_norm_w[:, :HEAD_DIM], _rope_pad(k_norm_w[:, HEAD_DIM:])])


def _rope_rows():
    inv_freq = ROPE_THETA ** (-jnp.arange(ROPE_HALF, dtype=F32) / ROPE_HALF)
    z = jnp.zeros((ROPE_HALF,), F32)
    freq = jnp.concatenate([inv_freq, z, inv_freq, z])
    sign = jnp.concatenate([-jnp.ones((ROPE_HALF,), F32), z, jnp.ones((ROPE_HALF,), F32), z])
    return _rows8([freq[None], sign[None]])


def _column_shards(a):
    return a.reshape(a.shape[0], 4, a.shape[1] // 4).transpose(1, 0, 2)


def _from_column_shards(a):
    return a.transpose(1, 0, 2).reshape(a.shape[1], 4 * a.shape[2])


_ANY = pl.BlockSpec(memory_space=pl.ANY)
_OTHER_CHIPS = ((1, 0), (0, 1), (1, 1))


def _here():
    return lax.axis_index("x"), lax.axis_index("y"), lax.axis_index("c")


def _flip(v, bit):
    return 1 - v if bit else v


def _remote(src, dst, send_sems, recv_sems, k, to):
    return pltpu.make_async_remote_copy(src_ref=src, dst_ref=dst, send_sem=send_sems.at[k], recv_sem=recv_sems.at[k],
                                        device_id=to, device_id_type=MESH)


def _half_of(ref, k, shape):
    r, c = shape
    if (r // 2) % 16 == 0:
        return ref.at[pl.ds(pl.multiple_of(k * (r // 2), 16), r // 2)]
    if (c // 2) % 128 == 0:
        return ref.at[:, pl.ds(pl.multiple_of(k * (c // 2), 128), c // 2)]
    return None


def _gather_copies(srcs, dsts, send_sems, recv_sems, local_sems):
    x, y, c = _here()
    slot, sibling, n = 2 * x + y, (x, y, 1 - c), len(srcs)
    starts, passes, waits = [], [], []
    for i, (src, dst) in enumerate(zip(srcs, dsts)):
        own = pltpu.make_async_copy(src, dst.at[slot], local_sems.at[i])
        starts.append(own.start)
        waits.append(own.wait)
        halves = _half_of(src, c, src.shape) is not None
        for j, (fx, fy) in enumerate(_OTHER_CHIPS):
            cx, cy = _flip(x, fx), _flip(y, fy)
            there = dst.at[2 * cx + cy]
            if halves:
                push = _remote(_half_of(src, c, src.shape), _half_of(dst.at[slot], c, src.shape), send_sems, recv_sems, 3 * i + j, (cx, cy, c))
                landed, other = _half_of(there, c, src.shape), _half_of(there, 1 - c, src.shape)
                onward = _remote(landed, landed, send_sems, recv_sems, 3 * n + 3 * i + j, sibling)
                passes += [_remote(landed, landed, send_sems, recv_sems, 3 * i + j, (cx, cy, c)).wait_recv, onward.start]
                waits += [_remote(other, other, send_sems, recv_sems, 3 * n + 3 * i + j, sibling).wait_recv, onward.wait_send]
            else:
                push = _remote(src, dst.at[slot], send_sems, recv_sems, 3 * i + j, (cx, cy, c))
                waits.append(_remote(there, there, send_sems, recv_sems, 3 * i + j, (cx, cy, c)).wait_recv)
            starts.append(push.start)
            waits.append(push.wait_send)
    return starts, passes, waits


def _gather_scratch(n):
    return [pltpu.SemaphoreType.DMA((6 * n,)), pltpu.SemaphoreType.DMA((6 * n,)), pltpu.SemaphoreType.DMA((n,))]


def _all_gather(shards, name):
    ns = len(shards)

    def body(*refs):
        starts, passes, waits = _gather_copies(refs[:ns], refs[ns:2 * ns], *refs[2 * ns:])
        for call in starts + passes + waits:
            call()

    return pl.pallas_call(
        body, in_specs=[_ANY] * ns, out_specs=[_ANY] * ns, out_shape=[_sds((4,) + s.shape, s.dtype) for s in shards],
        scratch_shapes=_gather_scratch(ns), name=name,
    )(*shards)


def _by_lanes(shape):
    return (shape[-2] // 2) % 16 != 0


def _scattered_shape(p):
    r, c = p.shape[1:]
    return _sds((8, r, c // 2) if _by_lanes(p.shape) else (8, r // 2, c), p.dtype)


def _scatter_copies(srcs, dsts, send_sems, recv_sems, local_sems, whole=0):
    x, y, c = _here()
    me = 4 * x + 2 * y + c
    starts, waits = [], []
    for i, (src, dst) in enumerate(zip(srcs, dsts)):
        def piece(px, py, pc, src=src, entire=i >= len(srcs) - whole):
            if entire:
                return src
            if _by_lanes(src.shape):
                half = src.shape[2] // 2
                return src.at[2 * px + py, :, pl.ds(pl.multiple_of(pc * half, 128), half)]
            half = src.shape[1] // 2
            return src.at[2 * px + py, pl.ds(pl.multiple_of(pc * half, 16), half)]

        own = pltpu.make_async_copy(piece(x, y, c), dst.at[me], local_sems.at[i])
        starts.append(own.start)
        waits.append(own.wait)
        for k in range(1, 8):
            px, py, pc = _flip(x, k & 4), _flip(y, k & 2), _flip(c, k & 1)
            push = _remote(piece(px, py, pc), dst.at[me], send_sems, recv_sems, 7 * i + k - 1, (px, py, pc))
            landed = dst.at[4 * px + 2 * py + pc]
            starts.append(push.start)
            waits += [_remote(landed, landed, send_sems, recv_sems, 7 * i + k - 1, (px, py, pc)).wait_recv, push.wait_send]
    return starts, waits


def _scatter_scratch(n):
    return [pltpu.SemaphoreType.DMA((7 * n,)), pltpu.SemaphoreType.DMA((7 * n,)), pltpu.SemaphoreType.DMA((n,))]


def _swapped_shape(half):
    r, c = half.shape
    return _sds((r, 2 * c) if _by_lanes((r, 2 * c)) else (2, r, c), half.dtype)


def _swap_copies(srcs, dsts, send_sems=None, recv_sems=None, local_sems=None):
    if not srcs:
        return [], []
    x, y, c = _here()
    sibling = (x, y, 1 - c)
    starts, waits = [], []
    for i, (src, dst) in enumerate(zip(srcs, dsts)):
        if len(dst.shape) == 2:
            lanes = src.shape[1]
            mine, other = (dst.at[:, pl.ds(pl.multiple_of(k * lanes, 128), lanes)] for k in (c, 1 - c))
        else:
            mine, other = dst.at[c], dst.at[1 - c]
        own = pltpu.make_async_copy(src, mine, local_sems.at[i])
        push = _remote(src, mine, send_sems, recv_sems, i, sibling)
        starts += [own.start, push.start]
        waits += [_remote(other, other, send_sems, recv_sems, i, sibling).wait_recv, push.wait_send, own.wait]
    return starts, waits


def _swap_scratch(n):
    return [pltpu.SemaphoreType.DMA((n,)), pltpu.SemaphoreType.DMA((n,)), pltpu.SemaphoreType.DMA((n,))] if n else []


def _exchange_halves(halves, wholes):
    ns, nw = len(halves), len(wholes)

    def body(*refs):
        srcs, dsts = refs[:ns + nw], refs[ns + nw:2 * (ns + nw)]
        sems = refs[2 * (ns + nw):]
        starts, waits = _swap_copies(srcs[:ns], dsts[:ns], *sems[:3])
        more = _scatter_copies(srcs[ns:], dsts[ns:], *sems[3:], whole=nw)
        for call in starts + more[0] + waits + more[1]:
            call()

    return pl.pallas_call(
        body, in_specs=[_ANY] * (ns + nw), out_specs=[_ANY] * (ns + nw),
        out_shape=[_swapped_shape(h) for h in halves] + [_sds((8,) + a.shape, a.dtype) for a in wholes],
        scratch_shapes=_swap_scratch(ns) + _scatter_scratch(nw), name="exchange_halves",
    )(*halves, *wholes)


def _row_tile(rows, row_bytes, budget):
    tr = rows
    while tr * row_bytes > budget and tr % 16 == 0:
        tr //= 2
    return tr


def _sum_slots(parts, name):
    _, rows, cols = parts.shape
    tr = _row_tile(rows, 8 * cols * 4, 2 * 1024 * 1024)

    def body(p_ref, o_ref):
        acc = p_ref[0].astype(F32)
        for d in range(1, 8):
            acc = acc + p_ref[d].astype(F32)
        o_ref[...] = acc

    return pl.pallas_call(
        body, grid=(rows // tr,), in_specs=[pl.BlockSpec((8, tr, cols), lambda i: (0, i, 0))],
        out_specs=pl.BlockSpec((tr, cols), lambda i: (i, 0)), out_shape=_sds((rows, cols), F32),
        compiler_params=_params(("parallel",)), name=name,
    )(parts)


def _adam_update(w, g, m, v):
    m = ADAM_B1 * m + (1.0 - ADAM_B1) * g
    v = ADAM_B2 * v + (1.0 - ADAM_B2) * jnp.square(g)
    m_hat = m / (1.0 - ADAM_B1 ** ADAM_STEP)
    v_hat = v / (1.0 - ADAM_B2 ** ADAM_STEP)
    return -ADAM_LR * (m_hat / (jnp.sqrt(v_hat) + ADAM_EPS) + ADAM_WD * w), m, v


SMALL_ROWS = {"attn_norm_w": 0, "mlp_norm_w": 1, "q_lat_norm_w": 2, "kv_lat_norm_w": 3, "q_norm_w": 4, "k_norm_w": 5,
              "mla_out_norm_w": 6, "gdn_norm_w": 10, "a_log": 11, "dt_bias": 12}
LOSS_ROW = 13
SMALL_SHAPE = (16, 1024)


def _pack_small_partials(d_attn_nw, d_mlp_nw, d_ln, d_qk_nw, d_mix_nw, d_scal, conv_parts, sq):
    D = d_attn_nw.shape[1]

    def body(an_ref, mn_ref, ln_ref, qk_ref, mix_ref, sc_ref, cq_ref, ck_ref, cv_ref, sq_ref, a_ref, c_ref):
        a_ref[...] = jnp.zeros_like(a_ref)
        a_ref[0:1, :D] = an_ref[...]
        a_ref[1:2, :D] = mn_ref[...]
        a_ref[2:4, :LORA] = ln_ref[...]
        for row, base in ((4, 0), (5, 2)):
            rope = qk_ref[base + 1:base + 2, :]
            a_ref[row:row + 1, :QK_DIM] = jnp.concatenate(
                [qk_ref[base:base + 1, :], rope[:, :ROPE_HALF], rope[:, 2 * ROPE_HALF:3 * ROPE_HALF]], axis=1)
        a_ref[6:6 + HEADS, :HEAD_DIM] = mix_ref[0:HEADS, :]
        a_ref[10:11, :HEAD_DIM] = mix_ref[HEADS:HEADS + 1, :]
        a_ref[11:13, :128] = sc_ref[0:2, :]
        a_ref[LOSS_ROW:LOSS_ROW + 1, :128] = jnp.zeros((1, 128), F32) + jnp.sum(sq_ref[:, 0:1, 0:1]) * (0.5 / D)
        c_ref[...] = jnp.concatenate([cq_ref[...], ck_ref[...], cv_ref[...]], axis=1)

    return pl.pallas_call(
        body, out_shape=[_sds(SMALL_SHAPE, F32), _sds((CONV_TAPS, GQKV_W), F32)], name="pack_small_partials",
    )(d_attn_nw, d_mlp_nw, d_ln, d_qk_nw, d_mix_nw, d_scal, *conv_parts, sq)


def _adamw_small(parts, conv_parts, w, m, v):
    names = tuple(SMALL_ROWS) + ("conv_w",)
    cols = w["conv_w"].shape[2]

    def body(*refs):
        p_ref, c_ref = refs[:2]
        n = len(names)
        w_refs, m_refs, v_refs = (dict(zip(names, refs[2 + k * n:2 + (k + 1) * n])) for k in range(3))
        loss_ref = refs[2 + 3 * n]
        out = [dict(zip(names, refs[3 + (3 + k) * n:3 + (4 + k) * n])) for k in range(4)]
        acc_ref, cacc_ref = refs[3 + 7 * n:]
        acc, cacc = p_ref[0], c_ref[0]
        for d in range(1, 8):
            acc, cacc = acc + p_ref[d], cacc + c_ref[d]
        acc_ref[...] = acc
        cacc_ref[...] = cacc
        loss_ref[...] = acc_ref[LOSS_ROW:LOSS_ROW + 1, 0:1]
        chip = 2 * lax.axis_index("x") + lax.axis_index("y")
        for name in names:
            shape = w_refs[name].shape
            if name == "conv_w":
                g = sum(jnp.where(chip == s, cacc_ref[:, s * cols:(s + 1) * cols], 0.0) for s in range(4))[None]
            else:
                row = SMALL_ROWS[name]
                g = acc_ref[row:row + math.prod(shape[:-1]), 0:shape[-1]].reshape(shape)
            delta, new_m, new_v = _adam_update(w_refs[name][...], g, m_refs[name][...], v_refs[name][...])
            for ref, val in zip((o[name] for o in out), (g, delta, new_m, new_v)):
                ref[...] = val

    ins = [x[n] for x in (w, m, v) for n in names]
    shapes = [_sds(w[n].shape, F32) for n in names]
    outs = pl.pallas_call(
        body, out_shape=[_sds((1, 1), F32)] + shapes * 4,
        scratch_shapes=[pltpu.VMEM(parts.shape[1:], F32), pltpu.VMEM(conv_parts.shape[1:], F32)], name="adamw_small",
    )(parts, conv_parts, *ins)
    n = len(names)
    return (outs[0],) + tuple(dict(zip(names, outs[1 + k * n:1 + (k + 1) * n])) for k in range(4))


def _adamw(w, g, m, v, name):
    rows, cols = w.shape[0], w.shape[-1]
    if w.ndim == 3:
        tr = max(d for d in range(1, rows + 1) if rows % d == 0 and d * 8 * cols * 4 * 14 <= VMEM_LIMIT // 2)
    else:
        tr = _row_tile(rows, 7 * cols * 4, 4 * 1024 * 1024)

    def body(w_ref, g_ref, m_ref, v_ref, d_ref, mo_ref, vo_ref):
        d_ref[...], mo_ref[...], vo_ref[...] = _adam_update(w_ref[...], g_ref[...], m_ref[...], v_ref[...])

    block = (tr,) + w.shape[1:]
    spec = pl.BlockSpec(block, lambda i: (i,) + (0,) * (len(block) - 1))
    return pl.pallas_call(
        body, grid=(rows // tr,), in_specs=[spec] * 4, out_specs=[spec] * 3, out_shape=[_sds(w.shape, F32)] * 3,
        compiler_params=_params(("parallel",)), name=name,
    )(w, g, m, v)


def kernel(x, positions, attn_norm_w, w_in, q_lat_norm_w, w_uq, kv_lat_norm_w, w_ukv, q_norm_w, k_norm_w, mla_out_norm_w, conv_w, a_log, dt_bias, gdn_norm_w, w_out, mlp_norm_w, w_up, w_down, loss_target, m_attn_norm_w, m_w_in, m_q_lat_norm_w, m_w_uq, m_kv_lat_norm_w, m_w_ukv, m_q_norm_w, m_k_norm_w, m_mla_out_norm_w, m_conv_w, m_a_log, m_dt_bias, m_gdn_norm_w, m_w_out, m_mlp_norm_w, m_w_up, m_w_down, v_attn_norm_w, v_w_in, v_q_lat_norm_w, v_w_uq, v_kv_lat_norm_w, v_w_ukv, v_q_norm_w, v_k_norm_w, v_mla_out_norm_w, v_conv_w, v_a_log, v_dt_bias, v_gdn_norm_w, v_w_out, v_mlp_norm_w, v_w_up, v_w_down):
    w = dict(zip(WEIGHTS, (attn_norm_w, w_in, q_lat_norm_w, w_uq, kv_lat_norm_w, w_ukv, q_norm_w, k_norm_w, mla_out_norm_w, conv_w,
                           a_log, dt_bias, gdn_norm_w, w_out, mlp_norm_w, w_up, w_down)))
    m = dict(zip(WEIGHTS, (m_attn_norm_w, m_w_in, m_q_lat_norm_w, m_w_uq, m_kv_lat_norm_w, m_w_ukv, m_q_norm_w, m_k_norm_w,
                           m_mla_out_norm_w, m_conv_w, m_a_log, m_dt_bias, m_gdn_norm_w, m_w_out, m_mlp_norm_w, m_w_up, m_w_down)))
    v = dict(zip(WEIGHTS, (v_attn_norm_w, v_w_in, v_q_lat_norm_w, v_w_uq, v_kv_lat_norm_w, v_w_ukv, v_q_norm_w, v_k_norm_w,
                           v_mla_out_norm_w, v_conv_w, v_a_log, v_dt_bias, v_gdn_norm_w, v_w_out, v_mlp_norm_w, v_w_up, v_w_down)))
    B, S, D = x.shape
    T = B * S
    x2, pos, target = x.reshape(T, D), positions.reshape(T, 1), loss_target.reshape(T, D)
    seq = lambda a: a.reshape(B, S, a.shape[-1])
    tok = lambda a: a.reshape(T, a.shape[-1])
    local = {n: w[n][0] for n in SHARDED}

    g_in, g_uq, g_ukv, g_conv = _all_gather([jnp.swapaxes(w_in, 1, 2)[0].astype(BF16), local["w_uq"].astype(BF16),
                                             local["w_ukv"].astype(BF16), local["conv_w"]], "gather_first_weights")
    w_in_p = _widen_w_in_t(g_in.reshape(-1, D))
    w_mla = _stack_mla(_from_column_shards(g_uq), _from_column_shards(g_ukv))
    conv_full = _from_column_shards(g_conv)
    ln_w = jnp.concatenate([q_lat_norm_w, kv_lat_norm_w], axis=0)
    qk_nw = _qk_norm_rows(q_norm_w, k_norm_w)
    rope_rows = _rope_rows()
    scal = _rows8([jnp.pad(a_log, ((0, 0), (0, 128 - HEADS))), jnp.pad(dt_bias, ((0, 0), (0, 128 - HEADS)))])
    mix_nw = _rows8([mla_out_norm_w[0], gdn_norm_w])

    xn, lat, gqkv, gz, gab = _in_proj_fwd(x2, attn_norm_w, w_in_p)
    q, k, v_att = _mla_pre_fwd(lat, pos, ln_w, w_mla, qk_nw, rope_rows)
    ao, lse, g_down = _attn_fwd(seq(q), seq(k), seq(v_att), [local["w_down"].astype(BF16)])
    gq, gk, gv = _gdn_pre_fwd(seq(gqkv), conv_full)
    go, states, powers, solutions, g_out, w_up_b = _gdn_chunk_fwd(gq, gk, gv, seq(gab), scal,
                                                                  [local["w_out"].astype(BF16), local["w_up"].astype(BF16)])
    w_out_b = g_out.reshape(-1, D)
    w_down_b = g_down.reshape(-1, D)
    mix, h2 = _mix_fwd(tok(ao), tok(go), gz, mix_nw, w_out_b, x2)
    hn, act, dy, sq = _mlp_fwd(h2, mlp_norm_w, w_up_b, w_down_b, target)

    dh, d_up, d_mlp_nw = _mlp_bwd(h2, mlp_norm_w, act, w_up_b, w_down_b, dy)
    p_down = _wgrad(act, dy, "wgrad_down").reshape(4, -1, D)
    p_up = _wgrad(hn, d_up, "wgrad_up", column_shards=4)
    d_ao, d_go, d_gz, d_mix_nw = _mix_bwd(tok(ao), tok(go), gz, mix_nw, w_out_b, dh)
    p_out = _wgrad(mix, dh, "wgrad_out").reshape(4, -1, D)
    d_gq, d_gk, d_gv, d_gab, d_scal, s_up, s_out = _gdn_chunk_bwd(gq, gk, gv, seq(gab), scal, states, powers, solutions, seq(d_go),
                                                                  [p_up, p_out])
    early = ("w_up", "w_out", "w_down")
    dxq, dxk, dxv, dcq, dck, dcv, g_out = _gdn_pre_bwd(seq(gqkv), conv_full, d_gq, d_gk, d_gv, [_sum_slots(s_out, "sum_w_out")])
    dq, dk, dv, s_down, g_up = _attn_bwd(seq(q), seq(k), seq(v_att), ao, lse, seq(d_ao), [p_down], [_sum_slots(s_up, "sum_w_up")])
    d_lat, d_ln, d_w_mla, d_qk_nw = _mla_pre_bwd(lat, pos, ln_w, w_mla, qk_nw, rope_rows, tok(dq), tok(dk), tok(dv), [])
    d_pieces = [d_lat, tok(dxq), tok(dxk), tok(dxv), d_gz, tok(d_gab)]
    p_in = _narrow_w_in_t(_wgrad_pieces(d_pieces, xn, "wgrad_in")).reshape(4, -1, D)
    p_uq, p_ukv = (_column_shards(a).astype(BF16) for a in _unstack_mla(d_w_mla))
    grad_x2, d_attn_nw, s_in, s_uq, s_ukv, g_down = _in_proj_bwd(d_pieces, w_in_p, x2, attn_norm_w, dh, [p_in, p_uq, p_ukv],
                                                                 [_sum_slots(s_down, "sum_w_down")])
    early_grads = [g_up, g_out, g_down]
    small_buf, conv_buf = _pack_small_partials(d_attn_nw, d_mlp_nw, d_ln, d_qk_nw, d_mix_nw, d_scal, (dcq, dck, dcv), sq)

    late = ("w_in", "w_uq", "w_ukv")
    *late_grads, s_small, s_conv = _exchange_halves([_sum_slots(s, "sum_" + n) for n, s in zip(late, (s_in, s_uq, s_ukv))],
                                                    [small_buf, conv_buf])
    names = early + late
    grad = {n: g.reshape(-1, g.shape[-1]) for n, g in zip(names, list(early_grads) + list(late_grads))}

    loss, g_small, delta, new_m, new_v = _adamw_small(s_small, s_conv, w, m, v)
    grad.update(g_small)
    for n in names:
        if n == "w_in":
            stored = lambda a: jnp.transpose(a, (2, 0, 1))
            outs = _adamw(stored(w[n]), grad[n][:, None, :], stored(m[n]), stored(v[n]), "adamw_" + n)
            grad[n], delta[n], new_m[n], new_v[n] = (jnp.transpose(a, (1, 2, 0)) for a in (grad[n][:, None, :], *outs))
        else:
            delta[n], new_m[n], new_v[n] = _adamw(local[n], grad[n], m[n][0], v[n][0], "adamw_" + n)
    def in_order(d):
        return [d[n].reshape(w[n].shape) for n in WEIGHTS]

    return (loss.reshape(()), grad_x2.reshape(B, S, D), *in_order(grad), *in_order(delta), *in_order(new_m), *in_order(new_v))
```

```python
import functools
import math

import jax
import jax.numpy as jnp
from jax import lax
from jax.experimental import pallas as pl
from jax.experimental.pallas import tpu as pltpu

F32 = jnp.float32
BF16 = jnp.bfloat16
MESH = pl.DeviceIdType.MESH

EPS = 1e-6
HEADS = 4
HEAD_DIM = 128
ROPE_DIM = 64
ROPE_HALF = 32
QK_DIM = 192
QK_PAD = 256
LORA = 256
CHUNK = 64
SOLVE_POWERS = 5
CONV_TAPS = 4
ROPE_THETA = 10000.0
ATTN_SCALE = QK_DIM ** -0.5

LAT_W = 640
GQKV_W = 3 * HEADS * HEAD_DIM
GZ_W = HEADS * HEAD_DIM
GAB_W = 128
PROJ_SPLITS = ((0, LAT_W), (LAT_W, LAT_W + GQKV_W), (LAT_W + GQKV_W, LAT_W + GQKV_W + GZ_W),
               (LAT_W + GQKV_W + GZ_W, LAT_W + GQKV_W + GZ_W + GAB_W))
PROJ_W = PROJ_SPLITS[-1][1]

ADAM_LR = 0.001
ADAM_B1 = 0.9
ADAM_B2 = 0.999
ADAM_EPS = 1e-08
ADAM_WD = 0.01
ADAM_STEP = 10

TOKEN_TILE = 512
WGRAD_TOKEN_TILE = 1024
MLP_TOKEN_TILE = 512
FF_TILE = 1024
ATTN_TILE = 512
ATTN_HEADS_PER_STEP = 2
WGRAD_OUT_BYTES = 8 * 1024 * 1024
VMEM_LIMIT = 48 * 1024 * 1024

SHARDED = ("w_in", "w_uq", "w_ukv", "conv_w", "w_out", "w_up", "w_down")
WEIGHTS = ("attn_norm_w", "w_in", "q_lat_norm_w", "w_uq", "kv_lat_norm_w", "w_ukv", "q_norm_w", "k_norm_w", "mla_out_norm_w",
           "conv_w", "a_log", "dt_bias", "gdn_norm_w", "w_out", "mlp_norm_w", "w_up", "w_down")


def _sds(shape, dtype):
    return jax.ShapeDtypeStruct(shape, dtype)


def _params(semantics):
    return pltpu.CompilerParams(dimension_semantics=semantics, vmem_limit_bytes=VMEM_LIMIT)


def _block(n):
    for b in (512, 256, 128):
        if n % b == 0:
            return b
    return n


def _dg(a, b, ca, cb, prec):
    lead = a.ndim - 2
    batch = (tuple(range(lead)),) * 2
    return lax.dot_general(a, b, (((ca + lead,), (cb + lead,)), batch), precision=prec, preferred_element_type=F32)


def _split_bf16(a):
    hi = a.astype(BF16)
    return hi, (a - hi.astype(F32)).astype(BF16)


def _dot_bf16(a, b, ca, cb):
    return _dg(a.astype(BF16), b.astype(BF16), ca, cb, None)


def _dot_bf16x3(a, b, ca, cb):
    a_hi, a_lo = _split_bf16(a)
    b_hi, b_lo = _split_bf16(b)
    lead = a.ndim - 2
    return _dg(jnp.concatenate([a_hi, a_hi, a_lo], axis=ca + lead), jnp.concatenate([b_hi, b_lo, b_hi], axis=cb + lead), ca, cb, None)


def _matmul_family(dot):
    def nn_raw(a, b):
        return dot(a, b, 1, 0)

    def nt_raw(a, b):
        return dot(a, b, 1, 1)

    def tn_raw(a, b):
        return dot(a, b, 0, 0)

    @jax.custom_vjp
    def nn(a, b):
        return nn_raw(a, b)

    nn.defvjp(lambda a, b: (nn_raw(a, b), (a, b)), lambda r, g: (nt_raw(g, r[1]), tn_raw(r[0], g)))

    @jax.custom_vjp
    def nt(a, b):
        return nt_raw(a, b)

    nt.defvjp(lambda a, b: (nt_raw(a, b), (a, b)), lambda r, g: (nn_raw(g, r[1]), tn_raw(g, r[0])))

    @jax.custom_vjp
    def tn(a, b):
        return tn_raw(a, b)

    tn.defvjp(lambda a, b: (tn_raw(a, b), (a, b)), lambda r, g: (nt_raw(r[1], g), nn_raw(r[0], g)))
    return nn, nt, tn


_bf_nn, _bf_nt, _bf_tn = _matmul_family(_dot_bf16)
_hi_nn, _hi_nt, _hi_tn = _matmul_family(_dot_bf16x3)


def _lower_powers(lmat):
    powers = []
    while 2 ** (len(powers) + 1) < lmat.shape[-1]:
        powers.append(_dot_bf16x3(powers[-1] if powers else lmat, powers[-1] if powers else lmat, 1, 0))
    return powers


@jax.custom_vjp
def _unit_lower_solve(lmat, rhs):
    return _unit_lower_solve_fwd(lmat, rhs)[0]


def _unit_lower_solve_fwd(lmat, rhs):
    powers = _lower_powers(lmat)
    x = rhs - _dot_bf16x3(lmat, rhs, 1, 0)
    for p in powers:
        x = x + _dot_bf16x3(p, x, 1, 0)
    return x, (lmat, powers, x)


def _unit_lower_solve_bwd(res, g):
    lmat, powers, x = res
    y = g - _dot_bf16x3(lmat, g, 0, 0)
    for p in powers:
        y = y + _dot_bf16x3(p, y, 0, 0)
    return -_dot_bf16x3(y, x, 1, 1), y


_unit_lower_solve.defvjp(_unit_lower_solve_fwd, _unit_lower_solve_bwd)


@jax.custom_vjp
def _unit_lower_solve_kept(lmat, rhs, powers, x):
    return x


_unit_lower_solve_kept.defvjp(
    lambda lmat, rhs, powers, x: (x, (lmat, powers, x)),
    lambda res, g: _unit_lower_solve_bwd(res, g) + ([jnp.zeros_like(p) for p in res[1]], jnp.zeros_like(res[2])))


@jax.custom_vjp
def _lane_halves(x):
    n = x.shape[-1] // 2
    return x[..., :n], x[..., n:]


_lane_halves.defvjp(lambda x: (_lane_halves(x), None), lambda _, g: (jnp.concatenate(g, axis=-1),))


@jax.custom_vjp
def _row_halves(x):
    n = x.shape[-2] // 2
    return x[..., :n, :], x[..., n:, :]


_row_halves.defvjp(lambda x: (_row_halves(x), None), lambda _, g: (jnp.concatenate(g, axis=-2),))


@jax.custom_vjp
def _swap_halves(t):
    return pltpu.roll(t, 64, 1)


_swap_halves.defvjp(lambda t: (pltpu.roll(t, 64, 1), None), lambda _, g: (pltpu.roll(g, 64, 1),))


@functools.partial(jax.custom_vjp, nondiff_argnums=(2,))
def _shift_rows(x, keep, s):
    return pltpu.roll(x, s, 0) * keep


def _shift_rows_fwd(x, keep, s):
    return pltpu.roll(x, s, 0) * keep, keep


def _shift_rows_bwd(s, keep, g):
    return pltpu.roll(g * keep, keep.shape[0] - s, 0), jnp.zeros_like(keep)


_shift_rows.defvjp(_shift_rows_fwd, _shift_rows_bwd)


def _sigmoid(x):
    return 0.5 * jnp.tanh(0.5 * x) + 0.5


def _softplus(x):
    return jnp.maximum(x, 0.0) + jnp.log(1.0 + jnp.exp(jnp.minimum(x, -x)))


def _silu(x):
    return x * _sigmoid(x)


def _rms(x, w, n=None):
    n = x.shape[-1] if n is None else n
    r = lax.rsqrt(jnp.sum(x * x, axis=-1, keepdims=True) * (1.0 / n) + EPS)
    return x * r * w


def _rope(t, cos_f, sin_f):
    return t * cos_f + _swap_halves(t) * sin_f


def _rope_tables(pos_col, freq_row, sign_row):
    ang = pos_col.astype(F32) * freq_row
    return jnp.cos(ang), jnp.sin(ang) * sign_row


def _onehot_row(lane):
    return (lax.broadcasted_iota(jnp.int32, (1, 128), 1) == lane).astype(F32)


def _row_spec(tm, w):
    return pl.BlockSpec((tm, w), lambda i: (i, 0))


def _const_spec(shape):
    return pl.BlockSpec(shape, lambda *_: (0,) * len(shape))


def _in_proj_fwd(x2, w_an, w_in_p):
    T, D = x2.shape
    tm = min(TOKEN_TILE, T)

    def body(x_ref, wn_ref, w_ref, xn_ref, lat_ref, gqkv_ref, gz_ref, gab_ref):
        x = x_ref[...]
        r = lax.rsqrt(jnp.mean(x * x, axis=-1, keepdims=True) + EPS)
        xn = (x * r * wn_ref[...]).astype(BF16)
        xn_ref[...] = xn
        for ref, (a, b) in zip((lat_ref, gqkv_ref, gz_ref, gab_ref), PROJ_SPLITS):
            ref[...] = _dg(xn, w_ref[a:b, :], 1, 1, None)

    widths = [b - a for a, b in PROJ_SPLITS]
    return pl.pallas_call(
        body, grid=(T // tm,),
        in_specs=[_row_spec(tm, D), _const_spec((1, D)), _const_spec((PROJ_W, D))],
        out_specs=[_row_spec(tm, D)] + [_row_spec(tm, w) for w in widths],
        out_shape=[_sds((T, D), BF16)] + [_sds((T, w), F32) for w in widths],
        compiler_params=_params(("parallel",)), name="in_proj_fwd",
    )(x2, w_an, w_in_p)


def _in_proj_bwd(pieces, w_in_p, x2, w_an, dh, partials, halves):
    T, D = x2.shape
    tm = min(TOKEN_TILE, T)
    widths = [p.shape[1] for p in pieces]
    starts = [sum(widths[:i]) for i in range(len(widths))]
    assert sum(widths) == PROJ_W
    npc, ns, nh = len(pieces), len(partials), len(halves)

    def body(*refs):
        piece_refs = refs[:npc]
        w_ref, x_ref, wn_ref, dh_ref = refs[npc:npc + 4]
        src_refs = refs[npc + 4:npc + 4 + ns + nh]
        dx_ref, dwn_ref = refs[npc + 4 + ns + nh:npc + 6 + ns + nh]
        dst_refs = refs[npc + 6 + ns + nh:npc + 6 + 2 * (ns + nh)]
        sems = refs[npc + 6 + 2 * (ns + nh):]

        def riders():
            scatter = _scatter_copies(src_refs[:ns], dst_refs[:ns], *sems[:3])
            swap = _swap_copies(src_refs[ns:], dst_refs[ns:], *sems[3:])
            return scatter[0] + swap[0], scatter[1] + swap[1]

        @pl.when(pl.program_id(0) == 0)
        def _():
            for start in riders()[0]:
                start()
            dwn_ref[...] = jnp.zeros_like(dwn_ref)

        dxn = jnp.zeros((tm, D), F32)
        for ref, a, width in zip(piece_refs, starts, widths):
            dxn += _dg(ref[...], w_ref[a:a + width, :], 1, 0, None)
        _, pull = jax.vjp(_rms, x_ref[...], wn_ref[...])
        dx, dwn = pull(dxn)
        dx_ref[...] = dx + dh_ref[...]
        dwn_ref[...] += dwn

        @pl.when(pl.program_id(0) == T // tm - 1)
        def _():
            for wait in riders()[1]:
                wait()

    return pl.pallas_call(
        body, grid=(T // tm,),
        in_specs=[_row_spec(tm, w) for w in widths] + [_const_spec((PROJ_W, D)), _row_spec(tm, D), _const_spec((1, D)),
                                                       _row_spec(tm, D)] + [_ANY] * (ns + nh),
        out_specs=[_row_spec(tm, D), _const_spec((1, D))] + [_ANY] * (ns + nh),
        out_shape=[_sds((T, D), F32), _sds((1, D), F32)] + [_scattered_shape(p) for p in partials]
                  + [_swapped_shape(h) for h in halves],
        scratch_shapes=_scatter_scratch(ns) + _swap_scratch(nh),
        compiler_params=_params(("arbitrary",)), name="in_proj_bwd",
    )(*pieces, w_in_p, x2, w_an, dh, *partials, *halves)


def _wgrad_pieces(pieces, b, name, partials):
    T, k2 = b.shape
    tt = min(WGRAD_TOKEN_TILE, T)
    widths = [p.shape[1] for p in pieces]
    starts = [sum(widths[:i]) for i in range(len(widths))]
    k1 = sum(widths)
    npc, ns = len(pieces), len(partials)

    def body(*refs):
        piece_refs, b_ref = refs[:npc], refs[npc]
        src_refs = refs[npc + 1:npc + 1 + ns]
        o_ref = refs[npc + 1 + ns]
        dst_refs = refs[npc + 2 + ns:npc + 2 + 2 * ns]
        acc_ref, *sems = refs[npc + 2 + 2 * ns:]
        t = pl.program_id(0)

        @pl.when(t == 0)
        def _():
            for start in _scatter_copies(src_refs, dst_refs, *sems)[0]:
                start()
            acc_ref[...] = jnp.zeros_like(acc_ref)

        bt = b_ref[...].astype(BF16)
        for ref, r0, width in zip(piece_refs, starts, widths):
            acc_ref[r0:r0 + width, :] += jnp.dot(ref[...].T, bt, preferred_element_type=F32)

        @pl.when(t == T // tt - 1)
        def _():
            o_ref[...] = acc_ref[...].astype(o_ref.dtype)
            for wait in _scatter_copies(src_refs, dst_refs, *sems)[1]:
                wait()

    out, *scattered = pl.pallas_call(
        body, grid=(T // tt,),
        in_specs=[pl.BlockSpec((tt, w), lambda t: (t, 0)) for w in widths] + [pl.BlockSpec((tt, k2), lambda t: (t, 0))] + [_ANY] * ns,
        out_specs=[_const_spec((k1, k2))] + [_ANY] * ns,
        out_shape=[_sds((k1, k2), BF16)] + [_scattered_shape(p) for p in partials],
        scratch_shapes=[pltpu.VMEM((k1, k2), F32)] + _scatter_scratch(ns),
        compiler_params=_params(("arbitrary",)), name=name,
    )(*pieces, b, *partials)
    return out, scattered


def _wgrad(a, b, name, column_shards=1, out_dtype=BF16):
    T, k1 = a.shape
    k2 = b.shape[1]
    per_shard = k2 // column_shards
    tt = min(WGRAD_TOKEN_TILE, T)
    b1 = k1
    while b1 * k2 * 4 > WGRAD_OUT_BYTES and b1 % 256 == 0:
        b1 //= 2
    step = _block(per_shard)

    def body(a_ref, b_ref, o_ref, acc_ref):
        t = pl.program_id(1)

        @pl.when(t == 0)
        def _():
            acc_ref[...] = jnp.zeros_like(acc_ref)

        a_t = a_ref[...].astype(BF16).T
        for c0 in range(0, k2, step):
            part = jnp.dot(a_t, b_ref[:, c0:c0 + step].astype(BF16), preferred_element_type=F32)
            if column_shards == 1:
                acc_ref[:, c0:c0 + step] += part
            else:
                acc_ref[c0 // per_shard, :, c0 % per_shard:c0 % per_shard + step] += part

        @pl.when(t == T // tt - 1)
        def _():
            o_ref[...] = acc_ref[...].astype(o_ref.dtype)

    if column_shards == 1:
        block, out_spec, out_shape = (b1, k2), pl.BlockSpec((b1, k2), lambda i, t: (i, 0)), _sds((k1, k2), out_dtype)
    else:
        block = (column_shards, b1, per_shard)
        out_spec, out_shape = pl.BlockSpec(block, lambda i, t: (0, i, 0)), _sds((column_shards, k1, per_shard), out_dtype)
    return pl.pallas_call(
        body, grid=(k1 // b1, T // tt),
        in_specs=[pl.BlockSpec((tt, b1), lambda i, t: (t, i)), pl.BlockSpec((tt, k2), lambda i, t: (t, 0))],
        out_specs=out_spec, out_shape=out_shape, scratch_shapes=[pltpu.VMEM(block, F32)],
        compiler_params=_params(("parallel", "arbitrary")), name=name,
    )(a, b)


@jax.custom_vjp
def _lane_blocks(x):
    return tuple(x[:, i:i + 128] for i in range(0, x.shape[1], 128))


_lane_blocks.defvjp(lambda x: (_lane_blocks(x), None), lambda _, g: (jnp.concatenate(g, axis=1),))


def _mla_pre_fn(q_lat, kv_lat, kpe, ln_q, ln_kv, w_q, w_kv, qn_n, qn_p, kn_n, kn_p, cos_f, sin_f):
    qn = _rms(q_lat, ln_q)
    kvn = _rms(kv_lat, ln_kv)
    kp = _rope(_rms(kpe, kn_p, ROPE_DIM), cos_f, sin_f)
    q_blocks = _lane_blocks(_bf_nn(qn, w_q))
    kv_blocks = _lane_blocks(_bf_nn(kvn, w_kv))
    outs = []
    for h in range(HEADS):
        outs.append(_rms(q_blocks[h], qn_n))
        outs.append(_rope(_rms(q_blocks[HEADS + h], qn_p, ROPE_DIM), cos_f, sin_f))
        outs.append(_rms(kv_blocks[h], kn_n))
        outs.append(kv_blocks[HEADS + h])
    return tuple(outs) + (kp,)


def _mla_pre_operands(lat_ref, pos_ref, ln_ref, w_ref, nw_ref, rope_ref):
    cos_f, sin_f = _rope_tables(pos_ref[...], rope_ref[0:1, :], rope_ref[1:2, :])
    side_by_side = lambda blocks: jnp.concatenate([w_ref[i].astype(F32) for i in blocks], axis=1)
    diff = (lat_ref[:, 0:LORA], lat_ref[:, LORA:2 * LORA], lat_ref[:, 2 * LORA:LAT_W], ln_ref[0:1, :], ln_ref[1:2, :],
            side_by_side(range(2 * HEADS)), side_by_side(range(2 * HEADS, 4 * HEADS)),
            nw_ref[0:1, :], nw_ref[1:2, :], nw_ref[2:3, :], nw_ref[3:4, :])
    return diff, cos_f, sin_f


def _mla_pre_fwd(lat, pos, ln_w, w_mla, nw, rope_rows):
    T = lat.shape[0]
    tm = min(TOKEN_TILE, T)

    def body(lat_ref, pos_ref, ln_ref, w_ref, nw_ref, rope_ref, q_ref, k_ref, v_ref):
        diff, cos_f, sin_f = _mla_pre_operands(lat_ref, pos_ref, ln_ref, w_ref, nw_ref, rope_ref)
        outs = _mla_pre_fn(*diff, cos_f, sin_f)
        kp = outs[-1].astype(BF16)
        for h in range(HEADS):
            q_n, q_p, k_n, v = outs[4 * h:4 * h + 4]
            q_ref[:, h * QK_PAD:h * QK_PAD + HEAD_DIM] = q_n.astype(BF16)
            q_ref[:, h * QK_PAD + HEAD_DIM:(h + 1) * QK_PAD] = q_p.astype(BF16)
            k_ref[:, h * QK_PAD:h * QK_PAD + HEAD_DIM] = k_n.astype(BF16)
            k_ref[:, h * QK_PAD + HEAD_DIM:(h + 1) * QK_PAD] = kp
            v_ref[:, h * HEAD_DIM:(h + 1) * HEAD_DIM] = v.astype(BF16)

    return pl.pallas_call(
        body, grid=(T // tm,),
        in_specs=[_row_spec(tm, LAT_W), _row_spec(tm, 1), _const_spec((2, LORA)), _const_spec((4 * HEADS, LORA, 128)),
                  _const_spec((8, 128)), _const_spec((8, 128))],
        out_specs=[_row_spec(tm, HEADS * QK_PAD), _row_spec(tm, HEADS * QK_PAD), _row_spec(tm, HEADS * HEAD_DIM)],
        out_shape=[_sds((T, HEADS * QK_PAD), BF16), _sds((T, HEADS * QK_PAD), BF16), _sds((T, HEADS * HEAD_DIM), BF16)],
        compiler_params=_params(("parallel",)), name="mla_pre_fwd",
    )(lat, pos, ln_w, w_mla, nw, rope_rows)


def _mla_pre_bwd(lat, pos, ln_w, w_mla, nw, rope_rows, dq, dk, dv, halves):
    T = lat.shape[0]
    tm = min(TOKEN_TILE, T)
    ns = len(halves)

    def body(*refs):
        lat_ref, pos_ref, ln_ref, w_ref, nw_ref, rope_ref, dq_ref, dk_ref, dv_ref = refs[:9]
        src_refs = refs[9:9 + ns]
        dlat_ref, dln_ref, dw_ref, dnw_ref = refs[9 + ns:13 + ns]
        dst_refs = refs[13 + ns:13 + 2 * ns]
        sems = refs[13 + 2 * ns:]

        @pl.when(pl.program_id(0) == 0)
        def _():
            for start in _swap_copies(src_refs, dst_refs, *sems)[0]:
                start()
            dln_ref[...] = jnp.zeros_like(dln_ref)
            dw_ref[...] = jnp.zeros_like(dw_ref)
            dnw_ref[...] = jnp.zeros_like(dnw_ref)

        diff, cos_f, sin_f = _mla_pre_operands(lat_ref, pos_ref, ln_ref, w_ref, nw_ref, rope_ref)
        _, pull = jax.vjp(lambda *a: _mla_pre_fn(*a, cos_f, sin_f), *diff)
        cts = []
        d_kp = jnp.zeros((tm, 128), F32)
        for h in range(HEADS):
            cts.append(dq_ref[:, h * QK_PAD:h * QK_PAD + HEAD_DIM])
            cts.append(dq_ref[:, h * QK_PAD + HEAD_DIM:(h + 1) * QK_PAD])
            cts.append(dk_ref[:, h * QK_PAD:h * QK_PAD + HEAD_DIM])
            cts.append(dv_ref[:, h * HEAD_DIM:(h + 1) * HEAD_DIM])
            d_kp += dk_ref[:, h * QK_PAD + HEAD_DIM:(h + 1) * QK_PAD]
        d_ql, d_kvl, d_kpe, d_lnq, d_lnkv, d_wq, d_wkv, d_qn_n, d_qn_p, d_kn_n, d_kn_p = pull(tuple(cts) + (d_kp,))
        d_w = [d[:, i:i + 128] for d in (d_wq, d_wkv) for i in range(0, d.shape[1], 128)]
        dlat_ref[:, 0:LORA] = d_ql.astype(BF16)
        dlat_ref[:, LORA:2 * LORA] = d_kvl.astype(BF16)
        dlat_ref[:, 2 * LORA:LAT_W] = d_kpe.astype(BF16)
        dln_ref[0:1, :] += d_lnq
        dln_ref[1:2, :] += d_lnkv
        for i in range(4 * HEADS):
            dw_ref[i] += d_w[i]
        for i, d in enumerate((d_qn_n, d_qn_p, d_kn_n, d_kn_p)):
            dnw_ref[i:i + 1, :] += d

        @pl.when(pl.program_id(0) == T // tm - 1)
        def _():
            for wait in _swap_copies(src_refs, dst_refs, *sems)[1]:
                wait()

    return pl.pallas_call(
        body, grid=(T // tm,),
        in_specs=[_row_spec(tm, LAT_W), _row_spec(tm, 1), _const_spec((2, LORA)), _const_spec((4 * HEADS, LORA, 128)),
                  _const_spec((8, 128)), _const_spec((8, 128)),
                  _row_spec(tm, HEADS * QK_PAD), _row_spec(tm, HEADS * QK_PAD), _row_spec(tm, HEADS * HEAD_DIM)] + [_ANY] * ns,
        out_specs=[_row_spec(tm, LAT_W), _const_spec((2, LORA)), _const_spec((4 * HEADS, LORA, 128)), _const_spec((8, 128))]
                  + [_ANY] * ns,
        out_shape=[_sds((T, LAT_W), BF16), _sds((2, LORA), F32), _sds((4 * HEADS, LORA, 128), F32), _sds((8, 128), F32)]
                  + [_swapped_shape(h) for h in halves],
        scratch_shapes=_swap_scratch(ns),
        compiler_params=_params(("arbitrary",)), name="mla_pre_bwd",
    )(lat, pos, ln_w, w_mla, nw, rope_rows, dq, dk, dv, *halves)


def _causal_mask(i, j, tq, tk):
    row = i * tq + lax.broadcasted_iota(jnp.int32, (tq, tk), 0)
    col = j * tk + lax.broadcasted_iota(jnp.int32, (tq, tk), 1)
    return col <= row


def _attn_fwd(q, k, v, shards):
    B, S, _ = q.shape
    t = min(ATTN_TILE, S)
    nq = S // t
    ns = len(shards)

    hp = ATTN_HEADS_PER_STEP
    qk = lambda h: slice(h * QK_PAD, (h + 1) * QK_PAD)
    vd = lambda h: slice(h * HEAD_DIM, (h + 1) * HEAD_DIM)

    def body(*refs):
        q_ref, k_ref, v_ref = refs[:3]
        src_refs = refs[3:3 + ns]
        o_ref, lse_ref = refs[3 + ns:5 + ns]
        dst_refs = refs[5 + ns:5 + 2 * ns]
        sems = refs[5 + 2 * ns:]
        b, g, i = pl.program_id(0), pl.program_id(1), pl.program_id(2)
        qb = [q_ref[0, :, qk(h)] for h in range(hp)]

        step_no = (b * (HEADS // hp) + g) * nq + i
        for phase, at in enumerate((0, (3 * B * (HEADS // hp) * nq) // 4)):
            @pl.when(step_no == at)
            def _(phase=phase):
                for call in _gather_copies(src_refs, dst_refs, *sems)[phase]:
                    call()

        def step(j, carry, diagonal):
            rows = pl.ds(pl.multiple_of(j * t, t), t)
            s = [_dg(qb[h], k_ref[0, rows, qk(h)], 1, 1, None) * ATTN_SCALE for h in range(hp)]
            if diagonal:
                keep = _causal_mask(0, 0, t, t)
                s = [jnp.where(keep, x, -1e30) for x in s]
            m_new = [jnp.maximum(carry[h][0], jnp.max(s[h], axis=-1, keepdims=True)) for h in range(hp)]
            p = [jnp.exp(s[h] - m_new[h]) for h in range(hp)]
            alpha = [jnp.exp(carry[h][0] - m_new[h]) for h in range(hp)]
            l = [alpha[h] * carry[h][1] + jnp.sum(p[h], axis=-1, keepdims=True) for h in range(hp)]
            pv = [jnp.dot(p[h].astype(BF16), v_ref[0, rows, vd(h)], preferred_element_type=F32) for h in range(hp)]
            return tuple((m_new[h], l[h], alpha[h] * carry[h][2] + pv[h]) for h in range(hp))

        init = tuple((jnp.full((t, 1), -1e30, F32), jnp.zeros((t, 1), F32), jnp.zeros((t, HEAD_DIM), F32)) for _ in range(hp))
        below = lax.fori_loop(0, i, lambda j, carry: step(j, carry, False), init)
        for h, (m, l, acc) in enumerate(step(i, below, True)):
            o_ref[0, :, vd(h)] = acc / l
            lse_ref[0, h, 0] = (m + jnp.log(l)).T

        @pl.when((b == B - 1) & (g == HEADS // hp - 1) & (i == nq - 1))
        def _():
            for wait in _gather_copies(src_refs, dst_refs, *sems)[2]:
                wait()

    return pl.pallas_call(
        body, grid=(B, HEADS // hp, nq),
        in_specs=[pl.BlockSpec((1, t, hp * QK_PAD), lambda b, g, i: (b, i, g)),
                  pl.BlockSpec((1, S, hp * QK_PAD), lambda b, g, i: (b, 0, g)),
                  pl.BlockSpec((1, S, hp * HEAD_DIM), lambda b, g, i: (b, 0, g))] + [_ANY] * ns,
        out_specs=[pl.BlockSpec((1, t, hp * HEAD_DIM), lambda b, g, i: (b, i, g)),
                   pl.BlockSpec((1, hp, 1, 1, t), lambda b, g, i: (b, g, i, 0, 0))] + [_ANY] * ns,
        out_shape=[_sds((B, S, HEADS * HEAD_DIM), F32), _sds((B, HEADS, nq, 1, t), F32)] + [_sds((4,) + s.shape, s.dtype) for s in shards],
        scratch_shapes=_gather_scratch(ns),
        compiler_params=_params(("arbitrary", "arbitrary", "arbitrary")), name="attn_fwd",
    )(q, k, v, *shards)


def _attn_bwd(q, k, v, o, lse, do, partials):
    B, S, _ = q.shape
    t = min(ATTN_TILE, S)
    nq = S // t
    ns = len(partials)

    hp = ATTN_HEADS_PER_STEP
    qk = lambda h: slice(h * QK_PAD, (h + 1) * QK_PAD)
    vd = lambda h: slice(h * HEAD_DIM, (h + 1) * HEAD_DIM)
    heads = range(hp)

    def body(*refs):
        q_ref, k_ref, v_ref, o_ref, lse_ref, do_ref = refs[:6]
        src_refs = refs[6:6 + ns]
        dq_ref, dk_ref, dv_ref = refs[6 + ns:9 + ns]
        dst_refs = refs[9 + ns:9 + 2 * ns]
        dsum_ref, send_sems, recv_sems, local_sems = refs[9 + 2 * ns:]
        b, g, j = pl.program_id(0), pl.program_id(1), pl.program_id(2)

        @pl.when((b == 0) & (g == 0) & (j == 0))
        def _():
            for start in _scatter_copies(src_refs, dst_refs, send_sems, recv_sems, local_sems)[0]:
                start()

        @pl.when(j == 0)
        def _():
            dq_ref[...] = jnp.zeros_like(dq_ref)
            for h in heads:
                for blk in range(nq):
                    rows = slice(blk * t, (blk + 1) * t)
                    dsum_ref[h, blk] = jnp.sum(do_ref[0, rows, vd(h)] * o_ref[0, rows, vd(h)], axis=-1, keepdims=True).T

        kb = [k_ref[0, :, qk(h)] for h in heads]
        vb = [v_ref[0, :, vd(h)] for h in heads]

        def step(i, carry, diagonal):
            rows = pl.ds(pl.multiple_of(i * t, t), t)
            qb = [q_ref[0, rows, qk(h)] for h in heads]
            dob = [do_ref[0, rows, vd(h)].astype(BF16) for h in heads]
            s = [_dg(kb[h], qb[h], 1, 1, None) * ATTN_SCALE for h in heads]
            p = [jnp.exp(s[h] - lse_ref[0, h, i]) for h in heads]
            if diagonal:
                key = lax.broadcasted_iota(jnp.int32, (t, t), 0)
                query = lax.broadcasted_iota(jnp.int32, (t, t), 1)
                p = [jnp.where(key <= query, x, 0.0) for x in p]
            dp = [_dg(vb[h], dob[h], 1, 1, None) for h in heads]
            dv = [carry[h][1] + jnp.dot(p[h].astype(BF16), dob[h], preferred_element_type=F32) for h in heads]
            ds = [(p[h] * (dp[h] - dsum_ref[h, i]) * ATTN_SCALE).astype(BF16) for h in heads]
            for h in heads:
                dq_ref[0, rows, qk(h)] += _dg(ds[h], kb[h], 0, 0, None)
            return tuple((carry[h][0] + jnp.dot(ds[h], qb[h], preferred_element_type=F32), dv[h]) for h in heads)

        zeros = tuple((jnp.zeros((t, QK_PAD), F32), jnp.zeros((t, HEAD_DIM), F32)) for _ in heads)
        on_diagonal = step(j, zeros, True)
        done = lax.fori_loop(j + 1, nq, lambda i, carry: step(i, carry, False), on_diagonal)
        for h, (dk, dv) in enumerate(done):
            dk_ref[0, :, qk(h)] = dk
            dv_ref[0, :, vd(h)] = dv

        @pl.when((b == B - 1) & (g == HEADS // hp - 1) & (j == nq - 1))
        def _():
            for wait in _scatter_copies(src_refs, dst_refs, send_sems, recv_sems, local_sems)[1]:
                wait()

    return pl.pallas_call(
        body, grid=(B, HEADS // hp, nq),
        in_specs=[pl.BlockSpec((1, S, hp * QK_PAD), lambda b, g, j: (b, 0, g)),
                  pl.BlockSpec((1, t, hp * QK_PAD), lambda b, g, j: (b, j, g)),
                  pl.BlockSpec((1, t, hp * HEAD_DIM), lambda b, g, j: (b, j, g)),
                  pl.BlockSpec((1, S, hp * HEAD_DIM), lambda b, g, j: (b, 0, g)),
                  pl.BlockSpec((1, hp, nq, 1, t), lambda b, g, j: (b, g, 0, 0, 0)),
                  pl.BlockSpec((1, S, hp * HEAD_DIM), lambda b, g, j: (b, 0, g))] + [_ANY] * ns,
        out_specs=[pl.BlockSpec((1, S, hp * QK_PAD), lambda b, g, j: (b, 0, g)),
                   pl.BlockSpec((1, t, hp * QK_PAD), lambda b, g, j: (b, j, g)),
                   pl.BlockSpec((1, t, hp * HEAD_DIM), lambda b, g, j: (b, j, g))] + [_ANY] * ns,
        out_shape=[_sds((B, S, HEADS * QK_PAD), F32), _sds((B, S, HEADS * QK_PAD), F32), _sds((B, S, HEADS * HEAD_DIM), F32)]
                  + [_scattered_shape(p) for p in partials],
        scratch_shapes=[pltpu.VMEM((hp, nq, 1, t), F32)] + _scatter_scratch(ns),
        compiler_params=_params(("arbitrary", "arbitrary", "arbitrary")), name="attn_bwd",
    )(q, k, v, o, lse, do, *partials)


def _gdn_pre_fn(xq, xk, xv, wq, wk, wv, keeps):
    def conv_silu(x, w):
        acc = x * w[3]
        for s in (1, 2, 3):
            acc = acc + _shift_rows(x, keeps[s - 1], s) * w[3 - s]
        return _silu(acc)

    def l2(x):
        return x * lax.rsqrt(jnp.sum(x * x, axis=-1, keepdims=True) + EPS)

    return l2(conv_silu(xq, wq)) * (HEAD_DIM ** -0.5), l2(conv_silu(xk, wk)), conv_silu(xv, wv)


def _gdn_pre_specs(S):
    x_specs = [pl.BlockSpec((1, S, HEAD_DIM), lambda h, b, g=g: (b, 0, g * HEADS + h)) for g in range(3)]
    w_specs = [pl.BlockSpec((CONV_TAPS, HEAD_DIM), lambda h, b, g=g: (0, g * HEADS + h)) for g in range(3)]
    out_spec = pl.BlockSpec((1, S, HEAD_DIM), lambda h, b: (b, 0, h))
    return x_specs, w_specs, out_spec


def _row_keeps(S):
    t = lax.broadcasted_iota(jnp.int32, (S, HEAD_DIM), 0)
    return [(t >= s).astype(F32) for s in (1, 2, 3)]


def _gdn_pre_fwd(gqkv, conv_w):
    B, S, _ = gqkv.shape
    x_specs, w_specs, out_spec = _gdn_pre_specs(S)

    def body(xq_ref, xk_ref, xv_ref, wq_ref, wk_ref, wv_ref, q_ref, k_ref, v_ref):
        taps = [[w[i:i + 1, :] for i in range(CONV_TAPS)] for w in (wq_ref, wk_ref, wv_ref)]
        q, k, v = _gdn_pre_fn(xq_ref[0], xk_ref[0], xv_ref[0], *taps, _row_keeps(S))
        q_ref[0], k_ref[0], v_ref[0] = q, k, v

    return pl.pallas_call(
        body, grid=(HEADS, B), in_specs=x_specs + w_specs, out_specs=[out_spec] * 3,
        out_shape=[_sds((B, S, HEADS * HEAD_DIM), F32)] * 3,
        compiler_params=_params(("parallel", "parallel")), name="gdn_pre_fwd",
    )(gqkv, gqkv, gqkv, conv_w, conv_w, conv_w)


def _gdn_pre_bwd(gqkv, conv_w, dq, dk, dv, halves):
    B, S, _ = gqkv.shape
    x_specs, w_specs, out_spec = _gdn_pre_specs(S)
    dw_spec = pl.BlockSpec((CONV_TAPS, HEAD_DIM), lambda h, b: (0, h))
    ns = len(halves)

    def body(*refs):
        xq_ref, xk_ref, xv_ref, wq_ref, wk_ref, wv_ref, dq_ref, dk_ref, dv_ref = refs[:9]
        src_refs = refs[9:9 + ns]
        dxq_ref, dxk_ref, dxv_ref, dwq_ref, dwk_ref, dwv_ref = refs[9 + ns:15 + ns]
        dst_refs = refs[15 + ns:15 + 2 * ns]
        sems = refs[15 + 2 * ns:]
        first = (pl.program_id(0) == 0) & (pl.program_id(1) == 0)
        last = (pl.program_id(0) == HEADS - 1) & (pl.program_id(1) == B - 1)

        @pl.when(first)
        def _():
            for start in _swap_copies(src_refs, dst_refs, *sems)[0]:
                start()

        @pl.when(pl.program_id(1) == 0)
        def _():
            for r in (dwq_ref, dwk_ref, dwv_ref):
                r[...] = jnp.zeros_like(r)

        taps = [[w[i:i + 1, :] for i in range(CONV_TAPS)] for w in (wq_ref, wk_ref, wv_ref)]
        keeps = _row_keeps(S)
        _, pull = jax.vjp(lambda *a: _gdn_pre_fn(*a, keeps), xq_ref[0], xk_ref[0], xv_ref[0], *taps)
        dxq, dxk, dxv, dwq, dwk, dwv = pull((dq_ref[0], dk_ref[0], dv_ref[0]))
        dxq_ref[0], dxk_ref[0], dxv_ref[0] = dxq.astype(BF16), dxk.astype(BF16), dxv.astype(BF16)
        for ref, dw in ((dwq_ref, dwq), (dwk_ref, dwk), (dwv_ref, dwv)):
            for i in range(CONV_TAPS):
                ref[i:i + 1, :] += dw[i]

        @pl.when(last)
        def _():
            for wait in _swap_copies(src_refs, dst_refs, *sems)[1]:
                wait()

    hw = HEADS * HEAD_DIM
    return pl.pallas_call(
        body, grid=(HEADS, B), in_specs=x_specs + w_specs + [out_spec] * 3 + [_ANY] * ns,
        out_specs=[out_spec] * 3 + [dw_spec] * 3 + [_ANY] * ns,
        out_shape=[_sds((B, S, hw), BF16)] * 3 + [_sds((CONV_TAPS, hw), F32)] * 3 + [_swapped_shape(h) for h in halves],
        scratch_shapes=_swap_scratch(ns),
        compiler_params=_params(("arbitrary", "arbitrary")), name="gdn_pre_bwd",
    )(gqkv, gqkv, gqkv, conv_w, conv_w, conv_w, dq, dk, dv, *halves)


def _chunk_masks():
    i = lax.broadcasted_iota(jnp.int32, (CHUNK, CHUNK), 0)
    j = lax.broadcasted_iota(jnp.int32, (CHUNK, CHUNK), 1)
    lower, after = (j <= i).astype(F32), (j > i).astype(F32)
    return {"le": lower, "le_gt": jnp.concatenate([lower, after], axis=0), "strict": (j < i).astype(F32)}


def _gdn_chunk_fn(groups, masks, solve=_unit_lower_solve):
    lane = lax.broadcasted_iota(jnp.int32, (groups, 1, 128), 2)
    head = lax.broadcasted_iota(jnp.int32, (groups, 1, 128), 0) % HEADS
    pick_a, pick_b = (lane == head).astype(F32), (lane == head + HEADS).astype(F32)
    lower, lower_after, strict = (jnp.broadcast_to(masks[n], (groups,) + masks[n].shape) for n in ("le", "le_gt", "strict"))
    ones_row = jnp.ones((1, 1, HEAD_DIM), F32)

    def f(q, k, v, gab, a_row, dt_row, state):
        ga = jnp.sum(gab * pick_a, axis=2, keepdims=True)
        gb = jnp.sum(gab * pick_b, axis=2, keepdims=True)
        a_log = jnp.sum(a_row * pick_a, axis=2, keepdims=True)
        dt_bias = jnp.sum(dt_row * pick_a, axis=2, keepdims=True)
        beta = _sigmoid(gb)
        g = -jnp.exp(a_log) * _softplus(ga + dt_bias)
        g_wide = g * ones_row
        cum, rest = _row_halves(_hi_nn(lower_after, g_wide))
        total = jnp.sum(g_wide, axis=1, keepdims=True)
        diff = _hi_nn(lower, g * strict)
        decay = lower * jnp.exp(diff)
        e_cum = jnp.exp(cum)
        kk, qk = _row_halves(_bf_nt(jnp.concatenate([k, q], axis=1), k))
        lmat = strict * (beta * kk * decay)
        u, w = _lane_halves(solve(lmat, jnp.concatenate([v * beta, k * (beta * e_cum)], axis=2)))
        w_state, q_state = _row_halves(_bf_nn(jnp.concatenate([w, q * e_cum], axis=1), state))
        v_new = u - w_state
        o = q_state + _bf_nn(qk * decay, v_new)
        new_state = state * jnp.exp(total) + _bf_tn(k * jnp.exp(rest), v_new)
        return o, new_state

    return f


def _gdn_chunk_fwd(q, k, v, gab, scal, shards):
    B, S, W = q.shape
    N = S // CHUNK
    ns = len(shards)

    def body(*refs):
        q_ref, k_ref, v_ref, gab_ref, sc_ref = refs[:5]
        src_refs = refs[5:5 + ns]
        o_ref, st_ref, pw_ref, sol_ref = refs[5 + ns:9 + ns]
        dst_refs = refs[9 + ns:9 + 2 * ns]
        state_ref, send_sems, recv_sems, local_sems = refs[9 + 2 * ns:]
        n = pl.program_id(0)
        kept = {}

        @pl.when(n == 0)
        def _():
            for start in _gather_copies(src_refs, dst_refs, send_sems, recv_sems, local_sems)[0]:
                start()
            state_ref[...] = jnp.zeros_like(state_ref)

        @pl.when(n == (2 * N) // 3)
        def _():
            for pass_on in _gather_copies(src_refs, dst_refs, send_sems, recv_sems, local_sems)[1]:
                pass_on()

        groups = [(b, h) for b in range(B) for h in range(HEADS)]
        gather = lambda ref: jnp.stack([ref[b, :, h * HEAD_DIM:(h + 1) * HEAD_DIM] for b, h in groups])
        state = state_ref[...]
        for i, (b, h) in enumerate(groups):
            st_ref[b, 0, h] = state[i]
        def solve_and_keep(lmat, rhs):
            kept["x"], (_, kept["powers"], _) = _unit_lower_solve_fwd(lmat, rhs)
            return kept["x"]

        o, new_state = _gdn_chunk_fn(len(groups), _chunk_masks(), solve_and_keep)(
            gather(q_ref), gather(k_ref), gather(v_ref), jnp.stack([gab_ref[b] for b, _ in groups]), sc_ref[0:1, :], sc_ref[1:2, :], state)
        for i, (b, h) in enumerate(groups):
            o_ref[b, :, h * HEAD_DIM:(h + 1) * HEAD_DIM] = o[i]
            sol_ref[b, 0, h] = kept["x"][i]
            for p, power in enumerate(kept["powers"]):
                pw_ref[b, 0, h, p] = power[i]
        state_ref[...] = new_state

        @pl.when(n == N - 1)
        def _():
            for wait in _gather_copies(src_refs, dst_refs, send_sems, recv_sems, local_sems)[2]:
                wait()

    seq = pl.BlockSpec((B, CHUNK, W), lambda n: (0, n, 0))
    return pl.pallas_call(
        body, grid=(N,),
        in_specs=[seq, seq, seq, pl.BlockSpec((B, CHUNK, GAB_W), lambda n: (0, n, 0)), _const_spec((8, 128))] + [_ANY] * ns,
        out_specs=[seq, pl.BlockSpec((B, 1, HEADS, HEAD_DIM, HEAD_DIM), lambda n: (0, n, 0, 0, 0)),
                   pl.BlockSpec((B, 1, HEADS, SOLVE_POWERS, CHUNK, CHUNK), lambda n: (0, n, 0, 0, 0, 0)),
                   pl.BlockSpec((B, 1, HEADS, CHUNK, 2 * HEAD_DIM), lambda n: (0, n, 0, 0, 0))] + [_ANY] * ns,
        out_shape=[_sds((B, S, W), F32), _sds((B, N, HEADS, HEAD_DIM, HEAD_DIM), F32),
                   _sds((B, N, HEADS, SOLVE_POWERS, CHUNK, CHUNK), F32), _sds((B, N, HEADS, CHUNK, 2 * HEAD_DIM), F32)]
                  + [_sds((4,) + s.shape, s.dtype) for s in shards],
        scratch_shapes=[pltpu.VMEM((B * HEADS, HEAD_DIM, HEAD_DIM), F32)] + _gather_scratch(ns),
        compiler_params=_params(("arbitrary",)), name="gdn_chunk_fwd",
    )(q, k, v, gab, scal, *shards)


def _gdn_chunk_bwd(q, k, v, gab, scal, states, powers, solutions, do, partials):
    B, S, W = q.shape
    N = S // CHUNK
    ns = len(partials)

    def body(*refs):
        q_ref, k_ref, v_ref, gab_ref, sc_ref, st_ref, pw_ref, sol_ref, do_ref = refs[:9]
        src_refs = refs[9:9 + ns]
        dq_ref, dk_ref, dv_ref, dgab_ref, dsc_ref = refs[9 + ns:14 + ns]
        dst_refs = refs[14 + ns:14 + 2 * ns]
        dstate_ref, send_sems, recv_sems, local_sems = refs[14 + 2 * ns:]
        n = pl.program_id(0)

        @pl.when(n == 0)
        def _():
            for start in _scatter_copies(src_refs, dst_refs, send_sems, recv_sems, local_sems)[0]:
                start()
            dstate_ref[...] = jnp.zeros_like(dstate_ref)
            dsc_ref[...] = jnp.zeros_like(dsc_ref)

        groups = [(b, h) for b in range(B) for h in range(HEADS)]
        gather = lambda ref: jnp.stack([ref[b, :, h * HEAD_DIM:(h + 1) * HEAD_DIM] for b, h in groups])
        kept_powers = [jnp.stack([pw_ref[b, 0, h, p] for b, h in groups]) for p in range(SOLVE_POWERS)]
        kept_x = jnp.stack([sol_ref[b, 0, h] for b, h in groups])
        solve = lambda lmat, rhs: _unit_lower_solve_kept(lmat, rhs, kept_powers, kept_x)
        _, pull = jax.vjp(_gdn_chunk_fn(len(groups), _chunk_masks(), solve), gather(q_ref), gather(k_ref), gather(v_ref),
                          jnp.stack([gab_ref[b] for b, _ in groups]), sc_ref[0:1, :], sc_ref[1:2, :],
                          jnp.stack([st_ref[b, 0, h] for b, h in groups]))
        dq, dk, dv, dg, d_a, d_dt, dstate = pull((gather(do_ref), dstate_ref[...]))
        for i, (b, h) in enumerate(groups):
            lanes = slice(h * HEAD_DIM, (h + 1) * HEAD_DIM)
            dq_ref[b, :, lanes] = dq[i]
            dk_ref[b, :, lanes] = dk[i]
            dv_ref[b, :, lanes] = dv[i]
        for b in range(B):
            dgab_ref[b] = sum(dg[b * HEADS + h] for h in range(HEADS)).astype(BF16)
        dstate_ref[...] = dstate
        dsc_ref[0:1, :] += d_a
        dsc_ref[1:2, :] += d_dt

        @pl.when(n == N - 1)
        def _():
            for wait in _scatter_copies(src_refs, dst_refs, send_sems, recv_sems, local_sems)[1]:
                wait()

    seq = pl.BlockSpec((B, CHUNK, W), lambda n: (0, N - 1 - n, 0))
    gab_spec = pl.BlockSpec((B, CHUNK, GAB_W), lambda n: (0, N - 1 - n, 0))
    return pl.pallas_call(
        body, grid=(N,),
        in_specs=[seq, seq, seq, gab_spec, _const_spec((8, 128)),
                  pl.BlockSpec((B, 1, HEADS, HEAD_DIM, HEAD_DIM), lambda n: (0, N - 1 - n, 0, 0, 0)),
                  pl.BlockSpec((B, 1, HEADS, SOLVE_POWERS, CHUNK, CHUNK), lambda n: (0, N - 1 - n, 0, 0, 0, 0)),
                  pl.BlockSpec((B, 1, HEADS, CHUNK, 2 * HEAD_DIM), lambda n: (0, N - 1 - n, 0, 0, 0)), seq] + [_ANY] * ns,
        out_specs=[seq, seq, seq, gab_spec, _const_spec((8, 128))] + [_ANY] * ns,
        out_shape=[_sds((B, S, W), F32)] * 3 + [_sds((B, S, GAB_W), BF16), _sds((8, 128), F32)] + [_scattered_shape(p) for p in partials],
        scratch_shapes=[pltpu.VMEM((B * HEADS, HEAD_DIM, HEAD_DIM), F32)] + _scatter_scratch(ns),
        compiler_params=_params(("arbitrary",)), name="gdn_chunk_bwd",
    )(q, k, v, gab, scal, states, powers, solutions, do, *partials)


def _mix_fn(ao, go, gz, w_mla, w_gdn):
    return tuple(_rms(ao[h], w_mla[h]) for h in range(HEADS)) + tuple(_rms(go[h], w_gdn) * _silu(gz[h]) for h in range(HEADS))


def _mix_operands(ao_ref, go_ref, gz_ref, nw_ref):
    blocks = lambda ref: [ref[:, h * HEAD_DIM:(h + 1) * HEAD_DIM] for h in range(HEADS)]
    return blocks(ao_ref), blocks(go_ref), blocks(gz_ref), [nw_ref[h:h + 1, :] for h in range(HEADS)], nw_ref[HEADS:HEADS + 1, :]


def _mix_fwd(ao, go, gz, nw, w_out, x2):
    T, D = x2.shape
    tm = min(TOKEN_TILE, T)
    MW = 2 * HEADS * HEAD_DIM

    def body(ao_ref, go_ref, gz_ref, nw_ref, w_ref, x_ref, mix_ref, h_ref):
        outs = _mix_fn(*_mix_operands(ao_ref, go_ref, gz_ref, nw_ref))
        for i, piece in enumerate(outs):
            mix_ref[:, i * HEAD_DIM:(i + 1) * HEAD_DIM] = piece.astype(BF16)
        h_ref[...] = x_ref[...] + jnp.dot(mix_ref[...], w_ref[...], preferred_element_type=F32)

    half = HEADS * HEAD_DIM
    return pl.pallas_call(
        body, grid=(T // tm,),
        in_specs=[_row_spec(tm, half), _row_spec(tm, half), _row_spec(tm, half), _const_spec((8, 128)), _const_spec((MW, D)),
                  _row_spec(tm, D)],
        out_specs=[_row_spec(tm, MW), _row_spec(tm, D)],
        out_shape=[_sds((T, MW), BF16), _sds((T, D), F32)],
        compiler_params=_params(("parallel",)), name="mix_fwd",
    )(ao, go, gz, nw, w_out, x2)


def _mix_bwd(ao, go, gz, nw, w_out, dh):
    T, D = dh.shape
    tm = min(TOKEN_TILE, T)
    MW = 2 * HEADS * HEAD_DIM
    half = HEADS * HEAD_DIM

    def body(ao_ref, go_ref, gz_ref, nw_ref, w_ref, dh_ref, dao_ref, dgo_ref, dgz_ref, dnw_ref):
        @pl.when(pl.program_id(0) == 0)
        def _():
            dnw_ref[...] = jnp.zeros_like(dnw_ref)

        d_mix = _dg(dh_ref[...].astype(BF16), w_ref[...], 1, 1, None)
        cts = tuple(d_mix[:, i * HEAD_DIM:(i + 1) * HEAD_DIM] for i in range(2 * HEADS))
        _, pull = jax.vjp(_mix_fn, *_mix_operands(ao_ref, go_ref, gz_ref, nw_ref))
        d_ao, d_go, d_gz, d_wm, d_wg = pull(cts)
        for h in range(HEADS):
            lanes = slice(h * HEAD_DIM, (h + 1) * HEAD_DIM)
            dao_ref[:, lanes] = d_ao[h]
            dgo_ref[:, lanes] = d_go[h]
            dgz_ref[:, lanes] = d_gz[h].astype(BF16)
            dnw_ref[h:h + 1, :] += d_wm[h]
        dnw_ref[HEADS:HEADS + 1, :] += d_wg

    return pl.pallas_call(
        body, grid=(T // tm,),
        in_specs=[_row_spec(tm, half), _row_spec(tm, half), _row_spec(tm, half), _const_spec((8, 128)), _const_spec((MW, D)),
                  _row_spec(tm, D)],
        out_specs=[_row_spec(tm, half)] * 3 + [_const_spec((8, 128))],
        out_shape=[_sds((T, half), F32)] * 2 + [_sds((T, half), BF16), _sds((8, 128), F32)],
        compiler_params=_params(("arbitrary",)), name="mix_bwd",
    )(ao, go, gz, nw, w_out, dh)


def _up_spec(w_up, tf):
    per_shard = w_up.shape[2] // tf
    return pl.BlockSpec((None, w_up.shape[1], tf), lambda i, j: (j // per_shard, 0, j % per_shard))


def _mlp_fwd(h2, w_mn, w_up, w_down, target):
    T, D = h2.shape
    FF = w_down.shape[0]
    tm, tf = min(MLP_TOKEN_TILE, T), min(FF_TILE, w_up.shape[2])
    nf = FF // tf

    def body(h_ref, wn_ref, wu_ref, wd_ref, t_ref, hn_ref, act_ref, dy_ref, sq_ref, acc_ref):
        j = pl.program_id(1)

        @pl.when(j == 0)
        def _():
            hn_ref[...] = _rms(h_ref[...], wn_ref[...]).astype(BF16)
            acc_ref[...] = jnp.zeros_like(acc_ref)

        up = jnp.dot(hn_ref[...], wu_ref[...], preferred_element_type=F32)
        act = jnp.square(jnp.maximum(up, 0.0)).astype(BF16)
        act_ref[...] = act
        acc_ref[...] += jnp.dot(act, wd_ref[...], preferred_element_type=F32)

        @pl.when(j == nf - 1)
        def _():
            err = h_ref[...] + acc_ref[...] - t_ref[...]
            dy_ref[...] = err * (1.0 / D)
            sq_ref[...] = jnp.zeros_like(sq_ref) + jnp.sum(err * err)

    tok = lambda w: pl.BlockSpec((tm, w), lambda i, j: (i, 0))
    return pl.pallas_call(
        body, grid=(T // tm, nf),
        in_specs=[tok(D), _const_spec((1, D)), _up_spec(w_up, tf), pl.BlockSpec((tf, D), lambda i, j: (j, 0)), tok(D)],
        out_specs=[tok(D), pl.BlockSpec((tm, tf), lambda i, j: (i, j)), tok(D), pl.BlockSpec((1, 8, 128), lambda i, j: (i, 0, 0))],
        out_shape=[_sds((T, D), BF16), _sds((T, FF), BF16), _sds((T, D), F32), _sds((T // tm, 8, 128), F32)],
        scratch_shapes=[pltpu.VMEM((tm, D), F32)],
        compiler_params=_params(("parallel", "arbitrary")), name="mlp_fwd",
    )(h2, w_mn, w_up, w_down, target)


def _mlp_bwd(h2, w_mn, act, w_up, w_down, dy):
    T, D = h2.shape
    FF = w_down.shape[0]
    tm, tf = min(MLP_TOKEN_TILE, T), min(FF_TILE, w_up.shape[2])
    nf = FF // tf

    def body(h_ref, wn_ref, act_ref, wu_ref, wd_ref, dy_ref, dh_ref, dup_ref, dwn_ref, acc_ref, dyb_ref):
        i, j = pl.program_id(0), pl.program_id(1)

        @pl.when((i == 0) & (j == 0))
        def _():
            dwn_ref[...] = jnp.zeros_like(dwn_ref)

        @pl.when(j == 0)
        def _():
            acc_ref[...] = jnp.zeros_like(acc_ref)
            dyb_ref[...] = dy_ref[...].astype(BF16)

        r = jnp.sqrt(act_ref[...].astype(F32))
        d_act = _dg(dyb_ref[...], wd_ref[...], 1, 1, None)
        d_up = (d_act * (2.0 * r)).astype(BF16)
        dup_ref[...] = d_up
        acc_ref[...] += _dg(d_up, wu_ref[...], 1, 1, None)

        @pl.when(j == nf - 1)
        def _():
            _, pull = jax.vjp(_rms, h_ref[...], wn_ref[...])
            dh, dwn = pull(acc_ref[...])
            dh_ref[...] = dh + dy_ref[...]
            dwn_ref[...] += dwn

    tok = lambda w: pl.BlockSpec((tm, w), lambda i, j: (i, 0))
    ff = pl.BlockSpec((tm, tf), lambda i, j: (i, j))
    return pl.pallas_call(
        body, grid=(T // tm, nf),
        in_specs=[tok(D), _const_spec((1, D)), ff, _up_spec(w_up, tf), pl.BlockSpec((tf, D), lambda i, j: (j, 0)), tok(D)],
        out_specs=[tok(D), ff, _const_spec((1, D))],
        out_shape=[_sds((T, D), F32), _sds((T, FF), BF16), _sds((1, D), F32)],
        scratch_shapes=[pltpu.VMEM((tm, D), F32), pltpu.VMEM((tm, D), BF16)],
        compiler_params=_params(("arbitrary", "arbitrary")), name="mlp_bwd",
    )(h2, w_mn, act, w_up, w_down, dy)


def _rope_pad(a):
    z = jnp.zeros(a.shape[:-1] + (ROPE_HALF,), a.dtype)
    return jnp.concatenate([a[..., :ROPE_HALF], z, a[..., ROPE_HALF:], z], axis=-1)


def _rope_unpad(a):
    return jnp.concatenate([a[..., :ROPE_HALF], a[..., 2 * ROPE_HALF:3 * ROPE_HALF]], axis=-1)


_G0 = 2 * LORA + ROPE_DIM
W_IN_COLS = _G0 + GQKV_W + GZ_W + 2 * HEADS


def _widen_w_in_t(w_t):
    z = jnp.zeros((ROPE_HALF, w_t.shape[1]), w_t.dtype)
    pad = jnp.zeros((GAB_W - 2 * HEADS, w_t.shape[1]), w_t.dtype)
    return jnp.concatenate([w_t[:2 * LORA + ROPE_HALF], z, w_t[2 * LORA + ROPE_HALF:_G0], z, w_t[_G0:], pad], axis=0)


def _narrow_w_in_t(w_t):
    return jnp.concatenate([w_t[:2 * LORA + ROPE_HALF], w_t[2 * LORA + 2 * ROPE_HALF:2 * LORA + 3 * ROPE_HALF],
                            w_t[LAT_W:LAT_W + W_IN_COLS - _G0]], axis=0)


def _stack_mla(w_uq, w_ukv):
    uq = w_uq.reshape(LORA, HEADS, QK_DIM)
    ukv = w_ukv.reshape(LORA, HEADS, 2 * HEAD_DIM)
    parts = [uq[:, :, :HEAD_DIM], _rope_pad(uq[:, :, HEAD_DIM:]), ukv[:, :, :HEAD_DIM], ukv[:, :, HEAD_DIM:]]
    return jnp.concatenate([p.transpose(1, 0, 2) for p in parts], axis=0)


def _unstack_mla(w):
    p = [w[i * HEADS:(i + 1) * HEADS].transpose(1, 0, 2) for i in range(4)]
    uq = jnp.concatenate([p[0], _rope_unpad(p[1])], axis=-1).reshape(LORA, HEADS * QK_DIM)
    ukv = jnp.concatenate([p[2], p[3]], axis=-1).reshape(LORA, HEADS * 2 * HEAD_DIM)
    return uq, ukv


def _rows8(rows):
    a = jnp.concatenate(rows, axis=0)
    return jnp.pad(a, ((0, 8 - a.shape[0]), (0, 0)))


def _qk_norm_rows(q_norm_w, k_norm_w):
    return _rows8([q_norm_w[:, :HEAD_DIM], _rope_pad(q_norm_w[:, HEAD_DIM:]), k_norm_w[:, :HEAD_DIM], _rope_pad(k_norm_w[:, HEAD_DIM:])])


def _rope_rows():
    inv_freq = ROPE_THETA ** (-jnp.arange(ROPE_HALF, dtype=F32) / ROPE_HALF)
    z = jnp.zeros((ROPE_HALF,), F32)
    freq = jnp.concatenate([inv_freq, z, inv_freq, z])
    sign = jnp.concatenate([-jnp.ones((ROPE_HALF,), F32), z, jnp.ones((ROPE_HALF,), F32), z])
    return _rows8([freq[None], sign[None]])


def _column_shards(a):
    return a.reshape(a.shape[0], 4, a.shape[1] // 4).transpose(1, 0, 2)


def _from_column_shards(a):
    return a.transpose(1, 0, 2).reshape(a.shape[1], 4 * a.shape[2])


_ANY = pl.BlockSpec(memory_space=pl.ANY)
_OTHER_CHIPS = ((1, 0), (0, 1), (1, 1))


def _here():
    return lax.axis_index("x"), lax.axis_index("y"), lax.axis_index("c")


def _flip(v, bit):
    return 1 - v if bit else v


def _remote(src, dst, send_sems, recv_sems, k, to):
    return pltpu.make_async_remote_copy(src_ref=src, dst_ref=dst, send_sem=send_sems.at[k], recv_sem=recv_sems.at[k],
                                        device_id=to, device_id_type=MESH)


def _half_of(ref, k, shape):
    r, c = shape
    if (r // 2) % 16 == 0:
        return ref.at[pl.ds(pl.multiple_of(k * (r // 2), 16), r // 2)]
    if (c // 2) % 128 == 0:
        return ref.at[:, pl.ds(pl.multiple_of(k * (c // 2), 128), c // 2)]
    return None


def _gather_copies(srcs, dsts, send_sems, recv_sems, local_sems):
    x, y, c = _here()
    slot, sibling, n = 2 * x + y, (x, y, 1 - c), len(srcs)
    starts, passes, waits = [], [], []
    for i, (src, dst) in enumerate(zip(srcs, dsts)):
        own = pltpu.make_async_copy(src, dst.at[slot], local_sems.at[i])
        starts.append(own.start)
        waits.append(own.wait)
        halves = _half_of(src, c, src.shape) is not None
        for j, (fx, fy) in enumerate(_OTHER_CHIPS):
            cx, cy = _flip(x, fx), _flip(y, fy)
            there = dst.at[2 * cx + cy]
            if halves:
                push = _remote(_half_of(src, c, src.shape), _half_of(dst.at[slot], c, src.shape), send_sems, recv_sems, 3 * i + j, (cx, cy, c))
                landed, other = _half_of(there, c, src.shape), _half_of(there, 1 - c, src.shape)
                onward = _remote(landed, landed, send_sems, recv_sems, 3 * n + 3 * i + j, sibling)
                passes += [_remote(landed, landed, send_sems, recv_sems, 3 * i + j, (cx, cy, c)).wait_recv, onward.start]
                waits += [_remote(other, other, send_sems, recv_sems, 3 * n + 3 * i + j, sibling).wait_recv, onward.wait_send]
            else:
                push = _remote(src, dst.at[slot], send_sems, recv_sems, 3 * i + j, (cx, cy, c))
                waits.append(_remote(there, there, send_sems, recv_sems, 3 * i + j, (cx, cy, c)).wait_recv)
            starts.append(push.start)
            waits.append(push.wait_send)
    return starts, passes, waits


def _gather_scratch(n):
    return [pltpu.SemaphoreType.DMA((6 * n,)), pltpu.SemaphoreType.DMA((6 * n,)), pltpu.SemaphoreType.DMA((n,))]


def _all_gather(shards, name):
    ns = len(shards)

    def body(*refs):
        starts, passes, waits = _gather_copies(refs[:ns], refs[ns:2 * ns], *refs[2 * ns:])
        for call in starts + passes + waits:
            call()

    return pl.pallas_call(
        body, in_specs=[_ANY] * ns, out_specs=[_ANY] * ns, out_shape=[_sds((4,) + s.shape, s.dtype) for s in shards],
        scratch_shapes=_gather_scratch(ns), name=name,
    )(*shards)


def _by_lanes(shape):
    return (shape[-2] // 2) % 16 != 0


def _scattered_shape(p):
    r, c = p.shape[1:]
    return _sds((8, r, c // 2) if _by_lanes(p.shape) else (8, r // 2, c), p.dtype)


def _scatter_copies(srcs, dsts, send_sems, recv_sems, local_sems, whole=0):
    x, y, c = _here()
    me = 4 * x + 2 * y + c
    starts, waits = [], []
    for i, (src, dst) in enumerate(zip(srcs, dsts)):
        def piece(px, py, pc, src=src, entire=i >= len(srcs) - whole):
            if entire:
                return src
            if _by_lanes(src.shape):
                half = src.shape[2] // 2
                return src.at[2 * px + py, :, pl.ds(pl.multiple_of(pc * half, 128), half)]
            half = src.shape[1] // 2
            return src.at[2 * px + py, pl.ds(pl.multiple_of(pc * half, 16), half)]

        own = pltpu.make_async_copy(piece(x, y, c), dst.at[me], local_sems.at[i])
        starts.append(own.start)
        waits.append(own.wait)
        for k in range(1, 8):
            px, py, pc = _flip(x, k & 4), _flip(y, k & 2), _flip(c, k & 1)
            push = _remote(piece(px, py, pc), dst.at[me], send_sems, recv_sems, 7 * i + k - 1, (px, py, pc))
            landed = dst.at[4 * px + 2 * py + pc]
            starts.append(push.start)
            waits += [_remote(landed, landed, send_sems, recv_sems, 7 * i + k - 1, (px, py, pc)).wait_recv, push.wait_send]
    return starts, waits


def _scatter_scratch(n):
    return [pltpu.SemaphoreType.DMA((7 * n,)), pltpu.SemaphoreType.DMA((7 * n,)), pltpu.SemaphoreType.DMA((n,))]


def _swapped_shape(half):
    r, c = half.shape
    return _sds((r, 2 * c) if _by_lanes((r, 2 * c)) else (2, r, c), half.dtype)


def _swap_copies(srcs, dsts, send_sems=None, recv_sems=None, local_sems=None):
    if not srcs:
        return [], []
    x, y, c = _here()
    sibling = (x, y, 1 - c)
    starts, waits = [], []
    for i, (src, dst) in enumerate(zip(srcs, dsts)):
        if len(dst.shape) == 2:
            lanes = src.shape[1]
            mine, other = (dst.at[:, pl.ds(pl.multiple_of(k * lanes, 128), lanes)] for k in (c, 1 - c))
        else:
            mine, other = dst.at[c], dst.at[1 - c]
        own = pltpu.make_async_copy(src, mine, local_sems.at[i])
        push = _remote(src, mine, send_sems, recv_sems, i, sibling)
        starts += [own.start, push.start]
        waits += [_remote(other, other, send_sems, recv_sems, i, sibling).wait_recv, push.wait_send, own.wait]
    return starts, waits


def _swap_scratch(n):
    return [pltpu.SemaphoreType.DMA((n,)), pltpu.SemaphoreType.DMA((n,)), pltpu.SemaphoreType.DMA((n,))] if n else []


def _exchange_halves(halves, wholes):
    ns, nw = len(halves), len(wholes)

    def body(*refs):
        srcs, dsts = refs[:ns + nw], refs[ns + nw:2 * (ns + nw)]
        sems = refs[2 * (ns + nw):]
        starts, waits = _swap_copies(srcs[:ns], dsts[:ns], *sems[:3])
        more = _scatter_copies(srcs[ns:], dsts[ns:], *sems[3:], whole=nw)
        for call in starts + more[0] + waits + more[1]:
            call()

    return pl.pallas_call(
        body, in_specs=[_ANY] * (ns + nw), out_specs=[_ANY] * (ns + nw),
        out_shape=[_swapped_shape(h) for h in halves] + [_sds((8,) + a.shape, a.dtype) for a in wholes],
        scratch_shapes=_swap_scratch(ns) + _scatter_scratch(nw), name="exchange_halves",
    )(*halves, *wholes)


def _row_tile(rows, row_bytes, budget):
    tr = rows
    while tr * row_bytes > budget and tr % 16 == 0:
        tr //= 2
    return tr


def _sum_slots(parts, name):
    _, rows, cols = parts.shape
    tr = _row_tile(rows, 8 * cols * 4, 2 * 1024 * 1024)

    def body(p_ref, o_ref):
        acc = p_ref[0].astype(F32)
        for d in range(1, 8):
            acc = acc + p_ref[d].astype(F32)
        o_ref[...] = acc

    return pl.pallas_call(
        body, grid=(rows // tr,), in_specs=[pl.BlockSpec((8, tr, cols), lambda i: (0, i, 0))],
        out_specs=pl.BlockSpec((tr, cols), lambda i: (i, 0)), out_shape=_sds((rows, cols), F32),
        compiler_params=_params(("parallel",)), name=name,
    )(parts)


def _adam_update(w, g, m, v):
    m = ADAM_B1 * m + (1.0 - ADAM_B1) * g
    v = ADAM_B2 * v + (1.0 - ADAM_B2) * jnp.square(g)
    m_hat = m / (1.0 - ADAM_B1 ** ADAM_STEP)
    v_hat = v / (1.0 - ADAM_B2 ** ADAM_STEP)
    return -ADAM_LR * (m_hat / (jnp.sqrt(v_hat) + ADAM_EPS) + ADAM_WD * w), m, v


SMALL_ROWS = {"attn_norm_w": 0, "mlp_norm_w": 1, "q_lat_norm_w": 2, "kv_lat_norm_w": 3, "q_norm_w": 4, "k_norm_w": 5,
              "mla_out_norm_w": 6, "gdn_norm_w": 10, "a_log": 11, "dt_bias": 12}
LOSS_ROW = 13
SMALL_SHAPE = (16, 1024)


def _pack_small_partials(d_attn_nw, d_mlp_nw, d_ln, d_qk_nw, d_mix_nw, d_scal, conv_parts, sq):
    D = d_attn_nw.shape[1]

    def body(an_ref, mn_ref, ln_ref, qk_ref, mix_ref, sc_ref, cq_ref, ck_ref, cv_ref, sq_ref, a_ref, c_ref):
        a_ref[...] = jnp.zeros_like(a_ref)
        a_ref[0:1, :D] = an_ref[...]
        a_ref[1:2, :D] = mn_ref[...]
        a_ref[2:4, :LORA] = ln_ref[...]
        for row, base in ((4, 0), (5, 2)):
            rope = qk_ref[base + 1:base + 2, :]
            a_ref[row:row + 1, :QK_DIM] = jnp.concatenate(
                [qk_ref[base:base + 1, :], rope[:, :ROPE_HALF], rope[:, 2 * ROPE_HALF:3 * ROPE_HALF]], axis=1)
        a_ref[6:6 + HEADS, :HEAD_DIM] = mix_ref[0:HEADS, :]
        a_ref[10:11, :HEAD_DIM] = mix_ref[HEADS:HEADS + 1, :]
        a_ref[11:13, :128] = sc_ref[0:2, :]
        a_ref[LOSS_ROW:LOSS_ROW + 1, :128] = jnp.zeros((1, 128), F32) + jnp.sum(sq_ref[:, 0:1, 0:1]) * (0.5 / D)
        c_ref[...] = jnp.concatenate([cq_ref[...], ck_ref[...], cv_ref[...]], axis=1)

    return pl.pallas_call(
        body, out_shape=[_sds(SMALL_SHAPE, F32), _sds((CONV_TAPS, GQKV_W), F32)], name="pack_small_partials",
    )(d_attn_nw, d_mlp_nw, d_ln, d_qk_nw, d_mix_nw, d_scal, *conv_parts, sq)


def _adamw_small(parts, conv_parts, w, m, v):
    names = tuple(SMALL_ROWS) + ("conv_w",)
    cols = w["conv_w"].shape[2]

    def body(*refs):
        p_ref, c_ref = refs[:2]
        n = len(names)
        w_refs, m_refs, v_refs = (dict(zip(names, refs[2 + k * n:2 + (k + 1) * n])) for k in range(3))
        loss_ref = refs[2 + 3 * n]
        out = [dict(zip(names, refs[3 + (3 + k) * n:3 + (4 + k) * n])) for k in range(4)]
        acc_ref, cacc_ref = refs[3 + 7 * n:]
        acc, cacc = p_ref[0], c_ref[0]
        for d in range(1, 8):
            acc, cacc = acc + p_ref[d], cacc + c_ref[d]
        acc_ref[...] = acc
        cacc_ref[...] = cacc
        loss_ref[...] = acc_ref[LOSS_ROW:LOSS_ROW + 1, 0:1]
        chip = 2 * lax.axis_index("x") + lax.axis_index("y")
        for name in names:
            shape = w_refs[name].shape
            if name == "conv_w":
                g = sum(jnp.where(chip == s, cacc_ref[:, s * cols:(s + 1) * cols], 0.0) for s in range(4))[None]
            else:
                row = SMALL_ROWS[name]
                g = acc_ref[row:row + math.prod(shape[:-1]), 0:shape[-1]].reshape(shape)
            delta, new_m, new_v = _adam_update(w_refs[name][...], g, m_refs[name][...], v_refs[name][...])
            for ref, val in zip((o[name] for o in out), (g, delta, new_m, new_v)):
                ref[...] = val

    ins = [x[n] for x in (w, m, v) for n in names]
    shapes = [_sds(w[n].shape, F32) for n in names]
    outs = pl.pallas_call(
        body, out_shape=[_sds((1, 1), F32)] + shapes * 4,
        scratch_shapes=[pltpu.VMEM(parts.shape[1:], F32), pltpu.VMEM(conv_parts.shape[1:], F32)], name="adamw_small",
    )(parts, conv_parts, *ins)
    n = len(names)
    return (outs[0],) + tuple(dict(zip(names, outs[1 + k * n:1 + (k + 1) * n])) for k in range(4))


def _adamw(w, g, m, v, name):
    rows, cols = w.shape[0], w.shape[-1]
    if w.ndim == 3:
        tr = max(d for d in range(1, rows + 1) if rows % d == 0 and d * 8 * cols * 4 * 14 <= VMEM_LIMIT // 2)
    else:
        tr = _row_tile(rows, 7 * cols * 4, 4 * 1024 * 1024)

    def body(w_ref, g_ref, m_ref, v_ref, d_ref, mo_ref, vo_ref):
        d_ref[...], mo_ref[...], vo_ref[...] = _adam_update(w_ref[...], g_ref[...], m_ref[...], v_ref[...])

    block = (tr,) + w.shape[1:]
    spec = pl.BlockSpec(block, lambda i: (i,) + (0,) * (len(block) - 1))
    return pl.pallas_call(
        body, grid=(rows // tr,), in_specs=[spec] * 4, out_specs=[spec] * 3, out_shape=[_sds(w.shape, F32)] * 3,
        compiler_params=_params(("parallel",)), name=name,
    )(w, g, m, v)


def kernel(x, positions, attn_norm_w, w_in, q_lat_norm_w, w_uq, kv_lat_norm_w, w_ukv, q_norm_w, k_norm_w, mla_out_norm_w, conv_w, a_log, dt_bias, gdn_norm_w, w_out, mlp_norm_w, w_up, w_down, loss_target, m_attn_norm_w, m_w_in, m_q_lat_norm_w, m_w_uq, m_kv_lat_norm_w, m_w_ukv, m_q_norm_w, m_k_norm_w, m_mla_out_norm_w, m_conv_w, m_a_log, m_dt_bias, m_gdn_norm_w, m_w_out, m_mlp_norm_w, m_w_up, m_w_down, v_attn_norm_w, v_w_in, v_q_lat_norm_w, v_w_uq, v_kv_lat_norm_w, v_w_ukv, v_q_norm_w, v_k_norm_w, v_mla_out_norm_w, v_conv_w, v_a_log, v_dt_bias, v_gdn_norm_w, v_w_out, v_mlp_norm_w, v_w_up, v_w_down):
    w = dict(zip(WEIGHTS, (attn_norm_w, w_in, q_lat_norm_w, w_uq, kv_lat_norm_w, w_ukv, q_norm_w, k_norm_w, mla_out_norm_w, conv_w,
                           a_log, dt_bias, gdn_norm_w, w_out, mlp_norm_w, w_up, w_down)))
    m = dict(zip(WEIGHTS, (m_attn_norm_w, m_w_in, m_q_lat_norm_w, m_w_uq, m_kv_lat_norm_w, m_w_ukv, m_q_norm_w, m_k_norm_w,
                           m_mla_out_norm_w, m_conv_w, m_a_log, m_dt_bias, m_gdn_norm_w, m_w_out, m_mlp_norm_w, m_w_up, m_w_down)))
    v = dict(zip(WEIGHTS, (v_attn_norm_w, v_w_in, v_q_lat_norm_w, v_w_uq, v_kv_lat_norm_w, v_w_ukv, v_q_norm_w, v_k_norm_w,
                           v_mla_out_norm_w, v_conv_w, v_a_log, v_dt_bias, v_gdn_norm_w, v_w_out, v_mlp_norm_w, v_w_up, v_w_down)))
    B, S, D = x.shape
    T = B * S
    x2, pos, target = x.reshape(T, D), positions.reshape(T, 1), loss_target.reshape(T, D)
    seq = lambda a: a.reshape(B, S, a.shape[-1])
    tok = lambda a: a.reshape(T, a.shape[-1])
    local = {n: w[n][0] for n in SHARDED}

    g_in, g_uq, g_ukv, g_conv = _all_gather([jnp.swapaxes(w_in, 1, 2)[0].astype(BF16), local["w_uq"].astype(BF16),
                                             local["w_ukv"].astype(BF16), local["conv_w"]], "gather_first_weights")
    w_in_p = _widen_w_in_t(g_in.reshape(-1, D))
    w_mla = _stack_mla(_from_column_shards(g_uq), _from_column_shards(g_ukv))
    conv_full = _from_column_shards(g_conv)
    ln_w = jnp.concatenate([q_lat_norm_w, kv_lat_norm_w], axis=0)
    qk_nw = _qk_norm_rows(q_norm_w, k_norm_w)
    rope_rows = _rope_rows()
    scal = _rows8([jnp.pad(a_log, ((0, 0), (0, 128 - HEADS))), jnp.pad(dt_bias, ((0, 0), (0, 128 - HEADS)))])
    mix_nw = _rows8([mla_out_norm_w[0], gdn_norm_w])

    xn, lat, gqkv, gz, gab = _in_proj_fwd(x2, attn_norm_w, w_in_p)
    q, k, v_att = _mla_pre_fwd(lat, pos, ln_w, w_mla, qk_nw, rope_rows)
    ao, lse, g_down = _attn_fwd(seq(q), seq(k), seq(v_att), [local["w_down"].astype(BF16)])
    gq, gk, gv = _gdn_pre_fwd(seq(gqkv), conv_full)
    go, states, powers, solutions, g_out, w_up_b = _gdn_chunk_fwd(gq, gk, gv, seq(gab), scal,
                                                                  [local["w_out"].astype(BF16), local["w_up"].astype(BF16)])
    w_out_b = g_out.reshape(-1, D)
    w_down_b = g_down.reshape(-1, D)
    mix, h2 = _mix_fwd(tok(ao), tok(go), gz, mix_nw, w_out_b, x2)
    hn, act, dy, sq = _mlp_fwd(h2, mlp_norm_w, w_up_b, w_down_b, target)

    dh, d_up, d_mlp_nw = _mlp_bwd(h2, mlp_norm_w, act, w_up_b, w_down_b, dy)
    p_down = _wgrad(act, dy, "wgrad_down").reshape(4, -1, D)
    p_up = _wgrad(hn, d_up, "wgrad_up", column_shards=4)
    d_ao, d_go, d_gz, d_mix_nw = _mix_bwd(tok(ao), tok(go), gz, mix_nw, w_out_b, dh)
    p_out = _wgrad(mix, dh, "wgrad_out").reshape(4, -1, D)
    d_gq, d_gk, d_gv, d_gab, d_scal, s_up, s_out = _gdn_chunk_bwd(gq, gk, gv, seq(gab), scal, states, powers, solutions, seq(d_go),
                                                                  [p_up, p_out])
    early = ("w_up", "w_out", "w_down")
    dxq, dxk, dxv, dcq, dck, dcv, g_up, g_out = _gdn_pre_bwd(seq(gqkv), conv_full, d_gq, d_gk, d_gv,
                                                             [_sum_slots(s_up, "sum_w_up"), _sum_slots(s_out, "sum_w_out")])
    dq, dk, dv, s_down = _attn_bwd(seq(q), seq(k), seq(v_att), ao, lse, seq(d_ao), [p_down])
    d_lat, d_ln, d_w_mla, d_qk_nw = _mla_pre_bwd(lat, pos, ln_w, w_mla, qk_nw, rope_rows, tok(dq), tok(dk), tok(dv), [])
    d_pieces = [d_lat, tok(dxq), tok(dxk), tok(dxv), d_gz, tok(d_gab)]
    p_uq, p_ukv = (_column_shards(a).astype(BF16) for a in _unstack_mla(d_w_mla))
    p_in_wide, (s_uq, s_ukv) = _wgrad_pieces(d_pieces, xn, "wgrad_in", [p_uq, p_ukv])
    p_in = _narrow_w_in_t(p_in_wide).reshape(4, -1, D)
    grad_x2, d_attn_nw, s_in, g_down, g_uq, g_ukv = _in_proj_bwd(
        d_pieces, w_in_p, x2, attn_norm_w, dh, [p_in],
        [_sum_slots(s_down, "sum_w_down"), _sum_slots(s_uq, "sum_w_uq"), _sum_slots(s_ukv, "sum_w_ukv")])
    small_buf, conv_buf = _pack_small_partials(d_attn_nw, d_mlp_nw, d_ln, d_qk_nw, d_mix_nw, d_scal, (dcq, dck, dcv), sq)

    late = ("w_in", "w_uq", "w_ukv")
    g_in, s_small, s_conv = _exchange_halves([_sum_slots(s_in, "sum_w_in")], [small_buf, conv_buf])
    names = early + late
    grad = {n: g.reshape(-1, g.shape[-1]) for n, g in zip(names, [g_up, g_out, g_down, g_in, g_uq, g_ukv])}

    loss, g_small, delta, new_m, new_v = _adamw_small(s_small, s_conv, w, m, v)
    grad.update(g_small)
    for n in names:
        if n == "w_in":
            stored = lambda a: jnp.transpose(a, (2, 0, 1))
            outs = _adamw(stored(w[n]), grad[n][:, None, :], stored(m[n]), stored(v[n]), "adamw_" + n)
            grad[n], delta[n], new_m[n], new_v[n] = (jnp.transpose(a, (1, 2, 0)) for a in (grad[n][:, None, :], *outs))
        else:
            delta[n], new_m[n], new_v[n] = _adamw(local[n], grad[n], m[n][0], v[n][0], "adamw_" + n)
    def in_order(d):
        return [d[n].reshape(w[n].shape) for n in WEIGHTS]

    return (loss.reshape(()), grad_x2.reshape(B, S, D), *in_order(grad), *in_order(delta), *in_order(new_m), *in_order(new_v))
```

```python
import functools
import math

import jax
import jax.numpy as jnp
from jax import lax
from jax.experimental import pallas as pl
from jax.experimental.pallas import tpu as pltpu

F32 = jnp.float32
BF16 = jnp.bfloat16
MESH = pl.DeviceIdType.MESH

EPS = 1e-6
HEADS = 4
HEAD_DIM = 128
ROPE_DIM = 64
ROPE_HALF = 32
QK_DIM = 192
QK_PAD = 256
LORA = 256
CHUNK = 64
SOLVE_POWERS = 5
CONV_TAPS = 4
ROPE_THETA = 10000.0
ATTN_SCALE = QK_DIM ** -0.5

LAT_W = 640
GQKV_W = 3 * HEADS * HEAD_DIM
GZ_W = HEADS * HEAD_DIM
GAB_W = 128
PROJ_SPLITS = ((0, LAT_W), (LAT_W, LAT_W + GQKV_W), (LAT_W + GQKV_W, LAT_W + GQKV_W + GZ_W),
               (LAT_W + GQKV_W + GZ_W, LAT_W + GQKV_W + GZ_W + GAB_W))
PROJ_W = PROJ_SPLITS[-1][1]

ADAM_LR = 0.001
ADAM_B1 = 0.9
ADAM_B2 = 0.999
ADAM_EPS = 1e-08
ADAM_WD = 0.01
ADAM_STEP = 10

TOKEN_TILE = 512
WGRAD_TOKEN_TILE = 1024
MLP_TOKEN_TILE = 512
FF_TILE = 1024
ATTN_TILE = 512
ATTN_HEADS_PER_STEP = 2
WGRAD_OUT_BYTES = 8 * 1024 * 1024
VMEM_LIMIT = 48 * 1024 * 1024

SHARDED = ("w_in", "w_uq", "w_ukv", "conv_w", "w_out", "w_up", "w_down")
WEIGHTS = ("attn_norm_w", "w_in", "q_lat_norm_w", "w_uq", "kv_lat_norm_w", "w_ukv", "q_norm_w", "k_norm_w", "mla_out_norm_w",
           "conv_w", "a_log", "dt_bias", "gdn_norm_w", "w_out", "mlp_norm_w", "w_up", "w_down")


def _sds(shape, dtype):
    return jax.ShapeDtypeStruct(shape, dtype)


def _params(semantics):
    return pltpu.CompilerParams(dimension_semantics=semantics, vmem_limit_bytes=VMEM_LIMIT)


def _block(n):
    for b in (512, 256, 128):
        if n % b == 0:
            return b
    return n


def _dg(a, b, ca, cb, prec):
    lead = a.ndim - 2
    batch = (tuple(range(lead)),) * 2
    return lax.dot_general(a, b, (((ca + lead,), (cb + lead,)), batch), precision=prec, preferred_element_type=F32)


def _split_bf16(a):
    hi = a.astype(BF16)
    return hi, (a - hi.astype(F32)).astype(BF16)


def _dot_bf16(a, b, ca, cb):
    return _dg(a.astype(BF16), b.astype(BF16), ca, cb, None)


def _dot_bf16x3(a, b, ca, cb):
    a_hi, a_lo = _split_bf16(a)
    b_hi, b_lo = _split_bf16(b)
    lead = a.ndim - 2
    return _dg(jnp.concatenate([a_hi, a_hi, a_lo], axis=ca + lead), jnp.concatenate([b_hi, b_lo, b_hi], axis=cb + lead), ca, cb, None)


def _matmul_family(dot):
    def nn_raw(a, b):
        return dot(a, b, 1, 0)

    def nt_raw(a, b):
        return dot(a, b, 1, 1)

    def tn_raw(a, b):
        return dot(a, b, 0, 0)

    @jax.custom_vjp
    def nn(a, b):
        return nn_raw(a, b)

    nn.defvjp(lambda a, b: (nn_raw(a, b), (a, b)), lambda r, g: (nt_raw(g, r[1]), tn_raw(r[0], g)))

    @jax.custom_vjp
    def nt(a, b):
        return nt_raw(a, b)

    nt.defvjp(lambda a, b: (nt_raw(a, b), (a, b)), lambda r, g: (nn_raw(g, r[1]), tn_raw(g, r[0])))

    @jax.custom_vjp
    def tn(a, b):
        return tn_raw(a, b)

    tn.defvjp(lambda a, b: (tn_raw(a, b), (a, b)), lambda r, g: (nt_raw(r[1], g), nn_raw(r[0], g)))
    return nn, nt, tn


_bf_nn, _bf_nt, _bf_tn = _matmul_family(_dot_bf16)
_hi_nn, _hi_nt, _hi_tn = _matmul_family(_dot_bf16x3)


def _lower_powers(lmat):
    powers = []
    while 2 ** (len(powers) + 1) < lmat.shape[-1]:
        powers.append(_dot_bf16x3(powers[-1] if powers else lmat, powers[-1] if powers else lmat, 1, 0))
    return powers


@jax.custom_vjp
def _unit_lower_solve(lmat, rhs):
    return _unit_lower_solve_fwd(lmat, rhs)[0]


def _unit_lower_solve_fwd(lmat, rhs):
    powers = _lower_powers(lmat)
    x = rhs - _dot_bf16x3(lmat, rhs, 1, 0)
    for p in powers:
        x = x + _dot_bf16x3(p, x, 1, 0)
    return x, (lmat, powers, x)


def _unit_lower_solve_bwd(res, g):
    lmat, powers, x = res
    y = g - _dot_bf16x3(lmat, g, 0, 0)
    for p in powers:
        y = y + _dot_bf16x3(p, y, 0, 0)
    return -_dot_bf16x3(y, x, 1, 1), y


_unit_lower_solve.defvjp(_unit_lower_solve_fwd, _unit_lower_solve_bwd)


@jax.custom_vjp
def _unit_lower_solve_kept(lmat, rhs, powers, x):
    return x


_unit_lower_solve_kept.defvjp(
    lambda lmat, rhs, powers, x: (x, (lmat, powers, x)),
    lambda res, g: _unit_lower_solve_bwd(res, g) + ([jnp.zeros_like(p) for p in res[1]], jnp.zeros_like(res[2])))


@jax.custom_vjp
def _lane_halves(x):
    n = x.shape[-1] // 2
    return x[..., :n], x[..., n:]


_lane_halves.defvjp(lambda x: (_lane_halves(x), None), lambda _, g: (jnp.concatenate(g, axis=-1),))


@jax.custom_vjp
def _row_halves(x):
    n = x.shape[-2] // 2
    return x[..., :n, :], x[..., n:, :]


_row_halves.defvjp(lambda x: (_row_halves(x), None), lambda _, g: (jnp.concatenate(g, axis=-2),))


@jax.custom_vjp
def _swap_halves(t):
    return pltpu.roll(t, 64, 1)


_swap_halves.defvjp(lambda t: (pltpu.roll(t, 64, 1), None), lambda _, g: (pltpu.roll(g, 64, 1),))


@functools.partial(jax.custom_vjp, nondiff_argnums=(2,))
def _shift_rows(x, keep, s):
    return pltpu.roll(x, s, 0) * keep


def _shift_rows_fwd(x, keep, s):
    return pltpu.roll(x, s, 0) * keep, keep


def _shift_rows_bwd(s, keep, g):
    return pltpu.roll(g * keep, keep.shape[0] - s, 0), jnp.zeros_like(keep)


_shift_rows.defvjp(_shift_rows_fwd, _shift_rows_bwd)


def _sigmoid(x):
    return 0.5 * jnp.tanh(0.5 * x) + 0.5


def _softplus(x):
    return jnp.maximum(x, 0.0) + jnp.log(1.0 + jnp.exp(jnp.minimum(x, -x)))


def _silu(x):
    return x * _sigmoid(x)


def _rms(x, w, n=None):
    n = x.shape[-1] if n is None else n
    r = lax.rsqrt(jnp.sum(x * x, axis=-1, keepdims=True) * (1.0 / n) + EPS)
    return x * r * w


def _rope(t, cos_f, sin_f):
    return t * cos_f + _swap_halves(t) * sin_f


def _rope_tables(pos_col, freq_row, sign_row):
    ang = pos_col.astype(F32) * freq_row
    return jnp.cos(ang), jnp.sin(ang) * sign_row


def _onehot_row(lane):
    return (lax.broadcasted_iota(jnp.int32, (1, 128), 1) == lane).astype(F32)


def _row_spec(tm, w):
    return pl.BlockSpec((tm, w), lambda i: (i, 0))


def _const_spec(shape):
    return pl.BlockSpec(shape, lambda *_: (0,) * len(shape))


def _in_proj_fwd(x2, w_an, w_in_p):
    T, D = x2.shape
    tm = min(TOKEN_TILE, T)

    def body(x_ref, wn_ref, w_ref, xn_ref, lat_ref, gqkv_ref, gz_ref, gab_ref):
        x = x_ref[...]
        r = lax.rsqrt(jnp.mean(x * x, axis=-1, keepdims=True) + EPS)
        xn = (x * r * wn_ref[...]).astype(BF16)
        xn_ref[...] = xn
        for ref, (a, b) in zip((lat_ref, gqkv_ref, gz_ref, gab_ref), PROJ_SPLITS):
            ref[...] = _dg(xn, w_ref[a:b, :], 1, 1, None)

    widths = [b - a for a, b in PROJ_SPLITS]
    return pl.pallas_call(
        body, grid=(T // tm,),
        in_specs=[_row_spec(tm, D), _const_spec((1, D)), _const_spec((PROJ_W, D))],
        out_specs=[_row_spec(tm, D)] + [_row_spec(tm, w) for w in widths],
        out_shape=[_sds((T, D), BF16)] + [_sds((T, w), F32) for w in widths],
        compiler_params=_params(("parallel",)), name="in_proj_fwd",
    )(x2, w_an, w_in_p)


def _in_proj_bwd(pieces, w_in_p, x2, w_an, dh, partials, halves):
    T, D = x2.shape
    tm = min(TOKEN_TILE, T)
    widths = [p.shape[1] for p in pieces]
    starts = [sum(widths[:i]) for i in range(len(widths))]
    assert sum(widths) == PROJ_W
    npc, ns, nh = len(pieces), len(partials), len(halves)

    def body(*refs):
        piece_refs = refs[:npc]
        w_ref, x_ref, wn_ref, dh_ref = refs[npc:npc + 4]
        src_refs = refs[npc + 4:npc + 4 + ns + nh]
        dx_ref, dwn_ref = refs[npc + 4 + ns + nh:npc + 6 + ns + nh]
        dst_refs = refs[npc + 6 + ns + nh:npc + 6 + 2 * (ns + nh)]
        sems = refs[npc + 6 + 2 * (ns + nh):]

        def riders():
            scatter = _scatter_copies(src_refs[:ns], dst_refs[:ns], *sems[:3])
            swap = _swap_copies(src_refs[ns:], dst_refs[ns:], *sems[3:])
            return scatter[0] + swap[0], scatter[1] + swap[1]

        @pl.when(pl.program_id(0) == 0)
        def _():
            for start in riders()[0]:
                start()
            dwn_ref[...] = jnp.zeros_like(dwn_ref)

        dxn = jnp.zeros((tm, D), F32)
        for ref, a, width in zip(piece_refs, starts, widths):
            dxn += _dg(ref[...], w_ref[a:a + width, :], 1, 0, None)
        _, pull = jax.vjp(_rms, x_ref[...], wn_ref[...])
        dx, dwn = pull(dxn)
        dx_ref[...] = dx + dh_ref[...]
        dwn_ref[...] += dwn

        @pl.when(pl.program_id(0) == T // tm - 1)
        def _():
            for wait in riders()[1]:
                wait()

    return pl.pallas_call(
        body, grid=(T // tm,),
        in_specs=[_row_spec(tm, w) for w in widths] + [_const_spec((PROJ_W, D)), _row_spec(tm, D), _const_spec((1, D)),
                                                       _row_spec(tm, D)] + [_ANY] * (ns + nh),
        out_specs=[_row_spec(tm, D), _const_spec((1, D))] + [_ANY] * (ns + nh),
        out_shape=[_sds((T, D), F32), _sds((1, D), F32)] + [_scattered_shape(p) for p in partials]
                  + [_swapped_shape(h) for h in halves],
        scratch_shapes=_scatter_scratch(ns) + _swap_scratch(nh),
        compiler_params=_params(("arbitrary",)), name="in_proj_bwd",
    )(*pieces, w_in_p, x2, w_an, dh, *partials, *halves)


def _wgrad_pieces(pieces, b, name, partials):
    T, k2 = b.shape
    tt = min(WGRAD_TOKEN_TILE, T)
    widths = [p.shape[1] for p in pieces]
    starts = [sum(widths[:i]) for i in range(len(widths))]
    k1 = sum(widths)
    npc, ns = len(pieces), len(partials)

    def body(*refs):
        piece_refs, b_ref = refs[:npc], refs[npc]
        src_refs = refs[npc + 1:npc + 1 + ns]
        o_ref = refs[npc + 1 + ns]
        dst_refs = refs[npc + 2 + ns:npc + 2 + 2 * ns]
        acc_ref, *sems = refs[npc + 2 + 2 * ns:]
        t = pl.program_id(0)

        @pl.when(t == 0)
        def _():
            for start in _scatter_copies(src_refs, dst_refs, *sems)[0]:
                start()
            acc_ref[...] = jnp.zeros_like(acc_ref)

        bt = b_ref[...].astype(BF16)
        for ref, r0, width in zip(piece_refs, starts, widths):
            acc_ref[r0:r0 + width, :] += jnp.dot(ref[...].T, bt, preferred_element_type=F32)

        @pl.when(t == T // tt - 1)
        def _():
            o_ref[...] = acc_ref[...].astype(o_ref.dtype)
            for wait in _scatter_copies(src_refs, dst_refs, *sems)[1]:
                wait()

    out, *scattered = pl.pallas_call(
        body, grid=(T // tt,),
        in_specs=[pl.BlockSpec((tt, w), lambda t: (t, 0)) for w in widths] + [pl.BlockSpec((tt, k2), lambda t: (t, 0))] + [_ANY] * ns,
        out_specs=[_const_spec((k1, k2))] + [_ANY] * ns,
        out_shape=[_sds((k1, k2), BF16)] + [_scattered_shape(p) for p in partials],
        scratch_shapes=[pltpu.VMEM((k1, k2), F32)] + _scatter_scratch(ns),
        compiler_params=_params(("arbitrary",)), name=name,
    )(*pieces, b, *partials)
    return out, scattered


def _wgrad(a, b, name, column_shards=1, out_dtype=BF16):
    T, k1 = a.shape
    k2 = b.shape[1]
    per_shard = k2 // column_shards
    tt = min(WGRAD_TOKEN_TILE, T)
    b1 = k1
    while b1 * k2 * 4 > WGRAD_OUT_BYTES and b1 % 256 == 0:
        b1 //= 2
    step = _block(per_shard)

    def body(a_ref, b_ref, o_ref, acc_ref):
        t = pl.program_id(1)

        @pl.when(t == 0)
        def _():
            acc_ref[...] = jnp.zeros_like(acc_ref)

        a_t = a_ref[...].astype(BF16).T
        for c0 in range(0, k2, step):
            part = jnp.dot(a_t, b_ref[:, c0:c0 + step].astype(BF16), preferred_element_type=F32)
            if column_shards == 1:
                acc_ref[:, c0:c0 + step] += part
            else:
                acc_ref[c0 // per_shard, :, c0 % per_shard:c0 % per_shard + step] += part

        @pl.when(t == T // tt - 1)
        def _():
            o_ref[...] = acc_ref[...].astype(o_ref.dtype)

    if column_shards == 1:
        block, out_spec, out_shape = (b1, k2), pl.BlockSpec((b1, k2), lambda i, t: (i, 0)), _sds((k1, k2), out_dtype)
    else:
        block = (column_shards, b1, per_shard)
        out_spec, out_shape = pl.BlockSpec(block, lambda i, t: (0, i, 0)), _sds((column_shards, k1, per_shard), out_dtype)
    return pl.pallas_call(
        body, grid=(k1 // b1, T // tt),
        in_specs=[pl.BlockSpec((tt, b1), lambda i, t: (t, i)), pl.BlockSpec((tt, k2), lambda i, t: (t, 0))],
        out_specs=out_spec, out_shape=out_shape, scratch_shapes=[pltpu.VMEM(block, F32)],
        compiler_params=_params(("parallel", "arbitrary")), name=name,
    )(a, b)


@jax.custom_vjp
def _lane_blocks(x):
    return tuple(x[:, i:i + 128] for i in range(0, x.shape[1], 128))


_lane_blocks.defvjp(lambda x: (_lane_blocks(x), None), lambda _, g: (jnp.concatenate(g, axis=1),))


def _mla_pre_fn(q_lat, kv_lat, kpe, ln_q, ln_kv, w_q, w_kv, qn_n, qn_p, kn_n, kn_p, cos_f, sin_f):
    qn = _rms(q_lat, ln_q)
    kvn = _rms(kv_lat, ln_kv)
    kp = _rope(_rms(kpe, kn_p, ROPE_DIM), cos_f, sin_f)
    q_blocks = _lane_blocks(_bf_nn(qn, w_q))
    kv_blocks = _lane_blocks(_bf_nn(kvn, w_kv))
    outs = []
    for h in range(HEADS):
        outs.append(_rms(q_blocks[h], qn_n))
        outs.append(_rope(_rms(q_blocks[HEADS + h], qn_p, ROPE_DIM), cos_f, sin_f))
        outs.append(_rms(kv_blocks[h], kn_n))
        outs.append(kv_blocks[HEADS + h])
    return tuple(outs) + (kp,)


def _mla_pre_operands(lat_ref, pos_ref, ln_ref, w_ref, nw_ref, rope_ref):
    cos_f, sin_f = _rope_tables(pos_ref[...], rope_ref[0:1, :], rope_ref[1:2, :])
    side_by_side = lambda blocks: jnp.concatenate([w_ref[i].astype(F32) for i in blocks], axis=1)
    diff = (lat_ref[:, 0:LORA], lat_ref[:, LORA:2 * LORA], lat_ref[:, 2 * LORA:LAT_W], ln_ref[0:1, :], ln_ref[1:2, :],
            side_by_side(range(2 * HEADS)), side_by_side(range(2 * HEADS, 4 * HEADS)),
            nw_ref[0:1, :], nw_ref[1:2, :], nw_ref[2:3, :], nw_ref[3:4, :])
    return diff, cos_f, sin_f


def _mla_pre_fwd(lat, pos, ln_w, w_mla, nw, rope_rows):
    T = lat.shape[0]
    tm = min(TOKEN_TILE, T)

    def body(lat_ref, pos_ref, ln_ref, w_ref, nw_ref, rope_ref, q_ref, k_ref, v_ref):
        diff, cos_f, sin_f = _mla_pre_operands(lat_ref, pos_ref, ln_ref, w_ref, nw_ref, rope_ref)
        outs = _mla_pre_fn(*diff, cos_f, sin_f)
        kp = outs[-1].astype(BF16)
        for h in range(HEADS):
            q_n, q_p, k_n, v = outs[4 * h:4 * h + 4]
            q_ref[:, h * QK_PAD:h * QK_PAD + HEAD_DIM] = q_n.astype(BF16)
            q_ref[:, h * QK_PAD + HEAD_DIM:(h + 1) * QK_PAD] = q_p.astype(BF16)
            k_ref[:, h * QK_PAD:h * QK_PAD + HEAD_DIM] = k_n.astype(BF16)
            k_ref[:, h * QK_PAD + HEAD_DIM:(h + 1) * QK_PAD] = kp
            v_ref[:, h * HEAD_DIM:(h + 1) * HEAD_DIM] = v.astype(BF16)

    return pl.pallas_call(
        body, grid=(T // tm,),
        in_specs=[_row_spec(tm, LAT_W), _row_spec(tm, 1), _const_spec((2, LORA)), _const_spec((4 * HEADS, LORA, 128)),
                  _const_spec((8, 128)), _const_spec((8, 128))],
        out_specs=[_row_spec(tm, HEADS * QK_PAD), _row_spec(tm, HEADS * QK_PAD), _row_spec(tm, HEADS * HEAD_DIM)],
        out_shape=[_sds((T, HEADS * QK_PAD), BF16), _sds((T, HEADS * QK_PAD), BF16), _sds((T, HEADS * HEAD_DIM), BF16)],
        compiler_params=_params(("parallel",)), name="mla_pre_fwd",
    )(lat, pos, ln_w, w_mla, nw, rope_rows)


def _mla_pre_bwd(lat, pos, ln_w, w_mla, nw, rope_rows, dq, dk, dv, halves):
    T = lat.shape[0]
    tm = min(TOKEN_TILE, T)
    ns = len(halves)

    def body(*refs):
        lat_ref, pos_ref, ln_ref, w_ref, nw_ref, rope_ref, dq_ref, dk_ref, dv_ref = refs[:9]
        src_refs = refs[9:9 + ns]
        dlat_ref, dln_ref, dw_ref, dnw_ref = refs[9 + ns:13 + ns]
        dst_refs = refs[13 + ns:13 + 2 * ns]
        sems = refs[13 + 2 * ns:]

        @pl.when(pl.program_id(0) == 0)
        def _():
            for start in _swap_copies(src_refs, dst_refs, *sems)[0]:
                start()
            dln_ref[...] = jnp.zeros_like(dln_ref)
            dw_ref[...] = jnp.zeros_like(dw_ref)
            dnw_ref[...] = jnp.zeros_like(dnw_ref)

        diff, cos_f, sin_f = _mla_pre_operands(lat_ref, pos_ref, ln_ref, w_ref, nw_ref, rope_ref)
        _, pull = jax.vjp(lambda *a: _mla_pre_fn(*a, cos_f, sin_f), *diff)
        cts = []
        d_kp = jnp.zeros((tm, 128), F32)
        for h in range(HEADS):
            cts.append(dq_ref[:, h * QK_PAD:h * QK_PAD + HEAD_DIM])
            cts.append(dq_ref[:, h * QK_PAD + HEAD_DIM:(h + 1) * QK_PAD])
            cts.append(dk_ref[:, h * QK_PAD:h * QK_PAD + HEAD_DIM])
            cts.append(dv_ref[:, h * HEAD_DIM:(h + 1) * HEAD_DIM])
            d_kp += dk_ref[:, h * QK_PAD + HEAD_DIM:(h + 1) * QK_PAD]
        d_ql, d_kvl, d_kpe, d_lnq, d_lnkv, d_wq, d_wkv, d_qn_n, d_qn_p, d_kn_n, d_kn_p = pull(tuple(cts) + (d_kp,))
        d_w = [d[:, i:i + 128] for d in (d_wq, d_wkv) for i in range(0, d.shape[1], 128)]
        dlat_ref[:, 0:LORA] = d_ql.astype(BF16)
        dlat_ref[:, LORA:2 * LORA] = d_kvl.astype(BF16)
        dlat_ref[:, 2 * LORA:LAT_W] = d_kpe.astype(BF16)
        dln_ref[0:1, :] += d_lnq
        dln_ref[1:2, :] += d_lnkv
        for i in range(4 * HEADS):
            dw_ref[i] += d_w[i]
        for i, d in enumerate((d_qn_n, d_qn_p, d_kn_n, d_kn_p)):
            dnw_ref[i:i + 1, :] += d

        @pl.when(pl.program_id(0) == T // tm - 1)
        def _():
            for wait in _swap_copies(src_refs, dst_refs, *sems)[1]:
                wait()

    return pl.pallas_call(
        body, grid=(T // tm,),
        in_specs=[_row_spec(tm, LAT_W), _row_spec(tm, 1), _const_spec((2, LORA)), _const_spec((4 * HEADS, LORA, 128)),
                  _const_spec((8, 128)), _const_spec((8, 128)),
                  _row_spec(tm, HEADS * QK_PAD), _row_spec(tm, HEADS * QK_PAD), _row_spec(tm, HEADS * HEAD_DIM)] + [_ANY] * ns,
        out_specs=[_row_spec(tm, LAT_W), _const_spec((2, LORA)), _const_spec((4 * HEADS, LORA, 128)), _const_spec((8, 128))]
                  + [_ANY] * ns,
        out_shape=[_sds((T, LAT_W), BF16), _sds((2, LORA), F32), _sds((4 * HEADS, LORA, 128), F32), _sds((8, 128), F32)]
                  + [_swapped_shape(h) for h in halves],
        scratch_shapes=_swap_scratch(ns),
        compiler_params=_params(("arbitrary",)), name="mla_pre_bwd",
    )(lat, pos, ln_w, w_mla, nw, rope_rows, dq, dk, dv, *halves)


def _causal_mask(i, j, tq, tk):
    row = i * tq + lax.broadcasted_iota(jnp.int32, (tq, tk), 0)
    col = j * tk + lax.broadcasted_iota(jnp.int32, (tq, tk), 1)
    return col <= row


def _attn_fwd(q, k, v, shards):
    B, S, _ = q.shape
    t = min(ATTN_TILE, S)
    nq = S // t
    ns = len(shards)

    hp = ATTN_HEADS_PER_STEP
    qk = lambda h: slice(h * QK_PAD, (h + 1) * QK_PAD)
    vd = lambda h: slice(h * HEAD_DIM, (h + 1) * HEAD_DIM)

    def body(*refs):
        q_ref, k_ref, v_ref = refs[:3]
        src_refs = refs[3:3 + ns]
        o_ref, lse_ref = refs[3 + ns:5 + ns]
        dst_refs = refs[5 + ns:5 + 2 * ns]
        sems = refs[5 + 2 * ns:]
        b, g, i = pl.program_id(0), pl.program_id(1), pl.program_id(2)
        qb = [q_ref[0, :, qk(h)] for h in range(hp)]

        step_no = (b * (HEADS // hp) + g) * nq + i
        for phase, at in enumerate((0, (3 * B * (HEADS // hp) * nq) // 4)):
            @pl.when(step_no == at)
            def _(phase=phase):
                for call in _gather_copies(src_refs, dst_refs, *sems)[phase]:
                    call()

        def step(j, carry, diagonal):
            rows = pl.ds(pl.multiple_of(j * t, t), t)
            s = [_dg(qb[h], k_ref[0, rows, qk(h)], 1, 1, None) * ATTN_SCALE for h in range(hp)]
            if diagonal:
                keep = _causal_mask(0, 0, t, t)
                s = [jnp.where(keep, x, -1e30) for x in s]
            m_new = [jnp.maximum(carry[h][0], jnp.max(s[h], axis=-1, keepdims=True)) for h in range(hp)]
            p = [jnp.exp(s[h] - m_new[h]) for h in range(hp)]
            alpha = [jnp.exp(carry[h][0] - m_new[h]) for h in range(hp)]
            l = [alpha[h] * carry[h][1] + jnp.sum(p[h], axis=-1, keepdims=True) for h in range(hp)]
            pv = [jnp.dot(p[h].astype(BF16), v_ref[0, rows, vd(h)], preferred_element_type=F32) for h in range(hp)]
            return tuple((m_new[h], l[h], alpha[h] * carry[h][2] + pv[h]) for h in range(hp))

        init = tuple((jnp.full((t, 1), -1e30, F32), jnp.zeros((t, 1), F32), jnp.zeros((t, HEAD_DIM), F32)) for _ in range(hp))
        below = lax.fori_loop(0, i, lambda j, carry: step(j, carry, False), init)
        for h, (m, l, acc) in enumerate(step(i, below, True)):
            o_ref[0, :, vd(h)] = acc / l
            lse_ref[0, h, 0] = (m + jnp.log(l)).T

        @pl.when((b == B - 1) & (g == HEADS // hp - 1) & (i == nq - 1))
        def _():
            for wait in _gather_copies(src_refs, dst_refs, *sems)[2]:
                wait()

    return pl.pallas_call(
        body, grid=(B, HEADS // hp, nq),
        in_specs=[pl.BlockSpec((1, t, hp * QK_PAD), lambda b, g, i: (b, i, g)),
                  pl.BlockSpec((1, S, hp * QK_PAD), lambda b, g, i: (b, 0, g)),
                  pl.BlockSpec((1, S, hp * HEAD_DIM), lambda b, g, i: (b, 0, g))] + [_ANY] * ns,
        out_specs=[pl.BlockSpec((1, t, hp * HEAD_DIM), lambda b, g, i: (b, i, g)),
                   pl.BlockSpec((1, hp, 1, 1, t), lambda b, g, i: (b, g, i, 0, 0))] + [_ANY] * ns,
        out_shape=[_sds((B, S, HEADS * HEAD_DIM), F32), _sds((B, HEADS, nq, 1, t), F32)] + [_sds((4,) + s.shape, s.dtype) for s in shards],
        scratch_shapes=_gather_scratch(ns),
        compiler_params=_params(("arbitrary", "arbitrary", "arbitrary")), name="attn_fwd",
    )(q, k, v, *shards)


def _attn_bwd(q, k, v, o, lse, do, partials):
    B, S, _ = q.shape
    t = min(ATTN_TILE, S)
    nq = S // t
    ns = len(partials)

    hp = ATTN_HEADS_PER_STEP
    qk = lambda h: slice(h * QK_PAD, (h + 1) * QK_PAD)
    vd = lambda h: slice(h * HEAD_DIM, (h + 1) * HEAD_DIM)
    heads = range(hp)

    def body(*refs):
        q_ref, k_ref, v_ref, o_ref, lse_ref, do_ref = refs[:6]
        src_refs = refs[6:6 + ns]
        dq_ref, dk_ref, dv_ref = refs[6 + ns:9 + ns]
        dst_refs = refs[9 + ns:9 + 2 * ns]
        dsum_ref, send_sems, recv_sems, local_sems = refs[9 + 2 * ns:]
        b, g, j = pl.program_id(0), pl.program_id(1), pl.program_id(2)

        @pl.when((b == 0) & (g == 0) & (j == 0))
        def _():
            for start in _scatter_copies(src_refs, dst_refs, send_sems, recv_sems, local_sems)[0]:
                start()

        @pl.when(j == 0)
        def _():
            dq_ref[...] = jnp.zeros_like(dq_ref)
            for h in heads:
                for blk in range(nq):
                    rows = slice(blk * t, (blk + 1) * t)
                    dsum_ref[h, blk] = jnp.sum(do_ref[0, rows, vd(h)] * o_ref[0, rows, vd(h)], axis=-1, keepdims=True).T

        kb = [k_ref[0, :, qk(h)] for h in heads]
        vb = [v_ref[0, :, vd(h)] for h in heads]

        def step(i, carry, diagonal):
            rows = pl.ds(pl.multiple_of(i * t, t), t)
            qb = [q_ref[0, rows, qk(h)] for h in heads]
            dob = [do_ref[0, rows, vd(h)].astype(BF16) for h in heads]
            s = [_dg(kb[h], qb[h], 1, 1, None) * ATTN_SCALE for h in heads]
            p = [jnp.exp(s[h] - lse_ref[0, h, i]) for h in heads]
            if diagonal:
                key = lax.broadcasted_iota(jnp.int32, (t, t), 0)
                query = lax.broadcasted_iota(jnp.int32, (t, t), 1)
                p = [jnp.where(key <= query, x, 0.0) for x in p]
            dp = [_dg(vb[h], dob[h], 1, 1, None) for h in heads]
            dv = [carry[h][1] + jnp.dot(p[h].astype(BF16), dob[h], preferred_element_type=F32) for h in heads]
            ds = [(p[h] * (dp[h] - dsum_ref[h, i]) * ATTN_SCALE).astype(BF16) for h in heads]
            for h in heads:
                dq_ref[0, rows, qk(h)] += _dg(ds[h], kb[h], 0, 0, None)
            return tuple((carry[h][0] + jnp.dot(ds[h], qb[h], preferred_element_type=F32), dv[h]) for h in heads)

        zeros = tuple((jnp.zeros((t, QK_PAD), F32), jnp.zeros((t, HEAD_DIM), F32)) for _ in heads)
        on_diagonal = step(j, zeros, True)
        done = lax.fori_loop(j + 1, nq, lambda i, carry: step(i, carry, False), on_diagonal)
        for h, (dk, dv) in enumerate(done):
            dk_ref[0, :, qk(h)] = dk
            dv_ref[0, :, vd(h)] = dv

        @pl.when((b == B - 1) & (g == HEADS // hp - 1) & (j == nq - 1))
        def _():
            for wait in _scatter_copies(src_refs, dst_refs, send_sems, recv_sems, local_sems)[1]:
                wait()

    return pl.pallas_call(
        body, grid=(B, HEADS // hp, nq),
        in_specs=[pl.BlockSpec((1, S, hp * QK_PAD), lambda b, g, j: (b, 0, g)),
                  pl.BlockSpec((1, t, hp * QK_PAD), lambda b, g, j: (b, j, g)),
                  pl.BlockSpec((1, t, hp * HEAD_DIM), lambda b, g, j: (b, j, g)),
                  pl.BlockSpec((1, S, hp * HEAD_DIM), lambda b, g, j: (b, 0, g)),
                  pl.BlockSpec((1, hp, nq, 1, t), lambda b, g, j: (b, g, 0, 0, 0)),
                  pl.BlockSpec((1, S, hp * HEAD_DIM), lambda b, g, j: (b, 0, g))] + [_ANY] * ns,
        out_specs=[pl.BlockSpec((1, S, hp * QK_PAD), lambda b, g, j: (b, 0, g)),
                   pl.BlockSpec((1, t, hp * QK_PAD), lambda b, g, j: (b, j, g)),
                   pl.BlockSpec((1, t, hp * HEAD_DIM), lambda b, g, j: (b, j, g))] + [_ANY] * ns,
        out_shape=[_sds((B, S, HEADS * QK_PAD), F32), _sds((B, S, HEADS * QK_PAD), F32), _sds((B, S, HEADS * HEAD_DIM), F32)]
                  + [_scattered_shape(p) for p in partials],
        scratch_shapes=[pltpu.VMEM((hp, nq, 1, t), F32)] + _scatter_scratch(ns),
        compiler_params=_params(("arbitrary", "arbitrary", "arbitrary")), name="attn_bwd",
    )(q, k, v, o, lse, do, *partials)


def _gdn_pre_fn(xq, xk, xv, wq, wk, wv, keeps):
    def conv_silu(x, w):
        acc = x * w[3]
        for s in (1, 2, 3):
            acc = acc + _shift_rows(x, keeps[s - 1], s) * w[3 - s]
        return _silu(acc)

    def l2(x):
        return x * lax.rsqrt(jnp.sum(x * x, axis=-1, keepdims=True) + EPS)

    return l2(conv_silu(xq, wq)) * (HEAD_DIM ** -0.5), l2(conv_silu(xk, wk)), conv_silu(xv, wv)


def _gdn_pre_specs(S):
    x_specs = [pl.BlockSpec((1, S, HEAD_DIM), lambda h, b, g=g: (b, 0, g * HEADS + h)) for g in range(3)]
    w_specs = [pl.BlockSpec((CONV_TAPS, HEAD_DIM), lambda h, b, g=g: (0, g * HEADS + h)) for g in range(3)]
    out_spec = pl.BlockSpec((1, S, HEAD_DIM), lambda h, b: (b, 0, h))
    return x_specs, w_specs, out_spec


def _row_keeps(S):
    t = lax.broadcasted_iota(jnp.int32, (S, HEAD_DIM), 0)
    return [(t >= s).astype(F32) for s in (1, 2, 3)]


def _gdn_pre_fwd(gqkv, conv_w):
    B, S, _ = gqkv.shape
    x_specs, w_specs, out_spec = _gdn_pre_specs(S)

    def body(xq_ref, xk_ref, xv_ref, wq_ref, wk_ref, wv_ref, q_ref, k_ref, v_ref):
        taps = [[w[i:i + 1, :] for i in range(CONV_TAPS)] for w in (wq_ref, wk_ref, wv_ref)]
        q, k, v = _gdn_pre_fn(xq_ref[0], xk_ref[0], xv_ref[0], *taps, _row_keeps(S))
        q_ref[0], k_ref[0], v_ref[0] = q, k, v

    return pl.pallas_call(
        body, grid=(HEADS, B), in_specs=x_specs + w_specs, out_specs=[out_spec] * 3,
        out_shape=[_sds((B, S, HEADS * HEAD_DIM), F32)] * 3,
        compiler_params=_params(("parallel", "parallel")), name="gdn_pre_fwd",
    )(gqkv, gqkv, gqkv, conv_w, conv_w, conv_w)


def _gdn_pre_bwd(gqkv, conv_w, dq, dk, dv, halves):
    B, S, _ = gqkv.shape
    x_specs, w_specs, out_spec = _gdn_pre_specs(S)
    dw_spec = pl.BlockSpec((CONV_TAPS, HEAD_DIM), lambda h, b: (0, h))
    ns = len(halves)

    def body(*refs):
        xq_ref, xk_ref, xv_ref, wq_ref, wk_ref, wv_ref, dq_ref, dk_ref, dv_ref = refs[:9]
        src_refs = refs[9:9 + ns]
        dxq_ref, dxk_ref, dxv_ref, dwq_ref, dwk_ref, dwv_ref = refs[9 + ns:15 + ns]
        dst_refs = refs[15 + ns:15 + 2 * ns]
        sems = refs[15 + 2 * ns:]
        first = (pl.program_id(0) == 0) & (pl.program_id(1) == 0)
        last = (pl.program_id(0) == HEADS - 1) & (pl.program_id(1) == B - 1)

        @pl.when(first)
        def _():
            for start in _swap_copies(src_refs, dst_refs, *sems)[0]:
                start()

        @pl.when(pl.program_id(1) == 0)
        def _():
            for r in (dwq_ref, dwk_ref, dwv_ref):
                r[...] = jnp.zeros_like(r)

        taps = [[w[i:i + 1, :] for i in range(CONV_TAPS)] for w in (wq_ref, wk_ref, wv_ref)]
        keeps = _row_keeps(S)
        _, pull = jax.vjp(lambda *a: _gdn_pre_fn(*a, keeps), xq_ref[0], xk_ref[0], xv_ref[0], *taps)
        dxq, dxk, dxv, dwq, dwk, dwv = pull((dq_ref[0], dk_ref[0], dv_ref[0]))
        dxq_ref[0], dxk_ref[0], dxv_ref[0] = dxq.astype(BF16), dxk.astype(BF16), dxv.astype(BF16)
        for ref, dw in ((dwq_ref, dwq), (dwk_ref, dwk), (dwv_ref, dwv)):
            for i in range(CONV_TAPS):
                ref[i:i + 1, :] += dw[i]

        @pl.when(last)
        def _():
            for wait in _swap_copies(src_refs, dst_refs, *sems)[1]:
                wait()

    hw = HEADS * HEAD_DIM
    return pl.pallas_call(
        body, grid=(HEADS, B), in_specs=x_specs + w_specs + [out_spec] * 3 + [_ANY] * ns,
        out_specs=[out_spec] * 3 + [dw_spec] * 3 + [_ANY] * ns,
        out_shape=[_sds((B, S, hw), BF16)] * 3 + [_sds((CONV_TAPS, hw), F32)] * 3 + [_swapped_shape(h) for h in halves],
        scratch_shapes=_swap_scratch(ns),
        compiler_params=_params(("arbitrary", "arbitrary")), name="gdn_pre_bwd",
    )(gqkv, gqkv, gqkv, conv_w, conv_w, conv_w, dq, dk, dv, *halves)


def _chunk_masks():
    i = lax.broadcasted_iota(jnp.int32, (CHUNK, CHUNK), 0)
    j = lax.broadcasted_iota(jnp.int32, (CHUNK, CHUNK), 1)
    lower, after = (j <= i).astype(F32), (j > i).astype(F32)
    return {"le": lower, "le_gt": jnp.concatenate([lower, after], axis=0), "strict": (j < i).astype(F32)}


def _gdn_chunk_fn(groups, masks, solve=_unit_lower_solve):
    lane = lax.broadcasted_iota(jnp.int32, (groups, 1, 128), 2)
    head = lax.broadcasted_iota(jnp.int32, (groups, 1, 128), 0) % HEADS
    pick_a, pick_b = (lane == head).astype(F32), (lane == head + HEADS).astype(F32)
    lower, lower_after, strict = (jnp.broadcast_to(masks[n], (groups,) + masks[n].shape) for n in ("le", "le_gt", "strict"))
    ones_row = jnp.ones((1, 1, HEAD_DIM), F32)

    def f(q, k, v, gab, a_row, dt_row, state):
        ga = jnp.sum(gab * pick_a, axis=2, keepdims=True)
        gb = jnp.sum(gab * pick_b, axis=2, keepdims=True)
        a_log = jnp.sum(a_row * pick_a, axis=2, keepdims=True)
        dt_bias = jnp.sum(dt_row * pick_a, axis=2, keepdims=True)
        beta = _sigmoid(gb)
        g = -jnp.exp(a_log) * _softplus(ga + dt_bias)
        g_wide = g * ones_row
        cum, rest = _row_halves(_hi_nn(lower_after, g_wide))
        total = jnp.sum(g_wide, axis=1, keepdims=True)
        diff = _hi_nn(lower, g * strict)
        decay = lower * jnp.exp(diff)
        e_cum = jnp.exp(cum)
        kk, qk = _row_halves(_bf_nt(jnp.concatenate([k, q], axis=1), k))
        lmat = strict * (beta * kk * decay)
        u, w = _lane_halves(solve(lmat, jnp.concatenate([v * beta, k * (beta * e_cum)], axis=2)))
        w_state, q_state = _row_halves(_bf_nn(jnp.concatenate([w, q * e_cum], axis=1), state))
        v_new = u - w_state
        o = q_state + _bf_nn(qk * decay, v_new)
        new_state = state * jnp.exp(total) + _bf_tn(k * jnp.exp(rest), v_new)
        return o, new_state

    return f


def _gdn_chunk_fwd(q, k, v, gab, scal, shards):
    B, S, W = q.shape
    N = S // CHUNK
    ns = len(shards)

    def body(*refs):
        q_ref, k_ref, v_ref, gab_ref, sc_ref = refs[:5]
        src_refs = refs[5:5 + ns]
        o_ref, st_ref, pw_ref, sol_ref = refs[5 + ns:9 + ns]
        dst_refs = refs[9 + ns:9 + 2 * ns]
        state_ref, send_sems, recv_sems, local_sems = refs[9 + 2 * ns:]
        n = pl.program_id(0)
        kept = {}

        @pl.when(n == 0)
        def _():
            for start in _gather_copies(src_refs, dst_refs, send_sems, recv_sems, local_sems)[0]:
                start()
            state_ref[...] = jnp.zeros_like(state_ref)

        @pl.when(n == (2 * N) // 3)
        def _():
            for pass_on in _gather_copies(src_refs, dst_refs, send_sems, recv_sems, local_sems)[1]:
                pass_on()

        groups = [(b, h) for b in range(B) for h in range(HEADS)]
        gather = lambda ref: jnp.stack([ref[b, :, h * HEAD_DIM:(h + 1) * HEAD_DIM] for b, h in groups])
        state = state_ref[...]
        for i, (b, h) in enumerate(groups):
            st_ref[b, 0, h] = state[i]
        def solve_and_keep(lmat, rhs):
            kept["x"], (_, kept["powers"], _) = _unit_lower_solve_fwd(lmat, rhs)
            return kept["x"]

        o, new_state = _gdn_chunk_fn(len(groups), _chunk_masks(), solve_and_keep)(
            gather(q_ref), gather(k_ref), gather(v_ref), jnp.stack([gab_ref[b] for b, _ in groups]), sc_ref[0:1, :], sc_ref[1:2, :], state)
        for i, (b, h) in enumerate(groups):
            o_ref[b, :, h * HEAD_DIM:(h + 1) * HEAD_DIM] = o[i]
            sol_ref[b, 0, h] = kept["x"][i]
            for p, power in enumerate(kept["powers"]):
                pw_ref[b, 0, h, p] = power[i]
        state_ref[...] = new_state

        @pl.when(n == N - 1)
        def _():
            for wait in _gather_copies(src_refs, dst_refs, send_sems, recv_sems, local_sems)[2]:
                wait()

    seq = pl.BlockSpec((B, CHUNK, W), lambda n: (0, n, 0))
    return pl.pallas_call(
        body, grid=(N,),
        in_specs=[seq, seq, seq, pl.BlockSpec((B, CHUNK, GAB_W), lambda n: (0, n, 0)), _const_spec((8, 128))] + [_ANY] * ns,
        out_specs=[seq, pl.BlockSpec((B, 1, HEADS, HEAD_DIM, HEAD_DIM), lambda n: (0, n, 0, 0, 0)),
                   pl.BlockSpec((B, 1, HEADS, SOLVE_POWERS, CHUNK, CHUNK), lambda n: (0, n, 0, 0, 0, 0)),
                   pl.BlockSpec((B, 1, HEADS, CHUNK, 2 * HEAD_DIM), lambda n: (0, n, 0, 0, 0))] + [_ANY] * ns,
        out_shape=[_sds((B, S, W), F32), _sds((B, N, HEADS, HEAD_DIM, HEAD_DIM), F32),
                   _sds((B, N, HEADS, SOLVE_POWERS, CHUNK, CHUNK), F32), _sds((B, N, HEADS, CHUNK, 2 * HEAD_DIM), F32)]
                  + [_sds((4,) + s.shape, s.dtype) for s in shards],
        scratch_shapes=[pltpu.VMEM((B * HEADS, HEAD_DIM, HEAD_DIM), F32)] + _gather_scratch(ns),
        compiler_params=_params(("arbitrary",)), name="gdn_chunk_fwd",
    )(q, k, v, gab, scal, *shards)


def _gdn_chunk_bwd(q, k, v, gab, scal, states, powers, solutions, do, partials):
    B, S, W = q.shape
    N = S // CHUNK
    ns = len(partials)

    def body(*refs):
        q_ref, k_ref, v_ref, gab_ref, sc_ref, st_ref, pw_ref, sol_ref, do_ref = refs[:9]
        src_refs = refs[9:9 + ns]
        dq_ref, dk_ref, dv_ref, dgab_ref, dsc_ref = refs[9 + ns:14 + ns]
        dst_refs = refs[14 + ns:14 + 2 * ns]
        dstate_ref, send_sems, recv_sems, local_sems = refs[14 + 2 * ns:]
        n = pl.program_id(0)

        @pl.when(n == 0)
        def _():
            for start in _scatter_copies(src_refs, dst_refs, send_sems, recv_sems, local_sems)[0]:
                start()
            dstate_ref[...] = jnp.zeros_like(dstate_ref)
            dsc_ref[...] = jnp.zeros_like(dsc_ref)

        groups = [(b, h) for b in range(B) for h in range(HEADS)]
        gather = lambda ref: jnp.stack([ref[b, :, h * HEAD_DIM:(h + 1) * HEAD_DIM] for b, h in groups])
        kept_powers = [jnp.stack([pw_ref[b, 0, h, p] for b, h in groups]) for p in range(SOLVE_POWERS)]
        kept_x = jnp.stack([sol_ref[b, 0, h] for b, h in groups])
        solve = lambda lmat, rhs: _unit_lower_solve_kept(lmat, rhs, kept_powers, kept_x)
        _, pull = jax.vjp(_gdn_chunk_fn(len(groups), _chunk_masks(), solve), gather(q_ref), gather(k_ref), gather(v_ref),
                          jnp.stack([gab_ref[b] for b, _ in groups]), sc_ref[0:1, :], sc_ref[1:2, :],
                          jnp.stack([st_ref[b, 0, h] for b, h in groups]))
        dq, dk, dv, dg, d_a, d_dt, dstate = pull((gather(do_ref), dstate_ref[...]))
        for i, (b, h) in enumerate(groups):
            lanes = slice(h * HEAD_DIM, (h + 1) * HEAD_DIM)
            dq_ref[b, :, lanes] = dq[i]
            dk_ref[b, :, lanes] = dk[i]
            dv_ref[b, :, lanes] = dv[i]
        for b in range(B):
            dgab_ref[b] = sum(dg[b * HEADS + h] for h in range(HEADS)).astype(BF16)
        dstate_ref[...] = dstate
        dsc_ref[0:1, :] += d_a
        dsc_ref[1:2, :] += d_dt

        @pl.when(n == N - 1)
        def _():
            for wait in _scatter_copies(src_refs, dst_refs, send_sems, recv_sems, local_sems)[1]:
                wait()

    seq = pl.BlockSpec((B, CHUNK, W), lambda n: (0, N - 1 - n, 0))
    gab_spec = pl.BlockSpec((B, CHUNK, GAB_W), lambda n: (0, N - 1 - n, 0))
    return pl.pallas_call(
        body, grid=(N,),
        in_specs=[seq, seq, seq, gab_spec, _const_spec((8, 128)),
                  pl.BlockSpec((B, 1, HEADS, HEAD_DIM, HEAD_DIM), lambda n: (0, N - 1 - n, 0, 0, 0)),
                  pl.BlockSpec((B, 1, HEADS, SOLVE_POWERS, CHUNK, CHUNK), lambda n: (0, N - 1 - n, 0, 0, 0, 0)),
                  pl.BlockSpec((B, 1, HEADS, CHUNK, 2 * HEAD_DIM), lambda n: (0, N - 1 - n, 0, 0, 0)), seq] + [_ANY] * ns,
        out_specs=[seq, seq, seq, gab_spec, _const_spec((8, 128))] + [_ANY] * ns,
        out_shape=[_sds((B, S, W), F32)] * 3 + [_sds((B, S, GAB_W), BF16), _sds((8, 128), F32)] + [_scattered_shape(p) for p in partials],
        scratch_shapes=[pltpu.VMEM((B * HEADS, HEAD_DIM, HEAD_DIM), F32)] + _scatter_scratch(ns),
        compiler_params=_params(("arbitrary",)), name="gdn_chunk_bwd",
    )(q, k, v, gab, scal, states, powers, solutions, do, *partials)


def _mix_fn(ao, go, gz, w_mla, w_gdn):
    return tuple(_rms(ao[h], w_mla[h]) for h in range(HEADS)) + tuple(_rms(go[h], w_gdn) * _silu(gz[h]) for h in range(HEADS))


def _mix_operands(ao_ref, go_ref, gz_ref, nw_ref):
    blocks = lambda ref: [ref[:, h * HEAD_DIM:(h + 1) * HEAD_DIM] for h in range(HEADS)]
    return blocks(ao_ref), blocks(go_ref), blocks(gz_ref), [nw_ref[h:h + 1, :] for h in range(HEADS)], nw_ref[HEADS:HEADS + 1, :]


def _mix_fwd(ao, go, gz, nw, w_out, x2):
    T, D = x2.shape
    tm = min(TOKEN_TILE, T)
    MW = 2 * HEADS * HEAD_DIM

    def body(ao_ref, go_ref, gz_ref, nw_ref, w_ref, x_ref, mix_ref, h_ref):
        outs = _mix_fn(*_mix_operands(ao_ref, go_ref, gz_ref, nw_ref))
        for i, piece in enumerate(outs):
            mix_ref[:, i * HEAD_DIM:(i + 1) * HEAD_DIM] = piece.astype(BF16)
        h_ref[...] = x_ref[...] + jnp.dot(mix_ref[...], w_ref[...], preferred_element_type=F32)

    half = HEADS * HEAD_DIM
    return pl.pallas_call(
        body, grid=(T // tm,),
        in_specs=[_row_spec(tm, half), _row_spec(tm, half), _row_spec(tm, half), _const_spec((8, 128)), _const_spec((MW, D)),
                  _row_spec(tm, D)],
        out_specs=[_row_spec(tm, MW), _row_spec(tm, D)],
        out_shape=[_sds((T, MW), BF16), _sds((T, D), F32)],
        compiler_params=_params(("parallel",)), name="mix_fwd",
    )(ao, go, gz, nw, w_out, x2)


def _mix_bwd(ao, go, gz, nw, w_out, dh):
    T, D = dh.shape
    tm = min(TOKEN_TILE, T)
    MW = 2 * HEADS * HEAD_DIM
    half = HEADS * HEAD_DIM

    def body(ao_ref, go_ref, gz_ref, nw_ref, w_ref, dh_ref, dao_ref, dgo_ref, dgz_ref, dnw_ref):
        @pl.when(pl.program_id(0) == 0)
        def _():
            dnw_ref[...] = jnp.zeros_like(dnw_ref)

        d_mix = _dg(dh_ref[...].astype(BF16), w_ref[...], 1, 1, None)
        cts = tuple(d_mix[:, i * HEAD_DIM:(i + 1) * HEAD_DIM] for i in range(2 * HEADS))
        _, pull = jax.vjp(_mix_fn, *_mix_operands(ao_ref, go_ref, gz_ref, nw_ref))
        d_ao, d_go, d_gz, d_wm, d_wg = pull(cts)
        for h in range(HEADS):
            lanes = slice(h * HEAD_DIM, (h + 1) * HEAD_DIM)
            dao_ref[:, lanes] = d_ao[h]
            dgo_ref[:, lanes] = d_go[h]
            dgz_ref[:, lanes] = d_gz[h].astype(BF16)
            dnw_ref[h:h + 1, :] += d_wm[h]
        dnw_ref[HEADS:HEADS + 1, :] += d_wg

    return pl.pallas_call(
        body, grid=(T // tm,),
        in_specs=[_row_spec(tm, half), _row_spec(tm, half), _row_spec(tm, half), _const_spec((8, 128)), _const_spec((MW, D)),
                  _row_spec(tm, D)],
        out_specs=[_row_spec(tm, half)] * 3 + [_const_spec((8, 128))],
        out_shape=[_sds((T, half), F32)] * 2 + [_sds((T, half), BF16), _sds((8, 128), F32)],
        compiler_params=_params(("arbitrary",)), name="mix_bwd",
    )(ao, go, gz, nw, w_out, dh)


def _up_spec(w_up, tf):
    per_shard = w_up.shape[2] // tf
    return pl.BlockSpec((None, w_up.shape[1], tf), lambda i, j: (j // per_shard, 0, j % per_shard))


def _mlp_fwd(h2, w_mn, w_up, w_down, target):
    T, D = h2.shape
    FF = w_down.shape[0]
    tm, tf = min(MLP_TOKEN_TILE, T), min(FF_TILE, w_up.shape[2])
    nf = FF // tf

    def body(h_ref, wn_ref, wu_ref, wd_ref, t_ref, hn_ref, act_ref, dy_ref, sq_ref, acc_ref):
        j = pl.program_id(1)

        @pl.when(j == 0)
        def _():
            hn_ref[...] = _rms(h_ref[...], wn_ref[...]).astype(BF16)
            acc_ref[...] = jnp.zeros_like(acc_ref)

        up = jnp.dot(hn_ref[...], wu_ref[...], preferred_element_type=F32)
        act = jnp.square(jnp.maximum(up, 0.0)).astype(BF16)
        act_ref[...] = act
        acc_ref[...] += jnp.dot(act, wd_ref[...], preferred_element_type=F32)

        @pl.when(j == nf - 1)
        def _():
            err = h_ref[...] + acc_ref[...] - t_ref[...]
            dy_ref[...] = err * (1.0 / D)
            sq_ref[...] = jnp.zeros_like(sq_ref) + jnp.sum(err * err)

    tok = lambda w: pl.BlockSpec((tm, w), lambda i, j: (i, 0))
    return pl.pallas_call(
        body, grid=(T // tm, nf),
        in_specs=[tok(D), _const_spec((1, D)), _up_spec(w_up, tf), pl.BlockSpec((tf, D), lambda i, j: (j, 0)), tok(D)],
        out_specs=[tok(D), pl.BlockSpec((tm, tf), lambda i, j: (i, j)), tok(D), pl.BlockSpec((1, 8, 128), lambda i, j: (i, 0, 0))],
        out_shape=[_sds((T, D), BF16), _sds((T, FF), BF16), _sds((T, D), F32), _sds((T // tm, 8, 128), F32)],
        scratch_shapes=[pltpu.VMEM((tm, D), F32)],
        compiler_params=_params(("parallel", "arbitrary")), name="mlp_fwd",
    )(h2, w_mn, w_up, w_down, target)


def _mlp_bwd(h2, w_mn, act, w_up, w_down, dy):
    T, D = h2.shape
    FF = w_down.shape[0]
    tm, tf = min(MLP_TOKEN_TILE, T), min(FF_TILE, w_up.shape[2])
    nf = FF // tf

    def body(h_ref, wn_ref, act_ref, wu_ref, wd_ref, dy_ref, dh_ref, dup_ref, dwn_ref, acc_ref, dyb_ref):
        i, j = pl.program_id(0), pl.program_id(1)

        @pl.when((i == 0) & (j == 0))
        def _():
            dwn_ref[...] = jnp.zeros_like(dwn_ref)

        @pl.when(j == 0)
        def _():
            acc_ref[...] = jnp.zeros_like(acc_ref)
            dyb_ref[...] = dy_ref[...].astype(BF16)

        r = jnp.sqrt(act_ref[...].astype(F32))
        d_act = _dg(dyb_ref[...], wd_ref[...], 1, 1, None)
        d_up = (d_act * (2.0 * r)).astype(BF16)
        dup_ref[...] = d_up
        acc_ref[...] += _dg(d_up, wu_ref[...], 1, 1, None)

        @pl.when(j == nf - 1)
        def _():
            _, pull = jax.vjp(_rms, h_ref[...], wn_ref[...])
            dh, dwn = pull(acc_ref[...])
            dh_ref[...] = dh + dy_ref[...]
            dwn_ref[...] += dwn

    tok = lambda w: pl.BlockSpec((tm, w), lambda i, j: (i, 0))
    ff = pl.BlockSpec((tm, tf), lambda i, j: (i, j))
    return pl.pallas_call(
        body, grid=(T // tm, nf),
        in_specs=[tok(D), _const_spec((1, D)), ff, _up_spec(w_up, tf), pl.BlockSpec((tf, D), lambda i, j: (j, 0)), tok(D)],
        out_specs=[tok(D), ff, _const_spec((1, D))],
        out_shape=[_sds((T, D), F32), _sds((T, FF), BF16), _sds((1, D), F32)],
        scratch_shapes=[pltpu.VMEM((tm, D), F32), pltpu.VMEM((tm, D), BF16)],
        compiler_params=_params(("arbitrary", "arbitrary")), name="mlp_bwd",
    )(h2, w_mn, act, w_up, w_down, dy)


def _rope_pad(a):
    z = jnp.zeros(a.shape[:-1] + (ROPE_HALF,), a.dtype)
    return jnp.concatenate([a[..., :ROPE_HALF], z, a[..., ROPE_HALF:], z], axis=-1)


def _rope_unpad(a):
    return jnp.concatenate([a[..., :ROPE_HALF], a[..., 2 * ROPE_HALF:3 * ROPE_HALF]], axis=-1)


_G0 = 2 * LORA + ROPE_DIM
W_IN_COLS = _G0 + GQKV_W + GZ_W + 2 * HEADS


def _widen_w_in_t(w_t):
    z = jnp.zeros((ROPE_HALF, w_t.shape[1]), w_t.dtype)
    pad = jnp.zeros((GAB_W - 2 * HEADS, w_t.shape[1]), w_t.dtype)
    return jnp.concatenate([w_t[:2 * LORA + ROPE_HALF], z, w_t[2 * LORA + ROPE_HALF:_G0], z, w_t[_G0:], pad], axis=0)


def _narrow_w_in_t(w_t):
    return jnp.concatenate([w_t[:2 * LORA + ROPE_HALF], w_t[2 * LORA + 2 * ROPE_HALF:2 * LORA + 3 * ROPE_HALF],
                            w_t[LAT_W:LAT_W + W_IN_COLS - _G0]], axis=0)


def _stack_mla(w_uq, w_ukv):
    uq = w_uq.reshape(LORA, HEADS, QK_DIM)
    ukv = w_ukv.reshape(LORA, HEADS, 2 * HEAD_DIM)
    parts = [uq[:, :, :HEAD_DIM], _rope_pad(uq[:, :, HEAD_DIM:]), ukv[:, :, :HEAD_DIM], ukv[:, :, HEAD_DIM:]]
    return jnp.concatenate([p.transpose(1, 0, 2) for p in parts], axis=0)


def _unstack_mla(w):
    p = [w[i * HEADS:(i + 1) * HEADS].transpose(1, 0, 2) for i in range(4)]
    uq = jnp.concatenate([p[0], _rope_unpad(p[1])], axis=-1).reshape(LORA, HEADS * QK_DIM)
    ukv = jnp.concatenate([p[2], p[3]], axis=-1).reshape(LORA, HEADS * 2 * HEAD_DIM)
    return uq, ukv


def _rows8(rows):
    a = jnp.concatenate(rows, axis=0)
    return jnp.pad(a, ((0, 8 - a.shape[0]), (0, 0)))


def _qk_norm_rows(q_norm_w, k_norm_w):
    return _rows8([q_norm_w[:, :HEAD_DIM], _rope_pad(q_norm_w[:, HEAD_DIM:]), k_norm_w[:, :HEAD_DIM], _rope_pad(k_norm_w[:, HEAD_DIM:])])


def _rope_rows():
    inv_freq = ROPE_THETA ** (-jnp.arange(ROPE_HALF, dtype=F32) / ROPE_HALF)
    z = jnp.zeros((ROPE_HALF,), F32)
    freq = jnp.concatenate([inv_freq, z, inv_freq, z])
    sign = jnp.concatenate([-jnp.ones((ROPE_HALF,), F32), z, jnp.ones((ROPE_HALF,), F32), z])
    return _rows8([freq[None], sign[None]])


def _column_shards(a):
    return a.reshape(a.shape[0], 4, a.shape[1] // 4).transpose(1, 0, 2)


def _from_column_shards(a):
    return a.transpose(1, 0, 2).reshape(a.shape[1], 4 * a.shape[2])


_ANY = pl.BlockSpec(memory_space=pl.ANY)
_OTHER_CHIPS = ((1, 0), (0, 1), (1, 1))


def _here():
    return lax.axis_index("x"), lax.axis_index("y"), lax.axis_index("c")


def _flip(v, bit):
    return 1 - v if bit else v


def _remote(src, dst, send_sems, recv_sems, k, to):
    return pltpu.make_async_remote_copy(src_ref=src, dst_ref=dst, send_sem=send_sems.at[k], recv_sem=recv_sems.at[k],
                                        device_id=to, device_id_type=MESH)


def _half_of(ref, k, shape):
    r, c = shape
    if (r // 2) % 16 == 0:
        return ref.at[pl.ds(pl.multiple_of(k * (r // 2), 16), r // 2)]
    if (c // 2) % 128 == 0:
        return ref.at[:, pl.ds(pl.multiple_of(k * (c // 2), 128), c // 2)]
    return None


def _gather_copies(srcs, dsts, send_sems, recv_sems, local_sems):
    x, y, c = _here()
    slot, sibling, n = 2 * x + y, (x, y, 1 - c), len(srcs)
    starts, passes, waits = [], [], []
    for i, (src, dst) in enumerate(zip(srcs, dsts)):
        own = pltpu.make_async_copy(src, dst.at[slot], local_sems.at[i])
        starts.append(own.start)
        waits.append(own.wait)
        halves = _half_of(src, c, src.shape) is not None
        for j, (fx, fy) in enumerate(_OTHER_CHIPS):
            cx, cy = _flip(x, fx), _flip(y, fy)
            there = dst.at[2 * cx + cy]
            if halves:
                push = _remote(_half_of(src, c, src.shape), _half_of(dst.at[slot], c, src.shape), send_sems, recv_sems, 3 * i + j, (cx, cy, c))
                landed, other = _half_of(there, c, src.shape), _half_of(there, 1 - c, src.shape)
                onward = _remote(landed, landed, send_sems, recv_sems, 3 * n + 3 * i + j, sibling)
                passes += [_remote(landed, landed, send_sems, recv_sems, 3 * i + j, (cx, cy, c)).wait_recv, onward.start]
                waits += [_remote(other, other, send_sems, recv_sems, 3 * n + 3 * i + j, sibling).wait_recv, onward.wait_send]
            else:
                push = _remote(src, dst.at[slot], send_sems, recv_sems, 3 * i + j, (cx, cy, c))
                waits.append(_remote(there, there, send_sems, recv_sems, 3 * i + j, (cx, cy, c)).wait_recv)
            starts.append(push.start)
            waits.append(push.wait_send)
    return starts, passes, waits


def _gather_scratch(n):
    return [pltpu.SemaphoreType.DMA((6 * n,)), pltpu.SemaphoreType.DMA((6 * n,)), pltpu.SemaphoreType.DMA((n,))]


def _all_gather(shards, name):
    ns = len(shards)

    def body(*refs):
        starts, passes, waits = _gather_copies(refs[:ns], refs[ns:2 * ns], *refs[2 * ns:])
        for call in starts + passes + waits:
            call()

    return pl.pallas_call(
        body, in_specs=[_ANY] * ns, out_specs=[_ANY] * ns, out_shape=[_sds((4,) + s.shape, s.dtype) for s in shards],
        scratch_shapes=_gather_scratch(ns), name=name,
    )(*shards)


def _by_lanes(shape):
    return (shape[-2] // 2) % 16 != 0


def _scattered_shape(p):
    r, c = p.shape[1:]
    return _sds((8, r, c // 2) if _by_lanes(p.shape) else (8, r // 2, c), p.dtype)


def _scatter_copies(srcs, dsts, send_sems, recv_sems, local_sems, whole=0):
    x, y, c = _here()
    me = 4 * x + 2 * y + c
    starts, waits = [], []
    for i, (src, dst) in enumerate(zip(srcs, dsts)):
        def piece(px, py, pc, src=src, entire=i >= len(srcs) - whole):
            if entire:
                return src
            if _by_lanes(src.shape):
                half = src.shape[2] // 2
                return src.at[2 * px + py, :, pl.ds(pl.multiple_of(pc * half, 128), half)]
            half = src.shape[1] // 2
            return src.at[2 * px + py, pl.ds(pl.multiple_of(pc * half, 16), half)]

        own = pltpu.make_async_copy(piece(x, y, c), dst.at[me], local_sems.at[i])
        starts.append(own.start)
        waits.append(own.wait)
        for k in range(1, 8):
            px, py, pc = _flip(x, k & 4), _flip(y, k & 2), _flip(c, k & 1)
            push = _remote(piece(px, py, pc), dst.at[me], send_sems, recv_sems, 7 * i + k - 1, (px, py, pc))
            landed = dst.at[4 * px + 2 * py + pc]
            starts.append(push.start)
            waits += [_remote(landed, landed, send_sems, recv_sems, 7 * i + k - 1, (px, py, pc)).wait_recv, push.wait_send]
    return starts, waits


def _scatter_scratch(n):
    return [pltpu.SemaphoreType.DMA((7 * n,)), pltpu.SemaphoreType.DMA((7 * n,)), pltpu.SemaphoreType.DMA((n,))]


def _swapped_shape(half):
    r, c = half.shape
    return _sds((r, 2 * c) if _by_lanes((r, 2 * c)) else (2, r, c), half.dtype)


def _swap_copies(srcs, dsts, send_sems=None, recv_sems=None, local_sems=None):
    if not srcs:
        return [], []
    x, y, c = _here()
    sibling = (x, y, 1 - c)
    starts, waits = [], []
    for i, (src, dst) in enumerate(zip(srcs, dsts)):
        if len(dst.shape) == 2:
            lanes = src.shape[1]
            mine, other = (dst.at[:, pl.ds(pl.multiple_of(k * lanes, 128), lanes)] for k in (c, 1 - c))
        else:
            mine, other = dst.at[c], dst.at[1 - c]
        own = pltpu.make_async_copy(src, mine, local_sems.at[i])
        push = _remote(src, mine, send_sems, recv_sems, i, sibling)
        starts += [own.start, push.start]
        waits += [_remote(other, other, send_sems, recv_sems, i, sibling).wait_recv, push.wait_send, own.wait]
    return starts, waits


def _swap_scratch(n):
    return [pltpu.SemaphoreType.DMA((n,)), pltpu.SemaphoreType.DMA((n,)), pltpu.SemaphoreType.DMA((n,))] if n else []


def _exchange_halves(halves, wholes):
    ns, nw = len(halves), len(wholes)

    def body(*refs):
        srcs, dsts = refs[:ns + nw], refs[ns + nw:2 * (ns + nw)]
        sems = refs[2 * (ns + nw):]
        starts, waits = _swap_copies(srcs[:ns], dsts[:ns], *sems[:3])
        more = _scatter_copies(srcs[ns:], dsts[ns:], *sems[3:], whole=nw)
        for call in starts + more[0] + waits + more[1]:
            call()

    return pl.pallas_call(
        body, in_specs=[pl.BlockSpec(memory_space=pltpu.VMEM)] * ns + [_ANY] * nw, out_specs=[_ANY] * (ns + nw),
        out_shape=[_swapped_shape(h) for h in halves] + [_sds((8,) + a.shape, a.dtype) for a in wholes],
        scratch_shapes=_swap_scratch(ns) + _scatter_scratch(nw), name="exchange_halves",
    )(*halves, *wholes)


def _row_tile(rows, row_bytes, budget):
    tr = rows
    while tr * row_bytes > budget and tr % 16 == 0:
        tr //= 2
    return tr


def _sum_slots(parts, name):
    _, rows, cols = parts.shape
    tr = _row_tile(rows, 8 * cols * 4, 2 * 1024 * 1024)

    def body(p_ref, o_ref):
        acc = p_ref[0].astype(F32)
        for d in range(1, 8):
            acc = acc + p_ref[d].astype(F32)
        o_ref[...] = acc

    return pl.pallas_call(
        body, grid=(rows // tr,), in_specs=[pl.BlockSpec((8, tr, cols), lambda i: (0, i, 0))],
        out_specs=pl.BlockSpec((tr, cols), lambda i: (i, 0)), out_shape=_sds((rows, cols), F32),
        compiler_params=_params(("parallel",)), name=name,
    )(parts)


def _adam_update(w, g, m, v):
    m = ADAM_B1 * m + (1.0 - ADAM_B1) * g
    v = ADAM_B2 * v + (1.0 - ADAM_B2) * jnp.square(g)
    m_hat = m / (1.0 - ADAM_B1 ** ADAM_STEP)
    v_hat = v / (1.0 - ADAM_B2 ** ADAM_STEP)
    return -ADAM_LR * (m_hat / (jnp.sqrt(v_hat) + ADAM_EPS) + ADAM_WD * w), m, v


SMALL_ROWS = {"attn_norm_w": 0, "mlp_norm_w": 1, "q_lat_norm_w": 2, "kv_lat_norm_w": 3, "q_norm_w": 4, "k_norm_w": 5,
              "mla_out_norm_w": 6, "gdn_norm_w": 10, "a_log": 11, "dt_bias": 12}
LOSS_ROW = 13
SMALL_SHAPE = (16, 1024)


def _pack_small_partials(d_attn_nw, d_mlp_nw, d_ln, d_qk_nw, d_mix_nw, d_scal, conv_parts, sq):
    D = d_attn_nw.shape[1]

    def body(an_ref, mn_ref, ln_ref, qk_ref, mix_ref, sc_ref, cq_ref, ck_ref, cv_ref, sq_ref, a_ref, c_ref):
        a_ref[...] = jnp.zeros_like(a_ref)
        a_ref[0:1, :D] = an_ref[...]
        a_ref[1:2, :D] = mn_ref[...]
        a_ref[2:4, :LORA] = ln_ref[...]
        for row, base in ((4, 0), (5, 2)):
            rope = qk_ref[base + 1:base + 2, :]
            a_ref[row:row + 1, :QK_DIM] = jnp.concatenate(
                [qk_ref[base:base + 1, :], rope[:, :ROPE_HALF], rope[:, 2 * ROPE_HALF:3 * ROPE_HALF]], axis=1)
        a_ref[6:6 + HEADS, :HEAD_DIM] = mix_ref[0:HEADS, :]
        a_ref[10:11, :HEAD_DIM] = mix_ref[HEADS:HEADS + 1, :]
        a_ref[11:13, :128] = sc_ref[0:2, :]
        a_ref[LOSS_ROW:LOSS_ROW + 1, :128] = jnp.zeros((1, 128), F32) + jnp.sum(sq_ref[:, 0:1, 0:1]) * (0.5 / D)
        c_ref[...] = jnp.concatenate([cq_ref[...], ck_ref[...], cv_ref[...]], axis=1)

    return pl.pallas_call(
        body, out_shape=[_sds(SMALL_SHAPE, F32), _sds((CONV_TAPS, GQKV_W), F32)], name="pack_small_partials",
    )(d_attn_nw, d_mlp_nw, d_ln, d_qk_nw, d_mix_nw, d_scal, *conv_parts, sq)


def _adamw_small(parts, conv_parts, w, m, v):
    names = tuple(SMALL_ROWS) + ("conv_w",)
    cols = w["conv_w"].shape[2]

    def body(*refs):
        p_ref, c_ref = refs[:2]
        n = len(names)
        w_refs, m_refs, v_refs = (dict(zip(names, refs[2 + k * n:2 + (k + 1) * n])) for k in range(3))
        loss_ref = refs[2 + 3 * n]
        out = [dict(zip(names, refs[3 + (3 + k) * n:3 + (4 + k) * n])) for k in range(4)]
        acc_ref, cacc_ref = refs[3 + 7 * n:]
        acc, cacc = p_ref[0], c_ref[0]
        for d in range(1, 8):
            acc, cacc = acc + p_ref[d], cacc + c_ref[d]
        acc_ref[...] = acc
        cacc_ref[...] = cacc
        loss_ref[...] = acc_ref[LOSS_ROW:LOSS_ROW + 1, 0:1]
        chip = 2 * lax.axis_index("x") + lax.axis_index("y")
        for name in names:
            shape = w_refs[name].shape
            if name == "conv_w":
                g = sum(jnp.where(chip == s, cacc_ref[:, s * cols:(s + 1) * cols], 0.0) for s in range(4))[None]
            else:
                row = SMALL_ROWS[name]
                g = acc_ref[row:row + math.prod(shape[:-1]), 0:shape[-1]].reshape(shape)
            delta, new_m, new_v = _adam_update(w_refs[name][...], g, m_refs[name][...], v_refs[name][...])
            for ref, val in zip((o[name] for o in out), (g, delta, new_m, new_v)):
                ref[...] = val

    ins = [x[n] for x in (w, m, v) for n in names]
    shapes = [_sds(w[n].shape, F32) for n in names]
    outs = pl.pallas_call(
        body, out_shape=[_sds((1, 1), F32)] + shapes * 4,
        scratch_shapes=[pltpu.VMEM(parts.shape[1:], F32), pltpu.VMEM(conv_parts.shape[1:], F32)], name="adamw_small",
    )(parts, conv_parts, *ins)
    n = len(names)
    return (outs[0],) + tuple(dict(zip(names, outs[1 + k * n:1 + (k + 1) * n])) for k in range(4))


def _adamw(w, g, m, v, name):
    rows, cols = w.shape[0], w.shape[-1]
    if w.ndim == 3:
        tr = max(d for d in range(1, rows + 1) if rows % d == 0 and d * 8 * cols * 4 * 14 <= VMEM_LIMIT // 2)
    else:
        tr = _row_tile(rows, 7 * cols * 4, 4 * 1024 * 1024)

    def body(w_ref, g_ref, m_ref, v_ref, d_ref, mo_ref, vo_ref):
        d_ref[...], mo_ref[...], vo_ref[...] = _adam_update(w_ref[...], g_ref[...], m_ref[...], v_ref[...])

    block = (tr,) + w.shape[1:]
    spec = pl.BlockSpec(block, lambda i: (i,) + (0,) * (len(block) - 1))
    return pl.pallas_call(
        body, grid=(rows // tr,), in_specs=[spec] * 4, out_specs=[spec] * 3, out_shape=[_sds(w.shape, F32)] * 3,
        compiler_params=_params(("parallel",)), name=name,
    )(w, g, m, v)


def kernel(x, positions, attn_norm_w, w_in, q_lat_norm_w, w_uq, kv_lat_norm_w, w_ukv, q_norm_w, k_norm_w, mla_out_norm_w, conv_w, a_log, dt_bias, gdn_norm_w, w_out, mlp_norm_w, w_up, w_down, loss_target, m_attn_norm_w, m_w_in, m_q_lat_norm_w, m_w_uq, m_kv_lat_norm_w, m_w_ukv, m_q_norm_w, m_k_norm_w, m_mla_out_norm_w, m_conv_w, m_a_log, m_dt_bias, m_gdn_norm_w, m_w_out, m_mlp_norm_w, m_w_up, m_w_down, v_attn_norm_w, v_w_in, v_q_lat_norm_w, v_w_uq, v_kv_lat_norm_w, v_w_ukv, v_q_norm_w, v_k_norm_w, v_mla_out_norm_w, v_conv_w, v_a_log, v_dt_bias, v_gdn_norm_w, v_w_out, v_mlp_norm_w, v_w_up, v_w_down):
    w = dict(zip(WEIGHTS, (attn_norm_w, w_in, q_lat_norm_w, w_uq, kv_lat_norm_w, w_ukv, q_norm_w, k_norm_w, mla_out_norm_w, conv_w,
                           a_log, dt_bias, gdn_norm_w, w_out, mlp_norm_w, w_up, w_down)))
    m = dict(zip(WEIGHTS, (m_attn_norm_w, m_w_in, m_q_lat_norm_w, m_w_uq, m_kv_lat_norm_w, m_w_ukv, m_q_norm_w, m_k_norm_w,
                           m_mla_out_norm_w, m_conv_w, m_a_log, m_dt_bias, m_gdn_norm_w, m_w_out, m_mlp_norm_w, m_w_up, m_w_down)))
    v = dict(zip(WEIGHTS, (v_attn_norm_w, v_w_in, v_q_lat_norm_w, v_w_uq, v_kv_lat_norm_w, v_w_ukv, v_q_norm_w, v_k_norm_w,
                           v_mla_out_norm_w, v_conv_w, v_a_log, v_dt_bias, v_gdn_norm_w, v_w_out, v_mlp_norm_w, v_w_up, v_w_down)))
    B, S, D = x.shape
    T = B * S
    x2, pos, target = x.reshape(T, D), positions.reshape(T, 1), loss_target.reshape(T, D)
    seq = lambda a: a.reshape(B, S, a.shape[-1])
    tok = lambda a: a.reshape(T, a.shape[-1])
    local = {n: w[n][0] for n in SHARDED}

    g_in, g_uq, g_ukv, g_conv = _all_gather([jnp.swapaxes(w_in, 1, 2)[0].astype(BF16), local["w_uq"].astype(BF16),
                                             local["w_ukv"].astype(BF16), local["conv_w"]], "gather_first_weights")
    w_in_p = _widen_w_in_t(g_in.reshape(-1, D))
    w_mla = _stack_mla(_from_column_shards(g_uq), _from_column_shards(g_ukv))
    conv_full = _from_column_shards(g_conv)
    ln_w = jnp.concatenate([q_lat_norm_w, kv_lat_norm_w], axis=0)
    qk_nw = _qk_norm_rows(q_norm_w, k_norm_w)
    rope_rows = _rope_rows()
    scal = _rows8([jnp.pad(a_log, ((0, 0), (0, 128 - HEADS))), jnp.pad(dt_bias, ((0, 0), (0, 128 - HEADS)))])
    mix_nw = _rows8([mla_out_norm_w[0], gdn_norm_w])

    xn, lat, gqkv, gz, gab = _in_proj_fwd(x2, attn_norm_w, w_in_p)
    q, k, v_att = _mla_pre_fwd(lat, pos, ln_w, w_mla, qk_nw, rope_rows)
    ao, lse, g_down = _attn_fwd(seq(q), seq(k), seq(v_att), [local["w_down"].astype(BF16)])
    gq, gk, gv = _gdn_pre_fwd(seq(gqkv), conv_full)
    go, states, powers, solutions, g_out, w_up_b = _gdn_chunk_fwd(gq, gk, gv, seq(gab), scal,
                                                                  [local["w_out"].astype(BF16), local["w_up"].astype(BF16)])
    w_out_b = g_out.reshape(-1, D)
    w_down_b = g_down.reshape(-1, D)
    mix, h2 = _mix_fwd(tok(ao), tok(go), gz, mix_nw, w_out_b, x2)
    hn, act, dy, sq = _mlp_fwd(h2, mlp_norm_w, w_up_b, w_down_b, target)

    dh, d_up, d_mlp_nw = _mlp_bwd(h2, mlp_norm_w, act, w_up_b, w_down_b, dy)
    p_down = _wgrad(act, dy, "wgrad_down").reshape(4, -1, D)
    p_up = _wgrad(hn, d_up, "wgrad_up", column_shards=4)
    d_ao, d_go, d_gz, d_mix_nw = _mix_bwd(tok(ao), tok(go), gz, mix_nw, w_out_b, dh)
    p_out = _wgrad(mix, dh, "wgrad_out").reshape(4, -1, D)
    d_gq, d_gk, d_gv, d_gab, d_scal, s_up, s_out = _gdn_chunk_bwd(gq, gk, gv, seq(gab), scal, states, powers, solutions, seq(d_go),
                                                                  [p_up, p_out])
    early = ("w_up", "w_out", "w_down")
    dxq, dxk, dxv, dcq, dck, dcv, g_up, g_out = _gdn_pre_bwd(seq(gqkv), conv_full, d_gq, d_gk, d_gv,
                                                             [_sum_slots(s_up, "sum_w_up"), _sum_slots(s_out, "sum_w_out")])
    dq, dk, dv, s_down = _attn_bwd(seq(q), seq(k), seq(v_att), ao, lse, seq(d_ao), [p_down])
    d_lat, d_ln, d_w_mla, d_qk_nw = _mla_pre_bwd(lat, pos, ln_w, w_mla, qk_nw, rope_rows, tok(dq), tok(dk), tok(dv), [])
    d_pieces = [d_lat, tok(dxq), tok(dxk), tok(dxv), d_gz, tok(d_gab)]
    p_uq, p_ukv = (_column_shards(a).astype(BF16) for a in _unstack_mla(d_w_mla))
    p_in_wide, (s_uq, s_ukv) = _wgrad_pieces(d_pieces, xn, "wgrad_in", [p_uq, p_ukv])
    p_in = _narrow_w_in_t(p_in_wide).reshape(4, -1, D)
    grad_x2, d_attn_nw, s_in, g_down, g_uq, g_ukv = _in_proj_bwd(
        d_pieces, w_in_p, x2, attn_norm_w, dh, [p_in],
        [_sum_slots(s_down, "sum_w_down"), _sum_slots(s_uq, "sum_w_uq"), _sum_slots(s_ukv, "sum_w_ukv")])
    small_buf, conv_buf = _pack_small_partials(d_attn_nw, d_mlp_nw, d_ln, d_qk_nw, d_mix_nw, d_scal, (dcq, dck, dcv), sq)

    late = ("w_in", "w_uq", "w_ukv")
    g_in, s_small, s_conv = _exchange_halves([_sum_slots(s_in, "sum_w_in")], [small_buf, conv_buf])
    names = early + late
    grad = {n: g.reshape(-1, g.shape[-1]) for n, g in zip(names, [g_up, g_out, g_down, g_in, g_uq, g_ukv])}

    loss, g_small, delta, new_m, new_v = _adamw_small(s_small, s_conv, w, m, v)
    grad.update(g_small)
    for n in names:
        if n == "w_in":
            stored = lambda a: jnp.transpose(a, (2, 0, 1))
            outs = _adamw(stored(w[n]), grad[n][:, None, :], stored(m[n]), stored(v[n]), "adamw_" + n)
            grad[n], delta[n], new_m[n], new_v[n] = (jnp.transpose(a, (1, 2, 0)) for a in (grad[n][:, None, :], *outs))
        else:
            delta[n], new_m[n], new_v[n] = _adamw(local[n], grad[n], m[n][0], v[n][0], "adamw_" + n)
    def in_order(d):
        return [d[n].reshape(w[n].shape) for n in WEIGHTS]

    return (loss.reshape(()), grad_x2.reshape(B, S, D), *in_order(grad), *in_order(delta), *in_order(new_m), *in_order(new_v))
```

```python
import functools
import math

import jax
import jax.numpy as jnp
from jax import lax
from jax.experimental import pallas as pl
from jax.experimental.pallas import tpu as pltpu

F32 = jnp.float32
BF16 = jnp.bfloat16
MESH = pl.DeviceIdType.MESH

EPS = 1e-6
HEADS = 4
HEAD_DIM = 128
ROPE_DIM = 64
ROPE_HALF = 32
QK_DIM = 192
QK_PAD = 256
LORA = 256
CHUNK = 64
SOLVE_POWERS = 5
CONV_TAPS = 4
ROPE_THETA = 10000.0
ATTN_SCALE = QK_DIM ** -0.5

LAT_W = 640
GQKV_W = 3 * HEADS * HEAD_DIM
GZ_W = HEADS * HEAD_DIM
GAB_W = 128
PROJ_SPLITS = ((0, LAT_W), (LAT_W, LAT_W + GQKV_W), (LAT_W + GQKV_W, LAT_W + GQKV_W + GZ_W),
               (LAT_W + GQKV_W + GZ_W, LAT_W + GQKV_W + GZ_W + GAB_W))
PROJ_W = PROJ_SPLITS[-1][1]

ADAM_LR = 0.001
ADAM_B1 = 0.9
ADAM_B2 = 0.999
ADAM_EPS = 1e-08
ADAM_WD = 0.01
ADAM_STEP = 10

TOKEN_TILE = 512
WGRAD_TOKEN_TILE = 1024
MLP_TOKEN_TILE = 512
FF_TILE = 1024
ATTN_TILE = 512
ATTN_HEADS_PER_STEP = 2
WGRAD_OUT_BYTES = 8 * 1024 * 1024
VMEM_LIMIT = 48 * 1024 * 1024

SHARDED = ("w_in", "w_uq", "w_ukv", "conv_w", "w_out", "w_up", "w_down")
WEIGHTS = ("attn_norm_w", "w_in", "q_lat_norm_w", "w_uq", "kv_lat_norm_w", "w_ukv", "q_norm_w", "k_norm_w", "mla_out_norm_w",
           "conv_w", "a_log", "dt_bias", "gdn_norm_w", "w_out", "mlp_norm_w", "w_up", "w_down")


def _sds(shape, dtype):
    return jax.ShapeDtypeStruct(shape, dtype)


def _params(semantics):
    return pltpu.CompilerParams(dimension_semantics=semantics, vmem_limit_bytes=VMEM_LIMIT)


def _block(n):
    for b in (512, 256, 128):
        if n % b == 0:
            return b
    return n


def _dg(a, b, ca, cb, prec):
    lead = a.ndim - 2
    batch = (tuple(range(lead)),) * 2
    return lax.dot_general(a, b, (((ca + lead,), (cb + lead,)), batch), precision=prec, preferred_element_type=F32)


def _split_bf16(a):
    hi = a.astype(BF16)
    return hi, (a - hi.astype(F32)).astype(BF16)


def _dot_bf16(a, b, ca, cb):
    return _dg(a.astype(BF16), b.astype(BF16), ca, cb, None)


def _dot_bf16x3(a, b, ca, cb):
    a_hi, a_lo = _split_bf16(a)
    b_hi, b_lo = _split_bf16(b)
    lead = a.ndim - 2
    return _dg(jnp.concatenate([a_hi, a_hi, a_lo], axis=ca + lead), jnp.concatenate([b_hi, b_lo, b_hi], axis=cb + lead), ca, cb, None)


def _matmul_family(dot):
    def nn_raw(a, b):
        return dot(a, b, 1, 0)

    def nt_raw(a, b):
        return dot(a, b, 1, 1)

    def tn_raw(a, b):
        return dot(a, b, 0, 0)

    @jax.custom_vjp
    def nn(a, b):
        return nn_raw(a, b)

    nn.defvjp(lambda a, b: (nn_raw(a, b), (a, b)), lambda r, g: (nt_raw(g, r[1]), tn_raw(r[0], g)))

    @jax.custom_vjp
    def nt(a, b):
        return nt_raw(a, b)

    nt.defvjp(lambda a, b: (nt_raw(a, b), (a, b)), lambda r, g: (nn_raw(g, r[1]), tn_raw(g, r[0])))

    @jax.custom_vjp
    def tn(a, b):
        return tn_raw(a, b)

    tn.defvjp(lambda a, b: (tn_raw(a, b), (a, b)), lambda r, g: (nt_raw(r[1], g), nn_raw(r[0], g)))
    return nn, nt, tn


_bf_nn, _bf_nt, _bf_tn = _matmul_family(_dot_bf16)
_hi_nn, _hi_nt, _hi_tn = _matmul_family(_dot_bf16x3)


def _lower_powers(lmat):
    powers = []
    while 2 ** (len(powers) + 1) < lmat.shape[-1]:
        powers.append(_dot_bf16x3(powers[-1] if powers else lmat, powers[-1] if powers else lmat, 1, 0))
    return powers


@jax.custom_vjp
def _unit_lower_solve(lmat, rhs):
    return _unit_lower_solve_fwd(lmat, rhs)[0]


def _unit_lower_solve_fwd(lmat, rhs):
    powers = _lower_powers(lmat)
    x = rhs - _dot_bf16x3(lmat, rhs, 1, 0)
    for p in powers:
        x = x + _dot_bf16x3(p, x, 1, 0)
    return x, (lmat, powers, x)


def _unit_lower_solve_bwd(res, g):
    lmat, powers, x = res
    y = g - _dot_bf16x3(lmat, g, 0, 0)
    for p in powers:
        y = y + _dot_bf16x3(p, y, 0, 0)
    return -_dot_bf16x3(y, x, 1, 1), y


_unit_lower_solve.defvjp(_unit_lower_solve_fwd, _unit_lower_solve_bwd)


@jax.custom_vjp
def _unit_lower_solve_kept(lmat, rhs, powers, x):
    return x


_unit_lower_solve_kept.defvjp(
    lambda lmat, rhs, powers, x: (x, (lmat, powers, x)),
    lambda res, g: _unit_lower_solve_bwd(res, g) + ([jnp.zeros_like(p) for p in res[1]], jnp.zeros_like(res[2])))


@jax.custom_vjp
def _lane_halves(x):
    n = x.shape[-1] // 2
    return x[..., :n], x[..., n:]


_lane_halves.defvjp(lambda x: (_lane_halves(x), None), lambda _, g: (jnp.concatenate(g, axis=-1),))


@jax.custom_vjp
def _row_halves(x):
    n = x.shape[-2] // 2
    return x[..., :n, :], x[..., n:, :]


_row_halves.defvjp(lambda x: (_row_halves(x), None), lambda _, g: (jnp.concatenate(g, axis=-2),))


@jax.custom_vjp
def _swap_halves(t):
    return pltpu.roll(t, 64, 1)


_swap_halves.defvjp(lambda t: (pltpu.roll(t, 64, 1), None), lambda _, g: (pltpu.roll(g, 64, 1),))


@functools.partial(jax.custom_vjp, nondiff_argnums=(2,))
def _shift_rows(x, keep, s):
    return pltpu.roll(x, s, 0) * keep


def _shift_rows_fwd(x, keep, s):
    return pltpu.roll(x, s, 0) * keep, keep


def _shift_rows_bwd(s, keep, g):
    return pltpu.roll(g * keep, keep.shape[0] - s, 0), jnp.zeros_like(keep)


_shift_rows.defvjp(_shift_rows_fwd, _shift_rows_bwd)


def _sigmoid(x):
    return 0.5 * jnp.tanh(0.5 * x) + 0.5


def _softplus(x):
    return jnp.maximum(x, 0.0) + jnp.log(1.0 + jnp.exp(jnp.minimum(x, -x)))


def _silu(x):
    return x * _sigmoid(x)


def _rms(x, w, n=None):
    n = x.shape[-1] if n is None else n
    r = lax.rsqrt(jnp.sum(x * x, axis=-1, keepdims=True) * (1.0 / n) + EPS)
    return x * r * w


def _rope(t, cos_f, sin_f):
    return t * cos_f + _swap_halves(t) * sin_f


def _rope_tables(pos_col, freq_row, sign_row):
    ang = pos_col.astype(F32) * freq_row
    return jnp.cos(ang), jnp.sin(ang) * sign_row


def _onehot_row(lane):
    return (lax.broadcasted_iota(jnp.int32, (1, 128), 1) == lane).astype(F32)


def _row_spec(tm, w):
    return pl.BlockSpec((tm, w), lambda i: (i, 0))


def _const_spec(shape):
    return pl.BlockSpec(shape, lambda *_: (0,) * len(shape))


def _in_proj_fwd(x2, w_an, w_in_p):
    T, D = x2.shape
    tm = min(TOKEN_TILE, T)

    def body(x_ref, wn_ref, w_ref, xn_ref, lat_ref, gqkv_ref, gz_ref, gab_ref):
        x = x_ref[...]
        r = lax.rsqrt(jnp.mean(x * x, axis=-1, keepdims=True) + EPS)
        xn = (x * r * wn_ref[...]).astype(BF16)
        xn_ref[...] = xn
        for ref, (a, b) in zip((lat_ref, gqkv_ref, gz_ref, gab_ref), PROJ_SPLITS):
            ref[...] = _dg(xn, w_ref[a:b, :], 1, 1, None)

    widths = [b - a for a, b in PROJ_SPLITS]
    return pl.pallas_call(
        body, grid=(T // tm,),
        in_specs=[_row_spec(tm, D), _const_spec((1, D)), _const_spec((PROJ_W, D))],
        out_specs=[_row_spec(tm, D)] + [_row_spec(tm, w) for w in widths],
        out_shape=[_sds((T, D), BF16)] + [_sds((T, w), F32) for w in widths],
        compiler_params=_params(("parallel",)), name="in_proj_fwd",
    )(x2, w_an, w_in_p)


def _in_proj_bwd(pieces, w_in_p, x2, w_an, dh, partials, halves):
    T, D = x2.shape
    tm = min(TOKEN_TILE, T)
    widths = [p.shape[1] for p in pieces]
    starts = [sum(widths[:i]) for i in range(len(widths))]
    assert sum(widths) == PROJ_W
    npc, ns, nh = len(pieces), len(partials), len(halves)

    def body(*refs):
        piece_refs = refs[:npc]
        w_ref, x_ref, wn_ref, dh_ref = refs[npc:npc + 4]
        src_refs = refs[npc + 4:npc + 4 + ns + nh]
        dx_ref, dwn_ref = refs[npc + 4 + ns + nh:npc + 6 + ns + nh]
        dst_refs = refs[npc + 6 + ns + nh:npc + 6 + 2 * (ns + nh)]
        sems = refs[npc + 6 + 2 * (ns + nh):]

        def riders():
            scatter = _scatter_copies(src_refs[:ns], dst_refs[:ns], *sems[:3])
            swap = _swap_copies(src_refs[ns:], dst_refs[ns:], *sems[3:])
            return scatter[0] + swap[0], scatter[1] + swap[1]

        @pl.when(pl.program_id(0) == 0)
        def _():
            for start in riders()[0]:
                start()
            dwn_ref[...] = jnp.zeros_like(dwn_ref)

        dxn = jnp.zeros((tm, D), F32)
        for ref, a, width in zip(piece_refs, starts, widths):
            dxn += _dg(ref[...], w_ref[a:a + width, :], 1, 0, None)
        _, pull = jax.vjp(_rms, x_ref[...], wn_ref[...])
        dx, dwn = pull(dxn)
        dx_ref[...] = dx + dh_ref[...]
        dwn_ref[...] += dwn

        @pl.when(pl.program_id(0) == T // tm - 1)
        def _():
            for wait in riders()[1]:
                wait()

    return pl.pallas_call(
        body, grid=(T // tm,),
        in_specs=[_row_spec(tm, w) for w in widths] + [_const_spec((PROJ_W, D)), _row_spec(tm, D), _const_spec((1, D)),
                                                       _row_spec(tm, D)] + [_ANY] * (ns + nh),
        out_specs=[_row_spec(tm, D), _const_spec((1, D))] + [_ANY] * (ns + nh),
        out_shape=[_sds((T, D), F32), _sds((1, D), F32)] + [_scattered_shape(p) for p in partials]
                  + [_swapped_shape(h) for h in halves],
        scratch_shapes=_scatter_scratch(ns) + _swap_scratch(nh),
        compiler_params=_params(("arbitrary",)), name="in_proj_bwd",
    )(*pieces, w_in_p, x2, w_an, dh, *partials, *halves)


def _wgrad_pieces(pieces, b, name, partials):
    T, k2 = b.shape
    tt = min(WGRAD_TOKEN_TILE, T)
    widths = [p.shape[1] for p in pieces]
    starts = [sum(widths[:i]) for i in range(len(widths))]
    k1 = sum(widths)
    npc, ns = len(pieces), len(partials)

    def body(*refs):
        piece_refs, b_ref = refs[:npc], refs[npc]
        src_refs = refs[npc + 1:npc + 1 + ns]
        o_ref = refs[npc + 1 + ns]
        dst_refs = refs[npc + 2 + ns:npc + 2 + 2 * ns]
        acc_ref, *sems = refs[npc + 2 + 2 * ns:]
        t = pl.program_id(0)

        @pl.when(t == 0)
        def _():
            for start in _scatter_copies(src_refs, dst_refs, *sems)[0]:
                start()
            acc_ref[...] = jnp.zeros_like(acc_ref)

        bt = b_ref[...].astype(BF16)
        for ref, r0, width in zip(piece_refs, starts, widths):
            acc_ref[r0:r0 + width, :] += jnp.dot(ref[...].T, bt, preferred_element_type=F32)

        @pl.when(t == T // tt - 1)
        def _():
            o_ref[...] = acc_ref[...].astype(o_ref.dtype)
            for wait in _scatter_copies(src_refs, dst_refs, *sems)[1]:
                wait()

    out, *scattered = pl.pallas_call(
        body, grid=(T // tt,),
        in_specs=[pl.BlockSpec((tt, w), lambda t: (t, 0)) for w in widths] + [pl.BlockSpec((tt, k2), lambda t: (t, 0))] + [_ANY] * ns,
        out_specs=[_const_spec((k1, k2))] + [_ANY] * ns,
        out_shape=[_sds((k1, k2), BF16)] + [_scattered_shape(p) for p in partials],
        scratch_shapes=[pltpu.VMEM((k1, k2), F32)] + _scatter_scratch(ns),
        compiler_params=_params(("arbitrary",)), name=name,
    )(*pieces, b, *partials)
    return out, scattered


def _wgrad(a, b, name, column_shards=1, out_dtype=BF16):
    T, k1 = a.shape
    k2 = b.shape[1]
    per_shard = k2 // column_shards
    tt = min(WGRAD_TOKEN_TILE, T)
    b1 = k1
    while b1 * k2 * 4 > WGRAD_OUT_BYTES and b1 % 256 == 0:
        b1 //= 2
    step = _block(per_shard)

    def body(a_ref, b_ref, o_ref, acc_ref):
        t = pl.program_id(1)

        @pl.when(t == 0)
        def _():
            acc_ref[...] = jnp.zeros_like(acc_ref)

        a_t = a_ref[...].astype(BF16).T
        for c0 in range(0, k2, step):
            part = jnp.dot(a_t, b_ref[:, c0:c0 + step].astype(BF16), preferred_element_type=F32)
            if column_shards == 1:
                acc_ref[:, c0:c0 + step] += part
            else:
                acc_ref[c0 // per_shard, :, c0 % per_shard:c0 % per_shard + step] += part

        @pl.when(t == T // tt - 1)
        def _():
            o_ref[...] = acc_ref[...].astype(o_ref.dtype)

    if column_shards == 1:
        block, out_spec, out_shape = (b1, k2), pl.BlockSpec((b1, k2), lambda i, t: (i, 0)), _sds((k1, k2), out_dtype)
    else:
        block = (column_shards, b1, per_shard)
        out_spec, out_shape = pl.BlockSpec(block, lambda i, t: (0, i, 0)), _sds((column_shards, k1, per_shard), out_dtype)
    return pl.pallas_call(
        body, grid=(k1 // b1, T // tt),
        in_specs=[pl.BlockSpec((tt, b1), lambda i, t: (t, i)), pl.BlockSpec((tt, k2), lambda i, t: (t, 0))],
        out_specs=out_spec, out_shape=out_shape, scratch_shapes=[pltpu.VMEM(block, F32)],
        compiler_params=_params(("parallel", "arbitrary")), name=name,
    )(a, b)


@jax.custom_vjp
def _lane_blocks(x):
    return tuple(x[:, i:i + 128] for i in range(0, x.shape[1], 128))


_lane_blocks.defvjp(lambda x: (_lane_blocks(x), None), lambda _, g: (jnp.concatenate(g, axis=1),))


def _mla_pre_fn(q_lat, kv_lat, kpe, ln_q, ln_kv, w_q, w_kv, qn_n, qn_p, kn_n, kn_p, cos_f, sin_f):
    qn = _rms(q_lat, ln_q)
    kvn = _rms(kv_lat, ln_kv)
    kp = _rope(_rms(kpe, kn_p, ROPE_DIM), cos_f, sin_f)
    q_blocks = _lane_blocks(_bf_nn(qn, w_q))
    kv_blocks = _lane_blocks(_bf_nn(kvn, w_kv))
    outs = []
    for h in range(HEADS):
        outs.append(_rms(q_blocks[h], qn_n))
        outs.append(_rope(_rms(q_blocks[HEADS + h], qn_p, ROPE_DIM), cos_f, sin_f))
        outs.append(_rms(kv_blocks[h], kn_n))
        outs.append(kv_blocks[HEADS + h])
    return tuple(outs) + (kp,)


def _mla_pre_operands(lat_ref, pos_ref, ln_ref, w_ref, nw_ref, rope_ref):
    cos_f, sin_f = _rope_tables(pos_ref[...], rope_ref[0:1, :], rope_ref[1:2, :])
    side_by_side = lambda blocks: jnp.concatenate([w_ref[i].astype(F32) for i in blocks], axis=1)
    diff = (lat_ref[:, 0:LORA], lat_ref[:, LORA:2 * LORA], lat_ref[:, 2 * LORA:LAT_W], ln_ref[0:1, :], ln_ref[1:2, :],
            side_by_side(range(2 * HEADS)), side_by_side(range(2 * HEADS, 4 * HEADS)),
            nw_ref[0:1, :], nw_ref[1:2, :], nw_ref[2:3, :], nw_ref[3:4, :])
    return diff, cos_f, sin_f


def _mla_pre_fwd(lat, pos, ln_w, w_mla, nw, rope_rows):
    T = lat.shape[0]
    tm = min(TOKEN_TILE, T)

    def body(lat_ref, pos_ref, ln_ref, w_ref, nw_ref, rope_ref, q_ref, k_ref, v_ref):
        diff, cos_f, sin_f = _mla_pre_operands(lat_ref, pos_ref, ln_ref, w_ref, nw_ref, rope_ref)
        outs = _mla_pre_fn(*diff, cos_f, sin_f)
        kp = outs[-1].astype(BF16)
        for h in range(HEADS):
            q_n, q_p, k_n, v = outs[4 * h:4 * h + 4]
            q_ref[:, h * QK_PAD:h * QK_PAD + HEAD_DIM] = q_n.astype(BF16)
            q_ref[:, h * QK_PAD + HEAD_DIM:(h + 1) * QK_PAD] = q_p.astype(BF16)
            k_ref[:, h * QK_PAD:h * QK_PAD + HEAD_DIM] = k_n.astype(BF16)
            k_ref[:, h * QK_PAD + HEAD_DIM:(h + 1) * QK_PAD] = kp
            v_ref[:, h * HEAD_DIM:(h + 1) * HEAD_DIM] = v.astype(BF16)

    return pl.pallas_call(
        body, grid=(T // tm,),
        in_specs=[_row_spec(tm, LAT_W), _row_spec(tm, 1), _const_spec((2, LORA)), _const_spec((4 * HEADS, LORA, 128)),
                  _const_spec((8, 128)), _const_spec((8, 128))],
        out_specs=[_row_spec(tm, HEADS * QK_PAD), _row_spec(tm, HEADS * QK_PAD), _row_spec(tm, HEADS * HEAD_DIM)],
        out_shape=[_sds((T, HEADS * QK_PAD), BF16), _sds((T, HEADS * QK_PAD), BF16), _sds((T, HEADS * HEAD_DIM), BF16)],
        compiler_params=_params(("parallel",)), name="mla_pre_fwd",
    )(lat, pos, ln_w, w_mla, nw, rope_rows)


def _mla_pre_bwd(lat, pos, ln_w, w_mla, nw, rope_rows, dq, dk, dv, halves):
    T = lat.shape[0]
    tm = min(TOKEN_TILE, T)
    ns = len(halves)

    def body(*refs):
        lat_ref, pos_ref, ln_ref, w_ref, nw_ref, rope_ref, dq_ref, dk_ref, dv_ref = refs[:9]
        src_refs = refs[9:9 + ns]
        dlat_ref, dln_ref, dw_ref, dnw_ref = refs[9 + ns:13 + ns]
        dst_refs = refs[13 + ns:13 + 2 * ns]
        sems = refs[13 + 2 * ns:]

        @pl.when(pl.program_id(0) == 0)
        def _():
            for start in _swap_copies(src_refs, dst_refs, *sems)[0]:
                start()
            dln_ref[...] = jnp.zeros_like(dln_ref)
            dw_ref[...] = jnp.zeros_like(dw_ref)
            dnw_ref[...] = jnp.zeros_like(dnw_ref)

        diff, cos_f, sin_f = _mla_pre_operands(lat_ref, pos_ref, ln_ref, w_ref, nw_ref, rope_ref)
        _, pull = jax.vjp(lambda *a: _mla_pre_fn(*a, cos_f, sin_f), *diff)
        cts = []
        d_kp = jnp.zeros((tm, 128), F32)
        for h in range(HEADS):
            cts.append(dq_ref[:, h * QK_PAD:h * QK_PAD + HEAD_DIM])
            cts.append(dq_ref[:, h * QK_PAD + HEAD_DIM:(h + 1) * QK_PAD])
            cts.append(dk_ref[:, h * QK_PAD:h * QK_PAD + HEAD_DIM])
            cts.append(dv_ref[:, h * HEAD_DIM:(h + 1) * HEAD_DIM])
            d_kp += dk_ref[:, h * QK_PAD + HEAD_DIM:(h + 1) * QK_PAD]
        d_ql, d_kvl, d_kpe, d_lnq, d_lnkv, d_wq, d_wkv, d_qn_n, d_qn_p, d_kn_n, d_kn_p = pull(tuple(cts) + (d_kp,))
        d_w = [d[:, i:i + 128] for d in (d_wq, d_wkv) for i in range(0, d.shape[1], 128)]
        dlat_ref[:, 0:LORA] = d_ql.astype(BF16)
        dlat_ref[:, LORA:2 * LORA] = d_kvl.astype(BF16)
        dlat_ref[:, 2 * LORA:LAT_W] = d_kpe.astype(BF16)
        dln_ref[0:1, :] += d_lnq
        dln_ref[1:2, :] += d_lnkv
        for i in range(4 * HEADS):
            dw_ref[i] += d_w[i]
        for i, d in enumerate((d_qn_n, d_qn_p, d_kn_n, d_kn_p)):
            dnw_ref[i:i + 1, :] += d

        @pl.when(pl.program_id(0) == T // tm - 1)
        def _():
            for wait in _swap_copies(src_refs, dst_refs, *sems)[1]:
                wait()

    return pl.pallas_call(
        body, grid=(T // tm,),
        in_specs=[_row_spec(tm, LAT_W), _row_spec(tm, 1), _const_spec((2, LORA)), _const_spec((4 * HEADS, LORA, 128)),
                  _const_spec((8, 128)), _const_spec((8, 128)),
                  _row_spec(tm, HEADS * QK_PAD), _row_spec(tm, HEADS * QK_PAD), _row_spec(tm, HEADS * HEAD_DIM)] + [_ANY] * ns,
        out_specs=[_row_spec(tm, LAT_W), _const_spec((2, LORA)), _const_spec((4 * HEADS, LORA, 128)), _const_spec((8, 128))]
                  + [_ANY] * ns,
        out_shape=[_sds((T, LAT_W), BF16), _sds((2, LORA), F32), _sds((4 * HEADS, LORA, 128), F32), _sds((8, 128), F32)]
                  + [_swapped_shape(h) for h in halves],
        scratch_shapes=_swap_scratch(ns),
        compiler_params=_params(("arbitrary",)), name="mla_pre_bwd",
    )(lat, pos, ln_w, w_mla, nw, rope_rows, dq, dk, dv, *halves)


def _causal_mask(i, j, tq, tk):
    row = i * tq + lax.broadcasted_iota(jnp.int32, (tq, tk), 0)
    col = j * tk + lax.broadcasted_iota(jnp.int32, (tq, tk), 1)
    return col <= row


def _attn_fwd(q, k, v, shards):
    B, S, _ = q.shape
    t = min(ATTN_TILE, S)
    nq = S // t
    ns = len(shards)

    hp = ATTN_HEADS_PER_STEP
    qk = lambda h: slice(h * QK_PAD, (h + 1) * QK_PAD)
    vd = lambda h: slice(h * HEAD_DIM, (h + 1) * HEAD_DIM)

    def body(*refs):
        q_ref, k_ref, v_ref = refs[:3]
        src_refs = refs[3:3 + ns]
        o_ref, lse_ref = refs[3 + ns:5 + ns]
        dst_refs = refs[5 + ns:5 + 2 * ns]
        sems = refs[5 + 2 * ns:]
        b, g, i = pl.program_id(0), pl.program_id(1), pl.program_id(2)
        qb = [q_ref[0, :, qk(h)] for h in range(hp)]

        step_no = (b * (HEADS // hp) + g) * nq + i
        for phase, at in enumerate((0, (3 * B * (HEADS // hp) * nq) // 4)):
            @pl.when(step_no == at)
            def _(phase=phase):
                for call in _gather_copies(src_refs, dst_refs, *sems)[phase]:
                    call()

        def step(j, carry, diagonal):
            rows = pl.ds(pl.multiple_of(j * t, t), t)
            s = [_dg(qb[h], k_ref[0, rows, qk(h)], 1, 1, None) * ATTN_SCALE for h in range(hp)]
            if diagonal:
                keep = _causal_mask(0, 0, t, t)
                s = [jnp.where(keep, x, -1e30) for x in s]
            m_new = [jnp.maximum(carry[h][0], jnp.max(s[h], axis=-1, keepdims=True)) for h in range(hp)]
            p = [jnp.exp(s[h] - m_new[h]) for h in range(hp)]
            alpha = [jnp.exp(carry[h][0] - m_new[h]) for h in range(hp)]
            l = [alpha[h] * carry[h][1] + jnp.sum(p[h], axis=-1, keepdims=True) for h in range(hp)]
            pv = [jnp.dot(p[h].astype(BF16), v_ref[0, rows, vd(h)], preferred_element_type=F32) for h in range(hp)]
            return tuple((m_new[h], l[h], alpha[h] * carry[h][2] + pv[h]) for h in range(hp))

        init = tuple((jnp.full((t, 1), -1e30, F32), jnp.zeros((t, 1), F32), jnp.zeros((t, HEAD_DIM), F32)) for _ in range(hp))
        below = lax.fori_loop(0, i, lambda j, carry: step(j, carry, False), init)
        for h, (m, l, acc) in enumerate(step(i, below, True)):
            o_ref[0, :, vd(h)] = acc / l
            lse_ref[0, h, 0] = (m + jnp.log(l)).T

        @pl.when((b == B - 1) & (g == HEADS // hp - 1) & (i == nq - 1))
        def _():
            for wait in _gather_copies(src_refs, dst_refs, *sems)[2]:
                wait()

    return pl.pallas_call(
        body, grid=(B, HEADS // hp, nq),
        in_specs=[pl.BlockSpec((1, t, hp * QK_PAD), lambda b, g, i: (b, i, g)),
                  pl.BlockSpec((1, S, hp * QK_PAD), lambda b, g, i: (b, 0, g)),
                  pl.BlockSpec((1, S, hp * HEAD_DIM), lambda b, g, i: (b, 0, g))] + [_ANY] * ns,
        out_specs=[pl.BlockSpec((1, t, hp * HEAD_DIM), lambda b, g, i: (b, i, g)),
                   pl.BlockSpec((1, hp, 1, 1, t), lambda b, g, i: (b, g, i, 0, 0))] + [_ANY] * ns,
        out_shape=[_sds((B, S, HEADS * HEAD_DIM), F32), _sds((B, HEADS, nq, 1, t), F32)] + [_sds((4,) + s.shape, s.dtype) for s in shards],
        scratch_shapes=_gather_scratch(ns),
        compiler_params=_params(("arbitrary", "arbitrary", "arbitrary")), name="attn_fwd",
    )(q, k, v, *shards)


def _attn_bwd(q, k, v, o, lse, do, partials):
    B, S, _ = q.shape
    t = min(ATTN_TILE, S)
    nq = S // t
    ns = len(partials)

    hp = ATTN_HEADS_PER_STEP
    qk = lambda h: slice(h * QK_PAD, (h + 1) * QK_PAD)
    vd = lambda h: slice(h * HEAD_DIM, (h + 1) * HEAD_DIM)
    heads = range(hp)

    def body(*refs):
        q_ref, k_ref, v_ref, o_ref, lse_ref, do_ref = refs[:6]
        src_refs = refs[6:6 + ns]
        dq_ref, dk_ref, dv_ref = refs[6 + ns:9 + ns]
        dst_refs = refs[9 + ns:9 + 2 * ns]
        dsum_ref, send_sems, recv_sems, local_sems = refs[9 + 2 * ns:]
        b, g, j = pl.program_id(0), pl.program_id(1), pl.program_id(2)

        @pl.when((b == 0) & (g == 0) & (j == 0))
        def _():
            for start in _scatter_copies(src_refs, dst_refs, send_sems, recv_sems, local_sems)[0]:
                start()

        @pl.when(j == 0)
        def _():
            dq_ref[...] = jnp.zeros_like(dq_ref)
            for h in heads:
                for blk in range(nq):
                    rows = slice(blk * t, (blk + 1) * t)
                    dsum_ref[h, blk] = jnp.sum(do_ref[0, rows, vd(h)] * o_ref[0, rows, vd(h)], axis=-1, keepdims=True).T

        kb = [k_ref[0, :, qk(h)] for h in heads]
        vb = [v_ref[0, :, vd(h)] for h in heads]

        def step(i, carry, diagonal):
            rows = pl.ds(pl.multiple_of(i * t, t), t)
            qb = [q_ref[0, rows, qk(h)] for h in heads]
            dob = [do_ref[0, rows, vd(h)].astype(BF16) for h in heads]
            s = [_dg(kb[h], qb[h], 1, 1, None) * ATTN_SCALE for h in heads]
            p = [jnp.exp(s[h] - lse_ref[0, h, i]) for h in heads]
            if diagonal:
                key = lax.broadcasted_iota(jnp.int32, (t, t), 0)
                query = lax.broadcasted_iota(jnp.int32, (t, t), 1)
                p = [jnp.where(key <= query, x, 0.0) for x in p]
            dp = [_dg(vb[h], dob[h], 1, 1, None) for h in heads]
            dv = [carry[h][1] + jnp.dot(p[h].astype(BF16), dob[h], preferred_element_type=F32) for h in heads]
            ds = [(p[h] * (dp[h] - dsum_ref[h, i]) * ATTN_SCALE).astype(BF16) for h in heads]
            for h in heads:
                dq_ref[0, rows, qk(h)] += _dg(ds[h], kb[h], 0, 0, None)
            return tuple((carry[h][0] + jnp.dot(ds[h], qb[h], preferred_element_type=F32), dv[h]) for h in heads)

        zeros = tuple((jnp.zeros((t, QK_PAD), F32), jnp.zeros((t, HEAD_DIM), F32)) for _ in heads)
        on_diagonal = step(j, zeros, True)
        done = lax.fori_loop(j + 1, nq, lambda i, carry: step(i, carry, False), on_diagonal)
        for h, (dk, dv) in enumerate(done):
            dk_ref[0, :, qk(h)] = dk
            dv_ref[0, :, vd(h)] = dv

        @pl.when((b == B - 1) & (g == HEADS // hp - 1) & (j == nq - 1))
        def _():
            for wait in _scatter_copies(src_refs, dst_refs, send_sems, recv_sems, local_sems)[1]:
                wait()

    return pl.pallas_call(
        body, grid=(B, HEADS // hp, nq),
        in_specs=[pl.BlockSpec((1, S, hp * QK_PAD), lambda b, g, j: (b, 0, g)),
                  pl.BlockSpec((1, t, hp * QK_PAD), lambda b, g, j: (b, j, g)),
                  pl.BlockSpec((1, t, hp * HEAD_DIM), lambda b, g, j: (b, j, g)),
                  pl.BlockSpec((1, S, hp * HEAD_DIM), lambda b, g, j: (b, 0, g)),
                  pl.BlockSpec((1, hp, nq, 1, t), lambda b, g, j: (b, g, 0, 0, 0)),
                  pl.BlockSpec((1, S, hp * HEAD_DIM), lambda b, g, j: (b, 0, g))] + [_ANY] * ns,
        out_specs=[pl.BlockSpec((1, S, hp * QK_PAD), lambda b, g, j: (b, 0, g)),
                   pl.BlockSpec((1, t, hp * QK_PAD), lambda b, g, j: (b, j, g)),
                   pl.BlockSpec((1, t, hp * HEAD_DIM), lambda b, g, j: (b, j, g))] + [_ANY] * ns,
        out_shape=[_sds((B, S, HEADS * QK_PAD), F32), _sds((B, S, HEADS * QK_PAD), F32), _sds((B, S, HEADS * HEAD_DIM), F32)]
                  + [_scattered_shape(p) for p in partials],
        scratch_shapes=[pltpu.VMEM((hp, nq, 1, t), F32)] + _scatter_scratch(ns),
        compiler_params=_params(("arbitrary", "arbitrary", "arbitrary")), name="attn_bwd",
    )(q, k, v, o, lse, do, *partials)


def _gdn_pre_fn(xq, xk, xv, wq, wk, wv, keeps):
    def conv_silu(x, w):
        acc = x * w[3]
        for s in (1, 2, 3):
            acc = acc + _shift_rows(x, keeps[s - 1], s) * w[3 - s]
        return _silu(acc)

    def l2(x):
        return x * lax.rsqrt(jnp.sum(x * x, axis=-1, keepdims=True) + EPS)

    return l2(conv_silu(xq, wq)) * (HEAD_DIM ** -0.5), l2(conv_silu(xk, wk)), conv_silu(xv, wv)


def _gdn_pre_specs(S):
    x_specs = [pl.BlockSpec((1, S, HEAD_DIM), lambda h, b, g=g: (b, 0, g * HEADS + h)) for g in range(3)]
    w_specs = [pl.BlockSpec((CONV_TAPS, HEAD_DIM), lambda h, b, g=g: (0, g * HEADS + h)) for g in range(3)]
    out_spec = pl.BlockSpec((1, S, HEAD_DIM), lambda h, b: (b, 0, h))
    return x_specs, w_specs, out_spec


def _row_keeps(S):
    t = lax.broadcasted_iota(jnp.int32, (S, HEAD_DIM), 0)
    return [(t >= s).astype(F32) for s in (1, 2, 3)]


def _gdn_pre_fwd(gqkv, conv_w):
    B, S, _ = gqkv.shape
    x_specs, w_specs, out_spec = _gdn_pre_specs(S)

    def body(xq_ref, xk_ref, xv_ref, wq_ref, wk_ref, wv_ref, q_ref, k_ref, v_ref):
        taps = [[w[i:i + 1, :] for i in range(CONV_TAPS)] for w in (wq_ref, wk_ref, wv_ref)]
        q, k, v = _gdn_pre_fn(xq_ref[0], xk_ref[0], xv_ref[0], *taps, _row_keeps(S))
        q_ref[0], k_ref[0], v_ref[0] = q, k, v

    return pl.pallas_call(
        body, grid=(HEADS, B), in_specs=x_specs + w_specs, out_specs=[out_spec] * 3,
        out_shape=[_sds((B, S, HEADS * HEAD_DIM), F32)] * 3,
        compiler_params=_params(("parallel", "parallel")), name="gdn_pre_fwd",
    )(gqkv, gqkv, gqkv, conv_w, conv_w, conv_w)


def _gdn_pre_bwd(gqkv, conv_w, dq, dk, dv, halves):
    B, S, _ = gqkv.shape
    x_specs, w_specs, out_spec = _gdn_pre_specs(S)
    dw_spec = pl.BlockSpec((CONV_TAPS, HEAD_DIM), lambda h, b: (0, h))
    ns = len(halves)

    def body(*refs):
        xq_ref, xk_ref, xv_ref, wq_ref, wk_ref, wv_ref, dq_ref, dk_ref, dv_ref = refs[:9]
        src_refs = refs[9:9 + ns]
        dxq_ref, dxk_ref, dxv_ref, dwq_ref, dwk_ref, dwv_ref = refs[9 + ns:15 + ns]
        dst_refs = refs[15 + ns:15 + 2 * ns]
        sems = refs[15 + 2 * ns:]
        first = (pl.program_id(0) == 0) & (pl.program_id(1) == 0)
        last = (pl.program_id(0) == HEADS - 1) & (pl.program_id(1) == B - 1)

        @pl.when(first)
        def _():
            for start in _swap_copies(src_refs, dst_refs, *sems)[0]:
                start()

        @pl.when(pl.program_id(1) == 0)
        def _():
            for r in (dwq_ref, dwk_ref, dwv_ref):
                r[...] = jnp.zeros_like(r)

        taps = [[w[i:i + 1, :] for i in range(CONV_TAPS)] for w in (wq_ref, wk_ref, wv_ref)]
        keeps = _row_keeps(S)
        _, pull = jax.vjp(lambda *a: _gdn_pre_fn(*a, keeps), xq_ref[0], xk_ref[0], xv_ref[0], *taps)
        dxq, dxk, dxv, dwq, dwk, dwv = pull((dq_ref[0], dk_ref[0], dv_ref[0]))
        dxq_ref[0], dxk_ref[0], dxv_ref[0] = dxq.astype(BF16), dxk.astype(BF16), dxv.astype(BF16)
        for ref, dw in ((dwq_ref, dwq), (dwk_ref, dwk), (dwv_ref, dwv)):
            for i in range(CONV_TAPS):
                ref[i:i + 1, :] += dw[i]

        @pl.when(last)
        def _():
            for wait in _swap_copies(src_refs, dst_refs, *sems)[1]:
                wait()

    hw = HEADS * HEAD_DIM
    return pl.pallas_call(
        body, grid=(HEADS, B), in_specs=x_specs + w_specs + [out_spec] * 3 + [_ANY] * ns,
        out_specs=[out_spec] * 3 + [dw_spec] * 3 + [_ANY] * ns,
        out_shape=[_sds((B, S, hw), BF16)] * 3 + [_sds((CONV_TAPS, hw), F32)] * 3 + [_swapped_shape(h) for h in halves],
        scratch_shapes=_swap_scratch(ns),
        compiler_params=_params(("arbitrary", "arbitrary")), name="gdn_pre_bwd",
    )(gqkv, gqkv, gqkv, conv_w, conv_w, conv_w, dq, dk, dv, *halves)


def _chunk_masks():
    i = lax.broadcasted_iota(jnp.int32, (CHUNK, CHUNK), 0)
    j = lax.broadcasted_iota(jnp.int32, (CHUNK, CHUNK), 1)
    lower, after = (j <= i).astype(F32), (j > i).astype(F32)
    return {"le": lower, "le_gt": jnp.concatenate([lower, after], axis=0), "strict": (j < i).astype(F32)}


def _gdn_chunk_fn(groups, masks, solve=_unit_lower_solve):
    lane = lax.broadcasted_iota(jnp.int32, (groups, 1, 128), 2)
    head = lax.broadcasted_iota(jnp.int32, (groups, 1, 128), 0) % HEADS
    pick_a, pick_b = (lane == head).astype(F32), (lane == head + HEADS).astype(F32)
    lower, lower_after, strict = (jnp.broadcast_to(masks[n], (groups,) + masks[n].shape) for n in ("le", "le_gt", "strict"))
    ones_row = jnp.ones((1, 1, HEAD_DIM), F32)

    def f(q, k, v, gab, a_row, dt_row, state):
        ga = jnp.sum(gab * pick_a, axis=2, keepdims=True)
        gb = jnp.sum(gab * pick_b, axis=2, keepdims=True)
        a_log = jnp.sum(a_row * pick_a, axis=2, keepdims=True)
        dt_bias = jnp.sum(dt_row * pick_a, axis=2, keepdims=True)
        beta = _sigmoid(gb)
        g = -jnp.exp(a_log) * _softplus(ga + dt_bias)
        g_wide = g * ones_row
        cum, rest = _row_halves(_hi_nn(lower_after, g_wide))
        total = jnp.sum(g_wide, axis=1, keepdims=True)
        diff = _hi_nn(lower, g * strict)
        decay = lower * jnp.exp(diff)
        e_cum = jnp.exp(cum)
        kk, qk = _row_halves(_bf_nt(jnp.concatenate([k, q], axis=1), k))
        lmat = strict * (beta * kk * decay)
        u, w = _lane_halves(solve(lmat, jnp.concatenate([v * beta, k * (beta * e_cum)], axis=2)))
        w_state, q_state = _row_halves(_bf_nn(jnp.concatenate([w, q * e_cum], axis=1), state))
        v_new = u - w_state
        o = q_state + _bf_nn(qk * decay, v_new)
        new_state = state * jnp.exp(total) + _bf_tn(k * jnp.exp(rest), v_new)
        return o, new_state

    return f


def _gdn_chunk_fwd(q, k, v, gab, scal, shards):
    B, S, W = q.shape
    N = S // CHUNK
    ns = len(shards)

    def body(*refs):
        q_ref, k_ref, v_ref, gab_ref, sc_ref = refs[:5]
        src_refs = refs[5:5 + ns]
        o_ref, st_ref, pw_ref, sol_ref = refs[5 + ns:9 + ns]
        dst_refs = refs[9 + ns:9 + 2 * ns]
        state_ref, send_sems, recv_sems, local_sems = refs[9 + 2 * ns:]
        n = pl.program_id(0)
        kept = {}

        @pl.when(n == 0)
        def _():
            for start in _gather_copies(src_refs, dst_refs, send_sems, recv_sems, local_sems)[0]:
                start()
            state_ref[...] = jnp.zeros_like(state_ref)

        @pl.when(n == (2 * N) // 3)
        def _():
            for pass_on in _gather_copies(src_refs, dst_refs, send_sems, recv_sems, local_sems)[1]:
                pass_on()

        groups = [(b, h) for b in range(B) for h in range(HEADS)]
        gather = lambda ref: jnp.stack([ref[b, :, h * HEAD_DIM:(h + 1) * HEAD_DIM] for b, h in groups])
        state = state_ref[...]
        for i, (b, h) in enumerate(groups):
            st_ref[b, 0, h] = state[i]
        def solve_and_keep(lmat, rhs):
            kept["x"], (_, kept["powers"], _) = _unit_lower_solve_fwd(lmat, rhs)
            return kept["x"]

        o, new_state = _gdn_chunk_fn(len(groups), _chunk_masks(), solve_and_keep)(
            gather(q_ref), gather(k_ref), gather(v_ref), jnp.stack([gab_ref[b] for b, _ in groups]), sc_ref[0:1, :], sc_ref[1:2, :], state)
        for i, (b, h) in enumerate(groups):
            o_ref[b, :, h * HEAD_DIM:(h + 1) * HEAD_DIM] = o[i]
            sol_ref[b, 0, h] = kept["x"][i]
            for p, power in enumerate(kept["powers"]):
                pw_ref[b, 0, h, p] = power[i]
        state_ref[...] = new_state

        @pl.when(n == N - 1)
        def _():
            for wait in _gather_copies(src_refs, dst_refs, send_sems, recv_sems, local_sems)[2]:
                wait()

    seq = pl.BlockSpec((B, CHUNK, W), lambda n: (0, n, 0))
    return pl.pallas_call(
        body, grid=(N,),
        in_specs=[seq, seq, seq, pl.BlockSpec((B, CHUNK, GAB_W), lambda n: (0, n, 0)), _const_spec((8, 128))] + [_ANY] * ns,
        out_specs=[seq, pl.BlockSpec((B, 1, HEADS, HEAD_DIM, HEAD_DIM), lambda n: (0, n, 0, 0, 0)),
                   pl.BlockSpec((B, 1, HEADS, SOLVE_POWERS, CHUNK, CHUNK), lambda n: (0, n, 0, 0, 0, 0)),
                   pl.BlockSpec((B, 1, HEADS, CHUNK, 2 * HEAD_DIM), lambda n: (0, n, 0, 0, 0))] + [_ANY] * ns,
        out_shape=[_sds((B, S, W), F32), _sds((B, N, HEADS, HEAD_DIM, HEAD_DIM), F32),
                   _sds((B, N, HEADS, SOLVE_POWERS, CHUNK, CHUNK), F32), _sds((B, N, HEADS, CHUNK, 2 * HEAD_DIM), F32)]
                  + [_sds((4,) + s.shape, s.dtype) for s in shards],
        scratch_shapes=[pltpu.VMEM((B * HEADS, HEAD_DIM, HEAD_DIM), F32)] + _gather_scratch(ns),
        compiler_params=_params(("arbitrary",)), name="gdn_chunk_fwd",
    )(q, k, v, gab, scal, *shards)


def _gdn_chunk_bwd(q, k, v, gab, scal, states, powers, solutions, do, partials):
    B, S, W = q.shape
    N = S // CHUNK
    ns = len(partials)

    def body(*refs):
        q_ref, k_ref, v_ref, gab_ref, sc_ref, st_ref, pw_ref, sol_ref, do_ref = refs[:9]
        src_refs = refs[9:9 + ns]
        dq_ref, dk_ref, dv_ref, dgab_ref, dsc_ref = refs[9 + ns:14 + ns]
        dst_refs = refs[14 + ns:14 + 2 * ns]
        dstate_ref, send_sems, recv_sems, local_sems = refs[14 + 2 * ns:]
        n = pl.program_id(0)

        @pl.when(n == 0)
        def _():
            for start in _scatter_copies(src_refs, dst_refs, send_sems, recv_sems, local_sems)[0]:
                start()
            dstate_ref[...] = jnp.zeros_like(dstate_ref)
            dsc_ref[...] = jnp.zeros_like(dsc_ref)

        groups = [(b, h) for b in range(B) for h in range(HEADS)]
        gather = lambda ref: jnp.stack([ref[b, :, h * HEAD_DIM:(h + 1) * HEAD_DIM] for b, h in groups])
        kept_powers = [jnp.stack([pw_ref[b, 0, h, p] for b, h in groups]) for p in range(SOLVE_POWERS)]
        kept_x = jnp.stack([sol_ref[b, 0, h] for b, h in groups])
        solve = lambda lmat, rhs: _unit_lower_solve_kept(lmat, rhs, kept_powers, kept_x)
        _, pull = jax.vjp(_gdn_chunk_fn(len(groups), _chunk_masks(), solve), gather(q_ref), gather(k_ref), gather(v_ref),
                          jnp.stack([gab_ref[b] for b, _ in groups]), sc_ref[0:1, :], sc_ref[1:2, :],
                          jnp.stack([st_ref[b, 0, h] for b, h in groups]))
        dq, dk, dv, dg, d_a, d_dt, dstate = pull((gather(do_ref), dstate_ref[...]))
        for i, (b, h) in enumerate(groups):
            lanes = slice(h * HEAD_DIM, (h + 1) * HEAD_DIM)
            dq_ref[b, :, lanes] = dq[i]
            dk_ref[b, :, lanes] = dk[i]
            dv_ref[b, :, lanes] = dv[i]
        for b in range(B):
            dgab_ref[b] = sum(dg[b * HEADS + h] for h in range(HEADS)).astype(BF16)
        dstate_ref[...] = dstate
        dsc_ref[0:1, :] += d_a
        dsc_ref[1:2, :] += d_dt

        @pl.when(n == N - 1)
        def _():
            for wait in _scatter_copies(src_refs, dst_refs, send_sems, recv_sems, local_sems)[1]:
                wait()

    seq = pl.BlockSpec((B, CHUNK, W), lambda n: (0, N - 1 - n, 0))
    gab_spec = pl.BlockSpec((B, CHUNK, GAB_W), lambda n: (0, N - 1 - n, 0))
    return pl.pallas_call(
        body, grid=(N,),
        in_specs=[seq, seq, seq, gab_spec, _const_spec((8, 128)),
                  pl.BlockSpec((B, 1, HEADS, HEAD_DIM, HEAD_DIM), lambda n: (0, N - 1 - n, 0, 0, 0)),
                  pl.BlockSpec((B, 1, HEADS, SOLVE_POWERS, CHUNK, CHUNK), lambda n: (0, N - 1 - n, 0, 0, 0, 0)),
                  pl.BlockSpec((B, 1, HEADS, CHUNK, 2 * HEAD_DIM), lambda n: (0, N - 1 - n, 0, 0, 0)), seq] + [_ANY] * ns,
        out_specs=[seq, seq, seq, gab_spec, _const_spec((8, 128))] + [_ANY] * ns,
        out_shape=[_sds((B, S, W), F32)] * 3 + [_sds((B, S, GAB_W), BF16), _sds((8, 128), F32)] + [_scattered_shape(p) for p in partials],
        scratch_shapes=[pltpu.VMEM((B * HEADS, HEAD_DIM, HEAD_DIM), F32)] + _scatter_scratch(ns),
        compiler_params=_params(("arbitrary",)), name="gdn_chunk_bwd",
    )(q, k, v, gab, scal, states, powers, solutions, do, *partials)


def _mix_fn(ao, go, gz, w_mla, w_gdn):
    return tuple(_rms(ao[h], w_mla[h]) for h in range(HEADS)) + tuple(_rms(go[h], w_gdn) * _silu(gz[h]) for h in range(HEADS))


def _mix_operands(ao_ref, go_ref, gz_ref, nw_ref):
    blocks = lambda ref: [ref[:, h * HEAD_DIM:(h + 1) * HEAD_DIM] for h in range(HEADS)]
    return blocks(ao_ref), blocks(go_ref), blocks(gz_ref), [nw_ref[h:h + 1, :] for h in range(HEADS)], nw_ref[HEADS:HEADS + 1, :]


def _mix_fwd(ao, go, gz, nw, w_out, x2):
    T, D = x2.shape
    tm = min(TOKEN_TILE, T)
    MW = 2 * HEADS * HEAD_DIM

    def body(ao_ref, go_ref, gz_ref, nw_ref, w_ref, x_ref, mix_ref, h_ref):
        outs = _mix_fn(*_mix_operands(ao_ref, go_ref, gz_ref, nw_ref))
        for i, piece in enumerate(outs):
            mix_ref[:, i * HEAD_DIM:(i + 1) * HEAD_DIM] = piece.astype(BF16)
        h_ref[...] = x_ref[...] + jnp.dot(mix_ref[...], w_ref[...], preferred_element_type=F32)

    half = HEADS * HEAD_DIM
    return pl.pallas_call(
        body, grid=(T // tm,),
        in_specs=[_row_spec(tm, half), _row_spec(tm, half), _row_spec(tm, half), _const_spec((8, 128)), _const_spec((MW, D)),
                  _row_spec(tm, D)],
        out_specs=[_row_spec(tm, MW), _row_spec(tm, D)],
        out_shape=[_sds((T, MW), BF16), _sds((T, D), F32)],
        compiler_params=_params(("parallel",)), name="mix_fwd",
    )(ao, go, gz, nw, w_out, x2)


def _mix_bwd(ao, go, gz, nw, w_out, dh):
    T, D = dh.shape
    tm = min(TOKEN_TILE, T)
    MW = 2 * HEADS * HEAD_DIM
    half = HEADS * HEAD_DIM

    def body(ao_ref, go_ref, gz_ref, nw_ref, w_ref, dh_ref, dao_ref, dgo_ref, dgz_ref, dnw_ref):
        @pl.when(pl.program_id(0) == 0)
        def _():
            dnw_ref[...] = jnp.zeros_like(dnw_ref)

        d_mix = _dg(dh_ref[...].astype(BF16), w_ref[...], 1, 1, None)
        cts = tuple(d_mix[:, i * HEAD_DIM:(i + 1) * HEAD_DIM] for i in range(2 * HEADS))
        _, pull = jax.vjp(_mix_fn, *_mix_operands(ao_ref, go_ref, gz_ref, nw_ref))
        d_ao, d_go, d_gz, d_wm, d_wg = pull(cts)
        for h in range(HEADS):
            lanes = slice(h * HEAD_DIM, (h + 1) * HEAD_DIM)
            dao_ref[:, lanes] = d_ao[h]
            dgo_ref[:, lanes] = d_go[h]
            dgz_ref[:, lanes] = d_gz[h].astype(BF16)
            dnw_ref[h:h + 1, :] += d_wm[h]
        dnw_ref[HEADS:HEADS + 1, :] += d_wg

    return pl.pallas_call(
        body, grid=(T // tm,),
        in_specs=[_row_spec(tm, half), _row_spec(tm, half), _row_spec(tm, half), _const_spec((8, 128)), _const_spec((MW, D)),
                  _row_spec(tm, D)],
        out_specs=[_row_spec(tm, half)] * 3 + [_const_spec((8, 128))],
        out_shape=[_sds((T, half), F32)] * 2 + [_sds((T, half), BF16), _sds((8, 128), F32)],
        compiler_params=_params(("arbitrary",)), name="mix_bwd",
    )(ao, go, gz, nw, w_out, dh)


def _up_spec(w_up, tf):
    per_shard = w_up.shape[2] // tf
    return pl.BlockSpec((None, w_up.shape[1], tf), lambda i, j: (j // per_shard, 0, j % per_shard))


def _mlp_fwd(h2, w_mn, w_up, w_down, target):
    T, D = h2.shape
    FF = w_down.shape[0]
    tm, tf = min(MLP_TOKEN_TILE, T), min(FF_TILE, w_up.shape[2])
    nf = FF // tf

    def body(h_ref, wn_ref, wu_ref, wd_ref, t_ref, hn_ref, act_ref, dy_ref, sq_ref, acc_ref):
        j = pl.program_id(1)

        @pl.when(j == 0)
        def _():
            hn_ref[...] = _rms(h_ref[...], wn_ref[...]).astype(BF16)
            acc_ref[...] = jnp.zeros_like(acc_ref)

        up = jnp.dot(hn_ref[...], wu_ref[...], preferred_element_type=F32)
        act = jnp.square(jnp.maximum(up, 0.0)).astype(BF16)
        act_ref[...] = act
        acc_ref[...] += jnp.dot(act, wd_ref[...], preferred_element_type=F32)

        @pl.when(j == nf - 1)
        def _():
            err = h_ref[...] + acc_ref[...] - t_ref[...]
            dy_ref[...] = err * (1.0 / D)
            sq_ref[...] = jnp.zeros_like(sq_ref) + jnp.sum(err * err)

    tok = lambda w: pl.BlockSpec((tm, w), lambda i, j: (i, 0))
    return pl.pallas_call(
        body, grid=(T // tm, nf),
        in_specs=[tok(D), _const_spec((1, D)), _up_spec(w_up, tf), pl.BlockSpec((tf, D), lambda i, j: (j, 0)), tok(D)],
        out_specs=[tok(D), pl.BlockSpec((tm, tf), lambda i, j: (i, j)), tok(D), pl.BlockSpec((1, 8, 128), lambda i, j: (i, 0, 0))],
        out_shape=[_sds((T, D), BF16), _sds((T, FF), BF16), _sds((T, D), F32), _sds((T // tm, 8, 128), F32)],
        scratch_shapes=[pltpu.VMEM((tm, D), F32)],
        compiler_params=_params(("parallel", "arbitrary")), name="mlp_fwd",
    )(h2, w_mn, w_up, w_down, target)


def _mlp_bwd(h2, w_mn, act, w_up, w_down, dy):
    T, D = h2.shape
    FF = w_down.shape[0]
    tm, tf = min(MLP_TOKEN_TILE, T), min(FF_TILE, w_up.shape[2])
    nf = FF // tf

    def body(h_ref, wn_ref, act_ref, wu_ref, wd_ref, dy_ref, dh_ref, dup_ref, dwn_ref, acc_ref, dyb_ref):
        i, j = pl.program_id(0), pl.program_id(1)

        @pl.when((i == 0) & (j == 0))
        def _():
            dwn_ref[...] = jnp.zeros_like(dwn_ref)

        @pl.when(j == 0)
        def _():
            acc_ref[...] = jnp.zeros_like(acc_ref)
            dyb_ref[...] = dy_ref[...].astype(BF16)

        r = jnp.sqrt(act_ref[...].astype(F32))
        d_act = _dg(dyb_ref[...], wd_ref[...], 1, 1, None)
        d_up = (d_act * (2.0 * r)).astype(BF16)
        dup_ref[...] = d_up
        acc_ref[...] += _dg(d_up, wu_ref[...], 1, 1, None)

        @pl.when(j == nf - 1)
        def _():
            _, pull = jax.vjp(_rms, h_ref[...], wn_ref[...])
            dh, dwn = pull(acc_ref[...])
            dh_ref[...] = dh + dy_ref[...]
            dwn_ref[...] += dwn

    tok = lambda w: pl.BlockSpec((tm, w), lambda i, j: (i, 0))
    ff = pl.BlockSpec((tm, tf), lambda i, j: (i, j))
    return pl.pallas_call(
        body, grid=(T // tm, nf),
        in_specs=[tok(D), _const_spec((1, D)), ff, _up_spec(w_up, tf), pl.BlockSpec((tf, D), lambda i, j: (j, 0)), tok(D)],
        out_specs=[tok(D), ff, _const_spec((1, D))],
        out_shape=[_sds((T, D), F32), _sds((T, FF), BF16), _sds((1, D), F32)],
        scratch_shapes=[pltpu.VMEM((tm, D), F32), pltpu.VMEM((tm, D), BF16)],
        compiler_params=_params(("arbitrary", "arbitrary")), name="mlp_bwd",
    )(h2, w_mn, act, w_up, w_down, dy)


def _rope_pad(a):
    z = jnp.zeros(a.shape[:-1] + (ROPE_HALF,), a.dtype)
    return jnp.concatenate([a[..., :ROPE_HALF], z, a[..., ROPE_HALF:], z], axis=-1)


def _rope_unpad(a):
    return jnp.concatenate([a[..., :ROPE_HALF], a[..., 2 * ROPE_HALF:3 * ROPE_HALF]], axis=-1)


_G0 = 2 * LORA + ROPE_DIM
W_IN_COLS = _G0 + GQKV_W + GZ_W + 2 * HEADS


def _widen_w_in_t(w_t):
    z = jnp.zeros((ROPE_HALF, w_t.shape[1]), w_t.dtype)
    pad = jnp.zeros((GAB_W - 2 * HEADS, w_t.shape[1]), w_t.dtype)
    return jnp.concatenate([w_t[:2 * LORA + ROPE_HALF], z, w_t[2 * LORA + ROPE_HALF:_G0], z, w_t[_G0:], pad], axis=0)


def _narrow_w_in_t(w_t):
    return jnp.concatenate([w_t[:2 * LORA + ROPE_HALF], w_t[2 * LORA + 2 * ROPE_HALF:2 * LORA + 3 * ROPE_HALF],
                            w_t[LAT_W:LAT_W + W_IN_COLS - _G0]], axis=0)


def _stack_mla(w_uq, w_ukv):
    uq = w_uq.reshape(LORA, HEADS, QK_DIM)
    ukv = w_ukv.reshape(LORA, HEADS, 2 * HEAD_DIM)
    parts = [uq[:, :, :HEAD_DIM], _rope_pad(uq[:, :, HEAD_DIM:]), ukv[:, :, :HEAD_DIM], ukv[:, :, HEAD_DIM:]]
    return jnp.concatenate([p.transpose(1, 0, 2) for p in parts], axis=0)


def _unstack_mla(w):
    p = [w[i * HEADS:(i + 1) * HEADS].transpose(1, 0, 2) for i in range(4)]
    uq = jnp.concatenate([p[0], _rope_unpad(p[1])], axis=-1).reshape(LORA, HEADS * QK_DIM)
    ukv = jnp.concatenate([p[2], p[3]], axis=-1).reshape(LORA, HEADS * 2 * HEAD_DIM)
    return uq, ukv


def _rows8(rows):
    a = jnp.concatenate(rows, axis=0)
    return jnp.pad(a, ((0, 8 - a.shape[0]), (0, 0)))


def _qk_norm_rows(q_norm_w, k_norm_w):
    return _rows8([q_norm_w[:, :HEAD_DIM], _rope_pad(q_norm_w[:, HEAD_DIM:]), k_norm_w[:, :HEAD_DIM], _rope_pad(k_norm_w[:, HEAD_DIM:])])


def _rope_rows():
    inv_freq = ROPE_THETA ** (-jnp.arange(ROPE_HALF, dtype=F32) / ROPE_HALF)
    z = jnp.zeros((ROPE_HALF,), F32)
    freq = jnp.concatenate([inv_freq, z, inv_freq, z])
    sign = jnp.concatenate([-jnp.ones((ROPE_HALF,), F32), z, jnp.ones((ROPE_HALF,), F32), z])
    return _rows8([freq[None], sign[None]])


def _column_shards(a):
    return a.reshape(a.shape[0], 4, a.shape[1] // 4).transpose(1, 0, 2)


def _from_column_shards(a):
    return a.transpose(1, 0, 2).reshape(a.shape[1], 4 * a.shape[2])


_ANY = pl.BlockSpec(memory_space=pl.ANY)
_OTHER_CHIPS = ((1, 0), (0, 1), (1, 1))


def _here():
    return lax.axis_index("x"), lax.axis_index("y"), lax.axis_index("c")


def _flip(v, bit):
    return 1 - v if bit else v


def _remote(src, dst, send_sems, recv_sems, k, to):
    return pltpu.make_async_remote_copy(src_ref=src, dst_ref=dst, send_sem=send_sems.at[k], recv_sem=recv_sems.at[k],
                                        device_id=to, device_id_type=MESH)


def _half_of(ref, k, shape):
    r, c = shape
    if (r // 2) % 16 == 0:
        return ref.at[pl.ds(pl.multiple_of(k * (r // 2), 16), r // 2)]
    if (c // 2) % 128 == 0:
        return ref.at[:, pl.ds(pl.multiple_of(k * (c // 2), 128), c // 2)]
    return None


def _gather_copies(srcs, dsts, send_sems, recv_sems, local_sems):
    x, y, c = _here()
    slot, sibling, n = 2 * x + y, (x, y, 1 - c), len(srcs)
    starts, passes, waits = [], [], []
    for i, (src, dst) in enumerate(zip(srcs, dsts)):
        own = pltpu.make_async_copy(src, dst.at[slot], local_sems.at[i])
        starts.append(own.start)
        waits.append(own.wait)
        halves = _half_of(src, c, src.shape) is not None
        for j, (fx, fy) in enumerate(_OTHER_CHIPS):
            cx, cy = _flip(x, fx), _flip(y, fy)
            there = dst.at[2 * cx + cy]
            if halves:
                push = _remote(_half_of(src, c, src.shape), _half_of(dst.at[slot], c, src.shape), send_sems, recv_sems, 3 * i + j, (cx, cy, c))
                landed, other = _half_of(there, c, src.shape), _half_of(there, 1 - c, src.shape)
                onward = _remote(landed, landed, send_sems, recv_sems, 3 * n + 3 * i + j, sibling)
                passes += [_remote(landed, landed, send_sems, recv_sems, 3 * i + j, (cx, cy, c)).wait_recv, onward.start]
                waits += [_remote(other, other, send_sems, recv_sems, 3 * n + 3 * i + j, sibling).wait_recv, onward.wait_send]
            else:
                push = _remote(src, dst.at[slot], send_sems, recv_sems, 3 * i + j, (cx, cy, c))
                waits.append(_remote(there, there, send_sems, recv_sems, 3 * i + j, (cx, cy, c)).wait_recv)
            starts.append(push.start)
            waits.append(push.wait_send)
    return starts, passes, waits


def _gather_scratch(n):
    return [pltpu.SemaphoreType.DMA((6 * n,)), pltpu.SemaphoreType.DMA((6 * n,)), pltpu.SemaphoreType.DMA((n,))]


def _all_gather(shards, name):
    ns = len(shards)

    def body(*refs):
        starts, passes, waits = _gather_copies(refs[:ns], refs[ns:2 * ns], *refs[2 * ns:])
        for call in starts + passes + waits:
            call()

    in_vmem = pl.BlockSpec(memory_space=pltpu.VMEM)
    return pl.pallas_call(
        body, in_specs=[in_vmem] * ns, out_specs=[in_vmem] * ns, out_shape=[_sds((4,) + s.shape, s.dtype) for s in shards],
        scratch_shapes=_gather_scratch(ns), compiler_params=_params(()), name=name,
    )(*shards)


def _by_lanes(shape):
    return (shape[-2] // 2) % 16 != 0


def _scattered_shape(p):
    r, c = p.shape[1:]
    return _sds((8, r, c // 2) if _by_lanes(p.shape) else (8, r // 2, c), p.dtype)


def _scatter_copies(srcs, dsts, send_sems, recv_sems, local_sems, whole=0):
    x, y, c = _here()
    me = 4 * x + 2 * y + c
    starts, waits = [], []
    for i, (src, dst) in enumerate(zip(srcs, dsts)):
        def piece(px, py, pc, src=src, entire=i >= len(srcs) - whole):
            if entire:
                return src
            if _by_lanes(src.shape):
                half = src.shape[2] // 2
                return src.at[2 * px + py, :, pl.ds(pl.multiple_of(pc * half, 128), half)]
            half = src.shape[1] // 2
            return src.at[2 * px + py, pl.ds(pl.multiple_of(pc * half, 16), half)]

        own = pltpu.make_async_copy(piece(x, y, c), dst.at[me], local_sems.at[i])
        starts.append(own.start)
        waits.append(own.wait)
        for k in range(1, 8):
            px, py, pc = _flip(x, k & 4), _flip(y, k & 2), _flip(c, k & 1)
            push = _remote(piece(px, py, pc), dst.at[me], send_sems, recv_sems, 7 * i + k - 1, (px, py, pc))
            landed = dst.at[4 * px + 2 * py + pc]
            starts.append(push.start)
            waits += [_remote(landed, landed, send_sems, recv_sems, 7 * i + k - 1, (px, py, pc)).wait_recv, push.wait_send]
    return starts, waits


def _scatter_scratch(n):
    return [pltpu.SemaphoreType.DMA((7 * n,)), pltpu.SemaphoreType.DMA((7 * n,)), pltpu.SemaphoreType.DMA((n,))]


def _swapped_shape(half):
    r, c = half.shape
    return _sds((r, 2 * c) if _by_lanes((r, 2 * c)) else (2, r, c), half.dtype)


def _swap_copies(srcs, dsts, send_sems=None, recv_sems=None, local_sems=None):
    if not srcs:
        return [], []
    x, y, c = _here()
    sibling = (x, y, 1 - c)
    starts, waits = [], []
    for i, (src, dst) in enumerate(zip(srcs, dsts)):
        if len(dst.shape) == 2:
            lanes = src.shape[1]
            mine, other = (dst.at[:, pl.ds(pl.multiple_of(k * lanes, 128), lanes)] for k in (c, 1 - c))
        else:
            mine, other = dst.at[c], dst.at[1 - c]
        own = pltpu.make_async_copy(src, mine, local_sems.at[i])
        push = _remote(src, mine, send_sems, recv_sems, i, sibling)
        starts += [own.start, push.start]
        waits += [_remote(other, other, send_sems, recv_sems, i, sibling).wait_recv, push.wait_send, own.wait]
    return starts, waits


def _swap_scratch(n):
    return [pltpu.SemaphoreType.DMA((n,)), pltpu.SemaphoreType.DMA((n,)), pltpu.SemaphoreType.DMA((n,))] if n else []


def _exchange_halves(halves, wholes):
    ns, nw = len(halves), len(wholes)

    def body(*refs):
        srcs, dsts = refs[:ns + nw], refs[ns + nw:2 * (ns + nw)]
        sems = refs[2 * (ns + nw):]
        starts, waits = _swap_copies(srcs[:ns], dsts[:ns], *sems[:3])
        more = _scatter_copies(srcs[ns:], dsts[ns:], *sems[3:], whole=nw)
        for call in starts + more[0] + waits + more[1]:
            call()

    return pl.pallas_call(
        body, in_specs=[pl.BlockSpec(memory_space=pltpu.VMEM)] * ns + [_ANY] * nw, out_specs=[_ANY] * (ns + nw),
        out_shape=[_swapped_shape(h) for h in halves] + [_sds((8,) + a.shape, a.dtype) for a in wholes],
        scratch_shapes=_swap_scratch(ns) + _scatter_scratch(nw), name="exchange_halves",
    )(*halves, *wholes)


def _row_tile(rows, row_bytes, budget):
    tr = rows
    while tr * row_bytes > budget and tr % 16 == 0:
        tr //= 2
    return tr


def _sum_slots(parts, name):
    _, rows, cols = parts.shape
    tr = _row_tile(rows, 8 * cols * 4, 2 * 1024 * 1024)

    def body(p_ref, o_ref):
        acc = p_ref[0].astype(F32)
        for d in range(1, 8):
            acc = acc + p_ref[d].astype(F32)
        o_ref[...] = acc

    return pl.pallas_call(
        body, grid=(rows // tr,), in_specs=[pl.BlockSpec((8, tr, cols), lambda i: (0, i, 0))],
        out_specs=pl.BlockSpec((tr, cols), lambda i: (i, 0)), out_shape=_sds((rows, cols), F32),
        compiler_params=_params(("parallel",)), name=name,
    )(parts)


def _adam_update(w, g, m, v):
    m = ADAM_B1 * m + (1.0 - ADAM_B1) * g
    v = ADAM_B2 * v + (1.0 - ADAM_B2) * jnp.square(g)
    m_hat = m / (1.0 - ADAM_B1 ** ADAM_STEP)
    v_hat = v / (1.0 - ADAM_B2 ** ADAM_STEP)
    return -ADAM_LR * (m_hat / (jnp.sqrt(v_hat) + ADAM_EPS) + ADAM_WD * w), m, v


SMALL_ROWS = {"attn_norm_w": 0, "mlp_norm_w": 1, "q_lat_norm_w": 2, "kv_lat_norm_w": 3, "q_norm_w": 4, "k_norm_w": 5,
              "mla_out_norm_w": 6, "gdn_norm_w": 10, "a_log": 11, "dt_bias": 12}
LOSS_ROW = 13
SMALL_SHAPE = (16, 1024)


def _pack_small_partials(d_attn_nw, d_mlp_nw, d_ln, d_qk_nw, d_mix_nw, d_scal, conv_parts, sq):
    D = d_attn_nw.shape[1]

    def body(an_ref, mn_ref, ln_ref, qk_ref, mix_ref, sc_ref, cq_ref, ck_ref, cv_ref, sq_ref, a_ref, c_ref):
        a_ref[...] = jnp.zeros_like(a_ref)
        a_ref[0:1, :D] = an_ref[...]
        a_ref[1:2, :D] = mn_ref[...]
        a_ref[2:4, :LORA] = ln_ref[...]
        for row, base in ((4, 0), (5, 2)):
            rope = qk_ref[base + 1:base + 2, :]
            a_ref[row:row + 1, :QK_DIM] = jnp.concatenate(
                [qk_ref[base:base + 1, :], rope[:, :ROPE_HALF], rope[:, 2 * ROPE_HALF:3 * ROPE_HALF]], axis=1)
        a_ref[6:6 + HEADS, :HEAD_DIM] = mix_ref[0:HEADS, :]
        a_ref[10:11, :HEAD_DIM] = mix_ref[HEADS:HEADS + 1, :]
        a_ref[11:13, :128] = sc_ref[0:2, :]
        a_ref[LOSS_ROW:LOSS_ROW + 1, :128] = jnp.zeros((1, 128), F32) + jnp.sum(sq_ref[:, 0:1, 0:1]) * (0.5 / D)
        c_ref[...] = jnp.concatenate([cq_ref[...], ck_ref[...], cv_ref[...]], axis=1)

    return pl.pallas_call(
        body, out_shape=[_sds(SMALL_SHAPE, F32), _sds((CONV_TAPS, GQKV_W), F32)], name="pack_small_partials",
    )(d_attn_nw, d_mlp_nw, d_ln, d_qk_nw, d_mix_nw, d_scal, *conv_parts, sq)


def _adamw_small(parts, conv_parts, w, m, v):
    names = tuple(SMALL_ROWS) + ("conv_w",)
    cols = w["conv_w"].shape[2]

    def body(*refs):
        p_ref, c_ref = refs[:2]
        n = len(names)
        w_refs, m_refs, v_refs = (dict(zip(names, refs[2 + k * n:2 + (k + 1) * n])) for k in range(3))
        loss_ref = refs[2 + 3 * n]
        out = [dict(zip(names, refs[3 + (3 + k) * n:3 + (4 + k) * n])) for k in range(4)]
        acc_ref, cacc_ref = refs[3 + 7 * n:]
        acc, cacc = p_ref[0], c_ref[0]
        for d in range(1, 8):
            acc, cacc = acc + p_ref[d], cacc + c_ref[d]
        acc_ref[...] = acc
        cacc_ref[...] = cacc
        loss_ref[...] = acc_ref[LOSS_ROW:LOSS_ROW + 1, 0:1]
        chip = 2 * lax.axis_index("x") + lax.axis_index("y")
        for name in names:
            shape = w_refs[name].shape
            if name == "conv_w":
                g = sum(jnp.where(chip == s, cacc_ref[:, s * cols:(s + 1) * cols], 0.0) for s in range(4))[None]
            else:
                row = SMALL_ROWS[name]
                g = acc_ref[row:row + math.prod(shape[:-1]), 0:shape[-1]].reshape(shape)
            delta, new_m, new_v = _adam_update(w_refs[name][...], g, m_refs[name][...], v_refs[name][...])
            for ref, val in zip((o[name] for o in out), (g, delta, new_m, new_v)):
                ref[...] = val

    ins = [x[n] for x in (w, m, v) for n in names]
    shapes = [_sds(w[n].shape, F32) for n in names]
    outs = pl.pallas_call(
        body, out_shape=[_sds((1, 1), F32)] + shapes * 4,
        scratch_shapes=[pltpu.VMEM(parts.shape[1:], F32), pltpu.VMEM(conv_parts.shape[1:], F32)], name="adamw_small",
    )(parts, conv_parts, *ins)
    n = len(names)
    return (outs[0],) + tuple(dict(zip(names, outs[1 + k * n:1 + (k + 1) * n])) for k in range(4))


def _adamw(w, g, m, v, name):
    rows, cols = w.shape[0], w.shape[-1]
    if w.ndim == 3:
        tr = max(d for d in range(1, rows + 1) if rows % d == 0 and d * 8 * cols * 4 * 14 <= VMEM_LIMIT // 2)
    else:
        tr = _row_tile(rows, 7 * cols * 4, 4 * 1024 * 1024)

    def body(w_ref, g_ref, m_ref, v_ref, d_ref, mo_ref, vo_ref):
        d_ref[...], mo_ref[...], vo_ref[...] = _adam_update(w_ref[...], g_ref[...], m_ref[...], v_ref[...])

    block = (tr,) + w.shape[1:]
    spec = pl.BlockSpec(block, lambda i: (i,) + (0,) * (len(block) - 1))
    return pl.pallas_call(
        body, grid=(rows // tr,), in_specs=[spec] * 4, out_specs=[spec] * 3, out_shape=[_sds(w.shape, F32)] * 3,
        compiler_params=_params(("parallel",)), name=name,
    )(w, g, m, v)


def kernel(x, positions, attn_norm_w, w_in, q_lat_norm_w, w_uq, kv_lat_norm_w, w_ukv, q_norm_w, k_norm_w, mla_out_norm_w, conv_w, a_log, dt_bias, gdn_norm_w, w_out, mlp_norm_w, w_up, w_down, loss_target, m_attn_norm_w, m_w_in, m_q_lat_norm_w, m_w_uq, m_kv_lat_norm_w, m_w_ukv, m_q_norm_w, m_k_norm_w, m_mla_out_norm_w, m_conv_w, m_a_log, m_dt_bias, m_gdn_norm_w, m_w_out, m_mlp_norm_w, m_w_up, m_w_down, v_attn_norm_w, v_w_in, v_q_lat_norm_w, v_w_uq, v_kv_lat_norm_w, v_w_ukv, v_q_norm_w, v_k_norm_w, v_mla_out_norm_w, v_conv_w, v_a_log, v_dt_bias, v_gdn_norm_w, v_w_out, v_mlp_norm_w, v_w_up, v_w_down):
    w = dict(zip(WEIGHTS, (attn_norm_w, w_in, q_lat_norm_w, w_uq, kv_lat_norm_w, w_ukv, q_norm_w, k_norm_w, mla_out_norm_w, conv_w,
                           a_log, dt_bias, gdn_norm_w, w_out, mlp_norm_w, w_up, w_down)))
    m = dict(zip(WEIGHTS, (m_attn_norm_w, m_w_in, m_q_lat_norm_w, m_w_uq, m_kv_lat_norm_w, m_w_ukv, m_q_norm_w, m_k_norm_w,
                           m_mla_out_norm_w, m_conv_w, m_a_log, m_dt_bias, m_gdn_norm_w, m_w_out, m_mlp_norm_w, m_w_up, m_w_down)))
    v = dict(zip(WEIGHTS, (v_attn_norm_w, v_w_in, v_q_lat_norm_w, v_w_uq, v_kv_lat_norm_w, v_w_ukv, v_q_norm_w, v_k_norm_w,
                           v_mla_out_norm_w, v_conv_w, v_a_log, v_dt_bias, v_gdn_norm_w, v_w_out, v_mlp_norm_w, v_w_up, v_w_down)))
    B, S, D = x.shape
    T = B * S
    x2, pos, target = x.reshape(T, D), positions.reshape(T, 1), loss_target.reshape(T, D)
    seq = lambda a: a.reshape(B, S, a.shape[-1])
    tok = lambda a: a.reshape(T, a.shape[-1])
    local = {n: w[n][0] for n in SHARDED}

    g_in, g_uq, g_ukv, g_conv = _all_gather([jnp.swapaxes(w_in, 1, 2)[0].astype(BF16), local["w_uq"].astype(BF16),
                                             local["w_ukv"].astype(BF16), local["conv_w"]], "gather_first_weights")
    w_in_p = _widen_w_in_t(g_in.reshape(-1, D))
    w_mla = _stack_mla(_from_column_shards(g_uq), _from_column_shards(g_ukv))
    conv_full = _from_column_shards(g_conv)
    ln_w = jnp.concatenate([q_lat_norm_w, kv_lat_norm_w], axis=0)
    qk_nw = _qk_norm_rows(q_norm_w, k_norm_w)
    rope_rows = _rope_rows()
    scal = _rows8([jnp.pad(a_log, ((0, 0), (0, 128 - HEADS))), jnp.pad(dt_bias, ((0, 0), (0, 128 - HEADS)))])
    mix_nw = _rows8([mla_out_norm_w[0], gdn_norm_w])

    xn, lat, gqkv, gz, gab = _in_proj_fwd(x2, attn_norm_w, w_in_p)
    q, k, v_att = _mla_pre_fwd(lat, pos, ln_w, w_mla, qk_nw, rope_rows)
    ao, lse, g_down = _attn_fwd(seq(q), seq(k), seq(v_att), [local["w_down"].astype(BF16)])
    gq, gk, gv = _gdn_pre_fwd(seq(gqkv), conv_full)
    go, states, powers, solutions, g_out, w_up_b = _gdn_chunk_fwd(gq, gk, gv, seq(gab), scal,
                                                                  [local["w_out"].astype(BF16), local["w_up"].astype(BF16)])
    w_out_b = g_out.reshape(-1, D)
    w_down_b = g_down.reshape(-1, D)
    mix, h2 = _mix_fwd(tok(ao), tok(go), gz, mix_nw, w_out_b, x2)
    hn, act, dy, sq = _mlp_fwd(h2, mlp_norm_w, w_up_b, w_down_b, target)

    dh, d_up, d_mlp_nw = _mlp_bwd(h2, mlp_norm_w, act, w_up_b, w_down_b, dy)
    p_down = _wgrad(act, dy, "wgrad_down").reshape(4, -1, D)
    p_up = _wgrad(hn, d_up, "wgrad_up", column_shards=4)
    d_ao, d_go, d_gz, d_mix_nw = _mix_bwd(tok(ao), tok(go), gz, mix_nw, w_out_b, dh)
    p_out = _wgrad(mix, dh, "wgrad_out").reshape(4, -1, D)
    d_gq, d_gk, d_gv, d_gab, d_scal, s_up, s_out = _gdn_chunk_bwd(gq, gk, gv, seq(gab), scal, states, powers, solutions, seq(d_go),
                                                                  [p_up, p_out])
    early = ("w_up", "w_out", "w_down")
    dxq, dxk, dxv, dcq, dck, dcv, g_up, g_out = _gdn_pre_bwd(seq(gqkv), conv_full, d_gq, d_gk, d_gv,
                                                             [_sum_slots(s_up, "sum_w_up"), _sum_slots(s_out, "sum_w_out")])
    dq, dk, dv, s_down = _attn_bwd(seq(q), seq(k), seq(v_att), ao, lse, seq(d_ao), [p_down])
    d_lat, d_ln, d_w_mla, d_qk_nw = _mla_pre_bwd(lat, pos, ln_w, w_mla, qk_nw, rope_rows, tok(dq), tok(dk), tok(dv), [])
    d_pieces = [d_lat, tok(dxq), tok(dxk), tok(dxv), d_gz, tok(d_gab)]
    p_uq, p_ukv = (_column_shards(a).astype(BF16) for a in _unstack_mla(d_w_mla))
    p_in_wide, (s_uq, s_ukv) = _wgrad_pieces(d_pieces, xn, "wgrad_in", [p_uq, p_ukv])
    p_in = _narrow_w_in_t(p_in_wide).reshape(4, -1, D)
    grad_x2, d_attn_nw, s_in, g_down, g_uq, g_ukv = _in_proj_bwd(
        d_pieces, w_in_p, x2, attn_norm_w, dh, [p_in],
        [_sum_slots(s_down, "sum_w_down"), _sum_slots(s_uq, "sum_w_uq"), _sum_slots(s_ukv, "sum_w_ukv")])
    small_buf, conv_buf = _pack_small_partials(d_attn_nw, d_mlp_nw, d_ln, d_qk_nw, d_mix_nw, d_scal, (dcq, dck, dcv), sq)

    late = ("w_in", "w_uq", "w_ukv")
    g_in, s_small, s_conv = _exchange_halves([_sum_slots(s_in, "sum_w_in")], [small_buf, conv_buf])
    names = early + late
    grad = {n: g.reshape(-1, g.shape[-1]) for n, g in zip(names, [g_up, g_out, g_down, g_in, g_uq, g_ukv])}

    loss, g_small, delta, new_m, new_v = _adamw_small(s_small, s_conv, w, m, v)
    grad.update(g_small)
    for n in names:
        if n == "w_in":
            stored = lambda a: jnp.transpose(a, (2, 0, 1))
            outs = _adamw(stored(w[n]), grad[n][:, None, :], stored(m[n]), stored(v[n]), "adamw_" + n)
            grad[n], delta[n], new_m[n], new_v[n] = (jnp.transpose(a, (1, 2, 0)) for a in (grad[n][:, None, :], *outs))
        else:
            delta[n], new_m[n], new_v[n] = _adamw(local[n], grad[n], m[n][0], v[n][0], "adamw_" + n)
    def in_order(d):
        return [d[n].reshape(w[n].shape) for n in WEIGHTS]

    return (loss.reshape(()), grad_x2.reshape(B, S, D), *in_order(grad), *in_order(delta), *in_order(new_m), *in_order(new_v))
```
